```python
import jax, jax.numpy as jnp
from jax import lax
import numpy as np

D_MODEL = 1024
BATCH = 8
SEQ = 4096
DEPTH = 4

HEAD_DIM = 64
FOX_HEADS = 8
SB_HEADS = 4
POOL_GROUPS = 4
POOL_WINDOWS = (2, 4, 8, 16)
POOL_GROUP_DIM = 64
FOX_W = FOX_HEADS * HEAD_DIM
SB_W = SB_HEADS * HEAD_DIM
POOL_W = POOL_GROUPS * POOL_GROUP_DIM
D_MIX = FOX_W + POOL_W + SB_W
Q_BLOCK = 128
EPS = 1e-6
NEG = -1e30

IN_SPLITS = (
    FOX_W, FOX_W, FOX_W, FOX_W,
    FOX_HEADS,
    POOL_W, POOL_W,
    SB_W, SB_W, SB_W, SB_W,
)
D_IN = sum(IN_SPLITS)

kernel_name = "hybrid_fox_pool_stickbreak_parallel_heads"


def rms_norm(x, g):
    xf = x.astype(jnp.float32)
    y = xf * lax.rsqrt(jnp.mean(xf * xf, axis=-1, keepdims=True) + EPS)
    return (y * g.astype(jnp.float32)).astype(x.dtype)


def to_heads(t, n_heads):
    b, s, _ = t.shape
    return t.reshape(b, s, n_heads, HEAD_DIM).transpose(0, 2, 1, 3)


def from_heads(t):
    b, h, s, d = t.shape
    return t.transpose(0, 2, 1, 3).reshape(b, s, h * d)


def split_blocks(t):
    b, h, s = t.shape[:3]
    nb = s // Q_BLOCK
    t = t.reshape((b, h, nb, Q_BLOCK) + t.shape[3:])
    return jnp.moveaxis(t, 2, 0)


def merge_blocks(o):
    nb, b, h, qb, d = o.shape
    return jnp.moveaxis(o, 0, 2).reshape(b, h, nb * qb, d)


def forgetting_attention(q, k, v, log_f):
    s_len, d = q.shape[2], q.shape[3]
    c = jnp.cumsum(log_f, axis=-1)
    kpos = jnp.arange(s_len)
    scale = d ** -0.5

    def one_block(args):
        qi, ci, i = args
        qpos = i * Q_BLOCK + jnp.arange(Q_BLOCK)
        sc = jnp.einsum('bhqd,bhkd->bhqk', qi, k).astype(jnp.float32) * scale
        sc = sc + (ci[..., :, None] - c[..., None, :])
        sc = jnp.where(kpos[None, :] <= qpos[:, None], sc, NEG)
        p = jax.nn.softmax(sc, axis=-1)
        return jnp.einsum('bhqk,bhkd->bhqd', p.astype(v.dtype), v)

    nb = s_len // Q_BLOCK
    o = lax.map(one_block, (split_blocks(q), split_blocks(c), jnp.arange(nb)))
    return merge_blocks(o)


def stick_breaking_attention(q, k, v):
    s_len, d = q.shape[2], q.shape[3]
    kpos = jnp.arange(s_len)
    scale = d ** -0.5

    def one_block(args):
        qi, i = args
        qpos = i * Q_BLOCK + jnp.arange(Q_BLOCK)
        z = jnp.einsum('bhqd,bhkd->bhqk', qi, k).astype(jnp.float32) * scale
        causal = kpos[None, :] < qpos[:, None]
        log_1m_beta = jnp.where(causal, -jax.nn.softplus(z), 0.0)
        rest = lax.cumsum(log_1m_beta, axis=3, reverse=True) - log_1m_beta
        a = jnp.where(causal, jnp.exp(jax.nn.log_sigmoid(z) + rest), 0.0)
        return jnp.einsum('bhqk,bhkd->bhqd', a.astype(v.dtype), v)

    nb = s_len // Q_BLOCK
    o = lax.map(one_block, (split_blocks(q), jnp.arange(nb)))
    return merge_blocks(o)


def causal_window_mean(x, w):
    s_len = x.shape[1]
    xf = x.astype(jnp.float32)
    cs = jnp.cumsum(xf, axis=1)
    cs_prev = jnp.pad(cs, ((0, 0), (w, 0), (0, 0)))[:, :s_len]
    count = jnp.minimum(jnp.arange(s_len) + 1, w).astype(jnp.float32)
    return ((cs - cs_prev) / count[None, :, None]).astype(x.dtype)


def multiscale_pool(x, w_pool, scale):
    b, s_len, _ = x.shape
    groups = jnp.split(x, POOL_GROUPS, axis=-1)
    pooled = jnp.stack([causal_window_mean(g, w) - g for g, w in zip(groups, POOL_WINDOWS)], axis=2)
    y = jnp.einsum('bsgc,gcd->bsgd', pooled, w_pool).reshape(b, s_len, POOL_W)
    return y * scale


def hybrid_layer(x, norm_g, w_in, b_f, q_norm_g, k_norm_g, w_pool, pool_scale, w_out):
    h = rms_norm(x, norm_g)
    proj = jnp.einsum('bsd,de->bse', h, w_in)
    idx = np.cumsum(IN_SPLITS)[:-1].tolist()
    (fq, fk, fv, fg, ff, px, pg, sq, sk, sv, sg) = jnp.split(proj, idx, axis=-1)

    fq = rms_norm(to_heads(fq, FOX_HEADS), q_norm_g)
    fk = rms_norm(to_heads(fk, FOX_HEADS), k_norm_g)
    fv = to_heads(fv, FOX_HEADS)
    log_f = jax.nn.log_sigmoid((ff + b_f).astype(jnp.float32)).transpose(0, 2, 1)
    fox_out = from_heads(forgetting_attention(fq, fk, fv, log_f)) * jax.nn.silu(fg)

    pool_out = multiscale_pool(px, w_pool, pool_scale) * jax.nn.silu(pg)

    sb = stick_breaking_attention(to_heads(sq, SB_HEADS), to_heads(sk, SB_HEADS), to_heads(sv, SB_HEADS))
    sb_out = from_heads(sb) * jax.nn.silu(sg)

    mixed = jnp.concatenate([fox_out, pool_out, sb_out], axis=-1)
    return x + jnp.einsum('bse,ed->bsd', mixed, w_out)


def _fwd_setup_inputs(seed: int = 0) -> dict:
    key = jax.random.key(seed)
    ks = jax.random.split(key, 10)
    x = jax.random.normal(ks[0], (BATCH, SEQ, D_MODEL), jnp.float32)
    norm_g = 1.0 + 0.02 * jax.random.normal(ks[1], (DEPTH, D_MODEL), jnp.float32)
    w_in = jax.random.normal(ks[2], (DEPTH, D_MODEL, D_IN), jnp.float32) * D_MODEL ** -0.5
    b_f = jax.random.uniform(ks[3], (DEPTH, FOX_HEADS), jnp.float32, 1.0, 4.0)
    q_norm_g = 1.0 + 0.02 * jax.random.normal(ks[4], (DEPTH, HEAD_DIM), jnp.float32)
    k_norm_g = 1.0 + 0.02 * jax.random.normal(ks[5], (DEPTH, HEAD_DIM), jnp.float32)
    w_pool = jax.random.normal(ks[6], (DEPTH, POOL_GROUPS, POOL_GROUP_DIM, POOL_GROUP_DIM), jnp.float32) * POOL_GROUP_DIM ** -0.5
    pool_scale = 1.0 + 0.02 * jax.random.normal(ks[7], (DEPTH, POOL_W), jnp.float32)
    w_out = jax.random.normal(ks[8], (DEPTH, D_MIX, D_MODEL), jnp.float32) * D_MIX ** -0.5
    return {"x": x, "norm_g": norm_g, "w_in": w_in, "b_f": b_f, "q_norm_g": q_norm_g,
            "k_norm_g": k_norm_g, "w_pool": w_pool, "pool_scale": pool_scale, "w_out": w_out}


def _fwd_reference(x, norm_g, w_in, b_f, q_norm_g, k_norm_g, w_pool, pool_scale, w_out):
    for l in range(DEPTH):
        x = hybrid_layer(x, norm_g[l], w_in[l], b_f[l], q_norm_g[l], k_norm_g[l],
                         w_pool[l], pool_scale[l], w_out[l])
    return x


import jax as _jax
import jax.numpy as _jnp

TWIN_FORMAT = 'train_step'
FWD_PARAMS = ['x', 'norm_g', 'w_in', 'b_f', 'q_norm_g', 'k_norm_g', 'w_pool', 'pool_scale', 'w_out']
TWIN_WEIGHTS = ['norm_g', 'w_in', 'b_f', 'q_norm_g', 'k_norm_g', 'w_pool', 'pool_scale', 'w_out']
TWIN_DIFF_INPUT = 'x'
TWIN_INPUTS = ['x', 'norm_g', 'w_in', 'b_f', 'q_norm_g', 'k_norm_g', 'w_pool', 'pool_scale', 'w_out', 'loss_target', 'm_norm_g', 'm_w_in', 'm_b_f', 'm_q_norm_g', 'm_k_norm_g', 'm_w_pool', 'm_pool_scale', 'm_w_out', 'v_norm_g', 'v_w_in', 'v_b_f', 'v_q_norm_g', 'v_k_norm_g', 'v_w_pool', 'v_pool_scale', 'v_w_out']
TWIN_OUTPUTS = ['loss', 'grad_x', 'grad_norm_g', 'grad_w_in', 'grad_b_f', 'grad_q_norm_g', 'grad_k_norm_g', 'grad_w_pool', 'grad_pool_scale', 'grad_w_out', 'delta_norm_g', 'delta_w_in', 'delta_b_f', 'delta_q_norm_g', 'delta_k_norm_g', 'delta_w_pool', 'delta_pool_scale', 'delta_w_out', 'new_m_norm_g', 'new_m_w_in', 'new_m_b_f', 'new_m_q_norm_g', 'new_m_k_norm_g', 'new_m_w_pool', 'new_m_pool_scale', 'new_m_w_out', 'new_v_norm_g', 'new_v_w_in', 'new_v_b_f', 'new_v_q_norm_g', 'new_v_k_norm_g', 'new_v_w_pool', 'new_v_pool_scale', 'new_v_w_out']
TWIN_LEAF_KINDS = {'loss': 'loss', 'grad_x': 'grad_x', 'grad_norm_g': 'grad_w', 'grad_w_in': 'grad_w', 'grad_b_f': 'grad_w', 'grad_q_norm_g': 'grad_w', 'grad_k_norm_g': 'grad_w', 'grad_w_pool': 'grad_w', 'grad_pool_scale': 'grad_w', 'grad_w_out': 'grad_w', 'delta_norm_g': 'delta_w', 'delta_w_in': 'delta_w', 'delta_b_f': 'delta_w', 'delta_q_norm_g': 'delta_w', 'delta_k_norm_g': 'delta_w', 'delta_w_pool': 'delta_w', 'delta_pool_scale': 'delta_w', 'delta_w_out': 'delta_w', 'new_m_norm_g': 'new_m', 'new_m_w_in': 'new_m', 'new_m_b_f': 'new_m', 'new_m_q_norm_g': 'new_m', 'new_m_k_norm_g': 'new_m', 'new_m_w_pool': 'new_m', 'new_m_pool_scale': 'new_m', 'new_m_w_out': 'new_m', 'new_v_norm_g': 'new_v', 'new_v_w_in': 'new_v', 'new_v_b_f': 'new_v', 'new_v_q_norm_g': 'new_v', 'new_v_k_norm_g': 'new_v', 'new_v_w_pool': 'new_v', 'new_v_pool_scale': 'new_v', 'new_v_w_out': 'new_v'}


def _forward(args):
    return _fwd_reference(*[args[k] for k in FWD_PARAMS])


def _output_shape():
    def fwd():
        inp = _fwd_setup_inputs(0)
        return _fwd_reference(*[inp[k] for k in FWD_PARAMS])
    out = _jax.eval_shape(fwd)
    return out.shape, out.dtype

N_MICROBATCH = 1
ADAM_LR = 0.001
ADAM_B1 = 0.9
ADAM_B2 = 0.999
ADAM_EPS = 1e-08
ADAM_WD = 0.01
ADAM_STEP = 10
PER_EXAMPLE_BATCH_AXIS = {'x': 0, 'loss_target': 0}
SHARED_INPUTS = []
_WEIGHT_DTYPES = {'norm_g': _jnp.float32, 'w_in': _jnp.float32, 'b_f': _jnp.float32, 'q_norm_g': _jnp.float32, 'k_norm_g': _jnp.float32, 'w_pool': _jnp.float32, 'pool_scale': _jnp.float32, 'w_out': _jnp.float32}
MOMENT_SCALE = {'norm_g': 9.472511e+00, 'w_in': 1.797263e-01, 'b_f': 3.348898e+01, 'q_norm_g': 4.581384e+00, 'k_norm_g': 4.573357e+00, 'w_pool': 1.054915e+00, 'pool_scale': 9.000356e+00, 'w_out': 1.960890e-01}


def _to_microbatches(a, axis):
    t = _jnp.moveaxis(a, axis, 0)
    t = t.reshape((N_MICROBATCH, t.shape[0] // N_MICROBATCH) + t.shape[1:])
    return _jnp.moveaxis(t, 1, axis + 1)


def setup_inputs(seed: int = 0) -> dict:
    inp = _fwd_setup_inputs(seed)
    key = _jax.random.fold_in(_jax.random.key(seed), 7919)
    shape, _ = _output_shape()
    out = dict(inp)
    out["loss_target"] = _jax.random.normal(_jax.random.fold_in(key, 0), shape, _jnp.float32)
    for i, name in enumerate(TWIN_WEIGHTS):
        w = inp[name].astype(_jnp.float32)
        if MOMENT_SCALE is None:
            s = _jnp.sqrt(_jnp.mean(_jnp.square(w)) + 1e-30)
        else:
            s = MOMENT_SCALE[name]
        km, kv = _jax.random.split(_jax.random.fold_in(key, i + 1))
        out[name] = w
        out["m_" + name] = s * _jax.random.normal(km, w.shape, _jnp.float32)
        out["v_" + name] = (s * s) * _jax.random.uniform(kv, w.shape, _jnp.float32, 0.5, 1.5)
    if N_MICROBATCH > 1:
        for name, axis in PER_EXAMPLE_BATCH_AXIS.items():
            out[name] = _to_microbatches(out[name], axis)
    return {'x': out['x'], 'norm_g': out['norm_g'], 'w_in': out['w_in'], 'b_f': out['b_f'], 'q_norm_g': out['q_norm_g'], 'k_norm_g': out['k_norm_g'], 'w_pool': out['w_pool'], 'pool_scale': out['pool_scale'], 'w_out': out['w_out'], 'loss_target': out['loss_target'], 'm_norm_g': out['m_norm_g'], 'm_w_in': out['m_w_in'], 'm_b_f': out['m_b_f'], 'm_q_norm_g': out['m_q_norm_g'], 'm_k_norm_g': out['m_k_norm_g'], 'm_w_pool': out['m_w_pool'], 'm_pool_scale': out['m_pool_scale'], 'm_w_out': out['m_w_out'], 'v_norm_g': out['v_norm_g'], 'v_w_in': out['v_w_in'], 'v_b_f': out['v_b_f'], 'v_q_norm_g': out['v_q_norm_g'], 'v_k_norm_g': out['v_k_norm_g'], 'v_w_pool': out['v_w_pool'], 'v_pool_scale': out['v_pool_scale'], 'v_w_out': out['v_w_out']}


def _loss(weights, diff, rest, loss_target):
    with _jax.named_scope("forward"):
        args = {**rest, TWIN_DIFF_INPUT: diff, **{k: w.astype(_WEIGHT_DTYPES[k]) for k, w in weights.items()}}
        y = _forward(args)
    with _jax.named_scope("loss_head"):
        err = _jnp.square(y.astype(_jnp.float32) - loss_target)
        return 0.5 * _jnp.sum(_jnp.mean(err, axis=-1)) if err.ndim else 0.5 * err


def _adamw(w, g, m, v):
    m = ADAM_B1 * m + (1.0 - ADAM_B1) * g
    v = ADAM_B2 * v + (1.0 - ADAM_B2) * _jnp.square(g)
    m_hat = m / (1.0 - ADAM_B1 ** ADAM_STEP)
    v_hat = v / (1.0 - ADAM_B2 ** ADAM_STEP)
    delta = -ADAM_LR * (m_hat / (_jnp.sqrt(v_hat) + ADAM_EPS) + ADAM_WD * w)
    return delta, m, v


def reference(x, norm_g, w_in, b_f, q_norm_g, k_norm_g, w_pool, pool_scale, w_out, loss_target, m_norm_g, m_w_in, m_b_f, m_q_norm_g, m_k_norm_g, m_w_pool, m_pool_scale, m_w_out, v_norm_g, v_w_in, v_b_f, v_q_norm_g, v_k_norm_g, v_w_pool, v_pool_scale, v_w_out):
    given = dict(x=x, norm_g=norm_g, w_in=w_in, b_f=b_f, q_norm_g=q_norm_g, k_norm_g=k_norm_g, w_pool=w_pool, pool_scale=pool_scale, w_out=w_out, loss_target=loss_target, m_norm_g=m_norm_g, m_w_in=m_w_in, m_b_f=m_b_f, m_q_norm_g=m_q_norm_g, m_k_norm_g=m_k_norm_g, m_w_pool=m_w_pool, m_pool_scale=m_pool_scale, m_w_out=m_w_out, v_norm_g=v_norm_g, v_w_in=v_w_in, v_b_f=v_b_f, v_q_norm_g=v_q_norm_g, v_k_norm_g=v_k_norm_g, v_w_pool=v_w_pool, v_pool_scale=v_pool_scale, v_w_out=v_w_out)
    weights = {n: given[n] for n in TWIN_WEIGHTS}
    shared = {n: given[n] for n in SHARED_INPUTS}
    per_example = {n: given[n] for n in ['x']}
    grad_fn = _jax.value_and_grad(_loss, argnums=(0, 1))

    def one_microbatch(ex, loss_target):
        ex = dict(ex)
        diff = ex.pop(TWIN_DIFF_INPUT)
        return grad_fn(weights, diff, {**shared, **ex}, loss_target)

    if N_MICROBATCH == 1:
        loss, (grad_w, grad_x) = one_microbatch(per_example, given["loss_target"])
    else:
        def body(carry, xs):
            loss_sum, grad_sum = carry
            l_k, (gw_k, gx_k) = one_microbatch(xs[0], xs[1])
            with _jax.named_scope("update"):
                return (loss_sum + l_k, _jax.tree.map(_jnp.add, grad_sum, gw_k)), gx_k

        init = (_jnp.zeros((), _jnp.float32), _jax.tree.map(_jnp.zeros_like, weights))
        (loss, grad_w), grad_x = _jax.lax.scan(body, init, (per_example, given["loss_target"]))
    with _jax.named_scope("update"):
        delta_w, new_m, new_v = {}, {}, {}
        for n in TWIN_WEIGHTS:
            delta_w[n], new_m[n], new_v[n] = _adamw(weights[n], grad_w[n], given["m_" + n], given["v_" + n])
    return (loss, grad_x, *[grad_w[n] for n in TWIN_WEIGHTS], *[delta_w[n] for n in TWIN_WEIGHTS],
            *[new_m[n] for n in TWIN_WEIGHTS], *[new_v[n] for n in TWIN_WEIGHTS])
```

```python
import functools

import jax
import jax.numpy as jnp
from jax import lax
from jax.experimental import pallas as pl
from jax.experimental.pallas import tpu as pltpu

F32 = jnp.float32
BF16 = jnp.bfloat16

EPS = 1e-6
NEG = -1e30
HEAD_DIM = 64
FOX_HEADS = 8
FOX_W = 512
POOL_W = 256
SB_W = 256
D_MIX = 1024
N_FF = 8
N_MAIN = 3584
N_FFPAD = 128
OFF_FQ, OFF_FK, OFF_FV, OFF_FG = 0, 512, 1024, 1536
OFF_PX, OFF_PG = 2048, 2304
OFF_SQ, OFF_SK, OFF_SV, OFF_SG = 2560, 2816, 3072, 3328
D_IN = 3592
Q_SCALE = HEAD_DIM ** -0.5

ADAM_LR = 0.001
ADAM_B1 = 0.9
ADAM_B2 = 0.999
ADAM_EPS = 1e-08
ADAM_WD = 0.01
ADAM_STEP = 10

N_DEV = 8
MESH_AXES = ("x", "y", "c")

_T = 256
_TM = 512
_VMEM_BIG = 56 << 20


def _cp(sem=None, vmem=None):
    kw = {}
    if sem is not None:
        kw["dimension_semantics"] = sem
    if vmem is not None:
        kw["vmem_limit_bytes"] = vmem
    return pltpu.CompilerParams(**kw)


def _dot(a, b):
    return jnp.dot(a, b, preferred_element_type=F32)


def _dot_nt(a, b):
    return lax.dot_general(a, b, (((1,), (1,)), ((), ())), preferred_element_type=F32)


def _dot_tn(a, b):
    return lax.dot_general(a, b, (((0,), (0,)), ((), ())), preferred_element_type=F32)


def _mm2(v, m):
    hi = v.astype(BF16)
    lo = (v - hi.astype(F32)).astype(BF16)
    return _dot(hi, m) + _dot(lo, m)


def _mm3(v, m, left=False):
    a1 = v.astype(BF16)
    r1 = v - a1.astype(F32)
    a2 = r1.astype(BF16)
    a3 = (r1 - a2.astype(F32)).astype(BF16)
    if left:
        return _dot(m, a1) + _dot(m, a2) + _dot(m, a3)
    return _dot(a1, m) + _dot(a2, m) + _dot(a3, m)


def _sigmoid(z):
    return 1.0 / (1.0 + jnp.exp(-z))


def _rms_rows(x):
    return lax.rsqrt(jnp.mean(x * x, axis=-1, keepdims=True) + EPS)


def _inproj_fwd(x, g, wm, wff):
    S, D = x.shape
    tm = min(_TM, S)
    tn = 512

    def body(x_ref, g_ref, w_ref, wff_ref, o_ref, off_ref, h_ref):
        @pl.when(pl.program_id(1) == 0)
        def _():
            xv = x_ref[...]
            h = (xv * _rms_rows(xv)) * g_ref[...]
            h_ref[...] = h.astype(BF16)
            off_ref[...] = _dot(h_ref[...], wff_ref[...])

        o_ref[...] = _dot(h_ref[...], w_ref[...])

    return pl.pallas_call(
        body, name="inproj_fwd",
        grid=(S // tm, N_MAIN // tn),
        in_specs=[pl.BlockSpec((tm, D), lambda i, j: (i, 0)),
                  pl.BlockSpec((1, D), lambda i, j: (0, 0)),
                  pl.BlockSpec((D, tn), lambda i, j: (0, j)),
                  pl.BlockSpec((D, N_FFPAD), lambda i, j: (0, 0))],
        out_specs=[pl.BlockSpec((tm, tn), lambda i, j: (i, j)),
                   pl.BlockSpec((tm, N_FFPAD), lambda i, j: (i, 0))],
        out_shape=[jax.ShapeDtypeStruct((S, N_MAIN), F32), jax.ShapeDtypeStruct((S, N_FFPAD), F32)],
        scratch_shapes=[pltpu.VMEM((tm, D), BF16)],
        compiler_params=_cp(("parallel", "arbitrary"), 40 << 20),
    )(x, g, wm, wff)


def _head_norm(x, g, bd):
    ss = _mm2(x * x, bd)
    r = lax.rsqrt(ss * (1.0 / HEAD_DIM) + EPS)
    return (x * r) * g


def _fox_prep(proj, pff, bfp, gq, gk, bd, ex, tril):
    S = proj.shape[0]
    T = tril.shape[0]

    def body(q_ref, k_ref, ff_ref, b_ref, gq_ref, gk_ref, bd_ref, ex_ref, tri_ref,
             qs_ref, kn_ref, cc_ref, cqb_ref, carry):
        @pl.when(pl.program_id(0) == 0)
        def _():
            carry[...] = jnp.zeros_like(carry)

        bdv = bd_ref[...]
        qs_ref[...] = (_head_norm(q_ref[...], gq_ref[...], bdv) * Q_SCALE).astype(BF16)
        kn_ref[...] = _head_norm(k_ref[...], gk_ref[...], bdv).astype(BF16)
        u = ff_ref[...] + b_ref[...]
        lf = jnp.minimum(u, 0.0) - jnp.log1p(jnp.exp(-jnp.abs(u)))
        c = _mm3(lf, tri_ref[...], left=True) + carry[0:1, :]
        carry[0:1, :] = c[T - 1:T, :]
        cc_ref[...] = c
        cqb_ref[...] = _mm3(c, ex_ref[...])

    return pl.pallas_call(
        body, name="fox_prep",
        grid=(S // T,),
        in_specs=[pl.BlockSpec((T, FOX_W), lambda i: (i, OFF_FQ // FOX_W)),
                  pl.BlockSpec((T, FOX_W), lambda i: (i, OFF_FK // FOX_W)),
                  pl.BlockSpec((T, N_FFPAD), lambda i: (i, 0)),
                  pl.BlockSpec((1, N_FFPAD), lambda i: (0, 0)),
                  pl.BlockSpec((1, FOX_W), lambda i: (0, 0)),
                  pl.BlockSpec((1, FOX_W), lambda i: (0, 0)),
                  pl.BlockSpec((FOX_W, FOX_W), lambda i: (0, 0)),
                  pl.BlockSpec((N_FFPAD, FOX_W), lambda i: (0, 0)),
                  pl.BlockSpec((T, T), lambda i: (0, 0))],
        out_specs=[pl.BlockSpec((T, FOX_W), lambda i: (i, 0)),
                   pl.BlockSpec((T, FOX_W), lambda i: (i, 0)),
                   pl.BlockSpec((T, N_FFPAD), lambda i: (i, 0)),
                   pl.BlockSpec((T, FOX_W), lambda i: (i, 0))],
        out_shape=[jax.ShapeDtypeStruct((S, FOX_W), BF16), jax.ShapeDtypeStruct((S, FOX_W), BF16),
                   jax.ShapeDtypeStruct((S, N_FFPAD), F32), jax.ShapeDtypeStruct((S, FOX_W), F32)],
        scratch_shapes=[pltpu.VMEM((8, N_FFPAD), F32)],
        compiler_params=_cp(("arbitrary",), 40 << 20),
    )(proj, proj, pff, bfp, gq, gk, bd, ex, tril)


def _fox_fwd(qs, kn, proj, cqb, crow4):
    S = qs.shape[0]
    T = min(_T, S)
    nq = S // T

    def body(q_ref, k_ref, v_ref, cq_ref, cr_ref, o_ref, lse_ref, qa, qb, vb, m2, l2, acc):
        lane_s = lax.broadcasted_iota(jnp.int32, (S, 128), 1) < HEAD_DIM
        q = q_ref[...]
        zq = jnp.zeros_like(q)
        qa[...] = jnp.where(lane_s, q, zq)
        qb[...] = jnp.where(lane_s, zq, q)
        vb[...] = v_ref[...].astype(BF16)
        lane_t = lax.broadcasted_iota(jnp.int32, (T, 128), 1) < HEAD_DIM
        causal = (lax.broadcasted_iota(jnp.int32, (T, T), 1) <= lax.broadcasted_iota(jnp.int32, (T, T), 0))

        def head(qh, k, v, cqh, ckh, m_old, masked):
            s = _dot_nt(qh, k) + cqh - ckh
            if masked:
                s = jnp.where(causal, s, NEG)
            m_new = jnp.maximum(m_old, jnp.max(s, axis=1, keepdims=True))
            p = jnp.exp(s - m_new)
            return m_new, jnp.sum(p, axis=1, keepdims=True), _dot(p.astype(BF16), v)

        def kv(j, r0, masked):
            c0 = pl.multiple_of(j * T, T)
            k = k_ref[pl.ds(c0, T), :]
            v = vb[pl.ds(c0, T), :]
            ck = cr_ref[:, pl.ds(c0, T)]
            cq = cq_ref[pl.ds(r0, T), :]
            mo = m2[...]
            ma, sa, pva = head(qa[pl.ds(r0, T), :], k, v, cq[:, 0:1], ck[0:1, :], mo[:, 0:1], masked)
            mb, sb, pvb = head(qb[pl.ds(r0, T), :], k, v, cq[:, 64:65], ck[1:2, :], mo[:, 64:65], masked)
            mn = jnp.where(lane_t, ma, mb)
            al = jnp.exp(mo - mn)
            l2[...] = al * l2[...] + jnp.where(lane_t, sa, sb)
            acc[...] = al * acc[...] + jnp.where(lane_t, pva, pvb)
            m2[...] = mn

        def qblk(i, carry):
            r0 = pl.multiple_of(i * T, T)
            m2[...] = jnp.full((T, 128), NEG, F32)
            l2[...] = jnp.zeros((T, 128), F32)
            acc[...] = jnp.zeros((T, 128), F32)

            def inner(j, c):
                kv(j, r0, False)
                return c

            lax.fori_loop(0, i, inner, 0)
            kv(i, r0, True)
            l = l2[...]
            o_ref[pl.ds(r0, T), :] = acc[...] / l
            lse_ref[pl.ds(r0, T), :] = m2[...] + jnp.log(l)
            return carry

        lax.fori_loop(0, nq, qblk, 0)

    blk = lambda off: pl.BlockSpec((S, 128), lambda p: (0, off + p))
    return pl.pallas_call(
        body, name="fox_fwd",
        grid=(FOX_W // 128,),
        in_specs=[blk(0), blk(0), blk(OFF_FV // 128), blk(0),
                  pl.BlockSpec((None, 8, S), lambda p: (p, 0, 0))],
        out_specs=[blk(0), blk(0)],
        out_shape=[jax.ShapeDtypeStruct((S, FOX_W), F32), jax.ShapeDtypeStruct((S, FOX_W), F32)],
        scratch_shapes=[pltpu.VMEM((S, 128), BF16), pltpu.VMEM((S, 128), BF16), pltpu.VMEM((S, 128), BF16),
                        pltpu.VMEM((T, 128), F32), pltpu.VMEM((T, 128), F32), pltpu.VMEM((T, 128), F32)],
        compiler_params=_cp(("arbitrary",), _VMEM_BIG),
    )(qs, kn, proj, cqb, crow4)


def _softplus_parts(z):
    e = jnp.exp(-jnp.abs(z))
    return e, jnp.maximum(z, 0.0) + jnp.log1p(e)


def _sb_fwd(proj, tril):
    S = proj.shape[0]
    T = tril.shape[0]
    nq = S // T

    def body(q_ref, k_ref, v_ref, tri_ref, o_ref, lt_ref, qa, qb, kb, vb, ra, rb, acc):
        lane_s = lax.broadcasted_iota(jnp.int32, (S, 128), 1) < HEAD_DIM
        q = (q_ref[...] * Q_SCALE).astype(BF16)
        zq = jnp.zeros_like(q)
        qa[...] = jnp.where(lane_s, q, zq)
        qb[...] = jnp.where(lane_s, zq, q)
        kb[...] = k_ref[...].astype(BF16)
        vb[...] = v_ref[...].astype(BF16)
        lane_t = lax.broadcasted_iota(jnp.int32, (T, 128), 1) < HEAD_DIM
        strict = (lax.broadcasted_iota(jnp.int32, (T, T), 1) < lax.broadcasted_iota(jnp.int32, (T, T), 0))

        def head(qh, k, v, r_ref, masked):
            z = _dot_nt(qh, k)
            _, sp = _softplus_parts(z)
            lb = -sp
            if masked:
                lb = jnp.where(strict, lb, 0.0)
            inc = _mm2(lb, tri_ref[...])
            r = r_ref[:, 0:1]
            a = jnp.exp(z + inc + r)
            if masked:
                a = jnp.where(strict, a, 0.0)
            r_ref[...] = jnp.broadcast_to(r + inc[:, 0:1], (T, 128))
            return _dot(a.astype(BF16), v)

        def kv(j, r0, masked):
            c0 = pl.multiple_of(j * T, T)
            k = kb[pl.ds(c0, T), :]
            v = vb[pl.ds(c0, T), :]
            ava = head(qa[pl.ds(r0, T), :], k, v, ra, masked)
            avb = head(qb[pl.ds(r0, T), :], k, v, rb, masked)
            acc[...] = acc[...] + jnp.where(lane_t, ava, avb)

        def qblk(i, carry):
            r0 = pl.multiple_of(i * T, T)
            ra[...] = jnp.zeros((T, 128), F32)
            rb[...] = jnp.zeros((T, 128), F32)
            acc[...] = jnp.zeros((T, 128), F32)
            kv(i, r0, True)

            def inner(jj, c):
                kv(i - 1 - jj, r0, False)
                return c

            lax.fori_loop(0, i, inner, 0)
            o_ref[pl.ds(r0, T), :] = acc[...]
            lt_ref[pl.ds(r0, T), :] = jnp.where(lane_t, ra[...], rb[...])
            return carry

        lax.fori_loop(0, nq, qblk, 0)

    blk = lambda off: pl.BlockSpec((S, 128), lambda p: (0, off + p))
    return pl.pallas_call(
        body, name="sb_fwd",
        grid=(SB_W // 128,),
        in_specs=[blk(OFF_SQ // 128), blk(OFF_SK // 128), blk(OFF_SV // 128),
                  pl.BlockSpec((T, T), lambda p: (0, 0))],
        out_specs=[blk(0), blk(0)],
        out_shape=[jax.ShapeDtypeStruct((S, SB_W), F32), jax.ShapeDtypeStruct((S, SB_W), F32)],
        scratch_shapes=[pltpu.VMEM((S, 128), BF16)] * 4 + [pltpu.VMEM((T, 128), F32)] * 3,
        compiler_params=_cp(("arbitrary",), _VMEM_BIG),
    )(proj, proj, proj, tril)


def _pool_window_lanes(shape):
    lane = lax.broadcasted_iota(jnp.int32, shape, 1)
    return jnp.where(lane < 64, 2, jnp.where(lane < 128, 4, jnp.where(lane < 192, 8, 16)))


def _pool_fwd(proj):
    S = proj.shape[0]

    def body(x_ref, o_ref):
        x = x_ref[...]
        t = lax.broadcasted_iota(jnp.int32, x.shape, 0)
        lane = lax.broadcasted_iota(jnp.int32, x.shape, 1)

        def back(a, k):
            return jnp.where(t >= k, pltpu.roll(a, k, 0), 0.0)

        s1 = x + back(x, 1)
        s2 = s1 + back(s1, 2)
        s4 = s2 + back(s2, 4)
        s8 = s4 + back(s4, 8)
        win = jnp.where(lane < 64, s1, jnp.where(lane < 128, s2, jnp.where(lane < 192, s4, s8)))
        cnt = jnp.minimum(t + 1, _pool_window_lanes(x.shape)).astype(F32)
        o_ref[...] = win / cnt - x

    return pl.pallas_call(
        body, name="pool_fwd",
        grid=(1,),
        in_specs=[pl.BlockSpec((S, POOL_W), lambda i: (0, OFF_PX // POOL_W))],
        out_specs=pl.BlockSpec((S, POOL_W), lambda i: (0, 0)),
        out_shape=jax.ShapeDtypeStruct((S, POOL_W), F32),
        compiler_params=_cp(("arbitrary",), _VMEM_BIG),
    )(proj)


def _silu(g):
    return g * _sigmoid(g)


def _mix_out(fo, so, pooled, proj, wbd, scale, wout, x):
    S, D = x.shape
    tm = min(256, S)

    def body(fo_ref, fg_ref, so_ref, sg_ref, pl_ref, pg_ref, wbd_ref, sc_ref, w_ref, x_ref, y_ref, mx_ref):
        mx_ref[:, 0:FOX_W] = (fo_ref[...] * _silu(fg_ref[...])).astype(BF16)
        yp = _dot(pl_ref[...].astype(BF16), wbd_ref[...]) * sc_ref[...]
        mx_ref[:, FOX_W:FOX_W + POOL_W] = (yp * _silu(pg_ref[...])).astype(BF16)
        mx_ref[:, FOX_W + POOL_W:D_MIX] = (so_ref[...] * _silu(sg_ref[...])).astype(BF16)
        y_ref[...] = x_ref[...] + _dot(mx_ref[...], w_ref[...])

    return pl.pallas_call(
        body, name="mix_out",
        grid=(S // tm,),
        in_specs=[pl.BlockSpec((tm, FOX_W), lambda i: (i, 0)),
                  pl.BlockSpec((tm, FOX_W), lambda i: (i, OFF_FG // FOX_W)),
                  pl.BlockSpec((tm, SB_W), lambda i: (i, 0)),
                  pl.BlockSpec((tm, SB_W), lambda i: (i, OFF_SG // SB_W)),
                  pl.BlockSpec((tm, POOL_W), lambda i: (i, 0)),
                  pl.BlockSpec((tm, POOL_W), lambda i: (i, OFF_PG // POOL_W)),
                  pl.BlockSpec((POOL_W, POOL_W), lambda i: (0, 0)),
                  pl.BlockSpec((1, POOL_W), lambda i: (0, 0)),
                  pl.BlockSpec((D_MIX, D), lambda i: (0, 0)),
                  pl.BlockSpec((tm, D), lambda i: (i, 0))],
        out_specs=[pl.BlockSpec((tm, D), lambda i: (i, 0)), pl.BlockSpec((tm, D_MIX), lambda i: (i, 0))],
        out_shape=[jax.ShapeDtypeStruct((S, D), F32), jax.ShapeDtypeStruct((S, D_MIX), BF16)],
        compiler_params=_cp(("parallel",), 40 << 20),
    )(fo, proj, so, proj, pooled, proj, wbd, scale, wout, x)


def _loss_head(y, target):
    S, D = y.shape
    tm = min(_TM, S)

    def body(y_ref, t_ref, dy_ref, ls_ref):
        @pl.when(pl.program_id(0) == 0)
        def _():
            ls_ref[...] = jnp.zeros_like(ls_ref)

        e = y_ref[...] - t_ref[...]
        dy_ref[...] = e * (1.0 / D)
        ls_ref[...] = ls_ref[...] + jnp.sum(e * e) * (0.5 / D)

    dy, ls = pl.pallas_call(
        body, name="loss_head",
        grid=(S // tm,),
        in_specs=[pl.BlockSpec((tm, D), lambda i: (i, 0)), pl.BlockSpec((tm, D), lambda i: (i, 0))],
        out_specs=[pl.BlockSpec((tm, D), lambda i: (i, 0)), pl.BlockSpec((8, 128), lambda i: (0, 0))],
        out_shape=[jax.ShapeDtypeStruct((S, D), F32), jax.ShapeDtypeStruct((8, 128), F32)],
        compiler_params=_cp(("arbitrary",), 40 << 20),
    )(y, target)
    return dy, ls[0, 0]


def _dsilu(g):
    s = _sigmoid(g)
    return s * (1.0 + g * (1.0 - s))


def _gate_bwd(dy, wout, fo, so, pooled, proj, wbd, scale):
    S, D = dy.shape
    tm = min(256, S)

    def body(dy_ref, w_ref, fo_ref, fg_ref, so_ref, sg_ref, pl_ref, pg_ref, wbd_ref, sc_ref,
             dfo_ref, dfg_ref, dso_ref, dsg_ref, dpg_ref, dpl_ref, dsc_ref, dwbd_ref):
        @pl.when(pl.program_id(0) == 0)
        def _():
            dsc_ref[...] = jnp.zeros_like(dsc_ref)
            dwbd_ref[...] = jnp.zeros_like(dwbd_ref)

        dm = _dot_nt(dy_ref[...].astype(BF16), w_ref[...])
        dmf = dm[:, 0:FOX_W]
        dmp = dm[:, FOX_W:FOX_W + POOL_W]
        dms = dm[:, FOX_W + POOL_W:D_MIX]
        fg = fg_ref[...]
        dfo_ref[...] = dmf * _silu(fg)
        dfg_ref[...] = (dmf * fo_ref[...] * _dsilu(fg)).astype(BF16)
        sg = sg_ref[...]
        dso_ref[...] = dms * _silu(sg)
        dsg_ref[...] = (dms * so_ref[...] * _dsilu(sg)).astype(BF16)
        pg = pg_ref[...]
        plb = pl_ref[...].astype(BF16)
        yw = _dot(plb, wbd_ref[...])
        sc = sc_ref[...]
        dpg_ref[...] = (dmp * (yw * sc) * _dsilu(pg)).astype(BF16)
        dys = dmp * _silu(pg)
        dsc_ref[...] = dsc_ref[...] + jnp.sum(dys * yw, axis=0, keepdims=True)
        dyw = (dys * sc).astype(BF16)
        dpl_ref[...] = _dot_nt(dyw, wbd_ref[...])
        dwbd_ref[...] = dwbd_ref[...] + _dot_tn(plb, dyw)

    return pl.pallas_call(
        body, name="gate_bwd",
        grid=(S // tm,),
        in_specs=[pl.BlockSpec((tm, D), lambda i: (i, 0)),
                  pl.BlockSpec((D_MIX, D), lambda i: (0, 0)),
                  pl.BlockSpec((tm, FOX_W), lambda i: (i, 0)),
                  pl.BlockSpec((tm, FOX_W), lambda i: (i, OFF_FG // FOX_W)),
                  pl.BlockSpec((tm, SB_W), lambda i: (i, 0)),
                  pl.BlockSpec((tm, SB_W), lambda i: (i, OFF_SG // SB_W)),
                  pl.BlockSpec((tm, POOL_W), lambda i: (i, 0)),
                  pl.BlockSpec((tm, POOL_W), lambda i: (i, OFF_PG // POOL_W)),
                  pl.BlockSpec((POOL_W, POOL_W), lambda i: (0, 0)),
                  pl.BlockSpec((1, POOL_W), lambda i: (0, 0))],
        out_specs=[pl.BlockSpec((tm, FOX_W), lambda i: (i, 0)),
                   pl.BlockSpec((tm, FOX_W), lambda i: (i, 0)),
                   pl.BlockSpec((tm, SB_W), lambda i: (i, 0)),
                   pl.BlockSpec((tm, SB_W), lambda i: (i, 0)),
                   pl.BlockSpec((tm, POOL_W), lambda i: (i, 0)),
                   pl.BlockSpec((tm, POOL_W), lambda i: (i, 0)),
                   pl.BlockSpec((1, POOL_W), lambda i: (0, 0)),
                   pl.BlockSpec((POOL_W, POOL_W), lambda i: (0, 0))],
        out_shape=[jax.ShapeDtypeStruct((S, FOX_W), F32), jax.ShapeDtypeStruct((S, FOX_W), BF16),
                   jax.ShapeDtypeStruct((S, SB_W), F32), jax.ShapeDtypeStruct((S, SB_W), BF16),
                   jax.ShapeDtypeStruct((S, POOL_W), BF16), jax.ShapeDtypeStruct((S, POOL_W), F32),
                   jax.ShapeDtypeStruct((1, POOL_W), F32), jax.ShapeDtypeStruct((POOL_W, POOL_W), F32)],
        compiler_params=_cp(("arbitrary",), 40 << 20),
    )(dy, wout, fo, proj, so, proj, pooled, proj, wbd, scale)


def _matmul_tn(a, b, name):
    S, M = a.shape
    N = b.shape[1]
    tk = min(_TM, S)
    tn = min(512, N)

    def body(a_ref, b_ref, o_ref):
        @pl.when(pl.program_id(1) == 0)
        def _():
            o_ref[...] = jnp.zeros_like(o_ref)

        o_ref[...] = o_ref[...] + _dot_tn(a_ref[...].astype(BF16), b_ref[...].astype(BF16))

    return pl.pallas_call(
        body, name=name,
        grid=(N // tn, S // tk),
        in_specs=[pl.BlockSpec((tk, M), lambda j, k: (k, 0)), pl.BlockSpec((tk, tn), lambda j, k: (k, j))],
        out_specs=pl.BlockSpec((M, tn), lambda j, k: (0, j)),
        out_shape=jax.ShapeDtypeStruct((M, N), F32),
        compiler_params=_cp(("parallel", "arbitrary"), 40 << 20),
    )(a, b)


def _pool_bwd(dpooled):
    S = dpooled.shape[0]

    def body(d_ref, o_ref):
        d = d_ref[...]
        t = lax.broadcasted_iota(jnp.int32, d.shape, 0)
        lane = lax.broadcasted_iota(jnp.int32, d.shape, 1)
        cnt = jnp.minimum(t + 1, _pool_window_lanes(d.shape)).astype(F32)
        u = d / cnt

        def fwd(a, k):
            return jnp.where(t < S - k, pltpu.roll(a, S - k, 0), 0.0)

        s1 = u + fwd(u, 1)
        s2 = s1 + fwd(s1, 2)
        s4 = s2 + fwd(s2, 4)
        s8 = s4 + fwd(s4, 8)
        win = jnp.where(lane < 64, s1, jnp.where(lane < 128, s2, jnp.where(lane < 192, s4, s8)))
        o_ref[...] = (win - d).astype(BF16)

    return pl.pallas_call(
        body, name="pool_bwd",
        grid=(1,),
        in_specs=[pl.BlockSpec((S, POOL_W), lambda i: (0, 0))],
        out_specs=pl.BlockSpec((S, POOL_W), lambda i: (0, 0)),
        out_shape=jax.ShapeDtypeStruct((S, POOL_W), BF16),
        compiler_params=_cp(("arbitrary",), _VMEM_BIG),
    )(dpooled)


def _fox_bwd(qs, kn, proj, dfo, fo, lse, cqb, crow4):
    S = qs.shape[0]
    T = min(_T, S)
    nq = S // T

    def body(q_ref, k_ref, v_ref, do_ref, o_ref, lse_ref, cq_ref, cr_ref,
             dq_ref, dk_ref, dv_ref, dck_ref, dcq_ref, qa, qb, ka, kb, vb, doa, dob, dva, dqa, dcs):
        lane_s = lax.broadcasted_iota(jnp.int32, (S, 128), 1) < HEAD_DIM
        q = q_ref[...]
        zq = jnp.zeros_like(q)
        qa[...] = jnp.where(lane_s, q, zq)
        qb[...] = jnp.where(lane_s, zq, q)
        k = k_ref[...]
        ka[...] = jnp.where(lane_s, k, zq)
        kb[...] = jnp.where(lane_s, zq, k)
        vb[...] = v_ref[...].astype(BF16)
        do = do_ref[...].astype(BF16)
        doa[...] = jnp.where(lane_s, do, zq)
        dob[...] = jnp.where(lane_s, zq, do)
        dk_ref[...] = jnp.zeros((S, 128), F32)
        dva[...] = jnp.zeros((S, 128), F32)
        dck_ref[...] = jnp.zeros((8, S), F32)
        lane_t = lax.broadcasted_iota(jnp.int32, (T, 128), 1) < HEAD_DIM
        causal = (lax.broadcasted_iota(jnp.int32, (T, T), 1) <= lax.broadcasted_iota(jnp.int32, (T, T), 0))

        def head(hrow, qh, kfull, kh, v, doh, cqh, ckh, lseh, dlth, c0, masked):
            s = _dot_nt(qh, kfull) + cqh - ckh
            if masked:
                s = jnp.where(causal, s, NEG)
            p = jnp.exp(s - lseh)
            dp = _dot_nt(doh, v)
            ds = p * (dp - dlth)
            pb = p.astype(BF16)
            dsb = ds.astype(BF16)
            dva[pl.ds(c0, T), :] = dva[pl.ds(c0, T), :] + _dot_tn(pb, doh)
            dk_ref[pl.ds(c0, T), :] = dk_ref[pl.ds(c0, T), :] + _dot_tn(dsb, qh)
            dqa[...] = dqa[...] + _dot(dsb, kh)
            dck_ref[hrow:hrow + 1, pl.ds(c0, T)] = (dck_ref[hrow:hrow + 1, pl.ds(c0, T)]
                                                    - jnp.sum(ds, axis=0, keepdims=True))
            return jnp.sum(ds, axis=1, keepdims=True)

        def kv(j, r0, dlta, dltb, masked):
            c0 = pl.multiple_of(j * T, T)
            kfull = k_ref[pl.ds(c0, T), :]
            v = vb[pl.ds(c0, T), :]
            ck = cr_ref[:, pl.ds(c0, T)]
            cq = cq_ref[pl.ds(r0, T), :]
            ls = lse_ref[pl.ds(r0, T), :]
            rsa = head(0, qa[pl.ds(r0, T), :], kfull, ka[pl.ds(c0, T), :], v, doa[pl.ds(r0, T), :],
                       cq[:, 0:1], ck[0:1, :], ls[:, 0:1], dlta, c0, masked)
            rsb = head(1, qb[pl.ds(r0, T), :], kfull, kb[pl.ds(c0, T), :], v, dob[pl.ds(r0, T), :],
                       cq[:, 64:65], ck[1:2, :], ls[:, 64:65], dltb, c0, masked)
            dcs[...] = dcs[...] + jnp.where(lane_t, rsa, rsb)

        def qblk(i, carry):
            r0 = pl.multiple_of(i * T, T)
            d = do_ref[pl.ds(r0, T), :] * o_ref[pl.ds(r0, T), :]
            dlta = jnp.sum(jnp.where(lane_t, d, 0.0), axis=1, keepdims=True)
            dltb = jnp.sum(jnp.where(lane_t, 0.0, d), axis=1, keepdims=True)
            dqa[...] = jnp.zeros((T, 128), F32)
            dcs[...] = jnp.zeros((T, 128), F32)

            def inner(j, c):
                kv(j, r0, dlta, dltb, False)
                return c

            lax.fori_loop(0, i, inner, 0)
            kv(i, r0, dlta, dltb, True)
            dq_ref[pl.ds(r0, T), :] = dqa[...]
            dcq_ref[pl.ds(r0, T), :] = dcs[...]
            return carry

        lax.fori_loop(0, nq, qblk, 0)
        dv_ref[...] = dva[...].astype(BF16)

    blk = lambda off: pl.BlockSpec((S, 128), lambda p: (0, off + p))
    row4 = pl.BlockSpec((None, 8, S), lambda p: (p, 0, 0))
    return pl.pallas_call(
        body, name="fox_bwd",
        grid=(FOX_W // 128,),
        in_specs=[blk(0), blk(0), blk(OFF_FV // 128), blk(0), blk(0), blk(0), blk(0), row4],
        out_specs=[blk(0), blk(0), blk(0), row4, blk(0)],
        out_shape=[jax.ShapeDtypeStruct((S, FOX_W), F32), jax.ShapeDtypeStruct((S, FOX_W), F32),
                   jax.ShapeDtypeStruct((S, FOX_W), BF16), jax.ShapeDtypeStruct((FOX_W // 128, 8, S), F32),
                   jax.ShapeDtypeStruct((S, FOX_W), F32)],
        scratch_shapes=[pltpu.VMEM((S, 128), BF16)] * 7 + [pltpu.VMEM((S, 128), F32)] + [pltpu.VMEM((T, 128), F32)] * 2,
        compiler_params=_cp(("arbitrary",), _VMEM_BIG),
    )(qs, kn, proj, dfo, fo, lse, cqb, crow4)


def _sb_bwd(proj, dso, ltot, triu):
    S = proj.shape[0]
    T = triu.shape[0]
    nq = S // T

    def body(q_ref, k_ref, v_ref, do_ref, lt_ref, tri_ref, dq_ref, dk_ref, dv_ref,
             qa, qb, k2, ka, kb, vb, doa, dob, dka, dva, dqa, ra, rb, ga, gb):
        lane_s = lax.broadcasted_iota(jnp.int32, (S, 128), 1) < HEAD_DIM
        q = (q_ref[...] * Q_SCALE).astype(BF16)
        zq = jnp.zeros_like(q)
        qa[...] = jnp.where(lane_s, q, zq)
        qb[...] = jnp.where(lane_s, zq, q)
        k = k_ref[...].astype(BF16)
        k2[...] = k
        ka[...] = jnp.where(lane_s, k, zq)
        kb[...] = jnp.where(lane_s, zq, k)
        vb[...] = v_ref[...].astype(BF16)
        do = do_ref[...].astype(BF16)
        doa[...] = jnp.where(lane_s, do, zq)
        dob[...] = jnp.where(lane_s, zq, do)
        dka[...] = jnp.zeros((S, 128), F32)
        dva[...] = jnp.zeros((S, 128), F32)
        lane_t = lax.broadcasted_iota(jnp.int32, (T, 128), 1) < HEAD_DIM
        strict = (lax.broadcasted_iota(jnp.int32, (T, T), 1) < lax.broadcasted_iota(jnp.int32, (T, T), 0))

        def head(qh, kfull, kh, v, doh, r_ref, g_ref, lth, c0, masked):
            z = _dot_nt(qh, kfull)
            e, sp = _softplus_parts(z)
            lb = -sp
            if masked:
                lb = jnp.where(strict, lb, 0.0)
            tri = tri_ref[...]
            pre = _mm2(lb, tri)
            r = r_ref[:, 0:1]
            a = jnp.exp(z + lb + ((lth - r) - pre))
            if masked:
                a = jnp.where(strict, a, 0.0)
            da = _dot_nt(doh, v)
            g = a * da
            gpre = _mm2(g, tri)
            gc = g_ref[:, 0:1]
            big_g = gc + (gpre - g)
            inv = 1.0 / (1.0 + e)
            pos = z >= 0.0
            sig = jnp.where(pos, 1.0, e) * inv
            oms = jnp.where(pos, e, 1.0) * inv
            dz = g * oms - sig * big_g
            if masked:
                dz = jnp.where(strict, dz, 0.0)
            dzb = dz.astype(BF16)
            dqa[...] = dqa[...] + _dot(dzb, kh)
            dka[pl.ds(c0, T), :] = dka[pl.ds(c0, T), :] + _dot_tn(dzb, qh)
            dva[pl.ds(c0, T), :] = dva[pl.ds(c0, T), :] + _dot_tn(a.astype(BF16), doh)
            r_ref[...] = jnp.broadcast_to(r + pre[:, T - 1:T], (T, 128))
            g_ref[...] = jnp.broadcast_to(gc + gpre[:, T - 1:T], (T, 128))

        def kv(j, r0, lta, ltb, masked):
            c0 = pl.multiple_of(j * T, T)
            kfull = k2[pl.ds(c0, T), :]
            v = vb[pl.ds(c0, T), :]
            head(qa[pl.ds(r0, T), :], kfull, ka[pl.ds(c0, T), :], v, doa[pl.ds(r0, T), :], ra, ga, lta, c0, masked)
            head(qb[pl.ds(r0, T), :], kfull, kb[pl.ds(c0, T), :], v, dob[pl.ds(r0, T), :], rb, gb, ltb, c0, masked)

        def qblk(i, carry):
            r0 = pl.multiple_of(i * T, T)
            lt = lt_ref[pl.ds(r0, T), :]
            lta = lt[:, 0:1]
            ltb = lt[:, 64:65]
            zt = jnp.zeros((T, 128), F32)
            dqa[...] = zt
            ra[...] = zt
            rb[...] = zt
            ga[...] = zt
            gb[...] = zt

            def inner(j, c):
                kv(j, r0, lta, ltb, False)
                return c

            lax.fori_loop(0, i, inner, 0)
            kv(i, r0, lta, ltb, True)
            dq_ref[pl.ds(r0, T), :] = (dqa[...] * Q_SCALE).astype(BF16)
            return carry

        lax.fori_loop(0, nq, qblk, 0)
        dk_ref[...] = dka[...].astype(BF16)
        dv_ref[...] = dva[...].astype(BF16)

    blk = lambda off: pl.BlockSpec((S, 128), lambda p: (0, off + p))
    return pl.pallas_call(
        body, name="sb_bwd",
        grid=(SB_W // 128,),
        in_specs=[blk(OFF_SQ // 128), blk(OFF_SK // 128), blk(OFF_SV // 128), blk(0), blk(0),
                  pl.BlockSpec((T, T), lambda p: (0, 0))],
        out_specs=[blk(0), blk(0), blk(0)],
        out_shape=[jax.ShapeDtypeStruct((S, SB_W), BF16)] * 3,
        scratch_shapes=([pltpu.VMEM((S, 128), BF16)] * 8 + [pltpu.VMEM((S, 128), F32)] * 2
                        + [pltpu.VMEM((T, 128), F32)] * 5),
        compiler_params=_cp(("arbitrary",), _VMEM_BIG),
    )(proj, proj, proj, dso, ltot, triu)


def _head_norm_bwd(x, g, dy, bd):
    ss = _mm2(x * x, bd)
    r = lax.rsqrt(ss * (1.0 / HEAD_DIM) + EPS)
    xr = x * r
    gdy = g * dy
    m = _mm2(xr * gdy, bd) * (1.0 / HEAD_DIM)
    return r * (gdy - xr * m), dy * xr


def _qk_bwd(dqs, dkn, proj, pff, bfp, gq, gk, bd, dccol, triu):
    S = proj.shape[0]
    T = triu.shape[0]
    n = S // T
    rev = lambda col: (lambda i: (n - 1 - i, col))

    def body(dq_ref, dk_ref, q_ref, k_ref, ff_ref, b_ref, gq_ref, gk_ref, bd_ref, dc_ref, tri_ref,
             dfq_ref, dfk_ref, dff_ref, dgq_ref, dgk_ref, dbf_ref, carry):
        @pl.when(pl.program_id(0) == 0)
        def _():
            carry[...] = jnp.zeros_like(carry)
            dgq_ref[...] = jnp.zeros_like(dgq_ref)
            dgk_ref[...] = jnp.zeros_like(dgk_ref)
            dbf_ref[...] = jnp.zeros_like(dbf_ref)

        bdv = bd_ref[...]
        dxq, gq_rows = _head_norm_bwd(q_ref[...], gq_ref[...], dq_ref[...] * Q_SCALE, bdv)
        dfq_ref[...] = dxq.astype(BF16)
        dgq_ref[...] = dgq_ref[...] + jnp.sum(gq_rows, axis=0, keepdims=True)
        dxk, gk_rows = _head_norm_bwd(k_ref[...], gk_ref[...], dk_ref[...], bdv)
        dfk_ref[...] = dxk.astype(BF16)
        dgk_ref[...] = dgk_ref[...] + jnp.sum(gk_rows, axis=0, keepdims=True)
        dlf = _mm3(dc_ref[...], tri_ref[...], left=True) + carry[0:1, :]
        carry[0:1, :] = dlf[0:1, :]
        u = ff_ref[...] + b_ref[...]
        lane = lax.broadcasted_iota(jnp.int32, u.shape, 1)
        dff = jnp.where(lane < N_FF, dlf * _sigmoid(-u), 0.0)
        dff_ref[...] = dff.astype(BF16)
        dbf_ref[...] = dbf_ref[...] + jnp.sum(dff, axis=0, keepdims=True)

    return pl.pallas_call(
        body, name="qk_bwd",
        grid=(n,),
        in_specs=[pl.BlockSpec((T, FOX_W), rev(0)), pl.BlockSpec((T, FOX_W), rev(0)),
                  pl.BlockSpec((T, FOX_W), rev(OFF_FQ // FOX_W)), pl.BlockSpec((T, FOX_W), rev(OFF_FK // FOX_W)),
                  pl.BlockSpec((T, N_FFPAD), rev(0)),
                  pl.BlockSpec((1, N_FFPAD), lambda i: (0, 0)),
                  pl.BlockSpec((1, FOX_W), lambda i: (0, 0)), pl.BlockSpec((1, FOX_W), lambda i: (0, 0)),
                  pl.BlockSpec((FOX_W, FOX_W), lambda i: (0, 0)),
                  pl.BlockSpec((T, N_FFPAD), rev(0)),
                  pl.BlockSpec((T, T), lambda i: (0, 0))],
        out_specs=[pl.BlockSpec((T, FOX_W), rev(0)), pl.BlockSpec((T, FOX_W), rev(0)),
                   pl.BlockSpec((T, N_FFPAD), rev(0)),
                   pl.BlockSpec((1, FOX_W), lambda i: (0, 0)), pl.BlockSpec((1, FOX_W), lambda i: (0, 0)),
                   pl.BlockSpec((1, N_FFPAD), lambda i: (0, 0))],
        out_shape=[jax.ShapeDtypeStruct((S, FOX_W), BF16), jax.ShapeDtypeStruct((S, FOX_W), BF16),
                   jax.ShapeDtypeStruct((S, N_FFPAD), BF16),
                   jax.ShapeDtypeStruct((1, FOX_W), F32), jax.ShapeDtypeStruct((1, FOX_W), F32),
                   jax.ShapeDtypeStruct((1, N_FFPAD), F32)],
        scratch_shapes=[pltpu.VMEM((8, N_FFPAD), F32)],
        compiler_params=_cp(("arbitrary",), 40 << 20),
    )(dqs, dkn, proj, proj, pff, bfp, gq, gk, bd, dccol, triu)


def _inproj_bwd_dx(dpm, dff, wm, wff, x, g, dy):
    S, D = x.shape
    tm = min(256, S)

    def body(dp_ref, dff_ref, w_ref, wff_ref, x_ref, g_ref, dy_ref, dx_ref, dg_ref):
        @pl.when(pl.program_id(0) == 0)
        def _():
            dg_ref[...] = jnp.zeros_like(dg_ref)

        dh = _dot_nt(dp_ref[...], w_ref[...]) + _dot_nt(dff_ref[...], wff_ref[...])
        xv = x_ref[...]
        r = _rms_rows(xv)
        xr = xv * r
        dg_ref[...] = dg_ref[...] + jnp.sum(dh * xr, axis=0, keepdims=True)
        gdh = g_ref[...] * dh
        m = jnp.mean(gdh * xr, axis=-1, keepdims=True)
        dx_ref[...] = dy_ref[...] + r * (gdh - xr * m)

    return pl.pallas_call(
        body, name="inproj_bwd_dx",
        grid=(S // tm,),
        in_specs=[pl.BlockSpec((tm, N_MAIN), lambda i: (i, 0)),
                  pl.BlockSpec((tm, N_FFPAD), lambda i: (i, 0)),
                  pl.BlockSpec((D, N_MAIN), lambda i: (0, 0)),
                  pl.BlockSpec((D, N_FFPAD), lambda i: (0, 0)),
                  pl.BlockSpec((tm, D), lambda i: (i, 0)),
                  pl.BlockSpec((1, D), lambda i: (0, 0)),
                  pl.BlockSpec((tm, D), lambda i: (i, 0))],
        out_specs=[pl.BlockSpec((tm, D), lambda i: (i, 0)), pl.BlockSpec((1, D), lambda i: (0, 0))],
        out_shape=[jax.ShapeDtypeStruct((S, D), F32), jax.ShapeDtypeStruct((1, D), F32)],
        compiler_params=_cp(("arbitrary",), 48 << 20),
    )(dpm, dff, wm, wff, x, g, dy)


def _inproj_bwd_dw(x, g, dpm, dff):
    S, D = x.shape
    tk = min(_TM, S)
    tn = 512

    def body(x_ref, g_ref, dp_ref, dff_ref, dw_ref, dwff_ref):
        j, k = pl.program_id(0), pl.program_id(1)

        @pl.when(k == 0)
        def _():
            dw_ref[...] = jnp.zeros_like(dw_ref)

        @pl.when((k == 0) & (j == 0))
        def _():
            dwff_ref[...] = jnp.zeros_like(dwff_ref)

        xv = x_ref[...]
        h = ((xv * _rms_rows(xv)) * g_ref[...]).astype(BF16)
        dw_ref[...] = dw_ref[...] + _dot_tn(h, dp_ref[...])

        @pl.when(j == 0)
        def _():
            dwff_ref[...] = dwff_ref[...] + _dot_tn(h, dff_ref[...])

    return pl.pallas_call(
        body, name="inproj_bwd_dw",
        grid=(N_MAIN // tn, S // tk),
        in_specs=[pl.BlockSpec((tk, D), lambda j, k: (k, 0)),
                  pl.BlockSpec((1, D), lambda j, k: (0, 0)),
                  pl.BlockSpec((tk, tn), lambda j, k: (k, j)),
                  pl.BlockSpec((tk, N_FFPAD), lambda j, k: (k, 0))],
        out_specs=[pl.BlockSpec((D, tn), lambda j, k: (0, j)), pl.BlockSpec((D, N_FFPAD), lambda j, k: (0, 0))],
        out_shape=[jax.ShapeDtypeStruct((D, N_MAIN), F32), jax.ShapeDtypeStruct((D, N_FFPAD), F32)],
        compiler_params=_cp(("arbitrary", "arbitrary"), 40 << 20),
    )(x, g, dpm, dff)


def _constants(T):
    tril = jnp.tril(jnp.ones((T, T), F32)).astype(BF16)
    hid = jnp.arange(FOX_W) // HEAD_DIM
    bd = (hid[:, None] == hid[None, :]).astype(BF16)
    ex = (jnp.arange(N_FFPAD)[:, None] == hid[None, :]).astype(BF16)
    return tril, tril.T, bd, ex


def _crow4(ccol):
    S = ccol.shape[0]
    c = ccol[:, :FOX_HEADS].T.reshape(FOX_HEADS // 2, 2, S)
    return jnp.pad(c, ((0, 0), (0, 6), (0, 0)))


def _layer_fwd(x, lw, consts):
    tril, _, bd, ex = consts
    proj, pff = _inproj_fwd(x, lw["g"], lw["wm"], lw["wff"])
    qs, kn, ccol, cqb = _fox_prep(proj, pff, lw["bfp"], lw["gq"], lw["gk"], bd, ex, tril)
    crow4 = _crow4(ccol)
    fo, lse = _fox_fwd(qs, kn, proj, cqb, crow4)
    so, ltot = _sb_fwd(proj, tril)
    pooled = _pool_fwd(proj)
    y, mixed = _mix_out(fo, so, pooled, proj, lw["wbd"], lw["scale"], lw["wout"], x)
    return y, (x, proj, pff, qs, kn, cqb, crow4, fo, lse, so, ltot, pooled, mixed)


def _layer_bwd(dy, saved, lw, consts):
    tril, triu, bd, _ = consts
    x, proj, pff, qs, kn, cqb, crow4, fo, lse, so, ltot, pooled, mixed = saved
    S = x.shape[0]
    dfo, dfg, dso, dsg, dpg, dpooled, dscale, dwbd = _gate_bwd(dy, lw["wout"], fo, so, pooled, proj, lw["wbd"], lw["scale"])
    dwout = _matmul_tn(mixed, dy, "dw_out")
    dpx = _pool_bwd(dpooled)
    dqs, dkn, dfv, dck4, dcq = _fox_bwd(qs, kn, proj, dfo, fo, lse, cqb, crow4)
    dsq, dsk, dsv = _sb_bwd(proj, dso, ltot, triu)
    dc8 = dcq[:, ::HEAD_DIM] + dck4[:, :2, :].reshape(FOX_HEADS, S).T
    dccol = jnp.pad(dc8, ((0, 0), (0, N_FFPAD - FOX_HEADS)))
    dfq, dfk, dff, dgq, dgk, dbf = _qk_bwd(dqs, dkn, proj, pff, lw["bfp"], lw["gq"], lw["gk"], bd, dccol, triu)
    dpm = jnp.concatenate([dfq, dfk, dfv, dfg, dpx, dpg, dsq, dsk, dsv, dsg], axis=1)
    dx, dng = _inproj_bwd_dx(dpm, dff, lw["wm"], lw["wff"], x, lw["g"], dy)
    dwm, dwff = _inproj_bwd_dw(x, lw["g"], dpm, dff)
    dwin = jnp.concatenate([dwm[:, :OFF_PX], dwff[:, :N_FF], dwm[:, OFF_PX:]], axis=1)
    grads = {
        "norm_g": dng[0],
        "w_in": dwin,
        "b_f": dbf[0, :N_FF],
        "q_norm_g": dgq[0].reshape(FOX_HEADS, HEAD_DIM).sum(0),
        "k_norm_g": dgk[0].reshape(FOX_HEADS, HEAD_DIM).sum(0),
        "w_pool": jnp.stack([dwbd[64 * i:64 * i + 64, 64 * i:64 * i + 64] for i in range(4)]),
        "pool_scale": dscale[0],
        "w_out": dwout,
    }
    return dx, grads


def _layer_weights(l, norm_g, win_full, b_f, q_norm_g, k_norm_g, w_pool, pool_scale, wout_full):
    w = win_full[l]
    D = w.shape[0]
    wm = jnp.concatenate([w[:, :2048], w[:, 2048 + N_FF:]], axis=1)
    wff = jnp.pad(w[:, 2048:2048 + N_FF], ((0, 0), (0, N_FFPAD - N_FF)))
    wbd = jnp.zeros((POOL_W, POOL_W), F32)
    for i in range(4):
        wbd = wbd.at[64 * i:64 * i + 64, 64 * i:64 * i + 64].set(w_pool[l, i])
    return {
        "g": norm_g[l].reshape(1, D),
        "wm": wm, "wff": wff,
        "bfp": jnp.pad(b_f[l], (0, N_FFPAD - N_FF)).reshape(1, N_FFPAD),
        "gq": jnp.tile(q_norm_g[l], FOX_HEADS).reshape(1, FOX_W),
        "gk": jnp.tile(k_norm_g[l], FOX_HEADS).reshape(1, FOX_W),
        "wbd": wbd.astype(BF16),
        "scale": pool_scale[l].reshape(1, POOL_W),
        "wout": wout_full[l],
    }


def _local_step(x, target, norm_g, win_full, b_f, q_norm_g, k_norm_g, w_pool, pool_scale, wout_full):
    L = norm_g.shape[0]
    T = min(_T, x.shape[0])
    consts = _constants(T)
    lws = [_layer_weights(l, norm_g, win_full, b_f, q_norm_g, k_norm_g, w_pool, pool_scale, wout_full)
           for l in range(L)]
    saved = []
    h = x
    for l in range(L):
        h, sv = _layer_fwd(h, lws[l], consts)
        saved.append(sv)
    dy, loss = _loss_head(h, target)
    grads = [None] * L
    for l in reversed(range(L)):
        dy, grads[l] = _layer_bwd(dy, saved[l], lws[l], consts)
    stacked = {k: jnp.stack([grads[l][k] for l in range(L)]) for k in grads[0]}
    return loss, dy, stacked


def _mesh_pos():
    return lax.axis_index("x"), lax.axis_index("y"), lax.axis_index("c")


_FLIPS = [(0, 0, 1), (1, 0, 0), (0, 1, 0), (1, 1, 0), (1, 0, 1), (0, 1, 1), (1, 1, 1)]


def _peers():
    x, y, c = _mesh_pos()
    out = []
    for fx, fy, fc in _FLIPS:
        px = 1 - x if fx else x
        py = 1 - y if fy else y
        pc = 1 - c if fc else c
        out.append(((px, py, pc), 4 * px + 2 * py + pc))
    return out, 4 * x + 2 * y + c


def _all_gather_pair(a, b):
    ANY = pl.BlockSpec(memory_space=pl.ANY)

    def body(a_ref, b_ref, ga_ref, gb_ref, send_sems, recv_sems, loc_sems):
        peers, me = _peers()
        la = pltpu.make_async_copy(a_ref, ga_ref.at[me], loc_sems.at[0])
        lb = pltpu.make_async_copy(b_ref, gb_ref.at[me], loc_sems.at[1])
        la.start()
        lb.start()
        copies = []
        for k, (dev, _) in enumerate(peers):
            for t, (src, dst) in enumerate(((a_ref, ga_ref), (b_ref, gb_ref))):
                cp = pltpu.make_async_remote_copy(
                    src_ref=src, dst_ref=dst.at[me], send_sem=send_sems.at[2 * k + t],
                    recv_sem=recv_sems.at[2 * k + t], device_id=dev, device_id_type=pl.DeviceIdType.MESH)
                cp.start()
                copies.append(cp)
        for cp in copies:
            cp.wait_recv()
        for cp in copies:
            cp.wait_send()
        la.wait()
        lb.wait()

    return pl.pallas_call(
        body, name="gather_weights",
        in_specs=[ANY, ANY], out_specs=[ANY, ANY],
        out_shape=[jax.ShapeDtypeStruct((N_DEV,) + a.shape, a.dtype), jax.ShapeDtypeStruct((N_DEV,) + b.shape, b.dtype)],
        scratch_shapes=[pltpu.SemaphoreType.DMA((14,)), pltpu.SemaphoreType.DMA((14,)), pltpu.SemaphoreType.DMA((2,))],
    )(a, b)


def _all_to_all_pair(a, b):
    ANY = pl.BlockSpec(memory_space=pl.ANY)

    def body(a_ref, b_ref, ra_ref, rb_ref, send_sems, recv_sems, loc_sems):
        peers, me = _peers()
        la = pltpu.make_async_copy(a_ref.at[me], ra_ref.at[me], loc_sems.at[0])
        lb = pltpu.make_async_copy(b_ref.at[me], rb_ref.at[me], loc_sems.at[1])
        la.start()
        lb.start()
        copies = []
        for k, (dev, idx) in enumerate(peers):
            for t, (src, dst) in enumerate(((a_ref, ra_ref), (b_ref, rb_ref))):
                cp = pltpu.make_async_remote_copy(
                    src_ref=src.at[idx], dst_ref=dst.at[me], send_sem=send_sems.at[2 * k + t],
                    recv_sem=recv_sems.at[2 * k + t], device_id=dev, device_id_type=pl.DeviceIdType.MESH)
                cp.start()
                copies.append(cp)
        for cp in copies:
            cp.wait_recv()
        for cp in copies:
            cp.wait_send()
        la.wait()
        lb.wait()

    return pl.pallas_call(
        body, name="exchange_grads",
        in_specs=[ANY, ANY], out_specs=[ANY, ANY],
        out_shape=[jax.ShapeDtypeStruct(a.shape, a.dtype), jax.ShapeDtypeStruct(b.shape, b.dtype)],
        scratch_shapes=[pltpu.SemaphoreType.DMA((14,)), pltpu.SemaphoreType.DMA((14,)), pltpu.SemaphoreType.DMA((2,))],
    )(a, b)


def _adam_math(w, g, m, v):
    m_new = ADAM_B1 * m + (1.0 - ADAM_B1) * g
    v_new = ADAM_B2 * v + (1.0 - ADAM_B2) * (g * g)
    m_hat = m_new / (1.0 - ADAM_B1 ** ADAM_STEP)
    v_hat = v_new / (1.0 - ADAM_B2 ** ADAM_STEP)
    delta = -ADAM_LR * (m_hat / (jnp.sqrt(v_hat) + ADAM_EPS) + ADAM_WD * w)
    return delta, m_new, v_new


def _sum_adamw(gparts, w, m, v, name):
    L, R, C = w.shape
    tr = min(128, R)

    def body(gp_ref, w_ref, m_ref, v_ref, g_ref, d_ref, nm_ref, nv_ref):
        g = gp_ref[0]
        for s in range(1, N_DEV):
            g = g + gp_ref[s]
        d, mn, vn = _adam_math(w_ref[...], g, m_ref[...], v_ref[...])
        g_ref[...] = g
        d_ref[...] = d
        nm_ref[...] = mn
        nv_ref[...] = vn

    blk = pl.BlockSpec((None, tr, C), lambda l, r: (l, r, 0))
    return pl.pallas_call(
        body, name=name,
        grid=(L, R // tr),
        in_specs=[pl.BlockSpec((N_DEV, None, tr, C), lambda l, r: (0, l, r, 0)), blk, blk, blk],
        out_specs=[blk, blk, blk, blk],
        out_shape=[jax.ShapeDtypeStruct((L, R, C), F32)] * 4,
        compiler_params=_cp(("parallel", "parallel"), 40 << 20),
    )(gparts, w, m, v)


def _small_update(gpack, wpack, mpack, vpack):
    R = gpack.shape[0]
    VM = pl.BlockSpec(memory_space=pltpu.VMEM)

    def body(g_ref, w_ref, m_ref, v_ref, gs_ref, d_ref, nm_ref, nv_ref, buf, send_sems, recv_sems):
        peers, me = _peers()
        buf[me] = g_ref[...]
        copies = []
        for k, (dev, _) in enumerate(peers):
            cp = pltpu.make_async_remote_copy(
                src_ref=g_ref, dst_ref=buf.at[me], send_sem=send_sems.at[k], recv_sem=recv_sems.at[k],
                device_id=dev, device_id_type=pl.DeviceIdType.MESH)
            cp.start()
            copies.append(cp)
        for cp in copies:
            cp.wait_recv()
        for cp in copies:
            cp.wait_send()
        g = buf[0]
        for s in range(1, N_DEV):
            g = g + buf[s]
        d, mn, vn = _adam_math(w_ref[...], g, m_ref[...], v_ref[...])
        gs_ref[...] = g
        d_ref[...] = d
        nm_ref[...] = mn
        nv_ref[...] = vn

    return pl.pallas_call(
        body, name="small_update",
        in_specs=[VM] * 4, out_specs=[VM] * 4,
        out_shape=[jax.ShapeDtypeStruct((R, 128), F32)] * 4,
        scratch_shapes=[pltpu.VMEM((N_DEV, R, 128), F32), pltpu.SemaphoreType.DMA((7,)), pltpu.SemaphoreType.DMA((7,))],
        compiler_params=_cp(None, 40 << 20),
    )(gpack, wpack, mpack, vpack)


_SMALL = ("norm_g", "b_f", "q_norm_g", "k_norm_g", "w_pool", "pool_scale")


def _pack(parts):
    flat = jnp.concatenate([p.reshape(-1) for p in parts])
    n = flat.shape[0]
    rows = -(-n // (8 * 128)) * 8
    return jnp.pad(flat, (0, rows * 128 - n)).reshape(rows, 128)


def _unpack(packed, like):
    flat = packed.reshape(-1)
    out, o = [], 0
    for p in like:
        out.append(flat[o:o + p.size].reshape(p.shape))
        o += p.size
    return out


def kernel(x, norm_g, w_in, b_f, q_norm_g, k_norm_g, w_pool, pool_scale, w_out, loss_target, m_norm_g, m_w_in, m_b_f, m_q_norm_g, m_k_norm_g, m_w_pool, m_pool_scale, m_w_out, v_norm_g, v_w_in, v_b_f, v_q_norm_g, v_k_norm_g, v_w_pool, v_pool_scale, v_w_out):
    L, D, cin = w_in.shape
    rout = w_out.shape[1]

    gin, gout = _all_gather_pair(w_in.astype(BF16), w_out.astype(BF16))
    win_full = gin.transpose(1, 2, 0, 3).reshape(L, D, N_DEV * cin)
    wout_full = gout.transpose(1, 0, 2, 3).reshape(L, N_DEV * rout, D)

    loss_local, dx, g = _local_step(x[0], loss_target[0], norm_g, win_full, b_f, q_norm_g, k_norm_g,
                                    w_pool, pool_scale, wout_full)
    loss = lax.psum(loss_local, MESH_AXES)

    gin_parts = g["w_in"].reshape(L, D, N_DEV, cin).transpose(2, 0, 1, 3)
    gout_parts = g["w_out"].reshape(L, N_DEV, rout, D).transpose(1, 0, 2, 3)
    rin, rout_parts = _all_to_all_pair(gin_parts, gout_parts)
    g_win, d_win, nm_win, nv_win = _sum_adamw(rin, w_in, m_w_in, v_w_in, "adamw_w_in")
    g_wout, d_wout, nm_wout, nv_wout = _sum_adamw(rout_parts, w_out, m_w_out, v_w_out, "adamw_w_out")

    ws = dict(norm_g=norm_g, b_f=b_f, q_norm_g=q_norm_g, k_norm_g=k_norm_g, w_pool=w_pool, pool_scale=pool_scale)
    ms = dict(norm_g=m_norm_g, b_f=m_b_f, q_norm_g=m_q_norm_g, k_norm_g=m_k_norm_g, w_pool=m_w_pool, pool_scale=m_pool_scale)
    vs = dict(norm_g=v_norm_g, b_f=v_b_f, q_norm_g=v_q_norm_g, k_norm_g=v_k_norm_g, w_pool=v_w_pool, pool_scale=v_pool_scale)
    like = [ws[k] for k in _SMALL]
    gs_p, d_p, nm_p, nv_p = _small_update(_pack([g[k] for k in _SMALL]), _pack(like),
                                          _pack([ms[k] for k in _SMALL]), _pack([vs[k] for k in _SMALL]))
    gs = dict(zip(_SMALL, _unpack(gs_p, like)))
    ds = dict(zip(_SMALL, _unpack(d_p, like)))
    nms = dict(zip(_SMALL, _unpack(nm_p, like)))
    nvs = dict(zip(_SMALL, _unpack(nv_p, like)))
    gs["w_in"], ds["w_in"], nms["w_in"], nvs["w_in"] = g_win, d_win, nm_win, nv_win
    gs["w_out"], ds["w_out"], nms["w_out"], nvs["w_out"] = g_wout, d_wout, nm_wout, nv_wout

    order = ("norm_g", "w_in", "b_f", "q_norm_g", "k_norm_g", "w_pool", "pool_scale", "w_out")
    return (loss, dx[None], *[gs[k] for k in order], *[ds[k] for k in order],
            *[nms[k] for k in order], *[nvs[k] for k in order])
```

```python
import functools

import jax
import jax.numpy as jnp
from jax import lax
from jax.experimental import pallas as pl
from jax.experimental.pallas import tpu as pltpu

F32 = jnp.float32
BF16 = jnp.bfloat16

EPS = 1e-6
NEG = -1e30
HEAD_DIM = 64
FOX_HEADS = 8
FOX_W = 512
POOL_W = 256
SB_W = 256
D_MIX = 1024
N_FF = 8
N_MAIN = 3584
N_FFPAD = 128
OFF_FQ, OFF_FK, OFF_FV, OFF_FG = 0, 512, 1024, 1536
OFF_PX, OFF_PG = 2048, 2304
OFF_SQ, OFF_SK, OFF_SV, OFF_SG = 2560, 2816, 3072, 3328
D_IN = 3592
Q_SCALE = HEAD_DIM ** -0.5

ADAM_LR = 0.001
ADAM_B1 = 0.9
ADAM_B2 = 0.999
ADAM_EPS = 1e-08
ADAM_WD = 0.01
ADAM_STEP = 10

N_DEV = 8
MESH_AXES = ("x", "y", "c")

_T = 256
_TM = 512
_VMEM_BIG = 56 << 20


def _cp(sem=None, vmem=None):
    kw = {}
    if sem is not None:
        kw["dimension_semantics"] = sem
    if vmem is not None:
        kw["vmem_limit_bytes"] = vmem
    return pltpu.CompilerParams(**kw)


def _dot(a, b):
    return jnp.dot(a, b, preferred_element_type=F32)


def _dot_nt(a, b):
    return lax.dot_general(a, b, (((1,), (1,)), ((), ())), preferred_element_type=F32)


def _dot_tn(a, b):
    return lax.dot_general(a, b, (((0,), (0,)), ((), ())), preferred_element_type=F32)


def _mm2(v, m, left=False):
    hi = v.astype(BF16)
    lo = (v - hi.astype(F32)).astype(BF16)
    if left:
        return _dot(m, hi) + _dot(m, lo)
    return _dot(hi, m) + _dot(lo, m)


def _mm3(v, m, left=False):
    a1 = v.astype(BF16)
    r1 = v - a1.astype(F32)
    a2 = r1.astype(BF16)
    a3 = (r1 - a2.astype(F32)).astype(BF16)
    if left:
        return _dot(m, a1) + _dot(m, a2) + _dot(m, a3)
    return _dot(a1, m) + _dot(a2, m) + _dot(a3, m)


def _sigmoid(z):
    return 1.0 / (1.0 + jnp.exp(-z))


def _rms_rows(x):
    return lax.rsqrt(jnp.mean(x * x, axis=-1, keepdims=True) + EPS)


def _inproj_fwd(x, g, wm, wff):
    S, D = x.shape
    tm = min(_TM, S)
    tn = 512

    def body(x_ref, g_ref, w_ref, wff_ref, o_ref, off_ref, h_ref):
        @pl.when(pl.program_id(1) == 0)
        def _():
            xv = x_ref[...]
            h = (xv * _rms_rows(xv)) * g_ref[...]
            h_ref[...] = h.astype(BF16)
            off_ref[...] = _dot(h_ref[...], wff_ref[...])

        o_ref[...] = _dot(h_ref[...], w_ref[...])

    return pl.pallas_call(
        body, name="inproj_fwd",
        grid=(S // tm, N_MAIN // tn),
        in_specs=[pl.BlockSpec((tm, D), lambda i, j: (i, 0)),
                  pl.BlockSpec((1, D), lambda i, j: (0, 0)),
                  pl.BlockSpec((D, tn), lambda i, j: (0, j)),
                  pl.BlockSpec((D, N_FFPAD), lambda i, j: (0, 0))],
        out_specs=[pl.BlockSpec((tm, tn), lambda i, j: (i, j)),
                   pl.BlockSpec((tm, N_FFPAD), lambda i, j: (i, 0))],
        out_shape=[jax.ShapeDtypeStruct((S, N_MAIN), F32), jax.ShapeDtypeStruct((S, N_FFPAD), F32)],
        scratch_shapes=[pltpu.VMEM((tm, D), BF16)],
        compiler_params=_cp(("parallel", "arbitrary"), 40 << 20),
    )(x, g, wm, wff)


def _head_norm(x, g, bd):
    ss = _mm2(x * x, bd)
    r = lax.rsqrt(ss * (1.0 / HEAD_DIM) + EPS)
    return (x * r) * g


def _fox_prep(proj, pff, bfp, gq, gk, bd, ex, tril):
    S = proj.shape[0]
    T = tril.shape[0]

    def body(q_ref, k_ref, ff_ref, b_ref, gq_ref, gk_ref, bd_ref, ex_ref, tri_ref,
             qs_ref, kn_ref, cc_ref, cqb_ref, carry):
        @pl.when(pl.program_id(0) == 0)
        def _():
            carry[...] = jnp.zeros_like(carry)

        bdv = bd_ref[...]
        qs_ref[...] = (_head_norm(q_ref[...], gq_ref[...], bdv) * Q_SCALE).astype(BF16)
        kn_ref[...] = _head_norm(k_ref[...], gk_ref[...], bdv).astype(BF16)
        u = ff_ref[...] + b_ref[...]
        lf = jnp.minimum(u, 0.0) - jnp.log1p(jnp.exp(-jnp.abs(u)))
        c = _mm3(lf, tri_ref[...], left=True) + carry[0:1, :]
        carry[0:1, :] = c[T - 1:T, :]
        cc_ref[...] = c
        cqb_ref[...] = _mm3(c, ex_ref[...])

    return pl.pallas_call(
        body, name="fox_prep",
        grid=(S // T,),
        in_specs=[pl.BlockSpec((T, FOX_W), lambda i: (i, OFF_FQ // FOX_W)),
                  pl.BlockSpec((T, FOX_W), lambda i: (i, OFF_FK // FOX_W)),
                  pl.BlockSpec((T, N_FFPAD), lambda i: (i, 0)),
                  pl.BlockSpec((1, N_FFPAD), lambda i: (0, 0)),
                  pl.BlockSpec((1, FOX_W), lambda i: (0, 0)),
                  pl.BlockSpec((1, FOX_W), lambda i: (0, 0)),
                  pl.BlockSpec((FOX_W, FOX_W), lambda i: (0, 0)),
                  pl.BlockSpec((N_FFPAD, FOX_W), lambda i: (0, 0)),
                  pl.BlockSpec((T, T), lambda i: (0, 0))],
        out_specs=[pl.BlockSpec((T, FOX_W), lambda i: (i, 0)),
                   pl.BlockSpec((T, FOX_W), lambda i: (i, 0)),
                   pl.BlockSpec((T, N_FFPAD), lambda i: (i, 0)),
                   pl.BlockSpec((T, FOX_W), lambda i: (i, 0))],
        out_shape=[jax.ShapeDtypeStruct((S, FOX_W), BF16), jax.ShapeDtypeStruct((S, FOX_W), BF16),
                   jax.ShapeDtypeStruct((S, N_FFPAD), F32), jax.ShapeDtypeStruct((S, FOX_W), F32)],
        scratch_shapes=[pltpu.VMEM((8, N_FFPAD), F32)],
        compiler_params=_cp(("arbitrary",), 40 << 20),
    )(proj, proj, pff, bfp, gq, gk, bd, ex, tril)


def _pair_blk(S, off=0):
    return pl.BlockSpec((S, 128), lambda p: (0, off + p), pipeline_mode=pl.Buffered(1))


def _pair_rows(S):
    return pl.BlockSpec((None, 8, S), lambda p: (p, 0, 0), pipeline_mode=pl.Buffered(1))


def _head_masks(S):
    return lax.broadcasted_iota(jnp.int32, (S, 128), 1) < HEAD_DIM


def _fox_fwd(qs, kn, proj, cqb, crow4):
    S = qs.shape[0]
    T = min(_T, S)
    nq = S // T

    def body(q_ref, k_ref, v_ref, cq_ref, cr_ref, o_ref, lse_ref, qa, qb, vta, vtb, cka, ckb, ma, mb, acca, accb):
        lane_s = _head_masks(S)
        q = q_ref[...]
        zq = jnp.zeros_like(q)
        qa[...] = jnp.where(lane_s, q, zq)
        qb[...] = jnp.where(lane_s, zq, q)
        cq = cq_ref[...]
        cka[...] = jnp.broadcast_to(cq[:, 0:1], (S, 128))
        ckb[...] = jnp.broadcast_to(cq[:, 64:65], (S, 128))
        lse_ref[...] = jnp.zeros((8, S), F32)
        row_t = lax.broadcasted_iota(jnp.int32, (128, T), 0) < HEAD_DIM

        def prep(c, carry):
            c0 = pl.multiple_of(c * T, T)
            vt = v_ref[pl.ds(c0, T), :].T
            vta[:, pl.ds(c0, T)] = jnp.where(row_t, vt, 1.0).astype(BF16)
            vtb[:, pl.ds(c0, T)] = jnp.where(row_t, 1.0, vt).astype(BF16)
            return carry

        lax.fori_loop(0, nq, prep, 0)
        causal = (lax.broadcasted_iota(jnp.int32, (T, T), 0) <= lax.broadcasted_iota(jnp.int32, (T, T), 1))

        def head(qh, k, vt, cq_row, ck_t, m_ref, acc_ref, masked):
            s = _dot_nt(k, qh) + cq_row - ck_t
            if masked:
                s = jnp.where(causal, s, NEG)
            m_old = m_ref[0:1, :]
            m_new = jnp.maximum(m_old, jnp.max(s, axis=0, keepdims=True))
            p = jnp.exp(s - m_new)
            acc_ref[...] = jnp.exp(m_old - m_new) * acc_ref[...] + _dot(vt, p.astype(BF16))
            m_ref[0:1, :] = m_new

        def kv(j, r0, masked):
            c0 = pl.multiple_of(j * T, T)
            k = k_ref[pl.ds(c0, T), :]
            cr = cr_ref[:, pl.ds(r0, T)]
            head(qa[pl.ds(r0, T), :], k, vta[:, pl.ds(c0, T)], cr[0:1, :],
                 jnp.tile(cka[pl.ds(c0, T), :], (1, T // 128)), ma, acca, masked)
            head(qb[pl.ds(r0, T), :], k, vtb[:, pl.ds(c0, T)], cr[1:2, :],
                 jnp.tile(ckb[pl.ds(c0, T), :], (1, T // 128)), mb, accb, masked)

        def qblk(i, carry):
            r0 = pl.multiple_of(i * T, T)
            ma[...] = jnp.full((8, T), NEG, F32)
            mb[...] = jnp.full((8, T), NEG, F32)
            acca[...] = jnp.zeros((128, T), F32)
            accb[...] = jnp.zeros((128, T), F32)

            def inner(j, c):
                kv(j, r0, False)
                return c

            lax.fori_loop(0, i, inner, 0)
            kv(i, r0, True)
            aa = acca[...]
            ab = accb[...]
            la = aa[64:65, :]
            lb = ab[0:1, :]
            o_ref[pl.ds(r0, T), :] = jnp.where(row_t, aa / la, ab / lb).T
            lse_ref[0:1, pl.ds(r0, T)] = ma[0:1, :] + jnp.log(la)
            lse_ref[1:2, pl.ds(r0, T)] = mb[0:1, :] + jnp.log(lb)
            return carry

        lax.fori_loop(0, nq, qblk, 0)

    return pl.pallas_call(
        body, name="fox_fwd",
        grid=(FOX_W // 128,),
        in_specs=[_pair_blk(S), _pair_blk(S), _pair_blk(S, OFF_FV // 128), _pair_blk(S), _pair_rows(S)],
        out_specs=[_pair_blk(S), _pair_rows(S)],
        out_shape=[jax.ShapeDtypeStruct((S, FOX_W), F32), jax.ShapeDtypeStruct((FOX_W // 128, 8, S), F32)],
        scratch_shapes=[pltpu.VMEM((S, 128), BF16)] * 2 + [pltpu.VMEM((128, S), BF16)] * 2
        + [pltpu.VMEM((S, 128), F32)] * 2 + [pltpu.VMEM((8, T), F32)] * 2 + [pltpu.VMEM((128, T), F32)] * 2,
        compiler_params=_cp(("arbitrary",), _VMEM_BIG),
    )(qs, kn, proj, cqb, crow4)


def _softplus_parts(z):
    e = jnp.exp(-jnp.abs(z))
    return e, jnp.maximum(z, 0.0) + jnp.log1p(e)


def _sb_fwd(proj, triu):
    S = proj.shape[0]
    T = triu.shape[0]
    nq = S // T

    def body(q_ref, k_ref, v_ref, tri_ref, o_ref, lt_ref, qa, qb, kb, vt, ra, rb, acca, accb):
        lane_s = _head_masks(S)
        q = (q_ref[...] * Q_SCALE).astype(BF16)
        zq = jnp.zeros_like(q)
        qa[...] = jnp.where(lane_s, q, zq)
        qb[...] = jnp.where(lane_s, zq, q)
        kb[...] = k_ref[...].astype(BF16)
        lt_ref[...] = jnp.zeros((8, S), F32)
        row_t = lax.broadcasted_iota(jnp.int32, (128, T), 0) < HEAD_DIM

        def prep(c, carry):
            c0 = pl.multiple_of(c * T, T)
            vt[:, pl.ds(c0, T)] = v_ref[pl.ds(c0, T), :].T.astype(BF16)
            return carry

        lax.fori_loop(0, nq, prep, 0)
        strict = (lax.broadcasted_iota(jnp.int32, (T, T), 0) < lax.broadcasted_iota(jnp.int32, (T, T), 1))

        def head(qh, k, vtt, r_ref, acc_ref, masked):
            z = _dot_nt(k, qh)
            _, sp = _softplus_parts(z)
            lb = -sp
            if masked:
                lb = jnp.where(strict, lb, 0.0)
            inc = _mm2(lb, tri_ref[...], left=True)
            r = r_ref[0:1, :]
            a = jnp.exp(z + inc + r)
            if masked:
                a = jnp.where(strict, a, 0.0)
            r_ref[0:1, :] = r + inc[0:1, :]
            acc_ref[...] = acc_ref[...] + _dot(vtt, a.astype(BF16))

        def kv(j, r0, masked):
            c0 = pl.multiple_of(j * T, T)
            k = kb[pl.ds(c0, T), :]
            vtt = vt[:, pl.ds(c0, T)]
            head(qa[pl.ds(r0, T), :], k, vtt, ra, acca, masked)
            head(qb[pl.ds(r0, T), :], k, vtt, rb, accb, masked)

        def qblk(i, carry):
            r0 = pl.multiple_of(i * T, T)
            ra[...] = jnp.zeros((8, T), F32)
            rb[...] = jnp.zeros((8, T), F32)
            acca[...] = jnp.zeros((128, T), F32)
            accb[...] = jnp.zeros((128, T), F32)
            kv(i, r0, True)

            def inner(jj, c):
                kv(i - 1 - jj, r0, False)
                return c

            lax.fori_loop(0, i, inner, 0)
            o_ref[pl.ds(r0, T), :] = jnp.where(row_t, acca[...], accb[...]).T
            lt_ref[0:1, pl.ds(r0, T)] = ra[0:1, :]
            lt_ref[1:2, pl.ds(r0, T)] = rb[0:1, :]
            return carry

        lax.fori_loop(0, nq, qblk, 0)

    return pl.pallas_call(
        body, name="sb_fwd",
        grid=(SB_W // 128,),
        in_specs=[_pair_blk(S, OFF_SQ // 128), _pair_blk(S, OFF_SK // 128), _pair_blk(S, OFF_SV // 128),
                  pl.BlockSpec((T, T), lambda p: (0, 0))],
        out_specs=[_pair_blk(S), _pair_rows(S)],
        out_shape=[jax.ShapeDtypeStruct((S, SB_W), F32), jax.ShapeDtypeStruct((SB_W // 128, 8, S), F32)],
        scratch_shapes=[pltpu.VMEM((S, 128), BF16)] * 3 + [pltpu.VMEM((128, S), BF16)]
        + [pltpu.VMEM((8, T), F32)] * 2 + [pltpu.VMEM((128, T), F32)] * 2,
        compiler_params=_cp(("arbitrary",), _VMEM_BIG),
    )(proj, proj, proj, triu)


def _pool_window_lanes(shape):
    lane = lax.broadcasted_iota(jnp.int32, shape, 1)
    return jnp.where(lane < 64, 2, jnp.where(lane < 128, 4, jnp.where(lane < 192, 8, 16)))


def _pool_fwd(proj):
    S = proj.shape[0]

    def body(x_ref, o_ref):
        x = x_ref[...]
        t = lax.broadcasted_iota(jnp.int32, x.shape, 0)
        lane = lax.broadcasted_iota(jnp.int32, x.shape, 1)

        def back(a, k):
            return jnp.where(t >= k, pltpu.roll(a, k, 0), 0.0)

        s1 = x + back(x, 1)
        s2 = s1 + back(s1, 2)
        s4 = s2 + back(s2, 4)
        s8 = s4 + back(s4, 8)
        win = jnp.where(lane < 64, s1, jnp.where(lane < 128, s2, jnp.where(lane < 192, s4, s8)))
        cnt = jnp.minimum(t + 1, _pool_window_lanes(x.shape)).astype(F32)
        o_ref[...] = win / cnt - x

    return pl.pallas_call(
        body, name="pool_fwd",
        grid=(1,),
        in_specs=[pl.BlockSpec((S, POOL_W), lambda i: (0, OFF_PX // POOL_W))],
        out_specs=pl.BlockSpec((S, POOL_W), lambda i: (0, 0)),
        out_shape=jax.ShapeDtypeStruct((S, POOL_W), F32),
        compiler_params=_cp(("arbitrary",), _VMEM_BIG),
    )(proj)


def _silu(g):
    return g * _sigmoid(g)


def _mix_out(fo, so, pooled, proj, wbd, scale, wout, x):
    S, D = x.shape
    tm = min(256, S)

    def body(fo_ref, fg_ref, so_ref, sg_ref, pl_ref, pg_ref, wbd_ref, sc_ref, w_ref, x_ref, y_ref, mx_ref):
        mx_ref[:, 0:FOX_W] = (fo_ref[...] * _silu(fg_ref[...])).astype(BF16)
        yp = _dot(pl_ref[...].astype(BF16), wbd_ref[...]) * sc_ref[...]
        mx_ref[:, FOX_W:FOX_W + POOL_W] = (yp * _silu(pg_ref[...])).astype(BF16)
        mx_ref[:, FOX_W + POOL_W:D_MIX] = (so_ref[...] * _silu(sg_ref[...])).astype(BF16)
        y_ref[...] = x_ref[...] + _dot(mx_ref[...], w_ref[...])

    return pl.pallas_call(
        body, name="mix_out",
        grid=(S // tm,),
        in_specs=[pl.BlockSpec((tm, FOX_W), lambda i: (i, 0)),
                  pl.BlockSpec((tm, FOX_W), lambda i: (i, OFF_FG // FOX_W)),
                  pl.BlockSpec((tm, SB_W), lambda i: (i, 0)),
                  pl.BlockSpec((tm, SB_W), lambda i: (i, OFF_SG // SB_W)),
                  pl.BlockSpec((tm, POOL_W), lambda i: (i, 0)),
                  pl.BlockSpec((tm, POOL_W), lambda i: (i, OFF_PG // POOL_W)),
                  pl.BlockSpec((POOL_W, POOL_W), lambda i: (0, 0)),
                  pl.BlockSpec((1, POOL_W), lambda i: (0, 0)),
                  pl.BlockSpec((D_MIX, D), lambda i: (0, 0)),
                  pl.BlockSpec((tm, D), lambda i: (i, 0))],
        out_specs=[pl.BlockSpec((tm, D), lambda i: (i, 0)), pl.BlockSpec((tm, D_MIX), lambda i: (i, 0))],
        out_shape=[jax.ShapeDtypeStruct((S, D), F32), jax.ShapeDtypeStruct((S, D_MIX), BF16)],
        compiler_params=_cp(("parallel",), 40 << 20),
    )(fo, proj, so, proj, pooled, proj, wbd, scale, wout, x)


def _loss_head(y, target):
    S, D = y.shape
    tm = min(_TM, S)

    def body(y_ref, t_ref, dy_ref, ls_ref):
        @pl.when(pl.program_id(0) == 0)
        def _():
            ls_ref[...] = jnp.zeros_like(ls_ref)

        e = y_ref[...] - t_ref[...]
        dy_ref[...] = e * (1.0 / D)
        ls_ref[...] = ls_ref[...] + jnp.sum(e * e) * (0.5 / D)

    dy, ls = pl.pallas_call(
        body, name="loss_head",
        grid=(S // tm,),
        in_specs=[pl.BlockSpec((tm, D), lambda i: (i, 0)), pl.BlockSpec((tm, D), lambda i: (i, 0))],
        out_specs=[pl.BlockSpec((tm, D), lambda i: (i, 0)), pl.BlockSpec((8, 128), lambda i: (0, 0))],
        out_shape=[jax.ShapeDtypeStruct((S, D), F32), jax.ShapeDtypeStruct((8, 128), F32)],
        compiler_params=_cp(("arbitrary",), 40 << 20),
    )(y, target)
    return dy, ls[0, 0]


def _dsilu(g):
    s = _sigmoid(g)
    return s * (1.0 + g * (1.0 - s))


def _gate_bwd(dy, wout, fo, so, pooled, proj, wbd, scale):
    S, D = dy.shape
    tm = min(256, S)

    def body(dy_ref, w_ref, fo_ref, fg_ref, so_ref, sg_ref, pl_ref, pg_ref, wbd_ref, sc_ref,
             dfo_ref, dfg_ref, dso_ref, dsg_ref, dpg_ref, dpl_ref, dsc_ref, dwbd_ref):
        @pl.when(pl.program_id(0) == 0)
        def _():
            dsc_ref[...] = jnp.zeros_like(dsc_ref)
            dwbd_ref[...] = jnp.zeros_like(dwbd_ref)

        dm = _dot_nt(dy_ref[...].astype(BF16), w_ref[...])
        dmf = dm[:, 0:FOX_W]
        dmp = dm[:, FOX_W:FOX_W + POOL_W]
        dms = dm[:, FOX_W + POOL_W:D_MIX]
        fg = fg_ref[...]
        dfo_ref[...] = dmf * _silu(fg)
        dfg_ref[...] = (dmf * fo_ref[...] * _dsilu(fg)).astype(BF16)
        sg = sg_ref[...]
        dso_ref[...] = dms * _silu(sg)
        dsg_ref[...] = (dms * so_ref[...] * _dsilu(sg)).astype(BF16)
        pg = pg_ref[...]
        plb = pl_ref[...].astype(BF16)
        yw = _dot(plb, wbd_ref[...])
        sc = sc_ref[...]
        dpg_ref[...] = (dmp * (yw * sc) * _dsilu(pg)).astype(BF16)
        dys = dmp * _silu(pg)
        dsc_ref[...] = dsc_ref[...] + jnp.sum(dys * yw, axis=0, keepdims=True)
        dyw = (dys * sc).astype(BF16)
        dpl_ref[...] = _dot_nt(dyw, wbd_ref[...])
        dwbd_ref[...] = dwbd_ref[...] + _dot_tn(plb, dyw)

    return pl.pallas_call(
        body, name="gate_bwd",
        grid=(S // tm,),
        in_specs=[pl.BlockSpec((tm, D), lambda i: (i, 0)),
                  pl.BlockSpec((D_MIX, D), lambda i: (0, 0)),
                  pl.BlockSpec((tm, FOX_W), lambda i: (i, 0)),
                  pl.BlockSpec((tm, FOX_W), lambda i: (i, OFF_FG // FOX_W)),
                  pl.BlockSpec((tm, SB_W), lambda i: (i, 0)),
                  pl.BlockSpec((tm, SB_W), lambda i: (i, OFF_SG // SB_W)),
                  pl.BlockSpec((tm, POOL_W), lambda i: (i, 0)),
                  pl.BlockSpec((tm, POOL_W), lambda i: (i, OFF_PG // POOL_W)),
                  pl.BlockSpec((POOL_W, POOL_W), lambda i: (0, 0)),
                  pl.BlockSpec((1, POOL_W), lambda i: (0, 0))],
        out_specs=[pl.BlockSpec((tm, FOX_W), lambda i: (i, 0)),
                   pl.BlockSpec((tm, FOX_W), lambda i: (i, 0)),
                   pl.BlockSpec((tm, SB_W), lambda i: (i, 0)),
                   pl.BlockSpec((tm, SB_W), lambda i: (i, 0)),
                   pl.BlockSpec((tm, POOL_W), lambda i: (i, 0)),
                   pl.BlockSpec((tm, POOL_W), lambda i: (i, 0)),
                   pl.BlockSpec((1, POOL_W), lambda i: (0, 0)),
                   pl.BlockSpec((POOL_W, POOL_W), lambda i: (0, 0))],
        out_shape=[jax.ShapeDtypeStruct((S, FOX_W), F32), jax.ShapeDtypeStruct((S, FOX_W), BF16),
                   jax.ShapeDtypeStruct((S, SB_W), F32), jax.ShapeDtypeStruct((S, SB_W), BF16),
                   jax.ShapeDtypeStruct((S, POOL_W), BF16), jax.ShapeDtypeStruct((S, POOL_W), F32),
                   jax.ShapeDtypeStruct((1, POOL_W), F32), jax.ShapeDtypeStruct((POOL_W, POOL_W), F32)],
        compiler_params=_cp(("arbitrary",), 40 << 20),
    )(dy, wout, fo, proj, so, proj, pooled, proj, wbd, scale)


def _matmul_tn(a, b, name):
    S, M = a.shape
    N = b.shape[1]
    tk = min(_TM, S)
    tn = min(512, N)

    def body(a_ref, b_ref, o_ref):
        @pl.when(pl.program_id(1) == 0)
        def _():
            o_ref[...] = jnp.zeros_like(o_ref)

        o_ref[...] = o_ref[...] + _dot_tn(a_ref[...].astype(BF16), b_ref[...].astype(BF16))

    return pl.pallas_call(
        body, name=name,
        grid=(N // tn, S // tk),
        in_specs=[pl.BlockSpec((tk, M), lambda j, k: (k, 0)), pl.BlockSpec((tk, tn), lambda j, k: (k, j))],
        out_specs=pl.BlockSpec((M, tn), lambda j, k: (0, j)),
        out_shape=jax.ShapeDtypeStruct((M, N), F32),
        compiler_params=_cp(("parallel", "arbitrary"), 40 << 20),
    )(a, b)


def _pool_bwd(dpooled):
    S = dpooled.shape[0]

    def body(d_ref, o_ref):
        d = d_ref[...]
        t = lax.broadcasted_iota(jnp.int32, d.shape, 0)
        lane = lax.broadcasted_iota(jnp.int32, d.shape, 1)
        cnt = jnp.minimum(t + 1, _pool_window_lanes(d.shape)).astype(F32)
        u = d / cnt

        def fwd(a, k):
            return jnp.where(t < S - k, pltpu.roll(a, S - k, 0), 0.0)

        s1 = u + fwd(u, 1)
        s2 = s1 + fwd(s1, 2)
        s4 = s2 + fwd(s2, 4)
        s8 = s4 + fwd(s4, 8)
        win = jnp.where(lane < 64, s1, jnp.where(lane < 128, s2, jnp.where(lane < 192, s4, s8)))
        o_ref[...] = (win - d).astype(BF16)

    return pl.pallas_call(
        body, name="pool_bwd",
        grid=(1,),
        in_specs=[pl.BlockSpec((S, POOL_W), lambda i: (0, 0))],
        out_specs=pl.BlockSpec((S, POOL_W), lambda i: (0, 0)),
        out_shape=jax.ShapeDtypeStruct((S, POOL_W), BF16),
        compiler_params=_cp(("arbitrary",), _VMEM_BIG),
    )(dpooled)


def _fox_bwd(qs, kn, proj, dfo, fo, lse, cqb, crow4):
    S = qs.shape[0]
    T = min(_T, S)
    nq = S // T

    def body(q_ref, k_ref, v_ref, do_ref, o_ref, lse_ref, cq_ref, cr_ref,
             dq_ref, dk_ref, dv_ref, dck_ref, dcq_ref,
             qa, qb, kta, ktb, vb, doa, dob, cka, ckb, dcka, dckb, dva, dqt, dcqa, dcqb):
        lane_s = _head_masks(S)
        q = q_ref[...]
        zq = jnp.zeros_like(q)
        qa[...] = jnp.where(lane_s, q, zq)
        qb[...] = jnp.where(lane_s, zq, q)
        vb[...] = v_ref[...].astype(BF16)
        do = do_ref[...].astype(BF16)
        doa[...] = jnp.where(lane_s, do, zq)
        dob[...] = jnp.where(lane_s, zq, do)
        cq = cq_ref[...]
        cka[...] = jnp.broadcast_to(cq[:, 0:1], (S, 128))
        ckb[...] = jnp.broadcast_to(cq[:, 64:65], (S, 128))
        zs = jnp.zeros((S, 128), F32)
        dk_ref[...] = zs
        dva[...] = zs
        dcka[...] = zs
        dckb[...] = zs
        dcq_ref[...] = jnp.zeros((8, S), F32)
        row_t = lax.broadcasted_iota(jnp.int32, (128, T), 0) < HEAD_DIM

        def prep(c, carry):
            c0 = pl.multiple_of(c * T, T)
            kt = k_ref[pl.ds(c0, T), :].astype(F32).T
            kta[:, pl.ds(c0, T)] = jnp.where(row_t, kt, 0.0).astype(BF16)
            ktb[:, pl.ds(c0, T)] = jnp.where(row_t, 0.0, kt).astype(BF16)
            return carry

        lax.fori_loop(0, nq, prep, 0)
        causal = (lax.broadcasted_iota(jnp.int32, (T, T), 0) <= lax.broadcasted_iota(jnp.int32, (T, T), 1))

        def head(qh, kfull, kth, v, doh, cq_row, ck_t, lse_row, dl_row, dck_acc, dcq_acc, c0, masked):
            s = _dot_nt(kfull, qh) + cq_row - ck_t
            if masked:
                s = jnp.where(causal, s, NEG)
            p = jnp.exp(s - lse_row)
            dp = _dot_nt(v, doh)
            ds = p * (dp - dl_row)
            pb = p.astype(BF16)
            dsb = ds.astype(BF16)
            dva[pl.ds(c0, T), :] = dva[pl.ds(c0, T), :] + _dot(pb, doh)
            dk_ref[pl.ds(c0, T), :] = dk_ref[pl.ds(c0, T), :] + _dot(dsb, qh)
            dqt[...] = dqt[...] + _dot(kth, dsb)
            dcq_acc[0:1, :] = dcq_acc[0:1, :] + jnp.sum(ds, axis=0, keepdims=True)
            fold = ds[:, 0:128]
            for t in range(1, T // 128):
                fold = fold + ds[:, 128 * t:128 * (t + 1)]
            dck_acc[pl.ds(c0, T), :] = dck_acc[pl.ds(c0, T), :] - fold

        def kv(j, r0, lsa, lsb, dla, dlb, masked):
            c0 = pl.multiple_of(j * T, T)
            kfull = k_ref[pl.ds(c0, T), :]
            v = vb[pl.ds(c0, T), :]
            cr = cr_ref[:, pl.ds(r0, T)]
            head(qa[pl.ds(r0, T), :], kfull, kta[:, pl.ds(c0, T)], v, doa[pl.ds(r0, T), :], cr[0:1, :],
                 jnp.tile(cka[pl.ds(c0, T), :], (1, T // 128)), lsa, dla, dcka, dcqa, c0, masked)
            head(qb[pl.ds(r0, T), :], kfull, ktb[:, pl.ds(c0, T)], v, dob[pl.ds(r0, T), :], cr[1:2, :],
                 jnp.tile(ckb[pl.ds(c0, T), :], (1, T // 128)), lsb, dlb, dckb, dcqb, c0, masked)

        def qblk(i, carry):
            r0 = pl.multiple_of(i * T, T)
            dt = (do_ref[pl.ds(r0, T), :] * o_ref[pl.ds(r0, T), :]).T
            dla = jnp.sum(jnp.where(row_t, dt, 0.0), axis=0, keepdims=True)
            dlb = jnp.sum(jnp.where(row_t, 0.0, dt), axis=0, keepdims=True)
            ls = lse_ref[:, pl.ds(r0, T)]
            lsa = ls[0:1, :]
            lsb = ls[1:2, :]
            dqt[...] = jnp.zeros((128, T), F32)
            dcqa[...] = jnp.zeros((8, T), F32)
            dcqb[...] = jnp.zeros((8, T), F32)

            def inner(j, c):
                kv(j, r0, lsa, lsb, dla, dlb, False)
                return c

            lax.fori_loop(0, i, inner, 0)
            kv(i, r0, lsa, lsb, dla, dlb, True)
            dq_ref[pl.ds(r0, T), :] = dqt[...].T
            dcq_ref[0:1, pl.ds(r0, T)] = dcqa[0:1, :]
            dcq_ref[1:2, pl.ds(r0, T)] = dcqb[0:1, :]
            return carry

        lax.fori_loop(0, nq, qblk, 0)
        dv_ref[...] = dva[...].astype(BF16)
        dck_ref[...] = jnp.where(lane_s, jnp.sum(dcka[...], axis=1, keepdims=True),
                                 jnp.sum(dckb[...], axis=1, keepdims=True))

    return pl.pallas_call(
        body, name="fox_bwd",
        grid=(FOX_W // 128,),
        in_specs=[_pair_blk(S), _pair_blk(S), _pair_blk(S, OFF_FV // 128), _pair_blk(S), _pair_blk(S),
                  _pair_rows(S), _pair_blk(S), _pair_rows(S)],
        out_specs=[_pair_blk(S), _pair_blk(S), _pair_blk(S), _pair_blk(S), _pair_rows(S)],
        out_shape=[jax.ShapeDtypeStruct((S, FOX_W), F32), jax.ShapeDtypeStruct((S, FOX_W), F32),
                   jax.ShapeDtypeStruct((S, FOX_W), BF16), jax.ShapeDtypeStruct((S, FOX_W), F32),
                   jax.ShapeDtypeStruct((FOX_W // 128, 8, S), F32)],
        scratch_shapes=[pltpu.VMEM((S, 128), BF16)] * 2 + [pltpu.VMEM((128, S), BF16)] * 2
        + [pltpu.VMEM((S, 128), BF16)] * 3 + [pltpu.VMEM((S, 128), F32)] * 5
        + [pltpu.VMEM((128, T), F32)] + [pltpu.VMEM((8, T), F32)] * 2,
        compiler_params=_cp(("arbitrary",), _VMEM_BIG),
    )(qs, kn, proj, dfo, fo, lse, cqb, crow4)


def _sb_bwd(proj, dso, ltot, tril):
    S = proj.shape[0]
    T = tril.shape[0]
    nq = S // T

    def body(q_ref, k_ref, v_ref, do_ref, lt_ref, tri_ref, dq_ref, dk_ref, dv_ref,
             qa, qb, k2, kta, ktb, vb, doa, dob, dka, dva, dqt, ra, rb, ga, gb):
        lane_s = _head_masks(S)
        q = (q_ref[...] * Q_SCALE).astype(BF16)
        zq = jnp.zeros_like(q)
        qa[...] = jnp.where(lane_s, q, zq)
        qb[...] = jnp.where(lane_s, zq, q)
        k2[...] = k_ref[...].astype(BF16)
        vb[...] = v_ref[...].astype(BF16)
        do = do_ref[...].astype(BF16)
        doa[...] = jnp.where(lane_s, do, zq)
        dob[...] = jnp.where(lane_s, zq, do)
        dka[...] = jnp.zeros((S, 128), F32)
        dva[...] = jnp.zeros((S, 128), F32)
        row_t = lax.broadcasted_iota(jnp.int32, (128, T), 0) < HEAD_DIM

        def prep(c, carry):
            c0 = pl.multiple_of(c * T, T)
            kt = k_ref[pl.ds(c0, T), :].T
            kta[:, pl.ds(c0, T)] = jnp.where(row_t, kt, 0.0).astype(BF16)
            ktb[:, pl.ds(c0, T)] = jnp.where(row_t, 0.0, kt).astype(BF16)
            return carry

        lax.fori_loop(0, nq, prep, 0)
        strict = (lax.broadcasted_iota(jnp.int32, (T, T), 0) < lax.broadcasted_iota(jnp.int32, (T, T), 1))

        def head(qh, kfull, kth, v, doh, r_ref, g_ref, lt_row, c0, masked):
            z = _dot_nt(kfull, qh)
            e, sp = _softplus_parts(z)
            lb = -sp
            if masked:
                lb = jnp.where(strict, lb, 0.0)
            tri = tri_ref[...]
            pre = _mm2(lb, tri, left=True)
            r = r_ref[0:1, :]
            a = jnp.exp(z + lb + ((lt_row - r) - pre))
            if masked:
                a = jnp.where(strict, a, 0.0)
            da = _dot_nt(v, doh)
            g = a * da
            gpre = _mm2(g, tri, left=True)
            gc = g_ref[0:1, :]
            big_g = gc + (gpre - g)
            inv = 1.0 / (1.0 + e)
            pos = z >= 0.0
            sig = jnp.where(pos, 1.0, e) * inv
            oms = jnp.where(pos, e, 1.0) * inv
            dz = g * oms - sig * big_g
            if masked:
                dz = jnp.where(strict, dz, 0.0)
            dzb = dz.astype(BF16)
            dqt[...] = dqt[...] + _dot(kth, dzb)
            dka[pl.ds(c0, T), :] = dka[pl.ds(c0, T), :] + _dot(dzb, qh)
            dva[pl.ds(c0, T), :] = dva[pl.ds(c0, T), :] + _dot(a.astype(BF16), doh)
            r_ref[0:1, :] = r + pre[T - 1:T, :]
            g_ref[0:1, :] = gc + gpre[T - 1:T, :]

        def kv(j, r0, lta, ltb, masked):
            c0 = pl.multiple_of(j * T, T)
            kfull = k2[pl.ds(c0, T), :]
            v = vb[pl.ds(c0, T), :]
            head(qa[pl.ds(r0, T), :], kfull, kta[:, pl.ds(c0, T)], v, doa[pl.ds(r0, T), :], ra, ga, lta, c0, masked)
            head(qb[pl.ds(r0, T), :], kfull, ktb[:, pl.ds(c0, T)], v, dob[pl.ds(r0, T), :], rb, gb, ltb, c0, masked)

        def qblk(i, carry):
            r0 = pl.multiple_of(i * T, T)
            lt = lt_ref[:, pl.ds(r0, T)]
            lta = lt[0:1, :]
            ltb = lt[1:2, :]
            zt = jnp.zeros((8, T), F32)
            dqt[...] = jnp.zeros((128, T), F32)
            ra[...] = zt
            rb[...] = zt
            ga[...] = zt
            gb[...] = zt

            def inner(j, c):
                kv(j, r0, lta, ltb, False)
                return c

            lax.fori_loop(0, i, inner, 0)
            kv(i, r0, lta, ltb, True)
            dq_ref[pl.ds(r0, T), :] = (dqt[...] * Q_SCALE).T.astype(BF16)
            return carry

        lax.fori_loop(0, nq, qblk, 0)
        dk_ref[...] = dka[...].astype(BF16)
        dv_ref[...] = dva[...].astype(BF16)

    return pl.pallas_call(
        body, name="sb_bwd",
        grid=(SB_W // 128,),
        in_specs=[_pair_blk(S, OFF_SQ // 128), _pair_blk(S, OFF_SK // 128), _pair_blk(S, OFF_SV // 128),
                  _pair_blk(S), _pair_rows(S), pl.BlockSpec((T, T), lambda p: (0, 0))],
        out_specs=[_pair_blk(S), _pair_blk(S), _pair_blk(S)],
        out_shape=[jax.ShapeDtypeStruct((S, SB_W), BF16)] * 3,
        scratch_shapes=([pltpu.VMEM((S, 128), BF16)] * 3 + [pltpu.VMEM((128, S), BF16)] * 2
                        + [pltpu.VMEM((S, 128), BF16)] * 3 + [pltpu.VMEM((S, 128), F32)] * 2
                        + [pltpu.VMEM((128, T), F32)] + [pltpu.VMEM((8, T), F32)] * 4),
        compiler_params=_cp(("arbitrary",), _VMEM_BIG),
    )(proj, proj, proj, dso, ltot, tril)


def _head_norm_bwd(x, g, dy, bd):
    ss = _mm2(x * x, bd)
    r = lax.rsqrt(ss * (1.0 / HEAD_DIM) + EPS)
    xr = x * r
    gdy = g * dy
    m = _mm2(xr * gdy, bd) * (1.0 / HEAD_DIM)
    return r * (gdy - xr * m), dy * xr


def _qk_bwd(dqs, dkn, proj, pff, bfp, gq, gk, bd, dccol, triu):
    S = proj.shape[0]
    T = triu.shape[0]
    n = S // T
    rev = lambda col: (lambda i: (n - 1 - i, col))

    def body(dq_ref, dk_ref, q_ref, k_ref, ff_ref, b_ref, gq_ref, gk_ref, bd_ref, dc_ref, tri_ref,
             dfq_ref, dfk_ref, dff_ref, dgq_ref, dgk_ref, dbf_ref, carry):
        @pl.when(pl.program_id(0) == 0)
        def _():
            carry[...] = jnp.zeros_like(carry)
            dgq_ref[...] = jnp.zeros_like(dgq_ref)
            dgk_ref[...] = jnp.zeros_like(dgk_ref)
            dbf_ref[...] = jnp.zeros_like(dbf_ref)

        bdv = bd_ref[...]
        dxq, gq_rows = _head_norm_bwd(q_ref[...], gq_ref[...], dq_ref[...] * Q_SCALE, bdv)
        dfq_ref[...] = dxq.astype(BF16)
        dgq_ref[...] = dgq_ref[...] + jnp.sum(gq_rows, axis=0, keepdims=True)
        dxk, gk_rows = _head_norm_bwd(k_ref[...], gk_ref[...], dk_ref[...], bdv)
        dfk_ref[...] = dxk.astype(BF16)
        dgk_ref[...] = dgk_ref[...] + jnp.sum(gk_rows, axis=0, keepdims=True)
        dlf = _mm3(dc_ref[...], tri_ref[...], left=True) + carry[0:1, :]
        carry[0:1, :] = dlf[0:1, :]
        u = ff_ref[...] + b_ref[...]
        lane = lax.broadcasted_iota(jnp.int32, u.shape, 1)
        dff = jnp.where(lane < N_FF, dlf * _sigmoid(-u), 0.0)
        dff_ref[...] = dff.astype(BF16)
        dbf_ref[...] = dbf_ref[...] + jnp.sum(dff, axis=0, keepdims=True)

    return pl.pallas_call(
        body, name="qk_bwd",
        grid=(n,),
        in_specs=[pl.BlockSpec((T, FOX_W), rev(0)), pl.BlockSpec((T, FOX_W), rev(0)),
                  pl.BlockSpec((T, FOX_W), rev(OFF_FQ // FOX_W)), pl.BlockSpec((T, FOX_W), rev(OFF_FK // FOX_W)),
                  pl.BlockSpec((T, N_FFPAD), rev(0)),
                  pl.BlockSpec((1, N_FFPAD), lambda i: (0, 0)),
                  pl.BlockSpec((1, FOX_W), lambda i: (0, 0)), pl.BlockSpec((1, FOX_W), lambda i: (0, 0)),
                  pl.BlockSpec((FOX_W, FOX_W), lambda i: (0, 0)),
                  pl.BlockSpec((T, N_FFPAD), rev(0)),
                  pl.BlockSpec((T, T), lambda i: (0, 0))],
        out_specs=[pl.BlockSpec((T, FOX_W), rev(0)), pl.BlockSpec((T, FOX_W), rev(0)),
                   pl.BlockSpec((T, N_FFPAD), rev(0)),
                   pl.BlockSpec((1, FOX_W), lambda i: (0, 0)), pl.BlockSpec((1, FOX_W), lambda i: (0, 0)),
                   pl.BlockSpec((1, N_FFPAD), lambda i: (0, 0))],
        out_shape=[jax.ShapeDtypeStruct((S, FOX_W), BF16), jax.ShapeDtypeStruct((S, FOX_W), BF16),
                   jax.ShapeDtypeStruct((S, N_FFPAD), BF16),
                   jax.ShapeDtypeStruct((1, FOX_W), F32), jax.ShapeDtypeStruct((1, FOX_W), F32),
                   jax.ShapeDtypeStruct((1, N_FFPAD), F32)],
        scratch_shapes=[pltpu.VMEM((8, N_FFPAD), F32)],
        compiler_params=_cp(("arbitrary",), 40 << 20),
    )(dqs, dkn, proj, proj, pff, bfp, gq, gk, bd, dccol, triu)


def _inproj_bwd_dx(dpm, dff, wm, wff, x, g, dy):
    S, D = x.shape
    tm = min(256, S)

    def body(dp_ref, dff_ref, w_ref, wff_ref, x_ref, g_ref, dy_ref, dx_ref, dg_ref):
        @pl.when(pl.program_id(0) == 0)
        def _():
            dg_ref[...] = jnp.zeros_like(dg_ref)

        dh = _dot_nt(dp_ref[...], w_ref[...]) + _dot_nt(dff_ref[...], wff_ref[...])
        xv = x_ref[...]
        r = _rms_rows(xv)
        xr = xv * r
        dg_ref[...] = dg_ref[...] + jnp.sum(dh * xr, axis=0, keepdims=True)
        gdh = g_ref[...] * dh
        m = jnp.mean(gdh * xr, axis=-1, keepdims=True)
        dx_ref[...] = dy_ref[...] + r * (gdh - xr * m)

    return pl.pallas_call(
        body, name="inproj_bwd_dx",
        grid=(S // tm,),
        in_specs=[pl.BlockSpec((tm, N_MAIN), lambda i: (i, 0)),
                  pl.BlockSpec((tm, N_FFPAD), lambda i: (i, 0)),
                  pl.BlockSpec((D, N_MAIN), lambda i: (0, 0)),
                  pl.BlockSpec((D, N_FFPAD), lambda i: (0, 0)),
                  pl.BlockSpec((tm, D), lambda i: (i, 0)),
                  pl.BlockSpec((1, D), lambda i: (0, 0)),
                  pl.BlockSpec((tm, D), lambda i: (i, 0))],
        out_specs=[pl.BlockSpec((tm, D), lambda i: (i, 0)), pl.BlockSpec((1, D), lambda i: (0, 0))],
        out_shape=[jax.ShapeDtypeStruct((S, D), F32), jax.ShapeDtypeStruct((1, D), F32)],
        compiler_params=_cp(("arbitrary",), 48 << 20),
    )(dpm, dff, wm, wff, x, g, dy)


def _inproj_bwd_dw(x, g, dpm, dff):
    S, D = x.shape
    tk = min(_TM, S)
    tn = 512

    def body(x_ref, g_ref, dp_ref, dff_ref, dw_ref, dwff_ref):
        j, k = pl.program_id(0), pl.program_id(1)

        @pl.when(k == 0)
        def _():
            dw_ref[...] = jnp.zeros_like(dw_ref)

        @pl.when((k == 0) & (j == 0))
        def _():
            dwff_ref[...] = jnp.zeros_like(dwff_ref)

        xv = x_ref[...]
        h = ((xv * _rms_rows(xv)) * g_ref[...]).astype(BF16)
        dw_ref[...] = dw_ref[...] + _dot_tn(h, dp_ref[...])

        @pl.when(j == 0)
        def _():
            dwff_ref[...] = dwff_ref[...] + _dot_tn(h, dff_ref[...])

    return pl.pallas_call(
        body, name="inproj_bwd_dw",
        grid=(N_MAIN // tn, S // tk),
        in_specs=[pl.BlockSpec((tk, D), lambda j, k: (k, 0)),
                  pl.BlockSpec((1, D), lambda j, k: (0, 0)),
                  pl.BlockSpec((tk, tn), lambda j, k: (k, j)),
                  pl.BlockSpec((tk, N_FFPAD), lambda j, k: (k, 0))],
        out_specs=[pl.BlockSpec((D, tn), lambda j, k: (0, j)), pl.BlockSpec((D, N_FFPAD), lambda j, k: (0, 0))],
        out_shape=[jax.ShapeDtypeStruct((D, N_MAIN), F32), jax.ShapeDtypeStruct((D, N_FFPAD), F32)],
        compiler_params=_cp(("arbitrary", "arbitrary"), 40 << 20),
    )(x, g, dpm, dff)


def _constants(T):
    tril = jnp.tril(jnp.ones((T, T), F32)).astype(BF16)
    hid = jnp.arange(FOX_W) // HEAD_DIM
    bd = (hid[:, None] == hid[None, :]).astype(BF16)
    ex = (jnp.arange(N_FFPAD)[:, None] == hid[None, :]).astype(BF16)
    return tril, tril.T, bd, ex


def _crow4(ccol):
    S = ccol.shape[0]
    c = ccol[:, :FOX_HEADS].T.reshape(FOX_HEADS // 2, 2, S)
    return jnp.pad(c, ((0, 0), (0, 6), (0, 0)))


def _layer_fwd(x, lw, consts):
    tril, triu, bd, ex = consts
    proj, pff = _inproj_fwd(x, lw["g"], lw["wm"], lw["wff"])
    qs, kn, ccol, cqb = _fox_prep(proj, pff, lw["bfp"], lw["gq"], lw["gk"], bd, ex, tril)
    crow4 = _crow4(ccol)
    fo, lse = _fox_fwd(qs, kn, proj, cqb, crow4)
    so, ltot = _sb_fwd(proj, triu)
    pooled = _pool_fwd(proj)
    y, mixed = _mix_out(fo, so, pooled, proj, lw["wbd"], lw["scale"], lw["wout"], x)
    return y, (x, proj, pff, qs, kn, cqb, crow4, fo, lse, so, ltot, pooled, mixed)


def _layer_bwd(dy, saved, lw, consts):
    tril, triu, bd, _ = consts
    x, proj, pff, qs, kn, cqb, crow4, fo, lse, so, ltot, pooled, mixed = saved
    S = x.shape[0]
    dfo, dfg, dso, dsg, dpg, dpooled, dscale, dwbd = _gate_bwd(dy, lw["wout"], fo, so, pooled, proj, lw["wbd"], lw["scale"])
    dwout = _matmul_tn(mixed, dy, "dw_out")
    dpx = _pool_bwd(dpooled)
    dqs, dkn, dfv, dck, dcq4 = _fox_bwd(qs, kn, proj, dfo, fo, lse, cqb, crow4)
    dsq, dsk, dsv = _sb_bwd(proj, dso, ltot, tril)
    dc8 = dck[:, ::HEAD_DIM] + dcq4[:, :2, :].reshape(FOX_HEADS, S).T
    dccol = jnp.pad(dc8, ((0, 0), (0, N_FFPAD - FOX_HEADS)))
    dfq, dfk, dff, dgq, dgk, dbf = _qk_bwd(dqs, dkn, proj, pff, lw["bfp"], lw["gq"], lw["gk"], bd, dccol, triu)
    dpm = jnp.concatenate([dfq, dfk, dfv, dfg, dpx, dpg, dsq, dsk, dsv, dsg], axis=1)
    dx, dng = _inproj_bwd_dx(dpm, dff, lw["wm"], lw["wff"], x, lw["g"], dy)
    dwm, dwff = _inproj_bwd_dw(x, lw["g"], dpm, dff)
    dwin = jnp.concatenate([dwm[:, :OFF_PX], dwff[:, :N_FF], dwm[:, OFF_PX:]], axis=1)
    grads = {
        "norm_g": dng[0],
        "w_in": dwin,
        "b_f": dbf[0, :N_FF],
        "q_norm_g": dgq[0].reshape(FOX_HEADS, HEAD_DIM).sum(0),
        "k_norm_g": dgk[0].reshape(FOX_HEADS, HEAD_DIM).sum(0),
        "w_pool": jnp.stack([dwbd[64 * i:64 * i + 64, 64 * i:64 * i + 64] for i in range(4)]),
        "pool_scale": dscale[0],
        "w_out": dwout,
    }
    return dx, grads


def _layer_weights(l, norm_g, win_full, b_f, q_norm_g, k_norm_g, w_pool, pool_scale, wout_full):
    w = win_full[l]
    D = w.shape[0]
    wm = jnp.concatenate([w[:, :2048], w[:, 2048 + N_FF:]], axis=1)
    wff = jnp.pad(w[:, 2048:2048 + N_FF], ((0, 0), (0, N_FFPAD - N_FF)))
    wbd = jnp.zeros((POOL_W, POOL_W), F32)
    for i in range(4):
        wbd = wbd.at[64 * i:64 * i + 64, 64 * i:64 * i + 64].set(w_pool[l, i])
    return {
        "g": norm_g[l].reshape(1, D),
        "wm": wm, "wff": wff,
        "bfp": jnp.pad(b_f[l], (0, N_FFPAD - N_FF)).reshape(1, N_FFPAD),
        "gq": jnp.tile(q_norm_g[l], FOX_HEADS).reshape(1, FOX_W),
        "gk": jnp.tile(k_norm_g[l], FOX_HEADS).reshape(1, FOX_W),
        "wbd": wbd.astype(BF16),
        "scale": pool_scale[l].reshape(1, POOL_W),
        "wout": wout_full[l],
    }


def _local_step(x, target, norm_g, win_full, b_f, q_norm_g, k_norm_g, w_pool, pool_scale, wout_full):
    L = norm_g.shape[0]
    T = min(_T, x.shape[0])
    consts = _constants(T)
    lws = [_layer_weights(l, norm_g, win_full, b_f, q_norm_g, k_norm_g, w_pool, pool_scale, wout_full)
           for l in range(L)]
    saved = []
    h = x
    for l in range(L):
        h, sv = _layer_fwd(h, lws[l], consts)
        saved.append(sv)
    dy, loss = _loss_head(h, target)
    grads = [None] * L
    for l in reversed(range(L)):
        dy, grads[l] = _layer_bwd(dy, saved[l], lws[l], consts)
    stacked = {k: jnp.stack([grads[l][k] for l in range(L)]) for k in grads[0]}
    return loss, dy, stacked


def _mesh_pos():
    return lax.axis_index("x"), lax.axis_index("y"), lax.axis_index("c")


_FLIPS = [(0, 0, 1), (1, 0, 0), (0, 1, 0), (1, 1, 0), (1, 0, 1), (0, 1, 1), (1, 1, 1)]


def _peers():
    x, y, c = _mesh_pos()
    out = []
    for fx, fy, fc in _FLIPS:
        px = 1 - x if fx else x
        py = 1 - y if fy else y
        pc = 1 - c if fc else c
        out.append(((px, py, pc), 4 * px + 2 * py + pc))
    return out, 4 * x + 2 * y + c


def _all_gather_pair(a, b):
    ANY = pl.BlockSpec(memory_space=pl.ANY)

    def body(a_ref, b_ref, ga_ref, gb_ref, send_sems, recv_sems, loc_sems):
        peers, me = _peers()
        la = pltpu.make_async_copy(a_ref, ga_ref.at[me], loc_sems.at[0])
        lb = pltpu.make_async_copy(b_ref, gb_ref.at[me], loc_sems.at[1])
        la.start()
        lb.start()
        copies = []
        for k, (dev, _) in enumerate(peers):
            for t, (src, dst) in enumerate(((a_ref, ga_ref), (b_ref, gb_ref))):
                cp = pltpu.make_async_remote_copy(
                    src_ref=src, dst_ref=dst.at[me], send_sem=send_sems.at[2 * k + t],
                    recv_sem=recv_sems.at[2 * k + t], device_id=dev, device_id_type=pl.DeviceIdType.MESH)
                cp.start()
                copies.append(cp)
        for cp in copies:
            cp.wait_recv()
        for cp in copies:
            cp.wait_send()
        la.wait()
        lb.wait()

    return pl.pallas_call(
        body, name="gather_weights",
        in_specs=[ANY, ANY], out_specs=[ANY, ANY],
        out_shape=[jax.ShapeDtypeStruct((N_DEV,) + a.shape, a.dtype), jax.ShapeDtypeStruct((N_DEV,) + b.shape, b.dtype)],
        scratch_shapes=[pltpu.SemaphoreType.DMA((14,)), pltpu.SemaphoreType.DMA((14,)), pltpu.SemaphoreType.DMA((2,))],
    )(a, b)


def _all_to_all_pair(a, b):
    ANY = pl.BlockSpec(memory_space=pl.ANY)

    def body(a_ref, b_ref, ra_ref, rb_ref, send_sems, recv_sems, loc_sems):
        peers, me = _peers()
        la = pltpu.make_async_copy(a_ref.at[me], ra_ref.at[me], loc_sems.at[0])
        lb = pltpu.make_async_copy(b_ref.at[me], rb_ref.at[me], loc_sems.at[1])
        la.start()
        lb.start()
        copies = []
        for k, (dev, idx) in enumerate(peers):
            for t, (src, dst) in enumerate(((a_ref, ra_ref), (b_ref, rb_ref))):
                cp = pltpu.make_async_remote_copy(
                    src_ref=src.at[idx], dst_ref=dst.at[me], send_sem=send_sems.at[2 * k + t],
                    recv_sem=recv_sems.at[2 * k + t], device_id=dev, device_id_type=pl.DeviceIdType.MESH)
                cp.start()
                copies.append(cp)
        for cp in copies:
            cp.wait_recv()
        for cp in copies:
            cp.wait_send()
        la.wait()
        lb.wait()

    return pl.pallas_call(
        body, name="exchange_grads",
        in_specs=[ANY, ANY], out_specs=[ANY, ANY],
        out_shape=[jax.ShapeDtypeStruct(a.shape, a.dtype), jax.ShapeDtypeStruct(b.shape, b.dtype)],
        scratch_shapes=[pltpu.SemaphoreType.DMA((14,)), pltpu.SemaphoreType.DMA((14,)), pltpu.SemaphoreType.DMA((2,))],
    )(a, b)


def _adam_math(w, g, m, v):
    m_new = ADAM_B1 * m + (1.0 - ADAM_B1) * g
    v_new = ADAM_B2 * v + (1.0 - ADAM_B2) * (g * g)
    m_hat = m_new / (1.0 - ADAM_B1 ** ADAM_STEP)
    v_hat = v_new / (1.0 - ADAM_B2 ** ADAM_STEP)
    delta = -ADAM_LR * (m_hat / (jnp.sqrt(v_hat) + ADAM_EPS) + ADAM_WD * w)
    return delta, m_new, v_new


def _sum_adamw(gparts, w, m, v, name):
    L, R, C = w.shape
    tr = min(128, R)

    def body(gp_ref, w_ref, m_ref, v_ref, g_ref, d_ref, nm_ref, nv_ref):
        g = gp_ref[0]
        for s in range(1, N_DEV):
            g = g + gp_ref[s]
        d, mn, vn = _adam_math(w_ref[...], g, m_ref[...], v_ref[...])
        g_ref[...] = g
        d_ref[...] = d
        nm_ref[...] = mn
        nv_ref[...] = vn

    blk = pl.BlockSpec((None, tr, C), lambda l, r: (l, r, 0))
    return pl.pallas_call(
        body, name=name,
        grid=(L, R // tr),
        in_specs=[pl.BlockSpec((N_DEV, None, tr, C), lambda l, r: (0, l, r, 0)), blk, blk, blk],
        out_specs=[blk, blk, blk, blk],
        out_shape=[jax.ShapeDtypeStruct((L, R, C), F32)] * 4,
        compiler_params=_cp(("parallel", "parallel"), 40 << 20),
    )(gparts, w, m, v)


def _small_update(gpack, wpack, mpack, vpack):
    R = gpack.shape[0]
    VM = pl.BlockSpec(memory_space=pltpu.VMEM)

    def body(g_ref, w_ref, m_ref, v_ref, gs_ref, d_ref, nm_ref, nv_ref, buf, send_sems, recv_sems):
        peers, me = _peers()
        buf[me] = g_ref[...]
        copies = []
        for k, (dev, _) in enumerate(peers):
            cp = pltpu.make_async_remote_copy(
                src_ref=g_ref, dst_ref=buf.at[me], send_sem=send_sems.at[k], recv_sem=recv_sems.at[k],
                device_id=dev, device_id_type=pl.DeviceIdType.MESH)
            cp.start()
            copies.append(cp)
        for cp in copies:
            cp.wait_recv()
        for cp in copies:
            cp.wait_send()
        g = buf[0]
        for s in range(1, N_DEV):
            g = g + buf[s]
        d, mn, vn = _adam_math(w_ref[...], g, m_ref[...], v_ref[...])
        gs_ref[...] = g
        d_ref[...] = d
        nm_ref[...] = mn
        nv_ref[...] = vn

    return pl.pallas_call(
        body, name="small_update",
        in_specs=[VM] * 4, out_specs=[VM] * 4,
        out_shape=[jax.ShapeDtypeStruct((R, 128), F32)] * 4,
        scratch_shapes=[pltpu.VMEM((N_DEV, R, 128), F32), pltpu.SemaphoreType.DMA((7,)), pltpu.SemaphoreType.DMA((7,))],
        compiler_params=_cp(None, 40 << 20),
    )(gpack, wpack, mpack, vpack)


_SMALL = ("norm_g", "b_f", "q_norm_g", "k_norm_g", "w_pool", "pool_scale")


def _pack(parts):
    flat = jnp.concatenate([p.reshape(-1) for p in parts])
    n = flat.shape[0]
    rows = -(-n // (8 * 128)) * 8
    return jnp.pad(flat, (0, rows * 128 - n)).reshape(rows, 128)


def _unpack(packed, like):
    flat = packed.reshape(-1)
    out, o = [], 0
    for p in like:
        out.append(flat[o:o + p.size].reshape(p.shape))
        o += p.size
    return out


def kernel(x, norm_g, w_in, b_f, q_norm_g, k_norm_g, w_pool, pool_scale, w_out, loss_target, m_norm_g, m_w_in, m_b_f, m_q_norm_g, m_k_norm_g, m_w_pool, m_pool_scale, m_w_out, v_norm_g, v_w_in, v_b_f, v_q_norm_g, v_k_norm_g, v_w_pool, v_pool_scale, v_w_out):
    L, D, cin = w_in.shape
    rout = w_out.shape[1]

    gin, gout = _all_gather_pair(w_in.astype(BF16), w_out.astype(BF16))
    win_full = gin.transpose(1, 2, 0, 3).reshape(L, D, N_DEV * cin)
    wout_full = gout.transpose(1, 0, 2, 3).reshape(L, N_DEV * rout, D)

    loss_local, dx, g = _local_step(x[0], loss_target[0], norm_g, win_full, b_f, q_norm_g, k_norm_g,
                                    w_pool, pool_scale, wout_full)
    loss = lax.psum(loss_local, MESH_AXES)

    gin_parts = g["w_in"].reshape(L, D, N_DEV, cin).transpose(2, 0, 1, 3)
    gout_parts = g["w_out"].reshape(L, N_DEV, rout, D).transpose(1, 0, 2, 3)
    rin, rout_parts = _all_to_all_pair(gin_parts, gout_parts)
    g_win, d_win, nm_win, nv_win = _sum_adamw(rin, w_in, m_w_in, v_w_in, "adamw_w_in")
    g_wout, d_wout, nm_wout, nv_wout = _sum_adamw(rout_parts, w_out, m_w_out, v_w_out, "adamw_w_out")

    ws = dict(norm_g=norm_g, b_f=b_f, q_norm_g=q_norm_g, k_norm_g=k_norm_g, w_pool=w_pool, pool_scale=pool_scale)
    ms = dict(norm_g=m_norm_g, b_f=m_b_f, q_norm_g=m_q_norm_g, k_norm_g=m_k_norm_g, w_pool=m_w_pool, pool_scale=m_pool_scale)
    vs = dict(norm_g=v_norm_g, b_f=v_b_f, q_norm_g=v_q_norm_g, k_norm_g=v_k_norm_g, w_pool=v_w_pool, pool_scale=v_pool_scale)
    like = [ws[k] for k in _SMALL]
    gs_p, d_p, nm_p, nv_p = _small_update(_pack([g[k] for k in _SMALL]), _pack(like),
                                          _pack([ms[k] for k in _SMALL]), _pack([vs[k] for k in _SMALL]))
    gs = dict(zip(_SMALL, _unpack(gs_p, like)))
    ds = dict(zip(_SMALL, _unpack(d_p, like)))
    nms = dict(zip(_SMALL, _unpack(nm_p, like)))
    nvs = dict(zip(_SMALL, _unpack(nv_p, like)))
    gs["w_in"], ds["w_in"], nms["w_in"], nvs["w_in"] = g_win, d_win, nm_win, nv_win
    gs["w_out"], ds["w_out"], nms["w_out"], nvs["w_out"] = g_wout, d_wout, nm_wout, nv_wout

    order = ("norm_g", "w_in", "b_f", "q_norm_g", "k_norm_g", "w_pool", "pool_scale", "w_out")
    return (loss, dx[None], *[gs[k] for k in order], *[ds[k] for k in order],
            *[nms[k] for k in order], *[nvs[k] for k in order])
```

```python
import functools

import jax
import jax.numpy as jnp
from jax import lax
from jax.experimental import pallas as pl
from jax.experimental.pallas import tpu as pltpu

F32 = jnp.float32
BF16 = jnp.bfloat16

EPS = 1e-6
NEG = -1e30
HEAD_DIM = 64
FOX_HEADS = 8
FOX_W = 512
POOL_W = 256
SB_W = 256
D_MIX = 1024
N_FF = 8
N_MAIN = 3584
N_FFPAD = 128
OFF_FQ, OFF_FK, OFF_FV, OFF_FG = 0, 512, 1024, 1536
OFF_PX, OFF_PG = 2048, 2304
OFF_SQ, OFF_SK, OFF_SV, OFF_SG = 2560, 2816, 3072, 3328
D_IN = 3592
Q_SCALE = HEAD_DIM ** -0.5

ADAM_LR = 0.001
ADAM_B1 = 0.9
ADAM_B2 = 0.999
ADAM_EPS = 1e-08
ADAM_WD = 0.01
ADAM_STEP = 10

N_DEV = 8
MESH_AXES = ("x", "y", "c")

_T = 256
_TM = 512
_VMEM_BIG = 56 << 20


def _cp(sem=None, vmem=None):
    kw = {}
    if sem is not None:
        kw["dimension_semantics"] = sem
    if vmem is not None:
        kw["vmem_limit_bytes"] = vmem
    return pltpu.CompilerParams(**kw)


def _dot(a, b):
    return jnp.dot(a, b, preferred_element_type=F32)


def _dot_nt(a, b):
    return lax.dot_general(a, b, (((1,), (1,)), ((), ())), preferred_element_type=F32)


def _dot_tn(a, b):
    return lax.dot_general(a, b, (((0,), (0,)), ((), ())), preferred_element_type=F32)


def _mm2(v, m, left=False):
    hi = v.astype(BF16)
    lo = (v - hi.astype(F32)).astype(BF16)
    if left:
        return _dot(m, hi) + _dot(m, lo)
    return _dot(hi, m) + _dot(lo, m)


def _mm3(v, m, left=False):
    a1 = v.astype(BF16)
    r1 = v - a1.astype(F32)
    a2 = r1.astype(BF16)
    a3 = (r1 - a2.astype(F32)).astype(BF16)
    if left:
        return _dot(m, a1) + _dot(m, a2) + _dot(m, a3)
    return _dot(a1, m) + _dot(a2, m) + _dot(a3, m)


def _sigmoid(z):
    return 1.0 / (1.0 + jnp.exp(-z))


def _rms_rows(x):
    return lax.rsqrt(jnp.mean(x * x, axis=-1, keepdims=True) + EPS)


def _inproj_fwd(x, g, wm, wff):
    S, D = x.shape
    tm = min(_TM, S)
    tn = 512

    def body(x_ref, g_ref, w_ref, wff_ref, o_ref, off_ref, h_ref):
        @pl.when(pl.program_id(1) == 0)
        def _():
            xv = x_ref[...]
            h = (xv * _rms_rows(xv)) * g_ref[...]
            h_ref[...] = h.astype(BF16)
            off_ref[...] = _dot(h_ref[...], wff_ref[...])

        o_ref[...] = _dot(h_ref[...], w_ref[...])

    return pl.pallas_call(
        body, name="inproj_fwd",
        grid=(S // tm, N_MAIN // tn),
        in_specs=[pl.BlockSpec((tm, D), lambda i, j: (i, 0)),
                  pl.BlockSpec((1, D), lambda i, j: (0, 0)),
                  pl.BlockSpec((D, tn), lambda i, j: (0, j)),
                  pl.BlockSpec((D, N_FFPAD), lambda i, j: (0, 0))],
        out_specs=[pl.BlockSpec((tm, tn), lambda i, j: (i, j)),
                   pl.BlockSpec((tm, N_FFPAD), lambda i, j: (i, 0))],
        out_shape=[jax.ShapeDtypeStruct((S, N_MAIN), F32), jax.ShapeDtypeStruct((S, N_FFPAD), F32)],
        scratch_shapes=[pltpu.VMEM((tm, D), BF16)],
        compiler_params=_cp(("parallel", "arbitrary"), 40 << 20),
    )(x, g, wm, wff)


def _head_norm(x, g, bd):
    ss = _mm2(x * x, bd)
    r = lax.rsqrt(ss * (1.0 / HEAD_DIM) + EPS)
    return (x * r) * g


def _fox_prep(proj, pff, bfp, gq, gk, bd, ex, tril):
    S = proj.shape[0]
    T = tril.shape[0]

    def body(q_ref, k_ref, ff_ref, b_ref, gq_ref, gk_ref, bd_ref, ex_ref, tri_ref,
             qs_ref, kn_ref, cc_ref, cqb_ref, carry):
        @pl.when(pl.program_id(0) == 0)
        def _():
            carry[...] = jnp.zeros_like(carry)

        bdv = bd_ref[...]
        qs_ref[...] = (_head_norm(q_ref[...], gq_ref[...], bdv) * Q_SCALE).astype(BF16)
        kn_ref[...] = _head_norm(k_ref[...], gk_ref[...], bdv).astype(BF16)
        u = ff_ref[...] + b_ref[...]
        lf = jnp.minimum(u, 0.0) - jnp.log1p(jnp.exp(-jnp.abs(u)))
        c = _mm3(lf, tri_ref[...], left=True) + carry[0:1, :]
        carry[0:1, :] = c[T - 1:T, :]
        cc_ref[...] = c
        cqb_ref[...] = _mm3(c, ex_ref[...])

    return pl.pallas_call(
        body, name="fox_prep",
        grid=(S // T,),
        in_specs=[pl.BlockSpec((T, FOX_W), lambda i: (i, OFF_FQ // FOX_W)),
                  pl.BlockSpec((T, FOX_W), lambda i: (i, OFF_FK // FOX_W)),
                  pl.BlockSpec((T, N_FFPAD), lambda i: (i, 0)),
                  pl.BlockSpec((1, N_FFPAD), lambda i: (0, 0)),
                  pl.BlockSpec((1, FOX_W), lambda i: (0, 0)),
                  pl.BlockSpec((1, FOX_W), lambda i: (0, 0)),
                  pl.BlockSpec((FOX_W, FOX_W), lambda i: (0, 0)),
                  pl.BlockSpec((N_FFPAD, FOX_W), lambda i: (0, 0)),
                  pl.BlockSpec((T, T), lambda i: (0, 0))],
        out_specs=[pl.BlockSpec((T, FOX_W), lambda i: (i, 0)),
                   pl.BlockSpec((T, FOX_W), lambda i: (i, 0)),
                   pl.BlockSpec((T, N_FFPAD), lambda i: (i, 0)),
                   pl.BlockSpec((T, FOX_W), lambda i: (i, 0))],
        out_shape=[jax.ShapeDtypeStruct((S, FOX_W), BF16), jax.ShapeDtypeStruct((S, FOX_W), BF16),
                   jax.ShapeDtypeStruct((S, N_FFPAD), F32), jax.ShapeDtypeStruct((S, FOX_W), F32)],
        scratch_shapes=[pltpu.VMEM((8, N_FFPAD), F32)],
        compiler_params=_cp(("arbitrary",), 40 << 20),
    )(proj, proj, pff, bfp, gq, gk, bd, ex, tril)


def _pair_blk(S, off=0):
    return pl.BlockSpec((S, 128), lambda p: (0, off + p), pipeline_mode=pl.Buffered(1))


def _pair_rows(S):
    return pl.BlockSpec((None, 8, S), lambda p: (p, 0, 0), pipeline_mode=pl.Buffered(1))


def _head_masks(S):
    return lax.broadcasted_iota(jnp.int32, (S, 128), 1) < HEAD_DIM


_EXP_ZERO = 104.0


def _score_bounds(qa, qb, k, lane_s):
    def max_norm2(x2):
        return jnp.max(jnp.sum(x2, axis=1, keepdims=True), axis=0, keepdims=True)

    qaf = qa.astype(F32)
    qbf = qb.astype(F32)
    k2 = k.astype(F32)
    k2 = k2 * k2
    za = jnp.sqrt(max_norm2(qaf * qaf) * max_norm2(jnp.where(lane_s, k2, 0.0)))
    zb = jnp.sqrt(max_norm2(qbf * qbf) * max_norm2(jnp.where(lane_s, 0.0, k2)))
    return jnp.max(za) * 1.001 + 1e-3, jnp.max(zb) * 1.001 + 1e-3


def _fox_tiles_back(cr_ref, i, r0, T, zba, zbb, tile_fn):
    cf = cr_ref[:, pl.ds(r0, 128)]
    cfa = jnp.max(cf[0:1, 0:1])
    cfb = jnp.max(cf[1:2, 0:1])

    def alive(j):
        cl = cr_ref[:, pl.ds(pl.multiple_of(j * T + (T - 128), 128), 128)]
        gap_a = cfa - jnp.max(cl[0:1, 127:128])
        gap_b = cfb - jnp.max(cl[1:2, 127:128])
        return jnp.maximum(2.0 * zba + gap_a, 2.0 * zbb + gap_b) > -_EXP_ZERO

    def cond(st):
        return (st[0] < i) & st[1]

    def step(st):
        tile_fn(i - 1 - st[0])
        return st[0] + 1, alive(jnp.maximum(i - 2 - st[0], 0))

    done, _ = lax.while_loop(cond, step, (jnp.int32(0), alive(jnp.maximum(i - 1, 0))))
    return done


def _fox_fwd(qs, kn, proj, cqb, crow4):
    S = qs.shape[0]
    T = min(_T, S)
    nq = S // T

    def body(q_ref, k_ref, v_ref, cq_ref, cr_ref, o_ref, lse_ref, qa, qb, vta, vtb, cka, ckb, ma, mb, acca, accb):
        lane_s = _head_masks(S)
        q = q_ref[...]
        zq = jnp.zeros_like(q)
        qa[...] = jnp.where(lane_s, q, zq)
        qb[...] = jnp.where(lane_s, zq, q)
        cq = cq_ref[...]
        cka[...] = jnp.broadcast_to(cq[:, 0:1], (S, 128))
        ckb[...] = jnp.broadcast_to(cq[:, 64:65], (S, 128))
        lse_ref[...] = jnp.zeros((8, S), F32)
        row_t = lax.broadcasted_iota(jnp.int32, (128, T), 0) < HEAD_DIM
        zba, zbb = _score_bounds(qa[...], qb[...], k_ref[...], lane_s)

        def prep(c, carry):
            c0 = pl.multiple_of(c * T, T)
            vt = v_ref[pl.ds(c0, T), :].T
            vta[:, pl.ds(c0, T)] = jnp.where(row_t, vt, 1.0).astype(BF16)
            vtb[:, pl.ds(c0, T)] = jnp.where(row_t, 1.0, vt).astype(BF16)
            return carry

        lax.fori_loop(0, nq, prep, 0)
        causal = (lax.broadcasted_iota(jnp.int32, (T, T), 0) <= lax.broadcasted_iota(jnp.int32, (T, T), 1))

        def head(qh, k, vt, cq_row, ck_t, m_ref, acc_ref, masked):
            s = _dot_nt(k, qh) + cq_row - ck_t
            if masked:
                s = jnp.where(causal, s, NEG)
            m_old = m_ref[0:1, :]
            m_new = jnp.maximum(m_old, jnp.max(s, axis=0, keepdims=True))
            p = jnp.exp(s - m_new)
            acc_ref[...] = jnp.exp(m_old - m_new) * acc_ref[...] + _dot(vt, p.astype(BF16))
            m_ref[0:1, :] = m_new

        def kv(j, r0, masked):
            c0 = pl.multiple_of(j * T, T)
            k = k_ref[pl.ds(c0, T), :]
            cr = cr_ref[:, pl.ds(r0, T)]
            head(qa[pl.ds(r0, T), :], k, vta[:, pl.ds(c0, T)], cr[0:1, :],
                 jnp.tile(cka[pl.ds(c0, T), :], (1, T // 128)), ma, acca, masked)
            head(qb[pl.ds(r0, T), :], k, vtb[:, pl.ds(c0, T)], cr[1:2, :],
                 jnp.tile(ckb[pl.ds(c0, T), :], (1, T // 128)), mb, accb, masked)

        def qblk(i, carry):
            r0 = pl.multiple_of(i * T, T)
            ma[...] = jnp.full((8, T), NEG, F32)
            mb[...] = jnp.full((8, T), NEG, F32)
            acca[...] = jnp.zeros((128, T), F32)
            accb[...] = jnp.zeros((128, T), F32)
            kv(i, r0, True)
            done = _fox_tiles_back(cr_ref, i, r0, T, zba, zbb, lambda j: kv(j, r0, False))
            aa = acca[...]
            ab = accb[...]
            la = aa[64:65, :]
            lb = ab[0:1, :]
            o_ref[pl.ds(r0, T), :] = jnp.where(row_t, aa / la, ab / lb).T
            lse_ref[0:1, pl.ds(r0, T)] = ma[0:1, :] + jnp.log(la)
            lse_ref[1:2, pl.ds(r0, T)] = mb[0:1, :] + jnp.log(lb)
            lse_ref[2:3, pl.ds(r0, T)] = jnp.broadcast_to(done.astype(F32), (1, T))
            return carry

        lax.fori_loop(0, nq, qblk, 0)

    return pl.pallas_call(
        body, name="fox_fwd",
        grid=(FOX_W // 128,),
        in_specs=[_pair_blk(S), _pair_blk(S), _pair_blk(S, OFF_FV // 128), _pair_blk(S), _pair_rows(S)],
        out_specs=[_pair_blk(S), _pair_rows(S)],
        out_shape=[jax.ShapeDtypeStruct((S, FOX_W), F32), jax.ShapeDtypeStruct((FOX_W // 128, 8, S), F32)],
        scratch_shapes=[pltpu.VMEM((S, 128), BF16)] * 2 + [pltpu.VMEM((128, S), BF16)] * 2
        + [pltpu.VMEM((S, 128), F32)] * 2 + [pltpu.VMEM((8, T), F32)] * 2 + [pltpu.VMEM((128, T), F32)] * 2,
        compiler_params=_cp(("arbitrary",), _VMEM_BIG),
    )(qs, kn, proj, cqb, crow4)


def _softplus_parts(z):
    e = jnp.exp(-jnp.abs(z))
    return e, jnp.maximum(z, 0.0) + jnp.log1p(e)


def _sb_fwd(proj, triu):
    S = proj.shape[0]
    T = triu.shape[0]
    nq = S // T

    def body(q_ref, k_ref, v_ref, tri_ref, o_ref, lt_ref, qa, qb, kb, vt, ra, rb, acca, accb):
        lane_s = _head_masks(S)
        q = (q_ref[...] * Q_SCALE).astype(BF16)
        zq = jnp.zeros_like(q)
        qa[...] = jnp.where(lane_s, q, zq)
        qb[...] = jnp.where(lane_s, zq, q)
        kb[...] = k_ref[...].astype(BF16)
        lt_ref[...] = jnp.zeros((8, S), F32)
        row_t = lax.broadcasted_iota(jnp.int32, (128, T), 0) < HEAD_DIM
        zba, zbb = _score_bounds(qa[...], qb[...], kb[...], lane_s)

        def prep(c, carry):
            c0 = pl.multiple_of(c * T, T)
            vt[:, pl.ds(c0, T)] = v_ref[pl.ds(c0, T), :].T.astype(BF16)
            return carry

        lax.fori_loop(0, nq, prep, 0)
        strict = (lax.broadcasted_iota(jnp.int32, (T, T), 0) < lax.broadcasted_iota(jnp.int32, (T, T), 1))

        def head(qh, k, vtt, r_ref, acc_ref, masked):
            z = _dot_nt(k, qh)
            _, sp = _softplus_parts(z)
            lb = -sp
            if masked:
                lb = jnp.where(strict, lb, 0.0)
            inc = _mm2(lb, tri_ref[...], left=True)
            r = r_ref[0:1, :]
            a = jnp.exp(z + inc + r)
            if masked:
                a = jnp.where(strict, a, 0.0)
            r_ref[0:1, :] = r + inc[0:1, :]
            acc_ref[...] = acc_ref[...] + _dot(vtt, a.astype(BF16))

        def kv(j, r0, masked):
            c0 = pl.multiple_of(j * T, T)
            k = kb[pl.ds(c0, T), :]
            vtt = vt[:, pl.ds(c0, T)]
            head(qa[pl.ds(r0, T), :], k, vtt, ra, acca, masked)
            head(qb[pl.ds(r0, T), :], k, vtt, rb, accb, masked)

        def qblk(i, carry):
            r0 = pl.multiple_of(i * T, T)
            ra[...] = jnp.zeros((8, T), F32)
            rb[...] = jnp.zeros((8, T), F32)
            acca[...] = jnp.zeros((128, T), F32)
            accb[...] = jnp.zeros((128, T), F32)
            kv(i, r0, True)

            def alive():
                return jnp.maximum(jnp.max(ra[0:1, :]) + zba, jnp.max(rb[0:1, :]) + zbb) > -_EXP_ZERO

            def cond(st):
                return (st[0] < i) & st[1]

            def step(st):
                kv(i - 1 - st[0], r0, False)
                return st[0] + 1, alive()

            done, _ = lax.while_loop(cond, step, (jnp.int32(0), alive()))
            o_ref[pl.ds(r0, T), :] = jnp.where(row_t, acca[...], accb[...]).T
            lt_ref[0:1, pl.ds(r0, T)] = ra[0:1, :]
            lt_ref[1:2, pl.ds(r0, T)] = rb[0:1, :]
            lt_ref[2:3, pl.ds(r0, T)] = jnp.broadcast_to(done.astype(F32), (1, T))
            return carry

        lax.fori_loop(0, nq, qblk, 0)

    return pl.pallas_call(
        body, name="sb_fwd",
        grid=(SB_W // 128,),
        in_specs=[_pair_blk(S, OFF_SQ // 128), _pair_blk(S, OFF_SK // 128), _pair_blk(S, OFF_SV // 128),
                  pl.BlockSpec((T, T), lambda p: (0, 0))],
        out_specs=[_pair_blk(S), _pair_rows(S)],
        out_shape=[jax.ShapeDtypeStruct((S, SB_W), F32), jax.ShapeDtypeStruct((SB_W // 128, 8, S), F32)],
        scratch_shapes=[pltpu.VMEM((S, 128), BF16)] * 3 + [pltpu.VMEM((128, S), BF16)]
        + [pltpu.VMEM((8, T), F32)] * 2 + [pltpu.VMEM((128, T), F32)] * 2,
        compiler_params=_cp(("arbitrary",), _VMEM_BIG),
    )(proj, proj, proj, triu)


def _pool_window_lanes(shape):
    lane = lax.broadcasted_iota(jnp.int32, shape, 1)
    return jnp.where(lane < 64, 2, jnp.where(lane < 128, 4, jnp.where(lane < 192, 8, 16)))


def _pool_fwd(proj):
    S = proj.shape[0]

    def body(x_ref, o_ref):
        x = x_ref[...]
        t = lax.broadcasted_iota(jnp.int32, x.shape, 0)
        lane = lax.broadcasted_iota(jnp.int32, x.shape, 1)

        def back(a, k):
            return jnp.where(t >= k, pltpu.roll(a, k, 0), 0.0)

        s1 = x + back(x, 1)
        s2 = s1 + back(s1, 2)
        s4 = s2 + back(s2, 4)
        s8 = s4 + back(s4, 8)
        win = jnp.where(lane < 64, s1, jnp.where(lane < 128, s2, jnp.where(lane < 192, s4, s8)))
        cnt = jnp.minimum(t + 1, _pool_window_lanes(x.shape)).astype(F32)
        o_ref[...] = win / cnt - x

    return pl.pallas_call(
        body, name="pool_fwd",
        grid=(1,),
        in_specs=[pl.BlockSpec((S, POOL_W), lambda i: (0, OFF_PX // POOL_W))],
        out_specs=pl.BlockSpec((S, POOL_W), lambda i: (0, 0)),
        out_shape=jax.ShapeDtypeStruct((S, POOL_W), F32),
        compiler_params=_cp(("arbitrary",), _VMEM_BIG),
    )(proj)


def _silu(g):
    return g * _sigmoid(g)


def _mix_out(fo, so, pooled, proj, wbd, scale, wout, x):
    S, D = x.shape
    tm = min(256, S)

    def body(fo_ref, fg_ref, so_ref, sg_ref, pl_ref, pg_ref, wbd_ref, sc_ref, w_ref, x_ref, y_ref, mx_ref):
        mx_ref[:, 0:FOX_W] = (fo_ref[...] * _silu(fg_ref[...])).astype(BF16)
        yp = _dot(pl_ref[...].astype(BF16), wbd_ref[...]) * sc_ref[...]
        mx_ref[:, FOX_W:FOX_W + POOL_W] = (yp * _silu(pg_ref[...])).astype(BF16)
        mx_ref[:, FOX_W + POOL_W:D_MIX] = (so_ref[...] * _silu(sg_ref[...])).astype(BF16)
        y_ref[...] = x_ref[...] + _dot(mx_ref[...], w_ref[...])

    return pl.pallas_call(
        body, name="mix_out",
        grid=(S // tm,),
        in_specs=[pl.BlockSpec((tm, FOX_W), lambda i: (i, 0)),
                  pl.BlockSpec((tm, FOX_W), lambda i: (i, OFF_FG // FOX_W)),
                  pl.BlockSpec((tm, SB_W), lambda i: (i, 0)),
                  pl.BlockSpec((tm, SB_W), lambda i: (i, OFF_SG // SB_W)),
                  pl.BlockSpec((tm, POOL_W), lambda i: (i, 0)),
                  pl.BlockSpec((tm, POOL_W), lambda i: (i, OFF_PG // POOL_W)),
                  pl.BlockSpec((POOL_W, POOL_W), lambda i: (0, 0)),
                  pl.BlockSpec((1, POOL_W), lambda i: (0, 0)),
                  pl.BlockSpec((D_MIX, D), lambda i: (0, 0)),
                  pl.BlockSpec((tm, D), lambda i: (i, 0))],
        out_specs=[pl.BlockSpec((tm, D), lambda i: (i, 0)), pl.BlockSpec((tm, D_MIX), lambda i: (i, 0))],
        out_shape=[jax.ShapeDtypeStruct((S, D), F32), jax.ShapeDtypeStruct((S, D_MIX), BF16)],
        compiler_params=_cp(("parallel",), 40 << 20),
    )(fo, proj, so, proj, pooled, proj, wbd, scale, wout, x)


def _loss_head(y, target):
    S, D = y.shape
    tm = min(_TM, S)

    def body(y_ref, t_ref, dy_ref, ls_ref):
        @pl.when(pl.program_id(0) == 0)
        def _():
            ls_ref[...] = jnp.zeros_like(ls_ref)

        e = y_ref[...] - t_ref[...]
        dy_ref[...] = e * (1.0 / D)
        ls_ref[...] = ls_ref[...] + jnp.sum(e * e) * (0.5 / D)

    dy, ls = pl.pallas_call(
        body, name="loss_head",
        grid=(S // tm,),
        in_specs=[pl.BlockSpec((tm, D), lambda i: (i, 0)), pl.BlockSpec((tm, D), lambda i: (i, 0))],
        out_specs=[pl.BlockSpec((tm, D), lambda i: (i, 0)), pl.BlockSpec((8, 128), lambda i: (0, 0))],
        out_shape=[jax.ShapeDtypeStruct((S, D), F32), jax.ShapeDtypeStruct((8, 128), F32)],
        compiler_params=_cp(("arbitrary",), 40 << 20),
    )(y, target)
    return dy, ls[0, 0]


def _dsilu(g):
    s = _sigmoid(g)
    return s * (1.0 + g * (1.0 - s))


def _gate_bwd(dy, wout, fo, so, pooled, proj, wbd, scale):
    S, D = dy.shape
    tm = min(256, S)

    def body(dy_ref, w_ref, fo_ref, fg_ref, so_ref, sg_ref, pl_ref, pg_ref, wbd_ref, sc_ref,
             dfo_ref, dfg_ref, dso_ref, dsg_ref, dpg_ref, dpl_ref, dsc_ref, dwbd_ref):
        @pl.when(pl.program_id(0) == 0)
        def _():
            dsc_ref[...] = jnp.zeros_like(dsc_ref)
            dwbd_ref[...] = jnp.zeros_like(dwbd_ref)

        dm = _dot_nt(dy_ref[...].astype(BF16), w_ref[...])
        dmf = dm[:, 0:FOX_W]
        dmp = dm[:, FOX_W:FOX_W + POOL_W]
        dms = dm[:, FOX_W + POOL_W:D_MIX]
        fg = fg_ref[...]
        dfo_ref[...] = dmf * _silu(fg)
        dfg_ref[...] = (dmf * fo_ref[...] * _dsilu(fg)).astype(BF16)
        sg = sg_ref[...]
        dso_ref[...] = dms * _silu(sg)
        dsg_ref[...] = (dms * so_ref[...] * _dsilu(sg)).astype(BF16)
        pg = pg_ref[...]
        plb = pl_ref[...].astype(BF16)
        yw = _dot(plb, wbd_ref[...])
        sc = sc_ref[...]
        dpg_ref[...] = (dmp * (yw * sc) * _dsilu(pg)).astype(BF16)
        dys = dmp * _silu(pg)
        dsc_ref[...] = dsc_ref[...] + jnp.sum(dys * yw, axis=0, keepdims=True)
        dyw = (dys * sc).astype(BF16)
        dpl_ref[...] = _dot_nt(dyw, wbd_ref[...])
        dwbd_ref[...] = dwbd_ref[...] + _dot_tn(plb, dyw)

    return pl.pallas_call(
        body, name="gate_bwd",
        grid=(S // tm,),
        in_specs=[pl.BlockSpec((tm, D), lambda i: (i, 0)),
                  pl.BlockSpec((D_MIX, D), lambda i: (0, 0)),
                  pl.BlockSpec((tm, FOX_W), lambda i: (i, 0)),
                  pl.BlockSpec((tm, FOX_W), lambda i: (i, OFF_FG // FOX_W)),
                  pl.BlockSpec((tm, SB_W), lambda i: (i, 0)),
                  pl.BlockSpec((tm, SB_W), lambda i: (i, OFF_SG // SB_W)),
                  pl.BlockSpec((tm, POOL_W), lambda i: (i, 0)),
                  pl.BlockSpec((tm, POOL_W), lambda i: (i, OFF_PG // POOL_W)),
                  pl.BlockSpec((POOL_W, POOL_W), lambda i: (0, 0)),
                  pl.BlockSpec((1, POOL_W), lambda i: (0, 0))],
        out_specs=[pl.BlockSpec((tm, FOX_W), lambda i: (i, 0)),
                   pl.BlockSpec((tm, FOX_W), lambda i: (i, 0)),
                   pl.BlockSpec((tm, SB_W), lambda i: (i, 0)),
                   pl.BlockSpec((tm, SB_W), lambda i: (i, 0)),
                   pl.BlockSpec((tm, POOL_W), lambda i: (i, 0)),
                   pl.BlockSpec((tm, POOL_W), lambda i: (i, 0)),
                   pl.BlockSpec((1, POOL_W), lambda i: (0, 0)),
                   pl.BlockSpec((POOL_W, POOL_W), lambda i: (0, 0))],
        out_shape=[jax.ShapeDtypeStruct((S, FOX_W), F32), jax.ShapeDtypeStruct((S, FOX_W), BF16),
                   jax.ShapeDtypeStruct((S, SB_W), F32), jax.ShapeDtypeStruct((S, SB_W), BF16),
                   jax.ShapeDtypeStruct((S, POOL_W), BF16), jax.ShapeDtypeStruct((S, POOL_W), F32),
                   jax.ShapeDtypeStruct((1, POOL_W), F32), jax.ShapeDtypeStruct((POOL_W, POOL_W), F32)],
        compiler_params=_cp(("arbitrary",), 40 << 20),
    )(dy, wout, fo, proj, so, proj, pooled, proj, wbd, scale)


def _matmul_tn(a, b, name):
    S, M = a.shape
    N = b.shape[1]
    tk = min(_TM, S)
    tn = min(512, N)

    def body(a_ref, b_ref, o_ref):
        @pl.when(pl.program_id(1) == 0)
        def _():
            o_ref[...] = jnp.zeros_like(o_ref)

        o_ref[...] = o_ref[...] + _dot_tn(a_ref[...].astype(BF16), b_ref[...].astype(BF16))

    return pl.pallas_call(
        body, name=name,
        grid=(N // tn, S // tk),
        in_specs=[pl.BlockSpec((tk, M), lambda j, k: (k, 0)), pl.BlockSpec((tk, tn), lambda j, k: (k, j))],
        out_specs=pl.BlockSpec((M, tn), lambda j, k: (0, j)),
        out_shape=jax.ShapeDtypeStruct((M, N), F32),
        compiler_params=_cp(("parallel", "arbitrary"), 40 << 20),
    )(a, b)


def _pool_bwd(dpooled):
    S = dpooled.shape[0]

    def body(d_ref, o_ref):
        d = d_ref[...]
        t = lax.broadcasted_iota(jnp.int32, d.shape, 0)
        lane = lax.broadcasted_iota(jnp.int32, d.shape, 1)
        cnt = jnp.minimum(t + 1, _pool_window_lanes(d.shape)).astype(F32)
        u = d / cnt

        def fwd(a, k):
            return jnp.where(t < S - k, pltpu.roll(a, S - k, 0), 0.0)

        s1 = u + fwd(u, 1)
        s2 = s1 + fwd(s1, 2)
        s4 = s2 + fwd(s2, 4)
        s8 = s4 + fwd(s4, 8)
        win = jnp.where(lane < 64, s1, jnp.where(lane < 128, s2, jnp.where(lane < 192, s4, s8)))
        o_ref[...] = (win - d).astype(BF16)

    return pl.pallas_call(
        body, name="pool_bwd",
        grid=(1,),
        in_specs=[pl.BlockSpec((S, POOL_W), lambda i: (0, 0))],
        out_specs=pl.BlockSpec((S, POOL_W), lambda i: (0, 0)),
        out_shape=jax.ShapeDtypeStruct((S, POOL_W), BF16),
        compiler_params=_cp(("arbitrary",), _VMEM_BIG),
    )(dpooled)


def _fox_bwd(qs, kn, proj, dfo, fo, lse, cqb, crow4):
    S = qs.shape[0]
    T = min(_T, S)
    nq = S // T

    def body(q_ref, k_ref, v_ref, do_ref, o_ref, lse_ref, cq_ref, cr_ref,
             dq_ref, dk_ref, dv_ref, dck_ref, dcq_ref,
             qa, qb, kta, ktb, vb, doa, dob, cka, ckb, dcka, dckb, dva, dqt, dcqa, dcqb):
        lane_s = _head_masks(S)
        q = q_ref[...]
        zq = jnp.zeros_like(q)
        qa[...] = jnp.where(lane_s, q, zq)
        qb[...] = jnp.where(lane_s, zq, q)
        vb[...] = v_ref[...].astype(BF16)
        do = do_ref[...].astype(BF16)
        doa[...] = jnp.where(lane_s, do, zq)
        dob[...] = jnp.where(lane_s, zq, do)
        cq = cq_ref[...]
        cka[...] = jnp.broadcast_to(cq[:, 0:1], (S, 128))
        ckb[...] = jnp.broadcast_to(cq[:, 64:65], (S, 128))
        zs = jnp.zeros((S, 128), F32)
        dk_ref[...] = zs
        dva[...] = zs
        dcka[...] = zs
        dckb[...] = zs
        dcq_ref[...] = jnp.zeros((8, S), F32)
        row_t = lax.broadcasted_iota(jnp.int32, (128, T), 0) < HEAD_DIM

        def prep(c, carry):
            c0 = pl.multiple_of(c * T, T)
            kt = k_ref[pl.ds(c0, T), :].astype(F32).T
            kta[:, pl.ds(c0, T)] = jnp.where(row_t, kt, 0.0).astype(BF16)
            ktb[:, pl.ds(c0, T)] = jnp.where(row_t, 0.0, kt).astype(BF16)
            return carry

        lax.fori_loop(0, nq, prep, 0)
        causal = (lax.broadcasted_iota(jnp.int32, (T, T), 0) <= lax.broadcasted_iota(jnp.int32, (T, T), 1))

        def head(qh, kfull, kth, v, doh, cq_row, ck_t, lse_row, dl_row, dck_acc, dcq_acc, c0, masked):
            s = _dot_nt(kfull, qh) + cq_row - ck_t
            if masked:
                s = jnp.where(causal, s, NEG)
            p = jnp.exp(s - lse_row)
            dp = _dot_nt(v, doh)
            ds = p * (dp - dl_row)
            pb = p.astype(BF16)
            dsb = ds.astype(BF16)
            dva[pl.ds(c0, T), :] = dva[pl.ds(c0, T), :] + _dot(pb, doh)
            dk_ref[pl.ds(c0, T), :] = dk_ref[pl.ds(c0, T), :] + _dot(dsb, qh)
            dqt[...] = dqt[...] + _dot(kth, dsb)
            dcq_acc[0:1, :] = dcq_acc[0:1, :] + jnp.sum(ds, axis=0, keepdims=True)
            fold = ds[:, 0:128]
            for t in range(1, T // 128):
                fold = fold + ds[:, 128 * t:128 * (t + 1)]
            dck_acc[pl.ds(c0, T), :] = dck_acc[pl.ds(c0, T), :] - fold

        def kv(j, r0, lsa, lsb, dla, dlb, masked):
            c0 = pl.multiple_of(j * T, T)
            kfull = k_ref[pl.ds(c0, T), :]
            v = vb[pl.ds(c0, T), :]
            cr = cr_ref[:, pl.ds(r0, T)]
            head(qa[pl.ds(r0, T), :], kfull, kta[:, pl.ds(c0, T)], v, doa[pl.ds(r0, T), :], cr[0:1, :],
                 jnp.tile(cka[pl.ds(c0, T), :], (1, T // 128)), lsa, dla, dcka, dcqa, c0, masked)
            head(qb[pl.ds(r0, T), :], kfull, ktb[:, pl.ds(c0, T)], v, dob[pl.ds(r0, T), :], cr[1:2, :],
                 jnp.tile(ckb[pl.ds(c0, T), :], (1, T // 128)), lsb, dlb, dckb, dcqb, c0, masked)

        def qblk(i, carry):
            r0 = pl.multiple_of(i * T, T)
            dt = (do_ref[pl.ds(r0, T), :] * o_ref[pl.ds(r0, T), :]).T
            dla = jnp.sum(jnp.where(row_t, dt, 0.0), axis=0, keepdims=True)
            dlb = jnp.sum(jnp.where(row_t, 0.0, dt), axis=0, keepdims=True)
            ls = lse_ref[:, pl.ds(r0, T)]
            lsa = ls[0:1, :]
            lsb = ls[1:2, :]
            back = jnp.max(ls[2:3, :]).astype(jnp.int32)
            dqt[...] = jnp.zeros((128, T), F32)
            dcqa[...] = jnp.zeros((8, T), F32)
            dcqb[...] = jnp.zeros((8, T), F32)

            def inner(j, c):
                kv(j, r0, lsa, lsb, dla, dlb, False)
                return c

            lax.fori_loop(i - back, i, inner, 0)
            kv(i, r0, lsa, lsb, dla, dlb, True)
            dq_ref[pl.ds(r0, T), :] = dqt[...].T
            dcq_ref[0:1, pl.ds(r0, T)] = dcqa[0:1, :]
            dcq_ref[1:2, pl.ds(r0, T)] = dcqb[0:1, :]
            return carry

        lax.fori_loop(0, nq, qblk, 0)
        dv_ref[...] = dva[...].astype(BF16)
        dck_ref[...] = jnp.where(lane_s, jnp.sum(dcka[...], axis=1, keepdims=True),
                                 jnp.sum(dckb[...], axis=1, keepdims=True))

    return pl.pallas_call(
        body, name="fox_bwd",
        grid=(FOX_W // 128,),
        in_specs=[_pair_blk(S), _pair_blk(S), _pair_blk(S, OFF_FV // 128), _pair_blk(S), _pair_blk(S),
                  _pair_rows(S), _pair_blk(S), _pair_rows(S)],
        out_specs=[_pair_blk(S), _pair_blk(S), _pair_blk(S), _pair_blk(S), _pair_rows(S)],
        out_shape=[jax.ShapeDtypeStruct((S, FOX_W), F32), jax.ShapeDtypeStruct((S, FOX_W), F32),
                   jax.ShapeDtypeStruct((S, FOX_W), BF16), jax.ShapeDtypeStruct((S, FOX_W), F32),
                   jax.ShapeDtypeStruct((FOX_W // 128, 8, S), F32)],
        scratch_shapes=[pltpu.VMEM((S, 128), BF16)] * 2 + [pltpu.VMEM((128, S), BF16)] * 2
        + [pltpu.VMEM((S, 128), BF16)] * 3 + [pltpu.VMEM((S, 128), F32)] * 5
        + [pltpu.VMEM((128, T), F32)] + [pltpu.VMEM((8, T), F32)] * 2,
        compiler_params=_cp(("arbitrary",), _VMEM_BIG),
    )(qs, kn, proj, dfo, fo, lse, cqb, crow4)


def _sb_bwd(proj, dso, ltot, tril):
    S = proj.shape[0]
    T = tril.shape[0]
    nq = S // T

    def body(q_ref, k_ref, v_ref, do_ref, lt_ref, tri_ref, dq_ref, dk_ref, dv_ref,
             qa, qb, k2, kta, ktb, vb, doa, dob, dka, dva, dqt, ra, rb, ga, gb):
        lane_s = _head_masks(S)
        q = (q_ref[...] * Q_SCALE).astype(BF16)
        zq = jnp.zeros_like(q)
        qa[...] = jnp.where(lane_s, q, zq)
        qb[...] = jnp.where(lane_s, zq, q)
        k2[...] = k_ref[...].astype(BF16)
        vb[...] = v_ref[...].astype(BF16)
        do = do_ref[...].astype(BF16)
        doa[...] = jnp.where(lane_s, do, zq)
        dob[...] = jnp.where(lane_s, zq, do)
        dka[...] = jnp.zeros((S, 128), F32)
        dva[...] = jnp.zeros((S, 128), F32)
        row_t = lax.broadcasted_iota(jnp.int32, (128, T), 0) < HEAD_DIM

        def prep(c, carry):
            c0 = pl.multiple_of(c * T, T)
            kt = k_ref[pl.ds(c0, T), :].T
            kta[:, pl.ds(c0, T)] = jnp.where(row_t, kt, 0.0).astype(BF16)
            ktb[:, pl.ds(c0, T)] = jnp.where(row_t, 0.0, kt).astype(BF16)
            return carry

        lax.fori_loop(0, nq, prep, 0)
        strict = (lax.broadcasted_iota(jnp.int32, (T, T), 0) < lax.broadcasted_iota(jnp.int32, (T, T), 1))

        def head(qh, kfull, kth, v, doh, r_ref, g_ref, lt_row, c0, masked):
            z = _dot_nt(kfull, qh)
            e, sp = _softplus_parts(z)
            lb = -sp
            if masked:
                lb = jnp.where(strict, lb, 0.0)
            tri = tri_ref[...]
            pre = _mm2(lb, tri, left=True)
            r = r_ref[0:1, :]
            a = jnp.exp(z + lb + ((lt_row - r) - pre))
            if masked:
                a = jnp.where(strict, a, 0.0)
            da = _dot_nt(v, doh)
            g = a * da
            gpre = _mm2(g, tri, left=True)
            gc = g_ref[0:1, :]
            big_g = gc + (gpre - g)
            inv = 1.0 / (1.0 + e)
            pos = z >= 0.0
            sig = jnp.where(pos, 1.0, e) * inv
            oms = jnp.where(pos, e, 1.0) * inv
            dz = g * oms - sig * big_g
            if masked:
                dz = jnp.where(strict, dz, 0.0)
            dzb = dz.astype(BF16)
            dqt[...] = dqt[...] + _dot(kth, dzb)
            dka[pl.ds(c0, T), :] = dka[pl.ds(c0, T), :] + _dot(dzb, qh)
            dva[pl.ds(c0, T), :] = dva[pl.ds(c0, T), :] + _dot(a.astype(BF16), doh)
            r_ref[0:1, :] = r + pre[T - 1:T, :]
            g_ref[0:1, :] = gc + gpre[T - 1:T, :]

        def kv(j, r0, lta, ltb, masked):
            c0 = pl.multiple_of(j * T, T)
            kfull = k2[pl.ds(c0, T), :]
            v = vb[pl.ds(c0, T), :]
            head(qa[pl.ds(r0, T), :], kfull, kta[:, pl.ds(c0, T)], v, doa[pl.ds(r0, T), :], ra, ga, lta, c0, masked)
            head(qb[pl.ds(r0, T), :], kfull, ktb[:, pl.ds(c0, T)], v, dob[pl.ds(r0, T), :], rb, gb, ltb, c0, masked)

        def qblk(i, carry):
            r0 = pl.multiple_of(i * T, T)
            lt = lt_ref[:, pl.ds(r0, T)]
            lta = lt[0:1, :]
            ltb = lt[1:2, :]
            back = jnp.max(lt[2:3, :]).astype(jnp.int32)
            zt = jnp.zeros((8, T), F32)
            dqt[...] = jnp.zeros((128, T), F32)
            ra[...] = zt
            rb[...] = zt
            ga[...] = zt
            gb[...] = zt

            def inner(j, c):
                kv(j, r0, lta, ltb, False)
                return c

            lax.fori_loop(i - back, i, inner, 0)
            kv(i, r0, lta, ltb, True)
            dq_ref[pl.ds(r0, T), :] = (dqt[...] * Q_SCALE).T.astype(BF16)
            return carry

        lax.fori_loop(0, nq, qblk, 0)
        dk_ref[...] = dka[...].astype(BF16)
        dv_ref[...] = dva[...].astype(BF16)

    return pl.pallas_call(
        body, name="sb_bwd",
        grid=(SB_W // 128,),
        in_specs=[_pair_blk(S, OFF_SQ // 128), _pair_blk(S, OFF_SK // 128), _pair_blk(S, OFF_SV // 128),
                  _pair_blk(S), _pair_rows(S), pl.BlockSpec((T, T), lambda p: (0, 0))],
        out_specs=[_pair_blk(S), _pair_blk(S), _pair_blk(S)],
        out_shape=[jax.ShapeDtypeStruct((S, SB_W), BF16)] * 3,
        scratch_shapes=([pltpu.VMEM((S, 128), BF16)] * 3 + [pltpu.VMEM((128, S), BF16)] * 2
                        + [pltpu.VMEM((S, 128), BF16)] * 3 + [pltpu.VMEM((S, 128), F32)] * 2
                        + [pltpu.VMEM((128, T), F32)] + [pltpu.VMEM((8, T), F32)] * 4),
        compiler_params=_cp(("arbitrary",), _VMEM_BIG),
    )(proj, proj, proj, dso, ltot, tril)


def _head_norm_bwd(x, g, dy, bd):
    ss = _mm2(x * x, bd)
    r = lax.rsqrt(ss * (1.0 / HEAD_DIM) + EPS)
    xr = x * r
    gdy = g * dy
    m = _mm2(xr * gdy, bd) * (1.0 / HEAD_DIM)
    return r * (gdy - xr * m), dy * xr


def _qk_bwd(dqs, dkn, proj, pff, bfp, gq, gk, bd, dccol, triu):
    S = proj.shape[0]
    T = triu.shape[0]
    n = S // T
    rev = lambda col: (lambda i: (n - 1 - i, col))

    def body(dq_ref, dk_ref, q_ref, k_ref, ff_ref, b_ref, gq_ref, gk_ref, bd_ref, dc_ref, tri_ref,
             dfq_ref, dfk_ref, dff_ref, dgq_ref, dgk_ref, dbf_ref, carry):
        @pl.when(pl.program_id(0) == 0)
        def _():
            carry[...] = jnp.zeros_like(carry)
            dgq_ref[...] = jnp.zeros_like(dgq_ref)
            dgk_ref[...] = jnp.zeros_like(dgk_ref)
            dbf_ref[...] = jnp.zeros_like(dbf_ref)

        bdv = bd_ref[...]
        dxq, gq_rows = _head_norm_bwd(q_ref[...], gq_ref[...], dq_ref[...] * Q_SCALE, bdv)
        dfq_ref[...] = dxq.astype(BF16)
        dgq_ref[...] = dgq_ref[...] + jnp.sum(gq_rows, axis=0, keepdims=True)
        dxk, gk_rows = _head_norm_bwd(k_ref[...], gk_ref[...], dk_ref[...], bdv)
        dfk_ref[...] = dxk.astype(BF16)
        dgk_ref[...] = dgk_ref[...] + jnp.sum(gk_rows, axis=0, keepdims=True)
        dlf = _mm3(dc_ref[...], tri_ref[...], left=True) + carry[0:1, :]
        carry[0:1, :] = dlf[0:1, :]
        u = ff_ref[...] + b_ref[...]
        lane = lax.broadcasted_iota(jnp.int32, u.shape, 1)
        dff = jnp.where(lane < N_FF, dlf * _sigmoid(-u), 0.0)
        dff_ref[...] = dff.astype(BF16)
        dbf_ref[...] = dbf_ref[...] + jnp.sum(dff, axis=0, keepdims=True)

    return pl.pallas_call(
        body, name="qk_bwd",
        grid=(n,),
        in_specs=[pl.BlockSpec((T, FOX_W), rev(0)), pl.BlockSpec((T, FOX_W), rev(0)),
                  pl.BlockSpec((T, FOX_W), rev(OFF_FQ // FOX_W)), pl.BlockSpec((T, FOX_W), rev(OFF_FK // FOX_W)),
                  pl.BlockSpec((T, N_FFPAD), rev(0)),
                  pl.BlockSpec((1, N_FFPAD), lambda i: (0, 0)),
                  pl.BlockSpec((1, FOX_W), lambda i: (0, 0)), pl.BlockSpec((1, FOX_W), lambda i: (0, 0)),
                  pl.BlockSpec((FOX_W, FOX_W), lambda i: (0, 0)),
                  pl.BlockSpec((T, N_FFPAD), rev(0)),
                  pl.BlockSpec((T, T), lambda i: (0, 0))],
        out_specs=[pl.BlockSpec((T, FOX_W), rev(0)), pl.BlockSpec((T, FOX_W), rev(0)),
                   pl.BlockSpec((T, N_FFPAD), rev(0)),
                   pl.BlockSpec((1, FOX_W), lambda i: (0, 0)), pl.BlockSpec((1, FOX_W), lambda i: (0, 0)),
                   pl.BlockSpec((1, N_FFPAD), lambda i: (0, 0))],
        out_shape=[jax.ShapeDtypeStruct((S, FOX_W), BF16), jax.ShapeDtypeStruct((S, FOX_W), BF16),
                   jax.ShapeDtypeStruct((S, N_FFPAD), BF16),
                   jax.ShapeDtypeStruct((1, FOX_W), F32), jax.ShapeDtypeStruct((1, FOX_W), F32),
                   jax.ShapeDtypeStruct((1, N_FFPAD), F32)],
        scratch_shapes=[pltpu.VMEM((8, N_FFPAD), F32)],
        compiler_params=_cp(("arbitrary",), 40 << 20),
    )(dqs, dkn, proj, proj, pff, bfp, gq, gk, bd, dccol, triu)


def _inproj_bwd_dx(dpm, dff, wm, wff, x, g, dy):
    S, D = x.shape
    tm = min(256, S)

    def body(dp_ref, dff_ref, w_ref, wff_ref, x_ref, g_ref, dy_ref, dx_ref, dg_ref):
        @pl.when(pl.program_id(0) == 0)
        def _():
            dg_ref[...] = jnp.zeros_like(dg_ref)

        dh = _dot_nt(dp_ref[...], w_ref[...]) + _dot_nt(dff_ref[...], wff_ref[...])
        xv = x_ref[...]
        r = _rms_rows(xv)
        xr = xv * r
        dg_ref[...] = dg_ref[...] + jnp.sum(dh * xr, axis=0, keepdims=True)
        gdh = g_ref[...] * dh
        m = jnp.mean(gdh * xr, axis=-1, keepdims=True)
        dx_ref[...] = dy_ref[...] + r * (gdh - xr * m)

    return pl.pallas_call(
        body, name="inproj_bwd_dx",
        grid=(S // tm,),
        in_specs=[pl.BlockSpec((tm, N_MAIN), lambda i: (i, 0)),
                  pl.BlockSpec((tm, N_FFPAD), lambda i: (i, 0)),
                  pl.BlockSpec((D, N_MAIN), lambda i: (0, 0)),
                  pl.BlockSpec((D, N_FFPAD), lambda i: (0, 0)),
                  pl.BlockSpec((tm, D), lambda i: (i, 0)),
                  pl.BlockSpec((1, D), lambda i: (0, 0)),
                  pl.BlockSpec((tm, D), lambda i: (i, 0))],
        out_specs=[pl.BlockSpec((tm, D), lambda i: (i, 0)), pl.BlockSpec((1, D), lambda i: (0, 0))],
        out_shape=[jax.ShapeDtypeStruct((S, D), F32), jax.ShapeDtypeStruct((1, D), F32)],
        compiler_params=_cp(("arbitrary",), 48 << 20),
    )(dpm, dff, wm, wff, x, g, dy)


def _inproj_bwd_dw(x, g, dpm, dff):
    S, D = x.shape
    tk = min(_TM, S)
    tn = 512

    def body(x_ref, g_ref, dp_ref, dff_ref, dw_ref, dwff_ref):
        j, k = pl.program_id(0), pl.program_id(1)

        @pl.when(k == 0)
        def _():
            dw_ref[...] = jnp.zeros_like(dw_ref)

        @pl.when((k == 0) & (j == 0))
        def _():
            dwff_ref[...] = jnp.zeros_like(dwff_ref)

        xv = x_ref[...]
        h = ((xv * _rms_rows(xv)) * g_ref[...]).astype(BF16)
        dw_ref[...] = dw_ref[...] + _dot_tn(h, dp_ref[...])

        @pl.when(j == 0)
        def _():
            dwff_ref[...] = dwff_ref[...] + _dot_tn(h, dff_ref[...])

    return pl.pallas_call(
        body, name="inproj_bwd_dw",
        grid=(N_MAIN // tn, S // tk),
        in_specs=[pl.BlockSpec((tk, D), lambda j, k: (k, 0)),
                  pl.BlockSpec((1, D), lambda j, k: (0, 0)),
                  pl.BlockSpec((tk, tn), lambda j, k: (k, j)),
                  pl.BlockSpec((tk, N_FFPAD), lambda j, k: (k, 0))],
        out_specs=[pl.BlockSpec((D, tn), lambda j, k: (0, j)), pl.BlockSpec((D, N_FFPAD), lambda j, k: (0, 0))],
        out_shape=[jax.ShapeDtypeStruct((D, N_MAIN), F32), jax.ShapeDtypeStruct((D, N_FFPAD), F32)],
        compiler_params=_cp(("arbitrary", "arbitrary"), 40 << 20),
    )(x, g, dpm, dff)


def _constants(T):
    tril = jnp.tril(jnp.ones((T, T), F32)).astype(BF16)
    hid = jnp.arange(FOX_W) // HEAD_DIM
    bd = (hid[:, None] == hid[None, :]).astype(BF16)
    ex = (jnp.arange(N_FFPAD)[:, None] == hid[None, :]).astype(BF16)
    return tril, tril.T, bd, ex


def _crow4(ccol):
    S = ccol.shape[0]
    c = ccol[:, :FOX_HEADS].T.reshape(FOX_HEADS // 2, 2, S)
    return jnp.pad(c, ((0, 0), (0, 6), (0, 0)))


def _layer_fwd(x, lw, consts):
    tril, triu, bd, ex = consts
    proj, pff = _inproj_fwd(x, lw["g"], lw["wm"], lw["wff"])
    qs, kn, ccol, cqb = _fox_prep(proj, pff, lw["bfp"], lw["gq"], lw["gk"], bd, ex, tril)
    crow4 = _crow4(ccol)
    fo, lse = _fox_fwd(qs, kn, proj, cqb, crow4)
    so, ltot = _sb_fwd(proj, triu)
    pooled = _pool_fwd(proj)
    y, mixed = _mix_out(fo, so, pooled, proj, lw["wbd"], lw["scale"], lw["wout"], x)
    return y, (x, proj, pff, qs, kn, cqb, crow4, fo, lse, so, ltot, pooled, mixed)


def _layer_bwd(dy, saved, lw, consts):
    tril, triu, bd, _ = consts
    x, proj, pff, qs, kn, cqb, crow4, fo, lse, so, ltot, pooled, mixed = saved
    S = x.shape[0]
    dfo, dfg, dso, dsg, dpg, dpooled, dscale, dwbd = _gate_bwd(dy, lw["wout"], fo, so, pooled, proj, lw["wbd"], lw["scale"])
    dwout = _matmul_tn(mixed, dy, "dw_out")
    dpx = _pool_bwd(dpooled)
    dqs, dkn, dfv, dck, dcq4 = _fox_bwd(qs, kn, proj, dfo, fo, lse, cqb, crow4)
    dsq, dsk, dsv = _sb_bwd(proj, dso, ltot, tril)
    dc8 = dck[:, ::HEAD_DIM] + dcq4[:, :2, :].reshape(FOX_HEADS, S).T
    dccol = jnp.pad(dc8, ((0, 0), (0, N_FFPAD - FOX_HEADS)))
    dfq, dfk, dff, dgq, dgk, dbf = _qk_bwd(dqs, dkn, proj, pff, lw["bfp"], lw["gq"], lw["gk"], bd, dccol, triu)
    dpm = jnp.concatenate([dfq, dfk, dfv, dfg, dpx, dpg, dsq, dsk, dsv, dsg], axis=1)
    dx, dng = _inproj_bwd_dx(dpm, dff, lw["wm"], lw["wff"], x, lw["g"], dy)
    dwm, dwff = _inproj_bwd_dw(x, lw["g"], dpm, dff)
    dwin = jnp.concatenate([dwm[:, :OFF_PX], dwff[:, :N_FF], dwm[:, OFF_PX:]], axis=1)
    grads = {
        "norm_g": dng[0],
        "w_in": dwin,
        "b_f": dbf[0, :N_FF],
        "q_norm_g": dgq[0].reshape(FOX_HEADS, HEAD_DIM).sum(0),
        "k_norm_g": dgk[0].reshape(FOX_HEADS, HEAD_DIM).sum(0),
        "w_pool": jnp.stack([dwbd[64 * i:64 * i + 64, 64 * i:64 * i + 64] for i in range(4)]),
        "pool_scale": dscale[0],
        "w_out": dwout,
    }
    return dx, grads


def _layer_weights(l, norm_g, win_full, b_f, q_norm_g, k_norm_g, w_pool, pool_scale, wout_full):
    w = win_full[l]
    D = w.shape[0]
    wm = jnp.concatenate([w[:, :2048], w[:, 2048 + N_FF:]], axis=1)
    wff = jnp.pad(w[:, 2048:2048 + N_FF], ((0, 0), (0, N_FFPAD - N_FF)))
    wbd = jnp.zeros((POOL_W, POOL_W), F32)
    for i in range(4):
        wbd = wbd.at[64 * i:64 * i + 64, 64 * i:64 * i + 64].set(w_pool[l, i])
    return {
        "g": norm_g[l].reshape(1, D),
        "wm": wm, "wff": wff,
        "bfp": jnp.pad(b_f[l], (0, N_FFPAD - N_FF)).reshape(1, N_FFPAD),
        "gq": jnp.tile(q_norm_g[l], FOX_HEADS).reshape(1, FOX_W),
        "gk": jnp.tile(k_norm_g[l], FOX_HEADS).reshape(1, FOX_W),
        "wbd": wbd.astype(BF16),
        "scale": pool_scale[l].reshape(1, POOL_W),
        "wout": wout_full[l],
    }


def _local_step(x, target, norm_g, win_full, b_f, q_norm_g, k_norm_g, w_pool, pool_scale, wout_full):
    L = norm_g.shape[0]
    T = min(_T, x.shape[0])
    consts = _constants(T)
    lws = [_layer_weights(l, norm_g, win_full, b_f, q_norm_g, k_norm_g, w_pool, pool_scale, wout_full)
           for l in range(L)]
    saved = []
    h = x
    for l in range(L):
        h, sv = _layer_fwd(h, lws[l], consts)
        saved.append(sv)
    dy, loss = _loss_head(h, target)
    grads = [None] * L
    for l in reversed(range(L)):
        dy, grads[l] = _layer_bwd(dy, saved[l], lws[l], consts)
    stacked = {k: jnp.stack([grads[l][k] for l in range(L)]) for k in grads[0]}
    return loss, dy, stacked


def _mesh_pos():
    return lax.axis_index("x"), lax.axis_index("y"), lax.axis_index("c")


_FLIPS = [(0, 0, 1), (1, 0, 0), (0, 1, 0), (1, 1, 0), (1, 0, 1), (0, 1, 1), (1, 1, 1)]


def _peers():
    x, y, c = _mesh_pos()
    out = []
    for fx, fy, fc in _FLIPS:
        px = 1 - x if fx else x
        py = 1 - y if fy else y
        pc = 1 - c if fc else c
        out.append(((px, py, pc), 4 * px + 2 * py + pc))
    return out, 4 * x + 2 * y + c


def _all_gather_pair(a, b):
    ANY = pl.BlockSpec(memory_space=pl.ANY)

    def body(a_ref, b_ref, ga_ref, gb_ref, send_sems, recv_sems, loc_sems):
        peers, me = _peers()
        la = pltpu.make_async_copy(a_ref, ga_ref.at[me], loc_sems.at[0])
        lb = pltpu.make_async_copy(b_ref, gb_ref.at[me], loc_sems.at[1])
        la.start()
        lb.start()
        copies = []
        for k, (dev, _) in enumerate(peers):
            for t, (src, dst) in enumerate(((a_ref, ga_ref), (b_ref, gb_ref))):
                cp = pltpu.make_async_remote_copy(
                    src_ref=src, dst_ref=dst.at[me], send_sem=send_sems.at[2 * k + t],
                    recv_sem=recv_sems.at[2 * k + t], device_id=dev, device_id_type=pl.DeviceIdType.MESH)
                cp.start()
                copies.append(cp)
        for cp in copies:
            cp.wait_recv()
        for cp in copies:
            cp.wait_send()
        la.wait()
        lb.wait()

    return pl.pallas_call(
        body, name="gather_weights",
        in_specs=[ANY, ANY], out_specs=[ANY, ANY],
        out_shape=[jax.ShapeDtypeStruct((N_DEV,) + a.shape, a.dtype), jax.ShapeDtypeStruct((N_DEV,) + b.shape, b.dtype)],
        scratch_shapes=[pltpu.SemaphoreType.DMA((14,)), pltpu.SemaphoreType.DMA((14,)), pltpu.SemaphoreType.DMA((2,))],
    )(a, b)


def _all_to_all_pair(a, b):
    ANY = pl.BlockSpec(memory_space=pl.ANY)

    def body(a_ref, b_ref, ra_ref, rb_ref, send_sems, recv_sems, loc_sems):
        peers, me = _peers()
        la = pltpu.make_async_copy(a_ref.at[me], ra_ref.at[me], loc_sems.at[0])
        lb = pltpu.make_async_copy(b_ref.at[me], rb_ref.at[me], loc_sems.at[1])
        la.start()
        lb.start()
        copies = []
        for k, (dev, idx) in enumerate(peers):
            for t, (src, dst) in enumerate(((a_ref, ra_ref), (b_ref, rb_ref))):
                cp = pltpu.make_async_remote_copy(
                    src_ref=src.at[idx], dst_ref=dst.at[me], send_sem=send_sems.at[2 * k + t],
                    recv_sem=recv_sems.at[2 * k + t], device_id=dev, device_id_type=pl.DeviceIdType.MESH)
                cp.start()
                copies.append(cp)
        for cp in copies:
            cp.wait_recv()
        for cp in copies:
            cp.wait_send()
        la.wait()
        lb.wait()

    return pl.pallas_call(
        body, name="exchange_grads",
        in_specs=[ANY, ANY], out_specs=[ANY, ANY],
        out_shape=[jax.ShapeDtypeStruct(a.shape, a.dtype), jax.ShapeDtypeStruct(b.shape, b.dtype)],
        scratch_shapes=[pltpu.SemaphoreType.DMA((14,)), pltpu.SemaphoreType.DMA((14,)), pltpu.SemaphoreType.DMA((2,))],
    )(a, b)


def _adam_math(w, g, m, v):
    m_new = ADAM_B1 * m + (1.0 - ADAM_B1) * g
    v_new = ADAM_B2 * v + (1.0 - ADAM_B2) * (g * g)
    m_hat = m_new / (1.0 - ADAM_B1 ** ADAM_STEP)
    v_hat = v_new / (1.0 - ADAM_B2 ** ADAM_STEP)
    delta = -ADAM_LR * (m_hat / (jnp.sqrt(v_hat) + ADAM_EPS) + ADAM_WD * w)
    return delta, m_new, v_new


def _sum_adamw(gparts, w, m, v, name):
    L, R, C = w.shape
    tr = min(128, R)

    def body(gp_ref, w_ref, m_ref, v_ref, g_ref, d_ref, nm_ref, nv_ref):
        g = gp_ref[0]
        for s in range(1, N_DEV):
            g = g + gp_ref[s]
        d, mn, vn = _adam_math(w_ref[...], g, m_ref[...], v_ref[...])
        g_ref[...] = g
        d_ref[...] = d
        nm_ref[...] = mn
        nv_ref[...] = vn

    blk = pl.BlockSpec((None, tr, C), lambda l, r: (l, r, 0))
    return pl.pallas_call(
        body, name=name,
        grid=(L, R // tr),
        in_specs=[pl.BlockSpec((N_DEV, None, tr, C), lambda l, r: (0, l, r, 0)), blk, blk, blk],
        out_specs=[blk, blk, blk, blk],
        out_shape=[jax.ShapeDtypeStruct((L, R, C), F32)] * 4,
        compiler_params=_cp(("parallel", "parallel"), 40 << 20),
    )(gparts, w, m, v)


def _small_update(gpack, wpack, mpack, vpack):
    R = gpack.shape[0]
    VM = pl.BlockSpec(memory_space=pltpu.VMEM)

    def body(g_ref, w_ref, m_ref, v_ref, gs_ref, d_ref, nm_ref, nv_ref, buf, send_sems, recv_sems):
        peers, me = _peers()
        buf[me] = g_ref[...]
        copies = []
        for k, (dev, _) in enumerate(peers):
            cp = pltpu.make_async_remote_copy(
                src_ref=g_ref, dst_ref=buf.at[me], send_sem=send_sems.at[k], recv_sem=recv_sems.at[k],
                device_id=dev, device_id_type=pl.DeviceIdType.MESH)
            cp.start()
            copies.append(cp)
        for cp in copies:
            cp.wait_recv()
        for cp in copies:
            cp.wait_send()
        g = buf[0]
        for s in range(1, N_DEV):
            g = g + buf[s]
        d, mn, vn = _adam_math(w_ref[...], g, m_ref[...], v_ref[...])
        gs_ref[...] = g
        d_ref[...] = d
        nm_ref[...] = mn
        nv_ref[...] = vn

    return pl.pallas_call(
        body, name="small_update",
        in_specs=[VM] * 4, out_specs=[VM] * 4,
        out_shape=[jax.ShapeDtypeStruct((R, 128), F32)] * 4,
        scratch_shapes=[pltpu.VMEM((N_DEV, R, 128), F32), pltpu.SemaphoreType.DMA((7,)), pltpu.SemaphoreType.DMA((7,))],
        compiler_params=_cp(None, 40 << 20),
    )(gpack, wpack, mpack, vpack)


_SMALL = ("norm_g", "b_f", "q_norm_g", "k_norm_g", "w_pool", "pool_scale")


def _pack(parts):
    flat = jnp.concatenate([p.reshape(-1) for p in parts])
    n = flat.shape[0]
    rows = -(-n // (8 * 128)) * 8
    return jnp.pad(flat, (0, rows * 128 - n)).reshape(rows, 128)


def _unpack(packed, like):
    flat = packed.reshape(-1)
    out, o = [], 0
    for p in like:
        out.append(flat[o:o + p.size].reshape(p.shape))
        o += p.size
    return out


def kernel(x, norm_g, w_in, b_f, q_norm_g, k_norm_g, w_pool, pool_scale, w_out, loss_target, m_norm_g, m_w_in, m_b_f, m_q_norm_g, m_k_norm_g, m_w_pool, m_pool_scale, m_w_out, v_norm_g, v_w_in, v_b_f, v_q_norm_g, v_k_norm_g, v_w_pool, v_pool_scale, v_w_out):
    L, D, cin = w_in.shape
    rout = w_out.shape[1]

    gin, gout = _all_gather_pair(w_in.astype(BF16), w_out.astype(BF16))
    win_full = gin.transpose(1, 2, 0, 3).reshape(L, D, N_DEV * cin)
    wout_full = gout.transpose(1, 0, 2, 3).reshape(L, N_DEV * rout, D)

    loss_local, dx, g = _local_step(x[0], loss_target[0], norm_g, win_full, b_f, q_norm_g, k_norm_g,
                                    w_pool, pool_scale, wout_full)
    loss = lax.psum(loss_local, MESH_AXES)

    gin_parts = g["w_in"].reshape(L, D, N_DEV, cin).transpose(2, 0, 1, 3)
    gout_parts = g["w_out"].reshape(L, N_DEV, rout, D).transpose(1, 0, 2, 3)
    rin, rout_parts = _all_to_all_pair(gin_parts, gout_parts)
    g_win, d_win, nm_win, nv_win = _sum_adamw(rin, w_in, m_w_in, v_w_in, "adamw_w_in")
    g_wout, d_wout, nm_wout, nv_wout = _sum_adamw(rout_parts, w_out, m_w_out, v_w_out, "adamw_w_out")

    ws = dict(norm_g=norm_g, b_f=b_f, q_norm_g=q_norm_g, k_norm_g=k_norm_g, w_pool=w_pool, pool_scale=pool_scale)
    ms = dict(norm_g=m_norm_g, b_f=m_b_f, q_norm_g=m_q_norm_g, k_norm_g=m_k_norm_g, w_pool=m_w_pool, pool_scale=m_pool_scale)
    vs = dict(norm_g=v_norm_g, b_f=v_b_f, q_norm_g=v_q_norm_g, k_norm_g=v_k_norm_g, w_pool=v_w_pool, pool_scale=v_pool_scale)
    like = [ws[k] for k in _SMALL]
    gs_p, d_p, nm_p, nv_p = _small_update(_pack([g[k] for k in _SMALL]), _pack(like),
                                          _pack([ms[k] for k in _SMALL]), _pack([vs[k] for k in _SMALL]))
    gs = dict(zip(_SMALL, _unpack(gs_p, like)))
    ds = dict(zip(_SMALL, _unpack(d_p, like)))
    nms = dict(zip(_SMALL, _unpack(nm_p, like)))
    nvs = dict(zip(_SMALL, _unpack(nv_p, like)))
    gs["w_in"], ds["w_in"], nms["w_in"], nvs["w_in"] = g_win, d_win, nm_win, nv_win
    gs["w_out"], ds["w_out"], nms["w_out"], nvs["w_out"] = g_wout, d_wout, nm_wout, nv_wout

    order = ("norm_g", "w_in", "b_f", "q_norm_g", "k_norm_g", "w_pool", "pool_scale", "w_out")
    return (loss, dx[None], *[gs[k] for k in order], *[ds[k] for k in order],
            *[nms[k] for k in order], *[nvs[k] for k in order])
```

```python
import functools

import jax
import jax.numpy as jnp
from jax import lax
from jax.experimental import pallas as pl
from jax.experimental.pallas import tpu as pltpu

F32 = jnp.float32
BF16 = jnp.bfloat16

EPS = 1e-6
NEG = -1e30
HEAD_DIM = 64
FOX_HEADS = 8
FOX_W = 512
POOL_W = 256
SB_W = 256
D_MIX = 1024
N_FF = 8
N_MAIN = 3584
N_FFPAD = 128
OFF_FQ, OFF_FK, OFF_FV, OFF_FG = 0, 512, 1024, 1536
OFF_PX, OFF_PG = 2048, 2304
OFF_SQ, OFF_SK, OFF_SV, OFF_SG = 2560, 2816, 3072, 3328
D_IN = 3592
Q_SCALE = HEAD_DIM ** -0.5

ADAM_LR = 0.001
ADAM_B1 = 0.9
ADAM_B2 = 0.999
ADAM_EPS = 1e-08
ADAM_WD = 0.01
ADAM_STEP = 10

N_DEV = 8
MESH_AXES = ("x", "y", "c")

_T = 256
_TM = 512
_VMEM_BIG = 56 << 20


def _cp(sem=None, vmem=None):
    kw = {}
    if sem is not None:
        kw["dimension_semantics"] = sem
    if vmem is not None:
        kw["vmem_limit_bytes"] = vmem
    return pltpu.CompilerParams(**kw)


def _dot(a, b):
    return jnp.dot(a, b, preferred_element_type=F32)


def _dot_nt(a, b):
    return lax.dot_general(a, b, (((1,), (1,)), ((), ())), preferred_element_type=F32)


def _dot_tn(a, b):
    return lax.dot_general(a, b, (((0,), (0,)), ((), ())), preferred_element_type=F32)


def _mm2(v, m, left=False):
    hi = v.astype(BF16)
    lo = (v - hi.astype(F32)).astype(BF16)
    if left:
        return _dot(m, hi) + _dot(m, lo)
    return _dot(hi, m) + _dot(lo, m)


def _mm3(v, m, left=False):
    a1 = v.astype(BF16)
    r1 = v - a1.astype(F32)
    a2 = r1.astype(BF16)
    a3 = (r1 - a2.astype(F32)).astype(BF16)
    if left:
        return _dot(m, a1) + _dot(m, a2) + _dot(m, a3)
    return _dot(a1, m) + _dot(a2, m) + _dot(a3, m)


def _sigmoid(z):
    return 1.0 / (1.0 + jnp.exp(-z))


def _rms_rows(x):
    return lax.rsqrt(jnp.mean(x * x, axis=-1, keepdims=True) + EPS)


def _inproj_fwd(x, g, wm, wff):
    S, D = x.shape
    tm = min(_TM, S)
    tn = 512

    def body(x_ref, g_ref, w_ref, wff_ref, o_ref, off_ref, h_ref):
        @pl.when(pl.program_id(1) == 0)
        def _():
            xv = x_ref[...]
            h = (xv * _rms_rows(xv)) * g_ref[...]
            h_ref[...] = h.astype(BF16)
            off_ref[...] = _dot(h_ref[...], wff_ref[...])

        o_ref[...] = _dot(h_ref[...], w_ref[...])

    return pl.pallas_call(
        body, name="inproj_fwd",
        grid=(S // tm, N_MAIN // tn),
        in_specs=[pl.BlockSpec((tm, D), lambda i, j: (i, 0)),
                  pl.BlockSpec((1, D), lambda i, j: (0, 0)),
                  pl.BlockSpec((D, tn), lambda i, j: (0, j)),
                  pl.BlockSpec((D, N_FFPAD), lambda i, j: (0, 0))],
        out_specs=[pl.BlockSpec((tm, tn), lambda i, j: (i, j)),
                   pl.BlockSpec((tm, N_FFPAD), lambda i, j: (i, 0))],
        out_shape=[jax.ShapeDtypeStruct((S, N_MAIN), F32), jax.ShapeDtypeStruct((S, N_FFPAD), F32)],
        scratch_shapes=[pltpu.VMEM((tm, D), BF16)],
        compiler_params=_cp(("parallel", "arbitrary"), 40 << 20),
    )(x, g, wm, wff)


def _head_norm(x, g, bd):
    ss = _mm2(x * x, bd)
    r = lax.rsqrt(ss * (1.0 / HEAD_DIM) + EPS)
    return (x * r) * g


def _fox_prep(proj, pff, bfp, gq, gk, bd, ex, tril):
    S = proj.shape[0]
    T = tril.shape[0]

    def body(q_ref, k_ref, ff_ref, b_ref, gq_ref, gk_ref, bd_ref, ex_ref, tri_ref,
             qs_ref, kn_ref, cc_ref, cqb_ref, carry):
        @pl.when(pl.program_id(0) == 0)
        def _():
            carry[...] = jnp.zeros_like(carry)

        bdv = bd_ref[...]
        qs_ref[...] = (_head_norm(q_ref[...], gq_ref[...], bdv) * Q_SCALE).astype(BF16)
        kn_ref[...] = _head_norm(k_ref[...], gk_ref[...], bdv).astype(BF16)
        u = ff_ref[...] + b_ref[...]
        lf = jnp.minimum(u, 0.0) - jnp.log1p(jnp.exp(-jnp.abs(u)))
        c = _mm3(lf, tri_ref[...], left=True) + carry[0:1, :]
        carry[0:1, :] = c[T - 1:T, :]
        cc_ref[...] = c
        cqb_ref[...] = _mm3(c, ex_ref[...])

    return pl.pallas_call(
        body, name="fox_prep",
        grid=(S // T,),
        in_specs=[pl.BlockSpec((T, FOX_W), lambda i: (i, OFF_FQ // FOX_W)),
                  pl.BlockSpec((T, FOX_W), lambda i: (i, OFF_FK // FOX_W)),
                  pl.BlockSpec((T, N_FFPAD), lambda i: (i, 0)),
                  pl.BlockSpec((1, N_FFPAD), lambda i: (0, 0)),
                  pl.BlockSpec((1, FOX_W), lambda i: (0, 0)),
                  pl.BlockSpec((1, FOX_W), lambda i: (0, 0)),
                  pl.BlockSpec((FOX_W, FOX_W), lambda i: (0, 0)),
                  pl.BlockSpec((N_FFPAD, FOX_W), lambda i: (0, 0)),
                  pl.BlockSpec((T, T), lambda i: (0, 0))],
        out_specs=[pl.BlockSpec((T, FOX_W), lambda i: (i, 0)),
                   pl.BlockSpec((T, FOX_W), lambda i: (i, 0)),
                   pl.BlockSpec((T, N_FFPAD), lambda i: (i, 0)),
                   pl.BlockSpec((T, FOX_W), lambda i: (i, 0))],
        out_shape=[jax.ShapeDtypeStruct((S, FOX_W), BF16), jax.ShapeDtypeStruct((S, FOX_W), BF16),
                   jax.ShapeDtypeStruct((S, N_FFPAD), F32), jax.ShapeDtypeStruct((S, FOX_W), F32)],
        scratch_shapes=[pltpu.VMEM((8, N_FFPAD), F32)],
        compiler_params=_cp(("arbitrary",), 40 << 20),
    )(proj, proj, pff, bfp, gq, gk, bd, ex, tril)


def _pair_blk(S, off=0):
    return pl.BlockSpec((S, 128), lambda p: (0, off + p), pipeline_mode=pl.Buffered(1))


def _pair_rows(S):
    return pl.BlockSpec((None, 8, S), lambda p: (p, 0, 0), pipeline_mode=pl.Buffered(1))


def _head_masks(S):
    return lax.broadcasted_iota(jnp.int32, (S, 128), 1) < HEAD_DIM


_EXP_ZERO = 104.0


def _score_bounds(qa, qb, k, lane_s):
    def max_norm2(x2):
        return jnp.max(jnp.sum(x2, axis=1, keepdims=True), axis=0, keepdims=True)

    qaf = qa.astype(F32)
    qbf = qb.astype(F32)
    k2 = k.astype(F32)
    k2 = k2 * k2
    za = jnp.sqrt(max_norm2(qaf * qaf) * max_norm2(jnp.where(lane_s, k2, 0.0)))
    zb = jnp.sqrt(max_norm2(qbf * qbf) * max_norm2(jnp.where(lane_s, 0.0, k2)))
    return jnp.max(za) * 1.001 + 1e-3, jnp.max(zb) * 1.001 + 1e-3


def _fox_tiles_back(cr_ref, i, r0, T, zba, zbb, tile_fn):
    cf = cr_ref[:, pl.ds(r0, 128)]
    cfa = jnp.max(cf[0:1, 0:1])
    cfb = jnp.max(cf[1:2, 0:1])

    def alive(j):
        cl = cr_ref[:, pl.ds(pl.multiple_of(j * T + (T - 128), 128), 128)]
        gap_a = cfa - jnp.max(cl[0:1, 127:128])
        gap_b = cfb - jnp.max(cl[1:2, 127:128])
        return jnp.maximum(2.0 * zba + gap_a, 2.0 * zbb + gap_b) > -_EXP_ZERO

    def cond(st):
        return (st[0] < i) & st[1]

    def step(st):
        tile_fn(i - 1 - st[0])
        return st[0] + 1, alive(jnp.maximum(i - 2 - st[0], 0))

    done, _ = lax.while_loop(cond, step, (jnp.int32(0), alive(jnp.maximum(i - 1, 0))))
    return done


def _fox_fwd(qs, kn, proj, cqb, crow4, ride=None):
    S = qs.shape[0]
    T = min(_T, S)
    nq = S // T
    n_pairs = FOX_W // 128

    def body(*refs):
        if ride is None:
            q_ref, k_ref, v_ref, cq_ref, cr_ref, o_ref, lse_ref = refs[:7]
            qa, qb, vta, vtb, cka, ckb, ma, mb, acca, accb = refs[7:]
        else:
            q_ref, k_ref, v_ref, cq_ref, cr_ref, wa_ref, wb_ref, o_ref, lse_ref, ga_ref, gb_ref = refs[:11]
            qa, qb, vta, vtb, cka, ckb, ma, mb, acca, accb = refs[11:21]
            xrefs = (wa_ref, wb_ref, ga_ref, gb_ref) + tuple(refs[21:])

            @pl.when(pl.program_id(0) == 0)
            def _():
                _start_exchange("gather", *xrefs)

        lane_s = _head_masks(S)
        q = q_ref[...]
        zq = jnp.zeros_like(q)
        qa[...] = jnp.where(lane_s, q, zq)
        qb[...] = jnp.where(lane_s, zq, q)
        cq = cq_ref[...]
        cka[...] = jnp.broadcast_to(cq[:, 0:1], (S, 128))
        ckb[...] = jnp.broadcast_to(cq[:, 64:65], (S, 128))
        lse_ref[...] = jnp.zeros((8, S), F32)
        row_t = lax.broadcasted_iota(jnp.int32, (128, T), 0) < HEAD_DIM
        zba, zbb = _score_bounds(qa[...], qb[...], k_ref[...], lane_s)

        def prep(c, carry):
            c0 = pl.multiple_of(c * T, T)
            vt = v_ref[pl.ds(c0, T), :].T
            vta[:, pl.ds(c0, T)] = jnp.where(row_t, vt, 1.0).astype(BF16)
            vtb[:, pl.ds(c0, T)] = jnp.where(row_t, 1.0, vt).astype(BF16)
            return carry

        lax.fori_loop(0, nq, prep, 0)
        causal = (lax.broadcasted_iota(jnp.int32, (T, T), 0) <= lax.broadcasted_iota(jnp.int32, (T, T), 1))

        def head(qh, k, vt, cq_row, ck_t, m_ref, acc_ref, masked):
            s = _dot_nt(k, qh) + cq_row - ck_t
            if masked:
                s = jnp.where(causal, s, NEG)
            m_old = m_ref[0:1, :]
            m_new = jnp.maximum(m_old, jnp.max(s, axis=0, keepdims=True))
            p = jnp.exp(s - m_new)
            acc_ref[...] = jnp.exp(m_old - m_new) * acc_ref[...] + _dot(vt, p.astype(BF16))
            m_ref[0:1, :] = m_new

        def kv(j, r0, masked):
            c0 = pl.multiple_of(j * T, T)
            k = k_ref[pl.ds(c0, T), :]
            cr = cr_ref[:, pl.ds(r0, T)]
            head(qa[pl.ds(r0, T), :], k, vta[:, pl.ds(c0, T)], cr[0:1, :],
                 jnp.tile(cka[pl.ds(c0, T), :], (1, T // 128)), ma, acca, masked)
            head(qb[pl.ds(r0, T), :], k, vtb[:, pl.ds(c0, T)], cr[1:2, :],
                 jnp.tile(ckb[pl.ds(c0, T), :], (1, T // 128)), mb, accb, masked)

        def qblk(i, carry):
            r0 = pl.multiple_of(i * T, T)
            ma[...] = jnp.full((8, T), NEG, F32)
            mb[...] = jnp.full((8, T), NEG, F32)
            acca[...] = jnp.zeros((128, T), F32)
            accb[...] = jnp.zeros((128, T), F32)
            kv(i, r0, True)
            done = _fox_tiles_back(cr_ref, i, r0, T, zba, zbb, lambda j: kv(j, r0, False))
            aa = acca[...]
            ab = accb[...]
            la = aa[64:65, :]
            lb = ab[0:1, :]
            o_ref[pl.ds(r0, T), :] = jnp.where(row_t, aa / la, ab / lb).T
            lse_ref[0:1, pl.ds(r0, T)] = ma[0:1, :] + jnp.log(la)
            lse_ref[1:2, pl.ds(r0, T)] = mb[0:1, :] + jnp.log(lb)
            lse_ref[2:3, pl.ds(r0, T)] = jnp.broadcast_to(done.astype(F32), (1, T))
            return carry

        lax.fori_loop(0, nq, qblk, 0)
        if ride is not None:
            @pl.when(pl.program_id(0) == n_pairs - 1)
            def _():
                _wait_exchange("gather", *xrefs)

    extra = () if ride is None else tuple(ride)
    return pl.pallas_call(
        body, name="fox_fwd" if ride is None else "fox_fwd_gather",
        grid=(n_pairs,),
        in_specs=[_pair_blk(S), _pair_blk(S), _pair_blk(S, OFF_FV // 128), _pair_blk(S), _pair_rows(S)]
        + [_ANY] * len(extra),
        out_specs=[_pair_blk(S), _pair_rows(S)] + [_ANY] * len(extra),
        out_shape=[jax.ShapeDtypeStruct((S, FOX_W), F32), jax.ShapeDtypeStruct((n_pairs, 8, S), F32)]
        + (_exchange_out_shapes("gather", *extra) if extra else []),
        scratch_shapes=[pltpu.VMEM((S, 128), BF16)] * 2 + [pltpu.VMEM((128, S), BF16)] * 2
        + [pltpu.VMEM((S, 128), F32)] * 2 + [pltpu.VMEM((8, T), F32)] * 2 + [pltpu.VMEM((128, T), F32)] * 2
        + (_EXCHANGE_SEMS if extra else []),
        compiler_params=_cp(("arbitrary",), _VMEM_BIG),
    )(qs, kn, proj, cqb, crow4, *extra)


def _softplus_parts(z):
    e = jnp.exp(-jnp.abs(z))
    return e, jnp.maximum(z, 0.0) + jnp.log1p(e)


def _sb_fwd(proj, triu):
    S = proj.shape[0]
    T = triu.shape[0]
    nq = S // T

    def body(q_ref, k_ref, v_ref, tri_ref, o_ref, lt_ref, qa, qb, kb, vt, ra, rb, acca, accb):
        lane_s = _head_masks(S)
        q = (q_ref[...] * Q_SCALE).astype(BF16)
        zq = jnp.zeros_like(q)
        qa[...] = jnp.where(lane_s, q, zq)
        qb[...] = jnp.where(lane_s, zq, q)
        kb[...] = k_ref[...].astype(BF16)
        lt_ref[...] = jnp.zeros((8, S), F32)
        row_t = lax.broadcasted_iota(jnp.int32, (128, T), 0) < HEAD_DIM
        zba, zbb = _score_bounds(qa[...], qb[...], kb[...], lane_s)

        def prep(c, carry):
            c0 = pl.multiple_of(c * T, T)
            vt[:, pl.ds(c0, T)] = v_ref[pl.ds(c0, T), :].T.astype(BF16)
            return carry

        lax.fori_loop(0, nq, prep, 0)
        strict = (lax.broadcasted_iota(jnp.int32, (T, T), 0) < lax.broadcasted_iota(jnp.int32, (T, T), 1))

        def head(qh, k, vtt, r_ref, acc_ref, masked):
            z = _dot_nt(k, qh)
            _, sp = _softplus_parts(z)
            lb = -sp
            if masked:
                lb = jnp.where(strict, lb, 0.0)
            inc = _mm2(lb, tri_ref[...], left=True)
            r = r_ref[0:1, :]
            a = jnp.exp(z + inc + r)
            if masked:
                a = jnp.where(strict, a, 0.0)
            r_ref[0:1, :] = r + inc[0:1, :]
            acc_ref[...] = acc_ref[...] + _dot(vtt, a.astype(BF16))

        def kv(j, r0, masked):
            c0 = pl.multiple_of(j * T, T)
            k = kb[pl.ds(c0, T), :]
            vtt = vt[:, pl.ds(c0, T)]
            head(qa[pl.ds(r0, T), :], k, vtt, ra, acca, masked)
            head(qb[pl.ds(r0, T), :], k, vtt, rb, accb, masked)

        def qblk(i, carry):
            r0 = pl.multiple_of(i * T, T)
            ra[...] = jnp.zeros((8, T), F32)
            rb[...] = jnp.zeros((8, T), F32)
            acca[...] = jnp.zeros((128, T), F32)
            accb[...] = jnp.zeros((128, T), F32)
            kv(i, r0, True)

            def alive():
                return jnp.maximum(jnp.max(ra[0:1, :]) + zba, jnp.max(rb[0:1, :]) + zbb) > -_EXP_ZERO

            def cond(st):
                return (st[0] < i) & st[1]

            def step(st):
                kv(i - 1 - st[0], r0, False)
                return st[0] + 1, alive()

            done, _ = lax.while_loop(cond, step, (jnp.int32(0), alive()))
            o_ref[pl.ds(r0, T), :] = jnp.where(row_t, acca[...], accb[...]).T
            lt_ref[0:1, pl.ds(r0, T)] = ra[0:1, :]
            lt_ref[1:2, pl.ds(r0, T)] = rb[0:1, :]
            lt_ref[2:3, pl.ds(r0, T)] = jnp.broadcast_to(done.astype(F32), (1, T))
            return carry

        lax.fori_loop(0, nq, qblk, 0)

    return pl.pallas_call(
        body, name="sb_fwd",
        grid=(SB_W // 128,),
        in_specs=[_pair_blk(S, OFF_SQ // 128), _pair_blk(S, OFF_SK // 128), _pair_blk(S, OFF_SV // 128),
                  pl.BlockSpec((T, T), lambda p: (0, 0))],
        out_specs=[_pair_blk(S), _pair_rows(S)],
        out_shape=[jax.ShapeDtypeStruct((S, SB_W), F32), jax.ShapeDtypeStruct((SB_W // 128, 8, S), F32)],
        scratch_shapes=[pltpu.VMEM((S, 128), BF16)] * 3 + [pltpu.VMEM((128, S), BF16)]
        + [pltpu.VMEM((8, T), F32)] * 2 + [pltpu.VMEM((128, T), F32)] * 2,
        compiler_params=_cp(("arbitrary",), _VMEM_BIG),
    )(proj, proj, proj, triu)


def _pool_window_lanes(shape):
    lane = lax.broadcasted_iota(jnp.int32, shape, 1)
    return jnp.where(lane < 64, 2, jnp.where(lane < 128, 4, jnp.where(lane < 192, 8, 16)))


def _pool_fwd(proj):
    S = proj.shape[0]

    def body(x_ref, o_ref):
        x = x_ref[...]
        t = lax.broadcasted_iota(jnp.int32, x.shape, 0)
        lane = lax.broadcasted_iota(jnp.int32, x.shape, 1)

        def back(a, k):
            return jnp.where(t >= k, pltpu.roll(a, k, 0), 0.0)

        s1 = x + back(x, 1)
        s2 = s1 + back(s1, 2)
        s4 = s2 + back(s2, 4)
        s8 = s4 + back(s4, 8)
        win = jnp.where(lane < 64, s1, jnp.where(lane < 128, s2, jnp.where(lane < 192, s4, s8)))
        cnt = jnp.minimum(t + 1, _pool_window_lanes(x.shape)).astype(F32)
        o_ref[...] = win / cnt - x

    return pl.pallas_call(
        body, name="pool_fwd",
        grid=(1,),
        in_specs=[pl.BlockSpec((S, POOL_W), lambda i: (0, OFF_PX // POOL_W))],
        out_specs=pl.BlockSpec((S, POOL_W), lambda i: (0, 0)),
        out_shape=jax.ShapeDtypeStruct((S, POOL_W), F32),
        compiler_params=_cp(("arbitrary",), _VMEM_BIG),
    )(proj)


def _silu(g):
    return g * _sigmoid(g)


def _mix_out(fo, so, pooled, proj, wbd, scale, wout, x):
    S, D = x.shape
    tm = min(256, S)

    def body(fo_ref, fg_ref, so_ref, sg_ref, pl_ref, pg_ref, wbd_ref, sc_ref, w_ref, x_ref, y_ref, mx_ref):
        mx_ref[:, 0:FOX_W] = (fo_ref[...] * _silu(fg_ref[...])).astype(BF16)
        yp = _dot(pl_ref[...].astype(BF16), wbd_ref[...]) * sc_ref[...]
        mx_ref[:, FOX_W:FOX_W + POOL_W] = (yp * _silu(pg_ref[...])).astype(BF16)
        mx_ref[:, FOX_W + POOL_W:D_MIX] = (so_ref[...] * _silu(sg_ref[...])).astype(BF16)
        y_ref[...] = x_ref[...] + _dot(mx_ref[...], w_ref[...])

    return pl.pallas_call(
        body, name="mix_out",
        grid=(S // tm,),
        in_specs=[pl.BlockSpec((tm, FOX_W), lambda i: (i, 0)),
                  pl.BlockSpec((tm, FOX_W), lambda i: (i, OFF_FG // FOX_W)),
                  pl.BlockSpec((tm, SB_W), lambda i: (i, 0)),
                  pl.BlockSpec((tm, SB_W), lambda i: (i, OFF_SG // SB_W)),
                  pl.BlockSpec((tm, POOL_W), lambda i: (i, 0)),
                  pl.BlockSpec((tm, POOL_W), lambda i: (i, OFF_PG // POOL_W)),
                  pl.BlockSpec((POOL_W, POOL_W), lambda i: (0, 0)),
                  pl.BlockSpec((1, POOL_W), lambda i: (0, 0)),
                  pl.BlockSpec((D_MIX, D), lambda i: (0, 0)),
                  pl.BlockSpec((tm, D), lambda i: (i, 0))],
        out_specs=[pl.BlockSpec((tm, D), lambda i: (i, 0)), pl.BlockSpec((tm, D_MIX), lambda i: (i, 0))],
        out_shape=[jax.ShapeDtypeStruct((S, D), F32), jax.ShapeDtypeStruct((S, D_MIX), BF16)],
        compiler_params=_cp(("parallel",), 40 << 20),
    )(fo, proj, so, proj, pooled, proj, wbd, scale, wout, x)


def _loss_head(y, target):
    S, D = y.shape
    tm = min(_TM, S)

    def body(y_ref, t_ref, dy_ref, ls_ref):
        @pl.when(pl.program_id(0) == 0)
        def _():
            ls_ref[...] = jnp.zeros_like(ls_ref)

        e = y_ref[...] - t_ref[...]
        dy_ref[...] = e * (1.0 / D)
        ls_ref[...] = ls_ref[...] + jnp.sum(e * e) * (0.5 / D)

    dy, ls = pl.pallas_call(
        body, name="loss_head",
        grid=(S // tm,),
        in_specs=[pl.BlockSpec((tm, D), lambda i: (i, 0)), pl.BlockSpec((tm, D), lambda i: (i, 0))],
        out_specs=[pl.BlockSpec((tm, D), lambda i: (i, 0)), pl.BlockSpec((8, 128), lambda i: (0, 0))],
        out_shape=[jax.ShapeDtypeStruct((S, D), F32), jax.ShapeDtypeStruct((8, 128), F32)],
        compiler_params=_cp(("arbitrary",), 40 << 20),
    )(y, target)
    return dy, ls[0, 0]


def _dsilu(g):
    s = _sigmoid(g)
    return s * (1.0 + g * (1.0 - s))


def _gate_bwd(dy, wout, fo, so, pooled, proj, wbd, scale):
    S, D = dy.shape
    tm = min(256, S)

    def body(dy_ref, w_ref, fo_ref, fg_ref, so_ref, sg_ref, pl_ref, pg_ref, wbd_ref, sc_ref,
             dfo_ref, dfg_ref, dso_ref, dsg_ref, dpg_ref, dpl_ref, dsc_ref, dwbd_ref):
        @pl.when(pl.program_id(0) == 0)
        def _():
            dsc_ref[...] = jnp.zeros_like(dsc_ref)
            dwbd_ref[...] = jnp.zeros_like(dwbd_ref)

        dm = _dot_nt(dy_ref[...].astype(BF16), w_ref[...])
        dmf = dm[:, 0:FOX_W]
        dmp = dm[:, FOX_W:FOX_W + POOL_W]
        dms = dm[:, FOX_W + POOL_W:D_MIX]
        fg = fg_ref[...]
        dfo_ref[...] = dmf * _silu(fg)
        dfg_ref[...] = (dmf * fo_ref[...] * _dsilu(fg)).astype(BF16)
        sg = sg_ref[...]
        dso_ref[...] = dms * _silu(sg)
        dsg_ref[...] = (dms * so_ref[...] * _dsilu(sg)).astype(BF16)
        pg = pg_ref[...]
        plb = pl_ref[...].astype(BF16)
        yw = _dot(plb, wbd_ref[...])
        sc = sc_ref[...]
        dpg_ref[...] = (dmp * (yw * sc) * _dsilu(pg)).astype(BF16)
        dys = dmp * _silu(pg)
        dsc_ref[...] = dsc_ref[...] + jnp.sum(dys * yw, axis=0, keepdims=True)
        dyw = (dys * sc).astype(BF16)
        dpl_ref[...] = _dot_nt(dyw, wbd_ref[...])
        dwbd_ref[...] = dwbd_ref[...] + _dot_tn(plb, dyw)

    return pl.pallas_call(
        body, name="gate_bwd",
        grid=(S // tm,),
        in_specs=[pl.BlockSpec((tm, D), lambda i: (i, 0)),
                  pl.BlockSpec((D_MIX, D), lambda i: (0, 0)),
                  pl.BlockSpec((tm, FOX_W), lambda i: (i, 0)),
                  pl.BlockSpec((tm, FOX_W), lambda i: (i, OFF_FG // FOX_W)),
                  pl.BlockSpec((tm, SB_W), lambda i: (i, 0)),
                  pl.BlockSpec((tm, SB_W), lambda i: (i, OFF_SG // SB_W)),
                  pl.BlockSpec((tm, POOL_W), lambda i: (i, 0)),
                  pl.BlockSpec((tm, POOL_W), lambda i: (i, OFF_PG // POOL_W)),
                  pl.BlockSpec((POOL_W, POOL_W), lambda i: (0, 0)),
                  pl.BlockSpec((1, POOL_W), lambda i: (0, 0))],
        out_specs=[pl.BlockSpec((tm, FOX_W), lambda i: (i, 0)),
                   pl.BlockSpec((tm, FOX_W), lambda i: (i, 0)),
                   pl.BlockSpec((tm, SB_W), lambda i: (i, 0)),
                   pl.BlockSpec((tm, SB_W), lambda i: (i, 0)),
                   pl.BlockSpec((tm, POOL_W), lambda i: (i, 0)),
                   pl.BlockSpec((tm, POOL_W), lambda i: (i, 0)),
                   pl.BlockSpec((1, POOL_W), lambda i: (0, 0)),
                   pl.BlockSpec((POOL_W, POOL_W), lambda i: (0, 0))],
        out_shape=[jax.ShapeDtypeStruct((S, FOX_W), F32), jax.ShapeDtypeStruct((S, FOX_W), BF16),
                   jax.ShapeDtypeStruct((S, SB_W), F32), jax.ShapeDtypeStruct((S, SB_W), BF16),
                   jax.ShapeDtypeStruct((S, POOL_W), BF16), jax.ShapeDtypeStruct((S, POOL_W), F32),
                   jax.ShapeDtypeStruct((1, POOL_W), F32), jax.ShapeDtypeStruct((POOL_W, POOL_W), F32)],
        compiler_params=_cp(("arbitrary",), 40 << 20),
    )(dy, wout, fo, proj, so, proj, pooled, proj, wbd, scale)


def _matmul_tn(a, b, name):
    S, M = a.shape
    N = b.shape[1]
    tk = min(_TM, S)
    tn = min(512, N)

    def body(a_ref, b_ref, o_ref):
        @pl.when(pl.program_id(1) == 0)
        def _():
            o_ref[...] = jnp.zeros_like(o_ref)

        o_ref[...] = o_ref[...] + _dot_tn(a_ref[...].astype(BF16), b_ref[...].astype(BF16))

    return pl.pallas_call(
        body, name=name,
        grid=(N // tn, S // tk),
        in_specs=[pl.BlockSpec((tk, M), lambda j, k: (k, 0)), pl.BlockSpec((tk, tn), lambda j, k: (k, j))],
        out_specs=pl.BlockSpec((M, tn), lambda j, k: (0, j)),
        out_shape=jax.ShapeDtypeStruct((M, N), F32),
        compiler_params=_cp(("parallel", "arbitrary"), 40 << 20),
    )(a, b)


def _pool_bwd(dpooled):
    S = dpooled.shape[0]

    def body(d_ref, o_ref):
        d = d_ref[...]
        t = lax.broadcasted_iota(jnp.int32, d.shape, 0)
        lane = lax.broadcasted_iota(jnp.int32, d.shape, 1)
        cnt = jnp.minimum(t + 1, _pool_window_lanes(d.shape)).astype(F32)
        u = d / cnt

        def fwd(a, k):
            return jnp.where(t < S - k, pltpu.roll(a, S - k, 0), 0.0)

        s1 = u + fwd(u, 1)
        s2 = s1 + fwd(s1, 2)
        s4 = s2 + fwd(s2, 4)
        s8 = s4 + fwd(s4, 8)
        win = jnp.where(lane < 64, s1, jnp.where(lane < 128, s2, jnp.where(lane < 192, s4, s8)))
        o_ref[...] = (win - d).astype(BF16)

    return pl.pallas_call(
        body, name="pool_bwd",
        grid=(1,),
        in_specs=[pl.BlockSpec((S, POOL_W), lambda i: (0, 0))],
        out_specs=pl.BlockSpec((S, POOL_W), lambda i: (0, 0)),
        out_shape=jax.ShapeDtypeStruct((S, POOL_W), BF16),
        compiler_params=_cp(("arbitrary",), _VMEM_BIG),
    )(dpooled)


def _fox_bwd(qs, kn, proj, dfo, fo, lse, cqb, crow4, ride=None):
    S = qs.shape[0]
    T = min(_T, S)
    nq = S // T
    n_pairs = FOX_W // 128

    def body(*refs):
        if ride is None:
            q_ref, k_ref, v_ref, do_ref, o_ref, lse_ref, cq_ref, cr_ref = refs[:8]
            dq_ref, dk_ref, dv_ref, dck_ref, dcq_ref = refs[8:13]
            scr = refs[13:]
        else:
            q_ref, k_ref, v_ref, do_ref, o_ref, lse_ref, cq_ref, cr_ref, pa_ref, pb_ref = refs[:10]
            dq_ref, dk_ref, dv_ref, dck_ref, dcq_ref, ra_ref, rb_ref = refs[10:17]
            scr = refs[17:32]
            xrefs = (pa_ref, pb_ref, ra_ref, rb_ref) + tuple(refs[32:])

            @pl.when(pl.program_id(0) == 0)
            def _():
                _start_exchange("scatter", *xrefs)

        qa, qb, kta, ktb, vb, doa, dob, cka, ckb, dcka, dckb, dva, dqt, dcqa, dcqb = scr
        lane_s = _head_masks(S)
        q = q_ref[...]
        zq = jnp.zeros_like(q)
        qa[...] = jnp.where(lane_s, q, zq)
        qb[...] = jnp.where(lane_s, zq, q)
        vb[...] = v_ref[...].astype(BF16)
        do = do_ref[...].astype(BF16)
        doa[...] = jnp.where(lane_s, do, zq)
        dob[...] = jnp.where(lane_s, zq, do)
        cq = cq_ref[...]
        cka[...] = jnp.broadcast_to(cq[:, 0:1], (S, 128))
        ckb[...] = jnp.broadcast_to(cq[:, 64:65], (S, 128))
        zs = jnp.zeros((S, 128), F32)
        dk_ref[...] = zs
        dva[...] = zs
        dcka[...] = zs
        dckb[...] = zs
        dcq_ref[...] = jnp.zeros((8, S), F32)
        row_t = lax.broadcasted_iota(jnp.int32, (128, T), 0) < HEAD_DIM

        def prep(c, carry):
            c0 = pl.multiple_of(c * T, T)
            kt = k_ref[pl.ds(c0, T), :].astype(F32).T
            kta[:, pl.ds(c0, T)] = jnp.where(row_t, kt, 0.0).astype(BF16)
            ktb[:, pl.ds(c0, T)] = jnp.where(row_t, 0.0, kt).astype(BF16)
            return carry

        lax.fori_loop(0, nq, prep, 0)
        causal = (lax.broadcasted_iota(jnp.int32, (T, T), 0) <= lax.broadcasted_iota(jnp.int32, (T, T), 1))

        def head(qh, kfull, kth, v, doh, cq_row, ck_t, lse_row, dl_row, dck_acc, dcq_acc, c0, masked):
            s = _dot_nt(kfull, qh) + cq_row - ck_t
            if masked:
                s = jnp.where(causal, s, NEG)
            p = jnp.exp(s - lse_row)
            dp = _dot_nt(v, doh)
            ds = p * (dp - dl_row)
            pb = p.astype(BF16)
            dsb = ds.astype(BF16)
            dva[pl.ds(c0, T), :] = dva[pl.ds(c0, T), :] + _dot(pb, doh)
            dk_ref[pl.ds(c0, T), :] = dk_ref[pl.ds(c0, T), :] + _dot(dsb, qh)
            dqt[...] = dqt[...] + _dot(kth, dsb)
            dcq_acc[0:1, :] = dcq_acc[0:1, :] + jnp.sum(ds, axis=0, keepdims=True)
            fold = ds[:, 0:128]
            for t in range(1, T // 128):
                fold = fold + ds[:, 128 * t:128 * (t + 1)]
            dck_acc[pl.ds(c0, T), :] = dck_acc[pl.ds(c0, T), :] - fold

        def kv(j, r0, lsa, lsb, dla, dlb, masked):
            c0 = pl.multiple_of(j * T, T)
            kfull = k_ref[pl.ds(c0, T), :]
            v = vb[pl.ds(c0, T), :]
            cr = cr_ref[:, pl.ds(r0, T)]
            head(qa[pl.ds(r0, T), :], kfull, kta[:, pl.ds(c0, T)], v, doa[pl.ds(r0, T), :], cr[0:1, :],
                 jnp.tile(cka[pl.ds(c0, T), :], (1, T // 128)), lsa, dla, dcka, dcqa, c0, masked)
            head(qb[pl.ds(r0, T), :], kfull, ktb[:, pl.ds(c0, T)], v, dob[pl.ds(r0, T), :], cr[1:2, :],
                 jnp.tile(ckb[pl.ds(c0, T), :], (1, T // 128)), lsb, dlb, dckb, dcqb, c0, masked)

        def qblk(i, carry):
            r0 = pl.multiple_of(i * T, T)
            dt = (do_ref[pl.ds(r0, T), :] * o_ref[pl.ds(r0, T), :]).T
            dla = jnp.sum(jnp.where(row_t, dt, 0.0), axis=0, keepdims=True)
            dlb = jnp.sum(jnp.where(row_t, 0.0, dt), axis=0, keepdims=True)
            ls = lse_ref[:, pl.ds(r0, T)]
            lsa = ls[0:1, :]
            lsb = ls[1:2, :]
            back = jnp.max(ls[2:3, :]).astype(jnp.int32)
            dqt[...] = jnp.zeros((128, T), F32)
            dcqa[...] = jnp.zeros((8, T), F32)
            dcqb[...] = jnp.zeros((8, T), F32)

            def inner(j, c):
                kv(j, r0, lsa, lsb, dla, dlb, False)
                return c

            lax.fori_loop(i - back, i, inner, 0)
            kv(i, r0, lsa, lsb, dla, dlb, True)
            dq_ref[pl.ds(r0, T), :] = dqt[...].T
            dcq_ref[0:1, pl.ds(r0, T)] = dcqa[0:1, :]
            dcq_ref[1:2, pl.ds(r0, T)] = dcqb[0:1, :]
            return carry

        lax.fori_loop(0, nq, qblk, 0)
        dv_ref[...] = dva[...].astype(BF16)
        dck_ref[...] = jnp.where(lane_s, jnp.sum(dcka[...], axis=1, keepdims=True),
                                 jnp.sum(dckb[...], axis=1, keepdims=True))
        if ride is not None:
            @pl.when(pl.program_id(0) == n_pairs - 1)
            def _():
                _wait_exchange("scatter", *xrefs)

    extra = () if ride is None else tuple(ride)
    return pl.pallas_call(
        body, name="fox_bwd" if ride is None else "fox_bwd_exchange",
        grid=(n_pairs,),
        in_specs=[_pair_blk(S), _pair_blk(S), _pair_blk(S, OFF_FV // 128), _pair_blk(S), _pair_blk(S),
                  _pair_rows(S), _pair_blk(S), _pair_rows(S)] + [_ANY] * len(extra),
        out_specs=[_pair_blk(S), _pair_blk(S), _pair_blk(S), _pair_blk(S), _pair_rows(S)] + [_ANY] * len(extra),
        out_shape=[jax.ShapeDtypeStruct((S, FOX_W), F32), jax.ShapeDtypeStruct((S, FOX_W), F32),
                   jax.ShapeDtypeStruct((S, FOX_W), BF16), jax.ShapeDtypeStruct((S, FOX_W), F32),
                   jax.ShapeDtypeStruct((n_pairs, 8, S), F32)]
        + (_exchange_out_shapes("scatter", *extra) if extra else []),
        scratch_shapes=[pltpu.VMEM((S, 128), BF16)] * 2 + [pltpu.VMEM((128, S), BF16)] * 2
        + [pltpu.VMEM((S, 128), BF16)] * 3 + [pltpu.VMEM((S, 128), F32)] * 5
        + [pltpu.VMEM((128, T), F32)] + [pltpu.VMEM((8, T), F32)] * 2
        + (_EXCHANGE_SEMS if extra else []),
        compiler_params=_cp(("arbitrary",), _VMEM_BIG),
    )(qs, kn, proj, dfo, fo, lse, cqb, crow4, *extra)


def _sb_bwd(proj, dso, ltot, tril):
    S = proj.shape[0]
    T = tril.shape[0]
    nq = S // T

    def body(q_ref, k_ref, v_ref, do_ref, lt_ref, tri_ref, dq_ref, dk_ref, dv_ref,
             qa, qb, k2, kta, ktb, vb, doa, dob, dka, dva, dqt, ra, rb, ga, gb):
        lane_s = _head_masks(S)
        q = (q_ref[...] * Q_SCALE).astype(BF16)
        zq = jnp.zeros_like(q)
        qa[...] = jnp.where(lane_s, q, zq)
        qb[...] = jnp.where(lane_s, zq, q)
        k2[...] = k_ref[...].astype(BF16)
        vb[...] = v_ref[...].astype(BF16)
        do = do_ref[...].astype(BF16)
        doa[...] = jnp.where(lane_s, do, zq)
        dob[...] = jnp.where(lane_s, zq, do)
        dka[...] = jnp.zeros((S, 128), F32)
        dva[...] = jnp.zeros((S, 128), F32)
        row_t = lax.broadcasted_iota(jnp.int32, (128, T), 0) < HEAD_DIM

        def prep(c, carry):
            c0 = pl.multiple_of(c * T, T)
            kt = k_ref[pl.ds(c0, T), :].T
            kta[:, pl.ds(c0, T)] = jnp.where(row_t, kt, 0.0).astype(BF16)
            ktb[:, pl.ds(c0, T)] = jnp.where(row_t, 0.0, kt).astype(BF16)
            return carry

        lax.fori_loop(0, nq, prep, 0)
        strict = (lax.broadcasted_iota(jnp.int32, (T, T), 0) < lax.broadcasted_iota(jnp.int32, (T, T), 1))

        def head(qh, kfull, kth, v, doh, r_ref, g_ref, lt_row, c0, masked):
            z = _dot_nt(kfull, qh)
            e, sp = _softplus_parts(z)
            lb = -sp
            if masked:
                lb = jnp.where(strict, lb, 0.0)
            tri = tri_ref[...]
            pre = _mm2(lb, tri, left=True)
            r = r_ref[0:1, :]
            a = jnp.exp(z + lb + ((lt_row - r) - pre))
            if masked:
                a = jnp.where(strict, a, 0.0)
            da = _dot_nt(v, doh)
            g = a * da
            gpre = _mm2(g, tri, left=True)
            gc = g_ref[0:1, :]
            big_g = gc + (gpre - g)
            inv = 1.0 / (1.0 + e)
            pos = z >= 0.0
            sig = jnp.where(pos, 1.0, e) * inv
            oms = jnp.where(pos, e, 1.0) * inv
            dz = g * oms - sig * big_g
            if masked:
                dz = jnp.where(strict, dz, 0.0)
            dzb = dz.astype(BF16)
            dqt[...] = dqt[...] + _dot(kth, dzb)
            dka[pl.ds(c0, T), :] = dka[pl.ds(c0, T), :] + _dot(dzb, qh)
            dva[pl.ds(c0, T), :] = dva[pl.ds(c0, T), :] + _dot(a.astype(BF16), doh)
            r_ref[0:1, :] = r + pre[T - 1:T, :]
            g_ref[0:1, :] = gc + gpre[T - 1:T, :]

        def kv(j, r0, lta, ltb, masked):
            c0 = pl.multiple_of(j * T, T)
            kfull = k2[pl.ds(c0, T), :]
            v = vb[pl.ds(c0, T), :]
            head(qa[pl.ds(r0, T), :], kfull, kta[:, pl.ds(c0, T)], v, doa[pl.ds(r0, T), :], ra, ga, lta, c0, masked)
            head(qb[pl.ds(r0, T), :], kfull, ktb[:, pl.ds(c0, T)], v, dob[pl.ds(r0, T), :], rb, gb, ltb, c0, masked)

        def qblk(i, carry):
            r0 = pl.multiple_of(i * T, T)
            lt = lt_ref[:, pl.ds(r0, T)]
            lta = lt[0:1, :]
            ltb = lt[1:2, :]
            back = jnp.max(lt[2:3, :]).astype(jnp.int32)
            zt = jnp.zeros((8, T), F32)
            dqt[...] = jnp.zeros((128, T), F32)
            ra[...] = zt
            rb[...] = zt
            ga[...] = zt
            gb[...] = zt

            def inner(j, c):
                kv(j, r0, lta, ltb, False)
                return c

            lax.fori_loop(i - back, i, inner, 0)
            kv(i, r0, lta, ltb, True)
            dq_ref[pl.ds(r0, T), :] = (dqt[...] * Q_SCALE).T.astype(BF16)
            return carry

        lax.fori_loop(0, nq, qblk, 0)
        dk_ref[...] = dka[...].astype(BF16)
        dv_ref[...] = dva[...].astype(BF16)

    return pl.pallas_call(
        body, name="sb_bwd",
        grid=(SB_W // 128,),
        in_specs=[_pair_blk(S, OFF_SQ // 128), _pair_blk(S, OFF_SK // 128), _pair_blk(S, OFF_SV // 128),
                  _pair_blk(S), _pair_rows(S), pl.BlockSpec((T, T), lambda p: (0, 0))],
        out_specs=[_pair_blk(S), _pair_blk(S), _pair_blk(S)],
        out_shape=[jax.ShapeDtypeStruct((S, SB_W), BF16)] * 3,
        scratch_shapes=([pltpu.VMEM((S, 128), BF16)] * 3 + [pltpu.VMEM((128, S), BF16)] * 2
                        + [pltpu.VMEM((S, 128), BF16)] * 3 + [pltpu.VMEM((S, 128), F32)] * 2
                        + [pltpu.VMEM((128, T), F32)] + [pltpu.VMEM((8, T), F32)] * 4),
        compiler_params=_cp(("arbitrary",), _VMEM_BIG),
    )(proj, proj, proj, dso, ltot, tril)


def _head_norm_bwd(x, g, dy, bd):
    ss = _mm2(x * x, bd)
    r = lax.rsqrt(ss * (1.0 / HEAD_DIM) + EPS)
    xr = x * r
    gdy = g * dy
    m = _mm2(xr * gdy, bd) * (1.0 / HEAD_DIM)
    return r * (gdy - xr * m), dy * xr


def _qk_bwd(dqs, dkn, proj, pff, bfp, gq, gk, bd, dccol, triu):
    S = proj.shape[0]
    T = triu.shape[0]
    n = S // T
    rev = lambda col: (lambda i: (n - 1 - i, col))

    def body(dq_ref, dk_ref, q_ref, k_ref, ff_ref, b_ref, gq_ref, gk_ref, bd_ref, dc_ref, tri_ref,
             dfq_ref, dfk_ref, dff_ref, dgq_ref, dgk_ref, dbf_ref, carry):
        @pl.when(pl.program_id(0) == 0)
        def _():
            carry[...] = jnp.zeros_like(carry)
            dgq_ref[...] = jnp.zeros_like(dgq_ref)
            dgk_ref[...] = jnp.zeros_like(dgk_ref)
            dbf_ref[...] = jnp.zeros_like(dbf_ref)

        bdv = bd_ref[...]
        dxq, gq_rows = _head_norm_bwd(q_ref[...], gq_ref[...], dq_ref[...] * Q_SCALE, bdv)
        dfq_ref[...] = dxq.astype(BF16)
        dgq_ref[...] = dgq_ref[...] + jnp.sum(gq_rows, axis=0, keepdims=True)
        dxk, gk_rows = _head_norm_bwd(k_ref[...], gk_ref[...], dk_ref[...], bdv)
        dfk_ref[...] = dxk.astype(BF16)
        dgk_ref[...] = dgk_ref[...] + jnp.sum(gk_rows, axis=0, keepdims=True)
        dlf = _mm3(dc_ref[...], tri_ref[...], left=True) + carry[0:1, :]
        carry[0:1, :] = dlf[0:1, :]
        u = ff_ref[...] + b_ref[...]
        lane = lax.broadcasted_iota(jnp.int32, u.shape, 1)
        dff = jnp.where(lane < N_FF, dlf * _sigmoid(-u), 0.0)
        dff_ref[...] = dff.astype(BF16)
        dbf_ref[...] = dbf_ref[...] + jnp.sum(dff, axis=0, keepdims=True)

    return pl.pallas_call(
        body, name="qk_bwd",
        grid=(n,),
        in_specs=[pl.BlockSpec((T, FOX_W), rev(0)), pl.BlockSpec((T, FOX_W), rev(0)),
                  pl.BlockSpec((T, FOX_W), rev(OFF_FQ // FOX_W)), pl.BlockSpec((T, FOX_W), rev(OFF_FK // FOX_W)),
                  pl.BlockSpec((T, N_FFPAD), rev(0)),
                  pl.BlockSpec((1, N_FFPAD), lambda i: (0, 0)),
                  pl.BlockSpec((1, FOX_W), lambda i: (0, 0)), pl.BlockSpec((1, FOX_W), lambda i: (0, 0)),
                  pl.BlockSpec((FOX_W, FOX_W), lambda i: (0, 0)),
                  pl.BlockSpec((T, N_FFPAD), rev(0)),
                  pl.BlockSpec((T, T), lambda i: (0, 0))],
        out_specs=[pl.BlockSpec((T, FOX_W), rev(0)), pl.BlockSpec((T, FOX_W), rev(0)),
                   pl.BlockSpec((T, N_FFPAD), rev(0)),
                   pl.BlockSpec((1, FOX_W), lambda i: (0, 0)), pl.BlockSpec((1, FOX_W), lambda i: (0, 0)),
                   pl.BlockSpec((1, N_FFPAD), lambda i: (0, 0))],
        out_shape=[jax.ShapeDtypeStruct((S, FOX_W), BF16), jax.ShapeDtypeStruct((S, FOX_W), BF16),
                   jax.ShapeDtypeStruct((S, N_FFPAD), BF16),
                   jax.ShapeDtypeStruct((1, FOX_W), F32), jax.ShapeDtypeStruct((1, FOX_W), F32),
                   jax.ShapeDtypeStruct((1, N_FFPAD), F32)],
        scratch_shapes=[pltpu.VMEM((8, N_FFPAD), F32)],
        compiler_params=_cp(("arbitrary",), 40 << 20),
    )(dqs, dkn, proj, proj, pff, bfp, gq, gk, bd, dccol, triu)


def _inproj_bwd_dx(dpm, dff, wm, wff, x, g, dy):
    S, D = x.shape
    tm = min(256, S)

    def body(dp_ref, dff_ref, w_ref, wff_ref, x_ref, g_ref, dy_ref, dx_ref, dg_ref):
        @pl.when(pl.program_id(0) == 0)
        def _():
            dg_ref[...] = jnp.zeros_like(dg_ref)

        dh = _dot_nt(dp_ref[...], w_ref[...]) + _dot_nt(dff_ref[...], wff_ref[...])
        xv = x_ref[...]
        r = _rms_rows(xv)
        xr = xv * r
        dg_ref[...] = dg_ref[...] + jnp.sum(dh * xr, axis=0, keepdims=True)
        gdh = g_ref[...] * dh
        m = jnp.mean(gdh * xr, axis=-1, keepdims=True)
        dx_ref[...] = dy_ref[...] + r * (gdh - xr * m)

    return pl.pallas_call(
        body, name="inproj_bwd_dx",
        grid=(S // tm,),
        in_specs=[pl.BlockSpec((tm, N_MAIN), lambda i: (i, 0)),
                  pl.BlockSpec((tm, N_FFPAD), lambda i: (i, 0)),
                  pl.BlockSpec((D, N_MAIN), lambda i: (0, 0)),
                  pl.BlockSpec((D, N_FFPAD), lambda i: (0, 0)),
                  pl.BlockSpec((tm, D), lambda i: (i, 0)),
                  pl.BlockSpec((1, D), lambda i: (0, 0)),
                  pl.BlockSpec((tm, D), lambda i: (i, 0))],
        out_specs=[pl.BlockSpec((tm, D), lambda i: (i, 0)), pl.BlockSpec((1, D), lambda i: (0, 0))],
        out_shape=[jax.ShapeDtypeStruct((S, D), F32), jax.ShapeDtypeStruct((1, D), F32)],
        compiler_params=_cp(("arbitrary",), 48 << 20),
    )(dpm, dff, wm, wff, x, g, dy)


def _inproj_bwd_dw(x, g, dpm, dff):
    S, D = x.shape
    tk = min(_TM, S)
    tn = 512

    def body(x_ref, g_ref, dp_ref, dff_ref, dw_ref, dwff_ref):
        j, k = pl.program_id(0), pl.program_id(1)

        @pl.when(k == 0)
        def _():
            dw_ref[...] = jnp.zeros_like(dw_ref)

        @pl.when((k == 0) & (j == 0))
        def _():
            dwff_ref[...] = jnp.zeros_like(dwff_ref)

        xv = x_ref[...]
        h = ((xv * _rms_rows(xv)) * g_ref[...]).astype(BF16)
        dw_ref[...] = dw_ref[...] + _dot_tn(h, dp_ref[...])

        @pl.when(j == 0)
        def _():
            dwff_ref[...] = dwff_ref[...] + _dot_tn(h, dff_ref[...])

    return pl.pallas_call(
        body, name="inproj_bwd_dw",
        grid=(N_MAIN // tn, S // tk),
        in_specs=[pl.BlockSpec((tk, D), lambda j, k: (k, 0)),
                  pl.BlockSpec((1, D), lambda j, k: (0, 0)),
                  pl.BlockSpec((tk, tn), lambda j, k: (k, j)),
                  pl.BlockSpec((tk, N_FFPAD), lambda j, k: (k, 0))],
        out_specs=[pl.BlockSpec((D, tn), lambda j, k: (0, j)), pl.BlockSpec((D, N_FFPAD), lambda j, k: (0, 0))],
        out_shape=[jax.ShapeDtypeStruct((D, N_MAIN), F32), jax.ShapeDtypeStruct((D, N_FFPAD), F32)],
        compiler_params=_cp(("arbitrary", "arbitrary"), 40 << 20),
    )(x, g, dpm, dff)


def _constants(T):
    tril = jnp.tril(jnp.ones((T, T), F32)).astype(BF16)
    hid = jnp.arange(FOX_W) // HEAD_DIM
    bd = (hid[:, None] == hid[None, :]).astype(BF16)
    ex = (jnp.arange(N_FFPAD)[:, None] == hid[None, :]).astype(BF16)
    return tril, tril.T, bd, ex


def _crow4(ccol):
    S = ccol.shape[0]
    c = ccol[:, :FOX_HEADS].T.reshape(FOX_HEADS // 2, 2, S)
    return jnp.pad(c, ((0, 0), (0, 6), (0, 0)))


def _layer_fwd(x, lw, consts, ride=None):
    tril, triu, bd, ex = consts
    proj, pff = _inproj_fwd(x, lw["g"], lw["wm"], lw["wff"])
    qs, kn, ccol, cqb = _fox_prep(proj, pff, lw["bfp"], lw["gq"], lw["gk"], bd, ex, tril)
    crow4 = _crow4(ccol)
    fo, lse, *gathered = _fox_fwd(qs, kn, proj, cqb, crow4, ride)
    so, ltot = _sb_fwd(proj, triu)
    pooled = _pool_fwd(proj)
    y, mixed = _mix_out(fo, so, pooled, proj, lw["wbd"], lw["scale"], lw["wout"], x)
    return y, (x, proj, pff, qs, kn, cqb, crow4, fo, lse, so, ltot, pooled, mixed), gathered


def _layer_bwd(dy, saved, lw, consts, ride=None):
    tril, triu, bd, _ = consts
    x, proj, pff, qs, kn, cqb, crow4, fo, lse, so, ltot, pooled, mixed = saved
    S = x.shape[0]
    dfo, dfg, dso, dsg, dpg, dpooled, dscale, dwbd = _gate_bwd(dy, lw["wout"], fo, so, pooled, proj, lw["wbd"], lw["scale"])
    dwout = _matmul_tn(mixed, dy, "dw_out")
    dpx = _pool_bwd(dpooled)
    dqs, dkn, dfv, dck, dcq4, *received = _fox_bwd(qs, kn, proj, dfo, fo, lse, cqb, crow4, ride)
    dsq, dsk, dsv = _sb_bwd(proj, dso, ltot, tril)
    dc8 = dck[:, ::HEAD_DIM] + dcq4[:, :2, :].reshape(FOX_HEADS, S).T
    dccol = jnp.pad(dc8, ((0, 0), (0, N_FFPAD - FOX_HEADS)))
    dfq, dfk, dff, dgq, dgk, dbf = _qk_bwd(dqs, dkn, proj, pff, lw["bfp"], lw["gq"], lw["gk"], bd, dccol, triu)
    dpm = jnp.concatenate([dfq, dfk, dfv, dfg, dpx, dpg, dsq, dsk, dsv, dsg], axis=1)
    dx, dng = _inproj_bwd_dx(dpm, dff, lw["wm"], lw["wff"], x, lw["g"], dy)
    dwm, dwff = _inproj_bwd_dw(x, lw["g"], dpm, dff)
    dwin = jnp.concatenate([dwm[:, :OFF_PX], dwff[:, :N_FF], dwm[:, OFF_PX:]], axis=1)
    grads = {
        "norm_g": dng[0],
        "w_in": dwin,
        "b_f": dbf[0, :N_FF],
        "q_norm_g": dgq[0].reshape(FOX_HEADS, HEAD_DIM).sum(0),
        "k_norm_g": dgk[0].reshape(FOX_HEADS, HEAD_DIM).sum(0),
        "w_pool": jnp.stack([dwbd[64 * i:64 * i + 64, 64 * i:64 * i + 64] for i in range(4)]),
        "pool_scale": dscale[0],
        "w_out": dwout,
    }
    return dx, grads, received


def _layer_weights(l, norm_g, gin, b_f, q_norm_g, k_norm_g, w_pool, pool_scale, gout):
    D = gin.shape[1]
    w = gin.transpose(1, 0, 2).reshape(D, D_IN)
    wm = jnp.concatenate([w[:, :2048], w[:, 2048 + N_FF:]], axis=1)
    wff = jnp.pad(w[:, 2048:2048 + N_FF], ((0, 0), (0, N_FFPAD - N_FF)))
    wbd = jnp.zeros((POOL_W, POOL_W), F32)
    for i in range(4):
        wbd = wbd.at[64 * i:64 * i + 64, 64 * i:64 * i + 64].set(w_pool[l, i])
    return {
        "g": norm_g[l].reshape(1, D),
        "wm": wm, "wff": wff,
        "bfp": jnp.pad(b_f[l], (0, N_FFPAD - N_FF)).reshape(1, N_FFPAD),
        "gq": jnp.tile(q_norm_g[l], FOX_HEADS).reshape(1, FOX_W),
        "gk": jnp.tile(k_norm_g[l], FOX_HEADS).reshape(1, FOX_W),
        "wbd": wbd.astype(BF16),
        "scale": pool_scale[l].reshape(1, POOL_W),
        "wout": gout.reshape(D_MIX, D),
    }


def _grad_parts(g):
    dwin, dwout = g["w_in"], g["w_out"]
    D = dwin.shape[0]
    return (dwin.reshape(D, N_DEV, D_IN // N_DEV).transpose(1, 0, 2),
            dwout.reshape(N_DEV, D_MIX // N_DEV, dwout.shape[1]))


def _train_step(x, target, norm_g, win_sh, b_f, q_norm_g, k_norm_g, w_pool, pool_scale, wout_sh):
    L = norm_g.shape[0]
    consts = _constants(min(_T, x.shape[0]))
    gathered = _exchange_pair("gather", win_sh[0], wout_sh[0], "gather_weights")
    lws, saved = [], []
    h = x
    for l in range(L):
        lws.append(_layer_weights(l, norm_g, gathered[0], b_f, q_norm_g, k_norm_g, w_pool, pool_scale, gathered[1]))
        ride = (win_sh[l + 1], wout_sh[l + 1]) if l + 1 < L else None
        h, sv, gathered = _layer_fwd(h, lws[l], consts, ride)
        saved.append(sv)
    dy, loss = _loss_head(h, target)
    grads, received = [None] * L, [None] * L
    ride = None
    for l in reversed(range(L)):
        dy, grads[l], got = _layer_bwd(dy, saved[l], lws[l], consts, ride)
        if ride is not None:
            received[l + 1] = got
        ride = _grad_parts(grads[l])
    received[0] = _exchange_pair("scatter", ride[0], ride[1], "exchange_grads")
    return loss, dy, grads, received


def _mesh_pos():
    return lax.axis_index("x"), lax.axis_index("y"), lax.axis_index("c")


_FLIPS = [(0, 0, 1), (1, 0, 0), (0, 1, 0), (1, 1, 0), (1, 0, 1), (0, 1, 1), (1, 1, 1)]


def _peers():
    x, y, c = _mesh_pos()
    out = []
    for fx, fy, fc in _FLIPS:
        px = 1 - x if fx else x
        py = 1 - y if fy else y
        pc = 1 - c if fc else c
        out.append(((px, py, pc), 4 * px + 2 * py + pc))
    return out, 4 * x + 2 * y + c


_EXCHANGE_SEMS = [pltpu.SemaphoreType.DMA((14,)), pltpu.SemaphoreType.DMA((14,)), pltpu.SemaphoreType.DMA((2,))]
_ANY = pl.BlockSpec(memory_space=pl.ANY)


def _exchange_copies(kind, a_ref, b_ref, oa_ref, ob_ref, send_sems, recv_sems, loc_sems):
    peers, me = _peers()
    pairs = ((a_ref, oa_ref), (b_ref, ob_ref))
    local = [pltpu.make_async_copy(src if kind == "gather" else src.at[me], dst.at[me], loc_sems.at[t])
             for t, (src, dst) in enumerate(pairs)]
    remote = []
    for k, (dev, idx) in enumerate(peers):
        for t, (src, dst) in enumerate(pairs):
            remote.append(pltpu.make_async_remote_copy(
                src_ref=src if kind == "gather" else src.at[idx], dst_ref=dst.at[me],
                send_sem=send_sems.at[2 * k + t], recv_sem=recv_sems.at[2 * k + t],
                device_id=dev, device_id_type=pl.DeviceIdType.MESH))
    return local, remote


def _start_exchange(kind, *refs):
    local, remote = _exchange_copies(kind, *refs)
    for cp in local + remote:
        cp.start()


def _wait_exchange(kind, *refs):
    local, remote = _exchange_copies(kind, *refs)
    for cp in remote:
        cp.wait_recv()
    for cp in remote:
        cp.wait_send()
    for cp in local:
        cp.wait()


def _exchange_out_shapes(kind, a, b):
    if kind == "gather":
        return [jax.ShapeDtypeStruct((N_DEV,) + a.shape, a.dtype), jax.ShapeDtypeStruct((N_DEV,) + b.shape, b.dtype)]
    return [jax.ShapeDtypeStruct(a.shape, a.dtype), jax.ShapeDtypeStruct(b.shape, b.dtype)]


def _exchange_pair(kind, a, b, name):
    def body(*refs):
        _start_exchange(kind, *refs)
        _wait_exchange(kind, *refs)

    return pl.pallas_call(
        body, name=name,
        in_specs=[_ANY, _ANY], out_specs=[_ANY, _ANY],
        out_shape=_exchange_out_shapes(kind, a, b),
        scratch_shapes=_EXCHANGE_SEMS,
    )(a, b)


def _adam_math(w, g, m, v):
    m_new = ADAM_B1 * m + (1.0 - ADAM_B1) * g
    v_new = ADAM_B2 * v + (1.0 - ADAM_B2) * (g * g)
    m_hat = m_new / (1.0 - ADAM_B1 ** ADAM_STEP)
    v_hat = v_new / (1.0 - ADAM_B2 ** ADAM_STEP)
    delta = -ADAM_LR * (m_hat / (jnp.sqrt(v_hat) + ADAM_EPS) + ADAM_WD * w)
    return delta, m_new, v_new


def _sum_adamw(gparts, w, m, v, name):
    R, C = w.shape
    tr = min(128, R)

    def body(gp_ref, w_ref, m_ref, v_ref, g_ref, d_ref, nm_ref, nv_ref):
        g = gp_ref[0]
        for s in range(1, N_DEV):
            g = g + gp_ref[s]
        d, mn, vn = _adam_math(w_ref[...], g, m_ref[...], v_ref[...])
        g_ref[...] = g
        d_ref[...] = d
        nm_ref[...] = mn
        nv_ref[...] = vn

    blk = pl.BlockSpec((tr, C), lambda r: (r, 0))
    return pl.pallas_call(
        body, name=name,
        grid=(R // tr,),
        in_specs=[pl.BlockSpec((N_DEV, tr, C), lambda r: (0, r, 0)), blk, blk, blk],
        out_specs=[blk, blk, blk, blk],
        out_shape=[jax.ShapeDtypeStruct((R, C), F32)] * 4,
        compiler_params=_cp(("parallel",), 40 << 20),
    )(gparts, w, m, v)


def _small_update(gpack, wpack, mpack, vpack):
    R = gpack.shape[0]
    VM = pl.BlockSpec(memory_space=pltpu.VMEM)

    def body(g_ref, w_ref, m_ref, v_ref, gs_ref, d_ref, nm_ref, nv_ref, buf, send_sems, recv_sems):
        peers, me = _peers()
        buf[me] = g_ref[...]
        copies = []
        for k, (dev, _) in enumerate(peers):
            cp = pltpu.make_async_remote_copy(
                src_ref=g_ref, dst_ref=buf.at[me], send_sem=send_sems.at[k], recv_sem=recv_sems.at[k],
                device_id=dev, device_id_type=pl.DeviceIdType.MESH)
            cp.start()
            copies.append(cp)
        for cp in copies:
            cp.wait_recv()
        for cp in copies:
            cp.wait_send()
        g = buf[0]
        for s in range(1, N_DEV):
            g = g + buf[s]
        d, mn, vn = _adam_math(w_ref[...], g, m_ref[...], v_ref[...])
        gs_ref[...] = g
        d_ref[...] = d
        nm_ref[...] = mn
        nv_ref[...] = vn

    return pl.pallas_call(
        body, name="small_update",
        in_specs=[VM] * 4, out_specs=[VM] * 4,
        out_shape=[jax.ShapeDtypeStruct((R, 128), F32)] * 4,
        scratch_shapes=[pltpu.VMEM((N_DEV, R, 128), F32), pltpu.SemaphoreType.DMA((7,)), pltpu.SemaphoreType.DMA((7,))],
        compiler_params=_cp(None, 40 << 20),
    )(gpack, wpack, mpack, vpack)


_SMALL = ("norm_g", "b_f", "q_norm_g", "k_norm_g", "w_pool", "pool_scale")


def _pack(parts):
    flat = jnp.concatenate([p.reshape(-1) for p in parts])
    n = flat.shape[0]
    rows = -(-n // (8 * 128)) * 8
    return jnp.pad(flat, (0, rows * 128 - n)).reshape(rows, 128)


def _unpack(packed, like):
    flat = packed.reshape(-1)
    out, o = [], 0
    for p in like:
        out.append(flat[o:o + p.size].reshape(p.shape))
        o += p.size
    return out


def kernel(x, norm_g, w_in, b_f, q_norm_g, k_norm_g, w_pool, pool_scale, w_out, loss_target, m_norm_g, m_w_in, m_b_f, m_q_norm_g, m_k_norm_g, m_w_pool, m_pool_scale, m_w_out, v_norm_g, v_w_in, v_b_f, v_q_norm_g, v_k_norm_g, v_w_pool, v_pool_scale, v_w_out):
    L = w_in.shape[0]

    loss_local, dx, grads, received = _train_step(x[0], loss_target[0], norm_g, w_in.astype(BF16), b_f, q_norm_g,
                                                  k_norm_g, w_pool, pool_scale, w_out.astype(BF16))
    loss = lax.psum(loss_local, MESH_AXES)
    g = {k: jnp.stack([grads[l][k] for l in range(L)]) for k in _SMALL}

    upd_in = [_sum_adamw(received[l][0], w_in[l], m_w_in[l], v_w_in[l], "adamw_w_in") for l in range(L)]
    upd_out = [_sum_adamw(received[l][1], w_out[l], m_w_out[l], v_w_out[l], "adamw_w_out") for l in range(L)]
    g_win, d_win, nm_win, nv_win = [jnp.stack([u[i] for u in upd_in]) for i in range(4)]
    g_wout, d_wout, nm_wout, nv_wout = [jnp.stack([u[i] for u in upd_out]) for i in range(4)]

    ws = dict(norm_g=norm_g, b_f=b_f, q_norm_g=q_norm_g, k_norm_g=k_norm_g, w_pool=w_pool, pool_scale=pool_scale)
    ms = dict(norm_g=m_norm_g, b_f=m_b_f, q_norm_g=m_q_norm_g, k_norm_g=m_k_norm_g, w_pool=m_w_pool, pool_scale=m_pool_scale)
    vs = dict(norm_g=v_norm_g, b_f=v_b_f, q_norm_g=v_q_norm_g, k_norm_g=v_k_norm_g, w_pool=v_w_pool, pool_scale=v_pool_scale)
    like = [ws[k] for k in _SMALL]
    gs_p, d_p, nm_p, nv_p = _small_update(_pack([g[k] for k in _SMALL]), _pack(like),
                                          _pack([ms[k] for k in _SMALL]), _pack([vs[k] for k in _SMALL]))
    gs = dict(zip(_SMALL, _unpack(gs_p, like)))
    ds = dict(zip(_SMALL, _unpack(d_p, like)))
    nms = dict(zip(_SMALL, _unpack(nm_p, like)))
    nvs = dict(zip(_SMALL, _unpack(nv_p, like)))
    gs["w_in"], ds["w_in"], nms["w_in"], nvs["w_in"] = g_win, d_win, nm_win, nv_win
    gs["w_out"], ds["w_out"], nms["w_out"], nvs["w_out"] = g_wout, d_wout, nm_wout, nv_wout

    order = ("norm_g", "w_in", "b_f", "q_norm_g", "k_norm_g", "w_pool", "pool_scale", "w_out")
    return (loss, dx[None], *[gs[k] for k in order], *[ds[k] for k in order],
            *[nms[k] for k in order], *[nvs[k] for k in order])
```

```python
import functools

import jax
import jax.numpy as jnp
from jax import lax
from jax.experimental import pallas as pl
from jax.experimental.pallas import tpu as pltpu

F32 = jnp.float32
BF16 = jnp.bfloat16

EPS = 1e-6
NEG = -1e30
HEAD_DIM = 64
FOX_HEADS = 8
FOX_W = 512
POOL_W = 256
SB_W = 256
D_MIX = 1024
N_FF = 8
N_MAIN = 3584
N_FFPAD = 128
OFF_FQ, OFF_FK, OFF_FV, OFF_FG = 0, 512, 1024, 1536
OFF_PX, OFF_PG = 2048, 2304
OFF_SQ, OFF_SK, OFF_SV, OFF_SG = 2560, 2816, 3072, 3328
D_IN = 3592
Q_SCALE = HEAD_DIM ** -0.5

ADAM_LR = 0.001
ADAM_B1 = 0.9
ADAM_B2 = 0.999
ADAM_EPS = 1e-08
ADAM_WD = 0.01
ADAM_STEP = 10

N_DEV = 8
MESH_AXES = ("x", "y", "c")

_T = 256
_TM = 512
_VMEM_BIG = 56 << 20


def _cp(sem=None, vmem=None):
    kw = {}
    if sem is not None:
        kw["dimension_semantics"] = sem
    if vmem is not None:
        kw["vmem_limit_bytes"] = vmem
    return pltpu.CompilerParams(**kw)


def _dot(a, b):
    return jnp.dot(a, b, preferred_element_type=F32)


def _dot_nt(a, b):
    return lax.dot_general(a, b, (((1,), (1,)), ((), ())), preferred_element_type=F32)


def _dot_tn(a, b):
    return lax.dot_general(a, b, (((0,), (0,)), ((), ())), preferred_element_type=F32)


def _mm2(v, m, left=False):
    hi = v.astype(BF16)
    lo = (v - hi.astype(F32)).astype(BF16)
    if left:
        return _dot(m, hi) + _dot(m, lo)
    return _dot(hi, m) + _dot(lo, m)


def _mm3(v, m, left=False):
    a1 = v.astype(BF16)
    r1 = v - a1.astype(F32)
    a2 = r1.astype(BF16)
    a3 = (r1 - a2.astype(F32)).astype(BF16)
    if left:
        return _dot(m, a1) + _dot(m, a2) + _dot(m, a3)
    return _dot(a1, m) + _dot(a2, m) + _dot(a3, m)


def _sigmoid(z):
    return 1.0 / (1.0 + jnp.exp(-z))


def _rms_rows(x):
    return lax.rsqrt(jnp.mean(x * x, axis=-1, keepdims=True) + EPS)


def _inproj_fwd(x, g, wm, wff):
    S, D = x.shape
    tm = min(_TM, S)
    tn = 512

    def body(x_ref, g_ref, w_ref, wff_ref, o_ref, off_ref, h_ref):
        @pl.when(pl.program_id(1) == 0)
        def _():
            xv = x_ref[...]
            h = (xv * _rms_rows(xv)) * g_ref[...]
            h_ref[...] = h.astype(BF16)
            off_ref[...] = _dot(h_ref[...], wff_ref[...])

        o_ref[...] = _dot(h_ref[...], w_ref[...])

    return pl.pallas_call(
        body, name="inproj_fwd",
        grid=(S // tm, N_MAIN // tn),
        in_specs=[pl.BlockSpec((tm, D), lambda i, j: (i, 0)),
                  pl.BlockSpec((1, D), lambda i, j: (0, 0)),
                  pl.BlockSpec((D, tn), lambda i, j: (0, j)),
                  pl.BlockSpec((D, N_FFPAD), lambda i, j: (0, 0))],
        out_specs=[pl.BlockSpec((tm, tn), lambda i, j: (i, j)),
                   pl.BlockSpec((tm, N_FFPAD), lambda i, j: (i, 0))],
        out_shape=[jax.ShapeDtypeStruct((S, N_MAIN), F32), jax.ShapeDtypeStruct((S, N_FFPAD), F32)],
        scratch_shapes=[pltpu.VMEM((tm, D), BF16)],
        compiler_params=_cp(("parallel", "arbitrary"), 40 << 20),
    )(x, g, wm, wff)


def _head_norm(x, g, bd):
    ss = _mm2(x * x, bd)
    r = lax.rsqrt(ss * (1.0 / HEAD_DIM) + EPS)
    return (x * r) * g


def _fox_prep(proj, pff, bfp, gq, gk, bd, ex, tril):
    S = proj.shape[0]
    T = tril.shape[0]

    def body(q_ref, k_ref, ff_ref, b_ref, gq_ref, gk_ref, bd_ref, ex_ref, tri_ref,
             qs_ref, kn_ref, cc_ref, cqb_ref, carry):
        @pl.when(pl.program_id(0) == 0)
        def _():
            carry[...] = jnp.zeros_like(carry)

        bdv = bd_ref[...]
        qs_ref[...] = (_head_norm(q_ref[...], gq_ref[...], bdv) * Q_SCALE).astype(BF16)
        kn_ref[...] = _head_norm(k_ref[...], gk_ref[...], bdv).astype(BF16)
        u = ff_ref[...] + b_ref[...]
        lf = jnp.minimum(u, 0.0) - jnp.log1p(jnp.exp(-jnp.abs(u)))
        c = _mm3(lf, tri_ref[...], left=True) + carry[0:1, :]
        carry[0:1, :] = c[T - 1:T, :]
        cc_ref[...] = c
        cqb_ref[...] = _mm3(c, ex_ref[...])

    return pl.pallas_call(
        body, name="fox_prep",
        grid=(S // T,),
        in_specs=[pl.BlockSpec((T, FOX_W), lambda i: (i, OFF_FQ // FOX_W)),
                  pl.BlockSpec((T, FOX_W), lambda i: (i, OFF_FK // FOX_W)),
                  pl.BlockSpec((T, N_FFPAD), lambda i: (i, 0)),
                  pl.BlockSpec((1, N_FFPAD), lambda i: (0, 0)),
                  pl.BlockSpec((1, FOX_W), lambda i: (0, 0)),
                  pl.BlockSpec((1, FOX_W), lambda i: (0, 0)),
                  pl.BlockSpec((FOX_W, FOX_W), lambda i: (0, 0)),
                  pl.BlockSpec((N_FFPAD, FOX_W), lambda i: (0, 0)),
                  pl.BlockSpec((T, T), lambda i: (0, 0))],
        out_specs=[pl.BlockSpec((T, FOX_W), lambda i: (i, 0)),
                   pl.BlockSpec((T, FOX_W), lambda i: (i, 0)),
                   pl.BlockSpec((T, N_FFPAD), lambda i: (i, 0)),
                   pl.BlockSpec((T, FOX_W), lambda i: (i, 0))],
        out_shape=[jax.ShapeDtypeStruct((S, FOX_W), BF16), jax.ShapeDtypeStruct((S, FOX_W), BF16),
                   jax.ShapeDtypeStruct((S, N_FFPAD), F32), jax.ShapeDtypeStruct((S, FOX_W), F32)],
        scratch_shapes=[pltpu.VMEM((8, N_FFPAD), F32)],
        compiler_params=_cp(("arbitrary",), 40 << 20),
    )(proj, proj, pff, bfp, gq, gk, bd, ex, tril)


def _pair_blk(S, off=0):
    return pl.BlockSpec((S, 128), lambda p: (0, off + p), pipeline_mode=pl.Buffered(1))


def _pair_rows(S):
    return pl.BlockSpec((None, 8, S), lambda p: (p, 0, 0), pipeline_mode=pl.Buffered(1))


def _head_masks(S):
    return lax.broadcasted_iota(jnp.int32, (S, 128), 1) < HEAD_DIM


_EXP_ZERO = 104.0


def _spread_heads(x):
    src = lax.broadcasted_iota(jnp.int32, (128, 128), 0)
    return (_mm3(x, (src == 0).astype(BF16)), _mm3(x, (src == HEAD_DIM).astype(BF16)))


def _score_bounds(q, k):
    same_head = ((lax.broadcasted_iota(jnp.int32, (128, 128), 0) < HEAD_DIM)
                 == (lax.broadcasted_iota(jnp.int32, (128, 128), 1) < HEAD_DIM)).astype(BF16)

    def max_norm2(x):
        xf = x.astype(F32)
        return jnp.max(_mm2(xf * xf, same_head), axis=0, keepdims=True)

    z = jnp.sqrt(max_norm2(q) * max_norm2(k))
    return jnp.max(z[:, 0:1]) * 1.001 + 1e-3, jnp.max(z[:, 64:65]) * 1.001 + 1e-3


def _for_tiles_back(i, n, tiles_fn):
    def two(t, c):
        tiles_fn([i - 1 - 2 * t, i - 2 - 2 * t])
        return c

    lax.fori_loop(0, lax.shift_right_logical(n, 1), two, 0)

    @pl.when((n & 1) == 1)
    def _():
        tiles_fn([i - n])


def _fox_tiles_back(cr_ref, i, r0, T, zba, zbb):
    cf = cr_ref[:, pl.ds(r0, 128)]
    cfa = jnp.max(cf[0:1, 0:1])
    cfb = jnp.max(cf[1:2, 0:1])

    def alive(j):
        cl = cr_ref[:, pl.ds(pl.multiple_of(j * T + (T - 128), 128), 128)]
        gap_a = cfa - jnp.max(cl[0:1, 127:128])
        gap_b = cfb - jnp.max(cl[1:2, 127:128])
        return jnp.maximum(2.0 * zba + gap_a, 2.0 * zbb + gap_b) > -_EXP_ZERO

    def cond(st):
        return (st[0] < i) & st[1]

    def step(st):
        return st[0] + 1, alive(jnp.maximum(i - 2 - st[0], 0))

    n, _ = lax.while_loop(cond, step, (jnp.int32(0), alive(jnp.maximum(i - 1, 0))))
    return n


def _fox_fwd(qs, kn, proj, cqb, crow4, ride=None):
    S = qs.shape[0]
    T = min(_T, S)
    nq = S // T
    n_pairs = FOX_W // 128

    def body(*refs):
        if ride is None:
            q_ref, k_ref, v_ref, cq_ref, cr_ref, o_ref, lse_ref = refs[:7]
            qa, qb, vta, vtb, cka, ckb, ma, mb, acca, accb = refs[7:]
        else:
            q_ref, k_ref, v_ref, cq_ref, cr_ref, wa_ref, wb_ref, o_ref, lse_ref, ga_ref, gb_ref = refs[:11]
            qa, qb, vta, vtb, cka, ckb, ma, mb, acca, accb = refs[11:21]
            xrefs = (wa_ref, wb_ref, ga_ref, gb_ref) + tuple(refs[21:])

            @pl.when(pl.program_id(0) == 0)
            def _():
                _start_exchange("gather", *xrefs)

        lane_s = _head_masks(S)
        q = q_ref[...]
        zq = jnp.zeros_like(q)
        qa[...] = jnp.where(lane_s, q, zq)
        qb[...] = jnp.where(lane_s, zq, q)
        cq = cq_ref[...]
        cka[...], ckb[...] = _spread_heads(cq)
        lse_ref[...] = jnp.zeros((8, S), F32)
        row_t = lax.broadcasted_iota(jnp.int32, (128, T), 0) < HEAD_DIM
        zba, zbb = _score_bounds(q, k_ref[...])

        def prep(c, carry):
            c0 = pl.multiple_of(c * T, T)
            vt = v_ref[pl.ds(c0, T), :].T
            vta[:, pl.ds(c0, T)] = jnp.where(row_t, vt, 1.0).astype(BF16)
            vtb[:, pl.ds(c0, T)] = jnp.where(row_t, 1.0, vt).astype(BF16)
            return carry

        lax.fori_loop(0, nq, prep, 0)
        causal = (lax.broadcasted_iota(jnp.int32, (T, T), 0) <= lax.broadcasted_iota(jnp.int32, (T, T), 1))

        heads = ((qa, vta, cka, ma, acca), (qb, vtb, ckb, mb, accb))

        def kv(js, r0, masked):
            cr = cr_ref[:, pl.ds(r0, T)]
            c0s = [pl.multiple_of(j * T, T) for j in js]
            ks = [k_ref[pl.ds(c0, T), :] for c0 in c0s]
            ss = []
            for h, (qr, _, ckr, _, _) in enumerate(heads):
                qh = qr[pl.ds(r0, T), :]
                row = []
                for k, c0 in zip(ks, c0s):
                    s = _dot_nt(k, qh) + cr[h:h + 1, :] - jnp.tile(ckr[pl.ds(c0, T), :], (1, T // 128))
                    row.append(jnp.where(causal, s, NEG) if masked else s)
                ss.append(row)
            ms = []
            for row, (_, _, _, mr, _) in zip(ss, heads):
                top = row[0]
                for s in row[1:]:
                    top = jnp.maximum(top, s)
                m_old = mr[0:1, :]
                ms.append((m_old, jnp.maximum(m_old, jnp.max(top, axis=0, keepdims=True))))
            ps = [[jnp.exp(s - m_new).astype(BF16) for s in row] for row, (_, m_new) in zip(ss, ms)]
            pvs = []
            for row, (_, vr, _, _, _) in zip(ps, heads):
                pv = _dot(vr[:, pl.ds(c0s[0], T)], row[0])
                for p, c0 in zip(row[1:], c0s[1:]):
                    pv = pv + _dot(vr[:, pl.ds(c0, T)], p)
                pvs.append(pv)
            for pv, (m_old, m_new), (_, _, _, mr, ar) in zip(pvs, ms, heads):
                ar[...] = jnp.exp(m_old - m_new) * ar[...] + pv
                mr[0:1, :] = m_new

        def qblk(i, carry):
            r0 = pl.multiple_of(i * T, T)
            ma[...] = jnp.full((8, T), NEG, F32)
            mb[...] = jnp.full((8, T), NEG, F32)
            acca[...] = jnp.zeros((128, T), F32)
            accb[...] = jnp.zeros((128, T), F32)
            kv([i], r0, True)
            done = _fox_tiles_back(cr_ref, i, r0, T, zba, zbb)
            _for_tiles_back(i, done, lambda js: kv(js, r0, False))
            aa = acca[...]
            ab = accb[...]
            la = aa[64:65, :]
            lb = ab[0:1, :]
            o_ref[pl.ds(r0, T), :] = jnp.where(row_t, aa / la, ab / lb).T
            lse_ref[0:1, pl.ds(r0, T)] = ma[0:1, :] + jnp.log(la)
            lse_ref[1:2, pl.ds(r0, T)] = mb[0:1, :] + jnp.log(lb)
            lse_ref[2:3, pl.ds(r0, T)] = jnp.broadcast_to(done.astype(F32), (1, T))
            return carry

        lax.fori_loop(0, nq, qblk, 0)
        if ride is not None:
            @pl.when(pl.program_id(0) == n_pairs - 1)
            def _():
                _wait_exchange("gather", *xrefs)

    extra = () if ride is None else tuple(ride)
    return pl.pallas_call(
        body, name="fox_fwd" if ride is None else "fox_fwd_gather",
        grid=(n_pairs,),
        in_specs=[_pair_blk(S), _pair_blk(S), _pair_blk(S, OFF_FV // 128), _pair_blk(S), _pair_rows(S)]
        + [_ANY] * len(extra),
        out_specs=[_pair_blk(S), _pair_rows(S)] + [_ANY] * len(extra),
        out_shape=[jax.ShapeDtypeStruct((S, FOX_W), F32), jax.ShapeDtypeStruct((n_pairs, 8, S), F32)]
        + (_exchange_out_shapes("gather", *extra) if extra else []),
        scratch_shapes=[pltpu.VMEM((S, 128), BF16)] * 2 + [pltpu.VMEM((128, S), BF16)] * 2
        + [pltpu.VMEM((S, 128), F32)] * 2 + [pltpu.VMEM((8, T), F32)] * 2 + [pltpu.VMEM((128, T), F32)] * 2
        + (_EXCHANGE_SEMS if extra else []),
        compiler_params=_cp(("arbitrary",), _VMEM_BIG),
    )(qs, kn, proj, cqb, crow4, *extra)


def _softplus_parts(z):
    e = jnp.exp(-jnp.abs(z))
    return e, jnp.maximum(z, 0.0) + jnp.log1p(e)


def _sb_fwd(proj, triu):
    S = proj.shape[0]
    T = triu.shape[0]
    nq = S // T

    def body(q_ref, k_ref, v_ref, tri_ref, o_ref, lt_ref, qa, qb, kb, vt, ra, rb, acca, accb):
        lane_s = _head_masks(S)
        q = (q_ref[...] * Q_SCALE).astype(BF16)
        zq = jnp.zeros_like(q)
        qa[...] = jnp.where(lane_s, q, zq)
        qb[...] = jnp.where(lane_s, zq, q)
        kb[...] = k_ref[...].astype(BF16)
        lt_ref[...] = jnp.zeros((8, S), F32)
        row_t = lax.broadcasted_iota(jnp.int32, (128, T), 0) < HEAD_DIM
        zba, zbb = _score_bounds(q, kb[...])

        def prep(c, carry):
            c0 = pl.multiple_of(c * T, T)
            vt[:, pl.ds(c0, T)] = v_ref[pl.ds(c0, T), :].T.astype(BF16)
            return carry

        lax.fori_loop(0, nq, prep, 0)
        strict = (lax.broadcasted_iota(jnp.int32, (T, T), 0) < lax.broadcasted_iota(jnp.int32, (T, T), 1))

        heads = ((qa, ra, acca), (qb, rb, accb))

        def kv(j, r0, masked):
            c0 = pl.multiple_of(j * T, T)
            k = kb[pl.ds(c0, T), :]
            vtt = vt[:, pl.ds(c0, T)]
            tri = tri_ref[...]
            zs = [_dot_nt(k, qr[pl.ds(r0, T), :]) for qr, _, _ in heads]
            lbs = [-_softplus_parts(z)[1] for z in zs]
            if masked:
                lbs = [jnp.where(strict, lb, 0.0) for lb in lbs]
            incs = [_mm2(lb, tri, left=True) for lb in lbs]
            rs = [r_ref[0:1, :] for _, r_ref, _ in heads]
            aas = [jnp.exp(z + inc + r) for z, inc, r in zip(zs, incs, rs)]
            if masked:
                aas = [jnp.where(strict, a, 0.0) for a in aas]
            avs = [_dot(vtt, a.astype(BF16)) for a in aas]
            for (_, r_ref, acc_ref), r, inc, av in zip(heads, rs, incs, avs):
                r_ref[0:1, :] = r + inc[0:1, :]
                acc_ref[...] = acc_ref[...] + av

        def qblk(i, carry):
            r0 = pl.multiple_of(i * T, T)
            ra[...] = jnp.zeros((8, T), F32)
            rb[...] = jnp.zeros((8, T), F32)
            acca[...] = jnp.zeros((128, T), F32)
            accb[...] = jnp.zeros((128, T), F32)
            kv(i, r0, True)

            def alive():
                return jnp.maximum(jnp.max(ra[0:1, :]) + zba, jnp.max(rb[0:1, :]) + zbb) > -_EXP_ZERO

            def cond(st):
                return (st[0] < i) & st[1]

            def step(st):
                kv(i - 1 - st[0], r0, False)
                return st[0] + 1, alive()

            done, _ = lax.while_loop(cond, step, (jnp.int32(0), alive()))
            o_ref[pl.ds(r0, T), :] = jnp.where(row_t, acca[...], accb[...]).T
            lt_ref[0:1, pl.ds(r0, T)] = ra[0:1, :]
            lt_ref[1:2, pl.ds(r0, T)] = rb[0:1, :]
            lt_ref[2:3, pl.ds(r0, T)] = jnp.broadcast_to(done.astype(F32), (1, T))
            return carry

        lax.fori_loop(0, nq, qblk, 0)

    return pl.pallas_call(
        body, name="sb_fwd",
        grid=(SB_W // 128,),
        in_specs=[_pair_blk(S, OFF_SQ // 128), _pair_blk(S, OFF_SK // 128), _pair_blk(S, OFF_SV // 128),
                  pl.BlockSpec((T, T), lambda p: (0, 0))],
        out_specs=[_pair_blk(S), _pair_rows(S)],
        out_shape=[jax.ShapeDtypeStruct((S, SB_W), F32), jax.ShapeDtypeStruct((SB_W // 128, 8, S), F32)],
        scratch_shapes=[pltpu.VMEM((S, 128), BF16)] * 3 + [pltpu.VMEM((128, S), BF16)]
        + [pltpu.VMEM((8, T), F32)] * 2 + [pltpu.VMEM((128, T), F32)] * 2,
        compiler_params=_cp(("arbitrary",), _VMEM_BIG),
    )(proj, proj, proj, triu)


def _pool_window_lanes(shape):
    lane = lax.broadcasted_iota(jnp.int32, shape, 1)
    return jnp.where(lane < 64, 2, jnp.where(lane < 128, 4, jnp.where(lane < 192, 8, 16)))


def _pool_fwd(proj):
    S = proj.shape[0]

    def body(x_ref, o_ref):
        x = x_ref[...]
        t = lax.broadcasted_iota(jnp.int32, x.shape, 0)
        lane = lax.broadcasted_iota(jnp.int32, x.shape, 1)

        def back(a, k):
            return jnp.where(t >= k, pltpu.roll(a, k, 0), 0.0)

        s1 = x + back(x, 1)
        s2 = s1 + back(s1, 2)
        s4 = s2 + back(s2, 4)
        s8 = s4 + back(s4, 8)
        win = jnp.where(lane < 64, s1, jnp.where(lane < 128, s2, jnp.where(lane < 192, s4, s8)))
        cnt = jnp.minimum(t + 1, _pool_window_lanes(x.shape)).astype(F32)
        o_ref[...] = win / cnt - x

    return pl.pallas_call(
        body, name="pool_fwd",
        grid=(1,),
        in_specs=[pl.BlockSpec((S, POOL_W), lambda i: (0, OFF_PX // POOL_W))],
        out_specs=pl.BlockSpec((S, POOL_W), lambda i: (0, 0)),
        out_shape=jax.ShapeDtypeStruct((S, POOL_W), F32),
        compiler_params=_cp(("arbitrary",), _VMEM_BIG),
    )(proj)


def _silu(g):
    return g * _sigmoid(g)


def _mix_out(fo, so, pooled, proj, wbd, scale, wout, x):
    S, D = x.shape
    tm = min(256, S)

    def body(fo_ref, fg_ref, so_ref, sg_ref, pl_ref, pg_ref, wbd_ref, sc_ref, w_ref, x_ref, y_ref, mx_ref):
        mx_ref[:, 0:FOX_W] = (fo_ref[...] * _silu(fg_ref[...])).astype(BF16)
        yp = _dot(pl_ref[...].astype(BF16), wbd_ref[...]) * sc_ref[...]
        mx_ref[:, FOX_W:FOX_W + POOL_W] = (yp * _silu(pg_ref[...])).astype(BF16)
        mx_ref[:, FOX_W + POOL_W:D_MIX] = (so_ref[...] * _silu(sg_ref[...])).astype(BF16)
        y_ref[...] = x_ref[...] + _dot(mx_ref[...], w_ref[...])

    return pl.pallas_call(
        body, name="mix_out",
        grid=(S // tm,),
        in_specs=[pl.BlockSpec((tm, FOX_W), lambda i: (i, 0)),
                  pl.BlockSpec((tm, FOX_W), lambda i: (i, OFF_FG // FOX_W)),
                  pl.BlockSpec((tm, SB_W), lambda i: (i, 0)),
                  pl.BlockSpec((tm, SB_W), lambda i: (i, OFF_SG // SB_W)),
                  pl.BlockSpec((tm, POOL_W), lambda i: (i, 0)),
                  pl.BlockSpec((tm, POOL_W), lambda i: (i, OFF_PG // POOL_W)),
                  pl.BlockSpec((POOL_W, POOL_W), lambda i: (0, 0)),
                  pl.BlockSpec((1, POOL_W), lambda i: (0, 0)),
                  pl.BlockSpec((D_MIX, D), lambda i: (0, 0)),
                  pl.BlockSpec((tm, D), lambda i: (i, 0))],
        out_specs=[pl.BlockSpec((tm, D), lambda i: (i, 0)), pl.BlockSpec((tm, D_MIX), lambda i: (i, 0))],
        out_shape=[jax.ShapeDtypeStruct((S, D), F32), jax.ShapeDtypeStruct((S, D_MIX), BF16)],
        compiler_params=_cp(("parallel",), 40 << 20),
    )(fo, proj, so, proj, pooled, proj, wbd, scale, wout, x)


def _loss_head(y, target):
    S, D = y.shape
    tm = min(_TM, S)

    def body(y_ref, t_ref, dy_ref, ls_ref):
        @pl.when(pl.program_id(0) == 0)
        def _():
            ls_ref[...] = jnp.zeros_like(ls_ref)

        e = y_ref[...] - t_ref[...]
        dy_ref[...] = e * (1.0 / D)
        ls_ref[...] = ls_ref[...] + jnp.sum(e * e) * (0.5 / D)

    dy, ls = pl.pallas_call(
        body, name="loss_head",
        grid=(S // tm,),
        in_specs=[pl.BlockSpec((tm, D), lambda i: (i, 0)), pl.BlockSpec((tm, D), lambda i: (i, 0))],
        out_specs=[pl.BlockSpec((tm, D), lambda i: (i, 0)), pl.BlockSpec((8, 128), lambda i: (0, 0))],
        out_shape=[jax.ShapeDtypeStruct((S, D), F32), jax.ShapeDtypeStruct((8, 128), F32)],
        compiler_params=_cp(("arbitrary",), 40 << 20),
    )(y, target)
    return dy, ls[0, 0]


def _dsilu(g):
    s = _sigmoid(g)
    return s * (1.0 + g * (1.0 - s))


def _gate_bwd(dy, wout, fo, so, pooled, proj, wbd, scale):
    S, D = dy.shape
    tm = min(256, S)

    def body(dy_ref, w_ref, fo_ref, fg_ref, so_ref, sg_ref, pl_ref, pg_ref, wbd_ref, sc_ref,
             dfo_ref, dfg_ref, dso_ref, dsg_ref, dpg_ref, dpl_ref, dsc_ref, dwbd_ref):
        @pl.when(pl.program_id(0) == 0)
        def _():
            dsc_ref[...] = jnp.zeros_like(dsc_ref)
            dwbd_ref[...] = jnp.zeros_like(dwbd_ref)

        dm = _dot_nt(dy_ref[...].astype(BF16), w_ref[...])
        dmf = dm[:, 0:FOX_W]
        dmp = dm[:, FOX_W:FOX_W + POOL_W]
        dms = dm[:, FOX_W + POOL_W:D_MIX]
        fg = fg_ref[...]
        dfo_ref[...] = dmf * _silu(fg)
        dfg_ref[...] = (dmf * fo_ref[...] * _dsilu(fg)).astype(BF16)
        sg = sg_ref[...]
        dso_ref[...] = dms * _silu(sg)
        dsg_ref[...] = (dms * so_ref[...] * _dsilu(sg)).astype(BF16)
        pg = pg_ref[...]
        plb = pl_ref[...].astype(BF16)
        yw = _dot(plb, wbd_ref[...])
        sc = sc_ref[...]
        dpg_ref[...] = (dmp * (yw * sc) * _dsilu(pg)).astype(BF16)
        dys = dmp * _silu(pg)
        dsc_ref[...] = dsc_ref[...] + jnp.sum(dys * yw, axis=0, keepdims=True)
        dyw = (dys * sc).astype(BF16)
        dpl_ref[...] = _dot_nt(dyw, wbd_ref[...])
        dwbd_ref[...] = dwbd_ref[...] + _dot_tn(plb, dyw)

    return pl.pallas_call(
        body, name="gate_bwd",
        grid=(S // tm,),
        in_specs=[pl.BlockSpec((tm, D), lambda i: (i, 0)),
                  pl.BlockSpec((D_MIX, D), lambda i: (0, 0)),
                  pl.BlockSpec((tm, FOX_W), lambda i: (i, 0)),
                  pl.BlockSpec((tm, FOX_W), lambda i: (i, OFF_FG // FOX_W)),
                  pl.BlockSpec((tm, SB_W), lambda i: (i, 0)),
                  pl.BlockSpec((tm, SB_W), lambda i: (i, OFF_SG // SB_W)),
                  pl.BlockSpec((tm, POOL_W), lambda i: (i, 0)),
                  pl.BlockSpec((tm, POOL_W), lambda i: (i, OFF_PG // POOL_W)),
                  pl.BlockSpec((POOL_W, POOL_W), lambda i: (0, 0)),
                  pl.BlockSpec((1, POOL_W), lambda i: (0, 0))],
        out_specs=[pl.BlockSpec((tm, FOX_W), lambda i: (i, 0)),
                   pl.BlockSpec((tm, FOX_W), lambda i: (i, 0)),
                   pl.BlockSpec((tm, SB_W), lambda i: (i, 0)),
                   pl.BlockSpec((tm, SB_W), lambda i: (i, 0)),
                   pl.BlockSpec((tm, POOL_W), lambda i: (i, 0)),
                   pl.BlockSpec((tm, POOL_W), lambda i: (i, 0)),
                   pl.BlockSpec((1, POOL_W), lambda i: (0, 0)),
                   pl.BlockSpec((POOL_W, POOL_W), lambda i: (0, 0))],
        out_shape=[jax.ShapeDtypeStruct((S, FOX_W), F32), jax.ShapeDtypeStruct((S, FOX_W), BF16),
                   jax.ShapeDtypeStruct((S, SB_W), F32), jax.ShapeDtypeStruct((S, SB_W), BF16),
                   jax.ShapeDtypeStruct((S, POOL_W), BF16), jax.ShapeDtypeStruct((S, POOL_W), F32),
                   jax.ShapeDtypeStruct((1, POOL_W), F32), jax.ShapeDtypeStruct((POOL_W, POOL_W), F32)],
        compiler_params=_cp(("arbitrary",), 40 << 20),
    )(dy, wout, fo, proj, so, proj, pooled, proj, wbd, scale)


def _matmul_tn(a, b, name):
    S, M = a.shape
    N = b.shape[1]
    tk = min(_TM, S)
    tn = min(512, N)

    def body(a_ref, b_ref, o_ref):
        @pl.when(pl.program_id(1) == 0)
        def _():
            o_ref[...] = jnp.zeros_like(o_ref)

        o_ref[...] = o_ref[...] + _dot_tn(a_ref[...].astype(BF16), b_ref[...].astype(BF16))

    return pl.pallas_call(
        body, name=name,
        grid=(N // tn, S // tk),
        in_specs=[pl.BlockSpec((tk, M), lambda j, k: (k, 0)), pl.BlockSpec((tk, tn), lambda j, k: (k, j))],
        out_specs=pl.BlockSpec((M, tn), lambda j, k: (0, j)),
        out_shape=jax.ShapeDtypeStruct((M, N), F32),
        compiler_params=_cp(("parallel", "arbitrary"), 40 << 20),
    )(a, b)


def _pool_bwd(dpooled):
    S = dpooled.shape[0]

    def body(d_ref, o_ref):
        d = d_ref[...]
        t = lax.broadcasted_iota(jnp.int32, d.shape, 0)
        lane = lax.broadcasted_iota(jnp.int32, d.shape, 1)
        cnt = jnp.minimum(t + 1, _pool_window_lanes(d.shape)).astype(F32)
        u = d / cnt

        def fwd(a, k):
            return jnp.where(t < S - k, pltpu.roll(a, S - k, 0), 0.0)

        s1 = u + fwd(u, 1)
        s2 = s1 + fwd(s1, 2)
        s4 = s2 + fwd(s2, 4)
        s8 = s4 + fwd(s4, 8)
        win = jnp.where(lane < 64, s1, jnp.where(lane < 128, s2, jnp.where(lane < 192, s4, s8)))
        o_ref[...] = (win - d).astype(BF16)

    return pl.pallas_call(
        body, name="pool_bwd",
        grid=(1,),
        in_specs=[pl.BlockSpec((S, POOL_W), lambda i: (0, 0))],
        out_specs=pl.BlockSpec((S, POOL_W), lambda i: (0, 0)),
        out_shape=jax.ShapeDtypeStruct((S, POOL_W), BF16),
        compiler_params=_cp(("arbitrary",), _VMEM_BIG),
    )(dpooled)


def _fox_bwd(qs, kn, proj, dfo, fo, lse, cqb, crow4, ride=None):
    S = qs.shape[0]
    T = min(_T, S)
    nq = S // T
    n_pairs = FOX_W // 128

    def body(*refs):
        if ride is None:
            q_ref, k_ref, v_ref, do_ref, o_ref, lse_ref, cq_ref, cr_ref = refs[:8]
            dq_ref, dk_ref, dv_ref, dck_ref, dcq_ref = refs[8:13]
            scr = refs[13:]
        else:
            q_ref, k_ref, v_ref, do_ref, o_ref, lse_ref, cq_ref, cr_ref, pa_ref, pb_ref = refs[:10]
            dq_ref, dk_ref, dv_ref, dck_ref, dcq_ref, ra_ref, rb_ref = refs[10:17]
            scr = refs[17:32]
            xrefs = (pa_ref, pb_ref, ra_ref, rb_ref) + tuple(refs[32:])

            @pl.when(pl.program_id(0) == 0)
            def _():
                _start_exchange("scatter", *xrefs)

        qa, qb, kta, ktb, vb, doa, dob, cka, ckb, dcka, dckb, dva, dqt, dcqa, dcqb = scr
        lane_s = _head_masks(S)
        q = q_ref[...]
        zq = jnp.zeros_like(q)
        qa[...] = jnp.where(lane_s, q, zq)
        qb[...] = jnp.where(lane_s, zq, q)
        vb[...] = v_ref[...].astype(BF16)
        do = do_ref[...].astype(BF16)
        doa[...] = jnp.where(lane_s, do, zq)
        dob[...] = jnp.where(lane_s, zq, do)
        cq = cq_ref[...]
        cka[...], ckb[...] = _spread_heads(cq)
        zs = jnp.zeros((S, 128), F32)
        dk_ref[...] = zs
        dva[...] = zs
        dcka[...] = zs
        dckb[...] = zs
        dcq_ref[...] = jnp.zeros((8, S), F32)
        row_t = lax.broadcasted_iota(jnp.int32, (128, T), 0) < HEAD_DIM

        def prep(c, carry):
            c0 = pl.multiple_of(c * T, T)
            kt = k_ref[pl.ds(c0, T), :].astype(F32).T
            kta[:, pl.ds(c0, T)] = jnp.where(row_t, kt, 0.0).astype(BF16)
            ktb[:, pl.ds(c0, T)] = jnp.where(row_t, 0.0, kt).astype(BF16)
            return carry

        lax.fori_loop(0, nq, prep, 0)
        causal = (lax.broadcasted_iota(jnp.int32, (T, T), 0) <= lax.broadcasted_iota(jnp.int32, (T, T), 1))

        heads = ((qa, kta, doa, cka, dcka, dcqa), (qb, ktb, dob, ckb, dckb, dcqb))

        def kv(js, r0, lss, dls, masked):
            cr = cr_ref[:, pl.ds(r0, T)]
            c0s = [pl.multiple_of(j * T, T) for j in js]
            ks = [k_ref[pl.ds(c0, T), :] for c0 in c0s]
            vs = [vb[pl.ds(c0, T), :] for c0 in c0s]
            qhs = [hd[0][pl.ds(r0, T), :] for hd in heads]
            dohs = [hd[2][pl.ds(r0, T), :] for hd in heads]
            ss = []
            for h, hd in enumerate(heads):
                row = []
                for k, c0 in zip(ks, c0s):
                    s = _dot_nt(k, qhs[h]) + cr[h:h + 1, :] - jnp.tile(hd[3][pl.ds(c0, T), :], (1, T // 128))
                    row.append(jnp.where(causal, s, NEG) if masked else s)
                ss.append(row)
            ps = [[jnp.exp(s - lss[h]) for s in row] for h, row in enumerate(ss)]
            dps = [[_dot_nt(v, dohs[h]) for v in vs] for h in range(2)]
            dss = [[p * (dp - dls[h]) for p, dp in zip(ps[h], dps[h])] for h in range(2)]
            pbs = [[p.astype(BF16) for p in row] for row in ps]
            dsbs = [[ds.astype(BF16) for ds in row] for row in dss]
            for t, c0 in enumerate(c0s):
                dva[pl.ds(c0, T), :] = dva[pl.ds(c0, T), :] + (_dot(pbs[0][t], dohs[0]) + _dot(pbs[1][t], dohs[1]))
                dk_ref[pl.ds(c0, T), :] = dk_ref[pl.ds(c0, T), :] + (_dot(dsbs[0][t], qhs[0]) + _dot(dsbs[1][t], qhs[1]))
            dq = None
            for h, hd in enumerate(heads):
                for t, c0 in enumerate(c0s):
                    term = _dot(hd[1][:, pl.ds(c0, T)], dsbs[h][t])
                    dq = term if dq is None else dq + term
            dqt[...] = dqt[...] + dq
            for h, hd in enumerate(heads):
                col = jnp.sum(dss[h][0], axis=0, keepdims=True)
                for ds in dss[h][1:]:
                    col = col + jnp.sum(ds, axis=0, keepdims=True)
                hd[5][0:1, :] = hd[5][0:1, :] + col
                for ds, c0 in zip(dss[h], c0s):
                    fold = ds[:, 0:128]
                    for u in range(1, T // 128):
                        fold = fold + ds[:, 128 * u:128 * (u + 1)]
                    hd[4][pl.ds(c0, T), :] = hd[4][pl.ds(c0, T), :] - fold

        def qblk(i, carry):
            r0 = pl.multiple_of(i * T, T)
            dt = (do_ref[pl.ds(r0, T), :] * o_ref[pl.ds(r0, T), :]).T
            dla = jnp.sum(jnp.where(row_t, dt, 0.0), axis=0, keepdims=True)
            dlb = jnp.sum(jnp.where(row_t, 0.0, dt), axis=0, keepdims=True)
            ls = lse_ref[:, pl.ds(r0, T)]
            lss = (ls[0:1, :], ls[1:2, :])
            back = jnp.max(ls[2:3, :]).astype(jnp.int32)
            dqt[...] = jnp.zeros((128, T), F32)
            dcqa[...] = jnp.zeros((8, T), F32)
            dcqb[...] = jnp.zeros((8, T), F32)
            kv([i], r0, lss, (dla, dlb), True)
            _for_tiles_back(i, back, lambda js: kv(js, r0, lss, (dla, dlb), False))
            dq_ref[pl.ds(r0, T), :] = dqt[...].T
            dcq_ref[0:1, pl.ds(r0, T)] = dcqa[0:1, :]
            dcq_ref[1:2, pl.ds(r0, T)] = dcqb[0:1, :]
            return carry

        lax.fori_loop(0, nq, qblk, 0)
        dv_ref[...] = dva[...].astype(BF16)
        dck_ref[...] = jnp.where(lane_s, jnp.sum(dcka[...], axis=1, keepdims=True),
                                 jnp.sum(dckb[...], axis=1, keepdims=True))
        if ride is not None:
            @pl.when(pl.program_id(0) == n_pairs - 1)
            def _():
                _wait_exchange("scatter", *xrefs)

    extra = () if ride is None else tuple(ride)
    return pl.pallas_call(
        body, name="fox_bwd" if ride is None else "fox_bwd_exchange",
        grid=(n_pairs,),
        in_specs=[_pair_blk(S), _pair_blk(S), _pair_blk(S, OFF_FV // 128), _pair_blk(S), _pair_blk(S),
                  _pair_rows(S), _pair_blk(S), _pair_rows(S)] + [_ANY] * len(extra),
        out_specs=[_pair_blk(S), _pair_blk(S), _pair_blk(S), _pair_blk(S), _pair_rows(S)] + [_ANY] * len(extra),
        out_shape=[jax.ShapeDtypeStruct((S, FOX_W), F32), jax.ShapeDtypeStruct((S, FOX_W), F32),
                   jax.ShapeDtypeStruct((S, FOX_W), BF16), jax.ShapeDtypeStruct((S, FOX_W), F32),
                   jax.ShapeDtypeStruct((n_pairs, 8, S), F32)]
        + (_exchange_out_shapes("scatter", *extra) if extra else []),
        scratch_shapes=[pltpu.VMEM((S, 128), BF16)] * 2 + [pltpu.VMEM((128, S), BF16)] * 2
        + [pltpu.VMEM((S, 128), BF16)] * 3 + [pltpu.VMEM((S, 128), F32)] * 5
        + [pltpu.VMEM((128, T), F32)] + [pltpu.VMEM((8, T), F32)] * 2
        + (_EXCHANGE_SEMS if extra else []),
        compiler_params=_cp(("arbitrary",), _VMEM_BIG),
    )(qs, kn, proj, dfo, fo, lse, cqb, crow4, *extra)


def _sb_bwd(proj, dso, ltot, tril):
    S = proj.shape[0]
    T = tril.shape[0]
    nq = S // T

    def body(q_ref, k_ref, v_ref, do_ref, lt_ref, tri_ref, dq_ref, dk_ref, dv_ref,
             qa, qb, k2, kta, ktb, vb, doa, dob, dka, dva, dqt, ra, rb, ga, gb):
        lane_s = _head_masks(S)
        q = (q_ref[...] * Q_SCALE).astype(BF16)
        zq = jnp.zeros_like(q)
        qa[...] = jnp.where(lane_s, q, zq)
        qb[...] = jnp.where(lane_s, zq, q)
        k2[...] = k_ref[...].astype(BF16)
        vb[...] = v_ref[...].astype(BF16)
        do = do_ref[...].astype(BF16)
        doa[...] = jnp.where(lane_s, do, zq)
        dob[...] = jnp.where(lane_s, zq, do)
        dka[...] = jnp.zeros((S, 128), F32)
        dva[...] = jnp.zeros((S, 128), F32)
        row_t = lax.broadcasted_iota(jnp.int32, (128, T), 0) < HEAD_DIM

        def prep(c, carry):
            c0 = pl.multiple_of(c * T, T)
            kt = k_ref[pl.ds(c0, T), :].T
            kta[:, pl.ds(c0, T)] = jnp.where(row_t, kt, 0.0).astype(BF16)
            ktb[:, pl.ds(c0, T)] = jnp.where(row_t, 0.0, kt).astype(BF16)
            return carry

        lax.fori_loop(0, nq, prep, 0)
        strict = (lax.broadcasted_iota(jnp.int32, (T, T), 0) < lax.broadcasted_iota(jnp.int32, (T, T), 1))

        heads = ((qa, kta, doa, ra, ga), (qb, ktb, dob, rb, gb))

        def kv(j, r0, lta, ltb, masked):
            c0 = pl.multiple_of(j * T, T)
            kfull = k2[pl.ds(c0, T), :]
            v = vb[pl.ds(c0, T), :]
            tri = tri_ref[...]
            lts = (lta, ltb)
            qhs = [hd[0][pl.ds(r0, T), :] for hd in heads]
            dohs = [hd[2][pl.ds(r0, T), :] for hd in heads]
            zs = [_dot_nt(kfull, qh) for qh in qhs]
            das = [_dot_nt(v, doh) for doh in dohs]
            es, lbs = [], []
            for z in zs:
                e, sp = _softplus_parts(z)
                es.append(e)
                lbs.append(jnp.where(strict, -sp, 0.0) if masked else -sp)
            pres = [_mm2(lb, tri, left=True) for lb in lbs]
            rs = [hd[3][0:1, :] for hd in heads]
            aas = [jnp.exp(z + lb + ((lt - r) - pre)) for z, lb, lt, r, pre in zip(zs, lbs, lts, rs, pres)]
            if masked:
                aas = [jnp.where(strict, a, 0.0) for a in aas]
            gs = [a * da for a, da in zip(aas, das)]
            gpres = [_mm2(g, tri, left=True) for g in gs]
            gcs = [hd[4][0:1, :] for hd in heads]
            dzbs = []
            for z, e, g, gpre, gc in zip(zs, es, gs, gpres, gcs):
                inv = 1.0 / (1.0 + e)
                pos = z >= 0.0
                sig = jnp.where(pos, 1.0, e) * inv
                oms = jnp.where(pos, e, 1.0) * inv
                dz = g * oms - sig * (gc + (gpre - g))
                if masked:
                    dz = jnp.where(strict, dz, 0.0)
                dzbs.append(dz.astype(BF16))
            dqt[...] = dqt[...] + (_dot(heads[0][1][:, pl.ds(c0, T)], dzbs[0]) + _dot(heads[1][1][:, pl.ds(c0, T)], dzbs[1]))
            dka[pl.ds(c0, T), :] = dka[pl.ds(c0, T), :] + (_dot(dzbs[0], qhs[0]) + _dot(dzbs[1], qhs[1]))
            dva[pl.ds(c0, T), :] = dva[pl.ds(c0, T), :] + (_dot(aas[0].astype(BF16), dohs[0]) + _dot(aas[1].astype(BF16), dohs[1]))
            for hd, r, pre, gc, gpre in zip(heads, rs, pres, gcs, gpres):
                hd[3][0:1, :] = r + pre[T - 1:T, :]
                hd[4][0:1, :] = gc + gpre[T - 1:T, :]

        def qblk(i, carry):
            r0 = pl.multiple_of(i * T, T)
            lt = lt_ref[:, pl.ds(r0, T)]
            lta = lt[0:1, :]
            ltb = lt[1:2, :]
            back = jnp.max(lt[2:3, :]).astype(jnp.int32)
            zt = jnp.zeros((8, T), F32)
            dqt[...] = jnp.zeros((128, T), F32)
            ra[...] = zt
            rb[...] = zt
            ga[...] = zt
            gb[...] = zt

            def inner(j, c):
                kv(j, r0, lta, ltb, False)
                return c

            lax.fori_loop(i - back, i, inner, 0)
            kv(i, r0, lta, ltb, True)
            dq_ref[pl.ds(r0, T), :] = (dqt[...] * Q_SCALE).T.astype(BF16)
            return carry

        lax.fori_loop(0, nq, qblk, 0)
        dk_ref[...] = dka[...].astype(BF16)
        dv_ref[...] = dva[...].astype(BF16)

    return pl.pallas_call(
        body, name="sb_bwd",
        grid=(SB_W // 128,),
        in_specs=[_pair_blk(S, OFF_SQ // 128), _pair_blk(S, OFF_SK // 128), _pair_blk(S, OFF_SV // 128),
                  _pair_blk(S), _pair_rows(S), pl.BlockSpec((T, T), lambda p: (0, 0))],
        out_specs=[_pair_blk(S), _pair_blk(S), _pair_blk(S)],
        out_shape=[jax.ShapeDtypeStruct((S, SB_W), BF16)] * 3,
        scratch_shapes=([pltpu.VMEM((S, 128), BF16)] * 3 + [pltpu.VMEM((128, S), BF16)] * 2
                        + [pltpu.VMEM((S, 128), BF16)] * 3 + [pltpu.VMEM((S, 128), F32)] * 2
                        + [pltpu.VMEM((128, T), F32)] + [pltpu.VMEM((8, T), F32)] * 4),
        compiler_params=_cp(("arbitrary",), _VMEM_BIG),
    )(proj, proj, proj, dso, ltot, tril)


def _head_norm_bwd(x, g, dy, bd):
    ss = _mm2(x * x, bd)
    r = lax.rsqrt(ss * (1.0 / HEAD_DIM) + EPS)
    xr = x * r
    gdy = g * dy
    m = _mm2(xr * gdy, bd) * (1.0 / HEAD_DIM)
    return r * (gdy - xr * m), dy * xr


def _qk_bwd(dqs, dkn, proj, pff, bfp, gq, gk, bd, dccol, triu):
    S = proj.shape[0]
    T = triu.shape[0]
    n = S // T
    rev = lambda col: (lambda i: (n - 1 - i, col))

    def body(dq_ref, dk_ref, q_ref, k_ref, ff_ref, b_ref, gq_ref, gk_ref, bd_ref, dc_ref, tri_ref,
             dfq_ref, dfk_ref, dff_ref, dgq_ref, dgk_ref, dbf_ref, carry):
        @pl.when(pl.program_id(0) == 0)
        def _():
            carry[...] = jnp.zeros_like(carry)
            dgq_ref[...] = jnp.zeros_like(dgq_ref)
            dgk_ref[...] = jnp.zeros_like(dgk_ref)
            dbf_ref[...] = jnp.zeros_like(dbf_ref)

        bdv = bd_ref[...]
        dxq, gq_rows = _head_norm_bwd(q_ref[...], gq_ref[...], dq_ref[...] * Q_SCALE, bdv)
        dfq_ref[...] = dxq.astype(BF16)
        dgq_ref[...] = dgq_ref[...] + jnp.sum(gq_rows, axis=0, keepdims=True)
        dxk, gk_rows = _head_norm_bwd(k_ref[...], gk_ref[...], dk_ref[...], bdv)
        dfk_ref[...] = dxk.astype(BF16)
        dgk_ref[...] = dgk_ref[...] + jnp.sum(gk_rows, axis=0, keepdims=True)
        dlf = _mm3(dc_ref[...], tri_ref[...], left=True) + carry[0:1, :]
        carry[0:1, :] = dlf[0:1, :]
        u = ff_ref[...] + b_ref[...]
        lane = lax.broadcasted_iota(jnp.int32, u.shape, 1)
        dff = jnp.where(lane < N_FF, dlf * _sigmoid(-u), 0.0)
        dff_ref[...] = dff.astype(BF16)
        dbf_ref[...] = dbf_ref[...] + jnp.sum(dff, axis=0, keepdims=True)

    return pl.pallas_call(
        body, name="qk_bwd",
        grid=(n,),
        in_specs=[pl.BlockSpec((T, FOX_W), rev(0)), pl.BlockSpec((T, FOX_W), rev(0)),
                  pl.BlockSpec((T, FOX_W), rev(OFF_FQ // FOX_W)), pl.BlockSpec((T, FOX_W), rev(OFF_FK // FOX_W)),
                  pl.BlockSpec((T, N_FFPAD), rev(0)),
                  pl.BlockSpec((1, N_FFPAD), lambda i: (0, 0)),
                  pl.BlockSpec((1, FOX_W), lambda i: (0, 0)), pl.BlockSpec((1, FOX_W), lambda i: (0, 0)),
                  pl.BlockSpec((FOX_W, FOX_W), lambda i: (0, 0)),
                  pl.BlockSpec((T, N_FFPAD), rev(0)),
                  pl.BlockSpec((T, T), lambda i: (0, 0))],
        out_specs=[pl.BlockSpec((T, FOX_W), rev(0)), pl.BlockSpec((T, FOX_W), rev(0)),
                   pl.BlockSpec((T, N_FFPAD), rev(0)),
                   pl.BlockSpec((1, FOX_W), lambda i: (0, 0)), pl.BlockSpec((1, FOX_W), lambda i: (0, 0)),
                   pl.BlockSpec((1, N_FFPAD), lambda i: (0, 0))],
        out_shape=[jax.ShapeDtypeStruct((S, FOX_W), BF16), jax.ShapeDtypeStruct((S, FOX_W), BF16),
                   jax.ShapeDtypeStruct((S, N_FFPAD), BF16),
                   jax.ShapeDtypeStruct((1, FOX_W), F32), jax.ShapeDtypeStruct((1, FOX_W), F32),
                   jax.ShapeDtypeStruct((1, N_FFPAD), F32)],
        scratch_shapes=[pltpu.VMEM((8, N_FFPAD), F32)],
        compiler_params=_cp(("arbitrary",), 40 << 20),
    )(dqs, dkn, proj, proj, pff, bfp, gq, gk, bd, dccol, triu)


def _inproj_bwd_dx(dpm, dff, wm, wff, x, g, dy):
    S, D = x.shape
    tm = min(256, S)

    def body(dp_ref, dff_ref, w_ref, wff_ref, x_ref, g_ref, dy_ref, dx_ref, dg_ref):
        @pl.when(pl.program_id(0) == 0)
        def _():
            dg_ref[...] = jnp.zeros_like(dg_ref)

        dh = _dot_nt(dp_ref[...], w_ref[...]) + _dot_nt(dff_ref[...], wff_ref[...])
        xv = x_ref[...]
        r = _rms_rows(xv)
        xr = xv * r
        dg_ref[...] = dg_ref[...] + jnp.sum(dh * xr, axis=0, keepdims=True)
        gdh = g_ref[...] * dh
        m = jnp.mean(gdh * xr, axis=-1, keepdims=True)
        dx_ref[...] = dy_ref[...] + r * (gdh - xr * m)

    return pl.pallas_call(
        body, name="inproj_bwd_dx",
        grid=(S // tm,),
        in_specs=[pl.BlockSpec((tm, N_MAIN), lambda i: (i, 0)),
                  pl.BlockSpec((tm, N_FFPAD), lambda i: (i, 0)),
                  pl.BlockSpec((D, N_MAIN), lambda i: (0, 0)),
                  pl.BlockSpec((D, N_FFPAD), lambda i: (0, 0)),
                  pl.BlockSpec((tm, D), lambda i: (i, 0)),
                  pl.BlockSpec((1, D), lambda i: (0, 0)),
                  pl.BlockSpec((tm, D), lambda i: (i, 0))],
        out_specs=[pl.BlockSpec((tm, D), lambda i: (i, 0)), pl.BlockSpec((1, D), lambda i: (0, 0))],
        out_shape=[jax.ShapeDtypeStruct((S, D), F32), jax.ShapeDtypeStruct((1, D), F32)],
        compiler_params=_cp(("arbitrary",), 48 << 20),
    )(dpm, dff, wm, wff, x, g, dy)


def _inproj_bwd_dw(x, g, dpm, dff):
    S, D = x.shape
    tk = min(_TM, S)
    tn = 512

    def body(x_ref, g_ref, dp_ref, dff_ref, dw_ref, dwff_ref):
        j, k = pl.program_id(0), pl.program_id(1)

        @pl.when(k == 0)
        def _():
            dw_ref[...] = jnp.zeros_like(dw_ref)

        @pl.when((k == 0) & (j == 0))
        def _():
            dwff_ref[...] = jnp.zeros_like(dwff_ref)

        xv = x_ref[...]
        h = ((xv * _rms_rows(xv)) * g_ref[...]).astype(BF16)
        dw_ref[...] = dw_ref[...] + _dot_tn(h, dp_ref[...])

        @pl.when(j == 0)
        def _():
            dwff_ref[...] = dwff_ref[...] + _dot_tn(h, dff_ref[...])

    return pl.pallas_call(
        body, name="inproj_bwd_dw",
        grid=(N_MAIN // tn, S // tk),
        in_specs=[pl.BlockSpec((tk, D), lambda j, k: (k, 0)),
                  pl.BlockSpec((1, D), lambda j, k: (0, 0)),
                  pl.BlockSpec((tk, tn), lambda j, k: (k, j)),
                  pl.BlockSpec((tk, N_FFPAD), lambda j, k: (k, 0))],
        out_specs=[pl.BlockSpec((D, tn), lambda j, k: (0, j)), pl.BlockSpec((D, N_FFPAD), lambda j, k: (0, 0))],
        out_shape=[jax.ShapeDtypeStruct((D, N_MAIN), F32), jax.ShapeDtypeStruct((D, N_FFPAD), F32)],
        compiler_params=_cp(("arbitrary", "arbitrary"), 40 << 20),
    )(x, g, dpm, dff)


def _constants(T):
    tril = jnp.tril(jnp.ones((T, T), F32)).astype(BF16)
    hid = jnp.arange(FOX_W) // HEAD_DIM
    bd = (hid[:, None] == hid[None, :]).astype(BF16)
    ex = (jnp.arange(N_FFPAD)[:, None] == hid[None, :]).astype(BF16)
    return tril, tril.T, bd, ex


def _crow4(ccol):
    S = ccol.shape[0]
    c = ccol[:, :FOX_HEADS].T.reshape(FOX_HEADS // 2, 2, S)
    return jnp.pad(c, ((0, 0), (0, 6), (0, 0)))


def _layer_fwd(x, lw, consts, ride=None):
    tril, triu, bd, ex = consts
    proj, pff = _inproj_fwd(x, lw["g"], lw["wm"], lw["wff"])
    qs, kn, ccol, cqb = _fox_prep(proj, pff, lw["bfp"], lw["gq"], lw["gk"], bd, ex, tril)
    crow4 = _crow4(ccol)
    fo, lse, *gathered = _fox_fwd(qs, kn, proj, cqb, crow4, ride)
    so, ltot = _sb_fwd(proj, triu)
    pooled = _pool_fwd(proj)
    y, mixed = _mix_out(fo, so, pooled, proj, lw["wbd"], lw["scale"], lw["wout"], x)
    return y, (x, proj, pff, qs, kn, cqb, crow4, fo, lse, so, ltot, pooled, mixed), gathered


def _layer_bwd(dy, saved, lw, consts, ride=None):
    tril, triu, bd, _ = consts
    x, proj, pff, qs, kn, cqb, crow4, fo, lse, so, ltot, pooled, mixed = saved
    S = x.shape[0]
    dfo, dfg, dso, dsg, dpg, dpooled, dscale, dwbd = _gate_bwd(dy, lw["wout"], fo, so, pooled, proj, lw["wbd"], lw["scale"])
    dwout = _matmul_tn(mixed, dy, "dw_out")
    dpx = _pool_bwd(dpooled)
    dqs, dkn, dfv, dck, dcq4, *received = _fox_bwd(qs, kn, proj, dfo, fo, lse, cqb, crow4, ride)
    dsq, dsk, dsv = _sb_bwd(proj, dso, ltot, tril)
    dc8 = dck[:, ::HEAD_DIM] + dcq4[:, :2, :].reshape(FOX_HEADS, S).T
    dccol = jnp.pad(dc8, ((0, 0), (0, N_FFPAD - FOX_HEADS)))
    dfq, dfk, dff, dgq, dgk, dbf = _qk_bwd(dqs, dkn, proj, pff, lw["bfp"], lw["gq"], lw["gk"], bd, dccol, triu)
    dpm = jnp.concatenate([dfq, dfk, dfv, dfg, dpx, dpg, dsq, dsk, dsv, dsg], axis=1)
    dx, dng = _inproj_bwd_dx(dpm, dff, lw["wm"], lw["wff"], x, lw["g"], dy)
    dwm, dwff = _inproj_bwd_dw(x, lw["g"], dpm, dff)
    dwin = jnp.concatenate([dwm[:, :OFF_PX], dwff[:, :N_FF], dwm[:, OFF_PX:]], axis=1)
    grads = {
        "norm_g": dng[0],
        "w_in": dwin,
        "b_f": dbf[0, :N_FF],
        "q_norm_g": dgq[0].reshape(FOX_HEADS, HEAD_DIM).sum(0),
        "k_norm_g": dgk[0].reshape(FOX_HEADS, HEAD_DIM).sum(0),
        "w_pool": jnp.stack([dwbd[64 * i:64 * i + 64, 64 * i:64 * i + 64] for i in range(4)]),
        "pool_scale": dscale[0],
        "w_out": dwout,
    }
    return dx, grads, received


def _layer_weights(l, norm_g, gin, b_f, q_norm_g, k_norm_g, w_pool, pool_scale, gout):
    D = gin.shape[1]
    w = gin.transpose(1, 0, 2).reshape(D, D_IN)
    wm = jnp.concatenate([w[:, :2048], w[:, 2048 + N_FF:]], axis=1)
    wff = jnp.pad(w[:, 2048:2048 + N_FF], ((0, 0), (0, N_FFPAD - N_FF)))
    wbd = jnp.zeros((POOL_W, POOL_W), F32)
    for i in range(4):
        wbd = wbd.at[64 * i:64 * i + 64, 64 * i:64 * i + 64].set(w_pool[l, i])
    return {
        "g": norm_g[l].reshape(1, D),
        "wm": wm, "wff": wff,
        "bfp": jnp.pad(b_f[l], (0, N_FFPAD - N_FF)).reshape(1, N_FFPAD),
        "gq": jnp.tile(q_norm_g[l], FOX_HEADS).reshape(1, FOX_W),
        "gk": jnp.tile(k_norm_g[l], FOX_HEADS).reshape(1, FOX_W),
        "wbd": wbd.astype(BF16),
        "scale": pool_scale[l].reshape(1, POOL_W),
        "wout": gout.reshape(D_MIX, D),
    }


def _grad_parts(g):
    dwin, dwout = g["w_in"], g["w_out"]
    D = dwin.shape[0]
    return (dwin.reshape(D, N_DEV, D_IN // N_DEV).transpose(1, 0, 2),
            dwout.reshape(N_DEV, D_MIX // N_DEV, dwout.shape[1]))


def _train_step(x, target, norm_g, win_sh, b_f, q_norm_g, k_norm_g, w_pool, pool_scale, wout_sh):
    L = norm_g.shape[0]
    consts = _constants(min(_T, x.shape[0]))
    gathered = _exchange_pair("gather", win_sh[0], wout_sh[0], "gather_weights")
    lws, saved = [], []
    h = x
    for l in range(L):
        lws.append(_layer_weights(l, norm_g, gathered[0], b_f, q_norm_g, k_norm_g, w_pool, pool_scale, gathered[1]))
        ride = (win_sh[l + 1], wout_sh[l + 1]) if l + 1 < L else None
        h, sv, gathered = _layer_fwd(h, lws[l], consts, ride)
        saved.append(sv)
    dy, loss = _loss_head(h, target)
    grads, received = [None] * L, [None] * L
    ride = None
    for l in reversed(range(L)):
        dy, grads[l], got = _layer_bwd(dy, saved[l], lws[l], consts, ride)
        if ride is not None:
            received[l + 1] = got
        ride = _grad_parts(grads[l])
    received[0] = _exchange_pair("scatter", ride[0], ride[1], "exchange_grads")
    return loss, dy, grads, received


def _mesh_pos():
    return lax.axis_index("x"), lax.axis_index("y"), lax.axis_index("c")


_FLIPS = [(0, 0, 1), (1, 0, 0), (0, 1, 0), (1, 1, 0), (1, 0, 1), (0, 1, 1), (1, 1, 1)]


def _peers():
    x, y, c = _mesh_pos()
    out = []
    for fx, fy, fc in _FLIPS:
        px = 1 - x if fx else x
        py = 1 - y if fy else y
        pc = 1 - c if fc else c
        out.append(((px, py, pc), 4 * px + 2 * py + pc))
    return out, 4 * x + 2 * y + c


_EXCHANGE_SEMS = [pltpu.SemaphoreType.DMA((14,)), pltpu.SemaphoreType.DMA((14,)), pltpu.SemaphoreType.DMA((2,))]
_ANY = pl.BlockSpec(memory_space=pl.ANY)


def _exchange_copies(kind, a_ref, b_ref, oa_ref, ob_ref, send_sems, recv_sems, loc_sems):
    peers, me = _peers()
    pairs = ((a_ref, oa_ref), (b_ref, ob_ref))
    local = [pltpu.make_async_copy(src if kind == "gather" else src.at[me], dst.at[me], loc_sems.at[t])
             for t, (src, dst) in enumerate(pairs)]
    remote = []
    for k, (dev, idx) in enumerate(peers):
        for t, (src, dst) in enumerate(pairs):
            remote.append(pltpu.make_async_remote_copy(
                src_ref=src if kind == "gather" else src.at[idx], dst_ref=dst.at[me],
                send_sem=send_sems.at[2 * k + t], recv_sem=recv_sems.at[2 * k + t],
                device_id=dev, device_id_type=pl.DeviceIdType.MESH))
    return local, remote


def _start_exchange(kind, *refs):
    local, remote = _exchange_copies(kind, *refs)
    for cp in local + remote:
        cp.start()


def _wait_exchange(kind, *refs):
    local, remote = _exchange_copies(kind, *refs)
    for cp in remote:
        cp.wait_recv()
    for cp in remote:
        cp.wait_send()
    for cp in local:
        cp.wait()


def _exchange_out_shapes(kind, a, b):
    if kind == "gather":
        return [jax.ShapeDtypeStruct((N_DEV,) + a.shape, a.dtype), jax.ShapeDtypeStruct((N_DEV,) + b.shape, b.dtype)]
    return [jax.ShapeDtypeStruct(a.shape, a.dtype), jax.ShapeDtypeStruct(b.shape, b.dtype)]


def _exchange_pair(kind, a, b, name):
    def body(*refs):
        _start_exchange(kind, *refs)
        _wait_exchange(kind, *refs)

    return pl.pallas_call(
        body, name=name,
        in_specs=[_ANY, _ANY], out_specs=[_ANY, _ANY],
        out_shape=_exchange_out_shapes(kind, a, b),
        scratch_shapes=_EXCHANGE_SEMS,
    )(a, b)


def _adam_math(w, g, m, v):
    m_new = ADAM_B1 * m + (1.0 - ADAM_B1) * g
    v_new = ADAM_B2 * v + (1.0 - ADAM_B2) * (g * g)
    m_hat = m_new / (1.0 - ADAM_B1 ** ADAM_STEP)
    v_hat = v_new / (1.0 - ADAM_B2 ** ADAM_STEP)
    delta = -ADAM_LR * (m_hat / (jnp.sqrt(v_hat) + ADAM_EPS) + ADAM_WD * w)
    return delta, m_new, v_new


def _sum_adamw(gparts, w, m, v, name):
    R, C = w.shape
    tr = min(128, R)

    def body(gp_ref, w_ref, m_ref, v_ref, g_ref, d_ref, nm_ref, nv_ref):
        g = gp_ref[0]
        for s in range(1, N_DEV):
            g = g + gp_ref[s]
        d, mn, vn = _adam_math(w_ref[...], g, m_ref[...], v_ref[...])
        g_ref[...] = g
        d_ref[...] = d
        nm_ref[...] = mn
        nv_ref[...] = vn

    blk = pl.BlockSpec((tr, C), lambda r: (r, 0))
    return pl.pallas_call(
        body, name=name,
        grid=(R // tr,),
        in_specs=[pl.BlockSpec((N_DEV, tr, C), lambda r: (0, r, 0)), blk, blk, blk],
        out_specs=[blk, blk, blk, blk],
        out_shape=[jax.ShapeDtypeStruct((R, C), F32)] * 4,
        compiler_params=_cp(("parallel",), 40 << 20),
    )(gparts, w, m, v)


def _small_update(gpack, wpack, mpack, vpack):
    R = gpack.shape[0]
    VM = pl.BlockSpec(memory_space=pltpu.VMEM)

    def body(g_ref, w_ref, m_ref, v_ref, gs_ref, d_ref, nm_ref, nv_ref, buf, send_sems, recv_sems):
        peers, me = _peers()
        buf[me] = g_ref[...]
        copies = []
        for k, (dev, _) in enumerate(peers):
            cp = pltpu.make_async_remote_copy(
                src_ref=g_ref, dst_ref=buf.at[me], send_sem=send_sems.at[k], recv_sem=recv_sems.at[k],
                device_id=dev, device_id_type=pl.DeviceIdType.MESH)
            cp.start()
            copies.append(cp)
        for cp in copies:
            cp.wait_recv()
        for cp in copies:
            cp.wait_send()
        g = buf[0]
        for s in range(1, N_DEV):
            g = g + buf[s]
        d, mn, vn = _adam_math(w_ref[...], g, m_ref[...], v_ref[...])
        gs_ref[...] = g
        d_ref[...] = d
        nm_ref[...] = mn
        nv_ref[...] = vn

    return pl.pallas_call(
        body, name="small_update",
        in_specs=[VM] * 4, out_specs=[VM] * 4,
        out_shape=[jax.ShapeDtypeStruct((R, 128), F32)] * 4,
        scratch_shapes=[pltpu.VMEM((N_DEV, R, 128), F32), pltpu.SemaphoreType.DMA((7,)), pltpu.SemaphoreType.DMA((7,))],
        compiler_params=_cp(None, 40 << 20),
    )(gpack, wpack, mpack, vpack)


_SMALL = ("norm_g", "b_f", "q_norm_g", "k_norm_g", "w_pool", "pool_scale")


def _pack(parts):
    flat = jnp.concatenate([p.reshape(-1) for p in parts])
    n = flat.shape[0]
    rows = -(-n // (8 * 128)) * 8
    return jnp.pad(flat, (0, rows * 128 - n)).reshape(rows, 128)


def _unpack(packed, like):
    flat = packed.reshape(-1)
    out, o = [], 0
    for p in like:
        out.append(flat[o:o + p.size].reshape(p.shape))
        o += p.size
    return out


def kernel(x, norm_g, w_in, b_f, q_norm_g, k_norm_g, w_pool, pool_scale, w_out, loss_target, m_norm_g, m_w_in, m_b_f, m_q_norm_g, m_k_norm_g, m_w_pool, m_pool_scale, m_w_out, v_norm_g, v_w_in, v_b_f, v_q_norm_g, v_k_norm_g, v_w_pool, v_pool_scale, v_w_out):
    L = w_in.shape[0]

    loss_local, dx, grads, received = _train_step(x[0], loss_target[0], norm_g, w_in.astype(BF16), b_f, q_norm_g,
                                                  k_norm_g, w_pool, pool_scale, w_out.astype(BF16))
    loss = lax.psum(loss_local, MESH_AXES)
    g = {k: jnp.stack([grads[l][k] for l in range(L)]) for k in _SMALL}

    upd_in = [_sum_adamw(received[l][0], w_in[l], m_w_in[l], v_w_in[l], "adamw_w_in") for l in range(L)]
    upd_out = [_sum_adamw(received[l][1], w_out[l], m_w_out[l], v_w_out[l], "adamw_w_out") for l in range(L)]
    g_win, d_win, nm_win, nv_win = [jnp.stack([u[i] for u in upd_in]) for i in range(4)]
    g_wout, d_wout, nm_wout, nv_wout = [jnp.stack([u[i] for u in upd_out]) for i in range(4)]

    ws = dict(norm_g=norm_g, b_f=b_f, q_norm_g=q_norm_g, k_norm_g=k_norm_g, w_pool=w_pool, pool_scale=pool_scale)
    ms = dict(norm_g=m_norm_g, b_f=m_b_f, q_norm_g=m_q_norm_g, k_norm_g=m_k_norm_g, w_pool=m_w_pool, pool_scale=m_pool_scale)
    vs = dict(norm_g=v_norm_g, b_f=v_b_f, q_norm_g=v_q_norm_g, k_norm_g=v_k_norm_g, w_pool=v_w_pool, pool_scale=v_pool_scale)
    like = [ws[k] for k in _SMALL]
    gs_p, d_p, nm_p, nv_p = _small_update(_pack([g[k] for k in _SMALL]), _pack(like),
                                          _pack([ms[k] for k in _SMALL]), _pack([vs[k] for k in _SMALL]))
    gs = dict(zip(_SMALL, _unpack(gs_p, like)))
    ds = dict(zip(_SMALL, _unpack(d_p, like)))
    nms = dict(zip(_SMALL, _unpack(nm_p, like)))
    nvs = dict(zip(_SMALL, _unpack(nv_p, like)))
    gs["w_in"], ds["w_in"], nms["w_in"], nvs["w_in"] = g_win, d_win, nm_win, nv_win
    gs["w_out"], ds["w_out"], nms["w_out"], nvs["w_out"] = g_wout, d_wout, nm_wout, nv_wout

    order = ("norm_g", "w_in", "b_f", "q_norm_g", "k_norm_g", "w_pool", "pool_scale", "w_out")
    return (loss, dx[None], *[gs[k] for k in order], *[ds[k] for k in order],
            *[nms[k] for k in order], *[nvs[k] for k in order])
```

```python
import functools

import jax
import jax.numpy as jnp
from jax import lax
from jax.experimental import pallas as pl
from jax.experimental.pallas import tpu as pltpu

F32 = jnp.float32
BF16 = jnp.bfloat16

EPS = 1e-6
NEG = -1e30
HEAD_DIM = 64
FOX_HEADS = 8
FOX_W = 512
POOL_W = 256
SB_W = 256
D_MIX = 1024
N_FF = 8
N_MAIN = 3584
N_FFPAD = 128
OFF_FQ, OFF_FK, OFF_FV, OFF_FG = 0, 512, 1024, 1536
OFF_PX, OFF_PG = 2048, 2304
OFF_SQ, OFF_SK, OFF_SV, OFF_SG = 2560, 2816, 3072, 3328
D_IN = 3592
Q_SCALE = HEAD_DIM ** -0.5

ADAM_LR = 0.001
ADAM_B1 = 0.9
ADAM_B2 = 0.999
ADAM_EPS = 1e-08
ADAM_WD = 0.01
ADAM_STEP = 10

N_DEV = 8
MESH_AXES = ("x", "y", "c")

_T = 256
_TM = 512
_TM_FWD, _TN_FWD = 2048, 512
_TM_DX = 512
_VMEM_BIG = 56 << 20


def _cp(sem=None, vmem=None):
    kw = {}
    if sem is not None:
        kw["dimension_semantics"] = sem
    if vmem is not None:
        kw["vmem_limit_bytes"] = vmem
    return pltpu.CompilerParams(**kw)


def _dot(a, b):
    return jnp.dot(a, b, preferred_element_type=F32)


def _dot_nt(a, b):
    return lax.dot_general(a, b, (((1,), (1,)), ((), ())), preferred_element_type=F32)


def _dot_tn(a, b):
    return lax.dot_general(a, b, (((0,), (0,)), ((), ())), preferred_element_type=F32)


def _mm2(v, m, left=False):
    hi = v.astype(BF16)
    lo = (v - hi.astype(F32)).astype(BF16)
    if left:
        return _dot(m, hi) + _dot(m, lo)
    return _dot(hi, m) + _dot(lo, m)


def _mm3(v, m, left=False):
    a1 = v.astype(BF16)
    r1 = v - a1.astype(F32)
    a2 = r1.astype(BF16)
    a3 = (r1 - a2.astype(F32)).astype(BF16)
    if left:
        return _dot(m, a1) + _dot(m, a2) + _dot(m, a3)
    return _dot(a1, m) + _dot(a2, m) + _dot(a3, m)


def _sigmoid(z):
    return 1.0 / (1.0 + jnp.exp(-z))


def _rms_rows(x):
    return lax.rsqrt(jnp.mean(x * x, axis=-1, keepdims=True) + EPS)


def _inproj_fwd(x, g, wm, wff):
    S, D = x.shape
    tm = min(_TM_FWD, S)
    tn = _TN_FWD

    def body(x_ref, g_ref, w_ref, wff_ref, o_ref, off_ref, h_ref):
        @pl.when(pl.program_id(1) == 0)
        def _():
            xv = x_ref[...]
            h = (xv * _rms_rows(xv)) * g_ref[...]
            h_ref[...] = h.astype(BF16)
            off_ref[...] = _dot(h_ref[...], wff_ref[...])

        o_ref[...] = _dot(h_ref[...], w_ref[...])

    return pl.pallas_call(
        body, name="inproj_fwd",
        grid=(S // tm, N_MAIN // tn),
        in_specs=[pl.BlockSpec((tm, D), lambda i, j: (i, 0)),
                  pl.BlockSpec((1, D), lambda i, j: (0, 0)),
                  pl.BlockSpec((D, tn), lambda i, j: (0, j)),
                  pl.BlockSpec((D, N_FFPAD), lambda i, j: (0, 0))],
        out_specs=[pl.BlockSpec((tm, tn), lambda i, j: (i, j)),
                   pl.BlockSpec((tm, N_FFPAD), lambda i, j: (i, 0))],
        out_shape=[jax.ShapeDtypeStruct((S, N_MAIN), F32), jax.ShapeDtypeStruct((S, N_FFPAD), F32)],
        scratch_shapes=[pltpu.VMEM((tm, D), BF16)],
        compiler_params=_cp(("parallel", "arbitrary"), 40 << 20),
    )(x, g, wm, wff)


def _head_norm(x, g, bd):
    ss = _mm2(x * x, bd)
    r = lax.rsqrt(ss * (1.0 / HEAD_DIM) + EPS)
    return (x * r) * g


def _fox_prep(proj, pff, bfp, gq, gk, bd, ex, tril):
    S = proj.shape[0]
    T = tril.shape[0]

    def body(q_ref, k_ref, ff_ref, b_ref, gq_ref, gk_ref, bd_ref, ex_ref, tri_ref,
             qs_ref, kn_ref, cc_ref, cqb_ref, carry):
        @pl.when(pl.program_id(0) == 0)
        def _():
            carry[...] = jnp.zeros_like(carry)

        bdv = bd_ref[...]
        qs_ref[...] = (_head_norm(q_ref[...], gq_ref[...], bdv) * Q_SCALE).astype(BF16)
        kn_ref[...] = _head_norm(k_ref[...], gk_ref[...], bdv).astype(BF16)
        u = ff_ref[...] + b_ref[...]
        lf = jnp.minimum(u, 0.0) - jnp.log1p(jnp.exp(-jnp.abs(u)))
        c = _mm3(lf, tri_ref[...], left=True) + carry[0:1, :]
        carry[0:1, :] = c[T - 1:T, :]
        cc_ref[...] = c
        cqb_ref[...] = _mm3(c, ex_ref[...])

    return pl.pallas_call(
        body, name="fox_prep",
        grid=(S // T,),
        in_specs=[pl.BlockSpec((T, FOX_W), lambda i: (i, OFF_FQ // FOX_W)),
                  pl.BlockSpec((T, FOX_W), lambda i: (i, OFF_FK // FOX_W)),
                  pl.BlockSpec((T, N_FFPAD), lambda i: (i, 0)),
                  pl.BlockSpec((1, N_FFPAD), lambda i: (0, 0)),
                  pl.BlockSpec((1, FOX_W), lambda i: (0, 0)),
                  pl.BlockSpec((1, FOX_W), lambda i: (0, 0)),
                  pl.BlockSpec((FOX_W, FOX_W), lambda i: (0, 0)),
                  pl.BlockSpec((N_FFPAD, FOX_W), lambda i: (0, 0)),
                  pl.BlockSpec((T, T), lambda i: (0, 0))],
        out_specs=[pl.BlockSpec((T, FOX_W), lambda i: (i, 0)),
                   pl.BlockSpec((T, FOX_W), lambda i: (i, 0)),
                   pl.BlockSpec((T, N_FFPAD), lambda i: (i, 0)),
                   pl.BlockSpec((T, FOX_W), lambda i: (i, 0))],
        out_shape=[jax.ShapeDtypeStruct((S, FOX_W), BF16), jax.ShapeDtypeStruct((S, FOX_W), BF16),
                   jax.ShapeDtypeStruct((S, N_FFPAD), F32), jax.ShapeDtypeStruct((S, FOX_W), F32)],
        scratch_shapes=[pltpu.VMEM((8, N_FFPAD), F32)],
        compiler_params=_cp(("arbitrary",), 40 << 20),
    )(proj, proj, pff, bfp, gq, gk, bd, ex, tril)


def _pair_blk(S, off=0):
    return pl.BlockSpec((S, 128), lambda p: (0, off + p), pipeline_mode=pl.Buffered(1))


def _pair_rows(S):
    return pl.BlockSpec((None, 8, S), lambda p: (p, 0, 0), pipeline_mode=pl.Buffered(1))


def _head_masks(S):
    return lax.broadcasted_iota(jnp.int32, (S, 128), 1) < HEAD_DIM


_EXP_ZERO = 104.0


def _spread_heads(x):
    src = lax.broadcasted_iota(jnp.int32, (128, 128), 0)
    return (_mm3(x, (src == 0).astype(BF16)), _mm3(x, (src == HEAD_DIM).astype(BF16)))


def _score_bounds(q, k):
    same_head = ((lax.broadcasted_iota(jnp.int32, (128, 128), 0) < HEAD_DIM)
                 == (lax.broadcasted_iota(jnp.int32, (128, 128), 1) < HEAD_DIM)).astype(BF16)

    def max_norm2(x):
        xf = x.astype(F32)
        return jnp.max(_mm2(xf * xf, same_head), axis=0, keepdims=True)

    z = jnp.sqrt(max_norm2(q) * max_norm2(k))
    return jnp.max(z[:, 0:1]) * 1.001 + 1e-3, jnp.max(z[:, 64:65]) * 1.001 + 1e-3


def _for_tiles_back(i, n, tiles_fn):
    def two(t, c):
        tiles_fn([i - 1 - 2 * t, i - 2 - 2 * t])
        return c

    lax.fori_loop(0, lax.shift_right_logical(n, 1), two, 0)

    @pl.when((n & 1) == 1)
    def _():
        tiles_fn([i - n])


def _fox_tiles_back(cr_ref, i, r0, T, zba, zbb):
    cf = cr_ref[:, pl.ds(r0, 128)]
    cfa = jnp.max(cf[0:1, 0:1])
    cfb = jnp.max(cf[1:2, 0:1])

    def alive(j):
        cl = cr_ref[:, pl.ds(pl.multiple_of(j * T + (T - 128), 128), 128)]
        gap_a = cfa - jnp.max(cl[0:1, 127:128])
        gap_b = cfb - jnp.max(cl[1:2, 127:128])
        return jnp.maximum(2.0 * zba + gap_a, 2.0 * zbb + gap_b) > -_EXP_ZERO

    def cond(st):
        return (st[0] < i) & st[1]

    def step(st):
        return st[0] + 1, alive(jnp.maximum(i - 2 - st[0], 0))

    n, _ = lax.while_loop(cond, step, (jnp.int32(0), alive(jnp.maximum(i - 1, 0))))
    return n


def _fox_fwd(qs, kn, proj, cqb, crow4, ride=None):
    S = qs.shape[0]
    T = min(_T, S)
    nq = S // T
    n_pairs = FOX_W // 128

    def body(*refs):
        if ride is None:
            q_ref, k_ref, v_ref, cq_ref, cr_ref, o_ref, lse_ref = refs[:7]
            qa, qb, vta, vtb, cka, ckb, ma, mb, acca, accb = refs[7:]
        else:
            q_ref, k_ref, v_ref, cq_ref, cr_ref, wa_ref, wb_ref, o_ref, lse_ref, ga_ref, gb_ref = refs[:11]
            qa, qb, vta, vtb, cka, ckb, ma, mb, acca, accb = refs[11:21]
            xrefs = (wa_ref, wb_ref, ga_ref, gb_ref) + tuple(refs[21:])

            @pl.when(pl.program_id(0) == 0)
            def _():
                _start_exchange("gather", *xrefs)

        lane_s = _head_masks(S)
        q = q_ref[...]
        zq = jnp.zeros_like(q)
        qa[...] = jnp.where(lane_s, q, zq)
        qb[...] = jnp.where(lane_s, zq, q)
        cq = cq_ref[...]
        cka[...], ckb[...] = _spread_heads(cq)
        lse_ref[...] = jnp.zeros((8, S), F32)
        row_t = lax.broadcasted_iota(jnp.int32, (128, T), 0) < HEAD_DIM
        zba, zbb = _score_bounds(q, k_ref[...])

        def prep(c, carry):
            c0 = pl.multiple_of(c * T, T)
            vt = v_ref[pl.ds(c0, T), :].T
            vta[:, pl.ds(c0, T)] = jnp.where(row_t, vt, 1.0).astype(BF16)
            vtb[:, pl.ds(c0, T)] = jnp.where(row_t, 1.0, vt).astype(BF16)
            return carry

        lax.fori_loop(0, nq, prep, 0)
        causal = (lax.broadcasted_iota(jnp.int32, (T, T), 0) <= lax.broadcasted_iota(jnp.int32, (T, T), 1))

        heads = ((qa, vta, cka, ma, acca), (qb, vtb, ckb, mb, accb))

        def kv(js, r0, masked):
            cr = cr_ref[:, pl.ds(r0, T)]
            c0s = [pl.multiple_of(j * T, T) for j in js]
            ks = [k_ref[pl.ds(c0, T), :] for c0 in c0s]
            ss = []
            for h, (qr, _, ckr, _, _) in enumerate(heads):
                qh = qr[pl.ds(r0, T), :]
                row = []
                for k, c0 in zip(ks, c0s):
                    s = _dot_nt(k, qh) + cr[h:h + 1, :] - jnp.tile(ckr[pl.ds(c0, T), :], (1, T // 128))
                    row.append(jnp.where(causal, s, NEG) if masked else s)
                ss.append(row)
            ms = []
            for row, (_, _, _, mr, _) in zip(ss, heads):
                top = row[0]
                for s in row[1:]:
                    top = jnp.maximum(top, s)
                m_old = mr[0:1, :]
                ms.append((m_old, jnp.maximum(m_old, jnp.max(top, axis=0, keepdims=True))))
            ps = [[jnp.exp(s - m_new).astype(BF16) for s in row] for row, (_, m_new) in zip(ss, ms)]
            pvs = []
            for row, (_, vr, _, _, _) in zip(ps, heads):
                pv = _dot(vr[:, pl.ds(c0s[0], T)], row[0])
                for p, c0 in zip(row[1:], c0s[1:]):
                    pv = pv + _dot(vr[:, pl.ds(c0, T)], p)
                pvs.append(pv)
            for pv, (m_old, m_new), (_, _, _, mr, ar) in zip(pvs, ms, heads):
                ar[...] = jnp.exp(m_old - m_new) * ar[...] + pv
                mr[0:1, :] = m_new

        def qblk(i, carry):
            r0 = pl.multiple_of(i * T, T)
            ma[...] = jnp.full((8, T), NEG, F32)
            mb[...] = jnp.full((8, T), NEG, F32)
            acca[...] = jnp.zeros((128, T), F32)
            accb[...] = jnp.zeros((128, T), F32)
            kv([i], r0, True)
            done = _fox_tiles_back(cr_ref, i, r0, T, zba, zbb)
            _for_tiles_back(i, done, lambda js: kv(js, r0, False))
            aa = acca[...]
            ab = accb[...]
            la = aa[64:65, :]
            lb = ab[0:1, :]
            o_ref[pl.ds(r0, T), :] = jnp.where(row_t, aa / la, ab / lb).T
            lse_ref[0:1, pl.ds(r0, T)] = ma[0:1, :] + jnp.log(la)
            lse_ref[1:2, pl.ds(r0, T)] = mb[0:1, :] + jnp.log(lb)
            lse_ref[2:3, pl.ds(r0, T)] = jnp.broadcast_to(done.astype(F32), (1, T))
            return carry

        lax.fori_loop(0, nq, qblk, 0)
        if ride is not None:
            @pl.when(pl.program_id(0) == n_pairs - 1)
            def _():
                _wait_exchange("gather", *xrefs)

    extra = () if ride is None else tuple(ride)
    return pl.pallas_call(
        body, name="fox_fwd" if ride is None else "fox_fwd_gather",
        grid=(n_pairs,),
        in_specs=[_pair_blk(S), _pair_blk(S), _pair_blk(S, OFF_FV // 128), _pair_blk(S), _pair_rows(S)]
        + [_ANY] * len(extra),
        out_specs=[_pair_blk(S), _pair_rows(S)] + [_ANY] * len(extra),
        out_shape=[jax.ShapeDtypeStruct((S, FOX_W), F32), jax.ShapeDtypeStruct((n_pairs, 8, S), F32)]
        + (_exchange_out_shapes("gather", *extra) if extra else []),
        scratch_shapes=[pltpu.VMEM((S, 128), BF16)] * 2 + [pltpu.VMEM((128, S), BF16)] * 2
        + [pltpu.VMEM((S, 128), F32)] * 2 + [pltpu.VMEM((8, T), F32)] * 2 + [pltpu.VMEM((128, T), F32)] * 2
        + (_EXCHANGE_SEMS if extra else []),
        compiler_params=_cp(("arbitrary",), _VMEM_BIG),
    )(qs, kn, proj, cqb, crow4, *extra)


def _softplus_parts(z):
    e = jnp.exp(-jnp.abs(z))
    return e, jnp.maximum(z, 0.0) + jnp.log1p(e)


def _sb_fwd(proj, triu):
    S = proj.shape[0]
    T = triu.shape[0]
    nq = S // T

    def body(q_ref, k_ref, v_ref, tri_ref, o_ref, lt_ref, qa, qb, kb, vt, ra, rb, acca, accb):
        lane_s = _head_masks(S)
        q = (q_ref[...] * Q_SCALE).astype(BF16)
        zq = jnp.zeros_like(q)
        qa[...] = jnp.where(lane_s, q, zq)
        qb[...] = jnp.where(lane_s, zq, q)
        kb[...] = k_ref[...].astype(BF16)
        lt_ref[...] = jnp.zeros((8, S), F32)
        row_t = lax.broadcasted_iota(jnp.int32, (128, T), 0) < HEAD_DIM
        zba, zbb = _score_bounds(q, kb[...])

        def prep(c, carry):
            c0 = pl.multiple_of(c * T, T)
            vt[:, pl.ds(c0, T)] = v_ref[pl.ds(c0, T), :].T.astype(BF16)
            return carry

        lax.fori_loop(0, nq, prep, 0)
        strict = (lax.broadcasted_iota(jnp.int32, (T, T), 0) < lax.broadcasted_iota(jnp.int32, (T, T), 1))

        heads = ((qa, ra, acca), (qb, rb, accb))

        def kv(j, r0, masked):
            c0 = pl.multiple_of(j * T, T)
            k = kb[pl.ds(c0, T), :]
            vtt = vt[:, pl.ds(c0, T)]
            tri = tri_ref[...]
            zs = [_dot_nt(k, qr[pl.ds(r0, T), :]) for qr, _, _ in heads]
            lbs = [-_softplus_parts(z)[1] for z in zs]
            if masked:
                lbs = [jnp.where(strict, lb, 0.0) for lb in lbs]
            incs = [_mm2(lb, tri, left=True) for lb in lbs]
            rs = [r_ref[0:1, :] for _, r_ref, _ in heads]
            aas = [jnp.exp(z + inc + r) for z, inc, r in zip(zs, incs, rs)]
            if masked:
                aas = [jnp.where(strict, a, 0.0) for a in aas]
            avs = [_dot(vtt, a.astype(BF16)) for a in aas]
            for (_, r_ref, acc_ref), r, inc, av in zip(heads, rs, incs, avs):
                r_ref[0:1, :] = r + inc[0:1, :]
                acc_ref[...] = acc_ref[...] + av

        def qblk(i, carry):
            r0 = pl.multiple_of(i * T, T)
            ra[...] = jnp.zeros((8, T), F32)
            rb[...] = jnp.zeros((8, T), F32)
            acca[...] = jnp.zeros((128, T), F32)
            accb[...] = jnp.zeros((128, T), F32)
            kv(i, r0, True)

            def alive():
                return jnp.maximum(jnp.max(ra[0:1, :]) + zba, jnp.max(rb[0:1, :]) + zbb) > -_EXP_ZERO

            def cond(st):
                return (st[0] < i) & st[1]

            def step(st):
                kv(i - 1 - st[0], r0, False)
                return st[0] + 1, alive()

            done, _ = lax.while_loop(cond, step, (jnp.int32(0), alive()))
            o_ref[pl.ds(r0, T), :] = jnp.where(row_t, acca[...], accb[...]).T
            lt_ref[0:1, pl.ds(r0, T)] = ra[0:1, :]
            lt_ref[1:2, pl.ds(r0, T)] = rb[0:1, :]
            lt_ref[2:3, pl.ds(r0, T)] = jnp.broadcast_to(done.astype(F32), (1, T))
            return carry

        lax.fori_loop(0, nq, qblk, 0)

    return pl.pallas_call(
        body, name="sb_fwd",
        grid=(SB_W // 128,),
        in_specs=[_pair_blk(S, OFF_SQ // 128), _pair_blk(S, OFF_SK // 128), _pair_blk(S, OFF_SV // 128),
                  pl.BlockSpec((T, T), lambda p: (0, 0))],
        out_specs=[_pair_blk(S), _pair_rows(S)],
        out_shape=[jax.ShapeDtypeStruct((S, SB_W), F32), jax.ShapeDtypeStruct((SB_W // 128, 8, S), F32)],
        scratch_shapes=[pltpu.VMEM((S, 128), BF16)] * 3 + [pltpu.VMEM((128, S), BF16)]
        + [pltpu.VMEM((8, T), F32)] * 2 + [pltpu.VMEM((128, T), F32)] * 2,
        compiler_params=_cp(("arbitrary",), _VMEM_BIG),
    )(proj, proj, proj, triu)


def _pool_window_lanes(shape):
    lane = lax.broadcasted_iota(jnp.int32, shape, 1)
    return jnp.where(lane < 64, 2, jnp.where(lane < 128, 4, jnp.where(lane < 192, 8, 16)))


def _pool_fwd(proj):
    S = proj.shape[0]

    def body(x_ref, o_ref):
        x = x_ref[...]
        t = lax.broadcasted_iota(jnp.int32, x.shape, 0)
        lane = lax.broadcasted_iota(jnp.int32, x.shape, 1)

        def back(a, k):
            return jnp.where(t >= k, pltpu.roll(a, k, 0), 0.0)

        s1 = x + back(x, 1)
        s2 = s1 + back(s1, 2)
        s4 = s2 + back(s2, 4)
        s8 = s4 + back(s4, 8)
        win = jnp.where(lane < 64, s1, jnp.where(lane < 128, s2, jnp.where(lane < 192, s4, s8)))
        cnt = jnp.minimum(t + 1, _pool_window_lanes(x.shape)).astype(F32)
        o_ref[...] = win / cnt - x

    return pl.pallas_call(
        body, name="pool_fwd",
        grid=(1,),
        in_specs=[pl.BlockSpec((S, POOL_W), lambda i: (0, OFF_PX // POOL_W))],
        out_specs=pl.BlockSpec((S, POOL_W), lambda i: (0, 0)),
        out_shape=jax.ShapeDtypeStruct((S, POOL_W), F32),
        compiler_params=_cp(("arbitrary",), _VMEM_BIG),
    )(proj)


def _silu(g):
    return g * _sigmoid(g)


def _mix_out(fo, so, pooled, proj, wbd, scale, wout, x):
    S, D = x.shape
    tm = min(256, S)

    def body(fo_ref, fg_ref, so_ref, sg_ref, pl_ref, pg_ref, wbd_ref, sc_ref, w_ref, x_ref, y_ref, mx_ref):
        mx_ref[:, 0:FOX_W] = (fo_ref[...] * _silu(fg_ref[...])).astype(BF16)
        yp = _dot(pl_ref[...].astype(BF16), wbd_ref[...]) * sc_ref[...]
        mx_ref[:, FOX_W:FOX_W + POOL_W] = (yp * _silu(pg_ref[...])).astype(BF16)
        mx_ref[:, FOX_W + POOL_W:D_MIX] = (so_ref[...] * _silu(sg_ref[...])).astype(BF16)
        y_ref[...] = x_ref[...] + _dot(mx_ref[...], w_ref[...])

    return pl.pallas_call(
        body, name="mix_out",
        grid=(S // tm,),
        in_specs=[pl.BlockSpec((tm, FOX_W), lambda i: (i, 0)),
                  pl.BlockSpec((tm, FOX_W), lambda i: (i, OFF_FG // FOX_W)),
                  pl.BlockSpec((tm, SB_W), lambda i: (i, 0)),
                  pl.BlockSpec((tm, SB_W), lambda i: (i, OFF_SG // SB_W)),
                  pl.BlockSpec((tm, POOL_W), lambda i: (i, 0)),
                  pl.BlockSpec((tm, POOL_W), lambda i: (i, OFF_PG // POOL_W)),
                  pl.BlockSpec((POOL_W, POOL_W), lambda i: (0, 0)),
                  pl.BlockSpec((1, POOL_W), lambda i: (0, 0)),
                  pl.BlockSpec((D_MIX, D), lambda i: (0, 0)),
                  pl.BlockSpec((tm, D), lambda i: (i, 0))],
        out_specs=[pl.BlockSpec((tm, D), lambda i: (i, 0)), pl.BlockSpec((tm, D_MIX), lambda i: (i, 0))],
        out_shape=[jax.ShapeDtypeStruct((S, D), F32), jax.ShapeDtypeStruct((S, D_MIX), BF16)],
        compiler_params=_cp(("parallel",), 40 << 20),
    )(fo, proj, so, proj, pooled, proj, wbd, scale, wout, x)


def _loss_head(y, target):
    S, D = y.shape
    tm = min(_TM, S)

    def body(y_ref, t_ref, dy_ref, ls_ref):
        @pl.when(pl.program_id(0) == 0)
        def _():
            ls_ref[...] = jnp.zeros_like(ls_ref)

        e = y_ref[...] - t_ref[...]
        dy_ref[...] = e * (1.0 / D)
        ls_ref[...] = ls_ref[...] + jnp.sum(e * e) * (0.5 / D)

    dy, ls = pl.pallas_call(
        body, name="loss_head",
        grid=(S // tm,),
        in_specs=[pl.BlockSpec((tm, D), lambda i: (i, 0)), pl.BlockSpec((tm, D), lambda i: (i, 0))],
        out_specs=[pl.BlockSpec((tm, D), lambda i: (i, 0)), pl.BlockSpec((8, 128), lambda i: (0, 0))],
        out_shape=[jax.ShapeDtypeStruct((S, D), F32), jax.ShapeDtypeStruct((8, 128), F32)],
        compiler_params=_cp(("arbitrary",), 40 << 20),
    )(y, target)
    return dy, ls[0, 0]


def _dsilu(g):
    s = _sigmoid(g)
    return s * (1.0 + g * (1.0 - s))


def _gate_bwd(dy, wout, fo, so, pooled, proj, wbd, scale):
    S, D = dy.shape
    tm = min(256, S)

    def body(dy_ref, w_ref, fo_ref, fg_ref, so_ref, sg_ref, pl_ref, pg_ref, wbd_ref, sc_ref,
             dfo_ref, dfg_ref, dso_ref, dsg_ref, dpg_ref, dpl_ref, dsc_ref, dwbd_ref):
        @pl.when(pl.program_id(0) == 0)
        def _():
            dsc_ref[...] = jnp.zeros_like(dsc_ref)
            dwbd_ref[...] = jnp.zeros_like(dwbd_ref)

        dm = _dot_nt(dy_ref[...].astype(BF16), w_ref[...])
        dmf = dm[:, 0:FOX_W]
        dmp = dm[:, FOX_W:FOX_W + POOL_W]
        dms = dm[:, FOX_W + POOL_W:D_MIX]
        fg = fg_ref[...]
        dfo_ref[...] = dmf * _silu(fg)
        dfg_ref[...] = (dmf * fo_ref[...] * _dsilu(fg)).astype(BF16)
        sg = sg_ref[...]
        dso_ref[...] = dms * _silu(sg)
        dsg_ref[...] = (dms * so_ref[...] * _dsilu(sg)).astype(BF16)
        pg = pg_ref[...]
        plb = pl_ref[...].astype(BF16)
        yw = _dot(plb, wbd_ref[...])
        sc = sc_ref[...]
        dpg_ref[...] = (dmp * (yw * sc) * _dsilu(pg)).astype(BF16)
        dys = dmp * _silu(pg)
        dsc_ref[...] = dsc_ref[...] + jnp.sum(dys * yw, axis=0, keepdims=True)
        dyw = (dys * sc).astype(BF16)
        dpl_ref[...] = _dot_nt(dyw, wbd_ref[...])
        dwbd_ref[...] = dwbd_ref[...] + _dot_tn(plb, dyw)

    return pl.pallas_call(
        body, name="gate_bwd",
        grid=(S // tm,),
        in_specs=[pl.BlockSpec((tm, D), lambda i: (i, 0)),
                  pl.BlockSpec((D_MIX, D), lambda i: (0, 0)),
                  pl.BlockSpec((tm, FOX_W), lambda i: (i, 0)),
                  pl.BlockSpec((tm, FOX_W), lambda i: (i, OFF_FG // FOX_W)),
                  pl.BlockSpec((tm, SB_W), lambda i: (i, 0)),
                  pl.BlockSpec((tm, SB_W), lambda i: (i, OFF_SG // SB_W)),
                  pl.BlockSpec((tm, POOL_W), lambda i: (i, 0)),
                  pl.BlockSpec((tm, POOL_W), lambda i: (i, OFF_PG // POOL_W)),
                  pl.BlockSpec((POOL_W, POOL_W), lambda i: (0, 0)),
                  pl.BlockSpec((1, POOL_W), lambda i: (0, 0))],
        out_specs=[pl.BlockSpec((tm, FOX_W), lambda i: (i, 0)),
                   pl.BlockSpec((tm, FOX_W), lambda i: (i, 0)),
                   pl.BlockSpec((tm, SB_W), lambda i: (i, 0)),
                   pl.BlockSpec((tm, SB_W), lambda i: (i, 0)),
                   pl.BlockSpec((tm, POOL_W), lambda i: (i, 0)),
                   pl.BlockSpec((tm, POOL_W), lambda i: (i, 0)),
                   pl.BlockSpec((1, POOL_W), lambda i: (0, 0)),
                   pl.BlockSpec((POOL_W, POOL_W), lambda i: (0, 0))],
        out_shape=[jax.ShapeDtypeStruct((S, FOX_W), F32), jax.ShapeDtypeStruct((S, FOX_W), BF16),
                   jax.ShapeDtypeStruct((S, SB_W), F32), jax.ShapeDtypeStruct((S, SB_W), BF16),
                   jax.ShapeDtypeStruct((S, POOL_W), BF16), jax.ShapeDtypeStruct((S, POOL_W), F32),
                   jax.ShapeDtypeStruct((1, POOL_W), F32), jax.ShapeDtypeStruct((POOL_W, POOL_W), F32)],
        compiler_params=_cp(("arbitrary",), 40 << 20),
    )(dy, wout, fo, proj, so, proj, pooled, proj, wbd, scale)


def _matmul_tn(a, b, name):
    S, M = a.shape
    N = b.shape[1]
    tk = min(_TM, S)
    tn = min(512, N)

    def body(a_ref, b_ref, o_ref):
        @pl.when(pl.program_id(1) == 0)
        def _():
            o_ref[...] = jnp.zeros_like(o_ref)

        o_ref[...] = o_ref[...] + _dot_tn(a_ref[...].astype(BF16), b_ref[...].astype(BF16))

    return pl.pallas_call(
        body, name=name,
        grid=(N // tn, S // tk),
        in_specs=[pl.BlockSpec((tk, M), lambda j, k: (k, 0)), pl.BlockSpec((tk, tn), lambda j, k: (k, j))],
        out_specs=pl.BlockSpec((M, tn), lambda j, k: (0, j)),
        out_shape=jax.ShapeDtypeStruct((M, N), F32),
        compiler_params=_cp(("parallel", "arbitrary"), 40 << 20),
    )(a, b)


def _pool_bwd(dpooled):
    S = dpooled.shape[0]

    def body(d_ref, o_ref):
        d = d_ref[...]
        t = lax.broadcasted_iota(jnp.int32, d.shape, 0)
        lane = lax.broadcasted_iota(jnp.int32, d.shape, 1)
        cnt = jnp.minimum(t + 1, _pool_window_lanes(d.shape)).astype(F32)
        u = d / cnt

        def fwd(a, k):
            return jnp.where(t < S - k, pltpu.roll(a, S - k, 0), 0.0)

        s1 = u + fwd(u, 1)
        s2 = s1 + fwd(s1, 2)
        s4 = s2 + fwd(s2, 4)
        s8 = s4 + fwd(s4, 8)
        win = jnp.where(lane < 64, s1, jnp.where(lane < 128, s2, jnp.where(lane < 192, s4, s8)))
        o_ref[...] = (win - d).astype(BF16)

    return pl.pallas_call(
        body, name="pool_bwd",
        grid=(1,),
        in_specs=[pl.BlockSpec((S, POOL_W), lambda i: (0, 0))],
        out_specs=pl.BlockSpec((S, POOL_W), lambda i: (0, 0)),
        out_shape=jax.ShapeDtypeStruct((S, POOL_W), BF16),
        compiler_params=_cp(("arbitrary",), _VMEM_BIG),
    )(dpooled)


def _fox_bwd(qs, kn, proj, dfo, fo, lse, cqb, crow4, ride=None):
    S = qs.shape[0]
    T = min(_T, S)
    nq = S // T
    n_pairs = FOX_W // 128

    def body(*refs):
        if ride is None:
            q_ref, k_ref, v_ref, do_ref, o_ref, lse_ref, cq_ref, cr_ref = refs[:8]
            dq_ref, dk_ref, dv_ref, dck_ref, dcq_ref = refs[8:13]
            scr = refs[13:]
        else:
            q_ref, k_ref, v_ref, do_ref, o_ref, lse_ref, cq_ref, cr_ref, pa_ref, pb_ref = refs[:10]
            dq_ref, dk_ref, dv_ref, dck_ref, dcq_ref, ra_ref, rb_ref = refs[10:17]
            scr = refs[17:32]
            xrefs = (pa_ref, pb_ref, ra_ref, rb_ref) + tuple(refs[32:])

            @pl.when(pl.program_id(0) == 0)
            def _():
                _start_exchange("scatter", *xrefs)

        qa, qb, kta, ktb, vb, doa, dob, cka, ckb, dcka, dckb, dva, dqt, dcqa, dcqb = scr
        lane_s = _head_masks(S)
        q = q_ref[...]
        zq = jnp.zeros_like(q)
        qa[...] = jnp.where(lane_s, q, zq)
        qb[...] = jnp.where(lane_s, zq, q)
        vb[...] = v_ref[...].astype(BF16)
        do = do_ref[...].astype(BF16)
        doa[...] = jnp.where(lane_s, do, zq)
        dob[...] = jnp.where(lane_s, zq, do)
        cq = cq_ref[...]
        cka[...], ckb[...] = _spread_heads(cq)
        zs = jnp.zeros((S, 128), F32)
        dk_ref[...] = zs
        dva[...] = zs
        dcka[...] = zs
        dckb[...] = zs
        dcq_ref[...] = jnp.zeros((8, S), F32)
        row_t = lax.broadcasted_iota(jnp.int32, (128, T), 0) < HEAD_DIM

        def prep(c, carry):
            c0 = pl.multiple_of(c * T, T)
            kt = k_ref[pl.ds(c0, T), :].astype(F32).T
            kta[:, pl.ds(c0, T)] = jnp.where(row_t, kt, 0.0).astype(BF16)
            ktb[:, pl.ds(c0, T)] = jnp.where(row_t, 0.0, kt).astype(BF16)
            return carry

        lax.fori_loop(0, nq, prep, 0)
        causal = (lax.broadcasted_iota(jnp.int32, (T, T), 0) <= lax.broadcasted_iota(jnp.int32, (T, T), 1))

        heads = ((qa, kta, doa, cka, dcka, dcqa), (qb, ktb, dob, ckb, dckb, dcqb))

        def kv(js, r0, lss, dls, masked):
            cr = cr_ref[:, pl.ds(r0, T)]
            c0s = [pl.multiple_of(j * T, T) for j in js]
            ks = [k_ref[pl.ds(c0, T), :] for c0 in c0s]
            vs = [vb[pl.ds(c0, T), :] for c0 in c0s]
            qhs = [hd[0][pl.ds(r0, T), :] for hd in heads]
            dohs = [hd[2][pl.ds(r0, T), :] for hd in heads]
            ss = []
            for h, hd in enumerate(heads):
                row = []
                for k, c0 in zip(ks, c0s):
                    s = _dot_nt(k, qhs[h]) + cr[h:h + 1, :] - jnp.tile(hd[3][pl.ds(c0, T), :], (1, T // 128))
                    row.append(jnp.where(causal, s, NEG) if masked else s)
                ss.append(row)
            ps = [[jnp.exp(s - lss[h]) for s in row] for h, row in enumerate(ss)]
            dps = [[_dot_nt(v, dohs[h]) for v in vs] for h in range(2)]
            dss = [[p * (dp - dls[h]) for p, dp in zip(ps[h], dps[h])] for h in range(2)]
            pbs = [[p.astype(BF16) for p in row] for row in ps]
            dsbs = [[ds.astype(BF16) for ds in row] for row in dss]
            for t, c0 in enumerate(c0s):
                dva[pl.ds(c0, T), :] = dva[pl.ds(c0, T), :] + (_dot(pbs[0][t], dohs[0]) + _dot(pbs[1][t], dohs[1]))
                dk_ref[pl.ds(c0, T), :] = dk_ref[pl.ds(c0, T), :] + (_dot(dsbs[0][t], qhs[0]) + _dot(dsbs[1][t], qhs[1]))
            dq = None
            for h, hd in enumerate(heads):
                for t, c0 in enumerate(c0s):
                    term = _dot(hd[1][:, pl.ds(c0, T)], dsbs[h][t])
                    dq = term if dq is None else dq + term
            dqt[...] = dqt[...] + dq
            for h, hd in enumerate(heads):
                col = jnp.sum(dss[h][0], axis=0, keepdims=True)
                for ds in dss[h][1:]:
                    col = col + jnp.sum(ds, axis=0, keepdims=True)
                hd[5][0:1, :] = hd[5][0:1, :] + col
                for ds, c0 in zip(dss[h], c0s):
                    fold = ds[:, 0:128]
                    for u in range(1, T // 128):
                        fold = fold + ds[:, 128 * u:128 * (u + 1)]
                    hd[4][pl.ds(c0, T), :] = hd[4][pl.ds(c0, T), :] - fold

        def qblk(i, carry):
            r0 = pl.multiple_of(i * T, T)
            dt = (do_ref[pl.ds(r0, T), :] * o_ref[pl.ds(r0, T), :]).T
            dla = jnp.sum(jnp.where(row_t, dt, 0.0), axis=0, keepdims=True)
            dlb = jnp.sum(jnp.where(row_t, 0.0, dt), axis=0, keepdims=True)
            ls = lse_ref[:, pl.ds(r0, T)]
            lss = (ls[0:1, :], ls[1:2, :])
            back = jnp.max(ls[2:3, :]).astype(jnp.int32)
            dqt[...] = jnp.zeros((128, T), F32)
            dcqa[...] = jnp.zeros((8, T), F32)
            dcqb[...] = jnp.zeros((8, T), F32)
            kv([i], r0, lss, (dla, dlb), True)
            _for_tiles_back(i, back, lambda js: kv(js, r0, lss, (dla, dlb), False))
            dq_ref[pl.ds(r0, T), :] = dqt[...].T
            dcq_ref[0:1, pl.ds(r0, T)] = dcqa[0:1, :]
            dcq_ref[1:2, pl.ds(r0, T)] = dcqb[0:1, :]
            return carry

        lax.fori_loop(0, nq, qblk, 0)
        dv_ref[...] = dva[...].astype(BF16)
        dck_ref[...] = jnp.where(lane_s, jnp.sum(dcka[...], axis=1, keepdims=True),
                                 jnp.sum(dckb[...], axis=1, keepdims=True))
        if ride is not None:
            @pl.when(pl.program_id(0) == n_pairs - 1)
            def _():
                _wait_exchange("scatter", *xrefs)

    extra = () if ride is None else tuple(ride)
    return pl.pallas_call(
        body, name="fox_bwd" if ride is None else "fox_bwd_exchange",
        grid=(n_pairs,),
        in_specs=[_pair_blk(S), _pair_blk(S), _pair_blk(S, OFF_FV // 128), _pair_blk(S), _pair_blk(S),
                  _pair_rows(S), _pair_blk(S), _pair_rows(S)] + [_ANY] * len(extra),
        out_specs=[_pair_blk(S), _pair_blk(S), _pair_blk(S), _pair_blk(S), _pair_rows(S)] + [_ANY] * len(extra),
        out_shape=[jax.ShapeDtypeStruct((S, FOX_W), F32), jax.ShapeDtypeStruct((S, FOX_W), F32),
                   jax.ShapeDtypeStruct((S, FOX_W), BF16), jax.ShapeDtypeStruct((S, FOX_W), F32),
                   jax.ShapeDtypeStruct((n_pairs, 8, S), F32)]
        + (_exchange_out_shapes("scatter", *extra) if extra else []),
        scratch_shapes=[pltpu.VMEM((S, 128), BF16)] * 2 + [pltpu.VMEM((128, S), BF16)] * 2
        + [pltpu.VMEM((S, 128), BF16)] * 3 + [pltpu.VMEM((S, 128), F32)] * 5
        + [pltpu.VMEM((128, T), F32)] + [pltpu.VMEM((8, T), F32)] * 2
        + (_EXCHANGE_SEMS if extra else []),
        compiler_params=_cp(("arbitrary",), _VMEM_BIG),
    )(qs, kn, proj, dfo, fo, lse, cqb, crow4, *extra)


def _sb_bwd(proj, dso, ltot, tril):
    S = proj.shape[0]
    T = tril.shape[0]
    nq = S // T

    def body(q_ref, k_ref, v_ref, do_ref, lt_ref, tri_ref, dq_ref, dk_ref, dv_ref,
             qa, qb, k2, kta, ktb, vb, doa, dob, dka, dva, dqt, ra, rb, ga, gb):
        lane_s = _head_masks(S)
        q = (q_ref[...] * Q_SCALE).astype(BF16)
        zq = jnp.zeros_like(q)
        qa[...] = jnp.where(lane_s, q, zq)
        qb[...] = jnp.where(lane_s, zq, q)
        k2[...] = k_ref[...].astype(BF16)
        vb[...] = v_ref[...].astype(BF16)
        do = do_ref[...].astype(BF16)
        doa[...] = jnp.where(lane_s, do, zq)
        dob[...] = jnp.where(lane_s, zq, do)
        dka[...] = jnp.zeros((S, 128), F32)
        dva[...] = jnp.zeros((S, 128), F32)
        row_t = lax.broadcasted_iota(jnp.int32, (128, T), 0) < HEAD_DIM

        def prep(c, carry):
            c0 = pl.multiple_of(c * T, T)
            kt = k_ref[pl.ds(c0, T), :].T
            kta[:, pl.ds(c0, T)] = jnp.where(row_t, kt, 0.0).astype(BF16)
            ktb[:, pl.ds(c0, T)] = jnp.where(row_t, 0.0, kt).astype(BF16)
            return carry

        lax.fori_loop(0, nq, prep, 0)
        strict = (lax.broadcasted_iota(jnp.int32, (T, T), 0) < lax.broadcasted_iota(jnp.int32, (T, T), 1))

        heads = ((qa, kta, doa, ra, ga), (qb, ktb, dob, rb, gb))

        def kv(j, r0, lta, ltb, masked):
            c0 = pl.multiple_of(j * T, T)
            kfull = k2[pl.ds(c0, T), :]
            v = vb[pl.ds(c0, T), :]
            tri = tri_ref[...]
            lts = (lta, ltb)
            qhs = [hd[0][pl.ds(r0, T), :] for hd in heads]
            dohs = [hd[2][pl.ds(r0, T), :] for hd in heads]
            zs = [_dot_nt(kfull, qh) for qh in qhs]
            das = [_dot_nt(v, doh) for doh in dohs]
            es, lbs = [], []
            for z in zs:
                e, sp = _softplus_parts(z)
                es.append(e)
                lbs.append(jnp.where(strict, -sp, 0.0) if masked else -sp)
            pres = [_mm2(lb, tri, left=True) for lb in lbs]
            rs = [hd[3][0:1, :] for hd in heads]
            aas = [jnp.exp(z + lb + ((lt - r) - pre)) for z, lb, lt, r, pre in zip(zs, lbs, lts, rs, pres)]
            if masked:
                aas = [jnp.where(strict, a, 0.0) for a in aas]
            gs = [a * da for a, da in zip(aas, das)]
            gpres = [_mm2(g, tri, left=True) for g in gs]
            gcs = [hd[4][0:1, :] for hd in heads]
            dzbs = []
            for z, e, g, gpre, gc in zip(zs, es, gs, gpres, gcs):
                inv = 1.0 / (1.0 + e)
                pos = z >= 0.0
                sig = jnp.where(pos, 1.0, e) * inv
                oms = jnp.where(pos, e, 1.0) * inv
                dz = g * oms - sig * (gc + (gpre - g))
                if masked:
                    dz = jnp.where(strict, dz, 0.0)
                dzbs.append(dz.astype(BF16))
            dqt[...] = dqt[...] + (_dot(heads[0][1][:, pl.ds(c0, T)], dzbs[0]) + _dot(heads[1][1][:, pl.ds(c0, T)], dzbs[1]))
            dka[pl.ds(c0, T), :] = dka[pl.ds(c0, T), :] + (_dot(dzbs[0], qhs[0]) + _dot(dzbs[1], qhs[1]))
            dva[pl.ds(c0, T), :] = dva[pl.ds(c0, T), :] + (_dot(aas[0].astype(BF16), dohs[0]) + _dot(aas[1].astype(BF16), dohs[1]))
            for hd, r, pre, gc, gpre in zip(heads, rs, pres, gcs, gpres):
                hd[3][0:1, :] = r + pre[T - 1:T, :]
                hd[4][0:1, :] = gc + gpre[T - 1:T, :]

        def qblk(i, carry):
            r0 = pl.multiple_of(i * T, T)
            lt = lt_ref[:, pl.ds(r0, T)]
            lta = lt[0:1, :]
            ltb = lt[1:2, :]
            back = jnp.max(lt[2:3, :]).astype(jnp.int32)
            zt = jnp.zeros((8, T), F32)
            dqt[...] = jnp.zeros((128, T), F32)
            ra[...] = zt
            rb[...] = zt
            ga[...] = zt
            gb[...] = zt

            def inner(j, c):
                kv(j, r0, lta, ltb, False)
                return c

            lax.fori_loop(i - back, i, inner, 0)
            kv(i, r0, lta, ltb, True)
            dq_ref[pl.ds(r0, T), :] = (dqt[...] * Q_SCALE).T.astype(BF16)
            return carry

        lax.fori_loop(0, nq, qblk, 0)
        dk_ref[...] = dka[...].astype(BF16)
        dv_ref[...] = dva[...].astype(BF16)

    return pl.pallas_call(
        body, name="sb_bwd",
        grid=(SB_W // 128,),
        in_specs=[_pair_blk(S, OFF_SQ // 128), _pair_blk(S, OFF_SK // 128), _pair_blk(S, OFF_SV // 128),
                  _pair_blk(S), _pair_rows(S), pl.BlockSpec((T, T), lambda p: (0, 0))],
        out_specs=[_pair_blk(S), _pair_blk(S), _pair_blk(S)],
        out_shape=[jax.ShapeDtypeStruct((S, SB_W), BF16)] * 3,
        scratch_shapes=([pltpu.VMEM((S, 128), BF16)] * 3 + [pltpu.VMEM((128, S), BF16)] * 2
                        + [pltpu.VMEM((S, 128), BF16)] * 3 + [pltpu.VMEM((S, 128), F32)] * 2
                        + [pltpu.VMEM((128, T), F32)] + [pltpu.VMEM((8, T), F32)] * 4),
        compiler_params=_cp(("arbitrary",), _VMEM_BIG),
    )(proj, proj, proj, dso, ltot, tril)


def _head_norm_bwd(x, g, dy, bd):
    ss = _mm2(x * x, bd)
    r = lax.rsqrt(ss * (1.0 / HEAD_DIM) + EPS)
    xr = x * r
    gdy = g * dy
    m = _mm2(xr * gdy, bd) * (1.0 / HEAD_DIM)
    return r * (gdy - xr * m), dy * xr


def _qk_bwd(dqs, dkn, proj, pff, bfp, gq, gk, bd, dccol, triu):
    S = proj.shape[0]
    T = triu.shape[0]
    n = S // T
    rev = lambda col: (lambda i: (n - 1 - i, col))

    def body(dq_ref, dk_ref, q_ref, k_ref, ff_ref, b_ref, gq_ref, gk_ref, bd_ref, dc_ref, tri_ref,
             dfq_ref, dfk_ref, dff_ref, dgq_ref, dgk_ref, dbf_ref, carry):
        @pl.when(pl.program_id(0) == 0)
        def _():
            carry[...] = jnp.zeros_like(carry)
            dgq_ref[...] = jnp.zeros_like(dgq_ref)
            dgk_ref[...] = jnp.zeros_like(dgk_ref)
            dbf_ref[...] = jnp.zeros_like(dbf_ref)

        bdv = bd_ref[...]
        dxq, gq_rows = _head_norm_bwd(q_ref[...], gq_ref[...], dq_ref[...] * Q_SCALE, bdv)
        dfq_ref[...] = dxq.astype(BF16)
        dgq_ref[...] = dgq_ref[...] + jnp.sum(gq_rows, axis=0, keepdims=True)
        dxk, gk_rows = _head_norm_bwd(k_ref[...], gk_ref[...], dk_ref[...], bdv)
        dfk_ref[...] = dxk.astype(BF16)
        dgk_ref[...] = dgk_ref[...] + jnp.sum(gk_rows, axis=0, keepdims=True)
        dlf = _mm3(dc_ref[...], tri_ref[...], left=True) + carry[0:1, :]
        carry[0:1, :] = dlf[0:1, :]
        u = ff_ref[...] + b_ref[...]
        lane = lax.broadcasted_iota(jnp.int32, u.shape, 1)
        dff = jnp.where(lane < N_FF, dlf * _sigmoid(-u), 0.0)
        dff_ref[...] = dff.astype(BF16)
        dbf_ref[...] = dbf_ref[...] + jnp.sum(dff, axis=0, keepdims=True)

    return pl.pallas_call(
        body, name="qk_bwd",
        grid=(n,),
        in_specs=[pl.BlockSpec((T, FOX_W), rev(0)), pl.BlockSpec((T, FOX_W), rev(0)),
                  pl.BlockSpec((T, FOX_W), rev(OFF_FQ // FOX_W)), pl.BlockSpec((T, FOX_W), rev(OFF_FK // FOX_W)),
                  pl.BlockSpec((T, N_FFPAD), rev(0)),
                  pl.BlockSpec((1, N_FFPAD), lambda i: (0, 0)),
                  pl.BlockSpec((1, FOX_W), lambda i: (0, 0)), pl.BlockSpec((1, FOX_W), lambda i: (0, 0)),
                  pl.BlockSpec((FOX_W, FOX_W), lambda i: (0, 0)),
                  pl.BlockSpec((T, N_FFPAD), rev(0)),
                  pl.BlockSpec((T, T), lambda i: (0, 0))],
        out_specs=[pl.BlockSpec((T, FOX_W), rev(0)), pl.BlockSpec((T, FOX_W), rev(0)),
                   pl.BlockSpec((T, N_FFPAD), rev(0)),
                   pl.BlockSpec((1, FOX_W), lambda i: (0, 0)), pl.BlockSpec((1, FOX_W), lambda i: (0, 0)),
                   pl.BlockSpec((1, N_FFPAD), lambda i: (0, 0))],
        out_shape=[jax.ShapeDtypeStruct((S, FOX_W), BF16), jax.ShapeDtypeStruct((S, FOX_W), BF16),
                   jax.ShapeDtypeStruct((S, N_FFPAD), BF16),
                   jax.ShapeDtypeStruct((1, FOX_W), F32), jax.ShapeDtypeStruct((1, FOX_W), F32),
                   jax.ShapeDtypeStruct((1, N_FFPAD), F32)],
        scratch_shapes=[pltpu.VMEM((8, N_FFPAD), F32)],
        compiler_params=_cp(("arbitrary",), 40 << 20),
    )(dqs, dkn, proj, proj, pff, bfp, gq, gk, bd, dccol, triu)


def _inproj_bwd_dx(dpm, dff, wm, wff, x, g, dy):
    S, D = x.shape
    tm = min(_TM_DX, S)

    def body(dp_ref, dff_ref, w_ref, wff_ref, x_ref, g_ref, dy_ref, dx_ref, dg_ref):
        @pl.when(pl.program_id(0) == 0)
        def _():
            dg_ref[...] = jnp.zeros_like(dg_ref)

        dh = _dot_nt(dp_ref[...], w_ref[...]) + _dot_nt(dff_ref[...], wff_ref[...])
        xv = x_ref[...]
        r = _rms_rows(xv)
        xr = xv * r
        dg_ref[...] = dg_ref[...] + jnp.sum(dh * xr, axis=0, keepdims=True)
        gdh = g_ref[...] * dh
        m = jnp.mean(gdh * xr, axis=-1, keepdims=True)
        dx_ref[...] = dy_ref[...] + r * (gdh - xr * m)

    return pl.pallas_call(
        body, name="inproj_bwd_dx",
        grid=(S // tm,),
        in_specs=[pl.BlockSpec((tm, N_MAIN), lambda i: (i, 0)),
                  pl.BlockSpec((tm, N_FFPAD), lambda i: (i, 0)),
                  pl.BlockSpec((D, N_MAIN), lambda i: (0, 0)),
                  pl.BlockSpec((D, N_FFPAD), lambda i: (0, 0)),
                  pl.BlockSpec((tm, D), lambda i: (i, 0)),
                  pl.BlockSpec((1, D), lambda i: (0, 0)),
                  pl.BlockSpec((tm, D), lambda i: (i, 0))],
        out_specs=[pl.BlockSpec((tm, D), lambda i: (i, 0)), pl.BlockSpec((1, D), lambda i: (0, 0))],
        out_shape=[jax.ShapeDtypeStruct((S, D), F32), jax.ShapeDtypeStruct((1, D), F32)],
        compiler_params=_cp(("arbitrary",), 48 << 20),
    )(dpm, dff, wm, wff, x, g, dy)


def _inproj_bwd_dw(x, g, dpm, dff):
    S, D = x.shape
    tk = min(_TM, S)
    tn = 512

    def body(x_ref, g_ref, dp_ref, dff_ref, dw_ref, dwff_ref):
        j, k = pl.program_id(0), pl.program_id(1)

        @pl.when(k == 0)
        def _():
            dw_ref[...] = jnp.zeros_like(dw_ref)

        @pl.when((k == 0) & (j == 0))
        def _():
            dwff_ref[...] = jnp.zeros_like(dwff_ref)

        xv = x_ref[...]
        h = ((xv * _rms_rows(xv)) * g_ref[...]).astype(BF16)
        dw_ref[...] = dw_ref[...] + _dot_tn(h, dp_ref[...])

        @pl.when(j == 0)
        def _():
            dwff_ref[...] = dwff_ref[...] + _dot_tn(h, dff_ref[...])

    return pl.pallas_call(
        body, name="inproj_bwd_dw",
        grid=(N_MAIN // tn, S // tk),
        in_specs=[pl.BlockSpec((tk, D), lambda j, k: (k, 0)),
                  pl.BlockSpec((1, D), lambda j, k: (0, 0)),
                  pl.BlockSpec((tk, tn), lambda j, k: (k, j)),
                  pl.BlockSpec((tk, N_FFPAD), lambda j, k: (k, 0))],
        out_specs=[pl.BlockSpec((D, tn), lambda j, k: (0, j)), pl.BlockSpec((D, N_FFPAD), lambda j, k: (0, 0))],
        out_shape=[jax.ShapeDtypeStruct((D, N_MAIN), F32), jax.ShapeDtypeStruct((D, N_FFPAD), F32)],
        compiler_params=_cp(("arbitrary", "arbitrary"), 40 << 20),
    )(x, g, dpm, dff)


def _constants(T):
    tril = jnp.tril(jnp.ones((T, T), F32)).astype(BF16)
    hid = jnp.arange(FOX_W) // HEAD_DIM
    bd = (hid[:, None] == hid[None, :]).astype(BF16)
    ex = (jnp.arange(N_FFPAD)[:, None] == hid[None, :]).astype(BF16)
    return tril, tril.T, bd, ex


def _crow4(ccol):
    S = ccol.shape[0]
    c = ccol[:, :FOX_HEADS].T.reshape(FOX_HEADS // 2, 2, S)
    return jnp.pad(c, ((0, 0), (0, 6), (0, 0)))


def _layer_fwd(x, lw, consts, ride=None):
    tril, triu, bd, ex = consts
    proj, pff = _inproj_fwd(x, lw["g"], lw["wm"], lw["wff"])
    qs, kn, ccol, cqb = _fox_prep(proj, pff, lw["bfp"], lw["gq"], lw["gk"], bd, ex, tril)
    crow4 = _crow4(ccol)
    fo, lse, *gathered = _fox_fwd(qs, kn, proj, cqb, crow4, ride)
    so, ltot = _sb_fwd(proj, triu)
    pooled = _pool_fwd(proj)
    y, mixed = _mix_out(fo, so, pooled, proj, lw["wbd"], lw["scale"], lw["wout"], x)
    return y, (x, proj, pff, qs, kn, cqb, crow4, fo, lse, so, ltot, pooled, mixed), gathered


def _layer_bwd(dy, saved, lw, consts, ride=None):
    tril, triu, bd, _ = consts
    x, proj, pff, qs, kn, cqb, crow4, fo, lse, so, ltot, pooled, mixed = saved
    S = x.shape[0]
    dfo, dfg, dso, dsg, dpg, dpooled, dscale, dwbd = _gate_bwd(dy, lw["wout"], fo, so, pooled, proj, lw["wbd"], lw["scale"])
    dwout = _matmul_tn(mixed, dy, "dw_out")
    dpx = _pool_bwd(dpooled)
    dqs, dkn, dfv, dck, dcq4, *received = _fox_bwd(qs, kn, proj, dfo, fo, lse, cqb, crow4, ride)
    dsq, dsk, dsv = _sb_bwd(proj, dso, ltot, tril)
    dc8 = dck[:, ::HEAD_DIM] + dcq4[:, :2, :].reshape(FOX_HEADS, S).T
    dccol = jnp.pad(dc8, ((0, 0), (0, N_FFPAD - FOX_HEADS)))
    dfq, dfk, dff, dgq, dgk, dbf = _qk_bwd(dqs, dkn, proj, pff, lw["bfp"], lw["gq"], lw["gk"], bd, dccol, triu)
    dpm = jnp.concatenate([dfq, dfk, dfv, dfg, dpx, dpg, dsq, dsk, dsv, dsg], axis=1)
    dx, dng = _inproj_bwd_dx(dpm, dff, lw["wm"], lw["wff"], x, lw["g"], dy)
    dwm, dwff = _inproj_bwd_dw(x, lw["g"], dpm, dff)
    dwin = jnp.concatenate([dwm[:, :OFF_PX], dwff[:, :N_FF], dwm[:, OFF_PX:]], axis=1)
    grads = {
        "norm_g": dng[0],
        "w_in": dwin,
        "b_f": dbf[0, :N_FF],
        "q_norm_g": dgq[0].reshape(FOX_HEADS, HEAD_DIM).sum(0),
        "k_norm_g": dgk[0].reshape(FOX_HEADS, HEAD_DIM).sum(0),
        "w_pool": jnp.stack([dwbd[64 * i:64 * i + 64, 64 * i:64 * i + 64] for i in range(4)]),
        "pool_scale": dscale[0],
        "w_out": dwout,
    }
    return dx, grads, received


def _layer_weights(l, norm_g, gin, b_f, q_norm_g, k_norm_g, w_pool, pool_scale, gout):
    D = gin.shape[1]
    w = gin.transpose(1, 0, 2).reshape(D, D_IN)
    wm = jnp.concatenate([w[:, :2048], w[:, 2048 + N_FF:]], axis=1)
    wff = jnp.pad(w[:, 2048:2048 + N_FF], ((0, 0), (0, N_FFPAD - N_FF)))
    grp = jnp.arange(POOL_W) // 64
    wbd = jnp.where(grp[:, None] == grp[None, :], jnp.tile(w_pool[l].transpose(1, 0, 2).reshape(64, POOL_W), (4, 1)), 0.0)
    return {
        "g": norm_g[l].reshape(1, D),
        "wm": wm, "wff": wff,
        "bfp": jnp.pad(b_f[l], (0, N_FFPAD - N_FF)).reshape(1, N_FFPAD),
        "gq": jnp.tile(q_norm_g[l], FOX_HEADS).reshape(1, FOX_W),
        "gk": jnp.tile(k_norm_g[l], FOX_HEADS).reshape(1, FOX_W),
        "wbd": wbd.astype(BF16),
        "scale": pool_scale[l].reshape(1, POOL_W),
        "wout": gout.reshape(D_MIX, D),
    }


def _grad_parts(g):
    dwin, dwout = g["w_in"].astype(BF16), g["w_out"].astype(BF16)
    D = dwin.shape[0]
    return (dwin.reshape(D, N_DEV, D_IN // N_DEV).transpose(1, 0, 2),
            dwout.reshape(N_DEV, D_MIX // N_DEV, dwout.shape[1]))


def _train_step(x, target, norm_g, win_sh, b_f, q_norm_g, k_norm_g, w_pool, pool_scale, wout_sh):
    L = norm_g.shape[0]
    consts = _constants(min(_T, x.shape[0]))
    gathered = _exchange_pair("gather", win_sh[0], wout_sh[0], "gather_weights")
    lws, saved = [], []
    h = x
    for l in range(L):
        lws.append(_layer_weights(l, norm_g, gathered[0], b_f, q_norm_g, k_norm_g, w_pool, pool_scale, gathered[1]))
        ride = (win_sh[l + 1], wout_sh[l + 1]) if l + 1 < L else None
        h, sv, gathered = _layer_fwd(h, lws[l], consts, ride)
        saved.append(sv)
    dy, loss = _loss_head(h, target)
    grads, received = [None] * L, [None] * L
    ride = None
    for l in reversed(range(L)):
        dy, grads[l], got = _layer_bwd(dy, saved[l], lws[l], consts, ride)
        if ride is not None:
            received[l + 1] = got
        ride = _grad_parts(grads[l])
    received[0] = _exchange_pair("scatter", ride[0], ride[1], "exchange_grads")
    return loss, dy, grads, received


def _mesh_pos():
    return lax.axis_index("x"), lax.axis_index("y"), lax.axis_index("c")


_FLIPS = [(0, 0, 1), (1, 0, 0), (0, 1, 0), (1, 1, 0), (1, 0, 1), (0, 1, 1), (1, 1, 1)]


def _peers():
    x, y, c = _mesh_pos()
    out = []
    for fx, fy, fc in _FLIPS:
        px = 1 - x if fx else x
        py = 1 - y if fy else y
        pc = 1 - c if fc else c
        out.append(((px, py, pc), 4 * px + 2 * py + pc))
    return out, 4 * x + 2 * y + c


_EXCHANGE_SEMS = [pltpu.SemaphoreType.DMA((14,)), pltpu.SemaphoreType.DMA((14,)), pltpu.SemaphoreType.DMA((2,))]
_ANY = pl.BlockSpec(memory_space=pl.ANY)


def _exchange_copies(kind, a_ref, b_ref, oa_ref, ob_ref, send_sems, recv_sems, loc_sems):
    peers, me = _peers()
    pairs = ((a_ref, oa_ref), (b_ref, ob_ref))
    local = [pltpu.make_async_copy(src if kind == "gather" else src.at[me], dst.at[me], loc_sems.at[t])
             for t, (src, dst) in enumerate(pairs)]
    remote = []
    for k, (dev, idx) in enumerate(peers):
        for t, (src, dst) in enumerate(pairs):
            remote.append(pltpu.make_async_remote_copy(
                src_ref=src if kind == "gather" else src.at[idx], dst_ref=dst.at[me],
                send_sem=send_sems.at[2 * k + t], recv_sem=recv_sems.at[2 * k + t],
                device_id=dev, device_id_type=pl.DeviceIdType.MESH))
    return local, remote


def _start_exchange(kind, *refs):
    local, remote = _exchange_copies(kind, *refs)
    for cp in local + remote:
        cp.start()


def _wait_exchange(kind, *refs):
    local, remote = _exchange_copies(kind, *refs)
    for cp in remote:
        cp.wait_recv()
    for cp in remote:
        cp.wait_send()
    for cp in local:
        cp.wait()


def _exchange_out_shapes(kind, a, b):
    if kind == "gather":
        return [jax.ShapeDtypeStruct((N_DEV,) + a.shape, a.dtype), jax.ShapeDtypeStruct((N_DEV,) + b.shape, b.dtype)]
    return [jax.ShapeDtypeStruct(a.shape, a.dtype), jax.ShapeDtypeStruct(b.shape, b.dtype)]


def _exchange_pair(kind, a, b, name):
    def body(*refs):
        _start_exchange(kind, *refs)
        _wait_exchange(kind, *refs)

    return pl.pallas_call(
        body, name=name,
        in_specs=[_ANY, _ANY], out_specs=[_ANY, _ANY],
        out_shape=_exchange_out_shapes(kind, a, b),
        scratch_shapes=_EXCHANGE_SEMS,
    )(a, b)


def _adam_math(w, g, m, v):
    m_new = ADAM_B1 * m + (1.0 - ADAM_B1) * g
    v_new = ADAM_B2 * v + (1.0 - ADAM_B2) * (g * g)
    m_hat = m_new / (1.0 - ADAM_B1 ** ADAM_STEP)
    v_hat = v_new / (1.0 - ADAM_B2 ** ADAM_STEP)
    delta = -ADAM_LR * (m_hat / (jnp.sqrt(v_hat) + ADAM_EPS) + ADAM_WD * w)
    return delta, m_new, v_new


def _sum_adamw(gparts, w, m, v, name):
    R, C = w.shape
    tr = min(128, R)

    def body(gp_ref, w_ref, m_ref, v_ref, g_ref, d_ref, nm_ref, nv_ref):
        g = gp_ref[0].astype(F32)
        for s in range(1, N_DEV):
            g = g + gp_ref[s].astype(F32)
        d, mn, vn = _adam_math(w_ref[...], g, m_ref[...], v_ref[...])
        g_ref[...] = g
        d_ref[...] = d
        nm_ref[...] = mn
        nv_ref[...] = vn

    blk = pl.BlockSpec((tr, C), lambda r: (r, 0))
    return pl.pallas_call(
        body, name=name,
        grid=(R // tr,),
        in_specs=[pl.BlockSpec((N_DEV, tr, C), lambda r: (0, r, 0)), blk, blk, blk],
        out_specs=[blk, blk, blk, blk],
        out_shape=[jax.ShapeDtypeStruct((R, C), F32)] * 4,
        compiler_params=_cp(("parallel",), 40 << 20),
    )(gparts, w, m, v)


def _small_update(gpack, wpack, mpack, vpack):
    R = gpack.shape[0]
    VM = pl.BlockSpec(memory_space=pltpu.VMEM)

    def body(g_ref, w_ref, m_ref, v_ref, gs_ref, d_ref, nm_ref, nv_ref, buf, send_sems, recv_sems):
        peers, me = _peers()
        buf[me] = g_ref[...]
        copies = []
        for k, (dev, _) in enumerate(peers):
            cp = pltpu.make_async_remote_copy(
                src_ref=g_ref, dst_ref=buf.at[me], send_sem=send_sems.at[k], recv_sem=recv_sems.at[k],
                device_id=dev, device_id_type=pl.DeviceIdType.MESH)
            cp.start()
            copies.append(cp)
        for cp in copies:
            cp.wait_recv()
        for cp in copies:
            cp.wait_send()
        g = buf[0]
        for s in range(1, N_DEV):
            g = g + buf[s]
        d, mn, vn = _adam_math(w_ref[...], g, m_ref[...], v_ref[...])
        gs_ref[...] = g
        d_ref[...] = d
        nm_ref[...] = mn
        nv_ref[...] = vn

    return pl.pallas_call(
        body, name="small_update",
        in_specs=[VM] * 4, out_specs=[VM] * 4,
        out_shape=[jax.ShapeDtypeStruct((R, 128), F32)] * 4,
        scratch_shapes=[pltpu.VMEM((N_DEV, R, 128), F32), pltpu.SemaphoreType.DMA((7,)), pltpu.SemaphoreType.DMA((7,))],
        compiler_params=_cp(None, 40 << 20),
    )(gpack, wpack, mpack, vpack)


_SMALL = ("norm_g", "b_f", "q_norm_g", "k_norm_g", "w_pool", "pool_scale")


def _pack(parts):
    flat = jnp.concatenate([p.reshape(-1) for p in parts])
    n = flat.shape[0]
    rows = -(-n // (8 * 128)) * 8
    return jnp.pad(flat, (0, rows * 128 - n)).reshape(rows, 128)


def _unpack(packed, like):
    flat = packed.reshape(-1)
    out, o = [], 0
    for p in like:
        out.append(flat[o:o + p.size].reshape(p.shape))
        o += p.size
    return out


def kernel(x, norm_g, w_in, b_f, q_norm_g, k_norm_g, w_pool, pool_scale, w_out, loss_target, m_norm_g, m_w_in, m_b_f, m_q_norm_g, m_k_norm_g, m_w_pool, m_pool_scale, m_w_out, v_norm_g, v_w_in, v_b_f, v_q_norm_g, v_k_norm_g, v_w_pool, v_pool_scale, v_w_out):
    L = w_in.shape[0]

    loss_local, dx, grads, received = _train_step(x[0], loss_target[0], norm_g, w_in.astype(BF16), b_f, q_norm_g,
                                                  k_norm_g, w_pool, pool_scale, w_out.astype(BF16))
    loss = lax.psum(loss_local, MESH_AXES)
    g = {k: jnp.stack([grads[l][k] for l in range(L)]) for k in _SMALL}

    upd_in = [_sum_adamw(received[l][0], w_in[l], m_w_in[l], v_w_in[l], "adamw_w_in") for l in range(L)]
    upd_out = [_sum_adamw(received[l][1], w_out[l], m_w_out[l], v_w_out[l], "adamw_w_out") for l in range(L)]
    g_win, d_win, nm_win, nv_win = [jnp.stack([u[i] for u in upd_in]) for i in range(4)]
    g_wout, d_wout, nm_wout, nv_wout = [jnp.stack([u[i] for u in upd_out]) for i in range(4)]

    ws = dict(norm_g=norm_g, b_f=b_f, q_norm_g=q_norm_g, k_norm_g=k_norm_g, w_pool=w_pool, pool_scale=pool_scale)
    ms = dict(norm_g=m_norm_g, b_f=m_b_f, q_norm_g=m_q_norm_g, k_norm_g=m_k_norm_g, w_pool=m_w_pool, pool_scale=m_pool_scale)
    vs = dict(norm_g=v_norm_g, b_f=v_b_f, q_norm_g=v_q_norm_g, k_norm_g=v_k_norm_g, w_pool=v_w_pool, pool_scale=v_pool_scale)
    like = [ws[k] for k in _SMALL]
    gs_p, d_p, nm_p, nv_p = _small_update(_pack([g[k] for k in _SMALL]), _pack(like),
                                          _pack([ms[k] for k in _SMALL]), _pack([vs[k] for k in _SMALL]))
    gs = dict(zip(_SMALL, _unpack(gs_p, like)))
    ds = dict(zip(_SMALL, _unpack(d_p, like)))
    nms = dict(zip(_SMALL, _unpack(nm_p, like)))
    nvs = dict(zip(_SMALL, _unpack(nv_p, like)))
    gs["w_in"], ds["w_in"], nms["w_in"], nvs["w_in"] = g_win, d_win, nm_win, nv_win
    gs["w_out"], ds["w_out"], nms["w_out"], nvs["w_out"] = g_wout, d_wout, nm_wout, nv_wout

    order = ("norm_g", "w_in", "b_f", "q_norm_g", "k_norm_g", "w_pool", "pool_scale", "w_out")
    return (loss, dx[None], *[gs[k] for k in order], *[ds[k] for k in order],
            *[nms[k] for k in order], *[nvs[k] for k in order])
```

```python
import functools

import jax
import jax.numpy as jnp
from jax import lax
from jax.experimental import pallas as pl
from jax.experimental.pallas import tpu as pltpu

F32 = jnp.float32
BF16 = jnp.bfloat16

EPS = 1e-6
NEG = -1e30
HEAD_DIM = 64
FOX_HEADS = 8
FOX_W = 512
POOL_W = 256
SB_W = 256
D_MIX = 1024
N_FF = 8
N_MAIN = 3584
N_FFPAD = 128
OFF_FQ, OFF_FK, OFF_FV, OFF_FG = 0, 512, 1024, 1536
OFF_PX, OFF_PG = 2048, 2304
OFF_SQ, OFF_SK, OFF_SV, OFF_SG = 2560, 2816, 3072, 3328
D_IN = 3592
Q_SCALE = HEAD_DIM ** -0.5

ADAM_LR = 0.001
ADAM_B1 = 0.9
ADAM_B2 = 0.999
ADAM_EPS = 1e-08
ADAM_WD = 0.01
ADAM_STEP = 10

N_DEV = 8
MESH_AXES = ("x", "y", "c")

_T = 256
_TM = 512
_TM_FWD, _TN_FWD = 2048, 512
_TM_DX = 512
_TK_DW = 1024
_VMEM_BIG = 56 << 20


def _cp(sem=None, vmem=None):
    kw = {}
    if sem is not None:
        kw["dimension_semantics"] = sem
    if vmem is not None:
        kw["vmem_limit_bytes"] = vmem
    return pltpu.CompilerParams(**kw)


def _dot(a, b):
    return jnp.dot(a, b, preferred_element_type=F32)


def _dot_nt(a, b):
    return lax.dot_general(a, b, (((1,), (1,)), ((), ())), preferred_element_type=F32)


def _dot_tn(a, b):
    return lax.dot_general(a, b, (((0,), (0,)), ((), ())), preferred_element_type=F32)


def _mm2(v, m, left=False):
    hi = v.astype(BF16)
    lo = (v - hi.astype(F32)).astype(BF16)
    if left:
        return _dot(m, hi) + _dot(m, lo)
    return _dot(hi, m) + _dot(lo, m)


def _mm3(v, m, left=False):
    a1 = v.astype(BF16)
    r1 = v - a1.astype(F32)
    a2 = r1.astype(BF16)
    a3 = (r1 - a2.astype(F32)).astype(BF16)
    if left:
        return _dot(m, a1) + _dot(m, a2) + _dot(m, a3)
    return _dot(a1, m) + _dot(a2, m) + _dot(a3, m)


def _sigmoid(z):
    return 1.0 / (1.0 + jnp.exp(-z))


def _rms_rows(x):
    return lax.rsqrt(jnp.mean(x * x, axis=-1, keepdims=True) + EPS)


def _inproj_fwd(x, g, wm, wff):
    S, D = x.shape
    tm = min(_TM_FWD, S)
    tn = _TN_FWD

    def body(x_ref, g_ref, w_ref, wff_ref, o_ref, off_ref, ht_ref, h_ref):
        @pl.when(pl.program_id(1) == 0)
        def _():
            xv = x_ref[...]
            h = (xv * _rms_rows(xv)) * g_ref[...]
            h_ref[...] = h.astype(BF16)
            ht_ref[...] = h.T.astype(BF16)
            off_ref[...] = _dot(h_ref[...], wff_ref[...])

        o_ref[...] = _dot(h_ref[...], w_ref[...])

    return pl.pallas_call(
        body, name="inproj_fwd",
        grid=(S // tm, N_MAIN // tn),
        in_specs=[pl.BlockSpec((tm, D), lambda i, j: (i, 0)),
                  pl.BlockSpec((1, D), lambda i, j: (0, 0)),
                  pl.BlockSpec((D, tn), lambda i, j: (0, j)),
                  pl.BlockSpec((D, N_FFPAD), lambda i, j: (0, 0))],
        out_specs=[pl.BlockSpec((tm, tn), lambda i, j: (i, j)),
                   pl.BlockSpec((tm, N_FFPAD), lambda i, j: (i, 0)),
                   pl.BlockSpec((D, tm), lambda i, j: (0, i))],
        out_shape=[jax.ShapeDtypeStruct((S, N_MAIN), F32), jax.ShapeDtypeStruct((S, N_FFPAD), F32),
                   jax.ShapeDtypeStruct((D, S), BF16)],
        scratch_shapes=[pltpu.VMEM((tm, D), BF16)],
        compiler_params=_cp(("parallel", "arbitrary"), 48 << 20),
    )(x, g, wm, wff)


def _head_norm(x, g, bd):
    ss = _mm2(x * x, bd)
    r = lax.rsqrt(ss * (1.0 / HEAD_DIM) + EPS)
    return (x * r) * g


def _fox_prep(proj, pff, bfp, gq, gk, bd, ex, tril):
    S = proj.shape[0]
    T = tril.shape[0]

    def body(q_ref, k_ref, ff_ref, b_ref, gq_ref, gk_ref, bd_ref, ex_ref, tri_ref,
             qs_ref, kn_ref, cc_ref, cqb_ref, carry):
        @pl.when(pl.program_id(0) == 0)
        def _():
            carry[...] = jnp.zeros_like(carry)

        bdv = bd_ref[...]
        qs_ref[...] = (_head_norm(q_ref[...], gq_ref[...], bdv) * Q_SCALE).astype(BF16)
        kn_ref[...] = _head_norm(k_ref[...], gk_ref[...], bdv).astype(BF16)
        u = ff_ref[...] + b_ref[...]
        lf = jnp.minimum(u, 0.0) - jnp.log1p(jnp.exp(-jnp.abs(u)))
        c = _mm3(lf, tri_ref[...], left=True) + carry[0:1, :]
        carry[0:1, :] = c[T - 1:T, :]
        cc_ref[...] = c
        cqb_ref[...] = _mm3(c, ex_ref[...])

    return pl.pallas_call(
        body, name="fox_prep",
        grid=(S // T,),
        in_specs=[pl.BlockSpec((T, FOX_W), lambda i: (i, OFF_FQ // FOX_W)),
                  pl.BlockSpec((T, FOX_W), lambda i: (i, OFF_FK // FOX_W)),
                  pl.BlockSpec((T, N_FFPAD), lambda i: (i, 0)),
                  pl.BlockSpec((1, N_FFPAD), lambda i: (0, 0)),
                  pl.BlockSpec((1, FOX_W), lambda i: (0, 0)),
                  pl.BlockSpec((1, FOX_W), lambda i: (0, 0)),
                  pl.BlockSpec((FOX_W, FOX_W), lambda i: (0, 0)),
                  pl.BlockSpec((N_FFPAD, FOX_W), lambda i: (0, 0)),
                  pl.BlockSpec((T, T), lambda i: (0, 0))],
        out_specs=[pl.BlockSpec((T, FOX_W), lambda i: (i, 0)),
                   pl.BlockSpec((T, FOX_W), lambda i: (i, 0)),
                   pl.BlockSpec((T, N_FFPAD), lambda i: (i, 0)),
                   pl.BlockSpec((T, FOX_W), lambda i: (i, 0))],
        out_shape=[jax.ShapeDtypeStruct((S, FOX_W), BF16), jax.ShapeDtypeStruct((S, FOX_W), BF16),
                   jax.ShapeDtypeStruct((S, N_FFPAD), F32), jax.ShapeDtypeStruct((S, FOX_W), F32)],
        scratch_shapes=[pltpu.VMEM((8, N_FFPAD), F32)],
        compiler_params=_cp(("arbitrary",), 40 << 20),
    )(proj, proj, pff, bfp, gq, gk, bd, ex, tril)


def _pair_blk(S, off=0):
    return pl.BlockSpec((S, 128), lambda p: (0, off + p), pipeline_mode=pl.Buffered(1))


def _pair_rows(S):
    return pl.BlockSpec((None, 8, S), lambda p: (p, 0, 0), pipeline_mode=pl.Buffered(1))


def _head_masks(S):
    return lax.broadcasted_iota(jnp.int32, (S, 128), 1) < HEAD_DIM


_EXP_ZERO = 104.0


def _spread_heads(x):
    src = lax.broadcasted_iota(jnp.int32, (128, 128), 0)
    return (_mm3(x, (src == 0).astype(BF16)), _mm3(x, (src == HEAD_DIM).astype(BF16)))


def _score_bounds(q, k):
    same_head = ((lax.broadcasted_iota(jnp.int32, (128, 128), 0) < HEAD_DIM)
                 == (lax.broadcasted_iota(jnp.int32, (128, 128), 1) < HEAD_DIM)).astype(BF16)

    def max_norm2(x):
        xf = x.astype(F32)
        return jnp.max(_mm2(xf * xf, same_head), axis=0, keepdims=True)

    z = jnp.sqrt(max_norm2(q) * max_norm2(k))
    return jnp.max(z[:, 0:1]) * 1.001 + 1e-3, jnp.max(z[:, 64:65]) * 1.001 + 1e-3


def _for_tiles_back(i, n, tiles_fn):
    def two(t, c):
        tiles_fn([i - 1 - 2 * t, i - 2 - 2 * t])
        return c

    lax.fori_loop(0, lax.shift_right_logical(n, 1), two, 0)

    @pl.when((n & 1) == 1)
    def _():
        tiles_fn([i - n])


def _fox_tiles_back(cr_ref, i, r0, T, zba, zbb):
    cf = cr_ref[:, pl.ds(r0, 128)]
    cfa = jnp.max(cf[0:1, 0:1])
    cfb = jnp.max(cf[1:2, 0:1])

    def alive(j):
        cl = cr_ref[:, pl.ds(pl.multiple_of(j * T + (T - 128), 128), 128)]
        gap_a = cfa - jnp.max(cl[0:1, 127:128])
        gap_b = cfb - jnp.max(cl[1:2, 127:128])
        return jnp.maximum(2.0 * zba + gap_a, 2.0 * zbb + gap_b) > -_EXP_ZERO

    def cond(st):
        return (st[0] < i) & st[1]

    def step(st):
        return st[0] + 1, alive(jnp.maximum(i - 2 - st[0], 0))

    n, _ = lax.while_loop(cond, step, (jnp.int32(0), alive(jnp.maximum(i - 1, 0))))
    return n


def _fox_fwd(qs, kn, proj, cqb, crow4, ride=None):
    S = qs.shape[0]
    T = min(_T, S)
    nq = S // T
    n_pairs = FOX_W // 128

    def body(*refs):
        if ride is None:
            q_ref, k_ref, v_ref, cq_ref, cr_ref, o_ref, lse_ref = refs[:7]
            qa, qb, vta, vtb, cka, ckb, ma, mb, acca, accb = refs[7:]
        else:
            q_ref, k_ref, v_ref, cq_ref, cr_ref, wa_ref, wb_ref, o_ref, lse_ref, ga_ref, gb_ref = refs[:11]
            qa, qb, vta, vtb, cka, ckb, ma, mb, acca, accb = refs[11:21]
            xrefs = (wa_ref, wb_ref, ga_ref, gb_ref) + tuple(refs[21:])

            @pl.when(pl.program_id(0) == 0)
            def _():
                _start_exchange("gather", *xrefs)

        lane_s = _head_masks(S)
        q = q_ref[...]
        zq = jnp.zeros_like(q)
        qa[...] = jnp.where(lane_s, q, zq)
        qb[...] = jnp.where(lane_s, zq, q)
        cq = cq_ref[...]
        cka[...], ckb[...] = _spread_heads(cq)
        lse_ref[...] = jnp.zeros((8, S), F32)
        row_t = lax.broadcasted_iota(jnp.int32, (128, T), 0) < HEAD_DIM
        zba, zbb = _score_bounds(q, k_ref[...])

        def prep(c, carry):
            c0 = pl.multiple_of(c * T, T)
            vt = v_ref[pl.ds(c0, T), :].T
            vta[:, pl.ds(c0, T)] = jnp.where(row_t, vt, 1.0).astype(BF16)
            vtb[:, pl.ds(c0, T)] = jnp.where(row_t, 1.0, vt).astype(BF16)
            return carry

        lax.fori_loop(0, nq, prep, 0)
        causal = (lax.broadcasted_iota(jnp.int32, (T, T), 0) <= lax.broadcasted_iota(jnp.int32, (T, T), 1))

        heads = ((qa, vta, cka, ma, acca), (qb, vtb, ckb, mb, accb))

        def kv(js, r0, masked):
            cr = cr_ref[:, pl.ds(r0, T)]
            c0s = [pl.multiple_of(j * T, T) for j in js]
            ks = [k_ref[pl.ds(c0, T), :] for c0 in c0s]
            ss = []
            for h, (qr, _, ckr, _, _) in enumerate(heads):
                qh = qr[pl.ds(r0, T), :]
                row = []
                for k, c0 in zip(ks, c0s):
                    s = _dot_nt(k, qh) + cr[h:h + 1, :] - jnp.tile(ckr[pl.ds(c0, T), :], (1, T // 128))
                    row.append(jnp.where(causal, s, NEG) if masked else s)
                ss.append(row)
            ms = []
            for row, (_, _, _, mr, _) in zip(ss, heads):
                top = row[0]
                for s in row[1:]:
                    top = jnp.maximum(top, s)
                m_old = mr[0:1, :]
                ms.append((m_old, jnp.maximum(m_old, jnp.max(top, axis=0, keepdims=True))))
            ps = [[jnp.exp(s - m_new).astype(BF16) for s in row] for row, (_, m_new) in zip(ss, ms)]
            pvs = []
            for row, (_, vr, _, _, _) in zip(ps, heads):
                pv = _dot(vr[:, pl.ds(c0s[0], T)], row[0])
                for p, c0 in zip(row[1:], c0s[1:]):
                    pv = pv + _dot(vr[:, pl.ds(c0, T)], p)
                pvs.append(pv)
            for pv, (m_old, m_new), (_, _, _, mr, ar) in zip(pvs, ms, heads):
                ar[...] = jnp.exp(m_old - m_new) * ar[...] + pv
                mr[0:1, :] = m_new

        def qblk(i, carry):
            r0 = pl.multiple_of(i * T, T)
            ma[...] = jnp.full((8, T), NEG, F32)
            mb[...] = jnp.full((8, T), NEG, F32)
            acca[...] = jnp.zeros((128, T), F32)
            accb[...] = jnp.zeros((128, T), F32)
            kv([i], r0, True)
            done = _fox_tiles_back(cr_ref, i, r0, T, zba, zbb)
            _for_tiles_back(i, done, lambda js: kv(js, r0, False))
            aa = acca[...]
            ab = accb[...]
            la = aa[64:65, :]
            lb = ab[0:1, :]
            o_ref[pl.ds(r0, T), :] = jnp.where(row_t, aa / la, ab / lb).T
            lse_ref[0:1, pl.ds(r0, T)] = ma[0:1, :] + jnp.log(la)
            lse_ref[1:2, pl.ds(r0, T)] = mb[0:1, :] + jnp.log(lb)
            lse_ref[2:3, pl.ds(r0, T)] = jnp.broadcast_to(done.astype(F32), (1, T))
            return carry

        lax.fori_loop(0, nq, qblk, 0)
        if ride is not None:
            @pl.when(pl.program_id(0) == n_pairs - 1)
            def _():
                _wait_exchange("gather", *xrefs)

    extra = () if ride is None else tuple(ride)
    return pl.pallas_call(
        body, name="fox_fwd" if ride is None else "fox_fwd_gather",
        grid=(n_pairs,),
        in_specs=[_pair_blk(S), _pair_blk(S), _pair_blk(S, OFF_FV // 128), _pair_blk(S), _pair_rows(S)]
        + [_ANY] * len(extra),
        out_specs=[_pair_blk(S), _pair_rows(S)] + [_ANY] * len(extra),
        out_shape=[jax.ShapeDtypeStruct((S, FOX_W), F32), jax.ShapeDtypeStruct((n_pairs, 8, S), F32)]
        + (_exchange_out_shapes("gather", *extra) if extra else []),
        scratch_shapes=[pltpu.VMEM((S, 128), BF16)] * 2 + [pltpu.VMEM((128, S), BF16)] * 2
        + [pltpu.VMEM((S, 128), F32)] * 2 + [pltpu.VMEM((8, T), F32)] * 2 + [pltpu.VMEM((128, T), F32)] * 2
        + (_EXCHANGE_SEMS if extra else []),
        compiler_params=_cp(("arbitrary",), _VMEM_BIG),
    )(qs, kn, proj, cqb, crow4, *extra)


def _softplus_parts(z):
    e = jnp.exp(-jnp.abs(z))
    return e, jnp.maximum(z, 0.0) + jnp.log1p(e)


def _sb_fwd(proj, triu):
    S = proj.shape[0]
    T = triu.shape[0]
    nq = S // T

    def body(q_ref, k_ref, v_ref, tri_ref, o_ref, lt_ref, qa, qb, kb, vt, ra, rb, acca, accb):
        lane_s = _head_masks(S)
        q = (q_ref[...] * Q_SCALE).astype(BF16)
        zq = jnp.zeros_like(q)
        qa[...] = jnp.where(lane_s, q, zq)
        qb[...] = jnp.where(lane_s, zq, q)
        kb[...] = k_ref[...].astype(BF16)
        lt_ref[...] = jnp.zeros((8, S), F32)
        row_t = lax.broadcasted_iota(jnp.int32, (128, T), 0) < HEAD_DIM
        zba, zbb = _score_bounds(q, kb[...])

        def prep(c, carry):
            c0 = pl.multiple_of(c * T, T)
            vt[:, pl.ds(c0, T)] = v_ref[pl.ds(c0, T), :].T.astype(BF16)
            return carry

        lax.fori_loop(0, nq, prep, 0)
        strict = (lax.broadcasted_iota(jnp.int32, (T, T), 0) < lax.broadcasted_iota(jnp.int32, (T, T), 1))

        heads = ((qa, ra, acca), (qb, rb, accb))

        def kv(j, r0, masked):
            c0 = pl.multiple_of(j * T, T)
            k = kb[pl.ds(c0, T), :]
            vtt = vt[:, pl.ds(c0, T)]
            tri = tri_ref[...]
            zs = [_dot_nt(k, qr[pl.ds(r0, T), :]) for qr, _, _ in heads]
            lbs = [-_softplus_parts(z)[1] for z in zs]
            if masked:
                lbs = [jnp.where(strict, lb, 0.0) for lb in lbs]
            incs = [_mm2(lb, tri, left=True) for lb in lbs]
            rs = [r_ref[0:1, :] for _, r_ref, _ in heads]
            aas = [jnp.exp(z + inc + r) for z, inc, r in zip(zs, incs, rs)]
            if masked:
                aas = [jnp.where(strict, a, 0.0) for a in aas]
            avs = [_dot(vtt, a.astype(BF16)) for a in aas]
            for (_, r_ref, acc_ref), r, inc, av in zip(heads, rs, incs, avs):
                r_ref[0:1, :] = r + inc[0:1, :]
                acc_ref[...] = acc_ref[...] + av

        def qblk(i, carry):
            r0 = pl.multiple_of(i * T, T)
            ra[...] = jnp.zeros((8, T), F32)
            rb[...] = jnp.zeros((8, T), F32)
            acca[...] = jnp.zeros((128, T), F32)
            accb[...] = jnp.zeros((128, T), F32)
            kv(i, r0, True)

            def alive():
                return jnp.maximum(jnp.max(ra[0:1, :]) + zba, jnp.max(rb[0:1, :]) + zbb) > -_EXP_ZERO

            def cond(st):
                return (st[0] < i) & st[1]

            def step(st):
                kv(i - 1 - st[0], r0, False)
                return st[0] + 1, alive()

            done, _ = lax.while_loop(cond, step, (jnp.int32(0), alive()))
            o_ref[pl.ds(r0, T), :] = jnp.where(row_t, acca[...], accb[...]).T
            lt_ref[0:1, pl.ds(r0, T)] = ra[0:1, :]
            lt_ref[1:2, pl.ds(r0, T)] = rb[0:1, :]
            lt_ref[2:3, pl.ds(r0, T)] = jnp.broadcast_to(done.astype(F32), (1, T))
            return carry

        lax.fori_loop(0, nq, qblk, 0)

    return pl.pallas_call(
        body, name="sb_fwd",
        grid=(SB_W // 128,),
        in_specs=[_pair_blk(S, OFF_SQ // 128), _pair_blk(S, OFF_SK // 128), _pair_blk(S, OFF_SV // 128),
                  pl.BlockSpec((T, T), lambda p: (0, 0))],
        out_specs=[_pair_blk(S), _pair_rows(S)],
        out_shape=[jax.ShapeDtypeStruct((S, SB_W), F32), jax.ShapeDtypeStruct((SB_W // 128, 8, S), F32)],
        scratch_shapes=[pltpu.VMEM((S, 128), BF16)] * 3 + [pltpu.VMEM((128, S), BF16)]
        + [pltpu.VMEM((8, T), F32)] * 2 + [pltpu.VMEM((128, T), F32)] * 2,
        compiler_params=_cp(("arbitrary",), _VMEM_BIG),
    )(proj, proj, proj, triu)


def _pool_window_lanes(shape):
    lane = lax.broadcasted_iota(jnp.int32, shape, 1)
    return jnp.where(lane < 64, 2, jnp.where(lane < 128, 4, jnp.where(lane < 192, 8, 16)))


def _pool_fwd(proj):
    S = proj.shape[0]

    def body(x_ref, o_ref):
        x = x_ref[...]
        t = lax.broadcasted_iota(jnp.int32, x.shape, 0)
        lane = lax.broadcasted_iota(jnp.int32, x.shape, 1)

        def back(a, k):
            return jnp.where(t >= k, pltpu.roll(a, k, 0), 0.0)

        s1 = x + back(x, 1)
        s2 = s1 + back(s1, 2)
        s4 = s2 + back(s2, 4)
        s8 = s4 + back(s4, 8)
        win = jnp.where(lane < 64, s1, jnp.where(lane < 128, s2, jnp.where(lane < 192, s4, s8)))
        cnt = jnp.minimum(t + 1, _pool_window_lanes(x.shape)).astype(F32)
        o_ref[...] = win / cnt - x

    return pl.pallas_call(
        body, name="pool_fwd",
        grid=(1,),
        in_specs=[pl.BlockSpec((S, POOL_W), lambda i: (0, OFF_PX // POOL_W))],
        out_specs=pl.BlockSpec((S, POOL_W), lambda i: (0, 0)),
        out_shape=jax.ShapeDtypeStruct((S, POOL_W), F32),
        compiler_params=_cp(("arbitrary",), _VMEM_BIG),
    )(proj)


def _silu(g):
    return g * _sigmoid(g)


def _mix_out(fo, so, pooled, proj, wbd, scale, wout, x):
    S, D = x.shape
    tm = min(256, S)

    def body(fo_ref, fg_ref, so_ref, sg_ref, pl_ref, pg_ref, wbd_ref, sc_ref, w_ref, x_ref, y_ref, mxt_ref, mx_ref):
        parts = ((0, fo_ref[...] * _silu(fg_ref[...])),
                 (FOX_W, (_dot(pl_ref[...].astype(BF16), wbd_ref[...]) * sc_ref[...]) * _silu(pg_ref[...])),
                 (FOX_W + POOL_W, so_ref[...] * _silu(sg_ref[...])))
        for off, part in parts:
            w = part.shape[1]
            mx_ref[:, off:off + w] = part.astype(BF16)
            mxt_ref[off:off + w, :] = part.T.astype(BF16)
        y_ref[...] = x_ref[...] + _dot(mx_ref[...], w_ref[...])

    return pl.pallas_call(
        body, name="mix_out",
        grid=(S // tm,),
        in_specs=[pl.BlockSpec((tm, FOX_W), lambda i: (i, 0)),
                  pl.BlockSpec((tm, FOX_W), lambda i: (i, OFF_FG // FOX_W)),
                  pl.BlockSpec((tm, SB_W), lambda i: (i, 0)),
                  pl.BlockSpec((tm, SB_W), lambda i: (i, OFF_SG // SB_W)),
                  pl.BlockSpec((tm, POOL_W), lambda i: (i, 0)),
                  pl.BlockSpec((tm, POOL_W), lambda i: (i, OFF_PG // POOL_W)),
                  pl.BlockSpec((POOL_W, POOL_W), lambda i: (0, 0)),
                  pl.BlockSpec((1, POOL_W), lambda i: (0, 0)),
                  pl.BlockSpec((D_MIX, D), lambda i: (0, 0)),
                  pl.BlockSpec((tm, D), lambda i: (i, 0))],
        out_specs=[pl.BlockSpec((tm, D), lambda i: (i, 0)), pl.BlockSpec((D_MIX, tm), lambda i: (0, i))],
        out_shape=[jax.ShapeDtypeStruct((S, D), F32), jax.ShapeDtypeStruct((D_MIX, S), BF16)],
        scratch_shapes=[pltpu.VMEM((tm, D_MIX), BF16)],
        compiler_params=_cp(("parallel",), 40 << 20),
    )(fo, proj, so, proj, pooled, proj, wbd, scale, wout, x)


def _loss_head(y, target):
    S, D = y.shape
    tm = min(_TM, S)

    def body(y_ref, t_ref, dy_ref, ls_ref):
        @pl.when(pl.program_id(0) == 0)
        def _():
            ls_ref[...] = jnp.zeros_like(ls_ref)

        e = y_ref[...] - t_ref[...]
        dy_ref[...] = e * (1.0 / D)
        ls_ref[...] = ls_ref[...] + jnp.sum(e * e) * (0.5 / D)

    dy, ls = pl.pallas_call(
        body, name="loss_head",
        grid=(S // tm,),
        in_specs=[pl.BlockSpec((tm, D), lambda i: (i, 0)), pl.BlockSpec((tm, D), lambda i: (i, 0))],
        out_specs=[pl.BlockSpec((tm, D), lambda i: (i, 0)), pl.BlockSpec((8, 128), lambda i: (0, 0))],
        out_shape=[jax.ShapeDtypeStruct((S, D), F32), jax.ShapeDtypeStruct((8, 128), F32)],
        compiler_params=_cp(("arbitrary",), 40 << 20),
    )(y, target)
    return dy, ls[0, 0]


def _dsilu(g):
    s = _sigmoid(g)
    return s * (1.0 + g * (1.0 - s))


def _gate_bwd(dy, wout, fo, so, pooled, proj, wbd, scale):
    S, D = dy.shape
    tm = min(256, S)

    def body(dy_ref, w_ref, fo_ref, fg_ref, so_ref, sg_ref, pl_ref, pg_ref, wbd_ref, sc_ref,
             dfo_ref, dfg_ref, dso_ref, dsg_ref, dpg_ref, dpl_ref, dsc_ref, dwbd_ref):
        @pl.when(pl.program_id(0) == 0)
        def _():
            dsc_ref[...] = jnp.zeros_like(dsc_ref)
            dwbd_ref[...] = jnp.zeros_like(dwbd_ref)

        dm = _dot_nt(dy_ref[...].astype(BF16), w_ref[...])
        dmf = dm[:, 0:FOX_W]
        dmp = dm[:, FOX_W:FOX_W + POOL_W]
        dms = dm[:, FOX_W + POOL_W:D_MIX]
        fg = fg_ref[...]
        dfo_ref[...] = dmf * _silu(fg)
        dfg_ref[...] = (dmf * fo_ref[...] * _dsilu(fg)).astype(BF16)
        sg = sg_ref[...]
        dso_ref[...] = dms * _silu(sg)
        dsg_ref[...] = (dms * so_ref[...] * _dsilu(sg)).astype(BF16)
        pg = pg_ref[...]
        plb = pl_ref[...].astype(BF16)
        yw = _dot(plb, wbd_ref[...])
        sc = sc_ref[...]
        dpg_ref[...] = (dmp * (yw * sc) * _dsilu(pg)).astype(BF16)
        dys = dmp * _silu(pg)
        dsc_ref[...] = dsc_ref[...] + jnp.sum(dys * yw, axis=0, keepdims=True)
        dyw = (dys * sc).astype(BF16)
        dpl_ref[...] = _dot_nt(dyw, wbd_ref[...])
        dwbd_ref[...] = dwbd_ref[...] + _dot_tn(plb, dyw)

    return pl.pallas_call(
        body, name="gate_bwd",
        grid=(S // tm,),
        in_specs=[pl.BlockSpec((tm, D), lambda i: (i, 0)),
                  pl.BlockSpec((D_MIX, D), lambda i: (0, 0)),
                  pl.BlockSpec((tm, FOX_W), lambda i: (i, 0)),
                  pl.BlockSpec((tm, FOX_W), lambda i: (i, OFF_FG // FOX_W)),
                  pl.BlockSpec((tm, SB_W), lambda i: (i, 0)),
                  pl.BlockSpec((tm, SB_W), lambda i: (i, OFF_SG // SB_W)),
                  pl.BlockSpec((tm, POOL_W), lambda i: (i, 0)),
                  pl.BlockSpec((tm, POOL_W), lambda i: (i, OFF_PG // POOL_W)),
                  pl.BlockSpec((POOL_W, POOL_W), lambda i: (0, 0)),
                  pl.BlockSpec((1, POOL_W), lambda i: (0, 0))],
        out_specs=[pl.BlockSpec((tm, FOX_W), lambda i: (i, 0)),
                   pl.BlockSpec((tm, FOX_W), lambda i: (i, 0)),
                   pl.BlockSpec((tm, SB_W), lambda i: (i, 0)),
                   pl.BlockSpec((tm, SB_W), lambda i: (i, 0)),
                   pl.BlockSpec((tm, POOL_W), lambda i: (i, 0)),
                   pl.BlockSpec((tm, POOL_W), lambda i: (i, 0)),
                   pl.BlockSpec((1, POOL_W), lambda i: (0, 0)),
                   pl.BlockSpec((POOL_W, POOL_W), lambda i: (0, 0))],
        out_shape=[jax.ShapeDtypeStruct((S, FOX_W), F32), jax.ShapeDtypeStruct((S, FOX_W), BF16),
                   jax.ShapeDtypeStruct((S, SB_W), F32), jax.ShapeDtypeStruct((S, SB_W), BF16),
                   jax.ShapeDtypeStruct((S, POOL_W), BF16), jax.ShapeDtypeStruct((S, POOL_W), F32),
                   jax.ShapeDtypeStruct((1, POOL_W), F32), jax.ShapeDtypeStruct((POOL_W, POOL_W), F32)],
        compiler_params=_cp(("arbitrary",), 40 << 20),
    )(dy, wout, fo, proj, so, proj, pooled, proj, wbd, scale)


def _matmul_acc(at, b, name):
    M, S = at.shape
    N = b.shape[1]
    tk = min(_TK_DW, S)
    tn = min(512, N)
    nk = S // tk

    def body(a_ref, b_ref, o_ref, acc):
        k = pl.program_id(1)

        @pl.when(k == 0)
        def _():
            acc[...] = jnp.zeros_like(acc)

        acc[...] = acc[...] + _dot(a_ref[...], b_ref[...].astype(BF16))

        @pl.when(k == nk - 1)
        def _():
            o_ref[...] = acc[...].astype(BF16)

    return pl.pallas_call(
        body, name=name,
        grid=(N // tn, nk),
        in_specs=[pl.BlockSpec((M, tk), lambda j, k: (0, k)), pl.BlockSpec((tk, tn), lambda j, k: (k, j))],
        out_specs=pl.BlockSpec((M, tn), lambda j, k: (0, j)),
        out_shape=jax.ShapeDtypeStruct((M, N), BF16),
        scratch_shapes=[pltpu.VMEM((M, tn), F32)],
        compiler_params=_cp(("parallel", "arbitrary"), 40 << 20),
    )(at, b)


def _pool_bwd(dpooled):
    S = dpooled.shape[0]

    def body(d_ref, o_ref):
        d = d_ref[...]
        t = lax.broadcasted_iota(jnp.int32, d.shape, 0)
        lane = lax.broadcasted_iota(jnp.int32, d.shape, 1)
        cnt = jnp.minimum(t + 1, _pool_window_lanes(d.shape)).astype(F32)
        u = d / cnt

        def fwd(a, k):
            return jnp.where(t < S - k, pltpu.roll(a, S - k, 0), 0.0)

        s1 = u + fwd(u, 1)
        s2 = s1 + fwd(s1, 2)
        s4 = s2 + fwd(s2, 4)
        s8 = s4 + fwd(s4, 8)
        win = jnp.where(lane < 64, s1, jnp.where(lane < 128, s2, jnp.where(lane < 192, s4, s8)))
        o_ref[...] = (win - d).astype(BF16)

    return pl.pallas_call(
        body, name="pool_bwd",
        grid=(1,),
        in_specs=[pl.BlockSpec((S, POOL_W), lambda i: (0, 0))],
        out_specs=pl.BlockSpec((S, POOL_W), lambda i: (0, 0)),
        out_shape=jax.ShapeDtypeStruct((S, POOL_W), BF16),
        compiler_params=_cp(("arbitrary",), _VMEM_BIG),
    )(dpooled)


def _fox_bwd(qs, kn, proj, dfo, fo, lse, cqb, crow4, ride=None):
    S = qs.shape[0]
    T = min(_T, S)
    nq = S // T
    n_pairs = FOX_W // 128

    def body(*refs):
        if ride is None:
            q_ref, k_ref, v_ref, do_ref, o_ref, lse_ref, cq_ref, cr_ref = refs[:8]
            dq_ref, dk_ref, dv_ref, dck_ref, dcq_ref = refs[8:13]
            scr = refs[13:]
        else:
            q_ref, k_ref, v_ref, do_ref, o_ref, lse_ref, cq_ref, cr_ref, pa_ref, pb_ref = refs[:10]
            dq_ref, dk_ref, dv_ref, dck_ref, dcq_ref, ra_ref, rb_ref = refs[10:17]
            scr = refs[17:32]
            xrefs = (pa_ref, pb_ref, ra_ref, rb_ref) + tuple(refs[32:])

            @pl.when(pl.program_id(0) == 0)
            def _():
                _start_exchange("scatter", *xrefs)

        qa, qb, kta, ktb, vb, doa, dob, cka, ckb, dcka, dckb, dva, dqt, dcqa, dcqb = scr
        lane_s = _head_masks(S)
        q = q_ref[...]
        zq = jnp.zeros_like(q)
        qa[...] = jnp.where(lane_s, q, zq)
        qb[...] = jnp.where(lane_s, zq, q)
        vb[...] = v_ref[...].astype(BF16)
        do = do_ref[...].astype(BF16)
        doa[...] = jnp.where(lane_s, do, zq)
        dob[...] = jnp.where(lane_s, zq, do)
        cq = cq_ref[...]
        cka[...], ckb[...] = _spread_heads(cq)
        zs = jnp.zeros((S, 128), F32)
        dk_ref[...] = zs
        dva[...] = zs
        dcka[...] = zs
        dckb[...] = zs
        dcq_ref[...] = jnp.zeros((8, S), F32)
        row_t = lax.broadcasted_iota(jnp.int32, (128, T), 0) < HEAD_DIM

        def prep(c, carry):
            c0 = pl.multiple_of(c * T, T)
            kt = k_ref[pl.ds(c0, T), :].astype(F32).T
            kta[:, pl.ds(c0, T)] = jnp.where(row_t, kt, 0.0).astype(BF16)
            ktb[:, pl.ds(c0, T)] = jnp.where(row_t, 0.0, kt).astype(BF16)
            return carry

        lax.fori_loop(0, nq, prep, 0)
        causal = (lax.broadcasted_iota(jnp.int32, (T, T), 0) <= lax.broadcasted_iota(jnp.int32, (T, T), 1))

        heads = ((qa, kta, doa, cka, dcka, dcqa), (qb, ktb, dob, ckb, dckb, dcqb))

        def kv(js, r0, lss, dls, masked):
            cr = cr_ref[:, pl.ds(r0, T)]
            c0s = [pl.multiple_of(j * T, T) for j in js]
            ks = [k_ref[pl.ds(c0, T), :] for c0 in c0s]
            vs = [vb[pl.ds(c0, T), :] for c0 in c0s]
            qhs = [hd[0][pl.ds(r0, T), :] for hd in heads]
            dohs = [hd[2][pl.ds(r0, T), :] for hd in heads]
            ss = []
            for h, hd in enumerate(heads):
                row = []
                for k, c0 in zip(ks, c0s):
                    s = _dot_nt(k, qhs[h]) + cr[h:h + 1, :] - jnp.tile(hd[3][pl.ds(c0, T), :], (1, T // 128))
                    row.append(jnp.where(causal, s, NEG) if masked else s)
                ss.append(row)
            ps = [[jnp.exp(s - lss[h]) for s in row] for h, row in enumerate(ss)]
            dps = [[_dot_nt(v, dohs[h]) for v in vs] for h in range(2)]
            dss = [[p * (dp - dls[h]) for p, dp in zip(ps[h], dps[h])] for h in range(2)]
            pbs = [[p.astype(BF16) for p in row] for row in ps]
            dsbs = [[ds.astype(BF16) for ds in row] for row in dss]
            for t, c0 in enumerate(c0s):
                dva[pl.ds(c0, T), :] = dva[pl.ds(c0, T), :] + (_dot(pbs[0][t], dohs[0]) + _dot(pbs[1][t], dohs[1]))
                dk_ref[pl.ds(c0, T), :] = dk_ref[pl.ds(c0, T), :] + (_dot(dsbs[0][t], qhs[0]) + _dot(dsbs[1][t], qhs[1]))
            dq = None
            for h, hd in enumerate(heads):
                for t, c0 in enumerate(c0s):
                    term = _dot(hd[1][:, pl.ds(c0, T)], dsbs[h][t])
                    dq = term if dq is None else dq + term
            dqt[...] = dqt[...] + dq
            for h, hd in enumerate(heads):
                col = jnp.sum(dss[h][0], axis=0, keepdims=True)
                for ds in dss[h][1:]:
                    col = col + jnp.sum(ds, axis=0, keepdims=True)
                hd[5][0:1, :] = hd[5][0:1, :] + col
                for ds, c0 in zip(dss[h], c0s):
                    fold = ds[:, 0:128]
                    for u in range(1, T // 128):
                        fold = fold + ds[:, 128 * u:128 * (u + 1)]
                    hd[4][pl.ds(c0, T), :] = hd[4][pl.ds(c0, T), :] - fold

        def qblk(i, carry):
            r0 = pl.multiple_of(i * T, T)
            dt = (do_ref[pl.ds(r0, T), :] * o_ref[pl.ds(r0, T), :]).T
            dla = jnp.sum(jnp.where(row_t, dt, 0.0), axis=0, keepdims=True)
            dlb = jnp.sum(jnp.where(row_t, 0.0, dt), axis=0, keepdims=True)
            ls = lse_ref[:, pl.ds(r0, T)]
            lss = (ls[0:1, :], ls[1:2, :])
            back = jnp.max(ls[2:3, :]).astype(jnp.int32)
            dqt[...] = jnp.zeros((128, T), F32)
            dcqa[...] = jnp.zeros((8, T), F32)
            dcqb[...] = jnp.zeros((8, T), F32)
            kv([i], r0, lss, (dla, dlb), True)
            _for_tiles_back(i, back, lambda js: kv(js, r0, lss, (dla, dlb), False))
            dq_ref[pl.ds(r0, T), :] = dqt[...].T
            dcq_ref[0:1, pl.ds(r0, T)] = dcqa[0:1, :]
            dcq_ref[1:2, pl.ds(r0, T)] = dcqb[0:1, :]
            return carry

        lax.fori_loop(0, nq, qblk, 0)
        dv_ref[...] = dva[...].astype(BF16)
        dck_ref[...] = jnp.where(lane_s, jnp.sum(dcka[...], axis=1, keepdims=True),
                                 jnp.sum(dckb[...], axis=1, keepdims=True))
        if ride is not None:
            @pl.when(pl.program_id(0) == n_pairs - 1)
            def _():
                _wait_exchange("scatter", *xrefs)

    extra = () if ride is None else tuple(ride)
    return pl.pallas_call(
        body, name="fox_bwd" if ride is None else "fox_bwd_exchange",
        grid=(n_pairs,),
        in_specs=[_pair_blk(S), _pair_blk(S), _pair_blk(S, OFF_FV // 128), _pair_blk(S), _pair_blk(S),
                  _pair_rows(S), _pair_blk(S), _pair_rows(S)] + [_ANY] * len(extra),
        out_specs=[_pair_blk(S), _pair_blk(S), _pair_blk(S), _pair_blk(S), _pair_rows(S)] + [_ANY] * len(extra),
        out_shape=[jax.ShapeDtypeStruct((S, FOX_W), F32), jax.ShapeDtypeStruct((S, FOX_W), F32),
                   jax.ShapeDtypeStruct((S, FOX_W), BF16), jax.ShapeDtypeStruct((S, FOX_W), F32),
                   jax.ShapeDtypeStruct((n_pairs, 8, S), F32)]
        + (_exchange_out_shapes("scatter", *extra) if extra else []),
        scratch_shapes=[pltpu.VMEM((S, 128), BF16)] * 2 + [pltpu.VMEM((128, S), BF16)] * 2
        + [pltpu.VMEM((S, 128), BF16)] * 3 + [pltpu.VMEM((S, 128), F32)] * 5
        + [pltpu.VMEM((128, T), F32)] + [pltpu.VMEM((8, T), F32)] * 2
        + (_EXCHANGE_SEMS if extra else []),
        compiler_params=_cp(("arbitrary",), _VMEM_BIG),
    )(qs, kn, proj, dfo, fo, lse, cqb, crow4, *extra)


def _sb_bwd(proj, dso, ltot, tril):
    S = proj.shape[0]
    T = tril.shape[0]
    nq = S // T

    def body(q_ref, k_ref, v_ref, do_ref, lt_ref, tri_ref, dq_ref, dk_ref, dv_ref,
             qa, qb, k2, kta, ktb, vb, doa, dob, dka, dva, dqt, ra, rb, ga, gb):
        lane_s = _head_masks(S)
        q = (q_ref[...] * Q_SCALE).astype(BF16)
        zq = jnp.zeros_like(q)
        qa[...] = jnp.where(lane_s, q, zq)
        qb[...] = jnp.where(lane_s, zq, q)
        k2[...] = k_ref[...].astype(BF16)
        vb[...] = v_ref[...].astype(BF16)
        do = do_ref[...].astype(BF16)
        doa[...] = jnp.where(lane_s, do, zq)
        dob[...] = jnp.where(lane_s, zq, do)
        dka[...] = jnp.zeros((S, 128), F32)
        dva[...] = jnp.zeros((S, 128), F32)
        row_t = lax.broadcasted_iota(jnp.int32, (128, T), 0) < HEAD_DIM

        def prep(c, carry):
            c0 = pl.multiple_of(c * T, T)
            kt = k_ref[pl.ds(c0, T), :].T
            kta[:, pl.ds(c0, T)] = jnp.where(row_t, kt, 0.0).astype(BF16)
            ktb[:, pl.ds(c0, T)] = jnp.where(row_t, 0.0, kt).astype(BF16)
            return carry

        lax.fori_loop(0, nq, prep, 0)
        strict = (lax.broadcasted_iota(jnp.int32, (T, T), 0) < lax.broadcasted_iota(jnp.int32, (T, T), 1))

        heads = ((qa, kta, doa, ra, ga), (qb, ktb, dob, rb, gb))

        def kv(j, r0, lta, ltb, masked):
            c0 = pl.multiple_of(j * T, T)
            kfull = k2[pl.ds(c0, T), :]
            v = vb[pl.ds(c0, T), :]
            tri = tri_ref[...]
            lts = (lta, ltb)
            qhs = [hd[0][pl.ds(r0, T), :] for hd in heads]
            dohs = [hd[2][pl.ds(r0, T), :] for hd in heads]
            zs = [_dot_nt(kfull, qh) for qh in qhs]
            das = [_dot_nt(v, doh) for doh in dohs]
            es, lbs = [], []
            for z in zs:
                e, sp = _softplus_parts(z)
                es.append(e)
                lbs.append(jnp.where(strict, -sp, 0.0) if masked else -sp)
            pres = [_mm2(lb, tri, left=True) for lb in lbs]
            rs = [hd[3][0:1, :] for hd in heads]
            aas = [jnp.exp(z + lb + ((lt - r) - pre)) for z, lb, lt, r, pre in zip(zs, lbs, lts, rs, pres)]
            if masked:
                aas = [jnp.where(strict, a, 0.0) for a in aas]
            gs = [a * da for a, da in zip(aas, das)]
            gpres = [_mm2(g, tri, left=True) for g in gs]
            gcs = [hd[4][0:1, :] for hd in heads]
            dzbs = []
            for z, e, g, gpre, gc in zip(zs, es, gs, gpres, gcs):
                inv = 1.0 / (1.0 + e)
                pos = z >= 0.0
                sig = jnp.where(pos, 1.0, e) * inv
                oms = jnp.where(pos, e, 1.0) * inv
                dz = g * oms - sig * (gc + (gpre - g))
                if masked:
                    dz = jnp.where(strict, dz, 0.0)
                dzbs.append(dz.astype(BF16))
            dqt[...] = dqt[...] + (_dot(heads[0][1][:, pl.ds(c0, T)], dzbs[0]) + _dot(heads[1][1][:, pl.ds(c0, T)], dzbs[1]))
            dka[pl.ds(c0, T), :] = dka[pl.ds(c0, T), :] + (_dot(dzbs[0], qhs[0]) + _dot(dzbs[1], qhs[1]))
            dva[pl.ds(c0, T), :] = dva[pl.ds(c0, T), :] + (_dot(aas[0].astype(BF16), dohs[0]) + _dot(aas[1].astype(BF16), dohs[1]))
            for hd, r, pre, gc, gpre in zip(heads, rs, pres, gcs, gpres):
                hd[3][0:1, :] = r + pre[T - 1:T, :]
                hd[4][0:1, :] = gc + gpre[T - 1:T, :]

        def qblk(i, carry):
            r0 = pl.multiple_of(i * T, T)
            lt = lt_ref[:, pl.ds(r0, T)]
            lta = lt[0:1, :]
            ltb = lt[1:2, :]
            back = jnp.max(lt[2:3, :]).astype(jnp.int32)
            zt = jnp.zeros((8, T), F32)
            dqt[...] = jnp.zeros((128, T), F32)
            ra[...] = zt
            rb[...] = zt
            ga[...] = zt
            gb[...] = zt

            def inner(j, c):
                kv(j, r0, lta, ltb, False)
                return c

            lax.fori_loop(i - back, i, inner, 0)
            kv(i, r0, lta, ltb, True)
            dq_ref[pl.ds(r0, T), :] = (dqt[...] * Q_SCALE).T.astype(BF16)
            return carry

        lax.fori_loop(0, nq, qblk, 0)
        dk_ref[...] = dka[...].astype(BF16)
        dv_ref[...] = dva[...].astype(BF16)

    return pl.pallas_call(
        body, name="sb_bwd",
        grid=(SB_W // 128,),
        in_specs=[_pair_blk(S, OFF_SQ // 128), _pair_blk(S, OFF_SK // 128), _pair_blk(S, OFF_SV // 128),
                  _pair_blk(S), _pair_rows(S), pl.BlockSpec((T, T), lambda p: (0, 0))],
        out_specs=[_pair_blk(S), _pair_blk(S), _pair_blk(S)],
        out_shape=[jax.ShapeDtypeStruct((S, SB_W), BF16)] * 3,
        scratch_shapes=([pltpu.VMEM((S, 128), BF16)] * 3 + [pltpu.VMEM((128, S), BF16)] * 2
                        + [pltpu.VMEM((S, 128), BF16)] * 3 + [pltpu.VMEM((S, 128), F32)] * 2
                        + [pltpu.VMEM((128, T), F32)] + [pltpu.VMEM((8, T), F32)] * 4),
        compiler_params=_cp(("arbitrary",), _VMEM_BIG),
    )(proj, proj, proj, dso, ltot, tril)


def _head_norm_bwd(x, g, dy, bd):
    ss = _mm2(x * x, bd)
    r = lax.rsqrt(ss * (1.0 / HEAD_DIM) + EPS)
    xr = x * r
    gdy = g * dy
    m = _mm2(xr * gdy, bd) * (1.0 / HEAD_DIM)
    return r * (gdy - xr * m), dy * xr


def _qk_bwd(dqs, dkn, proj, pff, bfp, gq, gk, bd, dccol, triu):
    S = proj.shape[0]
    T = triu.shape[0]
    n = S // T
    rev = lambda col: (lambda i: (n - 1 - i, col))

    def body(dq_ref, dk_ref, q_ref, k_ref, ff_ref, b_ref, gq_ref, gk_ref, bd_ref, dc_ref, tri_ref,
             dfq_ref, dfk_ref, dff_ref, dgq_ref, dgk_ref, dbf_ref, carry):
        @pl.when(pl.program_id(0) == 0)
        def _():
            carry[...] = jnp.zeros_like(carry)
            dgq_ref[...] = jnp.zeros_like(dgq_ref)
            dgk_ref[...] = jnp.zeros_like(dgk_ref)
            dbf_ref[...] = jnp.zeros_like(dbf_ref)

        bdv = bd_ref[...]
        dxq, gq_rows = _head_norm_bwd(q_ref[...], gq_ref[...], dq_ref[...] * Q_SCALE, bdv)
        dfq_ref[...] = dxq.astype(BF16)
        dgq_ref[...] = dgq_ref[...] + jnp.sum(gq_rows, axis=0, keepdims=True)
        dxk, gk_rows = _head_norm_bwd(k_ref[...], gk_ref[...], dk_ref[...], bdv)
        dfk_ref[...] = dxk.astype(BF16)
        dgk_ref[...] = dgk_ref[...] + jnp.sum(gk_rows, axis=0, keepdims=True)
        dlf = _mm3(dc_ref[...], tri_ref[...], left=True) + carry[0:1, :]
        carry[0:1, :] = dlf[0:1, :]
        u = ff_ref[...] + b_ref[...]
        lane = lax.broadcasted_iota(jnp.int32, u.shape, 1)
        dff = jnp.where(lane < N_FF, dlf * _sigmoid(-u), 0.0)
        dff_ref[...] = dff.astype(BF16)
        dbf_ref[...] = dbf_ref[...] + jnp.sum(dff, axis=0, keepdims=True)

    return pl.pallas_call(
        body, name="qk_bwd",
        grid=(n,),
        in_specs=[pl.BlockSpec((T, FOX_W), rev(0)), pl.BlockSpec((T, FOX_W), rev(0)),
                  pl.BlockSpec((T, FOX_W), rev(OFF_FQ // FOX_W)), pl.BlockSpec((T, FOX_W), rev(OFF_FK // FOX_W)),
                  pl.BlockSpec((T, N_FFPAD), rev(0)),
                  pl.BlockSpec((1, N_FFPAD), lambda i: (0, 0)),
                  pl.BlockSpec((1, FOX_W), lambda i: (0, 0)), pl.BlockSpec((1, FOX_W), lambda i: (0, 0)),
                  pl.BlockSpec((FOX_W, FOX_W), lambda i: (0, 0)),
                  pl.BlockSpec((T, N_FFPAD), rev(0)),
                  pl.BlockSpec((T, T), lambda i: (0, 0))],
        out_specs=[pl.BlockSpec((T, FOX_W), rev(0)), pl.BlockSpec((T, FOX_W), rev(0)),
                   pl.BlockSpec((T, N_FFPAD), rev(0)),
                   pl.BlockSpec((1, FOX_W), lambda i: (0, 0)), pl.BlockSpec((1, FOX_W), lambda i: (0, 0)),
                   pl.BlockSpec((1, N_FFPAD), lambda i: (0, 0))],
        out_shape=[jax.ShapeDtypeStruct((S, FOX_W), BF16), jax.ShapeDtypeStruct((S, FOX_W), BF16),
                   jax.ShapeDtypeStruct((S, N_FFPAD), BF16),
                   jax.ShapeDtypeStruct((1, FOX_W), F32), jax.ShapeDtypeStruct((1, FOX_W), F32),
                   jax.ShapeDtypeStruct((1, N_FFPAD), F32)],
        scratch_shapes=[pltpu.VMEM((8, N_FFPAD), F32)],
        compiler_params=_cp(("arbitrary",), 40 << 20),
    )(dqs, dkn, proj, proj, pff, bfp, gq, gk, bd, dccol, triu)


def _inproj_bwd_dx(dpm, dff, wm, wff, x, g, dy):
    S, D = x.shape
    tm = min(_TM_DX, S)

    def body(dp_ref, dff_ref, w_ref, wff_ref, x_ref, g_ref, dy_ref, dx_ref, dg_ref):
        @pl.when(pl.program_id(0) == 0)
        def _():
            dg_ref[...] = jnp.zeros_like(dg_ref)

        dh = _dot_nt(dp_ref[...], w_ref[...]) + _dot_nt(dff_ref[...], wff_ref[...])
        xv = x_ref[...]
        r = _rms_rows(xv)
        xr = xv * r
        dg_ref[...] = dg_ref[...] + jnp.sum(dh * xr, axis=0, keepdims=True)
        gdh = g_ref[...] * dh
        m = jnp.mean(gdh * xr, axis=-1, keepdims=True)
        dx_ref[...] = dy_ref[...] + r * (gdh - xr * m)

    return pl.pallas_call(
        body, name="inproj_bwd_dx",
        grid=(S // tm,),
        in_specs=[pl.BlockSpec((tm, N_MAIN), lambda i: (i, 0)),
                  pl.BlockSpec((tm, N_FFPAD), lambda i: (i, 0)),
                  pl.BlockSpec((D, N_MAIN), lambda i: (0, 0)),
                  pl.BlockSpec((D, N_FFPAD), lambda i: (0, 0)),
                  pl.BlockSpec((tm, D), lambda i: (i, 0)),
                  pl.BlockSpec((1, D), lambda i: (0, 0)),
                  pl.BlockSpec((tm, D), lambda i: (i, 0))],
        out_specs=[pl.BlockSpec((tm, D), lambda i: (i, 0)), pl.BlockSpec((1, D), lambda i: (0, 0))],
        out_shape=[jax.ShapeDtypeStruct((S, D), F32), jax.ShapeDtypeStruct((1, D), F32)],
        compiler_params=_cp(("arbitrary",), 48 << 20),
    )(dpm, dff, wm, wff, x, g, dy)


def _inproj_bwd_dw(ht, dpm, dff):
    D, S = ht.shape
    tk = min(_TK_DW, S)
    tn = 512
    nk = S // tk

    def body(ht_ref, dp_ref, dff_ref, dw_ref, dwff_ref, acc, accff):
        j, k = pl.program_id(0), pl.program_id(1)

        @pl.when(k == 0)
        def _():
            acc[...] = jnp.zeros_like(acc)

        @pl.when((k == 0) & (j == 0))
        def _():
            accff[...] = jnp.zeros_like(accff)

        acc[...] = acc[...] + _dot(ht_ref[...], dp_ref[...])

        @pl.when(j == 0)
        def _():
            accff[...] = accff[...] + _dot(ht_ref[...], dff_ref[...])

        @pl.when(k == nk - 1)
        def _():
            dw_ref[...] = acc[...].astype(BF16)

        @pl.when((k == nk - 1) & (j == 0))
        def _():
            dwff_ref[...] = accff[...].astype(BF16)

    return pl.pallas_call(
        body, name="inproj_bwd_dw",
        grid=(N_MAIN // tn, nk),
        in_specs=[pl.BlockSpec((D, tk), lambda j, k: (0, k)),
                  pl.BlockSpec((tk, tn), lambda j, k: (k, j)),
                  pl.BlockSpec((tk, N_FFPAD), lambda j, k: (k, 0))],
        out_specs=[pl.BlockSpec((D, tn), lambda j, k: (0, j)), pl.BlockSpec((D, N_FFPAD), lambda j, k: (0, 0))],
        out_shape=[jax.ShapeDtypeStruct((D, N_MAIN), BF16), jax.ShapeDtypeStruct((D, N_FFPAD), BF16)],
        scratch_shapes=[pltpu.VMEM((D, tn), F32), pltpu.VMEM((D, N_FFPAD), F32)],
        compiler_params=_cp(("arbitrary", "arbitrary"), 40 << 20),
    )(ht, dpm, dff)


def _constants(T):
    tril = jnp.tril(jnp.ones((T, T), F32)).astype(BF16)
    hid = jnp.arange(FOX_W) // HEAD_DIM
    bd = (hid[:, None] == hid[None, :]).astype(BF16)
    ex = (jnp.arange(N_FFPAD)[:, None] == hid[None, :]).astype(BF16)
    return tril, tril.T, bd, ex


def _crow4(ccol):
    S = ccol.shape[0]
    c = ccol[:, :FOX_HEADS].T.reshape(FOX_HEADS // 2, 2, S)
    return jnp.pad(c, ((0, 0), (0, 6), (0, 0)))


def _layer_fwd(x, lw, consts, ride=None):
    tril, triu, bd, ex = consts
    proj, pff, ht = _inproj_fwd(x, lw["g"], lw["wm"], lw["wff"])
    qs, kn, ccol, cqb = _fox_prep(proj, pff, lw["bfp"], lw["gq"], lw["gk"], bd, ex, tril)
    crow4 = _crow4(ccol)
    fo, lse, *gathered = _fox_fwd(qs, kn, proj, cqb, crow4, ride)
    so, ltot = _sb_fwd(proj, triu)
    pooled = _pool_fwd(proj)
    y, mixedt = _mix_out(fo, so, pooled, proj, lw["wbd"], lw["scale"], lw["wout"], x)
    return y, (x, proj, pff, ht, qs, kn, cqb, crow4, fo, lse, so, ltot, pooled, mixedt), gathered


def _layer_bwd(dy, saved, lw, consts, ride=None):
    tril, triu, bd, _ = consts
    x, proj, pff, ht, qs, kn, cqb, crow4, fo, lse, so, ltot, pooled, mixedt = saved
    S = x.shape[0]
    dfo, dfg, dso, dsg, dpg, dpooled, dscale, dwbd = _gate_bwd(dy, lw["wout"], fo, so, pooled, proj, lw["wbd"], lw["scale"])
    dwout = _matmul_acc(mixedt, dy, "dw_out")
    dpx = _pool_bwd(dpooled)
    dqs, dkn, dfv, dck, dcq4, *received = _fox_bwd(qs, kn, proj, dfo, fo, lse, cqb, crow4, ride)
    dsq, dsk, dsv = _sb_bwd(proj, dso, ltot, tril)
    dc8 = dck[:, ::HEAD_DIM] + dcq4[:, :2, :].reshape(FOX_HEADS, S).T
    dccol = jnp.pad(dc8, ((0, 0), (0, N_FFPAD - FOX_HEADS)))
    dfq, dfk, dff, dgq, dgk, dbf = _qk_bwd(dqs, dkn, proj, pff, lw["bfp"], lw["gq"], lw["gk"], bd, dccol, triu)
    dpm = jnp.concatenate([dfq, dfk, dfv, dfg, dpx, dpg, dsq, dsk, dsv, dsg], axis=1)
    dx, dng = _inproj_bwd_dx(dpm, dff, lw["wm"], lw["wff"], x, lw["g"], dy)
    dwm, dwff = _inproj_bwd_dw(ht, dpm, dff)
    dwin = jnp.concatenate([dwm[:, :OFF_PX], dwff[:, :N_FF], dwm[:, OFF_PX:]], axis=1)
    grads = {
        "norm_g": dng[0],
        "w_in": dwin,
        "b_f": dbf[0, :N_FF],
        "q_norm_g": dgq[0].reshape(FOX_HEADS, HEAD_DIM).sum(0),
        "k_norm_g": dgk[0].reshape(FOX_HEADS, HEAD_DIM).sum(0),
        "w_pool": jnp.stack([dwbd[64 * i:64 * i + 64, 64 * i:64 * i + 64] for i in range(4)]),
        "pool_scale": dscale[0],
        "w_out": dwout,
    }
    return dx, grads, received


def _layer_weights(l, norm_g, gin, b_f, q_norm_g, k_norm_g, w_pool, pool_scale, gout):
    D = gin.shape[1]
    w = gin.transpose(1, 0, 2).reshape(D, D_IN)
    wm = jnp.concatenate([w[:, :2048], w[:, 2048 + N_FF:]], axis=1)
    wff = jnp.pad(w[:, 2048:2048 + N_FF], ((0, 0), (0, N_FFPAD - N_FF)))
    grp = jnp.arange(POOL_W) // 64
    wbd = jnp.where(grp[:, None] == grp[None, :], jnp.tile(w_pool[l].transpose(1, 0, 2).reshape(64, POOL_W), (4, 1)), 0.0)
    return {
        "g": norm_g[l].reshape(1, D),
        "wm": wm, "wff": wff,
        "bfp": jnp.pad(b_f[l], (0, N_FFPAD - N_FF)).reshape(1, N_FFPAD),
        "gq": jnp.tile(q_norm_g[l], FOX_HEADS).reshape(1, FOX_W),
        "gk": jnp.tile(k_norm_g[l], FOX_HEADS).reshape(1, FOX_W),
        "wbd": wbd.astype(BF16),
        "scale": pool_scale[l].reshape(1, POOL_W),
        "wout": gout.reshape(D_MIX, D),
    }


def _grad_parts(g):
    dwin, dwout = g["w_in"].astype(BF16), g["w_out"].astype(BF16)
    D = dwin.shape[0]
    return (dwin.reshape(D, N_DEV, D_IN // N_DEV).transpose(1, 0, 2),
            dwout.reshape(N_DEV, D_MIX // N_DEV, dwout.shape[1]))


def _train_step(x, target, norm_g, win_sh, b_f, q_norm_g, k_norm_g, w_pool, pool_scale, wout_sh):
    L = norm_g.shape[0]
    consts = _constants(min(_T, x.shape[0]))
    gathered = _exchange_pair("gather", win_sh[0], wout_sh[0], "gather_weights")
    lws, saved = [], []
    h = x
    for l in range(L):
        lws.append(_layer_weights(l, norm_g, gathered[0], b_f, q_norm_g, k_norm_g, w_pool, pool_scale, gathered[1]))
        ride = (win_sh[l + 1], wout_sh[l + 1]) if l + 1 < L else None
        h, sv, gathered = _layer_fwd(h, lws[l], consts, ride)
        saved.append(sv)
    dy, loss = _loss_head(h, target)
    grads, received = [None] * L, [None] * L
    ride = None
    for l in reversed(range(L)):
        dy, grads[l], got = _layer_bwd(dy, saved[l], lws[l], consts, ride)
        if ride is not None:
            received[l + 1] = got
        ride = _grad_parts(grads[l])
    received[0] = _exchange_pair("scatter", ride[0], ride[1], "exchange_grads")
    return loss, dy, grads, received


def _mesh_pos():
    return lax.axis_index("x"), lax.axis_index("y"), lax.axis_index("c")


_FLIPS = [(0, 0, 1), (1, 0, 0), (0, 1, 0), (1, 1, 0), (1, 0, 1), (0, 1, 1), (1, 1, 1)]


def _peers():
    x, y, c = _mesh_pos()
    out = []
    for fx, fy, fc in _FLIPS:
        px = 1 - x if fx else x
        py = 1 - y if fy else y
        pc = 1 - c if fc else c
        out.append(((px, py, pc), 4 * px + 2 * py + pc))
    return out, 4 * x + 2 * y + c


_EXCHANGE_SEMS = [pltpu.SemaphoreType.DMA((14,)), pltpu.SemaphoreType.DMA((14,)), pltpu.SemaphoreType.DMA((2,))]
_ANY = pl.BlockSpec(memory_space=pl.ANY)


def _exchange_copies(kind, a_ref, b_ref, oa_ref, ob_ref, send_sems, recv_sems, loc_sems):
    peers, me = _peers()
    pairs = ((a_ref, oa_ref), (b_ref, ob_ref))
    local = [pltpu.make_async_copy(src if kind == "gather" else src.at[me], dst.at[me], loc_sems.at[t])
             for t, (src, dst) in enumerate(pairs)]
    remote = []
    for k, (dev, idx) in enumerate(peers):
        for t, (src, dst) in enumerate(pairs):
            remote.append(pltpu.make_async_remote_copy(
                src_ref=src if kind == "gather" else src.at[idx], dst_ref=dst.at[me],
                send_sem=send_sems.at[2 * k + t], recv_sem=recv_sems.at[2 * k + t],
                device_id=dev, device_id_type=pl.DeviceIdType.MESH))
    return local, remote


def _start_exchange(kind, *refs):
    local, remote = _exchange_copies(kind, *refs)
    for cp in local + remote:
        cp.start()


def _wait_exchange(kind, *refs):
    local, remote = _exchange_copies(kind, *refs)
    for cp in remote:
        cp.wait_recv()
    for cp in remote:
        cp.wait_send()
    for cp in local:
        cp.wait()


def _exchange_out_shapes(kind, a, b):
    if kind == "gather":
        return [jax.ShapeDtypeStruct((N_DEV,) + a.shape, a.dtype), jax.ShapeDtypeStruct((N_DEV,) + b.shape, b.dtype)]
    return [jax.ShapeDtypeStruct(a.shape, a.dtype), jax.ShapeDtypeStruct(b.shape, b.dtype)]


def _exchange_pair(kind, a, b, name):
    def body(*refs):
        _start_exchange(kind, *refs)
        _wait_exchange(kind, *refs)

    return pl.pallas_call(
        body, name=name,
        in_specs=[_ANY, _ANY], out_specs=[_ANY, _ANY],
        out_shape=_exchange_out_shapes(kind, a, b),
        scratch_shapes=_EXCHANGE_SEMS,
    )(a, b)


def _adam_math(w, g, m, v):
    m_new = ADAM_B1 * m + (1.0 - ADAM_B1) * g
    v_new = ADAM_B2 * v + (1.0 - ADAM_B2) * (g * g)
    m_hat = m_new / (1.0 - ADAM_B1 ** ADAM_STEP)
    v_hat = v_new / (1.0 - ADAM_B2 ** ADAM_STEP)
    delta = -ADAM_LR * (m_hat / (jnp.sqrt(v_hat) + ADAM_EPS) + ADAM_WD * w)
    return delta, m_new, v_new


def _sum_adamw(gparts, w, m, v, name):
    L, R, C = w.shape
    tr = min(128, R)

    def body(*refs):
        gp_refs = refs[:L]
        w_ref, m_ref, v_ref, g_ref, d_ref, nm_ref, nv_ref = refs[L:]
        for l in range(L):
            g = gp_refs[l][0].astype(F32)
            for s in range(1, N_DEV):
                g = g + gp_refs[l][s].astype(F32)
            d, mn, vn = _adam_math(w_ref[l], g, m_ref[l], v_ref[l])
            g_ref[l] = g
            d_ref[l] = d
            nm_ref[l] = mn
            nv_ref[l] = vn

    blk = pl.BlockSpec((L, tr, C), lambda r: (0, r, 0))
    return pl.pallas_call(
        body, name=name,
        grid=(R // tr,),
        in_specs=[pl.BlockSpec((N_DEV, tr, C), lambda r: (0, r, 0))] * L + [blk, blk, blk],
        out_specs=[blk, blk, blk, blk],
        out_shape=[jax.ShapeDtypeStruct((L, R, C), F32)] * 4,
        compiler_params=_cp(("parallel",), 48 << 20),
    )(*gparts, w, m, v)


def _small_update(gpack, wpack, mpack, vpack):
    R = gpack.shape[0]
    VM = pl.BlockSpec(memory_space=pltpu.VMEM)

    def body(g_ref, w_ref, m_ref, v_ref, gs_ref, d_ref, nm_ref, nv_ref, buf, send_sems, recv_sems):
        peers, me = _peers()
        buf[me] = g_ref[...]
        copies = []
        for k, (dev, _) in enumerate(peers):
            cp = pltpu.make_async_remote_copy(
                src_ref=g_ref, dst_ref=buf.at[me], send_sem=send_sems.at[k], recv_sem=recv_sems.at[k],
                device_id=dev, device_id_type=pl.DeviceIdType.MESH)
            cp.start()
            copies.append(cp)
        for cp in copies:
            cp.wait_recv()
        for cp in copies:
            cp.wait_send()
        g = buf[0]
        for s in range(1, N_DEV):
            g = g + buf[s]
        d, mn, vn = _adam_math(w_ref[...], g, m_ref[...], v_ref[...])
        gs_ref[...] = g
        d_ref[...] = d
        nm_ref[...] = mn
        nv_ref[...] = vn

    return pl.pallas_call(
        body, name="small_update",
        in_specs=[VM] * 4, out_specs=[VM] * 4,
        out_shape=[jax.ShapeDtypeStruct((R, 128), F32)] * 4,
        scratch_shapes=[pltpu.VMEM((N_DEV, R, 128), F32), pltpu.SemaphoreType.DMA((7,)), pltpu.SemaphoreType.DMA((7,))],
        compiler_params=_cp(None, 40 << 20),
    )(gpack, wpack, mpack, vpack)


_SMALL = ("norm_g", "b_f", "q_norm_g", "k_norm_g", "w_pool", "pool_scale")


def _pack(parts):
    flat = jnp.concatenate([p.reshape(-1) for p in parts])
    n = flat.shape[0]
    rows = -(-n // (8 * 128)) * 8
    return jnp.pad(flat, (0, rows * 128 - n)).reshape(rows, 128)


def _unpack(packed, like):
    flat = packed.reshape(-1)
    out, o = [], 0
    for p in like:
        out.append(flat[o:o + p.size].reshape(p.shape))
        o += p.size
    return out


def kernel(x, norm_g, w_in, b_f, q_norm_g, k_norm_g, w_pool, pool_scale, w_out, loss_target, m_norm_g, m_w_in, m_b_f, m_q_norm_g, m_k_norm_g, m_w_pool, m_pool_scale, m_w_out, v_norm_g, v_w_in, v_b_f, v_q_norm_g, v_k_norm_g, v_w_pool, v_pool_scale, v_w_out):
    L = w_in.shape[0]

    loss_local, dx, grads, received = _train_step(x[0], loss_target[0], norm_g, w_in.astype(BF16), b_f, q_norm_g,
                                                  k_norm_g, w_pool, pool_scale, w_out.astype(BF16))
    loss = lax.psum(loss_local, MESH_AXES)
    g = {k: jnp.stack([grads[l][k] for l in range(L)]) for k in _SMALL}

    g_win, d_win, nm_win, nv_win = _sum_adamw([r[0] for r in received], w_in, m_w_in, v_w_in, "adamw_w_in")
    g_wout, d_wout, nm_wout, nv_wout = _sum_adamw([r[1] for r in received], w_out, m_w_out, v_w_out, "adamw_w_out")

    ws = dict(norm_g=norm_g, b_f=b_f, q_norm_g=q_norm_g, k_norm_g=k_norm_g, w_pool=w_pool, pool_scale=pool_scale)
    ms = dict(norm_g=m_norm_g, b_f=m_b_f, q_norm_g=m_q_norm_g, k_norm_g=m_k_norm_g, w_pool=m_w_pool, pool_scale=m_pool_scale)
    vs = dict(norm_g=v_norm_g, b_f=v_b_f, q_norm_g=v_q_norm_g, k_norm_g=v_k_norm_g, w_pool=v_w_pool, pool_scale=v_pool_scale)
    like = [ws[k] for k in _SMALL]
    gs_p, d_p, nm_p, nv_p = _small_update(_pack([g[k] for k in _SMALL]), _pack(like),
                                          _pack([ms[k] for k in _SMALL]), _pack([vs[k] for k in _SMALL]))
    gs = dict(zip(_SMALL, _unpack(gs_p, like)))
    ds = dict(zip(_SMALL, _unpack(d_p, like)))
    nms = dict(zip(_SMALL, _unpack(nm_p, like)))
    nvs = dict(zip(_SMALL, _unpack(nv_p, like)))
    gs["w_in"], ds["w_in"], nms["w_in"], nvs["w_in"] = g_win, d_win, nm_win, nv_win
    gs["w_out"], ds["w_out"], nms["w_out"], nvs["w_out"] = g_wout, d_wout, nm_wout, nv_wout

    order = ("norm_g", "w_in", "b_f", "q_norm_g", "k_norm_g", "w_pool", "pool_scale", "w_out")
    return (loss, dx[None], *[gs[k] for k in order], *[ds[k] for k in order],
            *[nms[k] for k in order], *[nvs[k] for k in order])
```

```python
import functools

import jax
import jax.numpy as jnp
from jax import lax
from jax.experimental import pallas as pl
from jax.experimental.pallas import tpu as pltpu

F32 = jnp.float32
BF16 = jnp.bfloat16

EPS = 1e-6
NEG = -1e30
HEAD_DIM = 64
FOX_HEADS = 8
FOX_W = 512
POOL_W = 256
SB_W = 256
D_MIX = 1024
N_FF = 8
N_MAIN = 3584
N_FFPAD = 128
OFF_FQ, OFF_FK, OFF_FV, OFF_FG = 0, 512, 1024, 1536
OFF_PX, OFF_PG = 2048, 2304
OFF_SQ, OFF_SK, OFF_SV, OFF_SG = 2560, 2816, 3072, 3328
D_IN = 3592
Q_SCALE = HEAD_DIM ** -0.5

ADAM_LR = 0.001
ADAM_B1 = 0.9
ADAM_B2 = 0.999
ADAM_EPS = 1e-08
ADAM_WD = 0.01
ADAM_STEP = 10

N_DEV = 8
MESH_AXES = ("x", "y", "c")

_T = 256
_TM = 512
_TM_FWD, _TN_FWD = 2048, 512
_TM_DX = 512
_TK_DW = 1024
_VMEM_BIG = 56 << 20


def _cp(sem=None, vmem=None):
    kw = {}
    if sem is not None:
        kw["dimension_semantics"] = sem
    if vmem is not None:
        kw["vmem_limit_bytes"] = vmem
    return pltpu.CompilerParams(**kw)


def _dot(a, b):
    return jnp.dot(a, b, preferred_element_type=F32)


def _dot_nt(a, b):
    return lax.dot_general(a, b, (((1,), (1,)), ((), ())), preferred_element_type=F32)


def _dot_tn(a, b):
    return lax.dot_general(a, b, (((0,), (0,)), ((), ())), preferred_element_type=F32)


def _mm2(v, m, left=False):
    hi = v.astype(BF16)
    lo = (v - hi.astype(F32)).astype(BF16)
    if left:
        return _dot(m, hi) + _dot(m, lo)
    return _dot(hi, m) + _dot(lo, m)


def _mm3(v, m, left=False):
    a1 = v.astype(BF16)
    r1 = v - a1.astype(F32)
    a2 = r1.astype(BF16)
    a3 = (r1 - a2.astype(F32)).astype(BF16)
    if left:
        return _dot(m, a1) + _dot(m, a2) + _dot(m, a3)
    return _dot(a1, m) + _dot(a2, m) + _dot(a3, m)


def _sigmoid(z):
    return 1.0 / (1.0 + jnp.exp(-z))


def _rms_rows(x):
    return lax.rsqrt(jnp.mean(x * x, axis=-1, keepdims=True) + EPS)


def _inproj_fwd(x, g, wm, wff):
    S, D = x.shape
    tm = min(_TM_FWD, S)
    tn = _TN_FWD

    def body(x_ref, g_ref, w_ref, wff_ref, o_ref, off_ref, ht_ref, h_ref):
        @pl.when(pl.program_id(1) == 0)
        def _():
            xv = x_ref[...]
            h = (xv * _rms_rows(xv)) * g_ref[...]
            h_ref[...] = h.astype(BF16)
            ht_ref[...] = h.T.astype(BF16)
            off_ref[...] = _dot(h_ref[...], wff_ref[...])

        o_ref[...] = _dot(h_ref[...], w_ref[...])

    return pl.pallas_call(
        body, name="inproj_fwd",
        grid=(S // tm, N_MAIN // tn),
        in_specs=[pl.BlockSpec((tm, D), lambda i, j: (i, 0)),
                  pl.BlockSpec((1, D), lambda i, j: (0, 0)),
                  pl.BlockSpec((D, tn), lambda i, j: (0, j)),
                  pl.BlockSpec((D, N_FFPAD), lambda i, j: (0, 0))],
        out_specs=[pl.BlockSpec((tm, tn), lambda i, j: (i, j)),
                   pl.BlockSpec((tm, N_FFPAD), lambda i, j: (i, 0)),
                   pl.BlockSpec((D, tm), lambda i, j: (0, i))],
        out_shape=[jax.ShapeDtypeStruct((S, N_MAIN), F32), jax.ShapeDtypeStruct((S, N_FFPAD), F32),
                   jax.ShapeDtypeStruct((D, S), BF16)],
        scratch_shapes=[pltpu.VMEM((tm, D), BF16)],
        compiler_params=_cp(("parallel", "arbitrary"), 48 << 20),
    )(x, g, wm, wff)


def _head_norm(x, g, bd):
    ss = _mm2(x * x, bd)
    r = lax.rsqrt(ss * (1.0 / HEAD_DIM) + EPS)
    return (x * r) * g


def _fox_prep(proj, pff, bfp, gq, gk, bd, ex, tril):
    S = proj.shape[0]
    T = tril.shape[0]

    def body(q_ref, k_ref, ff_ref, b_ref, gq_ref, gk_ref, bd_ref, ex_ref, tri_ref,
             qs_ref, kn_ref, cc_ref, cqb_ref, carry):
        @pl.when(pl.program_id(0) == 0)
        def _():
            carry[...] = jnp.zeros_like(carry)

        bdv = bd_ref[...]
        qs_ref[...] = (_head_norm(q_ref[...], gq_ref[...], bdv) * Q_SCALE).astype(BF16)
        kn_ref[...] = _head_norm(k_ref[...], gk_ref[...], bdv).astype(BF16)
        u = ff_ref[...] + b_ref[...]
        lf = jnp.minimum(u, 0.0) - jnp.log1p(jnp.exp(-jnp.abs(u)))
        c = _mm3(lf, tri_ref[...], left=True) + carry[0:1, :]
        carry[0:1, :] = c[T - 1:T, :]
        cc_ref[...] = c
        cqb_ref[...] = _mm3(c, ex_ref[...])

    return pl.pallas_call(
        body, name="fox_prep",
        grid=(S // T,),
        in_specs=[pl.BlockSpec((T, FOX_W), lambda i: (i, OFF_FQ // FOX_W)),
                  pl.BlockSpec((T, FOX_W), lambda i: (i, OFF_FK // FOX_W)),
                  pl.BlockSpec((T, N_FFPAD), lambda i: (i, 0)),
                  pl.BlockSpec((1, N_FFPAD), lambda i: (0, 0)),
                  pl.BlockSpec((1, FOX_W), lambda i: (0, 0)),
                  pl.BlockSpec((1, FOX_W), lambda i: (0, 0)),
                  pl.BlockSpec((FOX_W, FOX_W), lambda i: (0, 0)),
                  pl.BlockSpec((N_FFPAD, FOX_W), lambda i: (0, 0)),
                  pl.BlockSpec((T, T), lambda i: (0, 0))],
        out_specs=[pl.BlockSpec((T, FOX_W), lambda i: (i, 0)),
                   pl.BlockSpec((T, FOX_W), lambda i: (i, 0)),
                   pl.BlockSpec((T, N_FFPAD), lambda i: (i, 0)),
                   pl.BlockSpec((T, FOX_W), lambda i: (i, 0))],
        out_shape=[jax.ShapeDtypeStruct((S, FOX_W), BF16), jax.ShapeDtypeStruct((S, FOX_W), BF16),
                   jax.ShapeDtypeStruct((S, N_FFPAD), F32), jax.ShapeDtypeStruct((S, FOX_W), F32)],
        scratch_shapes=[pltpu.VMEM((8, N_FFPAD), F32)],
        compiler_params=_cp(("arbitrary",), 40 << 20),
    )(proj, proj, pff, bfp, gq, gk, bd, ex, tril)


def _pair_blk(S, off=0):
    return pl.BlockSpec((S, 128), lambda p: (0, off + p), pipeline_mode=pl.Buffered(1))


def _pair_rows(S):
    return pl.BlockSpec((None, 8, S), lambda p: (p, 0, 0), pipeline_mode=pl.Buffered(1))


def _head_masks(S):
    return lax.broadcasted_iota(jnp.int32, (S, 128), 1) < HEAD_DIM


_EXP_ZERO = 104.0


def _spread_heads(x):
    src = lax.broadcasted_iota(jnp.int32, (128, 128), 0)
    return (_mm3(x, (src == 0).astype(BF16)), _mm3(x, (src == HEAD_DIM).astype(BF16)))


def _score_bounds(q, k):
    same_head = ((lax.broadcasted_iota(jnp.int32, (128, 128), 0) < HEAD_DIM)
                 == (lax.broadcasted_iota(jnp.int32, (128, 128), 1) < HEAD_DIM)).astype(BF16)

    def max_norm2(x):
        xf = x.astype(F32)
        return jnp.max(_mm2(xf * xf, same_head), axis=0, keepdims=True)

    z = jnp.sqrt(max_norm2(q) * max_norm2(k))
    return jnp.max(z[:, 0:1]) * 1.001 + 1e-3, jnp.max(z[:, 64:65]) * 1.001 + 1e-3


def _for_tiles_back(i, n, tiles_fn):
    def two(t, c):
        tiles_fn([i - 1 - 2 * t, i - 2 - 2 * t])
        return c

    lax.fori_loop(0, lax.shift_right_logical(n, 1), two, 0)

    @pl.when((n & 1) == 1)
    def _():
        tiles_fn([i - n])


def _fox_tiles_back(cr_ref, i, r0, zba, zbb):
    last = cr_ref[:, pl.ds(0, 128)]
    first = cr_ref[:, pl.ds(r0, 128)]
    alive_a = 2.0 * zba + first[0:1, 0:1] - last[2:3, :] > -_EXP_ZERO
    alive_b = 2.0 * zbb + first[1:2, 0:1] - last[3:4, :] > -_EXP_ZERO
    before = lax.broadcasted_iota(jnp.int32, (1, 128), 1) < i
    return jnp.sum((before & (alive_a | alive_b)).astype(jnp.int32))


def _fox_fwd(qs, kn, proj, cqb, crow4, ride=None):
    S = qs.shape[0]
    T = min(_T, S)
    nq = S // T
    n_pairs = FOX_W // 128

    def body(*refs):
        if ride is None:
            q_ref, k_ref, v_ref, cq_ref, cr_ref, o_ref, lse_ref = refs[:7]
            qa, qb, vta, vtb, cka, ckb, ma, mb, acca, accb = refs[7:]
        else:
            q_ref, k_ref, v_ref, cq_ref, cr_ref, wa_ref, wb_ref, o_ref, lse_ref, ga_ref, gb_ref = refs[:11]
            qa, qb, vta, vtb, cka, ckb, ma, mb, acca, accb = refs[11:21]
            xrefs = (wa_ref, wb_ref, ga_ref, gb_ref) + tuple(refs[21:])

            @pl.when(pl.program_id(0) == 0)
            def _():
                _start_exchange("gather", *xrefs)

        lane_s = _head_masks(S)
        q = q_ref[...]
        zq = jnp.zeros_like(q)
        qa[...] = jnp.where(lane_s, q, zq)
        qb[...] = jnp.where(lane_s, zq, q)
        cq = cq_ref[...]
        cka[...], ckb[...] = _spread_heads(cq)
        lse_ref[...] = jnp.zeros((8, S), F32)
        row_t = lax.broadcasted_iota(jnp.int32, (128, T), 0) < HEAD_DIM
        zba, zbb = _score_bounds(q, k_ref[...])

        def prep(c, carry):
            c0 = pl.multiple_of(c * T, T)
            vt = v_ref[pl.ds(c0, T), :].T
            vta[:, pl.ds(c0, T)] = jnp.where(row_t, vt, 1.0).astype(BF16)
            vtb[:, pl.ds(c0, T)] = jnp.where(row_t, 1.0, vt).astype(BF16)
            return carry

        lax.fori_loop(0, nq, prep, 0)
        causal = (lax.broadcasted_iota(jnp.int32, (T, T), 0) <= lax.broadcasted_iota(jnp.int32, (T, T), 1))

        heads = ((qa, vta, cka, ma, acca), (qb, vtb, ckb, mb, accb))

        def kv(js, r0, masked):
            cr = cr_ref[:, pl.ds(r0, T)]
            c0s = [pl.multiple_of(j * T, T) for j in js]
            ks = [k_ref[pl.ds(c0, T), :] for c0 in c0s]
            ss = []
            for h, (qr, _, ckr, _, _) in enumerate(heads):
                qh = qr[pl.ds(r0, T), :]
                row = []
                for k, c0 in zip(ks, c0s):
                    s = _dot_nt(k, qh) + cr[h:h + 1, :] - jnp.tile(ckr[pl.ds(c0, T), :], (1, T // 128))
                    row.append(jnp.where(causal, s, NEG) if masked else s)
                ss.append(row)
            ms = []
            for row, (_, _, _, mr, _) in zip(ss, heads):
                top = row[0]
                for s in row[1:]:
                    top = jnp.maximum(top, s)
                m_old = mr[0:1, :]
                ms.append((m_old, jnp.maximum(m_old, jnp.max(top, axis=0, keepdims=True))))
            ps = [[jnp.exp(s - m_new).astype(BF16) for s in row] for row, (_, m_new) in zip(ss, ms)]
            pvs = []
            for row, (_, vr, _, _, _) in zip(ps, heads):
                pv = _dot(vr[:, pl.ds(c0s[0], T)], row[0])
                for p, c0 in zip(row[1:], c0s[1:]):
                    pv = pv + _dot(vr[:, pl.ds(c0, T)], p)
                pvs.append(pv)
            for pv, (m_old, m_new), (_, _, _, mr, ar) in zip(pvs, ms, heads):
                ar[...] = jnp.exp(m_old - m_new) * ar[...] + pv
                mr[0:1, :] = m_new

        def qblk(i, carry):
            r0 = pl.multiple_of(i * T, T)
            ma[...] = jnp.full((8, T), NEG, F32)
            mb[...] = jnp.full((8, T), NEG, F32)
            acca[...] = jnp.zeros((128, T), F32)
            accb[...] = jnp.zeros((128, T), F32)
            kv([i], r0, True)
            done = _fox_tiles_back(cr_ref, i, r0, zba, zbb)
            _for_tiles_back(i, done, lambda js: kv(js, r0, False))
            aa = acca[...]
            ab = accb[...]
            la = aa[64:65, :]
            lb = ab[0:1, :]
            o_ref[pl.ds(r0, T), :] = jnp.where(row_t, aa / la, ab / lb).T
            lse_ref[0:1, pl.ds(r0, T)] = ma[0:1, :] + jnp.log(la)
            lse_ref[1:2, pl.ds(r0, T)] = mb[0:1, :] + jnp.log(lb)
            lse_ref[2:3, pl.ds(r0, T)] = jnp.broadcast_to(done.astype(F32), (1, T))
            return carry

        lax.fori_loop(0, nq, qblk, 0)
        if ride is not None:
            @pl.when(pl.program_id(0) == n_pairs - 1)
            def _():
                _wait_exchange("gather", *xrefs)

    extra = () if ride is None else tuple(ride)
    return pl.pallas_call(
        body, name="fox_fwd" if ride is None else "fox_fwd_gather",
        grid=(n_pairs,),
        in_specs=[_pair_blk(S), _pair_blk(S), _pair_blk(S, OFF_FV // 128), _pair_blk(S), _pair_rows(S)]
        + [_ANY] * len(extra),
        out_specs=[_pair_blk(S), _pair_rows(S)] + [_ANY] * len(extra),
        out_shape=[jax.ShapeDtypeStruct((S, FOX_W), F32), jax.ShapeDtypeStruct((n_pairs, 8, S), F32)]
        + (_exchange_out_shapes("gather", *extra) if extra else []),
        scratch_shapes=[pltpu.VMEM((S, 128), BF16)] * 2 + [pltpu.VMEM((128, S), BF16)] * 2
        + [pltpu.VMEM((S, 128), F32)] * 2 + [pltpu.VMEM((8, T), F32)] * 2 + [pltpu.VMEM((128, T), F32)] * 2
        + (_EXCHANGE_SEMS if extra else []),
        compiler_params=_cp(("arbitrary",), _VMEM_BIG),
    )(qs, kn, proj, cqb, crow4, *extra)


def _softplus_parts(z):
    e = jnp.exp(-jnp.abs(z))
    return e, jnp.maximum(z, 0.0) + jnp.log1p(e)


def _sb_fwd(proj, triu):
    S = proj.shape[0]
    T = triu.shape[0]
    nq = S // T

    def body(q_ref, k_ref, v_ref, tri_ref, o_ref, lt_ref, qa, qb, kb, vt, ra, rb, acca, accb):
        lane_s = _head_masks(S)
        q = (q_ref[...] * Q_SCALE).astype(BF16)
        zq = jnp.zeros_like(q)
        qa[...] = jnp.where(lane_s, q, zq)
        qb[...] = jnp.where(lane_s, zq, q)
        kb[...] = k_ref[...].astype(BF16)
        lt_ref[...] = jnp.zeros((8, S), F32)
        row_t = lax.broadcasted_iota(jnp.int32, (128, T), 0) < HEAD_DIM
        zba, zbb = _score_bounds(q, kb[...])

        def prep(c, carry):
            c0 = pl.multiple_of(c * T, T)
            vt[:, pl.ds(c0, T)] = v_ref[pl.ds(c0, T), :].T.astype(BF16)
            return carry

        lax.fori_loop(0, nq, prep, 0)
        strict = (lax.broadcasted_iota(jnp.int32, (T, T), 0) < lax.broadcasted_iota(jnp.int32, (T, T), 1))

        heads = ((qa, ra, acca), (qb, rb, accb))

        def kv(j, r0, masked):
            c0 = pl.multiple_of(j * T, T)
            k = kb[pl.ds(c0, T), :]
            vtt = vt[:, pl.ds(c0, T)]
            tri = tri_ref[...]
            zs = [_dot_nt(k, qr[pl.ds(r0, T), :]) for qr, _, _ in heads]
            lbs = [-_softplus_parts(z)[1] for z in zs]
            if masked:
                lbs = [jnp.where(strict, lb, 0.0) for lb in lbs]
            incs = [_mm2(lb, tri, left=True) for lb in lbs]
            rs = [r_ref[0:1, :] for _, r_ref, _ in heads]
            aas = [jnp.exp(z + inc + r) for z, inc, r in zip(zs, incs, rs)]
            if masked:
                aas = [jnp.where(strict, a, 0.0) for a in aas]
            avs = [_dot(vtt, a.astype(BF16)) for a in aas]
            for (_, r_ref, acc_ref), r, inc, av in zip(heads, rs, incs, avs):
                r_ref[0:1, :] = r + inc[0:1, :]
                acc_ref[...] = acc_ref[...] + av

        def qblk(i, carry):
            r0 = pl.multiple_of(i * T, T)
            ra[...] = jnp.zeros((8, T), F32)
            rb[...] = jnp.zeros((8, T), F32)
            acca[...] = jnp.zeros((128, T), F32)
            accb[...] = jnp.zeros((128, T), F32)
            kv(i, r0, True)

            def alive():
                return jnp.maximum(jnp.max(ra[0:1, :]) + zba, jnp.max(rb[0:1, :]) + zbb) > -_EXP_ZERO

            def cond(st):
                return (st[0] < i) & st[1]

            def step(st):
                kv(i - 1 - st[0], r0, False)
                return st[0] + 1, alive()

            done, _ = lax.while_loop(cond, step, (jnp.int32(0), alive()))
            o_ref[pl.ds(r0, T), :] = jnp.where(row_t, acca[...], accb[...]).T
            lt_ref[0:1, pl.ds(r0, T)] = ra[0:1, :]
            lt_ref[1:2, pl.ds(r0, T)] = rb[0:1, :]
            lt_ref[2:3, pl.ds(r0, T)] = jnp.broadcast_to(done.astype(F32), (1, T))
            return carry

        lax.fori_loop(0, nq, qblk, 0)

    return pl.pallas_call(
        body, name="sb_fwd",
        grid=(SB_W // 128,),
        in_specs=[_pair_blk(S, OFF_SQ // 128), _pair_blk(S, OFF_SK // 128), _pair_blk(S, OFF_SV // 128),
                  pl.BlockSpec((T, T), lambda p: (0, 0))],
        out_specs=[_pair_blk(S), _pair_rows(S)],
        out_shape=[jax.ShapeDtypeStruct((S, SB_W), F32), jax.ShapeDtypeStruct((SB_W // 128, 8, S), F32)],
        scratch_shapes=[pltpu.VMEM((S, 128), BF16)] * 3 + [pltpu.VMEM((128, S), BF16)]
        + [pltpu.VMEM((8, T), F32)] * 2 + [pltpu.VMEM((128, T), F32)] * 2,
        compiler_params=_cp(("arbitrary",), _VMEM_BIG),
    )(proj, proj, proj, triu)


def _pool_window_lanes(shape):
    lane = lax.broadcasted_iota(jnp.int32, shape, 1)
    return jnp.where(lane < 64, 2, jnp.where(lane < 128, 4, jnp.where(lane < 192, 8, 16)))


def _pool_fwd(proj):
    S = proj.shape[0]

    def body(x_ref, o_ref):
        x = x_ref[...]
        t = lax.broadcasted_iota(jnp.int32, x.shape, 0)
        lane = lax.broadcasted_iota(jnp.int32, x.shape, 1)

        def back(a, k):
            return jnp.where(t >= k, pltpu.roll(a, k, 0), 0.0)

        s1 = x + back(x, 1)
        s2 = s1 + back(s1, 2)
        s4 = s2 + back(s2, 4)
        s8 = s4 + back(s4, 8)
        win = jnp.where(lane < 64, s1, jnp.where(lane < 128, s2, jnp.where(lane < 192, s4, s8)))
        cnt = jnp.minimum(t + 1, _pool_window_lanes(x.shape)).astype(F32)
        o_ref[...] = win / cnt - x

    return pl.pallas_call(
        body, name="pool_fwd",
        grid=(1,),
        in_specs=[pl.BlockSpec((S, POOL_W), lambda i: (0, OFF_PX // POOL_W))],
        out_specs=pl.BlockSpec((S, POOL_W), lambda i: (0, 0)),
        out_shape=jax.ShapeDtypeStruct((S, POOL_W), F32),
        compiler_params=_cp(("arbitrary",), _VMEM_BIG),
    )(proj)


def _silu(g):
    return g * _sigmoid(g)


def _mix_out(fo, so, pooled, proj, wbd, scale, wout, x):
    S, D = x.shape
    tm = min(256, S)

    def body(fo_ref, fg_ref, so_ref, sg_ref, pl_ref, pg_ref, wbd_ref, sc_ref, w_ref, x_ref, y_ref, mxt_ref, mx_ref):
        parts = ((0, fo_ref[...] * _silu(fg_ref[...])),
                 (FOX_W, (_dot(pl_ref[...].astype(BF16), wbd_ref[...]) * sc_ref[...]) * _silu(pg_ref[...])),
                 (FOX_W + POOL_W, so_ref[...] * _silu(sg_ref[...])))
        for off, part in parts:
            w = part.shape[1]
            mx_ref[:, off:off + w] = part.astype(BF16)
            mxt_ref[off:off + w, :] = part.T.astype(BF16)
        y_ref[...] = x_ref[...] + _dot(mx_ref[...], w_ref[...])

    return pl.pallas_call(
        body, name="mix_out",
        grid=(S // tm,),
        in_specs=[pl.BlockSpec((tm, FOX_W), lambda i: (i, 0)),
                  pl.BlockSpec((tm, FOX_W), lambda i: (i, OFF_FG // FOX_W)),
                  pl.BlockSpec((tm, SB_W), lambda i: (i, 0)),
                  pl.BlockSpec((tm, SB_W), lambda i: (i, OFF_SG // SB_W)),
                  pl.BlockSpec((tm, POOL_W), lambda i: (i, 0)),
                  pl.BlockSpec((tm, POOL_W), lambda i: (i, OFF_PG // POOL_W)),
                  pl.BlockSpec((POOL_W, POOL_W), lambda i: (0, 0)),
                  pl.BlockSpec((1, POOL_W), lambda i: (0, 0)),
                  pl.BlockSpec((D_MIX, D), lambda i: (0, 0)),
                  pl.BlockSpec((tm, D), lambda i: (i, 0))],
        out_specs=[pl.BlockSpec((tm, D), lambda i: (i, 0)), pl.BlockSpec((D_MIX, tm), lambda i: (0, i))],
        out_shape=[jax.ShapeDtypeStruct((S, D), F32), jax.ShapeDtypeStruct((D_MIX, S), BF16)],
        scratch_shapes=[pltpu.VMEM((tm, D_MIX), BF16)],
        compiler_params=_cp(("parallel",), 40 << 20),
    )(fo, proj, so, proj, pooled, proj, wbd, scale, wout, x)


def _loss_head(y, target):
    S, D = y.shape
    tm = min(_TM, S)

    def body(y_ref, t_ref, dy_ref, ls_ref):
        @pl.when(pl.program_id(0) == 0)
        def _():
            ls_ref[...] = jnp.zeros_like(ls_ref)

        e = y_ref[...] - t_ref[...]
        dy_ref[...] = e * (1.0 / D)
        ls_ref[...] = ls_ref[...] + jnp.sum(e * e) * (0.5 / D)

    dy, ls = pl.pallas_call(
        body, name="loss_head",
        grid=(S // tm,),
        in_specs=[pl.BlockSpec((tm, D), lambda i: (i, 0)), pl.BlockSpec((tm, D), lambda i: (i, 0))],
        out_specs=[pl.BlockSpec((tm, D), lambda i: (i, 0)), pl.BlockSpec((8, 128), lambda i: (0, 0))],
        out_shape=[jax.ShapeDtypeStruct((S, D), F32), jax.ShapeDtypeStruct((8, 128), F32)],
        compiler_params=_cp(("arbitrary",), 40 << 20),
    )(y, target)
    return dy, ls[0, 0]


def _dsilu(g):
    s = _sigmoid(g)
    return s * (1.0 + g * (1.0 - s))


def _gate_bwd(dy, wout, fo, so, pooled, proj, wbd, scale):
    S, D = dy.shape
    tm = min(256, S)

    def body(dy_ref, w_ref, fo_ref, fg_ref, so_ref, sg_ref, pl_ref, pg_ref, wbd_ref, sc_ref,
             dfo_ref, dfg_ref, dso_ref, dsg_ref, dpg_ref, dpl_ref, dsc_ref, dwbd_ref):
        @pl.when(pl.program_id(0) == 0)
        def _():
            dsc_ref[...] = jnp.zeros_like(dsc_ref)
            dwbd_ref[...] = jnp.zeros_like(dwbd_ref)

        dm = _dot_nt(dy_ref[...].astype(BF16), w_ref[...])
        dmf = dm[:, 0:FOX_W]
        dmp = dm[:, FOX_W:FOX_W + POOL_W]
        dms = dm[:, FOX_W + POOL_W:D_MIX]
        fg = fg_ref[...]
        dfo_ref[...] = dmf * _silu(fg)
        dfg_ref[...] = (dmf * fo_ref[...] * _dsilu(fg)).astype(BF16)
        sg = sg_ref[...]
        dso_ref[...] = dms * _silu(sg)
        dsg_ref[...] = (dms * so_ref[...] * _dsilu(sg)).astype(BF16)
        pg = pg_ref[...]
        plb = pl_ref[...].astype(BF16)
        yw = _dot(plb, wbd_ref[...])
        sc = sc_ref[...]
        dpg_ref[...] = (dmp * (yw * sc) * _dsilu(pg)).astype(BF16)
        dys = dmp * _silu(pg)
        dsc_ref[...] = dsc_ref[...] + jnp.sum(dys * yw, axis=0, keepdims=True)
        dyw = (dys * sc).astype(BF16)
        dpl_ref[...] = _dot_nt(dyw, wbd_ref[...])
        dwbd_ref[...] = dwbd_ref[...] + _dot_tn(plb, dyw)

    return pl.pallas_call(
        body, name="gate_bwd",
        grid=(S // tm,),
        in_specs=[pl.BlockSpec((tm, D), lambda i: (i, 0)),
                  pl.BlockSpec((D_MIX, D), lambda i: (0, 0)),
                  pl.BlockSpec((tm, FOX_W), lambda i: (i, 0)),
                  pl.BlockSpec((tm, FOX_W), lambda i: (i, OFF_FG // FOX_W)),
                  pl.BlockSpec((tm, SB_W), lambda i: (i, 0)),
                  pl.BlockSpec((tm, SB_W), lambda i: (i, OFF_SG // SB_W)),
                  pl.BlockSpec((tm, POOL_W), lambda i: (i, 0)),
                  pl.BlockSpec((tm, POOL_W), lambda i: (i, OFF_PG // POOL_W)),
                  pl.BlockSpec((POOL_W, POOL_W), lambda i: (0, 0)),
                  pl.BlockSpec((1, POOL_W), lambda i: (0, 0))],
        out_specs=[pl.BlockSpec((tm, FOX_W), lambda i: (i, 0)),
                   pl.BlockSpec((tm, FOX_W), lambda i: (i, 0)),
                   pl.BlockSpec((tm, SB_W), lambda i: (i, 0)),
                   pl.BlockSpec((tm, SB_W), lambda i: (i, 0)),
                   pl.BlockSpec((tm, POOL_W), lambda i: (i, 0)),
                   pl.BlockSpec((tm, POOL_W), lambda i: (i, 0)),
                   pl.BlockSpec((1, POOL_W), lambda i: (0, 0)),
                   pl.BlockSpec((POOL_W, POOL_W), lambda i: (0, 0))],
        out_shape=[jax.ShapeDtypeStruct((S, FOX_W), F32), jax.ShapeDtypeStruct((S, FOX_W), BF16),
                   jax.ShapeDtypeStruct((S, SB_W), F32), jax.ShapeDtypeStruct((S, SB_W), BF16),
                   jax.ShapeDtypeStruct((S, POOL_W), BF16), jax.ShapeDtypeStruct((S, POOL_W), F32),
                   jax.ShapeDtypeStruct((1, POOL_W), F32), jax.ShapeDtypeStruct((POOL_W, POOL_W), F32)],
        compiler_params=_cp(("arbitrary",), 40 << 20),
    )(dy, wout, fo, proj, so, proj, pooled, proj, wbd, scale)


def _matmul_acc(at, b, name):
    M, S = at.shape
    N = b.shape[1]
    tk = min(_TK_DW, S)
    tn = min(512, N)
    nk = S // tk

    def body(a_ref, b_ref, o_ref, acc):
        k = pl.program_id(1)

        @pl.when(k == 0)
        def _():
            acc[...] = jnp.zeros_like(acc)

        acc[...] = acc[...] + _dot(a_ref[...], b_ref[...].astype(BF16))

        @pl.when(k == nk - 1)
        def _():
            o_ref[...] = acc[...].astype(BF16)

    return pl.pallas_call(
        body, name=name,
        grid=(N // tn, nk),
        in_specs=[pl.BlockSpec((M, tk), lambda j, k: (0, k)), pl.BlockSpec((tk, tn), lambda j, k: (k, j))],
        out_specs=pl.BlockSpec((M, tn), lambda j, k: (0, j)),
        out_shape=jax.ShapeDtypeStruct((M, N), BF16),
        scratch_shapes=[pltpu.VMEM((M, tn), F32)],
        compiler_params=_cp(("parallel", "arbitrary"), 40 << 20),
    )(at, b)


def _pool_bwd(dpooled):
    S = dpooled.shape[0]

    def body(d_ref, o_ref):
        d = d_ref[...]
        t = lax.broadcasted_iota(jnp.int32, d.shape, 0)
        lane = lax.broadcasted_iota(jnp.int32, d.shape, 1)
        cnt = jnp.minimum(t + 1, _pool_window_lanes(d.shape)).astype(F32)
        u = d / cnt

        def fwd(a, k):
            return jnp.where(t < S - k, pltpu.roll(a, S - k, 0), 0.0)

        s1 = u + fwd(u, 1)
        s2 = s1 + fwd(s1, 2)
        s4 = s2 + fwd(s2, 4)
        s8 = s4 + fwd(s4, 8)
        win = jnp.where(lane < 64, s1, jnp.where(lane < 128, s2, jnp.where(lane < 192, s4, s8)))
        o_ref[...] = (win - d).astype(BF16)

    return pl.pallas_call(
        body, name="pool_bwd",
        grid=(1,),
        in_specs=[pl.BlockSpec((S, POOL_W), lambda i: (0, 0))],
        out_specs=pl.BlockSpec((S, POOL_W), lambda i: (0, 0)),
        out_shape=jax.ShapeDtypeStruct((S, POOL_W), BF16),
        compiler_params=_cp(("arbitrary",), _VMEM_BIG),
    )(dpooled)


def _fox_bwd(qs, kn, proj, dfo, fo, lse, cqb, crow4, ride=None):
    S = qs.shape[0]
    T = min(_T, S)
    nq = S // T
    n_pairs = FOX_W // 128

    def body(*refs):
        if ride is None:
            q_ref, k_ref, v_ref, do_ref, o_ref, lse_ref, cq_ref, cr_ref = refs[:8]
            dq_ref, dk_ref, dv_ref, dck_ref, dcq_ref = refs[8:13]
            scr = refs[13:]
        else:
            q_ref, k_ref, v_ref, do_ref, o_ref, lse_ref, cq_ref, cr_ref, pa_ref, pb_ref = refs[:10]
            dq_ref, dk_ref, dv_ref, dck_ref, dcq_ref, ra_ref, rb_ref = refs[10:17]
            scr = refs[17:32]
            xrefs = (pa_ref, pb_ref, ra_ref, rb_ref) + tuple(refs[32:])

            @pl.when(pl.program_id(0) == 0)
            def _():
                _start_exchange("scatter", *xrefs)

        qa, qb, kta, ktb, vb, doa, dob, cka, ckb, dcka, dckb, dva, dqt, dcqa, dcqb = scr
        lane_s = _head_masks(S)
        q = q_ref[...]
        zq = jnp.zeros_like(q)
        qa[...] = jnp.where(lane_s, q, zq)
        qb[...] = jnp.where(lane_s, zq, q)
        vb[...] = v_ref[...].astype(BF16)
        do = do_ref[...].astype(BF16)
        doa[...] = jnp.where(lane_s, do, zq)
        dob[...] = jnp.where(lane_s, zq, do)
        cq = cq_ref[...]
        cka[...], ckb[...] = _spread_heads(cq)
        zs = jnp.zeros((S, 128), F32)
        dk_ref[...] = zs
        dva[...] = zs
        dcka[...] = zs
        dckb[...] = zs
        dcq_ref[...] = jnp.zeros((8, S), F32)
        row_t = lax.broadcasted_iota(jnp.int32, (128, T), 0) < HEAD_DIM

        def prep(c, carry):
            c0 = pl.multiple_of(c * T, T)
            kt = k_ref[pl.ds(c0, T), :].astype(F32).T
            kta[:, pl.ds(c0, T)] = jnp.where(row_t, kt, 0.0).astype(BF16)
            ktb[:, pl.ds(c0, T)] = jnp.where(row_t, 0.0, kt).astype(BF16)
            return carry

        lax.fori_loop(0, nq, prep, 0)
        causal = (lax.broadcasted_iota(jnp.int32, (T, T), 0) <= lax.broadcasted_iota(jnp.int32, (T, T), 1))

        heads = ((qa, kta, doa, cka, dcka, dcqa), (qb, ktb, dob, ckb, dckb, dcqb))

        def kv(js, r0, lss, dls, masked):
            cr = cr_ref[:, pl.ds(r0, T)]
            c0s = [pl.multiple_of(j * T, T) for j in js]
            ks = [k_ref[pl.ds(c0, T), :] for c0 in c0s]
            vs = [vb[pl.ds(c0, T), :] for c0 in c0s]
            qhs = [hd[0][pl.ds(r0, T), :] for hd in heads]
            dohs = [hd[2][pl.ds(r0, T), :] for hd in heads]
            ss = []
            for h, hd in enumerate(heads):
                row = []
                for k, c0 in zip(ks, c0s):
                    s = _dot_nt(k, qhs[h]) + cr[h:h + 1, :] - jnp.tile(hd[3][pl.ds(c0, T), :], (1, T // 128))
                    row.append(jnp.where(causal, s, NEG) if masked else s)
                ss.append(row)
            ps = [[jnp.exp(s - lss[h]) for s in row] for h, row in enumerate(ss)]
            dps = [[_dot_nt(v, dohs[h]) for v in vs] for h in range(2)]
            dss = [[p * (dp - dls[h]) for p, dp in zip(ps[h], dps[h])] for h in range(2)]
            pbs = [[p.astype(BF16) for p in row] for row in ps]
            dsbs = [[ds.astype(BF16) for ds in row] for row in dss]
            for t, c0 in enumerate(c0s):
                dva[pl.ds(c0, T), :] = dva[pl.ds(c0, T), :] + (_dot(pbs[0][t], dohs[0]) + _dot(pbs[1][t], dohs[1]))
                dk_ref[pl.ds(c0, T), :] = dk_ref[pl.ds(c0, T), :] + (_dot(dsbs[0][t], qhs[0]) + _dot(dsbs[1][t], qhs[1]))
            dq = None
            for h, hd in enumerate(heads):
                for t, c0 in enumerate(c0s):
                    term = _dot(hd[1][:, pl.ds(c0, T)], dsbs[h][t])
                    dq = term if dq is None else dq + term
            dqt[...] = dqt[...] + dq
            for h, hd in enumerate(heads):
                col = jnp.sum(dss[h][0], axis=0, keepdims=True)
                for ds in dss[h][1:]:
                    col = col + jnp.sum(ds, axis=0, keepdims=True)
                hd[5][0:1, :] = hd[5][0:1, :] + col
                for ds, c0 in zip(dss[h], c0s):
                    fold = ds[:, 0:128]
                    for u in range(1, T // 128):
                        fold = fold + ds[:, 128 * u:128 * (u + 1)]
                    hd[4][pl.ds(c0, T), :] = hd[4][pl.ds(c0, T), :] - fold

        def qblk(i, carry):
            r0 = pl.multiple_of(i * T, T)
            dt = (do_ref[pl.ds(r0, T), :] * o_ref[pl.ds(r0, T), :]).T
            dla = jnp.sum(jnp.where(row_t, dt, 0.0), axis=0, keepdims=True)
            dlb = jnp.sum(jnp.where(row_t, 0.0, dt), axis=0, keepdims=True)
            ls = lse_ref[:, pl.ds(r0, T)]
            lss = (ls[0:1, :], ls[1:2, :])
            back = jnp.max(ls[2:3, :]).astype(jnp.int32)
            dqt[...] = jnp.zeros((128, T), F32)
            dcqa[...] = jnp.zeros((8, T), F32)
            dcqb[...] = jnp.zeros((8, T), F32)
            kv([i], r0, lss, (dla, dlb), True)
            _for_tiles_back(i, back, lambda js: kv(js, r0, lss, (dla, dlb), False))
            dq_ref[pl.ds(r0, T), :] = dqt[...].T
            dcq_ref[0:1, pl.ds(r0, T)] = dcqa[0:1, :]
            dcq_ref[1:2, pl.ds(r0, T)] = dcqb[0:1, :]
            return carry

        lax.fori_loop(0, nq, qblk, 0)
        dv_ref[...] = dva[...].astype(BF16)
        dck_ref[...] = jnp.where(lane_s, jnp.sum(dcka[...], axis=1, keepdims=True),
                                 jnp.sum(dckb[...], axis=1, keepdims=True))
        if ride is not None:
            @pl.when(pl.program_id(0) == n_pairs - 1)
            def _():
                _wait_exchange("scatter", *xrefs)

    extra = () if ride is None else tuple(ride)
    return pl.pallas_call(
        body, name="fox_bwd" if ride is None else "fox_bwd_exchange",
        grid=(n_pairs,),
        in_specs=[_pair_blk(S), _pair_blk(S), _pair_blk(S, OFF_FV // 128), _pair_blk(S), _pair_blk(S),
                  _pair_rows(S), _pair_blk(S), _pair_rows(S)] + [_ANY] * len(extra),
        out_specs=[_pair_blk(S), _pair_blk(S), _pair_blk(S), _pair_blk(S), _pair_rows(S)] + [_ANY] * len(extra),
        out_shape=[jax.ShapeDtypeStruct((S, FOX_W), F32), jax.ShapeDtypeStruct((S, FOX_W), F32),
                   jax.ShapeDtypeStruct((S, FOX_W), BF16), jax.ShapeDtypeStruct((S, FOX_W), F32),
                   jax.ShapeDtypeStruct((n_pairs, 8, S), F32)]
        + (_exchange_out_shapes("scatter", *extra) if extra else []),
        scratch_shapes=[pltpu.VMEM((S, 128), BF16)] * 2 + [pltpu.VMEM((128, S), BF16)] * 2
        + [pltpu.VMEM((S, 128), BF16)] * 3 + [pltpu.VMEM((S, 128), F32)] * 5
        + [pltpu.VMEM((128, T), F32)] + [pltpu.VMEM((8, T), F32)] * 2
        + (_EXCHANGE_SEMS if extra else []),
        compiler_params=_cp(("arbitrary",), _VMEM_BIG),
    )(qs, kn, proj, dfo, fo, lse, cqb, crow4, *extra)


def _sb_bwd(proj, dso, ltot, tril):
    S = proj.shape[0]
    T = tril.shape[0]
    nq = S // T

    def body(q_ref, k_ref, v_ref, do_ref, lt_ref, tri_ref, dq_ref, dk_ref, dv_ref,
             qa, qb, k2, kta, ktb, vb, doa, dob, dka, dva, dqt, ra, rb, ga, gb):
        lane_s = _head_masks(S)
        q = (q_ref[...] * Q_SCALE).astype(BF16)
        zq = jnp.zeros_like(q)
        qa[...] = jnp.where(lane_s, q, zq)
        qb[...] = jnp.where(lane_s, zq, q)
        k2[...] = k_ref[...].astype(BF16)
        vb[...] = v_ref[...].astype(BF16)
        do = do_ref[...].astype(BF16)
        doa[...] = jnp.where(lane_s, do, zq)
        dob[...] = jnp.where(lane_s, zq, do)
        dka[...] = jnp.zeros((S, 128), F32)
        dva[...] = jnp.zeros((S, 128), F32)
        row_t = lax.broadcasted_iota(jnp.int32, (128, T), 0) < HEAD_DIM

        def prep(c, carry):
            c0 = pl.multiple_of(c * T, T)
            kt = k_ref[pl.ds(c0, T), :].T
            kta[:, pl.ds(c0, T)] = jnp.where(row_t, kt, 0.0).astype(BF16)
            ktb[:, pl.ds(c0, T)] = jnp.where(row_t, 0.0, kt).astype(BF16)
            return carry

        lax.fori_loop(0, nq, prep, 0)
        strict = (lax.broadcasted_iota(jnp.int32, (T, T), 0) < lax.broadcasted_iota(jnp.int32, (T, T), 1))

        heads = ((qa, kta, doa, ra, ga), (qb, ktb, dob, rb, gb))

        def kv(j, r0, lta, ltb, masked):
            c0 = pl.multiple_of(j * T, T)
            kfull = k2[pl.ds(c0, T), :]
            v = vb[pl.ds(c0, T), :]
            tri = tri_ref[...]
            lts = (lta, ltb)
            qhs = [hd[0][pl.ds(r0, T), :] for hd in heads]
            dohs = [hd[2][pl.ds(r0, T), :] for hd in heads]
            zs = [_dot_nt(kfull, qh) for qh in qhs]
            das = [_dot_nt(v, doh) for doh in dohs]
            es, lbs = [], []
            for z in zs:
                e, sp = _softplus_parts(z)
                es.append(e)
                lbs.append(jnp.where(strict, -sp, 0.0) if masked else -sp)
            pres = [_mm2(lb, tri, left=True) for lb in lbs]
            rs = [hd[3][0:1, :] for hd in heads]
            aas = [jnp.exp(z + lb + ((lt - r) - pre)) for z, lb, lt, r, pre in zip(zs, lbs, lts, rs, pres)]
            if masked:
                aas = [jnp.where(strict, a, 0.0) for a in aas]
            gs = [a * da for a, da in zip(aas, das)]
            gpres = [_mm2(g, tri, left=True) for g in gs]
            gcs = [hd[4][0:1, :] for hd in heads]
            dzbs = []
            for z, e, g, gpre, gc in zip(zs, es, gs, gpres, gcs):
                inv = 1.0 / (1.0 + e)
                pos = z >= 0.0
                sig = jnp.where(pos, 1.0, e) * inv
                oms = jnp.where(pos, e, 1.0) * inv
                dz = g * oms - sig * (gc + (gpre - g))
                if masked:
                    dz = jnp.where(strict, dz, 0.0)
                dzbs.append(dz.astype(BF16))
            dqt[...] = dqt[...] + (_dot(heads[0][1][:, pl.ds(c0, T)], dzbs[0]) + _dot(heads[1][1][:, pl.ds(c0, T)], dzbs[1]))
            dka[pl.ds(c0, T), :] = dka[pl.ds(c0, T), :] + (_dot(dzbs[0], qhs[0]) + _dot(dzbs[1], qhs[1]))
            dva[pl.ds(c0, T), :] = dva[pl.ds(c0, T), :] + (_dot(aas[0].astype(BF16), dohs[0]) + _dot(aas[1].astype(BF16), dohs[1]))
            for hd, r, pre, gc, gpre in zip(heads, rs, pres, gcs, gpres):
                hd[3][0:1, :] = r + pre[T - 1:T, :]
                hd[4][0:1, :] = gc + gpre[T - 1:T, :]

        def qblk(i, carry):
            r0 = pl.multiple_of(i * T, T)
            lt = lt_ref[:, pl.ds(r0, T)]
            lta = lt[0:1, :]
            ltb = lt[1:2, :]
            back = jnp.max(lt[2:3, :]).astype(jnp.int32)
            zt = jnp.zeros((8, T), F32)
            dqt[...] = jnp.zeros((128, T), F32)
            ra[...] = zt
            rb[...] = zt
            ga[...] = zt
            gb[...] = zt

            def inner(j, c):
                kv(j, r0, lta, ltb, False)
                return c

            lax.fori_loop(i - back, i, inner, 0)
            kv(i, r0, lta, ltb, True)
            dq_ref[pl.ds(r0, T), :] = (dqt[...] * Q_SCALE).T.astype(BF16)
            return carry

        lax.fori_loop(0, nq, qblk, 0)
        dk_ref[...] = dka[...].astype(BF16)
        dv_ref[...] = dva[...].astype(BF16)

    return pl.pallas_call(
        body, name="sb_bwd",
        grid=(SB_W // 128,),
        in_specs=[_pair_blk(S, OFF_SQ // 128), _pair_blk(S, OFF_SK // 128), _pair_blk(S, OFF_SV // 128),
                  _pair_blk(S), _pair_rows(S), pl.BlockSpec((T, T), lambda p: (0, 0))],
        out_specs=[_pair_blk(S), _pair_blk(S), _pair_blk(S)],
        out_shape=[jax.ShapeDtypeStruct((S, SB_W), BF16)] * 3,
        scratch_shapes=([pltpu.VMEM((S, 128), BF16)] * 3 + [pltpu.VMEM((128, S), BF16)] * 2
                        + [pltpu.VMEM((S, 128), BF16)] * 3 + [pltpu.VMEM((S, 128), F32)] * 2
                        + [pltpu.VMEM((128, T), F32)] + [pltpu.VMEM((8, T), F32)] * 4),
        compiler_params=_cp(("arbitrary",), _VMEM_BIG),
    )(proj, proj, proj, dso, ltot, tril)


def _head_norm_bwd(x, g, dy, bd):
    ss = _mm2(x * x, bd)
    r = lax.rsqrt(ss * (1.0 / HEAD_DIM) + EPS)
    xr = x * r
    gdy = g * dy
    m = _mm2(xr * gdy, bd) * (1.0 / HEAD_DIM)
    return r * (gdy - xr * m), dy * xr


def _qk_bwd(dqs, dkn, proj, pff, bfp, gq, gk, bd, dccol, triu):
    S = proj.shape[0]
    T = triu.shape[0]
    n = S // T
    rev = lambda col: (lambda i: (n - 1 - i, col))

    def body(dq_ref, dk_ref, q_ref, k_ref, ff_ref, b_ref, gq_ref, gk_ref, bd_ref, dc_ref, tri_ref,
             dfq_ref, dfk_ref, dff_ref, dgq_ref, dgk_ref, dbf_ref, carry):
        @pl.when(pl.program_id(0) == 0)
        def _():
            carry[...] = jnp.zeros_like(carry)
            dgq_ref[...] = jnp.zeros_like(dgq_ref)
            dgk_ref[...] = jnp.zeros_like(dgk_ref)
            dbf_ref[...] = jnp.zeros_like(dbf_ref)

        bdv = bd_ref[...]
        dxq, gq_rows = _head_norm_bwd(q_ref[...], gq_ref[...], dq_ref[...] * Q_SCALE, bdv)
        dfq_ref[...] = dxq.astype(BF16)
        dgq_ref[...] = dgq_ref[...] + jnp.sum(gq_rows, axis=0, keepdims=True)
        dxk, gk_rows = _head_norm_bwd(k_ref[...], gk_ref[...], dk_ref[...], bdv)
        dfk_ref[...] = dxk.astype(BF16)
        dgk_ref[...] = dgk_ref[...] + jnp.sum(gk_rows, axis=0, keepdims=True)
        dlf = _mm3(dc_ref[...], tri_ref[...], left=True) + carry[0:1, :]
        carry[0:1, :] = dlf[0:1, :]
        u = ff_ref[...] + b_ref[...]
        lane = lax.broadcasted_iota(jnp.int32, u.shape, 1)
        dff = jnp.where(lane < N_FF, dlf * _sigmoid(-u), 0.0)
        dff_ref[...] = dff.astype(BF16)
        dbf_ref[...] = dbf_ref[...] + jnp.sum(dff, axis=0, keepdims=True)

    return pl.pallas_call(
        body, name="qk_bwd",
        grid=(n,),
        in_specs=[pl.BlockSpec((T, FOX_W), rev(0)), pl.BlockSpec((T, FOX_W), rev(0)),
                  pl.BlockSpec((T, FOX_W), rev(OFF_FQ // FOX_W)), pl.BlockSpec((T, FOX_W), rev(OFF_FK // FOX_W)),
                  pl.BlockSpec((T, N_FFPAD), rev(0)),
                  pl.BlockSpec((1, N_FFPAD), lambda i: (0, 0)),
                  pl.BlockSpec((1, FOX_W), lambda i: (0, 0)), pl.BlockSpec((1, FOX_W), lambda i: (0, 0)),
                  pl.BlockSpec((FOX_W, FOX_W), lambda i: (0, 0)),
                  pl.BlockSpec((T, N_FFPAD), rev(0)),
                  pl.BlockSpec((T, T), lambda i: (0, 0))],
        out_specs=[pl.BlockSpec((T, FOX_W), rev(0)), pl.BlockSpec((T, FOX_W), rev(0)),
                   pl.BlockSpec((T, N_FFPAD), rev(0)),
                   pl.BlockSpec((1, FOX_W), lambda i: (0, 0)), pl.BlockSpec((1, FOX_W), lambda i: (0, 0)),
                   pl.BlockSpec((1, N_FFPAD), lambda i: (0, 0))],
        out_shape=[jax.ShapeDtypeStruct((S, FOX_W), BF16), jax.ShapeDtypeStruct((S, FOX_W), BF16),
                   jax.ShapeDtypeStruct((S, N_FFPAD), BF16),
                   jax.ShapeDtypeStruct((1, FOX_W), F32), jax.ShapeDtypeStruct((1, FOX_W), F32),
                   jax.ShapeDtypeStruct((1, N_FFPAD), F32)],
        scratch_shapes=[pltpu.VMEM((8, N_FFPAD), F32)],
        compiler_params=_cp(("arbitrary",), 40 << 20),
    )(dqs, dkn, proj, proj, pff, bfp, gq, gk, bd, dccol, triu)


def _inproj_bwd_dx(dpm, dff, wm, wff, x, g, dy):
    S, D = x.shape
    tm = min(_TM_DX, S)

    def body(dp_ref, dff_ref, w_ref, wff_ref, x_ref, g_ref, dy_ref, dx_ref, dg_ref):
        @pl.when(pl.program_id(0) == 0)
        def _():
            dg_ref[...] = jnp.zeros_like(dg_ref)

        dh = _dot_nt(dp_ref[...], w_ref[...]) + _dot_nt(dff_ref[...], wff_ref[...])
        xv = x_ref[...]
        r = _rms_rows(xv)
        xr = xv * r
        dg_ref[...] = dg_ref[...] + jnp.sum(dh * xr, axis=0, keepdims=True)
        gdh = g_ref[...] * dh
        m = jnp.mean(gdh * xr, axis=-1, keepdims=True)
        dx_ref[...] = dy_ref[...] + r * (gdh - xr * m)

    return pl.pallas_call(
        body, name="inproj_bwd_dx",
        grid=(S // tm,),
        in_specs=[pl.BlockSpec((tm, N_MAIN), lambda i: (i, 0)),
                  pl.BlockSpec((tm, N_FFPAD), lambda i: (i, 0)),
                  pl.BlockSpec((D, N_MAIN), lambda i: (0, 0)),
                  pl.BlockSpec((D, N_FFPAD), lambda i: (0, 0)),
                  pl.BlockSpec((tm, D), lambda i: (i, 0)),
                  pl.BlockSpec((1, D), lambda i: (0, 0)),
                  pl.BlockSpec((tm, D), lambda i: (i, 0))],
        out_specs=[pl.BlockSpec((tm, D), lambda i: (i, 0)), pl.BlockSpec((1, D), lambda i: (0, 0))],
        out_shape=[jax.ShapeDtypeStruct((S, D), F32), jax.ShapeDtypeStruct((1, D), F32)],
        compiler_params=_cp(("arbitrary",), 48 << 20),
    )(dpm, dff, wm, wff, x, g, dy)


def _inproj_bwd_dw(ht, dpm, dff):
    D, S = ht.shape
    tk = min(_TK_DW, S)
    tn = 512
    nk = S // tk

    def body(ht_ref, dp_ref, dff_ref, dw_ref, dwff_ref, acc, accff):
        j, k = pl.program_id(0), pl.program_id(1)

        @pl.when(k == 0)
        def _():
            acc[...] = jnp.zeros_like(acc)

        @pl.when((k == 0) & (j == 0))
        def _():
            accff[...] = jnp.zeros_like(accff)

        acc[...] = acc[...] + _dot(ht_ref[...], dp_ref[...])

        @pl.when(j == 0)
        def _():
            accff[...] = accff[...] + _dot(ht_ref[...], dff_ref[...])

        @pl.when(k == nk - 1)
        def _():
            dw_ref[...] = acc[...].astype(BF16)

        @pl.when((k == nk - 1) & (j == 0))
        def _():
            dwff_ref[...] = accff[...].astype(BF16)

    return pl.pallas_call(
        body, name="inproj_bwd_dw",
        grid=(N_MAIN // tn, nk),
        in_specs=[pl.BlockSpec((D, tk), lambda j, k: (0, k)),
                  pl.BlockSpec((tk, tn), lambda j, k: (k, j)),
                  pl.BlockSpec((tk, N_FFPAD), lambda j, k: (k, 0))],
        out_specs=[pl.BlockSpec((D, tn), lambda j, k: (0, j)), pl.BlockSpec((D, N_FFPAD), lambda j, k: (0, 0))],
        out_shape=[jax.ShapeDtypeStruct((D, N_MAIN), BF16), jax.ShapeDtypeStruct((D, N_FFPAD), BF16)],
        scratch_shapes=[pltpu.VMEM((D, tn), F32), pltpu.VMEM((D, N_FFPAD), F32)],
        compiler_params=_cp(("arbitrary", "arbitrary"), 40 << 20),
    )(ht, dpm, dff)


def _constants(T):
    tril = jnp.tril(jnp.ones((T, T), F32)).astype(BF16)
    hid = jnp.arange(FOX_W) // HEAD_DIM
    bd = (hid[:, None] == hid[None, :]).astype(BF16)
    ex = (jnp.arange(N_FFPAD)[:, None] == hid[None, :]).astype(BF16)
    return tril, tril.T, bd, ex


def _crow4(ccol, T):
    S = ccol.shape[0]
    c = ccol[:, :FOX_HEADS].T
    last = jnp.pad(c[:, T - 1::T], ((0, 0), (0, S - S // T)))
    rows = jnp.concatenate([c.reshape(FOX_HEADS // 2, 2, S), last.reshape(FOX_HEADS // 2, 2, S)], axis=1)
    return jnp.pad(rows, ((0, 0), (0, 4), (0, 0)))


def _layer_fwd(x, lw, consts, ride=None):
    tril, triu, bd, ex = consts
    proj, pff, ht = _inproj_fwd(x, lw["g"], lw["wm"], lw["wff"])
    qs, kn, ccol, cqb = _fox_prep(proj, pff, lw["bfp"], lw["gq"], lw["gk"], bd, ex, tril)
    crow4 = _crow4(ccol, tril.shape[0])
    fo, lse, *gathered = _fox_fwd(qs, kn, proj, cqb, crow4, ride)
    so, ltot = _sb_fwd(proj, triu)
    pooled = _pool_fwd(proj)
    y, mixedt = _mix_out(fo, so, pooled, proj, lw["wbd"], lw["scale"], lw["wout"], x)
    return y, (x, proj, pff, ht, qs, kn, cqb, crow4, fo, lse, so, ltot, pooled, mixedt), gathered


def _layer_bwd(dy, saved, lw, consts, ride=None):
    tril, triu, bd, _ = consts
    x, proj, pff, ht, qs, kn, cqb, crow4, fo, lse, so, ltot, pooled, mixedt = saved
    S = x.shape[0]
    dfo, dfg, dso, dsg, dpg, dpooled, dscale, dwbd = _gate_bwd(dy, lw["wout"], fo, so, pooled, proj, lw["wbd"], lw["scale"])
    dwout = _matmul_acc(mixedt, dy, "dw_out")
    dpx = _pool_bwd(dpooled)
    dqs, dkn, dfv, dck, dcq4, *received = _fox_bwd(qs, kn, proj, dfo, fo, lse, cqb, crow4, ride)
    dsq, dsk, dsv = _sb_bwd(proj, dso, ltot, tril)
    dc8 = dck[:, ::HEAD_DIM] + dcq4[:, :2, :].reshape(FOX_HEADS, S).T
    dccol = jnp.pad(dc8, ((0, 0), (0, N_FFPAD - FOX_HEADS)))
    dfq, dfk, dff, dgq, dgk, dbf = _qk_bwd(dqs, dkn, proj, pff, lw["bfp"], lw["gq"], lw["gk"], bd, dccol, triu)
    dpm = jnp.concatenate([dfq, dfk, dfv, dfg, dpx, dpg, dsq, dsk, dsv, dsg], axis=1)
    dx, dng = _inproj_bwd_dx(dpm, dff, lw["wm"], lw["wff"], x, lw["g"], dy)
    dwm, dwff = _inproj_bwd_dw(ht, dpm, dff)
    dwin = jnp.concatenate([dwm[:, :OFF_PX], dwff[:, :N_FF], dwm[:, OFF_PX:]], axis=1)
    grads = {
        "norm_g": dng[0],
        "w_in": dwin,
        "b_f": dbf[0, :N_FF],
        "q_norm_g": dgq[0].reshape(FOX_HEADS, HEAD_DIM).sum(0),
        "k_norm_g": dgk[0].reshape(FOX_HEADS, HEAD_DIM).sum(0),
        "w_pool": jnp.stack([dwbd[64 * i:64 * i + 64, 64 * i:64 * i + 64] for i in range(4)]),
        "pool_scale": dscale[0],
        "w_out": dwout,
    }
    return dx, grads, received


def _layer_weights(l, norm_g, gin, b_f, q_norm_g, k_norm_g, w_pool, pool_scale, gout):
    D = gin.shape[1]
    w = gin.transpose(1, 0, 2).reshape(D, D_IN)
    wm = jnp.concatenate([w[:, :2048], w[:, 2048 + N_FF:]], axis=1)
    wff = jnp.pad(w[:, 2048:2048 + N_FF], ((0, 0), (0, N_FFPAD - N_FF)))
    grp = jnp.arange(POOL_W) // 64
    wbd = jnp.where(grp[:, None] == grp[None, :], jnp.tile(w_pool[l].transpose(1, 0, 2).reshape(64, POOL_W), (4, 1)), 0.0)
    return {
        "g": norm_g[l].reshape(1, D),
        "wm": wm, "wff": wff,
        "bfp": jnp.pad(b_f[l], (0, N_FFPAD - N_FF)).reshape(1, N_FFPAD),
        "gq": jnp.tile(q_norm_g[l], FOX_HEADS).reshape(1, FOX_W),
        "gk": jnp.tile(k_norm_g[l], FOX_HEADS).reshape(1, FOX_W),
        "wbd": wbd.astype(BF16),
        "scale": pool_scale[l].reshape(1, POOL_W),
        "wout": gout.reshape(D_MIX, D),
    }


def _grad_parts(g):
    dwin, dwout = g["w_in"].astype(BF16), g["w_out"].astype(BF16)
    D = dwin.shape[0]
    return (dwin.reshape(D, N_DEV, D_IN // N_DEV).transpose(1, 0, 2),
            dwout.reshape(N_DEV, D_MIX // N_DEV, dwout.shape[1]))


def _train_step(x, target, norm_g, win_sh, b_f, q_norm_g, k_norm_g, w_pool, pool_scale, wout_sh):
    L = norm_g.shape[0]
    consts = _constants(min(_T, x.shape[0]))
    gathered = _exchange_pair("gather", win_sh[0], wout_sh[0], "gather_weights")
    lws, saved = [], []
    h = x
    for l in range(L):
        lws.append(_layer_weights(l, norm_g, gathered[0], b_f, q_norm_g, k_norm_g, w_pool, pool_scale, gathered[1]))
        ride = (win_sh[l + 1], wout_sh[l + 1]) if l + 1 < L else None
        h, sv, gathered = _layer_fwd(h, lws[l], consts, ride)
        saved.append(sv)
    dy, loss = _loss_head(h, target)
    grads, received = [None] * L, [None] * L
    ride = None
    for l in reversed(range(L)):
        dy, grads[l], got = _layer_bwd(dy, saved[l], lws[l], consts, ride)
        if ride is not None:
            received[l + 1] = got
        ride = _grad_parts(grads[l])
    received[0] = _exchange_pair("scatter", ride[0], ride[1], "exchange_grads")
    return loss, dy, grads, received


def _mesh_pos():
    return lax.axis_index("x"), lax.axis_index("y"), lax.axis_index("c")


_FLIPS = [(0, 0, 1), (1, 0, 0), (0, 1, 0), (1, 1, 0), (1, 0, 1), (0, 1, 1), (1, 1, 1)]


def _peers():
    x, y, c = _mesh_pos()
    out = []
    for fx, fy, fc in _FLIPS:
        px = 1 - x if fx else x
        py = 1 - y if fy else y
        pc = 1 - c if fc else c
        out.append(((px, py, pc), 4 * px + 2 * py + pc))
    return out, 4 * x + 2 * y + c


_EXCHANGE_SEMS = [pltpu.SemaphoreType.DMA((14,)), pltpu.SemaphoreType.DMA((14,)), pltpu.SemaphoreType.DMA((2,))]
_ANY = pl.BlockSpec(memory_space=pl.ANY)


def _exchange_copies(kind, a_ref, b_ref, oa_ref, ob_ref, send_sems, recv_sems, loc_sems):
    peers, me = _peers()
    pairs = ((a_ref, oa_ref), (b_ref, ob_ref))
    local = [pltpu.make_async_copy(src if kind == "gather" else src.at[me], dst.at[me], loc_sems.at[t])
             for t, (src, dst) in enumerate(pairs)]
    remote = []
    for k, (dev, idx) in enumerate(peers):
        for t, (src, dst) in enumerate(pairs):
            remote.append(pltpu.make_async_remote_copy(
                src_ref=src if kind == "gather" else src.at[idx], dst_ref=dst.at[me],
                send_sem=send_sems.at[2 * k + t], recv_sem=recv_sems.at[2 * k + t],
                device_id=dev, device_id_type=pl.DeviceIdType.MESH))
    return local, remote


def _start_exchange(kind, *refs):
    local, remote = _exchange_copies(kind, *refs)
    for cp in local + remote:
        cp.start()


def _wait_exchange(kind, *refs):
    local, remote = _exchange_copies(kind, *refs)
    for cp in remote:
        cp.wait_recv()
    for cp in remote:
        cp.wait_send()
    for cp in local:
        cp.wait()


def _exchange_out_shapes(kind, a, b):
    if kind == "gather":
        return [jax.ShapeDtypeStruct((N_DEV,) + a.shape, a.dtype), jax.ShapeDtypeStruct((N_DEV,) + b.shape, b.dtype)]
    return [jax.ShapeDtypeStruct(a.shape, a.dtype), jax.ShapeDtypeStruct(b.shape, b.dtype)]


def _exchange_pair(kind, a, b, name):
    def body(*refs):
        _start_exchange(kind, *refs)
        _wait_exchange(kind, *refs)

    return pl.pallas_call(
        body, name=name,
        in_specs=[_ANY, _ANY], out_specs=[_ANY, _ANY],
        out_shape=_exchange_out_shapes(kind, a, b),
        scratch_shapes=_EXCHANGE_SEMS,
    )(a, b)


def _adam_math(w, g, m, v):
    m_new = ADAM_B1 * m + (1.0 - ADAM_B1) * g
    v_new = ADAM_B2 * v + (1.0 - ADAM_B2) * (g * g)
    m_hat = m_new / (1.0 - ADAM_B1 ** ADAM_STEP)
    v_hat = v_new / (1.0 - ADAM_B2 ** ADAM_STEP)
    delta = -ADAM_LR * (m_hat / (jnp.sqrt(v_hat) + ADAM_EPS) + ADAM_WD * w)
    return delta, m_new, v_new


def _sum_adamw(gparts, w, m, v, name):
    L, R, C = w.shape
    tr = min(128, R)

    def body(*refs):
        gp_refs = refs[:L]
        w_ref, m_ref, v_ref, g_ref, d_ref, nm_ref, nv_ref = refs[L:]
        for l in range(L):
            g = gp_refs[l][0].astype(F32)
            for s in range(1, N_DEV):
                g = g + gp_refs[l][s].astype(F32)
            d, mn, vn = _adam_math(w_ref[l], g, m_ref[l], v_ref[l])
            g_ref[l] = g
            d_ref[l] = d
            nm_ref[l] = mn
            nv_ref[l] = vn

    blk = pl.BlockSpec((L, tr, C), lambda r: (0, r, 0))
    return pl.pallas_call(
        body, name=name,
        grid=(R // tr,),
        in_specs=[pl.BlockSpec((N_DEV, tr, C), lambda r: (0, r, 0))] * L + [blk, blk, blk],
        out_specs=[blk, blk, blk, blk],
        out_shape=[jax.ShapeDtypeStruct((L, R, C), F32)] * 4,
        compiler_params=_cp(("parallel",), 48 << 20),
    )(*gparts, w, m, v)


def _small_update(gpack, wpack, mpack, vpack):
    R = gpack.shape[0]
    VM = pl.BlockSpec(memory_space=pltpu.VMEM)

    def body(g_ref, w_ref, m_ref, v_ref, gs_ref, d_ref, nm_ref, nv_ref, buf, send_sems, recv_sems):
        peers, me = _peers()
        buf[me] = g_ref[...]
        copies = []
        for k, (dev, _) in enumerate(peers):
            cp = pltpu.make_async_remote_copy(
                src_ref=g_ref, dst_ref=buf.at[me], send_sem=send_sems.at[k], recv_sem=recv_sems.at[k],
                device_id=dev, device_id_type=pl.DeviceIdType.MESH)
            cp.start()
            copies.append(cp)
        for cp in copies:
            cp.wait_recv()
        for cp in copies:
            cp.wait_send()
        g = buf[0]
        for s in range(1, N_DEV):
            g = g + buf[s]
        d, mn, vn = _adam_math(w_ref[...], g, m_ref[...], v_ref[...])
        gs_ref[...] = g
        d_ref[...] = d
        nm_ref[...] = mn
        nv_ref[...] = vn

    return pl.pallas_call(
        body, name="small_update",
        in_specs=[VM] * 4, out_specs=[VM] * 4,
        out_shape=[jax.ShapeDtypeStruct((R, 128), F32)] * 4,
        scratch_shapes=[pltpu.VMEM((N_DEV, R, 128), F32), pltpu.SemaphoreType.DMA((7,)), pltpu.SemaphoreType.DMA((7,))],
        compiler_params=_cp(None, 40 << 20),
    )(gpack, wpack, mpack, vpack)


_SMALL = ("norm_g", "b_f", "q_norm_g", "k_norm_g", "w_pool", "pool_scale")


def _pack(parts):
    flat = jnp.concatenate([p.reshape(-1) for p in parts])
    n = flat.shape[0]
    rows = -(-n // (8 * 128)) * 8
    return jnp.pad(flat, (0, rows * 128 - n)).reshape(rows, 128)


def _unpack(packed, like):
    flat = packed.reshape(-1)
    out, o = [], 0
    for p in like:
        out.append(flat[o:o + p.size].reshape(p.shape))
        o += p.size
    return out


def kernel(x, norm_g, w_in, b_f, q_norm_g, k_norm_g, w_pool, pool_scale, w_out, loss_target, m_norm_g, m_w_in, m_b_f, m_q_norm_g, m_k_norm_g, m_w_pool, m_pool_scale, m_w_out, v_norm_g, v_w_in, v_b_f, v_q_norm_g, v_k_norm_g, v_w_pool, v_pool_scale, v_w_out):
    L = w_in.shape[0]

    loss_local, dx, grads, received = _train_step(x[0], loss_target[0], norm_g, w_in.astype(BF16), b_f, q_norm_g,
                                                  k_norm_g, w_pool, pool_scale, w_out.astype(BF16))
    loss = lax.psum(loss_local, MESH_AXES)
    g = {k: jnp.stack([grads[l][k] for l in range(L)]) for k in _SMALL}

    g_win, d_win, nm_win, nv_win = _sum_adamw([r[0] for r in received], w_in, m_w_in, v_w_in, "adamw_w_in")
    g_wout, d_wout, nm_wout, nv_wout = _sum_adamw([r[1] for r in received], w_out, m_w_out, v_w_out, "adamw_w_out")

    ws = dict(norm_g=norm_g, b_f=b_f, q_norm_g=q_norm_g, k_norm_g=k_norm_g, w_pool=w_pool, pool_scale=pool_scale)
    ms = dict(norm_g=m_norm_g, b_f=m_b_f, q_norm_g=m_q_norm_g, k_norm_g=m_k_norm_g, w_pool=m_w_pool, pool_scale=m_pool_scale)
    vs = dict(norm_g=v_norm_g, b_f=v_b_f, q_norm_g=v_q_norm_g, k_norm_g=v_k_norm_g, w_pool=v_w_pool, pool_scale=v_pool_scale)
    like = [ws[k] for k in _SMALL]
    gs_p, d_p, nm_p, nv_p = _small_update(_pack([g[k] for k in _SMALL]), _pack(like),
                                          _pack([ms[k] for k in _SMALL]), _pack([vs[k] for k in _SMALL]))
    gs = dict(zip(_SMALL, _unpack(gs_p, like)))
    ds = dict(zip(_SMALL, _unpack(d_p, like)))
    nms = dict(zip(_SMALL, _unpack(nm_p, like)))
    nvs = dict(zip(_SMALL, _unpack(nv_p, like)))
    gs["w_in"], ds["w_in"], nms["w_in"], nvs["w_in"] = g_win, d_win, nm_win, nv_win
    gs["w_out"], ds["w_out"], nms["w_out"], nvs["w_out"] = g_wout, d_wout, nm_wout, nv_wout

    order = ("norm_g", "w_in", "b_f", "q_norm_g", "k_norm_g", "w_pool", "pool_scale", "w_out")
    return (loss, dx[None], *[gs[k] for k in order], *[ds[k] for k in order],
            *[nms[k] for k in order], *[nvs[k] for k in order])
```

```python
import functools

import jax
import jax.numpy as jnp
from jax import lax
from jax.experimental import pallas as pl
from jax.experimental.pallas import tpu as pltpu

F32 = jnp.float32
BF16 = jnp.bfloat16

EPS = 1e-6
NEG = -1e30
HEAD_DIM = 64
FOX_HEADS = 8
FOX_W = 512
POOL_W = 256
SB_W = 256
D_MIX = 1024
N_FF = 8
N_MAIN = 3584
N_FFPAD = 128
OFF_FQ, OFF_FK, OFF_FV, OFF_FG = 0, 512, 1024, 1536
OFF_PX, OFF_PG = 2048, 2304
OFF_SQ, OFF_SK, OFF_SV, OFF_SG = 2560, 2816, 3072, 3328
D_IN = 3592
Q_SCALE = HEAD_DIM ** -0.5

ADAM_LR = 0.001
ADAM_B1 = 0.9
ADAM_B2 = 0.999
ADAM_EPS = 1e-08
ADAM_WD = 0.01
ADAM_STEP = 10

N_DEV = 8
MESH_AXES = ("x", "y", "c")

_T = 256
_TM = 512
_TM_FWD, _TN_FWD = 2048, 512
_TM_DX = 512
_TK_DW = 1024
_VMEM_BIG = 56 << 20


def _cp(sem=None, vmem=None):
    kw = {}
    if sem is not None:
        kw["dimension_semantics"] = sem
    if vmem is not None:
        kw["vmem_limit_bytes"] = vmem
    return pltpu.CompilerParams(**kw)


def _dot(a, b):
    return jnp.dot(a, b, preferred_element_type=F32)


def _dot_nt(a, b):
    return lax.dot_general(a, b, (((1,), (1,)), ((), ())), preferred_element_type=F32)


def _dot_tn(a, b):
    return lax.dot_general(a, b, (((0,), (0,)), ((), ())), preferred_element_type=F32)


def _mm2(v, m, left=False):
    hi = v.astype(BF16)
    lo = (v - hi.astype(F32)).astype(BF16)
    if left:
        return _dot(m, hi) + _dot(m, lo)
    return _dot(hi, m) + _dot(lo, m)


def _mm3(v, m, left=False):
    a1 = v.astype(BF16)
    r1 = v - a1.astype(F32)
    a2 = r1.astype(BF16)
    a3 = (r1 - a2.astype(F32)).astype(BF16)
    if left:
        return _dot(m, a1) + _dot(m, a2) + _dot(m, a3)
    return _dot(a1, m) + _dot(a2, m) + _dot(a3, m)


def _sigmoid(z):
    return 1.0 / (1.0 + jnp.exp(-z))


def _rms_rows(x):
    return lax.rsqrt(jnp.mean(x * x, axis=-1, keepdims=True) + EPS)


def _inproj_fwd(x, g, wm, wff):
    S, D = x.shape
    tm = min(_TM_FWD, S)
    tn = _TN_FWD

    def body(x_ref, g_ref, w_ref, wff_ref, o_ref, off_ref, ht_ref, h_ref):
        @pl.when(pl.program_id(1) == 0)
        def _():
            xv = x_ref[...]
            h = (xv * _rms_rows(xv)) * g_ref[...]
            h_ref[...] = h.astype(BF16)
            ht_ref[...] = h.T.astype(BF16)
            off_ref[...] = _dot(h_ref[...], wff_ref[...])

        o_ref[...] = _dot(h_ref[...], w_ref[...])

    return pl.pallas_call(
        body, name="inproj_fwd",
        grid=(S // tm, N_MAIN // tn),
        in_specs=[pl.BlockSpec((tm, D), lambda i, j: (i, 0)),
                  pl.BlockSpec((1, D), lambda i, j: (0, 0)),
                  pl.BlockSpec((D, tn), lambda i, j: (0, j)),
                  pl.BlockSpec((D, N_FFPAD), lambda i, j: (0, 0))],
        out_specs=[pl.BlockSpec((tm, tn), lambda i, j: (i, j)),
                   pl.BlockSpec((tm, N_FFPAD), lambda i, j: (i, 0)),
                   pl.BlockSpec((D, tm), lambda i, j: (0, i))],
        out_shape=[jax.ShapeDtypeStruct((S, N_MAIN), F32), jax.ShapeDtypeStruct((S, N_FFPAD), F32),
                   jax.ShapeDtypeStruct((D, S), BF16)],
        scratch_shapes=[pltpu.VMEM((tm, D), BF16)],
        compiler_params=_cp(("parallel", "arbitrary"), 48 << 20),
    )(x, g, wm, wff)


def _head_norm(x, g, bd):
    ss = _mm2(x * x, bd)
    r = lax.rsqrt(ss * (1.0 / HEAD_DIM) + EPS)
    return (x * r) * g


def _fox_prep(proj, pff, bfp, gq, gk, bd, ex, tril):
    S = proj.shape[0]
    T = tril.shape[0]

    def body(q_ref, k_ref, ff_ref, b_ref, gq_ref, gk_ref, bd_ref, ex_ref, tri_ref,
             qs_ref, kn_ref, cc_ref, cqb_ref, carry):
        @pl.when(pl.program_id(0) == 0)
        def _():
            carry[...] = jnp.zeros_like(carry)

        bdv = bd_ref[...]
        qs_ref[...] = (_head_norm(q_ref[...], gq_ref[...], bdv) * Q_SCALE).astype(BF16)
        kn_ref[...] = _head_norm(k_ref[...], gk_ref[...], bdv).astype(BF16)
        u = ff_ref[...] + b_ref[...]
        lf = jnp.minimum(u, 0.0) - jnp.log1p(jnp.exp(-jnp.abs(u)))
        c = _mm3(lf, tri_ref[...], left=True) + carry[0:1, :]
        carry[0:1, :] = c[T - 1:T, :]
        cc_ref[...] = c
        cqb_ref[...] = _mm3(c, ex_ref[...])

    return pl.pallas_call(
        body, name="fox_prep",
        grid=(S // T,),
        in_specs=[pl.BlockSpec((T, FOX_W), lambda i: (i, OFF_FQ // FOX_W)),
                  pl.BlockSpec((T, FOX_W), lambda i: (i, OFF_FK // FOX_W)),
                  pl.BlockSpec((T, N_FFPAD), lambda i: (i, 0)),
                  pl.BlockSpec((1, N_FFPAD), lambda i: (0, 0)),
                  pl.BlockSpec((1, FOX_W), lambda i: (0, 0)),
                  pl.BlockSpec((1, FOX_W), lambda i: (0, 0)),
                  pl.BlockSpec((FOX_W, FOX_W), lambda i: (0, 0)),
                  pl.BlockSpec((N_FFPAD, FOX_W), lambda i: (0, 0)),
                  pl.BlockSpec((T, T), lambda i: (0, 0))],
        out_specs=[pl.BlockSpec((T, FOX_W), lambda i: (i, 0)),
                   pl.BlockSpec((T, FOX_W), lambda i: (i, 0)),
                   pl.BlockSpec((T, N_FFPAD), lambda i: (i, 0)),
                   pl.BlockSpec((T, FOX_W), lambda i: (i, 0))],
        out_shape=[jax.ShapeDtypeStruct((S, FOX_W), BF16), jax.ShapeDtypeStruct((S, FOX_W), BF16),
                   jax.ShapeDtypeStruct((S, N_FFPAD), F32), jax.ShapeDtypeStruct((S, FOX_W), F32)],
        scratch_shapes=[pltpu.VMEM((8, N_FFPAD), F32)],
        compiler_params=_cp(("arbitrary",), 40 << 20),
    )(proj, proj, pff, bfp, gq, gk, bd, ex, tril)


def _pair_blk(S, off=0):
    return pl.BlockSpec((S, 128), lambda p: (0, off + p), pipeline_mode=pl.Buffered(1))


def _pair_rows(S):
    return pl.BlockSpec((None, 8, S), lambda p: (p, 0, 0), pipeline_mode=pl.Buffered(1))


def _head_masks(S):
    return lax.broadcasted_iota(jnp.int32, (S, 128), 1) < HEAD_DIM


_EXP_ZERO = 104.0


def _spread_heads(x):
    src = lax.broadcasted_iota(jnp.int32, (128, 128), 0)
    return (_mm3(x, (src == 0).astype(BF16)), _mm3(x, (src == HEAD_DIM).astype(BF16)))


def _score_bounds(q, k):
    same_head = ((lax.broadcasted_iota(jnp.int32, (128, 128), 0) < HEAD_DIM)
                 == (lax.broadcasted_iota(jnp.int32, (128, 128), 1) < HEAD_DIM)).astype(BF16)

    def max_norm2(x):
        xf = x.astype(F32)
        return jnp.max(_mm2(xf * xf, same_head), axis=0, keepdims=True)

    z = jnp.sqrt(max_norm2(q) * max_norm2(k))
    return jnp.max(z[:, 0:1]) * 1.001 + 1e-3, jnp.max(z[:, 64:65]) * 1.001 + 1e-3


def _for_tiles_back(i, n, tiles_fn):
    def two(t, c):
        tiles_fn([i - 1 - 2 * t, i - 2 - 2 * t])
        return c

    lax.fori_loop(0, lax.shift_right_logical(n, 1), two, 0)

    @pl.when((n & 1) == 1)
    def _():
        tiles_fn([i - n])


def _fox_tiles_back(cr_ref, i, r0, zba, zbb):
    last = cr_ref[:, pl.ds(0, 128)]
    first = cr_ref[:, pl.ds(r0, 128)]
    alive_a = 2.0 * zba + first[0:1, 0:1] - last[2:3, :] > -_EXP_ZERO
    alive_b = 2.0 * zbb + first[1:2, 0:1] - last[3:4, :] > -_EXP_ZERO
    before = lax.broadcasted_iota(jnp.int32, (1, 128), 1) < i
    return jnp.sum((before & (alive_a | alive_b)).astype(jnp.int32))


def _fox_fwd(qs, kn, proj, cqb, crow4, ride=None):
    S = qs.shape[0]
    T = min(_T, S)
    nq = S // T
    n_pairs = FOX_W // 128

    def body(*refs):
        if ride is None:
            q_ref, k_ref, v_ref, cq_ref, cr_ref, o_ref, lse_ref = refs[:7]
            qa, qb, vta, vtb, cka, ckb, ma, mb, acca, accb = refs[7:]
        else:
            q_ref, k_ref, v_ref, cq_ref, cr_ref, wa_ref, wb_ref, o_ref, lse_ref, ga_ref, gb_ref = refs[:11]
            qa, qb, vta, vtb, cka, ckb, ma, mb, acca, accb = refs[11:21]
            xrefs = (wa_ref, wb_ref, ga_ref, gb_ref) + tuple(refs[21:])

            @pl.when(pl.program_id(0) == 0)
            def _():
                _start_exchange("gather", *xrefs)

        lane_s = _head_masks(S)
        q = q_ref[...]
        zq = jnp.zeros_like(q)
        qa[...] = jnp.where(lane_s, q, zq)
        qb[...] = jnp.where(lane_s, zq, q)
        cq = cq_ref[...]
        cka[...], ckb[...] = _spread_heads(cq)
        lse_ref[...] = jnp.zeros((8, S), F32)
        row_t = lax.broadcasted_iota(jnp.int32, (128, T), 0) < HEAD_DIM
        zba, zbb = _score_bounds(q, k_ref[...])

        def prep(c, carry):
            c0 = pl.multiple_of(c * T, T)
            vt = v_ref[pl.ds(c0, T), :].T
            vta[:, pl.ds(c0, T)] = jnp.where(row_t, vt, 1.0).astype(BF16)
            vtb[:, pl.ds(c0, T)] = jnp.where(row_t, 1.0, vt).astype(BF16)
            return carry

        lax.fori_loop(0, nq, prep, 0)
        causal = (lax.broadcasted_iota(jnp.int32, (T, T), 0) <= lax.broadcasted_iota(jnp.int32, (T, T), 1))

        heads = ((qa, vta, cka, ma, acca), (qb, vtb, ckb, mb, accb))

        def kv(js, r0, masked):
            cr = cr_ref[:, pl.ds(r0, T)]
            c0s = [pl.multiple_of(j * T, T) for j in js]
            ks = [k_ref[pl.ds(c0, T), :] for c0 in c0s]
            ss = []
            for h, (qr, _, ckr, _, _) in enumerate(heads):
                qh = qr[pl.ds(r0, T), :]
                row = []
                for k, c0 in zip(ks, c0s):
                    s = _dot_nt(k, qh) + cr[h:h + 1, :] - jnp.tile(ckr[pl.ds(c0, T), :], (1, T // 128))
                    row.append(jnp.where(causal, s, NEG) if masked else s)
                ss.append(row)
            ms = []
            for row, (_, _, _, mr, _) in zip(ss, heads):
                top = row[0]
                for s in row[1:]:
                    top = jnp.maximum(top, s)
                m_old = mr[0:1, :]
                ms.append((m_old, jnp.maximum(m_old, jnp.max(top, axis=0, keepdims=True))))
            ps = [[jnp.exp(s - m_new).astype(BF16) for s in row] for row, (_, m_new) in zip(ss, ms)]
            pvs = []
            for row, (_, vr, _, _, _) in zip(ps, heads):
                pv = _dot(vr[:, pl.ds(c0s[0], T)], row[0])
                for p, c0 in zip(row[1:], c0s[1:]):
                    pv = pv + _dot(vr[:, pl.ds(c0, T)], p)
                pvs.append(pv)
            for pv, (m_old, m_new), (_, _, _, mr, ar) in zip(pvs, ms, heads):
                ar[...] = jnp.exp(m_old - m_new) * ar[...] + pv
                mr[0:1, :] = m_new

        def qblk(i, carry):
            r0 = pl.multiple_of(i * T, T)
            ma[...] = jnp.full((8, T), NEG, F32)
            mb[...] = jnp.full((8, T), NEG, F32)
            acca[...] = jnp.zeros((128, T), F32)
            accb[...] = jnp.zeros((128, T), F32)
            kv([i], r0, True)
            done = _fox_tiles_back(cr_ref, i, r0, zba, zbb)
            _for_tiles_back(i, done, lambda js: kv(js, r0, False))
            aa = acca[...]
            ab = accb[...]
            la = aa[64:65, :]
            lb = ab[0:1, :]
            o_ref[pl.ds(r0, T), :] = jnp.where(row_t, aa / la, ab / lb).T
            lse_ref[0:1, pl.ds(r0, T)] = ma[0:1, :] + jnp.log(la)
            lse_ref[1:2, pl.ds(r0, T)] = mb[0:1, :] + jnp.log(lb)
            lse_ref[2:3, pl.ds(r0, T)] = jnp.broadcast_to(done.astype(F32), (1, T))
            return carry

        lax.fori_loop(0, nq, qblk, 0)
        if ride is not None:
            @pl.when(pl.program_id(0) == n_pairs - 1)
            def _():
                _wait_exchange("gather", *xrefs)

    extra = () if ride is None else tuple(ride)
    return pl.pallas_call(
        body, name="fox_fwd" if ride is None else "fox_fwd_gather",
        grid=(n_pairs,),
        in_specs=[_pair_blk(S), _pair_blk(S), _pair_blk(S, OFF_FV // 128), _pair_blk(S), _pair_rows(S)]
        + [_ANY] * len(extra),
        out_specs=[_pair_blk(S), _pair_rows(S)] + [_ANY] * len(extra),
        out_shape=[jax.ShapeDtypeStruct((S, FOX_W), F32), jax.ShapeDtypeStruct((n_pairs, 8, S), F32)]
        + (_exchange_out_shapes("gather", *extra) if extra else []),
        scratch_shapes=[pltpu.VMEM((S, 128), BF16)] * 2 + [pltpu.VMEM((128, S), BF16)] * 2
        + [pltpu.VMEM((S, 128), F32)] * 2 + [pltpu.VMEM((8, T), F32)] * 2 + [pltpu.VMEM((128, T), F32)] * 2
        + (_EXCHANGE_SEMS if extra else []),
        compiler_params=_cp(("arbitrary",), _VMEM_BIG),
    )(qs, kn, proj, cqb, crow4, *extra)


def _softplus_parts(z):
    e = jnp.exp(-jnp.abs(z))
    return e, jnp.maximum(z, 0.0) + jnp.log(1.0 + e)


def _sb_fwd(proj, triu):
    S = proj.shape[0]
    T = triu.shape[0]
    nq = S // T

    def body(q_ref, k_ref, v_ref, tri_ref, o_ref, lt_ref, qa, qb, kb, vt, ra, rb, acca, accb):
        lane_s = _head_masks(S)
        q = (q_ref[...] * Q_SCALE).astype(BF16)
        zq = jnp.zeros_like(q)
        qa[...] = jnp.where(lane_s, q, zq)
        qb[...] = jnp.where(lane_s, zq, q)
        kb[...] = k_ref[...].astype(BF16)
        lt_ref[...] = jnp.zeros((8, S), F32)
        row_t = lax.broadcasted_iota(jnp.int32, (128, T), 0) < HEAD_DIM
        zba, zbb = _score_bounds(q, kb[...])

        def prep(c, carry):
            c0 = pl.multiple_of(c * T, T)
            vt[:, pl.ds(c0, T)] = v_ref[pl.ds(c0, T), :].T.astype(BF16)
            return carry

        lax.fori_loop(0, nq, prep, 0)
        strict = (lax.broadcasted_iota(jnp.int32, (T, T), 0) < lax.broadcasted_iota(jnp.int32, (T, T), 1))

        heads = ((qa, ra, acca), (qb, rb, accb))

        def kv(tiles, r0):
            tri = tri_ref[...]
            c0s = [pl.multiple_of(j * T, T) for j, _ in tiles]
            ks = [kb[pl.ds(c0, T), :] for c0 in c0s]
            qhs = [qr[pl.ds(r0, T), :] for qr, _, _ in heads]
            zs = [[_dot_nt(k, qh) for k in ks] for qh in qhs]
            lbs = [[jnp.where(strict, -_softplus_parts(z)[1], 0.0) if masked else -_softplus_parts(z)[1]
                    for z, (_, masked) in zip(row, tiles)] for row in zs]
            incs = [[_mm2(lb, tri, left=True) for lb in row] for row in lbs]
            avs = []
            for (_, r_ref, _), zrow, irow in zip(heads, zs, incs):
                r = r_ref[0:1, :]
                av = None
                for z, inc, c0, (_, masked) in zip(zrow, irow, c0s, tiles):
                    a = jnp.exp(z + inc + r)
                    if masked:
                        a = jnp.where(strict, a, 0.0)
                    term = _dot(vt[:, pl.ds(c0, T)], a.astype(BF16))
                    av = term if av is None else av + term
                    r = r + inc[0:1, :]
                avs.append((av, r))
            for (_, r_ref, acc_ref), (av, r) in zip(heads, avs):
                r_ref[0:1, :] = r
                acc_ref[...] = acc_ref[...] + av

        def qblk(i, carry):
            r0 = pl.multiple_of(i * T, T)
            ra[...] = jnp.zeros((8, T), F32)
            rb[...] = jnp.zeros((8, T), F32)
            acca[...] = jnp.zeros((128, T), F32)
            accb[...] = jnp.zeros((128, T), F32)

            @pl.when(i == 0)
            def _():
                kv([(i, True)], r0)

            @pl.when(i > 0)
            def _():
                kv([(i, True), (i - 1, False)], r0)

            def alive():
                return jnp.maximum(jnp.max(ra[0:1, :]) + zba, jnp.max(rb[0:1, :]) + zbb) > -_EXP_ZERO

            def cond(st):
                return (st[0] < i) & st[1]

            def step(st):
                kv([(i - 1 - st[0], False)], r0)
                return st[0] + 1, alive()

            done, _ = lax.while_loop(cond, step, (jnp.minimum(i, 1), alive()))
            o_ref[pl.ds(r0, T), :] = jnp.where(row_t, acca[...], accb[...]).T
            lt_ref[0:1, pl.ds(r0, T)] = ra[0:1, :]
            lt_ref[1:2, pl.ds(r0, T)] = rb[0:1, :]
            lt_ref[2:3, pl.ds(r0, T)] = jnp.broadcast_to(done.astype(F32), (1, T))
            return carry

        lax.fori_loop(0, nq, qblk, 0)

    return pl.pallas_call(
        body, name="sb_fwd",
        grid=(SB_W // 128,),
        in_specs=[_pair_blk(S, OFF_SQ // 128), _pair_blk(S, OFF_SK // 128), _pair_blk(S, OFF_SV // 128),
                  pl.BlockSpec((T, T), lambda p: (0, 0))],
        out_specs=[_pair_blk(S), _pair_rows(S)],
        out_shape=[jax.ShapeDtypeStruct((S, SB_W), F32), jax.ShapeDtypeStruct((SB_W // 128, 8, S), F32)],
        scratch_shapes=[pltpu.VMEM((S, 128), BF16)] * 3 + [pltpu.VMEM((128, S), BF16)]
        + [pltpu.VMEM((8, T), F32)] * 2 + [pltpu.VMEM((128, T), F32)] * 2,
        compiler_params=_cp(("arbitrary",), _VMEM_BIG),
    )(proj, proj, proj, triu)


def _pool_window_lanes(shape):
    lane = lax.broadcasted_iota(jnp.int32, shape, 1)
    return jnp.where(lane < 64, 2, jnp.where(lane < 128, 4, jnp.where(lane < 192, 8, 16)))


def _pool_fwd(proj):
    S = proj.shape[0]

    def body(x_ref, o_ref):
        x = x_ref[...]
        t = lax.broadcasted_iota(jnp.int32, x.shape, 0)
        lane = lax.broadcasted_iota(jnp.int32, x.shape, 1)

        def back(a, k):
            return jnp.where(t >= k, pltpu.roll(a, k, 0), 0.0)

        s1 = x + back(x, 1)
        s2 = s1 + back(s1, 2)
        s4 = s2 + back(s2, 4)
        s8 = s4 + back(s4, 8)
        win = jnp.where(lane < 64, s1, jnp.where(lane < 128, s2, jnp.where(lane < 192, s4, s8)))
        cnt = jnp.minimum(t + 1, _pool_window_lanes(x.shape)).astype(F32)
        o_ref[...] = win / cnt - x

    return pl.pallas_call(
        body, name="pool_fwd",
        grid=(1,),
        in_specs=[pl.BlockSpec((S, POOL_W), lambda i: (0, OFF_PX // POOL_W))],
        out_specs=pl.BlockSpec((S, POOL_W), lambda i: (0, 0)),
        out_shape=jax.ShapeDtypeStruct((S, POOL_W), F32),
        compiler_params=_cp(("arbitrary",), _VMEM_BIG),
    )(proj)


def _silu(g):
    return g * _sigmoid(g)


def _mix_out(fo, so, pooled, proj, wbd, scale, wout, x):
    S, D = x.shape
    tm = min(256, S)

    def body(fo_ref, fg_ref, so_ref, sg_ref, pl_ref, pg_ref, wbd_ref, sc_ref, w_ref, x_ref, y_ref, mxt_ref, mx_ref):
        parts = ((0, fo_ref[...] * _silu(fg_ref[...])),
                 (FOX_W, (_dot(pl_ref[...].astype(BF16), wbd_ref[...]) * sc_ref[...]) * _silu(pg_ref[...])),
                 (FOX_W + POOL_W, so_ref[...] * _silu(sg_ref[...])))
        for off, part in parts:
            w = part.shape[1]
            mx_ref[:, off:off + w] = part.astype(BF16)
            mxt_ref[off:off + w, :] = part.T.astype(BF16)
        y_ref[...] = x_ref[...] + _dot(mx_ref[...], w_ref[...])

    return pl.pallas_call(
        body, name="mix_out",
        grid=(S // tm,),
        in_specs=[pl.BlockSpec((tm, FOX_W), lambda i: (i, 0)),
                  pl.BlockSpec((tm, FOX_W), lambda i: (i, OFF_FG // FOX_W)),
                  pl.BlockSpec((tm, SB_W), lambda i: (i, 0)),
                  pl.BlockSpec((tm, SB_W), lambda i: (i, OFF_SG // SB_W)),
                  pl.BlockSpec((tm, POOL_W), lambda i: (i, 0)),
                  pl.BlockSpec((tm, POOL_W), lambda i: (i, OFF_PG // POOL_W)),
                  pl.BlockSpec((POOL_W, POOL_W), lambda i: (0, 0)),
                  pl.BlockSpec((1, POOL_W), lambda i: (0, 0)),
                  pl.BlockSpec((D_MIX, D), lambda i: (0, 0)),
                  pl.BlockSpec((tm, D), lambda i: (i, 0))],
        out_specs=[pl.BlockSpec((tm, D), lambda i: (i, 0)), pl.BlockSpec((D_MIX, tm), lambda i: (0, i))],
        out_shape=[jax.ShapeDtypeStruct((S, D), F32), jax.ShapeDtypeStruct((D_MIX, S), BF16)],
        scratch_shapes=[pltpu.VMEM((tm, D_MIX), BF16)],
        compiler_params=_cp(("parallel",), 40 << 20),
    )(fo, proj, so, proj, pooled, proj, wbd, scale, wout, x)


def _loss_head(y, target):
    S, D = y.shape
    tm = min(_TM, S)

    def body(y_ref, t_ref, dy_ref, ls_ref):
        @pl.when(pl.program_id(0) == 0)
        def _():
            ls_ref[...] = jnp.zeros_like(ls_ref)

        e = y_ref[...] - t_ref[...]
        dy_ref[...] = e * (1.0 / D)
        ls_ref[...] = ls_ref[...] + jnp.sum(e * e) * (0.5 / D)

    dy, ls = pl.pallas_call(
        body, name="loss_head",
        grid=(S // tm,),
        in_specs=[pl.BlockSpec((tm, D), lambda i: (i, 0)), pl.BlockSpec((tm, D), lambda i: (i, 0))],
        out_specs=[pl.BlockSpec((tm, D), lambda i: (i, 0)), pl.BlockSpec((8, 128), lambda i: (0, 0))],
        out_shape=[jax.ShapeDtypeStruct((S, D), F32), jax.ShapeDtypeStruct((8, 128), F32)],
        compiler_params=_cp(("arbitrary",), 40 << 20),
    )(y, target)
    return dy, ls[0, 0]


def _dsilu(g):
    s = _sigmoid(g)
    return s * (1.0 + g * (1.0 - s))


def _gate_bwd(dy, wout, fo, so, pooled, proj, wbd, scale):
    S, D = dy.shape
    tm = min(256, S)

    def body(dy_ref, w_ref, fo_ref, fg_ref, so_ref, sg_ref, pl_ref, pg_ref, wbd_ref, sc_ref,
             dfo_ref, dfg_ref, dso_ref, dsg_ref, dpg_ref, dpl_ref, dsc_ref, dwbd_ref):
        @pl.when(pl.program_id(0) == 0)
        def _():
            dsc_ref[...] = jnp.zeros_like(dsc_ref)
            dwbd_ref[...] = jnp.zeros_like(dwbd_ref)

        dm = _dot_nt(dy_ref[...].astype(BF16), w_ref[...])
        dmf = dm[:, 0:FOX_W]
        dmp = dm[:, FOX_W:FOX_W + POOL_W]
        dms = dm[:, FOX_W + POOL_W:D_MIX]
        fg = fg_ref[...]
        dfo_ref[...] = dmf * _silu(fg)
        dfg_ref[...] = (dmf * fo_ref[...] * _dsilu(fg)).astype(BF16)
        sg = sg_ref[...]
        dso_ref[...] = dms * _silu(sg)
        dsg_ref[...] = (dms * so_ref[...] * _dsilu(sg)).astype(BF16)
        pg = pg_ref[...]
        plb = pl_ref[...].astype(BF16)
        yw = _dot(plb, wbd_ref[...])
        sc = sc_ref[...]
        dpg_ref[...] = (dmp * (yw * sc) * _dsilu(pg)).astype(BF16)
        dys = dmp * _silu(pg)
        dsc_ref[...] = dsc_ref[...] + jnp.sum(dys * yw, axis=0, keepdims=True)
        dyw = (dys * sc).astype(BF16)
        dpl_ref[...] = _dot_nt(dyw, wbd_ref[...])
        dwbd_ref[...] = dwbd_ref[...] + _dot_tn(plb, dyw)

    return pl.pallas_call(
        body, name="gate_bwd",
        grid=(S // tm,),
        in_specs=[pl.BlockSpec((tm, D), lambda i: (i, 0)),
                  pl.BlockSpec((D_MIX, D), lambda i: (0, 0)),
                  pl.BlockSpec((tm, FOX_W), lambda i: (i, 0)),
                  pl.BlockSpec((tm, FOX_W), lambda i: (i, OFF_FG // FOX_W)),
                  pl.BlockSpec((tm, SB_W), lambda i: (i, 0)),
                  pl.BlockSpec((tm, SB_W), lambda i: (i, OFF_SG // SB_W)),
                  pl.BlockSpec((tm, POOL_W), lambda i: (i, 0)),
                  pl.BlockSpec((tm, POOL_W), lambda i: (i, OFF_PG // POOL_W)),
                  pl.BlockSpec((POOL_W, POOL_W), lambda i: (0, 0)),
                  pl.BlockSpec((1, POOL_W), lambda i: (0, 0))],
        out_specs=[pl.BlockSpec((tm, FOX_W), lambda i: (i, 0)),
                   pl.BlockSpec((tm, FOX_W), lambda i: (i, 0)),
                   pl.BlockSpec((tm, SB_W), lambda i: (i, 0)),
                   pl.BlockSpec((tm, SB_W), lambda i: (i, 0)),
                   pl.BlockSpec((tm, POOL_W), lambda i: (i, 0)),
                   pl.BlockSpec((tm, POOL_W), lambda i: (i, 0)),
                   pl.BlockSpec((1, POOL_W), lambda i: (0, 0)),
                   pl.BlockSpec((POOL_W, POOL_W), lambda i: (0, 0))],
        out_shape=[jax.ShapeDtypeStruct((S, FOX_W), F32), jax.ShapeDtypeStruct((S, FOX_W), BF16),
                   jax.ShapeDtypeStruct((S, SB_W), F32), jax.ShapeDtypeStruct((S, SB_W), BF16),
                   jax.ShapeDtypeStruct((S, POOL_W), BF16), jax.ShapeDtypeStruct((S, POOL_W), F32),
                   jax.ShapeDtypeStruct((1, POOL_W), F32), jax.ShapeDtypeStruct((POOL_W, POOL_W), F32)],
        compiler_params=_cp(("arbitrary",), 40 << 20),
    )(dy, wout, fo, proj, so, proj, pooled, proj, wbd, scale)


def _matmul_acc(at, b, name):
    M, S = at.shape
    N = b.shape[1]
    tk = min(_TK_DW, S)
    tn = min(512, N)
    nk = S // tk

    def body(a_ref, b_ref, o_ref, acc):
        k = pl.program_id(1)

        @pl.when(k == 0)
        def _():
            acc[...] = jnp.zeros_like(acc)

        acc[...] = acc[...] + _dot(a_ref[...], b_ref[...].astype(BF16))

        @pl.when(k == nk - 1)
        def _():
            o_ref[...] = acc[...].astype(BF16)

    return pl.pallas_call(
        body, name=name,
        grid=(N // tn, nk),
        in_specs=[pl.BlockSpec((M, tk), lambda j, k: (0, k)), pl.BlockSpec((tk, tn), lambda j, k: (k, j))],
        out_specs=pl.BlockSpec((M, tn), lambda j, k: (0, j)),
        out_shape=jax.ShapeDtypeStruct((M, N), BF16),
        scratch_shapes=[pltpu.VMEM((M, tn), F32)],
        compiler_params=_cp(("parallel", "arbitrary"), 40 << 20),
    )(at, b)


def _pool_bwd(dpooled):
    S = dpooled.shape[0]

    def body(d_ref, o_ref):
        d = d_ref[...]
        t = lax.broadcasted_iota(jnp.int32, d.shape, 0)
        lane = lax.broadcasted_iota(jnp.int32, d.shape, 1)
        cnt = jnp.minimum(t + 1, _pool_window_lanes(d.shape)).astype(F32)
        u = d / cnt

        def fwd(a, k):
            return jnp.where(t < S - k, pltpu.roll(a, S - k, 0), 0.0)

        s1 = u + fwd(u, 1)
        s2 = s1 + fwd(s1, 2)
        s4 = s2 + fwd(s2, 4)
        s8 = s4 + fwd(s4, 8)
        win = jnp.where(lane < 64, s1, jnp.where(lane < 128, s2, jnp.where(lane < 192, s4, s8)))
        o_ref[...] = (win - d).astype(BF16)

    return pl.pallas_call(
        body, name="pool_bwd",
        grid=(1,),
        in_specs=[pl.BlockSpec((S, POOL_W), lambda i: (0, 0))],
        out_specs=pl.BlockSpec((S, POOL_W), lambda i: (0, 0)),
        out_shape=jax.ShapeDtypeStruct((S, POOL_W), BF16),
        compiler_params=_cp(("arbitrary",), _VMEM_BIG),
    )(dpooled)


def _fox_bwd(qs, kn, proj, dfo, fo, lse, cqb, crow4, ride=None):
    S = qs.shape[0]
    T = min(_T, S)
    nq = S // T
    n_pairs = FOX_W // 128

    def body(*refs):
        if ride is None:
            q_ref, k_ref, v_ref, do_ref, o_ref, lse_ref, cq_ref, cr_ref = refs[:8]
            dq_ref, dk_ref, dv_ref, dck_ref, dcq_ref = refs[8:13]
            scr = refs[13:]
        else:
            q_ref, k_ref, v_ref, do_ref, o_ref, lse_ref, cq_ref, cr_ref, pa_ref, pb_ref = refs[:10]
            dq_ref, dk_ref, dv_ref, dck_ref, dcq_ref, ra_ref, rb_ref = refs[10:17]
            scr = refs[17:32]
            xrefs = (pa_ref, pb_ref, ra_ref, rb_ref) + tuple(refs[32:])

            @pl.when(pl.program_id(0) == 0)
            def _():
                _start_exchange("scatter", *xrefs)

        qa, qb, kta, ktb, vb, doa, dob, cka, ckb, dcka, dckb, dva, dqt, dcqa, dcqb = scr
        lane_s = _head_masks(S)
        q = q_ref[...]
        zq = jnp.zeros_like(q)
        qa[...] = jnp.where(lane_s, q, zq)
        qb[...] = jnp.where(lane_s, zq, q)
        vb[...] = v_ref[...].astype(BF16)
        do = do_ref[...].astype(BF16)
        doa[...] = jnp.where(lane_s, do, zq)
        dob[...] = jnp.where(lane_s, zq, do)
        cq = cq_ref[...]
        cka[...], ckb[...] = _spread_heads(cq)
        zs = jnp.zeros((S, 128), F32)
        dk_ref[...] = zs
        dva[...] = zs
        dcka[...] = zs
        dckb[...] = zs
        dcq_ref[...] = jnp.zeros((8, S), F32)
        row_t = lax.broadcasted_iota(jnp.int32, (128, T), 0) < HEAD_DIM

        def prep(c, carry):
            c0 = pl.multiple_of(c * T, T)
            kt = k_ref[pl.ds(c0, T), :].astype(F32).T
            kta[:, pl.ds(c0, T)] = jnp.where(row_t, kt, 0.0).astype(BF16)
            ktb[:, pl.ds(c0, T)] = jnp.where(row_t, 0.0, kt).astype(BF16)
            return carry

        lax.fori_loop(0, nq, prep, 0)
        causal = (lax.broadcasted_iota(jnp.int32, (T, T), 0) <= lax.broadcasted_iota(jnp.int32, (T, T), 1))

        heads = ((qa, kta, doa, cka, dcka, dcqa), (qb, ktb, dob, ckb, dckb, dcqb))

        def kv(js, r0, lss, dls, masked):
            cr = cr_ref[:, pl.ds(r0, T)]
            c0s = [pl.multiple_of(j * T, T) for j in js]
            ks = [k_ref[pl.ds(c0, T), :] for c0 in c0s]
            vs = [vb[pl.ds(c0, T), :] for c0 in c0s]
            qhs = [hd[0][pl.ds(r0, T), :] for hd in heads]
            dohs = [hd[2][pl.ds(r0, T), :] for hd in heads]
            ss = []
            for h, hd in enumerate(heads):
                row = []
                for k, c0 in zip(ks, c0s):
                    s = _dot_nt(k, qhs[h]) + cr[h:h + 1, :] - jnp.tile(hd[3][pl.ds(c0, T), :], (1, T // 128))
                    row.append(jnp.where(causal, s, NEG) if masked else s)
                ss.append(row)
            ps = [[jnp.exp(s - lss[h]) for s in row] for h, row in enumerate(ss)]
            dps = [[_dot_nt(v, dohs[h]) for v in vs] for h in range(2)]
            dss = [[p * (dp - dls[h]) for p, dp in zip(ps[h], dps[h])] for h in range(2)]
            pbs = [[p.astype(BF16) for p in row] for row in ps]
            dsbs = [[ds.astype(BF16) for ds in row] for row in dss]
            for t, c0 in enumerate(c0s):
                dva[pl.ds(c0, T), :] = dva[pl.ds(c0, T), :] + (_dot(pbs[0][t], dohs[0]) + _dot(pbs[1][t], dohs[1]))
                dk_ref[pl.ds(c0, T), :] = dk_ref[pl.ds(c0, T), :] + (_dot(dsbs[0][t], qhs[0]) + _dot(dsbs[1][t], qhs[1]))
            dq = None
            for h, hd in enumerate(heads):
                for t, c0 in enumerate(c0s):
                    term = _dot(hd[1][:, pl.ds(c0, T)], dsbs[h][t])
                    dq = term if dq is None else dq + term
            dqt[...] = dqt[...] + dq
            for h, hd in enumerate(heads):
                col = jnp.sum(dss[h][0], axis=0, keepdims=True)
                for ds in dss[h][1:]:
                    col = col + jnp.sum(ds, axis=0, keepdims=True)
                hd[5][0:1, :] = hd[5][0:1, :] + col
                for ds, c0 in zip(dss[h], c0s):
                    fold = ds[:, 0:128]
                    for u in range(1, T // 128):
                        fold = fold + ds[:, 128 * u:128 * (u + 1)]
                    hd[4][pl.ds(c0, T), :] = hd[4][pl.ds(c0, T), :] - fold

        def qblk(i, carry):
            r0 = pl.multiple_of(i * T, T)
            dt = (do_ref[pl.ds(r0, T), :] * o_ref[pl.ds(r0, T), :]).T
            dla = jnp.sum(jnp.where(row_t, dt, 0.0), axis=0, keepdims=True)
            dlb = jnp.sum(jnp.where(row_t, 0.0, dt), axis=0, keepdims=True)
            ls = lse_ref[:, pl.ds(r0, T)]
            lss = (ls[0:1, :], ls[1:2, :])
            back = jnp.max(ls[2:3, :]).astype(jnp.int32)
            dqt[...] = jnp.zeros((128, T), F32)
            dcqa[...] = jnp.zeros((8, T), F32)
            dcqb[...] = jnp.zeros((8, T), F32)
            kv([i], r0, lss, (dla, dlb), True)
            _for_tiles_back(i, back, lambda js: kv(js, r0, lss, (dla, dlb), False))
            dq_ref[pl.ds(r0, T), :] = dqt[...].T
            dcq_ref[0:1, pl.ds(r0, T)] = dcqa[0:1, :]
            dcq_ref[1:2, pl.ds(r0, T)] = dcqb[0:1, :]
            return carry

        lax.fori_loop(0, nq, qblk, 0)
        dv_ref[...] = dva[...].astype(BF16)
        dck_ref[...] = jnp.where(lane_s, jnp.sum(dcka[...], axis=1, keepdims=True),
                                 jnp.sum(dckb[...], axis=1, keepdims=True))
        if ride is not None:
            @pl.when(pl.program_id(0) == n_pairs - 1)
            def _():
                _wait_exchange("scatter", *xrefs)

    extra = () if ride is None else tuple(ride)
    return pl.pallas_call(
        body, name="fox_bwd" if ride is None else "fox_bwd_exchange",
        grid=(n_pairs,),
        in_specs=[_pair_blk(S), _pair_blk(S), _pair_blk(S, OFF_FV // 128), _pair_blk(S), _pair_blk(S),
                  _pair_rows(S), _pair_blk(S), _pair_rows(S)] + [_ANY] * len(extra),
        out_specs=[_pair_blk(S), _pair_blk(S), _pair_blk(S), _pair_blk(S), _pair_rows(S)] + [_ANY] * len(extra),
        out_shape=[jax.ShapeDtypeStruct((S, FOX_W), F32), jax.ShapeDtypeStruct((S, FOX_W), F32),
                   jax.ShapeDtypeStruct((S, FOX_W), BF16), jax.ShapeDtypeStruct((S, FOX_W), F32),
                   jax.ShapeDtypeStruct((n_pairs, 8, S), F32)]
        + (_exchange_out_shapes("scatter", *extra) if extra else []),
        scratch_shapes=[pltpu.VMEM((S, 128), BF16)] * 2 + [pltpu.VMEM((128, S), BF16)] * 2
        + [pltpu.VMEM((S, 128), BF16)] * 3 + [pltpu.VMEM((S, 128), F32)] * 5
        + [pltpu.VMEM((128, T), F32)] + [pltpu.VMEM((8, T), F32)] * 2
        + (_EXCHANGE_SEMS if extra else []),
        compiler_params=_cp(("arbitrary",), _VMEM_BIG),
    )(qs, kn, proj, dfo, fo, lse, cqb, crow4, *extra)


def _sb_bwd(proj, dso, ltot, tril):
    S = proj.shape[0]
    T = tril.shape[0]
    nq = S // T

    def body(q_ref, k_ref, v_ref, do_ref, lt_ref, tri_ref, dq_ref, dk_ref, dv_ref,
             qa, qb, k2, kta, ktb, vb, doa, dob, dka, dva, dqt, ra, rb, ga, gb):
        lane_s = _head_masks(S)
        q = (q_ref[...] * Q_SCALE).astype(BF16)
        zq = jnp.zeros_like(q)
        qa[...] = jnp.where(lane_s, q, zq)
        qb[...] = jnp.where(lane_s, zq, q)
        k2[...] = k_ref[...].astype(BF16)
        vb[...] = v_ref[...].astype(BF16)
        do = do_ref[...].astype(BF16)
        doa[...] = jnp.where(lane_s, do, zq)
        dob[...] = jnp.where(lane_s, zq, do)
        dka[...] = jnp.zeros((S, 128), F32)
        dva[...] = jnp.zeros((S, 128), F32)
        row_t = lax.broadcasted_iota(jnp.int32, (128, T), 0) < HEAD_DIM

        def prep(c, carry):
            c0 = pl.multiple_of(c * T, T)
            kt = k_ref[pl.ds(c0, T), :].T
            kta[:, pl.ds(c0, T)] = jnp.where(row_t, kt, 0.0).astype(BF16)
            ktb[:, pl.ds(c0, T)] = jnp.where(row_t, 0.0, kt).astype(BF16)
            return carry

        lax.fori_loop(0, nq, prep, 0)
        strict = (lax.broadcasted_iota(jnp.int32, (T, T), 0) < lax.broadcasted_iota(jnp.int32, (T, T), 1))

        heads = ((qa, kta, doa, ra, ga), (qb, ktb, dob, rb, gb))

        def kv(tiles, r0, lts):
            tri = tri_ref[...]
            c0s = [pl.multiple_of(j * T, T) for j, _ in tiles]
            ks = [k2[pl.ds(c0, T), :] for c0 in c0s]
            vs = [vb[pl.ds(c0, T), :] for c0 in c0s]
            qhs = [hd[0][pl.ds(r0, T), :] for hd in heads]
            dohs = [hd[2][pl.ds(r0, T), :] for hd in heads]
            zs = [[_dot_nt(k, qh) for k in ks] for qh in qhs]
            das = [[_dot_nt(v, doh) for v in vs] for doh in dohs]
            es, lbs = [], []
            for row in zs:
                erow, lrow = [], []
                for z, (_, masked) in zip(row, tiles):
                    e, sp = _softplus_parts(z)
                    erow.append(e)
                    lrow.append(jnp.where(strict, -sp, 0.0) if masked else -sp)
                es.append(erow)
                lbs.append(lrow)
            pres = [[_mm2(lb, tri, left=True) for lb in row] for row in lbs]
            aas, r_ends = [], []
            for hd, zrow, lrow, prow, lt in zip(heads, zs, lbs, pres, lts):
                r = hd[3][0:1, :]
                arow = []
                for z, lb, pre, (_, masked) in zip(zrow, lrow, prow, tiles):
                    a = jnp.exp(z + lb + ((lt - r) - pre))
                    arow.append(jnp.where(strict, a, 0.0) if masked else a)
                    r = r + pre[T - 1:T, :]
                aas.append(arow)
                r_ends.append(r)
            gs = [[a * da for a, da in zip(arow, drow)] for arow, drow in zip(aas, das)]
            gpres = [[_mm2(g, tri, left=True) for g in row] for row in gs]
            dzbs, g_ends = [], []
            for hd, zrow, erow, grow, gprow in zip(heads, zs, es, gs, gpres):
                gc = hd[4][0:1, :]
                drow = []
                for z, e, g, gpre, (_, masked) in zip(zrow, erow, grow, gprow, tiles):
                    inv = 1.0 / (1.0 + e)
                    pos = z >= 0.0
                    sig = jnp.where(pos, 1.0, e) * inv
                    oms = jnp.where(pos, e, 1.0) * inv
                    dz = g * oms - sig * (gc + (gpre - g))
                    if masked:
                        dz = jnp.where(strict, dz, 0.0)
                    drow.append(dz.astype(BF16))
                    gc = gc + gpre[T - 1:T, :]
                dzbs.append(drow)
                g_ends.append(gc)
            dq = None
            for h, hd in enumerate(heads):
                for t, c0 in enumerate(c0s):
                    term = _dot(hd[1][:, pl.ds(c0, T)], dzbs[h][t])
                    dq = term if dq is None else dq + term
            dqt[...] = dqt[...] + dq
            for t, c0 in enumerate(c0s):
                dka[pl.ds(c0, T), :] = dka[pl.ds(c0, T), :] + (_dot(dzbs[0][t], qhs[0]) + _dot(dzbs[1][t], qhs[1]))
                dva[pl.ds(c0, T), :] = dva[pl.ds(c0, T), :] + (_dot(aas[0][t].astype(BF16), dohs[0])
                                                               + _dot(aas[1][t].astype(BF16), dohs[1]))
            for hd, r, gc in zip(heads, r_ends, g_ends):
                hd[3][0:1, :] = r
                hd[4][0:1, :] = gc

        def qblk(i, carry):
            r0 = pl.multiple_of(i * T, T)
            lt = lt_ref[:, pl.ds(r0, T)]
            lts = (lt[0:1, :], lt[1:2, :])
            back = jnp.max(lt[2:3, :]).astype(jnp.int32)
            zt = jnp.zeros((8, T), F32)
            dqt[...] = jnp.zeros((128, T), F32)
            ra[...] = zt
            rb[...] = zt
            ga[...] = zt
            gb[...] = zt

            def inner(j, c):
                kv([(j, False)], r0, lts)
                return c

            @pl.when(back == 0)
            def _():
                kv([(i, True)], r0, lts)

            @pl.when(back > 0)
            def _():
                lax.fori_loop(i - back, i - 1, inner, 0)
                kv([(i - 1, False), (i, True)], r0, lts)
            dq_ref[pl.ds(r0, T), :] = (dqt[...] * Q_SCALE).T.astype(BF16)
            return carry

        lax.fori_loop(0, nq, qblk, 0)
        dk_ref[...] = dka[...].astype(BF16)
        dv_ref[...] = dva[...].astype(BF16)

    return pl.pallas_call(
        body, name="sb_bwd",
        grid=(SB_W // 128,),
        in_specs=[_pair_blk(S, OFF_SQ // 128), _pair_blk(S, OFF_SK // 128), _pair_blk(S, OFF_SV // 128),
                  _pair_blk(S), _pair_rows(S), pl.BlockSpec((T, T), lambda p: (0, 0))],
        out_specs=[_pair_blk(S), _pair_blk(S), _pair_blk(S)],
        out_shape=[jax.ShapeDtypeStruct((S, SB_W), BF16)] * 3,
        scratch_shapes=([pltpu.VMEM((S, 128), BF16)] * 3 + [pltpu.VMEM((128, S), BF16)] * 2
                        + [pltpu.VMEM((S, 128), BF16)] * 3 + [pltpu.VMEM((S, 128), F32)] * 2
                        + [pltpu.VMEM((128, T), F32)] + [pltpu.VMEM((8, T), F32)] * 4),
        compiler_params=_cp(("arbitrary",), _VMEM_BIG),
    )(proj, proj, proj, dso, ltot, tril)


def _head_norm_bwd(x, g, dy, bd):
    ss = _mm2(x * x, bd)
    r = lax.rsqrt(ss * (1.0 / HEAD_DIM) + EPS)
    xr = x * r
    gdy = g * dy
    m = _mm2(xr * gdy, bd) * (1.0 / HEAD_DIM)
    return r * (gdy - xr * m), dy * xr


def _qk_bwd(dqs, dkn, proj, pff, bfp, gq, gk, bd, dccol, triu):
    S = proj.shape[0]
    T = triu.shape[0]
    n = S // T
    rev = lambda col: (lambda i: (n - 1 - i, col))

    def body(dq_ref, dk_ref, q_ref, k_ref, ff_ref, b_ref, gq_ref, gk_ref, bd_ref, dc_ref, tri_ref,
             dfq_ref, dfk_ref, dff_ref, dgq_ref, dgk_ref, dbf_ref, carry):
        @pl.when(pl.program_id(0) == 0)
        def _():
            carry[...] = jnp.zeros_like(carry)
            dgq_ref[...] = jnp.zeros_like(dgq_ref)
            dgk_ref[...] = jnp.zeros_like(dgk_ref)
            dbf_ref[...] = jnp.zeros_like(dbf_ref)

        bdv = bd_ref[...]
        dxq, gq_rows = _head_norm_bwd(q_ref[...], gq_ref[...], dq_ref[...] * Q_SCALE, bdv)
        dfq_ref[...] = dxq.astype(BF16)
        dgq_ref[...] = dgq_ref[...] + jnp.sum(gq_rows, axis=0, keepdims=True)
        dxk, gk_rows = _head_norm_bwd(k_ref[...], gk_ref[...], dk_ref[...], bdv)
        dfk_ref[...] = dxk.astype(BF16)
        dgk_ref[...] = dgk_ref[...] + jnp.sum(gk_rows, axis=0, keepdims=True)
        dlf = _mm3(dc_ref[...], tri_ref[...], left=True) + carry[0:1, :]
        carry[0:1, :] = dlf[0:1, :]
        u = ff_ref[...] + b_ref[...]
        lane = lax.broadcasted_iota(jnp.int32, u.shape, 1)
        dff = jnp.where(lane < N_FF, dlf * _sigmoid(-u), 0.0)
        dff_ref[...] = dff.astype(BF16)
        dbf_ref[...] = dbf_ref[...] + jnp.sum(dff, axis=0, keepdims=True)

    return pl.pallas_call(
        body, name="qk_bwd",
        grid=(n,),
        in_specs=[pl.BlockSpec((T, FOX_W), rev(0)), pl.BlockSpec((T, FOX_W), rev(0)),
                  pl.BlockSpec((T, FOX_W), rev(OFF_FQ // FOX_W)), pl.BlockSpec((T, FOX_W), rev(OFF_FK // FOX_W)),
                  pl.BlockSpec((T, N_FFPAD), rev(0)),
                  pl.BlockSpec((1, N_FFPAD), lambda i: (0, 0)),
                  pl.BlockSpec((1, FOX_W), lambda i: (0, 0)), pl.BlockSpec((1, FOX_W), lambda i: (0, 0)),
                  pl.BlockSpec((FOX_W, FOX_W), lambda i: (0, 0)),
                  pl.BlockSpec((T, N_FFPAD), rev(0)),
                  pl.BlockSpec((T, T), lambda i: (0, 0))],
        out_specs=[pl.BlockSpec((T, FOX_W), rev(0)), pl.BlockSpec((T, FOX_W), rev(0)),
                   pl.BlockSpec((T, N_FFPAD), rev(0)),
                   pl.BlockSpec((1, FOX_W), lambda i: (0, 0)), pl.BlockSpec((1, FOX_W), lambda i: (0, 0)),
                   pl.BlockSpec((1, N_FFPAD), lambda i: (0, 0))],
        out_shape=[jax.ShapeDtypeStruct((S, FOX_W), BF16), jax.ShapeDtypeStruct((S, FOX_W), BF16),
                   jax.ShapeDtypeStruct((S, N_FFPAD), BF16),
                   jax.ShapeDtypeStruct((1, FOX_W), F32), jax.ShapeDtypeStruct((1, FOX_W), F32),
                   jax.ShapeDtypeStruct((1, N_FFPAD), F32)],
        scratch_shapes=[pltpu.VMEM((8, N_FFPAD), F32)],
        compiler_params=_cp(("arbitrary",), 40 << 20),
    )(dqs, dkn, proj, proj, pff, bfp, gq, gk, bd, dccol, triu)


def _inproj_bwd_dx(dpm, dff, wm, wff, x, g, dy):
    S, D = x.shape
    tm = min(_TM_DX, S)

    def body(dp_ref, dff_ref, w_ref, wff_ref, x_ref, g_ref, dy_ref, dx_ref, dg_ref):
        @pl.when(pl.program_id(0) == 0)
        def _():
            dg_ref[...] = jnp.zeros_like(dg_ref)

        dh = _dot_nt(dp_ref[...], w_ref[...]) + _dot_nt(dff_ref[...], wff_ref[...])
        xv = x_ref[...]
        r = _rms_rows(xv)
        xr = xv * r
        dg_ref[...] = dg_ref[...] + jnp.sum(dh * xr, axis=0, keepdims=True)
        gdh = g_ref[...] * dh
        m = jnp.mean(gdh * xr, axis=-1, keepdims=True)
        dx_ref[...] = dy_ref[...] + r * (gdh - xr * m)

    return pl.pallas_call(
        body, name="inproj_bwd_dx",
        grid=(S // tm,),
        in_specs=[pl.BlockSpec((tm, N_MAIN), lambda i: (i, 0)),
                  pl.BlockSpec((tm, N_FFPAD), lambda i: (i, 0)),
                  pl.BlockSpec((D, N_MAIN), lambda i: (0, 0)),
                  pl.BlockSpec((D, N_FFPAD), lambda i: (0, 0)),
                  pl.BlockSpec((tm, D), lambda i: (i, 0)),
                  pl.BlockSpec((1, D), lambda i: (0, 0)),
                  pl.BlockSpec((tm, D), lambda i: (i, 0))],
        out_specs=[pl.BlockSpec((tm, D), lambda i: (i, 0)), pl.BlockSpec((1, D), lambda i: (0, 0))],
        out_shape=[jax.ShapeDtypeStruct((S, D), F32), jax.ShapeDtypeStruct((1, D), F32)],
        compiler_params=_cp(("arbitrary",), 48 << 20),
    )(dpm, dff, wm, wff, x, g, dy)


def _inproj_bwd_dw(ht, dpm, dff):
    D, S = ht.shape
    tk = min(_TK_DW, S)
    tn = 512
    nk = S // tk

    def body(ht_ref, dp_ref, dff_ref, dw_ref, dwff_ref, acc, accff):
        j, k = pl.program_id(0), pl.program_id(1)

        @pl.when(k == 0)
        def _():
            acc[...] = jnp.zeros_like(acc)

        @pl.when((k == 0) & (j == 0))
        def _():
            accff[...] = jnp.zeros_like(accff)

        acc[...] = acc[...] + _dot(ht_ref[...], dp_ref[...])

        @pl.when(j == 0)
        def _():
            accff[...] = accff[...] + _dot(ht_ref[...], dff_ref[...])

        @pl.when(k == nk - 1)
        def _():
            dw_ref[...] = acc[...].astype(BF16)

        @pl.when((k == nk - 1) & (j == 0))
        def _():
            dwff_ref[...] = accff[...].astype(BF16)

    return pl.pallas_call(
        body, name="inproj_bwd_dw",
        grid=(N_MAIN // tn, nk),
        in_specs=[pl.BlockSpec((D, tk), lambda j, k: (0, k)),
                  pl.BlockSpec((tk, tn), lambda j, k: (k, j)),
                  pl.BlockSpec((tk, N_FFPAD), lambda j, k: (k, 0))],
        out_specs=[pl.BlockSpec((D, tn), lambda j, k: (0, j)), pl.BlockSpec((D, N_FFPAD), lambda j, k: (0, 0))],
        out_shape=[jax.ShapeDtypeStruct((D, N_MAIN), BF16), jax.ShapeDtypeStruct((D, N_FFPAD), BF16)],
        scratch_shapes=[pltpu.VMEM((D, tn), F32), pltpu.VMEM((D, N_FFPAD), F32)],
        compiler_params=_cp(("arbitrary", "arbitrary"), 40 << 20),
    )(ht, dpm, dff)


def _constants(T):
    tril = jnp.tril(jnp.ones((T, T), F32)).astype(BF16)
    hid = jnp.arange(FOX_W) // HEAD_DIM
    bd = (hid[:, None] == hid[None, :]).astype(BF16)
    ex = (jnp.arange(N_FFPAD)[:, None] == hid[None, :]).astype(BF16)
    return tril, tril.T, bd, ex


def _crow4(ccol, T):
    S = ccol.shape[0]
    c = ccol[:, :FOX_HEADS].T
    last = jnp.pad(c[:, T - 1::T], ((0, 0), (0, S - S // T)))
    rows = jnp.concatenate([c.reshape(FOX_HEADS // 2, 2, S), last.reshape(FOX_HEADS // 2, 2, S)], axis=1)
    return jnp.pad(rows, ((0, 0), (0, 4), (0, 0)))


def _layer_fwd(x, lw, consts, ride=None):
    tril, triu, bd, ex = consts
    proj, pff, ht = _inproj_fwd(x, lw["g"], lw["wm"], lw["wff"])
    qs, kn, ccol, cqb = _fox_prep(proj, pff, lw["bfp"], lw["gq"], lw["gk"], bd, ex, tril)
    crow4 = _crow4(ccol, tril.shape[0])
    fo, lse, *gathered = _fox_fwd(qs, kn, proj, cqb, crow4, ride)
    so, ltot = _sb_fwd(proj, triu)
    pooled = _pool_fwd(proj)
    y, mixedt = _mix_out(fo, so, pooled, proj, lw["wbd"], lw["scale"], lw["wout"], x)
    return y, (x, proj, pff, ht, qs, kn, cqb, crow4, fo, lse, so, ltot, pooled, mixedt), gathered


def _layer_bwd(dy, saved, lw, consts, ride=None):
    tril, triu, bd, _ = consts
    x, proj, pff, ht, qs, kn, cqb, crow4, fo, lse, so, ltot, pooled, mixedt = saved
    S = x.shape[0]
    dfo, dfg, dso, dsg, dpg, dpooled, dscale, dwbd = _gate_bwd(dy, lw["wout"], fo, so, pooled, proj, lw["wbd"], lw["scale"])
    dwout = _matmul_acc(mixedt, dy, "dw_out")
    dpx = _pool_bwd(dpooled)
    dqs, dkn, dfv, dck, dcq4, *received = _fox_bwd(qs, kn, proj, dfo, fo, lse, cqb, crow4, ride)
    dsq, dsk, dsv = _sb_bwd(proj, dso, ltot, tril)
    dc8 = dck[:, ::HEAD_DIM] + dcq4[:, :2, :].reshape(FOX_HEADS, S).T
    dccol = jnp.pad(dc8, ((0, 0), (0, N_FFPAD - FOX_HEADS)))
    dfq, dfk, dff, dgq, dgk, dbf = _qk_bwd(dqs, dkn, proj, pff, lw["bfp"], lw["gq"], lw["gk"], bd, dccol, triu)
    dpm = jnp.concatenate([dfq, dfk, dfv, dfg, dpx, dpg, dsq, dsk, dsv, dsg], axis=1)
    dx, dng = _inproj_bwd_dx(dpm, dff, lw["wm"], lw["wff"], x, lw["g"], dy)
    dwm, dwff = _inproj_bwd_dw(ht, dpm, dff)
    dwin = jnp.concatenate([dwm[:, :OFF_PX], dwff[:, :N_FF], dwm[:, OFF_PX:]], axis=1)
    grads = {
        "norm_g": dng[0],
        "w_in": dwin,
        "b_f": dbf[0, :N_FF],
        "q_norm_g": dgq[0].reshape(FOX_HEADS, HEAD_DIM).sum(0),
        "k_norm_g": dgk[0].reshape(FOX_HEADS, HEAD_DIM).sum(0),
        "w_pool": jnp.stack([dwbd[64 * i:64 * i + 64, 64 * i:64 * i + 64] for i in range(4)]),
        "pool_scale": dscale[0],
        "w_out": dwout,
    }
    return dx, grads, received


def _layer_weights(l, norm_g, gin, b_f, q_norm_g, k_norm_g, w_pool, pool_scale, gout):
    D = gin.shape[1]
    w = gin.transpose(1, 0, 2).reshape(D, D_IN)
    wm = jnp.concatenate([w[:, :2048], w[:, 2048 + N_FF:]], axis=1)
    wff = jnp.pad(w[:, 2048:2048 + N_FF], ((0, 0), (0, N_FFPAD - N_FF)))
    grp = jnp.arange(POOL_W) // 64
    wbd = jnp.where(grp[:, None] == grp[None, :], jnp.tile(w_pool[l].transpose(1, 0, 2).reshape(64, POOL_W), (4, 1)), 0.0)
    return {
        "g": norm_g[l].reshape(1, D),
        "wm": wm, "wff": wff,
        "bfp": jnp.pad(b_f[l], (0, N_FFPAD - N_FF)).reshape(1, N_FFPAD),
        "gq": jnp.tile(q_norm_g[l], FOX_HEADS).reshape(1, FOX_W),
        "gk": jnp.tile(k_norm_g[l], FOX_HEADS).reshape(1, FOX_W),
        "wbd": wbd.astype(BF16),
        "scale": pool_scale[l].reshape(1, POOL_W),
        "wout": gout.reshape(D_MIX, D),
    }


def _grad_parts(g):
    dwin, dwout = g["w_in"].astype(BF16), g["w_out"].astype(BF16)
    D = dwin.shape[0]
    return (dwin.reshape(D, N_DEV, D_IN // N_DEV).transpose(1, 0, 2),
            dwout.reshape(N_DEV, D_MIX // N_DEV, dwout.shape[1]))


def _train_step(x, target, norm_g, win_sh, b_f, q_norm_g, k_norm_g, w_pool, pool_scale, wout_sh):
    L = norm_g.shape[0]
    consts = _constants(min(_T, x.shape[0]))
    gathered = _exchange_pair("gather", win_sh[0], wout_sh[0], "gather_weights")
    lws, saved = [], []
    h = x
    for l in range(L):
        lws.append(_layer_weights(l, norm_g, gathered[0], b_f, q_norm_g, k_norm_g, w_pool, pool_scale, gathered[1]))
        ride = (win_sh[l + 1], wout_sh[l + 1]) if l + 1 < L else None
        h, sv, gathered = _layer_fwd(h, lws[l], consts, ride)
        saved.append(sv)
    dy, loss = _loss_head(h, target)
    grads, received = [None] * L, [None] * L
    ride = None
    for l in reversed(range(L)):
        dy, grads[l], got = _layer_bwd(dy, saved[l], lws[l], consts, ride)
        if ride is not None:
            received[l + 1] = got
        ride = _grad_parts(grads[l])
    received[0] = _exchange_pair("scatter", ride[0], ride[1], "exchange_grads")
    return loss, dy, grads, received


def _mesh_pos():
    return lax.axis_index("x"), lax.axis_index("y"), lax.axis_index("c")


_FLIPS = [(0, 0, 1), (1, 0, 0), (0, 1, 0), (1, 1, 0), (1, 0, 1), (0, 1, 1), (1, 1, 1)]


def _peers():
    x, y, c = _mesh_pos()
    out = []
    for fx, fy, fc in _FLIPS:
        px = 1 - x if fx else x
        py = 1 - y if fy else y
        pc = 1 - c if fc else c
        out.append(((px, py, pc), 4 * px + 2 * py + pc))
    return out, 4 * x + 2 * y + c


_EXCHANGE_SEMS = [pltpu.SemaphoreType.DMA((14,)), pltpu.SemaphoreType.DMA((14,)), pltpu.SemaphoreType.DMA((2,))]
_ANY = pl.BlockSpec(memory_space=pl.ANY)


def _exchange_copies(kind, a_ref, b_ref, oa_ref, ob_ref, send_sems, recv_sems, loc_sems):
    peers, me = _peers()
    pairs = ((a_ref, oa_ref), (b_ref, ob_ref))
    local = [pltpu.make_async_copy(src if kind == "gather" else src.at[me], dst.at[me], loc_sems.at[t])
             for t, (src, dst) in enumerate(pairs)]
    remote = []
    for k, (dev, idx) in enumerate(peers):
        for t, (src, dst) in enumerate(pairs):
            remote.append(pltpu.make_async_remote_copy(
                src_ref=src if kind == "gather" else src.at[idx], dst_ref=dst.at[me],
                send_sem=send_sems.at[2 * k + t], recv_sem=recv_sems.at[2 * k + t],
                device_id=dev, device_id_type=pl.DeviceIdType.MESH))
    return local, remote


def _start_exchange(kind, *refs):
    local, remote = _exchange_copies(kind, *refs)
    for cp in local + remote:
        cp.start()


def _wait_exchange(kind, *refs):
    local, remote = _exchange_copies(kind, *refs)
    for cp in remote:
        cp.wait_recv()
    for cp in remote:
        cp.wait_send()
    for cp in local:
        cp.wait()


def _exchange_out_shapes(kind, a, b):
    if kind == "gather":
        return [jax.ShapeDtypeStruct((N_DEV,) + a.shape, a.dtype), jax.ShapeDtypeStruct((N_DEV,) + b.shape, b.dtype)]
    return [jax.ShapeDtypeStruct(a.shape, a.dtype), jax.ShapeDtypeStruct(b.shape, b.dtype)]


def _exchange_pair(kind, a, b, name):
    def body(*refs):
        _start_exchange(kind, *refs)
        _wait_exchange(kind, *refs)

    return pl.pallas_call(
        body, name=name,
        in_specs=[_ANY, _ANY], out_specs=[_ANY, _ANY],
        out_shape=_exchange_out_shapes(kind, a, b),
        scratch_shapes=_EXCHANGE_SEMS,
    )(a, b)


def _adam_math(w, g, m, v):
    m_new = ADAM_B1 * m + (1.0 - ADAM_B1) * g
    v_new = ADAM_B2 * v + (1.0 - ADAM_B2) * (g * g)
    m_hat = m_new / (1.0 - ADAM_B1 ** ADAM_STEP)
    v_hat = v_new / (1.0 - ADAM_B2 ** ADAM_STEP)
    delta = -ADAM_LR * (m_hat / (jnp.sqrt(v_hat) + ADAM_EPS) + ADAM_WD * w)
    return delta, m_new, v_new


def _sum_adamw(gparts, w, m, v, name):
    L, R, C = w.shape
    tr = min(128, R)

    def body(*refs):
        gp_refs = refs[:L]
        w_ref, m_ref, v_ref, g_ref, d_ref, nm_ref, nv_ref = refs[L:]
        for l in range(L):
            g = gp_refs[l][0].astype(F32)
            for s in range(1, N_DEV):
                g = g + gp_refs[l][s].astype(F32)
            d, mn, vn = _adam_math(w_ref[l], g, m_ref[l], v_ref[l])
            g_ref[l] = g
            d_ref[l] = d
            nm_ref[l] = mn
            nv_ref[l] = vn

    blk = pl.BlockSpec((L, tr, C), lambda r: (0, r, 0))
    return pl.pallas_call(
        body, name=name,
        grid=(R // tr,),
        in_specs=[pl.BlockSpec((N_DEV, tr, C), lambda r: (0, r, 0))] * L + [blk, blk, blk],
        out_specs=[blk, blk, blk, blk],
        out_shape=[jax.ShapeDtypeStruct((L, R, C), F32)] * 4,
        compiler_params=_cp(("parallel",), 48 << 20),
    )(*gparts, w, m, v)


def _small_update(gpack, wpack, mpack, vpack):
    R = gpack.shape[0]
    VM = pl.BlockSpec(memory_space=pltpu.VMEM)

    def body(g_ref, w_ref, m_ref, v_ref, gs_ref, d_ref, nm_ref, nv_ref, buf, send_sems, recv_sems):
        peers, me = _peers()
        buf[me] = g_ref[...]
        copies = []
        for k, (dev, _) in enumerate(peers):
            cp = pltpu.make_async_remote_copy(
                src_ref=g_ref, dst_ref=buf.at[me], send_sem=send_sems.at[k], recv_sem=recv_sems.at[k],
                device_id=dev, device_id_type=pl.DeviceIdType.MESH)
            cp.start()
            copies.append(cp)
        for cp in copies:
            cp.wait_recv()
        for cp in copies:
            cp.wait_send()
        g = buf[0]
        for s in range(1, N_DEV):
            g = g + buf[s]
        d, mn, vn = _adam_math(w_ref[...], g, m_ref[...], v_ref[...])
        gs_ref[...] = g
        d_ref[...] = d
        nm_ref[...] = mn
        nv_ref[...] = vn

    return pl.pallas_call(
        body, name="small_update",
        in_specs=[VM] * 4, out_specs=[VM] * 4,
        out_shape=[jax.ShapeDtypeStruct((R, 128), F32)] * 4,
        scratch_shapes=[pltpu.VMEM((N_DEV, R, 128), F32), pltpu.SemaphoreType.DMA((7,)), pltpu.SemaphoreType.DMA((7,))],
        compiler_params=_cp(None, 40 << 20),
    )(gpack, wpack, mpack, vpack)


_SMALL = ("norm_g", "b_f", "q_norm_g", "k_norm_g", "w_pool", "pool_scale")


def _pack(parts):
    flat = jnp.concatenate([p.reshape(-1) for p in parts])
    n = flat.shape[0]
    rows = -(-n // (8 * 128)) * 8
    return jnp.pad(flat, (0, rows * 128 - n)).reshape(rows, 128)


def _unpack(packed, like):
    flat = packed.reshape(-1)
    out, o = [], 0
    for p in like:
        out.append(flat[o:o + p.size].reshape(p.shape))
        o += p.size
    return out


def kernel(x, norm_g, w_in, b_f, q_norm_g, k_norm_g, w_pool, pool_scale, w_out, loss_target, m_norm_g, m_w_in, m_b_f, m_q_norm_g, m_k_norm_g, m_w_pool, m_pool_scale, m_w_out, v_norm_g, v_w_in, v_b_f, v_q_norm_g, v_k_norm_g, v_w_pool, v_pool_scale, v_w_out):
    L = w_in.shape[0]

    loss_local, dx, grads, received = _train_step(x[0], loss_target[0], norm_g, w_in.astype(BF16), b_f, q_norm_g,
                                                  k_norm_g, w_pool, pool_scale, w_out.astype(BF16))
    loss = lax.psum(loss_local, MESH_AXES)
    g = {k: jnp.stack([grads[l][k] for l in range(L)]) for k in _SMALL}

    g_win, d_win, nm_win, nv_win = _sum_adamw([r[0] for r in received], w_in, m_w_in, v_w_in, "adamw_w_in")
    g_wout, d_wout, nm_wout, nv_wout = _sum_adamw([r[1] for r in received], w_out, m_w_out, v_w_out, "adamw_w_out")

    ws = dict(norm_g=norm_g, b_f=b_f, q_norm_g=q_norm_g, k_norm_g=k_norm_g, w_pool=w_pool, pool_scale=pool_scale)
    ms = dict(norm_g=m_norm_g, b_f=m_b_f, q_norm_g=m_q_norm_g, k_norm_g=m_k_norm_g, w_pool=m_w_pool, pool_scale=m_pool_scale)
    vs = dict(norm_g=v_norm_g, b_f=v_b_f, q_norm_g=v_q_norm_g, k_norm_g=v_k_norm_g, w_pool=v_w_pool, pool_scale=v_pool_scale)
    like = [ws[k] for k in _SMALL]
    gs_p, d_p, nm_p, nv_p = _small_update(_pack([g[k] for k in _SMALL]), _pack(like),
                                          _pack([ms[k] for k in _SMALL]), _pack([vs[k] for k in _SMALL]))
    gs = dict(zip(_SMALL, _unpack(gs_p, like)))
    ds = dict(zip(_SMALL, _unpack(d_p, like)))
    nms = dict(zip(_SMALL, _unpack(nm_p, like)))
    nvs = dict(zip(_SMALL, _unpack(nv_p, like)))
    gs["w_in"], ds["w_in"], nms["w_in"], nvs["w_in"] = g_win, d_win, nm_win, nv_win
    gs["w_out"], ds["w_out"], nms["w_out"], nvs["w_out"] = g_wout, d_wout, nm_wout, nv_wout

    order = ("norm_g", "w_in", "b_f", "q_norm_g", "k_norm_g", "w_pool", "pool_scale", "w_out")
    return (loss, dx[None], *[gs[k] for k in order], *[ds[k] for k in order],
            *[nms[k] for k in order], *[nvs[k] for k in order])
```

```python
import functools

import jax
import jax.numpy as jnp
from jax import lax
from jax.experimental import pallas as pl
from jax.experimental.pallas import tpu as pltpu

F32 = jnp.float32
BF16 = jnp.bfloat16

EPS = 1e-6
NEG = -1e30
HEAD_DIM = 64
FOX_HEADS = 8
FOX_W = 512
POOL_W = 256
SB_W = 256
D_MIX = 1024
N_FF = 8
N_MAIN = 3584
N_FFPAD = 128
OFF_FQ, OFF_FK, OFF_FV, OFF_FG = 0, 512, 1024, 1536
OFF_PX, OFF_PG = 2048, 2304
OFF_SQ, OFF_SK, OFF_SV, OFF_SG = 2560, 2816, 3072, 3328
D_IN = 3592
Q_SCALE = HEAD_DIM ** -0.5

ADAM_LR = 0.001
ADAM_B1 = 0.9
ADAM_B2 = 0.999
ADAM_EPS = 1e-08
ADAM_WD = 0.01
ADAM_STEP = 10

N_DEV = 8
MESH_AXES = ("x", "y", "c")

_T = 256
_TM = 512
_TM_FWD, _TN_FWD = 2048, 512
_TM_DX = 512
_TK_DW = 1024
_VMEM_BIG = 56 << 20


def _cp(sem=None, vmem=None):
    kw = {}
    if sem is not None:
        kw["dimension_semantics"] = sem
    if vmem is not None:
        kw["vmem_limit_bytes"] = vmem
    return pltpu.CompilerParams(**kw)


def _dot(a, b):
    return jnp.dot(a, b, preferred_element_type=F32)


def _dot_nt(a, b):
    return lax.dot_general(a, b, (((1,), (1,)), ((), ())), preferred_element_type=F32)


def _dot_tn(a, b):
    return lax.dot_general(a, b, (((0,), (0,)), ((), ())), preferred_element_type=F32)


def _mm2(v, m, left=False):
    hi = v.astype(BF16)
    lo = (v - hi.astype(F32)).astype(BF16)
    if left:
        return _dot(m, hi) + _dot(m, lo)
    return _dot(hi, m) + _dot(lo, m)


def _mm3(v, m, left=False):
    a1 = v.astype(BF16)
    r1 = v - a1.astype(F32)
    a2 = r1.astype(BF16)
    a3 = (r1 - a2.astype(F32)).astype(BF16)
    if left:
        return _dot(m, a1) + _dot(m, a2) + _dot(m, a3)
    return _dot(a1, m) + _dot(a2, m) + _dot(a3, m)


def _sigmoid(z):
    return 1.0 / (1.0 + jnp.exp(-z))


def _rms_rows(x):
    return lax.rsqrt(jnp.mean(x * x, axis=-1, keepdims=True) + EPS)


def _inproj_fwd(x, g, wm, wff):
    S, D = x.shape
    tm = min(_TM_FWD, S)
    tn = _TN_FWD

    def body(x_ref, g_ref, w_ref, wff_ref, o_ref, off_ref, ht_ref, h_ref):
        @pl.when(pl.program_id(1) == 0)
        def _():
            xv = x_ref[...]
            h = (xv * _rms_rows(xv)) * g_ref[...]
            h_ref[...] = h.astype(BF16)
            ht_ref[...] = h.T.astype(BF16)
            off_ref[...] = _dot(h_ref[...], wff_ref[...])

        o_ref[...] = _dot(h_ref[...], w_ref[...])

    return pl.pallas_call(
        body, name="inproj_fwd",
        grid=(S // tm, N_MAIN // tn),
        in_specs=[pl.BlockSpec((tm, D), lambda i, j: (i, 0)),
                  pl.BlockSpec((1, D), lambda i, j: (0, 0)),
                  pl.BlockSpec((D, tn), lambda i, j: (0, j)),
                  pl.BlockSpec((D, N_FFPAD), lambda i, j: (0, 0))],
        out_specs=[pl.BlockSpec((tm, tn), lambda i, j: (i, j)),
                   pl.BlockSpec((tm, N_FFPAD), lambda i, j: (i, 0)),
                   pl.BlockSpec((D, tm), lambda i, j: (0, i))],
        out_shape=[jax.ShapeDtypeStruct((S, N_MAIN), F32), jax.ShapeDtypeStruct((S, N_FFPAD), F32),
                   jax.ShapeDtypeStruct((D, S), BF16)],
        scratch_shapes=[pltpu.VMEM((tm, D), BF16)],
        compiler_params=_cp(("parallel", "arbitrary"), 48 << 20),
    )(x, g, wm, wff)


def _head_norm(x, g, bd):
    ss = _mm2(x * x, bd)
    r = lax.rsqrt(ss * (1.0 / HEAD_DIM) + EPS)
    return (x * r) * g


def _fox_prep(proj, pff, bfp, gq, gk, bd, ex, tril):
    S = proj.shape[0]
    T = tril.shape[0]

    def body(q_ref, k_ref, ff_ref, b_ref, gq_ref, gk_ref, bd_ref, ex_ref, tri_ref,
             qs_ref, kn_ref, cc_ref, cqb_ref, carry):
        @pl.when(pl.program_id(0) == 0)
        def _():
            carry[...] = jnp.zeros_like(carry)

        bdv = bd_ref[...]
        qs_ref[...] = (_head_norm(q_ref[...], gq_ref[...], bdv) * Q_SCALE).astype(BF16)
        kn_ref[...] = _head_norm(k_ref[...], gk_ref[...], bdv).astype(BF16)
        u = ff_ref[...] + b_ref[...]
        lf = jnp.minimum(u, 0.0) - jnp.log1p(jnp.exp(-jnp.abs(u)))
        c = _mm3(lf, tri_ref[...], left=True) + carry[0:1, :]
        carry[0:1, :] = c[T - 1:T, :]
        cc_ref[...] = c
        cqb_ref[...] = _mm3(c, ex_ref[...])

    return pl.pallas_call(
        body, name="fox_prep",
        grid=(S // T,),
        in_specs=[pl.BlockSpec((T, FOX_W), lambda i: (i, OFF_FQ // FOX_W)),
                  pl.BlockSpec((T, FOX_W), lambda i: (i, OFF_FK // FOX_W)),
                  pl.BlockSpec((T, N_FFPAD), lambda i: (i, 0)),
                  pl.BlockSpec((1, N_FFPAD), lambda i: (0, 0)),
                  pl.BlockSpec((1, FOX_W), lambda i: (0, 0)),
                  pl.BlockSpec((1, FOX_W), lambda i: (0, 0)),
                  pl.BlockSpec((FOX_W, FOX_W), lambda i: (0, 0)),
                  pl.BlockSpec((N_FFPAD, FOX_W), lambda i: (0, 0)),
                  pl.BlockSpec((T, T), lambda i: (0, 0))],
        out_specs=[pl.BlockSpec((T, FOX_W), lambda i: (i, 0)),
                   pl.BlockSpec((T, FOX_W), lambda i: (i, 0)),
                   pl.BlockSpec((T, N_FFPAD), lambda i: (i, 0)),
                   pl.BlockSpec((T, FOX_W), lambda i: (i, 0))],
        out_shape=[jax.ShapeDtypeStruct((S, FOX_W), BF16), jax.ShapeDtypeStruct((S, FOX_W), BF16),
                   jax.ShapeDtypeStruct((S, N_FFPAD), F32), jax.ShapeDtypeStruct((S, FOX_W), F32)],
        scratch_shapes=[pltpu.VMEM((8, N_FFPAD), F32)],
        compiler_params=_cp(("arbitrary",), 40 << 20),
    )(proj, proj, pff, bfp, gq, gk, bd, ex, tril)


def _pair_blk(S, off=0):
    return pl.BlockSpec((S, 128), lambda p: (0, off + p), pipeline_mode=pl.Buffered(1))


def _pair_rows(S):
    return pl.BlockSpec((None, 8, S), lambda p: (p, 0, 0), pipeline_mode=pl.Buffered(1))


def _head_masks(S):
    return lax.broadcasted_iota(jnp.int32, (S, 128), 1) < HEAD_DIM


_EXP_ZERO = 104.0


def _spread_heads(x):
    src = lax.broadcasted_iota(jnp.int32, (128, 128), 0)
    return (_mm3(x, (src == 0).astype(BF16)), _mm3(x, (src == HEAD_DIM).astype(BF16)))


def _score_bounds(q, k):
    same_head = ((lax.broadcasted_iota(jnp.int32, (128, 128), 0) < HEAD_DIM)
                 == (lax.broadcasted_iota(jnp.int32, (128, 128), 1) < HEAD_DIM)).astype(BF16)

    def max_norm2(x):
        xf = x.astype(F32)
        return jnp.max(_mm2(xf * xf, same_head), axis=0, keepdims=True)

    z = jnp.sqrt(max_norm2(q) * max_norm2(k))
    return jnp.max(z[:, 0:1]) * 1.001 + 1e-3, jnp.max(z[:, 64:65]) * 1.001 + 1e-3


def _for_tiles_back(i, n, tiles_fn, fours=False):
    if fours:
        def four(t, c):
            tiles_fn([i - 1 - 4 * t, i - 2 - 4 * t, i - 3 - 4 * t, i - 4 - 4 * t])
            return c

        lax.fori_loop(0, lax.shift_right_logical(n, 2), four, 0)
        rest = i - (n & ~3)

        @pl.when((n & 2) != 0)
        def _():
            tiles_fn([rest - 1, rest - 2])
    else:
        def two(t, c):
            tiles_fn([i - 1 - 2 * t, i - 2 - 2 * t])
            return c

        lax.fori_loop(0, lax.shift_right_logical(n, 1), two, 0)

    @pl.when((n & 1) != 0)
    def _():
        tiles_fn([i - n])


def _fox_tiles_back(cr_ref, i, r0, zba, zbb):
    last = cr_ref[:, pl.ds(0, 128)]
    first = cr_ref[:, pl.ds(r0, 128)]
    alive_a = 2.0 * zba + first[0:1, 0:1] - last[2:3, :] > -_EXP_ZERO
    alive_b = 2.0 * zbb + first[1:2, 0:1] - last[3:4, :] > -_EXP_ZERO
    before = lax.broadcasted_iota(jnp.int32, (1, 128), 1) < i
    return jnp.sum((before & (alive_a | alive_b)).astype(jnp.int32))


def _fox_fwd(qs, kn, proj, cqb, crow4, ride=None):
    S = qs.shape[0]
    T = min(_T, S)
    nq = S // T
    n_pairs = FOX_W // 128

    def body(*refs):
        if ride is None:
            q_ref, k_ref, v_ref, cq_ref, cr_ref, o_ref, lse_ref = refs[:7]
            qa, qb, vta, vtb, cka, ckb, ma, mb, acca, accb = refs[7:]
        else:
            q_ref, k_ref, v_ref, cq_ref, cr_ref, wa_ref, wb_ref, o_ref, lse_ref, ga_ref, gb_ref = refs[:11]
            qa, qb, vta, vtb, cka, ckb, ma, mb, acca, accb = refs[11:21]
            xrefs = (wa_ref, wb_ref, ga_ref, gb_ref) + tuple(refs[21:])

            @pl.when(pl.program_id(0) == 0)
            def _():
                _start_exchange("gather", *xrefs)

        lane_s = _head_masks(S)
        q = q_ref[...]
        zq = jnp.zeros_like(q)
        qa[...] = jnp.where(lane_s, q, zq)
        qb[...] = jnp.where(lane_s, zq, q)
        cq = cq_ref[...]
        cka[...], ckb[...] = _spread_heads(cq)
        lse_ref[...] = jnp.zeros((8, S), F32)
        row_t = lax.broadcasted_iota(jnp.int32, (128, T), 0) < HEAD_DIM
        zba, zbb = _score_bounds(q, k_ref[...])

        def prep(c, carry):
            c0 = pl.multiple_of(c * T, T)
            vt = v_ref[pl.ds(c0, T), :].T
            vta[:, pl.ds(c0, T)] = jnp.where(row_t, vt, 1.0).astype(BF16)
            vtb[:, pl.ds(c0, T)] = jnp.where(row_t, 1.0, vt).astype(BF16)
            return carry

        lax.fori_loop(0, nq, prep, 0)
        causal = (lax.broadcasted_iota(jnp.int32, (T, T), 0) <= lax.broadcasted_iota(jnp.int32, (T, T), 1))

        heads = ((qa, vta, cka, ma, acca), (qb, vtb, ckb, mb, accb))

        def kv(js, r0, masked):
            cr = cr_ref[:, pl.ds(r0, T)]
            c0s = [pl.multiple_of(j * T, T) for j in js]
            ks = [k_ref[pl.ds(c0, T), :] for c0 in c0s]
            ss = []
            for h, (qr, _, ckr, _, _) in enumerate(heads):
                qh = qr[pl.ds(r0, T), :]
                row = []
                for k, c0 in zip(ks, c0s):
                    s = _dot_nt(k, qh) + cr[h:h + 1, :] - jnp.tile(ckr[pl.ds(c0, T), :], (1, T // 128))
                    row.append(jnp.where(causal, s, NEG) if masked else s)
                ss.append(row)
            ms = []
            for row, (_, _, _, mr, _) in zip(ss, heads):
                top = row[0]
                for s in row[1:]:
                    top = jnp.maximum(top, s)
                m_old = mr[0:1, :]
                ms.append((m_old, jnp.maximum(m_old, jnp.max(top, axis=0, keepdims=True))))
            ps = [[jnp.exp(s - m_new).astype(BF16) for s in row] for row, (_, m_new) in zip(ss, ms)]
            pvs = []
            for row, (_, vr, _, _, _) in zip(ps, heads):
                pv = _dot(vr[:, pl.ds(c0s[0], T)], row[0])
                for p, c0 in zip(row[1:], c0s[1:]):
                    pv = pv + _dot(vr[:, pl.ds(c0, T)], p)
                pvs.append(pv)
            for pv, (m_old, m_new), (_, _, _, mr, ar) in zip(pvs, ms, heads):
                ar[...] = jnp.exp(m_old - m_new) * ar[...] + pv
                mr[0:1, :] = m_new

        def qblk(i, carry):
            r0 = pl.multiple_of(i * T, T)
            ma[...] = jnp.full((8, T), NEG, F32)
            mb[...] = jnp.full((8, T), NEG, F32)
            acca[...] = jnp.zeros((128, T), F32)
            accb[...] = jnp.zeros((128, T), F32)
            kv([i], r0, True)
            done = _fox_tiles_back(cr_ref, i, r0, zba, zbb)
            _for_tiles_back(i, done, lambda js: kv(js, r0, False), fours=True)
            aa = acca[...]
            ab = accb[...]
            la = aa[64:65, :]
            lb = ab[0:1, :]
            o_ref[pl.ds(r0, T), :] = jnp.where(row_t, aa / la, ab / lb).T
            lse_ref[0:1, pl.ds(r0, T)] = ma[0:1, :] + jnp.log(la)
            lse_ref[1:2, pl.ds(r0, T)] = mb[0:1, :] + jnp.log(lb)
            lse_ref[2:3, pl.ds(r0, T)] = jnp.broadcast_to(done.astype(F32), (1, T))
            return carry

        lax.fori_loop(0, nq, qblk, 0)
        if ride is not None:
            @pl.when(pl.program_id(0) == n_pairs - 1)
            def _():
                _wait_exchange("gather", *xrefs)

    extra = () if ride is None else tuple(ride)
    return pl.pallas_call(
        body, name="fox_fwd" if ride is None else "fox_fwd_gather",
        grid=(n_pairs,),
        in_specs=[_pair_blk(S), _pair_blk(S), _pair_blk(S, OFF_FV // 128), _pair_blk(S), _pair_rows(S)]
        + [_ANY] * len(extra),
        out_specs=[_pair_blk(S), _pair_rows(S)] + [_ANY] * len(extra),
        out_shape=[jax.ShapeDtypeStruct((S, FOX_W), F32), jax.ShapeDtypeStruct((n_pairs, 8, S), F32)]
        + (_exchange_out_shapes("gather", *extra) if extra else []),
        scratch_shapes=[pltpu.VMEM((S, 128), BF16)] * 2 + [pltpu.VMEM((128, S), BF16)] * 2
        + [pltpu.VMEM((S, 128), F32)] * 2 + [pltpu.VMEM((8, T), F32)] * 2 + [pltpu.VMEM((128, T), F32)] * 2
        + (_EXCHANGE_SEMS if extra else []),
        compiler_params=_cp(("arbitrary",), _VMEM_BIG),
    )(qs, kn, proj, cqb, crow4, *extra)


def _softplus_parts(z):
    e = jnp.exp(-jnp.abs(z))
    return e, jnp.maximum(z, 0.0) + jnp.log(1.0 + e)


def _sb_fwd(proj, triu):
    S = proj.shape[0]
    T = triu.shape[0]
    nq = S // T

    def body(q_ref, k_ref, v_ref, tri_ref, o_ref, lt_ref, qa, qb, kb, vt, ra, rb, acca, accb):
        lane_s = _head_masks(S)
        q = (q_ref[...] * Q_SCALE).astype(BF16)
        zq = jnp.zeros_like(q)
        qa[...] = jnp.where(lane_s, q, zq)
        qb[...] = jnp.where(lane_s, zq, q)
        kb[...] = k_ref[...].astype(BF16)
        lt_ref[...] = jnp.zeros((8, S), F32)
        row_t = lax.broadcasted_iota(jnp.int32, (128, T), 0) < HEAD_DIM
        zba, zbb = _score_bounds(q, kb[...])

        def prep(c, carry):
            c0 = pl.multiple_of(c * T, T)
            vt[:, pl.ds(c0, T)] = v_ref[pl.ds(c0, T), :].T.astype(BF16)
            return carry

        lax.fori_loop(0, nq, prep, 0)
        strict = (lax.broadcasted_iota(jnp.int32, (T, T), 0) < lax.broadcasted_iota(jnp.int32, (T, T), 1))

        heads = ((qa, ra, acca), (qb, rb, accb))

        def kv(tiles, r0):
            tri = tri_ref[...]
            c0s = [pl.multiple_of(j * T, T) for j, _ in tiles]
            ks = [kb[pl.ds(c0, T), :] for c0 in c0s]
            qhs = [qr[pl.ds(r0, T), :] for qr, _, _ in heads]
            zs = [[_dot_nt(k, qh) for k in ks] for qh in qhs]
            lbs = [[jnp.where(strict, -_softplus_parts(z)[1], 0.0) if masked else -_softplus_parts(z)[1]
                    for z, (_, masked) in zip(row, tiles)] for row in zs]
            incs = [[_mm2(lb, tri, left=True) for lb in row] for row in lbs]
            avs = []
            for (_, r_ref, _), zrow, irow in zip(heads, zs, incs):
                r = r_ref[0:1, :]
                av = None
                for z, inc, c0, (_, masked) in zip(zrow, irow, c0s, tiles):
                    a = jnp.exp(z + inc + r)
                    if masked:
                        a = jnp.where(strict, a, 0.0)
                    term = _dot(vt[:, pl.ds(c0, T)], a.astype(BF16))
                    av = term if av is None else av + term
                    r = r + inc[0:1, :]
                avs.append((av, r))
            for (_, r_ref, acc_ref), (av, r) in zip(heads, avs):
                r_ref[0:1, :] = r
                acc_ref[...] = acc_ref[...] + av

        def qblk(i, carry):
            r0 = pl.multiple_of(i * T, T)
            ra[...] = jnp.zeros((8, T), F32)
            rb[...] = jnp.zeros((8, T), F32)
            acca[...] = jnp.zeros((128, T), F32)
            accb[...] = jnp.zeros((128, T), F32)

            @pl.when(i == 0)
            def _():
                kv([(i, True)], r0)

            @pl.when(i > 0)
            def _():
                kv([(i, True), (i - 1, False)], r0)

            def alive():
                return jnp.maximum(jnp.max(ra[0:1, :]) + zba, jnp.max(rb[0:1, :]) + zbb) > -_EXP_ZERO

            def cond(st):
                return (st[0] < i) & st[1]

            def step(st):
                kv([(i - 1 - st[0], False)], r0)
                return st[0] + 1, alive()

            done, _ = lax.while_loop(cond, step, (jnp.minimum(i, 1), alive()))
            o_ref[pl.ds(r0, T), :] = jnp.where(row_t, acca[...], accb[...]).T
            lt_ref[0:1, pl.ds(r0, T)] = ra[0:1, :]
            lt_ref[1:2, pl.ds(r0, T)] = rb[0:1, :]
            lt_ref[2:3, pl.ds(r0, T)] = jnp.broadcast_to(done.astype(F32), (1, T))
            return carry

        lax.fori_loop(0, nq, qblk, 0)

    return pl.pallas_call(
        body, name="sb_fwd",
        grid=(SB_W // 128,),
        in_specs=[_pair_blk(S, OFF_SQ // 128), _pair_blk(S, OFF_SK // 128), _pair_blk(S, OFF_SV // 128),
                  pl.BlockSpec((T, T), lambda p: (0, 0))],
        out_specs=[_pair_blk(S), _pair_rows(S)],
        out_shape=[jax.ShapeDtypeStruct((S, SB_W), F32), jax.ShapeDtypeStruct((SB_W // 128, 8, S), F32)],
        scratch_shapes=[pltpu.VMEM((S, 128), BF16)] * 3 + [pltpu.VMEM((128, S), BF16)]
        + [pltpu.VMEM((8, T), F32)] * 2 + [pltpu.VMEM((128, T), F32)] * 2,
        compiler_params=_cp(("arbitrary",), _VMEM_BIG),
    )(proj, proj, proj, triu)


def _pool_window_lanes(shape):
    lane = lax.broadcasted_iota(jnp.int32, shape, 1)
    return jnp.where(lane < 64, 2, jnp.where(lane < 128, 4, jnp.where(lane < 192, 8, 16)))


def _pool_fwd(proj):
    S = proj.shape[0]

    def body(x_ref, o_ref):
        x = x_ref[...]
        t = lax.broadcasted_iota(jnp.int32, x.shape, 0)
        lane = lax.broadcasted_iota(jnp.int32, x.shape, 1)

        def back(a, k):
            return jnp.where(t >= k, pltpu.roll(a, k, 0), 0.0)

        s1 = x + back(x, 1)
        s2 = s1 + back(s1, 2)
        s4 = s2 + back(s2, 4)
        s8 = s4 + back(s4, 8)
        win = jnp.where(lane < 64, s1, jnp.where(lane < 128, s2, jnp.where(lane < 192, s4, s8)))
        cnt = jnp.minimum(t + 1, _pool_window_lanes(x.shape)).astype(F32)
        o_ref[...] = win / cnt - x

    return pl.pallas_call(
        body, name="pool_fwd",
        grid=(1,),
        in_specs=[pl.BlockSpec((S, POOL_W), lambda i: (0, OFF_PX // POOL_W))],
        out_specs=pl.BlockSpec((S, POOL_W), lambda i: (0, 0)),
        out_shape=jax.ShapeDtypeStruct((S, POOL_W), F32),
        compiler_params=_cp(("arbitrary",), _VMEM_BIG),
    )(proj)


def _silu(g):
    return g * _sigmoid(g)


def _mix_out(fo, so, pooled, proj, wbd, scale, wout, x):
    S, D = x.shape
    tm = min(256, S)

    def body(fo_ref, fg_ref, so_ref, sg_ref, pl_ref, pg_ref, wbd_ref, sc_ref, w_ref, x_ref, y_ref, mxt_ref, mx_ref):
        parts = ((0, fo_ref[...] * _silu(fg_ref[...])),
                 (FOX_W, (_dot(pl_ref[...].astype(BF16), wbd_ref[...]) * sc_ref[...]) * _silu(pg_ref[...])),
                 (FOX_W + POOL_W, so_ref[...] * _silu(sg_ref[...])))
        for off, part in parts:
            w = part.shape[1]
            mx_ref[:, off:off + w] = part.astype(BF16)
            mxt_ref[off:off + w, :] = part.T.astype(BF16)
        y_ref[...] = x_ref[...] + _dot(mx_ref[...], w_ref[...])

    return pl.pallas_call(
        body, name="mix_out",
        grid=(S // tm,),
        in_specs=[pl.BlockSpec((tm, FOX_W), lambda i: (i, 0)),
                  pl.BlockSpec((tm, FOX_W), lambda i: (i, OFF_FG // FOX_W)),
                  pl.BlockSpec((tm, SB_W), lambda i: (i, 0)),
                  pl.BlockSpec((tm, SB_W), lambda i: (i, OFF_SG // SB_W)),
                  pl.BlockSpec((tm, POOL_W), lambda i: (i, 0)),
                  pl.BlockSpec((tm, POOL_W), lambda i: (i, OFF_PG // POOL_W)),
                  pl.BlockSpec((POOL_W, POOL_W), lambda i: (0, 0)),
                  pl.BlockSpec((1, POOL_W), lambda i: (0, 0)),
                  pl.BlockSpec((D_MIX, D), lambda i: (0, 0)),
                  pl.BlockSpec((tm, D), lambda i: (i, 0))],
        out_specs=[pl.BlockSpec((tm, D), lambda i: (i, 0)), pl.BlockSpec((D_MIX, tm), lambda i: (0, i))],
        out_shape=[jax.ShapeDtypeStruct((S, D), F32), jax.ShapeDtypeStruct((D_MIX, S), BF16)],
        scratch_shapes=[pltpu.VMEM((tm, D_MIX), BF16)],
        compiler_params=_cp(("parallel",), 40 << 20),
    )(fo, proj, so, proj, pooled, proj, wbd, scale, wout, x)


def _loss_head(y, target):
    S, D = y.shape
    tm = min(_TM, S)

    def body(y_ref, t_ref, dy_ref, ls_ref):
        @pl.when(pl.program_id(0) == 0)
        def _():
            ls_ref[...] = jnp.zeros_like(ls_ref)

        e = y_ref[...] - t_ref[...]
        dy_ref[...] = e * (1.0 / D)
        ls_ref[...] = ls_ref[...] + jnp.sum(e * e) * (0.5 / D)

    dy, ls = pl.pallas_call(
        body, name="loss_head",
        grid=(S // tm,),
        in_specs=[pl.BlockSpec((tm, D), lambda i: (i, 0)), pl.BlockSpec((tm, D), lambda i: (i, 0))],
        out_specs=[pl.BlockSpec((tm, D), lambda i: (i, 0)), pl.BlockSpec((8, 128), lambda i: (0, 0))],
        out_shape=[jax.ShapeDtypeStruct((S, D), F32), jax.ShapeDtypeStruct((8, 128), F32)],
        compiler_params=_cp(("arbitrary",), 40 << 20),
    )(y, target)
    return dy, ls[0, 0]


def _dsilu(g):
    s = _sigmoid(g)
    return s * (1.0 + g * (1.0 - s))


def _gate_bwd(dy, wout, fo, so, pooled, proj, wbd, scale):
    S, D = dy.shape
    tm = min(256, S)

    def body(dy_ref, w_ref, fo_ref, fg_ref, so_ref, sg_ref, pl_ref, pg_ref, wbd_ref, sc_ref,
             dfo_ref, dfg_ref, dso_ref, dsg_ref, dpg_ref, dpl_ref, dsc_ref, dwbd_ref):
        @pl.when(pl.program_id(0) == 0)
        def _():
            dsc_ref[...] = jnp.zeros_like(dsc_ref)
            dwbd_ref[...] = jnp.zeros_like(dwbd_ref)

        dm = _dot_nt(dy_ref[...].astype(BF16), w_ref[...])
        dmf = dm[:, 0:FOX_W]
        dmp = dm[:, FOX_W:FOX_W + POOL_W]
        dms = dm[:, FOX_W + POOL_W:D_MIX]
        fg = fg_ref[...]
        dfo_ref[...] = dmf * _silu(fg)
        dfg_ref[...] = (dmf * fo_ref[...] * _dsilu(fg)).astype(BF16)
        sg = sg_ref[...]
        dso_ref[...] = dms * _silu(sg)
        dsg_ref[...] = (dms * so_ref[...] * _dsilu(sg)).astype(BF16)
        pg = pg_ref[...]
        plb = pl_ref[...].astype(BF16)
        yw = _dot(plb, wbd_ref[...])
        sc = sc_ref[...]
        dpg_ref[...] = (dmp * (yw * sc) * _dsilu(pg)).astype(BF16)
        dys = dmp * _silu(pg)
        dsc_ref[...] = dsc_ref[...] + jnp.sum(dys * yw, axis=0, keepdims=True)
        dyw = (dys * sc).astype(BF16)
        dpl_ref[...] = _dot_nt(dyw, wbd_ref[...])
        dwbd_ref[...] = dwbd_ref[...] + _dot_tn(plb, dyw)

    return pl.pallas_call(
        body, name="gate_bwd",
        grid=(S // tm,),
        in_specs=[pl.BlockSpec((tm, D), lambda i: (i, 0)),
                  pl.BlockSpec((D_MIX, D), lambda i: (0, 0)),
                  pl.BlockSpec((tm, FOX_W), lambda i: (i, 0)),
                  pl.BlockSpec((tm, FOX_W), lambda i: (i, OFF_FG // FOX_W)),
                  pl.BlockSpec((tm, SB_W), lambda i: (i, 0)),
                  pl.BlockSpec((tm, SB_W), lambda i: (i, OFF_SG // SB_W)),
                  pl.BlockSpec((tm, POOL_W), lambda i: (i, 0)),
                  pl.BlockSpec((tm, POOL_W), lambda i: (i, OFF_PG // POOL_W)),
                  pl.BlockSpec((POOL_W, POOL_W), lambda i: (0, 0)),
                  pl.BlockSpec((1, POOL_W), lambda i: (0, 0))],
        out_specs=[pl.BlockSpec((tm, FOX_W), lambda i: (i, 0)),
                   pl.BlockSpec((tm, FOX_W), lambda i: (i, 0)),
                   pl.BlockSpec((tm, SB_W), lambda i: (i, 0)),
                   pl.BlockSpec((tm, SB_W), lambda i: (i, 0)),
                   pl.BlockSpec((tm, POOL_W), lambda i: (i, 0)),
                   pl.BlockSpec((tm, POOL_W), lambda i: (i, 0)),
                   pl.BlockSpec((1, POOL_W), lambda i: (0, 0)),
                   pl.BlockSpec((POOL_W, POOL_W), lambda i: (0, 0))],
        out_shape=[jax.ShapeDtypeStruct((S, FOX_W), F32), jax.ShapeDtypeStruct((S, FOX_W), BF16),
                   jax.ShapeDtypeStruct((S, SB_W), F32), jax.ShapeDtypeStruct((S, SB_W), BF16),
                   jax.ShapeDtypeStruct((S, POOL_W), BF16), jax.ShapeDtypeStruct((S, POOL_W), F32),
                   jax.ShapeDtypeStruct((1, POOL_W), F32), jax.ShapeDtypeStruct((POOL_W, POOL_W), F32)],
        compiler_params=_cp(("arbitrary",), 40 << 20),
    )(dy, wout, fo, proj, so, proj, pooled, proj, wbd, scale)


def _matmul_acc(at, b, name):
    M, S = at.shape
    N = b.shape[1]
    tk = min(_TK_DW, S)
    tn = min(512, N)
    nk = S // tk

    def body(a_ref, b_ref, o_ref, acc):
        k = pl.program_id(1)

        @pl.when(k == 0)
        def _():
            acc[...] = jnp.zeros_like(acc)

        acc[...] = acc[...] + _dot(a_ref[...], b_ref[...].astype(BF16))

        @pl.when(k == nk - 1)
        def _():
            o_ref[...] = acc[...].astype(BF16)

    return pl.pallas_call(
        body, name=name,
        grid=(N // tn, nk),
        in_specs=[pl.BlockSpec((M, tk), lambda j, k: (0, k)), pl.BlockSpec((tk, tn), lambda j, k: (k, j))],
        out_specs=pl.BlockSpec((M, tn), lambda j, k: (0, j)),
        out_shape=jax.ShapeDtypeStruct((M, N), BF16),
        scratch_shapes=[pltpu.VMEM((M, tn), F32)],
        compiler_params=_cp(("parallel", "arbitrary"), 40 << 20),
    )(at, b)


def _pool_bwd(dpooled):
    S = dpooled.shape[0]

    def body(d_ref, o_ref):
        d = d_ref[...]
        t = lax.broadcasted_iota(jnp.int32, d.shape, 0)
        lane = lax.broadcasted_iota(jnp.int32, d.shape, 1)
        cnt = jnp.minimum(t + 1, _pool_window_lanes(d.shape)).astype(F32)
        u = d / cnt

        def fwd(a, k):
            return jnp.where(t < S - k, pltpu.roll(a, S - k, 0), 0.0)

        s1 = u + fwd(u, 1)
        s2 = s1 + fwd(s1, 2)
        s4 = s2 + fwd(s2, 4)
        s8 = s4 + fwd(s4, 8)
        win = jnp.where(lane < 64, s1, jnp.where(lane < 128, s2, jnp.where(lane < 192, s4, s8)))
        o_ref[...] = (win - d).astype(BF16)

    return pl.pallas_call(
        body, name="pool_bwd",
        grid=(1,),
        in_specs=[pl.BlockSpec((S, POOL_W), lambda i: (0, 0))],
        out_specs=pl.BlockSpec((S, POOL_W), lambda i: (0, 0)),
        out_shape=jax.ShapeDtypeStruct((S, POOL_W), BF16),
        compiler_params=_cp(("arbitrary",), _VMEM_BIG),
    )(dpooled)


def _fox_bwd(qs, kn, proj, dfo, fo, lse, cqb, crow4, ride=None):
    S = qs.shape[0]
    T = min(_T, S)
    nq = S // T
    n_pairs = FOX_W // 128

    def body(*refs):
        if ride is None:
            q_ref, k_ref, v_ref, do_ref, o_ref, lse_ref, cq_ref, cr_ref = refs[:8]
            dq_ref, dk_ref, dv_ref, dck_ref, dcq_ref = refs[8:13]
            scr = refs[13:]
        else:
            q_ref, k_ref, v_ref, do_ref, o_ref, lse_ref, cq_ref, cr_ref, pa_ref, pb_ref = refs[:10]
            dq_ref, dk_ref, dv_ref, dck_ref, dcq_ref, ra_ref, rb_ref = refs[10:17]
            scr = refs[17:32]
            xrefs = (pa_ref, pb_ref, ra_ref, rb_ref) + tuple(refs[32:])

            @pl.when(pl.program_id(0) == 0)
            def _():
                _start_exchange("scatter", *xrefs)

        qa, qb, kta, ktb, vb, doa, dob, cka, ckb, dcka, dckb, dva, dqt, dcqa, dcqb = scr
        lane_s = _head_masks(S)
        q = q_ref[...]
        zq = jnp.zeros_like(q)
        qa[...] = jnp.where(lane_s, q, zq)
        qb[...] = jnp.where(lane_s, zq, q)
        vb[...] = v_ref[...].astype(BF16)
        do = do_ref[...].astype(BF16)
        doa[...] = jnp.where(lane_s, do, zq)
        dob[...] = jnp.where(lane_s, zq, do)
        cq = cq_ref[...]
        cka[...], ckb[...] = _spread_heads(cq)
        zs = jnp.zeros((S, 128), F32)
        dk_ref[...] = zs
        dva[...] = zs
        dcka[...] = zs
        dckb[...] = zs
        dcq_ref[...] = jnp.zeros((8, S), F32)
        row_t = lax.broadcasted_iota(jnp.int32, (128, T), 0) < HEAD_DIM

        def prep(c, carry):
            c0 = pl.multiple_of(c * T, T)
            kt = k_ref[pl.ds(c0, T), :].astype(F32).T
            kta[:, pl.ds(c0, T)] = jnp.where(row_t, kt, 0.0).astype(BF16)
            ktb[:, pl.ds(c0, T)] = jnp.where(row_t, 0.0, kt).astype(BF16)
            return carry

        lax.fori_loop(0, nq, prep, 0)
        causal = (lax.broadcasted_iota(jnp.int32, (T, T), 0) <= lax.broadcasted_iota(jnp.int32, (T, T), 1))

        heads = ((qa, kta, doa, cka, dcka, dcqa), (qb, ktb, dob, ckb, dckb, dcqb))

        def kv(js, r0, lss, dls, masked):
            cr = cr_ref[:, pl.ds(r0, T)]
            c0s = [pl.multiple_of(j * T, T) for j in js]
            ks = [k_ref[pl.ds(c0, T), :] for c0 in c0s]
            vs = [vb[pl.ds(c0, T), :] for c0 in c0s]
            qhs = [hd[0][pl.ds(r0, T), :] for hd in heads]
            dohs = [hd[2][pl.ds(r0, T), :] for hd in heads]
            ss = []
            for h, hd in enumerate(heads):
                row = []
                for k, c0 in zip(ks, c0s):
                    s = _dot_nt(k, qhs[h]) + cr[h:h + 1, :] - jnp.tile(hd[3][pl.ds(c0, T), :], (1, T // 128))
                    row.append(jnp.where(causal, s, NEG) if masked else s)
                ss.append(row)
            ps = [[jnp.exp(s - lss[h]) for s in row] for h, row in enumerate(ss)]
            dps = [[_dot_nt(v, dohs[h]) for v in vs] for h in range(2)]
            dss = [[p * (dp - dls[h]) for p, dp in zip(ps[h], dps[h])] for h in range(2)]
            pbs = [[p.astype(BF16) for p in row] for row in ps]
            dsbs = [[ds.astype(BF16) for ds in row] for row in dss]
            for t, c0 in enumerate(c0s):
                dva[pl.ds(c0, T), :] = dva[pl.ds(c0, T), :] + (_dot(pbs[0][t], dohs[0]) + _dot(pbs[1][t], dohs[1]))
                dk_ref[pl.ds(c0, T), :] = dk_ref[pl.ds(c0, T), :] + (_dot(dsbs[0][t], qhs[0]) + _dot(dsbs[1][t], qhs[1]))
            dq = None
            for h, hd in enumerate(heads):
                for t, c0 in enumerate(c0s):
                    term = _dot(hd[1][:, pl.ds(c0, T)], dsbs[h][t])
                    dq = term if dq is None else dq + term
            dqt[...] = dqt[...] + dq
            for h, hd in enumerate(heads):
                col = jnp.sum(dss[h][0], axis=0, keepdims=True)
                for ds in dss[h][1:]:
                    col = col + jnp.sum(ds, axis=0, keepdims=True)
                hd[5][0:1, :] = hd[5][0:1, :] + col
                for ds, c0 in zip(dss[h], c0s):
                    fold = ds[:, 0:128]
                    for u in range(1, T // 128):
                        fold = fold + ds[:, 128 * u:128 * (u + 1)]
                    hd[4][pl.ds(c0, T), :] = hd[4][pl.ds(c0, T), :] - fold

        def qblk(i, carry):
            r0 = pl.multiple_of(i * T, T)
            dt = (do_ref[pl.ds(r0, T), :] * o_ref[pl.ds(r0, T), :]).T
            dla = jnp.sum(jnp.where(row_t, dt, 0.0), axis=0, keepdims=True)
            dlb = jnp.sum(jnp.where(row_t, 0.0, dt), axis=0, keepdims=True)
            ls = lse_ref[:, pl.ds(r0, T)]
            lss = (ls[0:1, :], ls[1:2, :])
            back = jnp.max(ls[2:3, :]).astype(jnp.int32)
            dqt[...] = jnp.zeros((128, T), F32)
            dcqa[...] = jnp.zeros((8, T), F32)
            dcqb[...] = jnp.zeros((8, T), F32)
            kv([i], r0, lss, (dla, dlb), True)
            _for_tiles_back(i, back, lambda js: kv(js, r0, lss, (dla, dlb), False))
            dq_ref[pl.ds(r0, T), :] = dqt[...].T
            dcq_ref[0:1, pl.ds(r0, T)] = dcqa[0:1, :]
            dcq_ref[1:2, pl.ds(r0, T)] = dcqb[0:1, :]
            return carry

        lax.fori_loop(0, nq, qblk, 0)
        dv_ref[...] = dva[...].astype(BF16)
        dck_ref[...] = jnp.where(lane_s, jnp.sum(dcka[...], axis=1, keepdims=True),
                                 jnp.sum(dckb[...], axis=1, keepdims=True))
        if ride is not None:
            @pl.when(pl.program_id(0) == n_pairs - 1)
            def _():
                _wait_exchange("scatter", *xrefs)

    extra = () if ride is None else tuple(ride)
    return pl.pallas_call(
        body, name="fox_bwd" if ride is None else "fox_bwd_exchange",
        grid=(n_pairs,),
        in_specs=[_pair_blk(S), _pair_blk(S), _pair_blk(S, OFF_FV // 128), _pair_blk(S), _pair_blk(S),
                  _pair_rows(S), _pair_blk(S), _pair_rows(S)] + [_ANY] * len(extra),
        out_specs=[_pair_blk(S), _pair_blk(S), _pair_blk(S), _pair_blk(S), _pair_rows(S)] + [_ANY] * len(extra),
        out_shape=[jax.ShapeDtypeStruct((S, FOX_W), F32), jax.ShapeDtypeStruct((S, FOX_W), F32),
                   jax.ShapeDtypeStruct((S, FOX_W), BF16), jax.ShapeDtypeStruct((S, FOX_W), F32),
                   jax.ShapeDtypeStruct((n_pairs, 8, S), F32)]
        + (_exchange_out_shapes("scatter", *extra) if extra else []),
        scratch_shapes=[pltpu.VMEM((S, 128), BF16)] * 2 + [pltpu.VMEM((128, S), BF16)] * 2
        + [pltpu.VMEM((S, 128), BF16)] * 3 + [pltpu.VMEM((S, 128), F32)] * 5
        + [pltpu.VMEM((128, T), F32)] + [pltpu.VMEM((8, T), F32)] * 2
        + (_EXCHANGE_SEMS if extra else []),
        compiler_params=_cp(("arbitrary",), _VMEM_BIG),
    )(qs, kn, proj, dfo, fo, lse, cqb, crow4, *extra)


def _sb_bwd(proj, dso, ltot, tril):
    S = proj.shape[0]
    T = tril.shape[0]
    nq = S // T

    def body(q_ref, k_ref, v_ref, do_ref, lt_ref, tri_ref, dq_ref, dk_ref, dv_ref,
             qa, qb, k2, kta, ktb, vb, doa, dob, dka, dva, dqt, ra, rb, ga, gb):
        lane_s = _head_masks(S)
        q = (q_ref[...] * Q_SCALE).astype(BF16)
        zq = jnp.zeros_like(q)
        qa[...] = jnp.where(lane_s, q, zq)
        qb[...] = jnp.where(lane_s, zq, q)
        k2[...] = k_ref[...].astype(BF16)
        vb[...] = v_ref[...].astype(BF16)
        do = do_ref[...].astype(BF16)
        doa[...] = jnp.where(lane_s, do, zq)
        dob[...] = jnp.where(lane_s, zq, do)
        dka[...] = jnp.zeros((S, 128), F32)
        dva[...] = jnp.zeros((S, 128), F32)
        row_t = lax.broadcasted_iota(jnp.int32, (128, T), 0) < HEAD_DIM

        def prep(c, carry):
            c0 = pl.multiple_of(c * T, T)
            kt = k_ref[pl.ds(c0, T), :].T
            kta[:, pl.ds(c0, T)] = jnp.where(row_t, kt, 0.0).astype(BF16)
            ktb[:, pl.ds(c0, T)] = jnp.where(row_t, 0.0, kt).astype(BF16)
            return carry

        lax.fori_loop(0, nq, prep, 0)
        strict = (lax.broadcasted_iota(jnp.int32, (T, T), 0) < lax.broadcasted_iota(jnp.int32, (T, T), 1))

        heads = ((qa, kta, doa, ra, ga), (qb, ktb, dob, rb, gb))

        def kv(tiles, r0, lts):
            tri = tri_ref[...]
            c0s = [pl.multiple_of(j * T, T) for j, _ in tiles]
            ks = [k2[pl.ds(c0, T), :] for c0 in c0s]
            vs = [vb[pl.ds(c0, T), :] for c0 in c0s]
            qhs = [hd[0][pl.ds(r0, T), :] for hd in heads]
            dohs = [hd[2][pl.ds(r0, T), :] for hd in heads]
            zs = [[_dot_nt(k, qh) for k in ks] for qh in qhs]
            das = [[_dot_nt(v, doh) for v in vs] for doh in dohs]
            es, lbs = [], []
            for row in zs:
                erow, lrow = [], []
                for z, (_, masked) in zip(row, tiles):
                    e, sp = _softplus_parts(z)
                    erow.append(e)
                    lrow.append(jnp.where(strict, -sp, 0.0) if masked else -sp)
                es.append(erow)
                lbs.append(lrow)
            pres = [[_mm2(lb, tri, left=True) for lb in row] for row in lbs]
            aas, r_ends = [], []
            for hd, zrow, lrow, prow, lt in zip(heads, zs, lbs, pres, lts):
                r = hd[3][0:1, :]
                arow = []
                for z, lb, pre, (_, masked) in zip(zrow, lrow, prow, tiles):
                    a = jnp.exp(z + lb + ((lt - r) - pre))
                    arow.append(jnp.where(strict, a, 0.0) if masked else a)
                    r = r + pre[T - 1:T, :]
                aas.append(arow)
                r_ends.append(r)
            gs = [[a * da for a, da in zip(arow, drow)] for arow, drow in zip(aas, das)]
            gpres = [[_mm2(g, tri, left=True) for g in row] for row in gs]
            dzbs, g_ends = [], []
            for hd, zrow, erow, grow, gprow in zip(heads, zs, es, gs, gpres):
                gc = hd[4][0:1, :]
                drow = []
                for z, e, g, gpre, (_, masked) in zip(zrow, erow, grow, gprow, tiles):
                    inv = 1.0 / (1.0 + e)
                    pos = z >= 0.0
                    sig = jnp.where(pos, 1.0, e) * inv
                    oms = jnp.where(pos, e, 1.0) * inv
                    dz = g * oms - sig * (gc + (gpre - g))
                    if masked:
                        dz = jnp.where(strict, dz, 0.0)
                    drow.append(dz.astype(BF16))
                    gc = gc + gpre[T - 1:T, :]
                dzbs.append(drow)
                g_ends.append(gc)
            dq = None
            for h, hd in enumerate(heads):
                for t, c0 in enumerate(c0s):
                    term = _dot(hd[1][:, pl.ds(c0, T)], dzbs[h][t])
                    dq = term if dq is None else dq + term
            dqt[...] = dqt[...] + dq
            for t, c0 in enumerate(c0s):
                dka[pl.ds(c0, T), :] = dka[pl.ds(c0, T), :] + (_dot(dzbs[0][t], qhs[0]) + _dot(dzbs[1][t], qhs[1]))
                dva[pl.ds(c0, T), :] = dva[pl.ds(c0, T), :] + (_dot(aas[0][t].astype(BF16), dohs[0])
                                                               + _dot(aas[1][t].astype(BF16), dohs[1]))
            for hd, r, gc in zip(heads, r_ends, g_ends):
                hd[3][0:1, :] = r
                hd[4][0:1, :] = gc

        def qblk(i, carry):
            r0 = pl.multiple_of(i * T, T)
            lt = lt_ref[:, pl.ds(r0, T)]
            lts = (lt[0:1, :], lt[1:2, :])
            back = jnp.max(lt[2:3, :]).astype(jnp.int32)
            zt = jnp.zeros((8, T), F32)
            dqt[...] = jnp.zeros((128, T), F32)
            ra[...] = zt
            rb[...] = zt
            ga[...] = zt
            gb[...] = zt

            def inner(j, c):
                kv([(j, False)], r0, lts)
                return c

            @pl.when(back == 0)
            def _():
                kv([(i, True)], r0, lts)

            @pl.when(back > 0)
            def _():
                lax.fori_loop(i - back, i - 1, inner, 0)
                kv([(i - 1, False), (i, True)], r0, lts)
            dq_ref[pl.ds(r0, T), :] = (dqt[...] * Q_SCALE).T.astype(BF16)
            return carry

        lax.fori_loop(0, nq, qblk, 0)
        dk_ref[...] = dka[...].astype(BF16)
        dv_ref[...] = dva[...].astype(BF16)

    return pl.pallas_call(
        body, name="sb_bwd",
        grid=(SB_W // 128,),
        in_specs=[_pair_blk(S, OFF_SQ // 128), _pair_blk(S, OFF_SK // 128), _pair_blk(S, OFF_SV // 128),
                  _pair_blk(S), _pair_rows(S), pl.BlockSpec((T, T), lambda p: (0, 0))],
        out_specs=[_pair_blk(S), _pair_blk(S), _pair_blk(S)],
        out_shape=[jax.ShapeDtypeStruct((S, SB_W), BF16)] * 3,
        scratch_shapes=([pltpu.VMEM((S, 128), BF16)] * 3 + [pltpu.VMEM((128, S), BF16)] * 2
                        + [pltpu.VMEM((S, 128), BF16)] * 3 + [pltpu.VMEM((S, 128), F32)] * 2
                        + [pltpu.VMEM((128, T), F32)] + [pltpu.VMEM((8, T), F32)] * 4),
        compiler_params=_cp(("arbitrary",), _VMEM_BIG),
    )(proj, proj, proj, dso, ltot, tril)


def _head_norm_bwd(x, g, dy, bd):
    ss = _mm2(x * x, bd)
    r = lax.rsqrt(ss * (1.0 / HEAD_DIM) + EPS)
    xr = x * r
    gdy = g * dy
    m = _mm2(xr * gdy, bd) * (1.0 / HEAD_DIM)
    return r * (gdy - xr * m), dy * xr


def _qk_bwd(dqs, dkn, proj, pff, bfp, gq, gk, bd, dccol, triu):
    S = proj.shape[0]
    T = triu.shape[0]
    n = S // T
    rev = lambda col: (lambda i: (n - 1 - i, col))

    def body(dq_ref, dk_ref, q_ref, k_ref, ff_ref, b_ref, gq_ref, gk_ref, bd_ref, dc_ref, tri_ref,
             dfq_ref, dfk_ref, dff_ref, dgq_ref, dgk_ref, dbf_ref, carry):
        @pl.when(pl.program_id(0) == 0)
        def _():
            carry[...] = jnp.zeros_like(carry)
            dgq_ref[...] = jnp.zeros_like(dgq_ref)
            dgk_ref[...] = jnp.zeros_like(dgk_ref)
            dbf_ref[...] = jnp.zeros_like(dbf_ref)

        bdv = bd_ref[...]
        dxq, gq_rows = _head_norm_bwd(q_ref[...], gq_ref[...], dq_ref[...] * Q_SCALE, bdv)
        dfq_ref[...] = dxq.astype(BF16)
        dgq_ref[...] = dgq_ref[...] + jnp.sum(gq_rows, axis=0, keepdims=True)
        dxk, gk_rows = _head_norm_bwd(k_ref[...], gk_ref[...], dk_ref[...], bdv)
        dfk_ref[...] = dxk.astype(BF16)
        dgk_ref[...] = dgk_ref[...] + jnp.sum(gk_rows, axis=0, keepdims=True)
        dlf = _mm3(dc_ref[...], tri_ref[...], left=True) + carry[0:1, :]
        carry[0:1, :] = dlf[0:1, :]
        u = ff_ref[...] + b_ref[...]
        lane = lax.broadcasted_iota(jnp.int32, u.shape, 1)
        dff = jnp.where(lane < N_FF, dlf * _sigmoid(-u), 0.0)
        dff_ref[...] = dff.astype(BF16)
        dbf_ref[...] = dbf_ref[...] + jnp.sum(dff, axis=0, keepdims=True)

    return pl.pallas_call(
        body, name="qk_bwd",
        grid=(n,),
        in_specs=[pl.BlockSpec((T, FOX_W), rev(0)), pl.BlockSpec((T, FOX_W), rev(0)),
                  pl.BlockSpec((T, FOX_W), rev(OFF_FQ // FOX_W)), pl.BlockSpec((T, FOX_W), rev(OFF_FK // FOX_W)),
                  pl.BlockSpec((T, N_FFPAD), rev(0)),
                  pl.BlockSpec((1, N_FFPAD), lambda i: (0, 0)),
                  pl.BlockSpec((1, FOX_W), lambda i: (0, 0)), pl.BlockSpec((1, FOX_W), lambda i: (0, 0)),
                  pl.BlockSpec((FOX_W, FOX_W), lambda i: (0, 0)),
                  pl.BlockSpec((T, N_FFPAD), rev(0)),
                  pl.BlockSpec((T, T), lambda i: (0, 0))],
        out_specs=[pl.BlockSpec((T, FOX_W), rev(0)), pl.BlockSpec((T, FOX_W), rev(0)),
                   pl.BlockSpec((T, N_FFPAD), rev(0)),
                   pl.BlockSpec((1, FOX_W), lambda i: (0, 0)), pl.BlockSpec((1, FOX_W), lambda i: (0, 0)),
                   pl.BlockSpec((1, N_FFPAD), lambda i: (0, 0))],
        out_shape=[jax.ShapeDtypeStruct((S, FOX_W), BF16), jax.ShapeDtypeStruct((S, FOX_W), BF16),
                   jax.ShapeDtypeStruct((S, N_FFPAD), BF16),
                   jax.ShapeDtypeStruct((1, FOX_W), F32), jax.ShapeDtypeStruct((1, FOX_W), F32),
                   jax.ShapeDtypeStruct((1, N_FFPAD), F32)],
        scratch_shapes=[pltpu.VMEM((8, N_FFPAD), F32)],
        compiler_params=_cp(("arbitrary",), 40 << 20),
    )(dqs, dkn, proj, proj, pff, bfp, gq, gk, bd, dccol, triu)


def _inproj_bwd_dx(dpm, dff, wm, wff, x, g, dy, ride=None):
    S, D = x.shape
    tm = min(_TM_DX, S)
    steps = S // tm

    def body(*refs):
        if ride is None:
            dp_ref, dff_ref, w_ref, wff_ref, x_ref, g_ref, dy_ref, dx_ref, dg_ref = refs
        else:
            dp_ref, dff_ref, w_ref, wff_ref, x_ref, g_ref, dy_ref, pa_ref, pb_ref = refs[:9]
            dx_ref, dg_ref, ra_ref, rb_ref = refs[9:13]
            xrefs = (pa_ref, pb_ref, ra_ref, rb_ref) + tuple(refs[13:])

        @pl.when(pl.program_id(0) == 0)
        def _():
            dg_ref[...] = jnp.zeros_like(dg_ref)
            if ride is not None:
                _start_exchange("scatter", *xrefs)

        dh = _dot_nt(dp_ref[...], w_ref[...]) + _dot_nt(dff_ref[...], wff_ref[...])
        xv = x_ref[...]
        r = _rms_rows(xv)
        xr = xv * r
        dg_ref[...] = dg_ref[...] + jnp.sum(dh * xr, axis=0, keepdims=True)
        gdh = g_ref[...] * dh
        m = jnp.mean(gdh * xr, axis=-1, keepdims=True)
        dx_ref[...] = dy_ref[...] + r * (gdh - xr * m)
        if ride is not None:
            @pl.when(pl.program_id(0) == steps - 1)
            def _():
                _wait_exchange("scatter", *xrefs)

    extra = () if ride is None else tuple(ride)
    return pl.pallas_call(
        body, name="inproj_bwd_dx" if ride is None else "inproj_bwd_dx_exchange",
        grid=(steps,),
        in_specs=[pl.BlockSpec((tm, N_MAIN), lambda i: (i, 0)),
                  pl.BlockSpec((tm, N_FFPAD), lambda i: (i, 0)),
                  pl.BlockSpec((D, N_MAIN), lambda i: (0, 0)),
                  pl.BlockSpec((D, N_FFPAD), lambda i: (0, 0)),
                  pl.BlockSpec((tm, D), lambda i: (i, 0)),
                  pl.BlockSpec((1, D), lambda i: (0, 0)),
                  pl.BlockSpec((tm, D), lambda i: (i, 0))] + [_ANY] * len(extra),
        out_specs=[pl.BlockSpec((tm, D), lambda i: (i, 0)), pl.BlockSpec((1, D), lambda i: (0, 0))] + [_ANY] * len(extra),
        out_shape=[jax.ShapeDtypeStruct((S, D), F32), jax.ShapeDtypeStruct((1, D), F32)]
        + (_exchange_out_shapes("scatter", *extra) if extra else []),
        scratch_shapes=_EXCHANGE_SEMS if extra else [],
        compiler_params=_cp(("arbitrary",), 48 << 20),
    )(dpm, dff, wm, wff, x, g, dy, *extra)


def _inproj_bwd_dw(ht, dpm, dff):
    D, S = ht.shape
    tk = min(_TK_DW, S)
    tn = 512
    nk = S // tk

    def body(ht_ref, dp_ref, dff_ref, dw_ref, dwff_ref, acc, accff):
        j, k = pl.program_id(0), pl.program_id(1)

        @pl.when(k == 0)
        def _():
            acc[...] = jnp.zeros_like(acc)

        @pl.when((k == 0) & (j == 0))
        def _():
            accff[...] = jnp.zeros_like(accff)

        acc[...] = acc[...] + _dot(ht_ref[...], dp_ref[...])

        @pl.when(j == 0)
        def _():
            accff[...] = accff[...] + _dot(ht_ref[...], dff_ref[...])

        @pl.when(k == nk - 1)
        def _():
            dw_ref[...] = acc[...].astype(BF16)

        @pl.when((k == nk - 1) & (j == 0))
        def _():
            dwff_ref[...] = accff[...].astype(BF16)

    return pl.pallas_call(
        body, name="inproj_bwd_dw",
        grid=(N_MAIN // tn, nk),
        in_specs=[pl.BlockSpec((D, tk), lambda j, k: (0, k)),
                  pl.BlockSpec((tk, tn), lambda j, k: (k, j)),
                  pl.BlockSpec((tk, N_FFPAD), lambda j, k: (k, 0))],
        out_specs=[pl.BlockSpec((D, tn), lambda j, k: (0, j)), pl.BlockSpec((D, N_FFPAD), lambda j, k: (0, 0))],
        out_shape=[jax.ShapeDtypeStruct((D, N_MAIN), BF16), jax.ShapeDtypeStruct((D, N_FFPAD), BF16)],
        scratch_shapes=[pltpu.VMEM((D, tn), F32), pltpu.VMEM((D, N_FFPAD), F32)],
        compiler_params=_cp(("arbitrary", "arbitrary"), 40 << 20),
    )(ht, dpm, dff)


def _constants(T):
    tril = jnp.tril(jnp.ones((T, T), F32)).astype(BF16)
    hid = jnp.arange(FOX_W) // HEAD_DIM
    bd = (hid[:, None] == hid[None, :]).astype(BF16)
    ex = (jnp.arange(N_FFPAD)[:, None] == hid[None, :]).astype(BF16)
    return tril, tril.T, bd, ex


def _crow4(ccol, T):
    S = ccol.shape[0]
    c = ccol[:, :FOX_HEADS].T
    last = jnp.pad(c[:, T - 1::T], ((0, 0), (0, S - S // T)))
    rows = jnp.concatenate([c.reshape(FOX_HEADS // 2, 2, S), last.reshape(FOX_HEADS // 2, 2, S)], axis=1)
    return jnp.pad(rows, ((0, 0), (0, 4), (0, 0)))


def _layer_fwd(x, lw, consts, ride=None):
    tril, triu, bd, ex = consts
    proj, pff, ht = _inproj_fwd(x, lw["g"], lw["wm"], lw["wff"])
    qs, kn, ccol, cqb = _fox_prep(proj, pff, lw["bfp"], lw["gq"], lw["gk"], bd, ex, tril)
    crow4 = _crow4(ccol, tril.shape[0])
    fo, lse, *gathered = _fox_fwd(qs, kn, proj, cqb, crow4, ride)
    so, ltot = _sb_fwd(proj, triu)
    pooled = _pool_fwd(proj)
    y, mixedt = _mix_out(fo, so, pooled, proj, lw["wbd"], lw["scale"], lw["wout"], x)
    return y, (x, proj, pff, ht, qs, kn, cqb, crow4, fo, lse, so, ltot, pooled, mixedt), gathered


def _layer_bwd(dy, saved, lw, consts, ride=None, exchange_own=False):
    tril, triu, bd, _ = consts
    x, proj, pff, ht, qs, kn, cqb, crow4, fo, lse, so, ltot, pooled, mixedt = saved
    S = x.shape[0]
    dfo, dfg, dso, dsg, dpg, dpooled, dscale, dwbd = _gate_bwd(dy, lw["wout"], fo, so, pooled, proj, lw["wbd"], lw["scale"])
    dwout = _matmul_acc(mixedt, dy, "dw_out")
    dpx = _pool_bwd(dpooled)
    dqs, dkn, dfv, dck, dcq4, *received = _fox_bwd(qs, kn, proj, dfo, fo, lse, cqb, crow4, ride)
    dsq, dsk, dsv = _sb_bwd(proj, dso, ltot, tril)
    dc8 = dck[:, ::HEAD_DIM] + dcq4[:, :2, :].reshape(FOX_HEADS, S).T
    dccol = jnp.pad(dc8, ((0, 0), (0, N_FFPAD - FOX_HEADS)))
    dfq, dfk, dff, dgq, dgk, dbf = _qk_bwd(dqs, dkn, proj, pff, lw["bfp"], lw["gq"], lw["gk"], bd, dccol, triu)
    dpm = jnp.concatenate([dfq, dfk, dfv, dfg, dpx, dpg, dsq, dsk, dsv, dsg], axis=1)
    dwm, dwff = _inproj_bwd_dw(ht, dpm, dff)
    dwin = jnp.concatenate([dwm[:, :OFF_PX], dwff[:, :N_FF], dwm[:, OFF_PX:]], axis=1)
    own = _grad_parts({"w_in": dwin, "w_out": dwout}) if exchange_own else None
    dx, dng, *received_own = _inproj_bwd_dx(dpm, dff, lw["wm"], lw["wff"], x, lw["g"], dy, own)
    grads = {
        "norm_g": dng[0],
        "w_in": dwin,
        "b_f": dbf[0, :N_FF],
        "q_norm_g": dgq[0].reshape(FOX_HEADS, HEAD_DIM).sum(0),
        "k_norm_g": dgk[0].reshape(FOX_HEADS, HEAD_DIM).sum(0),
        "w_pool": jnp.stack([dwbd[64 * i:64 * i + 64, 64 * i:64 * i + 64] for i in range(4)]),
        "pool_scale": dscale[0],
        "w_out": dwout,
    }
    return dx, grads, received, received_own


def _layer_weights(l, norm_g, gin, b_f, q_norm_g, k_norm_g, w_pool, pool_scale, gout):
    D = gin.shape[1]
    w = gin.transpose(1, 0, 2).reshape(D, D_IN)
    wm = jnp.concatenate([w[:, :2048], w[:, 2048 + N_FF:]], axis=1)
    wff = jnp.pad(w[:, 2048:2048 + N_FF], ((0, 0), (0, N_FFPAD - N_FF)))
    grp = jnp.arange(POOL_W) // 64
    wbd = jnp.where(grp[:, None] == grp[None, :], jnp.tile(w_pool[l].transpose(1, 0, 2).reshape(64, POOL_W), (4, 1)), 0.0)
    return {
        "g": norm_g[l].reshape(1, D),
        "wm": wm, "wff": wff,
        "bfp": jnp.pad(b_f[l], (0, N_FFPAD - N_FF)).reshape(1, N_FFPAD),
        "gq": jnp.tile(q_norm_g[l], FOX_HEADS).reshape(1, FOX_W),
        "gk": jnp.tile(k_norm_g[l], FOX_HEADS).reshape(1, FOX_W),
        "wbd": wbd.astype(BF16),
        "scale": pool_scale[l].reshape(1, POOL_W),
        "wout": gout.reshape(D_MIX, D),
    }


def _grad_parts(g):
    dwin, dwout = g["w_in"].astype(BF16), g["w_out"].astype(BF16)
    D = dwin.shape[0]
    return (dwin.reshape(D, N_DEV, D_IN // N_DEV).transpose(1, 0, 2),
            dwout.reshape(N_DEV, D_MIX // N_DEV, dwout.shape[1]))


def _train_step(x, target, norm_g, win_sh, b_f, q_norm_g, k_norm_g, w_pool, pool_scale, wout_sh):
    L = norm_g.shape[0]
    consts = _constants(min(_T, x.shape[0]))
    gathered = _exchange_pair("gather", win_sh[0], wout_sh[0], "gather_weights")
    lws, saved = [], []
    h = x
    for l in range(L):
        lws.append(_layer_weights(l, norm_g, gathered[0], b_f, q_norm_g, k_norm_g, w_pool, pool_scale, gathered[1]))
        ride = (win_sh[l + 1], wout_sh[l + 1]) if l + 1 < L else None
        h, sv, gathered = _layer_fwd(h, lws[l], consts, ride)
        saved.append(sv)
    dy, loss = _loss_head(h, target)
    grads, received = [None] * L, [None] * L
    ride = None
    for l in reversed(range(L)):
        dy, grads[l], got, got_own = _layer_bwd(dy, saved[l], lws[l], consts, ride, exchange_own=(l == 0))
        if ride is not None:
            received[l + 1] = got
        if l == 0:
            received[0] = got_own
        else:
            ride = _grad_parts(grads[l])
    return loss, dy, grads, received


def _mesh_pos():
    return lax.axis_index("x"), lax.axis_index("y"), lax.axis_index("c")


_FLIPS = [(0, 0, 1), (1, 0, 0), (0, 1, 0), (1, 1, 0), (1, 0, 1), (0, 1, 1), (1, 1, 1)]


def _peers():
    x, y, c = _mesh_pos()
    out = []
    for fx, fy, fc in _FLIPS:
        px = 1 - x if fx else x
        py = 1 - y if fy else y
        pc = 1 - c if fc else c
        out.append(((px, py, pc), 4 * px + 2 * py + pc))
    return out, 4 * x + 2 * y + c


_EXCHANGE_SEMS = [pltpu.SemaphoreType.DMA((14,)), pltpu.SemaphoreType.DMA((14,)), pltpu.SemaphoreType.DMA((2,))]
_ANY = pl.BlockSpec(memory_space=pl.ANY)


def _exchange_copies(kind, a_ref, b_ref, oa_ref, ob_ref, send_sems, recv_sems, loc_sems):
    peers, me = _peers()
    pairs = ((a_ref, oa_ref), (b_ref, ob_ref))
    local = [pltpu.make_async_copy(src if kind == "gather" else src.at[me], dst.at[me], loc_sems.at[t])
             for t, (src, dst) in enumerate(pairs)]
    remote = []
    for k, (dev, idx) in enumerate(peers):
        for t, (src, dst) in enumerate(pairs):
            remote.append(pltpu.make_async_remote_copy(
                src_ref=src if kind == "gather" else src.at[idx], dst_ref=dst.at[me],
                send_sem=send_sems.at[2 * k + t], recv_sem=recv_sems.at[2 * k + t],
                device_id=dev, device_id_type=pl.DeviceIdType.MESH))
    return local, remote


def _start_exchange(kind, *refs):
    local, remote = _exchange_copies(kind, *refs)
    for cp in local + remote:
        cp.start()


def _wait_exchange(kind, *refs):
    local, remote = _exchange_copies(kind, *refs)
    for cp in remote:
        cp.wait_recv()
    for cp in remote:
        cp.wait_send()
    for cp in local:
        cp.wait()


def _exchange_out_shapes(kind, a, b):
    if kind == "gather":
        return [jax.ShapeDtypeStruct((N_DEV,) + a.shape, a.dtype), jax.ShapeDtypeStruct((N_DEV,) + b.shape, b.dtype)]
    return [jax.ShapeDtypeStruct(a.shape, a.dtype), jax.ShapeDtypeStruct(b.shape, b.dtype)]


def _exchange_pair(kind, a, b, name):
    def body(*refs):
        _start_exchange(kind, *refs)
        _wait_exchange(kind, *refs)

    return pl.pallas_call(
        body, name=name,
        in_specs=[_ANY, _ANY], out_specs=[_ANY, _ANY],
        out_shape=_exchange_out_shapes(kind, a, b),
        scratch_shapes=_EXCHANGE_SEMS,
    )(a, b)


def _adam_math(w, g, m, v):
    m_new = ADAM_B1 * m + (1.0 - ADAM_B1) * g
    v_new = ADAM_B2 * v + (1.0 - ADAM_B2) * (g * g)
    m_hat = m_new / (1.0 - ADAM_B1 ** ADAM_STEP)
    v_hat = v_new / (1.0 - ADAM_B2 ** ADAM_STEP)
    delta = -ADAM_LR * (m_hat / (jnp.sqrt(v_hat) + ADAM_EPS) + ADAM_WD * w)
    return delta, m_new, v_new


def _sum_adamw(gparts, w, m, v, name):
    L, R, C = w.shape
    tr = min(128, R)

    def body(*refs):
        gp_refs = refs[:L]
        w_ref, m_ref, v_ref, g_ref, d_ref, nm_ref, nv_ref = refs[L:]
        for l in range(L):
            g = gp_refs[l][0].astype(F32)
            for s in range(1, N_DEV):
                g = g + gp_refs[l][s].astype(F32)
            d, mn, vn = _adam_math(w_ref[l], g, m_ref[l], v_ref[l])
            g_ref[l] = g
            d_ref[l] = d
            nm_ref[l] = mn
            nv_ref[l] = vn

    blk = pl.BlockSpec((L, tr, C), lambda r: (0, r, 0))
    return pl.pallas_call(
        body, name=name,
        grid=(R // tr,),
        in_specs=[pl.BlockSpec((N_DEV, tr, C), lambda r: (0, r, 0))] * L + [blk, blk, blk],
        out_specs=[blk, blk, blk, blk],
        out_shape=[jax.ShapeDtypeStruct((L, R, C), F32)] * 4,
        compiler_params=_cp(("parallel",), 48 << 20),
    )(*gparts, w, m, v)


def _small_update(gpack, wpack, mpack, vpack):
    R = gpack.shape[0]
    VM = pl.BlockSpec(memory_space=pltpu.VMEM)

    def body(g_ref, w_ref, m_ref, v_ref, gs_ref, d_ref, nm_ref, nv_ref, buf, send_sems, recv_sems):
        peers, me = _peers()
        buf[me] = g_ref[...]
        copies = []
        for k, (dev, _) in enumerate(peers):
            cp = pltpu.make_async_remote_copy(
                src_ref=g_ref, dst_ref=buf.at[me], send_sem=send_sems.at[k], recv_sem=recv_sems.at[k],
                device_id=dev, device_id_type=pl.DeviceIdType.MESH)
            cp.start()
            copies.append(cp)
        for cp in copies:
            cp.wait_recv()
        for cp in copies:
            cp.wait_send()
        g = buf[0]
        for s in range(1, N_DEV):
            g = g + buf[s]
        d, mn, vn = _adam_math(w_ref[...], g, m_ref[...], v_ref[...])
        gs_ref[...] = g
        d_ref[...] = d
        nm_ref[...] = mn
        nv_ref[...] = vn

    return pl.pallas_call(
        body, name="small_update",
        in_specs=[VM] * 4, out_specs=[VM] * 4,
        out_shape=[jax.ShapeDtypeStruct((R, 128), F32)] * 4,
        scratch_shapes=[pltpu.VMEM((N_DEV, R, 128), F32), pltpu.SemaphoreType.DMA((7,)), pltpu.SemaphoreType.DMA((7,))],
        compiler_params=_cp(None, 40 << 20),
    )(gpack, wpack, mpack, vpack)


_SMALL = ("norm_g", "b_f", "q_norm_g", "k_norm_g", "w_pool", "pool_scale")


def _pack(parts):
    flat = jnp.concatenate([p.reshape(-1) for p in parts])
    n = flat.shape[0]
    rows = -(-n // (8 * 128)) * 8
    return jnp.pad(flat, (0, rows * 128 - n)).reshape(rows, 128)


def _unpack(packed, like):
    flat = packed.reshape(-1)
    out, o = [], 0
    for p in like:
        out.append(flat[o:o + p.size].reshape(p.shape))
        o += p.size
    return out


def kernel(x, norm_g, w_in, b_f, q_norm_g, k_norm_g, w_pool, pool_scale, w_out, loss_target, m_norm_g, m_w_in, m_b_f, m_q_norm_g, m_k_norm_g, m_w_pool, m_pool_scale, m_w_out, v_norm_g, v_w_in, v_b_f, v_q_norm_g, v_k_norm_g, v_w_pool, v_pool_scale, v_w_out):
    L = w_in.shape[0]

    loss_local, dx, grads, received = _train_step(x[0], loss_target[0], norm_g, w_in.astype(BF16), b_f, q_norm_g,
                                                  k_norm_g, w_pool, pool_scale, w_out.astype(BF16))
    loss = lax.psum(loss_local, MESH_AXES)
    g = {k: jnp.stack([grads[l][k] for l in range(L)]) for k in _SMALL}

    g_win, d_win, nm_win, nv_win = _sum_adamw([r[0] for r in received], w_in, m_w_in, v_w_in, "adamw_w_in")
    g_wout, d_wout, nm_wout, nv_wout = _sum_adamw([r[1] for r in received], w_out, m_w_out, v_w_out, "adamw_w_out")

    ws = dict(norm_g=norm_g, b_f=b_f, q_norm_g=q_norm_g, k_norm_g=k_norm_g, w_pool=w_pool, pool_scale=pool_scale)
    ms = dict(norm_g=m_norm_g, b_f=m_b_f, q_norm_g=m_q_norm_g, k_norm_g=m_k_norm_g, w_pool=m_w_pool, pool_scale=m_pool_scale)
    vs = dict(norm_g=v_norm_g, b_f=v_b_f, q_norm_g=v_q_norm_g, k_norm_g=v_k_norm_g, w_pool=v_w_pool, pool_scale=v_pool_scale)
    like = [ws[k] for k in _SMALL]
    gs_p, d_p, nm_p, nv_p = _small_update(_pack([g[k] for k in _SMALL]), _pack(like),
                                          _pack([ms[k] for k in _SMALL]), _pack([vs[k] for k in _SMALL]))
    gs = dict(zip(_SMALL, _unpack(gs_p, like)))
    ds = dict(zip(_SMALL, _unpack(d_p, like)))
    nms = dict(zip(_SMALL, _unpack(nm_p, like)))
    nvs = dict(zip(_SMALL, _unpack(nv_p, like)))
    gs["w_in"], ds["w_in"], nms["w_in"], nvs["w_in"] = g_win, d_win, nm_win, nv_win
    gs["w_out"], ds["w_out"], nms["w_out"], nvs["w_out"] = g_wout, d_wout, nm_wout, nv_wout

    order = ("norm_g", "w_in", "b_f", "q_norm_g", "k_norm_g", "w_pool", "pool_scale", "w_out")
    return (loss, dx[None], *[gs[k] for k in order], *[ds[k] for k in order],
            *[nms[k] for k in order], *[nvs[k] for k in order])
```

```python
import functools

import jax
import jax.numpy as jnp
from jax import lax
from jax.experimental import pallas as pl
from jax.experimental.pallas import tpu as pltpu

F32 = jnp.float32
BF16 = jnp.bfloat16

EPS = 1e-6
NEG = -1e30
HEAD_DIM = 64
FOX_HEADS = 8
FOX_W = 512
POOL_W = 256
SB_W = 256
D_MIX = 1024
N_FF = 8
N_MAIN = 3584
N_FFPAD = 128
OFF_FQ, OFF_FK, OFF_FV, OFF_FG = 0, 512, 1024, 1536
OFF_PX, OFF_PG = 2048, 2304
OFF_SQ, OFF_SK, OFF_SV, OFF_SG = 2560, 2816, 3072, 3328
D_IN = 3592
Q_SCALE = HEAD_DIM ** -0.5

ADAM_LR = 0.001
ADAM_B1 = 0.9
ADAM_B2 = 0.999
ADAM_EPS = 1e-08
ADAM_WD = 0.01
ADAM_STEP = 10

N_DEV = 8
MESH_AXES = ("x", "y", "c")

_T = 256
_TM = 512
_TM_FWD, _TN_FWD = 2048, 512
_TM_DX = 512
_TK_DW = 1024
_VMEM_BIG = 56 << 20


def _cp(sem=None, vmem=None):
    kw = {}
    if sem is not None:
        kw["dimension_semantics"] = sem
    if vmem is not None:
        kw["vmem_limit_bytes"] = vmem
    return pltpu.CompilerParams(**kw)


def _dot(a, b):
    return jnp.dot(a, b, preferred_element_type=F32)


def _dot_nt(a, b):
    return lax.dot_general(a, b, (((1,), (1,)), ((), ())), preferred_element_type=F32)


def _dot_tn(a, b):
    return lax.dot_general(a, b, (((0,), (0,)), ((), ())), preferred_element_type=F32)


def _mm2(v, m, left=False):
    hi = v.astype(BF16)
    lo = (v - hi.astype(F32)).astype(BF16)
    if left:
        return _dot(m, hi) + _dot(m, lo)
    return _dot(hi, m) + _dot(lo, m)


def _mm3(v, m, left=False):
    a1 = v.astype(BF16)
    r1 = v - a1.astype(F32)
    a2 = r1.astype(BF16)
    a3 = (r1 - a2.astype(F32)).astype(BF16)
    if left:
        return _dot(m, a1) + _dot(m, a2) + _dot(m, a3)
    return _dot(a1, m) + _dot(a2, m) + _dot(a3, m)


def _sigmoid(z):
    return 1.0 / (1.0 + jnp.exp(-z))


def _rms_rows(x):
    return lax.rsqrt(jnp.mean(x * x, axis=-1, keepdims=True) + EPS)


def _inproj_fwd(x, g, wm, wff):
    S, D = x.shape
    tm = min(_TM_FWD, S)
    tn = _TN_FWD

    def body(x_ref, g_ref, w_ref, wff_ref, o_ref, off_ref, ht_ref, h_ref):
        @pl.when(pl.program_id(1) == 0)
        def _():
            xv = x_ref[...]
            h = (xv * _rms_rows(xv)) * g_ref[...]
            h_ref[...] = h.astype(BF16)
            ht_ref[...] = h.T.astype(BF16)
            off_ref[...] = _dot(h_ref[...], wff_ref[...])

        o_ref[...] = _dot(h_ref[...], w_ref[...])

    return pl.pallas_call(
        body, name="inproj_fwd",
        grid=(S // tm, N_MAIN // tn),
        in_specs=[pl.BlockSpec((tm, D), lambda i, j: (i, 0)),
                  pl.BlockSpec((1, D), lambda i, j: (0, 0)),
                  pl.BlockSpec((D, tn), lambda i, j: (0, j)),
                  pl.BlockSpec((D, N_FFPAD), lambda i, j: (0, 0))],
        out_specs=[pl.BlockSpec((tm, tn), lambda i, j: (i, j)),
                   pl.BlockSpec((tm, N_FFPAD), lambda i, j: (i, 0)),
                   pl.BlockSpec((D, tm), lambda i, j: (0, i))],
        out_shape=[jax.ShapeDtypeStruct((S, N_MAIN), F32), jax.ShapeDtypeStruct((S, N_FFPAD), F32),
                   jax.ShapeDtypeStruct((D, S), BF16)],
        scratch_shapes=[pltpu.VMEM((tm, D), BF16)],
        compiler_params=_cp(("parallel", "arbitrary"), 48 << 20),
    )(x, g, wm, wff)


def _head_norm(x, g, bd):
    ss = _mm2(x * x, bd)
    r = lax.rsqrt(ss * (1.0 / HEAD_DIM) + EPS)
    return (x * r) * g


def _fox_prep(proj, pff, bfp, gq, gk, bd, ex, tril):
    S = proj.shape[0]
    T = tril.shape[0]

    def body(q_ref, k_ref, ff_ref, b_ref, gq_ref, gk_ref, bd_ref, ex_ref, tri_ref,
             qs_ref, kn_ref, cc_ref, cqb_ref, carry):
        @pl.when(pl.program_id(0) == 0)
        def _():
            carry[...] = jnp.zeros_like(carry)

        bdv = bd_ref[...]
        qs_ref[...] = (_head_norm(q_ref[...], gq_ref[...], bdv) * Q_SCALE).astype(BF16)
        kn_ref[...] = _head_norm(k_ref[...], gk_ref[...], bdv).astype(BF16)
        u = ff_ref[...] + b_ref[...]
        lf = jnp.minimum(u, 0.0) - jnp.log1p(jnp.exp(-jnp.abs(u)))
        c = _mm3(lf, tri_ref[...], left=True) + carry[0:1, :]
        carry[0:1, :] = c[T - 1:T, :]
        cc_ref[...] = c
        cqb_ref[...] = _mm3(c, ex_ref[...])

    return pl.pallas_call(
        body, name="fox_prep",
        grid=(S // T,),
        in_specs=[pl.BlockSpec((T, FOX_W), lambda i: (i, OFF_FQ // FOX_W)),
                  pl.BlockSpec((T, FOX_W), lambda i: (i, OFF_FK // FOX_W)),
                  pl.BlockSpec((T, N_FFPAD), lambda i: (i, 0)),
                  pl.BlockSpec((1, N_FFPAD), lambda i: (0, 0)),
                  pl.BlockSpec((1, FOX_W), lambda i: (0, 0)),
                  pl.BlockSpec((1, FOX_W), lambda i: (0, 0)),
                  pl.BlockSpec((FOX_W, FOX_W), lambda i: (0, 0)),
                  pl.BlockSpec((N_FFPAD, FOX_W), lambda i: (0, 0)),
                  pl.BlockSpec((T, T), lambda i: (0, 0))],
        out_specs=[pl.BlockSpec((T, FOX_W), lambda i: (i, 0)),
                   pl.BlockSpec((T, FOX_W), lambda i: (i, 0)),
                   pl.BlockSpec((T, N_FFPAD), lambda i: (i, 0)),
                   pl.BlockSpec((T, FOX_W), lambda i: (i, 0))],
        out_shape=[jax.ShapeDtypeStruct((S, FOX_W), BF16), jax.ShapeDtypeStruct((S, FOX_W), BF16),
                   jax.ShapeDtypeStruct((S, N_FFPAD), F32), jax.ShapeDtypeStruct((S, FOX_W), F32)],
        scratch_shapes=[pltpu.VMEM((8, N_FFPAD), F32)],
        compiler_params=_cp(("arbitrary",), 40 << 20),
    )(proj, proj, pff, bfp, gq, gk, bd, ex, tril)


def _pair_blk(S, off=0):
    return pl.BlockSpec((S, 128), lambda p: (0, off + p), pipeline_mode=pl.Buffered(1))


def _pair_rows(S):
    return pl.BlockSpec((None, 8, S), lambda p: (p, 0, 0), pipeline_mode=pl.Buffered(1))


def _head_masks(S):
    return lax.broadcasted_iota(jnp.int32, (S, 128), 1) < HEAD_DIM


_EXP_ZERO = 104.0


def _spread_heads(x):
    src = lax.broadcasted_iota(jnp.int32, (128, 128), 0)
    return (_mm3(x, (src == 0).astype(BF16)), _mm3(x, (src == HEAD_DIM).astype(BF16)))


def _score_bounds(q, k):
    same_head = ((lax.broadcasted_iota(jnp.int32, (128, 128), 0) < HEAD_DIM)
                 == (lax.broadcasted_iota(jnp.int32, (128, 128), 1) < HEAD_DIM)).astype(BF16)

    def max_norm2(x):
        xf = x.astype(F32)
        return jnp.max(_mm2(xf * xf, same_head), axis=0, keepdims=True)

    z = jnp.sqrt(max_norm2(q) * max_norm2(k))
    return jnp.max(z[:, 0:1]) * 1.001 + 1e-3, jnp.max(z[:, 64:65]) * 1.001 + 1e-3


def _for_tiles_back(i, n, tiles_fn, fours=False):
    if fours:
        def four(t, c):
            tiles_fn([i - 1 - 4 * t, i - 2 - 4 * t, i - 3 - 4 * t, i - 4 - 4 * t])
            return c

        lax.fori_loop(0, lax.shift_right_logical(n, 2), four, 0)
        rest = i - (n & ~3)

        @pl.when((n & 2) != 0)
        def _():
            tiles_fn([rest - 1, rest - 2])
    else:
        def two(t, c):
            tiles_fn([i - 1 - 2 * t, i - 2 - 2 * t])
            return c

        lax.fori_loop(0, lax.shift_right_logical(n, 1), two, 0)

    @pl.when((n & 1) != 0)
    def _():
        tiles_fn([i - n])


def _fox_tiles_back(cr_ref, i, r0, zba, zbb):
    last = cr_ref[:, pl.ds(0, 128)]
    first = cr_ref[:, pl.ds(r0, 128)]
    alive_a = 2.0 * zba + first[0:1, 0:1] - last[2:3, :] > -_EXP_ZERO
    alive_b = 2.0 * zbb + first[1:2, 0:1] - last[3:4, :] > -_EXP_ZERO
    before = lax.broadcasted_iota(jnp.int32, (1, 128), 1) < i
    return jnp.sum((before & (alive_a | alive_b)).astype(jnp.int32))


def _fox_fwd(qs, kn, proj, cqb, crow4, ride=None):
    S = qs.shape[0]
    T = min(_T, S)
    nq = S // T
    n_pairs = FOX_W // 128

    def body(*refs):
        if ride is None:
            q_ref, k_ref, v_ref, cq_ref, cr_ref, o_ref, lse_ref = refs[:7]
            qa, qb, vta, vtb, cka, ckb, ma, mb, acca, accb = refs[7:]
        else:
            q_ref, k_ref, v_ref, cq_ref, cr_ref, wa_ref, wb_ref, o_ref, lse_ref, ga_ref, gb_ref = refs[:11]
            qa, qb, vta, vtb, cka, ckb, ma, mb, acca, accb = refs[11:21]
            xrefs = (wa_ref, wb_ref, ga_ref, gb_ref) + tuple(refs[21:])

            @pl.when(pl.program_id(0) == 0)
            def _():
                _start_exchange("gather", *xrefs)

        lane_s = _head_masks(S)
        q = q_ref[...]
        zq = jnp.zeros_like(q)
        qa[...] = jnp.where(lane_s, q, zq)
        qb[...] = jnp.where(lane_s, zq, q)
        cq = cq_ref[...]
        cka[...], ckb[...] = _spread_heads(cq)
        lse_ref[...] = jnp.zeros((8, S), F32)
        row_t = lax.broadcasted_iota(jnp.int32, (128, T), 0) < HEAD_DIM
        zba, zbb = _score_bounds(q, k_ref[...])

        def prep(c, carry):
            c0 = pl.multiple_of(c * T, T)
            vt = v_ref[pl.ds(c0, T), :].T
            vta[:, pl.ds(c0, T)] = jnp.where(row_t, vt, 1.0).astype(BF16)
            vtb[:, pl.ds(c0, T)] = jnp.where(row_t, 1.0, vt).astype(BF16)
            return carry

        lax.fori_loop(0, nq, prep, 0)
        causal = (lax.broadcasted_iota(jnp.int32, (T, T), 0) <= lax.broadcasted_iota(jnp.int32, (T, T), 1))

        heads = ((qa, vta, cka, ma, acca), (qb, vtb, ckb, mb, accb))

        def kv(js, r0, masked):
            cr = cr_ref[:, pl.ds(r0, T)]
            c0s = [pl.multiple_of(j * T, T) for j in js]
            ks = [k_ref[pl.ds(c0, T), :] for c0 in c0s]
            ss = []
            for h, (qr, _, ckr, _, _) in enumerate(heads):
                qh = qr[pl.ds(r0, T), :]
                row = []
                for k, c0 in zip(ks, c0s):
                    s = _dot_nt(k, qh) + cr[h:h + 1, :] - jnp.tile(ckr[pl.ds(c0, T), :], (1, T // 128))
                    row.append(jnp.where(causal, s, NEG) if masked else s)
                ss.append(row)
            ms = []
            for row, (_, _, _, mr, _) in zip(ss, heads):
                top = row[0]
                for s in row[1:]:
                    top = jnp.maximum(top, s)
                m_old = mr[0:1, :]
                ms.append((m_old, jnp.maximum(m_old, jnp.max(top, axis=0, keepdims=True))))
            ps = [[jnp.exp(s - m_new).astype(BF16) for s in row] for row, (_, m_new) in zip(ss, ms)]
            pvs = []
            for row, (_, vr, _, _, _) in zip(ps, heads):
                pv = _dot(vr[:, pl.ds(c0s[0], T)], row[0])
                for p, c0 in zip(row[1:], c0s[1:]):
                    pv = pv + _dot(vr[:, pl.ds(c0, T)], p)
                pvs.append(pv)
            for pv, (m_old, m_new), (_, _, _, mr, ar) in zip(pvs, ms, heads):
                ar[...] = jnp.exp(m_old - m_new) * ar[...] + pv
                mr[0:1, :] = m_new

        def qblk(i, carry):
            r0 = pl.multiple_of(i * T, T)
            ma[...] = jnp.full((8, T), NEG, F32)
            mb[...] = jnp.full((8, T), NEG, F32)
            acca[...] = jnp.zeros((128, T), F32)
            accb[...] = jnp.zeros((128, T), F32)
            kv([i], r0, True)
            done = _fox_tiles_back(cr_ref, i, r0, zba, zbb)
            _for_tiles_back(i, done, lambda js: kv(js, r0, False), fours=True)
            aa = acca[...]
            ab = accb[...]
            la = aa[64:65, :]
            lb = ab[0:1, :]
            o_ref[pl.ds(r0, T), :] = jnp.where(row_t, aa / la, ab / lb).T
            lse_ref[0:1, pl.ds(r0, T)] = ma[0:1, :] + jnp.log(la)
            lse_ref[1:2, pl.ds(r0, T)] = mb[0:1, :] + jnp.log(lb)
            lse_ref[2:3, pl.ds(r0, T)] = jnp.broadcast_to(done.astype(F32), (1, T))
            return carry

        lax.fori_loop(0, nq, qblk, 0)
        if ride is not None:
            @pl.when(pl.program_id(0) == n_pairs - 1)
            def _():
                _wait_exchange("gather", *xrefs)

    extra = () if ride is None else tuple(ride)
    return pl.pallas_call(
        body, name="fox_fwd" if ride is None else "fox_fwd_gather",
        grid=(n_pairs,),
        in_specs=[_pair_blk(S), _pair_blk(S), _pair_blk(S, OFF_FV // 128), _pair_blk(S), _pair_rows(S)]
        + [_ANY] * len(extra),
        out_specs=[_pair_blk(S), _pair_rows(S)] + [_ANY] * len(extra),
        out_shape=[jax.ShapeDtypeStruct((S, FOX_W), F32), jax.ShapeDtypeStruct((n_pairs, 8, S), F32)]
        + (_exchange_out_shapes("gather", *extra) if extra else []),
        scratch_shapes=[pltpu.VMEM((S, 128), BF16)] * 2 + [pltpu.VMEM((128, S), BF16)] * 2
        + [pltpu.VMEM((S, 128), F32)] * 2 + [pltpu.VMEM((8, T), F32)] * 2 + [pltpu.VMEM((128, T), F32)] * 2
        + (_EXCHANGE_SEMS if extra else []),
        compiler_params=_cp(("arbitrary",), _VMEM_BIG),
    )(qs, kn, proj, cqb, crow4, *extra)


def _softplus_parts(z):
    e = jnp.exp(-jnp.abs(z))
    return e, jnp.maximum(z, 0.0) + jnp.log(1.0 + e)


def _sb_fwd(proj, triu):
    S = proj.shape[0]
    T = triu.shape[0]
    nq = S // T

    def body(q_ref, k_ref, v_ref, tri_ref, o_ref, lt_ref, qa, qb, kb, vt, ra, rb, acca, accb):
        lane_s = _head_masks(S)
        q = (q_ref[...] * Q_SCALE).astype(BF16)
        zq = jnp.zeros_like(q)
        qa[...] = jnp.where(lane_s, q, zq)
        qb[...] = jnp.where(lane_s, zq, q)
        kb[...] = k_ref[...].astype(BF16)
        lt_ref[...] = jnp.zeros((8, S), F32)
        row_t = lax.broadcasted_iota(jnp.int32, (128, T), 0) < HEAD_DIM
        zba, zbb = _score_bounds(q, kb[...])

        def prep(c, carry):
            c0 = pl.multiple_of(c * T, T)
            vt[:, pl.ds(c0, T)] = v_ref[pl.ds(c0, T), :].T.astype(BF16)
            return carry

        lax.fori_loop(0, nq, prep, 0)
        strict = (lax.broadcasted_iota(jnp.int32, (T, T), 0) < lax.broadcasted_iota(jnp.int32, (T, T), 1))

        heads = ((qa, ra, acca), (qb, rb, accb))

        def kv(tiles, r0):
            tri = tri_ref[...]
            c0s = [pl.multiple_of(j * T, T) for j, _ in tiles]
            ks = [kb[pl.ds(c0, T), :] for c0 in c0s]
            qhs = [qr[pl.ds(r0, T), :] for qr, _, _ in heads]
            zs = [[_dot_nt(k, qh) for k in ks] for qh in qhs]
            lbs = [[jnp.where(strict, -_softplus_parts(z)[1], 0.0) if masked else -_softplus_parts(z)[1]
                    for z, (_, masked) in zip(row, tiles)] for row in zs]
            incs = [[_mm2(lb, tri, left=True) for lb in row] for row in lbs]
            avs = []
            for (_, r_ref, _), zrow, irow in zip(heads, zs, incs):
                r = r_ref[0:1, :]
                av = None
                for z, inc, c0, (_, masked) in zip(zrow, irow, c0s, tiles):
                    a = jnp.exp(z + inc + r)
                    if masked:
                        a = jnp.where(strict, a, 0.0)
                    term = _dot(vt[:, pl.ds(c0, T)], a.astype(BF16))
                    av = term if av is None else av + term
                    r = r + inc[0:1, :]
                avs.append((av, r))
            for (_, r_ref, acc_ref), (av, r) in zip(heads, avs):
                r_ref[0:1, :] = r
                acc_ref[...] = acc_ref[...] + av

        def qblk(i, carry):
            r0 = pl.multiple_of(i * T, T)
            ra[...] = jnp.zeros((8, T), F32)
            rb[...] = jnp.zeros((8, T), F32)
            acca[...] = jnp.zeros((128, T), F32)
            accb[...] = jnp.zeros((128, T), F32)

            @pl.when(i == 0)
            def _():
                kv([(i, True)], r0)

            @pl.when(i > 0)
            def _():
                kv([(i, True), (i - 1, False)], r0)

            def alive():
                return jnp.maximum(jnp.max(ra[0:1, :]) + zba, jnp.max(rb[0:1, :]) + zbb) > -_EXP_ZERO

            def cond(st):
                return (st[0] < i) & st[1]

            def step(st):
                kv([(i - 1 - st[0], False)], r0)
                return st[0] + 1, alive()

            done, _ = lax.while_loop(cond, step, (jnp.minimum(i, 1), alive()))
            o_ref[pl.ds(r0, T), :] = jnp.where(row_t, acca[...], accb[...]).T
            lt_ref[0:1, pl.ds(r0, T)] = ra[0:1, :]
            lt_ref[1:2, pl.ds(r0, T)] = rb[0:1, :]
            lt_ref[2:3, pl.ds(r0, T)] = jnp.broadcast_to(done.astype(F32), (1, T))
            return carry

        lax.fori_loop(0, nq, qblk, 0)

    return pl.pallas_call(
        body, name="sb_fwd",
        grid=(SB_W // 128,),
        in_specs=[_pair_blk(S, OFF_SQ // 128), _pair_blk(S, OFF_SK // 128), _pair_blk(S, OFF_SV // 128),
                  pl.BlockSpec((T, T), lambda p: (0, 0))],
        out_specs=[_pair_blk(S), _pair_rows(S)],
        out_shape=[jax.ShapeDtypeStruct((S, SB_W), F32), jax.ShapeDtypeStruct((SB_W // 128, 8, S), F32)],
        scratch_shapes=[pltpu.VMEM((S, 128), BF16)] * 3 + [pltpu.VMEM((128, S), BF16)]
        + [pltpu.VMEM((8, T), F32)] * 2 + [pltpu.VMEM((128, T), F32)] * 2,
        compiler_params=_cp(("arbitrary",), _VMEM_BIG),
    )(proj, proj, proj, triu)


def _pool_window_lanes(shape):
    lane = lax.broadcasted_iota(jnp.int32, shape, 1)
    return jnp.where(lane < 64, 2, jnp.where(lane < 128, 4, jnp.where(lane < 192, 8, 16)))


def _pool_fwd(proj):
    S = proj.shape[0]

    def body(x_ref, o_ref):
        x = x_ref[...]
        t = lax.broadcasted_iota(jnp.int32, x.shape, 0)
        lane = lax.broadcasted_iota(jnp.int32, x.shape, 1)

        def back(a, k):
            return jnp.where(t >= k, pltpu.roll(a, k, 0), 0.0)

        s1 = x + back(x, 1)
        s2 = s1 + back(s1, 2)
        s4 = s2 + back(s2, 4)
        s8 = s4 + back(s4, 8)
        win = jnp.where(lane < 64, s1, jnp.where(lane < 128, s2, jnp.where(lane < 192, s4, s8)))
        cnt = jnp.minimum(t + 1, _pool_window_lanes(x.shape)).astype(F32)
        o_ref[...] = win / cnt - x

    return pl.pallas_call(
        body, name="pool_fwd",
        grid=(1,),
        in_specs=[pl.BlockSpec((S, POOL_W), lambda i: (0, OFF_PX // POOL_W))],
        out_specs=pl.BlockSpec((S, POOL_W), lambda i: (0, 0)),
        out_shape=jax.ShapeDtypeStruct((S, POOL_W), F32),
        compiler_params=_cp(("arbitrary",), _VMEM_BIG),
    )(proj)


def _silu(g):
    return g * _sigmoid(g)


def _mix_out(fo, so, pooled, proj, wbd, scale, wout, x):
    S, D = x.shape
    tm = min(256, S)

    def body(fo_ref, fg_ref, so_ref, sg_ref, pl_ref, pg_ref, wbd_ref, sc_ref, w_ref, x_ref, y_ref, mxt_ref, mx_ref):
        parts = ((0, fo_ref[...] * _silu(fg_ref[...])),
                 (FOX_W, (_dot(pl_ref[...].astype(BF16), wbd_ref[...]) * sc_ref[...]) * _silu(pg_ref[...])),
                 (FOX_W + POOL_W, so_ref[...] * _silu(sg_ref[...])))
        for off, part in parts:
            w = part.shape[1]
            mx_ref[:, off:off + w] = part.astype(BF16)
            mxt_ref[off:off + w, :] = part.T.astype(BF16)
        y_ref[...] = x_ref[...] + _dot(mx_ref[...], w_ref[...])

    return pl.pallas_call(
        body, name="mix_out",
        grid=(S // tm,),
        in_specs=[pl.BlockSpec((tm, FOX_W), lambda i: (i, 0)),
                  pl.BlockSpec((tm, FOX_W), lambda i: (i, OFF_FG // FOX_W)),
                  pl.BlockSpec((tm, SB_W), lambda i: (i, 0)),
                  pl.BlockSpec((tm, SB_W), lambda i: (i, OFF_SG // SB_W)),
                  pl.BlockSpec((tm, POOL_W), lambda i: (i, 0)),
                  pl.BlockSpec((tm, POOL_W), lambda i: (i, OFF_PG // POOL_W)),
                  pl.BlockSpec((POOL_W, POOL_W), lambda i: (0, 0)),
                  pl.BlockSpec((1, POOL_W), lambda i: (0, 0)),
                  pl.BlockSpec((D_MIX, D), lambda i: (0, 0)),
                  pl.BlockSpec((tm, D), lambda i: (i, 0))],
        out_specs=[pl.BlockSpec((tm, D), lambda i: (i, 0)), pl.BlockSpec((D_MIX, tm), lambda i: (0, i))],
        out_shape=[jax.ShapeDtypeStruct((S, D), F32), jax.ShapeDtypeStruct((D_MIX, S), BF16)],
        scratch_shapes=[pltpu.VMEM((tm, D_MIX), BF16)],
        compiler_params=_cp(("parallel",), 40 << 20),
    )(fo, proj, so, proj, pooled, proj, wbd, scale, wout, x)


def _loss_head(y, target):
    S, D = y.shape
    tm = min(_TM, S)

    def body(y_ref, t_ref, dy_ref, ls_ref):
        @pl.when(pl.program_id(0) == 0)
        def _():
            ls_ref[...] = jnp.zeros_like(ls_ref)

        e = y_ref[...] - t_ref[...]
        dy_ref[...] = e * (1.0 / D)
        ls_ref[...] = ls_ref[...] + jnp.sum(e * e) * (0.5 / D)

    dy, ls = pl.pallas_call(
        body, name="loss_head",
        grid=(S // tm,),
        in_specs=[pl.BlockSpec((tm, D), lambda i: (i, 0)), pl.BlockSpec((tm, D), lambda i: (i, 0))],
        out_specs=[pl.BlockSpec((tm, D), lambda i: (i, 0)), pl.BlockSpec((8, 128), lambda i: (0, 0))],
        out_shape=[jax.ShapeDtypeStruct((S, D), F32), jax.ShapeDtypeStruct((8, 128), F32)],
        compiler_params=_cp(("arbitrary",), 40 << 20),
    )(y, target)
    return dy, ls[0, 0]


def _dsilu(g):
    s = _sigmoid(g)
    return s * (1.0 + g * (1.0 - s))


def _gate_bwd(dy, wout, fo, so, pooled, proj, wbd, scale):
    S, D = dy.shape
    tm = min(256, S)

    def body(dy_ref, w_ref, fo_ref, fg_ref, so_ref, sg_ref, pl_ref, pg_ref, wbd_ref, sc_ref,
             dfo_ref, dfg_ref, dso_ref, dsg_ref, dpg_ref, dpl_ref, dsc_ref, dwbd_ref):
        @pl.when(pl.program_id(0) == 0)
        def _():
            dsc_ref[...] = jnp.zeros_like(dsc_ref)
            dwbd_ref[...] = jnp.zeros_like(dwbd_ref)

        dm = _dot_nt(dy_ref[...].astype(BF16), w_ref[...])
        dmf = dm[:, 0:FOX_W]
        dmp = dm[:, FOX_W:FOX_W + POOL_W]
        dms = dm[:, FOX_W + POOL_W:D_MIX]
        fg = fg_ref[...]
        dfo_ref[...] = dmf * _silu(fg)
        dfg_ref[...] = (dmf * fo_ref[...] * _dsilu(fg)).astype(BF16)
        sg = sg_ref[...]
        dso_ref[...] = dms * _silu(sg)
        dsg_ref[...] = (dms * so_ref[...] * _dsilu(sg)).astype(BF16)
        pg = pg_ref[...]
        plb = pl_ref[...].astype(BF16)
        yw = _dot(plb, wbd_ref[...])
        sc = sc_ref[...]
        dpg_ref[...] = (dmp * (yw * sc) * _dsilu(pg)).astype(BF16)
        dys = dmp * _silu(pg)
        dsc_ref[...] = dsc_ref[...] + jnp.sum(dys * yw, axis=0, keepdims=True)
        dyw = (dys * sc).astype(BF16)
        dpl_ref[...] = _dot_nt(dyw, wbd_ref[...])
        dwbd_ref[...] = dwbd_ref[...] + _dot_tn(plb, dyw)

    return pl.pallas_call(
        body, name="gate_bwd",
        grid=(S // tm,),
        in_specs=[pl.BlockSpec((tm, D), lambda i: (i, 0)),
                  pl.BlockSpec((D_MIX, D), lambda i: (0, 0)),
                  pl.BlockSpec((tm, FOX_W), lambda i: (i, 0)),
                  pl.BlockSpec((tm, FOX_W), lambda i: (i, OFF_FG // FOX_W)),
                  pl.BlockSpec((tm, SB_W), lambda i: (i, 0)),
                  pl.BlockSpec((tm, SB_W), lambda i: (i, OFF_SG // SB_W)),
                  pl.BlockSpec((tm, POOL_W), lambda i: (i, 0)),
                  pl.BlockSpec((tm, POOL_W), lambda i: (i, OFF_PG // POOL_W)),
                  pl.BlockSpec((POOL_W, POOL_W), lambda i: (0, 0)),
                  pl.BlockSpec((1, POOL_W), lambda i: (0, 0))],
        out_specs=[pl.BlockSpec((tm, FOX_W), lambda i: (i, 0)),
                   pl.BlockSpec((tm, FOX_W), lambda i: (i, 0)),
                   pl.BlockSpec((tm, SB_W), lambda i: (i, 0)),
                   pl.BlockSpec((tm, SB_W), lambda i: (i, 0)),
                   pl.BlockSpec((tm, POOL_W), lambda i: (i, 0)),
                   pl.BlockSpec((tm, POOL_W), lambda i: (i, 0)),
                   pl.BlockSpec((1, POOL_W), lambda i: (0, 0)),
                   pl.BlockSpec((POOL_W, POOL_W), lambda i: (0, 0))],
        out_shape=[jax.ShapeDtypeStruct((S, FOX_W), F32), jax.ShapeDtypeStruct((S, FOX_W), BF16),
                   jax.ShapeDtypeStruct((S, SB_W), F32), jax.ShapeDtypeStruct((S, SB_W), BF16),
                   jax.ShapeDtypeStruct((S, POOL_W), BF16), jax.ShapeDtypeStruct((S, POOL_W), F32),
                   jax.ShapeDtypeStruct((1, POOL_W), F32), jax.ShapeDtypeStruct((POOL_W, POOL_W), F32)],
        compiler_params=_cp(("arbitrary",), 40 << 20),
    )(dy, wout, fo, proj, so, proj, pooled, proj, wbd, scale)


def _matmul_acc(at, b, name):
    M, S = at.shape
    N = b.shape[1]
    tk = min(_TK_DW, S)
    tn = min(512, N)
    nk = S // tk

    def body(a_ref, b_ref, o_ref, acc):
        k = pl.program_id(1)

        @pl.when(k == 0)
        def _():
            acc[...] = jnp.zeros_like(acc)

        acc[...] = acc[...] + _dot(a_ref[...], b_ref[...].astype(BF16))

        @pl.when(k == nk - 1)
        def _():
            o_ref[...] = acc[...].astype(BF16)

    return pl.pallas_call(
        body, name=name,
        grid=(N // tn, nk),
        in_specs=[pl.BlockSpec((M, tk), lambda j, k: (0, k)), pl.BlockSpec((tk, tn), lambda j, k: (k, j))],
        out_specs=pl.BlockSpec((M, tn), lambda j, k: (0, j)),
        out_shape=jax.ShapeDtypeStruct((M, N), BF16),
        scratch_shapes=[pltpu.VMEM((M, tn), F32)],
        compiler_params=_cp(("parallel", "arbitrary"), 40 << 20),
    )(at, b)


def _pool_bwd(dpooled):
    S = dpooled.shape[0]

    def body(d_ref, o_ref):
        d = d_ref[...]
        t = lax.broadcasted_iota(jnp.int32, d.shape, 0)
        lane = lax.broadcasted_iota(jnp.int32, d.shape, 1)
        cnt = jnp.minimum(t + 1, _pool_window_lanes(d.shape)).astype(F32)
        u = d / cnt

        def fwd(a, k):
            return jnp.where(t < S - k, pltpu.roll(a, S - k, 0), 0.0)

        s1 = u + fwd(u, 1)
        s2 = s1 + fwd(s1, 2)
        s4 = s2 + fwd(s2, 4)
        s8 = s4 + fwd(s4, 8)
        win = jnp.where(lane < 64, s1, jnp.where(lane < 128, s2, jnp.where(lane < 192, s4, s8)))
        o_ref[...] = (win - d).astype(BF16)

    return pl.pallas_call(
        body, name="pool_bwd",
        grid=(1,),
        in_specs=[pl.BlockSpec((S, POOL_W), lambda i: (0, 0))],
        out_specs=pl.BlockSpec((S, POOL_W), lambda i: (0, 0)),
        out_shape=jax.ShapeDtypeStruct((S, POOL_W), BF16),
        compiler_params=_cp(("arbitrary",), _VMEM_BIG),
    )(dpooled)


def _fox_bwd(qs, kn, proj, dfo, fo, lse, cqb, crow4, ride=None):
    S = qs.shape[0]
    T = min(_T, S)
    nq = S // T
    n_pairs = FOX_W // 128

    def body(*refs):
        if ride is None:
            q_ref, k_ref, v_ref, do_ref, o_ref, lse_ref, cq_ref, cr_ref = refs[:8]
            dq_ref, dk_ref, dv_ref, dck_ref, dcq_ref = refs[8:13]
            scr = refs[13:]
        else:
            q_ref, k_ref, v_ref, do_ref, o_ref, lse_ref, cq_ref, cr_ref, pa_ref, pb_ref = refs[:10]
            dq_ref, dk_ref, dv_ref, dck_ref, dcq_ref, ra_ref, rb_ref = refs[10:17]
            scr = refs[17:32]
            xrefs = (pa_ref, pb_ref, ra_ref, rb_ref) + tuple(refs[32:])

            @pl.when(pl.program_id(0) == 0)
            def _():
                _start_exchange("scatter", *xrefs)

        qa, qb, kta, ktb, vb, doa, dob, cka, ckb, dcka, dckb, dva, dqt, dcqa, dcqb = scr
        lane_s = _head_masks(S)
        q = q_ref[...]
        zq = jnp.zeros_like(q)
        qa[...] = jnp.where(lane_s, q, zq)
        qb[...] = jnp.where(lane_s, zq, q)
        vb[...] = v_ref[...].astype(BF16)
        do = do_ref[...].astype(BF16)
        doa[...] = jnp.where(lane_s, do, zq)
        dob[...] = jnp.where(lane_s, zq, do)
        cq = cq_ref[...]
        cka[...], ckb[...] = _spread_heads(cq)
        zs = jnp.zeros((S, 128), F32)
        dk_ref[...] = zs
        dva[...] = zs
        dcka[...] = zs
        dckb[...] = zs
        dcq_ref[...] = jnp.zeros((8, S), F32)
        row_t = lax.broadcasted_iota(jnp.int32, (128, T), 0) < HEAD_DIM

        def prep(c, carry):
            c0 = pl.multiple_of(c * T, T)
            kt = k_ref[pl.ds(c0, T), :].astype(F32).T
            kta[:, pl.ds(c0, T)] = jnp.where(row_t, kt, 0.0).astype(BF16)
            ktb[:, pl.ds(c0, T)] = jnp.where(row_t, 0.0, kt).astype(BF16)
            return carry

        lax.fori_loop(0, nq, prep, 0)
        causal = (lax.broadcasted_iota(jnp.int32, (T, T), 0) <= lax.broadcasted_iota(jnp.int32, (T, T), 1))

        heads = ((qa, kta, doa, cka, dcka, dcqa), (qb, ktb, dob, ckb, dckb, dcqb))

        def kv(js, r0, lss, dls, masked):
            cr = cr_ref[:, pl.ds(r0, T)]
            c0s = [pl.multiple_of(j * T, T) for j in js]
            ks = [k_ref[pl.ds(c0, T), :] for c0 in c0s]
            vs = [vb[pl.ds(c0, T), :] for c0 in c0s]
            qhs = [hd[0][pl.ds(r0, T), :] for hd in heads]
            dohs = [hd[2][pl.ds(r0, T), :] for hd in heads]
            ss = []
            for h, hd in enumerate(heads):
                row = []
                for k, c0 in zip(ks, c0s):
                    s = _dot_nt(k, qhs[h]) + cr[h:h + 1, :] - jnp.tile(hd[3][pl.ds(c0, T), :], (1, T // 128))
                    row.append(jnp.where(causal, s, NEG) if masked else s)
                ss.append(row)
            ps = [[jnp.exp(s - lss[h]) for s in row] for h, row in enumerate(ss)]
            dps = [[_dot_nt(v, dohs[h]) for v in vs] for h in range(2)]
            dss = [[p * (dp - dls[h]) for p, dp in zip(ps[h], dps[h])] for h in range(2)]
            pbs = [[p.astype(BF16) for p in row] for row in ps]
            dsbs = [[ds.astype(BF16) for ds in row] for row in dss]
            for t, c0 in enumerate(c0s):
                dva[pl.ds(c0, T), :] = dva[pl.ds(c0, T), :] + (_dot(pbs[0][t], dohs[0]) + _dot(pbs[1][t], dohs[1]))
                dk_ref[pl.ds(c0, T), :] = dk_ref[pl.ds(c0, T), :] + (_dot(dsbs[0][t], qhs[0]) + _dot(dsbs[1][t], qhs[1]))
            dq = None
            for h, hd in enumerate(heads):
                for t, c0 in enumerate(c0s):
                    term = _dot(hd[1][:, pl.ds(c0, T)], dsbs[h][t])
                    dq = term if dq is None else dq + term
            dqt[...] = dqt[...] + dq
            for h, hd in enumerate(heads):
                col = jnp.sum(dss[h][0], axis=0, keepdims=True)
                for ds in dss[h][1:]:
                    col = col + jnp.sum(ds, axis=0, keepdims=True)
                hd[5][0:1, :] = hd[5][0:1, :] + col
                for ds, c0 in zip(dss[h], c0s):
                    fold = ds[:, 0:128]
                    for u in range(1, T // 128):
                        fold = fold + ds[:, 128 * u:128 * (u + 1)]
                    hd[4][pl.ds(c0, T), :] = hd[4][pl.ds(c0, T), :] - fold

        def qblk(i, carry):
            r0 = pl.multiple_of(i * T, T)
            dt = (do_ref[pl.ds(r0, T), :] * o_ref[pl.ds(r0, T), :]).T
            dla = jnp.sum(jnp.where(row_t, dt, 0.0), axis=0, keepdims=True)
            dlb = jnp.sum(jnp.where(row_t, 0.0, dt), axis=0, keepdims=True)
            ls = lse_ref[:, pl.ds(r0, T)]
            lss = (ls[0:1, :], ls[1:2, :])
            back = jnp.max(ls[2:3, :]).astype(jnp.int32)
            dqt[...] = jnp.zeros((128, T), F32)
            dcqa[...] = jnp.zeros((8, T), F32)
            dcqb[...] = jnp.zeros((8, T), F32)
            kv([i], r0, lss, (dla, dlb), True)
            _for_tiles_back(i, back, lambda js: kv(js, r0, lss, (dla, dlb), False))
            dq_ref[pl.ds(r0, T), :] = dqt[...].T
            dcq_ref[0:1, pl.ds(r0, T)] = dcqa[0:1, :]
            dcq_ref[1:2, pl.ds(r0, T)] = dcqb[0:1, :]
            return carry

        lax.fori_loop(0, nq, qblk, 0)
        dv_ref[...] = dva[...].astype(BF16)
        dck_ref[...] = jnp.where(lane_s, jnp.sum(dcka[...], axis=1, keepdims=True),
                                 jnp.sum(dckb[...], axis=1, keepdims=True))
        if ride is not None:
            @pl.when(pl.program_id(0) == n_pairs - 1)
            def _():
                _wait_exchange("scatter", *xrefs)

    extra = () if ride is None else tuple(ride)
    return pl.pallas_call(
        body, name="fox_bwd" if ride is None else "fox_bwd_exchange",
        grid=(n_pairs,),
        in_specs=[_pair_blk(S), _pair_blk(S), _pair_blk(S, OFF_FV // 128), _pair_blk(S), _pair_blk(S),
                  _pair_rows(S), _pair_blk(S), _pair_rows(S)] + [_ANY] * len(extra),
        out_specs=[_pair_blk(S), _pair_blk(S), _pair_blk(S), _pair_blk(S), _pair_rows(S)] + [_ANY] * len(extra),
        out_shape=[jax.ShapeDtypeStruct((S, FOX_W), F32), jax.ShapeDtypeStruct((S, FOX_W), F32),
                   jax.ShapeDtypeStruct((S, FOX_W), BF16), jax.ShapeDtypeStruct((S, FOX_W), F32),
                   jax.ShapeDtypeStruct((n_pairs, 8, S), F32)]
        + (_exchange_out_shapes("scatter", *extra) if extra else []),
        scratch_shapes=[pltpu.VMEM((S, 128), BF16)] * 2 + [pltpu.VMEM((128, S), BF16)] * 2
        + [pltpu.VMEM((S, 128), BF16)] * 3 + [pltpu.VMEM((S, 128), F32)] * 5
        + [pltpu.VMEM((128, T), F32)] + [pltpu.VMEM((8, T), F32)] * 2
        + (_EXCHANGE_SEMS if extra else []),
        compiler_params=_cp(("arbitrary",), _VMEM_BIG),
    )(qs, kn, proj, dfo, fo, lse, cqb, crow4, *extra)


def _sb_bwd(proj, dso, ltot, tril):
    S = proj.shape[0]
    T = tril.shape[0]
    nq = S // T

    def body(q_ref, k_ref, v_ref, do_ref, lt_ref, tri_ref, dq_ref, dk_ref, dv_ref,
             qa, qb, k2, kta, ktb, vb, doa, dob, dka, dva, dqt, ra, rb, ga, gb):
        lane_s = _head_masks(S)
        q = (q_ref[...] * Q_SCALE).astype(BF16)
        zq = jnp.zeros_like(q)
        qa[...] = jnp.where(lane_s, q, zq)
        qb[...] = jnp.where(lane_s, zq, q)
        k2[...] = k_ref[...].astype(BF16)
        vb[...] = v_ref[...].astype(BF16)
        do = do_ref[...].astype(BF16)
        doa[...] = jnp.where(lane_s, do, zq)
        dob[...] = jnp.where(lane_s, zq, do)
        dka[...] = jnp.zeros((S, 128), F32)
        dva[...] = jnp.zeros((S, 128), F32)
        row_t = lax.broadcasted_iota(jnp.int32, (128, T), 0) < HEAD_DIM

        def prep(c, carry):
            c0 = pl.multiple_of(c * T, T)
            kt = k_ref[pl.ds(c0, T), :].T
            kta[:, pl.ds(c0, T)] = jnp.where(row_t, kt, 0.0).astype(BF16)
            ktb[:, pl.ds(c0, T)] = jnp.where(row_t, 0.0, kt).astype(BF16)
            return carry

        lax.fori_loop(0, nq, prep, 0)
        strict = (lax.broadcasted_iota(jnp.int32, (T, T), 0) < lax.broadcasted_iota(jnp.int32, (T, T), 1))

        heads = ((qa, kta, doa, ra, ga), (qb, ktb, dob, rb, gb))

        def kv(tiles, r0, lts):
            tri = tri_ref[...]
            c0s = [pl.multiple_of(j * T, T) for j, _ in tiles]
            ks = [k2[pl.ds(c0, T), :] for c0 in c0s]
            vs = [vb[pl.ds(c0, T), :] for c0 in c0s]
            qhs = [hd[0][pl.ds(r0, T), :] for hd in heads]
            dohs = [hd[2][pl.ds(r0, T), :] for hd in heads]
            zs = [[_dot_nt(k, qh) for k in ks] for qh in qhs]
            das = [[_dot_nt(v, doh) for v in vs] for doh in dohs]
            es, lbs = [], []
            for row in zs:
                erow, lrow = [], []
                for z, (_, masked) in zip(row, tiles):
                    e, sp = _softplus_parts(z)
                    erow.append(e)
                    lrow.append(jnp.where(strict, -sp, 0.0) if masked else -sp)
                es.append(erow)
                lbs.append(lrow)
            pres = [[_mm2(lb, tri, left=True) for lb in row] for row in lbs]
            aas, r_ends = [], []
            for hd, zrow, lrow, prow, lt in zip(heads, zs, lbs, pres, lts):
                r = hd[3][0:1, :]
                arow = []
                for z, lb, pre, (_, masked) in zip(zrow, lrow, prow, tiles):
                    a = jnp.exp(z + lb + ((lt - r) - pre))
                    arow.append(jnp.where(strict, a, 0.0) if masked else a)
                    r = r + pre[T - 1:T, :]
                aas.append(arow)
                r_ends.append(r)
            gs = [[a * da for a, da in zip(arow, drow)] for arow, drow in zip(aas, das)]
            gpres = [[_mm2(g, tri, left=True) for g in row] for row in gs]
            dzbs, g_ends = [], []
            for hd, zrow, erow, grow, gprow in zip(heads, zs, es, gs, gpres):
                gc = hd[4][0:1, :]
                drow = []
                for z, e, g, gpre, (_, masked) in zip(zrow, erow, grow, gprow, tiles):
                    inv = 1.0 / (1.0 + e)
                    pos = z >= 0.0
                    sig = jnp.where(pos, 1.0, e) * inv
                    oms = jnp.where(pos, e, 1.0) * inv
                    dz = g * oms - sig * (gc + (gpre - g))
                    if masked:
                        dz = jnp.where(strict, dz, 0.0)
                    drow.append(dz.astype(BF16))
                    gc = gc + gpre[T - 1:T, :]
                dzbs.append(drow)
                g_ends.append(gc)
            dq = None
            for h, hd in enumerate(heads):
                for t, c0 in enumerate(c0s):
                    term = _dot(hd[1][:, pl.ds(c0, T)], dzbs[h][t])
                    dq = term if dq is None else dq + term
            dqt[...] = dqt[...] + dq
            for t, c0 in enumerate(c0s):
                dka[pl.ds(c0, T), :] = dka[pl.ds(c0, T), :] + (_dot(dzbs[0][t], qhs[0]) + _dot(dzbs[1][t], qhs[1]))
                dva[pl.ds(c0, T), :] = dva[pl.ds(c0, T), :] + (_dot(aas[0][t].astype(BF16), dohs[0])
                                                               + _dot(aas[1][t].astype(BF16), dohs[1]))
            for hd, r, gc in zip(heads, r_ends, g_ends):
                hd[3][0:1, :] = r
                hd[4][0:1, :] = gc

        def qblk(i, carry):
            r0 = pl.multiple_of(i * T, T)
            lt = lt_ref[:, pl.ds(r0, T)]
            lts = (lt[0:1, :], lt[1:2, :])
            back = jnp.max(lt[2:3, :]).astype(jnp.int32)
            zt = jnp.zeros((8, T), F32)
            dqt[...] = jnp.zeros((128, T), F32)
            ra[...] = zt
            rb[...] = zt
            ga[...] = zt
            gb[...] = zt

            def inner(j, c):
                kv([(j, False)], r0, lts)
                return c

            @pl.when(back == 0)
            def _():
                kv([(i, True)], r0, lts)

            @pl.when(back > 0)
            def _():
                lax.fori_loop(i - back, i - 1, inner, 0)
                kv([(i - 1, False), (i, True)], r0, lts)
            dq_ref[pl.ds(r0, T), :] = (dqt[...] * Q_SCALE).T.astype(BF16)
            return carry

        lax.fori_loop(0, nq, qblk, 0)
        dk_ref[...] = dka[...].astype(BF16)
        dv_ref[...] = dva[...].astype(BF16)

    return pl.pallas_call(
        body, name="sb_bwd",
        grid=(SB_W // 128,),
        in_specs=[_pair_blk(S, OFF_SQ // 128), _pair_blk(S, OFF_SK // 128), _pair_blk(S, OFF_SV // 128),
                  _pair_blk(S), _pair_rows(S), pl.BlockSpec((T, T), lambda p: (0, 0))],
        out_specs=[_pair_blk(S), _pair_blk(S), _pair_blk(S)],
        out_shape=[jax.ShapeDtypeStruct((S, SB_W), BF16)] * 3,
        scratch_shapes=([pltpu.VMEM((S, 128), BF16)] * 3 + [pltpu.VMEM((128, S), BF16)] * 2
                        + [pltpu.VMEM((S, 128), BF16)] * 3 + [pltpu.VMEM((S, 128), F32)] * 2
                        + [pltpu.VMEM((128, T), F32)] + [pltpu.VMEM((8, T), F32)] * 4),
        compiler_params=_cp(("arbitrary",), _VMEM_BIG),
    )(proj, proj, proj, dso, ltot, tril)


def _head_norm_bwd(x, g, dy, bd):
    ss = _mm2(x * x, bd)
    r = lax.rsqrt(ss * (1.0 / HEAD_DIM) + EPS)
    xr = x * r
    gdy = g * dy
    m = _mm2(xr * gdy, bd) * (1.0 / HEAD_DIM)
    return r * (gdy - xr * m), dy * xr


def _qk_bwd(dqs, dkn, proj, pff, bfp, gq, gk, bd, dccol, triu):
    S = proj.shape[0]
    T = triu.shape[0]
    n = S // T
    rev = lambda col: (lambda i: (n - 1 - i, col))

    def body(dq_ref, dk_ref, q_ref, k_ref, ff_ref, b_ref, gq_ref, gk_ref, bd_ref, dc_ref, tri_ref,
             dfq_ref, dfk_ref, dff_ref, dgq_ref, dgk_ref, dbf_ref, carry):
        @pl.when(pl.program_id(0) == 0)
        def _():
            carry[...] = jnp.zeros_like(carry)
            dgq_ref[...] = jnp.zeros_like(dgq_ref)
            dgk_ref[...] = jnp.zeros_like(dgk_ref)
            dbf_ref[...] = jnp.zeros_like(dbf_ref)

        bdv = bd_ref[...]
        dxq, gq_rows = _head_norm_bwd(q_ref[...], gq_ref[...], dq_ref[...] * Q_SCALE, bdv)
        dfq_ref[...] = dxq.astype(BF16)
        dgq_ref[...] = dgq_ref[...] + jnp.sum(gq_rows, axis=0, keepdims=True)
        dxk, gk_rows = _head_norm_bwd(k_ref[...], gk_ref[...], dk_ref[...], bdv)
        dfk_ref[...] = dxk.astype(BF16)
        dgk_ref[...] = dgk_ref[...] + jnp.sum(gk_rows, axis=0, keepdims=True)
        dlf = _mm3(dc_ref[...], tri_ref[...], left=True) + carry[0:1, :]
        carry[0:1, :] = dlf[0:1, :]
        u = ff_ref[...] + b_ref[...]
        lane = lax.broadcasted_iota(jnp.int32, u.shape, 1)
        dff = jnp.where(lane < N_FF, dlf * _sigmoid(-u), 0.0)
        dff_ref[...] = dff.astype(BF16)
        dbf_ref[...] = dbf_ref[...] + jnp.sum(dff, axis=0, keepdims=True)

    return pl.pallas_call(
        body, name="qk_bwd",
        grid=(n,),
        in_specs=[pl.BlockSpec((T, FOX_W), rev(0)), pl.BlockSpec((T, FOX_W), rev(0)),
                  pl.BlockSpec((T, FOX_W), rev(OFF_FQ // FOX_W)), pl.BlockSpec((T, FOX_W), rev(OFF_FK // FOX_W)),
                  pl.BlockSpec((T, N_FFPAD), rev(0)),
                  pl.BlockSpec((1, N_FFPAD), lambda i: (0, 0)),
                  pl.BlockSpec((1, FOX_W), lambda i: (0, 0)), pl.BlockSpec((1, FOX_W), lambda i: (0, 0)),
                  pl.BlockSpec((FOX_W, FOX_W), lambda i: (0, 0)),
                  pl.BlockSpec((T, N_FFPAD), rev(0)),
                  pl.BlockSpec((T, T), lambda i: (0, 0))],
        out_specs=[pl.BlockSpec((T, FOX_W), rev(0)), pl.BlockSpec((T, FOX_W), rev(0)),
                   pl.BlockSpec((T, N_FFPAD), rev(0)),
                   pl.BlockSpec((1, FOX_W), lambda i: (0, 0)), pl.BlockSpec((1, FOX_W), lambda i: (0, 0)),
                   pl.BlockSpec((1, N_FFPAD), lambda i: (0, 0))],
        out_shape=[jax.ShapeDtypeStruct((S, FOX_W), BF16), jax.ShapeDtypeStruct((S, FOX_W), BF16),
                   jax.ShapeDtypeStruct((S, N_FFPAD), BF16),
                   jax.ShapeDtypeStruct((1, FOX_W), F32), jax.ShapeDtypeStruct((1, FOX_W), F32),
                   jax.ShapeDtypeStruct((1, N_FFPAD), F32)],
        scratch_shapes=[pltpu.VMEM((8, N_FFPAD), F32)],
        compiler_params=_cp(("arbitrary",), 40 << 20),
    )(dqs, dkn, proj, proj, pff, bfp, gq, gk, bd, dccol, triu)


def _dproj_layout(pieces):
    offs, o = [], 0
    for p in pieces:
        offs.append(o)
        o += p.shape[1]
    assert o == N_MAIN
    return offs


def _inproj_bwd_dx(pieces, dff, wm, wff, x, g, dy, ride=None):
    S, D = x.shape
    tm = min(_TM_DX, S)
    steps = S // tm
    offs = _dproj_layout(pieces)
    n = len(pieces)

    def body(*refs):
        p_refs = refs[:n]
        if ride is None:
            dff_ref, w_ref, wff_ref, x_ref, g_ref, dy_ref, dx_ref, dg_ref = refs[n:]
        else:
            dff_ref, w_ref, wff_ref, x_ref, g_ref, dy_ref, pa_ref, pb_ref = refs[n:n + 8]
            dx_ref, dg_ref, ra_ref, rb_ref = refs[n + 8:n + 12]
            xrefs = (pa_ref, pb_ref, ra_ref, rb_ref) + tuple(refs[n + 12:])

        @pl.when(pl.program_id(0) == 0)
        def _():
            dg_ref[...] = jnp.zeros_like(dg_ref)
            if ride is not None:
                _start_exchange("scatter", *xrefs)

        dh = _dot_nt(dff_ref[...], wff_ref[...])
        for p_ref, off in zip(p_refs, offs):
            dh = dh + _dot_nt(p_ref[...], w_ref[:, off:off + p_ref.shape[1]])
        xv = x_ref[...]
        r = _rms_rows(xv)
        xr = xv * r
        dg_ref[...] = dg_ref[...] + jnp.sum(dh * xr, axis=0, keepdims=True)
        gdh = g_ref[...] * dh
        m = jnp.mean(gdh * xr, axis=-1, keepdims=True)
        dx_ref[...] = dy_ref[...] + r * (gdh - xr * m)
        if ride is not None:
            @pl.when(pl.program_id(0) == steps - 1)
            def _():
                _wait_exchange("scatter", *xrefs)

    extra = () if ride is None else tuple(ride)
    return pl.pallas_call(
        body, name="inproj_bwd_dx" if ride is None else "inproj_bwd_dx_exchange",
        grid=(steps,),
        in_specs=[pl.BlockSpec((tm, p.shape[1]), lambda i: (i, 0)) for p in pieces]
        + [pl.BlockSpec((tm, N_FFPAD), lambda i: (i, 0)),
                  pl.BlockSpec((D, N_MAIN), lambda i: (0, 0)),
                  pl.BlockSpec((D, N_FFPAD), lambda i: (0, 0)),
                  pl.BlockSpec((tm, D), lambda i: (i, 0)),
                  pl.BlockSpec((1, D), lambda i: (0, 0)),
                  pl.BlockSpec((tm, D), lambda i: (i, 0))] + [_ANY] * len(extra),
        out_specs=[pl.BlockSpec((tm, D), lambda i: (i, 0)), pl.BlockSpec((1, D), lambda i: (0, 0))] + [_ANY] * len(extra),
        out_shape=[jax.ShapeDtypeStruct((S, D), F32), jax.ShapeDtypeStruct((1, D), F32)]
        + (_exchange_out_shapes("scatter", *extra) if extra else []),
        scratch_shapes=_EXCHANGE_SEMS if extra else [],
        compiler_params=_cp(("arbitrary",), 48 << 20),
    )(*pieces, dff, wm, wff, x, g, dy, *extra)


def _inproj_bwd_dw(ht, pieces, dff):
    D, S = ht.shape
    tk = min(_TK_DW, S)
    nk = S // tk
    offs = _dproj_layout(pieces)
    n = len(pieces)

    def body(*refs):
        ht_ref, p_refs, dff_ref = refs[0], refs[1:1 + n], refs[1 + n]
        dw_ref, dwff_ref, acc, accff = refs[2 + n:]
        k = pl.program_id(0)

        @pl.when(k == 0)
        def _():
            acc[...] = jnp.zeros_like(acc)
            accff[...] = jnp.zeros_like(accff)

        hb = ht_ref[...]
        for p_ref, off in zip(p_refs, offs):
            w = p_ref.shape[1]
            acc[:, off:off + w] = acc[:, off:off + w] + _dot(hb, p_ref[...])
        accff[...] = accff[...] + _dot(hb, dff_ref[...])

        @pl.when(k == nk - 1)
        def _():
            dw_ref[...] = acc[...].astype(BF16)
            dwff_ref[...] = accff[...].astype(BF16)

    return pl.pallas_call(
        body, name="inproj_bwd_dw",
        grid=(nk,),
        in_specs=[pl.BlockSpec((D, tk), lambda k: (0, k))]
        + [pl.BlockSpec((tk, p.shape[1]), lambda k: (k, 0)) for p in pieces]
        + [pl.BlockSpec((tk, N_FFPAD), lambda k: (k, 0))],
        out_specs=[pl.BlockSpec((D, N_MAIN), lambda k: (0, 0), pipeline_mode=pl.Buffered(1)),
                   pl.BlockSpec((D, N_FFPAD), lambda k: (0, 0), pipeline_mode=pl.Buffered(1))],
        out_shape=[jax.ShapeDtypeStruct((D, N_MAIN), BF16), jax.ShapeDtypeStruct((D, N_FFPAD), BF16)],
        scratch_shapes=[pltpu.VMEM((D, N_MAIN), F32), pltpu.VMEM((D, N_FFPAD), F32)],
        compiler_params=_cp(("arbitrary",), _VMEM_BIG),
    )(ht, *pieces, dff)


def _constants(T):
    tril = jnp.tril(jnp.ones((T, T), F32)).astype(BF16)
    hid = jnp.arange(FOX_W) // HEAD_DIM
    bd = (hid[:, None] == hid[None, :]).astype(BF16)
    ex = (jnp.arange(N_FFPAD)[:, None] == hid[None, :]).astype(BF16)
    return tril, tril.T, bd, ex


def _crow4(ccol, T):
    S = ccol.shape[0]
    c = ccol[:, :FOX_HEADS].T
    last = jnp.pad(c[:, T - 1::T], ((0, 0), (0, S - S // T)))
    rows = jnp.concatenate([c.reshape(FOX_HEADS // 2, 2, S), last.reshape(FOX_HEADS // 2, 2, S)], axis=1)
    return jnp.pad(rows, ((0, 0), (0, 4), (0, 0)))


def _layer_fwd(x, lw, consts, ride=None):
    tril, triu, bd, ex = consts
    proj, pff, ht = _inproj_fwd(x, lw["g"], lw["wm"], lw["wff"])
    qs, kn, ccol, cqb = _fox_prep(proj, pff, lw["bfp"], lw["gq"], lw["gk"], bd, ex, tril)
    crow4 = _crow4(ccol, tril.shape[0])
    fo, lse, *gathered = _fox_fwd(qs, kn, proj, cqb, crow4, ride)
    so, ltot = _sb_fwd(proj, triu)
    pooled = _pool_fwd(proj)
    y, mixedt = _mix_out(fo, so, pooled, proj, lw["wbd"], lw["scale"], lw["wout"], x)
    return y, (x, proj, pff, ht, qs, kn, cqb, crow4, fo, lse, so, ltot, pooled, mixedt), gathered


def _layer_bwd(dy, saved, lw, consts, ride=None, exchange_own=False):
    tril, triu, bd, _ = consts
    x, proj, pff, ht, qs, kn, cqb, crow4, fo, lse, so, ltot, pooled, mixedt = saved
    S = x.shape[0]
    dfo, dfg, dso, dsg, dpg, dpooled, dscale, dwbd = _gate_bwd(dy, lw["wout"], fo, so, pooled, proj, lw["wbd"], lw["scale"])
    dwout = _matmul_acc(mixedt, dy, "dw_out")
    dpx = _pool_bwd(dpooled)
    dqs, dkn, dfv, dck, dcq4, *received = _fox_bwd(qs, kn, proj, dfo, fo, lse, cqb, crow4, ride)
    dsq, dsk, dsv = _sb_bwd(proj, dso, ltot, tril)
    dc8 = dck[:, ::HEAD_DIM] + dcq4[:, :2, :].reshape(FOX_HEADS, S).T
    dccol = jnp.pad(dc8, ((0, 0), (0, N_FFPAD - FOX_HEADS)))
    dfq, dfk, dff, dgq, dgk, dbf = _qk_bwd(dqs, dkn, proj, pff, lw["bfp"], lw["gq"], lw["gk"], bd, dccol, triu)
    pieces = [dfq, dfk, dfv, dfg, dpx, dpg, dsq, dsk, dsv, dsg]
    dwm, dwff = _inproj_bwd_dw(ht, pieces, dff)
    dwin = jnp.concatenate([dwm[:, :OFF_PX], dwff[:, :N_FF], dwm[:, OFF_PX:]], axis=1)
    own = _grad_parts({"w_in": dwin, "w_out": dwout}) if exchange_own else None
    dx, dng, *received_own = _inproj_bwd_dx(pieces, dff, lw["wm"], lw["wff"], x, lw["g"], dy, own)
    grads = {
        "norm_g": dng[0],
        "w_in": dwin,
        "b_f": dbf[0, :N_FF],
        "q_norm_g": dgq[0].reshape(FOX_HEADS, HEAD_DIM).sum(0),
        "k_norm_g": dgk[0].reshape(FOX_HEADS, HEAD_DIM).sum(0),
        "w_pool": jnp.stack([dwbd[64 * i:64 * i + 64, 64 * i:64 * i + 64] for i in range(4)]),
        "pool_scale": dscale[0],
        "w_out": dwout,
    }
    return dx, grads, received, received_own


def _layer_weights(l, norm_g, gin, b_f, q_norm_g, k_norm_g, w_pool, pool_scale, gout):
    D = gin.shape[1]
    w = gin.transpose(1, 0, 2).reshape(D, D_IN)
    wm = jnp.concatenate([w[:, :2048], w[:, 2048 + N_FF:]], axis=1)
    wff = jnp.pad(w[:, 2048:2048 + N_FF], ((0, 0), (0, N_FFPAD - N_FF)))
    grp = jnp.arange(POOL_W) // 64
    wbd = jnp.where(grp[:, None] == grp[None, :], jnp.tile(w_pool[l].transpose(1, 0, 2).reshape(64, POOL_W), (4, 1)), 0.0)
    return {
        "g": norm_g[l].reshape(1, D),
        "wm": wm, "wff": wff,
        "bfp": jnp.pad(b_f[l], (0, N_FFPAD - N_FF)).reshape(1, N_FFPAD),
        "gq": jnp.tile(q_norm_g[l], FOX_HEADS).reshape(1, FOX_W),
        "gk": jnp.tile(k_norm_g[l], FOX_HEADS).reshape(1, FOX_W),
        "wbd": wbd.astype(BF16),
        "scale": pool_scale[l].reshape(1, POOL_W),
        "wout": gout.reshape(D_MIX, D),
    }


def _grad_parts(g):
    dwin, dwout = g["w_in"].astype(BF16), g["w_out"].astype(BF16)
    D = dwin.shape[0]
    return (dwin.reshape(D, N_DEV, D_IN // N_DEV).transpose(1, 0, 2),
            dwout.reshape(N_DEV, D_MIX // N_DEV, dwout.shape[1]))


def _train_step(x, target, norm_g, win_sh, b_f, q_norm_g, k_norm_g, w_pool, pool_scale, wout_sh):
    L = norm_g.shape[0]
    consts = _constants(min(_T, x.shape[0]))
    gathered = _exchange_pair("gather", win_sh[0], wout_sh[0], "gather_weights")
    lws, saved = [], []
    h = x
    for l in range(L):
        lws.append(_layer_weights(l, norm_g, gathered[0], b_f, q_norm_g, k_norm_g, w_pool, pool_scale, gathered[1]))
        ride = (win_sh[l + 1], wout_sh[l + 1]) if l + 1 < L else None
        h, sv, gathered = _layer_fwd(h, lws[l], consts, ride)
        saved.append(sv)
    dy, loss = _loss_head(h, target)
    grads, received = [None] * L, [None] * L
    ride = None
    for l in reversed(range(L)):
        dy, grads[l], got, got_own = _layer_bwd(dy, saved[l], lws[l], consts, ride, exchange_own=(l == 0))
        if ride is not None:
            received[l + 1] = got
        if l == 0:
            received[0] = got_own
        else:
            ride = _grad_parts(grads[l])
    return loss, dy, grads, received


def _mesh_pos():
    return lax.axis_index("x"), lax.axis_index("y"), lax.axis_index("c")


_FLIPS = [(0, 0, 1), (1, 0, 0), (0, 1, 0), (1, 1, 0), (1, 0, 1), (0, 1, 1), (1, 1, 1)]


def _peers():
    x, y, c = _mesh_pos()
    out = []
    for fx, fy, fc in _FLIPS:
        px = 1 - x if fx else x
        py = 1 - y if fy else y
        pc = 1 - c if fc else c
        out.append(((px, py, pc), 4 * px + 2 * py + pc))
    return out, 4 * x + 2 * y + c


_EXCHANGE_SEMS = [pltpu.SemaphoreType.DMA((14,)), pltpu.SemaphoreType.DMA((14,)), pltpu.SemaphoreType.DMA((2,))]
_ANY = pl.BlockSpec(memory_space=pl.ANY)


def _exchange_copies(kind, a_ref, b_ref, oa_ref, ob_ref, send_sems, recv_sems, loc_sems):
    peers, me = _peers()
    pairs = ((a_ref, oa_ref), (b_ref, ob_ref))
    local = [pltpu.make_async_copy(src if kind == "gather" else src.at[me], dst.at[me], loc_sems.at[t])
             for t, (src, dst) in enumerate(pairs)]
    remote = []
    for k, (dev, idx) in enumerate(peers):
        for t, (src, dst) in enumerate(pairs):
            remote.append(pltpu.make_async_remote_copy(
                src_ref=src if kind == "gather" else src.at[idx], dst_ref=dst.at[me],
                send_sem=send_sems.at[2 * k + t], recv_sem=recv_sems.at[2 * k + t],
                device_id=dev, device_id_type=pl.DeviceIdType.MESH))
    return local, remote


def _start_exchange(kind, *refs):
    local, remote = _exchange_copies(kind, *refs)
    for cp in local + remote:
        cp.start()


def _wait_exchange(kind, *refs):
    local, remote = _exchange_copies(kind, *refs)
    for cp in remote:
        cp.wait_recv()
    for cp in remote:
        cp.wait_send()
    for cp in local:
        cp.wait()


def _exchange_out_shapes(kind, a, b):
    if kind == "gather":
        return [jax.ShapeDtypeStruct((N_DEV,) + a.shape, a.dtype), jax.ShapeDtypeStruct((N_DEV,) + b.shape, b.dtype)]
    return [jax.ShapeDtypeStruct(a.shape, a.dtype), jax.ShapeDtypeStruct(b.shape, b.dtype)]


def _exchange_pair(kind, a, b, name):
    def body(*refs):
        _start_exchange(kind, *refs)
        _wait_exchange(kind, *refs)

    return pl.pallas_call(
        body, name=name,
        in_specs=[_ANY, _ANY], out_specs=[_ANY, _ANY],
        out_shape=_exchange_out_shapes(kind, a, b),
        scratch_shapes=_EXCHANGE_SEMS,
    )(a, b)


def _adam_math(w, g, m, v):
    m_new = ADAM_B1 * m + (1.0 - ADAM_B1) * g
    v_new = ADAM_B2 * v + (1.0 - ADAM_B2) * (g * g)
    m_hat = m_new / (1.0 - ADAM_B1 ** ADAM_STEP)
    v_hat = v_new / (1.0 - ADAM_B2 ** ADAM_STEP)
    delta = -ADAM_LR * (m_hat / (jnp.sqrt(v_hat) + ADAM_EPS) + ADAM_WD * w)
    return delta, m_new, v_new


def _sum_adamw(gparts, w, m, v, name):
    L, R, C = w.shape
    tr = min(128, R)

    def body(*refs):
        gp_refs = refs[:L]
        w_ref, m_ref, v_ref, g_ref, d_ref, nm_ref, nv_ref = refs[L:]
        for l in range(L):
            g = gp_refs[l][0].astype(F32)
            for s in range(1, N_DEV):
                g = g + gp_refs[l][s].astype(F32)
            d, mn, vn = _adam_math(w_ref[l], g, m_ref[l], v_ref[l])
            g_ref[l] = g
            d_ref[l] = d
            nm_ref[l] = mn
            nv_ref[l] = vn

    blk = pl.BlockSpec((L, tr, C), lambda r: (0, r, 0))
    return pl.pallas_call(
        body, name=name,
        grid=(R // tr,),
        in_specs=[pl.BlockSpec((N_DEV, tr, C), lambda r: (0, r, 0))] * L + [blk, blk, blk],
        out_specs=[blk, blk, blk, blk],
        out_shape=[jax.ShapeDtypeStruct((L, R, C), F32)] * 4,
        compiler_params=_cp(("parallel",), 48 << 20),
    )(*gparts, w, m, v)


def _small_update(gpack, wpack, mpack, vpack):
    R = gpack.shape[0]
    VM = pl.BlockSpec(memory_space=pltpu.VMEM)

    def body(g_ref, w_ref, m_ref, v_ref, gs_ref, d_ref, nm_ref, nv_ref, buf, send_sems, recv_sems):
        peers, me = _peers()
        buf[me] = g_ref[...]
        copies = []
        for k, (dev, _) in enumerate(peers):
            cp = pltpu.make_async_remote_copy(
                src_ref=g_ref, dst_ref=buf.at[me], send_sem=send_sems.at[k], recv_sem=recv_sems.at[k],
                device_id=dev, device_id_type=pl.DeviceIdType.MESH)
            cp.start()
            copies.append(cp)
        for cp in copies:
            cp.wait_recv()
        for cp in copies:
            cp.wait_send()
        g = buf[0]
        for s in range(1, N_DEV):
            g = g + buf[s]
        d, mn, vn = _adam_math(w_ref[...], g, m_ref[...], v_ref[...])
        gs_ref[...] = g
        d_ref[...] = d
        nm_ref[...] = mn
        nv_ref[...] = vn

    return pl.pallas_call(
        body, name="small_update",
        in_specs=[VM] * 4, out_specs=[VM] * 4,
        out_shape=[jax.ShapeDtypeStruct((R, 128), F32)] * 4,
        scratch_shapes=[pltpu.VMEM((N_DEV, R, 128), F32), pltpu.SemaphoreType.DMA((7,)), pltpu.SemaphoreType.DMA((7,))],
        compiler_params=_cp(None, 40 << 20),
    )(gpack, wpack, mpack, vpack)


_SMALL = ("norm_g", "b_f", "q_norm_g", "k_norm_g", "w_pool", "pool_scale")


def _pack(parts):
    flat = jnp.concatenate([p.reshape(-1) for p in parts])
    n = flat.shape[0]
    rows = -(-n // (8 * 128)) * 8
    return jnp.pad(flat, (0, rows * 128 - n)).reshape(rows, 128)


def _unpack(packed, like):
    flat = packed.reshape(-1)
    out, o = [], 0
    for p in like:
        out.append(flat[o:o + p.size].reshape(p.shape))
        o += p.size
    return out


def kernel(x, norm_g, w_in, b_f, q_norm_g, k_norm_g, w_pool, pool_scale, w_out, loss_target, m_norm_g, m_w_in, m_b_f, m_q_norm_g, m_k_norm_g, m_w_pool, m_pool_scale, m_w_out, v_norm_g, v_w_in, v_b_f, v_q_norm_g, v_k_norm_g, v_w_pool, v_pool_scale, v_w_out):
    L = w_in.shape[0]

    loss_local, dx, grads, received = _train_step(x[0], loss_target[0], norm_g, w_in.astype(BF16), b_f, q_norm_g,
                                                  k_norm_g, w_pool, pool_scale, w_out.astype(BF16))
    loss = lax.psum(loss_local, MESH_AXES)
    g = {k: jnp.stack([grads[l][k] for l in range(L)]) for k in _SMALL}

    g_win, d_win, nm_win, nv_win = _sum_adamw([r[0] for r in received], w_in, m_w_in, v_w_in, "adamw_w_in")
    g_wout, d_wout, nm_wout, nv_wout = _sum_adamw([r[1] for r in received], w_out, m_w_out, v_w_out, "adamw_w_out")

    ws = dict(norm_g=norm_g, b_f=b_f, q_norm_g=q_norm_g, k_norm_g=k_norm_g, w_pool=w_pool, pool_scale=pool_scale)
    ms = dict(norm_g=m_norm_g, b_f=m_b_f, q_norm_g=m_q_norm_g, k_norm_g=m_k_norm_g, w_pool=m_w_pool, pool_scale=m_pool_scale)
    vs = dict(norm_g=v_norm_g, b_f=v_b_f, q_norm_g=v_q_norm_g, k_norm_g=v_k_norm_g, w_pool=v_w_pool, pool_scale=v_pool_scale)
    like = [ws[k] for k in _SMALL]
    gs_p, d_p, nm_p, nv_p = _small_update(_pack([g[k] for k in _SMALL]), _pack(like),
                                          _pack([ms[k] for k in _SMALL]), _pack([vs[k] for k in _SMALL]))
    gs = dict(zip(_SMALL, _unpack(gs_p, like)))
    ds = dict(zip(_SMALL, _unpack(d_p, like)))
    nms = dict(zip(_SMALL, _unpack(nm_p, like)))
    nvs = dict(zip(_SMALL, _unpack(nv_p, like)))
    gs["w_in"], ds["w_in"], nms["w_in"], nvs["w_in"] = g_win, d_win, nm_win, nv_win
    gs["w_out"], ds["w_out"], nms["w_out"], nvs["w_out"] = g_wout, d_wout, nm_wout, nv_wout

    order = ("norm_g", "w_in", "b_f", "q_norm_g", "k_norm_g", "w_pool", "pool_scale", "w_out")
    return (loss, dx[None], *[gs[k] for k in order], *[ds[k] for k in order],
            *[nms[k] for k in order], *[nvs[k] for k in order])
```

```python
import functools

import jax
import jax.numpy as jnp
from jax import lax
from jax.experimental import pallas as pl
from jax.experimental.pallas import tpu as pltpu

F32 = jnp.float32
BF16 = jnp.bfloat16

EPS = 1e-6
NEG = -1e30
HEAD_DIM = 64
FOX_HEADS = 8
FOX_W = 512
POOL_W = 256
SB_W = 256
D_MIX = 1024
N_FF = 8
N_MAIN = 3584
N_FFPAD = 128
OFF_FQ, OFF_FK, OFF_FV, OFF_FG = 0, 512, 1024, 1536
OFF_PX, OFF_PG = 2048, 2304
OFF_SQ, OFF_SK, OFF_SV, OFF_SG = 2560, 2816, 3072, 3328
D_IN = 3592
Q_SCALE = HEAD_DIM ** -0.5

ADAM_LR = 0.001
ADAM_B1 = 0.9
ADAM_B2 = 0.999
ADAM_EPS = 1e-08
ADAM_WD = 0.01
ADAM_STEP = 10

N_DEV = 8
MESH_AXES = ("x", "y", "c")

_T = 256
_TM = 512
_TM_FWD, _TN_FWD = 2048, 512
_TM_DX = 512
_TK_DW = 1024
_VMEM_BIG = 56 << 20


def _cp(sem=None, vmem=None):
    kw = {}
    if sem is not None:
        kw["dimension_semantics"] = sem
    if vmem is not None:
        kw["vmem_limit_bytes"] = vmem
    return pltpu.CompilerParams(**kw)


def _dot(a, b):
    return jnp.dot(a, b, preferred_element_type=F32)


def _dot_nt(a, b):
    return lax.dot_general(a, b, (((1,), (1,)), ((), ())), preferred_element_type=F32)


def _dot_tn(a, b):
    return lax.dot_general(a, b, (((0,), (0,)), ((), ())), preferred_element_type=F32)


def _mm2(v, m, left=False):
    hi = v.astype(BF16)
    lo = (v - hi.astype(F32)).astype(BF16)
    if left:
        return _dot(m, hi) + _dot(m, lo)
    return _dot(hi, m) + _dot(lo, m)


def _mm3(v, m, left=False):
    a1 = v.astype(BF16)
    r1 = v - a1.astype(F32)
    a2 = r1.astype(BF16)
    a3 = (r1 - a2.astype(F32)).astype(BF16)
    if left:
        return _dot(m, a1) + _dot(m, a2) + _dot(m, a3)
    return _dot(a1, m) + _dot(a2, m) + _dot(a3, m)


def _sigmoid(z):
    return 1.0 / (1.0 + jnp.exp(-z))


def _rms_rows(x):
    return lax.rsqrt(jnp.mean(x * x, axis=-1, keepdims=True) + EPS)


def _inproj_fwd(x, g, wm, wff):
    S, D = x.shape
    tm = min(_TM_FWD, S)
    tn = _TN_FWD

    def body(x_ref, g_ref, w_ref, wff_ref, o_ref, off_ref, ht_ref, h_ref):
        @pl.when(pl.program_id(1) == 0)
        def _():
            xv = x_ref[...]
            h = (xv * _rms_rows(xv)) * g_ref[...]
            h_ref[...] = h.astype(BF16)
            ht_ref[...] = h.T.astype(BF16)
            off_ref[...] = _dot(h_ref[...], wff_ref[...])

        o_ref[...] = _dot(h_ref[...], w_ref[...])

    return pl.pallas_call(
        body, name="inproj_fwd",
        grid=(S // tm, N_MAIN // tn),
        in_specs=[pl.BlockSpec((tm, D), lambda i, j: (i, 0)),
                  pl.BlockSpec((1, D), lambda i, j: (0, 0)),
                  pl.BlockSpec((D, tn), lambda i, j: (0, j)),
                  pl.BlockSpec((D, N_FFPAD), lambda i, j: (0, 0))],
        out_specs=[pl.BlockSpec((tm, tn), lambda i, j: (i, j)),
                   pl.BlockSpec((tm, N_FFPAD), lambda i, j: (i, 0)),
                   pl.BlockSpec((D, tm), lambda i, j: (0, i))],
        out_shape=[jax.ShapeDtypeStruct((S, N_MAIN), F32), jax.ShapeDtypeStruct((S, N_FFPAD), F32),
                   jax.ShapeDtypeStruct((D, S), BF16)],
        scratch_shapes=[pltpu.VMEM((tm, D), BF16)],
        compiler_params=_cp(("parallel", "arbitrary"), 48 << 20),
    )(x, g, wm, wff)


def _head_norm(x, g, bd):
    ss = _mm2(x * x, bd)
    r = lax.rsqrt(ss * (1.0 / HEAD_DIM) + EPS)
    return (x * r) * g


def _fox_prep(proj, pff, bfp, gq, gk, bd, ex, tril):
    S = proj.shape[0]
    T = tril.shape[0]

    def body(q_ref, k_ref, ff_ref, b_ref, gq_ref, gk_ref, bd_ref, ex_ref, tri_ref,
             qs_ref, kn_ref, cc_ref, cqb_ref, carry):
        @pl.when(pl.program_id(0) == 0)
        def _():
            carry[...] = jnp.zeros_like(carry)

        bdv = bd_ref[...]
        qs_ref[...] = (_head_norm(q_ref[...], gq_ref[...], bdv) * Q_SCALE).astype(BF16)
        kn_ref[...] = _head_norm(k_ref[...], gk_ref[...], bdv).astype(BF16)
        u = ff_ref[...] + b_ref[...]
        lf = jnp.minimum(u, 0.0) - jnp.log1p(jnp.exp(-jnp.abs(u)))
        c = _mm3(lf, tri_ref[...], left=True) + carry[0:1, :]
        carry[0:1, :] = c[T - 1:T, :]
        cc_ref[...] = c
        cqb_ref[...] = _mm3(c, ex_ref[...])

    return pl.pallas_call(
        body, name="fox_prep",
        grid=(S // T,),
        in_specs=[pl.BlockSpec((T, FOX_W), lambda i: (i, OFF_FQ // FOX_W)),
                  pl.BlockSpec((T, FOX_W), lambda i: (i, OFF_FK // FOX_W)),
                  pl.BlockSpec((T, N_FFPAD), lambda i: (i, 0)),
                  pl.BlockSpec((1, N_FFPAD), lambda i: (0, 0)),
                  pl.BlockSpec((1, FOX_W), lambda i: (0, 0)),
                  pl.BlockSpec((1, FOX_W), lambda i: (0, 0)),
                  pl.BlockSpec((FOX_W, FOX_W), lambda i: (0, 0)),
                  pl.BlockSpec((N_FFPAD, FOX_W), lambda i: (0, 0)),
                  pl.BlockSpec((T, T), lambda i: (0, 0))],
        out_specs=[pl.BlockSpec((T, FOX_W), lambda i: (i, 0)),
                   pl.BlockSpec((T, FOX_W), lambda i: (i, 0)),
                   pl.BlockSpec((T, N_FFPAD), lambda i: (i, 0)),
                   pl.BlockSpec((T, FOX_W), lambda i: (i, 0))],
        out_shape=[jax.ShapeDtypeStruct((S, FOX_W), BF16), jax.ShapeDtypeStruct((S, FOX_W), BF16),
                   jax.ShapeDtypeStruct((S, N_FFPAD), F32), jax.ShapeDtypeStruct((S, FOX_W), F32)],
        scratch_shapes=[pltpu.VMEM((8, N_FFPAD), F32)],
        compiler_params=_cp(("arbitrary",), 40 << 20),
    )(proj, proj, pff, bfp, gq, gk, bd, ex, tril)


def _pair_blk(S, off=0):
    return pl.BlockSpec((S, 128), lambda p: (0, off + p), pipeline_mode=pl.Buffered(1))


def _pair_rows(S):
    return pl.BlockSpec((None, 8, S), lambda p: (p, 0, 0), pipeline_mode=pl.Buffered(1))


def _head_masks(S):
    return lax.broadcasted_iota(jnp.int32, (S, 128), 1) < HEAD_DIM


_EXP_ZERO = 104.0


def _spread_heads(x):
    src = lax.broadcasted_iota(jnp.int32, (128, 128), 0)
    return (_mm3(x, (src == 0).astype(BF16)), _mm3(x, (src == HEAD_DIM).astype(BF16)))


def _score_bounds(q, k):
    same_head = ((lax.broadcasted_iota(jnp.int32, (128, 128), 0) < HEAD_DIM)
                 == (lax.broadcasted_iota(jnp.int32, (128, 128), 1) < HEAD_DIM)).astype(BF16)

    def max_norm2(x):
        xf = x.astype(F32)
        return jnp.max(_mm2(xf * xf, same_head), axis=0, keepdims=True)

    z = jnp.sqrt(max_norm2(q) * max_norm2(k))
    return jnp.max(z[:, 0:1]) * 1.001 + 1e-3, jnp.max(z[:, 64:65]) * 1.001 + 1e-3


def _for_tiles_back(i, n, tiles_fn, fours=False):
    if fours:
        def four(t, c):
            tiles_fn([i - 1 - 4 * t, i - 2 - 4 * t, i - 3 - 4 * t, i - 4 - 4 * t])
            return c

        lax.fori_loop(0, lax.shift_right_logical(n, 2), four, 0)
        rest = i - (n & ~3)

        @pl.when((n & 2) != 0)
        def _():
            tiles_fn([rest - 1, rest - 2])
    else:
        def two(t, c):
            tiles_fn([i - 1 - 2 * t, i - 2 - 2 * t])
            return c

        lax.fori_loop(0, lax.shift_right_logical(n, 1), two, 0)

    @pl.when((n & 1) != 0)
    def _():
        tiles_fn([i - n])


def _fox_tiles_back(cr_ref, i, r0, zba, zbb):
    last = cr_ref[:, pl.ds(0, 128)]
    first = cr_ref[:, pl.ds(r0, 128)]
    alive_a = 2.0 * zba + first[0:1, 0:1] - last[2:3, :] > -_EXP_ZERO
    alive_b = 2.0 * zbb + first[1:2, 0:1] - last[3:4, :] > -_EXP_ZERO
    before = lax.broadcasted_iota(jnp.int32, (1, 128), 1) < i
    return jnp.sum((before & (alive_a | alive_b)).astype(jnp.int32))


def _fox_fwd(qs, kn, proj, cqb, crow4, ride=None):
    S = qs.shape[0]
    T = min(_T, S)
    nq = S // T
    n_pairs = FOX_W // 128

    def body(*refs):
        if ride is None:
            q_ref, k_ref, v_ref, cq_ref, cr_ref, o_ref, lse_ref = refs[:7]
            qa, qb, vta, vtb, cka, ckb, ma, mb, acca, accb = refs[7:]
        else:
            q_ref, k_ref, v_ref, cq_ref, cr_ref, wa_ref, wb_ref, o_ref, lse_ref, ga_ref, gb_ref = refs[:11]
            qa, qb, vta, vtb, cka, ckb, ma, mb, acca, accb = refs[11:21]
            xrefs = (wa_ref, wb_ref, ga_ref, gb_ref) + tuple(refs[21:])

            @pl.when(pl.program_id(0) == 0)
            def _():
                _start_exchange("gather", *xrefs)

        lane_s = _head_masks(S)
        q = q_ref[...]
        zq = jnp.zeros_like(q)
        qa[...] = jnp.where(lane_s, q, zq)
        qb[...] = jnp.where(lane_s, zq, q)
        cq = cq_ref[...]
        cka[...], ckb[...] = _spread_heads(cq)
        lse_ref[...] = jnp.zeros((8, S), F32)
        row_t = lax.broadcasted_iota(jnp.int32, (128, T), 0) < HEAD_DIM
        zba, zbb = _score_bounds(q, k_ref[...])

        def prep(c, carry):
            c0 = pl.multiple_of(c * T, T)
            vt = v_ref[pl.ds(c0, T), :].T
            vta[:, pl.ds(c0, T)] = jnp.where(row_t, vt, 1.0).astype(BF16)
            vtb[:, pl.ds(c0, T)] = jnp.where(row_t, 1.0, vt).astype(BF16)
            return carry

        lax.fori_loop(0, nq, prep, 0)
        causal = (lax.broadcasted_iota(jnp.int32, (T, T), 0) <= lax.broadcasted_iota(jnp.int32, (T, T), 1))

        heads = ((qa, vta, cka, ma, acca), (qb, vtb, ckb, mb, accb))

        def kv(js, r0, masked):
            cr = cr_ref[:, pl.ds(r0, T)]
            c0s = [pl.multiple_of(j * T, T) for j in js]
            ks = [k_ref[pl.ds(c0, T), :] for c0 in c0s]
            ss = []
            for h, (qr, _, ckr, _, _) in enumerate(heads):
                qh = qr[pl.ds(r0, T), :]
                row = []
                for k, c0 in zip(ks, c0s):
                    s = _dot_nt(k, qh) + cr[h:h + 1, :] - jnp.tile(ckr[pl.ds(c0, T), :], (1, T // 128))
                    row.append(jnp.where(causal, s, NEG) if masked else s)
                ss.append(row)
            ms = []
            for row, (_, _, _, mr, _) in zip(ss, heads):
                top = row[0]
                for s in row[1:]:
                    top = jnp.maximum(top, s)
                m_old = mr[0:1, :]
                ms.append((m_old, jnp.maximum(m_old, jnp.max(top, axis=0, keepdims=True))))
            ps = [[jnp.exp(s - m_new).astype(BF16) for s in row] for row, (_, m_new) in zip(ss, ms)]
            pvs = []
            for row, (_, vr, _, _, _) in zip(ps, heads):
                pv = _dot(vr[:, pl.ds(c0s[0], T)], row[0])
                for p, c0 in zip(row[1:], c0s[1:]):
                    pv = pv + _dot(vr[:, pl.ds(c0, T)], p)
                pvs.append(pv)
            for pv, (m_old, m_new), (_, _, _, mr, ar) in zip(pvs, ms, heads):
                ar[...] = jnp.exp(m_old - m_new) * ar[...] + pv
                mr[0:1, :] = m_new

        def qblk(i, carry):
            r0 = pl.multiple_of(i * T, T)
            ma[...] = jnp.full((8, T), NEG, F32)
            mb[...] = jnp.full((8, T), NEG, F32)
            acca[...] = jnp.zeros((128, T), F32)
            accb[...] = jnp.zeros((128, T), F32)
            kv([i], r0, True)
            done = _fox_tiles_back(cr_ref, i, r0, zba, zbb)
            _for_tiles_back(i, done, lambda js: kv(js, r0, False), fours=True)
            aa = acca[...]
            ab = accb[...]
            la = aa[64:65, :]
            lb = ab[0:1, :]
            o_ref[pl.ds(r0, T), :] = jnp.where(row_t, aa / la, ab / lb).T
            lse_ref[0:1, pl.ds(r0, T)] = ma[0:1, :] + jnp.log(la)
            lse_ref[1:2, pl.ds(r0, T)] = mb[0:1, :] + jnp.log(lb)
            lse_ref[2:3, pl.ds(r0, T)] = jnp.broadcast_to(done.astype(F32), (1, T))
            return carry

        lax.fori_loop(0, nq, qblk, 0)
        if ride is not None:
            @pl.when(pl.program_id(0) == n_pairs - 1)
            def _():
                _wait_exchange("gather", *xrefs)

    extra = () if ride is None else tuple(ride)
    return pl.pallas_call(
        body, name="fox_fwd" if ride is None else "fox_fwd_gather",
        grid=(n_pairs,),
        in_specs=[_pair_blk(S), _pair_blk(S), _pair_blk(S, OFF_FV // 128), _pair_blk(S), _pair_rows(S)]
        + [_ANY] * len(extra),
        out_specs=[_pair_blk(S), _pair_rows(S)] + [_ANY] * len(extra),
        out_shape=[jax.ShapeDtypeStruct((S, FOX_W), F32), jax.ShapeDtypeStruct((n_pairs, 8, S), F32)]
        + (_exchange_out_shapes("gather", *extra) if extra else []),
        scratch_shapes=[pltpu.VMEM((S, 128), BF16)] * 2 + [pltpu.VMEM((128, S), BF16)] * 2
        + [pltpu.VMEM((S, 128), F32)] * 2 + [pltpu.VMEM((8, T), F32)] * 2 + [pltpu.VMEM((128, T), F32)] * 2
        + (_EXCHANGE_SEMS if extra else []),
        compiler_params=_cp(("arbitrary",), _VMEM_BIG),
    )(qs, kn, proj, cqb, crow4, *extra)


def _softplus_parts(z):
    e = jnp.exp(-jnp.abs(z))
    return e, jnp.maximum(z, 0.0) + jnp.log(1.0 + e)


def _sb_fwd(proj, triu):
    S = proj.shape[0]
    T = triu.shape[0]
    nq = S // T

    def body(q_ref, k_ref, v_ref, tri_ref, o_ref, lt_ref, qa, qb, kb, vt, ra, rb, acca, accb):
        lane_s = _head_masks(S)
        q = (q_ref[...] * Q_SCALE).astype(BF16)
        zq = jnp.zeros_like(q)
        qa[...] = jnp.where(lane_s, q, zq)
        qb[...] = jnp.where(lane_s, zq, q)
        kb[...] = k_ref[...].astype(BF16)
        lt_ref[...] = jnp.zeros((8, S), F32)
        row_t = lax.broadcasted_iota(jnp.int32, (128, T), 0) < HEAD_DIM
        zba, zbb = _score_bounds(q, kb[...])

        def prep(c, carry):
            c0 = pl.multiple_of(c * T, T)
            vt[:, pl.ds(c0, T)] = v_ref[pl.ds(c0, T), :].T.astype(BF16)
            return carry

        lax.fori_loop(0, nq, prep, 0)
        strict = (lax.broadcasted_iota(jnp.int32, (T, T), 0) < lax.broadcasted_iota(jnp.int32, (T, T), 1))

        heads = ((qa, ra, acca), (qb, rb, accb))

        def kv(tiles, r0):
            tri = tri_ref[...]
            c0s = [pl.multiple_of(j * T, T) for j, _ in tiles]
            ks = [kb[pl.ds(c0, T), :] for c0 in c0s]
            qhs = [qr[pl.ds(r0, T), :] for qr, _, _ in heads]
            zs = [[_dot_nt(k, qh) for k in ks] for qh in qhs]
            lbs = [[jnp.where(strict, -_softplus_parts(z)[1], 0.0) if masked else -_softplus_parts(z)[1]
                    for z, (_, masked) in zip(row, tiles)] for row in zs]
            incs = [[_mm2(lb, tri, left=True) for lb in row] for row in lbs]
            avs = []
            for (_, r_ref, _), zrow, irow in zip(heads, zs, incs):
                r = r_ref[0:1, :]
                av = None
                for z, inc, c0, (_, masked) in zip(zrow, irow, c0s, tiles):
                    a = jnp.exp(z + inc + r)
                    if masked:
                        a = jnp.where(strict, a, 0.0)
                    term = _dot(vt[:, pl.ds(c0, T)], a.astype(BF16))
                    av = term if av is None else av + term
                    r = r + inc[0:1, :]
                avs.append((av, r))
            for (_, r_ref, acc_ref), (av, r) in zip(heads, avs):
                r_ref[0:1, :] = r
                acc_ref[...] = acc_ref[...] + av

        def qblk(i, carry):
            r0 = pl.multiple_of(i * T, T)
            ra[...] = jnp.zeros((8, T), F32)
            rb[...] = jnp.zeros((8, T), F32)
            acca[...] = jnp.zeros((128, T), F32)
            accb[...] = jnp.zeros((128, T), F32)

            @pl.when(i == 0)
            def _():
                kv([(i, True)], r0)

            @pl.when(i > 0)
            def _():
                kv([(i, True), (i - 1, False)], r0)

            def alive():
                return jnp.maximum(jnp.max(ra[0:1, :]) + zba, jnp.max(rb[0:1, :]) + zbb) > -_EXP_ZERO

            def cond(st):
                return (st[0] < i) & st[1]

            def step(st):
                kv([(i - 1 - st[0], False)], r0)
                return st[0] + 1, alive()

            done, _ = lax.while_loop(cond, step, (jnp.minimum(i, 1), alive()))
            o_ref[pl.ds(r0, T), :] = jnp.where(row_t, acca[...], accb[...]).T
            lt_ref[0:1, pl.ds(r0, T)] = ra[0:1, :]
            lt_ref[1:2, pl.ds(r0, T)] = rb[0:1, :]
            lt_ref[2:3, pl.ds(r0, T)] = jnp.broadcast_to(done.astype(F32), (1, T))
            return carry

        lax.fori_loop(0, nq, qblk, 0)

    return pl.pallas_call(
        body, name="sb_fwd",
        grid=(SB_W // 128,),
        in_specs=[_pair_blk(S, OFF_SQ // 128), _pair_blk(S, OFF_SK // 128), _pair_blk(S, OFF_SV // 128),
                  pl.BlockSpec((T, T), lambda p: (0, 0))],
        out_specs=[_pair_blk(S), _pair_rows(S)],
        out_shape=[jax.ShapeDtypeStruct((S, SB_W), F32), jax.ShapeDtypeStruct((SB_W // 128, 8, S), F32)],
        scratch_shapes=[pltpu.VMEM((S, 128), BF16)] * 3 + [pltpu.VMEM((128, S), BF16)]
        + [pltpu.VMEM((8, T), F32)] * 2 + [pltpu.VMEM((128, T), F32)] * 2,
        compiler_params=_cp(("arbitrary",), _VMEM_BIG),
    )(proj, proj, proj, triu)


def _pool_window_lanes(shape):
    lane = lax.broadcasted_iota(jnp.int32, shape, 1)
    return jnp.where(lane < 64, 2, jnp.where(lane < 128, 4, jnp.where(lane < 192, 8, 16)))


def _pool_fwd(proj):
    S = proj.shape[0]

    def body(x_ref, o_ref):
        x = x_ref[...]
        t = lax.broadcasted_iota(jnp.int32, x.shape, 0)
        lane = lax.broadcasted_iota(jnp.int32, x.shape, 1)

        def back(a, k):
            return jnp.where(t >= k, pltpu.roll(a, k, 0), 0.0)

        s1 = x + back(x, 1)
        s2 = s1 + back(s1, 2)
        s4 = s2 + back(s2, 4)
        s8 = s4 + back(s4, 8)
        win = jnp.where(lane < 64, s1, jnp.where(lane < 128, s2, jnp.where(lane < 192, s4, s8)))
        cnt = jnp.minimum(t + 1, _pool_window_lanes(x.shape)).astype(F32)
        o_ref[...] = win / cnt - x

    return pl.pallas_call(
        body, name="pool_fwd",
        grid=(1,),
        in_specs=[pl.BlockSpec((S, POOL_W), lambda i: (0, OFF_PX // POOL_W))],
        out_specs=pl.BlockSpec((S, POOL_W), lambda i: (0, 0)),
        out_shape=jax.ShapeDtypeStruct((S, POOL_W), F32),
        compiler_params=_cp(("arbitrary",), _VMEM_BIG),
    )(proj)


def _silu(g):
    return g * _sigmoid(g)


def _mix_out(fo, so, pooled, proj, wbd, scale, wout, x):
    S, D = x.shape
    tm = min(256, S)

    def body(fo_ref, fg_ref, so_ref, sg_ref, pl_ref, pg_ref, wbd_ref, sc_ref, w_ref, x_ref, y_ref, mxt_ref, mx_ref):
        parts = ((0, fo_ref[...] * _silu(fg_ref[...])),
                 (FOX_W, (_dot(pl_ref[...].astype(BF16), wbd_ref[...]) * sc_ref[...]) * _silu(pg_ref[...])),
                 (FOX_W + POOL_W, so_ref[...] * _silu(sg_ref[...])))
        for off, part in parts:
            w = part.shape[1]
            mx_ref[:, off:off + w] = part.astype(BF16)
            mxt_ref[off:off + w, :] = part.T.astype(BF16)
        y_ref[...] = x_ref[...] + _dot(mx_ref[...], w_ref[...])

    return pl.pallas_call(
        body, name="mix_out",
        grid=(S // tm,),
        in_specs=[pl.BlockSpec((tm, FOX_W), lambda i: (i, 0)),
                  pl.BlockSpec((tm, FOX_W), lambda i: (i, OFF_FG // FOX_W)),
                  pl.BlockSpec((tm, SB_W), lambda i: (i, 0)),
                  pl.BlockSpec((tm, SB_W), lambda i: (i, OFF_SG // SB_W)),
                  pl.BlockSpec((tm, POOL_W), lambda i: (i, 0)),
                  pl.BlockSpec((tm, POOL_W), lambda i: (i, OFF_PG // POOL_W)),
                  pl.BlockSpec((POOL_W, POOL_W), lambda i: (0, 0)),
                  pl.BlockSpec((1, POOL_W), lambda i: (0, 0)),
                  pl.BlockSpec((D_MIX, D), lambda i: (0, 0)),
                  pl.BlockSpec((tm, D), lambda i: (i, 0))],
        out_specs=[pl.BlockSpec((tm, D), lambda i: (i, 0)), pl.BlockSpec((D_MIX, tm), lambda i: (0, i))],
        out_shape=[jax.ShapeDtypeStruct((S, D), F32), jax.ShapeDtypeStruct((D_MIX, S), BF16)],
        scratch_shapes=[pltpu.VMEM((tm, D_MIX), BF16)],
        compiler_params=_cp(("parallel",), 40 << 20),
    )(fo, proj, so, proj, pooled, proj, wbd, scale, wout, x)


def _loss_head(y, target):
    S, D = y.shape
    tm = min(_TM, S)

    def body(y_ref, t_ref, dy_ref, ls_ref):
        @pl.when(pl.program_id(0) == 0)
        def _():
            ls_ref[...] = jnp.zeros_like(ls_ref)

        e = y_ref[...] - t_ref[...]
        dy_ref[...] = e * (1.0 / D)
        ls_ref[...] = ls_ref[...] + jnp.sum(e * e) * (0.5 / D)

    dy, ls = pl.pallas_call(
        body, name="loss_head",
        grid=(S // tm,),
        in_specs=[pl.BlockSpec((tm, D), lambda i: (i, 0)), pl.BlockSpec((tm, D), lambda i: (i, 0))],
        out_specs=[pl.BlockSpec((tm, D), lambda i: (i, 0)), pl.BlockSpec((8, 128), lambda i: (0, 0))],
        out_shape=[jax.ShapeDtypeStruct((S, D), F32), jax.ShapeDtypeStruct((8, 128), F32)],
        compiler_params=_cp(("arbitrary",), 40 << 20),
    )(y, target)
    return dy, ls[0, 0]


def _dsilu(g):
    s = _sigmoid(g)
    return s * (1.0 + g * (1.0 - s))


def _gate_bwd(dy, wout, fo, so, pooled, proj, wbd, scale):
    S, D = dy.shape
    tm = min(256, S)

    def body(dy_ref, w_ref, fo_ref, fg_ref, so_ref, sg_ref, pl_ref, pg_ref, wbd_ref, sc_ref,
             dfo_ref, dfg_ref, dso_ref, dsg_ref, dpg_ref, dpl_ref, dsc_ref, dwbd_ref):
        @pl.when(pl.program_id(0) == 0)
        def _():
            dsc_ref[...] = jnp.zeros_like(dsc_ref)
            dwbd_ref[...] = jnp.zeros_like(dwbd_ref)

        dm = _dot_nt(dy_ref[...].astype(BF16), w_ref[...])
        dmf = dm[:, 0:FOX_W]
        dmp = dm[:, FOX_W:FOX_W + POOL_W]
        dms = dm[:, FOX_W + POOL_W:D_MIX]
        fg = fg_ref[...]
        dfo_ref[...] = dmf * _silu(fg)
        dfg_ref[...] = (dmf * fo_ref[...] * _dsilu(fg)).astype(BF16)
        sg = sg_ref[...]
        dso_ref[...] = dms * _silu(sg)
        dsg_ref[...] = (dms * so_ref[...] * _dsilu(sg)).astype(BF16)
        pg = pg_ref[...]
        plb = pl_ref[...].astype(BF16)
        yw = _dot(plb, wbd_ref[...])
        sc = sc_ref[...]
        dpg_ref[...] = (dmp * (yw * sc) * _dsilu(pg)).astype(BF16)
        dys = dmp * _silu(pg)
        dsc_ref[...] = dsc_ref[...] + jnp.sum(dys * yw, axis=0, keepdims=True)
        dyw = (dys * sc).astype(BF16)
        dpl_ref[...] = _dot_nt(dyw, wbd_ref[...])
        dwbd_ref[...] = dwbd_ref[...] + _dot_tn(plb, dyw)

    return pl.pallas_call(
        body, name="gate_bwd",
        grid=(S // tm,),
        in_specs=[pl.BlockSpec((tm, D), lambda i: (i, 0)),
                  pl.BlockSpec((D_MIX, D), lambda i: (0, 0)),
                  pl.BlockSpec((tm, FOX_W), lambda i: (i, 0)),
                  pl.BlockSpec((tm, FOX_W), lambda i: (i, OFF_FG // FOX_W)),
                  pl.BlockSpec((tm, SB_W), lambda i: (i, 0)),
                  pl.BlockSpec((tm, SB_W), lambda i: (i, OFF_SG // SB_W)),
                  pl.BlockSpec((tm, POOL_W), lambda i: (i, 0)),
                  pl.BlockSpec((tm, POOL_W), lambda i: (i, OFF_PG // POOL_W)),
                  pl.BlockSpec((POOL_W, POOL_W), lambda i: (0, 0)),
                  pl.BlockSpec((1, POOL_W), lambda i: (0, 0))],
        out_specs=[pl.BlockSpec((tm, FOX_W), lambda i: (i, 0)),
                   pl.BlockSpec((tm, FOX_W), lambda i: (i, 0)),
                   pl.BlockSpec((tm, SB_W), lambda i: (i, 0)),
                   pl.BlockSpec((tm, SB_W), lambda i: (i, 0)),
                   pl.BlockSpec((tm, POOL_W), lambda i: (i, 0)),
                   pl.BlockSpec((tm, POOL_W), lambda i: (i, 0)),
                   pl.BlockSpec((1, POOL_W), lambda i: (0, 0)),
                   pl.BlockSpec((POOL_W, POOL_W), lambda i: (0, 0))],
        out_shape=[jax.ShapeDtypeStruct((S, FOX_W), F32), jax.ShapeDtypeStruct((S, FOX_W), BF16),
                   jax.ShapeDtypeStruct((S, SB_W), F32), jax.ShapeDtypeStruct((S, SB_W), BF16),
                   jax.ShapeDtypeStruct((S, POOL_W), BF16), jax.ShapeDtypeStruct((S, POOL_W), F32),
                   jax.ShapeDtypeStruct((1, POOL_W), F32), jax.ShapeDtypeStruct((POOL_W, POOL_W), F32)],
        compiler_params=_cp(("arbitrary",), 40 << 20),
    )(dy, wout, fo, proj, so, proj, pooled, proj, wbd, scale)


def _matmul_acc(at, b, name):
    M, S = at.shape
    N = b.shape[1]
    tk = min(_TK_DW, S)
    tn = min(512, N)
    nk = S // tk

    def body(a_ref, b_ref, o_ref, acc):
        k = pl.program_id(1)

        @pl.when(k == 0)
        def _():
            acc[...] = jnp.zeros_like(acc)

        acc[...] = acc[...] + _dot(a_ref[...], b_ref[...].astype(BF16))

        @pl.when(k == nk - 1)
        def _():
            o_ref[...] = acc[...].astype(BF16)

    return pl.pallas_call(
        body, name=name,
        grid=(N // tn, nk),
        in_specs=[pl.BlockSpec((M, tk), lambda j, k: (0, k)), pl.BlockSpec((tk, tn), lambda j, k: (k, j))],
        out_specs=pl.BlockSpec((M, tn), lambda j, k: (0, j)),
        out_shape=jax.ShapeDtypeStruct((M, N), BF16),
        scratch_shapes=[pltpu.VMEM((M, tn), F32)],
        compiler_params=_cp(("parallel", "arbitrary"), 40 << 20),
    )(at, b)


def _pool_bwd(dpooled):
    S = dpooled.shape[0]

    def body(d_ref, o_ref):
        d = d_ref[...]
        t = lax.broadcasted_iota(jnp.int32, d.shape, 0)
        lane = lax.broadcasted_iota(jnp.int32, d.shape, 1)
        cnt = jnp.minimum(t + 1, _pool_window_lanes(d.shape)).astype(F32)
        u = d / cnt

        def fwd(a, k):
            return jnp.where(t < S - k, pltpu.roll(a, S - k, 0), 0.0)

        s1 = u + fwd(u, 1)
        s2 = s1 + fwd(s1, 2)
        s4 = s2 + fwd(s2, 4)
        s8 = s4 + fwd(s4, 8)
        win = jnp.where(lane < 64, s1, jnp.where(lane < 128, s2, jnp.where(lane < 192, s4, s8)))
        o_ref[...] = (win - d).astype(BF16)

    return pl.pallas_call(
        body, name="pool_bwd",
        grid=(1,),
        in_specs=[pl.BlockSpec((S, POOL_W), lambda i: (0, 0))],
        out_specs=pl.BlockSpec((S, POOL_W), lambda i: (0, 0)),
        out_shape=jax.ShapeDtypeStruct((S, POOL_W), BF16),
        compiler_params=_cp(("arbitrary",), _VMEM_BIG),
    )(dpooled)


def _fox_bwd(qs, kn, proj, dfo, fo, lse, cqb, crow4, ride=None):
    S = qs.shape[0]
    T = min(_T, S)
    nq = S // T
    n_pairs = FOX_W // 128

    def body(*refs):
        if ride is None:
            q_ref, k_ref, v_ref, do_ref, o_ref, lse_ref, cq_ref, cr_ref = refs[:8]
            dq_ref, dk_ref, dv_ref, dck_ref, dcq_ref = refs[8:13]
            scr = refs[13:]
        else:
            q_ref, k_ref, v_ref, do_ref, o_ref, lse_ref, cq_ref, cr_ref, pa_ref, pb_ref = refs[:10]
            dq_ref, dk_ref, dv_ref, dck_ref, dcq_ref, ra_ref, rb_ref = refs[10:17]
            scr = refs[17:32]
            xrefs = (pa_ref, pb_ref, ra_ref, rb_ref) + tuple(refs[32:])

            @pl.when(pl.program_id(0) == 0)
            def _():
                _start_exchange("scatter", *xrefs)

        qa, qb, kta, ktb, vb, doa, dob, cka, ckb, dcka, dckb, dva, dqt, dcqa, dcqb = scr
        lane_s = _head_masks(S)
        q = q_ref[...]
        zq = jnp.zeros_like(q)
        qa[...] = jnp.where(lane_s, q, zq)
        qb[...] = jnp.where(lane_s, zq, q)
        vb[...] = v_ref[...].astype(BF16)
        do = do_ref[...].astype(BF16)
        doa[...] = jnp.where(lane_s, do, zq)
        dob[...] = jnp.where(lane_s, zq, do)
        cq = cq_ref[...]
        cka[...], ckb[...] = _spread_heads(cq)
        zs = jnp.zeros((S, 128), F32)
        dk_ref[...] = zs
        dva[...] = zs
        dcka[...] = zs
        dckb[...] = zs
        dcq_ref[...] = jnp.zeros((8, S), F32)
        row_t = lax.broadcasted_iota(jnp.int32, (128, T), 0) < HEAD_DIM

        def prep(c, carry):
            c0 = pl.multiple_of(c * T, T)
            kt = k_ref[pl.ds(c0, T), :].astype(F32).T
            kta[:, pl.ds(c0, T)] = jnp.where(row_t, kt, 0.0).astype(BF16)
            ktb[:, pl.ds(c0, T)] = jnp.where(row_t, 0.0, kt).astype(BF16)
            return carry

        lax.fori_loop(0, nq, prep, 0)
        causal = (lax.broadcasted_iota(jnp.int32, (T, T), 0) <= lax.broadcasted_iota(jnp.int32, (T, T), 1))

        heads = ((qa, kta, doa, cka, dcka, dcqa), (qb, ktb, dob, ckb, dckb, dcqb))

        def kv(js, r0, lss, dls, masked):
            cr = cr_ref[:, pl.ds(r0, T)]
            c0s = [pl.multiple_of(j * T, T) for j in js]
            ks = [k_ref[pl.ds(c0, T), :] for c0 in c0s]
            vs = [vb[pl.ds(c0, T), :] for c0 in c0s]
            qhs = [hd[0][pl.ds(r0, T), :] for hd in heads]
            dohs = [hd[2][pl.ds(r0, T), :] for hd in heads]
            ss = []
            for h, hd in enumerate(heads):
                row = []
                for k, c0 in zip(ks, c0s):
                    s = _dot_nt(k, qhs[h]) + cr[h:h + 1, :] - jnp.tile(hd[3][pl.ds(c0, T), :], (1, T // 128))
                    row.append(jnp.where(causal, s, NEG) if masked else s)
                ss.append(row)
            ps = [[jnp.exp(s - lss[h]) for s in row] for h, row in enumerate(ss)]
            dps = [[_dot_nt(v, dohs[h]) for v in vs] for h in range(2)]
            dss = [[p * (dp - dls[h]) for p, dp in zip(ps[h], dps[h])] for h in range(2)]
            pbs = [[p.astype(BF16) for p in row] for row in ps]
            dsbs = [[ds.astype(BF16) for ds in row] for row in dss]
            for t, c0 in enumerate(c0s):
                dva[pl.ds(c0, T), :] = dva[pl.ds(c0, T), :] + (_dot(pbs[0][t], dohs[0]) + _dot(pbs[1][t], dohs[1]))
                dk_ref[pl.ds(c0, T), :] = dk_ref[pl.ds(c0, T), :] + (_dot(dsbs[0][t], qhs[0]) + _dot(dsbs[1][t], qhs[1]))
            dq = None
            for h, hd in enumerate(heads):
                for t, c0 in enumerate(c0s):
                    term = _dot(hd[1][:, pl.ds(c0, T)], dsbs[h][t])
                    dq = term if dq is None else dq + term
            dqt[...] = dqt[...] + dq
            for h, hd in enumerate(heads):
                col = jnp.sum(dss[h][0], axis=0, keepdims=True)
                for ds in dss[h][1:]:
                    col = col + jnp.sum(ds, axis=0, keepdims=True)
                hd[5][0:1, :] = hd[5][0:1, :] + col
                for ds, c0 in zip(dss[h], c0s):
                    fold = ds[:, 0:128]
                    for u in range(1, T // 128):
                        fold = fold + ds[:, 128 * u:128 * (u + 1)]
                    hd[4][pl.ds(c0, T), :] = hd[4][pl.ds(c0, T), :] - fold

        def qblk(i, carry):
            r0 = pl.multiple_of(i * T, T)
            dt = (do_ref[pl.ds(r0, T), :] * o_ref[pl.ds(r0, T), :]).T
            dla = jnp.sum(jnp.where(row_t, dt, 0.0), axis=0, keepdims=True)
            dlb = jnp.sum(jnp.where(row_t, 0.0, dt), axis=0, keepdims=True)
            ls = lse_ref[:, pl.ds(r0, T)]
            lss = (ls[0:1, :], ls[1:2, :])
            back = jnp.max(ls[2:3, :]).astype(jnp.int32)
            dqt[...] = jnp.zeros((128, T), F32)
            dcqa[...] = jnp.zeros((8, T), F32)
            dcqb[...] = jnp.zeros((8, T), F32)
            kv([i], r0, lss, (dla, dlb), True)
            _for_tiles_back(i, back, lambda js: kv(js, r0, lss, (dla, dlb), False), fours=True)
            dq_ref[pl.ds(r0, T), :] = dqt[...].T
            dcq_ref[0:1, pl.ds(r0, T)] = dcqa[0:1, :]
            dcq_ref[1:2, pl.ds(r0, T)] = dcqb[0:1, :]
            return carry

        lax.fori_loop(0, nq, qblk, 0)
        dv_ref[...] = dva[...].astype(BF16)
        dck_ref[...] = jnp.where(lane_s, jnp.sum(dcka[...], axis=1, keepdims=True),
                                 jnp.sum(dckb[...], axis=1, keepdims=True))
        if ride is not None:
            @pl.when(pl.program_id(0) == n_pairs - 1)
            def _():
                _wait_exchange("scatter", *xrefs)

    extra = () if ride is None else tuple(ride)
    return pl.pallas_call(
        body, name="fox_bwd" if ride is None else "fox_bwd_exchange",
        grid=(n_pairs,),
        in_specs=[_pair_blk(S), _pair_blk(S), _pair_blk(S, OFF_FV // 128), _pair_blk(S), _pair_blk(S),
                  _pair_rows(S), _pair_blk(S), _pair_rows(S)] + [_ANY] * len(extra),
        out_specs=[_pair_blk(S), _pair_blk(S), _pair_blk(S), _pair_blk(S), _pair_rows(S)] + [_ANY] * len(extra),
        out_shape=[jax.ShapeDtypeStruct((S, FOX_W), F32), jax.ShapeDtypeStruct((S, FOX_W), F32),
                   jax.ShapeDtypeStruct((S, FOX_W), BF16), jax.ShapeDtypeStruct((S, FOX_W), F32),
                   jax.ShapeDtypeStruct((n_pairs, 8, S), F32)]
        + (_exchange_out_shapes("scatter", *extra) if extra else []),
        scratch_shapes=[pltpu.VMEM((S, 128), BF16)] * 2 + [pltpu.VMEM((128, S), BF16)] * 2
        + [pltpu.VMEM((S, 128), BF16)] * 3 + [pltpu.VMEM((S, 128), F32)] * 5
        + [pltpu.VMEM((128, T), F32)] + [pltpu.VMEM((8, T), F32)] * 2
        + (_EXCHANGE_SEMS if extra else []),
        compiler_params=_cp(("arbitrary",), _VMEM_BIG),
    )(qs, kn, proj, dfo, fo, lse, cqb, crow4, *extra)


def _sb_bwd(proj, dso, ltot, tril):
    S = proj.shape[0]
    T = tril.shape[0]
    nq = S // T

    def body(q_ref, k_ref, v_ref, do_ref, lt_ref, tri_ref, dq_ref, dk_ref, dv_ref,
             qa, qb, k2, kta, ktb, vb, doa, dob, dka, dva, dqt, ra, rb, ga, gb):
        lane_s = _head_masks(S)
        q = (q_ref[...] * Q_SCALE).astype(BF16)
        zq = jnp.zeros_like(q)
        qa[...] = jnp.where(lane_s, q, zq)
        qb[...] = jnp.where(lane_s, zq, q)
        k2[...] = k_ref[...].astype(BF16)
        vb[...] = v_ref[...].astype(BF16)
        do = do_ref[...].astype(BF16)
        doa[...] = jnp.where(lane_s, do, zq)
        dob[...] = jnp.where(lane_s, zq, do)
        dka[...] = jnp.zeros((S, 128), F32)
        dva[...] = jnp.zeros((S, 128), F32)
        row_t = lax.broadcasted_iota(jnp.int32, (128, T), 0) < HEAD_DIM

        def prep(c, carry):
            c0 = pl.multiple_of(c * T, T)
            kt = k_ref[pl.ds(c0, T), :].T
            kta[:, pl.ds(c0, T)] = jnp.where(row_t, kt, 0.0).astype(BF16)
            ktb[:, pl.ds(c0, T)] = jnp.where(row_t, 0.0, kt).astype(BF16)
            return carry

        lax.fori_loop(0, nq, prep, 0)
        strict = (lax.broadcasted_iota(jnp.int32, (T, T), 0) < lax.broadcasted_iota(jnp.int32, (T, T), 1))

        heads = ((qa, kta, doa, ra, ga), (qb, ktb, dob, rb, gb))

        def kv(tiles, r0, lts):
            tri = tri_ref[...]
            c0s = [pl.multiple_of(j * T, T) for j, _ in tiles]
            ks = [k2[pl.ds(c0, T), :] for c0 in c0s]
            vs = [vb[pl.ds(c0, T), :] for c0 in c0s]
            qhs = [hd[0][pl.ds(r0, T), :] for hd in heads]
            dohs = [hd[2][pl.ds(r0, T), :] for hd in heads]
            zs = [[_dot_nt(k, qh) for k in ks] for qh in qhs]
            das = [[_dot_nt(v, doh) for v in vs] for doh in dohs]
            es, lbs = [], []
            for row in zs:
                erow, lrow = [], []
                for z, (_, masked) in zip(row, tiles):
                    e, sp = _softplus_parts(z)
                    erow.append(e)
                    lrow.append(jnp.where(strict, -sp, 0.0) if masked else -sp)
                es.append(erow)
                lbs.append(lrow)
            pres = [[_mm2(lb, tri, left=True) for lb in row] for row in lbs]
            aas, r_ends = [], []
            for hd, zrow, lrow, prow, lt in zip(heads, zs, lbs, pres, lts):
                r = hd[3][0:1, :]
                arow = []
                for z, lb, pre, (_, masked) in zip(zrow, lrow, prow, tiles):
                    a = jnp.exp(z + lb + ((lt - r) - pre))
                    arow.append(jnp.where(strict, a, 0.0) if masked else a)
                    r = r + pre[T - 1:T, :]
                aas.append(arow)
                r_ends.append(r)
            gs = [[a * da for a, da in zip(arow, drow)] for arow, drow in zip(aas, das)]
            gpres = [[_mm2(g, tri, left=True) for g in row] for row in gs]
            dzbs, g_ends = [], []
            for hd, zrow, erow, grow, gprow in zip(heads, zs, es, gs, gpres):
                gc = hd[4][0:1, :]
                drow = []
                for z, e, g, gpre, (_, masked) in zip(zrow, erow, grow, gprow, tiles):
                    inv = 1.0 / (1.0 + e)
                    pos = z >= 0.0
                    sig = jnp.where(pos, 1.0, e) * inv
                    oms = jnp.where(pos, e, 1.0) * inv
                    dz = g * oms - sig * (gc + (gpre - g))
                    if masked:
                        dz = jnp.where(strict, dz, 0.0)
                    drow.append(dz.astype(BF16))
                    gc = gc + gpre[T - 1:T, :]
                dzbs.append(drow)
                g_ends.append(gc)
            dq = None
            for h, hd in enumerate(heads):
                for t, c0 in enumerate(c0s):
                    term = _dot(hd[1][:, pl.ds(c0, T)], dzbs[h][t])
                    dq = term if dq is None else dq + term
            dqt[...] = dqt[...] + dq
            for t, c0 in enumerate(c0s):
                dka[pl.ds(c0, T), :] = dka[pl.ds(c0, T), :] + (_dot(dzbs[0][t], qhs[0]) + _dot(dzbs[1][t], qhs[1]))
                dva[pl.ds(c0, T), :] = dva[pl.ds(c0, T), :] + (_dot(aas[0][t].astype(BF16), dohs[0])
                                                               + _dot(aas[1][t].astype(BF16), dohs[1]))
            for hd, r, gc in zip(heads, r_ends, g_ends):
                hd[3][0:1, :] = r
                hd[4][0:1, :] = gc

        def qblk(i, carry):
            r0 = pl.multiple_of(i * T, T)
            lt = lt_ref[:, pl.ds(r0, T)]
            lts = (lt[0:1, :], lt[1:2, :])
            back = jnp.max(lt[2:3, :]).astype(jnp.int32)
            zt = jnp.zeros((8, T), F32)
            dqt[...] = jnp.zeros((128, T), F32)
            ra[...] = zt
            rb[...] = zt
            ga[...] = zt
            gb[...] = zt

            def inner(j, c):
                kv([(j, False)], r0, lts)
                return c

            @pl.when(back == 0)
            def _():
                kv([(i, True)], r0, lts)

            @pl.when(back > 0)
            def _():
                lax.fori_loop(i - back, i - 1, inner, 0)
                kv([(i - 1, False), (i, True)], r0, lts)
            dq_ref[pl.ds(r0, T), :] = (dqt[...] * Q_SCALE).T.astype(BF16)
            return carry

        lax.fori_loop(0, nq, qblk, 0)
        dk_ref[...] = dka[...].astype(BF16)
        dv_ref[...] = dva[...].astype(BF16)

    return pl.pallas_call(
        body, name="sb_bwd",
        grid=(SB_W // 128,),
        in_specs=[_pair_blk(S, OFF_SQ // 128), _pair_blk(S, OFF_SK // 128), _pair_blk(S, OFF_SV // 128),
                  _pair_blk(S), _pair_rows(S), pl.BlockSpec((T, T), lambda p: (0, 0))],
        out_specs=[_pair_blk(S), _pair_blk(S), _pair_blk(S)],
        out_shape=[jax.ShapeDtypeStruct((S, SB_W), BF16)] * 3,
        scratch_shapes=([pltpu.VMEM((S, 128), BF16)] * 3 + [pltpu.VMEM((128, S), BF16)] * 2
                        + [pltpu.VMEM((S, 128), BF16)] * 3 + [pltpu.VMEM((S, 128), F32)] * 2
                        + [pltpu.VMEM((128, T), F32)] + [pltpu.VMEM((8, T), F32)] * 4),
        compiler_params=_cp(("arbitrary",), _VMEM_BIG),
    )(proj, proj, proj, dso, ltot, tril)


def _head_norm_bwd(x, g, dy, bd):
    ss = _mm2(x * x, bd)
    r = lax.rsqrt(ss * (1.0 / HEAD_DIM) + EPS)
    xr = x * r
    gdy = g * dy
    m = _mm2(xr * gdy, bd) * (1.0 / HEAD_DIM)
    return r * (gdy - xr * m), dy * xr


def _qk_bwd(dqs, dkn, proj, pff, bfp, gq, gk, bd, dccol, triu):
    S = proj.shape[0]
    T = triu.shape[0]
    n = S // T
    rev = lambda col: (lambda i: (n - 1 - i, col))

    def body(dq_ref, dk_ref, q_ref, k_ref, ff_ref, b_ref, gq_ref, gk_ref, bd_ref, dc_ref, tri_ref,
             dfq_ref, dfk_ref, dff_ref, dgq_ref, dgk_ref, dbf_ref, carry):
        @pl.when(pl.program_id(0) == 0)
        def _():
            carry[...] = jnp.zeros_like(carry)
            dgq_ref[...] = jnp.zeros_like(dgq_ref)
            dgk_ref[...] = jnp.zeros_like(dgk_ref)
            dbf_ref[...] = jnp.zeros_like(dbf_ref)

        bdv = bd_ref[...]
        dxq, gq_rows = _head_norm_bwd(q_ref[...], gq_ref[...], dq_ref[...] * Q_SCALE, bdv)
        dfq_ref[...] = dxq.astype(BF16)
        dgq_ref[...] = dgq_ref[...] + jnp.sum(gq_rows, axis=0, keepdims=True)
        dxk, gk_rows = _head_norm_bwd(k_ref[...], gk_ref[...], dk_ref[...], bdv)
        dfk_ref[...] = dxk.astype(BF16)
        dgk_ref[...] = dgk_ref[...] + jnp.sum(gk_rows, axis=0, keepdims=True)
        dlf = _mm3(dc_ref[...], tri_ref[...], left=True) + carry[0:1, :]
        carry[0:1, :] = dlf[0:1, :]
        u = ff_ref[...] + b_ref[...]
        lane = lax.broadcasted_iota(jnp.int32, u.shape, 1)
        dff = jnp.where(lane < N_FF, dlf * _sigmoid(-u), 0.0)
        dff_ref[...] = dff.astype(BF16)
        dbf_ref[...] = dbf_ref[...] + jnp.sum(dff, axis=0, keepdims=True)

    return pl.pallas_call(
        body, name="qk_bwd",
        grid=(n,),
        in_specs=[pl.BlockSpec((T, FOX_W), rev(0)), pl.BlockSpec((T, FOX_W), rev(0)),
                  pl.BlockSpec((T, FOX_W), rev(OFF_FQ // FOX_W)), pl.BlockSpec((T, FOX_W), rev(OFF_FK // FOX_W)),
                  pl.BlockSpec((T, N_FFPAD), rev(0)),
                  pl.BlockSpec((1, N_FFPAD), lambda i: (0, 0)),
                  pl.BlockSpec((1, FOX_W), lambda i: (0, 0)), pl.BlockSpec((1, FOX_W), lambda i: (0, 0)),
                  pl.BlockSpec((FOX_W, FOX_W), lambda i: (0, 0)),
                  pl.BlockSpec((T, N_FFPAD), rev(0)),
                  pl.BlockSpec((T, T), lambda i: (0, 0))],
        out_specs=[pl.BlockSpec((T, FOX_W), rev(0)), pl.BlockSpec((T, FOX_W), rev(0)),
                   pl.BlockSpec((T, N_FFPAD), rev(0)),
                   pl.BlockSpec((1, FOX_W), lambda i: (0, 0)), pl.BlockSpec((1, FOX_W), lambda i: (0, 0)),
                   pl.BlockSpec((1, N_FFPAD), lambda i: (0, 0))],
        out_shape=[jax.ShapeDtypeStruct((S, FOX_W), BF16), jax.ShapeDtypeStruct((S, FOX_W), BF16),
                   jax.ShapeDtypeStruct((S, N_FFPAD), BF16),
                   jax.ShapeDtypeStruct((1, FOX_W), F32), jax.ShapeDtypeStruct((1, FOX_W), F32),
                   jax.ShapeDtypeStruct((1, N_FFPAD), F32)],
        scratch_shapes=[pltpu.VMEM((8, N_FFPAD), F32)],
        compiler_params=_cp(("arbitrary",), 40 << 20),
    )(dqs, dkn, proj, proj, pff, bfp, gq, gk, bd, dccol, triu)


def _dproj_layout(pieces):
    offs, o = [], 0
    for p in pieces:
        offs.append(o)
        o += p.shape[1]
    assert o == N_MAIN
    return offs


def _inproj_bwd_dx(pieces, dff, wm, wff, x, g, dy, ride=None):
    S, D = x.shape
    tm = min(_TM_DX, S)
    steps = S // tm
    offs = _dproj_layout(pieces)
    n = len(pieces)

    def body(*refs):
        p_refs = refs[:n]
        if ride is None:
            dff_ref, w_ref, wff_ref, x_ref, g_ref, dy_ref, dx_ref, dg_ref = refs[n:]
        else:
            dff_ref, w_ref, wff_ref, x_ref, g_ref, dy_ref, pa_ref, pb_ref = refs[n:n + 8]
            dx_ref, dg_ref, ra_ref, rb_ref = refs[n + 8:n + 12]
            xrefs = (pa_ref, pb_ref, ra_ref, rb_ref) + tuple(refs[n + 12:])

        @pl.when(pl.program_id(0) == 0)
        def _():
            dg_ref[...] = jnp.zeros_like(dg_ref)
            if ride is not None:
                _start_exchange("scatter", *xrefs)

        dh = _dot_nt(dff_ref[...], wff_ref[...])
        for p_ref, off in zip(p_refs, offs):
            dh = dh + _dot_nt(p_ref[...], w_ref[:, off:off + p_ref.shape[1]])
        xv = x_ref[...]
        r = _rms_rows(xv)
        xr = xv * r
        dg_ref[...] = dg_ref[...] + jnp.sum(dh * xr, axis=0, keepdims=True)
        gdh = g_ref[...] * dh
        m = jnp.mean(gdh * xr, axis=-1, keepdims=True)
        dx_ref[...] = dy_ref[...] + r * (gdh - xr * m)
        if ride is not None:
            @pl.when(pl.program_id(0) == steps - 1)
            def _():
                _wait_exchange("scatter", *xrefs)

    extra = () if ride is None else tuple(ride)
    return pl.pallas_call(
        body, name="inproj_bwd_dx" if ride is None else "inproj_bwd_dx_exchange",
        grid=(steps,),
        in_specs=[pl.BlockSpec((tm, p.shape[1]), lambda i: (i, 0)) for p in pieces]
        + [pl.BlockSpec((tm, N_FFPAD), lambda i: (i, 0)),
                  pl.BlockSpec((D, N_MAIN), lambda i: (0, 0)),
                  pl.BlockSpec((D, N_FFPAD), lambda i: (0, 0)),
                  pl.BlockSpec((tm, D), lambda i: (i, 0)),
                  pl.BlockSpec((1, D), lambda i: (0, 0)),
                  pl.BlockSpec((tm, D), lambda i: (i, 0))] + [_ANY] * len(extra),
        out_specs=[pl.BlockSpec((tm, D), lambda i: (i, 0)), pl.BlockSpec((1, D), lambda i: (0, 0))] + [_ANY] * len(extra),
        out_shape=[jax.ShapeDtypeStruct((S, D), F32), jax.ShapeDtypeStruct((1, D), F32)]
        + (_exchange_out_shapes("scatter", *extra) if extra else []),
        scratch_shapes=_EXCHANGE_SEMS if extra else [],
        compiler_params=_cp(("arbitrary",), 48 << 20),
    )(*pieces, dff, wm, wff, x, g, dy, *extra)


def _inproj_bwd_dw(ht, pieces, dff):
    D, S = ht.shape
    tk = min(_TK_DW, S)
    nk = S // tk
    offs = _dproj_layout(pieces)
    n = len(pieces)

    def body(*refs):
        ht_ref, p_refs, dff_ref = refs[0], refs[1:1 + n], refs[1 + n]
        dw_ref, dwff_ref, acc, accff = refs[2 + n:]
        k = pl.program_id(0)

        @pl.when(k == 0)
        def _():
            acc[...] = jnp.zeros_like(acc)
            accff[...] = jnp.zeros_like(accff)

        hb = ht_ref[...]
        for p_ref, off in zip(p_refs, offs):
            w = p_ref.shape[1]
            acc[:, off:off + w] = acc[:, off:off + w] + _dot(hb, p_ref[...])
        accff[...] = accff[...] + _dot(hb, dff_ref[...])

        @pl.when(k == nk - 1)
        def _():
            dw_ref[...] = acc[...].astype(BF16)
            dwff_ref[...] = accff[...].astype(BF16)

    return pl.pallas_call(
        body, name="inproj_bwd_dw",
        grid=(nk,),
        in_specs=[pl.BlockSpec((D, tk), lambda k: (0, k))]
        + [pl.BlockSpec((tk, p.shape[1]), lambda k: (k, 0)) for p in pieces]
        + [pl.BlockSpec((tk, N_FFPAD), lambda k: (k, 0))],
        out_specs=[pl.BlockSpec((D, N_MAIN), lambda k: (0, 0), pipeline_mode=pl.Buffered(1)),
                   pl.BlockSpec((D, N_FFPAD), lambda k: (0, 0), pipeline_mode=pl.Buffered(1))],
        out_shape=[jax.ShapeDtypeStruct((D, N_MAIN), BF16), jax.ShapeDtypeStruct((D, N_FFPAD), BF16)],
        scratch_shapes=[pltpu.VMEM((D, N_MAIN), F32), pltpu.VMEM((D, N_FFPAD), F32)],
        compiler_params=_cp(("arbitrary",), _VMEM_BIG),
    )(ht, *pieces, dff)


def _constants(T):
    tril = jnp.tril(jnp.ones((T, T), F32)).astype(BF16)
    hid = jnp.arange(FOX_W) // HEAD_DIM
    bd = (hid[:, None] == hid[None, :]).astype(BF16)
    ex = (jnp.arange(N_FFPAD)[:, None] == hid[None, :]).astype(BF16)
    return tril, tril.T, bd, ex


def _crow4(ccol, T):
    S = ccol.shape[0]
    c = ccol[:, :FOX_HEADS].T
    last = jnp.pad(c[:, T - 1::T], ((0, 0), (0, S - S // T)))
    rows = jnp.concatenate([c.reshape(FOX_HEADS // 2, 2, S), last.reshape(FOX_HEADS // 2, 2, S)], axis=1)
    return jnp.pad(rows, ((0, 0), (0, 4), (0, 0)))


def _layer_fwd(x, lw, consts, ride=None):
    tril, triu, bd, ex = consts
    proj, pff, ht = _inproj_fwd(x, lw["g"], lw["wm"], lw["wff"])
    qs, kn, ccol, cqb = _fox_prep(proj, pff, lw["bfp"], lw["gq"], lw["gk"], bd, ex, tril)
    crow4 = _crow4(ccol, tril.shape[0])
    fo, lse, *gathered = _fox_fwd(qs, kn, proj, cqb, crow4, ride)
    so, ltot = _sb_fwd(proj, triu)
    pooled = _pool_fwd(proj)
    y, mixedt = _mix_out(fo, so, pooled, proj, lw["wbd"], lw["scale"], lw["wout"], x)
    return y, (x, proj, pff, ht, qs, kn, cqb, crow4, fo, lse, so, ltot, pooled, mixedt), gathered


def _layer_bwd(dy, saved, lw, consts, ride=None, exchange_own=False):
    tril, triu, bd, _ = consts
    x, proj, pff, ht, qs, kn, cqb, crow4, fo, lse, so, ltot, pooled, mixedt = saved
    S = x.shape[0]
    dfo, dfg, dso, dsg, dpg, dpooled, dscale, dwbd = _gate_bwd(dy, lw["wout"], fo, so, pooled, proj, lw["wbd"], lw["scale"])
    dwout = _matmul_acc(mixedt, dy, "dw_out")
    dpx = _pool_bwd(dpooled)
    dqs, dkn, dfv, dck, dcq4, *received = _fox_bwd(qs, kn, proj, dfo, fo, lse, cqb, crow4, ride)
    dsq, dsk, dsv = _sb_bwd(proj, dso, ltot, tril)
    dc8 = dck[:, ::HEAD_DIM] + dcq4[:, :2, :].reshape(FOX_HEADS, S).T
    dccol = jnp.pad(dc8, ((0, 0), (0, N_FFPAD - FOX_HEADS)))
    dfq, dfk, dff, dgq, dgk, dbf = _qk_bwd(dqs, dkn, proj, pff, lw["bfp"], lw["gq"], lw["gk"], bd, dccol, triu)
    pieces = [dfq, dfk, dfv, dfg, dpx, dpg, dsq, dsk, dsv, dsg]
    dwm, dwff = _inproj_bwd_dw(ht, pieces, dff)
    dwin = jnp.concatenate([dwm[:, :OFF_PX], dwff[:, :N_FF], dwm[:, OFF_PX:]], axis=1)
    own = _grad_parts({"w_in": dwin, "w_out": dwout}) if exchange_own else None
    dx, dng, *received_own = _inproj_bwd_dx(pieces, dff, lw["wm"], lw["wff"], x, lw["g"], dy, own)
    grads = {
        "norm_g": dng[0],
        "w_in": dwin,
        "b_f": dbf[0, :N_FF],
        "q_norm_g": dgq[0].reshape(FOX_HEADS, HEAD_DIM).sum(0),
        "k_norm_g": dgk[0].reshape(FOX_HEADS, HEAD_DIM).sum(0),
        "w_pool": jnp.stack([dwbd[64 * i:64 * i + 64, 64 * i:64 * i + 64] for i in range(4)]),
        "pool_scale": dscale[0],
        "w_out": dwout,
    }
    return dx, grads, received, received_own


def _layer_weights(l, norm_g, gin, b_f, q_norm_g, k_norm_g, w_pool, pool_scale, gout):
    D = gin.shape[1]
    w = gin.transpose(1, 0, 2).reshape(D, D_IN)
    wm = jnp.concatenate([w[:, :2048], w[:, 2048 + N_FF:]], axis=1)
    wff = jnp.pad(w[:, 2048:2048 + N_FF], ((0, 0), (0, N_FFPAD - N_FF)))
    grp = jnp.arange(POOL_W) // 64
    wbd = jnp.where(grp[:, None] == grp[None, :], jnp.tile(w_pool[l].transpose(1, 0, 2).reshape(64, POOL_W), (4, 1)), 0.0)
    return {
        "g": norm_g[l].reshape(1, D),
        "wm": wm, "wff": wff,
        "bfp": jnp.pad(b_f[l], (0, N_FFPAD - N_FF)).reshape(1, N_FFPAD),
        "gq": jnp.tile(q_norm_g[l], FOX_HEADS).reshape(1, FOX_W),
        "gk": jnp.tile(k_norm_g[l], FOX_HEADS).reshape(1, FOX_W),
        "wbd": wbd.astype(BF16),
        "scale": pool_scale[l].reshape(1, POOL_W),
        "wout": gout.reshape(D_MIX, D),
    }


def _grad_parts(g):
    dwin, dwout = g["w_in"].astype(BF16), g["w_out"].astype(BF16)
    D = dwin.shape[0]
    return (dwin.reshape(D, N_DEV, D_IN // N_DEV).transpose(1, 0, 2),
            dwout.reshape(N_DEV, D_MIX // N_DEV, dwout.shape[1]))


def _train_step(x, target, norm_g, win_sh, b_f, q_norm_g, k_norm_g, w_pool, pool_scale, wout_sh):
    L = norm_g.shape[0]
    consts = _constants(min(_T, x.shape[0]))
    gathered = _exchange_pair("gather", win_sh[0], wout_sh[0], "gather_weights")
    lws, saved = [], []
    h = x
    for l in range(L):
        lws.append(_layer_weights(l, norm_g, gathered[0], b_f, q_norm_g, k_norm_g, w_pool, pool_scale, gathered[1]))
        ride = (win_sh[l + 1], wout_sh[l + 1]) if l + 1 < L else None
        h, sv, gathered = _layer_fwd(h, lws[l], consts, ride)
        saved.append(sv)
    dy, loss = _loss_head(h, target)
    grads, received = [None] * L, [None] * L
    ride = None
    for l in reversed(range(L)):
        dy, grads[l], got, got_own = _layer_bwd(dy, saved[l], lws[l], consts, ride, exchange_own=(l == 0))
        if ride is not None:
            received[l + 1] = got
        if l == 0:
            received[0] = got_own
        else:
            ride = _grad_parts(grads[l])
    return loss, dy, grads, received


def _mesh_pos():
    return lax.axis_index("x"), lax.axis_index("y"), lax.axis_index("c")


_FLIPS = [(0, 0, 1), (1, 0, 0), (0, 1, 0), (1, 1, 0), (1, 0, 1), (0, 1, 1), (1, 1, 1)]


def _peers():
    x, y, c = _mesh_pos()
    out = []
    for fx, fy, fc in _FLIPS:
        px = 1 - x if fx else x
        py = 1 - y if fy else y
        pc = 1 - c if fc else c
        out.append(((px, py, pc), 4 * px + 2 * py + pc))
    return out, 4 * x + 2 * y + c


_EXCHANGE_SEMS = [pltpu.SemaphoreType.DMA((14,)), pltpu.SemaphoreType.DMA((14,)), pltpu.SemaphoreType.DMA((2,))]
_ANY = pl.BlockSpec(memory_space=pl.ANY)


def _exchange_copies(kind, a_ref, b_ref, oa_ref, ob_ref, send_sems, recv_sems, loc_sems):
    peers, me = _peers()
    pairs = ((a_ref, oa_ref), (b_ref, ob_ref))
    local = [pltpu.make_async_copy(src if kind == "gather" else src.at[me], dst.at[me], loc_sems.at[t])
             for t, (src, dst) in enumerate(pairs)]
    remote = []
    for k, (dev, idx) in enumerate(peers):
        for t, (src, dst) in enumerate(pairs):
            remote.append(pltpu.make_async_remote_copy(
                src_ref=src if kind == "gather" else src.at[idx], dst_ref=dst.at[me],
                send_sem=send_sems.at[2 * k + t], recv_sem=recv_sems.at[2 * k + t],
                device_id=dev, device_id_type=pl.DeviceIdType.MESH))
    return local, remote


def _start_exchange(kind, *refs):
    local, remote = _exchange_copies(kind, *refs)
    for cp in local + remote:
        cp.start()


def _wait_exchange(kind, *refs):
    local, remote = _exchange_copies(kind, *refs)
    for cp in remote:
        cp.wait_recv()
    for cp in remote:
        cp.wait_send()
    for cp in local:
        cp.wait()


def _exchange_out_shapes(kind, a, b):
    if kind == "gather":
        return [jax.ShapeDtypeStruct((N_DEV,) + a.shape, a.dtype), jax.ShapeDtypeStruct((N_DEV,) + b.shape, b.dtype)]
    return [jax.ShapeDtypeStruct(a.shape, a.dtype), jax.ShapeDtypeStruct(b.shape, b.dtype)]


def _exchange_pair(kind, a, b, name):
    def body(*refs):
        _start_exchange(kind, *refs)
        _wait_exchange(kind, *refs)

    return pl.pallas_call(
        body, name=name,
        in_specs=[_ANY, _ANY], out_specs=[_ANY, _ANY],
        out_shape=_exchange_out_shapes(kind, a, b),
        scratch_shapes=_EXCHANGE_SEMS,
    )(a, b)


def _adam_math(w, g, m, v):
    m_new = ADAM_B1 * m + (1.0 - ADAM_B1) * g
    v_new = ADAM_B2 * v + (1.0 - ADAM_B2) * (g * g)
    m_hat = m_new / (1.0 - ADAM_B1 ** ADAM_STEP)
    v_hat = v_new / (1.0 - ADAM_B2 ** ADAM_STEP)
    delta = -ADAM_LR * (m_hat / (jnp.sqrt(v_hat) + ADAM_EPS) + ADAM_WD * w)
    return delta, m_new, v_new


def _sum_adamw(gparts, w, m, v, name):
    L, R, C = w.shape
    tr = min(128, R)

    def body(*refs):
        gp_refs = refs[:L]
        w_ref, m_ref, v_ref, g_ref, d_ref, nm_ref, nv_ref = refs[L:]
        for l in range(L):
            g = gp_refs[l][0].astype(F32)
            for s in range(1, N_DEV):
                g = g + gp_refs[l][s].astype(F32)
            d, mn, vn = _adam_math(w_ref[l], g, m_ref[l], v_ref[l])
            g_ref[l] = g
            d_ref[l] = d
            nm_ref[l] = mn
            nv_ref[l] = vn

    blk = pl.BlockSpec((L, tr, C), lambda r: (0, r, 0))
    return pl.pallas_call(
        body, name=name,
        grid=(R // tr,),
        in_specs=[pl.BlockSpec((N_DEV, tr, C), lambda r: (0, r, 0))] * L + [blk, blk, blk],
        out_specs=[blk, blk, blk, blk],
        out_shape=[jax.ShapeDtypeStruct((L, R, C), F32)] * 4,
        compiler_params=_cp(("parallel",), 48 << 20),
    )(*gparts, w, m, v)


def _small_update(gpack, wpack, mpack, vpack):
    R = gpack.shape[0]
    VM = pl.BlockSpec(memory_space=pltpu.VMEM)

    def body(g_ref, w_ref, m_ref, v_ref, gs_ref, d_ref, nm_ref, nv_ref, buf, send_sems, recv_sems):
        peers, me = _peers()
        buf[me] = g_ref[...]
        copies = []
        for k, (dev, _) in enumerate(peers):
            cp = pltpu.make_async_remote_copy(
                src_ref=g_ref, dst_ref=buf.at[me], send_sem=send_sems.at[k], recv_sem=recv_sems.at[k],
                device_id=dev, device_id_type=pl.DeviceIdType.MESH)
            cp.start()
            copies.append(cp)
        for cp in copies:
            cp.wait_recv()
        for cp in copies:
            cp.wait_send()
        g = buf[0]
        for s in range(1, N_DEV):
            g = g + buf[s]
        d, mn, vn = _adam_math(w_ref[...], g, m_ref[...], v_ref[...])
        gs_ref[...] = g
        d_ref[...] = d
        nm_ref[...] = mn
        nv_ref[...] = vn

    return pl.pallas_call(
        body, name="small_update",
        in_specs=[VM] * 4, out_specs=[VM] * 4,
        out_shape=[jax.ShapeDtypeStruct((R, 128), F32)] * 4,
        scratch_shapes=[pltpu.VMEM((N_DEV, R, 128), F32), pltpu.SemaphoreType.DMA((7,)), pltpu.SemaphoreType.DMA((7,))],
        compiler_params=_cp(None, 40 << 20),
    )(gpack, wpack, mpack, vpack)


_SMALL = ("norm_g", "b_f", "q_norm_g", "k_norm_g", "w_pool", "pool_scale")


def _pack(parts):
    flat = jnp.concatenate([p.reshape(-1) for p in parts])
    n = flat.shape[0]
    rows = -(-n // (8 * 128)) * 8
    return jnp.pad(flat, (0, rows * 128 - n)).reshape(rows, 128)


def _unpack(packed, like):
    flat = packed.reshape(-1)
    out, o = [], 0
    for p in like:
        out.append(flat[o:o + p.size].reshape(p.shape))
        o += p.size
    return out


def kernel(x, norm_g, w_in, b_f, q_norm_g, k_norm_g, w_pool, pool_scale, w_out, loss_target, m_norm_g, m_w_in, m_b_f, m_q_norm_g, m_k_norm_g, m_w_pool, m_pool_scale, m_w_out, v_norm_g, v_w_in, v_b_f, v_q_norm_g, v_k_norm_g, v_w_pool, v_pool_scale, v_w_out):
    L = w_in.shape[0]

    loss_local, dx, grads, received = _train_step(x[0], loss_target[0], norm_g, w_in.astype(BF16), b_f, q_norm_g,
                                                  k_norm_g, w_pool, pool_scale, w_out.astype(BF16))
    loss = lax.psum(loss_local, MESH_AXES)
    g = {k: jnp.stack([grads[l][k] for l in range(L)]) for k in _SMALL}

    g_win, d_win, nm_win, nv_win = _sum_adamw([r[0] for r in received], w_in, m_w_in, v_w_in, "adamw_w_in")
    g_wout, d_wout, nm_wout, nv_wout = _sum_adamw([r[1] for r in received], w_out, m_w_out, v_w_out, "adamw_w_out")

    ws = dict(norm_g=norm_g, b_f=b_f, q_norm_g=q_norm_g, k_norm_g=k_norm_g, w_pool=w_pool, pool_scale=pool_scale)
    ms = dict(norm_g=m_norm_g, b_f=m_b_f, q_norm_g=m_q_norm_g, k_norm_g=m_k_norm_g, w_pool=m_w_pool, pool_scale=m_pool_scale)
    vs = dict(norm_g=v_norm_g, b_f=v_b_f, q_norm_g=v_q_norm_g, k_norm_g=v_k_norm_g, w_pool=v_w_pool, pool_scale=v_pool_scale)
    like = [ws[k] for k in _SMALL]
    gs_p, d_p, nm_p, nv_p = _small_update(_pack([g[k] for k in _SMALL]), _pack(like),
                                          _pack([ms[k] for k in _SMALL]), _pack([vs[k] for k in _SMALL]))
    gs = dict(zip(_SMALL, _unpack(gs_p, like)))
    ds = dict(zip(_SMALL, _unpack(d_p, like)))
    nms = dict(zip(_SMALL, _unpack(nm_p, like)))
    nvs = dict(zip(_SMALL, _unpack(nv_p, like)))
    gs["w_in"], ds["w_in"], nms["w_in"], nvs["w_in"] = g_win, d_win, nm_win, nv_win
    gs["w_out"], ds["w_out"], nms["w_out"], nvs["w_out"] = g_wout, d_wout, nm_wout, nv_wout

    order = ("norm_g", "w_in", "b_f", "q_norm_g", "k_norm_g", "w_pool", "pool_scale", "w_out")
    return (loss, dx[None], *[gs[k] for k in order], *[ds[k] for k in order],
            *[nms[k] for k in order], *[nvs[k] for k in order])
```

```python
import jax
import jax.numpy as jnp
from jax import lax
from jax.experimental import pallas as pl
from jax.experimental.pallas import tpu as pltpu

F32 = jnp.float32
BF16 = jnp.bfloat16

EPS = 1e-6
NEG = -1e30
HEAD_DIM = 64
FOX_HEADS = 8
FOX_W = 512
POOL_W = 256
SB_W = 256
D_MIX = 1024
N_FF = 8
N_MAIN = 3584
N_FFPAD = 128
OFF_FQ, OFF_FK, OFF_FV, OFF_FG = 0, 512, 1024, 1536
OFF_PX, OFF_PG = 2048, 2304
OFF_SQ, OFF_SK, OFF_SV, OFF_SG = 2560, 2816, 3072, 3328
D_IN = 3592
Q_SCALE = HEAD_DIM ** -0.5

ADAM_LR = 0.001
ADAM_B1 = 0.9
ADAM_B2 = 0.999
ADAM_EPS = 1e-08
ADAM_WD = 0.01
ADAM_STEP = 10

N_DEV = 8
MESH_AXES = ("x", "y", "c")

_T = 256
_TM = 512
_TM_ROWS = 256
_TM_FWD, _TN_FWD = 2048, 512
_TM_DX = 512
_TK_DW = 1024
_VMEM_V7X = 64 << 20
_VMEM_BIG = _VMEM_V7X - (8 << 20)
_VMEM_MID = 40 << 20
_VMEM_WIDE = 48 << 20


def _cp(sem=None, vmem=None):
    kw = {}
    if sem is not None:
        kw["dimension_semantics"] = sem
    if vmem is not None:
        kw["vmem_limit_bytes"] = vmem
    return pltpu.CompilerParams(**kw)


def _dot(a, b):
    return jnp.dot(a, b, preferred_element_type=F32)


def _dot_nt(a, b):
    return lax.dot_general(a, b, (((1,), (1,)), ((), ())), preferred_element_type=F32)


def _dot_tn(a, b):
    return lax.dot_general(a, b, (((0,), (0,)), ((), ())), preferred_element_type=F32)


def _mm2(v, m, left=False):
    hi = v.astype(BF16)
    lo = (v - hi.astype(F32)).astype(BF16)
    if left:
        return _dot(m, hi) + _dot(m, lo)
    return _dot(hi, m) + _dot(lo, m)


def _mm3(v, m, left=False):
    a1 = v.astype(BF16)
    r1 = v - a1.astype(F32)
    a2 = r1.astype(BF16)
    a3 = (r1 - a2.astype(F32)).astype(BF16)
    if left:
        return _dot(m, a1) + _dot(m, a2) + _dot(m, a3)
    return _dot(a1, m) + _dot(a2, m) + _dot(a3, m)


def _sigmoid(z):
    return 1.0 / (1.0 + jnp.exp(-z))


def _rms_rows(x):
    return lax.rsqrt(jnp.mean(x * x, axis=-1, keepdims=True) + EPS)


def _inproj_fwd(x, g, wm, wff):
    S, D = x.shape
    tm = min(_TM_FWD, S)
    tn = _TN_FWD

    def body(x_ref, g_ref, w_ref, wff_ref, o_ref, off_ref, ht_ref, h_ref):
        @pl.when(pl.program_id(1) == 0)
        def _():
            xv = x_ref[...]
            h = (xv * _rms_rows(xv)) * g_ref[...]
            h_ref[...] = h.astype(BF16)
            ht_ref[...] = h.T.astype(BF16)
            off_ref[...] = _dot(h_ref[...], wff_ref[...])

        o_ref[...] = _dot(h_ref[...], w_ref[...])

    return pl.pallas_call(
        body, name="inproj_fwd",
        grid=(S // tm, N_MAIN // tn),
        in_specs=[pl.BlockSpec((tm, D), lambda i, j: (i, 0)),
                  pl.BlockSpec((1, D), lambda i, j: (0, 0)),
                  pl.BlockSpec((D, tn), lambda i, j: (0, j)),
                  pl.BlockSpec((D, N_FFPAD), lambda i, j: (0, 0))],
        out_specs=[pl.BlockSpec((tm, tn), lambda i, j: (i, j)),
                   pl.BlockSpec((tm, N_FFPAD), lambda i, j: (i, 0)),
                   pl.BlockSpec((D, tm), lambda i, j: (0, i))],
        out_shape=[jax.ShapeDtypeStruct((S, N_MAIN), F32), jax.ShapeDtypeStruct((S, N_FFPAD), F32),
                   jax.ShapeDtypeStruct((D, S), BF16)],
        scratch_shapes=[pltpu.VMEM((tm, D), BF16)],
        compiler_params=_cp(("parallel", "arbitrary"), _VMEM_WIDE),
    )(x, g, wm, wff)


def _head_norm(x, g, bd):
    ss = _mm2(x * x, bd)
    r = lax.rsqrt(ss * (1.0 / HEAD_DIM) + EPS)
    return (x * r) * g


def _fox_prep(proj, pff, bfp, gq, gk, bd, ex, tril):
    S = proj.shape[0]
    T = tril.shape[0]

    def body(q_ref, k_ref, ff_ref, b_ref, gq_ref, gk_ref, bd_ref, ex_ref, tri_ref,
             qs_ref, kn_ref, cc_ref, cqb_ref, carry):
        @pl.when(pl.program_id(0) == 0)
        def _():
            carry[...] = jnp.zeros_like(carry)

        bdv = bd_ref[...]
        qs_ref[...] = (_head_norm(q_ref[...], gq_ref[...], bdv) * Q_SCALE).astype(BF16)
        kn_ref[...] = _head_norm(k_ref[...], gk_ref[...], bdv).astype(BF16)
        u = ff_ref[...] + b_ref[...]
        lf = jnp.minimum(u, 0.0) - jnp.log1p(jnp.exp(-jnp.abs(u)))
        c = _mm3(lf, tri_ref[...], left=True) + carry[0:1, :]
        carry[0:1, :] = c[T - 1:T, :]
        cc_ref[...] = c
        cqb_ref[...] = _mm3(c, ex_ref[...])

    return pl.pallas_call(
        body, name="fox_prep",
        grid=(S // T,),
        in_specs=[pl.BlockSpec((T, FOX_W), lambda i: (i, OFF_FQ // FOX_W)),
                  pl.BlockSpec((T, FOX_W), lambda i: (i, OFF_FK // FOX_W)),
                  pl.BlockSpec((T, N_FFPAD), lambda i: (i, 0)),
                  pl.BlockSpec((1, N_FFPAD), lambda i: (0, 0)),
                  pl.BlockSpec((1, FOX_W), lambda i: (0, 0)),
                  pl.BlockSpec((1, FOX_W), lambda i: (0, 0)),
                  pl.BlockSpec((FOX_W, FOX_W), lambda i: (0, 0)),
                  pl.BlockSpec((N_FFPAD, FOX_W), lambda i: (0, 0)),
                  pl.BlockSpec((T, T), lambda i: (0, 0))],
        out_specs=[pl.BlockSpec((T, FOX_W), lambda i: (i, 0)),
                   pl.BlockSpec((T, FOX_W), lambda i: (i, 0)),
                   pl.BlockSpec((T, N_FFPAD), lambda i: (i, 0)),
                   pl.BlockSpec((T, FOX_W), lambda i: (i, 0))],
        out_shape=[jax.ShapeDtypeStruct((S, FOX_W), BF16), jax.ShapeDtypeStruct((S, FOX_W), BF16),
                   jax.ShapeDtypeStruct((S, N_FFPAD), F32), jax.ShapeDtypeStruct((S, FOX_W), F32)],
        scratch_shapes=[pltpu.VMEM((8, N_FFPAD), F32)],
        compiler_params=_cp(("arbitrary",), _VMEM_MID),
    )(proj, proj, pff, bfp, gq, gk, bd, ex, tril)


def _pair_blk(S, off=0):
    return pl.BlockSpec((S, 128), lambda p: (0, off + p), pipeline_mode=pl.Buffered(1))


def _pair_rows(S):
    return pl.BlockSpec((None, 8, S), lambda p: (p, 0, 0), pipeline_mode=pl.Buffered(1))


def _head_masks(S):
    return lax.broadcasted_iota(jnp.int32, (S, 128), 1) < HEAD_DIM


_EXP_ZERO = 104.0


def _spread_heads(x):
    src = lax.broadcasted_iota(jnp.int32, (128, 128), 0)
    return (_mm3(x, (src == 0).astype(BF16)), _mm3(x, (src == HEAD_DIM).astype(BF16)))


def _score_bounds(q, k):
    same_head = ((lax.broadcasted_iota(jnp.int32, (128, 128), 0) < HEAD_DIM)
                 == (lax.broadcasted_iota(jnp.int32, (128, 128), 1) < HEAD_DIM)).astype(BF16)

    def max_norm2(x):
        xf = x.astype(F32)
        return jnp.max(_mm2(xf * xf, same_head), axis=0, keepdims=True)

    z = jnp.sqrt(max_norm2(q) * max_norm2(k))
    return jnp.max(z[:, 0:1]) * 1.001 + 1e-3, jnp.max(z[:, 64:65]) * 1.001 + 1e-3


def _for_tiles_back(i, n, tiles_fn, fours=False):
    if fours:
        def four(t, c):
            tiles_fn([i - 1 - 4 * t, i - 2 - 4 * t, i - 3 - 4 * t, i - 4 - 4 * t])
            return c

        lax.fori_loop(0, lax.shift_right_logical(n, 2), four, 0)
        rest = i - (n & ~3)

        @pl.when((n & 2) != 0)
        def _():
            tiles_fn([rest - 1, rest - 2])
    else:
        def two(t, c):
            tiles_fn([i - 1 - 2 * t, i - 2 - 2 * t])
            return c

        lax.fori_loop(0, lax.shift_right_logical(n, 1), two, 0)

    @pl.when((n & 1) != 0)
    def _():
        tiles_fn([i - n])


def _fox_tiles_back(cr_ref, i, r0, zba, zbb):
    last = cr_ref[:, pl.ds(0, 128)]
    first = cr_ref[:, pl.ds(r0, 128)]
    alive_a = 2.0 * zba + first[0:1, 0:1] - last[2:3, :] > -_EXP_ZERO
    alive_b = 2.0 * zbb + first[1:2, 0:1] - last[3:4, :] > -_EXP_ZERO
    before = lax.broadcasted_iota(jnp.int32, (1, 128), 1) < i
    return jnp.sum((before & (alive_a | alive_b)).astype(jnp.int32))


def _fox_fwd(qs, kn, proj, cqb, crow4, ride=None):
    S = qs.shape[0]
    T = min(_T, S)
    nq = S // T
    n_pairs = FOX_W // 128

    def body(*refs):
        if ride is None:
            q_ref, k_ref, v_ref, cq_ref, cr_ref, o_ref, lse_ref = refs[:7]
            qa, qb, vta, vtb, cka, ckb, ma, mb, acca, accb = refs[7:]
        else:
            q_ref, k_ref, v_ref, cq_ref, cr_ref, wa_ref, wb_ref, o_ref, lse_ref, ga_ref, gb_ref = refs[:11]
            qa, qb, vta, vtb, cka, ckb, ma, mb, acca, accb = refs[11:21]
            xrefs = (wa_ref, wb_ref, ga_ref, gb_ref) + tuple(refs[21:])

            @pl.when(pl.program_id(0) == 0)
            def _():
                _start_exchange("gather", *xrefs)

        lane_s = _head_masks(S)
        q = q_ref[...]
        zq = jnp.zeros_like(q)
        qa[...] = jnp.where(lane_s, q, zq)
        qb[...] = jnp.where(lane_s, zq, q)
        cq = cq_ref[...]
        cka[...], ckb[...] = _spread_heads(cq)
        lse_ref[...] = jnp.zeros((8, S), F32)
        row_t = lax.broadcasted_iota(jnp.int32, (128, T), 0) < HEAD_DIM
        zba, zbb = _score_bounds(q, k_ref[...])

        def prep(c, carry):
            c0 = pl.multiple_of(c * T, T)
            vt = v_ref[pl.ds(c0, T), :].T
            vta[:, pl.ds(c0, T)] = jnp.where(row_t, vt, 1.0).astype(BF16)
            vtb[:, pl.ds(c0, T)] = jnp.where(row_t, 1.0, vt).astype(BF16)
            return carry

        lax.fori_loop(0, nq, prep, 0)
        causal = (lax.broadcasted_iota(jnp.int32, (T, T), 0) <= lax.broadcasted_iota(jnp.int32, (T, T), 1))

        heads = ((qa, vta, cka, ma, acca), (qb, vtb, ckb, mb, accb))

        def kv(js, r0, masked):
            cr = cr_ref[:, pl.ds(r0, T)]
            c0s = [pl.multiple_of(j * T, T) for j in js]
            ks = [k_ref[pl.ds(c0, T), :] for c0 in c0s]
            ss = []
            for h, (qr, _, ckr, _, _) in enumerate(heads):
                qh = qr[pl.ds(r0, T), :]
                row = []
                for k, c0 in zip(ks, c0s):
                    s = _dot_nt(k, qh) + cr[h:h + 1, :] - jnp.tile(ckr[pl.ds(c0, T), :], (1, T // 128))
                    row.append(jnp.where(causal, s, NEG) if masked else s)
                ss.append(row)
            ms = []
            for row, (_, _, _, mr, _) in zip(ss, heads):
                top = row[0]
                for s in row[1:]:
                    top = jnp.maximum(top, s)
                m_old = mr[0:1, :]
                ms.append((m_old, jnp.maximum(m_old, jnp.max(top, axis=0, keepdims=True))))
            ps = [[jnp.exp(s - m_new).astype(BF16) for s in row] for row, (_, m_new) in zip(ss, ms)]
            pvs = []
            for row, (_, vr, _, _, _) in zip(ps, heads):
                pv = _dot(vr[:, pl.ds(c0s[0], T)], row[0])
                for p, c0 in zip(row[1:], c0s[1:]):
                    pv = pv + _dot(vr[:, pl.ds(c0, T)], p)
                pvs.append(pv)
            for pv, (m_old, m_new), (_, _, _, mr, ar) in zip(pvs, ms, heads):
                ar[...] = jnp.exp(m_old - m_new) * ar[...] + pv
                mr[0:1, :] = m_new

        def qblk(i, carry):
            r0 = pl.multiple_of(i * T, T)
            ma[...] = jnp.full((8, T), NEG, F32)
            mb[...] = jnp.full((8, T), NEG, F32)
            acca[...] = jnp.zeros((128, T), F32)
            accb[...] = jnp.zeros((128, T), F32)
            kv([i], r0, True)
            done = _fox_tiles_back(cr_ref, i, r0, zba, zbb)
            _for_tiles_back(i, done, lambda js: kv(js, r0, False), fours=True)
            aa = acca[...]
            ab = accb[...]
            la = aa[64:65, :]
            lb = ab[0:1, :]
            o_ref[pl.ds(r0, T), :] = jnp.where(row_t, aa / la, ab / lb).T
            lse_ref[0:1, pl.ds(r0, T)] = ma[0:1, :] + jnp.log(la)
            lse_ref[1:2, pl.ds(r0, T)] = mb[0:1, :] + jnp.log(lb)
            lse_ref[2:3, pl.ds(r0, T)] = jnp.broadcast_to(done.astype(F32), (1, T))
            return carry

        lax.fori_loop(0, nq, qblk, 0)
        if ride is not None:
            @pl.when(pl.program_id(0) == n_pairs - 1)
            def _():
                _wait_exchange("gather", *xrefs)

    extra = () if ride is None else tuple(ride)
    return pl.pallas_call(
        body, name="fox_fwd" if ride is None else "fox_fwd_gather",
        grid=(n_pairs,),
        in_specs=[_pair_blk(S), _pair_blk(S), _pair_blk(S, OFF_FV // 128), _pair_blk(S), _pair_rows(S)]
        + [_ANY] * len(extra),
        out_specs=[_pair_blk(S), _pair_rows(S)] + [_ANY] * len(extra),
        out_shape=[jax.ShapeDtypeStruct((S, FOX_W), F32), jax.ShapeDtypeStruct((n_pairs, 8, S), F32)]
        + (_exchange_out_shapes("gather", *extra) if extra else []),
        scratch_shapes=[pltpu.VMEM((S, 128), BF16)] * 2 + [pltpu.VMEM((128, S), BF16)] * 2
        + [pltpu.VMEM((S, 128), F32)] * 2 + [pltpu.VMEM((8, T), F32)] * 2 + [pltpu.VMEM((128, T), F32)] * 2
        + (_EXCHANGE_SEMS if extra else []),
        compiler_params=_cp(("arbitrary",), _VMEM_BIG),
    )(qs, kn, proj, cqb, crow4, *extra)


def _softplus_parts(z):
    e = jnp.exp(-jnp.abs(z))
    return e, jnp.maximum(z, 0.0) + jnp.log(1.0 + e)


def _sb_fwd(proj, triu):
    S = proj.shape[0]
    T = triu.shape[0]
    nq = S // T

    def body(q_ref, k_ref, v_ref, tri_ref, o_ref, lt_ref, qa, qb, kb, vt, ra, rb, acca, accb):
        lane_s = _head_masks(S)
        q = (q_ref[...] * Q_SCALE).astype(BF16)
        zq = jnp.zeros_like(q)
        qa[...] = jnp.where(lane_s, q, zq)
        qb[...] = jnp.where(lane_s, zq, q)
        kb[...] = k_ref[...].astype(BF16)
        lt_ref[...] = jnp.zeros((8, S), F32)
        row_t = lax.broadcasted_iota(jnp.int32, (128, T), 0) < HEAD_DIM
        zba, zbb = _score_bounds(q, kb[...])

        def prep(c, carry):
            c0 = pl.multiple_of(c * T, T)
            vt[:, pl.ds(c0, T)] = v_ref[pl.ds(c0, T), :].T.astype(BF16)
            return carry

        lax.fori_loop(0, nq, prep, 0)
        strict = (lax.broadcasted_iota(jnp.int32, (T, T), 0) < lax.broadcasted_iota(jnp.int32, (T, T), 1))

        heads = ((qa, ra, acca), (qb, rb, accb))

        def kv(tiles, r0):
            tri = tri_ref[...]
            c0s = [pl.multiple_of(j * T, T) for j, _ in tiles]
            ks = [kb[pl.ds(c0, T), :] for c0 in c0s]
            qhs = [qr[pl.ds(r0, T), :] for qr, _, _ in heads]
            zs = [[_dot_nt(k, qh) for k in ks] for qh in qhs]
            lbs = [[jnp.where(strict, -_softplus_parts(z)[1], 0.0) if masked else -_softplus_parts(z)[1]
                    for z, (_, masked) in zip(row, tiles)] for row in zs]
            incs = [[_mm2(lb, tri, left=True) for lb in row] for row in lbs]
            avs = []
            for (_, r_ref, _), zrow, irow in zip(heads, zs, incs):
                r = r_ref[0:1, :]
                av = None
                for z, inc, c0, (_, masked) in zip(zrow, irow, c0s, tiles):
                    a = jnp.exp(z + inc + r)
                    if masked:
                        a = jnp.where(strict, a, 0.0)
                    term = _dot(vt[:, pl.ds(c0, T)], a.astype(BF16))
                    av = term if av is None else av + term
                    r = r + inc[0:1, :]
                avs.append((av, r))
            for (_, r_ref, acc_ref), (av, r) in zip(heads, avs):
                r_ref[0:1, :] = r
                acc_ref[...] = acc_ref[...] + av

        def qblk(i, carry):
            r0 = pl.multiple_of(i * T, T)
            ra[...] = jnp.zeros((8, T), F32)
            rb[...] = jnp.zeros((8, T), F32)
            acca[...] = jnp.zeros((128, T), F32)
            accb[...] = jnp.zeros((128, T), F32)

            @pl.when(i == 0)
            def _():
                kv([(i, True)], r0)

            @pl.when(i > 0)
            def _():
                kv([(i, True), (i - 1, False)], r0)

            def alive():
                return jnp.maximum(jnp.max(ra[0:1, :]) + zba, jnp.max(rb[0:1, :]) + zbb) > -_EXP_ZERO

            def cond(st):
                return (st[0] < i) & st[1]

            def step(st):
                kv([(i - 1 - st[0], False)], r0)
                return st[0] + 1, alive()

            done, _ = lax.while_loop(cond, step, (jnp.minimum(i, 1), alive()))
            o_ref[pl.ds(r0, T), :] = jnp.where(row_t, acca[...], accb[...]).T
            lt_ref[0:1, pl.ds(r0, T)] = ra[0:1, :]
            lt_ref[1:2, pl.ds(r0, T)] = rb[0:1, :]
            lt_ref[2:3, pl.ds(r0, T)] = jnp.broadcast_to(done.astype(F32), (1, T))
            return carry

        lax.fori_loop(0, nq, qblk, 0)

    return pl.pallas_call(
        body, name="sb_fwd",
        grid=(SB_W // 128,),
        in_specs=[_pair_blk(S, OFF_SQ // 128), _pair_blk(S, OFF_SK // 128), _pair_blk(S, OFF_SV // 128),
                  pl.BlockSpec((T, T), lambda p: (0, 0))],
        out_specs=[_pair_blk(S), _pair_rows(S)],
        out_shape=[jax.ShapeDtypeStruct((S, SB_W), F32), jax.ShapeDtypeStruct((SB_W // 128, 8, S), F32)],
        scratch_shapes=[pltpu.VMEM((S, 128), BF16)] * 3 + [pltpu.VMEM((128, S), BF16)]
        + [pltpu.VMEM((8, T), F32)] * 2 + [pltpu.VMEM((128, T), F32)] * 2,
        compiler_params=_cp(("arbitrary",), _VMEM_BIG),
    )(proj, proj, proj, triu)


def _pool_window_lanes(shape):
    lane = lax.broadcasted_iota(jnp.int32, shape, 1)
    return jnp.where(lane < 64, 2, jnp.where(lane < 128, 4, jnp.where(lane < 192, 8, 16)))


def _pool_fwd(proj):
    S = proj.shape[0]

    def body(x_ref, o_ref):
        x = x_ref[...]
        t = lax.broadcasted_iota(jnp.int32, x.shape, 0)
        lane = lax.broadcasted_iota(jnp.int32, x.shape, 1)

        def back(a, k):
            return jnp.where(t >= k, pltpu.roll(a, k, 0), 0.0)

        s1 = x + back(x, 1)
        s2 = s1 + back(s1, 2)
        s4 = s2 + back(s2, 4)
        s8 = s4 + back(s4, 8)
        win = jnp.where(lane < 64, s1, jnp.where(lane < 128, s2, jnp.where(lane < 192, s4, s8)))
        cnt = jnp.minimum(t + 1, _pool_window_lanes(x.shape)).astype(F32)
        o_ref[...] = win / cnt - x

    return pl.pallas_call(
        body, name="pool_fwd",
        grid=(1,),
        in_specs=[pl.BlockSpec((S, POOL_W), lambda i: (0, OFF_PX // POOL_W))],
        out_specs=pl.BlockSpec((S, POOL_W), lambda i: (0, 0)),
        out_shape=jax.ShapeDtypeStruct((S, POOL_W), F32),
        compiler_params=_cp(("arbitrary",), _VMEM_BIG),
    )(proj)


def _silu(g):
    return g * _sigmoid(g)


def _mix_out(fo, so, pooled, proj, wbd, scale, wout, x):
    S, D = x.shape
    tm = min(_TM_ROWS, S)

    def body(fo_ref, fg_ref, so_ref, sg_ref, pl_ref, pg_ref, wbd_ref, sc_ref, w_ref, x_ref, y_ref, mxt_ref, mx_ref):
        parts = ((0, fo_ref[...] * _silu(fg_ref[...])),
                 (FOX_W, (_dot(pl_ref[...].astype(BF16), wbd_ref[...]) * sc_ref[...]) * _silu(pg_ref[...])),
                 (FOX_W + POOL_W, so_ref[...] * _silu(sg_ref[...])))
        for off, part in parts:
            w = part.shape[1]
            mx_ref[:, off:off + w] = part.astype(BF16)
            mxt_ref[off:off + w, :] = part.T.astype(BF16)
        y_ref[...] = x_ref[...] + _dot(mx_ref[...], w_ref[...])

    return pl.pallas_call(
        body, name="mix_out",
        grid=(S // tm,),
        in_specs=[pl.BlockSpec((tm, FOX_W), lambda i: (i, 0)),
                  pl.BlockSpec((tm, FOX_W), lambda i: (i, OFF_FG // FOX_W)),
                  pl.BlockSpec((tm, SB_W), lambda i: (i, 0)),
                  pl.BlockSpec((tm, SB_W), lambda i: (i, OFF_SG // SB_W)),
                  pl.BlockSpec((tm, POOL_W), lambda i: (i, 0)),
                  pl.BlockSpec((tm, POOL_W), lambda i: (i, OFF_PG // POOL_W)),
                  pl.BlockSpec((POOL_W, POOL_W), lambda i: (0, 0)),
                  pl.BlockSpec((1, POOL_W), lambda i: (0, 0)),
                  pl.BlockSpec((D_MIX, D), lambda i: (0, 0)),
                  pl.BlockSpec((tm, D), lambda i: (i, 0))],
        out_specs=[pl.BlockSpec((tm, D), lambda i: (i, 0)), pl.BlockSpec((D_MIX, tm), lambda i: (0, i))],
        out_shape=[jax.ShapeDtypeStruct((S, D), F32), jax.ShapeDtypeStruct((D_MIX, S), BF16)],
        scratch_shapes=[pltpu.VMEM((tm, D_MIX), BF16)],
        compiler_params=_cp(("parallel",), _VMEM_MID),
    )(fo, proj, so, proj, pooled, proj, wbd, scale, wout, x)


def _loss_head(y, target):
    S, D = y.shape
    tm = min(_TM, S)

    def body(y_ref, t_ref, dy_ref, ls_ref):
        @pl.when(pl.program_id(0) == 0)
        def _():
            ls_ref[...] = jnp.zeros_like(ls_ref)

        e = y_ref[...] - t_ref[...]
        dy_ref[...] = e * (1.0 / D)
        ls_ref[...] = ls_ref[...] + jnp.sum(e * e) * (0.5 / D)

    dy, ls = pl.pallas_call(
        body, name="loss_head",
        grid=(S // tm,),
        in_specs=[pl.BlockSpec((tm, D), lambda i: (i, 0)), pl.BlockSpec((tm, D), lambda i: (i, 0))],
        out_specs=[pl.BlockSpec((tm, D), lambda i: (i, 0)), pl.BlockSpec((8, 128), lambda i: (0, 0))],
        out_shape=[jax.ShapeDtypeStruct((S, D), F32), jax.ShapeDtypeStruct((8, 128), F32)],
        compiler_params=_cp(("arbitrary",), _VMEM_MID),
    )(y, target)
    return dy, ls[0, 0]


def _dsilu(g):
    s = _sigmoid(g)
    return s * (1.0 + g * (1.0 - s))


def _gate_bwd(dy, wout, fo, so, pooled, proj, wbd, scale):
    S, D = dy.shape
    tm = min(_TM_ROWS, S)

    def body(dy_ref, w_ref, fo_ref, fg_ref, so_ref, sg_ref, pl_ref, pg_ref, wbd_ref, sc_ref,
             dfo_ref, dfg_ref, dso_ref, dsg_ref, dpg_ref, dpl_ref, dsc_ref, dwbd_ref):
        @pl.when(pl.program_id(0) == 0)
        def _():
            dsc_ref[...] = jnp.zeros_like(dsc_ref)
            dwbd_ref[...] = jnp.zeros_like(dwbd_ref)

        dm = _dot_nt(dy_ref[...].astype(BF16), w_ref[...])
        dmf = dm[:, 0:FOX_W]
        dmp = dm[:, FOX_W:FOX_W + POOL_W]
        dms = dm[:, FOX_W + POOL_W:D_MIX]
        fg = fg_ref[...]
        dfo_ref[...] = dmf * _silu(fg)
        dfg_ref[...] = (dmf * fo_ref[...] * _dsilu(fg)).astype(BF16)
        sg = sg_ref[...]
        dso_ref[...] = dms * _silu(sg)
        dsg_ref[...] = (dms * so_ref[...] * _dsilu(sg)).astype(BF16)
        pg = pg_ref[...]
        plb = pl_ref[...].astype(BF16)
        yw = _dot(plb, wbd_ref[...])
        sc = sc_ref[...]
        dpg_ref[...] = (dmp * (yw * sc) * _dsilu(pg)).astype(BF16)
        dys = dmp * _silu(pg)
        dsc_ref[...] = dsc_ref[...] + jnp.sum(dys * yw, axis=0, keepdims=True)
        dyw = (dys * sc).astype(BF16)
        dpl_ref[...] = _dot_nt(dyw, wbd_ref[...])
        dwbd_ref[...] = dwbd_ref[...] + _dot_tn(plb, dyw)

    return pl.pallas_call(
        body, name="gate_bwd",
        grid=(S // tm,),
        in_specs=[pl.BlockSpec((tm, D), lambda i: (i, 0)),
                  pl.BlockSpec((D_MIX, D), lambda i: (0, 0)),
                  pl.BlockSpec((tm, FOX_W), lambda i: (i, 0)),
                  pl.BlockSpec((tm, FOX_W), lambda i: (i, OFF_FG // FOX_W)),
                  pl.BlockSpec((tm, SB_W), lambda i: (i, 0)),
                  pl.BlockSpec((tm, SB_W), lambda i: (i, OFF_SG // SB_W)),
                  pl.BlockSpec((tm, POOL_W), lambda i: (i, 0)),
                  pl.BlockSpec((tm, POOL_W), lambda i: (i, OFF_PG // POOL_W)),
                  pl.BlockSpec((POOL_W, POOL_W), lambda i: (0, 0)),
                  pl.BlockSpec((1, POOL_W), lambda i: (0, 0))],
        out_specs=[pl.BlockSpec((tm, FOX_W), lambda i: (i, 0)),
                   pl.BlockSpec((tm, FOX_W), lambda i: (i, 0)),
                   pl.BlockSpec((tm, SB_W), lambda i: (i, 0)),
                   pl.BlockSpec((tm, SB_W), lambda i: (i, 0)),
                   pl.BlockSpec((tm, POOL_W), lambda i: (i, 0)),
                   pl.BlockSpec((tm, POOL_W), lambda i: (i, 0)),
                   pl.BlockSpec((1, POOL_W), lambda i: (0, 0)),
                   pl.BlockSpec((POOL_W, POOL_W), lambda i: (0, 0))],
        out_shape=[jax.ShapeDtypeStruct((S, FOX_W), F32), jax.ShapeDtypeStruct((S, FOX_W), BF16),
                   jax.ShapeDtypeStruct((S, SB_W), F32), jax.ShapeDtypeStruct((S, SB_W), BF16),
                   jax.ShapeDtypeStruct((S, POOL_W), BF16), jax.ShapeDtypeStruct((S, POOL_W), F32),
                   jax.ShapeDtypeStruct((1, POOL_W), F32), jax.ShapeDtypeStruct((POOL_W, POOL_W), F32)],
        compiler_params=_cp(("arbitrary",), _VMEM_MID),
    )(dy, wout, fo, proj, so, proj, pooled, proj, wbd, scale)


def _matmul_acc(at, b, name):
    M, S = at.shape
    N = b.shape[1]
    tk = min(_TK_DW, S)
    tn = min(512, N)
    nk = S // tk

    def body(a_ref, b_ref, o_ref, acc):
        k = pl.program_id(1)

        @pl.when(k == 0)
        def _():
            acc[...] = jnp.zeros_like(acc)

        acc[...] = acc[...] + _dot(a_ref[...], b_ref[...].astype(BF16))

        @pl.when(k == nk - 1)
        def _():
            o_ref[...] = acc[...].astype(BF16)

    return pl.pallas_call(
        body, name=name,
        grid=(N // tn, nk),
        in_specs=[pl.BlockSpec((M, tk), lambda j, k: (0, k)), pl.BlockSpec((tk, tn), lambda j, k: (k, j))],
        out_specs=pl.BlockSpec((M, tn), lambda j, k: (0, j)),
        out_shape=jax.ShapeDtypeStruct((M, N), BF16),
        scratch_shapes=[pltpu.VMEM((M, tn), F32)],
        compiler_params=_cp(("parallel", "arbitrary"), _VMEM_MID),
    )(at, b)


def _pool_bwd(dpooled):
    S = dpooled.shape[0]

    def body(d_ref, o_ref):
        d = d_ref[...]
        t = lax.broadcasted_iota(jnp.int32, d.shape, 0)
        lane = lax.broadcasted_iota(jnp.int32, d.shape, 1)
        cnt = jnp.minimum(t + 1, _pool_window_lanes(d.shape)).astype(F32)
        u = d / cnt

        def fwd(a, k):
            return jnp.where(t < S - k, pltpu.roll(a, S - k, 0), 0.0)

        s1 = u + fwd(u, 1)
        s2 = s1 + fwd(s1, 2)
        s4 = s2 + fwd(s2, 4)
        s8 = s4 + fwd(s4, 8)
        win = jnp.where(lane < 64, s1, jnp.where(lane < 128, s2, jnp.where(lane < 192, s4, s8)))
        o_ref[...] = (win - d).astype(BF16)

    return pl.pallas_call(
        body, name="pool_bwd",
        grid=(1,),
        in_specs=[pl.BlockSpec((S, POOL_W), lambda i: (0, 0))],
        out_specs=pl.BlockSpec((S, POOL_W), lambda i: (0, 0)),
        out_shape=jax.ShapeDtypeStruct((S, POOL_W), BF16),
        compiler_params=_cp(("arbitrary",), _VMEM_BIG),
    )(dpooled)


def _fox_bwd(qs, kn, proj, dfo, fo, lse, cqb, crow4, ride=None):
    S = qs.shape[0]
    T = min(_T, S)
    nq = S // T
    n_pairs = FOX_W // 128

    def body(*refs):
        if ride is None:
            q_ref, k_ref, v_ref, do_ref, o_ref, lse_ref, cq_ref, cr_ref = refs[:8]
            dq_ref, dk_ref, dv_ref, dck_ref, dcq_ref = refs[8:13]
            scr = refs[13:]
        else:
            q_ref, k_ref, v_ref, do_ref, o_ref, lse_ref, cq_ref, cr_ref, pa_ref, pb_ref = refs[:10]
            dq_ref, dk_ref, dv_ref, dck_ref, dcq_ref, ra_ref, rb_ref = refs[10:17]
            scr = refs[17:32]
            xrefs = (pa_ref, pb_ref, ra_ref, rb_ref) + tuple(refs[32:])

            @pl.when(pl.program_id(0) == 0)
            def _():
                _start_exchange("scatter", *xrefs)

        qa, qb, kta, ktb, vb, doa, dob, cka, ckb, dcka, dckb, dva, dqt, dcqa, dcqb = scr
        lane_s = _head_masks(S)
        q = q_ref[...]
        zq = jnp.zeros_like(q)
        qa[...] = jnp.where(lane_s, q, zq)
        qb[...] = jnp.where(lane_s, zq, q)
        vb[...] = v_ref[...].astype(BF16)
        do = do_ref[...].astype(BF16)
        doa[...] = jnp.where(lane_s, do, zq)
        dob[...] = jnp.where(lane_s, zq, do)
        cq = cq_ref[...]
        cka[...], ckb[...] = _spread_heads(cq)
        zs = jnp.zeros((S, 128), F32)
        dk_ref[...] = zs
        dva[...] = zs
        dcka[...] = zs
        dckb[...] = zs
        dcq_ref[...] = jnp.zeros((8, S), F32)
        row_t = lax.broadcasted_iota(jnp.int32, (128, T), 0) < HEAD_DIM

        def prep(c, carry):
            c0 = pl.multiple_of(c * T, T)
            kt = k_ref[pl.ds(c0, T), :].astype(F32).T
            kta[:, pl.ds(c0, T)] = jnp.where(row_t, kt, 0.0).astype(BF16)
            ktb[:, pl.ds(c0, T)] = jnp.where(row_t, 0.0, kt).astype(BF16)
            return carry

        lax.fori_loop(0, nq, prep, 0)
        causal = (lax.broadcasted_iota(jnp.int32, (T, T), 0) <= lax.broadcasted_iota(jnp.int32, (T, T), 1))

        heads = ((qa, kta, doa, cka, dcka, dcqa), (qb, ktb, dob, ckb, dckb, dcqb))

        def kv(js, r0, lss, dls, masked):
            cr = cr_ref[:, pl.ds(r0, T)]
            c0s = [pl.multiple_of(j * T, T) for j in js]
            ks = [k_ref[pl.ds(c0, T), :] for c0 in c0s]
            vs = [vb[pl.ds(c0, T), :] for c0 in c0s]
            qhs = [hd[0][pl.ds(r0, T), :] for hd in heads]
            dohs = [hd[2][pl.ds(r0, T), :] for hd in heads]
            ss = []
            for h, hd in enumerate(heads):
                row = []
                for k, c0 in zip(ks, c0s):
                    s = _dot_nt(k, qhs[h]) + cr[h:h + 1, :] - jnp.tile(hd[3][pl.ds(c0, T), :], (1, T // 128))
                    row.append(jnp.where(causal, s, NEG) if masked else s)
                ss.append(row)
            ps = [[jnp.exp(s - lss[h]) for s in row] for h, row in enumerate(ss)]
            dps = [[_dot_nt(v, dohs[h]) for v in vs] for h in range(2)]
            dss = [[p * (dp - dls[h]) for p, dp in zip(ps[h], dps[h])] for h in range(2)]
            pbs = [[p.astype(BF16) for p in row] for row in ps]
            dsbs = [[ds.astype(BF16) for ds in row] for row in dss]
            for t, c0 in enumerate(c0s):
                dva[pl.ds(c0, T), :] = dva[pl.ds(c0, T), :] + (_dot(pbs[0][t], dohs[0]) + _dot(pbs[1][t], dohs[1]))
                dk_ref[pl.ds(c0, T), :] = dk_ref[pl.ds(c0, T), :] + (_dot(dsbs[0][t], qhs[0]) + _dot(dsbs[1][t], qhs[1]))
            dq = None
            for h, hd in enumerate(heads):
                for t, c0 in enumerate(c0s):
                    term = _dot(hd[1][:, pl.ds(c0, T)], dsbs[h][t])
                    dq = term if dq is None else dq + term
            dqt[...] = dqt[...] + dq
            for h, hd in enumerate(heads):
                col = jnp.sum(dss[h][0], axis=0, keepdims=True)
                for ds in dss[h][1:]:
                    col = col + jnp.sum(ds, axis=0, keepdims=True)
                hd[5][0:1, :] = hd[5][0:1, :] + col
                for ds, c0 in zip(dss[h], c0s):
                    fold = ds[:, 0:128]
                    for u in range(1, T // 128):
                        fold = fold + ds[:, 128 * u:128 * (u + 1)]
                    hd[4][pl.ds(c0, T), :] = hd[4][pl.ds(c0, T), :] - fold

        def qblk(i, carry):
            r0 = pl.multiple_of(i * T, T)
            dt = (do_ref[pl.ds(r0, T), :] * o_ref[pl.ds(r0, T), :]).T
            dla = jnp.sum(jnp.where(row_t, dt, 0.0), axis=0, keepdims=True)
            dlb = jnp.sum(jnp.where(row_t, 0.0, dt), axis=0, keepdims=True)
            ls = lse_ref[:, pl.ds(r0, T)]
            lss = (ls[0:1, :], ls[1:2, :])
            back = jnp.max(ls[2:3, :]).astype(jnp.int32)
            dqt[...] = jnp.zeros((128, T), F32)
            dcqa[...] = jnp.zeros((8, T), F32)
            dcqb[...] = jnp.zeros((8, T), F32)
            kv([i], r0, lss, (dla, dlb), True)
            _for_tiles_back(i, back, lambda js: kv(js, r0, lss, (dla, dlb), False), fours=True)
            dq_ref[pl.ds(r0, T), :] = dqt[...].T
            dcq_ref[0:1, pl.ds(r0, T)] = dcqa[0:1, :]
            dcq_ref[1:2, pl.ds(r0, T)] = dcqb[0:1, :]
            return carry

        lax.fori_loop(0, nq, qblk, 0)
        dv_ref[...] = dva[...].astype(BF16)
        dck_ref[...] = jnp.where(lane_s, jnp.sum(dcka[...], axis=1, keepdims=True),
                                 jnp.sum(dckb[...], axis=1, keepdims=True))
        if ride is not None:
            @pl.when(pl.program_id(0) == n_pairs - 1)
            def _():
                _wait_exchange("scatter", *xrefs)

    extra = () if ride is None else tuple(ride)
    return pl.pallas_call(
        body, name="fox_bwd" if ride is None else "fox_bwd_exchange",
        grid=(n_pairs,),
        in_specs=[_pair_blk(S), _pair_blk(S), _pair_blk(S, OFF_FV // 128), _pair_blk(S), _pair_blk(S),
                  _pair_rows(S), _pair_blk(S), _pair_rows(S)] + [_ANY] * len(extra),
        out_specs=[_pair_blk(S), _pair_blk(S), _pair_blk(S), _pair_blk(S), _pair_rows(S)] + [_ANY] * len(extra),
        out_shape=[jax.ShapeDtypeStruct((S, FOX_W), F32), jax.ShapeDtypeStruct((S, FOX_W), F32),
                   jax.ShapeDtypeStruct((S, FOX_W), BF16), jax.ShapeDtypeStruct((S, FOX_W), F32),
                   jax.ShapeDtypeStruct((n_pairs, 8, S), F32)]
        + (_exchange_out_shapes("scatter", *extra) if extra else []),
        scratch_shapes=[pltpu.VMEM((S, 128), BF16)] * 2 + [pltpu.VMEM((128, S), BF16)] * 2
        + [pltpu.VMEM((S, 128), BF16)] * 3 + [pltpu.VMEM((S, 128), F32)] * 5
        + [pltpu.VMEM((128, T), F32)] + [pltpu.VMEM((8, T), F32)] * 2
        + (_EXCHANGE_SEMS if extra else []),
        compiler_params=_cp(("arbitrary",), _VMEM_BIG),
    )(qs, kn, proj, dfo, fo, lse, cqb, crow4, *extra)


def _sb_bwd(proj, dso, ltot, tril):
    S = proj.shape[0]
    T = tril.shape[0]
    nq = S // T

    def body(q_ref, k_ref, v_ref, do_ref, lt_ref, tri_ref, dq_ref, dk_ref, dv_ref,
             qa, qb, k2, kta, ktb, vb, doa, dob, dka, dva, dqt, ra, rb, ga, gb):
        lane_s = _head_masks(S)
        q = (q_ref[...] * Q_SCALE).astype(BF16)
        zq = jnp.zeros_like(q)
        qa[...] = jnp.where(lane_s, q, zq)
        qb[...] = jnp.where(lane_s, zq, q)
        k2[...] = k_ref[...].astype(BF16)
        vb[...] = v_ref[...].astype(BF16)
        do = do_ref[...].astype(BF16)
        doa[...] = jnp.where(lane_s, do, zq)
        dob[...] = jnp.where(lane_s, zq, do)
        dka[...] = jnp.zeros((S, 128), F32)
        dva[...] = jnp.zeros((S, 128), F32)
        row_t = lax.broadcasted_iota(jnp.int32, (128, T), 0) < HEAD_DIM

        def prep(c, carry):
            c0 = pl.multiple_of(c * T, T)
            kt = k_ref[pl.ds(c0, T), :].T
            kta[:, pl.ds(c0, T)] = jnp.where(row_t, kt, 0.0).astype(BF16)
            ktb[:, pl.ds(c0, T)] = jnp.where(row_t, 0.0, kt).astype(BF16)
            return carry

        lax.fori_loop(0, nq, prep, 0)
        strict = (lax.broadcasted_iota(jnp.int32, (T, T), 0) < lax.broadcasted_iota(jnp.int32, (T, T), 1))

        heads = ((qa, kta, doa, ra, ga), (qb, ktb, dob, rb, gb))

        def kv(tiles, r0, lts):
            tri = tri_ref[...]
            c0s = [pl.multiple_of(j * T, T) for j, _ in tiles]
            ks = [k2[pl.ds(c0, T), :] for c0 in c0s]
            vs = [vb[pl.ds(c0, T), :] for c0 in c0s]
            qhs = [hd[0][pl.ds(r0, T), :] for hd in heads]
            dohs = [hd[2][pl.ds(r0, T), :] for hd in heads]
            zs = [[_dot_nt(k, qh) for k in ks] for qh in qhs]
            das = [[_dot_nt(v, doh) for v in vs] for doh in dohs]
            es, lbs = [], []
            for row in zs:
                erow, lrow = [], []
                for z, (_, masked) in zip(row, tiles):
                    e, sp = _softplus_parts(z)
                    erow.append(e)
                    lrow.append(jnp.where(strict, -sp, 0.0) if masked else -sp)
                es.append(erow)
                lbs.append(lrow)
            pres = [[_mm2(lb, tri, left=True) for lb in row] for row in lbs]
            aas, r_ends = [], []
            for hd, zrow, lrow, prow, lt in zip(heads, zs, lbs, pres, lts):
                r = hd[3][0:1, :]
                arow = []
                for z, lb, pre, (_, masked) in zip(zrow, lrow, prow, tiles):
                    a = jnp.exp(z + lb + ((lt - r) - pre))
                    arow.append(jnp.where(strict, a, 0.0) if masked else a)
                    r = r + pre[T - 1:T, :]
                aas.append(arow)
                r_ends.append(r)
            gs = [[a * da for a, da in zip(arow, drow)] for arow, drow in zip(aas, das)]
            gpres = [[_mm2(g, tri, left=True) for g in row] for row in gs]
            dzbs, g_ends = [], []
            for hd, zrow, erow, grow, gprow in zip(heads, zs, es, gs, gpres):
                gc = hd[4][0:1, :]
                drow = []
                for z, e, g, gpre, (_, masked) in zip(zrow, erow, grow, gprow, tiles):
                    inv = 1.0 / (1.0 + e)
                    pos = z >= 0.0
                    sig = jnp.where(pos, 1.0, e) * inv
                    oms = jnp.where(pos, e, 1.0) * inv
                    dz = g * oms - sig * (gc + (gpre - g))
                    if masked:
                        dz = jnp.where(strict, dz, 0.0)
                    drow.append(dz.astype(BF16))
                    gc = gc + gpre[T - 1:T, :]
                dzbs.append(drow)
                g_ends.append(gc)
            dq = None
            for h, hd in enumerate(heads):
                for t, c0 in enumerate(c0s):
                    term = _dot(hd[1][:, pl.ds(c0, T)], dzbs[h][t])
                    dq = term if dq is None else dq + term
            dqt[...] = dqt[...] + dq
            for t, c0 in enumerate(c0s):
                dka[pl.ds(c0, T), :] = dka[pl.ds(c0, T), :] + (_dot(dzbs[0][t], qhs[0]) + _dot(dzbs[1][t], qhs[1]))
                dva[pl.ds(c0, T), :] = dva[pl.ds(c0, T), :] + (_dot(aas[0][t].astype(BF16), dohs[0])
                                                               + _dot(aas[1][t].astype(BF16), dohs[1]))
            for hd, r, gc in zip(heads, r_ends, g_ends):
                hd[3][0:1, :] = r
                hd[4][0:1, :] = gc

        def qblk(i, carry):
            r0 = pl.multiple_of(i * T, T)
            lt = lt_ref[:, pl.ds(r0, T)]
            lts = (lt[0:1, :], lt[1:2, :])
            back = jnp.max(lt[2:3, :]).astype(jnp.int32)
            zt = jnp.zeros((8, T), F32)
            dqt[...] = jnp.zeros((128, T), F32)
            ra[...] = zt
            rb[...] = zt
            ga[...] = zt
            gb[...] = zt

            def inner(j, c):
                kv([(j, False)], r0, lts)
                return c

            @pl.when(back == 0)
            def _():
                kv([(i, True)], r0, lts)

            @pl.when(back > 0)
            def _():
                lax.fori_loop(i - back, i - 1, inner, 0)
                kv([(i - 1, False), (i, True)], r0, lts)
            dq_ref[pl.ds(r0, T), :] = (dqt[...] * Q_SCALE).T.astype(BF16)
            return carry

        lax.fori_loop(0, nq, qblk, 0)
        dk_ref[...] = dka[...].astype(BF16)
        dv_ref[...] = dva[...].astype(BF16)

    return pl.pallas_call(
        body, name="sb_bwd",
        grid=(SB_W // 128,),
        in_specs=[_pair_blk(S, OFF_SQ // 128), _pair_blk(S, OFF_SK // 128), _pair_blk(S, OFF_SV // 128),
                  _pair_blk(S), _pair_rows(S), pl.BlockSpec((T, T), lambda p: (0, 0))],
        out_specs=[_pair_blk(S), _pair_blk(S), _pair_blk(S)],
        out_shape=[jax.ShapeDtypeStruct((S, SB_W), BF16)] * 3,
        scratch_shapes=([pltpu.VMEM((S, 128), BF16)] * 3 + [pltpu.VMEM((128, S), BF16)] * 2
                        + [pltpu.VMEM((S, 128), BF16)] * 3 + [pltpu.VMEM((S, 128), F32)] * 2
                        + [pltpu.VMEM((128, T), F32)] + [pltpu.VMEM((8, T), F32)] * 4),
        compiler_params=_cp(("arbitrary",), _VMEM_BIG),
    )(proj, proj, proj, dso, ltot, tril)


def _head_norm_bwd(x, g, dy, bd):
    ss = _mm2(x * x, bd)
    r = lax.rsqrt(ss * (1.0 / HEAD_DIM) + EPS)
    xr = x * r
    gdy = g * dy
    m = _mm2(xr * gdy, bd) * (1.0 / HEAD_DIM)
    return r * (gdy - xr * m), dy * xr


def _qk_bwd(dqs, dkn, proj, pff, bfp, gq, gk, bd, dccol, triu):
    S = proj.shape[0]
    T = triu.shape[0]
    n = S // T
    rev = lambda col: (lambda i: (n - 1 - i, col))

    def body(dq_ref, dk_ref, q_ref, k_ref, ff_ref, b_ref, gq_ref, gk_ref, bd_ref, dc_ref, tri_ref,
             dfq_ref, dfk_ref, dff_ref, dgq_ref, dgk_ref, dbf_ref, carry):
        @pl.when(pl.program_id(0) == 0)
        def _():
            carry[...] = jnp.zeros_like(carry)
            dgq_ref[...] = jnp.zeros_like(dgq_ref)
            dgk_ref[...] = jnp.zeros_like(dgk_ref)
            dbf_ref[...] = jnp.zeros_like(dbf_ref)

        bdv = bd_ref[...]
        dxq, gq_rows = _head_norm_bwd(q_ref[...], gq_ref[...], dq_ref[...] * Q_SCALE, bdv)
        dfq_ref[...] = dxq.astype(BF16)
        dgq_ref[...] = dgq_ref[...] + jnp.sum(gq_rows, axis=0, keepdims=True)
        dxk, gk_rows = _head_norm_bwd(k_ref[...], gk_ref[...], dk_ref[...], bdv)
        dfk_ref[...] = dxk.astype(BF16)
        dgk_ref[...] = dgk_ref[...] + jnp.sum(gk_rows, axis=0, keepdims=True)
        dlf = _mm3(dc_ref[...], tri_ref[...], left=True) + carry[0:1, :]
        carry[0:1, :] = dlf[0:1, :]
        u = ff_ref[...] + b_ref[...]
        lane = lax.broadcasted_iota(jnp.int32, u.shape, 1)
        dff = jnp.where(lane < N_FF, dlf * _sigmoid(-u), 0.0)
        dff_ref[...] = dff.astype(BF16)
        dbf_ref[...] = dbf_ref[...] + jnp.sum(dff, axis=0, keepdims=True)

    return pl.pallas_call(
        body, name="qk_bwd",
        grid=(n,),
        in_specs=[pl.BlockSpec((T, FOX_W), rev(0)), pl.BlockSpec((T, FOX_W), rev(0)),
                  pl.BlockSpec((T, FOX_W), rev(OFF_FQ // FOX_W)), pl.BlockSpec((T, FOX_W), rev(OFF_FK // FOX_W)),
                  pl.BlockSpec((T, N_FFPAD), rev(0)),
                  pl.BlockSpec((1, N_FFPAD), lambda i: (0, 0)),
                  pl.BlockSpec((1, FOX_W), lambda i: (0, 0)), pl.BlockSpec((1, FOX_W), lambda i: (0, 0)),
                  pl.BlockSpec((FOX_W, FOX_W), lambda i: (0, 0)),
                  pl.BlockSpec((T, N_FFPAD), rev(0)),
                  pl.BlockSpec((T, T), lambda i: (0, 0))],
        out_specs=[pl.BlockSpec((T, FOX_W), rev(0)), pl.BlockSpec((T, FOX_W), rev(0)),
                   pl.BlockSpec((T, N_FFPAD), rev(0)),
                   pl.BlockSpec((1, FOX_W), lambda i: (0, 0)), pl.BlockSpec((1, FOX_W), lambda i: (0, 0)),
                   pl.BlockSpec((1, N_FFPAD), lambda i: (0, 0))],
        out_shape=[jax.ShapeDtypeStruct((S, FOX_W), BF16), jax.ShapeDtypeStruct((S, FOX_W), BF16),
                   jax.ShapeDtypeStruct((S, N_FFPAD), BF16),
                   jax.ShapeDtypeStruct((1, FOX_W), F32), jax.ShapeDtypeStruct((1, FOX_W), F32),
                   jax.ShapeDtypeStruct((1, N_FFPAD), F32)],
        scratch_shapes=[pltpu.VMEM((8, N_FFPAD), F32)],
        compiler_params=_cp(("arbitrary",), _VMEM_MID),
    )(dqs, dkn, proj, proj, pff, bfp, gq, gk, bd, dccol, triu)


def _dproj_layout(pieces):
    offs, o = [], 0
    for p in pieces:
        offs.append(o)
        o += p.shape[1]
    assert o == N_MAIN
    return offs


def _inproj_bwd_dx(pieces, dff, wm, wff, x, g, dy, ride=None):
    S, D = x.shape
    tm = min(_TM_DX, S)
    steps = S // tm
    offs = _dproj_layout(pieces)
    n = len(pieces)

    def body(*refs):
        p_refs = refs[:n]
        if ride is None:
            dff_ref, w_ref, wff_ref, x_ref, g_ref, dy_ref, dx_ref, dg_ref = refs[n:]
        else:
            dff_ref, w_ref, wff_ref, x_ref, g_ref, dy_ref, pa_ref, pb_ref = refs[n:n + 8]
            dx_ref, dg_ref, ra_ref, rb_ref = refs[n + 8:n + 12]
            xrefs = (pa_ref, pb_ref, ra_ref, rb_ref) + tuple(refs[n + 12:])

        @pl.when(pl.program_id(0) == 0)
        def _():
            dg_ref[...] = jnp.zeros_like(dg_ref)
            if ride is not None:
                _start_exchange("scatter", *xrefs)

        dh = _dot_nt(dff_ref[...], wff_ref[...])
        for p_ref, off in zip(p_refs, offs):
            dh = dh + _dot_nt(p_ref[...], w_ref[:, off:off + p_ref.shape[1]])
        xv = x_ref[...]
        r = _rms_rows(xv)
        xr = xv * r
        dg_ref[...] = dg_ref[...] + jnp.sum(dh * xr, axis=0, keepdims=True)
        gdh = g_ref[...] * dh
        m = jnp.mean(gdh * xr, axis=-1, keepdims=True)
        dx_ref[...] = dy_ref[...] + r * (gdh - xr * m)
        if ride is not None:
            @pl.when(pl.program_id(0) == steps - 1)
            def _():
                _wait_exchange("scatter", *xrefs)

    extra = () if ride is None else tuple(ride)
    return pl.pallas_call(
        body, name="inproj_bwd_dx" if ride is None else "inproj_bwd_dx_exchange",
        grid=(steps,),
        in_specs=[pl.BlockSpec((tm, p.shape[1]), lambda i: (i, 0)) for p in pieces]
        + [pl.BlockSpec((tm, N_FFPAD), lambda i: (i, 0)),
                  pl.BlockSpec((D, N_MAIN), lambda i: (0, 0)),
                  pl.BlockSpec((D, N_FFPAD), lambda i: (0, 0)),
                  pl.BlockSpec((tm, D), lambda i: (i, 0)),
                  pl.BlockSpec((1, D), lambda i: (0, 0)),
                  pl.BlockSpec((tm, D), lambda i: (i, 0))] + [_ANY] * len(extra),
        out_specs=[pl.BlockSpec((tm, D), lambda i: (i, 0)), pl.BlockSpec((1, D), lambda i: (0, 0))] + [_ANY] * len(extra),
        out_shape=[jax.ShapeDtypeStruct((S, D), F32), jax.ShapeDtypeStruct((1, D), F32)]
        + (_exchange_out_shapes("scatter", *extra) if extra else []),
        scratch_shapes=_EXCHANGE_SEMS if extra else [],
        compiler_params=_cp(("arbitrary",), _VMEM_WIDE),
    )(*pieces, dff, wm, wff, x, g, dy, *extra)


def _inproj_bwd_dw(ht, pieces, dff):
    D, S = ht.shape
    tk = min(_TK_DW, S)
    nk = S // tk
    offs = _dproj_layout(pieces)
    n = len(pieces)

    def body(*refs):
        ht_ref, p_refs, dff_ref = refs[0], refs[1:1 + n], refs[1 + n]
        dw_ref, dwff_ref, acc, accff = refs[2 + n:]
        k = pl.program_id(0)

        @pl.when(k == 0)
        def _():
            acc[...] = jnp.zeros_like(acc)
            accff[...] = jnp.zeros_like(accff)

        hb = ht_ref[...]
        for p_ref, off in zip(p_refs, offs):
            w = p_ref.shape[1]
            acc[:, off:off + w] = acc[:, off:off + w] + _dot(hb, p_ref[...])
        accff[...] = accff[...] + _dot(hb, dff_ref[...])

        @pl.when(k == nk - 1)
        def _():
            dw_ref[...] = acc[...].astype(BF16)
            dwff_ref[...] = accff[...].astype(BF16)

    return pl.pallas_call(
        body, name="inproj_bwd_dw",
        grid=(nk,),
        in_specs=[pl.BlockSpec((D, tk), lambda k: (0, k))]
        + [pl.BlockSpec((tk, p.shape[1]), lambda k: (k, 0)) for p in pieces]
        + [pl.BlockSpec((tk, N_FFPAD), lambda k: (k, 0))],
        out_specs=[pl.BlockSpec((D, N_MAIN), lambda k: (0, 0), pipeline_mode=pl.Buffered(1)),
                   pl.BlockSpec((D, N_FFPAD), lambda k: (0, 0), pipeline_mode=pl.Buffered(1))],
        out_shape=[jax.ShapeDtypeStruct((D, N_MAIN), BF16), jax.ShapeDtypeStruct((D, N_FFPAD), BF16)],
        scratch_shapes=[pltpu.VMEM((D, N_MAIN), F32), pltpu.VMEM((D, N_FFPAD), F32)],
        compiler_params=_cp(("arbitrary",), _VMEM_BIG),
    )(ht, *pieces, dff)


def _constants(T):
    tril = jnp.tril(jnp.ones((T, T), F32)).astype(BF16)
    hid = jnp.arange(FOX_W) // HEAD_DIM
    bd = (hid[:, None] == hid[None, :]).astype(BF16)
    ex = (jnp.arange(N_FFPAD)[:, None] == hid[None, :]).astype(BF16)
    return tril, tril.T, bd, ex


def _crow4(ccol, T):
    S = ccol.shape[0]
    c = ccol[:, :FOX_HEADS].T
    last = jnp.pad(c[:, T - 1::T], ((0, 0), (0, S - S // T)))
    rows = jnp.concatenate([c.reshape(FOX_HEADS // 2, 2, S), last.reshape(FOX_HEADS // 2, 2, S)], axis=1)
    return jnp.pad(rows, ((0, 0), (0, 4), (0, 0)))


def _layer_fwd(x, lw, consts, ride=None):
    tril, triu, bd, ex = consts
    proj, pff, ht = _inproj_fwd(x, lw["g"], lw["wm"], lw["wff"])
    qs, kn, ccol, cqb = _fox_prep(proj, pff, lw["bfp"], lw["gq"], lw["gk"], bd, ex, tril)
    crow4 = _crow4(ccol, tril.shape[0])
    fo, lse, *gathered = _fox_fwd(qs, kn, proj, cqb, crow4, ride)
    so, ltot = _sb_fwd(proj, triu)
    pooled = _pool_fwd(proj)
    y, mixedt = _mix_out(fo, so, pooled, proj, lw["wbd"], lw["scale"], lw["wout"], x)
    return y, (x, proj, pff, ht, qs, kn, cqb, crow4, fo, lse, so, ltot, pooled, mixedt), gathered


def _layer_bwd(dy, saved, lw, consts, ride=None, exchange_own=False):
    tril, triu, bd, _ = consts
    x, proj, pff, ht, qs, kn, cqb, crow4, fo, lse, so, ltot, pooled, mixedt = saved
    S = x.shape[0]
    dfo, dfg, dso, dsg, dpg, dpooled, dscale, dwbd = _gate_bwd(dy, lw["wout"], fo, so, pooled, proj, lw["wbd"], lw["scale"])
    dwout = _matmul_acc(mixedt, dy, "dw_out")
    dpx = _pool_bwd(dpooled)
    dqs, dkn, dfv, dck, dcq4, *received = _fox_bwd(qs, kn, proj, dfo, fo, lse, cqb, crow4, ride)
    dsq, dsk, dsv = _sb_bwd(proj, dso, ltot, tril)
    dc8 = dck[:, ::HEAD_DIM] + dcq4[:, :2, :].reshape(FOX_HEADS, S).T
    dccol = jnp.pad(dc8, ((0, 0), (0, N_FFPAD - FOX_HEADS)))
    dfq, dfk, dff, dgq, dgk, dbf = _qk_bwd(dqs, dkn, proj, pff, lw["bfp"], lw["gq"], lw["gk"], bd, dccol, triu)
    pieces = [dfq, dfk, dfv, dfg, dpx, dpg, dsq, dsk, dsv, dsg]
    dwm, dwff = _inproj_bwd_dw(ht, pieces, dff)
    dwin = jnp.concatenate([dwm[:, :OFF_PX], dwff[:, :N_FF], dwm[:, OFF_PX:]], axis=1)
    own = _grad_parts({"w_in": dwin, "w_out": dwout}) if exchange_own else None
    dx, dng, *received_own = _inproj_bwd_dx(pieces, dff, lw["wm"], lw["wff"], x, lw["g"], dy, own)
    grads = {
        "norm_g": dng[0],
        "w_in": dwin,
        "b_f": dbf[0, :N_FF],
        "q_norm_g": dgq[0].reshape(FOX_HEADS, HEAD_DIM).sum(0),
        "k_norm_g": dgk[0].reshape(FOX_HEADS, HEAD_DIM).sum(0),
        "w_pool": jnp.stack([dwbd[64 * i:64 * i + 64, 64 * i:64 * i + 64] for i in range(4)]),
        "pool_scale": dscale[0],
        "w_out": dwout,
    }
    return dx, grads, received, received_own


def _layer_weights(l, norm_g, gin, b_f, q_norm_g, k_norm_g, w_pool, pool_scale, gout):
    D = gin.shape[1]
    w = gin.transpose(1, 0, 2).reshape(D, D_IN)
    wm = jnp.concatenate([w[:, :2048], w[:, 2048 + N_FF:]], axis=1)
    wff = jnp.pad(w[:, 2048:2048 + N_FF], ((0, 0), (0, N_FFPAD - N_FF)))
    grp = jnp.arange(POOL_W) // 64
    wbd = jnp.where(grp[:, None] == grp[None, :], jnp.tile(w_pool[l].transpose(1, 0, 2).reshape(64, POOL_W), (4, 1)), 0.0)
    return {
        "g": norm_g[l].reshape(1, D),
        "wm": wm, "wff": wff,
        "bfp": jnp.pad(b_f[l], (0, N_FFPAD - N_FF)).reshape(1, N_FFPAD),
        "gq": jnp.tile(q_norm_g[l], FOX_HEADS).reshape(1, FOX_W),
        "gk": jnp.tile(k_norm_g[l], FOX_HEADS).reshape(1, FOX_W),
        "wbd": wbd.astype(BF16),
        "scale": pool_scale[l].reshape(1, POOL_W),
        "wout": gout.reshape(D_MIX, D),
    }


def _grad_parts(g):
    dwin, dwout = g["w_in"].astype(BF16), g["w_out"].astype(BF16)
    D = dwin.shape[0]
    return (dwin.reshape(D, N_DEV, D_IN // N_DEV).transpose(1, 0, 2),
            dwout.reshape(N_DEV, D_MIX // N_DEV, dwout.shape[1]))


def _train_step(x, target, norm_g, win_sh, b_f, q_norm_g, k_norm_g, w_pool, pool_scale, wout_sh):
    L = norm_g.shape[0]
    consts = _constants(min(_T, x.shape[0]))
    gathered = _gather_two_level(win_sh[0], wout_sh[0], "gather_weights")
    lws, saved = [], []
    h = x
    for l in range(L):
        lws.append(_layer_weights(l, norm_g, gathered[0], b_f, q_norm_g, k_norm_g, w_pool, pool_scale, gathered[1]))
        ride = (win_sh[l + 1], wout_sh[l + 1]) if l + 1 < L else None
        h, sv, gathered = _layer_fwd(h, lws[l], consts, ride)
        saved.append(sv)
    dy, loss = _loss_head(h, target)
    grads, received = [None] * L, [None] * L
    ride = None
    for l in reversed(range(L)):
        dy, grads[l], got, got_own = _layer_bwd(dy, saved[l], lws[l], consts, ride, exchange_own=(l == 0))
        if ride is not None:
            received[l + 1] = got
        if l == 0:
            received[0] = got_own
        else:
            ride = _grad_parts(grads[l])
    return loss, dy, grads, received


def _mesh_pos():
    return lax.axis_index("x"), lax.axis_index("y"), lax.axis_index("c")


_FLIPS = [(0, 0, 1), (1, 0, 0), (0, 1, 0), (1, 1, 0), (1, 0, 1), (0, 1, 1), (1, 1, 1)]


def _peers():
    x, y, c = _mesh_pos()
    out = []
    for fx, fy, fc in _FLIPS:
        px = 1 - x if fx else x
        py = 1 - y if fy else y
        pc = 1 - c if fc else c
        out.append(((px, py, pc), 4 * px + 2 * py + pc))
    return out, 4 * x + 2 * y + c


_EXCHANGE_SEMS = [pltpu.SemaphoreType.DMA((14,)), pltpu.SemaphoreType.DMA((14,)), pltpu.SemaphoreType.DMA((2,))]
_ANY = pl.BlockSpec(memory_space=pl.ANY)


def _exchange_copies(kind, a_ref, b_ref, oa_ref, ob_ref, send_sems, recv_sems, loc_sems):
    peers, me = _peers()
    pairs = ((a_ref, oa_ref), (b_ref, ob_ref))
    local = [pltpu.make_async_copy(src if kind == "gather" else src.at[me], dst.at[me], loc_sems.at[t])
             for t, (src, dst) in enumerate(pairs)]
    remote = []
    for k, (dev, idx) in enumerate(peers):
        for t, (src, dst) in enumerate(pairs):
            remote.append(pltpu.make_async_remote_copy(
                src_ref=src if kind == "gather" else src.at[idx], dst_ref=dst.at[me],
                send_sem=send_sems.at[2 * k + t], recv_sem=recv_sems.at[2 * k + t],
                device_id=dev, device_id_type=pl.DeviceIdType.MESH))
    return local, remote


def _start_exchange(kind, *refs):
    local, remote = _exchange_copies(kind, *refs)
    for cp in local + remote:
        cp.start()


def _wait_exchange(kind, *refs):
    local, remote = _exchange_copies(kind, *refs)
    for cp in remote:
        cp.wait_recv()
    for cp in remote:
        cp.wait_send()
    for cp in local:
        cp.wait()


def _exchange_out_shapes(kind, a, b):
    if kind == "gather":
        return [jax.ShapeDtypeStruct((N_DEV,) + a.shape, a.dtype), jax.ShapeDtypeStruct((N_DEV,) + b.shape, b.dtype)]
    return [jax.ShapeDtypeStruct(a.shape, a.dtype), jax.ShapeDtypeStruct(b.shape, b.dtype)]


def _gather_two_level(a, b, name):
    def body(a_ref, b_ref, ga_ref, gb_ref, send_sems, recv_sems, loc_sems):
        x, y, c = _mesh_pos()
        slot_of = lambda px, py, pc: 4 * px + 2 * py + pc
        me, sib = slot_of(x, y, c), slot_of(x, y, 1 - c)
        chips = [(1 - x, y), (x, 1 - y), (1 - x, 1 - y)]
        pairs = ((a_ref, ga_ref), (b_ref, gb_ref))

        def copy(k, t, slot, to, src=None):
            dst = pairs[t][1].at[slot]
            return pltpu.make_async_remote_copy(
                src_ref=dst if src is None else src, dst_ref=dst, send_sem=send_sems.at[2 * k + t],
                recv_sem=recv_sems.at[2 * k + t], device_id=to, device_id_type=pl.DeviceIdType.MESH)

        local = [pltpu.make_async_copy(src, dst.at[me], loc_sems.at[t]) for t, (src, dst) in enumerate(pairs)]
        first = []
        for t, (src, _) in enumerate(pairs):
            first.append(copy(0, t, me, (x, y, 1 - c), src))
            first += [copy(1 + j, t, me, (*chip, c), src) for j, chip in enumerate(chips)]
        for cp in local + first:
            cp.start()
        passed = []
        for j, chip in enumerate(chips):
            for t in range(2):
                landed = slot_of(*chip, c)
                copy(1 + j, t, landed, (x, y, c)).wait_recv()
                cp = copy(4 + j, t, landed, (x, y, 1 - c))
                cp.start()
                passed.append(cp)
        for t in range(2):
            copy(0, t, sib, (x, y, c)).wait_recv()
            for j, chip in enumerate(chips):
                copy(4 + j, t, slot_of(*chip, 1 - c), (x, y, c)).wait_recv()
        for cp in first + passed:
            cp.wait_send()
        for cp in local:
            cp.wait()

    return pl.pallas_call(
        body, name=name,
        in_specs=[_ANY, _ANY], out_specs=[_ANY, _ANY],
        out_shape=_exchange_out_shapes("gather", a, b),
        scratch_shapes=_EXCHANGE_SEMS,
    )(a, b)


def _adam_math(w, g, m, v):
    m_new = ADAM_B1 * m + (1.0 - ADAM_B1) * g
    v_new = ADAM_B2 * v + (1.0 - ADAM_B2) * (g * g)
    m_hat = m_new / (1.0 - ADAM_B1 ** ADAM_STEP)
    v_hat = v_new / (1.0 - ADAM_B2 ** ADAM_STEP)
    delta = -ADAM_LR * (m_hat / (jnp.sqrt(v_hat) + ADAM_EPS) + ADAM_WD * w)
    return delta, m_new, v_new


def _sum_adamw(gparts, w, m, v, name):
    L, R, C = w.shape
    tr = min(128, R)

    def body(*refs):
        gp_refs = refs[:L]
        w_ref, m_ref, v_ref, g_ref, d_ref, nm_ref, nv_ref = refs[L:]
        for l in range(L):
            g = gp_refs[l][0].astype(F32)
            for s in range(1, N_DEV):
                g = g + gp_refs[l][s].astype(F32)
            d, mn, vn = _adam_math(w_ref[l], g, m_ref[l], v_ref[l])
            g_ref[l] = g
            d_ref[l] = d
            nm_ref[l] = mn
            nv_ref[l] = vn

    blk = pl.BlockSpec((L, tr, C), lambda r: (0, r, 0))
    return pl.pallas_call(
        body, name=name,
        grid=(R // tr,),
        in_specs=[pl.BlockSpec((N_DEV, tr, C), lambda r: (0, r, 0))] * L + [blk, blk, blk],
        out_specs=[blk, blk, blk, blk],
        out_shape=[jax.ShapeDtypeStruct((L, R, C), F32)] * 4,
        compiler_params=_cp(("parallel",), _VMEM_WIDE),
    )(*gparts, w, m, v)


def _small_update(gpack, wpack, mpack, vpack):
    R = gpack.shape[0]
    VM = pl.BlockSpec(memory_space=pltpu.VMEM)

    def body(g_ref, w_ref, m_ref, v_ref, gs_ref, d_ref, nm_ref, nv_ref, buf, send_sems, recv_sems):
        peers, me = _peers()
        buf[me] = g_ref[...]
        copies = []
        for k, (dev, _) in enumerate(peers):
            cp = pltpu.make_async_remote_copy(
                src_ref=g_ref, dst_ref=buf.at[me], send_sem=send_sems.at[k], recv_sem=recv_sems.at[k],
                device_id=dev, device_id_type=pl.DeviceIdType.MESH)
            cp.start()
            copies.append(cp)
        for cp in copies:
            cp.wait_recv()
        for cp in copies:
            cp.wait_send()
        g = buf[0]
        for s in range(1, N_DEV):
            g = g + buf[s]
        d, mn, vn = _adam_math(w_ref[...], g, m_ref[...], v_ref[...])
        gs_ref[...] = g
        d_ref[...] = d
        nm_ref[...] = mn
        nv_ref[...] = vn

    return pl.pallas_call(
        body, name="small_update",
        in_specs=[VM] * 4, out_specs=[VM] * 4,
        out_shape=[jax.ShapeDtypeStruct((R, 128), F32)] * 4,
        scratch_shapes=[pltpu.VMEM((N_DEV, R, 128), F32), pltpu.SemaphoreType.DMA((7,)), pltpu.SemaphoreType.DMA((7,))],
        compiler_params=_cp(None, _VMEM_MID),
    )(gpack, wpack, mpack, vpack)


_SMALL = ("norm_g", "b_f", "q_norm_g", "k_norm_g", "w_pool", "pool_scale")


def _pack(parts):
    flat = jnp.concatenate([p.reshape(-1) for p in parts])
    n = flat.shape[0]
    rows = -(-n // (8 * 128)) * 8
    return jnp.pad(flat, (0, rows * 128 - n)).reshape(rows, 128)


def _unpack(packed, like):
    flat = packed.reshape(-1)
    out, o = [], 0
    for p in like:
        out.append(flat[o:o + p.size].reshape(p.shape))
        o += p.size
    return out


def kernel(x, norm_g, w_in, b_f, q_norm_g, k_norm_g, w_pool, pool_scale, w_out, loss_target, m_norm_g, m_w_in, m_b_f, m_q_norm_g, m_k_norm_g, m_w_pool, m_pool_scale, m_w_out, v_norm_g, v_w_in, v_b_f, v_q_norm_g, v_k_norm_g, v_w_pool, v_pool_scale, v_w_out):
    L = w_in.shape[0]

    loss_local, dx, grads, received = _train_step(x[0], loss_target[0], norm_g, w_in.astype(BF16), b_f, q_norm_g,
                                                  k_norm_g, w_pool, pool_scale, w_out.astype(BF16))
    loss = lax.psum(loss_local, MESH_AXES)
    g = {k: jnp.stack([grads[l][k] for l in range(L)]) for k in _SMALL}

    g_win, d_win, nm_win, nv_win = _sum_adamw([r[0] for r in received], w_in, m_w_in, v_w_in, "adamw_w_in")
    g_wout, d_wout, nm_wout, nv_wout = _sum_adamw([r[1] for r in received], w_out, m_w_out, v_w_out, "adamw_w_out")

    ws = dict(norm_g=norm_g, b_f=b_f, q_norm_g=q_norm_g, k_norm_g=k_norm_g, w_pool=w_pool, pool_scale=pool_scale)
    ms = dict(norm_g=m_norm_g, b_f=m_b_f, q_norm_g=m_q_norm_g, k_norm_g=m_k_norm_g, w_pool=m_w_pool, pool_scale=m_pool_scale)
    vs = dict(norm_g=v_norm_g, b_f=v_b_f, q_norm_g=v_q_norm_g, k_norm_g=v_k_norm_g, w_pool=v_w_pool, pool_scale=v_pool_scale)
    like = [ws[k] for k in _SMALL]
    gs_p, d_p, nm_p, nv_p = _small_update(_pack([g[k] for k in _SMALL]), _pack(like),
                                          _pack([ms[k] for k in _SMALL]), _pack([vs[k] for k in _SMALL]))
    gs = dict(zip(_SMALL, _unpack(gs_p, like)))
    ds = dict(zip(_SMALL, _unpack(d_p, like)))
    nms = dict(zip(_SMALL, _unpack(nm_p, like)))
    nvs = dict(zip(_SMALL, _unpack(nv_p, like)))
    gs["w_in"], ds["w_in"], nms["w_in"], nvs["w_in"] = g_win, d_win, nm_win, nv_win
    gs["w_out"], ds["w_out"], nms["w_out"], nvs["w_out"] = g_wout, d_wout, nm_wout, nv_wout

    order = ("norm_g", "w_in", "b_f", "q_norm_g", "k_norm_g", "w_pool", "pool_scale", "w_out")
    return (loss, dx[None], *[gs[k] for k in order], *[ds[k] for k in order],
            *[nms[k] for k in order], *[nvs[k] for k in order])
```

```python
import jax
import jax.numpy as jnp
from jax import lax
from jax.experimental import pallas as pl
from jax.experimental.pallas import tpu as pltpu

F32 = jnp.float32
BF16 = jnp.bfloat16

EPS = 1e-6
NEG = -1e30
HEAD_DIM = 64
FOX_HEADS = 8
FOX_W = 512
POOL_W = 256
SB_W = 256
D_MIX = 1024
N_FF = 8
N_MAIN = 3584
N_FFPAD = 128
OFF_FQ, OFF_FK, OFF_FV, OFF_FG = 0, 512, 1024, 1536
OFF_PX, OFF_PG = 2048, 2304
OFF_SQ, OFF_SK, OFF_SV, OFF_SG = 2560, 2816, 3072, 3328
D_IN = 3592
Q_SCALE = HEAD_DIM ** -0.5

ADAM_LR = 0.001
ADAM_B1 = 0.9
ADAM_B2 = 0.999
ADAM_EPS = 1e-08
ADAM_WD = 0.01
ADAM_STEP = 10

N_DEV = 8
MESH_AXES = ("x", "y", "c")

_T = 256
_TM = 512
_TM_ROWS = 512
_TM_FWD, _TN_FWD = 2048, 512
_TM_DX = 512
_TK_DW = 1024
_VMEM_V7X = 64 << 20
_VMEM_BIG = _VMEM_V7X - (8 << 20)
_VMEM_MID = 40 << 20
_VMEM_WIDE = 48 << 20


def _cp(sem=None, vmem=None):
    kw = {}
    if sem is not None:
        kw["dimension_semantics"] = sem
    if vmem is not None:
        kw["vmem_limit_bytes"] = vmem
    return pltpu.CompilerParams(**kw)


def _dot(a, b):
    return jnp.dot(a, b, preferred_element_type=F32)


def _dot_nt(a, b):
    return lax.dot_general(a, b, (((1,), (1,)), ((), ())), preferred_element_type=F32)


def _dot_tn(a, b):
    return lax.dot_general(a, b, (((0,), (0,)), ((), ())), preferred_element_type=F32)


def _mm2(v, m, left=False):
    hi = v.astype(BF16)
    lo = (v - hi.astype(F32)).astype(BF16)
    if left:
        return _dot(m, hi) + _dot(m, lo)
    return _dot(hi, m) + _dot(lo, m)


def _mm3(v, m, left=False):
    a1 = v.astype(BF16)
    r1 = v - a1.astype(F32)
    a2 = r1.astype(BF16)
    a3 = (r1 - a2.astype(F32)).astype(BF16)
    if left:
        return _dot(m, a1) + _dot(m, a2) + _dot(m, a3)
    return _dot(a1, m) + _dot(a2, m) + _dot(a3, m)


def _sigmoid(z):
    return 1.0 / (1.0 + jnp.exp(-z))


def _rms_rows(x):
    return lax.rsqrt(jnp.mean(x * x, axis=-1, keepdims=True) + EPS)


def _inproj_fwd(x, g, wm, wff):
    S, D = x.shape
    tm = min(_TM_FWD, S)
    tn = _TN_FWD

    def body(x_ref, g_ref, w_ref, wff_ref, o_ref, off_ref, ht_ref, h_ref):
        @pl.when(pl.program_id(1) == 0)
        def _():
            xv = x_ref[...]
            h = (xv * _rms_rows(xv)) * g_ref[...]
            h_ref[...] = h.astype(BF16)
            ht_ref[...] = h.T.astype(BF16)
            off_ref[...] = _dot(h_ref[...], wff_ref[...])

        o_ref[...] = _dot(h_ref[...], w_ref[...])

    return pl.pallas_call(
        body, name="inproj_fwd",
        grid=(S // tm, N_MAIN // tn),
        in_specs=[pl.BlockSpec((tm, D), lambda i, j: (i, 0)),
                  pl.BlockSpec((1, D), lambda i, j: (0, 0)),
                  pl.BlockSpec((D, tn), lambda i, j: (0, j)),
                  pl.BlockSpec((D, N_FFPAD), lambda i, j: (0, 0))],
        out_specs=[pl.BlockSpec((tm, tn), lambda i, j: (i, j)),
                   pl.BlockSpec((tm, N_FFPAD), lambda i, j: (i, 0)),
                   pl.BlockSpec((D, tm), lambda i, j: (0, i))],
        out_shape=[jax.ShapeDtypeStruct((S, N_MAIN), F32), jax.ShapeDtypeStruct((S, N_FFPAD), F32),
                   jax.ShapeDtypeStruct((D, S), BF16)],
        scratch_shapes=[pltpu.VMEM((tm, D), BF16)],
        compiler_params=_cp(("parallel", "arbitrary"), _VMEM_WIDE),
    )(x, g, wm, wff)


def _head_norm(x, g, bd):
    ss = _mm2(x * x, bd)
    r = lax.rsqrt(ss * (1.0 / HEAD_DIM) + EPS)
    return (x * r) * g


def _fox_prep(proj, pff, bfp, gq, gk, bd, ex, tril):
    S = proj.shape[0]
    T = tril.shape[0]

    def body(q_ref, k_ref, ff_ref, b_ref, gq_ref, gk_ref, bd_ref, ex_ref, tri_ref,
             qs_ref, kn_ref, cc_ref, cqb_ref, carry):
        @pl.when(pl.program_id(0) == 0)
        def _():
            carry[...] = jnp.zeros_like(carry)

        bdv = bd_ref[...]
        qs_ref[...] = (_head_norm(q_ref[...], gq_ref[...], bdv) * Q_SCALE).astype(BF16)
        kn_ref[...] = _head_norm(k_ref[...], gk_ref[...], bdv).astype(BF16)
        u = ff_ref[...] + b_ref[...]
        lf = jnp.minimum(u, 0.0) - jnp.log1p(jnp.exp(-jnp.abs(u)))
        c = _mm3(lf, tri_ref[...], left=True) + carry[0:1, :]
        carry[0:1, :] = c[T - 1:T, :]
        cc_ref[...] = c
        cqb_ref[...] = _mm3(c, ex_ref[...])

    return pl.pallas_call(
        body, name="fox_prep",
        grid=(S // T,),
        in_specs=[pl.BlockSpec((T, FOX_W), lambda i: (i, OFF_FQ // FOX_W)),
                  pl.BlockSpec((T, FOX_W), lambda i: (i, OFF_FK // FOX_W)),
                  pl.BlockSpec((T, N_FFPAD), lambda i: (i, 0)),
                  pl.BlockSpec((1, N_FFPAD), lambda i: (0, 0)),
                  pl.BlockSpec((1, FOX_W), lambda i: (0, 0)),
                  pl.BlockSpec((1, FOX_W), lambda i: (0, 0)),
                  pl.BlockSpec((FOX_W, FOX_W), lambda i: (0, 0)),
                  pl.BlockSpec((N_FFPAD, FOX_W), lambda i: (0, 0)),
                  pl.BlockSpec((T, T), lambda i: (0, 0))],
        out_specs=[pl.BlockSpec((T, FOX_W), lambda i: (i, 0)),
                   pl.BlockSpec((T, FOX_W), lambda i: (i, 0)),
                   pl.BlockSpec((T, N_FFPAD), lambda i: (i, 0)),
                   pl.BlockSpec((T, FOX_W), lambda i: (i, 0))],
        out_shape=[jax.ShapeDtypeStruct((S, FOX_W), BF16), jax.ShapeDtypeStruct((S, FOX_W), BF16),
                   jax.ShapeDtypeStruct((S, N_FFPAD), F32), jax.ShapeDtypeStruct((S, FOX_W), F32)],
        scratch_shapes=[pltpu.VMEM((8, N_FFPAD), F32)],
        compiler_params=_cp(("arbitrary",), _VMEM_MID),
    )(proj, proj, pff, bfp, gq, gk, bd, ex, tril)


def _pair_blk(S, off=0):
    return pl.BlockSpec((S, 128), lambda p: (0, off + p), pipeline_mode=pl.Buffered(1))


def _pair_rows(S):
    return pl.BlockSpec((None, 8, S), lambda p: (p, 0, 0), pipeline_mode=pl.Buffered(1))


def _head_masks(S):
    return lax.broadcasted_iota(jnp.int32, (S, 128), 1) < HEAD_DIM


_EXP_ZERO = 104.0


def _spread_heads(x):
    src = lax.broadcasted_iota(jnp.int32, (128, 128), 0)
    return (_mm3(x, (src == 0).astype(BF16)), _mm3(x, (src == HEAD_DIM).astype(BF16)))


def _score_bounds(q, k):
    same_head = ((lax.broadcasted_iota(jnp.int32, (128, 128), 0) < HEAD_DIM)
                 == (lax.broadcasted_iota(jnp.int32, (128, 128), 1) < HEAD_DIM)).astype(BF16)

    def max_norm2(x):
        xf = x.astype(F32)
        return jnp.max(_mm2(xf * xf, same_head), axis=0, keepdims=True)

    z = jnp.sqrt(max_norm2(q) * max_norm2(k))
    z = jnp.where(z == z, z, jnp.inf)
    return jnp.max(z[:, 0:1]) * 1.001 + 1e-3, jnp.max(z[:, 64:65]) * 1.001 + 1e-3


def _for_tiles_back(i, n, tiles_fn, fours=False):
    if fours:
        def four(t, c):
            tiles_fn([i - 1 - 4 * t, i - 2 - 4 * t, i - 3 - 4 * t, i - 4 - 4 * t])
            return c

        lax.fori_loop(0, lax.shift_right_logical(n, 2), four, 0)
        rest = i - (n & ~3)

        @pl.when((n & 2) != 0)
        def _():
            tiles_fn([rest - 1, rest - 2])
    else:
        def two(t, c):
            tiles_fn([i - 1 - 2 * t, i - 2 - 2 * t])
            return c

        lax.fori_loop(0, lax.shift_right_logical(n, 1), two, 0)

    @pl.when((n & 1) != 0)
    def _():
        tiles_fn([i - n])


def _fox_tiles_back(cr_ref, i, r0, zba, zbb):
    last = cr_ref[:, pl.ds(0, 128)]
    first = cr_ref[:, pl.ds(r0, 128)]
    alive_a = 2.0 * zba + first[0:1, 0:1] - last[2:3, :] > -_EXP_ZERO
    alive_b = 2.0 * zbb + first[1:2, 0:1] - last[3:4, :] > -_EXP_ZERO
    before = lax.broadcasted_iota(jnp.int32, (1, 128), 1) < i
    return jnp.sum((before & (alive_a | alive_b)).astype(jnp.int32))


def _fox_fwd(qs, kn, proj, cqb, crow4, ride=None):
    S = qs.shape[0]
    T = min(_T, S)
    nq = S // T
    n_pairs = FOX_W // 128

    def body(*refs):
        if ride is None:
            q_ref, k_ref, v_ref, cq_ref, cr_ref, o_ref, lse_ref = refs[:7]
            qa, qb, vta, vtb, cka, ckb, ma, mb, acca, accb = refs[7:]
        else:
            q_ref, k_ref, v_ref, cq_ref, cr_ref, wa_ref, wb_ref, o_ref, lse_ref, ga_ref, gb_ref = refs[:11]
            qa, qb, vta, vtb, cka, ckb, ma, mb, acca, accb = refs[11:21]
            xrefs = (wa_ref, wb_ref, ga_ref, gb_ref) + tuple(refs[21:])

            @pl.when(pl.program_id(0) == 0)
            def _():
                _start_exchange("gather", *xrefs)

        lane_s = _head_masks(S)
        q = q_ref[...]
        zq = jnp.zeros_like(q)
        qa[...] = jnp.where(lane_s, q, zq)
        qb[...] = jnp.where(lane_s, zq, q)
        cq = cq_ref[...]
        cka[...], ckb[...] = _spread_heads(cq)
        lse_ref[...] = jnp.zeros((8, S), F32)
        row_t = lax.broadcasted_iota(jnp.int32, (128, T), 0) < HEAD_DIM
        zba, zbb = _score_bounds(q, k_ref[...])

        def prep(c, carry):
            c0 = pl.multiple_of(c * T, T)
            vt = v_ref[pl.ds(c0, T), :].T
            vta[:, pl.ds(c0, T)] = jnp.where(row_t, vt, 1.0).astype(BF16)
            vtb[:, pl.ds(c0, T)] = jnp.where(row_t, 1.0, vt).astype(BF16)
            return carry

        lax.fori_loop(0, nq, prep, 0)
        causal = (lax.broadcasted_iota(jnp.int32, (T, T), 0) <= lax.broadcasted_iota(jnp.int32, (T, T), 1))

        heads = ((qa, vta, cka, ma, acca), (qb, vtb, ckb, mb, accb))

        def kv(js, r0, masked):
            cr = cr_ref[:, pl.ds(r0, T)]
            c0s = [pl.multiple_of(j * T, T) for j in js]
            ks = [k_ref[pl.ds(c0, T), :] for c0 in c0s]
            ss = []
            for h, (qr, _, ckr, _, _) in enumerate(heads):
                qh = qr[pl.ds(r0, T), :]
                row = []
                for k, c0 in zip(ks, c0s):
                    s = _dot_nt(k, qh) - jnp.tile(ckr[pl.ds(c0, T), :], (1, T // 128))
                    row.append(jnp.where(causal, s, NEG) if masked else s)
                ss.append(row)
            ms = []
            for h, (row, (_, _, _, mr, _)) in enumerate(zip(ss, heads)):
                top = row[0]
                for s in row[1:]:
                    top = jnp.maximum(top, s)
                m_old = mr[0:1, :]
                ms.append((m_old, jnp.maximum(m_old, jnp.max(top, axis=0, keepdims=True) + cr[h:h + 1, :])))
            ps = [[jnp.exp(s + (cr[h:h + 1, :] - m_new)).astype(BF16) for s in row]
                  for h, (row, (_, m_new)) in enumerate(zip(ss, ms))]
            pvs = []
            for row, (_, vr, _, _, _) in zip(ps, heads):
                pv = _dot(vr[:, pl.ds(c0s[0], T)], row[0])
                for p, c0 in zip(row[1:], c0s[1:]):
                    pv = pv + _dot(vr[:, pl.ds(c0, T)], p)
                pvs.append(pv)
            for pv, (m_old, m_new), (_, _, _, mr, ar) in zip(pvs, ms, heads):
                ar[...] = jnp.exp(m_old - m_new) * ar[...] + pv
                mr[0:1, :] = m_new

        def qblk(i, carry):
            r0 = pl.multiple_of(i * T, T)
            ma[...] = jnp.full((8, T), NEG, F32)
            mb[...] = jnp.full((8, T), NEG, F32)
            acca[...] = jnp.zeros((128, T), F32)
            accb[...] = jnp.zeros((128, T), F32)
            kv([i], r0, True)
            done = _fox_tiles_back(cr_ref, i, r0, zba, zbb)
            _for_tiles_back(i, done, lambda js: kv(js, r0, False), fours=True)
            aa = acca[...]
            ab = accb[...]
            la = aa[64:65, :]
            lb = ab[0:1, :]
            o_ref[pl.ds(r0, T), :] = jnp.where(row_t, aa / la, ab / lb).T
            lse_ref[0:1, pl.ds(r0, T)] = ma[0:1, :] + jnp.log(la)
            lse_ref[1:2, pl.ds(r0, T)] = mb[0:1, :] + jnp.log(lb)
            lse_ref[2:3, pl.ds(r0, T)] = jnp.broadcast_to(done.astype(F32), (1, T))
            return carry

        lax.fori_loop(0, nq, qblk, 0)
        if ride is not None:
            @pl.when(pl.program_id(0) == n_pairs - 1)
            def _():
                _wait_exchange("gather", *xrefs)

    extra = () if ride is None else tuple(ride)
    return pl.pallas_call(
        body, name="fox_fwd" if ride is None else "fox_fwd_gather",
        grid=(n_pairs,),
        in_specs=[_pair_blk(S), _pair_blk(S), _pair_blk(S, OFF_FV // 128), _pair_blk(S), _pair_rows(S)]
        + [_ANY] * len(extra),
        out_specs=[_pair_blk(S), _pair_rows(S)] + [_ANY] * len(extra),
        out_shape=[jax.ShapeDtypeStruct((S, FOX_W), F32), jax.ShapeDtypeStruct((n_pairs, 8, S), F32)]
        + (_exchange_out_shapes("gather", *extra) if extra else []),
        scratch_shapes=[pltpu.VMEM((S, 128), BF16)] * 2 + [pltpu.VMEM((128, S), BF16)] * 2
        + [pltpu.VMEM((S, 128), F32)] * 2 + [pltpu.VMEM((8, T), F32)] * 2 + [pltpu.VMEM((128, T), F32)] * 2
        + (_EXCHANGE_SEMS if extra else []),
        compiler_params=_cp(("arbitrary",), _VMEM_BIG),
    )(qs, kn, proj, cqb, crow4, *extra)


def _softplus_parts(z):
    e = jnp.exp(-jnp.abs(z))
    return e, jnp.maximum(z, 0.0) + jnp.log(1.0 + e)


def _sb_fwd(proj, triu):
    S = proj.shape[0]
    T = triu.shape[0]
    nq = S // T

    def body(q_ref, k_ref, v_ref, tri_ref, o_ref, lt_ref, qa, qb, kb, vt, ra, rb, acca, accb):
        lane_s = _head_masks(S)
        q = (q_ref[...] * Q_SCALE).astype(BF16)
        zq = jnp.zeros_like(q)
        qa[...] = jnp.where(lane_s, q, zq)
        qb[...] = jnp.where(lane_s, zq, q)
        kb[...] = k_ref[...].astype(BF16)
        lt_ref[...] = jnp.zeros((8, S), F32)
        row_t = lax.broadcasted_iota(jnp.int32, (128, T), 0) < HEAD_DIM
        zba, zbb = _score_bounds(q, kb[...])

        def prep(c, carry):
            c0 = pl.multiple_of(c * T, T)
            vt[:, pl.ds(c0, T)] = v_ref[pl.ds(c0, T), :].T.astype(BF16)
            return carry

        lax.fori_loop(0, nq, prep, 0)
        strict = (lax.broadcasted_iota(jnp.int32, (T, T), 0) < lax.broadcasted_iota(jnp.int32, (T, T), 1))

        heads = ((qa, ra, acca), (qb, rb, accb))

        def kv(tiles, r0):
            tri = tri_ref[...]
            c0s = [pl.multiple_of(j * T, T) for j, _ in tiles]
            ks = [kb[pl.ds(c0, T), :] for c0 in c0s]
            qhs = [qr[pl.ds(r0, T), :] for qr, _, _ in heads]
            zs = [[_dot_nt(k, qh) for k in ks] for qh in qhs]
            lbs = [[jnp.where(strict, -_softplus_parts(z)[1], 0.0) if masked else -_softplus_parts(z)[1]
                    for z, (_, masked) in zip(row, tiles)] for row in zs]
            incs = [[_mm2(lb, tri, left=True) for lb in row] for row in lbs]
            avs = []
            for (_, r_ref, _), zrow, irow in zip(heads, zs, incs):
                r = r_ref[0:1, :]
                av = None
                for z, inc, c0, (_, masked) in zip(zrow, irow, c0s, tiles):
                    a = jnp.exp(z + inc + r)
                    if masked:
                        a = jnp.where(strict, a, 0.0)
                    term = _dot(vt[:, pl.ds(c0, T)], a.astype(BF16))
                    av = term if av is None else av + term
                    r = r + inc[0:1, :]
                avs.append((av, r))
            for (_, r_ref, acc_ref), (av, r) in zip(heads, avs):
                r_ref[0:1, :] = r
                acc_ref[...] = acc_ref[...] + av

        def qblk(i, carry):
            r0 = pl.multiple_of(i * T, T)
            ra[...] = jnp.zeros((8, T), F32)
            rb[...] = jnp.zeros((8, T), F32)
            acca[...] = jnp.zeros((128, T), F32)
            accb[...] = jnp.zeros((128, T), F32)

            @pl.when(i == 0)
            def _():
                kv([(i, True)], r0)

            @pl.when(i > 0)
            def _():
                kv([(i, True), (i - 1, False)], r0)

            def alive():
                return jnp.maximum(jnp.max(ra[0:1, :]) + zba, jnp.max(rb[0:1, :]) + zbb) > -_EXP_ZERO

            def cond(st):
                return (st[0] < i) & st[1]

            def step(st):
                kv([(i - 1 - st[0], False)], r0)
                return st[0] + 1, alive()

            done, _ = lax.while_loop(cond, step, (jnp.minimum(i, 1), alive()))
            o_ref[pl.ds(r0, T), :] = jnp.where(row_t, acca[...], accb[...]).T
            lt_ref[0:1, pl.ds(r0, T)] = ra[0:1, :]
            lt_ref[1:2, pl.ds(r0, T)] = rb[0:1, :]
            lt_ref[2:3, pl.ds(r0, T)] = jnp.broadcast_to(done.astype(F32), (1, T))
            return carry

        lax.fori_loop(0, nq, qblk, 0)

    return pl.pallas_call(
        body, name="sb_fwd",
        grid=(SB_W // 128,),
        in_specs=[_pair_blk(S, OFF_SQ // 128), _pair_blk(S, OFF_SK // 128), _pair_blk(S, OFF_SV // 128),
                  pl.BlockSpec((T, T), lambda p: (0, 0))],
        out_specs=[_pair_blk(S), _pair_rows(S)],
        out_shape=[jax.ShapeDtypeStruct((S, SB_W), F32), jax.ShapeDtypeStruct((SB_W // 128, 8, S), F32)],
        scratch_shapes=[pltpu.VMEM((S, 128), BF16)] * 3 + [pltpu.VMEM((128, S), BF16)]
        + [pltpu.VMEM((8, T), F32)] * 2 + [pltpu.VMEM((128, T), F32)] * 2,
        compiler_params=_cp(("arbitrary",), _VMEM_BIG),
    )(proj, proj, proj, triu)


def _pool_window_lanes(shape):
    lane = lax.broadcasted_iota(jnp.int32, shape, 1)
    return jnp.where(lane < 64, 2, jnp.where(lane < 128, 4, jnp.where(lane < 192, 8, 16)))


def _pool_fwd(proj):
    S = proj.shape[0]

    def body(x_ref, o_ref):
        x = x_ref[...]
        t = lax.broadcasted_iota(jnp.int32, x.shape, 0)
        lane = lax.broadcasted_iota(jnp.int32, x.shape, 1)

        def back(a, k):
            return jnp.where(t >= k, pltpu.roll(a, k, 0), 0.0)

        s1 = x + back(x, 1)
        s2 = s1 + back(s1, 2)
        s4 = s2 + back(s2, 4)
        s8 = s4 + back(s4, 8)
        win = jnp.where(lane < 64, s1, jnp.where(lane < 128, s2, jnp.where(lane < 192, s4, s8)))
        cnt = jnp.minimum(t + 1, _pool_window_lanes(x.shape)).astype(F32)
        o_ref[...] = win / cnt - x

    return pl.pallas_call(
        body, name="pool_fwd",
        grid=(1,),
        in_specs=[pl.BlockSpec((S, POOL_W), lambda i: (0, OFF_PX // POOL_W))],
        out_specs=pl.BlockSpec((S, POOL_W), lambda i: (0, 0)),
        out_shape=jax.ShapeDtypeStruct((S, POOL_W), F32),
        compiler_params=_cp(("arbitrary",), _VMEM_BIG),
    )(proj)


def _silu(g):
    return g * _sigmoid(g)


def _mix_out(fo, so, pooled, proj, wbd, scale, wout, x):
    S, D = x.shape
    tm = min(_TM_ROWS, S)

    def body(fo_ref, fg_ref, so_ref, sg_ref, pl_ref, pg_ref, wbd_ref, sc_ref, w_ref, x_ref, y_ref, mxt_ref, mx_ref):
        parts = ((0, fo_ref[...] * _silu(fg_ref[...])),
                 (FOX_W, (_dot(pl_ref[...].astype(BF16), wbd_ref[...]) * sc_ref[...]) * _silu(pg_ref[...])),
                 (FOX_W + POOL_W, so_ref[...] * _silu(sg_ref[...])))
        for off, part in parts:
            w = part.shape[1]
            mx_ref[:, off:off + w] = part.astype(BF16)
            mxt_ref[off:off + w, :] = part.T.astype(BF16)
        y_ref[...] = x_ref[...] + _dot(mx_ref[...], w_ref[...])

    return pl.pallas_call(
        body, name="mix_out",
        grid=(S // tm,),
        in_specs=[pl.BlockSpec((tm, FOX_W), lambda i: (i, 0)),
                  pl.BlockSpec((tm, FOX_W), lambda i: (i, OFF_FG // FOX_W)),
                  pl.BlockSpec((tm, SB_W), lambda i: (i, 0)),
                  pl.BlockSpec((tm, SB_W), lambda i: (i, OFF_SG // SB_W)),
                  pl.BlockSpec((tm, POOL_W), lambda i: (i, 0)),
                  pl.BlockSpec((tm, POOL_W), lambda i: (i, OFF_PG // POOL_W)),
                  pl.BlockSpec((POOL_W, POOL_W), lambda i: (0, 0)),
                  pl.BlockSpec((1, POOL_W), lambda i: (0, 0)),
                  pl.BlockSpec((D_MIX, D), lambda i: (0, 0)),
                  pl.BlockSpec((tm, D), lambda i: (i, 0))],
        out_specs=[pl.BlockSpec((tm, D), lambda i: (i, 0)), pl.BlockSpec((D_MIX, tm), lambda i: (0, i))],
        out_shape=[jax.ShapeDtypeStruct((S, D), F32), jax.ShapeDtypeStruct((D_MIX, S), BF16)],
        scratch_shapes=[pltpu.VMEM((tm, D_MIX), BF16)],
        compiler_params=_cp(("parallel",), _VMEM_MID),
    )(fo, proj, so, proj, pooled, proj, wbd, scale, wout, x)


def _loss_head(y, target):
    S, D = y.shape
    tm = min(_TM, S)

    def body(y_ref, t_ref, dy_ref, ls_ref):
        @pl.when(pl.program_id(0) == 0)
        def _():
            ls_ref[...] = jnp.zeros_like(ls_ref)

        e = y_ref[...] - t_ref[...]
        dy_ref[...] = e * (1.0 / D)
        ls_ref[...] = ls_ref[...] + jnp.sum(e * e) * (0.5 / D)

    dy, ls = pl.pallas_call(
        body, name="loss_head",
        grid=(S // tm,),
        in_specs=[pl.BlockSpec((tm, D), lambda i: (i, 0)), pl.BlockSpec((tm, D), lambda i: (i, 0))],
        out_specs=[pl.BlockSpec((tm, D), lambda i: (i, 0)), pl.BlockSpec((8, 128), lambda i: (0, 0))],
        out_shape=[jax.ShapeDtypeStruct((S, D), F32), jax.ShapeDtypeStruct((8, 128), F32)],
        compiler_params=_cp(("arbitrary",), _VMEM_MID),
    )(y, target)
    return dy, ls[0, 0]


def _dsilu(g):
    s = _sigmoid(g)
    return s * (1.0 + g * (1.0 - s))


def _gate_bwd(dy, wout, fo, so, pooled, proj, wbd, scale):
    S, D = dy.shape
    tm = min(_TM_ROWS, S)

    def body(dy_ref, w_ref, fo_ref, fg_ref, so_ref, sg_ref, pl_ref, pg_ref, wbd_ref, sc_ref,
             dfo_ref, dfg_ref, dso_ref, dsg_ref, dpg_ref, dpl_ref, dsc_ref, dwbd_ref):
        @pl.when(pl.program_id(0) == 0)
        def _():
            dsc_ref[...] = jnp.zeros_like(dsc_ref)
            dwbd_ref[...] = jnp.zeros_like(dwbd_ref)

        dm = _dot_nt(dy_ref[...].astype(BF16), w_ref[...])
        dmf = dm[:, 0:FOX_W]
        dmp = dm[:, FOX_W:FOX_W + POOL_W]
        dms = dm[:, FOX_W + POOL_W:D_MIX]
        fg = fg_ref[...]
        dfo_ref[...] = dmf * _silu(fg)
        dfg_ref[...] = (dmf * fo_ref[...] * _dsilu(fg)).astype(BF16)
        sg = sg_ref[...]
        dso_ref[...] = dms * _silu(sg)
        dsg_ref[...] = (dms * so_ref[...] * _dsilu(sg)).astype(BF16)
        pg = pg_ref[...]
        plb = pl_ref[...].astype(BF16)
        yw = _dot(plb, wbd_ref[...])
        sc = sc_ref[...]
        dpg_ref[...] = (dmp * (yw * sc) * _dsilu(pg)).astype(BF16)
        dys = dmp * _silu(pg)
        dsc_ref[...] = dsc_ref[...] + jnp.sum(dys * yw, axis=0, keepdims=True)
        dyw = (dys * sc).astype(BF16)
        dpl_ref[...] = _dot_nt(dyw, wbd_ref[...])
        dwbd_ref[...] = dwbd_ref[...] + _dot_tn(plb, dyw)

    return pl.pallas_call(
        body, name="gate_bwd",
        grid=(S // tm,),
        in_specs=[pl.BlockSpec((tm, D), lambda i: (i, 0)),
                  pl.BlockSpec((D_MIX, D), lambda i: (0, 0)),
                  pl.BlockSpec((tm, FOX_W), lambda i: (i, 0)),
                  pl.BlockSpec((tm, FOX_W), lambda i: (i, OFF_FG // FOX_W)),
                  pl.BlockSpec((tm, SB_W), lambda i: (i, 0)),
                  pl.BlockSpec((tm, SB_W), lambda i: (i, OFF_SG // SB_W)),
                  pl.BlockSpec((tm, POOL_W), lambda i: (i, 0)),
                  pl.BlockSpec((tm, POOL_W), lambda i: (i, OFF_PG // POOL_W)),
                  pl.BlockSpec((POOL_W, POOL_W), lambda i: (0, 0)),
                  pl.BlockSpec((1, POOL_W), lambda i: (0, 0))],
        out_specs=[pl.BlockSpec((tm, FOX_W), lambda i: (i, 0)),
                   pl.BlockSpec((tm, FOX_W), lambda i: (i, 0)),
                   pl.BlockSpec((tm, SB_W), lambda i: (i, 0)),
                   pl.BlockSpec((tm, SB_W), lambda i: (i, 0)),
                   pl.BlockSpec((tm, POOL_W), lambda i: (i, 0)),
                   pl.BlockSpec((tm, POOL_W), lambda i: (i, 0)),
                   pl.BlockSpec((1, POOL_W), lambda i: (0, 0)),
                   pl.BlockSpec((POOL_W, POOL_W), lambda i: (0, 0))],
        out_shape=[jax.ShapeDtypeStruct((S, FOX_W), F32), jax.ShapeDtypeStruct((S, FOX_W), BF16),
                   jax.ShapeDtypeStruct((S, SB_W), F32), jax.ShapeDtypeStruct((S, SB_W), BF16),
                   jax.ShapeDtypeStruct((S, POOL_W), BF16), jax.ShapeDtypeStruct((S, POOL_W), F32),
                   jax.ShapeDtypeStruct((1, POOL_W), F32), jax.ShapeDtypeStruct((POOL_W, POOL_W), F32)],
        compiler_params=_cp(("arbitrary",), _VMEM_MID),
    )(dy, wout, fo, proj, so, proj, pooled, proj, wbd, scale)


def _matmul_acc(at, b, name):
    M, S = at.shape
    N = b.shape[1]
    tk = min(_TK_DW, S)
    tn = min(512, N)
    nk = S // tk

    def body(a_ref, b_ref, o_ref, acc):
        k = pl.program_id(1)

        @pl.when(k == 0)
        def _():
            acc[...] = jnp.zeros_like(acc)

        acc[...] = acc[...] + _dot(a_ref[...], b_ref[...].astype(BF16))

        @pl.when(k == nk - 1)
        def _():
            o_ref[...] = acc[...].astype(BF16)

    return pl.pallas_call(
        body, name=name,
        grid=(N // tn, nk),
        in_specs=[pl.BlockSpec((M, tk), lambda j, k: (0, k)), pl.BlockSpec((tk, tn), lambda j, k: (k, j))],
        out_specs=pl.BlockSpec((M, tn), lambda j, k: (0, j)),
        out_shape=jax.ShapeDtypeStruct((M, N), BF16),
        scratch_shapes=[pltpu.VMEM((M, tn), F32)],
        compiler_params=_cp(("parallel", "arbitrary"), _VMEM_MID),
    )(at, b)


def _pool_bwd(dpooled):
    S = dpooled.shape[0]

    def body(d_ref, o_ref):
        d = d_ref[...]
        t = lax.broadcasted_iota(jnp.int32, d.shape, 0)
        lane = lax.broadcasted_iota(jnp.int32, d.shape, 1)
        cnt = jnp.minimum(t + 1, _pool_window_lanes(d.shape)).astype(F32)
        u = d / cnt

        def fwd(a, k):
            return jnp.where(t < S - k, pltpu.roll(a, S - k, 0), 0.0)

        s1 = u + fwd(u, 1)
        s2 = s1 + fwd(s1, 2)
        s4 = s2 + fwd(s2, 4)
        s8 = s4 + fwd(s4, 8)
        win = jnp.where(lane < 64, s1, jnp.where(lane < 128, s2, jnp.where(lane < 192, s4, s8)))
        o_ref[...] = (win - d).astype(BF16)

    return pl.pallas_call(
        body, name="pool_bwd",
        grid=(1,),
        in_specs=[pl.BlockSpec((S, POOL_W), lambda i: (0, 0))],
        out_specs=pl.BlockSpec((S, POOL_W), lambda i: (0, 0)),
        out_shape=jax.ShapeDtypeStruct((S, POOL_W), BF16),
        compiler_params=_cp(("arbitrary",), _VMEM_BIG),
    )(dpooled)


def _fox_bwd(qs, kn, proj, dfo, fo, lse, cqb, crow4, ride=None):
    S = qs.shape[0]
    T = min(_T, S)
    nq = S // T
    n_pairs = FOX_W // 128

    def body(*refs):
        if ride is None:
            q_ref, k_ref, v_ref, do_ref, o_ref, lse_ref, cq_ref, cr_ref = refs[:8]
            dq_ref, dk_ref, dv_ref, dck_ref, dcq_ref = refs[8:13]
            scr = refs[13:]
        else:
            q_ref, k_ref, v_ref, do_ref, o_ref, lse_ref, cq_ref, cr_ref, pa_ref, pb_ref = refs[:10]
            dq_ref, dk_ref, dv_ref, dck_ref, dcq_ref, ra_ref, rb_ref = refs[10:17]
            scr = refs[17:32]
            xrefs = (pa_ref, pb_ref, ra_ref, rb_ref) + tuple(refs[32:])

            @pl.when(pl.program_id(0) == 0)
            def _():
                _start_exchange("scatter", *xrefs)

        qa, qb, kta, ktb, vb, doa, dob, cka, ckb, dcka, dckb, dva, dqt, dcqa, dcqb = scr
        lane_s = _head_masks(S)
        q = q_ref[...]
        zq = jnp.zeros_like(q)
        qa[...] = jnp.where(lane_s, q, zq)
        qb[...] = jnp.where(lane_s, zq, q)
        vb[...] = v_ref[...].astype(BF16)
        do = do_ref[...].astype(BF16)
        doa[...] = jnp.where(lane_s, do, zq)
        dob[...] = jnp.where(lane_s, zq, do)
        cq = cq_ref[...]
        cka[...], ckb[...] = _spread_heads(cq)
        zs = jnp.zeros((S, 128), F32)
        dk_ref[...] = zs
        dva[...] = zs
        dcka[...] = zs
        dckb[...] = zs
        dcq_ref[...] = jnp.zeros((8, S), F32)
        row_t = lax.broadcasted_iota(jnp.int32, (128, T), 0) < HEAD_DIM

        def prep(c, carry):
            c0 = pl.multiple_of(c * T, T)
            kt = k_ref[pl.ds(c0, T), :].astype(F32).T
            kta[:, pl.ds(c0, T)] = jnp.where(row_t, kt, 0.0).astype(BF16)
            ktb[:, pl.ds(c0, T)] = jnp.where(row_t, 0.0, kt).astype(BF16)
            return carry

        lax.fori_loop(0, nq, prep, 0)
        causal = (lax.broadcasted_iota(jnp.int32, (T, T), 0) <= lax.broadcasted_iota(jnp.int32, (T, T), 1))

        heads = ((qa, kta, doa, cka, dcka, dcqa), (qb, ktb, dob, ckb, dckb, dcqb))

        def kv(js, r0, lss, dls, masked):
            cr = cr_ref[:, pl.ds(r0, T)]
            c0s = [pl.multiple_of(j * T, T) for j in js]
            ks = [k_ref[pl.ds(c0, T), :] for c0 in c0s]
            vs = [vb[pl.ds(c0, T), :] for c0 in c0s]
            qhs = [hd[0][pl.ds(r0, T), :] for hd in heads]
            dohs = [hd[2][pl.ds(r0, T), :] for hd in heads]
            ss = []
            for h, hd in enumerate(heads):
                row = []
                for k, c0 in zip(ks, c0s):
                    s = _dot_nt(k, qhs[h]) - jnp.tile(hd[3][pl.ds(c0, T), :], (1, T // 128))
                    row.append(jnp.where(causal, s, NEG) if masked else s)
                ss.append(row)
            ps = [[jnp.exp(s + (cr[h:h + 1, :] - lss[h])) for s in row] for h, row in enumerate(ss)]
            dps = [[_dot_nt(v, dohs[h]) for v in vs] for h in range(2)]
            dss = [[p * (dp - dls[h]) for p, dp in zip(ps[h], dps[h])] for h in range(2)]
            pbs = [[p.astype(BF16) for p in row] for row in ps]
            dsbs = [[ds.astype(BF16) for ds in row] for row in dss]
            for t, c0 in enumerate(c0s):
                dva[pl.ds(c0, T), :] = dva[pl.ds(c0, T), :] + (_dot(pbs[0][t], dohs[0]) + _dot(pbs[1][t], dohs[1]))
                dk_ref[pl.ds(c0, T), :] = dk_ref[pl.ds(c0, T), :] + (_dot(dsbs[0][t], qhs[0]) + _dot(dsbs[1][t], qhs[1]))
            dq = None
            for h, hd in enumerate(heads):
                for t, c0 in enumerate(c0s):
                    term = _dot(hd[1][:, pl.ds(c0, T)], dsbs[h][t])
                    dq = term if dq is None else dq + term
            dqt[...] = dqt[...] + dq
            for h, hd in enumerate(heads):
                col = jnp.sum(dss[h][0], axis=0, keepdims=True)
                for ds in dss[h][1:]:
                    col = col + jnp.sum(ds, axis=0, keepdims=True)
                hd[5][0:1, :] = hd[5][0:1, :] + col
                for ds, c0 in zip(dss[h], c0s):
                    fold = ds[:, 0:128]
                    for u in range(1, T // 128):
                        fold = fold + ds[:, 128 * u:128 * (u + 1)]
                    hd[4][pl.ds(c0, T), :] = hd[4][pl.ds(c0, T), :] - fold

        def qblk(i, carry):
            r0 = pl.multiple_of(i * T, T)
            dt = (do_ref[pl.ds(r0, T), :] * o_ref[pl.ds(r0, T), :]).T
            dla = jnp.sum(jnp.where(row_t, dt, 0.0), axis=0, keepdims=True)
            dlb = jnp.sum(jnp.where(row_t, 0.0, dt), axis=0, keepdims=True)
            ls = lse_ref[:, pl.ds(r0, T)]
            lss = (ls[0:1, :], ls[1:2, :])
            back = jnp.max(ls[2:3, :]).astype(jnp.int32)
            dqt[...] = jnp.zeros((128, T), F32)
            dcqa[...] = jnp.zeros((8, T), F32)
            dcqb[...] = jnp.zeros((8, T), F32)
            kv([i], r0, lss, (dla, dlb), True)
            _for_tiles_back(i, back, lambda js: kv(js, r0, lss, (dla, dlb), False), fours=True)
            dq_ref[pl.ds(r0, T), :] = dqt[...].T
            dcq_ref[0:1, pl.ds(r0, T)] = dcqa[0:1, :]
            dcq_ref[1:2, pl.ds(r0, T)] = dcqb[0:1, :]
            return carry

        lax.fori_loop(0, nq, qblk, 0)
        dv_ref[...] = dva[...].astype(BF16)
        dck_ref[...] = jnp.where(lane_s, jnp.sum(dcka[...], axis=1, keepdims=True),
                                 jnp.sum(dckb[...], axis=1, keepdims=True))
        if ride is not None:
            @pl.when(pl.program_id(0) == n_pairs - 1)
            def _():
                _wait_exchange("scatter", *xrefs)

    extra = () if ride is None else tuple(ride)
    return pl.pallas_call(
        body, name="fox_bwd" if ride is None else "fox_bwd_exchange",
        grid=(n_pairs,),
        in_specs=[_pair_blk(S), _pair_blk(S), _pair_blk(S, OFF_FV // 128), _pair_blk(S), _pair_blk(S),
                  _pair_rows(S), _pair_blk(S), _pair_rows(S)] + [_ANY] * len(extra),
        out_specs=[_pair_blk(S), _pair_blk(S), _pair_blk(S), _pair_blk(S), _pair_rows(S)] + [_ANY] * len(extra),
        out_shape=[jax.ShapeDtypeStruct((S, FOX_W), F32), jax.ShapeDtypeStruct((S, FOX_W), F32),
                   jax.ShapeDtypeStruct((S, FOX_W), BF16), jax.ShapeDtypeStruct((S, FOX_W), F32),
                   jax.ShapeDtypeStruct((n_pairs, 8, S), F32)]
        + (_exchange_out_shapes("scatter", *extra) if extra else []),
        scratch_shapes=[pltpu.VMEM((S, 128), BF16)] * 2 + [pltpu.VMEM((128, S), BF16)] * 2
        + [pltpu.VMEM((S, 128), BF16)] * 3 + [pltpu.VMEM((S, 128), F32)] * 5
        + [pltpu.VMEM((128, T), F32)] + [pltpu.VMEM((8, T), F32)] * 2
        + (_EXCHANGE_SEMS if extra else []),
        compiler_params=_cp(("arbitrary",), _VMEM_BIG),
    )(qs, kn, proj, dfo, fo, lse, cqb, crow4, *extra)


def _sb_bwd(proj, dso, ltot, tril):
    S = proj.shape[0]
    T = tril.shape[0]
    nq = S // T

    def body(q_ref, k_ref, v_ref, do_ref, lt_ref, tri_ref, dq_ref, dk_ref, dv_ref,
             qa, qb, k2, kta, ktb, vb, doa, dob, dka, dva, dqt, ra, rb, ga, gb):
        lane_s = _head_masks(S)
        q = (q_ref[...] * Q_SCALE).astype(BF16)
        zq = jnp.zeros_like(q)
        qa[...] = jnp.where(lane_s, q, zq)
        qb[...] = jnp.where(lane_s, zq, q)
        k2[...] = k_ref[...].astype(BF16)
        vb[...] = v_ref[...].astype(BF16)
        do = do_ref[...].astype(BF16)
        doa[...] = jnp.where(lane_s, do, zq)
        dob[...] = jnp.where(lane_s, zq, do)
        dka[...] = jnp.zeros((S, 128), F32)
        dva[...] = jnp.zeros((S, 128), F32)
        row_t = lax.broadcasted_iota(jnp.int32, (128, T), 0) < HEAD_DIM

        def prep(c, carry):
            c0 = pl.multiple_of(c * T, T)
            kt = k_ref[pl.ds(c0, T), :].T
            kta[:, pl.ds(c0, T)] = jnp.where(row_t, kt, 0.0).astype(BF16)
            ktb[:, pl.ds(c0, T)] = jnp.where(row_t, 0.0, kt).astype(BF16)
            return carry

        lax.fori_loop(0, nq, prep, 0)
        strict = (lax.broadcasted_iota(jnp.int32, (T, T), 0) < lax.broadcasted_iota(jnp.int32, (T, T), 1))

        heads = ((qa, kta, doa, ra, ga), (qb, ktb, dob, rb, gb))

        def kv(tiles, r0, lts):
            tri = tri_ref[...]
            c0s = [pl.multiple_of(j * T, T) for j, _ in tiles]
            ks = [k2[pl.ds(c0, T), :] for c0 in c0s]
            vs = [vb[pl.ds(c0, T), :] for c0 in c0s]
            qhs = [hd[0][pl.ds(r0, T), :] for hd in heads]
            dohs = [hd[2][pl.ds(r0, T), :] for hd in heads]
            zs = [[_dot_nt(k, qh) for k in ks] for qh in qhs]
            das = [[_dot_nt(v, doh) for v in vs] for doh in dohs]
            es, lbs = [], []
            for row in zs:
                erow, lrow = [], []
                for z, (_, masked) in zip(row, tiles):
                    e, sp = _softplus_parts(z)
                    erow.append(e)
                    lrow.append(jnp.where(strict, -sp, 0.0) if masked else -sp)
                es.append(erow)
                lbs.append(lrow)
            pres = [[_mm2(lb, tri, left=True) for lb in row] for row in lbs]
            aas, r_ends = [], []
            for hd, zrow, lrow, prow, lt in zip(heads, zs, lbs, pres, lts):
                r = hd[3][0:1, :]
                arow = []
                for z, lb, pre, (_, masked) in zip(zrow, lrow, prow, tiles):
                    a = jnp.exp(z + lb + ((lt - r) - pre))
                    arow.append(jnp.where(strict, a, 0.0) if masked else a)
                    r = r + pre[T - 1:T, :]
                aas.append(arow)
                r_ends.append(r)
            gs = [[a * da for a, da in zip(arow, drow)] for arow, drow in zip(aas, das)]
            gpres = [[_mm2(g, tri, left=True) for g in row] for row in gs]
            dzbs, g_ends = [], []
            for hd, zrow, erow, grow, gprow in zip(heads, zs, es, gs, gpres):
                gc = hd[4][0:1, :]
                drow = []
                for z, e, g, gpre, (_, masked) in zip(zrow, erow, grow, gprow, tiles):
                    inv = 1.0 / (1.0 + e)
                    pos = z >= 0.0
                    sig = jnp.where(pos, 1.0, e) * inv
                    oms = jnp.where(pos, e, 1.0) * inv
                    dz = g * oms - sig * (gc + (gpre - g))
                    if masked:
                        dz = jnp.where(strict, dz, 0.0)
                    drow.append(dz.astype(BF16))
                    gc = gc + gpre[T - 1:T, :]
                dzbs.append(drow)
                g_ends.append(gc)
            dq = None
            for h, hd in enumerate(heads):
                for t, c0 in enumerate(c0s):
                    term = _dot(hd[1][:, pl.ds(c0, T)], dzbs[h][t])
                    dq = term if dq is None else dq + term
            dqt[...] = dqt[...] + dq
            for t, c0 in enumerate(c0s):
                dka[pl.ds(c0, T), :] = dka[pl.ds(c0, T), :] + (_dot(dzbs[0][t], qhs[0]) + _dot(dzbs[1][t], qhs[1]))
                dva[pl.ds(c0, T), :] = dva[pl.ds(c0, T), :] + (_dot(aas[0][t].astype(BF16), dohs[0])
                                                               + _dot(aas[1][t].astype(BF16), dohs[1]))
            for hd, r, gc in zip(heads, r_ends, g_ends):
                hd[3][0:1, :] = r
                hd[4][0:1, :] = gc

        def qblk(i, carry):
            r0 = pl.multiple_of(i * T, T)
            lt = lt_ref[:, pl.ds(r0, T)]
            lts = (lt[0:1, :], lt[1:2, :])
            back = jnp.max(lt[2:3, :]).astype(jnp.int32)
            zt = jnp.zeros((8, T), F32)
            dqt[...] = jnp.zeros((128, T), F32)
            ra[...] = zt
            rb[...] = zt
            ga[...] = zt
            gb[...] = zt

            def inner(j, c):
                kv([(j, False)], r0, lts)
                return c

            @pl.when(back == 0)
            def _():
                kv([(i, True)], r0, lts)

            @pl.when(back > 0)
            def _():
                lax.fori_loop(i - back, i - 1, inner, 0)
                kv([(i - 1, False), (i, True)], r0, lts)
            dq_ref[pl.ds(r0, T), :] = (dqt[...] * Q_SCALE).T.astype(BF16)
            return carry

        lax.fori_loop(0, nq, qblk, 0)
        dk_ref[...] = dka[...].astype(BF16)
        dv_ref[...] = dva[...].astype(BF16)

    return pl.pallas_call(
        body, name="sb_bwd",
        grid=(SB_W // 128,),
        in_specs=[_pair_blk(S, OFF_SQ // 128), _pair_blk(S, OFF_SK // 128), _pair_blk(S, OFF_SV // 128),
                  _pair_blk(S), _pair_rows(S), pl.BlockSpec((T, T), lambda p: (0, 0))],
        out_specs=[_pair_blk(S), _pair_blk(S), _pair_blk(S)],
        out_shape=[jax.ShapeDtypeStruct((S, SB_W), BF16)] * 3,
        scratch_shapes=([pltpu.VMEM((S, 128), BF16)] * 3 + [pltpu.VMEM((128, S), BF16)] * 2
                        + [pltpu.VMEM((S, 128), BF16)] * 3 + [pltpu.VMEM((S, 128), F32)] * 2
                        + [pltpu.VMEM((128, T), F32)] + [pltpu.VMEM((8, T), F32)] * 4),
        compiler_params=_cp(("arbitrary",), _VMEM_BIG),
    )(proj, proj, proj, dso, ltot, tril)


def _head_norm_bwd(x, g, dy, bd):
    ss = _mm2(x * x, bd)
    r = lax.rsqrt(ss * (1.0 / HEAD_DIM) + EPS)
    xr = x * r
    gdy = g * dy
    m = _mm2(xr * gdy, bd) * (1.0 / HEAD_DIM)
    return r * (gdy - xr * m), dy * xr


def _qk_bwd(dqs, dkn, proj, pff, bfp, gq, gk, bd, dccol, triu):
    S = proj.shape[0]
    T = triu.shape[0]
    n = S // T
    rev = lambda col: (lambda i: (n - 1 - i, col))

    def body(dq_ref, dk_ref, q_ref, k_ref, ff_ref, b_ref, gq_ref, gk_ref, bd_ref, dc_ref, tri_ref,
             dfq_ref, dfk_ref, dff_ref, dgq_ref, dgk_ref, dbf_ref, carry):
        @pl.when(pl.program_id(0) == 0)
        def _():
            carry[...] = jnp.zeros_like(carry)
            dgq_ref[...] = jnp.zeros_like(dgq_ref)
            dgk_ref[...] = jnp.zeros_like(dgk_ref)
            dbf_ref[...] = jnp.zeros_like(dbf_ref)

        bdv = bd_ref[...]
        dxq, gq_rows = _head_norm_bwd(q_ref[...], gq_ref[...], dq_ref[...] * Q_SCALE, bdv)
        dfq_ref[...] = dxq.astype(BF16)
        dgq_ref[...] = dgq_ref[...] + jnp.sum(gq_rows, axis=0, keepdims=True)
        dxk, gk_rows = _head_norm_bwd(k_ref[...], gk_ref[...], dk_ref[...], bdv)
        dfk_ref[...] = dxk.astype(BF16)
        dgk_ref[...] = dgk_ref[...] + jnp.sum(gk_rows, axis=0, keepdims=True)
        dlf = _mm3(dc_ref[...], tri_ref[...], left=True) + carry[0:1, :]
        carry[0:1, :] = dlf[0:1, :]
        u = ff_ref[...] + b_ref[...]
        lane = lax.broadcasted_iota(jnp.int32, u.shape, 1)
        dff = jnp.where(lane < N_FF, dlf * _sigmoid(-u), 0.0)
        dff_ref[...] = dff.astype(BF16)
        dbf_ref[...] = dbf_ref[...] + jnp.sum(dff, axis=0, keepdims=True)

    return pl.pallas_call(
        body, name="qk_bwd",
        grid=(n,),
        in_specs=[pl.BlockSpec((T, FOX_W), rev(0)), pl.BlockSpec((T, FOX_W), rev(0)),
                  pl.BlockSpec((T, FOX_W), rev(OFF_FQ // FOX_W)), pl.BlockSpec((T, FOX_W), rev(OFF_FK // FOX_W)),
                  pl.BlockSpec((T, N_FFPAD), rev(0)),
                  pl.BlockSpec((1, N_FFPAD), lambda i: (0, 0)),
                  pl.BlockSpec((1, FOX_W), lambda i: (0, 0)), pl.BlockSpec((1, FOX_W), lambda i: (0, 0)),
                  pl.BlockSpec((FOX_W, FOX_W), lambda i: (0, 0)),
                  pl.BlockSpec((T, N_FFPAD), rev(0)),
                  pl.BlockSpec((T, T), lambda i: (0, 0))],
        out_specs=[pl.BlockSpec((T, FOX_W), rev(0)), pl.BlockSpec((T, FOX_W), rev(0)),
                   pl.BlockSpec((T, N_FFPAD), rev(0)),
                   pl.BlockSpec((1, FOX_W), lambda i: (0, 0)), pl.BlockSpec((1, FOX_W), lambda i: (0, 0)),
                   pl.BlockSpec((1, N_FFPAD), lambda i: (0, 0))],
        out_shape=[jax.ShapeDtypeStruct((S, FOX_W), BF16), jax.ShapeDtypeStruct((S, FOX_W), BF16),
                   jax.ShapeDtypeStruct((S, N_FFPAD), BF16),
                   jax.ShapeDtypeStruct((1, FOX_W), F32), jax.ShapeDtypeStruct((1, FOX_W), F32),
                   jax.ShapeDtypeStruct((1, N_FFPAD), F32)],
        scratch_shapes=[pltpu.VMEM((8, N_FFPAD), F32)],
        compiler_params=_cp(("arbitrary",), _VMEM_MID),
    )(dqs, dkn, proj, proj, pff, bfp, gq, gk, bd, dccol, triu)


def _dproj_layout(pieces):
    offs, o = [], 0
    for p in pieces:
        offs.append(o)
        o += p.shape[1]
    assert o == N_MAIN
    return offs


def _inproj_bwd_dx(pieces, dff, wm, wff, x, g, dy, ride=None):
    S, D = x.shape
    tm = min(_TM_DX, S)
    steps = S // tm
    offs = _dproj_layout(pieces)
    n = len(pieces)

    def body(*refs):
        p_refs = refs[:n]
        if ride is None:
            dff_ref, w_ref, wff_ref, x_ref, g_ref, dy_ref, dx_ref, dg_ref = refs[n:]
        else:
            dff_ref, w_ref, wff_ref, x_ref, g_ref, dy_ref, pa_ref, pb_ref = refs[n:n + 8]
            dx_ref, dg_ref, ra_ref, rb_ref = refs[n + 8:n + 12]
            xrefs = (pa_ref, pb_ref, ra_ref, rb_ref) + tuple(refs[n + 12:])

        @pl.when(pl.program_id(0) == 0)
        def _():
            dg_ref[...] = jnp.zeros_like(dg_ref)
            if ride is not None:
                _start_exchange("scatter", *xrefs)

        dh = _dot_nt(dff_ref[...], wff_ref[...])
        for p_ref, off in zip(p_refs, offs):
            dh = dh + _dot_nt(p_ref[...], w_ref[:, off:off + p_ref.shape[1]])
        xv = x_ref[...]
        r = _rms_rows(xv)
        xr = xv * r
        dg_ref[...] = dg_ref[...] + jnp.sum(dh * xr, axis=0, keepdims=True)
        gdh = g_ref[...] * dh
        m = jnp.mean(gdh * xr, axis=-1, keepdims=True)
        dx_ref[...] = dy_ref[...] + r * (gdh - xr * m)
        if ride is not None:
            @pl.when(pl.program_id(0) == steps - 1)
            def _():
                _wait_exchange("scatter", *xrefs)

    extra = () if ride is None else tuple(ride)
    return pl.pallas_call(
        body, name="inproj_bwd_dx" if ride is None else "inproj_bwd_dx_exchange",
        grid=(steps,),
        in_specs=[pl.BlockSpec((tm, p.shape[1]), lambda i: (i, 0)) for p in pieces]
        + [pl.BlockSpec((tm, N_FFPAD), lambda i: (i, 0)),
                  pl.BlockSpec((D, N_MAIN), lambda i: (0, 0)),
                  pl.BlockSpec((D, N_FFPAD), lambda i: (0, 0)),
                  pl.BlockSpec((tm, D), lambda i: (i, 0)),
                  pl.BlockSpec((1, D), lambda i: (0, 0)),
                  pl.BlockSpec((tm, D), lambda i: (i, 0))] + [_ANY] * len(extra),
        out_specs=[pl.BlockSpec((tm, D), lambda i: (i, 0)), pl.BlockSpec((1, D), lambda i: (0, 0))] + [_ANY] * len(extra),
        out_shape=[jax.ShapeDtypeStruct((S, D), F32), jax.ShapeDtypeStruct((1, D), F32)]
        + (_exchange_out_shapes("scatter", *extra) if extra else []),
        scratch_shapes=_EXCHANGE_SEMS if extra else [],
        compiler_params=_cp(("arbitrary",), _VMEM_WIDE),
    )(*pieces, dff, wm, wff, x, g, dy, *extra)


def _inproj_bwd_dw(ht, pieces, dff):
    D, S = ht.shape
    tk = min(_TK_DW, S)
    nk = S // tk
    offs = _dproj_layout(pieces)
    n = len(pieces)

    def body(*refs):
        ht_ref, p_refs, dff_ref = refs[0], refs[1:1 + n], refs[1 + n]
        dw_ref, dwff_ref, acc, accff = refs[2 + n:]
        k = pl.program_id(0)

        @pl.when(k == 0)
        def _():
            acc[...] = jnp.zeros_like(acc)
            accff[...] = jnp.zeros_like(accff)

        hb = ht_ref[...]
        for p_ref, off in zip(p_refs, offs):
            w = p_ref.shape[1]
            acc[:, off:off + w] = acc[:, off:off + w] + _dot(hb, p_ref[...])
        accff[...] = accff[...] + _dot(hb, dff_ref[...])

        @pl.when(k == nk - 1)
        def _():
            dw_ref[...] = acc[...].astype(BF16)
            dwff_ref[...] = accff[...].astype(BF16)

    return pl.pallas_call(
        body, name="inproj_bwd_dw",
        grid=(nk,),
        in_specs=[pl.BlockSpec((D, tk), lambda k: (0, k))]
        + [pl.BlockSpec((tk, p.shape[1]), lambda k: (k, 0)) for p in pieces]
        + [pl.BlockSpec((tk, N_FFPAD), lambda k: (k, 0))],
        out_specs=[pl.BlockSpec((D, N_MAIN), lambda k: (0, 0), pipeline_mode=pl.Buffered(1)),
                   pl.BlockSpec((D, N_FFPAD), lambda k: (0, 0), pipeline_mode=pl.Buffered(1))],
        out_shape=[jax.ShapeDtypeStruct((D, N_MAIN), BF16), jax.ShapeDtypeStruct((D, N_FFPAD), BF16)],
        scratch_shapes=[pltpu.VMEM((D, N_MAIN), F32), pltpu.VMEM((D, N_FFPAD), F32)],
        compiler_params=_cp(("arbitrary",), _VMEM_BIG),
    )(ht, *pieces, dff)


def _constants(T):
    tril = jnp.tril(jnp.ones((T, T), F32)).astype(BF16)
    hid = jnp.arange(FOX_W) // HEAD_DIM
    bd = (hid[:, None] == hid[None, :]).astype(BF16)
    ex = (jnp.arange(N_FFPAD)[:, None] == hid[None, :]).astype(BF16)
    return tril, tril.T, bd, ex


def _crow4(ccol, T):
    S = ccol.shape[0]
    c = ccol[:, :FOX_HEADS].T
    last = jnp.pad(c[:, T - 1::T], ((0, 0), (0, S - S // T)))
    rows = jnp.concatenate([c.reshape(FOX_HEADS // 2, 2, S), last.reshape(FOX_HEADS // 2, 2, S)], axis=1)
    return jnp.pad(rows, ((0, 0), (0, 4), (0, 0)))


def _layer_fwd(x, lw, consts, ride=None):
    tril, triu, bd, ex = consts
    proj, pff, ht = _inproj_fwd(x, lw["g"], lw["wm"], lw["wff"])
    qs, kn, ccol, cqb = _fox_prep(proj, pff, lw["bfp"], lw["gq"], lw["gk"], bd, ex, tril)
    crow4 = _crow4(ccol, tril.shape[0])
    fo, lse, *gathered = _fox_fwd(qs, kn, proj, cqb, crow4, ride)
    so, ltot = _sb_fwd(proj, triu)
    pooled = _pool_fwd(proj)
    y, mixedt = _mix_out(fo, so, pooled, proj, lw["wbd"], lw["scale"], lw["wout"], x)
    return y, (x, proj, pff, ht, qs, kn, cqb, crow4, fo, lse, so, ltot, pooled, mixedt), gathered


def _layer_bwd(dy, saved, lw, consts, ride=None, exchange_own=False):
    tril, triu, bd, _ = consts
    x, proj, pff, ht, qs, kn, cqb, crow4, fo, lse, so, ltot, pooled, mixedt = saved
    S = x.shape[0]
    dfo, dfg, dso, dsg, dpg, dpooled, dscale, dwbd = _gate_bwd(dy, lw["wout"], fo, so, pooled, proj, lw["wbd"], lw["scale"])
    dwout = _matmul_acc(mixedt, dy, "dw_out")
    dpx = _pool_bwd(dpooled)
    dqs, dkn, dfv, dck, dcq4, *received = _fox_bwd(qs, kn, proj, dfo, fo, lse, cqb, crow4, ride)
    dsq, dsk, dsv = _sb_bwd(proj, dso, ltot, tril)
    dc8 = dck[:, ::HEAD_DIM] + dcq4[:, :2, :].reshape(FOX_HEADS, S).T
    dccol = jnp.pad(dc8, ((0, 0), (0, N_FFPAD - FOX_HEADS)))
    dfq, dfk, dff, dgq, dgk, dbf = _qk_bwd(dqs, dkn, proj, pff, lw["bfp"], lw["gq"], lw["gk"], bd, dccol, triu)
    pieces = [dfq, dfk, dfv, dfg, dpx, dpg, dsq, dsk, dsv, dsg]
    dwm, dwff = _inproj_bwd_dw(ht, pieces, dff)
    dwin = jnp.concatenate([dwm[:, :OFF_PX], dwff[:, :N_FF], dwm[:, OFF_PX:]], axis=1)
    own = _grad_parts({"w_in": dwin, "w_out": dwout}) if exchange_own else None
    dx, dng, *received_own = _inproj_bwd_dx(pieces, dff, lw["wm"], lw["wff"], x, lw["g"], dy, own)
    grads = {
        "norm_g": dng[0],
        "w_in": dwin,
        "b_f": dbf[0, :N_FF],
        "q_norm_g": dgq[0].reshape(FOX_HEADS, HEAD_DIM).sum(0),
        "k_norm_g": dgk[0].reshape(FOX_HEADS, HEAD_DIM).sum(0),
        "w_pool": jnp.stack([dwbd[64 * i:64 * i + 64, 64 * i:64 * i + 64] for i in range(4)]),
        "pool_scale": dscale[0],
        "w_out": dwout,
    }
    return dx, grads, received, received_own


def _layer_weights(l, norm_g, gin, b_f, q_norm_g, k_norm_g, w_pool, pool_scale, gout):
    D = gin.shape[1]
    w = gin.transpose(1, 0, 2).reshape(D, D_IN)
    wm = jnp.concatenate([w[:, :2048], w[:, 2048 + N_FF:]], axis=1)
    wff = jnp.pad(w[:, 2048:2048 + N_FF], ((0, 0), (0, N_FFPAD - N_FF)))
    grp = jnp.arange(POOL_W) // 64
    wbd = jnp.where(grp[:, None] == grp[None, :], jnp.tile(w_pool[l].transpose(1, 0, 2).reshape(64, POOL_W), (4, 1)), 0.0)
    return {
        "g": norm_g[l].reshape(1, D),
        "wm": wm, "wff": wff,
        "bfp": jnp.pad(b_f[l], (0, N_FFPAD - N_FF)).reshape(1, N_FFPAD),
        "gq": jnp.tile(q_norm_g[l], FOX_HEADS).reshape(1, FOX_W),
        "gk": jnp.tile(k_norm_g[l], FOX_HEADS).reshape(1, FOX_W),
        "wbd": wbd.astype(BF16),
        "scale": pool_scale[l].reshape(1, POOL_W),
        "wout": gout.reshape(D_MIX, D),
    }


def _grad_parts(g):
    dwin, dwout = g["w_in"].astype(BF16), g["w_out"].astype(BF16)
    D = dwin.shape[0]
    return (dwin.reshape(D, N_DEV, D_IN // N_DEV).transpose(1, 0, 2),
            dwout.reshape(N_DEV, D_MIX // N_DEV, dwout.shape[1]))


def _train_step(x, target, norm_g, win_sh, b_f, q_norm_g, k_norm_g, w_pool, pool_scale, wout_sh):
    L = norm_g.shape[0]
    consts = _constants(min(_T, x.shape[0]))
    gathered = _gather_two_level(win_sh[0], wout_sh[0], "gather_weights")
    lws, saved = [], []
    h = x
    for l in range(L):
        lws.append(_layer_weights(l, norm_g, gathered[0], b_f, q_norm_g, k_norm_g, w_pool, pool_scale, gathered[1]))
        ride = (win_sh[l + 1], wout_sh[l + 1]) if l + 1 < L else None
        h, sv, gathered = _layer_fwd(h, lws[l], consts, ride)
        saved.append(sv)
    dy, loss = _loss_head(h, target)
    grads, received = [None] * L, [None] * L
    ride = None
    for l in reversed(range(L)):
        dy, grads[l], got, got_own = _layer_bwd(dy, saved[l], lws[l], consts, ride, exchange_own=(l == 0))
        if ride is not None:
            received[l + 1] = got
        if l == 0:
            received[0] = got_own
        else:
            ride = _grad_parts(grads[l])
    return loss, dy, grads, received


def _mesh_pos():
    return lax.axis_index("x"), lax.axis_index("y"), lax.axis_index("c")


_FLIPS = [(0, 0, 1), (1, 0, 0), (0, 1, 0), (1, 1, 0), (1, 0, 1), (0, 1, 1), (1, 1, 1)]


def _peers():
    x, y, c = _mesh_pos()
    out = []
    for fx, fy, fc in _FLIPS:
        px = 1 - x if fx else x
        py = 1 - y if fy else y
        pc = 1 - c if fc else c
        out.append(((px, py, pc), 4 * px + 2 * py + pc))
    return out, 4 * x + 2 * y + c


_EXCHANGE_SEMS = [pltpu.SemaphoreType.DMA((14,)), pltpu.SemaphoreType.DMA((14,)), pltpu.SemaphoreType.DMA((2,))]
_ANY = pl.BlockSpec(memory_space=pl.ANY)


def _exchange_copies(kind, a_ref, b_ref, oa_ref, ob_ref, send_sems, recv_sems, loc_sems):
    peers, me = _peers()
    pairs = ((a_ref, oa_ref), (b_ref, ob_ref))
    local = [pltpu.make_async_copy(src if kind == "gather" else src.at[me], dst.at[me], loc_sems.at[t])
             for t, (src, dst) in enumerate(pairs)]
    remote = []
    for k, (dev, idx) in enumerate(peers):
        for t, (src, dst) in enumerate(pairs):
            remote.append(pltpu.make_async_remote_copy(
                src_ref=src if kind == "gather" else src.at[idx], dst_ref=dst.at[me],
                send_sem=send_sems.at[2 * k + t], recv_sem=recv_sems.at[2 * k + t],
                device_id=dev, device_id_type=pl.DeviceIdType.MESH))
    return local, remote


def _start_exchange(kind, *refs):
    local, remote = _exchange_copies(kind, *refs)
    for cp in local + remote:
        cp.start()


def _wait_exchange(kind, *refs):
    local, remote = _exchange_copies(kind, *refs)
    for cp in remote:
        cp.wait_recv()
    for cp in remote:
        cp.wait_send()
    for cp in local:
        cp.wait()


def _exchange_out_shapes(kind, a, b):
    if kind == "gather":
        return [jax.ShapeDtypeStruct((N_DEV,) + a.shape, a.dtype), jax.ShapeDtypeStruct((N_DEV,) + b.shape, b.dtype)]
    return [jax.ShapeDtypeStruct(a.shape, a.dtype), jax.ShapeDtypeStruct(b.shape, b.dtype)]


def _gather_two_level(a, b, name):
    def body(a_ref, b_ref, ga_ref, gb_ref, send_sems, recv_sems, loc_sems):
        x, y, c = _mesh_pos()
        slot_of = lambda px, py, pc: 4 * px + 2 * py + pc
        me, sib = slot_of(x, y, c), slot_of(x, y, 1 - c)
        chips = [(1 - x, y), (x, 1 - y), (1 - x, 1 - y)]
        pairs = ((a_ref, ga_ref), (b_ref, gb_ref))

        def copy(k, t, slot, to, src=None):
            dst = pairs[t][1].at[slot]
            return pltpu.make_async_remote_copy(
                src_ref=dst if src is None else src, dst_ref=dst, send_sem=send_sems.at[2 * k + t],
                recv_sem=recv_sems.at[2 * k + t], device_id=to, device_id_type=pl.DeviceIdType.MESH)

        local = [pltpu.make_async_copy(src, dst.at[me], loc_sems.at[t]) for t, (src, dst) in enumerate(pairs)]
        first = []
        for t, (src, _) in enumerate(pairs):
            first.append(copy(0, t, me, (x, y, 1 - c), src))
            first += [copy(1 + j, t, me, (*chip, c), src) for j, chip in enumerate(chips)]
        for cp in local + first:
            cp.start()
        passed = []
        for j, chip in enumerate(chips):
            for t in range(2):
                landed = slot_of(*chip, c)
                copy(1 + j, t, landed, (x, y, c)).wait_recv()
                cp = copy(4 + j, t, landed, (x, y, 1 - c))
                cp.start()
                passed.append(cp)
        for t in range(2):
            copy(0, t, sib, (x, y, c)).wait_recv()
            for j, chip in enumerate(chips):
                copy(4 + j, t, slot_of(*chip, 1 - c), (x, y, c)).wait_recv()
        for cp in first + passed:
            cp.wait_send()
        for cp in local:
            cp.wait()

    return pl.pallas_call(
        body, name=name,
        in_specs=[_ANY, _ANY], out_specs=[_ANY, _ANY],
        out_shape=_exchange_out_shapes("gather", a, b),
        scratch_shapes=_EXCHANGE_SEMS,
    )(a, b)


def _adam_math(w, g, m, v):
    m_new = ADAM_B1 * m + (1.0 - ADAM_B1) * g
    v_new = ADAM_B2 * v + (1.0 - ADAM_B2) * (g * g)
    m_hat = m_new / (1.0 - ADAM_B1 ** ADAM_STEP)
    v_hat = v_new / (1.0 - ADAM_B2 ** ADAM_STEP)
    delta = -ADAM_LR * (m_hat / (jnp.sqrt(v_hat) + ADAM_EPS) + ADAM_WD * w)
    return delta, m_new, v_new


def _sum_adamw(gparts, w, m, v, name):
    L, R, C = w.shape
    tr = min(128, R)

    def body(*refs):
        gp_refs = refs[:L]
        w_ref, m_ref, v_ref, g_ref, d_ref, nm_ref, nv_ref = refs[L:]
        for l in range(L):
            g = gp_refs[l][0].astype(F32)
            for s in range(1, N_DEV):
                g = g + gp_refs[l][s].astype(F32)
            d, mn, vn = _adam_math(w_ref[l], g, m_ref[l], v_ref[l])
            g_ref[l] = g
            d_ref[l] = d
            nm_ref[l] = mn
            nv_ref[l] = vn

    blk = pl.BlockSpec((L, tr, C), lambda r: (0, r, 0))
    return pl.pallas_call(
        body, name=name,
        grid=(R // tr,),
        in_specs=[pl.BlockSpec((N_DEV, tr, C), lambda r: (0, r, 0))] * L + [blk, blk, blk],
        out_specs=[blk, blk, blk, blk],
        out_shape=[jax.ShapeDtypeStruct((L, R, C), F32)] * 4,
        compiler_params=_cp(("parallel",), _VMEM_WIDE),
    )(*gparts, w, m, v)


def _small_update(gpack, wpack, mpack, vpack):
    R = gpack.shape[0]
    VM = pl.BlockSpec(memory_space=pltpu.VMEM)

    def body(g_ref, w_ref, m_ref, v_ref, gs_ref, d_ref, nm_ref, nv_ref, buf, send_sems, recv_sems):
        peers, me = _peers()
        buf[me] = g_ref[...]
        copies = []
        for k, (dev, _) in enumerate(peers):
            cp = pltpu.make_async_remote_copy(
                src_ref=g_ref, dst_ref=buf.at[me], send_sem=send_sems.at[k], recv_sem=recv_sems.at[k],
                device_id=dev, device_id_type=pl.DeviceIdType.MESH)
            cp.start()
            copies.append(cp)
        for cp in copies:
            cp.wait_recv()
        for cp in copies:
            cp.wait_send()
        g = buf[0]
        for s in range(1, N_DEV):
            g = g + buf[s]
        d, mn, vn = _adam_math(w_ref[...], g, m_ref[...], v_ref[...])
        gs_ref[...] = g
        d_ref[...] = d
        nm_ref[...] = mn
        nv_ref[...] = vn

    return pl.pallas_call(
        body, name="small_update",
        in_specs=[VM] * 4, out_specs=[VM] * 4,
        out_shape=[jax.ShapeDtypeStruct((R, 128), F32)] * 4,
        scratch_shapes=[pltpu.VMEM((N_DEV, R, 128), F32), pltpu.SemaphoreType.DMA((7,)), pltpu.SemaphoreType.DMA((7,))],
        compiler_params=_cp(None, _VMEM_MID),
    )(gpack, wpack, mpack, vpack)


_SMALL = ("norm_g", "b_f", "q_norm_g", "k_norm_g", "w_pool", "pool_scale")


def _pack(parts):
    flat = jnp.concatenate([p.reshape(-1) for p in parts])
    n = flat.shape[0]
    rows = -(-n // (8 * 128)) * 8
    return jnp.pad(flat, (0, rows * 128 - n)).reshape(rows, 128)


def _unpack(packed, like):
    flat = packed.reshape(-1)
    out, o = [], 0
    for p in like:
        out.append(flat[o:o + p.size].reshape(p.shape))
        o += p.size
    return out


def kernel(x, norm_g, w_in, b_f, q_norm_g, k_norm_g, w_pool, pool_scale, w_out, loss_target, m_norm_g, m_w_in, m_b_f, m_q_norm_g, m_k_norm_g, m_w_pool, m_pool_scale, m_w_out, v_norm_g, v_w_in, v_b_f, v_q_norm_g, v_k_norm_g, v_w_pool, v_pool_scale, v_w_out):
    L = w_in.shape[0]

    loss_local, dx, grads, received = _train_step(x[0], loss_target[0], norm_g, w_in.astype(BF16), b_f, q_norm_g,
                                                  k_norm_g, w_pool, pool_scale, w_out.astype(BF16))
    loss = lax.psum(loss_local, MESH_AXES)
    g = {k: jnp.stack([grads[l][k] for l in range(L)]) for k in _SMALL}

    g_win, d_win, nm_win, nv_win = _sum_adamw([r[0] for r in received], w_in, m_w_in, v_w_in, "adamw_w_in")
    g_wout, d_wout, nm_wout, nv_wout = _sum_adamw([r[1] for r in received], w_out, m_w_out, v_w_out, "adamw_w_out")

    ws = dict(norm_g=norm_g, b_f=b_f, q_norm_g=q_norm_g, k_norm_g=k_norm_g, w_pool=w_pool, pool_scale=pool_scale)
    ms = dict(norm_g=m_norm_g, b_f=m_b_f, q_norm_g=m_q_norm_g, k_norm_g=m_k_norm_g, w_pool=m_w_pool, pool_scale=m_pool_scale)
    vs = dict(norm_g=v_norm_g, b_f=v_b_f, q_norm_g=v_q_norm_g, k_norm_g=v_k_norm_g, w_pool=v_w_pool, pool_scale=v_pool_scale)
    like = [ws[k] for k in _SMALL]
    gs_p, d_p, nm_p, nv_p = _small_update(_pack([g[k] for k in _SMALL]), _pack(like),
                                          _pack([ms[k] for k in _SMALL]), _pack([vs[k] for k in _SMALL]))
    gs = dict(zip(_SMALL, _unpack(gs_p, like)))
    ds = dict(zip(_SMALL, _unpack(d_p, like)))
    nms = dict(zip(_SMALL, _unpack(nm_p, like)))
    nvs = dict(zip(_SMALL, _unpack(nv_p, like)))
    gs["w_in"], ds["w_in"], nms["w_in"], nvs["w_in"] = g_win, d_win, nm_win, nv_win
    gs["w_out"], ds["w_out"], nms["w_out"], nvs["w_out"] = g_wout, d_wout, nm_wout, nv_wout

    order = ("norm_g", "w_in", "b_f", "q_norm_g", "k_norm_g", "w_pool", "pool_scale", "w_out")
    return (loss, dx[None], *[gs[k] for k in order], *[ds[k] for k in order],
            *[nms[k] for k in order], *[nvs[k] for k in order])
```

```python
import jax
import jax.numpy as jnp
from jax import lax
from jax.experimental import pallas as pl
from jax.experimental.pallas import tpu as pltpu

F32 = jnp.float32
BF16 = jnp.bfloat16

EPS = 1e-6
NEG = -1e30
HEAD_DIM = 64
FOX_HEADS = 8
FOX_W = 512
POOL_W = 256
SB_W = 256
D_MIX = 1024
N_FF = 8
N_MAIN = 3584
N_FFPAD = 128
OFF_FQ, OFF_FK, OFF_FV, OFF_FG = 0, 512, 1024, 1536
OFF_PX, OFF_PG = 2048, 2304
OFF_SQ, OFF_SK, OFF_SV, OFF_SG = 2560, 2816, 3072, 3328
D_IN = 3592
Q_SCALE = HEAD_DIM ** -0.5

ADAM_LR = 0.001
ADAM_B1 = 0.9
ADAM_B2 = 0.999
ADAM_EPS = 1e-08
ADAM_WD = 0.01
ADAM_STEP = 10

N_DEV = 8
MESH_AXES = ("x", "y", "c")

_T = 256
_TM = 512
_TM_ROWS = 512
_TM_FWD, _TN_FWD = 2048, 512
_TM_DX = 512
_TK_DW = 1024
_VMEM_V7X = 64 << 20
_VMEM_BIG = _VMEM_V7X - (8 << 20)
_VMEM_MID = 40 << 20
_VMEM_WIDE = 48 << 20


def _cp(sem=None, vmem=None):
    kw = {}
    if sem is not None:
        kw["dimension_semantics"] = sem
    if vmem is not None:
        kw["vmem_limit_bytes"] = vmem
    return pltpu.CompilerParams(**kw)


def _dot(a, b):
    return jnp.dot(a, b, preferred_element_type=F32)


def _dot_nt(a, b):
    return lax.dot_general(a, b, (((1,), (1,)), ((), ())), preferred_element_type=F32)


def _dot_tn(a, b):
    return lax.dot_general(a, b, (((0,), (0,)), ((), ())), preferred_element_type=F32)


def _mm2(v, m, left=False):
    hi = v.astype(BF16)
    lo = (v - hi.astype(F32)).astype(BF16)
    if left:
        return _dot(m, hi) + _dot(m, lo)
    return _dot(hi, m) + _dot(lo, m)


def _mm3(v, m, left=False):
    a1 = v.astype(BF16)
    r1 = v - a1.astype(F32)
    a2 = r1.astype(BF16)
    a3 = (r1 - a2.astype(F32)).astype(BF16)
    if left:
        return _dot(m, a1) + _dot(m, a2) + _dot(m, a3)
    return _dot(a1, m) + _dot(a2, m) + _dot(a3, m)


def _sigmoid(z):
    return 1.0 / (1.0 + jnp.exp(-z))


def _rms_rows(x):
    return lax.rsqrt(jnp.mean(x * x, axis=-1, keepdims=True) + EPS)


def _inproj_fwd(x, g, wm, wff):
    S, D = x.shape
    tm = min(_TM_FWD, S)
    tn = _TN_FWD

    def body(x_ref, g_ref, w_ref, wff_ref, o_ref, off_ref, ht_ref, h_ref):
        @pl.when(pl.program_id(1) == 0)
        def _():
            xv = x_ref[...]
            h = (xv * _rms_rows(xv)) * g_ref[...]
            h_ref[...] = h.astype(BF16)
            ht_ref[...] = h.T.astype(BF16)
            off_ref[...] = _dot(h_ref[...], wff_ref[...])

        o_ref[...] = _dot(h_ref[...], w_ref[...])

    return pl.pallas_call(
        body, name="inproj_fwd",
        grid=(S // tm, N_MAIN // tn),
        in_specs=[pl.BlockSpec((tm, D), lambda i, j: (i, 0)),
                  pl.BlockSpec((1, D), lambda i, j: (0, 0)),
                  pl.BlockSpec((D, tn), lambda i, j: (0, j)),
                  pl.BlockSpec((D, N_FFPAD), lambda i, j: (0, 0))],
        out_specs=[pl.BlockSpec((tm, tn), lambda i, j: (i, j)),
                   pl.BlockSpec((tm, N_FFPAD), lambda i, j: (i, 0)),
                   pl.BlockSpec((D, tm), lambda i, j: (0, i))],
        out_shape=[jax.ShapeDtypeStruct((S, N_MAIN), F32), jax.ShapeDtypeStruct((S, N_FFPAD), F32),
                   jax.ShapeDtypeStruct((D, S), BF16)],
        scratch_shapes=[pltpu.VMEM((tm, D), BF16)],
        compiler_params=_cp(("parallel", "arbitrary"), _VMEM_WIDE),
    )(x, g, wm, wff)


def _head_norm(x, g, bd):
    ss = _mm2(x * x, bd)
    r = lax.rsqrt(ss * (1.0 / HEAD_DIM) + EPS)
    return (x * r) * g


def _fox_prep(proj, pff, bfp, gq, gk, bd, ex, tril):
    S = proj.shape[0]
    T = tril.shape[0]

    def body(q_ref, k_ref, ff_ref, b_ref, gq_ref, gk_ref, bd_ref, ex_ref, tri_ref,
             qs_ref, kn_ref, cc_ref, cqb_ref, carry):
        @pl.when(pl.program_id(0) == 0)
        def _():
            carry[...] = jnp.zeros_like(carry)

        bdv = bd_ref[...]
        qs_ref[...] = (_head_norm(q_ref[...], gq_ref[...], bdv) * Q_SCALE).astype(BF16)
        kn_ref[...] = _head_norm(k_ref[...], gk_ref[...], bdv).astype(BF16)
        u = ff_ref[...] + b_ref[...]
        lf = jnp.minimum(u, 0.0) - jnp.log1p(jnp.exp(-jnp.abs(u)))
        c = _mm3(lf, tri_ref[...], left=True) + carry[0:1, :]
        carry[0:1, :] = c[T - 1:T, :]
        cc_ref[...] = c
        cqb_ref[...] = _mm3(c, ex_ref[...])

    return pl.pallas_call(
        body, name="fox_prep",
        grid=(S // T,),
        in_specs=[pl.BlockSpec((T, FOX_W), lambda i: (i, OFF_FQ // FOX_W)),
                  pl.BlockSpec((T, FOX_W), lambda i: (i, OFF_FK // FOX_W)),
                  pl.BlockSpec((T, N_FFPAD), lambda i: (i, 0)),
                  pl.BlockSpec((1, N_FFPAD), lambda i: (0, 0)),
                  pl.BlockSpec((1, FOX_W), lambda i: (0, 0)),
                  pl.BlockSpec((1, FOX_W), lambda i: (0, 0)),
                  pl.BlockSpec((FOX_W, FOX_W), lambda i: (0, 0)),
                  pl.BlockSpec((N_FFPAD, FOX_W), lambda i: (0, 0)),
                  pl.BlockSpec((T, T), lambda i: (0, 0))],
        out_specs=[pl.BlockSpec((T, FOX_W), lambda i: (i, 0)),
                   pl.BlockSpec((T, FOX_W), lambda i: (i, 0)),
                   pl.BlockSpec((T, N_FFPAD), lambda i: (i, 0)),
                   pl.BlockSpec((T, FOX_W), lambda i: (i, 0))],
        out_shape=[jax.ShapeDtypeStruct((S, FOX_W), BF16), jax.ShapeDtypeStruct((S, FOX_W), BF16),
                   jax.ShapeDtypeStruct((S, N_FFPAD), F32), jax.ShapeDtypeStruct((S, FOX_W), F32)],
        scratch_shapes=[pltpu.VMEM((8, N_FFPAD), F32)],
        compiler_params=_cp(("arbitrary",), _VMEM_MID),
    )(proj, proj, pff, bfp, gq, gk, bd, ex, tril)


def _pair_blk(S, off=0):
    return pl.BlockSpec((S, 128), lambda p: (0, off + p), pipeline_mode=pl.Buffered(1))


def _pair_rows(S):
    return pl.BlockSpec((None, 8, S), lambda p: (p, 0, 0), pipeline_mode=pl.Buffered(1))


def _head_masks(S):
    return lax.broadcasted_iota(jnp.int32, (S, 128), 1) < HEAD_DIM


_EXP_ZERO = 104.0


def _spread_heads(x):
    src = lax.broadcasted_iota(jnp.int32, (128, 128), 0)
    return (_mm3(x, (src == 0).astype(BF16)), _mm3(x, (src == HEAD_DIM).astype(BF16)))


def _score_bounds(q, k):
    same_head = ((lax.broadcasted_iota(jnp.int32, (128, 128), 0) < HEAD_DIM)
                 == (lax.broadcasted_iota(jnp.int32, (128, 128), 1) < HEAD_DIM)).astype(BF16)

    def max_norm2(x):
        xf = x.astype(F32)
        return jnp.max(_mm2(xf * xf, same_head), axis=0, keepdims=True)

    z = jnp.sqrt(max_norm2(q) * max_norm2(k))
    z = jnp.where(z == z, z, jnp.inf)
    return jnp.max(z[:, 0:1]) * 1.001 + 1e-3, jnp.max(z[:, 64:65]) * 1.001 + 1e-3


def _for_tiles_back(i, n, tiles_fn, fours=False):
    if fours:
        def four(t, c):
            tiles_fn([i - 1 - 4 * t, i - 2 - 4 * t, i - 3 - 4 * t, i - 4 - 4 * t])
            return c

        lax.fori_loop(0, lax.shift_right_logical(n, 2), four, 0)
        rest = i - (n & ~3)

        @pl.when((n & 2) != 0)
        def _():
            tiles_fn([rest - 1, rest - 2])
    else:
        def two(t, c):
            tiles_fn([i - 1 - 2 * t, i - 2 - 2 * t])
            return c

        lax.fori_loop(0, lax.shift_right_logical(n, 1), two, 0)

    @pl.when((n & 1) != 0)
    def _():
        tiles_fn([i - n])


def _fox_tiles_back(cr_ref, i, r0, zba, zbb):
    last = cr_ref[:, pl.ds(0, 128)]
    first = cr_ref[:, pl.ds(r0, 128)]
    alive_a = 2.0 * zba + first[0:1, 0:1] - last[2:3, :] > -_EXP_ZERO
    alive_b = 2.0 * zbb + first[1:2, 0:1] - last[3:4, :] > -_EXP_ZERO
    before = lax.broadcasted_iota(jnp.int32, (1, 128), 1) < i
    return jnp.sum((before & (alive_a | alive_b)).astype(jnp.int32))


def _fox_fwd(qs, kn, proj, cqb, crow4, ride=None):
    S = qs.shape[0]
    T = min(_T, S)
    nq = S // T
    n_pairs = FOX_W // 128

    def body(*refs):
        if ride is None:
            q_ref, k_ref, v_ref, cq_ref, cr_ref, o_ref, lse_ref = refs[:7]
            qa, qb, vta, vtb, cka, ckb, ma, mb, acca, accb = refs[7:]
        else:
            q_ref, k_ref, v_ref, cq_ref, cr_ref, wa_ref, wb_ref, o_ref, lse_ref, ga_ref, gb_ref = refs[:11]
            qa, qb, vta, vtb, cka, ckb, ma, mb, acca, accb = refs[11:21]
            xrefs = (wa_ref, wb_ref, ga_ref, gb_ref) + tuple(refs[21:])

            @pl.when(pl.program_id(0) == 0)
            def _():
                _start_exchange("gather", *xrefs)

        lane_s = _head_masks(S)
        q = q_ref[...]
        zq = jnp.zeros_like(q)
        qa[...] = jnp.where(lane_s, q, zq)
        qb[...] = jnp.where(lane_s, zq, q)
        cq = cq_ref[...]
        cka[...], ckb[...] = _spread_heads(cq)
        lse_ref[...] = jnp.zeros((8, S), F32)
        row_t = lax.broadcasted_iota(jnp.int32, (128, T), 0) < HEAD_DIM
        zba, zbb = _score_bounds(q, k_ref[...])

        def prep(c, carry):
            c0 = pl.multiple_of(c * T, T)
            vt = v_ref[pl.ds(c0, T), :].T
            vta[:, pl.ds(c0, T)] = jnp.where(row_t, vt, 1.0).astype(BF16)
            vtb[:, pl.ds(c0, T)] = jnp.where(row_t, 1.0, vt).astype(BF16)
            return carry

        lax.fori_loop(0, nq, prep, 0)
        causal = (lax.broadcasted_iota(jnp.int32, (T, T), 0) <= lax.broadcasted_iota(jnp.int32, (T, T), 1))

        heads = ((qa, vta, cka, ma, acca), (qb, vtb, ckb, mb, accb))

        def kv(js, r0, masked):
            cr = cr_ref[:, pl.ds(r0, T)]
            c0s = [pl.multiple_of(j * T, T) for j in js]
            ks = [k_ref[pl.ds(c0, T), :] for c0 in c0s]
            ss = []
            for h, (qr, _, ckr, _, _) in enumerate(heads):
                qh = qr[pl.ds(r0, T), :]
                row = []
                for k, c0 in zip(ks, c0s):
                    s = _dot_nt(k, qh) - jnp.tile(ckr[pl.ds(c0, T), :], (1, T // 128))
                    row.append(jnp.where(causal, s, NEG) if masked else s)
                ss.append(row)
            ms = []
            for h, (row, (_, _, _, mr, _)) in enumerate(zip(ss, heads)):
                top = row[0]
                for s in row[1:]:
                    top = jnp.maximum(top, s)
                m_old = mr[0:1, :]
                ms.append((m_old, jnp.maximum(m_old, jnp.max(top, axis=0, keepdims=True) + cr[h:h + 1, :])))
            ps = [[jnp.exp(s + (cr[h:h + 1, :] - m_new)).astype(BF16) for s in row]
                  for h, (row, (_, m_new)) in enumerate(zip(ss, ms))]
            pvs = []
            for row, (_, vr, _, _, _) in zip(ps, heads):
                pv = _dot(vr[:, pl.ds(c0s[0], T)], row[0])
                for p, c0 in zip(row[1:], c0s[1:]):
                    pv = pv + _dot(vr[:, pl.ds(c0, T)], p)
                pvs.append(pv)
            for pv, (m_old, m_new), (_, _, _, mr, ar) in zip(pvs, ms, heads):
                ar[...] = jnp.exp(m_old - m_new) * ar[...] + pv
                mr[0:1, :] = m_new

        def qblk(i, carry):
            r0 = pl.multiple_of(i * T, T)
            ma[...] = jnp.full((8, T), NEG, F32)
            mb[...] = jnp.full((8, T), NEG, F32)
            acca[...] = jnp.zeros((128, T), F32)
            accb[...] = jnp.zeros((128, T), F32)
            kv([i], r0, True)
            done = _fox_tiles_back(cr_ref, i, r0, zba, zbb)
            _for_tiles_back(i, done, lambda js: kv(js, r0, False), fours=True)
            aa = acca[...]
            ab = accb[...]
            la = aa[64:65, :]
            lb = ab[0:1, :]
            o_ref[pl.ds(r0, T), :] = jnp.where(row_t, aa / la, ab / lb).T
            lse_ref[0:1, pl.ds(r0, T)] = ma[0:1, :] + jnp.log(la)
            lse_ref[1:2, pl.ds(r0, T)] = mb[0:1, :] + jnp.log(lb)
            lse_ref[2:3, pl.ds(r0, T)] = jnp.broadcast_to(done.astype(F32), (1, T))
            return carry

        lax.fori_loop(0, nq, qblk, 0)
        if ride is not None:
            @pl.when(pl.program_id(0) == n_pairs - 1)
            def _():
                _wait_exchange("gather", *xrefs)

    extra = () if ride is None else tuple(ride)
    return pl.pallas_call(
        body, name="fox_fwd" if ride is None else "fox_fwd_gather",
        grid=(n_pairs,),
        in_specs=[_pair_blk(S), _pair_blk(S), _pair_blk(S, OFF_FV // 128), _pair_blk(S), _pair_rows(S)]
        + [_ANY] * len(extra),
        out_specs=[_pair_blk(S), _pair_rows(S)] + [_ANY] * len(extra),
        out_shape=[jax.ShapeDtypeStruct((S, FOX_W), F32), jax.ShapeDtypeStruct((n_pairs, 8, S), F32)]
        + (_exchange_out_shapes("gather", *extra) if extra else []),
        scratch_shapes=[pltpu.VMEM((S, 128), BF16)] * 2 + [pltpu.VMEM((128, S), BF16)] * 2
        + [pltpu.VMEM((S, 128), F32)] * 2 + [pltpu.VMEM((8, T), F32)] * 2 + [pltpu.VMEM((128, T), F32)] * 2
        + (_EXCHANGE_SEMS if extra else []),
        compiler_params=_cp(("arbitrary",), _VMEM_BIG),
    )(qs, kn, proj, cqb, crow4, *extra)


def _softplus_parts(z):
    e = jnp.exp(-jnp.abs(z))
    return e, jnp.maximum(z, 0.0) + jnp.log(1.0 + e)


def _sb_fwd(proj, triu):
    S = proj.shape[0]
    T = triu.shape[0]
    nq = S // T

    n_pairs = SB_W // 128
    H = 2 * n_pairs

    def body(q_ref, k_ref, v_ref, tri_ref, o_ref, lt_ref, qm, kb, vt, rr, acc):
        lane_s = _head_masks(S)
        zbs = []
        for p in range(n_pairs):
            q = (q_ref[:, 128 * p:128 * (p + 1)] * Q_SCALE).astype(BF16)
            zq = jnp.zeros_like(q)
            qm[2 * p] = jnp.where(lane_s, q, zq)
            qm[2 * p + 1] = jnp.where(lane_s, zq, q)
            kb[p] = k_ref[:, 128 * p:128 * (p + 1)].astype(BF16)
            zbs += list(_score_bounds(q, kb[p]))
        lt_ref[...] = jnp.zeros((n_pairs, 8, S), F32)
        row_t = lax.broadcasted_iota(jnp.int32, (128, T), 0) < HEAD_DIM

        def prep(c, carry):
            c0 = pl.multiple_of(c * T, T)
            for p in range(n_pairs):
                vt[p, :, pl.ds(c0, T)] = v_ref[pl.ds(c0, T), 128 * p:128 * (p + 1)].T.astype(BF16)
            return carry

        lax.fori_loop(0, nq, prep, 0)
        strict = (lax.broadcasted_iota(jnp.int32, (T, T), 0) < lax.broadcasted_iota(jnp.int32, (T, T), 1))

        def kv(tiles, r0):
            tri = tri_ref[...]
            c0s = [pl.multiple_of(j * T, T) for j, _ in tiles]
            zs = [[_dot_nt(kb[h // 2, pl.ds(c0, T), :], qm[h, pl.ds(r0, T), :]) for c0 in c0s]
                  for h in range(H)]
            lbs = [[jnp.where(strict, -_softplus_parts(z)[1], 0.0) if masked else -_softplus_parts(z)[1]
                    for z, (_, masked) in zip(row, tiles)] for row in zs]
            incs = [[_mm2(lb, tri, left=True) for lb in row] for row in lbs]
            avs = []
            for h in range(H):
                r = rr[h, 0:1, :]
                av = None
                for z, inc, c0, (_, masked) in zip(zs[h], incs[h], c0s, tiles):
                    a = jnp.exp(z + inc + r)
                    if masked:
                        a = jnp.where(strict, a, 0.0)
                    term = _dot(vt[h // 2, :, pl.ds(c0, T)], a.astype(BF16))
                    av = term if av is None else av + term
                    r = r + inc[0:1, :]
                avs.append((av, r))
            for h, (av, r) in enumerate(avs):
                rr[h, 0:1, :] = r
                acc[h] = acc[h] + av

        def qblk(i, carry):
            r0 = pl.multiple_of(i * T, T)
            rr[...] = jnp.zeros((H, 8, T), F32)
            acc[...] = jnp.zeros((H, 128, T), F32)

            @pl.when(i == 0)
            def _():
                kv([(i, True)], r0)

            @pl.when(i > 0)
            def _():
                kv([(i, True), (i - 1, False)], r0)

            def alive():
                m = jnp.max(rr[0, 0:1, :]) + zbs[0]
                for h in range(1, H):
                    m = jnp.maximum(m, jnp.max(rr[h, 0:1, :]) + zbs[h])
                return m > -_EXP_ZERO

            def cond(st):
                return (st[0] < i) & st[1]

            def step(st):
                kv([(i - 1 - st[0], False)], r0)
                return st[0] + 1, alive()

            done, _ = lax.while_loop(cond, step, (jnp.minimum(i, 1), alive()))
            for p in range(n_pairs):
                o_ref[pl.ds(r0, T), 128 * p:128 * (p + 1)] = jnp.where(row_t, acc[2 * p], acc[2 * p + 1]).T
                lt_ref[p, 0:1, pl.ds(r0, T)] = rr[2 * p, 0:1, :]
                lt_ref[p, 1:2, pl.ds(r0, T)] = rr[2 * p + 1, 0:1, :]
                lt_ref[p, 2:3, pl.ds(r0, T)] = jnp.broadcast_to(done.astype(F32), (1, T))
            return carry

        lax.fori_loop(0, nq, qblk, 0)

    wide = lambda off: pl.BlockSpec((S, SB_W), lambda g: (0, off), pipeline_mode=pl.Buffered(1))
    return pl.pallas_call(
        body, name="sb_fwd",
        grid=(1,),
        in_specs=[wide(OFF_SQ // SB_W), wide(OFF_SK // SB_W), wide(OFF_SV // SB_W),
                  pl.BlockSpec((T, T), lambda g: (0, 0))],
        out_specs=[wide(0), pl.BlockSpec((n_pairs, 8, S), lambda g: (0, 0, 0), pipeline_mode=pl.Buffered(1))],
        out_shape=[jax.ShapeDtypeStruct((S, SB_W), F32), jax.ShapeDtypeStruct((n_pairs, 8, S), F32)],
        scratch_shapes=[pltpu.VMEM((H, S, 128), BF16), pltpu.VMEM((n_pairs, S, 128), BF16),
                        pltpu.VMEM((n_pairs, 128, S), BF16), pltpu.VMEM((H, 8, T), F32), pltpu.VMEM((H, 128, T), F32)],
        compiler_params=_cp(("arbitrary",), _VMEM_BIG),
    )(proj, proj, proj, triu)


def _pool_window_lanes(shape):
    lane = lax.broadcasted_iota(jnp.int32, shape, 1)
    return jnp.where(lane < 64, 2, jnp.where(lane < 128, 4, jnp.where(lane < 192, 8, 16)))


def _pool_fwd(proj):
    S = proj.shape[0]

    def body(x_ref, o_ref):
        x = x_ref[...]
        t = lax.broadcasted_iota(jnp.int32, x.shape, 0)
        lane = lax.broadcasted_iota(jnp.int32, x.shape, 1)

        def back(a, k):
            return jnp.where(t >= k, pltpu.roll(a, k, 0), 0.0)

        s1 = x + back(x, 1)
        s2 = s1 + back(s1, 2)
        s4 = s2 + back(s2, 4)
        s8 = s4 + back(s4, 8)
        win = jnp.where(lane < 64, s1, jnp.where(lane < 128, s2, jnp.where(lane < 192, s4, s8)))
        cnt = jnp.minimum(t + 1, _pool_window_lanes(x.shape)).astype(F32)
        o_ref[...] = win / cnt - x

    return pl.pallas_call(
        body, name="pool_fwd",
        grid=(1,),
        in_specs=[pl.BlockSpec((S, POOL_W), lambda i: (0, OFF_PX // POOL_W))],
        out_specs=pl.BlockSpec((S, POOL_W), lambda i: (0, 0)),
        out_shape=jax.ShapeDtypeStruct((S, POOL_W), F32),
        compiler_params=_cp(("arbitrary",), _VMEM_BIG),
    )(proj)


def _silu(g):
    return g * _sigmoid(g)


def _mix_out(fo, so, pooled, proj, wbd, scale, wout, x):
    S, D = x.shape
    tm = min(_TM_ROWS, S)

    def body(fo_ref, fg_ref, so_ref, sg_ref, pl_ref, pg_ref, wbd_ref, sc_ref, w_ref, x_ref, y_ref, mxt_ref, mx_ref):
        parts = ((0, fo_ref[...] * _silu(fg_ref[...])),
                 (FOX_W, (_dot(pl_ref[...].astype(BF16), wbd_ref[...]) * sc_ref[...]) * _silu(pg_ref[...])),
                 (FOX_W + POOL_W, so_ref[...] * _silu(sg_ref[...])))
        for off, part in parts:
            w = part.shape[1]
            mx_ref[:, off:off + w] = part.astype(BF16)
            mxt_ref[off:off + w, :] = part.T.astype(BF16)
        y_ref[...] = x_ref[...] + _dot(mx_ref[...], w_ref[...])

    return pl.pallas_call(
        body, name="mix_out",
        grid=(S // tm,),
        in_specs=[pl.BlockSpec((tm, FOX_W), lambda i: (i, 0)),
                  pl.BlockSpec((tm, FOX_W), lambda i: (i, OFF_FG // FOX_W)),
                  pl.BlockSpec((tm, SB_W), lambda i: (i, 0)),
                  pl.BlockSpec((tm, SB_W), lambda i: (i, OFF_SG // SB_W)),
                  pl.BlockSpec((tm, POOL_W), lambda i: (i, 0)),
                  pl.BlockSpec((tm, POOL_W), lambda i: (i, OFF_PG // POOL_W)),
                  pl.BlockSpec((POOL_W, POOL_W), lambda i: (0, 0)),
                  pl.BlockSpec((1, POOL_W), lambda i: (0, 0)),
                  pl.BlockSpec((D_MIX, D), lambda i: (0, 0)),
                  pl.BlockSpec((tm, D), lambda i: (i, 0))],
        out_specs=[pl.BlockSpec((tm, D), lambda i: (i, 0)), pl.BlockSpec((D_MIX, tm), lambda i: (0, i))],
        out_shape=[jax.ShapeDtypeStruct((S, D), F32), jax.ShapeDtypeStruct((D_MIX, S), BF16)],
        scratch_shapes=[pltpu.VMEM((tm, D_MIX), BF16)],
        compiler_params=_cp(("parallel",), _VMEM_MID),
    )(fo, proj, so, proj, pooled, proj, wbd, scale, wout, x)


def _loss_head(y, target):
    S, D = y.shape
    tm = min(_TM, S)

    def body(y_ref, t_ref, dy_ref, ls_ref):
        @pl.when(pl.program_id(0) == 0)
        def _():
            ls_ref[...] = jnp.zeros_like(ls_ref)

        e = y_ref[...] - t_ref[...]
        dy_ref[...] = e * (1.0 / D)
        ls_ref[...] = ls_ref[...] + jnp.sum(e * e) * (0.5 / D)

    dy, ls = pl.pallas_call(
        body, name="loss_head",
        grid=(S // tm,),
        in_specs=[pl.BlockSpec((tm, D), lambda i: (i, 0)), pl.BlockSpec((tm, D), lambda i: (i, 0))],
        out_specs=[pl.BlockSpec((tm, D), lambda i: (i, 0)), pl.BlockSpec((8, 128), lambda i: (0, 0))],
        out_shape=[jax.ShapeDtypeStruct((S, D), F32), jax.ShapeDtypeStruct((8, 128), F32)],
        compiler_params=_cp(("arbitrary",), _VMEM_MID),
    )(y, target)
    return dy, ls[0, 0]


def _dsilu(g):
    s = _sigmoid(g)
    return s * (1.0 + g * (1.0 - s))


def _gate_bwd(dy, wout, fo, so, pooled, proj, wbd, scale):
    S, D = dy.shape
    tm = min(_TM_ROWS, S)

    def body(dy_ref, w_ref, fo_ref, fg_ref, so_ref, sg_ref, pl_ref, pg_ref, wbd_ref, sc_ref,
             dfo_ref, dfg_ref, dso_ref, dsg_ref, dpg_ref, dpl_ref, dsc_ref, dwbd_ref):
        @pl.when(pl.program_id(0) == 0)
        def _():
            dsc_ref[...] = jnp.zeros_like(dsc_ref)
            dwbd_ref[...] = jnp.zeros_like(dwbd_ref)

        dm = _dot_nt(dy_ref[...].astype(BF16), w_ref[...])
        dmf = dm[:, 0:FOX_W]
        dmp = dm[:, FOX_W:FOX_W + POOL_W]
        dms = dm[:, FOX_W + POOL_W:D_MIX]
        fg = fg_ref[...]
        dfo_ref[...] = dmf * _silu(fg)
        dfg_ref[...] = (dmf * fo_ref[...] * _dsilu(fg)).astype(BF16)
        sg = sg_ref[...]
        dso_ref[...] = dms * _silu(sg)
        dsg_ref[...] = (dms * so_ref[...] * _dsilu(sg)).astype(BF16)
        pg = pg_ref[...]
        plb = pl_ref[...].astype(BF16)
        yw = _dot(plb, wbd_ref[...])
        sc = sc_ref[...]
        dpg_ref[...] = (dmp * (yw * sc) * _dsilu(pg)).astype(BF16)
        dys = dmp * _silu(pg)
        dsc_ref[...] = dsc_ref[...] + jnp.sum(dys * yw, axis=0, keepdims=True)
        dyw = (dys * sc).astype(BF16)
        dpl_ref[...] = _dot_nt(dyw, wbd_ref[...])
        dwbd_ref[...] = dwbd_ref[...] + _dot_tn(plb, dyw)

    return pl.pallas_call(
        body, name="gate_bwd",
        grid=(S // tm,),
        in_specs=[pl.BlockSpec((tm, D), lambda i: (i, 0)),
                  pl.BlockSpec((D_MIX, D), lambda i: (0, 0)),
                  pl.BlockSpec((tm, FOX_W), lambda i: (i, 0)),
                  pl.BlockSpec((tm, FOX_W), lambda i: (i, OFF_FG // FOX_W)),
                  pl.BlockSpec((tm, SB_W), lambda i: (i, 0)),
                  pl.BlockSpec((tm, SB_W), lambda i: (i, OFF_SG // SB_W)),
                  pl.BlockSpec((tm, POOL_W), lambda i: (i, 0)),
                  pl.BlockSpec((tm, POOL_W), lambda i: (i, OFF_PG // POOL_W)),
                  pl.BlockSpec((POOL_W, POOL_W), lambda i: (0, 0)),
                  pl.BlockSpec((1, POOL_W), lambda i: (0, 0))],
        out_specs=[pl.BlockSpec((tm, FOX_W), lambda i: (i, 0)),
                   pl.BlockSpec((tm, FOX_W), lambda i: (i, 0)),
                   pl.BlockSpec((tm, SB_W), lambda i: (i, 0)),
                   pl.BlockSpec((tm, SB_W), lambda i: (i, 0)),
                   pl.BlockSpec((tm, POOL_W), lambda i: (i, 0)),
                   pl.BlockSpec((tm, POOL_W), lambda i: (i, 0)),
                   pl.BlockSpec((1, POOL_W), lambda i: (0, 0)),
                   pl.BlockSpec((POOL_W, POOL_W), lambda i: (0, 0))],
        out_shape=[jax.ShapeDtypeStruct((S, FOX_W), F32), jax.ShapeDtypeStruct((S, FOX_W), BF16),
                   jax.ShapeDtypeStruct((S, SB_W), F32), jax.ShapeDtypeStruct((S, SB_W), BF16),
                   jax.ShapeDtypeStruct((S, POOL_W), BF16), jax.ShapeDtypeStruct((S, POOL_W), F32),
                   jax.ShapeDtypeStruct((1, POOL_W), F32), jax.ShapeDtypeStruct((POOL_W, POOL_W), F32)],
        compiler_params=_cp(("arbitrary",), _VMEM_MID),
    )(dy, wout, fo, proj, so, proj, pooled, proj, wbd, scale)


def _matmul_acc(at, b, name):
    M, S = at.shape
    N = b.shape[1]
    tk = min(_TK_DW, S)
    tn = min(512, N)
    nk = S // tk

    def body(a_ref, b_ref, o_ref, acc):
        k = pl.program_id(1)

        @pl.when(k == 0)
        def _():
            acc[...] = jnp.zeros_like(acc)

        acc[...] = acc[...] + _dot(a_ref[...], b_ref[...].astype(BF16))

        @pl.when(k == nk - 1)
        def _():
            o_ref[...] = acc[...].astype(BF16)

    return pl.pallas_call(
        body, name=name,
        grid=(N // tn, nk),
        in_specs=[pl.BlockSpec((M, tk), lambda j, k: (0, k)), pl.BlockSpec((tk, tn), lambda j, k: (k, j))],
        out_specs=pl.BlockSpec((M, tn), lambda j, k: (0, j)),
        out_shape=jax.ShapeDtypeStruct((M, N), BF16),
        scratch_shapes=[pltpu.VMEM((M, tn), F32)],
        compiler_params=_cp(("parallel", "arbitrary"), _VMEM_MID),
    )(at, b)


def _pool_bwd(dpooled):
    S = dpooled.shape[0]

    def body(d_ref, o_ref):
        d = d_ref[...]
        t = lax.broadcasted_iota(jnp.int32, d.shape, 0)
        lane = lax.broadcasted_iota(jnp.int32, d.shape, 1)
        cnt = jnp.minimum(t + 1, _pool_window_lanes(d.shape)).astype(F32)
        u = d / cnt

        def fwd(a, k):
            return jnp.where(t < S - k, pltpu.roll(a, S - k, 0), 0.0)

        s1 = u + fwd(u, 1)
        s2 = s1 + fwd(s1, 2)
        s4 = s2 + fwd(s2, 4)
        s8 = s4 + fwd(s4, 8)
        win = jnp.where(lane < 64, s1, jnp.where(lane < 128, s2, jnp.where(lane < 192, s4, s8)))
        o_ref[...] = (win - d).astype(BF16)

    return pl.pallas_call(
        body, name="pool_bwd",
        grid=(1,),
        in_specs=[pl.BlockSpec((S, POOL_W), lambda i: (0, 0))],
        out_specs=pl.BlockSpec((S, POOL_W), lambda i: (0, 0)),
        out_shape=jax.ShapeDtypeStruct((S, POOL_W), BF16),
        compiler_params=_cp(("arbitrary",), _VMEM_BIG),
    )(dpooled)


def _fox_bwd(qs, kn, proj, dfo, fo, lse, cqb, crow4, ride=None):
    S = qs.shape[0]
    T = min(_T, S)
    nq = S // T
    n_pairs = FOX_W // 128

    def body(*refs):
        if ride is None:
            q_ref, k_ref, v_ref, do_ref, o_ref, lse_ref, cq_ref, cr_ref = refs[:8]
            dq_ref, dk_ref, dv_ref, dck_ref, dcq_ref = refs[8:13]
            scr = refs[13:]
        else:
            q_ref, k_ref, v_ref, do_ref, o_ref, lse_ref, cq_ref, cr_ref, pa_ref, pb_ref = refs[:10]
            dq_ref, dk_ref, dv_ref, dck_ref, dcq_ref, ra_ref, rb_ref = refs[10:17]
            scr = refs[17:32]
            xrefs = (pa_ref, pb_ref, ra_ref, rb_ref) + tuple(refs[32:])

            @pl.when(pl.program_id(0) == 0)
            def _():
                _start_exchange("scatter", *xrefs)

        qa, qb, kta, ktb, vb, doa, dob, cka, ckb, dcka, dckb, dva, dqt, dcqa, dcqb = scr
        lane_s = _head_masks(S)
        q = q_ref[...]
        zq = jnp.zeros_like(q)
        qa[...] = jnp.where(lane_s, q, zq)
        qb[...] = jnp.where(lane_s, zq, q)
        vb[...] = v_ref[...].astype(BF16)
        do = do_ref[...].astype(BF16)
        doa[...] = jnp.where(lane_s, do, zq)
        dob[...] = jnp.where(lane_s, zq, do)
        cq = cq_ref[...]
        cka[...], ckb[...] = _spread_heads(cq)
        zs = jnp.zeros((S, 128), F32)
        dk_ref[...] = zs
        dva[...] = zs
        dcka[...] = zs
        dckb[...] = zs
        dcq_ref[...] = jnp.zeros((8, S), F32)
        row_t = lax.broadcasted_iota(jnp.int32, (128, T), 0) < HEAD_DIM

        def prep(c, carry):
            c0 = pl.multiple_of(c * T, T)
            kt = k_ref[pl.ds(c0, T), :].astype(F32).T
            kta[:, pl.ds(c0, T)] = jnp.where(row_t, kt, 0.0).astype(BF16)
            ktb[:, pl.ds(c0, T)] = jnp.where(row_t, 0.0, kt).astype(BF16)
            return carry

        lax.fori_loop(0, nq, prep, 0)
        causal = (lax.broadcasted_iota(jnp.int32, (T, T), 0) <= lax.broadcasted_iota(jnp.int32, (T, T), 1))

        heads = ((qa, kta, doa, cka, dcka, dcqa), (qb, ktb, dob, ckb, dckb, dcqb))

        def kv(js, r0, lss, dls, masked):
            cr = cr_ref[:, pl.ds(r0, T)]
            c0s = [pl.multiple_of(j * T, T) for j in js]
            ks = [k_ref[pl.ds(c0, T), :] for c0 in c0s]
            vs = [vb[pl.ds(c0, T), :] for c0 in c0s]
            qhs = [hd[0][pl.ds(r0, T), :] for hd in heads]
            dohs = [hd[2][pl.ds(r0, T), :] for hd in heads]
            ss = []
            for h, hd in enumerate(heads):
                row = []
                for k, c0 in zip(ks, c0s):
                    s = _dot_nt(k, qhs[h]) - jnp.tile(hd[3][pl.ds(c0, T), :], (1, T // 128))
                    row.append(jnp.where(causal, s, NEG) if masked else s)
                ss.append(row)
            ps = [[jnp.exp(s + (cr[h:h + 1, :] - lss[h])) for s in row] for h, row in enumerate(ss)]
            dps = [[_dot_nt(v, dohs[h]) for v in vs] for h in range(2)]
            dss = [[p * (dp - dls[h]) for p, dp in zip(ps[h], dps[h])] for h in range(2)]
            pbs = [[p.astype(BF16) for p in row] for row in ps]
            dsbs = [[ds.astype(BF16) for ds in row] for row in dss]
            for t, c0 in enumerate(c0s):
                dva[pl.ds(c0, T), :] = dva[pl.ds(c0, T), :] + (_dot(pbs[0][t], dohs[0]) + _dot(pbs[1][t], dohs[1]))
                dk_ref[pl.ds(c0, T), :] = dk_ref[pl.ds(c0, T), :] + (_dot(dsbs[0][t], qhs[0]) + _dot(dsbs[1][t], qhs[1]))
            dq = None
            for h, hd in enumerate(heads):
                for t, c0 in enumerate(c0s):
                    term = _dot(hd[1][:, pl.ds(c0, T)], dsbs[h][t])
                    dq = term if dq is None else dq + term
            dqt[...] = dqt[...] + dq
            for h, hd in enumerate(heads):
                col = jnp.sum(dss[h][0], axis=0, keepdims=True)
                for ds in dss[h][1:]:
                    col = col + jnp.sum(ds, axis=0, keepdims=True)
                hd[5][0:1, :] = hd[5][0:1, :] + col
                for ds, c0 in zip(dss[h], c0s):
                    fold = ds[:, 0:128]
                    for u in range(1, T // 128):
                        fold = fold + ds[:, 128 * u:128 * (u + 1)]
                    hd[4][pl.ds(c0, T), :] = hd[4][pl.ds(c0, T), :] - fold

        def qblk(i, carry):
            r0 = pl.multiple_of(i * T, T)
            dt = (do_ref[pl.ds(r0, T), :] * o_ref[pl.ds(r0, T), :]).T
            dla = jnp.sum(jnp.where(row_t, dt, 0.0), axis=0, keepdims=True)
            dlb = jnp.sum(jnp.where(row_t, 0.0, dt), axis=0, keepdims=True)
            ls = lse_ref[:, pl.ds(r0, T)]
            lss = (ls[0:1, :], ls[1:2, :])
            back = jnp.max(ls[2:3, :]).astype(jnp.int32)
            dqt[...] = jnp.zeros((128, T), F32)
            dcqa[...] = jnp.zeros((8, T), F32)
            dcqb[...] = jnp.zeros((8, T), F32)
            kv([i], r0, lss, (dla, dlb), True)
            _for_tiles_back(i, back, lambda js: kv(js, r0, lss, (dla, dlb), False), fours=True)
            dq_ref[pl.ds(r0, T), :] = dqt[...].T
            dcq_ref[0:1, pl.ds(r0, T)] = dcqa[0:1, :]
            dcq_ref[1:2, pl.ds(r0, T)] = dcqb[0:1, :]
            return carry

        lax.fori_loop(0, nq, qblk, 0)
        dv_ref[...] = dva[...].astype(BF16)
        dck_ref[...] = jnp.where(lane_s, jnp.sum(dcka[...], axis=1, keepdims=True),
                                 jnp.sum(dckb[...], axis=1, keepdims=True))
        if ride is not None:
            @pl.when(pl.program_id(0) == n_pairs - 1)
            def _():
                _wait_exchange("scatter", *xrefs)

    extra = () if ride is None else tuple(ride)
    return pl.pallas_call(
        body, name="fox_bwd" if ride is None else "fox_bwd_exchange",
        grid=(n_pairs,),
        in_specs=[_pair_blk(S), _pair_blk(S), _pair_blk(S, OFF_FV // 128), _pair_blk(S), _pair_blk(S),
                  _pair_rows(S), _pair_blk(S), _pair_rows(S)] + [_ANY] * len(extra),
        out_specs=[_pair_blk(S), _pair_blk(S), _pair_blk(S), _pair_blk(S), _pair_rows(S)] + [_ANY] * len(extra),
        out_shape=[jax.ShapeDtypeStruct((S, FOX_W), F32), jax.ShapeDtypeStruct((S, FOX_W), F32),
                   jax.ShapeDtypeStruct((S, FOX_W), BF16), jax.ShapeDtypeStruct((S, FOX_W), F32),
                   jax.ShapeDtypeStruct((n_pairs, 8, S), F32)]
        + (_exchange_out_shapes("scatter", *extra) if extra else []),
        scratch_shapes=[pltpu.VMEM((S, 128), BF16)] * 2 + [pltpu.VMEM((128, S), BF16)] * 2
        + [pltpu.VMEM((S, 128), BF16)] * 3 + [pltpu.VMEM((S, 128), F32)] * 5
        + [pltpu.VMEM((128, T), F32)] + [pltpu.VMEM((8, T), F32)] * 2
        + (_EXCHANGE_SEMS if extra else []),
        compiler_params=_cp(("arbitrary",), _VMEM_BIG),
    )(qs, kn, proj, dfo, fo, lse, cqb, crow4, *extra)


def _sb_bwd(proj, dso, ltot, tril):
    S = proj.shape[0]
    T = tril.shape[0]
    nq = S // T

    def body(q_ref, k_ref, v_ref, do_ref, lt_ref, tri_ref, dq_ref, dk_ref, dv_ref,
             qa, qb, k2, kta, ktb, vb, doa, dob, dka, dva, dqt, ra, rb, ga, gb):
        lane_s = _head_masks(S)
        q = (q_ref[...] * Q_SCALE).astype(BF16)
        zq = jnp.zeros_like(q)
        qa[...] = jnp.where(lane_s, q, zq)
        qb[...] = jnp.where(lane_s, zq, q)
        k2[...] = k_ref[...].astype(BF16)
        vb[...] = v_ref[...].astype(BF16)
        do = do_ref[...].astype(BF16)
        doa[...] = jnp.where(lane_s, do, zq)
        dob[...] = jnp.where(lane_s, zq, do)
        dka[...] = jnp.zeros((S, 128), F32)
        dva[...] = jnp.zeros((S, 128), F32)
        row_t = lax.broadcasted_iota(jnp.int32, (128, T), 0) < HEAD_DIM

        def prep(c, carry):
            c0 = pl.multiple_of(c * T, T)
            kt = k_ref[pl.ds(c0, T), :].T
            kta[:, pl.ds(c0, T)] = jnp.where(row_t, kt, 0.0).astype(BF16)
            ktb[:, pl.ds(c0, T)] = jnp.where(row_t, 0.0, kt).astype(BF16)
            return carry

        lax.fori_loop(0, nq, prep, 0)
        strict = (lax.broadcasted_iota(jnp.int32, (T, T), 0) < lax.broadcasted_iota(jnp.int32, (T, T), 1))

        heads = ((qa, kta, doa, ra, ga), (qb, ktb, dob, rb, gb))

        def kv(tiles, r0, lts):
            tri = tri_ref[...]
            c0s = [pl.multiple_of(j * T, T) for j, _ in tiles]
            ks = [k2[pl.ds(c0, T), :] for c0 in c0s]
            vs = [vb[pl.ds(c0, T), :] for c0 in c0s]
            qhs = [hd[0][pl.ds(r0, T), :] for hd in heads]
            dohs = [hd[2][pl.ds(r0, T), :] for hd in heads]
            zs = [[_dot_nt(k, qh) for k in ks] for qh in qhs]
            das = [[_dot_nt(v, doh) for v in vs] for doh in dohs]
            es, lbs = [], []
            for row in zs:
                erow, lrow = [], []
                for z, (_, masked) in zip(row, tiles):
                    e, sp = _softplus_parts(z)
                    erow.append(e)
                    lrow.append(jnp.where(strict, -sp, 0.0) if masked else -sp)
                es.append(erow)
                lbs.append(lrow)
            pres = [[_mm2(lb, tri, left=True) for lb in row] for row in lbs]
            aas, r_ends = [], []
            for hd, zrow, lrow, prow, lt in zip(heads, zs, lbs, pres, lts):
                r = hd[3][0:1, :]
                arow = []
                for z, lb, pre, (_, masked) in zip(zrow, lrow, prow, tiles):
                    a = jnp.exp(z + lb + ((lt - r) - pre))
                    arow.append(jnp.where(strict, a, 0.0) if masked else a)
                    r = r + pre[T - 1:T, :]
                aas.append(arow)
                r_ends.append(r)
            gs = [[a * da for a, da in zip(arow, drow)] for arow, drow in zip(aas, das)]
            gpres = [[_mm2(g, tri, left=True) for g in row] for row in gs]
            dzbs, g_ends = [], []
            for hd, zrow, erow, grow, gprow in zip(heads, zs, es, gs, gpres):
                gc = hd[4][0:1, :]
                drow = []
                for z, e, g, gpre, (_, masked) in zip(zrow, erow, grow, gprow, tiles):
                    inv = 1.0 / (1.0 + e)
                    pos = z >= 0.0
                    sig = jnp.where(pos, 1.0, e) * inv
                    oms = jnp.where(pos, e, 1.0) * inv
                    dz = g * oms - sig * (gc + (gpre - g))
                    if masked:
                        dz = jnp.where(strict, dz, 0.0)
                    drow.append(dz.astype(BF16))
                    gc = gc + gpre[T - 1:T, :]
                dzbs.append(drow)
                g_ends.append(gc)
            dq = None
            for h, hd in enumerate(heads):
                for t, c0 in enumerate(c0s):
                    term = _dot(hd[1][:, pl.ds(c0, T)], dzbs[h][t])
                    dq = term if dq is None else dq + term
            dqt[...] = dqt[...] + dq
            for t, c0 in enumerate(c0s):
                dka[pl.ds(c0, T), :] = dka[pl.ds(c0, T), :] + (_dot(dzbs[0][t], qhs[0]) + _dot(dzbs[1][t], qhs[1]))
                dva[pl.ds(c0, T), :] = dva[pl.ds(c0, T), :] + (_dot(aas[0][t].astype(BF16), dohs[0])
                                                               + _dot(aas[1][t].astype(BF16), dohs[1]))
            for hd, r, gc in zip(heads, r_ends, g_ends):
                hd[3][0:1, :] = r
                hd[4][0:1, :] = gc

        def qblk(i, carry):
            r0 = pl.multiple_of(i * T, T)
            lt = lt_ref[:, pl.ds(r0, T)]
            lts = (lt[0:1, :], lt[1:2, :])
            back = jnp.max(lt[2:3, :]).astype(jnp.int32)
            zt = jnp.zeros((8, T), F32)
            dqt[...] = jnp.zeros((128, T), F32)
            ra[...] = zt
            rb[...] = zt
            ga[...] = zt
            gb[...] = zt

            def inner(j, c):
                kv([(j, False)], r0, lts)
                return c

            @pl.when(back == 0)
            def _():
                kv([(i, True)], r0, lts)

            @pl.when(back > 0)
            def _():
                lax.fori_loop(i - back, i - 1, inner, 0)
                kv([(i - 1, False), (i, True)], r0, lts)
            dq_ref[pl.ds(r0, T), :] = (dqt[...] * Q_SCALE).T.astype(BF16)
            return carry

        lax.fori_loop(0, nq, qblk, 0)
        dk_ref[...] = dka[...].astype(BF16)
        dv_ref[...] = dva[...].astype(BF16)

    return pl.pallas_call(
        body, name="sb_bwd",
        grid=(SB_W // 128,),
        in_specs=[_pair_blk(S, OFF_SQ // 128), _pair_blk(S, OFF_SK // 128), _pair_blk(S, OFF_SV // 128),
                  _pair_blk(S), _pair_rows(S), pl.BlockSpec((T, T), lambda p: (0, 0))],
        out_specs=[_pair_blk(S), _pair_blk(S), _pair_blk(S)],
        out_shape=[jax.ShapeDtypeStruct((S, SB_W), BF16)] * 3,
        scratch_shapes=([pltpu.VMEM((S, 128), BF16)] * 3 + [pltpu.VMEM((128, S), BF16)] * 2
                        + [pltpu.VMEM((S, 128), BF16)] * 3 + [pltpu.VMEM((S, 128), F32)] * 2
                        + [pltpu.VMEM((128, T), F32)] + [pltpu.VMEM((8, T), F32)] * 4),
        compiler_params=_cp(("arbitrary",), _VMEM_BIG),
    )(proj, proj, proj, dso, ltot, tril)


def _head_norm_bwd(x, g, dy, bd):
    ss = _mm2(x * x, bd)
    r = lax.rsqrt(ss * (1.0 / HEAD_DIM) + EPS)
    xr = x * r
    gdy = g * dy
    m = _mm2(xr * gdy, bd) * (1.0 / HEAD_DIM)
    return r * (gdy - xr * m), dy * xr


def _qk_bwd(dqs, dkn, proj, pff, bfp, gq, gk, bd, dccol, triu):
    S = proj.shape[0]
    T = triu.shape[0]
    n = S // T
    rev = lambda col: (lambda i: (n - 1 - i, col))

    def body(dq_ref, dk_ref, q_ref, k_ref, ff_ref, b_ref, gq_ref, gk_ref, bd_ref, dc_ref, tri_ref,
             dfq_ref, dfk_ref, dff_ref, dgq_ref, dgk_ref, dbf_ref, carry):
        @pl.when(pl.program_id(0) == 0)
        def _():
            carry[...] = jnp.zeros_like(carry)
            dgq_ref[...] = jnp.zeros_like(dgq_ref)
            dgk_ref[...] = jnp.zeros_like(dgk_ref)
            dbf_ref[...] = jnp.zeros_like(dbf_ref)

        bdv = bd_ref[...]
        dxq, gq_rows = _head_norm_bwd(q_ref[...], gq_ref[...], dq_ref[...] * Q_SCALE, bdv)
        dfq_ref[...] = dxq.astype(BF16)
        dgq_ref[...] = dgq_ref[...] + jnp.sum(gq_rows, axis=0, keepdims=True)
        dxk, gk_rows = _head_norm_bwd(k_ref[...], gk_ref[...], dk_ref[...], bdv)
        dfk_ref[...] = dxk.astype(BF16)
        dgk_ref[...] = dgk_ref[...] + jnp.sum(gk_rows, axis=0, keepdims=True)
        dlf = _mm3(dc_ref[...], tri_ref[...], left=True) + carry[0:1, :]
        carry[0:1, :] = dlf[0:1, :]
        u = ff_ref[...] + b_ref[...]
        lane = lax.broadcasted_iota(jnp.int32, u.shape, 1)
        dff = jnp.where(lane < N_FF, dlf * _sigmoid(-u), 0.0)
        dff_ref[...] = dff.astype(BF16)
        dbf_ref[...] = dbf_ref[...] + jnp.sum(dff, axis=0, keepdims=True)

    return pl.pallas_call(
        body, name="qk_bwd",
        grid=(n,),
        in_specs=[pl.BlockSpec((T, FOX_W), rev(0)), pl.BlockSpec((T, FOX_W), rev(0)),
                  pl.BlockSpec((T, FOX_W), rev(OFF_FQ // FOX_W)), pl.BlockSpec((T, FOX_W), rev(OFF_FK // FOX_W)),
                  pl.BlockSpec((T, N_FFPAD), rev(0)),
                  pl.BlockSpec((1, N_FFPAD), lambda i: (0, 0)),
                  pl.BlockSpec((1, FOX_W), lambda i: (0, 0)), pl.BlockSpec((1, FOX_W), lambda i: (0, 0)),
                  pl.BlockSpec((FOX_W, FOX_W), lambda i: (0, 0)),
                  pl.BlockSpec((T, N_FFPAD), rev(0)),
                  pl.BlockSpec((T, T), lambda i: (0, 0))],
        out_specs=[pl.BlockSpec((T, FOX_W), rev(0)), pl.BlockSpec((T, FOX_W), rev(0)),
                   pl.BlockSpec((T, N_FFPAD), rev(0)),
                   pl.BlockSpec((1, FOX_W), lambda i: (0, 0)), pl.BlockSpec((1, FOX_W), lambda i: (0, 0)),
                   pl.BlockSpec((1, N_FFPAD), lambda i: (0, 0))],
        out_shape=[jax.ShapeDtypeStruct((S, FOX_W), BF16), jax.ShapeDtypeStruct((S, FOX_W), BF16),
                   jax.ShapeDtypeStruct((S, N_FFPAD), BF16),
                   jax.ShapeDtypeStruct((1, FOX_W), F32), jax.ShapeDtypeStruct((1, FOX_W), F32),
                   jax.ShapeDtypeStruct((1, N_FFPAD), F32)],
        scratch_shapes=[pltpu.VMEM((8, N_FFPAD), F32)],
        compiler_params=_cp(("arbitrary",), _VMEM_MID),
    )(dqs, dkn, proj, proj, pff, bfp, gq, gk, bd, dccol, triu)


def _dproj_layout(pieces):
    offs, o = [], 0
    for p in pieces:
        offs.append(o)
        o += p.shape[1]
    assert o == N_MAIN
    return offs


def _inproj_bwd_dx(pieces, dff, wm, wff, x, g, dy, ride=None):
    S, D = x.shape
    tm = min(_TM_DX, S)
    steps = S // tm
    offs = _dproj_layout(pieces)
    n = len(pieces)

    def body(*refs):
        p_refs = refs[:n]
        if ride is None:
            dff_ref, w_ref, wff_ref, x_ref, g_ref, dy_ref, dx_ref, dg_ref = refs[n:]
        else:
            dff_ref, w_ref, wff_ref, x_ref, g_ref, dy_ref, pa_ref, pb_ref = refs[n:n + 8]
            dx_ref, dg_ref, ra_ref, rb_ref = refs[n + 8:n + 12]
            xrefs = (pa_ref, pb_ref, ra_ref, rb_ref) + tuple(refs[n + 12:])

        @pl.when(pl.program_id(0) == 0)
        def _():
            dg_ref[...] = jnp.zeros_like(dg_ref)
            if ride is not None:
                _start_exchange("scatter", *xrefs)

        dh = _dot_nt(dff_ref[...], wff_ref[...])
        for p_ref, off in zip(p_refs, offs):
            dh = dh + _dot_nt(p_ref[...], w_ref[:, off:off + p_ref.shape[1]])
        xv = x_ref[...]
        r = _rms_rows(xv)
        xr = xv * r
        dg_ref[...] = dg_ref[...] + jnp.sum(dh * xr, axis=0, keepdims=True)
        gdh = g_ref[...] * dh
        m = jnp.mean(gdh * xr, axis=-1, keepdims=True)
        dx_ref[...] = dy_ref[...] + r * (gdh - xr * m)
        if ride is not None:
            @pl.when(pl.program_id(0) == steps - 1)
            def _():
                _wait_exchange("scatter", *xrefs)

    extra = () if ride is None else tuple(ride)
    return pl.pallas_call(
        body, name="inproj_bwd_dx" if ride is None else "inproj_bwd_dx_exchange",
        grid=(steps,),
        in_specs=[pl.BlockSpec((tm, p.shape[1]), lambda i: (i, 0)) for p in pieces]
        + [pl.BlockSpec((tm, N_FFPAD), lambda i: (i, 0)),
                  pl.BlockSpec((D, N_MAIN), lambda i: (0, 0)),
                  pl.BlockSpec((D, N_FFPAD), lambda i: (0, 0)),
                  pl.BlockSpec((tm, D), lambda i: (i, 0)),
                  pl.BlockSpec((1, D), lambda i: (0, 0)),
                  pl.BlockSpec((tm, D), lambda i: (i, 0))] + [_ANY] * len(extra),
        out_specs=[pl.BlockSpec((tm, D), lambda i: (i, 0)), pl.BlockSpec((1, D), lambda i: (0, 0))] + [_ANY] * len(extra),
        out_shape=[jax.ShapeDtypeStruct((S, D), F32), jax.ShapeDtypeStruct((1, D), F32)]
        + (_exchange_out_shapes("scatter", *extra) if extra else []),
        scratch_shapes=_EXCHANGE_SEMS if extra else [],
        compiler_params=_cp(("arbitrary",), _VMEM_WIDE),
    )(*pieces, dff, wm, wff, x, g, dy, *extra)


def _inproj_bwd_dw(ht, pieces, dff):
    D, S = ht.shape
    tk = min(_TK_DW, S)
    nk = S // tk
    offs = _dproj_layout(pieces)
    n = len(pieces)

    def body(*refs):
        ht_ref, p_refs, dff_ref = refs[0], refs[1:1 + n], refs[1 + n]
        dw_ref, dwff_ref, acc, accff = refs[2 + n:]
        k = pl.program_id(0)

        @pl.when(k == 0)
        def _():
            acc[...] = jnp.zeros_like(acc)
            accff[...] = jnp.zeros_like(accff)

        hb = ht_ref[...]
        for p_ref, off in zip(p_refs, offs):
            w = p_ref.shape[1]
            acc[:, off:off + w] = acc[:, off:off + w] + _dot(hb, p_ref[...])
        accff[...] = accff[...] + _dot(hb, dff_ref[...])

        @pl.when(k == nk - 1)
        def _():
            dw_ref[...] = acc[...].astype(BF16)
            dwff_ref[...] = accff[...].astype(BF16)

    return pl.pallas_call(
        body, name="inproj_bwd_dw",
        grid=(nk,),
        in_specs=[pl.BlockSpec((D, tk), lambda k: (0, k))]
        + [pl.BlockSpec((tk, p.shape[1]), lambda k: (k, 0)) for p in pieces]
        + [pl.BlockSpec((tk, N_FFPAD), lambda k: (k, 0))],
        out_specs=[pl.BlockSpec((D, N_MAIN), lambda k: (0, 0), pipeline_mode=pl.Buffered(1)),
                   pl.BlockSpec((D, N_FFPAD), lambda k: (0, 0), pipeline_mode=pl.Buffered(1))],
        out_shape=[jax.ShapeDtypeStruct((D, N_MAIN), BF16), jax.ShapeDtypeStruct((D, N_FFPAD), BF16)],
        scratch_shapes=[pltpu.VMEM((D, N_MAIN), F32), pltpu.VMEM((D, N_FFPAD), F32)],
        compiler_params=_cp(("arbitrary",), _VMEM_BIG),
    )(ht, *pieces, dff)


def _constants(T):
    tril = jnp.tril(jnp.ones((T, T), F32)).astype(BF16)
    hid = jnp.arange(FOX_W) // HEAD_DIM
    bd = (hid[:, None] == hid[None, :]).astype(BF16)
    ex = (jnp.arange(N_FFPAD)[:, None] == hid[None, :]).astype(BF16)
    return tril, tril.T, bd, ex


def _crow4(ccol, T):
    S = ccol.shape[0]
    c = ccol[:, :FOX_HEADS].T
    last = jnp.pad(c[:, T - 1::T], ((0, 0), (0, S - S // T)))
    rows = jnp.concatenate([c.reshape(FOX_HEADS // 2, 2, S), last.reshape(FOX_HEADS // 2, 2, S)], axis=1)
    return jnp.pad(rows, ((0, 0), (0, 4), (0, 0)))


def _layer_fwd(x, lw, consts, ride=None):
    tril, triu, bd, ex = consts
    proj, pff, ht = _inproj_fwd(x, lw["g"], lw["wm"], lw["wff"])
    qs, kn, ccol, cqb = _fox_prep(proj, pff, lw["bfp"], lw["gq"], lw["gk"], bd, ex, tril)
    crow4 = _crow4(ccol, tril.shape[0])
    fo, lse, *gathered = _fox_fwd(qs, kn, proj, cqb, crow4, ride)
    so, ltot = _sb_fwd(proj, triu)
    pooled = _pool_fwd(proj)
    y, mixedt = _mix_out(fo, so, pooled, proj, lw["wbd"], lw["scale"], lw["wout"], x)
    return y, (x, proj, pff, ht, qs, kn, cqb, crow4, fo, lse, so, ltot, pooled, mixedt), gathered


def _layer_bwd(dy, saved, lw, consts, ride=None, exchange_own=False):
    tril, triu, bd, _ = consts
    x, proj, pff, ht, qs, kn, cqb, crow4, fo, lse, so, ltot, pooled, mixedt = saved
    S = x.shape[0]
    dfo, dfg, dso, dsg, dpg, dpooled, dscale, dwbd = _gate_bwd(dy, lw["wout"], fo, so, pooled, proj, lw["wbd"], lw["scale"])
    dwout = _matmul_acc(mixedt, dy, "dw_out")
    dpx = _pool_bwd(dpooled)
    dqs, dkn, dfv, dck, dcq4, *received = _fox_bwd(qs, kn, proj, dfo, fo, lse, cqb, crow4, ride)
    dsq, dsk, dsv = _sb_bwd(proj, dso, ltot, tril)
    dc8 = dck[:, ::HEAD_DIM] + dcq4[:, :2, :].reshape(FOX_HEADS, S).T
    dccol = jnp.pad(dc8, ((0, 0), (0, N_FFPAD - FOX_HEADS)))
    dfq, dfk, dff, dgq, dgk, dbf = _qk_bwd(dqs, dkn, proj, pff, lw["bfp"], lw["gq"], lw["gk"], bd, dccol, triu)
    pieces = [dfq, dfk, dfv, dfg, dpx, dpg, dsq, dsk, dsv, dsg]
    dwm, dwff = _inproj_bwd_dw(ht, pieces, dff)
    dwin = jnp.concatenate([dwm[:, :OFF_PX], dwff[:, :N_FF], dwm[:, OFF_PX:]], axis=1)
    own = _grad_parts({"w_in": dwin, "w_out": dwout}) if exchange_own else None
    dx, dng, *received_own = _inproj_bwd_dx(pieces, dff, lw["wm"], lw["wff"], x, lw["g"], dy, own)
    grads = {
        "norm_g": dng[0],
        "w_in": dwin,
        "b_f": dbf[0, :N_FF],
        "q_norm_g": dgq[0].reshape(FOX_HEADS, HEAD_DIM).sum(0),
        "k_norm_g": dgk[0].reshape(FOX_HEADS, HEAD_DIM).sum(0),
        "w_pool": jnp.stack([dwbd[64 * i:64 * i + 64, 64 * i:64 * i + 64] for i in range(4)]),
        "pool_scale": dscale[0],
        "w_out": dwout,
    }
    return dx, grads, received, received_own


def _layer_weights(l, norm_g, gin, b_f, q_norm_g, k_norm_g, w_pool, pool_scale, gout):
    D = gin.shape[1]
    w = gin.transpose(1, 0, 2).reshape(D, D_IN)
    wm = jnp.concatenate([w[:, :2048], w[:, 2048 + N_FF:]], axis=1)
    wff = jnp.pad(w[:, 2048:2048 + N_FF], ((0, 0), (0, N_FFPAD - N_FF)))
    grp = jnp.arange(POOL_W) // 64
    wbd = jnp.where(grp[:, None] == grp[None, :], jnp.tile(w_pool[l].transpose(1, 0, 2).reshape(64, POOL_W), (4, 1)), 0.0)
    return {
        "g": norm_g[l].reshape(1, D),
        "wm": wm, "wff": wff,
        "bfp": jnp.pad(b_f[l], (0, N_FFPAD - N_FF)).reshape(1, N_FFPAD),
        "gq": jnp.tile(q_norm_g[l], FOX_HEADS).reshape(1, FOX_W),
        "gk": jnp.tile(k_norm_g[l], FOX_HEADS).reshape(1, FOX_W),
        "wbd": wbd.astype(BF16),
        "scale": pool_scale[l].reshape(1, POOL_W),
        "wout": gout.reshape(D_MIX, D),
    }


def _grad_parts(g):
    dwin, dwout = g["w_in"].astype(BF16), g["w_out"].astype(BF16)
    D = dwin.shape[0]
    return (dwin.reshape(D, N_DEV, D_IN // N_DEV).transpose(1, 0, 2),
            dwout.reshape(N_DEV, D_MIX // N_DEV, dwout.shape[1]))


def _train_step(x, target, norm_g, win_sh, b_f, q_norm_g, k_norm_g, w_pool, pool_scale, wout_sh):
    L = norm_g.shape[0]
    consts = _constants(min(_T, x.shape[0]))
    gathered = _gather_two_level(win_sh[0], wout_sh[0], "gather_weights")
    lws, saved = [], []
    h = x
    for l in range(L):
        lws.append(_layer_weights(l, norm_g, gathered[0], b_f, q_norm_g, k_norm_g, w_pool, pool_scale, gathered[1]))
        ride = (win_sh[l + 1], wout_sh[l + 1]) if l + 1 < L else None
        h, sv, gathered = _layer_fwd(h, lws[l], consts, ride)
        saved.append(sv)
    dy, loss = _loss_head(h, target)
    grads, received = [None] * L, [None] * L
    ride = None
    for l in reversed(range(L)):
        dy, grads[l], got, got_own = _layer_bwd(dy, saved[l], lws[l], consts, ride, exchange_own=(l == 0))
        if ride is not None:
            received[l + 1] = got
        if l == 0:
            received[0] = got_own
        else:
            ride = _grad_parts(grads[l])
    return loss, dy, grads, received


def _mesh_pos():
    return lax.axis_index("x"), lax.axis_index("y"), lax.axis_index("c")


_FLIPS = [(0, 0, 1), (1, 0, 0), (0, 1, 0), (1, 1, 0), (1, 0, 1), (0, 1, 1), (1, 1, 1)]


def _peers():
    x, y, c = _mesh_pos()
    out = []
    for fx, fy, fc in _FLIPS:
        px = 1 - x if fx else x
        py = 1 - y if fy else y
        pc = 1 - c if fc else c
        out.append(((px, py, pc), 4 * px + 2 * py + pc))
    return out, 4 * x + 2 * y + c


_EXCHANGE_SEMS = [pltpu.SemaphoreType.DMA((14,)), pltpu.SemaphoreType.DMA((14,)), pltpu.SemaphoreType.DMA((2,))]
_ANY = pl.BlockSpec(memory_space=pl.ANY)


def _exchange_copies(kind, a_ref, b_ref, oa_ref, ob_ref, send_sems, recv_sems, loc_sems):
    peers, me = _peers()
    pairs = ((a_ref, oa_ref), (b_ref, ob_ref))
    local = [pltpu.make_async_copy(src if kind == "gather" else src.at[me], dst.at[me], loc_sems.at[t])
             for t, (src, dst) in enumerate(pairs)]
    remote = []
    for k, (dev, idx) in enumerate(peers):
        for t, (src, dst) in enumerate(pairs):
            remote.append(pltpu.make_async_remote_copy(
                src_ref=src if kind == "gather" else src.at[idx], dst_ref=dst.at[me],
                send_sem=send_sems.at[2 * k + t], recv_sem=recv_sems.at[2 * k + t],
                device_id=dev, device_id_type=pl.DeviceIdType.MESH))
    return local, remote


def _start_exchange(kind, *refs):
    local, remote = _exchange_copies(kind, *refs)
    for cp in local + remote:
        cp.start()


def _wait_exchange(kind, *refs):
    local, remote = _exchange_copies(kind, *refs)
    for cp in remote:
        cp.wait_recv()
    for cp in remote:
        cp.wait_send()
    for cp in local:
        cp.wait()


def _exchange_out_shapes(kind, a, b):
    if kind == "gather":
        return [jax.ShapeDtypeStruct((N_DEV,) + a.shape, a.dtype), jax.ShapeDtypeStruct((N_DEV,) + b.shape, b.dtype)]
    return [jax.ShapeDtypeStruct(a.shape, a.dtype), jax.ShapeDtypeStruct(b.shape, b.dtype)]


def _gather_two_level(a, b, name):
    def body(a_ref, b_ref, ga_ref, gb_ref, send_sems, recv_sems, loc_sems):
        x, y, c = _mesh_pos()
        slot_of = lambda px, py, pc: 4 * px + 2 * py + pc
        me, sib = slot_of(x, y, c), slot_of(x, y, 1 - c)
        chips = [(1 - x, y), (x, 1 - y), (1 - x, 1 - y)]
        pairs = ((a_ref, ga_ref), (b_ref, gb_ref))

        def copy(k, t, slot, to, src=None):
            dst = pairs[t][1].at[slot]
            return pltpu.make_async_remote_copy(
                src_ref=dst if src is None else src, dst_ref=dst, send_sem=send_sems.at[2 * k + t],
                recv_sem=recv_sems.at[2 * k + t], device_id=to, device_id_type=pl.DeviceIdType.MESH)

        local = [pltpu.make_async_copy(src, dst.at[me], loc_sems.at[t]) for t, (src, dst) in enumerate(pairs)]
        first = []
        for t, (src, _) in enumerate(pairs):
            first.append(copy(0, t, me, (x, y, 1 - c), src))
            first += [copy(1 + j, t, me, (*chip, c), src) for j, chip in enumerate(chips)]
        for cp in local + first:
            cp.start()
        passed = []
        for j, chip in enumerate(chips):
            for t in range(2):
                landed = slot_of(*chip, c)
                copy(1 + j, t, landed, (x, y, c)).wait_recv()
                cp = copy(4 + j, t, landed, (x, y, 1 - c))
                cp.start()
                passed.append(cp)
        for t in range(2):
            copy(0, t, sib, (x, y, c)).wait_recv()
            for j, chip in enumerate(chips):
                copy(4 + j, t, slot_of(*chip, 1 - c), (x, y, c)).wait_recv()
        for cp in first + passed:
            cp.wait_send()
        for cp in local:
            cp.wait()

    return pl.pallas_call(
        body, name=name,
        in_specs=[_ANY, _ANY], out_specs=[_ANY, _ANY],
        out_shape=_exchange_out_shapes("gather", a, b),
        scratch_shapes=_EXCHANGE_SEMS,
    )(a, b)


def _adam_math(w, g, m, v):
    m_new = ADAM_B1 * m + (1.0 - ADAM_B1) * g
    v_new = ADAM_B2 * v + (1.0 - ADAM_B2) * (g * g)
    m_hat = m_new / (1.0 - ADAM_B1 ** ADAM_STEP)
    v_hat = v_new / (1.0 - ADAM_B2 ** ADAM_STEP)
    delta = -ADAM_LR * (m_hat / (jnp.sqrt(v_hat) + ADAM_EPS) + ADAM_WD * w)
    return delta, m_new, v_new


def _sum_adamw(gparts, w, m, v, name):
    L, R, C = w.shape
    tr = min(128, R)

    def body(*refs):
        gp_refs = refs[:L]
        w_ref, m_ref, v_ref, g_ref, d_ref, nm_ref, nv_ref = refs[L:]
        for l in range(L):
            g = gp_refs[l][0].astype(F32)
            for s in range(1, N_DEV):
                g = g + gp_refs[l][s].astype(F32)
            d, mn, vn = _adam_math(w_ref[l], g, m_ref[l], v_ref[l])
            g_ref[l] = g
            d_ref[l] = d
            nm_ref[l] = mn
            nv_ref[l] = vn

    blk = pl.BlockSpec((L, tr, C), lambda r: (0, r, 0))
    return pl.pallas_call(
        body, name=name,
        grid=(R // tr,),
        in_specs=[pl.BlockSpec((N_DEV, tr, C), lambda r: (0, r, 0))] * L + [blk, blk, blk],
        out_specs=[blk, blk, blk, blk],
        out_shape=[jax.ShapeDtypeStruct((L, R, C), F32)] * 4,
        compiler_params=_cp(("parallel",), _VMEM_WIDE),
    )(*gparts, w, m, v)


def _small_update(gpack, wpack, mpack, vpack):
    R = gpack.shape[0]
    VM = pl.BlockSpec(memory_space=pltpu.VMEM)

    def body(g_ref, w_ref, m_ref, v_ref, gs_ref, d_ref, nm_ref, nv_ref, buf, send_sems, recv_sems):
        peers, me = _peers()
        buf[me] = g_ref[...]
        copies = []
        for k, (dev, _) in enumerate(peers):
            cp = pltpu.make_async_remote_copy(
                src_ref=g_ref, dst_ref=buf.at[me], send_sem=send_sems.at[k], recv_sem=recv_sems.at[k],
                device_id=dev, device_id_type=pl.DeviceIdType.MESH)
            cp.start()
            copies.append(cp)
        for cp in copies:
            cp.wait_recv()
        for cp in copies:
            cp.wait_send()
        g = buf[0]
        for s in range(1, N_DEV):
            g = g + buf[s]
        d, mn, vn = _adam_math(w_ref[...], g, m_ref[...], v_ref[...])
        gs_ref[...] = g
        d_ref[...] = d
        nm_ref[...] = mn
        nv_ref[...] = vn

    return pl.pallas_call(
        body, name="small_update",
        in_specs=[VM] * 4, out_specs=[VM] * 4,
        out_shape=[jax.ShapeDtypeStruct((R, 128), F32)] * 4,
        scratch_shapes=[pltpu.VMEM((N_DEV, R, 128), F32), pltpu.SemaphoreType.DMA((7,)), pltpu.SemaphoreType.DMA((7,))],
        compiler_params=_cp(None, _VMEM_MID),
    )(gpack, wpack, mpack, vpack)


_SMALL = ("norm_g", "b_f", "q_norm_g", "k_norm_g", "w_pool", "pool_scale")


def _pack(parts):
    flat = jnp.concatenate([p.reshape(-1) for p in parts])
    n = flat.shape[0]
    rows = -(-n // (8 * 128)) * 8
    return jnp.pad(flat, (0, rows * 128 - n)).reshape(rows, 128)


def _unpack(packed, like):
    flat = packed.reshape(-1)
    out, o = [], 0
    for p in like:
        out.append(flat[o:o + p.size].reshape(p.shape))
        o += p.size
    return out


def kernel(x, norm_g, w_in, b_f, q_norm_g, k_norm_g, w_pool, pool_scale, w_out, loss_target, m_norm_g, m_w_in, m_b_f, m_q_norm_g, m_k_norm_g, m_w_pool, m_pool_scale, m_w_out, v_norm_g, v_w_in, v_b_f, v_q_norm_g, v_k_norm_g, v_w_pool, v_pool_scale, v_w_out):
    L = w_in.shape[0]

    loss_local, dx, grads, received = _train_step(x[0], loss_target[0], norm_g, w_in.astype(BF16), b_f, q_norm_g,
                                                  k_norm_g, w_pool, pool_scale, w_out.astype(BF16))
    loss = lax.psum(loss_local, MESH_AXES)
    g = {k: jnp.stack([grads[l][k] for l in range(L)]) for k in _SMALL}

    g_win, d_win, nm_win, nv_win = _sum_adamw([r[0] for r in received], w_in, m_w_in, v_w_in, "adamw_w_in")
    g_wout, d_wout, nm_wout, nv_wout = _sum_adamw([r[1] for r in received], w_out, m_w_out, v_w_out, "adamw_w_out")

    ws = dict(norm_g=norm_g, b_f=b_f, q_norm_g=q_norm_g, k_norm_g=k_norm_g, w_pool=w_pool, pool_scale=pool_scale)
    ms = dict(norm_g=m_norm_g, b_f=m_b_f, q_norm_g=m_q_norm_g, k_norm_g=m_k_norm_g, w_pool=m_w_pool, pool_scale=m_pool_scale)
    vs = dict(norm_g=v_norm_g, b_f=v_b_f, q_norm_g=v_q_norm_g, k_norm_g=v_k_norm_g, w_pool=v_w_pool, pool_scale=v_pool_scale)
    like = [ws[k] for k in _SMALL]
    gs_p, d_p, nm_p, nv_p = _small_update(_pack([g[k] for k in _SMALL]), _pack(like),
                                          _pack([ms[k] for k in _SMALL]), _pack([vs[k] for k in _SMALL]))
    gs = dict(zip(_SMALL, _unpack(gs_p, like)))
    ds = dict(zip(_SMALL, _unpack(d_p, like)))
    nms = dict(zip(_SMALL, _unpack(nm_p, like)))
    nvs = dict(zip(_SMALL, _unpack(nv_p, like)))
    gs["w_in"], ds["w_in"], nms["w_in"], nvs["w_in"] = g_win, d_win, nm_win, nv_win
    gs["w_out"], ds["w_out"], nms["w_out"], nvs["w_out"] = g_wout, d_wout, nm_wout, nv_wout

    order = ("norm_g", "w_in", "b_f", "q_norm_g", "k_norm_g", "w_pool", "pool_scale", "w_out")
    return (loss, dx[None], *[gs[k] for k in order], *[ds[k] for k in order],
            *[nms[k] for k in order], *[nvs[k] for k in order])
```

```python
import jax
import jax.numpy as jnp
from jax import lax
from jax.experimental import pallas as pl
from jax.experimental.pallas import tpu as pltpu

F32 = jnp.float32
BF16 = jnp.bfloat16

EPS = 1e-6
NEG = -1e30
HEAD_DIM = 64
FOX_HEADS = 8
FOX_W = 512
POOL_W = 256
SB_W = 256
D_MIX = 1024
N_FF = 8
N_MAIN = 3584
N_FFPAD = 128
OFF_FQ, OFF_FK, OFF_FV, OFF_FG = 0, 512, 1024, 1536
OFF_PX, OFF_PG = 2048, 2304
OFF_SQ, OFF_SK, OFF_SV, OFF_SG = 2560, 2816, 3072, 3328
D_IN = 3592
Q_SCALE = HEAD_DIM ** -0.5

ADAM_LR = 0.001
ADAM_B1 = 0.9
ADAM_B2 = 0.999
ADAM_EPS = 1e-08
ADAM_WD = 0.01
ADAM_STEP = 10

N_DEV = 8
MESH_AXES = ("x", "y", "c")

_T = 256
_TM = 512
_TM_ROWS = 512
_TM_FWD, _TN_FWD = 2048, 512
_TM_DX = 512
_TK_DW = 1024
_VMEM_V7X = 64 << 20
_VMEM_BIG = _VMEM_V7X - (8 << 20)
_VMEM_MID = 40 << 20
_VMEM_WIDE = 48 << 20


def _cp(sem=None, vmem=None):
    kw = {}
    if sem is not None:
        kw["dimension_semantics"] = sem
    if vmem is not None:
        kw["vmem_limit_bytes"] = vmem
    return pltpu.CompilerParams(**kw)


def _dot(a, b):
    return jnp.dot(a, b, preferred_element_type=F32)


def _dot_nt(a, b):
    return lax.dot_general(a, b, (((1,), (1,)), ((), ())), preferred_element_type=F32)


def _dot_tn(a, b):
    return lax.dot_general(a, b, (((0,), (0,)), ((), ())), preferred_element_type=F32)


def _mm2(v, m, left=False):
    hi = v.astype(BF16)
    lo = (v - hi.astype(F32)).astype(BF16)
    if left:
        return _dot(m, hi) + _dot(m, lo)
    return _dot(hi, m) + _dot(lo, m)


def _mm3(v, m, left=False):
    a1 = v.astype(BF16)
    r1 = v - a1.astype(F32)
    a2 = r1.astype(BF16)
    a3 = (r1 - a2.astype(F32)).astype(BF16)
    if left:
        return _dot(m, a1) + _dot(m, a2) + _dot(m, a3)
    return _dot(a1, m) + _dot(a2, m) + _dot(a3, m)


def _sigmoid(z):
    return 1.0 / (1.0 + jnp.exp(-z))


def _rms_rows(x):
    return lax.rsqrt(jnp.mean(x * x, axis=-1, keepdims=True) + EPS)


def _inproj_fwd(x, g, wm, wff):
    S, D = x.shape
    tm = min(_TM_FWD, S)
    tn = _TN_FWD
    assert OFF_SQ % tn == 0 and N_MAIN - OFF_SQ == 4 * SB_W
    j_sb = OFF_SQ // tn

    def body(x_ref, g_ref, w_ref, wff_ref, o_ref, off_ref, ht_ref, sb_ref, h_ref):
        j = pl.program_id(1)

        @pl.when(j == 0)
        def _():
            xv = x_ref[...]
            h = (xv * _rms_rows(xv)) * g_ref[...]
            h_ref[...] = h.astype(BF16)
            ht_ref[...] = h.T.astype(BF16)
            off_ref[...] = _dot(h_ref[...], wff_ref[...])

        res = _dot(h_ref[...], w_ref[...])
        o_ref[...] = res

        @pl.when(j >= j_sb)
        def _():
            sb_ref[...] = res.astype(BF16)

    return pl.pallas_call(
        body, name="inproj_fwd",
        grid=(S // tm, N_MAIN // tn),
        in_specs=[pl.BlockSpec((tm, D), lambda i, j: (i, 0)),
                  pl.BlockSpec((1, D), lambda i, j: (0, 0)),
                  pl.BlockSpec((D, tn), lambda i, j: (0, j)),
                  pl.BlockSpec((D, N_FFPAD), lambda i, j: (0, 0))],
        out_specs=[pl.BlockSpec((tm, tn), lambda i, j: (i, j)),
                   pl.BlockSpec((tm, N_FFPAD), lambda i, j: (i, 0)),
                   pl.BlockSpec((D, tm), lambda i, j: (0, i)),
                   pl.BlockSpec((tm, tn), lambda i, j: (i, jnp.maximum(j - j_sb, 0)))],
        out_shape=[jax.ShapeDtypeStruct((S, N_MAIN), F32), jax.ShapeDtypeStruct((S, N_FFPAD), F32),
                   jax.ShapeDtypeStruct((D, S), BF16), jax.ShapeDtypeStruct((S, 4 * SB_W), BF16)],
        scratch_shapes=[pltpu.VMEM((tm, D), BF16)],
        compiler_params=_cp(("parallel", "arbitrary"), _VMEM_BIG),
    )(x, g, wm, wff)


def _head_norm(x, g, bd):
    ss = _mm2(x * x, bd)
    r = lax.rsqrt(ss * (1.0 / HEAD_DIM) + EPS)
    return (x * r) * g


def _fox_prep(proj, pff, bfp, gq, gk, bd, ex, tril):
    S = proj.shape[0]
    T = tril.shape[0]

    def body(q_ref, k_ref, ff_ref, b_ref, gq_ref, gk_ref, bd_ref, ex_ref, tri_ref,
             qs_ref, kn_ref, cc_ref, cqb_ref, carry):
        @pl.when(pl.program_id(0) == 0)
        def _():
            carry[...] = jnp.zeros_like(carry)

        bdv = bd_ref[...]
        qs_ref[...] = (_head_norm(q_ref[...], gq_ref[...], bdv) * Q_SCALE).astype(BF16)
        kn_ref[...] = _head_norm(k_ref[...], gk_ref[...], bdv).astype(BF16)
        u = ff_ref[...] + b_ref[...]
        lf = jnp.minimum(u, 0.0) - jnp.log1p(jnp.exp(-jnp.abs(u)))
        c = _mm3(lf, tri_ref[...], left=True) + carry[0:1, :]
        carry[0:1, :] = c[T - 1:T, :]
        cc_ref[...] = c
        cqb_ref[...] = _mm3(c, ex_ref[...])

    return pl.pallas_call(
        body, name="fox_prep",
        grid=(S // T,),
        in_specs=[pl.BlockSpec((T, FOX_W), lambda i: (i, OFF_FQ // FOX_W)),
                  pl.BlockSpec((T, FOX_W), lambda i: (i, OFF_FK // FOX_W)),
                  pl.BlockSpec((T, N_FFPAD), lambda i: (i, 0)),
                  pl.BlockSpec((1, N_FFPAD), lambda i: (0, 0)),
                  pl.BlockSpec((1, FOX_W), lambda i: (0, 0)),
                  pl.BlockSpec((1, FOX_W), lambda i: (0, 0)),
                  pl.BlockSpec((FOX_W, FOX_W), lambda i: (0, 0)),
                  pl.BlockSpec((N_FFPAD, FOX_W), lambda i: (0, 0)),
                  pl.BlockSpec((T, T), lambda i: (0, 0))],
        out_specs=[pl.BlockSpec((T, FOX_W), lambda i: (i, 0)),
                   pl.BlockSpec((T, FOX_W), lambda i: (i, 0)),
                   pl.BlockSpec((T, N_FFPAD), lambda i: (i, 0)),
                   pl.BlockSpec((T, FOX_W), lambda i: (i, 0))],
        out_shape=[jax.ShapeDtypeStruct((S, FOX_W), BF16), jax.ShapeDtypeStruct((S, FOX_W), BF16),
                   jax.ShapeDtypeStruct((S, N_FFPAD), F32), jax.ShapeDtypeStruct((S, FOX_W), F32)],
        scratch_shapes=[pltpu.VMEM((8, N_FFPAD), F32)],
        compiler_params=_cp(("arbitrary",), _VMEM_MID),
    )(proj, proj, pff, bfp, gq, gk, bd, ex, tril)


def _pair_blk(S, off=0):
    return pl.BlockSpec((S, 128), lambda p: (0, off + p), pipeline_mode=pl.Buffered(1))


def _pair_rows(S):
    return pl.BlockSpec((None, 8, S), lambda p: (p, 0, 0), pipeline_mode=pl.Buffered(1))


def _head_masks(S):
    return lax.broadcasted_iota(jnp.int32, (S, 128), 1) < HEAD_DIM


_EXP_ZERO = 104.0


def _spread_heads(x):
    src = lax.broadcasted_iota(jnp.int32, (128, 128), 0)
    return (_mm3(x, (src == 0).astype(BF16)), _mm3(x, (src == HEAD_DIM).astype(BF16)))


def _score_bounds(q, k):
    same_head = ((lax.broadcasted_iota(jnp.int32, (128, 128), 0) < HEAD_DIM)
                 == (lax.broadcasted_iota(jnp.int32, (128, 128), 1) < HEAD_DIM)).astype(BF16)

    def max_norm2(x):
        xf = x.astype(F32)
        return jnp.max(_mm2(xf * xf, same_head), axis=0, keepdims=True)

    z = jnp.sqrt(max_norm2(q) * max_norm2(k))
    z = jnp.where(z == z, z, jnp.inf)
    return jnp.max(z[:, 0:1]) * 1.001 + 1e-3, jnp.max(z[:, 64:65]) * 1.001 + 1e-3


def _for_tiles_back(i, n, tiles_fn, fours=False):
    if fours:
        def four(t, c):
            tiles_fn([i - 1 - 4 * t, i - 2 - 4 * t, i - 3 - 4 * t, i - 4 - 4 * t])
            return c

        lax.fori_loop(0, lax.shift_right_logical(n, 2), four, 0)
        rest = i - (n & ~3)

        @pl.when((n & 2) != 0)
        def _():
            tiles_fn([rest - 1, rest - 2])
    else:
        def two(t, c):
            tiles_fn([i - 1 - 2 * t, i - 2 - 2 * t])
            return c

        lax.fori_loop(0, lax.shift_right_logical(n, 1), two, 0)

    @pl.when((n & 1) != 0)
    def _():
        tiles_fn([i - n])


def _fox_tiles_back(cr_ref, i, r0, zba, zbb):
    last = cr_ref[:, pl.ds(0, 128)]
    first = cr_ref[:, pl.ds(r0, 128)]
    alive_a = 2.0 * zba + first[0:1, 0:1] - last[2:3, :] > -_EXP_ZERO
    alive_b = 2.0 * zbb + first[1:2, 0:1] - last[3:4, :] > -_EXP_ZERO
    before = lax.broadcasted_iota(jnp.int32, (1, 128), 1) < i
    return jnp.sum((before & (alive_a | alive_b)).astype(jnp.int32))


def _fox_fwd(qs, kn, proj, cqb, crow4, ride=None):
    S = qs.shape[0]
    T = min(_T, S)
    nq = S // T
    n_pairs = FOX_W // 128

    def body(*refs):
        if ride is None:
            q_ref, k_ref, v_ref, cq_ref, cr_ref, o_ref, lse_ref = refs[:7]
            qa, qb, vta, vtb, cka, ckb, ma, mb, acca, accb = refs[7:]
        else:
            q_ref, k_ref, v_ref, cq_ref, cr_ref, wa_ref, wb_ref, o_ref, lse_ref, ga_ref, gb_ref = refs[:11]
            qa, qb, vta, vtb, cka, ckb, ma, mb, acca, accb = refs[11:21]
            xrefs = (wa_ref, wb_ref, ga_ref, gb_ref) + tuple(refs[21:])

            @pl.when(pl.program_id(0) == 0)
            def _():
                _start_exchange("gather", *xrefs)

        lane_s = _head_masks(S)
        q = q_ref[...]
        zq = jnp.zeros_like(q)
        qa[...] = jnp.where(lane_s, q, zq)
        qb[...] = jnp.where(lane_s, zq, q)
        cq = cq_ref[...]
        cka[...], ckb[...] = _spread_heads(cq)
        lse_ref[...] = jnp.zeros((8, S), F32)
        row_t = lax.broadcasted_iota(jnp.int32, (128, T), 0) < HEAD_DIM
        zba, zbb = _score_bounds(q, k_ref[...])

        def prep(c, carry):
            c0 = pl.multiple_of(c * T, T)
            vt = v_ref[pl.ds(c0, T), :].T
            vta[:, pl.ds(c0, T)] = jnp.where(row_t, vt, 1.0).astype(BF16)
            vtb[:, pl.ds(c0, T)] = jnp.where(row_t, 1.0, vt).astype(BF16)
            return carry

        lax.fori_loop(0, nq, prep, 0)
        causal = (lax.broadcasted_iota(jnp.int32, (T, T), 0) <= lax.broadcasted_iota(jnp.int32, (T, T), 1))

        heads = ((qa, vta, cka, ma, acca), (qb, vtb, ckb, mb, accb))

        def kv(js, r0, masked):
            cr = cr_ref[:, pl.ds(r0, T)]
            c0s = [pl.multiple_of(j * T, T) for j in js]
            ks = [k_ref[pl.ds(c0, T), :] for c0 in c0s]
            ss = []
            for h, (qr, _, ckr, _, _) in enumerate(heads):
                qh = qr[pl.ds(r0, T), :]
                row = []
                for k, c0 in zip(ks, c0s):
                    s = _dot_nt(k, qh) - jnp.tile(ckr[pl.ds(c0, T), :], (1, T // 128))
                    row.append(jnp.where(causal, s, NEG) if masked else s)
                ss.append(row)
            ms = []
            for h, (row, (_, _, _, mr, _)) in enumerate(zip(ss, heads)):
                top = row[0]
                for s in row[1:]:
                    top = jnp.maximum(top, s)
                m_old = mr[0:1, :]
                ms.append((m_old, jnp.maximum(m_old, jnp.max(top, axis=0, keepdims=True) + cr[h:h + 1, :])))
            ps = [[jnp.exp(s + (cr[h:h + 1, :] - m_new)).astype(BF16) for s in row]
                  for h, (row, (_, m_new)) in enumerate(zip(ss, ms))]
            pvs = []
            for row, (_, vr, _, _, _) in zip(ps, heads):
                pv = _dot(vr[:, pl.ds(c0s[0], T)], row[0])
                for p, c0 in zip(row[1:], c0s[1:]):
                    pv = pv + _dot(vr[:, pl.ds(c0, T)], p)
                pvs.append(pv)
            for pv, (m_old, m_new), (_, _, _, mr, ar) in zip(pvs, ms, heads):
                ar[...] = jnp.exp(m_old - m_new) * ar[...] + pv
                mr[0:1, :] = m_new

        def qblk(i, carry):
            r0 = pl.multiple_of(i * T, T)
            ma[...] = jnp.full((8, T), NEG, F32)
            mb[...] = jnp.full((8, T), NEG, F32)
            acca[...] = jnp.zeros((128, T), F32)
            accb[...] = jnp.zeros((128, T), F32)
            kv([i], r0, True)
            done = _fox_tiles_back(cr_ref, i, r0, zba, zbb)
            _for_tiles_back(i, done, lambda js: kv(js, r0, False), fours=True)
            aa = acca[...]
            ab = accb[...]
            la = aa[64:65, :]
            lb = ab[0:1, :]
            o_ref[pl.ds(r0, T), :] = jnp.where(row_t, aa / la, ab / lb).T
            lse_ref[0:1, pl.ds(r0, T)] = ma[0:1, :] + jnp.log(la)
            lse_ref[1:2, pl.ds(r0, T)] = mb[0:1, :] + jnp.log(lb)
            lse_ref[2:3, pl.ds(r0, T)] = jnp.broadcast_to(done.astype(F32), (1, T))
            return carry

        lax.fori_loop(0, nq, qblk, 0)
        if ride is not None:
            @pl.when(pl.program_id(0) == n_pairs - 1)
            def _():
                _wait_exchange("gather", *xrefs)

    extra = () if ride is None else tuple(ride)
    return pl.pallas_call(
        body, name="fox_fwd" if ride is None else "fox_fwd_gather",
        grid=(n_pairs,),
        in_specs=[_pair_blk(S), _pair_blk(S), _pair_blk(S, OFF_FV // 128), _pair_blk(S), _pair_rows(S)]
        + [_ANY] * len(extra),
        out_specs=[_pair_blk(S), _pair_rows(S)] + [_ANY] * len(extra),
        out_shape=[jax.ShapeDtypeStruct((S, FOX_W), F32), jax.ShapeDtypeStruct((n_pairs, 8, S), F32)]
        + (_exchange_out_shapes("gather", *extra) if extra else []),
        scratch_shapes=[pltpu.VMEM((S, 128), BF16)] * 2 + [pltpu.VMEM((128, S), BF16)] * 2
        + [pltpu.VMEM((S, 128), F32)] * 2 + [pltpu.VMEM((8, T), F32)] * 2 + [pltpu.VMEM((128, T), F32)] * 2
        + (_EXCHANGE_SEMS if extra else []),
        compiler_params=_cp(("arbitrary",), _VMEM_BIG),
    )(qs, kn, proj, cqb, crow4, *extra)


def _softplus_parts(z):
    e = jnp.exp(-jnp.abs(z))
    return e, jnp.maximum(z, 0.0) + jnp.log(1.0 + e)


def _sb_fwd(psb, triu):
    S = psb.shape[0]
    T = triu.shape[0]
    nq = S // T

    n_pairs = SB_W // 128
    H = 2 * n_pairs

    def body(q_ref, k_ref, v_ref, tri_ref, o_ref, lt_ref, qm, vt, rr, acc):
        lane_s = _head_masks(S)
        zbs = []
        for p in range(n_pairs):
            q = (q_ref[:, 128 * p:128 * (p + 1)].astype(F32) * Q_SCALE).astype(BF16)
            zq = jnp.zeros_like(q)
            qm[2 * p] = jnp.where(lane_s, q, zq)
            qm[2 * p + 1] = jnp.where(lane_s, zq, q)
            zbs += list(_score_bounds(q, k_ref[:, 128 * p:128 * (p + 1)]))
        lt_ref[...] = jnp.zeros((n_pairs, 8, S), F32)
        row_t = lax.broadcasted_iota(jnp.int32, (128, T), 0) < HEAD_DIM

        def prep(c, carry):
            c0 = pl.multiple_of(c * T, T)
            for p in range(n_pairs):
                vt[p, :, pl.ds(c0, T)] = v_ref[pl.ds(c0, T), 128 * p:128 * (p + 1)].astype(F32).T.astype(BF16)
            return carry

        lax.fori_loop(0, nq, prep, 0)
        strict = (lax.broadcasted_iota(jnp.int32, (T, T), 0) < lax.broadcasted_iota(jnp.int32, (T, T), 1))

        def kv(tiles, r0):
            tri = tri_ref[...]
            c0s = [pl.multiple_of(j * T, T) for j, _ in tiles]
            zs = [[_dot_nt(k_ref[pl.ds(c0, T), 128 * (h // 2):128 * (h // 2 + 1)], qm[h, pl.ds(r0, T), :])
                   for c0 in c0s] for h in range(H)]
            lbs = [[jnp.where(strict, -_softplus_parts(z)[1], 0.0) if masked else -_softplus_parts(z)[1]
                    for z, (_, masked) in zip(row, tiles)] for row in zs]
            incs = [[_mm2(lb, tri, left=True) for lb in row] for row in lbs]
            avs = []
            for h in range(H):
                r = rr[h, 0:1, :]
                av = None
                for z, inc, c0, (_, masked) in zip(zs[h], incs[h], c0s, tiles):
                    a = jnp.exp(z + inc + r)
                    if masked:
                        a = jnp.where(strict, a, 0.0)
                    term = _dot(vt[h // 2, :, pl.ds(c0, T)], a.astype(BF16))
                    av = term if av is None else av + term
                    r = r + inc[0:1, :]
                avs.append((av, r))
            for h, (av, r) in enumerate(avs):
                rr[h, 0:1, :] = r
                acc[h] = acc[h] + av

        def qblk(i, carry):
            r0 = pl.multiple_of(i * T, T)
            rr[...] = jnp.zeros((H, 8, T), F32)
            acc[...] = jnp.zeros((H, 128, T), F32)

            @pl.when(i == 0)
            def _():
                kv([(i, True)], r0)

            @pl.when(i > 0)
            def _():
                kv([(i, True), (i - 1, False)], r0)

            def alive():
                m = jnp.max(rr[0, 0:1, :]) + zbs[0]
                for h in range(1, H):
                    m = jnp.maximum(m, jnp.max(rr[h, 0:1, :]) + zbs[h])
                return m > -_EXP_ZERO

            def cond(st):
                return (st[0] < i) & st[1]

            def step(st):
                kv([(i - 1 - st[0], False)], r0)
                return st[0] + 1, alive()

            done, _ = lax.while_loop(cond, step, (jnp.minimum(i, 1), alive()))
            for p in range(n_pairs):
                o_ref[pl.ds(r0, T), 128 * p:128 * (p + 1)] = jnp.where(row_t, acc[2 * p], acc[2 * p + 1]).T
                lt_ref[p, 0:1, pl.ds(r0, T)] = rr[2 * p, 0:1, :]
                lt_ref[p, 1:2, pl.ds(r0, T)] = rr[2 * p + 1, 0:1, :]
                lt_ref[p, 2:3, pl.ds(r0, T)] = jnp.broadcast_to(done.astype(F32), (1, T))
            return carry

        lax.fori_loop(0, nq, qblk, 0)

    wide = lambda off: pl.BlockSpec((S, SB_W), lambda g: (0, off), pipeline_mode=pl.Buffered(1))
    return pl.pallas_call(
        body, name="sb_fwd",
        grid=(1,),
        in_specs=[wide(0), wide(1), wide(2), pl.BlockSpec((T, T), lambda g: (0, 0))],
        out_specs=[wide(0), pl.BlockSpec((n_pairs, 8, S), lambda g: (0, 0, 0), pipeline_mode=pl.Buffered(1))],
        out_shape=[jax.ShapeDtypeStruct((S, SB_W), F32), jax.ShapeDtypeStruct((n_pairs, 8, S), F32)],
        scratch_shapes=[pltpu.VMEM((H, S, 128), BF16), pltpu.VMEM((n_pairs, 128, S), BF16),
                        pltpu.VMEM((H, 8, T), F32), pltpu.VMEM((H, 128, T), F32)],
        compiler_params=_cp(("arbitrary",), _VMEM_BIG),
    )(psb, psb, psb, triu)


def _pool_window_lanes(shape):
    lane = lax.broadcasted_iota(jnp.int32, shape, 1)
    return jnp.where(lane < 64, 2, jnp.where(lane < 128, 4, jnp.where(lane < 192, 8, 16)))


def _pool_fwd(proj):
    S = proj.shape[0]

    def body(x_ref, o_ref):
        x = x_ref[...]
        t = lax.broadcasted_iota(jnp.int32, x.shape, 0)
        lane = lax.broadcasted_iota(jnp.int32, x.shape, 1)

        def back(a, k):
            return jnp.where(t >= k, pltpu.roll(a, k, 0), 0.0)

        s1 = x + back(x, 1)
        s2 = s1 + back(s1, 2)
        s4 = s2 + back(s2, 4)
        s8 = s4 + back(s4, 8)
        win = jnp.where(lane < 64, s1, jnp.where(lane < 128, s2, jnp.where(lane < 192, s4, s8)))
        cnt = jnp.minimum(t + 1, _pool_window_lanes(x.shape)).astype(F32)
        o_ref[...] = win / cnt - x

    return pl.pallas_call(
        body, name="pool_fwd",
        grid=(1,),
        in_specs=[pl.BlockSpec((S, POOL_W), lambda i: (0, OFF_PX // POOL_W))],
        out_specs=pl.BlockSpec((S, POOL_W), lambda i: (0, 0)),
        out_shape=jax.ShapeDtypeStruct((S, POOL_W), F32),
        compiler_params=_cp(("arbitrary",), _VMEM_BIG),
    )(proj)


def _silu(g):
    return g * _sigmoid(g)


def _mix_out(fo, so, pooled, proj, wbd, scale, wout, x):
    S, D = x.shape
    tm = min(_TM_ROWS, S)

    def body(fo_ref, fg_ref, so_ref, sg_ref, pl_ref, pg_ref, wbd_ref, sc_ref, w_ref, x_ref, y_ref, mxt_ref, mx_ref):
        parts = ((0, fo_ref[...] * _silu(fg_ref[...])),
                 (FOX_W, (_dot(pl_ref[...].astype(BF16), wbd_ref[...]) * sc_ref[...]) * _silu(pg_ref[...])),
                 (FOX_W + POOL_W, so_ref[...] * _silu(sg_ref[...])))
        for off, part in parts:
            w = part.shape[1]
            mx_ref[:, off:off + w] = part.astype(BF16)
            mxt_ref[off:off + w, :] = part.T.astype(BF16)
        y_ref[...] = x_ref[...] + _dot(mx_ref[...], w_ref[...])

    return pl.pallas_call(
        body, name="mix_out",
        grid=(S // tm,),
        in_specs=[pl.BlockSpec((tm, FOX_W), lambda i: (i, 0)),
                  pl.BlockSpec((tm, FOX_W), lambda i: (i, OFF_FG // FOX_W)),
                  pl.BlockSpec((tm, SB_W), lambda i: (i, 0)),
                  pl.BlockSpec((tm, SB_W), lambda i: (i, OFF_SG // SB_W)),
                  pl.BlockSpec((tm, POOL_W), lambda i: (i, 0)),
                  pl.BlockSpec((tm, POOL_W), lambda i: (i, OFF_PG // POOL_W)),
                  pl.BlockSpec((POOL_W, POOL_W), lambda i: (0, 0)),
                  pl.BlockSpec((1, POOL_W), lambda i: (0, 0)),
                  pl.BlockSpec((D_MIX, D), lambda i: (0, 0)),
                  pl.BlockSpec((tm, D), lambda i: (i, 0))],
        out_specs=[pl.BlockSpec((tm, D), lambda i: (i, 0)), pl.BlockSpec((D_MIX, tm), lambda i: (0, i))],
        out_shape=[jax.ShapeDtypeStruct((S, D), F32), jax.ShapeDtypeStruct((D_MIX, S), BF16)],
        scratch_shapes=[pltpu.VMEM((tm, D_MIX), BF16)],
        compiler_params=_cp(("parallel",), _VMEM_MID),
    )(fo, proj, so, proj, pooled, proj, wbd, scale, wout, x)


def _loss_head(y, target):
    S, D = y.shape
    tm = min(_TM, S)

    def body(y_ref, t_ref, dy_ref, ls_ref):
        @pl.when(pl.program_id(0) == 0)
        def _():
            ls_ref[...] = jnp.zeros_like(ls_ref)

        e = y_ref[...] - t_ref[...]
        dy_ref[...] = e * (1.0 / D)
        ls_ref[...] = ls_ref[...] + jnp.sum(e * e) * (0.5 / D)

    dy, ls = pl.pallas_call(
        body, name="loss_head",
        grid=(S // tm,),
        in_specs=[pl.BlockSpec((tm, D), lambda i: (i, 0)), pl.BlockSpec((tm, D), lambda i: (i, 0))],
        out_specs=[pl.BlockSpec((tm, D), lambda i: (i, 0)), pl.BlockSpec((8, 128), lambda i: (0, 0))],
        out_shape=[jax.ShapeDtypeStruct((S, D), F32), jax.ShapeDtypeStruct((8, 128), F32)],
        compiler_params=_cp(("arbitrary",), _VMEM_MID),
    )(y, target)
    return dy, ls[0, 0]


def _dsilu(g):
    s = _sigmoid(g)
    return s * (1.0 + g * (1.0 - s))


def _gate_bwd(dy, wout, fo, so, pooled, proj, wbd, scale):
    S, D = dy.shape
    tm = min(_TM_ROWS, S)

    def body(dy_ref, w_ref, fo_ref, fg_ref, so_ref, sg_ref, pl_ref, pg_ref, wbd_ref, sc_ref,
             dfo_ref, dfg_ref, dso_ref, dsg_ref, dpg_ref, dpl_ref, dsc_ref, dwbd_ref):
        @pl.when(pl.program_id(0) == 0)
        def _():
            dsc_ref[...] = jnp.zeros_like(dsc_ref)
            dwbd_ref[...] = jnp.zeros_like(dwbd_ref)

        dm = _dot_nt(dy_ref[...].astype(BF16), w_ref[...])
        dmf = dm[:, 0:FOX_W]
        dmp = dm[:, FOX_W:FOX_W + POOL_W]
        dms = dm[:, FOX_W + POOL_W:D_MIX]
        fg = fg_ref[...]
        dfo_ref[...] = dmf * _silu(fg)
        dfg_ref[...] = (dmf * fo_ref[...] * _dsilu(fg)).astype(BF16)
        sg = sg_ref[...]
        dso_ref[...] = (dms * _silu(sg)).astype(BF16)
        dsg_ref[...] = (dms * so_ref[...] * _dsilu(sg)).astype(BF16)
        pg = pg_ref[...]
        plb = pl_ref[...].astype(BF16)
        yw = _dot(plb, wbd_ref[...])
        sc = sc_ref[...]
        dpg_ref[...] = (dmp * (yw * sc) * _dsilu(pg)).astype(BF16)
        dys = dmp * _silu(pg)
        dsc_ref[...] = dsc_ref[...] + jnp.sum(dys * yw, axis=0, keepdims=True)
        dyw = (dys * sc).astype(BF16)
        dpl_ref[...] = _dot_nt(dyw, wbd_ref[...])
        dwbd_ref[...] = dwbd_ref[...] + _dot_tn(plb, dyw)

    return pl.pallas_call(
        body, name="gate_bwd",
        grid=(S // tm,),
        in_specs=[pl.BlockSpec((tm, D), lambda i: (i, 0)),
                  pl.BlockSpec((D_MIX, D), lambda i: (0, 0)),
                  pl.BlockSpec((tm, FOX_W), lambda i: (i, 0)),
                  pl.BlockSpec((tm, FOX_W), lambda i: (i, OFF_FG // FOX_W)),
                  pl.BlockSpec((tm, SB_W), lambda i: (i, 0)),
                  pl.BlockSpec((tm, SB_W), lambda i: (i, OFF_SG // SB_W)),
                  pl.BlockSpec((tm, POOL_W), lambda i: (i, 0)),
                  pl.BlockSpec((tm, POOL_W), lambda i: (i, OFF_PG // POOL_W)),
                  pl.BlockSpec((POOL_W, POOL_W), lambda i: (0, 0)),
                  pl.BlockSpec((1, POOL_W), lambda i: (0, 0))],
        out_specs=[pl.BlockSpec((tm, FOX_W), lambda i: (i, 0)),
                   pl.BlockSpec((tm, FOX_W), lambda i: (i, 0)),
                   pl.BlockSpec((tm, SB_W), lambda i: (i, 0)),
                   pl.BlockSpec((tm, SB_W), lambda i: (i, 0)),
                   pl.BlockSpec((tm, POOL_W), lambda i: (i, 0)),
                   pl.BlockSpec((tm, POOL_W), lambda i: (i, 0)),
                   pl.BlockSpec((1, POOL_W), lambda i: (0, 0)),
                   pl.BlockSpec((POOL_W, POOL_W), lambda i: (0, 0))],
        out_shape=[jax.ShapeDtypeStruct((S, FOX_W), F32), jax.ShapeDtypeStruct((S, FOX_W), BF16),
                   jax.ShapeDtypeStruct((S, SB_W), BF16), jax.ShapeDtypeStruct((S, SB_W), BF16),
                   jax.ShapeDtypeStruct((S, POOL_W), BF16), jax.ShapeDtypeStruct((S, POOL_W), F32),
                   jax.ShapeDtypeStruct((1, POOL_W), F32), jax.ShapeDtypeStruct((POOL_W, POOL_W), F32)],
        compiler_params=_cp(("arbitrary",), _VMEM_MID),
    )(dy, wout, fo, proj, so, proj, pooled, proj, wbd, scale)


def _matmul_acc(at, b, name):
    M, S = at.shape
    N = b.shape[1]
    tk = min(_TK_DW, S)
    tn = min(512, N)
    nk = S // tk

    def body(a_ref, b_ref, o_ref, acc):
        k = pl.program_id(1)

        @pl.when(k == 0)
        def _():
            acc[...] = jnp.zeros_like(acc)

        acc[...] = acc[...] + _dot(a_ref[...], b_ref[...].astype(BF16))

        @pl.when(k == nk - 1)
        def _():
            o_ref[...] = acc[...].astype(BF16)

    return pl.pallas_call(
        body, name=name,
        grid=(N // tn, nk),
        in_specs=[pl.BlockSpec((M, tk), lambda j, k: (0, k)), pl.BlockSpec((tk, tn), lambda j, k: (k, j))],
        out_specs=pl.BlockSpec((M, tn), lambda j, k: (0, j)),
        out_shape=jax.ShapeDtypeStruct((M, N), BF16),
        scratch_shapes=[pltpu.VMEM((M, tn), F32)],
        compiler_params=_cp(("parallel", "arbitrary"), _VMEM_MID),
    )(at, b)


def _pool_bwd(dpooled):
    S = dpooled.shape[0]

    def body(d_ref, o_ref):
        d = d_ref[...]
        t = lax.broadcasted_iota(jnp.int32, d.shape, 0)
        lane = lax.broadcasted_iota(jnp.int32, d.shape, 1)
        cnt = jnp.minimum(t + 1, _pool_window_lanes(d.shape)).astype(F32)
        u = d / cnt

        def fwd(a, k):
            return jnp.where(t < S - k, pltpu.roll(a, S - k, 0), 0.0)

        s1 = u + fwd(u, 1)
        s2 = s1 + fwd(s1, 2)
        s4 = s2 + fwd(s2, 4)
        s8 = s4 + fwd(s4, 8)
        win = jnp.where(lane < 64, s1, jnp.where(lane < 128, s2, jnp.where(lane < 192, s4, s8)))
        o_ref[...] = (win - d).astype(BF16)

    return pl.pallas_call(
        body, name="pool_bwd",
        grid=(1,),
        in_specs=[pl.BlockSpec((S, POOL_W), lambda i: (0, 0))],
        out_specs=pl.BlockSpec((S, POOL_W), lambda i: (0, 0)),
        out_shape=jax.ShapeDtypeStruct((S, POOL_W), BF16),
        compiler_params=_cp(("arbitrary",), _VMEM_BIG),
    )(dpooled)


def _fox_bwd(qs, kn, proj, dfo, fo, lse, cqb, crow4, ride=None):
    S = qs.shape[0]
    T = min(_T, S)
    nq = S // T
    n_pairs = FOX_W // 128

    def body(*refs):
        if ride is None:
            q_ref, k_ref, v_ref, do_ref, o_ref, lse_ref, cq_ref, cr_ref = refs[:8]
            dq_ref, dk_ref, dv_ref, dck_ref, dcq_ref = refs[8:13]
            scr = refs[13:]
        else:
            q_ref, k_ref, v_ref, do_ref, o_ref, lse_ref, cq_ref, cr_ref, pa_ref, pb_ref = refs[:10]
            dq_ref, dk_ref, dv_ref, dck_ref, dcq_ref, ra_ref, rb_ref = refs[10:17]
            scr = refs[17:32]
            xrefs = (pa_ref, pb_ref, ra_ref, rb_ref) + tuple(refs[32:])

            @pl.when(pl.program_id(0) == 0)
            def _():
                _start_exchange("scatter", *xrefs)

        qa, qb, kta, ktb, vb, doa, dob, cka, ckb, dcka, dckb, dva, dqt, dcqa, dcqb = scr
        lane_s = _head_masks(S)
        q = q_ref[...]
        zq = jnp.zeros_like(q)
        qa[...] = jnp.where(lane_s, q, zq)
        qb[...] = jnp.where(lane_s, zq, q)
        vb[...] = v_ref[...].astype(BF16)
        do = do_ref[...].astype(BF16)
        doa[...] = jnp.where(lane_s, do, zq)
        dob[...] = jnp.where(lane_s, zq, do)
        cq = cq_ref[...]
        cka[...], ckb[...] = _spread_heads(cq)
        zs = jnp.zeros((S, 128), F32)
        dk_ref[...] = zs
        dva[...] = zs
        dcka[...] = zs
        dckb[...] = zs
        dcq_ref[...] = jnp.zeros((8, S), F32)
        row_t = lax.broadcasted_iota(jnp.int32, (128, T), 0) < HEAD_DIM

        def prep(c, carry):
            c0 = pl.multiple_of(c * T, T)
            kt = k_ref[pl.ds(c0, T), :].astype(F32).T
            kta[:, pl.ds(c0, T)] = jnp.where(row_t, kt, 0.0).astype(BF16)
            ktb[:, pl.ds(c0, T)] = jnp.where(row_t, 0.0, kt).astype(BF16)
            return carry

        lax.fori_loop(0, nq, prep, 0)
        causal = (lax.broadcasted_iota(jnp.int32, (T, T), 0) <= lax.broadcasted_iota(jnp.int32, (T, T), 1))

        heads = ((qa, kta, doa, cka, dcka, dcqa), (qb, ktb, dob, ckb, dckb, dcqb))

        def kv(js, r0, lss, dls, masked):
            cr = cr_ref[:, pl.ds(r0, T)]
            c0s = [pl.multiple_of(j * T, T) for j in js]
            ks = [k_ref[pl.ds(c0, T), :] for c0 in c0s]
            vs = [vb[pl.ds(c0, T), :] for c0 in c0s]
            qhs = [hd[0][pl.ds(r0, T), :] for hd in heads]
            dohs = [hd[2][pl.ds(r0, T), :] for hd in heads]
            ss = []
            for h, hd in enumerate(heads):
                row = []
                for k, c0 in zip(ks, c0s):
                    s = _dot_nt(k, qhs[h]) - jnp.tile(hd[3][pl.ds(c0, T), :], (1, T // 128))
                    row.append(jnp.where(causal, s, NEG) if masked else s)
                ss.append(row)
            ps = [[jnp.exp(s + (cr[h:h + 1, :] - lss[h])) for s in row] for h, row in enumerate(ss)]
            dps = [[_dot_nt(v, dohs[h]) for v in vs] for h in range(2)]
            dss = [[p * (dp - dls[h]) for p, dp in zip(ps[h], dps[h])] for h in range(2)]
            pbs = [[p.astype(BF16) for p in row] for row in ps]
            dsbs = [[ds.astype(BF16) for ds in row] for row in dss]
            for t, c0 in enumerate(c0s):
                dva[pl.ds(c0, T), :] = dva[pl.ds(c0, T), :] + (_dot(pbs[0][t], dohs[0]) + _dot(pbs[1][t], dohs[1]))
                dk_ref[pl.ds(c0, T), :] = dk_ref[pl.ds(c0, T), :] + (_dot(dsbs[0][t], qhs[0]) + _dot(dsbs[1][t], qhs[1]))
            dq = None
            for h, hd in enumerate(heads):
                for t, c0 in enumerate(c0s):
                    term = _dot(hd[1][:, pl.ds(c0, T)], dsbs[h][t])
                    dq = term if dq is None else dq + term
            dqt[...] = dqt[...] + dq
            for h, hd in enumerate(heads):
                col = jnp.sum(dss[h][0], axis=0, keepdims=True)
                for ds in dss[h][1:]:
                    col = col + jnp.sum(ds, axis=0, keepdims=True)
                hd[5][0:1, :] = hd[5][0:1, :] + col
                for ds, c0 in zip(dss[h], c0s):
                    fold = ds[:, 0:128]
                    for u in range(1, T // 128):
                        fold = fold + ds[:, 128 * u:128 * (u + 1)]
                    hd[4][pl.ds(c0, T), :] = hd[4][pl.ds(c0, T), :] - fold

        def qblk(i, carry):
            r0 = pl.multiple_of(i * T, T)
            dt = (do_ref[pl.ds(r0, T), :] * o_ref[pl.ds(r0, T), :]).T
            dla = jnp.sum(jnp.where(row_t, dt, 0.0), axis=0, keepdims=True)
            dlb = jnp.sum(jnp.where(row_t, 0.0, dt), axis=0, keepdims=True)
            ls = lse_ref[:, pl.ds(r0, T)]
            lss = (ls[0:1, :], ls[1:2, :])
            back = jnp.max(ls[2:3, :]).astype(jnp.int32)
            dqt[...] = jnp.zeros((128, T), F32)
            dcqa[...] = jnp.zeros((8, T), F32)
            dcqb[...] = jnp.zeros((8, T), F32)
            kv([i], r0, lss, (dla, dlb), True)
            _for_tiles_back(i, back, lambda js: kv(js, r0, lss, (dla, dlb), False), fours=True)
            dq_ref[pl.ds(r0, T), :] = dqt[...].T
            dcq_ref[0:1, pl.ds(r0, T)] = dcqa[0:1, :]
            dcq_ref[1:2, pl.ds(r0, T)] = dcqb[0:1, :]
            return carry

        lax.fori_loop(0, nq, qblk, 0)
        dv_ref[...] = dva[...].astype(BF16)
        dck_ref[...] = jnp.where(lane_s, jnp.sum(dcka[...], axis=1, keepdims=True),
                                 jnp.sum(dckb[...], axis=1, keepdims=True))
        if ride is not None:
            @pl.when(pl.program_id(0) == n_pairs - 1)
            def _():
                _wait_exchange("scatter", *xrefs)

    extra = () if ride is None else tuple(ride)
    return pl.pallas_call(
        body, name="fox_bwd" if ride is None else "fox_bwd_exchange",
        grid=(n_pairs,),
        in_specs=[_pair_blk(S), _pair_blk(S), _pair_blk(S, OFF_FV // 128), _pair_blk(S), _pair_blk(S),
                  _pair_rows(S), _pair_blk(S), _pair_rows(S)] + [_ANY] * len(extra),
        out_specs=[_pair_blk(S), _pair_blk(S), _pair_blk(S), _pair_blk(S), _pair_rows(S)] + [_ANY] * len(extra),
        out_shape=[jax.ShapeDtypeStruct((S, FOX_W), F32), jax.ShapeDtypeStruct((S, FOX_W), F32),
                   jax.ShapeDtypeStruct((S, FOX_W), BF16), jax.ShapeDtypeStruct((S, FOX_W), F32),
                   jax.ShapeDtypeStruct((n_pairs, 8, S), F32)]
        + (_exchange_out_shapes("scatter", *extra) if extra else []),
        scratch_shapes=[pltpu.VMEM((S, 128), BF16)] * 2 + [pltpu.VMEM((128, S), BF16)] * 2
        + [pltpu.VMEM((S, 128), BF16)] * 3 + [pltpu.VMEM((S, 128), F32)] * 5
        + [pltpu.VMEM((128, T), F32)] + [pltpu.VMEM((8, T), F32)] * 2
        + (_EXCHANGE_SEMS if extra else []),
        compiler_params=_cp(("arbitrary",), _VMEM_BIG),
    )(qs, kn, proj, dfo, fo, lse, cqb, crow4, *extra)


def _sb_bwd(psb, dso, ltot, tril):
    S = psb.shape[0]
    T = tril.shape[0]
    nq = S // T
    n_pairs = SB_W // 128
    H = 2 * n_pairs

    def body(q_ref, k_ref, v_ref, do_ref, lt_ref, tri_ref, dq_ref, dk_ref, dv_ref,
             qm, kt, dka, dva, dqt, rr, gg):
        lane_s = _head_masks(S)
        for p in range(n_pairs):
            q = (q_ref[:, 128 * p:128 * (p + 1)].astype(F32) * Q_SCALE).astype(BF16)
            zq = jnp.zeros_like(q)
            qm[2 * p] = jnp.where(lane_s, q, zq)
            qm[2 * p + 1] = jnp.where(lane_s, zq, q)
        dka[...] = jnp.zeros((n_pairs, S, 128), F32)
        dva[...] = jnp.zeros((n_pairs, S, 128), F32)
        row_t = lax.broadcasted_iota(jnp.int32, (128, T), 0) < HEAD_DIM
        lane_t = lax.broadcasted_iota(jnp.int32, (T, 128), 1) < HEAD_DIM

        def prep(c, carry):
            c0 = pl.multiple_of(c * T, T)
            for p in range(n_pairs):
                kt[p, :, pl.ds(c0, T)] = k_ref[pl.ds(c0, T), 128 * p:128 * (p + 1)].astype(F32).T.astype(BF16)
            return carry

        lax.fori_loop(0, nq, prep, 0)
        strict = (lax.broadcasted_iota(jnp.int32, (T, T), 0) < lax.broadcasted_iota(jnp.int32, (T, T), 1))

        def own(x, h, mask):
            z = jnp.zeros_like(x)
            return jnp.where(mask, x, z) if h % 2 == 0 else jnp.where(mask, z, x)

        def pair(ref, p, c0):
            return ref[pl.ds(c0, T), 128 * p:128 * (p + 1)]

        def kv(tiles, r0, lts):
            tri = tri_ref[...]
            c0s = [pl.multiple_of(j * T, T) for j, _ in tiles]
            qhs = [qm[h, pl.ds(r0, T), :] for h in range(H)]
            dohs = [own(pair(do_ref, h // 2, r0), h, lane_t) for h in range(H)]
            zs = [[_dot_nt(pair(k_ref, h // 2, c0), qhs[h]) for c0 in c0s] for h in range(H)]
            das = [[_dot_nt(pair(v_ref, h // 2, c0), dohs[h]) for c0 in c0s] for h in range(H)]
            es, lbs = [], []
            for row in zs:
                erow, lrow = [], []
                for z, (_, masked) in zip(row, tiles):
                    e, sp = _softplus_parts(z)
                    erow.append(e)
                    lrow.append(jnp.where(strict, -sp, 0.0) if masked else -sp)
                es.append(erow)
                lbs.append(lrow)
            pres = [[_mm2(lb, tri, left=True) for lb in row] for row in lbs]
            aas, r_ends = [], []
            for h in range(H):
                r = rr[h, 0:1, :]
                arow = []
                for z, lb, pre, (_, masked) in zip(zs[h], lbs[h], pres[h], tiles):
                    a = jnp.exp(z + lb + ((lts[h] - r) - pre))
                    arow.append(jnp.where(strict, a, 0.0) if masked else a)
                    r = r + pre[T - 1:T, :]
                aas.append(arow)
                r_ends.append(r)
            gs = [[a * da for a, da in zip(arow, drow)] for arow, drow in zip(aas, das)]
            gpres = [[_mm2(g, tri, left=True) for g in row] for row in gs]
            dzbs, g_ends = [], []
            for h in range(H):
                gc = gg[h, 0:1, :]
                drow = []
                for z, e, g, gpre, (_, masked) in zip(zs[h], es[h], gs[h], gpres[h], tiles):
                    inv = 1.0 / (1.0 + e)
                    pos = z >= 0.0
                    sig = jnp.where(pos, 1.0, e) * inv
                    oms = jnp.where(pos, e, 1.0) * inv
                    dz = g * oms - sig * (gc + (gpre - g))
                    if masked:
                        dz = jnp.where(strict, dz, 0.0)
                    drow.append(dz.astype(BF16))
                    gc = gc + gpre[T - 1:T, :]
                dzbs.append(drow)
                g_ends.append(gc)
            for p in range(n_pairs):
                a, b = 2 * p, 2 * p + 1
                dq = None
                for h in (a, b):
                    for t, c0 in enumerate(c0s):
                        term = _dot(own(kt[p, :, pl.ds(c0, T)], h, row_t), dzbs[h][t])
                        dq = term if dq is None else dq + term
                dqt[p] = dqt[p] + dq
                for t, c0 in enumerate(c0s):
                    dka[p, pl.ds(c0, T), :] = dka[p, pl.ds(c0, T), :] + (_dot(dzbs[a][t], qhs[a]) + _dot(dzbs[b][t], qhs[b]))
                    dva[p, pl.ds(c0, T), :] = dva[p, pl.ds(c0, T), :] + (_dot(aas[a][t].astype(BF16), dohs[a])
                                                                      + _dot(aas[b][t].astype(BF16), dohs[b]))
            for h in range(H):
                rr[h, 0:1, :] = r_ends[h]
                gg[h, 0:1, :] = g_ends[h]

        def qblk(i, carry):
            r0 = pl.multiple_of(i * T, T)
            lts = []
            for p in range(n_pairs):
                lt = lt_ref[p, :, pl.ds(r0, T)]
                lts += [lt[0:1, :], lt[1:2, :]]
            back = jnp.max(lt_ref[0, 2:3, pl.ds(r0, T)]).astype(jnp.int32)
            dqt[...] = jnp.zeros((n_pairs, 128, T), F32)
            rr[...] = jnp.zeros((H, 8, T), F32)
            gg[...] = jnp.zeros((H, 8, T), F32)

            def inner(j, c):
                kv([(j, False)], r0, lts)
                return c

            @pl.when(back == 0)
            def _():
                kv([(i, True)], r0, lts)

            @pl.when(back > 0)
            def _():
                lax.fori_loop(i - back, i - 1, inner, 0)
                kv([(i - 1, False), (i, True)], r0, lts)

            for p in range(n_pairs):
                dq_ref[pl.ds(r0, T), 128 * p:128 * (p + 1)] = (dqt[p] * Q_SCALE).T.astype(BF16)
            return carry

        lax.fori_loop(0, nq, qblk, 0)
        for p in range(n_pairs):
            dk_ref[:, 128 * p:128 * (p + 1)] = dka[p].astype(BF16)
            dv_ref[:, 128 * p:128 * (p + 1)] = dva[p].astype(BF16)

    wide = lambda off: pl.BlockSpec((S, SB_W), lambda g: (0, off), pipeline_mode=pl.Buffered(1))
    return pl.pallas_call(
        body, name="sb_bwd",
        grid=(1,),
        in_specs=[wide(0), wide(1), wide(2), wide(0),
                  pl.BlockSpec((n_pairs, 8, S), lambda g: (0, 0, 0), pipeline_mode=pl.Buffered(1)),
                  pl.BlockSpec((T, T), lambda g: (0, 0))],
        out_specs=[wide(0), wide(0), wide(0)],
        out_shape=[jax.ShapeDtypeStruct((S, SB_W), BF16)] * 3,
        scratch_shapes=[pltpu.VMEM((H, S, 128), BF16), pltpu.VMEM((n_pairs, 128, S), BF16),
                        pltpu.VMEM((n_pairs, S, 128), F32), pltpu.VMEM((n_pairs, S, 128), F32),
                        pltpu.VMEM((n_pairs, 128, T), F32), pltpu.VMEM((H, 8, T), F32), pltpu.VMEM((H, 8, T), F32)],
        compiler_params=_cp(("arbitrary",), _VMEM_BIG),
    )(psb, psb, psb, dso, ltot, tril)


def _head_norm_bwd(x, g, dy, bd):
    ss = _mm2(x * x, bd)
    r = lax.rsqrt(ss * (1.0 / HEAD_DIM) + EPS)
    xr = x * r
    gdy = g * dy
    m = _mm2(xr * gdy, bd) * (1.0 / HEAD_DIM)
    return r * (gdy - xr * m), dy * xr


def _qk_bwd(dqs, dkn, proj, pff, bfp, gq, gk, bd, dccol, triu):
    S = proj.shape[0]
    T = triu.shape[0]
    n = S // T
    rev = lambda col: (lambda i: (n - 1 - i, col))

    def body(dq_ref, dk_ref, q_ref, k_ref, ff_ref, b_ref, gq_ref, gk_ref, bd_ref, dc_ref, tri_ref,
             dfq_ref, dfk_ref, dff_ref, dgq_ref, dgk_ref, dbf_ref, carry):
        @pl.when(pl.program_id(0) == 0)
        def _():
            carry[...] = jnp.zeros_like(carry)
            dgq_ref[...] = jnp.zeros_like(dgq_ref)
            dgk_ref[...] = jnp.zeros_like(dgk_ref)
            dbf_ref[...] = jnp.zeros_like(dbf_ref)

        bdv = bd_ref[...]
        dxq, gq_rows = _head_norm_bwd(q_ref[...], gq_ref[...], dq_ref[...] * Q_SCALE, bdv)
        dfq_ref[...] = dxq.astype(BF16)
        dgq_ref[...] = dgq_ref[...] + jnp.sum(gq_rows, axis=0, keepdims=True)
        dxk, gk_rows = _head_norm_bwd(k_ref[...], gk_ref[...], dk_ref[...], bdv)
        dfk_ref[...] = dxk.astype(BF16)
        dgk_ref[...] = dgk_ref[...] + jnp.sum(gk_rows, axis=0, keepdims=True)
        dlf = _mm3(dc_ref[...], tri_ref[...], left=True) + carry[0:1, :]
        carry[0:1, :] = dlf[0:1, :]
        u = ff_ref[...] + b_ref[...]
        lane = lax.broadcasted_iota(jnp.int32, u.shape, 1)
        dff = jnp.where(lane < N_FF, dlf * _sigmoid(-u), 0.0)
        dff_ref[...] = dff.astype(BF16)
        dbf_ref[...] = dbf_ref[...] + jnp.sum(dff, axis=0, keepdims=True)

    return pl.pallas_call(
        body, name="qk_bwd",
        grid=(n,),
        in_specs=[pl.BlockSpec((T, FOX_W), rev(0)), pl.BlockSpec((T, FOX_W), rev(0)),
                  pl.BlockSpec((T, FOX_W), rev(OFF_FQ // FOX_W)), pl.BlockSpec((T, FOX_W), rev(OFF_FK // FOX_W)),
                  pl.BlockSpec((T, N_FFPAD), rev(0)),
                  pl.BlockSpec((1, N_FFPAD), lambda i: (0, 0)),
                  pl.BlockSpec((1, FOX_W), lambda i: (0, 0)), pl.BlockSpec((1, FOX_W), lambda i: (0, 0)),
                  pl.BlockSpec((FOX_W, FOX_W), lambda i: (0, 0)),
                  pl.BlockSpec((T, N_FFPAD), rev(0)),
                  pl.BlockSpec((T, T), lambda i: (0, 0))],
        out_specs=[pl.BlockSpec((T, FOX_W), rev(0)), pl.BlockSpec((T, FOX_W), rev(0)),
                   pl.BlockSpec((T, N_FFPAD), rev(0)),
                   pl.BlockSpec((1, FOX_W), lambda i: (0, 0)), pl.BlockSpec((1, FOX_W), lambda i: (0, 0)),
                   pl.BlockSpec((1, N_FFPAD), lambda i: (0, 0))],
        out_shape=[jax.ShapeDtypeStruct((S, FOX_W), BF16), jax.ShapeDtypeStruct((S, FOX_W), BF16),
                   jax.ShapeDtypeStruct((S, N_FFPAD), BF16),
                   jax.ShapeDtypeStruct((1, FOX_W), F32), jax.ShapeDtypeStruct((1, FOX_W), F32),
                   jax.ShapeDtypeStruct((1, N_FFPAD), F32)],
        scratch_shapes=[pltpu.VMEM((8, N_FFPAD), F32)],
        compiler_params=_cp(("arbitrary",), _VMEM_MID),
    )(dqs, dkn, proj, proj, pff, bfp, gq, gk, bd, dccol, triu)


def _dproj_layout(pieces):
    offs, o = [], 0
    for p in pieces:
        offs.append(o)
        o += p.shape[1]
    assert o == N_MAIN
    return offs


def _inproj_bwd_dx(pieces, dff, wm, wff, x, g, dy, ride=None):
    S, D = x.shape
    tm = min(_TM_DX, S)
    steps = S // tm
    offs = _dproj_layout(pieces)
    n = len(pieces)

    def body(*refs):
        p_refs = refs[:n]
        if ride is None:
            dff_ref, w_ref, wff_ref, x_ref, g_ref, dy_ref, dx_ref, dg_ref = refs[n:]
        else:
            dff_ref, w_ref, wff_ref, x_ref, g_ref, dy_ref, pa_ref, pb_ref = refs[n:n + 8]
            dx_ref, dg_ref, ra_ref, rb_ref = refs[n + 8:n + 12]
            xrefs = (pa_ref, pb_ref, ra_ref, rb_ref) + tuple(refs[n + 12:])

        @pl.when(pl.program_id(0) == 0)
        def _():
            dg_ref[...] = jnp.zeros_like(dg_ref)
            if ride is not None:
                _start_exchange("scatter", *xrefs)

        dh = _dot_nt(dff_ref[...], wff_ref[...])
        for p_ref, off in zip(p_refs, offs):
            dh = dh + _dot_nt(p_ref[...], w_ref[:, off:off + p_ref.shape[1]])
        xv = x_ref[...]
        r = _rms_rows(xv)
        xr = xv * r
        dg_ref[...] = dg_ref[...] + jnp.sum(dh * xr, axis=0, keepdims=True)
        gdh = g_ref[...] * dh
        m = jnp.mean(gdh * xr, axis=-1, keepdims=True)
        dx_ref[...] = dy_ref[...] + r * (gdh - xr * m)
        if ride is not None:
            @pl.when(pl.program_id(0) == steps - 1)
            def _():
                _wait_exchange("scatter", *xrefs)

    extra = () if ride is None else tuple(ride)
    return pl.pallas_call(
        body, name="inproj_bwd_dx" if ride is None else "inproj_bwd_dx_exchange",
        grid=(steps,),
        in_specs=[pl.BlockSpec((tm, p.shape[1]), lambda i: (i, 0)) for p in pieces]
        + [pl.BlockSpec((tm, N_FFPAD), lambda i: (i, 0)),
                  pl.BlockSpec((D, N_MAIN), lambda i: (0, 0)),
                  pl.BlockSpec((D, N_FFPAD), lambda i: (0, 0)),
                  pl.BlockSpec((tm, D), lambda i: (i, 0)),
                  pl.BlockSpec((1, D), lambda i: (0, 0)),
                  pl.BlockSpec((tm, D), lambda i: (i, 0))] + [_ANY] * len(extra),
        out_specs=[pl.BlockSpec((tm, D), lambda i: (i, 0)), pl.BlockSpec((1, D), lambda i: (0, 0))] + [_ANY] * len(extra),
        out_shape=[jax.ShapeDtypeStruct((S, D), F32), jax.ShapeDtypeStruct((1, D), F32)]
        + (_exchange_out_shapes("scatter", *extra) if extra else []),
        scratch_shapes=_EXCHANGE_SEMS if extra else [],
        compiler_params=_cp(("arbitrary",), _VMEM_WIDE),
    )(*pieces, dff, wm, wff, x, g, dy, *extra)


def _inproj_bwd_dw(ht, pieces, dff):
    D, S = ht.shape
    tk = min(_TK_DW, S)
    nk = S // tk
    offs = _dproj_layout(pieces)
    n = len(pieces)

    def body(*refs):
        ht_ref, p_refs, dff_ref = refs[0], refs[1:1 + n], refs[1 + n]
        dw_ref, dwff_ref, acc, accff = refs[2 + n:]
        k = pl.program_id(0)

        @pl.when(k == 0)
        def _():
            acc[...] = jnp.zeros_like(acc)
            accff[...] = jnp.zeros_like(accff)

        hb = ht_ref[...]
        for p_ref, off in zip(p_refs, offs):
            w = p_ref.shape[1]
            acc[:, off:off + w] = acc[:, off:off + w] + _dot(hb, p_ref[...])
        accff[...] = accff[...] + _dot(hb, dff_ref[...])

        @pl.when(k == nk - 1)
        def _():
            dw_ref[...] = acc[...].astype(BF16)
            dwff_ref[...] = accff[...].astype(BF16)

    return pl.pallas_call(
        body, name="inproj_bwd_dw",
        grid=(nk,),
        in_specs=[pl.BlockSpec((D, tk), lambda k: (0, k))]
        + [pl.BlockSpec((tk, p.shape[1]), lambda k: (k, 0)) for p in pieces]
        + [pl.BlockSpec((tk, N_FFPAD), lambda k: (k, 0))],
        out_specs=[pl.BlockSpec((D, N_MAIN), lambda k: (0, 0), pipeline_mode=pl.Buffered(1)),
                   pl.BlockSpec((D, N_FFPAD), lambda k: (0, 0), pipeline_mode=pl.Buffered(1))],
        out_shape=[jax.ShapeDtypeStruct((D, N_MAIN), BF16), jax.ShapeDtypeStruct((D, N_FFPAD), BF16)],
        scratch_shapes=[pltpu.VMEM((D, N_MAIN), F32), pltpu.VMEM((D, N_FFPAD), F32)],
        compiler_params=_cp(("arbitrary",), _VMEM_BIG),
    )(ht, *pieces, dff)


def _constants(T):
    tril = jnp.tril(jnp.ones((T, T), F32)).astype(BF16)
    hid = jnp.arange(FOX_W) // HEAD_DIM
    bd = (hid[:, None] == hid[None, :]).astype(BF16)
    ex = (jnp.arange(N_FFPAD)[:, None] == hid[None, :]).astype(BF16)
    return tril, tril.T, bd, ex


def _crow4(ccol, T):
    S = ccol.shape[0]
    c = ccol[:, :FOX_HEADS].T
    last = jnp.pad(c[:, T - 1::T], ((0, 0), (0, S - S // T)))
    rows = jnp.concatenate([c.reshape(FOX_HEADS // 2, 2, S), last.reshape(FOX_HEADS // 2, 2, S)], axis=1)
    return jnp.pad(rows, ((0, 0), (0, 4), (0, 0)))


def _layer_fwd(x, lw, consts, ride=None):
    tril, triu, bd, ex = consts
    proj, pff, ht, psb = _inproj_fwd(x, lw["g"], lw["wm"], lw["wff"])
    qs, kn, ccol, cqb = _fox_prep(proj, pff, lw["bfp"], lw["gq"], lw["gk"], bd, ex, tril)
    crow4 = _crow4(ccol, tril.shape[0])
    fo, lse, *gathered = _fox_fwd(qs, kn, proj, cqb, crow4, ride)
    so, ltot = _sb_fwd(psb, triu)
    pooled = _pool_fwd(proj)
    y, mixedt = _mix_out(fo, so, pooled, proj, lw["wbd"], lw["scale"], lw["wout"], x)
    return y, (x, proj, pff, ht, psb, qs, kn, cqb, crow4, fo, lse, so, ltot, pooled, mixedt), gathered


def _layer_bwd(dy, saved, lw, consts, ride=None, exchange_own=False):
    tril, triu, bd, _ = consts
    x, proj, pff, ht, psb, qs, kn, cqb, crow4, fo, lse, so, ltot, pooled, mixedt = saved
    S = x.shape[0]
    dfo, dfg, dso, dsg, dpg, dpooled, dscale, dwbd = _gate_bwd(dy, lw["wout"], fo, so, pooled, proj, lw["wbd"], lw["scale"])
    dwout = _matmul_acc(mixedt, dy, "dw_out")
    dpx = _pool_bwd(dpooled)
    dqs, dkn, dfv, dck, dcq4, *received = _fox_bwd(qs, kn, proj, dfo, fo, lse, cqb, crow4, ride)
    dsq, dsk, dsv = _sb_bwd(psb, dso, ltot, tril)
    dc8 = dck[:, ::HEAD_DIM] + dcq4[:, :2, :].reshape(FOX_HEADS, S).T
    dccol = jnp.pad(dc8, ((0, 0), (0, N_FFPAD - FOX_HEADS)))
    dfq, dfk, dff, dgq, dgk, dbf = _qk_bwd(dqs, dkn, proj, pff, lw["bfp"], lw["gq"], lw["gk"], bd, dccol, triu)
    pieces = [dfq, dfk, dfv, dfg, dpx, dpg, dsq, dsk, dsv, dsg]
    dwm, dwff = _inproj_bwd_dw(ht, pieces, dff)
    dwin = jnp.concatenate([dwm[:, :OFF_PX], dwff[:, :N_FF], dwm[:, OFF_PX:]], axis=1)
    own = _grad_parts({"w_in": dwin, "w_out": dwout}) if exchange_own else None
    dx, dng, *received_own = _inproj_bwd_dx(pieces, dff, lw["wm"], lw["wff"], x, lw["g"], dy, own)
    grads = {
        "norm_g": dng[0],
        "w_in": dwin,
        "b_f": dbf[0, :N_FF],
        "q_norm_g": dgq[0].reshape(FOX_HEADS, HEAD_DIM).sum(0),
        "k_norm_g": dgk[0].reshape(FOX_HEADS, HEAD_DIM).sum(0),
        "w_pool": jnp.stack([dwbd[64 * i:64 * i + 64, 64 * i:64 * i + 64] for i in range(4)]),
        "pool_scale": dscale[0],
        "w_out": dwout,
    }
    return dx, grads, received, received_own


def _layer_weights(l, norm_g, gin, b_f, q_norm_g, k_norm_g, w_pool, pool_scale, gout):
    D = gin.shape[1]
    w = gin.transpose(1, 0, 2).reshape(D, D_IN)
    wm = jnp.concatenate([w[:, :2048], w[:, 2048 + N_FF:]], axis=1)
    wff = jnp.pad(w[:, 2048:2048 + N_FF], ((0, 0), (0, N_FFPAD - N_FF)))
    grp = jnp.arange(POOL_W) // 64
    wbd = jnp.where(grp[:, None] == grp[None, :], jnp.tile(w_pool[l].transpose(1, 0, 2).reshape(64, POOL_W), (4, 1)), 0.0)
    return {
        "g": norm_g[l].reshape(1, D),
        "wm": wm, "wff": wff,
        "bfp": jnp.pad(b_f[l], (0, N_FFPAD - N_FF)).reshape(1, N_FFPAD),
        "gq": jnp.tile(q_norm_g[l], FOX_HEADS).reshape(1, FOX_W),
        "gk": jnp.tile(k_norm_g[l], FOX_HEADS).reshape(1, FOX_W),
        "wbd": wbd.astype(BF16),
        "scale": pool_scale[l].reshape(1, POOL_W),
        "wout": gout.reshape(D_MIX, D),
    }


def _grad_parts(g):
    dwin, dwout = g["w_in"].astype(BF16), g["w_out"].astype(BF16)
    D = dwin.shape[0]
    return (dwin.reshape(D, N_DEV, D_IN // N_DEV).transpose(1, 0, 2),
            dwout.reshape(N_DEV, D_MIX // N_DEV, dwout.shape[1]))


def _train_step(x, target, norm_g, win_sh, b_f, q_norm_g, k_norm_g, w_pool, pool_scale, wout_sh):
    L = norm_g.shape[0]
    consts = _constants(min(_T, x.shape[0]))
    gathered = _gather_two_level(win_sh[0], wout_sh[0], "gather_weights")
    lws, saved = [], []
    h = x
    for l in range(L):
        lws.append(_layer_weights(l, norm_g, gathered[0], b_f, q_norm_g, k_norm_g, w_pool, pool_scale, gathered[1]))
        ride = (win_sh[l + 1], wout_sh[l + 1]) if l + 1 < L else None
        h, sv, gathered = _layer_fwd(h, lws[l], consts, ride)
        saved.append(sv)
    dy, loss = _loss_head(h, target)
    grads, received = [None] * L, [None] * L
    ride = None
    for l in reversed(range(L)):
        dy, grads[l], got, got_own = _layer_bwd(dy, saved[l], lws[l], consts, ride, exchange_own=(l == 0))
        if ride is not None:
            received[l + 1] = got
        if l == 0:
            received[0] = got_own
        else:
            ride = _grad_parts(grads[l])
    return loss, dy, grads, received


def _mesh_pos():
    return lax.axis_index("x"), lax.axis_index("y"), lax.axis_index("c")


_FLIPS = [(0, 0, 1), (1, 0, 0), (0, 1, 0), (1, 1, 0), (1, 0, 1), (0, 1, 1), (1, 1, 1)]


def _peers():
    x, y, c = _mesh_pos()
    out = []
    for fx, fy, fc in _FLIPS:
        px = 1 - x if fx else x
        py = 1 - y if fy else y
        pc = 1 - c if fc else c
        out.append(((px, py, pc), 4 * px + 2 * py + pc))
    return out, 4 * x + 2 * y + c


_EXCHANGE_SEMS = [pltpu.SemaphoreType.DMA((14,)), pltpu.SemaphoreType.DMA((14,)), pltpu.SemaphoreType.DMA((2,))]
_ANY = pl.BlockSpec(memory_space=pl.ANY)


def _exchange_copies(kind, a_ref, b_ref, oa_ref, ob_ref, send_sems, recv_sems, loc_sems):
    peers, me = _peers()
    pairs = ((a_ref, oa_ref), (b_ref, ob_ref))
    local = [pltpu.make_async_copy(src if kind == "gather" else src.at[me], dst.at[me], loc_sems.at[t])
             for t, (src, dst) in enumerate(pairs)]
    remote = []
    for k, (dev, idx) in enumerate(peers):
        for t, (src, dst) in enumerate(pairs):
            remote.append(pltpu.make_async_remote_copy(
                src_ref=src if kind == "gather" else src.at[idx], dst_ref=dst.at[me],
                send_sem=send_sems.at[2 * k + t], recv_sem=recv_sems.at[2 * k + t],
                device_id=dev, device_id_type=pl.DeviceIdType.MESH))
    return local, remote


def _start_exchange(kind, *refs):
    local, remote = _exchange_copies(kind, *refs)
    for cp in local + remote:
        cp.start()


def _wait_exchange(kind, *refs):
    local, remote = _exchange_copies(kind, *refs)
    for cp in remote:
        cp.wait_recv()
    for cp in remote:
        cp.wait_send()
    for cp in local:
        cp.wait()


def _exchange_out_shapes(kind, a, b):
    if kind == "gather":
        return [jax.ShapeDtypeStruct((N_DEV,) + a.shape, a.dtype), jax.ShapeDtypeStruct((N_DEV,) + b.shape, b.dtype)]
    return [jax.ShapeDtypeStruct(a.shape, a.dtype), jax.ShapeDtypeStruct(b.shape, b.dtype)]


def _gather_two_level(a, b, name):
    def body(a_ref, b_ref, ga_ref, gb_ref, send_sems, recv_sems, loc_sems):
        x, y, c = _mesh_pos()
        slot_of = lambda px, py, pc: 4 * px + 2 * py + pc
        me, sib = slot_of(x, y, c), slot_of(x, y, 1 - c)
        chips = [(1 - x, y), (x, 1 - y), (1 - x, 1 - y)]
        pairs = ((a_ref, ga_ref), (b_ref, gb_ref))

        def copy(k, t, slot, to, src=None):
            dst = pairs[t][1].at[slot]
            return pltpu.make_async_remote_copy(
                src_ref=dst if src is None else src, dst_ref=dst, send_sem=send_sems.at[2 * k + t],
                recv_sem=recv_sems.at[2 * k + t], device_id=to, device_id_type=pl.DeviceIdType.MESH)

        local = [pltpu.make_async_copy(src, dst.at[me], loc_sems.at[t]) for t, (src, dst) in enumerate(pairs)]
        first = []
        for t, (src, _) in enumerate(pairs):
            first.append(copy(0, t, me, (x, y, 1 - c), src))
            first += [copy(1 + j, t, me, (*chip, c), src) for j, chip in enumerate(chips)]
        for cp in local + first:
            cp.start()
        passed = []
        for j, chip in enumerate(chips):
            for t in range(2):
                landed = slot_of(*chip, c)
                copy(1 + j, t, landed, (x, y, c)).wait_recv()
                cp = copy(4 + j, t, landed, (x, y, 1 - c))
                cp.start()
                passed.append(cp)
        for t in range(2):
            copy(0, t, sib, (x, y, c)).wait_recv()
            for j, chip in enumerate(chips):
                copy(4 + j, t, slot_of(*chip, 1 - c), (x, y, c)).wait_recv()
        for cp in first + passed:
            cp.wait_send()
        for cp in local:
            cp.wait()

    return pl.pallas_call(
        body, name=name,
        in_specs=[_ANY, _ANY], out_specs=[_ANY, _ANY],
        out_shape=_exchange_out_shapes("gather", a, b),
        scratch_shapes=_EXCHANGE_SEMS,
    )(a, b)


def _adam_math(w, g, m, v):
    m_new = ADAM_B1 * m + (1.0 - ADAM_B1) * g
    v_new = ADAM_B2 * v + (1.0 - ADAM_B2) * (g * g)
    m_hat = m_new / (1.0 - ADAM_B1 ** ADAM_STEP)
    v_hat = v_new / (1.0 - ADAM_B2 ** ADAM_STEP)
    delta = -ADAM_LR * (m_hat / (jnp.sqrt(v_hat) + ADAM_EPS) + ADAM_WD * w)
    return delta, m_new, v_new


def _sum_adamw(gparts, w, m, v, name):
    L, R, C = w.shape
    tr = min(128, R)

    def body(*refs):
        gp_refs = refs[:L]
        w_ref, m_ref, v_ref, g_ref, d_ref, nm_ref, nv_ref = refs[L:]
        for l in range(L):
            g = gp_refs[l][0].astype(F32)
            for s in range(1, N_DEV):
                g = g + gp_refs[l][s].astype(F32)
            d, mn, vn = _adam_math(w_ref[l], g, m_ref[l], v_ref[l])
            g_ref[l] = g
            d_ref[l] = d
            nm_ref[l] = mn
            nv_ref[l] = vn

    blk = pl.BlockSpec((L, tr, C), lambda r: (0, r, 0))
    return pl.pallas_call(
        body, name=name,
        grid=(R // tr,),
        in_specs=[pl.BlockSpec((N_DEV, tr, C), lambda r: (0, r, 0))] * L + [blk, blk, blk],
        out_specs=[blk, blk, blk, blk],
        out_shape=[jax.ShapeDtypeStruct((L, R, C), F32)] * 4,
        compiler_params=_cp(("parallel",), _VMEM_WIDE),
    )(*gparts, w, m, v)


def _small_update(gpack, wpack, mpack, vpack):
    R = gpack.shape[0]
    VM = pl.BlockSpec(memory_space=pltpu.VMEM)

    def body(g_ref, w_ref, m_ref, v_ref, gs_ref, d_ref, nm_ref, nv_ref, buf, send_sems, recv_sems):
        peers, me = _peers()
        buf[me] = g_ref[...]
        copies = []
        for k, (dev, _) in enumerate(peers):
            cp = pltpu.make_async_remote_copy(
                src_ref=g_ref, dst_ref=buf.at[me], send_sem=send_sems.at[k], recv_sem=recv_sems.at[k],
                device_id=dev, device_id_type=pl.DeviceIdType.MESH)
            cp.start()
            copies.append(cp)
        for cp in copies:
            cp.wait_recv()
        for cp in copies:
            cp.wait_send()
        g = buf[0]
        for s in range(1, N_DEV):
            g = g + buf[s]
        d, mn, vn = _adam_math(w_ref[...], g, m_ref[...], v_ref[...])
        gs_ref[...] = g
        d_ref[...] = d
        nm_ref[...] = mn
        nv_ref[...] = vn

    return pl.pallas_call(
        body, name="small_update",
        in_specs=[VM] * 4, out_specs=[VM] * 4,
        out_shape=[jax.ShapeDtypeStruct((R, 128), F32)] * 4,
        scratch_shapes=[pltpu.VMEM((N_DEV, R, 128), F32), pltpu.SemaphoreType.DMA((7,)), pltpu.SemaphoreType.DMA((7,))],
        compiler_params=_cp(None, _VMEM_MID),
    )(gpack, wpack, mpack, vpack)


_SMALL = ("norm_g", "b_f", "q_norm_g", "k_norm_g", "w_pool", "pool_scale")


def _pack(parts):
    flat = jnp.concatenate([p.reshape(-1) for p in parts])
    n = flat.shape[0]
    rows = -(-n // (8 * 128)) * 8
    return jnp.pad(flat, (0, rows * 128 - n)).reshape(rows, 128)


def _unpack(packed, like):
    flat = packed.reshape(-1)
    out, o = [], 0
    for p in like:
        out.append(flat[o:o + p.size].reshape(p.shape))
        o += p.size
    return out


def kernel(x, norm_g, w_in, b_f, q_norm_g, k_norm_g, w_pool, pool_scale, w_out, loss_target, m_norm_g, m_w_in, m_b_f, m_q_norm_g, m_k_norm_g, m_w_pool, m_pool_scale, m_w_out, v_norm_g, v_w_in, v_b_f, v_q_norm_g, v_k_norm_g, v_w_pool, v_pool_scale, v_w_out):
    L = w_in.shape[0]

    loss_local, dx, grads, received = _train_step(x[0], loss_target[0], norm_g, w_in.astype(BF16), b_f, q_norm_g,
                                                  k_norm_g, w_pool, pool_scale, w_out.astype(BF16))
    loss = lax.psum(loss_local, MESH_AXES)
    g = {k: jnp.stack([grads[l][k] for l in range(L)]) for k in _SMALL}

    g_win, d_win, nm_win, nv_win = _sum_adamw([r[0] for r in received], w_in, m_w_in, v_w_in, "adamw_w_in")
    g_wout, d_wout, nm_wout, nv_wout = _sum_adamw([r[1] for r in received], w_out, m_w_out, v_w_out, "adamw_w_out")

    ws = dict(norm_g=norm_g, b_f=b_f, q_norm_g=q_norm_g, k_norm_g=k_norm_g, w_pool=w_pool, pool_scale=pool_scale)
    ms = dict(norm_g=m_norm_g, b_f=m_b_f, q_norm_g=m_q_norm_g, k_norm_g=m_k_norm_g, w_pool=m_w_pool, pool_scale=m_pool_scale)
    vs = dict(norm_g=v_norm_g, b_f=v_b_f, q_norm_g=v_q_norm_g, k_norm_g=v_k_norm_g, w_pool=v_w_pool, pool_scale=v_pool_scale)
    like = [ws[k] for k in _SMALL]
    gs_p, d_p, nm_p, nv_p = _small_update(_pack([g[k] for k in _SMALL]), _pack(like),
                                          _pack([ms[k] for k in _SMALL]), _pack([vs[k] for k in _SMALL]))
    gs = dict(zip(_SMALL, _unpack(gs_p, like)))
    ds = dict(zip(_SMALL, _unpack(d_p, like)))
    nms = dict(zip(_SMALL, _unpack(nm_p, like)))
    nvs = dict(zip(_SMALL, _unpack(nv_p, like)))
    gs["w_in"], ds["w_in"], nms["w_in"], nvs["w_in"] = g_win, d_win, nm_win, nv_win
    gs["w_out"], ds["w_out"], nms["w_out"], nvs["w_out"] = g_wout, d_wout, nm_wout, nv_wout

    order = ("norm_g", "w_in", "b_f", "q_norm_g", "k_norm_g", "w_pool", "pool_scale", "w_out")
    return (loss, dx[None], *[gs[k] for k in order], *[ds[k] for k in order],
            *[nms[k] for k in order], *[nvs[k] for k in order])
```

```python
import jax
import jax.numpy as jnp
from jax import lax
from jax.experimental import pallas as pl
from jax.experimental.pallas import tpu as pltpu

F32 = jnp.float32
BF16 = jnp.bfloat16

EPS = 1e-6
NEG = -1e30
HEAD_DIM = 64
FOX_HEADS = 8
FOX_W = 512
POOL_W = 256
SB_W = 256
D_MIX = 1024
N_FF = 8
N_MAIN = 3584
N_FFPAD = 128
OFF_FQ, OFF_FK, OFF_FV, OFF_FG = 0, 512, 1024, 1536
OFF_PX, OFF_PG = 2048, 2304
OFF_SQ, OFF_SK, OFF_SV, OFF_SG = 2560, 2816, 3072, 3328
D_IN = 3592
Q_SCALE = HEAD_DIM ** -0.5

ADAM_LR = 0.001
ADAM_B1 = 0.9
ADAM_B2 = 0.999
ADAM_EPS = 1e-08
ADAM_WD = 0.01
ADAM_STEP = 10

N_DEV = 8
MESH_AXES = ("x", "y", "c")

_T = 256
_TM = 512
_TM_ROWS = 512
_TM_FWD, _TN_FWD = 2048, 512
_TM_DX = 512
_TK_DW = 1024
_VMEM_V7X = 64 << 20
_VMEM_BIG = _VMEM_V7X - (8 << 20)
_VMEM_MID = 40 << 20
_VMEM_WIDE = 48 << 20


def _cp(sem=None, vmem=None):
    kw = {}
    if sem is not None:
        kw["dimension_semantics"] = sem
    if vmem is not None:
        kw["vmem_limit_bytes"] = vmem
    return pltpu.CompilerParams(**kw)


def _dot(a, b):
    return jnp.dot(a, b, preferred_element_type=F32)


def _dot_nt(a, b):
    return lax.dot_general(a, b, (((1,), (1,)), ((), ())), preferred_element_type=F32)


def _dot_tn(a, b):
    return lax.dot_general(a, b, (((0,), (0,)), ((), ())), preferred_element_type=F32)


def _mm2(v, m, left=False):
    hi = v.astype(BF16)
    lo = (v - hi.astype(F32)).astype(BF16)
    if left:
        return _dot(m, hi) + _dot(m, lo)
    return _dot(hi, m) + _dot(lo, m)


def _mm3(v, m, left=False):
    a1 = v.astype(BF16)
    r1 = v - a1.astype(F32)
    a2 = r1.astype(BF16)
    a3 = (r1 - a2.astype(F32)).astype(BF16)
    if left:
        return _dot(m, a1) + _dot(m, a2) + _dot(m, a3)
    return _dot(a1, m) + _dot(a2, m) + _dot(a3, m)


def _sigmoid(z):
    return 1.0 / (1.0 + jnp.exp(-z))


def _rms_rows(x):
    return lax.rsqrt(jnp.mean(x * x, axis=-1, keepdims=True) + EPS)


def _inproj_fwd(x, g, wm, wff):
    S, D = x.shape
    tm = min(_TM_FWD, S)
    tn = _TN_FWD
    assert OFF_SQ % tn == 0 and N_MAIN - OFF_SQ == 4 * SB_W
    j_sb = OFF_SQ // tn

    def body(x_ref, g_ref, w_ref, wff_ref, o_ref, off_ref, ht_ref, sb_ref, h_ref):
        j = pl.program_id(1)

        @pl.when(j == 0)
        def _():
            xv = x_ref[...]
            h = (xv * _rms_rows(xv)) * g_ref[...]
            h_ref[...] = h.astype(BF16)
            ht_ref[...] = h.T.astype(BF16)
            off_ref[...] = _dot(h_ref[...], wff_ref[...])

        res = _dot(h_ref[...], w_ref[...])
        o_ref[...] = res

        @pl.when(j >= j_sb)
        def _():
            sb_ref[...] = res.astype(BF16)

    return pl.pallas_call(
        body, name="inproj_fwd",
        grid=(S // tm, N_MAIN // tn),
        in_specs=[pl.BlockSpec((tm, D), lambda i, j: (i, 0)),
                  pl.BlockSpec((1, D), lambda i, j: (0, 0)),
                  pl.BlockSpec((D, tn), lambda i, j: (0, j)),
                  pl.BlockSpec((D, N_FFPAD), lambda i, j: (0, 0))],
        out_specs=[pl.BlockSpec((tm, tn), lambda i, j: (i, j)),
                   pl.BlockSpec((tm, N_FFPAD), lambda i, j: (i, 0)),
                   pl.BlockSpec((D, tm), lambda i, j: (0, i)),
                   pl.BlockSpec((tm, tn), lambda i, j: (i, jnp.maximum(j - j_sb, 0)))],
        out_shape=[jax.ShapeDtypeStruct((S, N_MAIN), F32), jax.ShapeDtypeStruct((S, N_FFPAD), F32),
                   jax.ShapeDtypeStruct((D, S), BF16), jax.ShapeDtypeStruct((S, 4 * SB_W), BF16)],
        scratch_shapes=[pltpu.VMEM((tm, D), BF16)],
        compiler_params=_cp(("parallel", "arbitrary"), _VMEM_BIG),
    )(x, g, wm, wff)


def _head_norm(x, g, bd):
    ss = _mm2(x * x, bd)
    r = lax.rsqrt(ss * (1.0 / HEAD_DIM) + EPS)
    return (x * r) * g


def _fox_prep(proj, pff, bfp, gq, gk, bd, ex, tril):
    S = proj.shape[0]
    T = tril.shape[0]

    def body(q_ref, k_ref, ff_ref, b_ref, gq_ref, gk_ref, bd_ref, ex_ref, tri_ref,
             qs_ref, kn_ref, cc_ref, cqb_ref, carry):
        @pl.when(pl.program_id(0) == 0)
        def _():
            carry[...] = jnp.zeros_like(carry)

        bdv = bd_ref[...]
        qs_ref[...] = (_head_norm(q_ref[...], gq_ref[...], bdv) * Q_SCALE).astype(BF16)
        kn_ref[...] = _head_norm(k_ref[...], gk_ref[...], bdv).astype(BF16)
        u = ff_ref[...] + b_ref[...]
        lf = jnp.minimum(u, 0.0) - jnp.log1p(jnp.exp(-jnp.abs(u)))
        c = _mm3(lf, tri_ref[...], left=True) + carry[0:1, :]
        carry[0:1, :] = c[T - 1:T, :]
        cc_ref[...] = c
        cqb_ref[...] = _mm3(c, ex_ref[...])

    return pl.pallas_call(
        body, name="fox_prep",
        grid=(S // T,),
        in_specs=[pl.BlockSpec((T, FOX_W), lambda i: (i, OFF_FQ // FOX_W)),
                  pl.BlockSpec((T, FOX_W), lambda i: (i, OFF_FK // FOX_W)),
                  pl.BlockSpec((T, N_FFPAD), lambda i: (i, 0)),
                  pl.BlockSpec((1, N_FFPAD), lambda i: (0, 0)),
                  pl.BlockSpec((1, FOX_W), lambda i: (0, 0)),
                  pl.BlockSpec((1, FOX_W), lambda i: (0, 0)),
                  pl.BlockSpec((FOX_W, FOX_W), lambda i: (0, 0)),
                  pl.BlockSpec((N_FFPAD, FOX_W), lambda i: (0, 0)),
                  pl.BlockSpec((T, T), lambda i: (0, 0))],
        out_specs=[pl.BlockSpec((T, FOX_W), lambda i: (i, 0)),
                   pl.BlockSpec((T, FOX_W), lambda i: (i, 0)),
                   pl.BlockSpec((T, N_FFPAD), lambda i: (i, 0)),
                   pl.BlockSpec((T, FOX_W), lambda i: (i, 0))],
        out_shape=[jax.ShapeDtypeStruct((S, FOX_W), BF16), jax.ShapeDtypeStruct((S, FOX_W), BF16),
                   jax.ShapeDtypeStruct((S, N_FFPAD), F32), jax.ShapeDtypeStruct((S, FOX_W), F32)],
        scratch_shapes=[pltpu.VMEM((8, N_FFPAD), F32)],
        compiler_params=_cp(("arbitrary",), _VMEM_MID),
    )(proj, proj, pff, bfp, gq, gk, bd, ex, tril)


def _pair_blk(S, off=0):
    return pl.BlockSpec((S, 128), lambda p: (0, off + p), pipeline_mode=pl.Buffered(1))


def _pair_rows(S):
    return pl.BlockSpec((None, 8, S), lambda p: (p, 0, 0), pipeline_mode=pl.Buffered(1))


def _head_masks(S):
    return lax.broadcasted_iota(jnp.int32, (S, 128), 1) < HEAD_DIM


_EXP_ZERO = 104.0


def _spread_heads(x):
    src = lax.broadcasted_iota(jnp.int32, (128, 128), 0)
    return (_mm3(x, (src == 0).astype(BF16)), _mm3(x, (src == HEAD_DIM).astype(BF16)))


def _score_bounds(q, k):
    same_head = ((lax.broadcasted_iota(jnp.int32, (128, 128), 0) < HEAD_DIM)
                 == (lax.broadcasted_iota(jnp.int32, (128, 128), 1) < HEAD_DIM)).astype(BF16)

    def max_norm2(x):
        xf = x.astype(F32)
        return jnp.max(_mm2(xf * xf, same_head), axis=0, keepdims=True)

    z = jnp.sqrt(max_norm2(q) * max_norm2(k))
    z = jnp.where(z == z, z, jnp.inf)
    return jnp.max(z[:, 0:1]) * 1.001 + 1e-3, jnp.max(z[:, 64:65]) * 1.001 + 1e-3


def _for_tiles_back(i, n, tiles_fn, fours=False):
    if fours:
        def four(t, c):
            tiles_fn([i - 1 - 4 * t, i - 2 - 4 * t, i - 3 - 4 * t, i - 4 - 4 * t])
            return c

        lax.fori_loop(0, lax.shift_right_logical(n, 2), four, 0)
        rest = i - (n & ~3)

        @pl.when((n & 2) != 0)
        def _():
            tiles_fn([rest - 1, rest - 2])
    else:
        def two(t, c):
            tiles_fn([i - 1 - 2 * t, i - 2 - 2 * t])
            return c

        lax.fori_loop(0, lax.shift_right_logical(n, 1), two, 0)

    @pl.when((n & 1) != 0)
    def _():
        tiles_fn([i - n])


def _fox_tiles_back(cr_ref, i, r0, zba, zbb):
    last = cr_ref[:, pl.ds(0, 128)]
    first = cr_ref[:, pl.ds(r0, 128)]
    alive_a = 2.0 * zba + first[0:1, 0:1] - last[2:3, :] > -_EXP_ZERO
    alive_b = 2.0 * zbb + first[1:2, 0:1] - last[3:4, :] > -_EXP_ZERO
    before = lax.broadcasted_iota(jnp.int32, (1, 128), 1) < i
    return jnp.sum((before & (alive_a | alive_b)).astype(jnp.int32))


def _fox_fwd(qs, kn, proj, cqb, crow4, ride=None):
    S = qs.shape[0]
    T = min(_T, S)
    nq = S // T
    n_pairs = FOX_W // 128

    def body(*refs):
        if ride is None:
            q_ref, k_ref, v_ref, cq_ref, cr_ref, o_ref, lse_ref = refs[:7]
            qa, qb, vta, vtb, cka, ckb, ma, mb, acca, accb = refs[7:]
        else:
            q_ref, k_ref, v_ref, cq_ref, cr_ref, wa_ref, wb_ref, o_ref, lse_ref, ga_ref, gb_ref = refs[:11]
            qa, qb, vta, vtb, cka, ckb, ma, mb, acca, accb = refs[11:21]
            xrefs = (wa_ref, wb_ref, ga_ref, gb_ref) + tuple(refs[21:])

            @pl.when(pl.program_id(0) == 0)
            def _():
                _start_exchange("gather", *xrefs)

        lane_s = _head_masks(S)
        q = q_ref[...]
        zq = jnp.zeros_like(q)
        qa[...] = jnp.where(lane_s, q, zq)
        qb[...] = jnp.where(lane_s, zq, q)
        cq = cq_ref[...]
        cka[...], ckb[...] = _spread_heads(cq)
        lse_ref[...] = jnp.zeros((8, S), F32)
        row_t = lax.broadcasted_iota(jnp.int32, (128, T), 0) < HEAD_DIM
        zba, zbb = _score_bounds(q, k_ref[...])

        def prep(c, carry):
            c0 = pl.multiple_of(c * T, T)
            vt = v_ref[pl.ds(c0, T), :].T
            vta[:, pl.ds(c0, T)] = jnp.where(row_t, vt, 1.0).astype(BF16)
            vtb[:, pl.ds(c0, T)] = jnp.where(row_t, 1.0, vt).astype(BF16)
            return carry

        lax.fori_loop(0, nq, prep, 0)
        causal = (lax.broadcasted_iota(jnp.int32, (T, T), 0) <= lax.broadcasted_iota(jnp.int32, (T, T), 1))

        heads = ((qa, vta, cka, ma, acca), (qb, vtb, ckb, mb, accb))

        def kv(js, r0, masked):
            cr = cr_ref[:, pl.ds(r0, T)]
            c0s = [pl.multiple_of(j * T, T) for j in js]
            ks = [k_ref[pl.ds(c0, T), :] for c0 in c0s]
            ss = []
            for h, (qr, _, ckr, _, _) in enumerate(heads):
                qh = qr[pl.ds(r0, T), :]
                row = []
                for k, c0 in zip(ks, c0s):
                    s = _dot_nt(k, qh) - jnp.tile(ckr[pl.ds(c0, T), :], (1, T // 128))
                    row.append(jnp.where(causal, s, NEG) if masked else s)
                ss.append(row)
            ms = []
            for h, (row, (_, _, _, mr, _)) in enumerate(zip(ss, heads)):
                top = row[0]
                for s in row[1:]:
                    top = jnp.maximum(top, s)
                m_old = mr[0:1, :]
                ms.append((m_old, jnp.maximum(m_old, jnp.max(top, axis=0, keepdims=True) + cr[h:h + 1, :])))
            ps = [[jnp.exp(s + (cr[h:h + 1, :] - m_new)).astype(BF16) for s in row]
                  for h, (row, (_, m_new)) in enumerate(zip(ss, ms))]
            pvs = []
            for row, (_, vr, _, _, _) in zip(ps, heads):
                pv = _dot(vr[:, pl.ds(c0s[0], T)], row[0])
                for p, c0 in zip(row[1:], c0s[1:]):
                    pv = pv + _dot(vr[:, pl.ds(c0, T)], p)
                pvs.append(pv)
            for pv, (m_old, m_new), (_, _, _, mr, ar) in zip(pvs, ms, heads):
                ar[...] = jnp.exp(m_old - m_new) * ar[...] + pv
                mr[0:1, :] = m_new

        def qblk(i, carry):
            r0 = pl.multiple_of(i * T, T)
            ma[...] = jnp.full((8, T), NEG, F32)
            mb[...] = jnp.full((8, T), NEG, F32)
            acca[...] = jnp.zeros((128, T), F32)
            accb[...] = jnp.zeros((128, T), F32)
            kv([i], r0, True)
            done = _fox_tiles_back(cr_ref, i, r0, zba, zbb)
            _for_tiles_back(i, done, lambda js: kv(js, r0, False), fours=True)
            aa = acca[...]
            ab = accb[...]
            la = aa[64:65, :]
            lb = ab[0:1, :]
            o_ref[pl.ds(r0, T), :] = jnp.where(row_t, aa / la, ab / lb).T
            lse_ref[0:1, pl.ds(r0, T)] = ma[0:1, :] + jnp.log(la)
            lse_ref[1:2, pl.ds(r0, T)] = mb[0:1, :] + jnp.log(lb)
            lse_ref[2:3, pl.ds(r0, T)] = jnp.broadcast_to(done.astype(F32), (1, T))
            return carry

        lax.fori_loop(0, nq, qblk, 0)
        if ride is not None:
            @pl.when(pl.program_id(0) == n_pairs - 1)
            def _():
                _wait_exchange("gather", *xrefs)

    extra = () if ride is None else tuple(ride)
    return pl.pallas_call(
        body, name="fox_fwd" if ride is None else "fox_fwd_gather",
        grid=(n_pairs,),
        in_specs=[_pair_blk(S), _pair_blk(S), _pair_blk(S, OFF_FV // 128), _pair_blk(S), _pair_rows(S)]
        + [_ANY] * len(extra),
        out_specs=[_pair_blk(S), _pair_rows(S)] + [_ANY] * len(extra),
        out_shape=[jax.ShapeDtypeStruct((S, FOX_W), F32), jax.ShapeDtypeStruct((n_pairs, 8, S), F32)]
        + (_exchange_out_shapes("gather", *extra) if extra else []),
        scratch_shapes=[pltpu.VMEM((S, 128), BF16)] * 2 + [pltpu.VMEM((128, S), BF16)] * 2
        + [pltpu.VMEM((S, 128), F32)] * 2 + [pltpu.VMEM((8, T), F32)] * 2 + [pltpu.VMEM((128, T), F32)] * 2
        + (_EXCHANGE_SEMS if extra else []),
        compiler_params=_cp(("arbitrary",), _VMEM_BIG),
    )(qs, kn, proj, cqb, crow4, *extra)


def _softplus_parts(z):
    e = jnp.exp(-jnp.abs(z))
    return e, jnp.maximum(z, 0.0) + jnp.log(1.0 + e)


def _sb_fwd(psb, triu):
    S = psb.shape[0]
    T = triu.shape[0]
    nq = S // T

    n_pairs = SB_W // 128
    H = 2 * n_pairs

    def body(q_ref, k_ref, v_ref, tri_ref, o_ref, lt_ref, qm, vt, rr, acc):
        lane_s = _head_masks(S)
        zbs = []
        for p in range(n_pairs):
            q = (q_ref[:, 128 * p:128 * (p + 1)].astype(F32) * Q_SCALE).astype(BF16)
            zq = jnp.zeros_like(q)
            qm[2 * p] = jnp.where(lane_s, q, zq)
            qm[2 * p + 1] = jnp.where(lane_s, zq, q)
            zbs += list(_score_bounds(q, k_ref[:, 128 * p:128 * (p + 1)]))
        lt_ref[...] = jnp.zeros((n_pairs, 8, S), F32)
        row_t = lax.broadcasted_iota(jnp.int32, (128, T), 0) < HEAD_DIM

        def prep(c, carry):
            c0 = pl.multiple_of(c * T, T)
            for p in range(n_pairs):
                vt[p, :, pl.ds(c0, T)] = v_ref[pl.ds(c0, T), 128 * p:128 * (p + 1)].astype(F32).T.astype(BF16)
            return carry

        lax.fori_loop(0, nq, prep, 0)
        strict = (lax.broadcasted_iota(jnp.int32, (T, T), 0) < lax.broadcasted_iota(jnp.int32, (T, T), 1))

        def kv(tiles, r0):
            tri = tri_ref[...]
            c0s = [pl.multiple_of(j * T, T) for j, _ in tiles]
            zs = [[_dot_nt(k_ref[pl.ds(c0, T), 128 * (h // 2):128 * (h // 2 + 1)], qm[h, pl.ds(r0, T), :])
                   for c0 in c0s] for h in range(H)]
            lbs = [[jnp.where(strict, -_softplus_parts(z)[1], 0.0) if masked else -_softplus_parts(z)[1]
                    for z, (_, masked) in zip(row, tiles)] for row in zs]
            incs = [[_mm2(lb, tri, left=True) for lb in row] for row in lbs]
            avs = []
            for h in range(H):
                r = rr[h, 0:1, :]
                av = None
                for z, inc, c0, (_, masked) in zip(zs[h], incs[h], c0s, tiles):
                    a = jnp.exp(z + inc + r)
                    if masked:
                        a = jnp.where(strict, a, 0.0)
                    term = _dot(vt[h // 2, :, pl.ds(c0, T)], a.astype(BF16))
                    av = term if av is None else av + term
                    r = r + inc[0:1, :]
                avs.append((av, r))
            for h, (av, r) in enumerate(avs):
                rr[h, 0:1, :] = r
                acc[h] = acc[h] + av

        def qblk(i, carry):
            r0 = pl.multiple_of(i * T, T)
            rr[...] = jnp.zeros((H, 8, T), F32)
            acc[...] = jnp.zeros((H, 128, T), F32)

            @pl.when(i == 0)
            def _():
                kv([(i, True)], r0)

            @pl.when(i > 0)
            def _():
                kv([(i, True), (i - 1, False)], r0)

            def alive():
                m = jnp.max(rr[0, 0:1, :]) + zbs[0]
                for h in range(1, H):
                    m = jnp.maximum(m, jnp.max(rr[h, 0:1, :]) + zbs[h])
                return m > -_EXP_ZERO

            def cond(st):
                return (st[0] < i) & st[1]

            def step(st):
                kv([(i - 1 - st[0], False)], r0)
                return st[0] + 1, alive()

            done, _ = lax.while_loop(cond, step, (jnp.minimum(i, 1), alive()))
            for p in range(n_pairs):
                o_ref[pl.ds(r0, T), 128 * p:128 * (p + 1)] = jnp.where(row_t, acc[2 * p], acc[2 * p + 1]).T
                lt_ref[p, 0:1, pl.ds(r0, T)] = rr[2 * p, 0:1, :]
                lt_ref[p, 1:2, pl.ds(r0, T)] = rr[2 * p + 1, 0:1, :]
                lt_ref[p, 2:3, pl.ds(r0, T)] = jnp.broadcast_to(done.astype(F32), (1, T))
            return carry

        lax.fori_loop(0, nq, qblk, 0)

    wide = lambda off: pl.BlockSpec((S, SB_W), lambda g: (0, off), pipeline_mode=pl.Buffered(1))
    return pl.pallas_call(
        body, name="sb_fwd",
        grid=(1,),
        in_specs=[wide(0), wide(1), wide(2), pl.BlockSpec((T, T), lambda g: (0, 0))],
        out_specs=[wide(0), pl.BlockSpec((n_pairs, 8, S), lambda g: (0, 0, 0), pipeline_mode=pl.Buffered(1))],
        out_shape=[jax.ShapeDtypeStruct((S, SB_W), F32), jax.ShapeDtypeStruct((n_pairs, 8, S), F32)],
        scratch_shapes=[pltpu.VMEM((H, S, 128), BF16), pltpu.VMEM((n_pairs, 128, S), BF16),
                        pltpu.VMEM((H, 8, T), F32), pltpu.VMEM((H, 128, T), F32)],
        compiler_params=_cp(("arbitrary",), _VMEM_BIG),
    )(psb, psb, psb, triu)


def _pool_window_lanes(shape):
    lane = lax.broadcasted_iota(jnp.int32, shape, 1)
    return jnp.where(lane < 64, 2, jnp.where(lane < 128, 4, jnp.where(lane < 192, 8, 16)))


def _pool_fwd(proj):
    S = proj.shape[0]

    def body(x_ref, o_ref):
        x = x_ref[...]
        t = lax.broadcasted_iota(jnp.int32, x.shape, 0)
        lane = lax.broadcasted_iota(jnp.int32, x.shape, 1)

        def back(a, k):
            return jnp.where(t >= k, pltpu.roll(a, k, 0), 0.0)

        s1 = x + back(x, 1)
        s2 = s1 + back(s1, 2)
        s4 = s2 + back(s2, 4)
        s8 = s4 + back(s4, 8)
        win = jnp.where(lane < 64, s1, jnp.where(lane < 128, s2, jnp.where(lane < 192, s4, s8)))
        cnt = jnp.minimum(t + 1, _pool_window_lanes(x.shape)).astype(F32)
        o_ref[...] = win / cnt - x

    return pl.pallas_call(
        body, name="pool_fwd",
        grid=(1,),
        in_specs=[pl.BlockSpec((S, POOL_W), lambda i: (0, OFF_PX // POOL_W))],
        out_specs=pl.BlockSpec((S, POOL_W), lambda i: (0, 0)),
        out_shape=jax.ShapeDtypeStruct((S, POOL_W), F32),
        compiler_params=_cp(("arbitrary",), _VMEM_BIG),
    )(proj)


def _silu(g):
    return g * _sigmoid(g)


def _mix_out(fo, so, pooled, proj, wbd, scale, wout, x):
    S, D = x.shape
    tm = min(_TM_ROWS, S)

    def body(fo_ref, fg_ref, so_ref, sg_ref, pl_ref, pg_ref, wbd_ref, sc_ref, w_ref, x_ref, y_ref, mxt_ref, mx_ref):
        parts = ((0, fo_ref[...] * _silu(fg_ref[...])),
                 (FOX_W, (_dot(pl_ref[...].astype(BF16), wbd_ref[...]) * sc_ref[...]) * _silu(pg_ref[...])),
                 (FOX_W + POOL_W, so_ref[...] * _silu(sg_ref[...])))
        for off, part in parts:
            w = part.shape[1]
            mx_ref[:, off:off + w] = part.astype(BF16)
            mxt_ref[off:off + w, :] = part.T.astype(BF16)
        y_ref[...] = x_ref[...] + _dot(mx_ref[...], w_ref[...])

    return pl.pallas_call(
        body, name="mix_out",
        grid=(S // tm,),
        in_specs=[pl.BlockSpec((tm, FOX_W), lambda i: (i, 0)),
                  pl.BlockSpec((tm, FOX_W), lambda i: (i, OFF_FG // FOX_W)),
                  pl.BlockSpec((tm, SB_W), lambda i: (i, 0)),
                  pl.BlockSpec((tm, SB_W), lambda i: (i, OFF_SG // SB_W)),
                  pl.BlockSpec((tm, POOL_W), lambda i: (i, 0)),
                  pl.BlockSpec((tm, POOL_W), lambda i: (i, OFF_PG // POOL_W)),
                  pl.BlockSpec((POOL_W, POOL_W), lambda i: (0, 0)),
                  pl.BlockSpec((1, POOL_W), lambda i: (0, 0)),
                  pl.BlockSpec((D_MIX, D), lambda i: (0, 0)),
                  pl.BlockSpec((tm, D), lambda i: (i, 0))],
        out_specs=[pl.BlockSpec((tm, D), lambda i: (i, 0)), pl.BlockSpec((D_MIX, tm), lambda i: (0, i))],
        out_shape=[jax.ShapeDtypeStruct((S, D), F32), jax.ShapeDtypeStruct((D_MIX, S), BF16)],
        scratch_shapes=[pltpu.VMEM((tm, D_MIX), BF16)],
        compiler_params=_cp(("parallel",), _VMEM_MID),
    )(fo, proj, so, proj, pooled, proj, wbd, scale, wout, x)


def _loss_head(y, target):
    S, D = y.shape
    tm = min(_TM, S)

    def body(y_ref, t_ref, dy_ref, ls_ref):
        @pl.when(pl.program_id(0) == 0)
        def _():
            ls_ref[...] = jnp.zeros_like(ls_ref)

        e = y_ref[...] - t_ref[...]
        dy_ref[...] = e * (1.0 / D)
        ls_ref[...] = ls_ref[...] + jnp.sum(e * e) * (0.5 / D)

    dy, ls = pl.pallas_call(
        body, name="loss_head",
        grid=(S // tm,),
        in_specs=[pl.BlockSpec((tm, D), lambda i: (i, 0)), pl.BlockSpec((tm, D), lambda i: (i, 0))],
        out_specs=[pl.BlockSpec((tm, D), lambda i: (i, 0)), pl.BlockSpec((8, 128), lambda i: (0, 0))],
        out_shape=[jax.ShapeDtypeStruct((S, D), F32), jax.ShapeDtypeStruct((8, 128), F32)],
        compiler_params=_cp(("arbitrary",), _VMEM_MID),
    )(y, target)
    return dy, ls[0, 0]


def _dsilu(g):
    s = _sigmoid(g)
    return s * (1.0 + g * (1.0 - s))


def _gate_bwd(dy, wout, fo, so, pooled, proj, wbd, scale):
    S, D = dy.shape
    tm = min(_TM_ROWS, S)

    def body(dy_ref, w_ref, fo_ref, fg_ref, so_ref, sg_ref, pl_ref, pg_ref, wbd_ref, sc_ref,
             dfo_ref, dfg_ref, dso_ref, dsg_ref, dpg_ref, dpl_ref, dsc_ref, dwbd_ref):
        @pl.when(pl.program_id(0) == 0)
        def _():
            dsc_ref[...] = jnp.zeros_like(dsc_ref)
            dwbd_ref[...] = jnp.zeros_like(dwbd_ref)

        dm = _dot_nt(dy_ref[...].astype(BF16), w_ref[...])
        dmf = dm[:, 0:FOX_W]
        dmp = dm[:, FOX_W:FOX_W + POOL_W]
        dms = dm[:, FOX_W + POOL_W:D_MIX]
        fg = fg_ref[...]
        dfo_ref[...] = dmf * _silu(fg)
        dfg_ref[...] = (dmf * fo_ref[...] * _dsilu(fg)).astype(BF16)
        sg = sg_ref[...]
        dso_ref[...] = (dms * _silu(sg)).astype(BF16)
        dsg_ref[...] = (dms * so_ref[...] * _dsilu(sg)).astype(BF16)
        pg = pg_ref[...]
        plb = pl_ref[...].astype(BF16)
        yw = _dot(plb, wbd_ref[...])
        sc = sc_ref[...]
        dpg_ref[...] = (dmp * (yw * sc) * _dsilu(pg)).astype(BF16)
        dys = dmp * _silu(pg)
        dsc_ref[...] = dsc_ref[...] + jnp.sum(dys * yw, axis=0, keepdims=True)
        dyw = (dys * sc).astype(BF16)
        dpl_ref[...] = _dot_nt(dyw, wbd_ref[...])
        dwbd_ref[...] = dwbd_ref[...] + _dot_tn(plb, dyw)

    return pl.pallas_call(
        body, name="gate_bwd",
        grid=(S // tm,),
        in_specs=[pl.BlockSpec((tm, D), lambda i: (i, 0)),
                  pl.BlockSpec((D_MIX, D), lambda i: (0, 0)),
                  pl.BlockSpec((tm, FOX_W), lambda i: (i, 0)),
                  pl.BlockSpec((tm, FOX_W), lambda i: (i, OFF_FG // FOX_W)),
                  pl.BlockSpec((tm, SB_W), lambda i: (i, 0)),
                  pl.BlockSpec((tm, SB_W), lambda i: (i, OFF_SG // SB_W)),
                  pl.BlockSpec((tm, POOL_W), lambda i: (i, 0)),
                  pl.BlockSpec((tm, POOL_W), lambda i: (i, OFF_PG // POOL_W)),
                  pl.BlockSpec((POOL_W, POOL_W), lambda i: (0, 0)),
                  pl.BlockSpec((1, POOL_W), lambda i: (0, 0))],
        out_specs=[pl.BlockSpec((tm, FOX_W), lambda i: (i, 0)),
                   pl.BlockSpec((tm, FOX_W), lambda i: (i, 0)),
                   pl.BlockSpec((tm, SB_W), lambda i: (i, 0)),
                   pl.BlockSpec((tm, SB_W), lambda i: (i, 0)),
                   pl.BlockSpec((tm, POOL_W), lambda i: (i, 0)),
                   pl.BlockSpec((tm, POOL_W), lambda i: (i, 0)),
                   pl.BlockSpec((1, POOL_W), lambda i: (0, 0)),
                   pl.BlockSpec((POOL_W, POOL_W), lambda i: (0, 0))],
        out_shape=[jax.ShapeDtypeStruct((S, FOX_W), F32), jax.ShapeDtypeStruct((S, FOX_W), BF16),
                   jax.ShapeDtypeStruct((S, SB_W), BF16), jax.ShapeDtypeStruct((S, SB_W), BF16),
                   jax.ShapeDtypeStruct((S, POOL_W), BF16), jax.ShapeDtypeStruct((S, POOL_W), F32),
                   jax.ShapeDtypeStruct((1, POOL_W), F32), jax.ShapeDtypeStruct((POOL_W, POOL_W), F32)],
        compiler_params=_cp(("arbitrary",), _VMEM_MID),
    )(dy, wout, fo, proj, so, proj, pooled, proj, wbd, scale)


def _matmul_acc(at, b, name):
    M, S = at.shape
    N = b.shape[1]
    tk = min(_TK_DW, S)
    tn = min(512, N)
    nk = S // tk

    def body(a_ref, b_ref, o_ref, acc):
        k = pl.program_id(1)

        @pl.when(k == 0)
        def _():
            acc[...] = jnp.zeros_like(acc)

        acc[...] = acc[...] + _dot(a_ref[...], b_ref[...].astype(BF16))

        @pl.when(k == nk - 1)
        def _():
            o_ref[...] = acc[...].astype(BF16)

    return pl.pallas_call(
        body, name=name,
        grid=(N // tn, nk),
        in_specs=[pl.BlockSpec((M, tk), lambda j, k: (0, k)), pl.BlockSpec((tk, tn), lambda j, k: (k, j))],
        out_specs=pl.BlockSpec((M, tn), lambda j, k: (0, j)),
        out_shape=jax.ShapeDtypeStruct((M, N), BF16),
        scratch_shapes=[pltpu.VMEM((M, tn), F32)],
        compiler_params=_cp(("parallel", "arbitrary"), _VMEM_MID),
    )(at, b)


def _pool_bwd(dpooled):
    S = dpooled.shape[0]

    def body(d_ref, o_ref):
        d = d_ref[...]
        t = lax.broadcasted_iota(jnp.int32, d.shape, 0)
        lane = lax.broadcasted_iota(jnp.int32, d.shape, 1)
        cnt = jnp.minimum(t + 1, _pool_window_lanes(d.shape)).astype(F32)
        u = d / cnt

        def fwd(a, k):
            return jnp.where(t < S - k, pltpu.roll(a, S - k, 0), 0.0)

        s1 = u + fwd(u, 1)
        s2 = s1 + fwd(s1, 2)
        s4 = s2 + fwd(s2, 4)
        s8 = s4 + fwd(s4, 8)
        win = jnp.where(lane < 64, s1, jnp.where(lane < 128, s2, jnp.where(lane < 192, s4, s8)))
        o_ref[...] = (win - d).astype(BF16)

    return pl.pallas_call(
        body, name="pool_bwd",
        grid=(1,),
        in_specs=[pl.BlockSpec((S, POOL_W), lambda i: (0, 0))],
        out_specs=pl.BlockSpec((S, POOL_W), lambda i: (0, 0)),
        out_shape=jax.ShapeDtypeStruct((S, POOL_W), BF16),
        compiler_params=_cp(("arbitrary",), _VMEM_BIG),
    )(dpooled)


def _fox_bwd(qs, kn, proj, dfo, fo, lse, cqb, crow4, ride=None):
    S = qs.shape[0]
    T = min(_T, S)
    nq = S // T
    n_pairs = FOX_W // 128

    def body(*refs):
        if ride is None:
            q_ref, k_ref, v_ref, do_ref, o_ref, lse_ref, cq_ref, cr_ref = refs[:8]
            dq_ref, dk_ref, dv_ref, dck_ref, dcq_ref = refs[8:13]
            scr = refs[13:]
        else:
            q_ref, k_ref, v_ref, do_ref, o_ref, lse_ref, cq_ref, cr_ref, pa_ref, pb_ref = refs[:10]
            dq_ref, dk_ref, dv_ref, dck_ref, dcq_ref, ra_ref, rb_ref = refs[10:17]
            scr = refs[17:32]
            xrefs = (pa_ref, pb_ref, ra_ref, rb_ref) + tuple(refs[32:])

            @pl.when(pl.program_id(0) == 0)
            def _():
                _start_exchange("scatter", *xrefs)

        qa, qb, kta, ktb, vb, doa, dob, cka, ckb, dcka, dckb, dva, dqt, dcqa, dcqb = scr
        lane_s = _head_masks(S)
        q = q_ref[...]
        zq = jnp.zeros_like(q)
        qa[...] = jnp.where(lane_s, q, zq)
        qb[...] = jnp.where(lane_s, zq, q)
        vb[...] = v_ref[...].astype(BF16)
        do = do_ref[...].astype(BF16)
        doa[...] = jnp.where(lane_s, do, zq)
        dob[...] = jnp.where(lane_s, zq, do)
        cq = cq_ref[...]
        cka[...], ckb[...] = _spread_heads(cq)
        zs = jnp.zeros((S, 128), F32)
        dk_ref[...] = zs
        dva[...] = zs
        dcka[...] = zs
        dckb[...] = zs
        dcq_ref[...] = jnp.zeros((8, S), F32)
        row_t = lax.broadcasted_iota(jnp.int32, (128, T), 0) < HEAD_DIM

        def prep(c, carry):
            c0 = pl.multiple_of(c * T, T)
            kt = k_ref[pl.ds(c0, T), :].astype(F32).T
            kta[:, pl.ds(c0, T)] = jnp.where(row_t, kt, 0.0).astype(BF16)
            ktb[:, pl.ds(c0, T)] = jnp.where(row_t, 0.0, kt).astype(BF16)
            return carry

        lax.fori_loop(0, nq, prep, 0)
        causal = (lax.broadcasted_iota(jnp.int32, (T, T), 0) <= lax.broadcasted_iota(jnp.int32, (T, T), 1))

        heads = ((qa, kta, doa, cka, dcka, dcqa), (qb, ktb, dob, ckb, dckb, dcqb))

        def kv(js, r0, lss, dls, masked):
            cr = cr_ref[:, pl.ds(r0, T)]
            c0s = [pl.multiple_of(j * T, T) for j in js]
            ks = [k_ref[pl.ds(c0, T), :] for c0 in c0s]
            vs = [vb[pl.ds(c0, T), :] for c0 in c0s]
            qhs = [hd[0][pl.ds(r0, T), :] for hd in heads]
            dohs = [hd[2][pl.ds(r0, T), :] for hd in heads]
            ss = []
            for h, hd in enumerate(heads):
                row = []
                for k, c0 in zip(ks, c0s):
                    s = _dot_nt(k, qhs[h]) - jnp.tile(hd[3][pl.ds(c0, T), :], (1, T // 128))
                    row.append(jnp.where(causal, s, NEG) if masked else s)
                ss.append(row)
            ps = [[jnp.exp(s + (cr[h:h + 1, :] - lss[h])) for s in row] for h, row in enumerate(ss)]
            dps = [[_dot_nt(v, dohs[h]) for v in vs] for h in range(2)]
            dss = [[p * (dp - dls[h]) for p, dp in zip(ps[h], dps[h])] for h in range(2)]
            pbs = [[p.astype(BF16) for p in row] for row in ps]
            dsbs = [[ds.astype(BF16) for ds in row] for row in dss]
            for t, c0 in enumerate(c0s):
                dva[pl.ds(c0, T), :] = dva[pl.ds(c0, T), :] + (_dot(pbs[0][t], dohs[0]) + _dot(pbs[1][t], dohs[1]))
                dk_ref[pl.ds(c0, T), :] = dk_ref[pl.ds(c0, T), :] + (_dot(dsbs[0][t], qhs[0]) + _dot(dsbs[1][t], qhs[1]))
            dq = None
            for h, hd in enumerate(heads):
                for t, c0 in enumerate(c0s):
                    term = _dot(hd[1][:, pl.ds(c0, T)], dsbs[h][t])
                    dq = term if dq is None else dq + term
            dqt[...] = dqt[...] + dq
            for h, hd in enumerate(heads):
                col = jnp.sum(dss[h][0], axis=0, keepdims=True)
                for ds in dss[h][1:]:
                    col = col + jnp.sum(ds, axis=0, keepdims=True)
                hd[5][0:1, :] = hd[5][0:1, :] + col
                for ds, c0 in zip(dss[h], c0s):
                    fold = ds[:, 0:128]
                    for u in range(1, T // 128):
                        fold = fold + ds[:, 128 * u:128 * (u + 1)]
                    hd[4][pl.ds(c0, T), :] = hd[4][pl.ds(c0, T), :] - fold

        def qblk(i, carry):
            r0 = pl.multiple_of(i * T, T)
            dt = (do_ref[pl.ds(r0, T), :] * o_ref[pl.ds(r0, T), :]).T
            dla = jnp.sum(jnp.where(row_t, dt, 0.0), axis=0, keepdims=True)
            dlb = jnp.sum(jnp.where(row_t, 0.0, dt), axis=0, keepdims=True)
            ls = lse_ref[:, pl.ds(r0, T)]
            lss = (ls[0:1, :], ls[1:2, :])
            back = jnp.max(ls[2:3, :]).astype(jnp.int32)
            dqt[...] = jnp.zeros((128, T), F32)
            dcqa[...] = jnp.zeros((8, T), F32)
            dcqb[...] = jnp.zeros((8, T), F32)
            kv([i], r0, lss, (dla, dlb), True)
            _for_tiles_back(i, back, lambda js: kv(js, r0, lss, (dla, dlb), False), fours=True)
            dq_ref[pl.ds(r0, T), :] = dqt[...].T
            dcq_ref[0:1, pl.ds(r0, T)] = dcqa[0:1, :]
            dcq_ref[1:2, pl.ds(r0, T)] = dcqb[0:1, :]
            return carry

        lax.fori_loop(0, nq, qblk, 0)
        dv_ref[...] = dva[...].astype(BF16)
        dck_ref[...] = jnp.where(lane_s, jnp.sum(dcka[...], axis=1, keepdims=True),
                                 jnp.sum(dckb[...], axis=1, keepdims=True))
        if ride is not None:
            @pl.when(pl.program_id(0) == n_pairs - 1)
            def _():
                _wait_exchange("scatter", *xrefs)

    extra = () if ride is None else tuple(ride)
    return pl.pallas_call(
        body, name="fox_bwd" if ride is None else "fox_bwd_exchange",
        grid=(n_pairs,),
        in_specs=[_pair_blk(S), _pair_blk(S), _pair_blk(S, OFF_FV // 128), _pair_blk(S), _pair_blk(S),
                  _pair_rows(S), _pair_blk(S), _pair_rows(S)] + [_ANY] * len(extra),
        out_specs=[_pair_blk(S), _pair_blk(S), _pair_blk(S), _pair_blk(S), _pair_rows(S)] + [_ANY] * len(extra),
        out_shape=[jax.ShapeDtypeStruct((S, FOX_W), F32), jax.ShapeDtypeStruct((S, FOX_W), F32),
                   jax.ShapeDtypeStruct((S, FOX_W), BF16), jax.ShapeDtypeStruct((S, FOX_W), F32),
                   jax.ShapeDtypeStruct((n_pairs, 8, S), F32)]
        + (_exchange_out_shapes("scatter", *extra) if extra else []),
        scratch_shapes=[pltpu.VMEM((S, 128), BF16)] * 2 + [pltpu.VMEM((128, S), BF16)] * 2
        + [pltpu.VMEM((S, 128), BF16)] * 3 + [pltpu.VMEM((S, 128), F32)] * 5
        + [pltpu.VMEM((128, T), F32)] + [pltpu.VMEM((8, T), F32)] * 2
        + (_EXCHANGE_SEMS if extra else []),
        compiler_params=_cp(("arbitrary",), _VMEM_BIG),
    )(qs, kn, proj, dfo, fo, lse, cqb, crow4, *extra)


def _sb_bwd(psb, dso, ltot, tril):
    S = psb.shape[0]
    T = tril.shape[0]
    nq = S // T
    n_pairs = SB_W // 128
    H = 2 * n_pairs

    def body(q_ref, k_ref, v_ref, do_ref, lt_ref, tri_ref, dq_ref, dk_ref, dv_ref,
             qm, kt, dka, dva, dqt, rr, gg):
        lane_s = _head_masks(S)
        for p in range(n_pairs):
            q = (q_ref[:, 128 * p:128 * (p + 1)].astype(F32) * Q_SCALE).astype(BF16)
            zq = jnp.zeros_like(q)
            qm[2 * p] = jnp.where(lane_s, q, zq)
            qm[2 * p + 1] = jnp.where(lane_s, zq, q)
        dka[...] = jnp.zeros((n_pairs, S, 128), F32)
        dva[...] = jnp.zeros((n_pairs, S, 128), F32)
        row_t = lax.broadcasted_iota(jnp.int32, (128, T), 0) < HEAD_DIM
        lane_t = lax.broadcasted_iota(jnp.int32, (T, 128), 1) < HEAD_DIM

        def prep(c, carry):
            c0 = pl.multiple_of(c * T, T)
            for p in range(n_pairs):
                kt[p, :, pl.ds(c0, T)] = k_ref[pl.ds(c0, T), 128 * p:128 * (p + 1)].astype(F32).T.astype(BF16)
            return carry

        lax.fori_loop(0, nq, prep, 0)
        strict = (lax.broadcasted_iota(jnp.int32, (T, T), 0) < lax.broadcasted_iota(jnp.int32, (T, T), 1))

        def own(x, h, mask):
            z = jnp.zeros_like(x)
            return jnp.where(mask, x, z) if h % 2 == 0 else jnp.where(mask, z, x)

        def pair(ref, p, c0):
            return ref[pl.ds(c0, T), 128 * p:128 * (p + 1)]

        def kv(tiles, r0, lts):
            tri = tri_ref[...]
            c0s = [pl.multiple_of(j * T, T) for j, _ in tiles]
            qhs = [qm[h, pl.ds(r0, T), :] for h in range(H)]
            dohs = [own(pair(do_ref, h // 2, r0), h, lane_t) for h in range(H)]
            zs = [[_dot_nt(pair(k_ref, h // 2, c0), qhs[h]) for c0 in c0s] for h in range(H)]
            das = [[_dot_nt(pair(v_ref, h // 2, c0), dohs[h]) for c0 in c0s] for h in range(H)]
            es, lbs = [], []
            for row in zs:
                erow, lrow = [], []
                for z, (_, masked) in zip(row, tiles):
                    e, sp = _softplus_parts(z)
                    erow.append(e)
                    lrow.append(jnp.where(strict, -sp, 0.0) if masked else -sp)
                es.append(erow)
                lbs.append(lrow)
            pres = [[_mm2(lb, tri, left=True) for lb in row] for row in lbs]
            aas, r_ends = [], []
            for h in range(H):
                r = rr[h, 0:1, :]
                arow = []
                for z, lb, pre, (_, masked) in zip(zs[h], lbs[h], pres[h], tiles):
                    a = jnp.exp(z + lb + ((lts[h] - r) - pre))
                    arow.append(jnp.where(strict, a, 0.0) if masked else a)
                    r = r + pre[T - 1:T, :]
                aas.append(arow)
                r_ends.append(r)
            gs = [[a * da for a, da in zip(arow, drow)] for arow, drow in zip(aas, das)]
            gpres = [[_mm2(g, tri, left=True) for g in row] for row in gs]
            dzbs, g_ends = [], []
            for h in range(H):
                gc = gg[h, 0:1, :]
                drow = []
                for z, e, g, gpre, (_, masked) in zip(zs[h], es[h], gs[h], gpres[h], tiles):
                    inv = 1.0 / (1.0 + e)
                    pos = z >= 0.0
                    sig = jnp.where(pos, 1.0, e) * inv
                    oms = jnp.where(pos, e, 1.0) * inv
                    dz = g * oms - sig * (gc + (gpre - g))
                    if masked:
                        dz = jnp.where(strict, dz, 0.0)
                    drow.append(dz.astype(BF16))
                    gc = gc + gpre[T - 1:T, :]
                dzbs.append(drow)
                g_ends.append(gc)
            for p in range(n_pairs):
                a, b = 2 * p, 2 * p + 1
                dq = None
                for h in (a, b):
                    for t, c0 in enumerate(c0s):
                        term = _dot(own(kt[p, :, pl.ds(c0, T)], h, row_t), dzbs[h][t])
                        dq = term if dq is None else dq + term
                dqt[p] = dqt[p] + dq
                for t, c0 in enumerate(c0s):
                    dka[p, pl.ds(c0, T), :] = dka[p, pl.ds(c0, T), :] + (_dot(dzbs[a][t], qhs[a]) + _dot(dzbs[b][t], qhs[b]))
                    dva[p, pl.ds(c0, T), :] = dva[p, pl.ds(c0, T), :] + (_dot(aas[a][t].astype(BF16), dohs[a])
                                                                      + _dot(aas[b][t].astype(BF16), dohs[b]))
            for h in range(H):
                rr[h, 0:1, :] = r_ends[h]
                gg[h, 0:1, :] = g_ends[h]

        def qblk(i, carry):
            r0 = pl.multiple_of(i * T, T)
            lts = []
            for p in range(n_pairs):
                lt = lt_ref[p, :, pl.ds(r0, T)]
                lts += [lt[0:1, :], lt[1:2, :]]
            back = jnp.max(lt_ref[0, 2:3, pl.ds(r0, T)]).astype(jnp.int32)
            dqt[...] = jnp.zeros((n_pairs, 128, T), F32)
            rr[...] = jnp.zeros((H, 8, T), F32)
            gg[...] = jnp.zeros((H, 8, T), F32)

            def inner(j, c):
                kv([(j, False)], r0, lts)
                return c

            @pl.when(back == 0)
            def _():
                kv([(i, True)], r0, lts)

            @pl.when(back > 0)
            def _():
                lax.fori_loop(i - back, i - 1, inner, 0)
                kv([(i - 1, False), (i, True)], r0, lts)

            for p in range(n_pairs):
                dq_ref[pl.ds(r0, T), 128 * p:128 * (p + 1)] = (dqt[p] * Q_SCALE).T.astype(BF16)
            return carry

        lax.fori_loop(0, nq, qblk, 0)
        for p in range(n_pairs):
            dk_ref[:, 128 * p:128 * (p + 1)] = dka[p].astype(BF16)
            dv_ref[:, 128 * p:128 * (p + 1)] = dva[p].astype(BF16)

    wide = lambda off: pl.BlockSpec((S, SB_W), lambda g: (0, off), pipeline_mode=pl.Buffered(1))
    return pl.pallas_call(
        body, name="sb_bwd",
        grid=(1,),
        in_specs=[wide(0), wide(1), wide(2), wide(0),
                  pl.BlockSpec((n_pairs, 8, S), lambda g: (0, 0, 0), pipeline_mode=pl.Buffered(1)),
                  pl.BlockSpec((T, T), lambda g: (0, 0))],
        out_specs=[wide(0), wide(0), wide(0)],
        out_shape=[jax.ShapeDtypeStruct((S, SB_W), BF16)] * 3,
        scratch_shapes=[pltpu.VMEM((H, S, 128), BF16), pltpu.VMEM((n_pairs, 128, S), BF16),
                        pltpu.VMEM((n_pairs, S, 128), F32), pltpu.VMEM((n_pairs, S, 128), F32),
                        pltpu.VMEM((n_pairs, 128, T), F32), pltpu.VMEM((H, 8, T), F32), pltpu.VMEM((H, 8, T), F32)],
        compiler_params=_cp(("arbitrary",), _VMEM_BIG),
    )(psb, psb, psb, dso, ltot, tril)


def _head_norm_bwd(x, g, dy, bd):
    ss = _mm2(x * x, bd)
    r = lax.rsqrt(ss * (1.0 / HEAD_DIM) + EPS)
    xr = x * r
    gdy = g * dy
    m = _mm2(xr * gdy, bd) * (1.0 / HEAD_DIM)
    return r * (gdy - xr * m), dy * xr


def _qk_bwd(dqs, dkn, proj, pff, bfp, gq, gk, bd, dccol, triu):
    S = proj.shape[0]
    T = triu.shape[0]
    n = S // T
    rev = lambda col: (lambda i: (n - 1 - i, col))

    def body(dq_ref, dk_ref, q_ref, k_ref, ff_ref, b_ref, gq_ref, gk_ref, bd_ref, dc_ref, tri_ref,
             dfq_ref, dfk_ref, dff_ref, dgq_ref, dgk_ref, dbf_ref, carry):
        @pl.when(pl.program_id(0) == 0)
        def _():
            carry[...] = jnp.zeros_like(carry)
            dgq_ref[...] = jnp.zeros_like(dgq_ref)
            dgk_ref[...] = jnp.zeros_like(dgk_ref)
            dbf_ref[...] = jnp.zeros_like(dbf_ref)

        bdv = bd_ref[...]
        dxq, gq_rows = _head_norm_bwd(q_ref[...], gq_ref[...], dq_ref[...] * Q_SCALE, bdv)
        dfq_ref[...] = dxq.astype(BF16)
        dgq_ref[...] = dgq_ref[...] + jnp.sum(gq_rows, axis=0, keepdims=True)
        dxk, gk_rows = _head_norm_bwd(k_ref[...], gk_ref[...], dk_ref[...], bdv)
        dfk_ref[...] = dxk.astype(BF16)
        dgk_ref[...] = dgk_ref[...] + jnp.sum(gk_rows, axis=0, keepdims=True)
        dlf = _mm3(dc_ref[...], tri_ref[...], left=True) + carry[0:1, :]
        carry[0:1, :] = dlf[0:1, :]
        u = ff_ref[...] + b_ref[...]
        lane = lax.broadcasted_iota(jnp.int32, u.shape, 1)
        dff = jnp.where(lane < N_FF, dlf * _sigmoid(-u), 0.0)
        dff_ref[...] = dff.astype(BF16)
        dbf_ref[...] = dbf_ref[...] + jnp.sum(dff, axis=0, keepdims=True)

    return pl.pallas_call(
        body, name="qk_bwd",
        grid=(n,),
        in_specs=[pl.BlockSpec((T, FOX_W), rev(0)), pl.BlockSpec((T, FOX_W), rev(0)),
                  pl.BlockSpec((T, FOX_W), rev(OFF_FQ // FOX_W)), pl.BlockSpec((T, FOX_W), rev(OFF_FK // FOX_W)),
                  pl.BlockSpec((T, N_FFPAD), rev(0)),
                  pl.BlockSpec((1, N_FFPAD), lambda i: (0, 0)),
                  pl.BlockSpec((1, FOX_W), lambda i: (0, 0)), pl.BlockSpec((1, FOX_W), lambda i: (0, 0)),
                  pl.BlockSpec((FOX_W, FOX_W), lambda i: (0, 0)),
                  pl.BlockSpec((T, N_FFPAD), rev(0)),
                  pl.BlockSpec((T, T), lambda i: (0, 0))],
        out_specs=[pl.BlockSpec((T, FOX_W), rev(0)), pl.BlockSpec((T, FOX_W), rev(0)),
                   pl.BlockSpec((T, N_FFPAD), rev(0)),
                   pl.BlockSpec((1, FOX_W), lambda i: (0, 0)), pl.BlockSpec((1, FOX_W), lambda i: (0, 0)),
                   pl.BlockSpec((1, N_FFPAD), lambda i: (0, 0))],
        out_shape=[jax.ShapeDtypeStruct((S, FOX_W), BF16), jax.ShapeDtypeStruct((S, FOX_W), BF16),
                   jax.ShapeDtypeStruct((S, N_FFPAD), BF16),
                   jax.ShapeDtypeStruct((1, FOX_W), F32), jax.ShapeDtypeStruct((1, FOX_W), F32),
                   jax.ShapeDtypeStruct((1, N_FFPAD), F32)],
        scratch_shapes=[pltpu.VMEM((8, N_FFPAD), F32)],
        compiler_params=_cp(("arbitrary",), _VMEM_MID),
    )(dqs, dkn, proj, proj, pff, bfp, gq, gk, bd, dccol, triu)


def _dproj_layout(pieces):
    offs, o = [], 0
    for p in pieces:
        offs.append(o)
        o += p.shape[1]
    assert o == N_MAIN
    return offs


def _inproj_bwd_dx(pieces, dff, wm, wff, x, g, dy, ride=None):
    S, D = x.shape
    tm = min(_TM_DX, S)
    steps = S // tm
    offs = _dproj_layout(pieces)
    n = len(pieces)

    def body(*refs):
        p_refs = refs[:n]
        if ride is None:
            dff_ref, w_ref, wff_ref, x_ref, g_ref, dy_ref, dx_ref, dg_ref = refs[n:]
        else:
            dff_ref, w_ref, wff_ref, x_ref, g_ref, dy_ref, pa_ref, pb_ref = refs[n:n + 8]
            dx_ref, dg_ref, ra_ref, rb_ref = refs[n + 8:n + 12]
            xrefs = (pa_ref, pb_ref, ra_ref, rb_ref) + tuple(refs[n + 12:])

        @pl.when(pl.program_id(0) == 0)
        def _():
            dg_ref[...] = jnp.zeros_like(dg_ref)
            if ride is not None:
                _start_exchange("scatter", *xrefs)

        dh = _dot_nt(dff_ref[...], wff_ref[...])
        for p_ref, off in zip(p_refs, offs):
            dh = dh + _dot_nt(p_ref[...], w_ref[:, off:off + p_ref.shape[1]])
        xv = x_ref[...]
        r = _rms_rows(xv)
        xr = xv * r
        dg_ref[...] = dg_ref[...] + jnp.sum(dh * xr, axis=0, keepdims=True)
        gdh = g_ref[...] * dh
        m = jnp.mean(gdh * xr, axis=-1, keepdims=True)
        dx_ref[...] = dy_ref[...] + r * (gdh - xr * m)
        if ride is not None:
            @pl.when(pl.program_id(0) == steps - 1)
            def _():
                _wait_exchange("scatter", *xrefs)

    extra = () if ride is None else tuple(ride)
    return pl.pallas_call(
        body, name="inproj_bwd_dx" if ride is None else "inproj_bwd_dx_exchange",
        grid=(steps,),
        in_specs=[pl.BlockSpec((tm, p.shape[1]), lambda i: (i, 0)) for p in pieces]
        + [pl.BlockSpec((tm, N_FFPAD), lambda i: (i, 0)),
                  pl.BlockSpec((D, N_MAIN), lambda i: (0, 0)),
                  pl.BlockSpec((D, N_FFPAD), lambda i: (0, 0)),
                  pl.BlockSpec((tm, D), lambda i: (i, 0)),
                  pl.BlockSpec((1, D), lambda i: (0, 0)),
                  pl.BlockSpec((tm, D), lambda i: (i, 0))] + [_ANY] * len(extra),
        out_specs=[pl.BlockSpec((tm, D), lambda i: (i, 0)), pl.BlockSpec((1, D), lambda i: (0, 0))] + [_ANY] * len(extra),
        out_shape=[jax.ShapeDtypeStruct((S, D), F32), jax.ShapeDtypeStruct((1, D), F32)]
        + (_exchange_out_shapes("scatter", *extra) if extra else []),
        scratch_shapes=_EXCHANGE_SEMS if extra else [],
        compiler_params=_cp(("arbitrary",), _VMEM_WIDE),
    )(*pieces, dff, wm, wff, x, g, dy, *extra)


def _inproj_bwd_dw(ht, pieces, dff):
    D, S = ht.shape
    tk = min(_TK_DW, S)
    nk = S // tk
    offs = _dproj_layout(pieces)
    n = len(pieces)

    def body(*refs):
        ht_ref, p_refs, dff_ref = refs[0], refs[1:1 + n], refs[1 + n]
        dw_ref, dwff_ref, acc, accff = refs[2 + n:]
        k = pl.program_id(0)

        @pl.when(k == 0)
        def _():
            acc[...] = jnp.zeros_like(acc)
            accff[...] = jnp.zeros_like(accff)

        hb = ht_ref[...]
        for p_ref, off in zip(p_refs, offs):
            w = p_ref.shape[1]
            acc[:, off:off + w] = acc[:, off:off + w] + _dot(hb, p_ref[...])
        accff[...] = accff[...] + _dot(hb, dff_ref[...])

        @pl.when(k == nk - 1)
        def _():
            dw_ref[...] = acc[...].astype(BF16)
            dwff_ref[...] = accff[...].astype(BF16)

    return pl.pallas_call(
        body, name="inproj_bwd_dw",
        grid=(nk,),
        in_specs=[pl.BlockSpec((D, tk), lambda k: (0, k))]
        + [pl.BlockSpec((tk, p.shape[1]), lambda k: (k, 0)) for p in pieces]
        + [pl.BlockSpec((tk, N_FFPAD), lambda k: (k, 0))],
        out_specs=[pl.BlockSpec((D, N_MAIN), lambda k: (0, 0), pipeline_mode=pl.Buffered(1)),
                   pl.BlockSpec((D, N_FFPAD), lambda k: (0, 0), pipeline_mode=pl.Buffered(1))],
        out_shape=[jax.ShapeDtypeStruct((D, N_MAIN), BF16), jax.ShapeDtypeStruct((D, N_FFPAD), BF16)],
        scratch_shapes=[pltpu.VMEM((D, N_MAIN), F32), pltpu.VMEM((D, N_FFPAD), F32)],
        compiler_params=_cp(("arbitrary",), _VMEM_BIG),
    )(ht, *pieces, dff)


def _constants(T, rows):
    tril = jnp.tril(jnp.ones((T, T), F32)).astype(BF16)
    tril_rows = jnp.tril(jnp.ones((rows, rows), F32)).astype(BF16)
    hid = jnp.arange(FOX_W) // HEAD_DIM
    bd = (hid[:, None] == hid[None, :]).astype(BF16)
    ex = (jnp.arange(N_FFPAD)[:, None] == hid[None, :]).astype(BF16)
    return tril, tril.T, bd, ex, tril_rows, tril_rows.T


def _crow4(ccol, T):
    S = ccol.shape[0]
    c = ccol[:, :FOX_HEADS].T
    last = jnp.pad(c[:, T - 1::T], ((0, 0), (0, S - S // T)))
    rows = jnp.concatenate([c.reshape(FOX_HEADS // 2, 2, S), last.reshape(FOX_HEADS // 2, 2, S)], axis=1)
    return jnp.pad(rows, ((0, 0), (0, 4), (0, 0)))


def _layer_fwd(x, lw, consts, ride=None):
    tril, triu, bd, ex, tril_rows, _ = consts
    proj, pff, ht, psb = _inproj_fwd(x, lw["g"], lw["wm"], lw["wff"])
    qs, kn, ccol, cqb = _fox_prep(proj, pff, lw["bfp"], lw["gq"], lw["gk"], bd, ex, tril_rows)
    crow4 = _crow4(ccol, tril.shape[0])
    fo, lse, *gathered = _fox_fwd(qs, kn, proj, cqb, crow4, ride)
    so, ltot = _sb_fwd(psb, triu)
    pooled = _pool_fwd(proj)
    y, mixedt = _mix_out(fo, so, pooled, proj, lw["wbd"], lw["scale"], lw["wout"], x)
    return y, (x, proj, pff, ht, psb, qs, kn, cqb, crow4, fo, lse, so, ltot, pooled, mixedt), gathered


def _layer_bwd(dy, saved, lw, consts, ride=None, exchange_own=False):
    tril, _, bd, _, _, triu_rows = consts
    x, proj, pff, ht, psb, qs, kn, cqb, crow4, fo, lse, so, ltot, pooled, mixedt = saved
    S = x.shape[0]
    dfo, dfg, dso, dsg, dpg, dpooled, dscale, dwbd = _gate_bwd(dy, lw["wout"], fo, so, pooled, proj, lw["wbd"], lw["scale"])
    dwout = _matmul_acc(mixedt, dy, "dw_out")
    dpx = _pool_bwd(dpooled)
    dqs, dkn, dfv, dck, dcq4, *received = _fox_bwd(qs, kn, proj, dfo, fo, lse, cqb, crow4, ride)
    dsq, dsk, dsv = _sb_bwd(psb, dso, ltot, tril)
    dc8 = dck[:, ::HEAD_DIM] + dcq4[:, :2, :].reshape(FOX_HEADS, S).T
    dccol = jnp.pad(dc8, ((0, 0), (0, N_FFPAD - FOX_HEADS)))
    dfq, dfk, dff, dgq, dgk, dbf = _qk_bwd(dqs, dkn, proj, pff, lw["bfp"], lw["gq"], lw["gk"], bd, dccol, triu_rows)
    pieces = [dfq, dfk, dfv, dfg, dpx, dpg, dsq, dsk, dsv, dsg]
    dwm, dwff = _inproj_bwd_dw(ht, pieces, dff)
    dwin = jnp.concatenate([dwm[:, :OFF_PX], dwff[:, :N_FF], dwm[:, OFF_PX:]], axis=1)
    own = _grad_parts({"w_in": dwin, "w_out": dwout}) if exchange_own else None
    dx, dng, *received_own = _inproj_bwd_dx(pieces, dff, lw["wm"], lw["wff"], x, lw["g"], dy, own)
    grads = {
        "norm_g": dng[0],
        "w_in": dwin,
        "b_f": dbf[0, :N_FF],
        "q_norm_g": dgq[0].reshape(FOX_HEADS, HEAD_DIM).sum(0),
        "k_norm_g": dgk[0].reshape(FOX_HEADS, HEAD_DIM).sum(0),
        "w_pool": jnp.stack([dwbd[64 * i:64 * i + 64, 64 * i:64 * i + 64] for i in range(4)]),
        "pool_scale": dscale[0],
        "w_out": dwout,
    }
    return dx, grads, received, received_own


def _layer_weights(l, norm_g, gin, b_f, q_norm_g, k_norm_g, w_pool, pool_scale, gout):
    D = gin.shape[1]
    w = gin.transpose(1, 0, 2).reshape(D, D_IN)
    wm = jnp.concatenate([w[:, :2048], w[:, 2048 + N_FF:]], axis=1)
    wff = jnp.pad(w[:, 2048:2048 + N_FF], ((0, 0), (0, N_FFPAD - N_FF)))
    grp = jnp.arange(POOL_W) // 64
    wbd = jnp.where(grp[:, None] == grp[None, :], jnp.tile(w_pool[l].transpose(1, 0, 2).reshape(64, POOL_W), (4, 1)), 0.0)
    return {
        "g": norm_g[l].reshape(1, D),
        "wm": wm, "wff": wff,
        "bfp": jnp.pad(b_f[l], (0, N_FFPAD - N_FF)).reshape(1, N_FFPAD),
        "gq": jnp.tile(q_norm_g[l], FOX_HEADS).reshape(1, FOX_W),
        "gk": jnp.tile(k_norm_g[l], FOX_HEADS).reshape(1, FOX_W),
        "wbd": wbd.astype(BF16),
        "scale": pool_scale[l].reshape(1, POOL_W),
        "wout": gout.reshape(D_MIX, D),
    }


def _grad_parts(g):
    dwin, dwout = g["w_in"].astype(BF16), g["w_out"].astype(BF16)
    D = dwin.shape[0]
    return (dwin.reshape(D, N_DEV, D_IN // N_DEV).transpose(1, 0, 2),
            dwout.reshape(N_DEV, D_MIX // N_DEV, dwout.shape[1]))


def _train_step(x, target, norm_g, win_sh, b_f, q_norm_g, k_norm_g, w_pool, pool_scale, wout_sh):
    L = norm_g.shape[0]
    consts = _constants(min(_T, x.shape[0]), min(_TM_ROWS, x.shape[0]))
    gathered = _gather_two_level(win_sh[0], wout_sh[0], "gather_weights")
    lws, saved = [], []
    h = x
    for l in range(L):
        lws.append(_layer_weights(l, norm_g, gathered[0], b_f, q_norm_g, k_norm_g, w_pool, pool_scale, gathered[1]))
        ride = (win_sh[l + 1], wout_sh[l + 1]) if l + 1 < L else None
        h, sv, gathered = _layer_fwd(h, lws[l], consts, ride)
        saved.append(sv)
    dy, loss = _loss_head(h, target)
    grads, received = [None] * L, [None] * L
    ride = None
    for l in reversed(range(L)):
        dy, grads[l], got, got_own = _layer_bwd(dy, saved[l], lws[l], consts, ride, exchange_own=(l == 0))
        if ride is not None:
            received[l + 1] = got
        if l == 0:
            received[0] = got_own
        else:
            ride = _grad_parts(grads[l])
    return loss, dy, grads, received


def _mesh_pos():
    return lax.axis_index("x"), lax.axis_index("y"), lax.axis_index("c")


_FLIPS = [(0, 0, 1), (1, 0, 0), (0, 1, 0), (1, 1, 0), (1, 0, 1), (0, 1, 1), (1, 1, 1)]


def _peers():
    x, y, c = _mesh_pos()
    out = []
    for fx, fy, fc in _FLIPS:
        px = 1 - x if fx else x
        py = 1 - y if fy else y
        pc = 1 - c if fc else c
        out.append(((px, py, pc), 4 * px + 2 * py + pc))
    return out, 4 * x + 2 * y + c


_EXCHANGE_SEMS = [pltpu.SemaphoreType.DMA((14,)), pltpu.SemaphoreType.DMA((14,)), pltpu.SemaphoreType.DMA((2,))]
_ANY = pl.BlockSpec(memory_space=pl.ANY)


def _exchange_copies(kind, a_ref, b_ref, oa_ref, ob_ref, send_sems, recv_sems, loc_sems):
    peers, me = _peers()
    pairs = ((a_ref, oa_ref), (b_ref, ob_ref))
    local = [pltpu.make_async_copy(src if kind == "gather" else src.at[me], dst.at[me], loc_sems.at[t])
             for t, (src, dst) in enumerate(pairs)]
    remote = []
    for k, (dev, idx) in enumerate(peers):
        for t, (src, dst) in enumerate(pairs):
            remote.append(pltpu.make_async_remote_copy(
                src_ref=src if kind == "gather" else src.at[idx], dst_ref=dst.at[me],
                send_sem=send_sems.at[2 * k + t], recv_sem=recv_sems.at[2 * k + t],
                device_id=dev, device_id_type=pl.DeviceIdType.MESH))
    return local, remote


def _start_exchange(kind, *refs):
    local, remote = _exchange_copies(kind, *refs)
    for cp in local + remote:
        cp.start()


def _wait_exchange(kind, *refs):
    local, remote = _exchange_copies(kind, *refs)
    for cp in remote:
        cp.wait_recv()
    for cp in remote:
        cp.wait_send()
    for cp in local:
        cp.wait()


def _exchange_out_shapes(kind, a, b):
    if kind == "gather":
        return [jax.ShapeDtypeStruct((N_DEV,) + a.shape, a.dtype), jax.ShapeDtypeStruct((N_DEV,) + b.shape, b.dtype)]
    return [jax.ShapeDtypeStruct(a.shape, a.dtype), jax.ShapeDtypeStruct(b.shape, b.dtype)]


def _gather_two_level(a, b, name):
    def body(a_ref, b_ref, ga_ref, gb_ref, send_sems, recv_sems, loc_sems):
        x, y, c = _mesh_pos()
        slot_of = lambda px, py, pc: 4 * px + 2 * py + pc
        me, sib = slot_of(x, y, c), slot_of(x, y, 1 - c)
        chips = [(1 - x, y), (x, 1 - y), (1 - x, 1 - y)]
        pairs = ((a_ref, ga_ref), (b_ref, gb_ref))

        def copy(k, t, slot, to, src=None):
            dst = pairs[t][1].at[slot]
            return pltpu.make_async_remote_copy(
                src_ref=dst if src is None else src, dst_ref=dst, send_sem=send_sems.at[2 * k + t],
                recv_sem=recv_sems.at[2 * k + t], device_id=to, device_id_type=pl.DeviceIdType.MESH)

        local = [pltpu.make_async_copy(src, dst.at[me], loc_sems.at[t]) for t, (src, dst) in enumerate(pairs)]
        first = []
        for t, (src, _) in enumerate(pairs):
            first.append(copy(0, t, me, (x, y, 1 - c), src))
            first += [copy(1 + j, t, me, (*chip, c), src) for j, chip in enumerate(chips)]
        for cp in local + first:
            cp.start()
        passed = []
        for j, chip in enumerate(chips):
            for t in range(2):
                landed = slot_of(*chip, c)
                copy(1 + j, t, landed, (x, y, c)).wait_recv()
                cp = copy(4 + j, t, landed, (x, y, 1 - c))
                cp.start()
                passed.append(cp)
        for t in range(2):
            copy(0, t, sib, (x, y, c)).wait_recv()
            for j, chip in enumerate(chips):
                copy(4 + j, t, slot_of(*chip, 1 - c), (x, y, c)).wait_recv()
        for cp in first + passed:
            cp.wait_send()
        for cp in local:
            cp.wait()

    return pl.pallas_call(
        body, name=name,
        in_specs=[_ANY, _ANY], out_specs=[_ANY, _ANY],
        out_shape=_exchange_out_shapes("gather", a, b),
        scratch_shapes=_EXCHANGE_SEMS,
    )(a, b)


def _adam_math(w, g, m, v):
    m_new = ADAM_B1 * m + (1.0 - ADAM_B1) * g
    v_new = ADAM_B2 * v + (1.0 - ADAM_B2) * (g * g)
    m_hat = m_new / (1.0 - ADAM_B1 ** ADAM_STEP)
    v_hat = v_new / (1.0 - ADAM_B2 ** ADAM_STEP)
    delta = -ADAM_LR * (m_hat / (jnp.sqrt(v_hat) + ADAM_EPS) + ADAM_WD * w)
    return delta, m_new, v_new


def _sum_adamw(gparts, w, m, v, name):
    L, R, C = w.shape
    tr = min(128, R)

    def body(*refs):
        gp_refs = refs[:L]
        w_ref, m_ref, v_ref, g_ref, d_ref, nm_ref, nv_ref = refs[L:]
        for l in range(L):
            g = gp_refs[l][0].astype(F32)
            for s in range(1, N_DEV):
                g = g + gp_refs[l][s].astype(F32)
            d, mn, vn = _adam_math(w_ref[l], g, m_ref[l], v_ref[l])
            g_ref[l] = g
            d_ref[l] = d
            nm_ref[l] = mn
            nv_ref[l] = vn

    blk = pl.BlockSpec((L, tr, C), lambda r: (0, r, 0))
    return pl.pallas_call(
        body, name=name,
        grid=(R // tr,),
        in_specs=[pl.BlockSpec((N_DEV, tr, C), lambda r: (0, r, 0))] * L + [blk, blk, blk],
        out_specs=[blk, blk, blk, blk],
        out_shape=[jax.ShapeDtypeStruct((L, R, C), F32)] * 4,
        compiler_params=_cp(("parallel",), _VMEM_WIDE),
    )(*gparts, w, m, v)


def _small_update(gpack, wpack, mpack, vpack):
    R = gpack.shape[0]
    VM = pl.BlockSpec(memory_space=pltpu.VMEM)

    def body(g_ref, w_ref, m_ref, v_ref, gs_ref, d_ref, nm_ref, nv_ref, buf, send_sems, recv_sems):
        peers, me = _peers()
        buf[me] = g_ref[...]
        copies = []
        for k, (dev, _) in enumerate(peers):
            cp = pltpu.make_async_remote_copy(
                src_ref=g_ref, dst_ref=buf.at[me], send_sem=send_sems.at[k], recv_sem=recv_sems.at[k],
                device_id=dev, device_id_type=pl.DeviceIdType.MESH)
            cp.start()
            copies.append(cp)
        for cp in copies:
            cp.wait_recv()
        for cp in copies:
            cp.wait_send()
        g = buf[0]
        for s in range(1, N_DEV):
            g = g + buf[s]
        d, mn, vn = _adam_math(w_ref[...], g, m_ref[...], v_ref[...])
        gs_ref[...] = g
        d_ref[...] = d
        nm_ref[...] = mn
        nv_ref[...] = vn

    return pl.pallas_call(
        body, name="small_update",
        in_specs=[VM] * 4, out_specs=[VM] * 4,
        out_shape=[jax.ShapeDtypeStruct((R, 128), F32)] * 4,
        scratch_shapes=[pltpu.VMEM((N_DEV, R, 128), F32), pltpu.SemaphoreType.DMA((7,)), pltpu.SemaphoreType.DMA((7,))],
        compiler_params=_cp(None, _VMEM_MID),
    )(gpack, wpack, mpack, vpack)


_SMALL = ("norm_g", "b_f", "q_norm_g", "k_norm_g", "w_pool", "pool_scale")


def _pack(parts):
    flat = jnp.concatenate([p.reshape(-1) for p in parts])
    n = flat.shape[0]
    rows = -(-n // (8 * 128)) * 8
    return jnp.pad(flat, (0, rows * 128 - n)).reshape(rows, 128)


def _unpack(packed, like):
    flat = packed.reshape(-1)
    out, o = [], 0
    for p in like:
        out.append(flat[o:o + p.size].reshape(p.shape))
        o += p.size
    return out


def kernel(x, norm_g, w_in, b_f, q_norm_g, k_norm_g, w_pool, pool_scale, w_out, loss_target, m_norm_g, m_w_in, m_b_f, m_q_norm_g, m_k_norm_g, m_w_pool, m_pool_scale, m_w_out, v_norm_g, v_w_in, v_b_f, v_q_norm_g, v_k_norm_g, v_w_pool, v_pool_scale, v_w_out):
    L = w_in.shape[0]

    loss_local, dx, grads, received = _train_step(x[0], loss_target[0], norm_g, w_in.astype(BF16), b_f, q_norm_g,
                                                  k_norm_g, w_pool, pool_scale, w_out.astype(BF16))
    loss = lax.psum(loss_local, MESH_AXES)
    g = {k: jnp.stack([grads[l][k] for l in range(L)]) for k in _SMALL}

    g_win, d_win, nm_win, nv_win = _sum_adamw([r[0] for r in received], w_in, m_w_in, v_w_in, "adamw_w_in")
    g_wout, d_wout, nm_wout, nv_wout = _sum_adamw([r[1] for r in received], w_out, m_w_out, v_w_out, "adamw_w_out")

    ws = dict(norm_g=norm_g, b_f=b_f, q_norm_g=q_norm_g, k_norm_g=k_norm_g, w_pool=w_pool, pool_scale=pool_scale)
    ms = dict(norm_g=m_norm_g, b_f=m_b_f, q_norm_g=m_q_norm_g, k_norm_g=m_k_norm_g, w_pool=m_w_pool, pool_scale=m_pool_scale)
    vs = dict(norm_g=v_norm_g, b_f=v_b_f, q_norm_g=v_q_norm_g, k_norm_g=v_k_norm_g, w_pool=v_w_pool, pool_scale=v_pool_scale)
    like = [ws[k] for k in _SMALL]
    gs_p, d_p, nm_p, nv_p = _small_update(_pack([g[k] for k in _SMALL]), _pack(like),
                                          _pack([ms[k] for k in _SMALL]), _pack([vs[k] for k in _SMALL]))
    gs = dict(zip(_SMALL, _unpack(gs_p, like)))
    ds = dict(zip(_SMALL, _unpack(d_p, like)))
    nms = dict(zip(_SMALL, _unpack(nm_p, like)))
    nvs = dict(zip(_SMALL, _unpack(nv_p, like)))
    gs["w_in"], ds["w_in"], nms["w_in"], nvs["w_in"] = g_win, d_win, nm_win, nv_win
    gs["w_out"], ds["w_out"], nms["w_out"], nvs["w_out"] = g_wout, d_wout, nm_wout, nv_wout

    order = ("norm_g", "w_in", "b_f", "q_norm_g", "k_norm_g", "w_pool", "pool_scale", "w_out")
    return (loss, dx[None], *[gs[k] for k in order], *[ds[k] for k in order],
            *[nms[k] for k in order], *[nvs[k] for k in order])
```

```python
import jax
import jax.numpy as jnp
from jax import lax
from jax.experimental import pallas as pl
from jax.experimental.pallas import tpu as pltpu

F32 = jnp.float32
BF16 = jnp.bfloat16

EPS = 1e-6
NEG = -1e30
HEAD_DIM = 64
FOX_HEADS = 8
FOX_W = 512
POOL_W = 256
SB_W = 256
D_MIX = 1024
N_FF = 8
N_MAIN = 3584
N_FFPAD = 128
OFF_FQ, OFF_FK, OFF_FV, OFF_FG = 0, 512, 1024, 1536
OFF_PX, OFF_PG = 2048, 2304
OFF_SQ, OFF_SK, OFF_SV, OFF_SG = 2560, 2816, 3072, 3328
D_IN = 3592
Q_SCALE = HEAD_DIM ** -0.5

ADAM_LR = 0.001
ADAM_B1 = 0.9
ADAM_B2 = 0.999
ADAM_EPS = 1e-08
ADAM_WD = 0.01
ADAM_STEP = 10

N_DEV = 8
MESH_AXES = ("x", "y", "c")

_T = 256
_TM = 512
_TM_ROWS = 512
_TM_FWD, _TN_FWD = 2048, 512
_TM_DX = 512
_TK_DW = 1024
_VMEM_V7X = 64 << 20
_VMEM_BIG = _VMEM_V7X - (8 << 20)
_VMEM_MID = 40 << 20
_VMEM_WIDE = 48 << 20


def _cp(sem=None, vmem=None):
    kw = {}
    if sem is not None:
        kw["dimension_semantics"] = sem
    if vmem is not None:
        kw["vmem_limit_bytes"] = vmem
    return pltpu.CompilerParams(**kw)


def _dot(a, b):
    return jnp.dot(a, b, preferred_element_type=F32)


def _dot_nt(a, b):
    return lax.dot_general(a, b, (((1,), (1,)), ((), ())), preferred_element_type=F32)


def _dot_tn(a, b):
    return lax.dot_general(a, b, (((0,), (0,)), ((), ())), preferred_element_type=F32)


def _mm2(v, m, left=False):
    hi = v.astype(BF16)
    lo = (v - hi.astype(F32)).astype(BF16)
    if left:
        return _dot(m, hi) + _dot(m, lo)
    return _dot(hi, m) + _dot(lo, m)


def _mm3(v, m, left=False):
    a1 = v.astype(BF16)
    r1 = v - a1.astype(F32)
    a2 = r1.astype(BF16)
    a3 = (r1 - a2.astype(F32)).astype(BF16)
    if left:
        return _dot(m, a1) + _dot(m, a2) + _dot(m, a3)
    return _dot(a1, m) + _dot(a2, m) + _dot(a3, m)


def _sigmoid(z):
    return 1.0 / (1.0 + jnp.exp(-z))


def _rms_rows(x):
    return lax.rsqrt(jnp.mean(x * x, axis=-1, keepdims=True) + EPS)


def _inproj_fwd(x, g, wm, wff):
    S, D = x.shape
    tm = min(_TM_FWD, S)
    tn = _TN_FWD
    assert OFF_SQ % tn == 0 and N_MAIN - OFF_SQ == 4 * SB_W
    j_sb = OFF_SQ // tn

    def body(x_ref, g_ref, w_ref, wff_ref, o_ref, off_ref, ht_ref, sb_ref, h_ref):
        j = pl.program_id(1)

        @pl.when(j == 0)
        def _():
            xv = x_ref[...]
            h = (xv * _rms_rows(xv)) * g_ref[...]
            h_ref[...] = h.astype(BF16)
            ht_ref[...] = h.T.astype(BF16)
            off_ref[...] = _dot(h_ref[...], wff_ref[...])

        res = _dot(h_ref[...], w_ref[...])
        o_ref[...] = res

        @pl.when(j >= j_sb)
        def _():
            sb_ref[...] = res.astype(BF16)

    return pl.pallas_call(
        body, name="inproj_fwd",
        grid=(S // tm, N_MAIN // tn),
        in_specs=[pl.BlockSpec((tm, D), lambda i, j: (i, 0)),
                  pl.BlockSpec((1, D), lambda i, j: (0, 0)),
                  pl.BlockSpec((D, tn), lambda i, j: (0, j)),
                  pl.BlockSpec((D, N_FFPAD), lambda i, j: (0, 0))],
        out_specs=[pl.BlockSpec((tm, tn), lambda i, j: (i, j)),
                   pl.BlockSpec((tm, N_FFPAD), lambda i, j: (i, 0)),
                   pl.BlockSpec((D, tm), lambda i, j: (0, i)),
                   pl.BlockSpec((tm, tn), lambda i, j: (i, jnp.maximum(j - j_sb, 0)))],
        out_shape=[jax.ShapeDtypeStruct((S, N_MAIN), F32), jax.ShapeDtypeStruct((S, N_FFPAD), F32),
                   jax.ShapeDtypeStruct((D, S), BF16), jax.ShapeDtypeStruct((S, 4 * SB_W), BF16)],
        scratch_shapes=[pltpu.VMEM((tm, D), BF16)],
        compiler_params=_cp(("parallel", "arbitrary"), _VMEM_BIG),
    )(x, g, wm, wff)


def _head_norm(x, g, bd):
    ss = _mm2(x * x, bd)
    r = lax.rsqrt(ss * (1.0 / HEAD_DIM) + EPS)
    return (x * r) * g


def _fox_prep(proj, pff, bfp, gq, gk, bd, ex, tril):
    S = proj.shape[0]
    T = tril.shape[0]

    def body(q_ref, k_ref, ff_ref, b_ref, gq_ref, gk_ref, bd_ref, ex_ref, tri_ref,
             qs_ref, kn_ref, cc_ref, cqb_ref, carry):
        @pl.when(pl.program_id(0) == 0)
        def _():
            carry[...] = jnp.zeros_like(carry)

        bdv = bd_ref[...]
        qs_ref[...] = (_head_norm(q_ref[...], gq_ref[...], bdv) * Q_SCALE).astype(BF16)
        kn_ref[...] = _head_norm(k_ref[...], gk_ref[...], bdv).astype(BF16)
        u = ff_ref[...] + b_ref[...]
        lf = jnp.minimum(u, 0.0) - jnp.log1p(jnp.exp(-jnp.abs(u)))
        c = _mm3(lf, tri_ref[...], left=True) + carry[0:1, :]
        carry[0:1, :] = c[T - 1:T, :]
        cc_ref[...] = c
        cqb_ref[...] = _mm3(c, ex_ref[...])

    return pl.pallas_call(
        body, name="fox_prep",
        grid=(S // T,),
        in_specs=[pl.BlockSpec((T, FOX_W), lambda i: (i, OFF_FQ // FOX_W)),
                  pl.BlockSpec((T, FOX_W), lambda i: (i, OFF_FK // FOX_W)),
                  pl.BlockSpec((T, N_FFPAD), lambda i: (i, 0)),
                  pl.BlockSpec((1, N_FFPAD), lambda i: (0, 0)),
                  pl.BlockSpec((1, FOX_W), lambda i: (0, 0)),
                  pl.BlockSpec((1, FOX_W), lambda i: (0, 0)),
                  pl.BlockSpec((FOX_W, FOX_W), lambda i: (0, 0)),
                  pl.BlockSpec((N_FFPAD, FOX_W), lambda i: (0, 0)),
                  pl.BlockSpec((T, T), lambda i: (0, 0))],
        out_specs=[pl.BlockSpec((T, FOX_W), lambda i: (i, 0)),
                   pl.BlockSpec((T, FOX_W), lambda i: (i, 0)),
                   pl.BlockSpec((T, N_FFPAD), lambda i: (i, 0)),
                   pl.BlockSpec((T, FOX_W), lambda i: (i, 0))],
        out_shape=[jax.ShapeDtypeStruct((S, FOX_W), BF16), jax.ShapeDtypeStruct((S, FOX_W), BF16),
                   jax.ShapeDtypeStruct((S, N_FFPAD), F32), jax.ShapeDtypeStruct((S, FOX_W), F32)],
        scratch_shapes=[pltpu.VMEM((8, N_FFPAD), F32)],
        compiler_params=_cp(("arbitrary",), _VMEM_MID),
    )(proj, proj, pff, bfp, gq, gk, bd, ex, tril)


def _pair_blk(S, off=0):
    return pl.BlockSpec((S, 128), lambda p: (0, off + p), pipeline_mode=pl.Buffered(1))


def _pair_rows(S):
    return pl.BlockSpec((None, 8, S), lambda p: (p, 0, 0), pipeline_mode=pl.Buffered(1))


def _head_masks(S):
    return lax.broadcasted_iota(jnp.int32, (S, 128), 1) < HEAD_DIM


_EXP_ZERO = 104.0


def _spread_heads(x):
    src = lax.broadcasted_iota(jnp.int32, (128, 128), 0)
    return (_mm3(x, (src == 0).astype(BF16)), _mm3(x, (src == HEAD_DIM).astype(BF16)))


def _score_bounds(q, k):
    same_head = ((lax.broadcasted_iota(jnp.int32, (128, 128), 0) < HEAD_DIM)
                 == (lax.broadcasted_iota(jnp.int32, (128, 128), 1) < HEAD_DIM)).astype(BF16)

    def max_norm2(x):
        xf = x.astype(F32)
        return jnp.max(_mm2(xf * xf, same_head), axis=0, keepdims=True)

    z = jnp.sqrt(max_norm2(q) * max_norm2(k))
    z = jnp.where(z == z, z, jnp.inf)
    return jnp.max(z[:, 0:1]) * 1.001 + 1e-3, jnp.max(z[:, 64:65]) * 1.001 + 1e-3


def _for_tiles_back(i, n, tiles_fn, fours=False):
    if fours:
        def four(t, c):
            tiles_fn([i - 1 - 4 * t, i - 2 - 4 * t, i - 3 - 4 * t, i - 4 - 4 * t])
            return c

        lax.fori_loop(0, lax.shift_right_logical(n, 2), four, 0)
        rest = i - (n & ~3)

        @pl.when((n & 2) != 0)
        def _():
            tiles_fn([rest - 1, rest - 2])
    else:
        def two(t, c):
            tiles_fn([i - 1 - 2 * t, i - 2 - 2 * t])
            return c

        lax.fori_loop(0, lax.shift_right_logical(n, 1), two, 0)

    @pl.when((n & 1) != 0)
    def _():
        tiles_fn([i - n])


def _fox_tiles_back(cr_ref, i, r0, zba, zbb):
    last = cr_ref[:, pl.ds(0, 128)]
    first = cr_ref[:, pl.ds(r0, 128)]
    alive_a = 2.0 * zba + first[0:1, 0:1] - last[2:3, :] > -_EXP_ZERO
    alive_b = 2.0 * zbb + first[1:2, 0:1] - last[3:4, :] > -_EXP_ZERO
    before = lax.broadcasted_iota(jnp.int32, (1, 128), 1) < i
    return jnp.sum((before & (alive_a | alive_b)).astype(jnp.int32))


def _fox_fwd(qs, kn, proj, cqb, crow4, ride=None):
    S = qs.shape[0]
    T = min(_T, S)
    nq = S // T
    n_pairs = FOX_W // 128

    def body(*refs):
        if ride is None:
            q_ref, k_ref, v_ref, cq_ref, cr_ref, o_ref, lse_ref = refs[:7]
            qa, qb, vta, vtb, cka, ckb, ma, mb, acca, accb = refs[7:]
        else:
            q_ref, k_ref, v_ref, cq_ref, cr_ref, wa_ref, wb_ref, o_ref, lse_ref, ga_ref, gb_ref = refs[:11]
            qa, qb, vta, vtb, cka, ckb, ma, mb, acca, accb = refs[11:21]
            xrefs = (wa_ref, wb_ref, ga_ref, gb_ref) + tuple(refs[21:])

            @pl.when(pl.program_id(0) == 0)
            def _():
                _start_exchange("gather", *xrefs)

        lane_s = _head_masks(S)
        q = q_ref[...]
        zq = jnp.zeros_like(q)
        qa[...] = jnp.where(lane_s, q, zq)
        qb[...] = jnp.where(lane_s, zq, q)
        cq = cq_ref[...]
        cka[...], ckb[...] = _spread_heads(cq)
        lse_ref[...] = jnp.zeros((8, S), F32)
        row_t = lax.broadcasted_iota(jnp.int32, (128, T), 0) < HEAD_DIM
        zba, zbb = _score_bounds(q, k_ref[...])

        def prep(c, carry):
            c0 = pl.multiple_of(c * T, T)
            vt = v_ref[pl.ds(c0, T), :].T
            vta[:, pl.ds(c0, T)] = jnp.where(row_t, vt, 1.0).astype(BF16)
            vtb[:, pl.ds(c0, T)] = jnp.where(row_t, 1.0, vt).astype(BF16)
            return carry

        lax.fori_loop(0, nq, prep, 0)
        causal = (lax.broadcasted_iota(jnp.int32, (T, T), 0) <= lax.broadcasted_iota(jnp.int32, (T, T), 1))

        heads = ((qa, vta, cka, ma, acca), (qb, vtb, ckb, mb, accb))

        def kv(js, r0, masked):
            cr = cr_ref[:, pl.ds(r0, T)]
            c0s = [pl.multiple_of(j * T, T) for j in js]
            ks = [k_ref[pl.ds(c0, T), :] for c0 in c0s]
            ss = []
            for h, (qr, _, ckr, _, _) in enumerate(heads):
                qh = qr[pl.ds(r0, T), :]
                row = []
                for k, c0 in zip(ks, c0s):
                    s = _dot_nt(k, qh) - jnp.tile(ckr[pl.ds(c0, T), :], (1, T // 128))
                    row.append(jnp.where(causal, s, NEG) if masked else s)
                ss.append(row)
            ms = []
            for h, (row, (_, _, _, mr, _)) in enumerate(zip(ss, heads)):
                top = row[0]
                for s in row[1:]:
                    top = jnp.maximum(top, s)
                m_old = mr[0:1, :]
                ms.append((m_old, jnp.maximum(m_old, jnp.max(top, axis=0, keepdims=True) + cr[h:h + 1, :])))
            ps = [[jnp.exp(s + (cr[h:h + 1, :] - m_new)).astype(BF16) for s in row]
                  for h, (row, (_, m_new)) in enumerate(zip(ss, ms))]
            pvs = []
            for row, (_, vr, _, _, _) in zip(ps, heads):
                pv = _dot(vr[:, pl.ds(c0s[0], T)], row[0])
                for p, c0 in zip(row[1:], c0s[1:]):
                    pv = pv + _dot(vr[:, pl.ds(c0, T)], p)
                pvs.append(pv)
            for pv, (m_old, m_new), (_, _, _, mr, ar) in zip(pvs, ms, heads):
                ar[...] = jnp.exp(m_old - m_new) * ar[...] + pv
                mr[0:1, :] = m_new

        def qblk(i, carry):
            r0 = pl.multiple_of(i * T, T)
            ma[...] = jnp.full((8, T), NEG, F32)
            mb[...] = jnp.full((8, T), NEG, F32)
            acca[...] = jnp.zeros((128, T), F32)
            accb[...] = jnp.zeros((128, T), F32)
            kv([i], r0, True)
            done = _fox_tiles_back(cr_ref, i, r0, zba, zbb)
            _for_tiles_back(i, done, lambda js: kv(js, r0, False), fours=True)
            aa = acca[...]
            ab = accb[...]
            la = aa[64:65, :]
            lb = ab[0:1, :]
            o_ref[pl.ds(r0, T), :] = jnp.where(row_t, aa / la, ab / lb).T
            lse_ref[0:1, pl.ds(r0, T)] = ma[0:1, :] + jnp.log(la)
            lse_ref[1:2, pl.ds(r0, T)] = mb[0:1, :] + jnp.log(lb)
            lse_ref[2:3, pl.ds(r0, T)] = jnp.broadcast_to(done.astype(F32), (1, T))
            return carry

        lax.fori_loop(0, nq, qblk, 0)
        if ride is not None:
            @pl.when(pl.program_id(0) == n_pairs - 1)
            def _():
                _wait_exchange("gather", *xrefs)

    extra = () if ride is None else tuple(ride)
    return pl.pallas_call(
        body, name="fox_fwd" if ride is None else "fox_fwd_gather",
        grid=(n_pairs,),
        in_specs=[_pair_blk(S), _pair_blk(S), _pair_blk(S, OFF_FV // 128), _pair_blk(S), _pair_rows(S)]
        + [_ANY] * len(extra),
        out_specs=[_pair_blk(S), _pair_rows(S)] + [_ANY] * len(extra),
        out_shape=[jax.ShapeDtypeStruct((S, FOX_W), F32), jax.ShapeDtypeStruct((n_pairs, 8, S), F32)]
        + (_exchange_out_shapes("gather", *extra) if extra else []),
        scratch_shapes=[pltpu.VMEM((S, 128), BF16)] * 2 + [pltpu.VMEM((128, S), BF16)] * 2
        + [pltpu.VMEM((S, 128), F32)] * 2 + [pltpu.VMEM((8, T), F32)] * 2 + [pltpu.VMEM((128, T), F32)] * 2
        + (_EXCHANGE_SEMS if extra else []),
        compiler_params=_cp(("arbitrary",), _VMEM_BIG),
    )(qs, kn, proj, cqb, crow4, *extra)


def _softplus_parts(z):
    e = jnp.exp(-jnp.abs(z))
    return e, jnp.maximum(z, 0.0) + jnp.log(1.0 + e)


def _sb_fwd(psb, triu):
    S = psb.shape[0]
    T = triu.shape[0]
    nq = S // T

    n_pairs = SB_W // 128
    H = 2 * n_pairs

    def body(q_ref, k_ref, v_ref, tri_ref, o_ref, lt_ref, qm, vt, rr, acc):
        lane_s = _head_masks(S)
        zbs = []
        for p in range(n_pairs):
            q = (q_ref[:, 128 * p:128 * (p + 1)].astype(F32) * Q_SCALE).astype(BF16)
            zq = jnp.zeros_like(q)
            qm[2 * p] = jnp.where(lane_s, q, zq)
            qm[2 * p + 1] = jnp.where(lane_s, zq, q)
            zbs += list(_score_bounds(q, k_ref[:, 128 * p:128 * (p + 1)]))
        lt_ref[...] = jnp.zeros((n_pairs, 8, S), F32)
        row_t = lax.broadcasted_iota(jnp.int32, (128, T), 0) < HEAD_DIM

        def prep(c, carry):
            c0 = pl.multiple_of(c * T, T)
            for p in range(n_pairs):
                vt[p, :, pl.ds(c0, T)] = v_ref[pl.ds(c0, T), 128 * p:128 * (p + 1)].astype(F32).T.astype(BF16)
            return carry

        lax.fori_loop(0, nq, prep, 0)
        strict = (lax.broadcasted_iota(jnp.int32, (T, T), 0) < lax.broadcasted_iota(jnp.int32, (T, T), 1))

        def kv(tiles, r0):
            tri = tri_ref[...]
            c0s = [pl.multiple_of(j * T, T) for j, _ in tiles]
            zs = [[_dot_nt(k_ref[pl.ds(c0, T), 128 * (h // 2):128 * (h // 2 + 1)], qm[h, pl.ds(r0, T), :])
                   for c0 in c0s] for h in range(H)]
            lbs = [[jnp.where(strict, -_softplus_parts(z)[1], 0.0) if masked else -_softplus_parts(z)[1]
                    for z, (_, masked) in zip(row, tiles)] for row in zs]
            incs = [[_mm2(lb, tri, left=True) for lb in row] for row in lbs]
            avs = []
            for h in range(H):
                r = rr[h, 0:1, :]
                av = None
                for z, inc, c0, (_, masked) in zip(zs[h], incs[h], c0s, tiles):
                    a = jnp.exp(z + inc + r)
                    if masked:
                        a = jnp.where(strict, a, 0.0)
                    term = _dot(vt[h // 2, :, pl.ds(c0, T)], a.astype(BF16))
                    av = term if av is None else av + term
                    r = r + inc[0:1, :]
                avs.append((av, r))
            for h, (av, r) in enumerate(avs):
                rr[h, 0:1, :] = r
                acc[h] = acc[h] + av

        def qblk(i, carry):
            r0 = pl.multiple_of(i * T, T)
            rr[...] = jnp.zeros((H, 8, T), F32)
            acc[...] = jnp.zeros((H, 128, T), F32)

            @pl.when(i == 0)
            def _():
                kv([(i, True)], r0)

            @pl.when(i > 0)
            def _():
                kv([(i, True), (i - 1, False)], r0)

            def alive():
                m = jnp.max(rr[0, 0:1, :]) + zbs[0]
                for h in range(1, H):
                    m = jnp.maximum(m, jnp.max(rr[h, 0:1, :]) + zbs[h])
                return m > -_EXP_ZERO

            def cond(st):
                return (st[0] < i) & st[1]

            def step(st):
                kv([(i - 1 - st[0], False)], r0)
                return st[0] + 1, alive()

            done, _ = lax.while_loop(cond, step, (jnp.minimum(i, 1), alive()))
            for p in range(n_pairs):
                o_ref[pl.ds(r0, T), 128 * p:128 * (p + 1)] = jnp.where(row_t, acc[2 * p], acc[2 * p + 1]).T
                lt_ref[p, 0:1, pl.ds(r0, T)] = rr[2 * p, 0:1, :]
                lt_ref[p, 1:2, pl.ds(r0, T)] = rr[2 * p + 1, 0:1, :]
                lt_ref[p, 2:3, pl.ds(r0, T)] = jnp.broadcast_to(done.astype(F32), (1, T))
            return carry

        lax.fori_loop(0, nq, qblk, 0)

    wide = lambda off: pl.BlockSpec((S, SB_W), lambda g: (0, off), pipeline_mode=pl.Buffered(1))
    return pl.pallas_call(
        body, name="sb_fwd",
        grid=(1,),
        in_specs=[wide(0), wide(1), wide(2), pl.BlockSpec((T, T), lambda g: (0, 0))],
        out_specs=[wide(0), pl.BlockSpec((n_pairs, 8, S), lambda g: (0, 0, 0), pipeline_mode=pl.Buffered(1))],
        out_shape=[jax.ShapeDtypeStruct((S, SB_W), F32), jax.ShapeDtypeStruct((n_pairs, 8, S), F32)],
        scratch_shapes=[pltpu.VMEM((H, S, 128), BF16), pltpu.VMEM((n_pairs, 128, S), BF16),
                        pltpu.VMEM((H, 8, T), F32), pltpu.VMEM((H, 128, T), F32)],
        compiler_params=_cp(("arbitrary",), _VMEM_BIG),
    )(psb, psb, psb, triu)


def _pool_window_lanes(shape):
    lane = lax.broadcasted_iota(jnp.int32, shape, 1)
    return jnp.where(lane < 64, 2, jnp.where(lane < 128, 4, jnp.where(lane < 192, 8, 16)))


def _pool_fwd(proj):
    S = proj.shape[0]

    def body(x_ref, o_ref):
        x = x_ref[...]
        t = lax.broadcasted_iota(jnp.int32, x.shape, 0)
        lane = lax.broadcasted_iota(jnp.int32, x.shape, 1)

        def back(a, k):
            return jnp.where(t >= k, pltpu.roll(a, k, 0), 0.0)

        s1 = x + back(x, 1)
        s2 = s1 + back(s1, 2)
        s4 = s2 + back(s2, 4)
        s8 = s4 + back(s4, 8)
        win = jnp.where(lane < 64, s1, jnp.where(lane < 128, s2, jnp.where(lane < 192, s4, s8)))
        cnt = jnp.minimum(t + 1, _pool_window_lanes(x.shape)).astype(F32)
        o_ref[...] = win / cnt - x

    return pl.pallas_call(
        body, name="pool_fwd",
        grid=(1,),
        in_specs=[pl.BlockSpec((S, POOL_W), lambda i: (0, OFF_PX // POOL_W))],
        out_specs=pl.BlockSpec((S, POOL_W), lambda i: (0, 0)),
        out_shape=jax.ShapeDtypeStruct((S, POOL_W), F32),
        compiler_params=_cp(("arbitrary",), _VMEM_BIG),
    )(proj)


def _silu(g):
    return g * _sigmoid(g)


def _mix_out(fo, so, pooled, proj, wbd, scale, wout, x):
    S, D = x.shape
    tm = min(_TM_ROWS, S)

    def body(fo_ref, fg_ref, so_ref, sg_ref, pl_ref, pg_ref, wbd_ref, sc_ref, w_ref, x_ref, y_ref, mxt_ref, mx_ref):
        parts = ((0, fo_ref[...] * _silu(fg_ref[...])),
                 (FOX_W, (_dot(pl_ref[...].astype(BF16), wbd_ref[...]) * sc_ref[...]) * _silu(pg_ref[...])),
                 (FOX_W + POOL_W, so_ref[...] * _silu(sg_ref[...])))
        for off, part in parts:
            w = part.shape[1]
            mx_ref[:, off:off + w] = part.astype(BF16)
            mxt_ref[off:off + w, :] = part.T.astype(BF16)
        y_ref[...] = x_ref[...] + _dot(mx_ref[...], w_ref[...])

    return pl.pallas_call(
        body, name="mix_out",
        grid=(S // tm,),
        in_specs=[pl.BlockSpec((tm, FOX_W), lambda i: (i, 0)),
                  pl.BlockSpec((tm, FOX_W), lambda i: (i, OFF_FG // FOX_W)),
                  pl.BlockSpec((tm, SB_W), lambda i: (i, 0)),
                  pl.BlockSpec((tm, SB_W), lambda i: (i, OFF_SG // SB_W)),
                  pl.BlockSpec((tm, POOL_W), lambda i: (i, 0)),
                  pl.BlockSpec((tm, POOL_W), lambda i: (i, OFF_PG // POOL_W)),
                  pl.BlockSpec((POOL_W, POOL_W), lambda i: (0, 0)),
                  pl.BlockSpec((1, POOL_W), lambda i: (0, 0)),
                  pl.BlockSpec((D_MIX, D), lambda i: (0, 0)),
                  pl.BlockSpec((tm, D), lambda i: (i, 0))],
        out_specs=[pl.BlockSpec((tm, D), lambda i: (i, 0)), pl.BlockSpec((D_MIX, tm), lambda i: (0, i))],
        out_shape=[jax.ShapeDtypeStruct((S, D), F32), jax.ShapeDtypeStruct((D_MIX, S), BF16)],
        scratch_shapes=[pltpu.VMEM((tm, D_MIX), BF16)],
        compiler_params=_cp(("parallel",), _VMEM_MID),
    )(fo, proj, so, proj, pooled, proj, wbd, scale, wout, x)


def _loss_head(y, target):
    S, D = y.shape
    tm = min(_TM, S)

    def body(y_ref, t_ref, dy_ref, ls_ref):
        @pl.when(pl.program_id(0) == 0)
        def _():
            ls_ref[...] = jnp.zeros_like(ls_ref)

        e = y_ref[...] - t_ref[...]
        dy_ref[...] = e * (1.0 / D)
        ls_ref[...] = ls_ref[...] + jnp.sum(e * e) * (0.5 / D)

    dy, ls = pl.pallas_call(
        body, name="loss_head",
        grid=(S // tm,),
        in_specs=[pl.BlockSpec((tm, D), lambda i: (i, 0)), pl.BlockSpec((tm, D), lambda i: (i, 0))],
        out_specs=[pl.BlockSpec((tm, D), lambda i: (i, 0)), pl.BlockSpec((8, 128), lambda i: (0, 0))],
        out_shape=[jax.ShapeDtypeStruct((S, D), F32), jax.ShapeDtypeStruct((8, 128), F32)],
        compiler_params=_cp(("arbitrary",), _VMEM_MID),
    )(y, target)
    return dy, ls[0, 0]


def _dsilu(g):
    s = _sigmoid(g)
    return s * (1.0 + g * (1.0 - s))


def _gate_bwd(dy, wout, fo, so, pooled, proj, wbd, scale):
    S, D = dy.shape
    tm = min(_TM_ROWS, S)

    def body(dy_ref, w_ref, fo_ref, fg_ref, so_ref, sg_ref, pl_ref, pg_ref, wbd_ref, sc_ref,
             dfo_ref, dfg_ref, dso_ref, dsg_ref, dpg_ref, dpl_ref, dsc_ref, dwbd_ref):
        @pl.when(pl.program_id(0) == 0)
        def _():
            dsc_ref[...] = jnp.zeros_like(dsc_ref)
            dwbd_ref[...] = jnp.zeros_like(dwbd_ref)

        dm = _dot_nt(dy_ref[...].astype(BF16), w_ref[...])
        dmf = dm[:, 0:FOX_W]
        dmp = dm[:, FOX_W:FOX_W + POOL_W]
        dms = dm[:, FOX_W + POOL_W:D_MIX]
        fg = fg_ref[...]
        dfo_ref[...] = dmf * _silu(fg)
        dfg_ref[...] = (dmf * fo_ref[...] * _dsilu(fg)).astype(BF16)
        sg = sg_ref[...]
        dso_ref[...] = (dms * _silu(sg)).astype(BF16)
        dsg_ref[...] = (dms * so_ref[...] * _dsilu(sg)).astype(BF16)
        pg = pg_ref[...]
        plb = pl_ref[...].astype(BF16)
        yw = _dot(plb, wbd_ref[...])
        sc = sc_ref[...]
        dpg_ref[...] = (dmp * (yw * sc) * _dsilu(pg)).astype(BF16)
        dys = dmp * _silu(pg)
        dsc_ref[...] = dsc_ref[...] + jnp.sum(dys * yw, axis=0, keepdims=True)
        dyw = (dys * sc).astype(BF16)
        dpl_ref[...] = _dot_nt(dyw, wbd_ref[...])
        dwbd_ref[...] = dwbd_ref[...] + _dot_tn(plb, dyw)

    return pl.pallas_call(
        body, name="gate_bwd",
        grid=(S // tm,),
        in_specs=[pl.BlockSpec((tm, D), lambda i: (i, 0)),
                  pl.BlockSpec((D_MIX, D), lambda i: (0, 0)),
                  pl.BlockSpec((tm, FOX_W), lambda i: (i, 0)),
                  pl.BlockSpec((tm, FOX_W), lambda i: (i, OFF_FG // FOX_W)),
                  pl.BlockSpec((tm, SB_W), lambda i: (i, 0)),
                  pl.BlockSpec((tm, SB_W), lambda i: (i, OFF_SG // SB_W)),
                  pl.BlockSpec((tm, POOL_W), lambda i: (i, 0)),
                  pl.BlockSpec((tm, POOL_W), lambda i: (i, OFF_PG // POOL_W)),
                  pl.BlockSpec((POOL_W, POOL_W), lambda i: (0, 0)),
                  pl.BlockSpec((1, POOL_W), lambda i: (0, 0))],
        out_specs=[pl.BlockSpec((tm, FOX_W), lambda i: (i, 0)),
                   pl.BlockSpec((tm, FOX_W), lambda i: (i, 0)),
                   pl.BlockSpec((tm, SB_W), lambda i: (i, 0)),
                   pl.BlockSpec((tm, SB_W), lambda i: (i, 0)),
                   pl.BlockSpec((tm, POOL_W), lambda i: (i, 0)),
                   pl.BlockSpec((tm, POOL_W), lambda i: (i, 0)),
                   pl.BlockSpec((1, POOL_W), lambda i: (0, 0)),
                   pl.BlockSpec((POOL_W, POOL_W), lambda i: (0, 0))],
        out_shape=[jax.ShapeDtypeStruct((S, FOX_W), F32), jax.ShapeDtypeStruct((S, FOX_W), BF16),
                   jax.ShapeDtypeStruct((S, SB_W), BF16), jax.ShapeDtypeStruct((S, SB_W), BF16),
                   jax.ShapeDtypeStruct((S, POOL_W), BF16), jax.ShapeDtypeStruct((S, POOL_W), F32),
                   jax.ShapeDtypeStruct((1, POOL_W), F32), jax.ShapeDtypeStruct((POOL_W, POOL_W), F32)],
        compiler_params=_cp(("arbitrary",), _VMEM_MID),
    )(dy, wout, fo, proj, so, proj, pooled, proj, wbd, scale)


def _matmul_acc(at, b, name):
    M, S = at.shape
    N = b.shape[1]
    tk = min(_TK_DW, S)
    tn = min(512, N)
    nk = S // tk

    def body(a_ref, b_ref, o_ref, acc):
        k = pl.program_id(1)

        @pl.when(k == 0)
        def _():
            acc[...] = jnp.zeros_like(acc)

        acc[...] = acc[...] + _dot(a_ref[...], b_ref[...].astype(BF16))

        @pl.when(k == nk - 1)
        def _():
            o_ref[...] = acc[...].astype(BF16)

    return pl.pallas_call(
        body, name=name,
        grid=(N // tn, nk),
        in_specs=[pl.BlockSpec((M, tk), lambda j, k: (0, k)), pl.BlockSpec((tk, tn), lambda j, k: (k, j))],
        out_specs=pl.BlockSpec((M, tn), lambda j, k: (0, j)),
        out_shape=jax.ShapeDtypeStruct((M, N), BF16),
        scratch_shapes=[pltpu.VMEM((M, tn), F32)],
        compiler_params=_cp(("parallel", "arbitrary"), _VMEM_MID),
    )(at, b)


def _pool_bwd(dpooled):
    S = dpooled.shape[0]

    def body(d_ref, o_ref):
        d = d_ref[...]
        t = lax.broadcasted_iota(jnp.int32, d.shape, 0)
        lane = lax.broadcasted_iota(jnp.int32, d.shape, 1)
        cnt = jnp.minimum(t + 1, _pool_window_lanes(d.shape)).astype(F32)
        u = d / cnt

        def fwd(a, k):
            return jnp.where(t < S - k, pltpu.roll(a, S - k, 0), 0.0)

        s1 = u + fwd(u, 1)
        s2 = s1 + fwd(s1, 2)
        s4 = s2 + fwd(s2, 4)
        s8 = s4 + fwd(s4, 8)
        win = jnp.where(lane < 64, s1, jnp.where(lane < 128, s2, jnp.where(lane < 192, s4, s8)))
        o_ref[...] = (win - d).astype(BF16)

    return pl.pallas_call(
        body, name="pool_bwd",
        grid=(1,),
        in_specs=[pl.BlockSpec((S, POOL_W), lambda i: (0, 0))],
        out_specs=pl.BlockSpec((S, POOL_W), lambda i: (0, 0)),
        out_shape=jax.ShapeDtypeStruct((S, POOL_W), BF16),
        compiler_params=_cp(("arbitrary",), _VMEM_BIG),
    )(dpooled)


def _fox_bwd(qs, kn, proj, dfo, fo, lse, cqb, crow4, ride=None):
    S = qs.shape[0]
    T = min(_T, S)
    nq = S // T
    n_pairs = FOX_W // 128

    def body(*refs):
        if ride is None:
            q_ref, k_ref, v_ref, do_ref, o_ref, lse_ref, cq_ref, cr_ref = refs[:8]
            dq_ref, dk_ref, dv_ref, dck_ref, dcq_ref = refs[8:13]
            scr = refs[13:]
        else:
            q_ref, k_ref, v_ref, do_ref, o_ref, lse_ref, cq_ref, cr_ref, pa_ref, pb_ref = refs[:10]
            dq_ref, dk_ref, dv_ref, dck_ref, dcq_ref, ra_ref, rb_ref = refs[10:17]
            scr = refs[17:32]
            xrefs = (pa_ref, pb_ref, ra_ref, rb_ref) + tuple(refs[32:])

            @pl.when(pl.program_id(0) == 0)
            def _():
                _start_exchange("scatter", *xrefs)

        qa, qb, kta, ktb, vb, doa, dob, cka, ckb, dcka, dckb, dva, dqt, dcqa, dcqb = scr
        lane_s = _head_masks(S)
        q = q_ref[...]
        zq = jnp.zeros_like(q)
        qa[...] = jnp.where(lane_s, q, zq)
        qb[...] = jnp.where(lane_s, zq, q)
        vb[...] = v_ref[...].astype(BF16)
        do = do_ref[...].astype(BF16)
        doa[...] = jnp.where(lane_s, do, zq)
        dob[...] = jnp.where(lane_s, zq, do)
        cq = cq_ref[...]
        cka[...], ckb[...] = _spread_heads(cq)
        zs = jnp.zeros((S, 128), F32)
        dk_ref[...] = zs
        dva[...] = zs
        dcka[...] = zs
        dckb[...] = zs
        dcq_ref[...] = jnp.zeros((8, S), F32)
        row_t = lax.broadcasted_iota(jnp.int32, (128, T), 0) < HEAD_DIM

        def prep(c, carry):
            c0 = pl.multiple_of(c * T, T)
            kt = k_ref[pl.ds(c0, T), :].astype(F32).T
            kta[:, pl.ds(c0, T)] = jnp.where(row_t, kt, 0.0).astype(BF16)
            ktb[:, pl.ds(c0, T)] = jnp.where(row_t, 0.0, kt).astype(BF16)
            return carry

        lax.fori_loop(0, nq, prep, 0)
        causal = (lax.broadcasted_iota(jnp.int32, (T, T), 0) <= lax.broadcasted_iota(jnp.int32, (T, T), 1))

        heads = ((qa, kta, doa, cka, dcka, dcqa), (qb, ktb, dob, ckb, dckb, dcqb))

        def kv(js, r0, lss, dls, masked):
            cr = cr_ref[:, pl.ds(r0, T)]
            c0s = [pl.multiple_of(j * T, T) for j in js]
            ks = [k_ref[pl.ds(c0, T), :] for c0 in c0s]
            vs = [vb[pl.ds(c0, T), :] for c0 in c0s]
            qhs = [hd[0][pl.ds(r0, T), :] for hd in heads]
            dohs = [hd[2][pl.ds(r0, T), :] for hd in heads]
            ss = []
            for h, hd in enumerate(heads):
                row = []
                for k, c0 in zip(ks, c0s):
                    s = _dot_nt(k, qhs[h]) - jnp.tile(hd[3][pl.ds(c0, T), :], (1, T // 128))
                    row.append(jnp.where(causal, s, NEG) if masked else s)
                ss.append(row)
            ps = [[jnp.exp(s + (cr[h:h + 1, :] - lss[h])) for s in row] for h, row in enumerate(ss)]
            dps = [[_dot_nt(v, dohs[h]) for v in vs] for h in range(2)]
            dss = [[p * (dp - dls[h]) for p, dp in zip(ps[h], dps[h])] for h in range(2)]
            pbs = [[p.astype(BF16) for p in row] for row in ps]
            dsbs = [[ds.astype(BF16) for ds in row] for row in dss]
            for t, c0 in enumerate(c0s):
                dva[pl.ds(c0, T), :] = dva[pl.ds(c0, T), :] + (_dot(pbs[0][t], dohs[0]) + _dot(pbs[1][t], dohs[1]))
                dk_ref[pl.ds(c0, T), :] = dk_ref[pl.ds(c0, T), :] + (_dot(dsbs[0][t], qhs[0]) + _dot(dsbs[1][t], qhs[1]))
            dq = None
            for h, hd in enumerate(heads):
                for t, c0 in enumerate(c0s):
                    term = _dot(hd[1][:, pl.ds(c0, T)], dsbs[h][t])
                    dq = term if dq is None else dq + term
            dqt[...] = dqt[...] + dq
            for h, hd in enumerate(heads):
                col = jnp.sum(dss[h][0], axis=0, keepdims=True)
                for ds in dss[h][1:]:
                    col = col + jnp.sum(ds, axis=0, keepdims=True)
                hd[5][0:1, :] = hd[5][0:1, :] + col
                for ds, c0 in zip(dss[h], c0s):
                    fold = ds[:, 0:128]
                    for u in range(1, T // 128):
                        fold = fold + ds[:, 128 * u:128 * (u + 1)]
                    hd[4][pl.ds(c0, T), :] = hd[4][pl.ds(c0, T), :] - fold

        def qblk(i, carry):
            r0 = pl.multiple_of(i * T, T)
            dt = (do_ref[pl.ds(r0, T), :] * o_ref[pl.ds(r0, T), :]).T
            dla = jnp.sum(jnp.where(row_t, dt, 0.0), axis=0, keepdims=True)
            dlb = jnp.sum(jnp.where(row_t, 0.0, dt), axis=0, keepdims=True)
            ls = lse_ref[:, pl.ds(r0, T)]
            lss = (ls[0:1, :], ls[1:2, :])
            back = jnp.max(ls[2:3, :]).astype(jnp.int32)
            dqt[...] = jnp.zeros((128, T), F32)
            dcqa[...] = jnp.zeros((8, T), F32)
            dcqb[...] = jnp.zeros((8, T), F32)
            kv([i], r0, lss, (dla, dlb), True)
            _for_tiles_back(i, back, lambda js: kv(js, r0, lss, (dla, dlb), False), fours=True)
            dq_ref[pl.ds(r0, T), :] = dqt[...].T
            dcq_ref[0:1, pl.ds(r0, T)] = dcqa[0:1, :]
            dcq_ref[1:2, pl.ds(r0, T)] = dcqb[0:1, :]
            return carry

        lax.fori_loop(0, nq, qblk, 0)
        dv_ref[...] = dva[...].astype(BF16)
        dck_ref[...] = jnp.where(lane_s, jnp.sum(dcka[...], axis=1, keepdims=True),
                                 jnp.sum(dckb[...], axis=1, keepdims=True))
        if ride is not None:
            @pl.when(pl.program_id(0) == n_pairs - 1)
            def _():
                _wait_exchange("scatter", *xrefs)

    extra = () if ride is None else tuple(ride)
    return pl.pallas_call(
        body, name="fox_bwd" if ride is None else "fox_bwd_exchange",
        grid=(n_pairs,),
        in_specs=[_pair_blk(S), _pair_blk(S), _pair_blk(S, OFF_FV // 128), _pair_blk(S), _pair_blk(S),
                  _pair_rows(S), _pair_blk(S), _pair_rows(S)] + [_ANY] * len(extra),
        out_specs=[_pair_blk(S), _pair_blk(S), _pair_blk(S), _pair_blk(S), _pair_rows(S)] + [_ANY] * len(extra),
        out_shape=[jax.ShapeDtypeStruct((S, FOX_W), F32), jax.ShapeDtypeStruct((S, FOX_W), F32),
                   jax.ShapeDtypeStruct((S, FOX_W), BF16), jax.ShapeDtypeStruct((S, FOX_W), F32),
                   jax.ShapeDtypeStruct((n_pairs, 8, S), F32)]
        + (_exchange_out_shapes("scatter", *extra) if extra else []),
        scratch_shapes=[pltpu.VMEM((S, 128), BF16)] * 2 + [pltpu.VMEM((128, S), BF16)] * 2
        + [pltpu.VMEM((S, 128), BF16)] * 3 + [pltpu.VMEM((S, 128), F32)] * 5
        + [pltpu.VMEM((128, T), F32)] + [pltpu.VMEM((8, T), F32)] * 2
        + (_EXCHANGE_SEMS if extra else []),
        compiler_params=_cp(("arbitrary",), _VMEM_BIG),
    )(qs, kn, proj, dfo, fo, lse, cqb, crow4, *extra)


def _sb_bwd(psb, dso, ltot, tril):
    S = psb.shape[0]
    T = tril.shape[0]
    nq = S // T
    n_pairs = SB_W // 128
    H = 2 * n_pairs

    def body(q_ref, k_ref, v_ref, do_ref, lt_ref, tri_ref, dq_ref, dk_ref, dv_ref,
             qm, kt, dka, dva, dqt, rr, gg):
        lane_s = _head_masks(S)
        for p in range(n_pairs):
            q = (q_ref[:, 128 * p:128 * (p + 1)].astype(F32) * Q_SCALE).astype(BF16)
            zq = jnp.zeros_like(q)
            qm[2 * p] = jnp.where(lane_s, q, zq)
            qm[2 * p + 1] = jnp.where(lane_s, zq, q)
        dka[...] = jnp.zeros((n_pairs, S, 128), F32)
        dva[...] = jnp.zeros((n_pairs, S, 128), F32)
        row_t = lax.broadcasted_iota(jnp.int32, (128, T), 0) < HEAD_DIM
        lane_t = lax.broadcasted_iota(jnp.int32, (T, 128), 1) < HEAD_DIM

        def prep(c, carry):
            c0 = pl.multiple_of(c * T, T)
            for p in range(n_pairs):
                kt[p, :, pl.ds(c0, T)] = k_ref[pl.ds(c0, T), 128 * p:128 * (p + 1)].astype(F32).T.astype(BF16)
            return carry

        lax.fori_loop(0, nq, prep, 0)
        strict = (lax.broadcasted_iota(jnp.int32, (T, T), 0) < lax.broadcasted_iota(jnp.int32, (T, T), 1))

        def own(x, h, mask):
            z = jnp.zeros_like(x)
            return jnp.where(mask, x, z) if h % 2 == 0 else jnp.where(mask, z, x)

        def pair(ref, p, c0):
            return ref[pl.ds(c0, T), 128 * p:128 * (p + 1)]

        def kv(tiles, r0, lts):
            tri = tri_ref[...]
            c0s = [pl.multiple_of(j * T, T) for j, _ in tiles]
            qhs = [qm[h, pl.ds(r0, T), :] for h in range(H)]
            dohs = [own(pair(do_ref, h // 2, r0), h, lane_t) for h in range(H)]
            zs = [[_dot_nt(pair(k_ref, h // 2, c0), qhs[h]) for c0 in c0s] for h in range(H)]
            das = [[_dot_nt(pair(v_ref, h // 2, c0), dohs[h]) for c0 in c0s] for h in range(H)]
            es, lbs = [], []
            for row in zs:
                erow, lrow = [], []
                for z, (_, masked) in zip(row, tiles):
                    e, sp = _softplus_parts(z)
                    erow.append(e)
                    lrow.append(jnp.where(strict, -sp, 0.0) if masked else -sp)
                es.append(erow)
                lbs.append(lrow)
            pres = [[_mm2(lb, tri, left=True) for lb in row] for row in lbs]
            aas, r_ends = [], []
            for h in range(H):
                r = rr[h, 0:1, :]
                arow = []
                for z, lb, pre, (_, masked) in zip(zs[h], lbs[h], pres[h], tiles):
                    a = jnp.exp(z + lb + ((lts[h] - r) - pre))
                    arow.append(jnp.where(strict, a, 0.0) if masked else a)
                    r = r + pre[T - 1:T, :]
                aas.append(arow)
                r_ends.append(r)
            gs = [[a * da for a, da in zip(arow, drow)] for arow, drow in zip(aas, das)]
            gpres = [[_mm2(g, tri, left=True) for g in row] for row in gs]
            dzbs, g_ends = [], []
            for h in range(H):
                gc = gg[h, 0:1, :]
                drow = []
                for z, e, g, gpre, (_, masked) in zip(zs[h], es[h], gs[h], gpres[h], tiles):
                    inv = 1.0 / (1.0 + e)
                    pos = z >= 0.0
                    sig = jnp.where(pos, 1.0, e) * inv
                    oms = jnp.where(pos, e, 1.0) * inv
                    dz = g * oms - sig * (gc + (gpre - g))
                    if masked:
                        dz = jnp.where(strict, dz, 0.0)
                    drow.append(dz.astype(BF16))
                    gc = gc + gpre[T - 1:T, :]
                dzbs.append(drow)
                g_ends.append(gc)
            for p in range(n_pairs):
                a, b = 2 * p, 2 * p + 1
                dq = None
                for h in (a, b):
                    for t, c0 in enumerate(c0s):
                        term = _dot(own(kt[p, :, pl.ds(c0, T)], h, row_t), dzbs[h][t])
                        dq = term if dq is None else dq + term
                dqt[p] = dqt[p] + dq
                for t, c0 in enumerate(c0s):
                    dka[p, pl.ds(c0, T), :] = dka[p, pl.ds(c0, T), :] + (_dot(dzbs[a][t], qhs[a]) + _dot(dzbs[b][t], qhs[b]))
                    dva[p, pl.ds(c0, T), :] = dva[p, pl.ds(c0, T), :] + (_dot(aas[a][t].astype(BF16), dohs[a])
                                                                      + _dot(aas[b][t].astype(BF16), dohs[b]))
            for h in range(H):
                rr[h, 0:1, :] = r_ends[h]
                gg[h, 0:1, :] = g_ends[h]

        def qblk(i, carry):
            r0 = pl.multiple_of(i * T, T)
            lts = []
            for p in range(n_pairs):
                lt = lt_ref[p, :, pl.ds(r0, T)]
                lts += [lt[0:1, :], lt[1:2, :]]
            back = jnp.max(lt_ref[0, 2:3, pl.ds(r0, T)]).astype(jnp.int32)
            dqt[...] = jnp.zeros((n_pairs, 128, T), F32)
            rr[...] = jnp.zeros((H, 8, T), F32)
            gg[...] = jnp.zeros((H, 8, T), F32)

            def inner(j, c):
                kv([(j, False)], r0, lts)
                return c

            @pl.when(back == 0)
            def _():
                kv([(i, True)], r0, lts)

            @pl.when(back > 0)
            def _():
                lax.fori_loop(i - back, i - 1, inner, 0)
                kv([(i - 1, False), (i, True)], r0, lts)

            for p in range(n_pairs):
                dq_ref[pl.ds(r0, T), 128 * p:128 * (p + 1)] = (dqt[p] * Q_SCALE).T.astype(BF16)
            return carry

        lax.fori_loop(0, nq, qblk, 0)
        for p in range(n_pairs):
            dk_ref[:, 128 * p:128 * (p + 1)] = dka[p].astype(BF16)
            dv_ref[:, 128 * p:128 * (p + 1)] = dva[p].astype(BF16)

    wide = lambda off: pl.BlockSpec((S, SB_W), lambda g: (0, off), pipeline_mode=pl.Buffered(1))
    return pl.pallas_call(
        body, name="sb_bwd",
        grid=(1,),
        in_specs=[wide(0), wide(1), wide(2), wide(0),
                  pl.BlockSpec((n_pairs, 8, S), lambda g: (0, 0, 0), pipeline_mode=pl.Buffered(1)),
                  pl.BlockSpec((T, T), lambda g: (0, 0))],
        out_specs=[wide(0), wide(0), wide(0)],
        out_shape=[jax.ShapeDtypeStruct((S, SB_W), BF16)] * 3,
        scratch_shapes=[pltpu.VMEM((H, S, 128), BF16), pltpu.VMEM((n_pairs, 128, S), BF16),
                        pltpu.VMEM((n_pairs, S, 128), F32), pltpu.VMEM((n_pairs, S, 128), F32),
                        pltpu.VMEM((n_pairs, 128, T), F32), pltpu.VMEM((H, 8, T), F32), pltpu.VMEM((H, 8, T), F32)],
        compiler_params=_cp(("arbitrary",), _VMEM_BIG),
    )(psb, psb, psb, dso, ltot, tril)


def _head_norm_bwd(x, g, dy, bd):
    ss = _mm2(x * x, bd)
    r = lax.rsqrt(ss * (1.0 / HEAD_DIM) + EPS)
    xr = x * r
    gdy = g * dy
    m = _mm2(xr * gdy, bd) * (1.0 / HEAD_DIM)
    return r * (gdy - xr * m), dy * xr


def _qk_bwd(dqs, dkn, proj, pff, bfp, gq, gk, bd, dccol, triu):
    S = proj.shape[0]
    T = triu.shape[0]
    n = S // T
    rev = lambda col: (lambda i: (n - 1 - i, col))

    def body(dq_ref, dk_ref, q_ref, k_ref, ff_ref, b_ref, gq_ref, gk_ref, bd_ref, dc_ref, tri_ref,
             dfq_ref, dfk_ref, dff_ref, dgq_ref, dgk_ref, dbf_ref, carry):
        @pl.when(pl.program_id(0) == 0)
        def _():
            carry[...] = jnp.zeros_like(carry)
            dgq_ref[...] = jnp.zeros_like(dgq_ref)
            dgk_ref[...] = jnp.zeros_like(dgk_ref)
            dbf_ref[...] = jnp.zeros_like(dbf_ref)

        bdv = bd_ref[...]
        dxq, gq_rows = _head_norm_bwd(q_ref[...], gq_ref[...], dq_ref[...] * Q_SCALE, bdv)
        dfq_ref[...] = dxq.astype(BF16)
        dgq_ref[...] = dgq_ref[...] + jnp.sum(gq_rows, axis=0, keepdims=True)
        dxk, gk_rows = _head_norm_bwd(k_ref[...], gk_ref[...], dk_ref[...], bdv)
        dfk_ref[...] = dxk.astype(BF16)
        dgk_ref[...] = dgk_ref[...] + jnp.sum(gk_rows, axis=0, keepdims=True)
        dlf = _mm3(dc_ref[...], tri_ref[...], left=True) + carry[0:1, :]
        carry[0:1, :] = dlf[0:1, :]
        u = ff_ref[...] + b_ref[...]
        lane = lax.broadcasted_iota(jnp.int32, u.shape, 1)
        dff = jnp.where(lane < N_FF, dlf * _sigmoid(-u), 0.0)
        dff_ref[...] = dff.astype(BF16)
        dbf_ref[...] = dbf_ref[...] + jnp.sum(dff, axis=0, keepdims=True)

    return pl.pallas_call(
        body, name="qk_bwd",
        grid=(n,),
        in_specs=[pl.BlockSpec((T, FOX_W), rev(0)), pl.BlockSpec((T, FOX_W), rev(0)),
                  pl.BlockSpec((T, FOX_W), rev(OFF_FQ // FOX_W)), pl.BlockSpec((T, FOX_W), rev(OFF_FK // FOX_W)),
                  pl.BlockSpec((T, N_FFPAD), rev(0)),
                  pl.BlockSpec((1, N_FFPAD), lambda i: (0, 0)),
                  pl.BlockSpec((1, FOX_W), lambda i: (0, 0)), pl.BlockSpec((1, FOX_W), lambda i: (0, 0)),
                  pl.BlockSpec((FOX_W, FOX_W), lambda i: (0, 0)),
                  pl.BlockSpec((T, N_FFPAD), rev(0)),
                  pl.BlockSpec((T, T), lambda i: (0, 0))],
        out_specs=[pl.BlockSpec((T, FOX_W), rev(0)), pl.BlockSpec((T, FOX_W), rev(0)),
                   pl.BlockSpec((T, N_FFPAD), rev(0)),
                   pl.BlockSpec((1, FOX_W), lambda i: (0, 0)), pl.BlockSpec((1, FOX_W), lambda i: (0, 0)),
                   pl.BlockSpec((1, N_FFPAD), lambda i: (0, 0))],
        out_shape=[jax.ShapeDtypeStruct((S, FOX_W), BF16), jax.ShapeDtypeStruct((S, FOX_W), BF16),
                   jax.ShapeDtypeStruct((S, N_FFPAD), BF16),
                   jax.ShapeDtypeStruct((1, FOX_W), F32), jax.ShapeDtypeStruct((1, FOX_W), F32),
                   jax.ShapeDtypeStruct((1, N_FFPAD), F32)],
        scratch_shapes=[pltpu.VMEM((8, N_FFPAD), F32)],
        compiler_params=_cp(("arbitrary",), _VMEM_MID),
    )(dqs, dkn, proj, proj, pff, bfp, gq, gk, bd, dccol, triu)


def _dproj_layout(pieces):
    offs, o = [], 0
    for p in pieces:
        offs.append(o)
        o += p.shape[1]
    assert o == N_MAIN
    return offs


def _inproj_bwd_dx(pieces, dff, wm, wff, x, g, dy, ride=None):
    S, D = x.shape
    tm = min(_TM_DX, S)
    steps = S // tm
    offs = _dproj_layout(pieces)
    n = len(pieces)

    def body(*refs):
        p_refs = refs[:n]
        if ride is None:
            dff_ref, w_ref, wff_ref, x_ref, g_ref, dy_ref, dx_ref, dg_ref = refs[n:]
        else:
            dff_ref, w_ref, wff_ref, x_ref, g_ref, dy_ref, pa_ref, pb_ref = refs[n:n + 8]
            dx_ref, dg_ref, ra_ref, rb_ref = refs[n + 8:n + 12]
            xrefs = (pa_ref, pb_ref, ra_ref, rb_ref) + tuple(refs[n + 12:])

        @pl.when(pl.program_id(0) == 0)
        def _():
            dg_ref[...] = jnp.zeros_like(dg_ref)
            if ride is not None:
                _start_exchange("scatter", *xrefs)

        dh = _dot_nt(dff_ref[...], wff_ref[...])
        for p_ref, off in zip(p_refs, offs):
            dh = dh + _dot_nt(p_ref[...], w_ref[:, off:off + p_ref.shape[1]])
        xv = x_ref[...]
        r = _rms_rows(xv)
        xr = xv * r
        dg_ref[...] = dg_ref[...] + jnp.sum(dh * xr, axis=0, keepdims=True)
        gdh = g_ref[...] * dh
        m = jnp.mean(gdh * xr, axis=-1, keepdims=True)
        dx_ref[...] = dy_ref[...] + r * (gdh - xr * m)
        if ride is not None:
            @pl.when(pl.program_id(0) == steps - 1)
            def _():
                _wait_exchange("scatter", *xrefs)

    extra = () if ride is None else tuple(ride)
    return pl.pallas_call(
        body, name="inproj_bwd_dx" if ride is None else "inproj_bwd_dx_exchange",
        grid=(steps,),
        in_specs=[pl.BlockSpec((tm, p.shape[1]), lambda i: (i, 0)) for p in pieces]
        + [pl.BlockSpec((tm, N_FFPAD), lambda i: (i, 0)),
                  pl.BlockSpec((D, N_MAIN), lambda i: (0, 0)),
                  pl.BlockSpec((D, N_FFPAD), lambda i: (0, 0)),
                  pl.BlockSpec((tm, D), lambda i: (i, 0)),
                  pl.BlockSpec((1, D), lambda i: (0, 0)),
                  pl.BlockSpec((tm, D), lambda i: (i, 0))] + [_ANY] * len(extra),
        out_specs=[pl.BlockSpec((tm, D), lambda i: (i, 0)), pl.BlockSpec((1, D), lambda i: (0, 0))] + [_ANY] * len(extra),
        out_shape=[jax.ShapeDtypeStruct((S, D), F32), jax.ShapeDtypeStruct((1, D), F32)]
        + (_exchange_out_shapes("scatter", *extra) if extra else []),
        scratch_shapes=_EXCHANGE_SEMS if extra else [],
        compiler_params=_cp(("arbitrary",), _VMEM_WIDE),
    )(*pieces, dff, wm, wff, x, g, dy, *extra)


def _inproj_bwd_dw(ht, pieces, dff):
    D, S = ht.shape
    tk = min(_TK_DW, S)
    nk = S // tk
    offs = _dproj_layout(pieces)
    n = len(pieces)

    def body(*refs):
        ht_ref, p_refs, dff_ref = refs[0], refs[1:1 + n], refs[1 + n]
        dw_ref, dwff_ref, acc, accff = refs[2 + n:]
        k = pl.program_id(0)

        @pl.when(k == 0)
        def _():
            acc[...] = jnp.zeros_like(acc)
            accff[...] = jnp.zeros_like(accff)

        hb = ht_ref[...]
        for p_ref, off in zip(p_refs, offs):
            w = p_ref.shape[1]
            acc[:, off:off + w] = acc[:, off:off + w] + _dot(hb, p_ref[...])
        accff[...] = accff[...] + _dot(hb, dff_ref[...])

        @pl.when(k == nk - 1)
        def _():
            dw_ref[...] = acc[...].astype(BF16)
            dwff_ref[...] = accff[...].astype(BF16)

    return pl.pallas_call(
        body, name="inproj_bwd_dw",
        grid=(nk,),
        in_specs=[pl.BlockSpec((D, tk), lambda k: (0, k))]
        + [pl.BlockSpec((tk, p.shape[1]), lambda k: (k, 0)) for p in pieces]
        + [pl.BlockSpec((tk, N_FFPAD), lambda k: (k, 0))],
        out_specs=[pl.BlockSpec((D, N_MAIN), lambda k: (0, 0), pipeline_mode=pl.Buffered(1)),
                   pl.BlockSpec((D, N_FFPAD), lambda k: (0, 0), pipeline_mode=pl.Buffered(1))],
        out_shape=[jax.ShapeDtypeStruct((D, N_MAIN), BF16), jax.ShapeDtypeStruct((D, N_FFPAD), BF16)],
        scratch_shapes=[pltpu.VMEM((D, N_MAIN), F32), pltpu.VMEM((D, N_FFPAD), F32)],
        compiler_params=_cp(("arbitrary",), _VMEM_BIG),
    )(ht, *pieces, dff)


def _constants(T, rows):
    tril = jnp.tril(jnp.ones((T, T), F32)).astype(BF16)
    tril_rows = jnp.tril(jnp.ones((rows, rows), F32)).astype(BF16)
    hid = jnp.arange(FOX_W) // HEAD_DIM
    bd = (hid[:, None] == hid[None, :]).astype(BF16)
    ex = (jnp.arange(N_FFPAD)[:, None] == hid[None, :]).astype(BF16)
    return tril, tril.T, bd, ex, tril_rows, tril_rows.T


def _crow4(ccol, T):
    S = ccol.shape[0]
    c = ccol[:, :FOX_HEADS].T
    last = jnp.pad(c[:, T - 1::T], ((0, 0), (0, S - S // T)))
    rows = jnp.concatenate([c.reshape(FOX_HEADS // 2, 2, S), last.reshape(FOX_HEADS // 2, 2, S)], axis=1)
    return jnp.pad(rows, ((0, 0), (0, 4), (0, 0)))


def _layer_fwd(x, lw, consts, ride=None):
    tril, triu, bd, ex, tril_rows, _ = consts
    proj, pff, ht, psb = _inproj_fwd(x, lw["g"], lw["wm"], lw["wff"])
    qs, kn, ccol, cqb = _fox_prep(proj, pff, lw["bfp"], lw["gq"], lw["gk"], bd, ex, tril_rows)
    crow4 = _crow4(ccol, tril.shape[0])
    fo, lse, *gathered = _fox_fwd(qs, kn, proj, cqb, crow4, ride)
    so, ltot = _sb_fwd(psb, triu)
    pooled = _pool_fwd(proj)
    y, mixedt = _mix_out(fo, so, pooled, proj, lw["wbd"], lw["scale"], lw["wout"], x)
    return y, (x, proj, pff, ht, psb, qs, kn, cqb, crow4, fo, lse, so, ltot, pooled, mixedt), gathered


def _layer_bwd(dy, saved, lw, consts, ride=None, exchange_own=False):
    tril, _, bd, _, _, triu_rows = consts
    x, proj, pff, ht, psb, qs, kn, cqb, crow4, fo, lse, so, ltot, pooled, mixedt = saved
    S = x.shape[0]
    dfo, dfg, dso, dsg, dpg, dpooled, dscale, dwbd = _gate_bwd(dy, lw["wout"], fo, so, pooled, proj, lw["wbd"], lw["scale"])
    dwout = _matmul_acc(mixedt, dy, "dw_out")
    dpx = _pool_bwd(dpooled)
    dqs, dkn, dfv, dck, dcq4, *received = _fox_bwd(qs, kn, proj, dfo, fo, lse, cqb, crow4, ride)
    dsq, dsk, dsv = _sb_bwd(psb, dso, ltot, tril)
    dc8 = dck[:, ::HEAD_DIM] + dcq4[:, :2, :].reshape(FOX_HEADS, S).T
    dccol = jnp.pad(dc8, ((0, 0), (0, N_FFPAD - FOX_HEADS)))
    dfq, dfk, dff, dgq, dgk, dbf = _qk_bwd(dqs, dkn, proj, pff, lw["bfp"], lw["gq"], lw["gk"], bd, dccol, triu_rows)
    pieces = [dfq, dfk, dfv, dfg, dpx, dpg, dsq, dsk, dsv, dsg]
    dwm, dwff = _inproj_bwd_dw(ht, pieces, dff)
    dwm_t = dwm.T
    dwin_t = jnp.concatenate([dwm_t[:OFF_PX], dwff.T[:N_FF], dwm_t[OFF_PX:]], axis=0)
    own = _grad_parts({"w_in_t": dwin_t, "w_out": dwout}) if exchange_own else None
    dx, dng, *received_own = _inproj_bwd_dx(pieces, dff, lw["wm"], lw["wff"], x, lw["g"], dy, own)
    grads = {
        "norm_g": dng[0],
        "w_in_t": dwin_t,
        "b_f": dbf[0, :N_FF],
        "q_norm_g": dgq[0].reshape(FOX_HEADS, HEAD_DIM).sum(0),
        "k_norm_g": dgk[0].reshape(FOX_HEADS, HEAD_DIM).sum(0),
        "w_pool": jnp.stack([dwbd[64 * i:64 * i + 64, 64 * i:64 * i + 64] for i in range(4)]),
        "pool_scale": dscale[0],
        "w_out": dwout,
    }
    return dx, grads, received, received_own


def _layer_weights(l, norm_g, gin, b_f, q_norm_g, k_norm_g, w_pool, pool_scale, gout):
    D = gin.shape[1]
    w = gin.transpose(1, 0, 2).reshape(D, D_IN)
    wm = jnp.concatenate([w[:, :2048], w[:, 2048 + N_FF:]], axis=1)
    wff = jnp.pad(w[:, 2048:2048 + N_FF], ((0, 0), (0, N_FFPAD - N_FF)))
    grp = jnp.arange(POOL_W) // 64
    wbd = jnp.where(grp[:, None] == grp[None, :], jnp.tile(w_pool[l].transpose(1, 0, 2).reshape(64, POOL_W), (4, 1)), 0.0)
    return {
        "g": norm_g[l].reshape(1, D),
        "wm": wm, "wff": wff,
        "bfp": jnp.pad(b_f[l], (0, N_FFPAD - N_FF)).reshape(1, N_FFPAD),
        "gq": jnp.tile(q_norm_g[l], FOX_HEADS).reshape(1, FOX_W),
        "gk": jnp.tile(k_norm_g[l], FOX_HEADS).reshape(1, FOX_W),
        "wbd": wbd.astype(BF16),
        "scale": pool_scale[l].reshape(1, POOL_W),
        "wout": gout.reshape(D_MIX, D),
    }


def _grad_parts(g):
    dwin_t, dwout = g["w_in_t"].astype(BF16), g["w_out"].astype(BF16)
    return (dwin_t.reshape(N_DEV, D_IN // N_DEV, dwin_t.shape[1]),
            dwout.reshape(N_DEV, D_MIX // N_DEV, dwout.shape[1]))


def _train_step(x, target, norm_g, win_sh, b_f, q_norm_g, k_norm_g, w_pool, pool_scale, wout_sh):
    L = norm_g.shape[0]
    consts = _constants(min(_T, x.shape[0]), min(_TM_ROWS, x.shape[0]))
    gathered = _gather_two_level(win_sh[0], wout_sh[0], "gather_weights")
    lws, saved = [], []
    h = x
    for l in range(L):
        lws.append(_layer_weights(l, norm_g, gathered[0], b_f, q_norm_g, k_norm_g, w_pool, pool_scale, gathered[1]))
        ride = (win_sh[l + 1], wout_sh[l + 1]) if l + 1 < L else None
        h, sv, gathered = _layer_fwd(h, lws[l], consts, ride)
        saved.append(sv)
    dy, loss = _loss_head(h, target)
    grads, received = [None] * L, [None] * L
    ride = None
    for l in reversed(range(L)):
        dy, grads[l], got, got_own = _layer_bwd(dy, saved[l], lws[l], consts, ride, exchange_own=(l == 0))
        if ride is not None:
            received[l + 1] = got
        if l == 0:
            received[0] = got_own
        else:
            ride = _grad_parts(grads[l])
    return loss, dy, grads, received


def _mesh_pos():
    return lax.axis_index("x"), lax.axis_index("y"), lax.axis_index("c")


_FLIPS = [(0, 0, 1), (1, 0, 0), (0, 1, 0), (1, 1, 0), (1, 0, 1), (0, 1, 1), (1, 1, 1)]


def _peers():
    x, y, c = _mesh_pos()
    out = []
    for fx, fy, fc in _FLIPS:
        px = 1 - x if fx else x
        py = 1 - y if fy else y
        pc = 1 - c if fc else c
        out.append(((px, py, pc), 4 * px + 2 * py + pc))
    return out, 4 * x + 2 * y + c


_EXCHANGE_SEMS = [pltpu.SemaphoreType.DMA((14,)), pltpu.SemaphoreType.DMA((14,)), pltpu.SemaphoreType.DMA((2,))]
_ANY = pl.BlockSpec(memory_space=pl.ANY)


def _exchange_copies(kind, a_ref, b_ref, oa_ref, ob_ref, send_sems, recv_sems, loc_sems):
    peers, me = _peers()
    pairs = ((a_ref, oa_ref), (b_ref, ob_ref))
    local = [pltpu.make_async_copy(src if kind == "gather" else src.at[me], dst.at[me], loc_sems.at[t])
             for t, (src, dst) in enumerate(pairs)]
    remote = []
    for k, (dev, idx) in enumerate(peers):
        for t, (src, dst) in enumerate(pairs):
            remote.append(pltpu.make_async_remote_copy(
                src_ref=src if kind == "gather" else src.at[idx], dst_ref=dst.at[me],
                send_sem=send_sems.at[2 * k + t], recv_sem=recv_sems.at[2 * k + t],
                device_id=dev, device_id_type=pl.DeviceIdType.MESH))
    return local, remote


def _start_exchange(kind, *refs):
    local, remote = _exchange_copies(kind, *refs)
    for cp in local + remote:
        cp.start()


def _wait_exchange(kind, *refs):
    local, remote = _exchange_copies(kind, *refs)
    for cp in remote:
        cp.wait_recv()
    for cp in remote:
        cp.wait_send()
    for cp in local:
        cp.wait()


def _exchange_out_shapes(kind, a, b):
    if kind == "gather":
        return [jax.ShapeDtypeStruct((N_DEV,) + a.shape, a.dtype), jax.ShapeDtypeStruct((N_DEV,) + b.shape, b.dtype)]
    return [jax.ShapeDtypeStruct(a.shape, a.dtype), jax.ShapeDtypeStruct(b.shape, b.dtype)]


def _gather_two_level(a, b, name):
    def body(a_ref, b_ref, ga_ref, gb_ref, send_sems, recv_sems, loc_sems):
        x, y, c = _mesh_pos()
        slot_of = lambda px, py, pc: 4 * px + 2 * py + pc
        me, sib = slot_of(x, y, c), slot_of(x, y, 1 - c)
        chips = [(1 - x, y), (x, 1 - y), (1 - x, 1 - y)]
        pairs = ((a_ref, ga_ref), (b_ref, gb_ref))

        def copy(k, t, slot, to, src=None):
            dst = pairs[t][1].at[slot]
            return pltpu.make_async_remote_copy(
                src_ref=dst if src is None else src, dst_ref=dst, send_sem=send_sems.at[2 * k + t],
                recv_sem=recv_sems.at[2 * k + t], device_id=to, device_id_type=pl.DeviceIdType.MESH)

        local = [pltpu.make_async_copy(src, dst.at[me], loc_sems.at[t]) for t, (src, dst) in enumerate(pairs)]
        first = []
        for t, (src, _) in enumerate(pairs):
            first.append(copy(0, t, me, (x, y, 1 - c), src))
            first += [copy(1 + j, t, me, (*chip, c), src) for j, chip in enumerate(chips)]
        for cp in local + first:
            cp.start()
        passed = []
        for j, chip in enumerate(chips):
            for t in range(2):
                landed = slot_of(*chip, c)
                copy(1 + j, t, landed, (x, y, c)).wait_recv()
                cp = copy(4 + j, t, landed, (x, y, 1 - c))
                cp.start()
                passed.append(cp)
        for t in range(2):
            copy(0, t, sib, (x, y, c)).wait_recv()
            for j, chip in enumerate(chips):
                copy(4 + j, t, slot_of(*chip, 1 - c), (x, y, c)).wait_recv()
        for cp in first + passed:
            cp.wait_send()
        for cp in local:
            cp.wait()

    return pl.pallas_call(
        body, name=name,
        in_specs=[_ANY, _ANY], out_specs=[_ANY, _ANY],
        out_shape=_exchange_out_shapes("gather", a, b),
        scratch_shapes=_EXCHANGE_SEMS,
    )(a, b)


def _adam_math(w, g, m, v):
    m_new = ADAM_B1 * m + (1.0 - ADAM_B1) * g
    v_new = ADAM_B2 * v + (1.0 - ADAM_B2) * (g * g)
    m_hat = m_new / (1.0 - ADAM_B1 ** ADAM_STEP)
    v_hat = v_new / (1.0 - ADAM_B2 ** ADAM_STEP)
    delta = -ADAM_LR * (m_hat / (jnp.sqrt(v_hat) + ADAM_EPS) + ADAM_WD * w)
    return delta, m_new, v_new


def _sum_adamw(gparts, w, m, v, name):
    L, R, C = w.shape
    tr = min(128, R)

    def body(*refs):
        gp_refs = refs[:L]
        w_ref, m_ref, v_ref, g_ref, d_ref, nm_ref, nv_ref = refs[L:]
        for l in range(L):
            g = gp_refs[l][0].astype(F32)
            for s in range(1, N_DEV):
                g = g + gp_refs[l][s].astype(F32)
            d, mn, vn = _adam_math(w_ref[l], g, m_ref[l], v_ref[l])
            g_ref[l] = g
            d_ref[l] = d
            nm_ref[l] = mn
            nv_ref[l] = vn

    blk = pl.BlockSpec((L, tr, C), lambda r: (0, r, 0))
    return pl.pallas_call(
        body, name=name,
        grid=(R // tr,),
        in_specs=[pl.BlockSpec((N_DEV, tr, C), lambda r: (0, r, 0))] * L + [blk, blk, blk],
        out_specs=[blk, blk, blk, blk],
        out_shape=[jax.ShapeDtypeStruct((L, R, C), F32)] * 4,
        compiler_params=_cp(("parallel",), _VMEM_WIDE),
    )(*gparts, w, m, v)


def _sum_adamw_cols(gparts, w_t, m_t, v_t, name):
    C, L, D = w_t.shape
    td = min(128, D)

    def body(*refs):
        gp_refs = refs[:L]
        w_ref, m_ref, v_ref, g_ref, d_ref, nm_ref, nv_ref = refs[L:]
        for l in range(L):
            g = gp_refs[l][0].astype(F32)
            for s in range(1, N_DEV):
                g = g + gp_refs[l][s].astype(F32)
            d, mn, vn = _adam_math(w_ref[:, l, :], g, m_ref[:, l, :], v_ref[:, l, :])
            g_ref[:, l, :] = g
            d_ref[:, l, :] = d
            nm_ref[:, l, :] = mn
            nv_ref[:, l, :] = vn

    blk = pl.BlockSpec((C, L, td), lambda j: (0, 0, j))
    return pl.pallas_call(
        body, name=name,
        grid=(D // td,),
        in_specs=[pl.BlockSpec((N_DEV, C, td), lambda j: (0, 0, j))] * L + [blk, blk, blk],
        out_specs=[blk, blk, blk, blk],
        out_shape=[jax.ShapeDtypeStruct((C, L, D), F32)] * 4,
        compiler_params=_cp(("parallel",), _VMEM_WIDE),
    )(*gparts, w_t, m_t, v_t)


def _small_update(gpack, wpack, mpack, vpack):
    R = gpack.shape[0]
    VM = pl.BlockSpec(memory_space=pltpu.VMEM)

    def body(g_ref, w_ref, m_ref, v_ref, gs_ref, d_ref, nm_ref, nv_ref, buf, send_sems, recv_sems):
        peers, me = _peers()
        buf[me] = g_ref[...]
        copies = []
        for k, (dev, _) in enumerate(peers):
            cp = pltpu.make_async_remote_copy(
                src_ref=g_ref, dst_ref=buf.at[me], send_sem=send_sems.at[k], recv_sem=recv_sems.at[k],
                device_id=dev, device_id_type=pl.DeviceIdType.MESH)
            cp.start()
            copies.append(cp)
        for cp in copies:
            cp.wait_recv()
        for cp in copies:
            cp.wait_send()
        g = buf[0]
        for s in range(1, N_DEV):
            g = g + buf[s]
        d, mn, vn = _adam_math(w_ref[...], g, m_ref[...], v_ref[...])
        gs_ref[...] = g
        d_ref[...] = d
        nm_ref[...] = mn
        nv_ref[...] = vn

    return pl.pallas_call(
        body, name="small_update",
        in_specs=[VM] * 4, out_specs=[VM] * 4,
        out_shape=[jax.ShapeDtypeStruct((R, 128), F32)] * 4,
        scratch_shapes=[pltpu.VMEM((N_DEV, R, 128), F32), pltpu.SemaphoreType.DMA((7,)), pltpu.SemaphoreType.DMA((7,))],
        compiler_params=_cp(None, _VMEM_MID),
    )(gpack, wpack, mpack, vpack)


_SMALL = ("norm_g", "b_f", "q_norm_g", "k_norm_g", "w_pool", "pool_scale")


def _pack(parts):
    flat = jnp.concatenate([p.reshape(-1) for p in parts])
    n = flat.shape[0]
    rows = -(-n // (8 * 128)) * 8
    return jnp.pad(flat, (0, rows * 128 - n)).reshape(rows, 128)


def _unpack(packed, like):
    flat = packed.reshape(-1)
    out, o = [], 0
    for p in like:
        out.append(flat[o:o + p.size].reshape(p.shape))
        o += p.size
    return out


def kernel(x, norm_g, w_in, b_f, q_norm_g, k_norm_g, w_pool, pool_scale, w_out, loss_target, m_norm_g, m_w_in, m_b_f, m_q_norm_g, m_k_norm_g, m_w_pool, m_pool_scale, m_w_out, v_norm_g, v_w_in, v_b_f, v_q_norm_g, v_k_norm_g, v_w_pool, v_pool_scale, v_w_out):
    L = w_in.shape[0]

    loss_local, dx, grads, received = _train_step(x[0], loss_target[0], norm_g, w_in.astype(BF16), b_f, q_norm_g,
                                                  k_norm_g, w_pool, pool_scale, w_out.astype(BF16))
    loss = lax.psum(loss_local, MESH_AXES)
    g = {k: jnp.stack([grads[l][k] for l in range(L)]) for k in _SMALL}

    cols = lambda a: a.transpose(2, 0, 1)
    g_win, d_win, nm_win, nv_win = [a.transpose(1, 2, 0) for a in _sum_adamw_cols(
        [r[0] for r in received], cols(w_in), cols(m_w_in), cols(v_w_in), "adamw_w_in")]
    g_wout, d_wout, nm_wout, nv_wout = _sum_adamw([r[1] for r in received], w_out, m_w_out, v_w_out, "adamw_w_out")

    ws = dict(norm_g=norm_g, b_f=b_f, q_norm_g=q_norm_g, k_norm_g=k_norm_g, w_pool=w_pool, pool_scale=pool_scale)
    ms = dict(norm_g=m_norm_g, b_f=m_b_f, q_norm_g=m_q_norm_g, k_norm_g=m_k_norm_g, w_pool=m_w_pool, pool_scale=m_pool_scale)
    vs = dict(norm_g=v_norm_g, b_f=v_b_f, q_norm_g=v_q_norm_g, k_norm_g=v_k_norm_g, w_pool=v_w_pool, pool_scale=v_pool_scale)
    like = [ws[k] for k in _SMALL]
    gs_p, d_p, nm_p, nv_p = _small_update(_pack([g[k] for k in _SMALL]), _pack(like),
                                          _pack([ms[k] for k in _SMALL]), _pack([vs[k] for k in _SMALL]))
    gs = dict(zip(_SMALL, _unpack(gs_p, like)))
    ds = dict(zip(_SMALL, _unpack(d_p, like)))
    nms = dict(zip(_SMALL, _unpack(nm_p, like)))
    nvs = dict(zip(_SMALL, _unpack(nv_p, like)))
    gs["w_in"], ds["w_in"], nms["w_in"], nvs["w_in"] = g_win, d_win, nm_win, nv_win
    gs["w_out"], ds["w_out"], nms["w_out"], nvs["w_out"] = g_wout, d_wout, nm_wout, nv_wout

    order = ("norm_g", "w_in", "b_f", "q_norm_g", "k_norm_g", "w_pool", "pool_scale", "w_out")
    return (loss, dx[None], *[gs[k] for k in order], *[ds[k] for k in order],
            *[nms[k] for k in order], *[nvs[k] for k in order])
```

```python
import jax
import jax.numpy as jnp
from jax import lax
from jax.experimental import pallas as pl
from jax.experimental.pallas import tpu as pltpu

F32 = jnp.float32
BF16 = jnp.bfloat16

EPS = 1e-6
NEG = -1e30
HEAD_DIM = 64
FOX_HEADS = 8
FOX_W = 512
POOL_W = 256
SB_W = 256
D_MIX = 1024
N_FF = 8
N_MAIN = 3584
N_FFPAD = 128
OFF_FQ, OFF_FK, OFF_FV, OFF_FG = 0, 512, 1024, 1536
OFF_PX, OFF_PG = 2048, 2304
OFF_SQ, OFF_SK, OFF_SV, OFF_SG = 2560, 2816, 3072, 3328
D_IN = 3592
Q_SCALE = HEAD_DIM ** -0.5

ADAM_LR = 0.001
ADAM_B1 = 0.9
ADAM_B2 = 0.999
ADAM_EPS = 1e-08
ADAM_WD = 0.01
ADAM_STEP = 10

N_DEV = 8
MESH_AXES = ("x", "y", "c")

_T = 256
_TM = 512
_TM_ROWS = 512
_TM_FWD, _TN_FWD = 2048, 512
_TM_DX = 512
_TK_DW = 1024
_ADAM_ROWS = 16
_VMEM_V7X = 64 << 20
_VMEM_BIG = _VMEM_V7X - (8 << 20)
_VMEM_MID = 40 << 20
_VMEM_WIDE = 48 << 20


def _cp(sem=None, vmem=None):
    kw = {}
    if sem is not None:
        kw["dimension_semantics"] = sem
    if vmem is not None:
        kw["vmem_limit_bytes"] = vmem
    return pltpu.CompilerParams(**kw)


def _dot(a, b):
    return jnp.dot(a, b, preferred_element_type=F32)


def _dot_nt(a, b):
    return lax.dot_general(a, b, (((1,), (1,)), ((), ())), preferred_element_type=F32)


def _dot_tn(a, b):
    return lax.dot_general(a, b, (((0,), (0,)), ((), ())), preferred_element_type=F32)


def _mm2(v, m, left=False):
    hi = v.astype(BF16)
    lo = (v - hi.astype(F32)).astype(BF16)
    if left:
        return _dot(m, hi) + _dot(m, lo)
    return _dot(hi, m) + _dot(lo, m)


def _mm3(v, m, left=False):
    a1 = v.astype(BF16)
    r1 = v - a1.astype(F32)
    a2 = r1.astype(BF16)
    a3 = (r1 - a2.astype(F32)).astype(BF16)
    if left:
        return _dot(m, a1) + _dot(m, a2) + _dot(m, a3)
    return _dot(a1, m) + _dot(a2, m) + _dot(a3, m)


def _sigmoid(z):
    return 1.0 / (1.0 + jnp.exp(-z))


def _rms_rows(x):
    return lax.rsqrt(jnp.mean(x * x, axis=-1, keepdims=True) + EPS)


def _inproj_fwd(x, g, wm, wff):
    S, D = x.shape
    tm = min(_TM_FWD, S)
    tn = _TN_FWD
    assert OFF_SQ % tn == 0 and N_MAIN - OFF_SQ == 4 * SB_W
    j_sb = OFF_SQ // tn

    def body(x_ref, g_ref, w_ref, wff_ref, o_ref, off_ref, ht_ref, sb_ref, h_ref):
        j = pl.program_id(1)

        @pl.when(j == 0)
        def _():
            xv = x_ref[...]
            h = (xv * _rms_rows(xv)) * g_ref[...]
            h_ref[...] = h.astype(BF16)
            ht_ref[...] = h.T.astype(BF16)
            off_ref[...] = _dot(h_ref[...], wff_ref[...])

        res = _dot(h_ref[...], w_ref[...])
        o_ref[...] = res

        @pl.when(j >= j_sb)
        def _():
            sb_ref[...] = res.astype(BF16)

    return pl.pallas_call(
        body, name="inproj_fwd",
        grid=(S // tm, N_MAIN // tn),
        in_specs=[pl.BlockSpec((tm, D), lambda i, j: (i, 0)),
                  pl.BlockSpec((1, D), lambda i, j: (0, 0)),
                  pl.BlockSpec((D, tn), lambda i, j: (0, j)),
                  pl.BlockSpec((D, N_FFPAD), lambda i, j: (0, 0))],
        out_specs=[pl.BlockSpec((tm, tn), lambda i, j: (i, j)),
                   pl.BlockSpec((tm, N_FFPAD), lambda i, j: (i, 0)),
                   pl.BlockSpec((D, tm), lambda i, j: (0, i)),
                   pl.BlockSpec((tm, tn), lambda i, j: (i, jnp.maximum(j - j_sb, 0)))],
        out_shape=[jax.ShapeDtypeStruct((S, N_MAIN), F32), jax.ShapeDtypeStruct((S, N_FFPAD), F32),
                   jax.ShapeDtypeStruct((D, S), BF16), jax.ShapeDtypeStruct((S, 4 * SB_W), BF16)],
        scratch_shapes=[pltpu.VMEM((tm, D), BF16)],
        compiler_params=_cp(("parallel", "arbitrary"), _VMEM_BIG),
    )(x, g, wm, wff)


def _head_norm(x, g, bd):
    ss = _mm2(x * x, bd)
    r = lax.rsqrt(ss * (1.0 / HEAD_DIM) + EPS)
    return (x * r) * g


def _fox_prep(proj, pff, bfp, gq, gk, bd, ex, tril):
    S = proj.shape[0]
    T = tril.shape[0]

    def body(q_ref, k_ref, ff_ref, b_ref, gq_ref, gk_ref, bd_ref, ex_ref, tri_ref,
             qs_ref, kn_ref, cc_ref, cqb_ref, carry):
        @pl.when(pl.program_id(0) == 0)
        def _():
            carry[...] = jnp.zeros_like(carry)

        bdv = bd_ref[...]
        qs_ref[...] = (_head_norm(q_ref[...], gq_ref[...], bdv) * Q_SCALE).astype(BF16)
        kn_ref[...] = _head_norm(k_ref[...], gk_ref[...], bdv).astype(BF16)
        u = ff_ref[...] + b_ref[...]
        lf = jnp.minimum(u, 0.0) - jnp.log1p(jnp.exp(-jnp.abs(u)))
        c = _mm3(lf, tri_ref[...], left=True) + carry[0:1, :]
        carry[0:1, :] = c[T - 1:T, :]
        cc_ref[...] = c
        cqb_ref[...] = _mm3(c, ex_ref[...])

    return pl.pallas_call(
        body, name="fox_prep",
        grid=(S // T,),
        in_specs=[pl.BlockSpec((T, FOX_W), lambda i: (i, OFF_FQ // FOX_W)),
                  pl.BlockSpec((T, FOX_W), lambda i: (i, OFF_FK // FOX_W)),
                  pl.BlockSpec((T, N_FFPAD), lambda i: (i, 0)),
                  pl.BlockSpec((1, N_FFPAD), lambda i: (0, 0)),
                  pl.BlockSpec((1, FOX_W), lambda i: (0, 0)),
                  pl.BlockSpec((1, FOX_W), lambda i: (0, 0)),
                  pl.BlockSpec((FOX_W, FOX_W), lambda i: (0, 0)),
                  pl.BlockSpec((N_FFPAD, FOX_W), lambda i: (0, 0)),
                  pl.BlockSpec((T, T), lambda i: (0, 0))],
        out_specs=[pl.BlockSpec((T, FOX_W), lambda i: (i, 0)),
                   pl.BlockSpec((T, FOX_W), lambda i: (i, 0)),
                   pl.BlockSpec((T, N_FFPAD), lambda i: (i, 0)),
                   pl.BlockSpec((T, FOX_W), lambda i: (i, 0))],
        out_shape=[jax.ShapeDtypeStruct((S, FOX_W), BF16), jax.ShapeDtypeStruct((S, FOX_W), BF16),
                   jax.ShapeDtypeStruct((S, N_FFPAD), F32), jax.ShapeDtypeStruct((S, FOX_W), F32)],
        scratch_shapes=[pltpu.VMEM((8, N_FFPAD), F32)],
        compiler_params=_cp(("arbitrary",), _VMEM_MID),
    )(proj, proj, pff, bfp, gq, gk, bd, ex, tril)


def _pair_blk(S, off=0):
    return pl.BlockSpec((S, 128), lambda p: (0, off + p), pipeline_mode=pl.Buffered(1))


def _pair_rows(S):
    return pl.BlockSpec((None, 8, S), lambda p: (p, 0, 0), pipeline_mode=pl.Buffered(1))


def _head_masks(S):
    return lax.broadcasted_iota(jnp.int32, (S, 128), 1) < HEAD_DIM


_EXP_ZERO = 104.0


def _spread_heads(x):
    src = lax.broadcasted_iota(jnp.int32, (128, 128), 0)
    return (_mm3(x, (src == 0).astype(BF16)), _mm3(x, (src == HEAD_DIM).astype(BF16)))


def _score_bounds(q, k):
    same_head = ((lax.broadcasted_iota(jnp.int32, (128, 128), 0) < HEAD_DIM)
                 == (lax.broadcasted_iota(jnp.int32, (128, 128), 1) < HEAD_DIM)).astype(BF16)

    def max_norm2(x):
        xf = x.astype(F32)
        return jnp.max(_mm2(xf * xf, same_head), axis=0, keepdims=True)

    z = jnp.sqrt(max_norm2(q) * max_norm2(k))
    z = jnp.where(z == z, z, jnp.inf)
    return jnp.max(z[:, 0:1]) * 1.001 + 1e-3, jnp.max(z[:, 64:65]) * 1.001 + 1e-3


def _for_tiles_back(i, n, tiles_fn, fours=False):
    if fours:
        def four(t, c):
            tiles_fn([i - 1 - 4 * t, i - 2 - 4 * t, i - 3 - 4 * t, i - 4 - 4 * t])
            return c

        lax.fori_loop(0, lax.shift_right_logical(n, 2), four, 0)
        rest = i - (n & ~3)

        @pl.when((n & 2) != 0)
        def _():
            tiles_fn([rest - 1, rest - 2])
    else:
        def two(t, c):
            tiles_fn([i - 1 - 2 * t, i - 2 - 2 * t])
            return c

        lax.fori_loop(0, lax.shift_right_logical(n, 1), two, 0)

    @pl.when((n & 1) != 0)
    def _():
        tiles_fn([i - n])


def _fox_tiles_back(cr_ref, i, r0, zba, zbb):
    last = cr_ref[:, pl.ds(0, 128)]
    first = cr_ref[:, pl.ds(r0, 128)]
    alive_a = 2.0 * zba + first[0:1, 0:1] - last[2:3, :] > -_EXP_ZERO
    alive_b = 2.0 * zbb + first[1:2, 0:1] - last[3:4, :] > -_EXP_ZERO
    before = lax.broadcasted_iota(jnp.int32, (1, 128), 1) < i
    return jnp.sum((before & (alive_a | alive_b)).astype(jnp.int32))


def _fox_fwd(qs, kn, proj, cqb, crow4, ride=None):
    S = qs.shape[0]
    T = min(_T, S)
    nq = S // T
    n_pairs = FOX_W // 128

    def body(*refs):
        if ride is None:
            q_ref, k_ref, v_ref, cq_ref, cr_ref, o_ref, lse_ref = refs[:7]
            qa, qb, vta, vtb, cka, ckb, ma, mb, acca, accb = refs[7:]
        else:
            q_ref, k_ref, v_ref, cq_ref, cr_ref, wa_ref, wb_ref, o_ref, lse_ref, ga_ref, gb_ref = refs[:11]
            qa, qb, vta, vtb, cka, ckb, ma, mb, acca, accb = refs[11:21]
            xrefs = (wa_ref, wb_ref, ga_ref, gb_ref) + tuple(refs[21:])

            @pl.when(pl.program_id(0) == 0)
            def _():
                _start_exchange("gather", *xrefs)

        lane_s = _head_masks(S)
        q = q_ref[...]
        zq = jnp.zeros_like(q)
        qa[...] = jnp.where(lane_s, q, zq)
        qb[...] = jnp.where(lane_s, zq, q)
        cq = cq_ref[...]
        cka[...], ckb[...] = _spread_heads(cq)
        lse_ref[...] = jnp.zeros((8, S), F32)
        row_t = lax.broadcasted_iota(jnp.int32, (128, T), 0) < HEAD_DIM
        zba, zbb = _score_bounds(q, k_ref[...])

        def prep(c, carry):
            c0 = pl.multiple_of(c * T, T)
            vt = v_ref[pl.ds(c0, T), :].T
            vta[:, pl.ds(c0, T)] = jnp.where(row_t, vt, 1.0).astype(BF16)
            vtb[:, pl.ds(c0, T)] = jnp.where(row_t, 1.0, vt).astype(BF16)
            return carry

        lax.fori_loop(0, nq, prep, 0)
        causal = (lax.broadcasted_iota(jnp.int32, (T, T), 0) <= lax.broadcasted_iota(jnp.int32, (T, T), 1))

        heads = ((qa, vta, cka, ma, acca), (qb, vtb, ckb, mb, accb))

        def kv(js, r0, masked):
            cr = cr_ref[:, pl.ds(r0, T)]
            c0s = [pl.multiple_of(j * T, T) for j in js]
            ks = [k_ref[pl.ds(c0, T), :] for c0 in c0s]
            ss = []
            for h, (qr, _, ckr, _, _) in enumerate(heads):
                qh = qr[pl.ds(r0, T), :]
                row = []
                for k, c0 in zip(ks, c0s):
                    s = _dot_nt(k, qh) - jnp.tile(ckr[pl.ds(c0, T), :], (1, T // 128))
                    row.append(jnp.where(causal, s, NEG) if masked else s)
                ss.append(row)
            ms = []
            for h, (row, (_, _, _, mr, _)) in enumerate(zip(ss, heads)):
                top = row[0]
                for s in row[1:]:
                    top = jnp.maximum(top, s)
                m_old = mr[0:1, :]
                ms.append((m_old, jnp.maximum(m_old, jnp.max(top, axis=0, keepdims=True) + cr[h:h + 1, :])))
            ps = [[jnp.exp(s + (cr[h:h + 1, :] - m_new)).astype(BF16) for s in row]
                  for h, (row, (_, m_new)) in enumerate(zip(ss, ms))]
            pvs = []
            for row, (_, vr, _, _, _) in zip(ps, heads):
                pv = _dot(vr[:, pl.ds(c0s[0], T)], row[0])
                for p, c0 in zip(row[1:], c0s[1:]):
                    pv = pv + _dot(vr[:, pl.ds(c0, T)], p)
                pvs.append(pv)
            for pv, (m_old, m_new), (_, _, _, mr, ar) in zip(pvs, ms, heads):
                ar[...] = jnp.exp(m_old - m_new) * ar[...] + pv
                mr[0:1, :] = m_new

        def qblk(i, carry):
            r0 = pl.multiple_of(i * T, T)
            ma[...] = jnp.full((8, T), NEG, F32)
            mb[...] = jnp.full((8, T), NEG, F32)
            acca[...] = jnp.zeros((128, T), F32)
            accb[...] = jnp.zeros((128, T), F32)
            kv([i], r0, True)
            done = _fox_tiles_back(cr_ref, i, r0, zba, zbb)
            _for_tiles_back(i, done, lambda js: kv(js, r0, False), fours=True)
            aa = acca[...]
            ab = accb[...]
            la = aa[64:65, :]
            lb = ab[0:1, :]
            o_ref[pl.ds(r0, T), :] = jnp.where(row_t, aa / la, ab / lb).T
            lse_ref[0:1, pl.ds(r0, T)] = ma[0:1, :] + jnp.log(la)
            lse_ref[1:2, pl.ds(r0, T)] = mb[0:1, :] + jnp.log(lb)
            lse_ref[2:3, pl.ds(r0, T)] = jnp.broadcast_to(done.astype(F32), (1, T))
            return carry

        lax.fori_loop(0, nq, qblk, 0)
        if ride is not None:
            @pl.when(pl.program_id(0) == n_pairs - 1)
            def _():
                _wait_exchange("gather", *xrefs)

    extra = () if ride is None else tuple(ride)
    return pl.pallas_call(
        body, name="fox_fwd" if ride is None else "fox_fwd_gather",
        grid=(n_pairs,),
        in_specs=[_pair_blk(S), _pair_blk(S), _pair_blk(S, OFF_FV // 128), _pair_blk(S), _pair_rows(S)]
        + [_ANY] * len(extra),
        out_specs=[_pair_blk(S), _pair_rows(S)] + [_ANY] * len(extra),
        out_shape=[jax.ShapeDtypeStruct((S, FOX_W), F32), jax.ShapeDtypeStruct((n_pairs, 8, S), F32)]
        + (_exchange_out_shapes("gather", *extra) if extra else []),
        scratch_shapes=[pltpu.VMEM((S, 128), BF16)] * 2 + [pltpu.VMEM((128, S), BF16)] * 2
        + [pltpu.VMEM((S, 128), F32)] * 2 + [pltpu.VMEM((8, T), F32)] * 2 + [pltpu.VMEM((128, T), F32)] * 2
        + (_EXCHANGE_SEMS if extra else []),
        compiler_params=_cp(("arbitrary",), _VMEM_BIG),
    )(qs, kn, proj, cqb, crow4, *extra)


def _softplus_parts(z):
    e = jnp.exp(-jnp.abs(z))
    return e, jnp.maximum(z, 0.0) + jnp.log(1.0 + e)


def _sb_fwd(psb, triu):
    S = psb.shape[0]
    T = triu.shape[0]
    nq = S // T

    n_pairs = SB_W // 128
    H = 2 * n_pairs

    def body(q_ref, k_ref, v_ref, tri_ref, o_ref, lt_ref, qm, vt, rr, acc):
        lane_s = _head_masks(S)
        zbs = []
        for p in range(n_pairs):
            q = (q_ref[:, 128 * p:128 * (p + 1)].astype(F32) * Q_SCALE).astype(BF16)
            zq = jnp.zeros_like(q)
            qm[2 * p] = jnp.where(lane_s, q, zq)
            qm[2 * p + 1] = jnp.where(lane_s, zq, q)
            zbs += list(_score_bounds(q, k_ref[:, 128 * p:128 * (p + 1)]))
        lt_ref[...] = jnp.zeros((n_pairs, 8, S), F32)
        row_t = lax.broadcasted_iota(jnp.int32, (128, T), 0) < HEAD_DIM

        def prep(c, carry):
            c0 = pl.multiple_of(c * T, T)
            for p in range(n_pairs):
                vt[p, :, pl.ds(c0, T)] = v_ref[pl.ds(c0, T), 128 * p:128 * (p + 1)].astype(F32).T.astype(BF16)
            return carry

        lax.fori_loop(0, nq, prep, 0)
        strict = (lax.broadcasted_iota(jnp.int32, (T, T), 0) < lax.broadcasted_iota(jnp.int32, (T, T), 1))

        def kv(tiles, r0):
            tri = tri_ref[...]
            c0s = [pl.multiple_of(j * T, T) for j, _ in tiles]
            zs = [[_dot_nt(k_ref[pl.ds(c0, T), 128 * (h // 2):128 * (h // 2 + 1)], qm[h, pl.ds(r0, T), :])
                   for c0 in c0s] for h in range(H)]
            lbs = [[jnp.where(strict, -_softplus_parts(z)[1], 0.0) if masked else -_softplus_parts(z)[1]
                    for z, (_, masked) in zip(row, tiles)] for row in zs]
            incs = [[_mm2(lb, tri, left=True) for lb in row] for row in lbs]
            avs = []
            for h in range(H):
                r = rr[h, 0:1, :]
                av = None
                for z, inc, c0, (_, masked) in zip(zs[h], incs[h], c0s, tiles):
                    a = jnp.exp(z + inc + r)
                    if masked:
                        a = jnp.where(strict, a, 0.0)
                    term = _dot(vt[h // 2, :, pl.ds(c0, T)], a.astype(BF16))
                    av = term if av is None else av + term
                    r = r + inc[0:1, :]
                avs.append((av, r))
            for h, (av, r) in enumerate(avs):
                rr[h, 0:1, :] = r
                acc[h] = acc[h] + av

        def qblk(i, carry):
            r0 = pl.multiple_of(i * T, T)
            rr[...] = jnp.zeros((H, 8, T), F32)
            acc[...] = jnp.zeros((H, 128, T), F32)

            @pl.when(i == 0)
            def _():
                kv([(i, True)], r0)

            @pl.when(i > 0)
            def _():
                kv([(i, True), (i - 1, False)], r0)

            def alive():
                m = jnp.max(rr[0, 0:1, :]) + zbs[0]
                for h in range(1, H):
                    m = jnp.maximum(m, jnp.max(rr[h, 0:1, :]) + zbs[h])
                return m > -_EXP_ZERO

            def cond(st):
                return (st[0] < i) & st[1]

            def step(st):
                kv([(i - 1 - st[0], False)], r0)
                return st[0] + 1, alive()

            done, _ = lax.while_loop(cond, step, (jnp.minimum(i, 1), alive()))
            for p in range(n_pairs):
                o_ref[pl.ds(r0, T), 128 * p:128 * (p + 1)] = jnp.where(row_t, acc[2 * p], acc[2 * p + 1]).T
                lt_ref[p, 0:1, pl.ds(r0, T)] = rr[2 * p, 0:1, :]
                lt_ref[p, 1:2, pl.ds(r0, T)] = rr[2 * p + 1, 0:1, :]
                lt_ref[p, 2:3, pl.ds(r0, T)] = jnp.broadcast_to(done.astype(F32), (1, T))
            return carry

        lax.fori_loop(0, nq, qblk, 0)

    wide = lambda off: pl.BlockSpec((S, SB_W), lambda g: (0, off), pipeline_mode=pl.Buffered(1))
    return pl.pallas_call(
        body, name="sb_fwd",
        grid=(1,),
        in_specs=[wide(0), wide(1), wide(2), pl.BlockSpec((T, T), lambda g: (0, 0))],
        out_specs=[wide(0), pl.BlockSpec((n_pairs, 8, S), lambda g: (0, 0, 0), pipeline_mode=pl.Buffered(1))],
        out_shape=[jax.ShapeDtypeStruct((S, SB_W), F32), jax.ShapeDtypeStruct((n_pairs, 8, S), F32)],
        scratch_shapes=[pltpu.VMEM((H, S, 128), BF16), pltpu.VMEM((n_pairs, 128, S), BF16),
                        pltpu.VMEM((H, 8, T), F32), pltpu.VMEM((H, 128, T), F32)],
        compiler_params=_cp(("arbitrary",), _VMEM_BIG),
    )(psb, psb, psb, triu)


def _pool_window_lanes(shape):
    lane = lax.broadcasted_iota(jnp.int32, shape, 1)
    return jnp.where(lane < 64, 2, jnp.where(lane < 128, 4, jnp.where(lane < 192, 8, 16)))


def _pool_fwd(proj):
    S = proj.shape[0]

    def body(x_ref, o_ref):
        x = x_ref[...]
        t = lax.broadcasted_iota(jnp.int32, x.shape, 0)
        lane = lax.broadcasted_iota(jnp.int32, x.shape, 1)

        def back(a, k):
            return jnp.where(t >= k, pltpu.roll(a, k, 0), 0.0)

        s1 = x + back(x, 1)
        s2 = s1 + back(s1, 2)
        s4 = s2 + back(s2, 4)
        s8 = s4 + back(s4, 8)
        win = jnp.where(lane < 64, s1, jnp.where(lane < 128, s2, jnp.where(lane < 192, s4, s8)))
        cnt = jnp.minimum(t + 1, _pool_window_lanes(x.shape)).astype(F32)
        o_ref[...] = win / cnt - x

    return pl.pallas_call(
        body, name="pool_fwd",
        grid=(1,),
        in_specs=[pl.BlockSpec((S, POOL_W), lambda i: (0, OFF_PX // POOL_W))],
        out_specs=pl.BlockSpec((S, POOL_W), lambda i: (0, 0)),
        out_shape=jax.ShapeDtypeStruct((S, POOL_W), F32),
        compiler_params=_cp(("arbitrary",), _VMEM_BIG),
    )(proj)


def _silu(g):
    return g * _sigmoid(g)


def _mix_out(fo, so, pooled, proj, wbd, scale, wout, x):
    S, D = x.shape
    tm = min(_TM_ROWS, S)

    def body(fo_ref, fg_ref, so_ref, sg_ref, pl_ref, pg_ref, wbd_ref, sc_ref, w_ref, x_ref, y_ref, mxt_ref, mx_ref):
        parts = ((0, fo_ref[...] * _silu(fg_ref[...])),
                 (FOX_W, (_dot(pl_ref[...].astype(BF16), wbd_ref[...]) * sc_ref[...]) * _silu(pg_ref[...])),
                 (FOX_W + POOL_W, so_ref[...] * _silu(sg_ref[...])))
        for off, part in parts:
            w = part.shape[1]
            mx_ref[:, off:off + w] = part.astype(BF16)
            mxt_ref[off:off + w, :] = part.T.astype(BF16)
        y_ref[...] = x_ref[...] + _dot(mx_ref[...], w_ref[...])

    return pl.pallas_call(
        body, name="mix_out",
        grid=(S // tm,),
        in_specs=[pl.BlockSpec((tm, FOX_W), lambda i: (i, 0)),
                  pl.BlockSpec((tm, FOX_W), lambda i: (i, OFF_FG // FOX_W)),
                  pl.BlockSpec((tm, SB_W), lambda i: (i, 0)),
                  pl.BlockSpec((tm, SB_W), lambda i: (i, OFF_SG // SB_W)),
                  pl.BlockSpec((tm, POOL_W), lambda i: (i, 0)),
                  pl.BlockSpec((tm, POOL_W), lambda i: (i, OFF_PG // POOL_W)),
                  pl.BlockSpec((POOL_W, POOL_W), lambda i: (0, 0)),
                  pl.BlockSpec((1, POOL_W), lambda i: (0, 0)),
                  pl.BlockSpec((D_MIX, D), lambda i: (0, 0)),
                  pl.BlockSpec((tm, D), lambda i: (i, 0))],
        out_specs=[pl.BlockSpec((tm, D), lambda i: (i, 0)), pl.BlockSpec((D_MIX, tm), lambda i: (0, i))],
        out_shape=[jax.ShapeDtypeStruct((S, D), F32), jax.ShapeDtypeStruct((D_MIX, S), BF16)],
        scratch_shapes=[pltpu.VMEM((tm, D_MIX), BF16)],
        compiler_params=_cp(("parallel",), _VMEM_MID),
    )(fo, proj, so, proj, pooled, proj, wbd, scale, wout, x)


def _loss_head(y, target):
    S, D = y.shape
    tm = min(_TM, S)

    def body(y_ref, t_ref, dy_ref, ls_ref):
        @pl.when(pl.program_id(0) == 0)
        def _():
            ls_ref[...] = jnp.zeros_like(ls_ref)

        e = y_ref[...] - t_ref[...]
        dy_ref[...] = e * (1.0 / D)
        ls_ref[...] = ls_ref[...] + jnp.sum(e * e) * (0.5 / D)

    dy, ls = pl.pallas_call(
        body, name="loss_head",
        grid=(S // tm,),
        in_specs=[pl.BlockSpec((tm, D), lambda i: (i, 0)), pl.BlockSpec((tm, D), lambda i: (i, 0))],
        out_specs=[pl.BlockSpec((tm, D), lambda i: (i, 0)), pl.BlockSpec((8, 128), lambda i: (0, 0))],
        out_shape=[jax.ShapeDtypeStruct((S, D), F32), jax.ShapeDtypeStruct((8, 128), F32)],
        compiler_params=_cp(("arbitrary",), _VMEM_MID),
    )(y, target)
    return dy, ls[0, 0]


def _dsilu(g):
    s = _sigmoid(g)
    return s * (1.0 + g * (1.0 - s))


def _gate_bwd(dy, wout, fo, so, pooled, proj, wbd, scale):
    S, D = dy.shape
    tm = min(_TM_ROWS, S)

    def body(dy_ref, w_ref, fo_ref, fg_ref, so_ref, sg_ref, pl_ref, pg_ref, wbd_ref, sc_ref,
             dfo_ref, dfg_ref, dso_ref, dsg_ref, dpg_ref, dpl_ref, dsc_ref, dwbd_ref):
        @pl.when(pl.program_id(0) == 0)
        def _():
            dsc_ref[...] = jnp.zeros_like(dsc_ref)
            dwbd_ref[...] = jnp.zeros_like(dwbd_ref)

        dm = _dot_nt(dy_ref[...].astype(BF16), w_ref[...])
        dmf = dm[:, 0:FOX_W]
        dmp = dm[:, FOX_W:FOX_W + POOL_W]
        dms = dm[:, FOX_W + POOL_W:D_MIX]
        fg = fg_ref[...]
        dfo_ref[...] = dmf * _silu(fg)
        dfg_ref[...] = (dmf * fo_ref[...] * _dsilu(fg)).astype(BF16)
        sg = sg_ref[...]
        dso_ref[...] = (dms * _silu(sg)).astype(BF16)
        dsg_ref[...] = (dms * so_ref[...] * _dsilu(sg)).astype(BF16)
        pg = pg_ref[...]
        plb = pl_ref[...].astype(BF16)
        yw = _dot(plb, wbd_ref[...])
        sc = sc_ref[...]
        dpg_ref[...] = (dmp * (yw * sc) * _dsilu(pg)).astype(BF16)
        dys = dmp * _silu(pg)
        dsc_ref[...] = dsc_ref[...] + jnp.sum(dys * yw, axis=0, keepdims=True)
        dyw = (dys * sc).astype(BF16)
        dpl_ref[...] = _dot_nt(dyw, wbd_ref[...])
        dwbd_ref[...] = dwbd_ref[...] + _dot_tn(plb, dyw)

    return pl.pallas_call(
        body, name="gate_bwd",
        grid=(S // tm,),
        in_specs=[pl.BlockSpec((tm, D), lambda i: (i, 0)),
                  pl.BlockSpec((D_MIX, D), lambda i: (0, 0)),
                  pl.BlockSpec((tm, FOX_W), lambda i: (i, 0)),
                  pl.BlockSpec((tm, FOX_W), lambda i: (i, OFF_FG // FOX_W)),
                  pl.BlockSpec((tm, SB_W), lambda i: (i, 0)),
                  pl.BlockSpec((tm, SB_W), lambda i: (i, OFF_SG // SB_W)),
                  pl.BlockSpec((tm, POOL_W), lambda i: (i, 0)),
                  pl.BlockSpec((tm, POOL_W), lambda i: (i, OFF_PG // POOL_W)),
                  pl.BlockSpec((POOL_W, POOL_W), lambda i: (0, 0)),
                  pl.BlockSpec((1, POOL_W), lambda i: (0, 0))],
        out_specs=[pl.BlockSpec((tm, FOX_W), lambda i: (i, 0)),
                   pl.BlockSpec((tm, FOX_W), lambda i: (i, 0)),
                   pl.BlockSpec((tm, SB_W), lambda i: (i, 0)),
                   pl.BlockSpec((tm, SB_W), lambda i: (i, 0)),
                   pl.BlockSpec((tm, POOL_W), lambda i: (i, 0)),
                   pl.BlockSpec((tm, POOL_W), lambda i: (i, 0)),
                   pl.BlockSpec((1, POOL_W), lambda i: (0, 0)),
                   pl.BlockSpec((POOL_W, POOL_W), lambda i: (0, 0))],
        out_shape=[jax.ShapeDtypeStruct((S, FOX_W), F32), jax.ShapeDtypeStruct((S, FOX_W), BF16),
                   jax.ShapeDtypeStruct((S, SB_W), BF16), jax.ShapeDtypeStruct((S, SB_W), BF16),
                   jax.ShapeDtypeStruct((S, POOL_W), BF16), jax.ShapeDtypeStruct((S, POOL_W), F32),
                   jax.ShapeDtypeStruct((1, POOL_W), F32), jax.ShapeDtypeStruct((POOL_W, POOL_W), F32)],
        compiler_params=_cp(("arbitrary",), _VMEM_MID),
    )(dy, wout, fo, proj, so, proj, pooled, proj, wbd, scale)


def _matmul_acc(at, b, name):
    M, S = at.shape
    N = b.shape[1]
    tk = min(_TK_DW, S)
    tn = min(512, N)
    nk = S // tk

    def body(a_ref, b_ref, o_ref, acc):
        k = pl.program_id(1)

        @pl.when(k == 0)
        def _():
            acc[...] = jnp.zeros_like(acc)

        acc[...] = acc[...] + _dot(a_ref[...], b_ref[...].astype(BF16))

        @pl.when(k == nk - 1)
        def _():
            o_ref[...] = acc[...].astype(BF16)

    return pl.pallas_call(
        body, name=name,
        grid=(N // tn, nk),
        in_specs=[pl.BlockSpec((M, tk), lambda j, k: (0, k)), pl.BlockSpec((tk, tn), lambda j, k: (k, j))],
        out_specs=pl.BlockSpec((M, tn), lambda j, k: (0, j)),
        out_shape=jax.ShapeDtypeStruct((M, N), BF16),
        scratch_shapes=[pltpu.VMEM((M, tn), F32)],
        compiler_params=_cp(("parallel", "arbitrary"), _VMEM_MID),
    )(at, b)


def _pool_bwd(dpooled):
    S = dpooled.shape[0]

    def body(d_ref, o_ref):
        d = d_ref[...]
        t = lax.broadcasted_iota(jnp.int32, d.shape, 0)
        lane = lax.broadcasted_iota(jnp.int32, d.shape, 1)
        cnt = jnp.minimum(t + 1, _pool_window_lanes(d.shape)).astype(F32)
        u = d / cnt

        def fwd(a, k):
            return jnp.where(t < S - k, pltpu.roll(a, S - k, 0), 0.0)

        s1 = u + fwd(u, 1)
        s2 = s1 + fwd(s1, 2)
        s4 = s2 + fwd(s2, 4)
        s8 = s4 + fwd(s4, 8)
        win = jnp.where(lane < 64, s1, jnp.where(lane < 128, s2, jnp.where(lane < 192, s4, s8)))
        o_ref[...] = (win - d).astype(BF16)

    return pl.pallas_call(
        body, name="pool_bwd",
        grid=(1,),
        in_specs=[pl.BlockSpec((S, POOL_W), lambda i: (0, 0))],
        out_specs=pl.BlockSpec((S, POOL_W), lambda i: (0, 0)),
        out_shape=jax.ShapeDtypeStruct((S, POOL_W), BF16),
        compiler_params=_cp(("arbitrary",), _VMEM_BIG),
    )(dpooled)


def _fox_bwd(qs, kn, proj, dfo, fo, lse, cqb, crow4, ride=None):
    S = qs.shape[0]
    T = min(_T, S)
    nq = S // T
    n_pairs = FOX_W // 128

    def body(*refs):
        if ride is None:
            q_ref, k_ref, v_ref, do_ref, o_ref, lse_ref, cq_ref, cr_ref = refs[:8]
            dq_ref, dk_ref, dv_ref, dck_ref, dcq_ref = refs[8:13]
            scr = refs[13:]
        else:
            q_ref, k_ref, v_ref, do_ref, o_ref, lse_ref, cq_ref, cr_ref, pa_ref, pb_ref = refs[:10]
            dq_ref, dk_ref, dv_ref, dck_ref, dcq_ref, ra_ref, rb_ref = refs[10:17]
            scr = refs[17:32]
            xrefs = (pa_ref, pb_ref, ra_ref, rb_ref) + tuple(refs[32:])

            @pl.when(pl.program_id(0) == 0)
            def _():
                _start_exchange("scatter", *xrefs)

        qa, qb, kta, ktb, vb, doa, dob, cka, ckb, dcka, dckb, dva, dqt, dcqa, dcqb = scr
        lane_s = _head_masks(S)
        q = q_ref[...]
        zq = jnp.zeros_like(q)
        qa[...] = jnp.where(lane_s, q, zq)
        qb[...] = jnp.where(lane_s, zq, q)
        vb[...] = v_ref[...].astype(BF16)
        do = do_ref[...].astype(BF16)
        doa[...] = jnp.where(lane_s, do, zq)
        dob[...] = jnp.where(lane_s, zq, do)
        cq = cq_ref[...]
        cka[...], ckb[...] = _spread_heads(cq)
        zs = jnp.zeros((S, 128), F32)
        dk_ref[...] = zs
        dva[...] = zs
        dcka[...] = zs
        dckb[...] = zs
        dcq_ref[...] = jnp.zeros((8, S), F32)
        row_t = lax.broadcasted_iota(jnp.int32, (128, T), 0) < HEAD_DIM

        def prep(c, carry):
            c0 = pl.multiple_of(c * T, T)
            kt = k_ref[pl.ds(c0, T), :].astype(F32).T
            kta[:, pl.ds(c0, T)] = jnp.where(row_t, kt, 0.0).astype(BF16)
            ktb[:, pl.ds(c0, T)] = jnp.where(row_t, 0.0, kt).astype(BF16)
            return carry

        lax.fori_loop(0, nq, prep, 0)
        causal = (lax.broadcasted_iota(jnp.int32, (T, T), 0) <= lax.broadcasted_iota(jnp.int32, (T, T), 1))

        heads = ((qa, kta, doa, cka, dcka, dcqa), (qb, ktb, dob, ckb, dckb, dcqb))

        def kv(js, r0, lss, dls, masked):
            cr = cr_ref[:, pl.ds(r0, T)]
            c0s = [pl.multiple_of(j * T, T) for j in js]
            ks = [k_ref[pl.ds(c0, T), :] for c0 in c0s]
            vs = [vb[pl.ds(c0, T), :] for c0 in c0s]
            qhs = [hd[0][pl.ds(r0, T), :] for hd in heads]
            dohs = [hd[2][pl.ds(r0, T), :] for hd in heads]
            ss = []
            for h, hd in enumerate(heads):
                row = []
                for k, c0 in zip(ks, c0s):
                    s = _dot_nt(k, qhs[h]) - jnp.tile(hd[3][pl.ds(c0, T), :], (1, T // 128))
                    row.append(jnp.where(causal, s, NEG) if masked else s)
                ss.append(row)
            ps = [[jnp.exp(s + (cr[h:h + 1, :] - lss[h])) for s in row] for h, row in enumerate(ss)]
            dps = [[_dot_nt(v, dohs[h]) for v in vs] for h in range(2)]
            dss = [[p * (dp - dls[h]) for p, dp in zip(ps[h], dps[h])] for h in range(2)]
            pbs = [[p.astype(BF16) for p in row] for row in ps]
            dsbs = [[ds.astype(BF16) for ds in row] for row in dss]
            for t, c0 in enumerate(c0s):
                dva[pl.ds(c0, T), :] = dva[pl.ds(c0, T), :] + (_dot(pbs[0][t], dohs[0]) + _dot(pbs[1][t], dohs[1]))
                dk_ref[pl.ds(c0, T), :] = dk_ref[pl.ds(c0, T), :] + (_dot(dsbs[0][t], qhs[0]) + _dot(dsbs[1][t], qhs[1]))
            dq = None
            for h, hd in enumerate(heads):
                for t, c0 in enumerate(c0s):
                    term = _dot(hd[1][:, pl.ds(c0, T)], dsbs[h][t])
                    dq = term if dq is None else dq + term
            dqt[...] = dqt[...] + dq
            for h, hd in enumerate(heads):
                col = jnp.sum(dss[h][0], axis=0, keepdims=True)
                for ds in dss[h][1:]:
                    col = col + jnp.sum(ds, axis=0, keepdims=True)
                hd[5][0:1, :] = hd[5][0:1, :] + col
                for ds, c0 in zip(dss[h], c0s):
                    fold = ds[:, 0:128]
                    for u in range(1, T // 128):
                        fold = fold + ds[:, 128 * u:128 * (u + 1)]
                    hd[4][pl.ds(c0, T), :] = hd[4][pl.ds(c0, T), :] - fold

        def qblk(i, carry):
            r0 = pl.multiple_of(i * T, T)
            dt = (do_ref[pl.ds(r0, T), :] * o_ref[pl.ds(r0, T), :]).T
            dla = jnp.sum(jnp.where(row_t, dt, 0.0), axis=0, keepdims=True)
            dlb = jnp.sum(jnp.where(row_t, 0.0, dt), axis=0, keepdims=True)
            ls = lse_ref[:, pl.ds(r0, T)]
            lss = (ls[0:1, :], ls[1:2, :])
            back = jnp.max(ls[2:3, :]).astype(jnp.int32)
            dqt[...] = jnp.zeros((128, T), F32)
            dcqa[...] = jnp.zeros((8, T), F32)
            dcqb[...] = jnp.zeros((8, T), F32)
            kv([i], r0, lss, (dla, dlb), True)
            _for_tiles_back(i, back, lambda js: kv(js, r0, lss, (dla, dlb), False), fours=True)
            dq_ref[pl.ds(r0, T), :] = dqt[...].T
            dcq_ref[0:1, pl.ds(r0, T)] = dcqa[0:1, :]
            dcq_ref[1:2, pl.ds(r0, T)] = dcqb[0:1, :]
            return carry

        lax.fori_loop(0, nq, qblk, 0)
        dv_ref[...] = dva[...].astype(BF16)
        dck_ref[...] = jnp.where(lane_s, jnp.sum(dcka[...], axis=1, keepdims=True),
                                 jnp.sum(dckb[...], axis=1, keepdims=True))
        if ride is not None:
            @pl.when(pl.program_id(0) == n_pairs - 1)
            def _():
                _wait_exchange("scatter", *xrefs)

    extra = () if ride is None else tuple(ride)
    return pl.pallas_call(
        body, name="fox_bwd" if ride is None else "fox_bwd_exchange",
        grid=(n_pairs,),
        in_specs=[_pair_blk(S), _pair_blk(S), _pair_blk(S, OFF_FV // 128), _pair_blk(S), _pair_blk(S),
                  _pair_rows(S), _pair_blk(S), _pair_rows(S)] + [_ANY] * len(extra),
        out_specs=[_pair_blk(S), _pair_blk(S), _pair_blk(S), _pair_blk(S), _pair_rows(S)] + [_ANY] * len(extra),
        out_shape=[jax.ShapeDtypeStruct((S, FOX_W), F32), jax.ShapeDtypeStruct((S, FOX_W), F32),
                   jax.ShapeDtypeStruct((S, FOX_W), BF16), jax.ShapeDtypeStruct((S, FOX_W), F32),
                   jax.ShapeDtypeStruct((n_pairs, 8, S), F32)]
        + (_exchange_out_shapes("scatter", *extra) if extra else []),
        scratch_shapes=[pltpu.VMEM((S, 128), BF16)] * 2 + [pltpu.VMEM((128, S), BF16)] * 2
        + [pltpu.VMEM((S, 128), BF16)] * 3 + [pltpu.VMEM((S, 128), F32)] * 5
        + [pltpu.VMEM((128, T), F32)] + [pltpu.VMEM((8, T), F32)] * 2
        + (_EXCHANGE_SEMS if extra else []),
        compiler_params=_cp(("arbitrary",), _VMEM_BIG),
    )(qs, kn, proj, dfo, fo, lse, cqb, crow4, *extra)


def _sb_bwd(psb, dso, ltot, tril):
    S = psb.shape[0]
    T = tril.shape[0]
    nq = S // T
    n_pairs = SB_W // 128
    H = 2 * n_pairs

    def body(q_ref, k_ref, v_ref, do_ref, lt_ref, tri_ref, dq_ref, dk_ref, dv_ref,
             qm, kt, dka, dva, dqt, rr, gg):
        lane_s = _head_masks(S)
        for p in range(n_pairs):
            q = (q_ref[:, 128 * p:128 * (p + 1)].astype(F32) * Q_SCALE).astype(BF16)
            zq = jnp.zeros_like(q)
            qm[2 * p] = jnp.where(lane_s, q, zq)
            qm[2 * p + 1] = jnp.where(lane_s, zq, q)
        dka[...] = jnp.zeros((n_pairs, S, 128), F32)
        dva[...] = jnp.zeros((n_pairs, S, 128), F32)
        row_t = lax.broadcasted_iota(jnp.int32, (128, T), 0) < HEAD_DIM
        lane_t = lax.broadcasted_iota(jnp.int32, (T, 128), 1) < HEAD_DIM

        def prep(c, carry):
            c0 = pl.multiple_of(c * T, T)
            for p in range(n_pairs):
                kt[p, :, pl.ds(c0, T)] = k_ref[pl.ds(c0, T), 128 * p:128 * (p + 1)].astype(F32).T.astype(BF16)
            return carry

        lax.fori_loop(0, nq, prep, 0)
        strict = (lax.broadcasted_iota(jnp.int32, (T, T), 0) < lax.broadcasted_iota(jnp.int32, (T, T), 1))

        def own(x, h, mask):
            z = jnp.zeros_like(x)
            return jnp.where(mask, x, z) if h % 2 == 0 else jnp.where(mask, z, x)

        def pair(ref, p, c0):
            return ref[pl.ds(c0, T), 128 * p:128 * (p + 1)]

        def kv(tiles, r0, lts):
            tri = tri_ref[...]
            c0s = [pl.multiple_of(j * T, T) for j, _ in tiles]
            qhs = [qm[h, pl.ds(r0, T), :] for h in range(H)]
            dohs = [own(pair(do_ref, h // 2, r0), h, lane_t) for h in range(H)]
            zs = [[_dot_nt(pair(k_ref, h // 2, c0), qhs[h]) for c0 in c0s] for h in range(H)]
            das = [[_dot_nt(pair(v_ref, h // 2, c0), dohs[h]) for c0 in c0s] for h in range(H)]
            es, lbs = [], []
            for row in zs:
                erow, lrow = [], []
                for z, (_, masked) in zip(row, tiles):
                    e, sp = _softplus_parts(z)
                    erow.append(e)
                    lrow.append(jnp.where(strict, -sp, 0.0) if masked else -sp)
                es.append(erow)
                lbs.append(lrow)
            pres = [[_mm2(lb, tri, left=True) for lb in row] for row in lbs]
            aas, r_ends = [], []
            for h in range(H):
                r = rr[h, 0:1, :]
                arow = []
                for z, lb, pre, (_, masked) in zip(zs[h], lbs[h], pres[h], tiles):
                    a = jnp.exp(z + lb + ((lts[h] - r) - pre))
                    arow.append(jnp.where(strict, a, 0.0) if masked else a)
                    r = r + pre[T - 1:T, :]
                aas.append(arow)
                r_ends.append(r)
            gs = [[a * da for a, da in zip(arow, drow)] for arow, drow in zip(aas, das)]
            gpres = [[_mm2(g, tri, left=True) for g in row] for row in gs]
            dzbs, g_ends = [], []
            for h in range(H):
                gc = gg[h, 0:1, :]
                drow = []
                for z, e, g, gpre, (_, masked) in zip(zs[h], es[h], gs[h], gpres[h], tiles):
                    inv = 1.0 / (1.0 + e)
                    pos = z >= 0.0
                    sig = jnp.where(pos, 1.0, e) * inv
                    oms = jnp.where(pos, e, 1.0) * inv
                    dz = g * oms - sig * (gc + (gpre - g))
                    if masked:
                        dz = jnp.where(strict, dz, 0.0)
                    drow.append(dz.astype(BF16))
                    gc = gc + gpre[T - 1:T, :]
                dzbs.append(drow)
                g_ends.append(gc)
            for p in range(n_pairs):
                a, b = 2 * p, 2 * p + 1
                dq = None
                for h in (a, b):
                    for t, c0 in enumerate(c0s):
                        term = _dot(own(kt[p, :, pl.ds(c0, T)], h, row_t), dzbs[h][t])
                        dq = term if dq is None else dq + term
                dqt[p] = dqt[p] + dq
                for t, c0 in enumerate(c0s):
                    dka[p, pl.ds(c0, T), :] = dka[p, pl.ds(c0, T), :] + (_dot(dzbs[a][t], qhs[a]) + _dot(dzbs[b][t], qhs[b]))
                    dva[p, pl.ds(c0, T), :] = dva[p, pl.ds(c0, T), :] + (_dot(aas[a][t].astype(BF16), dohs[a])
                                                                      + _dot(aas[b][t].astype(BF16), dohs[b]))
            for h in range(H):
                rr[h, 0:1, :] = r_ends[h]
                gg[h, 0:1, :] = g_ends[h]

        def qblk(i, carry):
            r0 = pl.multiple_of(i * T, T)
            lts = []
            for p in range(n_pairs):
                lt = lt_ref[p, :, pl.ds(r0, T)]
                lts += [lt[0:1, :], lt[1:2, :]]
            back = jnp.max(lt_ref[0, 2:3, pl.ds(r0, T)]).astype(jnp.int32)
            dqt[...] = jnp.zeros((n_pairs, 128, T), F32)
            rr[...] = jnp.zeros((H, 8, T), F32)
            gg[...] = jnp.zeros((H, 8, T), F32)

            def inner(j, c):
                kv([(j, False)], r0, lts)
                return c

            @pl.when(back == 0)
            def _():
                kv([(i, True)], r0, lts)

            @pl.when(back > 0)
            def _():
                lax.fori_loop(i - back, i - 1, inner, 0)
                kv([(i - 1, False), (i, True)], r0, lts)

            for p in range(n_pairs):
                dq_ref[pl.ds(r0, T), 128 * p:128 * (p + 1)] = (dqt[p] * Q_SCALE).T.astype(BF16)
            return carry

        lax.fori_loop(0, nq, qblk, 0)
        for p in range(n_pairs):
            dk_ref[:, 128 * p:128 * (p + 1)] = dka[p].astype(BF16)
            dv_ref[:, 128 * p:128 * (p + 1)] = dva[p].astype(BF16)

    wide = lambda off: pl.BlockSpec((S, SB_W), lambda g: (0, off), pipeline_mode=pl.Buffered(1))
    return pl.pallas_call(
        body, name="sb_bwd",
        grid=(1,),
        in_specs=[wide(0), wide(1), wide(2), wide(0),
                  pl.BlockSpec((n_pairs, 8, S), lambda g: (0, 0, 0), pipeline_mode=pl.Buffered(1)),
                  pl.BlockSpec((T, T), lambda g: (0, 0))],
        out_specs=[wide(0), wide(0), wide(0)],
        out_shape=[jax.ShapeDtypeStruct((S, SB_W), BF16)] * 3,
        scratch_shapes=[pltpu.VMEM((H, S, 128), BF16), pltpu.VMEM((n_pairs, 128, S), BF16),
                        pltpu.VMEM((n_pairs, S, 128), F32), pltpu.VMEM((n_pairs, S, 128), F32),
                        pltpu.VMEM((n_pairs, 128, T), F32), pltpu.VMEM((H, 8, T), F32), pltpu.VMEM((H, 8, T), F32)],
        compiler_params=_cp(("arbitrary",), _VMEM_BIG),
    )(psb, psb, psb, dso, ltot, tril)


def _head_norm_bwd(x, g, dy, bd):
    ss = _mm2(x * x, bd)
    r = lax.rsqrt(ss * (1.0 / HEAD_DIM) + EPS)
    xr = x * r
    gdy = g * dy
    m = _mm2(xr * gdy, bd) * (1.0 / HEAD_DIM)
    return r * (gdy - xr * m), dy * xr


def _qk_bwd(dqs, dkn, proj, pff, bfp, gq, gk, bd, dccol, triu):
    S = proj.shape[0]
    T = triu.shape[0]
    n = S // T
    rev = lambda col: (lambda i: (n - 1 - i, col))

    def body(dq_ref, dk_ref, q_ref, k_ref, ff_ref, b_ref, gq_ref, gk_ref, bd_ref, dc_ref, tri_ref,
             dfq_ref, dfk_ref, dff_ref, dgq_ref, dgk_ref, dbf_ref, carry):
        @pl.when(pl.program_id(0) == 0)
        def _():
            carry[...] = jnp.zeros_like(carry)
            dgq_ref[...] = jnp.zeros_like(dgq_ref)
            dgk_ref[...] = jnp.zeros_like(dgk_ref)
            dbf_ref[...] = jnp.zeros_like(dbf_ref)

        bdv = bd_ref[...]
        dxq, gq_rows = _head_norm_bwd(q_ref[...], gq_ref[...], dq_ref[...] * Q_SCALE, bdv)
        dfq_ref[...] = dxq.astype(BF16)
        dgq_ref[...] = dgq_ref[...] + jnp.sum(gq_rows, axis=0, keepdims=True)
        dxk, gk_rows = _head_norm_bwd(k_ref[...], gk_ref[...], dk_ref[...], bdv)
        dfk_ref[...] = dxk.astype(BF16)
        dgk_ref[...] = dgk_ref[...] + jnp.sum(gk_rows, axis=0, keepdims=True)
        dlf = _mm3(dc_ref[...], tri_ref[...], left=True) + carry[0:1, :]
        carry[0:1, :] = dlf[0:1, :]
        u = ff_ref[...] + b_ref[...]
        lane = lax.broadcasted_iota(jnp.int32, u.shape, 1)
        dff = jnp.where(lane < N_FF, dlf * _sigmoid(-u), 0.0)
        dff_ref[...] = dff.astype(BF16)
        dbf_ref[...] = dbf_ref[...] + jnp.sum(dff, axis=0, keepdims=True)

    return pl.pallas_call(
        body, name="qk_bwd",
        grid=(n,),
        in_specs=[pl.BlockSpec((T, FOX_W), rev(0)), pl.BlockSpec((T, FOX_W), rev(0)),
                  pl.BlockSpec((T, FOX_W), rev(OFF_FQ // FOX_W)), pl.BlockSpec((T, FOX_W), rev(OFF_FK // FOX_W)),
                  pl.BlockSpec((T, N_FFPAD), rev(0)),
                  pl.BlockSpec((1, N_FFPAD), lambda i: (0, 0)),
                  pl.BlockSpec((1, FOX_W), lambda i: (0, 0)), pl.BlockSpec((1, FOX_W), lambda i: (0, 0)),
                  pl.BlockSpec((FOX_W, FOX_W), lambda i: (0, 0)),
                  pl.BlockSpec((T, N_FFPAD), rev(0)),
                  pl.BlockSpec((T, T), lambda i: (0, 0))],
        out_specs=[pl.BlockSpec((T, FOX_W), rev(0)), pl.BlockSpec((T, FOX_W), rev(0)),
                   pl.BlockSpec((T, N_FFPAD), rev(0)),
                   pl.BlockSpec((1, FOX_W), lambda i: (0, 0)), pl.BlockSpec((1, FOX_W), lambda i: (0, 0)),
                   pl.BlockSpec((1, N_FFPAD), lambda i: (0, 0))],
        out_shape=[jax.ShapeDtypeStruct((S, FOX_W), BF16), jax.ShapeDtypeStruct((S, FOX_W), BF16),
                   jax.ShapeDtypeStruct((S, N_FFPAD), BF16),
                   jax.ShapeDtypeStruct((1, FOX_W), F32), jax.ShapeDtypeStruct((1, FOX_W), F32),
                   jax.ShapeDtypeStruct((1, N_FFPAD), F32)],
        scratch_shapes=[pltpu.VMEM((8, N_FFPAD), F32)],
        compiler_params=_cp(("arbitrary",), _VMEM_MID),
    )(dqs, dkn, proj, proj, pff, bfp, gq, gk, bd, dccol, triu)


def _dproj_layout(pieces):
    offs, o = [], 0
    for p in pieces:
        offs.append(o)
        o += p.shape[1]
    assert o == N_MAIN
    return offs


def _inproj_bwd_dx(pieces, dff, wm, wff, x, g, dy, ride=None):
    S, D = x.shape
    tm = min(_TM_DX, S)
    steps = S // tm
    offs = _dproj_layout(pieces)
    n = len(pieces)

    def body(*refs):
        p_refs = refs[:n]
        if ride is None:
            dff_ref, w_ref, wff_ref, x_ref, g_ref, dy_ref, dx_ref, dg_ref = refs[n:]
        else:
            dff_ref, w_ref, wff_ref, x_ref, g_ref, dy_ref, pa_ref, pb_ref = refs[n:n + 8]
            dx_ref, dg_ref, ra_ref, rb_ref = refs[n + 8:n + 12]
            xrefs = (pa_ref, pb_ref, ra_ref, rb_ref) + tuple(refs[n + 12:])

        @pl.when(pl.program_id(0) == 0)
        def _():
            dg_ref[...] = jnp.zeros_like(dg_ref)
            if ride is not None:
                _start_exchange("scatter", *xrefs)

        dh = _dot_nt(dff_ref[...], wff_ref[...])
        for p_ref, off in zip(p_refs, offs):
            dh = dh + _dot_nt(p_ref[...], w_ref[:, off:off + p_ref.shape[1]])
        xv = x_ref[...]
        r = _rms_rows(xv)
        xr = xv * r
        dg_ref[...] = dg_ref[...] + jnp.sum(dh * xr, axis=0, keepdims=True)
        gdh = g_ref[...] * dh
        m = jnp.mean(gdh * xr, axis=-1, keepdims=True)
        dx_ref[...] = dy_ref[...] + r * (gdh - xr * m)
        if ride is not None:
            @pl.when(pl.program_id(0) == steps - 1)
            def _():
                _wait_exchange("scatter", *xrefs)

    extra = () if ride is None else tuple(ride)
    return pl.pallas_call(
        body, name="inproj_bwd_dx" if ride is None else "inproj_bwd_dx_exchange",
        grid=(steps,),
        in_specs=[pl.BlockSpec((tm, p.shape[1]), lambda i: (i, 0)) for p in pieces]
        + [pl.BlockSpec((tm, N_FFPAD), lambda i: (i, 0)),
                  pl.BlockSpec((D, N_MAIN), lambda i: (0, 0)),
                  pl.BlockSpec((D, N_FFPAD), lambda i: (0, 0)),
                  pl.BlockSpec((tm, D), lambda i: (i, 0)),
                  pl.BlockSpec((1, D), lambda i: (0, 0)),
                  pl.BlockSpec((tm, D), lambda i: (i, 0))] + [_ANY] * len(extra),
        out_specs=[pl.BlockSpec((tm, D), lambda i: (i, 0)), pl.BlockSpec((1, D), lambda i: (0, 0))] + [_ANY] * len(extra),
        out_shape=[jax.ShapeDtypeStruct((S, D), F32), jax.ShapeDtypeStruct((1, D), F32)]
        + (_exchange_out_shapes("scatter", *extra) if extra else []),
        scratch_shapes=_EXCHANGE_SEMS if extra else [],
        compiler_params=_cp(("arbitrary",), _VMEM_WIDE),
    )(*pieces, dff, wm, wff, x, g, dy, *extra)


def _inproj_bwd_dw(ht, pieces, dff):
    D, S = ht.shape
    tk = min(_TK_DW, S)
    nk = S // tk
    offs = _dproj_layout(pieces)
    n = len(pieces)

    def body(*refs):
        ht_ref, p_refs, dff_ref = refs[0], refs[1:1 + n], refs[1 + n]
        dw_ref, dwff_ref, acc, accff = refs[2 + n:]
        k = pl.program_id(0)

        @pl.when(k == 0)
        def _():
            acc[...] = jnp.zeros_like(acc)
            accff[...] = jnp.zeros_like(accff)

        hb = ht_ref[...]
        for p_ref, off in zip(p_refs, offs):
            w = p_ref.shape[1]
            acc[:, off:off + w] = acc[:, off:off + w] + _dot(hb, p_ref[...])
        accff[...] = accff[...] + _dot(hb, dff_ref[...])

        @pl.when(k == nk - 1)
        def _():
            dw_ref[...] = acc[...].astype(BF16)
            dwff_ref[...] = accff[...].astype(BF16)

    return pl.pallas_call(
        body, name="inproj_bwd_dw",
        grid=(nk,),
        in_specs=[pl.BlockSpec((D, tk), lambda k: (0, k))]
        + [pl.BlockSpec((tk, p.shape[1]), lambda k: (k, 0)) for p in pieces]
        + [pl.BlockSpec((tk, N_FFPAD), lambda k: (k, 0))],
        out_specs=[pl.BlockSpec((D, N_MAIN), lambda k: (0, 0), pipeline_mode=pl.Buffered(1)),
                   pl.BlockSpec((D, N_FFPAD), lambda k: (0, 0), pipeline_mode=pl.Buffered(1))],
        out_shape=[jax.ShapeDtypeStruct((D, N_MAIN), BF16), jax.ShapeDtypeStruct((D, N_FFPAD), BF16)],
        scratch_shapes=[pltpu.VMEM((D, N_MAIN), F32), pltpu.VMEM((D, N_FFPAD), F32)],
        compiler_params=_cp(("arbitrary",), _VMEM_BIG),
    )(ht, *pieces, dff)


def _constants(T, rows):
    tril = jnp.tril(jnp.ones((T, T), F32)).astype(BF16)
    tril_rows = jnp.tril(jnp.ones((rows, rows), F32)).astype(BF16)
    hid = jnp.arange(FOX_W) // HEAD_DIM
    bd = (hid[:, None] == hid[None, :]).astype(BF16)
    ex = (jnp.arange(N_FFPAD)[:, None] == hid[None, :]).astype(BF16)
    return tril, tril.T, bd, ex, tril_rows, tril_rows.T


def _crow4(ccol, T):
    S = ccol.shape[0]
    c = ccol[:, :FOX_HEADS].T
    last = jnp.pad(c[:, T - 1::T], ((0, 0), (0, S - S // T)))
    rows = jnp.concatenate([c.reshape(FOX_HEADS // 2, 2, S), last.reshape(FOX_HEADS // 2, 2, S)], axis=1)
    return jnp.pad(rows, ((0, 0), (0, 4), (0, 0)))


def _layer_fwd(x, lw, consts, ride=None):
    tril, triu, bd, ex, tril_rows, _ = consts
    proj, pff, ht, psb = _inproj_fwd(x, lw["g"], lw["wm"], lw["wff"])
    qs, kn, ccol, cqb = _fox_prep(proj, pff, lw["bfp"], lw["gq"], lw["gk"], bd, ex, tril_rows)
    crow4 = _crow4(ccol, tril.shape[0])
    fo, lse, *gathered = _fox_fwd(qs, kn, proj, cqb, crow4, ride)
    so, ltot = _sb_fwd(psb, triu)
    pooled = _pool_fwd(proj)
    y, mixedt = _mix_out(fo, so, pooled, proj, lw["wbd"], lw["scale"], lw["wout"], x)
    return y, (x, proj, pff, ht, psb, qs, kn, cqb, crow4, fo, lse, so, ltot, pooled, mixedt), gathered


def _layer_bwd(dy, saved, lw, consts, ride=None, exchange_own=False):
    tril, _, bd, _, _, triu_rows = consts
    x, proj, pff, ht, psb, qs, kn, cqb, crow4, fo, lse, so, ltot, pooled, mixedt = saved
    S = x.shape[0]
    dfo, dfg, dso, dsg, dpg, dpooled, dscale, dwbd = _gate_bwd(dy, lw["wout"], fo, so, pooled, proj, lw["wbd"], lw["scale"])
    dwout = _matmul_acc(mixedt, dy, "dw_out")
    dpx = _pool_bwd(dpooled)
    dqs, dkn, dfv, dck, dcq4, *received = _fox_bwd(qs, kn, proj, dfo, fo, lse, cqb, crow4, ride)
    dsq, dsk, dsv = _sb_bwd(psb, dso, ltot, tril)
    dc8 = dck[:, ::HEAD_DIM] + dcq4[:, :2, :].reshape(FOX_HEADS, S).T
    dccol = jnp.pad(dc8, ((0, 0), (0, N_FFPAD - FOX_HEADS)))
    dfq, dfk, dff, dgq, dgk, dbf = _qk_bwd(dqs, dkn, proj, pff, lw["bfp"], lw["gq"], lw["gk"], bd, dccol, triu_rows)
    pieces = [dfq, dfk, dfv, dfg, dpx, dpg, dsq, dsk, dsv, dsg]
    dwm, dwff = _inproj_bwd_dw(ht, pieces, dff)
    dwm_t = dwm.T
    dwin_t = jnp.concatenate([dwm_t[:OFF_PX], dwff.T[:N_FF], dwm_t[OFF_PX:]], axis=0)
    own = _grad_parts({"w_in_t": dwin_t, "w_out": dwout}) if exchange_own else None
    dx, dng, *received_own = _inproj_bwd_dx(pieces, dff, lw["wm"], lw["wff"], x, lw["g"], dy, own)
    grads = {
        "norm_g": dng[0],
        "w_in_t": dwin_t,
        "b_f": dbf[0, :N_FF],
        "q_norm_g": dgq[0].reshape(FOX_HEADS, HEAD_DIM).sum(0),
        "k_norm_g": dgk[0].reshape(FOX_HEADS, HEAD_DIM).sum(0),
        "w_pool": jnp.stack([dwbd[64 * i:64 * i + 64, 64 * i:64 * i + 64] for i in range(4)]),
        "pool_scale": dscale[0],
        "w_out": dwout,
    }
    return dx, grads, received, received_own


def _layer_weights(l, norm_g, gin, b_f, q_norm_g, k_norm_g, w_pool, pool_scale, gout):
    D = gin.shape[1]
    w = gin.transpose(1, 0, 2).reshape(D, D_IN)
    wm = jnp.concatenate([w[:, :2048], w[:, 2048 + N_FF:]], axis=1)
    wff = jnp.pad(w[:, 2048:2048 + N_FF], ((0, 0), (0, N_FFPAD - N_FF)))
    grp = jnp.arange(POOL_W) // 64
    wbd = jnp.where(grp[:, None] == grp[None, :], jnp.tile(w_pool[l].transpose(1, 0, 2).reshape(64, POOL_W), (4, 1)), 0.0)
    return {
        "g": norm_g[l].reshape(1, D),
        "wm": wm, "wff": wff,
        "bfp": jnp.pad(b_f[l], (0, N_FFPAD - N_FF)).reshape(1, N_FFPAD),
        "gq": jnp.tile(q_norm_g[l], FOX_HEADS).reshape(1, FOX_W),
        "gk": jnp.tile(k_norm_g[l], FOX_HEADS).reshape(1, FOX_W),
        "wbd": wbd.astype(BF16),
        "scale": pool_scale[l].reshape(1, POOL_W),
        "wout": gout.reshape(D_MIX, D),
    }


def _grad_parts(g):
    dwin_t, dwout = g["w_in_t"].astype(BF16), g["w_out"].astype(BF16)
    return (dwin_t.reshape(N_DEV, D_IN // N_DEV, dwin_t.shape[1]),
            dwout.reshape(N_DEV, D_MIX // N_DEV, dwout.shape[1]))


def _train_step(x, target, norm_g, win_sh, b_f, q_norm_g, k_norm_g, w_pool, pool_scale, wout_sh):
    L = norm_g.shape[0]
    consts = _constants(min(_T, x.shape[0]), min(_TM_ROWS, x.shape[0]))
    gathered = _gather_two_level(win_sh[0], wout_sh[0], "gather_weights")
    lws, saved = [], []
    h = x
    for l in range(L):
        lws.append(_layer_weights(l, norm_g, gathered[0], b_f, q_norm_g, k_norm_g, w_pool, pool_scale, gathered[1]))
        ride = (win_sh[l + 1], wout_sh[l + 1]) if l + 1 < L else None
        h, sv, gathered = _layer_fwd(h, lws[l], consts, ride)
        saved.append(sv)
    dy, loss = _loss_head(h, target)
    grads, received = [None] * L, [None] * L
    ride = None
    for l in reversed(range(L)):
        dy, grads[l], got, got_own = _layer_bwd(dy, saved[l], lws[l], consts, ride, exchange_own=(l == 0))
        if ride is not None:
            received[l + 1] = got
        if l == 0:
            received[0] = got_own
        else:
            ride = _grad_parts(grads[l])
    return loss, dy, grads, received


def _mesh_pos():
    return lax.axis_index("x"), lax.axis_index("y"), lax.axis_index("c")


_FLIPS = [(0, 0, 1), (1, 0, 0), (0, 1, 0), (1, 1, 0), (1, 0, 1), (0, 1, 1), (1, 1, 1)]


def _peers():
    x, y, c = _mesh_pos()
    out = []
    for fx, fy, fc in _FLIPS:
        px = 1 - x if fx else x
        py = 1 - y if fy else y
        pc = 1 - c if fc else c
        out.append(((px, py, pc), 4 * px + 2 * py + pc))
    return out, 4 * x + 2 * y + c


_EXCHANGE_SEMS = [pltpu.SemaphoreType.DMA((14,)), pltpu.SemaphoreType.DMA((14,)), pltpu.SemaphoreType.DMA((2,))]
_ANY = pl.BlockSpec(memory_space=pl.ANY)


def _exchange_copies(kind, a_ref, b_ref, oa_ref, ob_ref, send_sems, recv_sems, loc_sems):
    peers, me = _peers()
    pairs = ((a_ref, oa_ref), (b_ref, ob_ref))
    local = [pltpu.make_async_copy(src if kind == "gather" else src.at[me], dst.at[me], loc_sems.at[t])
             for t, (src, dst) in enumerate(pairs)]
    remote = []
    for k, (dev, idx) in enumerate(peers):
        for t, (src, dst) in enumerate(pairs):
            remote.append(pltpu.make_async_remote_copy(
                src_ref=src if kind == "gather" else src.at[idx], dst_ref=dst.at[me],
                send_sem=send_sems.at[2 * k + t], recv_sem=recv_sems.at[2 * k + t],
                device_id=dev, device_id_type=pl.DeviceIdType.MESH))
    return local, remote


def _start_exchange(kind, *refs):
    local, remote = _exchange_copies(kind, *refs)
    for cp in local + remote:
        cp.start()


def _wait_exchange(kind, *refs):
    local, remote = _exchange_copies(kind, *refs)
    for cp in remote:
        cp.wait_recv()
    for cp in remote:
        cp.wait_send()
    for cp in local:
        cp.wait()


def _exchange_out_shapes(kind, a, b):
    if kind == "gather":
        return [jax.ShapeDtypeStruct((N_DEV,) + a.shape, a.dtype), jax.ShapeDtypeStruct((N_DEV,) + b.shape, b.dtype)]
    return [jax.ShapeDtypeStruct(a.shape, a.dtype), jax.ShapeDtypeStruct(b.shape, b.dtype)]


def _gather_two_level(a, b, name):
    def body(a_ref, b_ref, ga_ref, gb_ref, send_sems, recv_sems, loc_sems):
        x, y, c = _mesh_pos()
        slot_of = lambda px, py, pc: 4 * px + 2 * py + pc
        me, sib = slot_of(x, y, c), slot_of(x, y, 1 - c)
        chips = [(1 - x, y), (x, 1 - y), (1 - x, 1 - y)]
        pairs = ((a_ref, ga_ref), (b_ref, gb_ref))

        def copy(k, t, slot, to, src=None):
            dst = pairs[t][1].at[slot]
            return pltpu.make_async_remote_copy(
                src_ref=dst if src is None else src, dst_ref=dst, send_sem=send_sems.at[2 * k + t],
                recv_sem=recv_sems.at[2 * k + t], device_id=to, device_id_type=pl.DeviceIdType.MESH)

        local = [pltpu.make_async_copy(src, dst.at[me], loc_sems.at[t]) for t, (src, dst) in enumerate(pairs)]
        first = []
        for t, (src, _) in enumerate(pairs):
            first.append(copy(0, t, me, (x, y, 1 - c), src))
            first += [copy(1 + j, t, me, (*chip, c), src) for j, chip in enumerate(chips)]
        for cp in local + first:
            cp.start()
        passed = []
        for j, chip in enumerate(chips):
            for t in range(2):
                landed = slot_of(*chip, c)
                copy(1 + j, t, landed, (x, y, c)).wait_recv()
                cp = copy(4 + j, t, landed, (x, y, 1 - c))
                cp.start()
                passed.append(cp)
        for t in range(2):
            copy(0, t, sib, (x, y, c)).wait_recv()
            for j, chip in enumerate(chips):
                copy(4 + j, t, slot_of(*chip, 1 - c), (x, y, c)).wait_recv()
        for cp in first + passed:
            cp.wait_send()
        for cp in local:
            cp.wait()

    return pl.pallas_call(
        body, name=name,
        in_specs=[_ANY, _ANY], out_specs=[_ANY, _ANY],
        out_shape=_exchange_out_shapes("gather", a, b),
        scratch_shapes=_EXCHANGE_SEMS,
    )(a, b)


def _adam_math(w, g, m, v):
    m_new = ADAM_B1 * m + (1.0 - ADAM_B1) * g
    v_new = ADAM_B2 * v + (1.0 - ADAM_B2) * (g * g)
    m_hat = m_new / (1.0 - ADAM_B1 ** ADAM_STEP)
    v_hat = v_new / (1.0 - ADAM_B2 ** ADAM_STEP)
    delta = -ADAM_LR * (m_hat / (jnp.sqrt(v_hat) + ADAM_EPS) + ADAM_WD * w)
    return delta, m_new, v_new


def _sum_adamw(gparts, w, m, v, name):
    L, R, C = w.shape
    tr = min(128, R)

    def body(*refs):
        gp_refs = refs[:L]
        w_ref, m_ref, v_ref, g_ref, d_ref, nm_ref, nv_ref = refs[L:]
        for l in range(L):
            g = gp_refs[l][0].astype(F32)
            for s in range(1, N_DEV):
                g = g + gp_refs[l][s].astype(F32)
            d, mn, vn = _adam_math(w_ref[l], g, m_ref[l], v_ref[l])
            g_ref[l] = g
            d_ref[l] = d
            nm_ref[l] = mn
            nv_ref[l] = vn

    blk = pl.BlockSpec((L, tr, C), lambda r: (0, r, 0))
    return pl.pallas_call(
        body, name=name,
        grid=(R // tr,),
        in_specs=[pl.BlockSpec((N_DEV, tr, C), lambda r: (0, r, 0))] * L + [blk, blk, blk],
        out_specs=[blk, blk, blk, blk],
        out_shape=[jax.ShapeDtypeStruct((L, R, C), F32)] * 4,
        compiler_params=_cp(("parallel",), _VMEM_WIDE),
    )(*gparts, w, m, v)


def _sum_adamw_cols(gparts, w_t, m_t, v_t, name):
    C, L, D = w_t.shape
    td = min(128, D)

    def body(*refs):
        gp_refs = refs[:L]
        w_ref, m_ref, v_ref, g_ref, d_ref, nm_ref, nv_ref = refs[L:]
        starts = list(range(0, C - _ADAM_ROWS + 1, _ADAM_ROWS))
        for c0 in starts:
            rows = slice(c0, C if c0 == starts[-1] else c0 + _ADAM_ROWS)
            for l in range(L):
                g = gp_refs[l][0, rows, :].astype(F32)
                for s in range(1, N_DEV):
                    g = g + gp_refs[l][s, rows, :].astype(F32)
                g_ref[rows, l, :] = g
            d, mn, vn = _adam_math(w_ref[rows], g_ref[rows], m_ref[rows], v_ref[rows])
            d_ref[rows] = d
            nm_ref[rows] = mn
            nv_ref[rows] = vn

    blk = pl.BlockSpec((C, L, td), lambda j: (0, 0, j))
    return pl.pallas_call(
        body, name=name,
        grid=(D // td,),
        in_specs=[pl.BlockSpec((N_DEV, C, td), lambda j: (0, 0, j))] * L + [blk, blk, blk],
        out_specs=[blk, blk, blk, blk],
        out_shape=[jax.ShapeDtypeStruct((C, L, D), F32)] * 4,
        compiler_params=_cp(("parallel",), _VMEM_WIDE),
    )(*gparts, w_t, m_t, v_t)


def _small_update(gpack, wpack, mpack, vpack):
    R = gpack.shape[0]
    VM = pl.BlockSpec(memory_space=pltpu.VMEM)

    def body(g_ref, w_ref, m_ref, v_ref, gs_ref, d_ref, nm_ref, nv_ref, buf, send_sems, recv_sems):
        peers, me = _peers()
        buf[me] = g_ref[...]
        copies = []
        for k, (dev, _) in enumerate(peers):
            cp = pltpu.make_async_remote_copy(
                src_ref=g_ref, dst_ref=buf.at[me], send_sem=send_sems.at[k], recv_sem=recv_sems.at[k],
                device_id=dev, device_id_type=pl.DeviceIdType.MESH)
            cp.start()
            copies.append(cp)
        for cp in copies:
            cp.wait_recv()
        for cp in copies:
            cp.wait_send()
        g = buf[0]
        for s in range(1, N_DEV):
            g = g + buf[s]
        d, mn, vn = _adam_math(w_ref[...], g, m_ref[...], v_ref[...])
        gs_ref[...] = g
        d_ref[...] = d
        nm_ref[...] = mn
        nv_ref[...] = vn

    return pl.pallas_call(
        body, name="small_update",
        in_specs=[VM] * 4, out_specs=[VM] * 4,
        out_shape=[jax.ShapeDtypeStruct((R, 128), F32)] * 4,
        scratch_shapes=[pltpu.VMEM((N_DEV, R, 128), F32), pltpu.SemaphoreType.DMA((7,)), pltpu.SemaphoreType.DMA((7,))],
        compiler_params=_cp(None, _VMEM_MID),
    )(gpack, wpack, mpack, vpack)


_SMALL = ("norm_g", "b_f", "q_norm_g", "k_norm_g", "w_pool", "pool_scale")


def _pack(parts):
    flat = jnp.concatenate([p.reshape(-1) for p in parts])
    n = flat.shape[0]
    rows = -(-n // (8 * 128)) * 8
    return jnp.pad(flat, (0, rows * 128 - n)).reshape(rows, 128)


def _unpack(packed, like):
    flat = packed.reshape(-1)
    out, o = [], 0
    for p in like:
        out.append(flat[o:o + p.size].reshape(p.shape))
        o += p.size
    return out


def kernel(x, norm_g, w_in, b_f, q_norm_g, k_norm_g, w_pool, pool_scale, w_out, loss_target, m_norm_g, m_w_in, m_b_f, m_q_norm_g, m_k_norm_g, m_w_pool, m_pool_scale, m_w_out, v_norm_g, v_w_in, v_b_f, v_q_norm_g, v_k_norm_g, v_w_pool, v_pool_scale, v_w_out):
    L = w_in.shape[0]

    loss_local, dx, grads, received = _train_step(x[0], loss_target[0], norm_g, w_in.astype(BF16), b_f, q_norm_g,
                                                  k_norm_g, w_pool, pool_scale, w_out.astype(BF16))
    loss = lax.psum(loss_local, MESH_AXES)
    g = {k: jnp.stack([grads[l][k] for l in range(L)]) for k in _SMALL}

    cols = lambda a: a.transpose(2, 0, 1)
    g_win, d_win, nm_win, nv_win = [a.transpose(1, 2, 0) for a in _sum_adamw_cols(
        [r[0] for r in received], cols(w_in), cols(m_w_in), cols(v_w_in), "adamw_w_in")]
    g_wout, d_wout, nm_wout, nv_wout = _sum_adamw([r[1] for r in received], w_out, m_w_out, v_w_out, "adamw_w_out")

    ws = dict(norm_g=norm_g, b_f=b_f, q_norm_g=q_norm_g, k_norm_g=k_norm_g, w_pool=w_pool, pool_scale=pool_scale)
    ms = dict(norm_g=m_norm_g, b_f=m_b_f, q_norm_g=m_q_norm_g, k_norm_g=m_k_norm_g, w_pool=m_w_pool, pool_scale=m_pool_scale)
    vs = dict(norm_g=v_norm_g, b_f=v_b_f, q_norm_g=v_q_norm_g, k_norm_g=v_k_norm_g, w_pool=v_w_pool, pool_scale=v_pool_scale)
    like = [ws[k] for k in _SMALL]
    gs_p, d_p, nm_p, nv_p = _small_update(_pack([g[k] for k in _SMALL]), _pack(like),
                                          _pack([ms[k] for k in _SMALL]), _pack([vs[k] for k in _SMALL]))
    gs = dict(zip(_SMALL, _unpack(gs_p, like)))
    ds = dict(zip(_SMALL, _unpack(d_p, like)))
    nms = dict(zip(_SMALL, _unpack(nm_p, like)))
    nvs = dict(zip(_SMALL, _unpack(nv_p, like)))
    gs["w_in"], ds["w_in"], nms["w_in"], nvs["w_in"] = g_win, d_win, nm_win, nv_win
    gs["w_out"], ds["w_out"], nms["w_out"], nvs["w_out"] = g_wout, d_wout, nm_wout, nv_wout

    order = ("norm_g", "w_in", "b_f", "q_norm_g", "k_norm_g", "w_pool", "pool_scale", "w_out")
    return (loss, dx[None], *[gs[k] for k in order], *[ds[k] for k in order],
            *[nms[k] for k in order], *[nvs[k] for k in order])
```

```python
import jax
import jax.numpy as jnp
from jax import lax
from jax.experimental import pallas as pl
from jax.experimental.pallas import tpu as pltpu

F32 = jnp.float32
BF16 = jnp.bfloat16

EPS = 1e-6
NEG = -1e30
HEAD_DIM = 64
FOX_HEADS = 8
FOX_W = 512
POOL_W = 256
SB_W = 256
D_MIX = 1024
N_FF = 8
N_MAIN = 3584
N_FFPAD = 128
OFF_FQ, OFF_FK, OFF_FV, OFF_FG = 0, 512, 1024, 1536
OFF_PX, OFF_PG = 2048, 2304
OFF_SQ, OFF_SK, OFF_SV, OFF_SG = 2560, 2816, 3072, 3328
D_IN = 3592
Q_SCALE = HEAD_DIM ** -0.5

ADAM_LR = 0.001
ADAM_B1 = 0.9
ADAM_B2 = 0.999
ADAM_EPS = 1e-08
ADAM_WD = 0.01
ADAM_STEP = 10

N_DEV = 8
MESH_AXES = ("x", "y", "c")

_T = 256
_TM = 512
_TM_ROWS = 512
_TM_FWD, _TN_FWD = 2048, 512
_TM_DX = 512
_TK_DW = 1024
_ADAM_ROWS = 16
_VMEM_V7X = 64 << 20
_VMEM_BIG = _VMEM_V7X - (8 << 20)
_VMEM_MID = 40 << 20
_VMEM_WIDE = 48 << 20


def _cp(sem=None, vmem=None):
    kw = {}
    if sem is not None:
        kw["dimension_semantics"] = sem
    if vmem is not None:
        kw["vmem_limit_bytes"] = vmem
    return pltpu.CompilerParams(**kw)


def _dot(a, b):
    return jnp.dot(a, b, preferred_element_type=F32)


def _dot_nt(a, b):
    return lax.dot_general(a, b, (((1,), (1,)), ((), ())), preferred_element_type=F32)


def _dot_tn(a, b):
    return lax.dot_general(a, b, (((0,), (0,)), ((), ())), preferred_element_type=F32)


def _mm2(v, m, left=False):
    hi = v.astype(BF16)
    lo = (v - hi.astype(F32)).astype(BF16)
    if left:
        return _dot(m, hi) + _dot(m, lo)
    return _dot(hi, m) + _dot(lo, m)


def _mm3(v, m, left=False):
    a1 = v.astype(BF16)
    r1 = v - a1.astype(F32)
    a2 = r1.astype(BF16)
    a3 = (r1 - a2.astype(F32)).astype(BF16)
    if left:
        return _dot(m, a1) + _dot(m, a2) + _dot(m, a3)
    return _dot(a1, m) + _dot(a2, m) + _dot(a3, m)


def _sigmoid(z):
    return 1.0 / (1.0 + jnp.exp(-z))


def _rms_rows(x):
    return lax.rsqrt(jnp.mean(x * x, axis=-1, keepdims=True) + EPS)


def _inproj_fwd(x, g, wm, wff):
    S, D = x.shape
    tm = min(_TM_FWD, S)
    tn = _TN_FWD
    assert OFF_SQ % tn == 0 and N_MAIN - OFF_SQ == 4 * SB_W
    j_sb = OFF_SQ // tn

    def body(x_ref, g_ref, w_ref, wff_ref, o_ref, off_ref, ht_ref, sb_ref, h_ref):
        j = pl.program_id(1)

        @pl.when(j == 0)
        def _():
            xv = x_ref[...]
            h = (xv * _rms_rows(xv)) * g_ref[...]
            h_ref[...] = h.astype(BF16)
            ht_ref[...] = h.T.astype(BF16)
            off_ref[...] = _dot(h_ref[...], wff_ref[...])

        res = _dot(h_ref[...], w_ref[...])
        o_ref[...] = res

        @pl.when(j >= j_sb)
        def _():
            sb_ref[...] = res.astype(BF16)

    return pl.pallas_call(
        body, name="inproj_fwd",
        grid=(S // tm, N_MAIN // tn),
        in_specs=[pl.BlockSpec((tm, D), lambda i, j: (i, 0)),
                  pl.BlockSpec((1, D), lambda i, j: (0, 0)),
                  pl.BlockSpec((D, tn), lambda i, j: (0, j)),
                  pl.BlockSpec((D, N_FFPAD), lambda i, j: (0, 0))],
        out_specs=[pl.BlockSpec((tm, tn), lambda i, j: (i, j)),
                   pl.BlockSpec((tm, N_FFPAD), lambda i, j: (i, 0)),
                   pl.BlockSpec((D, tm), lambda i, j: (0, i)),
                   pl.BlockSpec((tm, tn), lambda i, j: (i, jnp.maximum(j - j_sb, 0)))],
        out_shape=[jax.ShapeDtypeStruct((S, N_MAIN), F32), jax.ShapeDtypeStruct((S, N_FFPAD), F32),
                   jax.ShapeDtypeStruct((D, S), BF16), jax.ShapeDtypeStruct((S, 4 * SB_W), BF16)],
        scratch_shapes=[pltpu.VMEM((tm, D), BF16)],
        compiler_params=_cp(("parallel", "arbitrary"), _VMEM_BIG),
    )(x, g, wm, wff)


def _head_norm(x, g, bd):
    ss = _mm2(x * x, bd)
    r = lax.rsqrt(ss * (1.0 / HEAD_DIM) + EPS)
    return (x * r) * g


def _fox_prep(proj, pff, bfp, gq, gk, bd, ex, tril):
    S = proj.shape[0]
    T = tril.shape[0]

    def body(q_ref, k_ref, ff_ref, b_ref, gq_ref, gk_ref, bd_ref, ex_ref, tri_ref,
             qs_ref, kn_ref, cc_ref, cqb_ref, carry):
        @pl.when(pl.program_id(0) == 0)
        def _():
            carry[...] = jnp.zeros_like(carry)

        bdv = bd_ref[...]
        qs_ref[...] = (_head_norm(q_ref[...], gq_ref[...], bdv) * Q_SCALE).astype(BF16)
        kn_ref[...] = _head_norm(k_ref[...], gk_ref[...], bdv).astype(BF16)
        u = ff_ref[...] + b_ref[...]
        lf = jnp.minimum(u, 0.0) - jnp.log1p(jnp.exp(-jnp.abs(u)))
        c = _mm3(lf, tri_ref[...], left=True) + carry[0:1, :]
        carry[0:1, :] = c[T - 1:T, :]
        cc_ref[...] = c
        cqb_ref[...] = _mm3(c, ex_ref[...])

    return pl.pallas_call(
        body, name="fox_prep",
        grid=(S // T,),
        in_specs=[pl.BlockSpec((T, FOX_W), lambda i: (i, OFF_FQ // FOX_W)),
                  pl.BlockSpec((T, FOX_W), lambda i: (i, OFF_FK // FOX_W)),
                  pl.BlockSpec((T, N_FFPAD), lambda i: (i, 0)),
                  pl.BlockSpec((1, N_FFPAD), lambda i: (0, 0)),
                  pl.BlockSpec((1, FOX_W), lambda i: (0, 0)),
                  pl.BlockSpec((1, FOX_W), lambda i: (0, 0)),
                  pl.BlockSpec((FOX_W, FOX_W), lambda i: (0, 0)),
                  pl.BlockSpec((N_FFPAD, FOX_W), lambda i: (0, 0)),
                  pl.BlockSpec((T, T), lambda i: (0, 0))],
        out_specs=[pl.BlockSpec((T, FOX_W), lambda i: (i, 0)),
                   pl.BlockSpec((T, FOX_W), lambda i: (i, 0)),
                   pl.BlockSpec((T, N_FFPAD), lambda i: (i, 0)),
                   pl.BlockSpec((T, FOX_W), lambda i: (i, 0))],
        out_shape=[jax.ShapeDtypeStruct((S, FOX_W), BF16), jax.ShapeDtypeStruct((S, FOX_W), BF16),
                   jax.ShapeDtypeStruct((S, N_FFPAD), F32), jax.ShapeDtypeStruct((S, FOX_W), F32)],
        scratch_shapes=[pltpu.VMEM((8, N_FFPAD), F32)],
        compiler_params=_cp(("arbitrary",), _VMEM_MID),
    )(proj, proj, pff, bfp, gq, gk, bd, ex, tril)


def _pair_blk(S, off=0):
    return pl.BlockSpec((S, 128), lambda p: (0, off + p), pipeline_mode=pl.Buffered(1))


def _pair_rows(S):
    return pl.BlockSpec((None, 8, S), lambda p: (p, 0, 0), pipeline_mode=pl.Buffered(1))


def _head_masks(S):
    return lax.broadcasted_iota(jnp.int32, (S, 128), 1) < HEAD_DIM


_EXP_ZERO = 104.0


def _spread_heads(x):
    src = lax.broadcasted_iota(jnp.int32, (128, 128), 0)
    return (_mm3(x, (src == 0).astype(BF16)), _mm3(x, (src == HEAD_DIM).astype(BF16)))


def _score_bounds(q, k):
    same_head = ((lax.broadcasted_iota(jnp.int32, (128, 128), 0) < HEAD_DIM)
                 == (lax.broadcasted_iota(jnp.int32, (128, 128), 1) < HEAD_DIM)).astype(BF16)

    def max_norm2(x):
        xf = x.astype(F32)
        return jnp.max(_mm2(xf * xf, same_head), axis=0, keepdims=True)

    z = jnp.sqrt(max_norm2(q) * max_norm2(k))
    z = jnp.where(z == z, z, jnp.inf)
    return jnp.max(z[:, 0:1]) * 1.001 + 1e-3, jnp.max(z[:, 64:65]) * 1.001 + 1e-3


def _for_tiles_back(i, n, tiles_fn, fours=False):
    if fours:
        def four(t, c):
            tiles_fn([i - 1 - 4 * t, i - 2 - 4 * t, i - 3 - 4 * t, i - 4 - 4 * t])
            return c

        lax.fori_loop(0, lax.shift_right_logical(n, 2), four, 0)
        rest = i - (n & ~3)

        @pl.when((n & 2) != 0)
        def _():
            tiles_fn([rest - 1, rest - 2])
    else:
        def two(t, c):
            tiles_fn([i - 1 - 2 * t, i - 2 - 2 * t])
            return c

        lax.fori_loop(0, lax.shift_right_logical(n, 1), two, 0)

    @pl.when((n & 1) != 0)
    def _():
        tiles_fn([i - n])


def _fox_tiles_back(cr_ref, i, r0, zba, zbb):
    last = cr_ref[:, pl.ds(0, 128)]
    first = cr_ref[:, pl.ds(r0, 128)]
    alive_a = 2.0 * zba + first[0:1, 0:1] - last[2:3, :] > -_EXP_ZERO
    alive_b = 2.0 * zbb + first[1:2, 0:1] - last[3:4, :] > -_EXP_ZERO
    before = lax.broadcasted_iota(jnp.int32, (1, 128), 1) < i
    return jnp.sum((before & (alive_a | alive_b)).astype(jnp.int32))


def _fox_fwd(qs, kn, proj, cqb, crow4, ride=None):
    S = qs.shape[0]
    T = min(_T, S)
    nq = S // T
    n_pairs = FOX_W // 128

    def body(*refs):
        if ride is None:
            q_ref, k_ref, v_ref, cq_ref, cr_ref, o_ref, lse_ref = refs[:7]
            qa, qb, vta, vtb, cka, ckb, ma, mb, acca, accb = refs[7:]
        else:
            q_ref, k_ref, v_ref, cq_ref, cr_ref, wa_ref, wb_ref, o_ref, lse_ref, ga_ref, gb_ref = refs[:11]
            qa, qb, vta, vtb, cka, ckb, ma, mb, acca, accb = refs[11:21]
            xrefs = (wa_ref, wb_ref, ga_ref, gb_ref) + tuple(refs[21:])

            @pl.when(pl.program_id(0) == 0)
            def _():
                _start_exchange("gather", *xrefs)

        lane_s = _head_masks(S)
        q = q_ref[...]
        zq = jnp.zeros_like(q)
        qa[...] = jnp.where(lane_s, q, zq)
        qb[...] = jnp.where(lane_s, zq, q)
        cq = cq_ref[...]
        cka[...], ckb[...] = _spread_heads(cq)
        lse_ref[...] = jnp.zeros((8, S), F32)
        row_t = lax.broadcasted_iota(jnp.int32, (128, T), 0) < HEAD_DIM
        zba, zbb = _score_bounds(q, k_ref[...])

        def prep(c, carry):
            c0 = pl.multiple_of(c * T, T)
            vt = v_ref[pl.ds(c0, T), :].T
            vta[:, pl.ds(c0, T)] = jnp.where(row_t, vt, 1.0).astype(BF16)
            vtb[:, pl.ds(c0, T)] = jnp.where(row_t, 1.0, vt).astype(BF16)
            return carry

        lax.fori_loop(0, nq, prep, 0)
        causal = (lax.broadcasted_iota(jnp.int32, (T, T), 0) <= lax.broadcasted_iota(jnp.int32, (T, T), 1))

        heads = ((qa, vta, cka, ma, acca), (qb, vtb, ckb, mb, accb))

        def kv(js, r0, masked):
            cr = cr_ref[:, pl.ds(r0, T)]
            c0s = [pl.multiple_of(j * T, T) for j in js]
            ks = [k_ref[pl.ds(c0, T), :] for c0 in c0s]
            ss = []
            for h, (qr, _, ckr, _, _) in enumerate(heads):
                qh = qr[pl.ds(r0, T), :]
                row = []
                for k, c0 in zip(ks, c0s):
                    s = _dot_nt(k, qh) - jnp.tile(ckr[pl.ds(c0, T), :], (1, T // 128))
                    row.append(jnp.where(causal, s, NEG) if masked else s)
                ss.append(row)
            ms = []
            for h, (row, (_, _, _, mr, _)) in enumerate(zip(ss, heads)):
                top = row[0]
                for s in row[1:]:
                    top = jnp.maximum(top, s)
                m_old = mr[0:1, :]
                ms.append((m_old, jnp.maximum(m_old, jnp.max(top, axis=0, keepdims=True) + cr[h:h + 1, :])))
            ps = [[jnp.exp(s + (cr[h:h + 1, :] - m_new)).astype(BF16) for s in row]
                  for h, (row, (_, m_new)) in enumerate(zip(ss, ms))]
            pvs = []
            for row, (_, vr, _, _, _) in zip(ps, heads):
                pv = _dot(vr[:, pl.ds(c0s[0], T)], row[0])
                for p, c0 in zip(row[1:], c0s[1:]):
                    pv = pv + _dot(vr[:, pl.ds(c0, T)], p)
                pvs.append(pv)
            for pv, (m_old, m_new), (_, _, _, mr, ar) in zip(pvs, ms, heads):
                ar[...] = jnp.exp(m_old - m_new) * ar[...] + pv
                mr[0:1, :] = m_new

        def qblk(i, carry):
            r0 = pl.multiple_of(i * T, T)
            ma[...] = jnp.full((8, T), NEG, F32)
            mb[...] = jnp.full((8, T), NEG, F32)
            acca[...] = jnp.zeros((128, T), F32)
            accb[...] = jnp.zeros((128, T), F32)
            kv([i], r0, True)
            done = _fox_tiles_back(cr_ref, i, r0, zba, zbb)
            _for_tiles_back(i, done, lambda js: kv(js, r0, False), fours=True)
            aa = acca[...]
            ab = accb[...]
            la = aa[64:65, :]
            lb = ab[0:1, :]
            o_ref[pl.ds(r0, T), :] = jnp.where(row_t, aa / la, ab / lb).T
            lse_ref[0:1, pl.ds(r0, T)] = ma[0:1, :] + jnp.log(la)
            lse_ref[1:2, pl.ds(r0, T)] = mb[0:1, :] + jnp.log(lb)
            lse_ref[2:3, pl.ds(r0, T)] = jnp.broadcast_to(done.astype(F32), (1, T))
            return carry

        lax.fori_loop(0, nq, qblk, 0)
        if ride is not None:
            @pl.when(pl.program_id(0) == n_pairs - 1)
            def _():
                _wait_exchange("gather", *xrefs)

    extra = () if ride is None else tuple(ride)
    return pl.pallas_call(
        body, name="fox_fwd" if ride is None else "fox_fwd_gather",
        grid=(n_pairs,),
        in_specs=[_pair_blk(S), _pair_blk(S), _pair_blk(S, OFF_FV // 128), _pair_blk(S), _pair_rows(S)]
        + [_ANY] * len(extra),
        out_specs=[_pair_blk(S), _pair_rows(S)] + [_ANY] * len(extra),
        out_shape=[jax.ShapeDtypeStruct((S, FOX_W), F32), jax.ShapeDtypeStruct((n_pairs, 8, S), F32)]
        + (_exchange_out_shapes("gather", *extra) if extra else []),
        scratch_shapes=[pltpu.VMEM((S, 128), BF16)] * 2 + [pltpu.VMEM((128, S), BF16)] * 2
        + [pltpu.VMEM((S, 128), F32)] * 2 + [pltpu.VMEM((8, T), F32)] * 2 + [pltpu.VMEM((128, T), F32)] * 2
        + (_EXCHANGE_SEMS if extra else []),
        compiler_params=_cp(("arbitrary",), _VMEM_BIG),
    )(qs, kn, proj, cqb, crow4, *extra)


def _softplus_parts(z):
    e = jnp.exp(-jnp.abs(z))
    return e, jnp.maximum(z, 0.0) + jnp.log(1.0 + e)


def _sb_fwd(psb, triu):
    S = psb.shape[0]
    T = triu.shape[0]
    nq = S // T

    n_pairs = SB_W // 128
    H = 2 * n_pairs

    def body(q_ref, k_ref, v_ref, tri_ref, o_ref, lt_ref, qm, vt, rr, acc):
        lane_s = _head_masks(S)
        zbs = []
        for p in range(n_pairs):
            q = (q_ref[:, 128 * p:128 * (p + 1)].astype(F32) * Q_SCALE).astype(BF16)
            zq = jnp.zeros_like(q)
            qm[2 * p] = jnp.where(lane_s, q, zq)
            qm[2 * p + 1] = jnp.where(lane_s, zq, q)
            zbs += list(_score_bounds(q, k_ref[:, 128 * p:128 * (p + 1)]))
        lt_ref[...] = jnp.zeros((n_pairs, 8, S), F32)
        row_t = lax.broadcasted_iota(jnp.int32, (128, T), 0) < HEAD_DIM

        def prep(c, carry):
            c0 = pl.multiple_of(c * T, T)
            for p in range(n_pairs):
                vt[p, :, pl.ds(c0, T)] = v_ref[pl.ds(c0, T), 128 * p:128 * (p + 1)].astype(F32).T.astype(BF16)
            return carry

        lax.fori_loop(0, nq, prep, 0)
        strict = (lax.broadcasted_iota(jnp.int32, (T, T), 0) < lax.broadcasted_iota(jnp.int32, (T, T), 1))

        def kv(tiles, r0):
            tri = tri_ref[...]
            c0s = [pl.multiple_of(j * T, T) for j, _ in tiles]
            zs = [[_dot_nt(k_ref[pl.ds(c0, T), 128 * (h // 2):128 * (h // 2 + 1)], qm[h, pl.ds(r0, T), :])
                   for c0 in c0s] for h in range(H)]
            lbs = [[jnp.where(strict, -_softplus_parts(z)[1], 0.0) if masked else -_softplus_parts(z)[1]
                    for z, (_, masked) in zip(row, tiles)] for row in zs]
            incs = [[_mm2(lb, tri, left=True) for lb in row] for row in lbs]
            avs = []
            for h in range(H):
                r = rr[h, 0:1, :]
                av = None
                for z, inc, c0, (_, masked) in zip(zs[h], incs[h], c0s, tiles):
                    a = jnp.exp(z + inc + r)
                    if masked:
                        a = jnp.where(strict, a, 0.0)
                    term = _dot(vt[h // 2, :, pl.ds(c0, T)], a.astype(BF16))
                    av = term if av is None else av + term
                    r = r + inc[0:1, :]
                avs.append((av, r))
            for h, (av, r) in enumerate(avs):
                rr[h, 0:1, :] = r
                acc[h] = acc[h] + av

        def qblk(i, carry):
            r0 = pl.multiple_of(i * T, T)
            rr[...] = jnp.zeros((H, 8, T), F32)
            acc[...] = jnp.zeros((H, 128, T), F32)

            @pl.when(i == 0)
            def _():
                kv([(i, True)], r0)

            @pl.when(i > 0)
            def _():
                kv([(i, True), (i - 1, False)], r0)

            def alive():
                m = jnp.max(rr[0, 0:1, :]) + zbs[0]
                for h in range(1, H):
                    m = jnp.maximum(m, jnp.max(rr[h, 0:1, :]) + zbs[h])
                return m > -_EXP_ZERO

            def cond(st):
                return (st[0] < i) & st[1]

            def step(st):
                kv([(i - 1 - st[0], False)], r0)
                return st[0] + 1, alive()

            done, _ = lax.while_loop(cond, step, (jnp.minimum(i, 1), alive()))
            for p in range(n_pairs):
                o_ref[pl.ds(r0, T), 128 * p:128 * (p + 1)] = jnp.where(row_t, acc[2 * p], acc[2 * p + 1]).T
                lt_ref[p, 0:1, pl.ds(r0, T)] = rr[2 * p, 0:1, :]
                lt_ref[p, 1:2, pl.ds(r0, T)] = rr[2 * p + 1, 0:1, :]
                lt_ref[p, 2:3, pl.ds(r0, T)] = jnp.broadcast_to(done.astype(F32), (1, T))
            return carry

        lax.fori_loop(0, nq, qblk, 0)

    wide = lambda off: pl.BlockSpec((S, SB_W), lambda g: (0, off), pipeline_mode=pl.Buffered(1))
    return pl.pallas_call(
        body, name="sb_fwd",
        grid=(1,),
        in_specs=[wide(0), wide(1), wide(2), pl.BlockSpec((T, T), lambda g: (0, 0))],
        out_specs=[wide(0), pl.BlockSpec((n_pairs, 8, S), lambda g: (0, 0, 0), pipeline_mode=pl.Buffered(1))],
        out_shape=[jax.ShapeDtypeStruct((S, SB_W), F32), jax.ShapeDtypeStruct((n_pairs, 8, S), F32)],
        scratch_shapes=[pltpu.VMEM((H, S, 128), BF16), pltpu.VMEM((n_pairs, 128, S), BF16),
                        pltpu.VMEM((H, 8, T), F32), pltpu.VMEM((H, 128, T), F32)],
        compiler_params=_cp(("arbitrary",), _VMEM_BIG),
    )(psb, psb, psb, triu)


def _pool_window_lanes(shape):
    lane = lax.broadcasted_iota(jnp.int32, shape, 1)
    return jnp.where(lane < 64, 2, jnp.where(lane < 128, 4, jnp.where(lane < 192, 8, 16)))


def _pool_fwd(proj):
    S = proj.shape[0]

    def body(x_ref, o_ref):
        x = x_ref[...]
        t = lax.broadcasted_iota(jnp.int32, x.shape, 0)
        lane = lax.broadcasted_iota(jnp.int32, x.shape, 1)

        def back(a, k):
            return jnp.where(t >= k, pltpu.roll(a, k, 0), 0.0)

        s1 = x + back(x, 1)
        s2 = s1 + back(s1, 2)
        s4 = s2 + back(s2, 4)
        s8 = s4 + back(s4, 8)
        win = jnp.where(lane < 64, s1, jnp.where(lane < 128, s2, jnp.where(lane < 192, s4, s8)))
        cnt = jnp.minimum(t + 1, _pool_window_lanes(x.shape)).astype(F32)
        o_ref[...] = win / cnt - x

    return pl.pallas_call(
        body, name="pool_fwd",
        grid=(1,),
        in_specs=[pl.BlockSpec((S, POOL_W), lambda i: (0, OFF_PX // POOL_W))],
        out_specs=pl.BlockSpec((S, POOL_W), lambda i: (0, 0)),
        out_shape=jax.ShapeDtypeStruct((S, POOL_W), F32),
        compiler_params=_cp(("arbitrary",), _VMEM_BIG),
    )(proj)


def _silu(g):
    return g * _sigmoid(g)


def _mix_out(fo, so, pooled, proj, wbd, scale, wout, x):
    S, D = x.shape
    tm = min(_TM_ROWS, S)

    def body(fo_ref, fg_ref, so_ref, sg_ref, pl_ref, pg_ref, wbd_ref, sc_ref, w_ref, x_ref, y_ref, mxt_ref, mx_ref):
        parts = ((0, fo_ref[...] * _silu(fg_ref[...])),
                 (FOX_W, (_dot(pl_ref[...].astype(BF16), wbd_ref[...]) * sc_ref[...]) * _silu(pg_ref[...])),
                 (FOX_W + POOL_W, so_ref[...] * _silu(sg_ref[...])))
        for off, part in parts:
            w = part.shape[1]
            mx_ref[:, off:off + w] = part.astype(BF16)
            mxt_ref[off:off + w, :] = part.T.astype(BF16)
        y_ref[...] = x_ref[...] + _dot(mx_ref[...], w_ref[...])

    return pl.pallas_call(
        body, name="mix_out",
        grid=(S // tm,),
        in_specs=[pl.BlockSpec((tm, FOX_W), lambda i: (i, 0)),
                  pl.BlockSpec((tm, FOX_W), lambda i: (i, OFF_FG // FOX_W)),
                  pl.BlockSpec((tm, SB_W), lambda i: (i, 0)),
                  pl.BlockSpec((tm, SB_W), lambda i: (i, OFF_SG // SB_W)),
                  pl.BlockSpec((tm, POOL_W), lambda i: (i, 0)),
                  pl.BlockSpec((tm, POOL_W), lambda i: (i, OFF_PG // POOL_W)),
                  pl.BlockSpec((POOL_W, POOL_W), lambda i: (0, 0)),
                  pl.BlockSpec((1, POOL_W), lambda i: (0, 0)),
                  pl.BlockSpec((D_MIX, D), lambda i: (0, 0)),
                  pl.BlockSpec((tm, D), lambda i: (i, 0))],
        out_specs=[pl.BlockSpec((tm, D), lambda i: (i, 0)), pl.BlockSpec((D_MIX, tm), lambda i: (0, i))],
        out_shape=[jax.ShapeDtypeStruct((S, D), F32), jax.ShapeDtypeStruct((D_MIX, S), BF16)],
        scratch_shapes=[pltpu.VMEM((tm, D_MIX), BF16)],
        compiler_params=_cp(("parallel",), _VMEM_MID),
    )(fo, proj, so, proj, pooled, proj, wbd, scale, wout, x)


def _loss_head(y, target):
    S, D = y.shape
    tm = min(_TM, S)

    def body(y_ref, t_ref, dy_ref, ls_ref):
        @pl.when(pl.program_id(0) == 0)
        def _():
            ls_ref[...] = jnp.zeros_like(ls_ref)

        e = y_ref[...] - t_ref[...]
        dy_ref[...] = e * (1.0 / D)
        ls_ref[...] = ls_ref[...] + jnp.sum(e * e) * (0.5 / D)

    dy, ls = pl.pallas_call(
        body, name="loss_head",
        grid=(S // tm,),
        in_specs=[pl.BlockSpec((tm, D), lambda i: (i, 0)), pl.BlockSpec((tm, D), lambda i: (i, 0))],
        out_specs=[pl.BlockSpec((tm, D), lambda i: (i, 0)), pl.BlockSpec((8, 128), lambda i: (0, 0))],
        out_shape=[jax.ShapeDtypeStruct((S, D), F32), jax.ShapeDtypeStruct((8, 128), F32)],
        compiler_params=_cp(("arbitrary",), _VMEM_MID),
    )(y, target)
    return dy, ls[0, 0]


def _dsilu(g):
    s = _sigmoid(g)
    return s * (1.0 + g * (1.0 - s))


def _gate_bwd(dy, wout, fo, so, pooled, proj, wbd, scale):
    S, D = dy.shape
    tm = min(_TM_ROWS, S)

    def body(dy_ref, w_ref, fo_ref, fg_ref, so_ref, sg_ref, pl_ref, pg_ref, wbd_ref, sc_ref,
             dfo_ref, dfg_ref, dso_ref, dsg_ref, dpg_ref, dpl_ref, dsc_ref, dwbd_ref):
        @pl.when(pl.program_id(0) == 0)
        def _():
            dsc_ref[...] = jnp.zeros_like(dsc_ref)
            dwbd_ref[...] = jnp.zeros_like(dwbd_ref)

        dm = _dot_nt(dy_ref[...].astype(BF16), w_ref[...])
        dmf = dm[:, 0:FOX_W]
        dmp = dm[:, FOX_W:FOX_W + POOL_W]
        dms = dm[:, FOX_W + POOL_W:D_MIX]
        fg = fg_ref[...]
        dfo_ref[...] = dmf * _silu(fg)
        dfg_ref[...] = (dmf * fo_ref[...] * _dsilu(fg)).astype(BF16)
        sg = sg_ref[...]
        dso_ref[...] = (dms * _silu(sg)).astype(BF16)
        dsg_ref[...] = (dms * so_ref[...] * _dsilu(sg)).astype(BF16)
        pg = pg_ref[...]
        plb = pl_ref[...].astype(BF16)
        yw = _dot(plb, wbd_ref[...])
        sc = sc_ref[...]
        dpg_ref[...] = (dmp * (yw * sc) * _dsilu(pg)).astype(BF16)
        dys = dmp * _silu(pg)
        dsc_ref[...] = dsc_ref[...] + jnp.sum(dys * yw, axis=0, keepdims=True)
        dyw = (dys * sc).astype(BF16)
        dpl_ref[...] = _dot_nt(dyw, wbd_ref[...])
        dwbd_ref[...] = dwbd_ref[...] + _dot_tn(plb, dyw)

    return pl.pallas_call(
        body, name="gate_bwd",
        grid=(S // tm,),
        in_specs=[pl.BlockSpec((tm, D), lambda i: (i, 0)),
                  pl.BlockSpec((D_MIX, D), lambda i: (0, 0)),
                  pl.BlockSpec((tm, FOX_W), lambda i: (i, 0)),
                  pl.BlockSpec((tm, FOX_W), lambda i: (i, OFF_FG // FOX_W)),
                  pl.BlockSpec((tm, SB_W), lambda i: (i, 0)),
                  pl.BlockSpec((tm, SB_W), lambda i: (i, OFF_SG // SB_W)),
                  pl.BlockSpec((tm, POOL_W), lambda i: (i, 0)),
                  pl.BlockSpec((tm, POOL_W), lambda i: (i, OFF_PG // POOL_W)),
                  pl.BlockSpec((POOL_W, POOL_W), lambda i: (0, 0)),
                  pl.BlockSpec((1, POOL_W), lambda i: (0, 0))],
        out_specs=[pl.BlockSpec((tm, FOX_W), lambda i: (i, 0)),
                   pl.BlockSpec((tm, FOX_W), lambda i: (i, 0)),
                   pl.BlockSpec((tm, SB_W), lambda i: (i, 0)),
                   pl.BlockSpec((tm, SB_W), lambda i: (i, 0)),
                   pl.BlockSpec((tm, POOL_W), lambda i: (i, 0)),
                   pl.BlockSpec((tm, POOL_W), lambda i: (i, 0)),
                   pl.BlockSpec((1, POOL_W), lambda i: (0, 0)),
                   pl.BlockSpec((POOL_W, POOL_W), lambda i: (0, 0))],
        out_shape=[jax.ShapeDtypeStruct((S, FOX_W), F32), jax.ShapeDtypeStruct((S, FOX_W), BF16),
                   jax.ShapeDtypeStruct((S, SB_W), BF16), jax.ShapeDtypeStruct((S, SB_W), BF16),
                   jax.ShapeDtypeStruct((S, POOL_W), BF16), jax.ShapeDtypeStruct((S, POOL_W), F32),
                   jax.ShapeDtypeStruct((1, POOL_W), F32), jax.ShapeDtypeStruct((POOL_W, POOL_W), F32)],
        compiler_params=_cp(("arbitrary",), _VMEM_MID),
    )(dy, wout, fo, proj, so, proj, pooled, proj, wbd, scale)


def _matmul_acc(at, b, name):
    M, S = at.shape
    N = b.shape[1]
    tk = min(_TK_DW, S)
    tn = min(512, N)
    nk = S // tk

    def body(a_ref, b_ref, o_ref, acc):
        k = pl.program_id(1)

        @pl.when(k == 0)
        def _():
            acc[...] = jnp.zeros_like(acc)

        acc[...] = acc[...] + _dot(a_ref[...], b_ref[...].astype(BF16))

        @pl.when(k == nk - 1)
        def _():
            o_ref[...] = acc[...].astype(BF16)

    return pl.pallas_call(
        body, name=name,
        grid=(N // tn, nk),
        in_specs=[pl.BlockSpec((M, tk), lambda j, k: (0, k)), pl.BlockSpec((tk, tn), lambda j, k: (k, j))],
        out_specs=pl.BlockSpec((M, tn), lambda j, k: (0, j)),
        out_shape=jax.ShapeDtypeStruct((M, N), BF16),
        scratch_shapes=[pltpu.VMEM((M, tn), F32)],
        compiler_params=_cp(("parallel", "arbitrary"), _VMEM_MID),
    )(at, b)


def _pool_bwd(dpooled):
    S = dpooled.shape[0]

    def body(d_ref, o_ref):
        d = d_ref[...]
        t = lax.broadcasted_iota(jnp.int32, d.shape, 0)
        lane = lax.broadcasted_iota(jnp.int32, d.shape, 1)
        cnt = jnp.minimum(t + 1, _pool_window_lanes(d.shape)).astype(F32)
        u = d / cnt

        def fwd(a, k):
            return jnp.where(t < S - k, pltpu.roll(a, S - k, 0), 0.0)

        s1 = u + fwd(u, 1)
        s2 = s1 + fwd(s1, 2)
        s4 = s2 + fwd(s2, 4)
        s8 = s4 + fwd(s4, 8)
        win = jnp.where(lane < 64, s1, jnp.where(lane < 128, s2, jnp.where(lane < 192, s4, s8)))
        o_ref[...] = (win - d).astype(BF16)

    return pl.pallas_call(
        body, name="pool_bwd",
        grid=(1,),
        in_specs=[pl.BlockSpec((S, POOL_W), lambda i: (0, 0))],
        out_specs=pl.BlockSpec((S, POOL_W), lambda i: (0, 0)),
        out_shape=jax.ShapeDtypeStruct((S, POOL_W), BF16),
        compiler_params=_cp(("arbitrary",), _VMEM_BIG),
    )(dpooled)


def _fox_bwd(qs, kn, proj, dfo, fo, lse, cqb, crow4, ride=None):
    S = qs.shape[0]
    T = min(_T, S)
    nq = S // T
    n_pairs = FOX_W // 128

    def body(*refs):
        if ride is None:
            q_ref, k_ref, v_ref, do_ref, o_ref, lse_ref, cq_ref, cr_ref = refs[:8]
            dq_ref, dk_ref, dv_ref, dck_ref, dcq_ref = refs[8:13]
            scr = refs[13:]
        else:
            q_ref, k_ref, v_ref, do_ref, o_ref, lse_ref, cq_ref, cr_ref, pa_ref, pb_ref = refs[:10]
            dq_ref, dk_ref, dv_ref, dck_ref, dcq_ref, ra_ref, rb_ref = refs[10:17]
            scr = refs[17:32]
            xrefs = (pa_ref, pb_ref, ra_ref, rb_ref) + tuple(refs[32:])

            @pl.when(pl.program_id(0) == 0)
            def _():
                _start_exchange("scatter", *xrefs)

        qa, qb, kta, ktb, vb, doa, dob, cka, ckb, dcka, dckb, dva, dqt, dcqa, dcqb = scr
        lane_s = _head_masks(S)
        q = q_ref[...]
        zq = jnp.zeros_like(q)
        qa[...] = jnp.where(lane_s, q, zq)
        qb[...] = jnp.where(lane_s, zq, q)
        vb[...] = v_ref[...].astype(BF16)
        do = do_ref[...].astype(BF16)
        doa[...] = jnp.where(lane_s, do, zq)
        dob[...] = jnp.where(lane_s, zq, do)
        cq = cq_ref[...]
        cka[...], ckb[...] = _spread_heads(cq)
        zs = jnp.zeros((S, 128), F32)
        dk_ref[...] = zs
        dva[...] = zs
        dcka[...] = zs
        dckb[...] = zs
        dcq_ref[...] = jnp.zeros((8, S), F32)
        row_t = lax.broadcasted_iota(jnp.int32, (128, T), 0) < HEAD_DIM

        def prep(c, carry):
            c0 = pl.multiple_of(c * T, T)
            kt = k_ref[pl.ds(c0, T), :].astype(F32).T
            kta[:, pl.ds(c0, T)] = jnp.where(row_t, kt, 0.0).astype(BF16)
            ktb[:, pl.ds(c0, T)] = jnp.where(row_t, 0.0, kt).astype(BF16)
            return carry

        lax.fori_loop(0, nq, prep, 0)
        causal = (lax.broadcasted_iota(jnp.int32, (T, T), 0) <= lax.broadcasted_iota(jnp.int32, (T, T), 1))

        heads = ((qa, kta, doa, cka, dcka, dcqa), (qb, ktb, dob, ckb, dckb, dcqb))

        def kv(js, r0, lss, dls, masked):
            cr = cr_ref[:, pl.ds(r0, T)]
            c0s = [pl.multiple_of(j * T, T) for j in js]
            ks = [k_ref[pl.ds(c0, T), :] for c0 in c0s]
            vs = [vb[pl.ds(c0, T), :] for c0 in c0s]
            qhs = [hd[0][pl.ds(r0, T), :] for hd in heads]
            dohs = [hd[2][pl.ds(r0, T), :] for hd in heads]
            ss = []
            for h, hd in enumerate(heads):
                row = []
                for k, c0 in zip(ks, c0s):
                    s = _dot_nt(k, qhs[h]) - jnp.tile(hd[3][pl.ds(c0, T), :], (1, T // 128))
                    row.append(jnp.where(causal, s, NEG) if masked else s)
                ss.append(row)
            ps = [[jnp.exp(s + (cr[h:h + 1, :] - lss[h])) for s in row] for h, row in enumerate(ss)]
            dps = [[_dot_nt(v, dohs[h]) for v in vs] for h in range(2)]
            dss = [[p * (dp - dls[h]) for p, dp in zip(ps[h], dps[h])] for h in range(2)]
            pbs = [[p.astype(BF16) for p in row] for row in ps]
            dsbs = [[ds.astype(BF16) for ds in row] for row in dss]
            for t, c0 in enumerate(c0s):
                dva[pl.ds(c0, T), :] = dva[pl.ds(c0, T), :] + (_dot(pbs[0][t], dohs[0]) + _dot(pbs[1][t], dohs[1]))
                dk_ref[pl.ds(c0, T), :] = dk_ref[pl.ds(c0, T), :] + (_dot(dsbs[0][t], qhs[0]) + _dot(dsbs[1][t], qhs[1]))
            dq = None
            for h, hd in enumerate(heads):
                for t, c0 in enumerate(c0s):
                    term = _dot(hd[1][:, pl.ds(c0, T)], dsbs[h][t])
                    dq = term if dq is None else dq + term
            dqt[...] = dqt[...] + dq
            for h, hd in enumerate(heads):
                col = jnp.sum(dss[h][0], axis=0, keepdims=True)
                for ds in dss[h][1:]:
                    col = col + jnp.sum(ds, axis=0, keepdims=True)
                hd[5][0:1, :] = hd[5][0:1, :] + col
                for ds, c0 in zip(dss[h], c0s):
                    fold = ds[:, 0:128]
                    for u in range(1, T // 128):
                        fold = fold + ds[:, 128 * u:128 * (u + 1)]
                    hd[4][pl.ds(c0, T), :] = hd[4][pl.ds(c0, T), :] - fold

        def qblk(i, carry):
            r0 = pl.multiple_of(i * T, T)
            dt = (do_ref[pl.ds(r0, T), :] * o_ref[pl.ds(r0, T), :]).T
            dla = jnp.sum(jnp.where(row_t, dt, 0.0), axis=0, keepdims=True)
            dlb = jnp.sum(jnp.where(row_t, 0.0, dt), axis=0, keepdims=True)
            ls = lse_ref[:, pl.ds(r0, T)]
            lss = (ls[0:1, :], ls[1:2, :])
            back = jnp.max(ls[2:3, :]).astype(jnp.int32)
            dqt[...] = jnp.zeros((128, T), F32)
            dcqa[...] = jnp.zeros((8, T), F32)
            dcqb[...] = jnp.zeros((8, T), F32)
            kv([i], r0, lss, (dla, dlb), True)
            _for_tiles_back(i, back, lambda js: kv(js, r0, lss, (dla, dlb), False), fours=True)
            dq_ref[pl.ds(r0, T), :] = dqt[...].T
            dcq_ref[0:1, pl.ds(r0, T)] = dcqa[0:1, :]
            dcq_ref[1:2, pl.ds(r0, T)] = dcqb[0:1, :]
            return carry

        lax.fori_loop(0, nq, qblk, 0)
        dv_ref[...] = dva[...].astype(BF16)
        dck_ref[...] = jnp.where(lane_s, jnp.sum(dcka[...], axis=1, keepdims=True),
                                 jnp.sum(dckb[...], axis=1, keepdims=True))
        if ride is not None:
            @pl.when(pl.program_id(0) == n_pairs - 1)
            def _():
                _wait_exchange("scatter", *xrefs)

    extra = () if ride is None else tuple(ride)
    return pl.pallas_call(
        body, name="fox_bwd" if ride is None else "fox_bwd_exchange",
        grid=(n_pairs,),
        in_specs=[_pair_blk(S), _pair_blk(S), _pair_blk(S, OFF_FV // 128), _pair_blk(S), _pair_blk(S),
                  _pair_rows(S), _pair_blk(S), _pair_rows(S)] + [_ANY] * len(extra),
        out_specs=[_pair_blk(S), _pair_blk(S), _pair_blk(S), _pair_blk(S), _pair_rows(S)] + [_ANY] * len(extra),
        out_shape=[jax.ShapeDtypeStruct((S, FOX_W), F32), jax.ShapeDtypeStruct((S, FOX_W), F32),
                   jax.ShapeDtypeStruct((S, FOX_W), BF16), jax.ShapeDtypeStruct((S, FOX_W), F32),
                   jax.ShapeDtypeStruct((n_pairs, 8, S), F32)]
        + (_exchange_out_shapes("scatter", *extra) if extra else []),
        scratch_shapes=[pltpu.VMEM((S, 128), BF16)] * 2 + [pltpu.VMEM((128, S), BF16)] * 2
        + [pltpu.VMEM((S, 128), BF16)] * 3 + [pltpu.VMEM((S, 128), F32)] * 5
        + [pltpu.VMEM((128, T), F32)] + [pltpu.VMEM((8, T), F32)] * 2
        + (_EXCHANGE_SEMS if extra else []),
        compiler_params=_cp(("arbitrary",), _VMEM_BIG),
    )(qs, kn, proj, dfo, fo, lse, cqb, crow4, *extra)


def _sb_bwd(psb, dso, ltot, tril):
    S = psb.shape[0]
    T = tril.shape[0]
    nq = S // T
    n_pairs = SB_W // 128
    H = 2 * n_pairs

    def body(q_ref, k_ref, v_ref, do_ref, lt_ref, tri_ref, dq_ref, dk_ref, dv_ref,
             qm, kt, dka, dva, dqt, rr, gg):
        lane_s = _head_masks(S)
        for p in range(n_pairs):
            q = (q_ref[:, 128 * p:128 * (p + 1)].astype(F32) * Q_SCALE).astype(BF16)
            zq = jnp.zeros_like(q)
            qm[2 * p] = jnp.where(lane_s, q, zq)
            qm[2 * p + 1] = jnp.where(lane_s, zq, q)
        dka[...] = jnp.zeros((n_pairs, S, 128), F32)
        dva[...] = jnp.zeros((n_pairs, S, 128), F32)
        row_t = lax.broadcasted_iota(jnp.int32, (128, T), 0) < HEAD_DIM
        lane_t = lax.broadcasted_iota(jnp.int32, (T, 128), 1) < HEAD_DIM

        def prep(c, carry):
            c0 = pl.multiple_of(c * T, T)
            for p in range(n_pairs):
                kt[p, :, pl.ds(c0, T)] = k_ref[pl.ds(c0, T), 128 * p:128 * (p + 1)].astype(F32).T.astype(BF16)
            return carry

        lax.fori_loop(0, nq, prep, 0)
        strict = (lax.broadcasted_iota(jnp.int32, (T, T), 0) < lax.broadcasted_iota(jnp.int32, (T, T), 1))

        def own(x, h, mask):
            z = jnp.zeros_like(x)
            return jnp.where(mask, x, z) if h % 2 == 0 else jnp.where(mask, z, x)

        def pair(ref, p, c0):
            return ref[pl.ds(c0, T), 128 * p:128 * (p + 1)]

        def kv(tiles, r0, lts):
            tri = tri_ref[...]
            c0s = [pl.multiple_of(j * T, T) for j, _ in tiles]
            qhs = [qm[h, pl.ds(r0, T), :] for h in range(H)]
            dohs = [own(pair(do_ref, h // 2, r0), h, lane_t) for h in range(H)]
            zs = [[_dot_nt(pair(k_ref, h // 2, c0), qhs[h]) for c0 in c0s] for h in range(H)]
            das = [[_dot_nt(pair(v_ref, h // 2, c0), dohs[h]) for c0 in c0s] for h in range(H)]
            es, lbs = [], []
            for row in zs:
                erow, lrow = [], []
                for z, (_, masked) in zip(row, tiles):
                    e, sp = _softplus_parts(z)
                    erow.append(e)
                    lrow.append(jnp.where(strict, -sp, 0.0) if masked else -sp)
                es.append(erow)
                lbs.append(lrow)
            pres = [[_mm2(lb, tri, left=True) for lb in row] for row in lbs]
            aas, r_ends = [], []
            for h in range(H):
                r = rr[h, 0:1, :]
                arow = []
                for z, lb, pre, (_, masked) in zip(zs[h], lbs[h], pres[h], tiles):
                    a = jnp.exp(z + lb + ((lts[h] - r) - pre))
                    arow.append(jnp.where(strict, a, 0.0) if masked else a)
                    r = r + pre[T - 1:T, :]
                aas.append(arow)
                r_ends.append(r)
            gs = [[a * da for a, da in zip(arow, drow)] for arow, drow in zip(aas, das)]
            gpres = [[_mm2(g, tri, left=True) for g in row] for row in gs]
            dzbs, g_ends = [], []
            for h in range(H):
                gc = gg[h, 0:1, :]
                drow = []
                for z, e, g, gpre, (_, masked) in zip(zs[h], es[h], gs[h], gpres[h], tiles):
                    inv = 1.0 / (1.0 + e)
                    pos = z >= 0.0
                    sig = jnp.where(pos, 1.0, e) * inv
                    oms = jnp.where(pos, e, 1.0) * inv
                    dz = g * oms - sig * (gc + (gpre - g))
                    if masked:
                        dz = jnp.where(strict, dz, 0.0)
                    drow.append(dz.astype(BF16))
                    gc = gc + gpre[T - 1:T, :]
                dzbs.append(drow)
                g_ends.append(gc)
            for p in range(n_pairs):
                a, b = 2 * p, 2 * p + 1
                dq = None
                for h in (a, b):
                    for t, c0 in enumerate(c0s):
                        term = _dot(own(kt[p, :, pl.ds(c0, T)], h, row_t), dzbs[h][t])
                        dq = term if dq is None else dq + term
                dqt[p] = dqt[p] + dq
                for t, c0 in enumerate(c0s):
                    dka[p, pl.ds(c0, T), :] = dka[p, pl.ds(c0, T), :] + (_dot(dzbs[a][t], qhs[a]) + _dot(dzbs[b][t], qhs[b]))
                    dva[p, pl.ds(c0, T), :] = dva[p, pl.ds(c0, T), :] + (_dot(aas[a][t].astype(BF16), dohs[a])
                                                                      + _dot(aas[b][t].astype(BF16), dohs[b]))
            for h in range(H):
                rr[h, 0:1, :] = r_ends[h]
                gg[h, 0:1, :] = g_ends[h]

        def qblk(i, carry):
            r0 = pl.multiple_of(i * T, T)
            lts = []
            for p in range(n_pairs):
                lt = lt_ref[p, :, pl.ds(r0, T)]
                lts += [lt[0:1, :], lt[1:2, :]]
            back = jnp.max(lt_ref[0, 2:3, pl.ds(r0, T)]).astype(jnp.int32)
            dqt[...] = jnp.zeros((n_pairs, 128, T), F32)
            rr[...] = jnp.zeros((H, 8, T), F32)
            gg[...] = jnp.zeros((H, 8, T), F32)

            def inner(j, c):
                kv([(j, False)], r0, lts)
                return c

            @pl.when(back == 0)
            def _():
                kv([(i, True)], r0, lts)

            @pl.when(back > 0)
            def _():
                lax.fori_loop(i - back, i - 1, inner, 0)
                kv([(i - 1, False), (i, True)], r0, lts)

            for p in range(n_pairs):
                dq_ref[pl.ds(r0, T), 128 * p:128 * (p + 1)] = (dqt[p] * Q_SCALE).T.astype(BF16)
            return carry

        lax.fori_loop(0, nq, qblk, 0)
        for p in range(n_pairs):
            dk_ref[:, 128 * p:128 * (p + 1)] = dka[p].astype(BF16)
            dv_ref[:, 128 * p:128 * (p + 1)] = dva[p].astype(BF16)

    wide = lambda off: pl.BlockSpec((S, SB_W), lambda g: (0, off), pipeline_mode=pl.Buffered(1))
    return pl.pallas_call(
        body, name="sb_bwd",
        grid=(1,),
        in_specs=[wide(0), wide(1), wide(2), wide(0),
                  pl.BlockSpec((n_pairs, 8, S), lambda g: (0, 0, 0), pipeline_mode=pl.Buffered(1)),
                  pl.BlockSpec((T, T), lambda g: (0, 0))],
        out_specs=[wide(0), wide(0), wide(0)],
        out_shape=[jax.ShapeDtypeStruct((S, SB_W), BF16)] * 3,
        scratch_shapes=[pltpu.VMEM((H, S, 128), BF16), pltpu.VMEM((n_pairs, 128, S), BF16),
                        pltpu.VMEM((n_pairs, S, 128), F32), pltpu.VMEM((n_pairs, S, 128), F32),
                        pltpu.VMEM((n_pairs, 128, T), F32), pltpu.VMEM((H, 8, T), F32), pltpu.VMEM((H, 8, T), F32)],
        compiler_params=_cp(("arbitrary",), _VMEM_BIG),
    )(psb, psb, psb, dso, ltot, tril)


def _head_norm_bwd(x, g, dy, bd):
    ss = _mm2(x * x, bd)
    r = lax.rsqrt(ss * (1.0 / HEAD_DIM) + EPS)
    xr = x * r
    gdy = g * dy
    m = _mm2(xr * gdy, bd) * (1.0 / HEAD_DIM)
    return r * (gdy - xr * m), dy * xr


def _qk_bwd(dqs, dkn, proj, pff, bfp, gq, gk, bd, dck, dcq, triu):
    S = proj.shape[0]
    T = triu.shape[0]
    n = S // T
    rev = lambda col: (lambda i: (n - 1 - i, col))

    def body(dq_ref, dk_ref, q_ref, k_ref, ff_ref, b_ref, gq_ref, gk_ref, bd_ref, dck_ref, dcq_ref, tri_ref,
             dfq_ref, dfk_ref, dff_ref, dgq_ref, dgk_ref, dbf_ref, carry):
        @pl.when(pl.program_id(0) == 0)
        def _():
            carry[...] = jnp.zeros_like(carry)
            dgq_ref[...] = jnp.zeros_like(dgq_ref)
            dgk_ref[...] = jnp.zeros_like(dgk_ref)
            dbf_ref[...] = jnp.zeros_like(dbf_ref)

        bdv = bd_ref[...]
        dxq, gq_rows = _head_norm_bwd(q_ref[...], gq_ref[...], dq_ref[...] * Q_SCALE, bdv)
        dfq_ref[...] = dxq.astype(BF16)
        dgq_ref[...] = dgq_ref[...] + jnp.sum(gq_rows, axis=0, keepdims=True)
        dxk, gk_rows = _head_norm_bwd(k_ref[...], gk_ref[...], dk_ref[...], bdv)
        dfk_ref[...] = dxk.astype(BF16)
        dgk_ref[...] = dgk_ref[...] + jnp.sum(gk_rows, axis=0, keepdims=True)
        first = (lax.broadcasted_iota(jnp.int32, (FOX_W, N_FFPAD), 0)
                 == HEAD_DIM * lax.broadcasted_iota(jnp.int32, (FOX_W, N_FFPAD), 1)).astype(BF16)
        dc = _mm3(dck_ref[...], first) + dcq_ref[...]
        dlf = _mm3(dc, tri_ref[...], left=True) + carry[0:1, :]
        carry[0:1, :] = dlf[0:1, :]
        u = ff_ref[...] + b_ref[...]
        lane = lax.broadcasted_iota(jnp.int32, u.shape, 1)
        dff = jnp.where(lane < N_FF, dlf * _sigmoid(-u), 0.0)
        dff_ref[...] = dff.astype(BF16)
        dbf_ref[...] = dbf_ref[...] + jnp.sum(dff, axis=0, keepdims=True)

    return pl.pallas_call(
        body, name="qk_bwd",
        grid=(n,),
        in_specs=[pl.BlockSpec((T, FOX_W), rev(0)), pl.BlockSpec((T, FOX_W), rev(0)),
                  pl.BlockSpec((T, FOX_W), rev(OFF_FQ // FOX_W)), pl.BlockSpec((T, FOX_W), rev(OFF_FK // FOX_W)),
                  pl.BlockSpec((T, N_FFPAD), rev(0)),
                  pl.BlockSpec((1, N_FFPAD), lambda i: (0, 0)),
                  pl.BlockSpec((1, FOX_W), lambda i: (0, 0)), pl.BlockSpec((1, FOX_W), lambda i: (0, 0)),
                  pl.BlockSpec((FOX_W, FOX_W), lambda i: (0, 0)),
                  pl.BlockSpec((T, FOX_W), rev(0)), pl.BlockSpec((T, N_FFPAD), rev(0)),
                  pl.BlockSpec((T, T), lambda i: (0, 0))],
        out_specs=[pl.BlockSpec((T, FOX_W), rev(0)), pl.BlockSpec((T, FOX_W), rev(0)),
                   pl.BlockSpec((T, N_FFPAD), rev(0)),
                   pl.BlockSpec((1, FOX_W), lambda i: (0, 0)), pl.BlockSpec((1, FOX_W), lambda i: (0, 0)),
                   pl.BlockSpec((1, N_FFPAD), lambda i: (0, 0))],
        out_shape=[jax.ShapeDtypeStruct((S, FOX_W), BF16), jax.ShapeDtypeStruct((S, FOX_W), BF16),
                   jax.ShapeDtypeStruct((S, N_FFPAD), BF16),
                   jax.ShapeDtypeStruct((1, FOX_W), F32), jax.ShapeDtypeStruct((1, FOX_W), F32),
                   jax.ShapeDtypeStruct((1, N_FFPAD), F32)],
        scratch_shapes=[pltpu.VMEM((8, N_FFPAD), F32)],
        compiler_params=_cp(("arbitrary",), _VMEM_MID),
    )(dqs, dkn, proj, proj, pff, bfp, gq, gk, bd, dck, dcq, triu)


def _dproj_layout(pieces):
    offs, o = [], 0
    for p in pieces:
        offs.append(o)
        o += p.shape[1]
    assert o == N_MAIN
    return offs


def _inproj_bwd_dx(pieces, dff, wm, wff, x, g, dy, ride=None):
    S, D = x.shape
    tm = min(_TM_DX, S)
    steps = S // tm
    offs = _dproj_layout(pieces)
    n = len(pieces)

    def body(*refs):
        p_refs = refs[:n]
        if ride is None:
            dff_ref, w_ref, wff_ref, x_ref, g_ref, dy_ref, dx_ref, dg_ref = refs[n:]
        else:
            dff_ref, w_ref, wff_ref, x_ref, g_ref, dy_ref, pa_ref, pb_ref = refs[n:n + 8]
            dx_ref, dg_ref, ra_ref, rb_ref = refs[n + 8:n + 12]
            xrefs = (pa_ref, pb_ref, ra_ref, rb_ref) + tuple(refs[n + 12:])

        @pl.when(pl.program_id(0) == 0)
        def _():
            dg_ref[...] = jnp.zeros_like(dg_ref)
            if ride is not None:
                _start_exchange("scatter", *xrefs)

        dh = _dot_nt(dff_ref[...], wff_ref[...])
        for p_ref, off in zip(p_refs, offs):
            dh = dh + _dot_nt(p_ref[...], w_ref[:, off:off + p_ref.shape[1]])
        xv = x_ref[...]
        r = _rms_rows(xv)
        xr = xv * r
        dg_ref[...] = dg_ref[...] + jnp.sum(dh * xr, axis=0, keepdims=True)
        gdh = g_ref[...] * dh
        m = jnp.mean(gdh * xr, axis=-1, keepdims=True)
        dx_ref[...] = dy_ref[...] + r * (gdh - xr * m)
        if ride is not None:
            @pl.when(pl.program_id(0) == steps - 1)
            def _():
                _wait_exchange("scatter", *xrefs)

    extra = () if ride is None else tuple(ride)
    return pl.pallas_call(
        body, name="inproj_bwd_dx" if ride is None else "inproj_bwd_dx_exchange",
        grid=(steps,),
        in_specs=[pl.BlockSpec((tm, p.shape[1]), lambda i: (i, 0)) for p in pieces]
        + [pl.BlockSpec((tm, N_FFPAD), lambda i: (i, 0)),
                  pl.BlockSpec((D, N_MAIN), lambda i: (0, 0)),
                  pl.BlockSpec((D, N_FFPAD), lambda i: (0, 0)),
                  pl.BlockSpec((tm, D), lambda i: (i, 0)),
                  pl.BlockSpec((1, D), lambda i: (0, 0)),
                  pl.BlockSpec((tm, D), lambda i: (i, 0))] + [_ANY] * len(extra),
        out_specs=[pl.BlockSpec((tm, D), lambda i: (i, 0)), pl.BlockSpec((1, D), lambda i: (0, 0))] + [_ANY] * len(extra),
        out_shape=[jax.ShapeDtypeStruct((S, D), F32), jax.ShapeDtypeStruct((1, D), F32)]
        + (_exchange_out_shapes("scatter", *extra) if extra else []),
        scratch_shapes=_EXCHANGE_SEMS if extra else [],
        compiler_params=_cp(("arbitrary",), _VMEM_WIDE),
    )(*pieces, dff, wm, wff, x, g, dy, *extra)


def _inproj_bwd_dw(ht, pieces, dff):
    D, S = ht.shape
    tk = min(_TK_DW, S)
    nk = S // tk
    offs = _dproj_layout(pieces)
    n = len(pieces)

    def body(*refs):
        ht_ref, p_refs, dff_ref = refs[0], refs[1:1 + n], refs[1 + n]
        dw_ref, dwff_ref, acc, accff = refs[2 + n:]
        k = pl.program_id(0)

        @pl.when(k == 0)
        def _():
            acc[...] = jnp.zeros_like(acc)
            accff[...] = jnp.zeros_like(accff)

        hb = ht_ref[...]
        for p_ref, off in zip(p_refs, offs):
            w = p_ref.shape[1]
            acc[:, off:off + w] = acc[:, off:off + w] + _dot(hb, p_ref[...])
        accff[...] = accff[...] + _dot(hb, dff_ref[...])

        @pl.when(k == nk - 1)
        def _():
            dw_ref[...] = acc[...].astype(BF16)
            dwff_ref[...] = accff[...].astype(BF16)

    return pl.pallas_call(
        body, name="inproj_bwd_dw",
        grid=(nk,),
        in_specs=[pl.BlockSpec((D, tk), lambda k: (0, k))]
        + [pl.BlockSpec((tk, p.shape[1]), lambda k: (k, 0)) for p in pieces]
        + [pl.BlockSpec((tk, N_FFPAD), lambda k: (k, 0))],
        out_specs=[pl.BlockSpec((D, N_MAIN), lambda k: (0, 0), pipeline_mode=pl.Buffered(1)),
                   pl.BlockSpec((D, N_FFPAD), lambda k: (0, 0), pipeline_mode=pl.Buffered(1))],
        out_shape=[jax.ShapeDtypeStruct((D, N_MAIN), BF16), jax.ShapeDtypeStruct((D, N_FFPAD), BF16)],
        scratch_shapes=[pltpu.VMEM((D, N_MAIN), F32), pltpu.VMEM((D, N_FFPAD), F32)],
        compiler_params=_cp(("arbitrary",), _VMEM_BIG),
    )(ht, *pieces, dff)


def _constants(T, rows):
    tril = jnp.tril(jnp.ones((T, T), F32)).astype(BF16)
    tril_rows = jnp.tril(jnp.ones((rows, rows), F32)).astype(BF16)
    hid = jnp.arange(FOX_W) // HEAD_DIM
    bd = (hid[:, None] == hid[None, :]).astype(BF16)
    ex = (jnp.arange(N_FFPAD)[:, None] == hid[None, :]).astype(BF16)
    return tril, tril.T, bd, ex, tril_rows, tril_rows.T


def _crow4(ccol, T):
    S = ccol.shape[0]
    c = ccol[:, :FOX_HEADS].T
    last = jnp.pad(c[:, T - 1::T], ((0, 0), (0, S - S // T)))
    rows = jnp.concatenate([c.reshape(FOX_HEADS // 2, 2, S), last.reshape(FOX_HEADS // 2, 2, S)], axis=1)
    return jnp.pad(rows, ((0, 0), (0, 4), (0, 0)))


def _layer_fwd(x, lw, consts, ride=None):
    tril, triu, bd, ex, tril_rows, _ = consts
    proj, pff, ht, psb = _inproj_fwd(x, lw["g"], lw["wm"], lw["wff"])
    qs, kn, ccol, cqb = _fox_prep(proj, pff, lw["bfp"], lw["gq"], lw["gk"], bd, ex, tril_rows)
    crow4 = _crow4(ccol, tril.shape[0])
    fo, lse, *gathered = _fox_fwd(qs, kn, proj, cqb, crow4, ride)
    so, ltot = _sb_fwd(psb, triu)
    pooled = _pool_fwd(proj)
    y, mixedt = _mix_out(fo, so, pooled, proj, lw["wbd"], lw["scale"], lw["wout"], x)
    return y, (x, proj, pff, ht, psb, qs, kn, cqb, crow4, fo, lse, so, ltot, pooled, mixedt), gathered


def _layer_bwd(dy, saved, lw, consts, ride=None, exchange_own=False):
    tril, _, bd, _, _, triu_rows = consts
    x, proj, pff, ht, psb, qs, kn, cqb, crow4, fo, lse, so, ltot, pooled, mixedt = saved
    S = x.shape[0]
    dfo, dfg, dso, dsg, dpg, dpooled, dscale, dwbd = _gate_bwd(dy, lw["wout"], fo, so, pooled, proj, lw["wbd"], lw["scale"])
    dwout = _matmul_acc(mixedt, dy, "dw_out")
    dpx = _pool_bwd(dpooled)
    dqs, dkn, dfv, dck, dcq4, *received = _fox_bwd(qs, kn, proj, dfo, fo, lse, cqb, crow4, ride)
    dsq, dsk, dsv = _sb_bwd(psb, dso, ltot, tril)
    dcq = jnp.pad(dcq4[:, :2, :].reshape(FOX_HEADS, S).T, ((0, 0), (0, N_FFPAD - FOX_HEADS)))
    dfq, dfk, dff, dgq, dgk, dbf = _qk_bwd(dqs, dkn, proj, pff, lw["bfp"], lw["gq"], lw["gk"], bd, dck, dcq, triu_rows)
    pieces = [dfq, dfk, dfv, dfg, dpx, dpg, dsq, dsk, dsv, dsg]
    dwm, dwff = _inproj_bwd_dw(ht, pieces, dff)
    dwm_t = dwm.T
    dwin_t = jnp.concatenate([dwm_t[:OFF_PX], dwff.T[:N_FF], dwm_t[OFF_PX:]], axis=0)
    own = _grad_parts({"w_in_t": dwin_t, "w_out": dwout}) if exchange_own else None
    dx, dng, *received_own = _inproj_bwd_dx(pieces, dff, lw["wm"], lw["wff"], x, lw["g"], dy, own)
    grads = {
        "norm_g": dng[0],
        "w_in_t": dwin_t,
        "b_f": dbf[0, :N_FF],
        "q_norm_g": dgq[0].reshape(FOX_HEADS, HEAD_DIM).sum(0),
        "k_norm_g": dgk[0].reshape(FOX_HEADS, HEAD_DIM).sum(0),
        "w_pool": jnp.stack([dwbd[64 * i:64 * i + 64, 64 * i:64 * i + 64] for i in range(4)]),
        "pool_scale": dscale[0],
        "w_out": dwout,
    }
    return dx, grads, received, received_own


def _layer_weights(l, norm_g, gin, b_f, q_norm_g, k_norm_g, w_pool, pool_scale, gout):
    D = gin.shape[1]
    w = gin.transpose(1, 0, 2).reshape(D, D_IN)
    wm = jnp.concatenate([w[:, :2048], w[:, 2048 + N_FF:]], axis=1)
    wff = jnp.pad(w[:, 2048:2048 + N_FF], ((0, 0), (0, N_FFPAD - N_FF)))
    grp = jnp.arange(POOL_W) // 64
    wbd = jnp.where(grp[:, None] == grp[None, :], jnp.tile(w_pool[l].transpose(1, 0, 2).reshape(64, POOL_W), (4, 1)), 0.0)
    return {
        "g": norm_g[l].reshape(1, D),
        "wm": wm, "wff": wff,
        "bfp": jnp.pad(b_f[l], (0, N_FFPAD - N_FF)).reshape(1, N_FFPAD),
        "gq": jnp.tile(q_norm_g[l], FOX_HEADS).reshape(1, FOX_W),
        "gk": jnp.tile(k_norm_g[l], FOX_HEADS).reshape(1, FOX_W),
        "wbd": wbd.astype(BF16),
        "scale": pool_scale[l].reshape(1, POOL_W),
        "wout": gout.reshape(D_MIX, D),
    }


def _grad_parts(g):
    dwin_t, dwout = g["w_in_t"].astype(BF16), g["w_out"].astype(BF16)
    return (dwin_t.reshape(N_DEV, D_IN // N_DEV, dwin_t.shape[1]),
            dwout.reshape(N_DEV, D_MIX // N_DEV, dwout.shape[1]))


def _train_step(x, target, norm_g, win_sh, b_f, q_norm_g, k_norm_g, w_pool, pool_scale, wout_sh):
    L = norm_g.shape[0]
    consts = _constants(min(_T, x.shape[0]), min(_TM_ROWS, x.shape[0]))
    gathered = _gather_two_level(win_sh[0], wout_sh[0], "gather_weights")
    lws, saved = [], []
    h = x
    for l in range(L):
        lws.append(_layer_weights(l, norm_g, gathered[0], b_f, q_norm_g, k_norm_g, w_pool, pool_scale, gathered[1]))
        ride = (win_sh[l + 1], wout_sh[l + 1]) if l + 1 < L else None
        h, sv, gathered = _layer_fwd(h, lws[l], consts, ride)
        saved.append(sv)
    dy, loss = _loss_head(h, target)
    grads, received = [None] * L, [None] * L
    ride = None
    for l in reversed(range(L)):
        dy, grads[l], got, got_own = _layer_bwd(dy, saved[l], lws[l], consts, ride, exchange_own=(l == 0))
        if ride is not None:
            received[l + 1] = got
        if l == 0:
            received[0] = got_own
        else:
            ride = _grad_parts(grads[l])
    return loss, dy, grads, received


def _mesh_pos():
    return lax.axis_index("x"), lax.axis_index("y"), lax.axis_index("c")


_FLIPS = [(0, 0, 1), (1, 0, 0), (0, 1, 0), (1, 1, 0), (1, 0, 1), (0, 1, 1), (1, 1, 1)]


def _peers():
    x, y, c = _mesh_pos()
    out = []
    for fx, fy, fc in _FLIPS:
        px = 1 - x if fx else x
        py = 1 - y if fy else y
        pc = 1 - c if fc else c
        out.append(((px, py, pc), 4 * px + 2 * py + pc))
    return out, 4 * x + 2 * y + c


_EXCHANGE_SEMS = [pltpu.SemaphoreType.DMA((14,)), pltpu.SemaphoreType.DMA((14,)), pltpu.SemaphoreType.DMA((2,))]
_ANY = pl.BlockSpec(memory_space=pl.ANY)


def _exchange_copies(kind, a_ref, b_ref, oa_ref, ob_ref, send_sems, recv_sems, loc_sems):
    peers, me = _peers()
    pairs = ((a_ref, oa_ref), (b_ref, ob_ref))
    local = [pltpu.make_async_copy(src if kind == "gather" else src.at[me], dst.at[me], loc_sems.at[t])
             for t, (src, dst) in enumerate(pairs)]
    remote = []
    for k, (dev, idx) in enumerate(peers):
        for t, (src, dst) in enumerate(pairs):
            remote.append(pltpu.make_async_remote_copy(
                src_ref=src if kind == "gather" else src.at[idx], dst_ref=dst.at[me],
                send_sem=send_sems.at[2 * k + t], recv_sem=recv_sems.at[2 * k + t],
                device_id=dev, device_id_type=pl.DeviceIdType.MESH))
    return local, remote


def _start_exchange(kind, *refs):
    local, remote = _exchange_copies(kind, *refs)
    for cp in local + remote:
        cp.start()


def _wait_exchange(kind, *refs):
    local, remote = _exchange_copies(kind, *refs)
    for cp in remote:
        cp.wait_recv()
    for cp in remote:
        cp.wait_send()
    for cp in local:
        cp.wait()


def _exchange_out_shapes(kind, a, b):
    if kind == "gather":
        return [jax.ShapeDtypeStruct((N_DEV,) + a.shape, a.dtype), jax.ShapeDtypeStruct((N_DEV,) + b.shape, b.dtype)]
    return [jax.ShapeDtypeStruct(a.shape, a.dtype), jax.ShapeDtypeStruct(b.shape, b.dtype)]


def _gather_two_level(a, b, name):
    def body(a_ref, b_ref, ga_ref, gb_ref, send_sems, recv_sems, loc_sems):
        x, y, c = _mesh_pos()
        slot_of = lambda px, py, pc: 4 * px + 2 * py + pc
        me, sib = slot_of(x, y, c), slot_of(x, y, 1 - c)
        chips = [(1 - x, y), (x, 1 - y), (1 - x, 1 - y)]
        pairs = ((a_ref, ga_ref), (b_ref, gb_ref))

        def copy(k, t, slot, to, src=None):
            dst = pairs[t][1].at[slot]
            return pltpu.make_async_remote_copy(
                src_ref=dst if src is None else src, dst_ref=dst, send_sem=send_sems.at[2 * k + t],
                recv_sem=recv_sems.at[2 * k + t], device_id=to, device_id_type=pl.DeviceIdType.MESH)

        local = [pltpu.make_async_copy(src, dst.at[me], loc_sems.at[t]) for t, (src, dst) in enumerate(pairs)]
        first = []
        for t, (src, _) in enumerate(pairs):
            first.append(copy(0, t, me, (x, y, 1 - c), src))
            first += [copy(1 + j, t, me, (*chip, c), src) for j, chip in enumerate(chips)]
        for cp in local + first:
            cp.start()
        passed = []
        for j, chip in enumerate(chips):
            for t in range(2):
                landed = slot_of(*chip, c)
                copy(1 + j, t, landed, (x, y, c)).wait_recv()
                cp = copy(4 + j, t, landed, (x, y, 1 - c))
                cp.start()
                passed.append(cp)
        for t in range(2):
            copy(0, t, sib, (x, y, c)).wait_recv()
            for j, chip in enumerate(chips):
                copy(4 + j, t, slot_of(*chip, 1 - c), (x, y, c)).wait_recv()
        for cp in first + passed:
            cp.wait_send()
        for cp in local:
            cp.wait()

    return pl.pallas_call(
        body, name=name,
        in_specs=[_ANY, _ANY], out_specs=[_ANY, _ANY],
        out_shape=_exchange_out_shapes("gather", a, b),
        scratch_shapes=_EXCHANGE_SEMS,
    )(a, b)


def _adam_math(w, g, m, v):
    m_new = ADAM_B1 * m + (1.0 - ADAM_B1) * g
    v_new = ADAM_B2 * v + (1.0 - ADAM_B2) * (g * g)
    m_hat = m_new / (1.0 - ADAM_B1 ** ADAM_STEP)
    v_hat = v_new / (1.0 - ADAM_B2 ** ADAM_STEP)
    delta = -ADAM_LR * (m_hat / (jnp.sqrt(v_hat) + ADAM_EPS) + ADAM_WD * w)
    return delta, m_new, v_new


def _sum_adamw(gparts, w, m, v, name):
    L, R, C = w.shape
    tr = min(128, R)

    def body(*refs):
        gp_refs = refs[:L]
        w_ref, m_ref, v_ref, g_ref, d_ref, nm_ref, nv_ref = refs[L:]
        for l in range(L):
            g = gp_refs[l][0].astype(F32)
            for s in range(1, N_DEV):
                g = g + gp_refs[l][s].astype(F32)
            d, mn, vn = _adam_math(w_ref[l], g, m_ref[l], v_ref[l])
            g_ref[l] = g
            d_ref[l] = d
            nm_ref[l] = mn
            nv_ref[l] = vn

    blk = pl.BlockSpec((L, tr, C), lambda r: (0, r, 0))
    return pl.pallas_call(
        body, name=name,
        grid=(R // tr,),
        in_specs=[pl.BlockSpec((N_DEV, tr, C), lambda r: (0, r, 0))] * L + [blk, blk, blk],
        out_specs=[blk, blk, blk, blk],
        out_shape=[jax.ShapeDtypeStruct((L, R, C), F32)] * 4,
        compiler_params=_cp(("parallel",), _VMEM_WIDE),
    )(*gparts, w, m, v)


def _sum_adamw_cols(gparts, w_t, m_t, v_t, name):
    C, L, D = w_t.shape
    td = min(128, D)

    def body(*refs):
        gp_refs = refs[:L]
        w_ref, m_ref, v_ref, g_ref, d_ref, nm_ref, nv_ref = refs[L:]
        starts = list(range(0, C - _ADAM_ROWS + 1, _ADAM_ROWS))
        for c0 in starts:
            rows = slice(c0, C if c0 == starts[-1] else c0 + _ADAM_ROWS)
            for l in range(L):
                g = gp_refs[l][0, rows, :].astype(F32)
                for s in range(1, N_DEV):
                    g = g + gp_refs[l][s, rows, :].astype(F32)
                g_ref[rows, l, :] = g
            d, mn, vn = _adam_math(w_ref[rows], g_ref[rows], m_ref[rows], v_ref[rows])
            d_ref[rows] = d
            nm_ref[rows] = mn
            nv_ref[rows] = vn

    blk = pl.BlockSpec((C, L, td), lambda j: (0, 0, j))
    return pl.pallas_call(
        body, name=name,
        grid=(D // td,),
        in_specs=[pl.BlockSpec((N_DEV, C, td), lambda j: (0, 0, j))] * L + [blk, blk, blk],
        out_specs=[blk, blk, blk, blk],
        out_shape=[jax.ShapeDtypeStruct((C, L, D), F32)] * 4,
        compiler_params=_cp(("parallel",), _VMEM_WIDE),
    )(*gparts, w_t, m_t, v_t)


def _small_update(gpack, wpack, mpack, vpack):
    R = gpack.shape[0]
    VM = pl.BlockSpec(memory_space=pltpu.VMEM)

    def body(g_ref, w_ref, m_ref, v_ref, gs_ref, d_ref, nm_ref, nv_ref, buf, send_sems, recv_sems):
        peers, me = _peers()
        buf[me] = g_ref[...]
        copies = []
        for k, (dev, _) in enumerate(peers):
            cp = pltpu.make_async_remote_copy(
                src_ref=g_ref, dst_ref=buf.at[me], send_sem=send_sems.at[k], recv_sem=recv_sems.at[k],
                device_id=dev, device_id_type=pl.DeviceIdType.MESH)
            cp.start()
            copies.append(cp)
        for cp in copies:
            cp.wait_recv()
        for cp in copies:
            cp.wait_send()
        g = buf[0]
        for s in range(1, N_DEV):
            g = g + buf[s]
        d, mn, vn = _adam_math(w_ref[...], g, m_ref[...], v_ref[...])
        gs_ref[...] = g
        d_ref[...] = d
        nm_ref[...] = mn
        nv_ref[...] = vn

    return pl.pallas_call(
        body, name="small_update",
        in_specs=[VM] * 4, out_specs=[VM] * 4,
        out_shape=[jax.ShapeDtypeStruct((R, 128), F32)] * 4,
        scratch_shapes=[pltpu.VMEM((N_DEV, R, 128), F32), pltpu.SemaphoreType.DMA((7,)), pltpu.SemaphoreType.DMA((7,))],
        compiler_params=_cp(None, _VMEM_MID),
    )(gpack, wpack, mpack, vpack)


_SMALL = ("norm_g", "b_f", "q_norm_g", "k_norm_g", "w_pool", "pool_scale")


def _pack(parts):
    flat = jnp.concatenate([p.reshape(-1) for p in parts])
    n = flat.shape[0]
    rows = -(-n // (8 * 128)) * 8
    return jnp.pad(flat, (0, rows * 128 - n)).reshape(rows, 128)


def _unpack(packed, like):
    flat = packed.reshape(-1)
    out, o = [], 0
    for p in like:
        out.append(flat[o:o + p.size].reshape(p.shape))
        o += p.size
    return out


def kernel(x, norm_g, w_in, b_f, q_norm_g, k_norm_g, w_pool, pool_scale, w_out, loss_target, m_norm_g, m_w_in, m_b_f, m_q_norm_g, m_k_norm_g, m_w_pool, m_pool_scale, m_w_out, v_norm_g, v_w_in, v_b_f, v_q_norm_g, v_k_norm_g, v_w_pool, v_pool_scale, v_w_out):
    L = w_in.shape[0]

    loss_local, dx, grads, received = _train_step(x[0], loss_target[0], norm_g, w_in.astype(BF16), b_f, q_norm_g,
                                                  k_norm_g, w_pool, pool_scale, w_out.astype(BF16))
    loss = lax.psum(loss_local, MESH_AXES)
    g = {k: jnp.stack([grads[l][k] for l in range(L)]) for k in _SMALL}

    cols = lambda a: a.transpose(2, 0, 1)
    g_win, d_win, nm_win, nv_win = [a.transpose(1, 2, 0) for a in _sum_adamw_cols(
        [r[0] for r in received], cols(w_in), cols(m_w_in), cols(v_w_in), "adamw_w_in")]
    g_wout, d_wout, nm_wout, nv_wout = _sum_adamw([r[1] for r in received], w_out, m_w_out, v_w_out, "adamw_w_out")

    ws = dict(norm_g=norm_g, b_f=b_f, q_norm_g=q_norm_g, k_norm_g=k_norm_g, w_pool=w_pool, pool_scale=pool_scale)
    ms = dict(norm_g=m_norm_g, b_f=m_b_f, q_norm_g=m_q_norm_g, k_norm_g=m_k_norm_g, w_pool=m_w_pool, pool_scale=m_pool_scale)
    vs = dict(norm_g=v_norm_g, b_f=v_b_f, q_norm_g=v_q_norm_g, k_norm_g=v_k_norm_g, w_pool=v_w_pool, pool_scale=v_pool_scale)
    like = [ws[k] for k in _SMALL]
    gs_p, d_p, nm_p, nv_p = _small_update(_pack([g[k] for k in _SMALL]), _pack(like),
                                          _pack([ms[k] for k in _SMALL]), _pack([vs[k] for k in _SMALL]))
    gs = dict(zip(_SMALL, _unpack(gs_p, like)))
    ds = dict(zip(_SMALL, _unpack(d_p, like)))
    nms = dict(zip(_SMALL, _unpack(nm_p, like)))
    nvs = dict(zip(_SMALL, _unpack(nv_p, like)))
    gs["w_in"], ds["w_in"], nms["w_in"], nvs["w_in"] = g_win, d_win, nm_win, nv_win
    gs["w_out"], ds["w_out"], nms["w_out"], nvs["w_out"] = g_wout, d_wout, nm_wout, nv_wout

    order = ("norm_g", "w_in", "b_f", "q_norm_g", "k_norm_g", "w_pool", "pool_scale", "w_out")
    return (loss, dx[None], *[gs[k] for k in order], *[ds[k] for k in order],
            *[nms[k] for k in order], *[nvs[k] for k in order])
```

```python
import jax
import jax.numpy as jnp
from jax import lax
from jax.experimental import pallas as pl
from jax.experimental.pallas import tpu as pltpu

F32 = jnp.float32
BF16 = jnp.bfloat16

EPS = 1e-6
NEG = -1e30
HEAD_DIM = 64
FOX_HEADS = 8
FOX_W = 512
POOL_W = 256
SB_W = 256
D_MIX = 1024
N_FF = 8
N_MAIN = 3584
N_FFPAD = 128
OFF_FQ, OFF_FK, OFF_FV, OFF_FG = 0, 512, 1024, 1536
OFF_PX, OFF_PG = 2048, 2304
OFF_SQ, OFF_SK, OFF_SV, OFF_SG = 2560, 2816, 3072, 3328
D_IN = 3592
Q_SCALE = HEAD_DIM ** -0.5

ADAM_LR = 0.001
ADAM_B1 = 0.9
ADAM_B2 = 0.999
ADAM_EPS = 1e-08
ADAM_WD = 0.01
ADAM_STEP = 10

N_DEV = 8
MESH_AXES = ("x", "y", "c")

_T = 256
_TM = 512
_TM_ROWS = 512
_TM_FWD, _TN_FWD = 2048, 512
_TM_DX = 512
_TK_DW = 1024
_ADAM_ROWS = 16
_VMEM_V7X = 64 << 20
_VMEM_BIG = _VMEM_V7X - (8 << 20)
_VMEM_MID = 40 << 20
_VMEM_WIDE = 48 << 20


def _cp(sem=None, vmem=None):
    kw = {}
    if sem is not None:
        kw["dimension_semantics"] = sem
    if vmem is not None:
        kw["vmem_limit_bytes"] = vmem
    return pltpu.CompilerParams(**kw)


def _dot(a, b):
    return jnp.dot(a, b, preferred_element_type=F32)


def _dot_nt(a, b):
    return lax.dot_general(a, b, (((1,), (1,)), ((), ())), preferred_element_type=F32)


def _dot_tn(a, b):
    return lax.dot_general(a, b, (((0,), (0,)), ((), ())), preferred_element_type=F32)


def _mm2(v, m, left=False):
    hi = v.astype(BF16)
    lo = (v - hi.astype(F32)).astype(BF16)
    if left:
        return _dot(m, hi) + _dot(m, lo)
    return _dot(hi, m) + _dot(lo, m)


def _mm3(v, m, left=False):
    a1 = v.astype(BF16)
    r1 = v - a1.astype(F32)
    a2 = r1.astype(BF16)
    a3 = (r1 - a2.astype(F32)).astype(BF16)
    if left:
        return _dot(m, a1) + _dot(m, a2) + _dot(m, a3)
    return _dot(a1, m) + _dot(a2, m) + _dot(a3, m)


def _sigmoid(z):
    return 1.0 / (1.0 + jnp.exp(-z))


def _rms_rows(x):
    return lax.rsqrt(jnp.mean(x * x, axis=-1, keepdims=True) + EPS)


def _inproj_fwd(x, g, wm, wff):
    S, D = x.shape
    tm = min(_TM_FWD, S)
    tn = _TN_FWD
    assert OFF_SQ % tn == 0 and N_MAIN - OFF_SQ == 4 * SB_W
    j_sb = OFF_SQ // tn

    def body(x_ref, g_ref, w_ref, wff_ref, o_ref, off_ref, ht_ref, sb_ref, h_ref):
        j = pl.program_id(1)

        @pl.when(j == 0)
        def _():
            xv = x_ref[...]
            h = (xv * _rms_rows(xv)) * g_ref[...]
            h_ref[...] = h.astype(BF16)
            ht_ref[...] = h.T.astype(BF16)
            off_ref[...] = _dot(h_ref[...], wff_ref[...])

        res = _dot(h_ref[...], w_ref[...])
        o_ref[...] = res

        @pl.when(j >= j_sb)
        def _():
            sb_ref[...] = res.astype(BF16)

    return pl.pallas_call(
        body, name="inproj_fwd",
        grid=(S // tm, N_MAIN // tn),
        in_specs=[pl.BlockSpec((tm, D), lambda i, j: (i, 0)),
                  pl.BlockSpec((1, D), lambda i, j: (0, 0)),
                  pl.BlockSpec((D, tn), lambda i, j: (0, j)),
                  pl.BlockSpec((D, N_FFPAD), lambda i, j: (0, 0))],
        out_specs=[pl.BlockSpec((tm, tn), lambda i, j: (i, j)),
                   pl.BlockSpec((tm, N_FFPAD), lambda i, j: (i, 0)),
                   pl.BlockSpec((D, tm), lambda i, j: (0, i)),
                   pl.BlockSpec((tm, tn), lambda i, j: (i, jnp.maximum(j - j_sb, 0)))],
        out_shape=[jax.ShapeDtypeStruct((S, N_MAIN), F32), jax.ShapeDtypeStruct((S, N_FFPAD), F32),
                   jax.ShapeDtypeStruct((D, S), BF16), jax.ShapeDtypeStruct((S, 4 * SB_W), BF16)],
        scratch_shapes=[pltpu.VMEM((tm, D), BF16)],
        compiler_params=_cp(("parallel", "arbitrary"), _VMEM_BIG),
    )(x, g, wm, wff)


def _head_norm(x, g, bd):
    ss = _mm2(x * x, bd)
    r = lax.rsqrt(ss * (1.0 / HEAD_DIM) + EPS)
    return (x * r) * g


def _fox_prep(proj, pff, bfp, gq, gk, bd, ex, tril):
    S = proj.shape[0]
    T = tril.shape[0]

    def body(q_ref, k_ref, ff_ref, b_ref, gq_ref, gk_ref, bd_ref, ex_ref, tri_ref,
             qs_ref, kn_ref, cc_ref, cqb_ref, carry):
        @pl.when(pl.program_id(0) == 0)
        def _():
            carry[...] = jnp.zeros_like(carry)

        bdv = bd_ref[...]
        qs_ref[...] = (_head_norm(q_ref[...], gq_ref[...], bdv) * Q_SCALE).astype(BF16)
        kn_ref[...] = _head_norm(k_ref[...], gk_ref[...], bdv).astype(BF16)
        u = ff_ref[...] + b_ref[...]
        lf = jnp.minimum(u, 0.0) - jnp.log1p(jnp.exp(-jnp.abs(u)))
        c = _mm3(lf, tri_ref[...], left=True) + carry[0:1, :]
        carry[0:1, :] = c[T - 1:T, :]
        cc_ref[...] = c
        cqb_ref[...] = _mm3(c, ex_ref[...])

    return pl.pallas_call(
        body, name="fox_prep",
        grid=(S // T,),
        in_specs=[pl.BlockSpec((T, FOX_W), lambda i: (i, OFF_FQ // FOX_W)),
                  pl.BlockSpec((T, FOX_W), lambda i: (i, OFF_FK // FOX_W)),
                  pl.BlockSpec((T, N_FFPAD), lambda i: (i, 0)),
                  pl.BlockSpec((1, N_FFPAD), lambda i: (0, 0)),
                  pl.BlockSpec((1, FOX_W), lambda i: (0, 0)),
                  pl.BlockSpec((1, FOX_W), lambda i: (0, 0)),
                  pl.BlockSpec((FOX_W, FOX_W), lambda i: (0, 0)),
                  pl.BlockSpec((N_FFPAD, FOX_W), lambda i: (0, 0)),
                  pl.BlockSpec((T, T), lambda i: (0, 0))],
        out_specs=[pl.BlockSpec((T, FOX_W), lambda i: (i, 0)),
                   pl.BlockSpec((T, FOX_W), lambda i: (i, 0)),
                   pl.BlockSpec((T, N_FFPAD), lambda i: (i, 0)),
                   pl.BlockSpec((T, FOX_W), lambda i: (i, 0))],
        out_shape=[jax.ShapeDtypeStruct((S, FOX_W), BF16), jax.ShapeDtypeStruct((S, FOX_W), BF16),
                   jax.ShapeDtypeStruct((S, N_FFPAD), F32), jax.ShapeDtypeStruct((S, FOX_W), F32)],
        scratch_shapes=[pltpu.VMEM((8, N_FFPAD), F32)],
        compiler_params=_cp(("arbitrary",), _VMEM_MID),
    )(proj, proj, pff, bfp, gq, gk, bd, ex, tril)


def _pair_blk(S, off=0):
    return pl.BlockSpec((S, 128), lambda p: (0, off + p), pipeline_mode=pl.Buffered(1))


def _pair_rows(S):
    return pl.BlockSpec((None, 8, S), lambda p: (p, 0, 0), pipeline_mode=pl.Buffered(1))


def _head_masks(S):
    return lax.broadcasted_iota(jnp.int32, (S, 128), 1) < HEAD_DIM


_EXP_ZERO = 104.0


def _spread_heads(x):
    src = lax.broadcasted_iota(jnp.int32, (128, 128), 0)
    return (_mm3(x, (src == 0).astype(BF16)), _mm3(x, (src == HEAD_DIM).astype(BF16)))


def _score_bounds(q, k):
    same_head = ((lax.broadcasted_iota(jnp.int32, (128, 128), 0) < HEAD_DIM)
                 == (lax.broadcasted_iota(jnp.int32, (128, 128), 1) < HEAD_DIM)).astype(BF16)

    def max_norm2(x):
        xf = x.astype(F32)
        return jnp.max(_mm2(xf * xf, same_head), axis=0, keepdims=True)

    z = jnp.sqrt(max_norm2(q) * max_norm2(k))
    z = jnp.where(z == z, z, jnp.inf)
    return jnp.max(z[:, 0:1]) * 1.001 + 1e-3, jnp.max(z[:, 64:65]) * 1.001 + 1e-3


def _for_tiles_back(i, n, tiles_fn, fours=False):
    if fours:
        def four(t, c):
            tiles_fn([i - 1 - 4 * t, i - 2 - 4 * t, i - 3 - 4 * t, i - 4 - 4 * t])
            return c

        lax.fori_loop(0, lax.shift_right_logical(n, 2), four, 0)
        rest = i - (n & ~3)

        @pl.when((n & 2) != 0)
        def _():
            tiles_fn([rest - 1, rest - 2])
    else:
        def two(t, c):
            tiles_fn([i - 1 - 2 * t, i - 2 - 2 * t])
            return c

        lax.fori_loop(0, lax.shift_right_logical(n, 1), two, 0)

    @pl.when((n & 1) != 0)
    def _():
        tiles_fn([i - n])


def _fox_tiles_back(cr_ref, i, r0, zba, zbb):
    last = cr_ref[:, pl.ds(0, 128)]
    first = cr_ref[:, pl.ds(r0, 128)]
    alive_a = 2.0 * zba + first[0:1, 0:1] - last[2:3, :] > -_EXP_ZERO
    alive_b = 2.0 * zbb + first[1:2, 0:1] - last[3:4, :] > -_EXP_ZERO
    before = lax.broadcasted_iota(jnp.int32, (1, 128), 1) < i
    return jnp.sum((before & (alive_a | alive_b)).astype(jnp.int32))


def _fox_fwd(qs, kn, proj, cqb, crow4, ride=None):
    S = qs.shape[0]
    T = min(_T, S)
    nq = S // T
    n_pairs = FOX_W // 128

    def body(*refs):
        if ride is None:
            q_ref, k_ref, v_ref, cq_ref, cr_ref, o_ref, lse_ref = refs[:7]
            qa, qb, vta, vtb, cka, ckb, ma, mb, acca, accb = refs[7:]
        else:
            q_ref, k_ref, v_ref, cq_ref, cr_ref, wa_ref, wb_ref, o_ref, lse_ref, ga_ref, gb_ref = refs[:11]
            qa, qb, vta, vtb, cka, ckb, ma, mb, acca, accb = refs[11:21]
            xrefs = (wa_ref, wb_ref, ga_ref, gb_ref) + tuple(refs[21:])

            @pl.when(pl.program_id(0) == 0)
            def _():
                _start_exchange("gather", *xrefs)

        lane_s = _head_masks(S)
        q = q_ref[...]
        zq = jnp.zeros_like(q)
        qa[...] = jnp.where(lane_s, q, zq)
        qb[...] = jnp.where(lane_s, zq, q)
        cq = cq_ref[...]
        cka[...], ckb[...] = _spread_heads(cq)
        lse_ref[...] = jnp.zeros((8, S), F32)
        row_t = lax.broadcasted_iota(jnp.int32, (128, T), 0) < HEAD_DIM
        zba, zbb = _score_bounds(q, k_ref[...])

        def prep(c, carry):
            c0 = pl.multiple_of(c * T, T)
            vt = v_ref[pl.ds(c0, T), :].T
            vta[:, pl.ds(c0, T)] = jnp.where(row_t, vt, 1.0).astype(BF16)
            vtb[:, pl.ds(c0, T)] = jnp.where(row_t, 1.0, vt).astype(BF16)
            return carry

        lax.fori_loop(0, nq, prep, 0)
        causal = (lax.broadcasted_iota(jnp.int32, (T, T), 0) <= lax.broadcasted_iota(jnp.int32, (T, T), 1))

        heads = ((qa, vta, cka, ma, acca), (qb, vtb, ckb, mb, accb))

        def kv(js, r0, masked):
            cr = cr_ref[:, pl.ds(r0, T)]
            c0s = [pl.multiple_of(j * T, T) for j in js]
            ks = [k_ref[pl.ds(c0, T), :] for c0 in c0s]
            ss = []
            for h, (qr, _, ckr, _, _) in enumerate(heads):
                qh = qr[pl.ds(r0, T), :]
                row = []
                for k, c0 in zip(ks, c0s):
                    s = _dot_nt(k, qh) - jnp.tile(ckr[pl.ds(c0, T), :], (1, T // 128))
                    row.append(jnp.where(causal, s, NEG) if masked else s)
                ss.append(row)
            ms = []
            for h, (row, (_, _, _, mr, _)) in enumerate(zip(ss, heads)):
                top = row[0]
                for s in row[1:]:
                    top = jnp.maximum(top, s)
                m_old = mr[0:1, :]
                ms.append((m_old, jnp.maximum(m_old, jnp.max(top, axis=0, keepdims=True) + cr[h:h + 1, :])))
            ps = [[jnp.exp(s + (cr[h:h + 1, :] - m_new)).astype(BF16) for s in row]
                  for h, (row, (_, m_new)) in enumerate(zip(ss, ms))]
            pvs = []
            for row, (_, vr, _, _, _) in zip(ps, heads):
                pv = _dot(vr[:, pl.ds(c0s[0], T)], row[0])
                for p, c0 in zip(row[1:], c0s[1:]):
                    pv = pv + _dot(vr[:, pl.ds(c0, T)], p)
                pvs.append(pv)
            for pv, (m_old, m_new), (_, _, _, mr, ar) in zip(pvs, ms, heads):
                ar[...] = jnp.exp(m_old - m_new) * ar[...] + pv
                mr[0:1, :] = m_new

        def qblk(i, carry):
            r0 = pl.multiple_of(i * T, T)
            ma[...] = jnp.full((8, T), NEG, F32)
            mb[...] = jnp.full((8, T), NEG, F32)
            acca[...] = jnp.zeros((128, T), F32)
            accb[...] = jnp.zeros((128, T), F32)
            kv([i], r0, True)
            done = _fox_tiles_back(cr_ref, i, r0, zba, zbb)
            _for_tiles_back(i, done, lambda js: kv(js, r0, False), fours=True)
            aa = acca[...]
            ab = accb[...]
            la = aa[64:65, :]
            lb = ab[0:1, :]
            o_ref[pl.ds(r0, T), :] = jnp.where(row_t, aa / la, ab / lb).T
            lse_ref[0:1, pl.ds(r0, T)] = ma[0:1, :] + jnp.log(la)
            lse_ref[1:2, pl.ds(r0, T)] = mb[0:1, :] + jnp.log(lb)
            lse_ref[2:3, pl.ds(r0, T)] = jnp.broadcast_to(done.astype(F32), (1, T))
            return carry

        lax.fori_loop(0, nq, qblk, 0)
        if ride is not None:
            @pl.when(pl.program_id(0) == n_pairs - 1)
            def _():
                _wait_exchange("gather", *xrefs)

    extra = () if ride is None else tuple(ride)
    return pl.pallas_call(
        body, name="fox_fwd" if ride is None else "fox_fwd_gather",
        grid=(n_pairs,),
        in_specs=[_pair_blk(S), _pair_blk(S), _pair_blk(S, OFF_FV // 128), _pair_blk(S), _pair_rows(S)]
        + [_ANY] * len(extra),
        out_specs=[_pair_blk(S), _pair_rows(S)] + [_ANY] * len(extra),
        out_shape=[jax.ShapeDtypeStruct((S, FOX_W), F32), jax.ShapeDtypeStruct((n_pairs, 8, S), F32)]
        + (_exchange_out_shapes("gather", *extra) if extra else []),
        scratch_shapes=[pltpu.VMEM((S, 128), BF16)] * 2 + [pltpu.VMEM((128, S), BF16)] * 2
        + [pltpu.VMEM((S, 128), F32)] * 2 + [pltpu.VMEM((8, T), F32)] * 2 + [pltpu.VMEM((128, T), F32)] * 2
        + (_EXCHANGE_SEMS if extra else []),
        compiler_params=_cp(("arbitrary",), _VMEM_BIG),
    )(qs, kn, proj, cqb, crow4, *extra)


def _softplus_parts(z):
    e = jnp.exp(-jnp.abs(z))
    return e, jnp.maximum(z, 0.0) + jnp.log(1.0 + e)


def _sb_fwd(psb, triu):
    S = psb.shape[0]
    T = triu.shape[0]
    nq = S // T

    n_pairs = SB_W // 128
    H = 2 * n_pairs

    def body(q_ref, k_ref, v_ref, tri_ref, o_ref, lt_ref, qm, vt, rr, acc):
        lane_s = _head_masks(S)
        zbs = []
        for p in range(n_pairs):
            q = (q_ref[:, 128 * p:128 * (p + 1)].astype(F32) * Q_SCALE).astype(BF16)
            zq = jnp.zeros_like(q)
            qm[2 * p] = jnp.where(lane_s, q, zq)
            qm[2 * p + 1] = jnp.where(lane_s, zq, q)
            zbs += list(_score_bounds(q, k_ref[:, 128 * p:128 * (p + 1)]))
        lt_ref[...] = jnp.zeros((n_pairs, 8, S), F32)
        row_t = lax.broadcasted_iota(jnp.int32, (128, T), 0) < HEAD_DIM

        def prep(c, carry):
            c0 = pl.multiple_of(c * T, T)
            for p in range(n_pairs):
                vt[p, :, pl.ds(c0, T)] = v_ref[pl.ds(c0, T), 128 * p:128 * (p + 1)].astype(F32).T.astype(BF16)
            return carry

        lax.fori_loop(0, nq, prep, 0)
        strict = (lax.broadcasted_iota(jnp.int32, (T, T), 0) < lax.broadcasted_iota(jnp.int32, (T, T), 1))

        def kv(tiles, r0):
            tri = tri_ref[...]
            c0s = [pl.multiple_of(j * T, T) for j, _ in tiles]
            zs = [[_dot_nt(k_ref[pl.ds(c0, T), 128 * (h // 2):128 * (h // 2 + 1)], qm[h, pl.ds(r0, T), :])
                   for c0 in c0s] for h in range(H)]
            lbs = [[jnp.where(strict, -_softplus_parts(z)[1], 0.0) if masked else -_softplus_parts(z)[1]
                    for z, (_, masked) in zip(row, tiles)] for row in zs]
            incs = [[_mm2(lb, tri, left=True) for lb in row] for row in lbs]
            avs = []
            for h in range(H):
                r = rr[h, 0:1, :]
                av = None
                for z, inc, c0, (_, masked) in zip(zs[h], incs[h], c0s, tiles):
                    a = jnp.exp(z + inc + r)
                    if masked:
                        a = jnp.where(strict, a, 0.0)
                    term = _dot(vt[h // 2, :, pl.ds(c0, T)], a.astype(BF16))
                    av = term if av is None else av + term
                    r = r + inc[0:1, :]
                avs.append((av, r))
            for h, (av, r) in enumerate(avs):
                rr[h, 0:1, :] = r
                acc[h] = acc[h] + av

        def qblk(i, carry):
            r0 = pl.multiple_of(i * T, T)
            rr[...] = jnp.zeros((H, 8, T), F32)
            acc[...] = jnp.zeros((H, 128, T), F32)

            @pl.when(i == 0)
            def _():
                kv([(i, True)], r0)

            @pl.when(i > 0)
            def _():
                kv([(i, True), (i - 1, False)], r0)

            def alive():
                m = jnp.max(rr[0, 0:1, :]) + zbs[0]
                for h in range(1, H):
                    m = jnp.maximum(m, jnp.max(rr[h, 0:1, :]) + zbs[h])
                return m > -_EXP_ZERO

            def cond(st):
                return (st[0] < i) & st[1]

            def step(st):
                kv([(i - 1 - st[0], False)], r0)
                return st[0] + 1, alive()

            done, _ = lax.while_loop(cond, step, (jnp.minimum(i, 1), alive()))
            for p in range(n_pairs):
                o_ref[pl.ds(r0, T), 128 * p:128 * (p + 1)] = jnp.where(row_t, acc[2 * p], acc[2 * p + 1]).T
                lt_ref[p, 0:1, pl.ds(r0, T)] = rr[2 * p, 0:1, :]
                lt_ref[p, 1:2, pl.ds(r0, T)] = rr[2 * p + 1, 0:1, :]
                lt_ref[p, 2:3, pl.ds(r0, T)] = jnp.broadcast_to(done.astype(F32), (1, T))
            return carry

        lax.fori_loop(0, nq, qblk, 0)

    wide = lambda off: pl.BlockSpec((S, SB_W), lambda g: (0, off), pipeline_mode=pl.Buffered(1))
    return pl.pallas_call(
        body, name="sb_fwd",
        grid=(1,),
        in_specs=[wide(0), wide(1), wide(2), pl.BlockSpec((T, T), lambda g: (0, 0))],
        out_specs=[wide(0), pl.BlockSpec((n_pairs, 8, S), lambda g: (0, 0, 0), pipeline_mode=pl.Buffered(1))],
        out_shape=[jax.ShapeDtypeStruct((S, SB_W), F32), jax.ShapeDtypeStruct((n_pairs, 8, S), F32)],
        scratch_shapes=[pltpu.VMEM((H, S, 128), BF16), pltpu.VMEM((n_pairs, 128, S), BF16),
                        pltpu.VMEM((H, 8, T), F32), pltpu.VMEM((H, 128, T), F32)],
        compiler_params=_cp(("arbitrary",), _VMEM_BIG),
    )(psb, psb, psb, triu)


def _pool_window_lanes(shape):
    lane = lax.broadcasted_iota(jnp.int32, shape, 1)
    return jnp.where(lane < 64, 2, jnp.where(lane < 128, 4, jnp.where(lane < 192, 8, 16)))


def _pool_fwd(proj):
    S = proj.shape[0]

    def body(x_ref, o_ref):
        x = x_ref[...]
        t = lax.broadcasted_iota(jnp.int32, x.shape, 0)
        lane = lax.broadcasted_iota(jnp.int32, x.shape, 1)

        def back(a, k):
            return jnp.where(t >= k, pltpu.roll(a, k, 0), 0.0)

        s1 = x + back(x, 1)
        s2 = s1 + back(s1, 2)
        s4 = s2 + back(s2, 4)
        s8 = s4 + back(s4, 8)
        win = jnp.where(lane < 64, s1, jnp.where(lane < 128, s2, jnp.where(lane < 192, s4, s8)))
        cnt = jnp.minimum(t + 1, _pool_window_lanes(x.shape)).astype(F32)
        o_ref[...] = win / cnt - x

    return pl.pallas_call(
        body, name="pool_fwd",
        grid=(1,),
        in_specs=[pl.BlockSpec((S, POOL_W), lambda i: (0, OFF_PX // POOL_W))],
        out_specs=pl.BlockSpec((S, POOL_W), lambda i: (0, 0)),
        out_shape=jax.ShapeDtypeStruct((S, POOL_W), F32),
        compiler_params=_cp(("arbitrary",), _VMEM_BIG),
    )(proj)


def _silu(g):
    return g * _sigmoid(g)


def _mix_out(fo, so, pooled, proj, wbd, scale, wout, x):
    S, D = x.shape
    tm = min(_TM_ROWS, S)

    def body(fo_ref, fg_ref, so_ref, sg_ref, pl_ref, pg_ref, wbd_ref, sc_ref, w_ref, x_ref, y_ref, mxt_ref, mx_ref):
        parts = ((0, fo_ref[...] * _silu(fg_ref[...])),
                 (FOX_W, (_dot(pl_ref[...].astype(BF16), wbd_ref[...]) * sc_ref[...]) * _silu(pg_ref[...])),
                 (FOX_W + POOL_W, so_ref[...] * _silu(sg_ref[...])))
        for off, part in parts:
            w = part.shape[1]
            mx_ref[:, off:off + w] = part.astype(BF16)
            mxt_ref[off:off + w, :] = part.T.astype(BF16)
        y_ref[...] = x_ref[...] + _dot(mx_ref[...], w_ref[...])

    return pl.pallas_call(
        body, name="mix_out",
        grid=(S // tm,),
        in_specs=[pl.BlockSpec((tm, FOX_W), lambda i: (i, 0)),
                  pl.BlockSpec((tm, FOX_W), lambda i: (i, OFF_FG // FOX_W)),
                  pl.BlockSpec((tm, SB_W), lambda i: (i, 0)),
                  pl.BlockSpec((tm, SB_W), lambda i: (i, OFF_SG // SB_W)),
                  pl.BlockSpec((tm, POOL_W), lambda i: (i, 0)),
                  pl.BlockSpec((tm, POOL_W), lambda i: (i, OFF_PG // POOL_W)),
                  pl.BlockSpec((POOL_W, POOL_W), lambda i: (0, 0)),
                  pl.BlockSpec((1, POOL_W), lambda i: (0, 0)),
                  pl.BlockSpec((D_MIX, D), lambda i: (0, 0)),
                  pl.BlockSpec((tm, D), lambda i: (i, 0))],
        out_specs=[pl.BlockSpec((tm, D), lambda i: (i, 0)), pl.BlockSpec((D_MIX, tm), lambda i: (0, i))],
        out_shape=[jax.ShapeDtypeStruct((S, D), F32), jax.ShapeDtypeStruct((D_MIX, S), BF16)],
        scratch_shapes=[pltpu.VMEM((tm, D_MIX), BF16)],
        compiler_params=_cp(("parallel",), _VMEM_MID),
    )(fo, proj, so, proj, pooled, proj, wbd, scale, wout, x)


def _loss_head(y, target):
    S, D = y.shape
    tm = min(_TM, S)

    def body(y_ref, t_ref, dy_ref, ls_ref):
        @pl.when(pl.program_id(0) == 0)
        def _():
            ls_ref[...] = jnp.zeros_like(ls_ref)

        e = y_ref[...] - t_ref[...]
        dy_ref[...] = e * (1.0 / D)
        ls_ref[...] = ls_ref[...] + jnp.sum(e * e) * (0.5 / D)

    dy, ls = pl.pallas_call(
        body, name="loss_head",
        grid=(S // tm,),
        in_specs=[pl.BlockSpec((tm, D), lambda i: (i, 0)), pl.BlockSpec((tm, D), lambda i: (i, 0))],
        out_specs=[pl.BlockSpec((tm, D), lambda i: (i, 0)), pl.BlockSpec((8, 128), lambda i: (0, 0))],
        out_shape=[jax.ShapeDtypeStruct((S, D), F32), jax.ShapeDtypeStruct((8, 128), F32)],
        compiler_params=_cp(("arbitrary",), _VMEM_MID),
    )(y, target)
    return dy, ls[0, 0]


def _dsilu(g):
    s = _sigmoid(g)
    return s * (1.0 + g * (1.0 - s))


def _gate_bwd(dy, wout, fo, so, pooled, proj, wbd, scale):
    S, D = dy.shape
    tm = min(_TM_ROWS, S)

    def body(dy_ref, w_ref, fo_ref, fg_ref, so_ref, sg_ref, pl_ref, pg_ref, wbd_ref, sc_ref,
             dfo_ref, dfg_ref, dso_ref, dsg_ref, dpg_ref, dpl_ref, dsc_ref, dwbd_ref):
        @pl.when(pl.program_id(0) == 0)
        def _():
            dsc_ref[...] = jnp.zeros_like(dsc_ref)
            dwbd_ref[...] = jnp.zeros_like(dwbd_ref)

        dm = _dot_nt(dy_ref[...].astype(BF16), w_ref[...])
        dmf = dm[:, 0:FOX_W]
        dmp = dm[:, FOX_W:FOX_W + POOL_W]
        dms = dm[:, FOX_W + POOL_W:D_MIX]
        fg = fg_ref[...]
        dfo_ref[...] = dmf * _silu(fg)
        dfg_ref[...] = (dmf * fo_ref[...] * _dsilu(fg)).astype(BF16)
        sg = sg_ref[...]
        dso_ref[...] = (dms * _silu(sg)).astype(BF16)
        dsg_ref[...] = (dms * so_ref[...] * _dsilu(sg)).astype(BF16)
        pg = pg_ref[...]
        plb = pl_ref[...].astype(BF16)
        yw = _dot(plb, wbd_ref[...])
        sc = sc_ref[...]
        dpg_ref[...] = (dmp * (yw * sc) * _dsilu(pg)).astype(BF16)
        dys = dmp * _silu(pg)
        dsc_ref[...] = dsc_ref[...] + jnp.sum(dys * yw, axis=0, keepdims=True)
        dyw = (dys * sc).astype(BF16)
        dpl_ref[...] = _dot_nt(dyw, wbd_ref[...])
        dwbd_ref[...] = dwbd_ref[...] + _dot_tn(plb, dyw)

    return pl.pallas_call(
        body, name="gate_bwd",
        grid=(S // tm,),
        in_specs=[pl.BlockSpec((tm, D), lambda i: (i, 0)),
                  pl.BlockSpec((D_MIX, D), lambda i: (0, 0)),
                  pl.BlockSpec((tm, FOX_W), lambda i: (i, 0)),
                  pl.BlockSpec((tm, FOX_W), lambda i: (i, OFF_FG // FOX_W)),
                  pl.BlockSpec((tm, SB_W), lambda i: (i, 0)),
                  pl.BlockSpec((tm, SB_W), lambda i: (i, OFF_SG // SB_W)),
                  pl.BlockSpec((tm, POOL_W), lambda i: (i, 0)),
                  pl.BlockSpec((tm, POOL_W), lambda i: (i, OFF_PG // POOL_W)),
                  pl.BlockSpec((POOL_W, POOL_W), lambda i: (0, 0)),
                  pl.BlockSpec((1, POOL_W), lambda i: (0, 0))],
        out_specs=[pl.BlockSpec((tm, FOX_W), lambda i: (i, 0)),
                   pl.BlockSpec((tm, FOX_W), lambda i: (i, 0)),
                   pl.BlockSpec((tm, SB_W), lambda i: (i, 0)),
                   pl.BlockSpec((tm, SB_W), lambda i: (i, 0)),
                   pl.BlockSpec((tm, POOL_W), lambda i: (i, 0)),
                   pl.BlockSpec((tm, POOL_W), lambda i: (i, 0)),
                   pl.BlockSpec((1, POOL_W), lambda i: (0, 0)),
                   pl.BlockSpec((POOL_W, POOL_W), lambda i: (0, 0))],
        out_shape=[jax.ShapeDtypeStruct((S, FOX_W), F32), jax.ShapeDtypeStruct((S, FOX_W), BF16),
                   jax.ShapeDtypeStruct((S, SB_W), BF16), jax.ShapeDtypeStruct((S, SB_W), BF16),
                   jax.ShapeDtypeStruct((S, POOL_W), BF16), jax.ShapeDtypeStruct((S, POOL_W), F32),
                   jax.ShapeDtypeStruct((1, POOL_W), F32), jax.ShapeDtypeStruct((POOL_W, POOL_W), F32)],
        compiler_params=_cp(("arbitrary",), _VMEM_MID),
    )(dy, wout, fo, proj, so, proj, pooled, proj, wbd, scale)


def _matmul_acc(at, b, name):
    M, S = at.shape
    N = b.shape[1]
    tk = min(_TK_DW, S)
    tn = min(512, N)
    nk = S // tk

    def body(a_ref, b_ref, o_ref, acc):
        k = pl.program_id(1)

        @pl.when(k == 0)
        def _():
            acc[...] = jnp.zeros_like(acc)

        acc[...] = acc[...] + _dot(a_ref[...], b_ref[...].astype(BF16))

        @pl.when(k == nk - 1)
        def _():
            o_ref[...] = acc[...].astype(BF16)

    return pl.pallas_call(
        body, name=name,
        grid=(N // tn, nk),
        in_specs=[pl.BlockSpec((M, tk), lambda j, k: (0, k)), pl.BlockSpec((tk, tn), lambda j, k: (k, j))],
        out_specs=pl.BlockSpec((M, tn), lambda j, k: (0, j)),
        out_shape=jax.ShapeDtypeStruct((M, N), BF16),
        scratch_shapes=[pltpu.VMEM((M, tn), F32)],
        compiler_params=_cp(("parallel", "arbitrary"), _VMEM_MID),
    )(at, b)


def _pool_bwd(dpooled):
    S = dpooled.shape[0]

    def body(d_ref, o_ref):
        d = d_ref[...]
        t = lax.broadcasted_iota(jnp.int32, d.shape, 0)
        lane = lax.broadcasted_iota(jnp.int32, d.shape, 1)
        cnt = jnp.minimum(t + 1, _pool_window_lanes(d.shape)).astype(F32)
        u = d / cnt

        def fwd(a, k):
            return jnp.where(t < S - k, pltpu.roll(a, S - k, 0), 0.0)

        s1 = u + fwd(u, 1)
        s2 = s1 + fwd(s1, 2)
        s4 = s2 + fwd(s2, 4)
        s8 = s4 + fwd(s4, 8)
        win = jnp.where(lane < 64, s1, jnp.where(lane < 128, s2, jnp.where(lane < 192, s4, s8)))
        o_ref[...] = (win - d).astype(BF16)

    return pl.pallas_call(
        body, name="pool_bwd",
        grid=(1,),
        in_specs=[pl.BlockSpec((S, POOL_W), lambda i: (0, 0))],
        out_specs=pl.BlockSpec((S, POOL_W), lambda i: (0, 0)),
        out_shape=jax.ShapeDtypeStruct((S, POOL_W), BF16),
        compiler_params=_cp(("arbitrary",), _VMEM_BIG),
    )(dpooled)


def _fox_bwd(qs, kn, proj, dfo, fo, lse, cqb, crow4, ride=None):
    S = qs.shape[0]
    T = min(_T, S)
    nq = S // T
    n_pairs = FOX_W // 128

    def body(*refs):
        if ride is None:
            q_ref, k_ref, v_ref, do_ref, o_ref, lse_ref, cq_ref, cr_ref = refs[:8]
            dq_ref, dk_ref, dv_ref, dck_ref, dcq_ref = refs[8:13]
            scr = refs[13:]
        else:
            q_ref, k_ref, v_ref, do_ref, o_ref, lse_ref, cq_ref, cr_ref, pa_ref, pb_ref = refs[:10]
            dq_ref, dk_ref, dv_ref, dck_ref, dcq_ref, ra_ref, rb_ref = refs[10:17]
            scr = refs[17:32]
            xrefs = (pa_ref, pb_ref, ra_ref, rb_ref) + tuple(refs[32:])

            @pl.when(pl.program_id(0) == 0)
            def _():
                _start_exchange("scatter", *xrefs)

        qa, qb, kta, ktb, vb, doa, dob, cka, ckb, dcka, dckb, dva, dqt, dcqa, dcqb = scr
        lane_s = _head_masks(S)
        q = q_ref[...]
        zq = jnp.zeros_like(q)
        qa[...] = jnp.where(lane_s, q, zq)
        qb[...] = jnp.where(lane_s, zq, q)
        vb[...] = v_ref[...].astype(BF16)
        do = do_ref[...].astype(BF16)
        doa[...] = jnp.where(lane_s, do, zq)
        dob[...] = jnp.where(lane_s, zq, do)
        cq = cq_ref[...]
        cka[...], ckb[...] = _spread_heads(cq)
        zs = jnp.zeros((S, 128), F32)
        dk_ref[...] = zs
        dva[...] = zs
        dcka[...] = zs
        dckb[...] = zs
        dcq_ref[...] = jnp.zeros((8, S), F32)
        row_t = lax.broadcasted_iota(jnp.int32, (128, T), 0) < HEAD_DIM

        def prep(c, carry):
            c0 = pl.multiple_of(c * T, T)
            kt = k_ref[pl.ds(c0, T), :].astype(F32).T
            kta[:, pl.ds(c0, T)] = jnp.where(row_t, kt, 0.0).astype(BF16)
            ktb[:, pl.ds(c0, T)] = jnp.where(row_t, 0.0, kt).astype(BF16)
            return carry

        lax.fori_loop(0, nq, prep, 0)
        causal = (lax.broadcasted_iota(jnp.int32, (T, T), 0) <= lax.broadcasted_iota(jnp.int32, (T, T), 1))

        heads = ((qa, kta, doa, cka, dcka, dcqa), (qb, ktb, dob, ckb, dckb, dcqb))

        def kv(js, r0, lss, dls, masked):
            cr = cr_ref[:, pl.ds(r0, T)]
            c0s = [pl.multiple_of(j * T, T) for j in js]
            ks = [k_ref[pl.ds(c0, T), :] for c0 in c0s]
            vs = [vb[pl.ds(c0, T), :] for c0 in c0s]
            qhs = [hd[0][pl.ds(r0, T), :] for hd in heads]
            dohs = [hd[2][pl.ds(r0, T), :] for hd in heads]
            ss = []
            for h, hd in enumerate(heads):
                row = []
                for k, c0 in zip(ks, c0s):
                    s = _dot_nt(k, qhs[h]) - jnp.tile(hd[3][pl.ds(c0, T), :], (1, T // 128))
                    row.append(jnp.where(causal, s, NEG) if masked else s)
                ss.append(row)
            ps = [[jnp.exp(s + (cr[h:h + 1, :] - lss[h])) for s in row] for h, row in enumerate(ss)]
            dps = [[_dot_nt(v, dohs[h]) for v in vs] for h in range(2)]
            dss = [[p * (dp - dls[h]) for p, dp in zip(ps[h], dps[h])] for h in range(2)]
            pbs = [[p.astype(BF16) for p in row] for row in ps]
            dsbs = [[ds.astype(BF16) for ds in row] for row in dss]
            for t, c0 in enumerate(c0s):
                dva[pl.ds(c0, T), :] = dva[pl.ds(c0, T), :] + (_dot(pbs[0][t], dohs[0]) + _dot(pbs[1][t], dohs[1]))
                dk_ref[pl.ds(c0, T), :] = dk_ref[pl.ds(c0, T), :] + (_dot(dsbs[0][t], qhs[0]) + _dot(dsbs[1][t], qhs[1]))
            dq = None
            for h, hd in enumerate(heads):
                for t, c0 in enumerate(c0s):
                    term = _dot(hd[1][:, pl.ds(c0, T)], dsbs[h][t])
                    dq = term if dq is None else dq + term
            dqt[...] = dqt[...] + dq
            for h, hd in enumerate(heads):
                col = jnp.sum(dss[h][0], axis=0, keepdims=True)
                for ds in dss[h][1:]:
                    col = col + jnp.sum(ds, axis=0, keepdims=True)
                hd[5][0:1, :] = hd[5][0:1, :] + col
                for ds, c0 in zip(dss[h], c0s):
                    fold = ds[:, 0:128]
                    for u in range(1, T // 128):
                        fold = fold + ds[:, 128 * u:128 * (u + 1)]
                    hd[4][pl.ds(c0, T), :] = hd[4][pl.ds(c0, T), :] - fold

        def qblk(i, carry):
            r0 = pl.multiple_of(i * T, T)
            dt = (do_ref[pl.ds(r0, T), :] * o_ref[pl.ds(r0, T), :]).T
            dla = jnp.sum(jnp.where(row_t, dt, 0.0), axis=0, keepdims=True)
            dlb = jnp.sum(jnp.where(row_t, 0.0, dt), axis=0, keepdims=True)
            ls = lse_ref[:, pl.ds(r0, T)]
            lss = (ls[0:1, :], ls[1:2, :])
            back = jnp.max(ls[2:3, :]).astype(jnp.int32)
            dqt[...] = jnp.zeros((128, T), F32)
            dcqa[...] = jnp.zeros((8, T), F32)
            dcqb[...] = jnp.zeros((8, T), F32)
            kv([i], r0, lss, (dla, dlb), True)
            _for_tiles_back(i, back, lambda js: kv(js, r0, lss, (dla, dlb), False), fours=True)
            dq_ref[pl.ds(r0, T), :] = dqt[...].T
            dcq_ref[0:1, pl.ds(r0, T)] = dcqa[0:1, :]
            dcq_ref[1:2, pl.ds(r0, T)] = dcqb[0:1, :]
            return carry

        lax.fori_loop(0, nq, qblk, 0)
        dv_ref[...] = dva[...].astype(BF16)
        dck_ref[...] = jnp.where(lane_s, jnp.sum(dcka[...], axis=1, keepdims=True),
                                 jnp.sum(dckb[...], axis=1, keepdims=True))
        if ride is not None:
            @pl.when(pl.program_id(0) == n_pairs - 1)
            def _():
                _wait_exchange("scatter", *xrefs)

    extra = () if ride is None else tuple(ride)
    return pl.pallas_call(
        body, name="fox_bwd" if ride is None else "fox_bwd_exchange",
        grid=(n_pairs,),
        in_specs=[_pair_blk(S), _pair_blk(S), _pair_blk(S, OFF_FV // 128), _pair_blk(S), _pair_blk(S),
                  _pair_rows(S), _pair_blk(S), _pair_rows(S)] + [_ANY] * len(extra),
        out_specs=[_pair_blk(S), _pair_blk(S), _pair_blk(S), _pair_blk(S), _pair_rows(S)] + [_ANY] * len(extra),
        out_shape=[jax.ShapeDtypeStruct((S, FOX_W), F32), jax.ShapeDtypeStruct((S, FOX_W), F32),
                   jax.ShapeDtypeStruct((S, FOX_W), BF16), jax.ShapeDtypeStruct((S, FOX_W), F32),
                   jax.ShapeDtypeStruct((n_pairs, 8, S), F32)]
        + (_exchange_out_shapes("scatter", *extra) if extra else []),
        scratch_shapes=[pltpu.VMEM((S, 128), BF16)] * 2 + [pltpu.VMEM((128, S), BF16)] * 2
        + [pltpu.VMEM((S, 128), BF16)] * 3 + [pltpu.VMEM((S, 128), F32)] * 5
        + [pltpu.VMEM((128, T), F32)] + [pltpu.VMEM((8, T), F32)] * 2
        + (_EXCHANGE_SEMS if extra else []),
        compiler_params=_cp(("arbitrary",), _VMEM_BIG),
    )(qs, kn, proj, dfo, fo, lse, cqb, crow4, *extra)


def _sb_bwd(psb, dso, ltot, tril):
    S = psb.shape[0]
    T = tril.shape[0]
    nq = S // T
    n_pairs = SB_W // 128
    H = 2 * n_pairs

    def body(q_ref, k_ref, v_ref, do_ref, lt_ref, tri_ref, dq_ref, dk_ref, dv_ref,
             qm, kt, dka, dva, dqt, rr, gg):
        lane_s = _head_masks(S)
        for p in range(n_pairs):
            q = (q_ref[:, 128 * p:128 * (p + 1)].astype(F32) * Q_SCALE).astype(BF16)
            zq = jnp.zeros_like(q)
            qm[2 * p] = jnp.where(lane_s, q, zq)
            qm[2 * p + 1] = jnp.where(lane_s, zq, q)
        dka[...] = jnp.zeros((n_pairs, S, 128), F32)
        dva[...] = jnp.zeros((n_pairs, S, 128), F32)
        row_t = lax.broadcasted_iota(jnp.int32, (128, T), 0) < HEAD_DIM
        lane_t = lax.broadcasted_iota(jnp.int32, (T, 128), 1) < HEAD_DIM

        def prep(c, carry):
            c0 = pl.multiple_of(c * T, T)
            for p in range(n_pairs):
                kt[p, :, pl.ds(c0, T)] = k_ref[pl.ds(c0, T), 128 * p:128 * (p + 1)].astype(F32).T.astype(BF16)
            return carry

        lax.fori_loop(0, nq, prep, 0)
        strict = (lax.broadcasted_iota(jnp.int32, (T, T), 0) < lax.broadcasted_iota(jnp.int32, (T, T), 1))

        def own(x, h, mask):
            z = jnp.zeros_like(x)
            return jnp.where(mask, x, z) if h % 2 == 0 else jnp.where(mask, z, x)

        def pair(ref, p, c0):
            return ref[pl.ds(c0, T), 128 * p:128 * (p + 1)]

        def kv(tiles, r0, lts):
            tri = tri_ref[...]
            c0s = [pl.multiple_of(j * T, T) for j, _ in tiles]
            qhs = [qm[h, pl.ds(r0, T), :] for h in range(H)]
            dohs = [own(pair(do_ref, h // 2, r0), h, lane_t) for h in range(H)]
            zs = [[_dot_nt(pair(k_ref, h // 2, c0), qhs[h]) for c0 in c0s] for h in range(H)]
            das = [[_dot_nt(pair(v_ref, h // 2, c0), dohs[h]) for c0 in c0s] for h in range(H)]
            es, lbs = [], []
            for row in zs:
                erow, lrow = [], []
                for z, (_, masked) in zip(row, tiles):
                    e, sp = _softplus_parts(z)
                    erow.append(e)
                    lrow.append(jnp.where(strict, -sp, 0.0) if masked else -sp)
                es.append(erow)
                lbs.append(lrow)
            pres = [[_mm2(lb, tri, left=True) for lb in row] for row in lbs]
            aas, r_ends = [], []
            for h in range(H):
                r = rr[h, 0:1, :]
                arow = []
                for z, lb, pre, (_, masked) in zip(zs[h], lbs[h], pres[h], tiles):
                    a = jnp.exp(z + lb + ((lts[h] - r) - pre))
                    arow.append(jnp.where(strict, a, 0.0) if masked else a)
                    r = r + pre[T - 1:T, :]
                aas.append(arow)
                r_ends.append(r)
            gs = [[a * da for a, da in zip(arow, drow)] for arow, drow in zip(aas, das)]
            gpres = [[_mm2(g, tri, left=True) for g in row] for row in gs]
            dzbs, g_ends = [], []
            for h in range(H):
                gc = gg[h, 0:1, :]
                drow = []
                for z, e, g, gpre, (_, masked) in zip(zs[h], es[h], gs[h], gpres[h], tiles):
                    inv = 1.0 / (1.0 + e)
                    pos = z >= 0.0
                    sig = jnp.where(pos, 1.0, e) * inv
                    oms = jnp.where(pos, e, 1.0) * inv
                    dz = g * oms - sig * (gc + (gpre - g))
                    if masked:
                        dz = jnp.where(strict, dz, 0.0)
                    drow.append(dz.astype(BF16))
                    gc = gc + gpre[T - 1:T, :]
                dzbs.append(drow)
                g_ends.append(gc)
            for p in range(n_pairs):
                a, b = 2 * p, 2 * p + 1
                dq = None
                for h in (a, b):
                    for t, c0 in enumerate(c0s):
                        term = _dot(own(kt[p, :, pl.ds(c0, T)], h, row_t), dzbs[h][t])
                        dq = term if dq is None else dq + term
                dqt[p] = dqt[p] + dq
                for t, c0 in enumerate(c0s):
                    dka[p, pl.ds(c0, T), :] = dka[p, pl.ds(c0, T), :] + (_dot(dzbs[a][t], qhs[a]) + _dot(dzbs[b][t], qhs[b]))
                    dva[p, pl.ds(c0, T), :] = dva[p, pl.ds(c0, T), :] + (_dot(aas[a][t].astype(BF16), dohs[a])
                                                                      + _dot(aas[b][t].astype(BF16), dohs[b]))
            for h in range(H):
                rr[h, 0:1, :] = r_ends[h]
                gg[h, 0:1, :] = g_ends[h]

        def qblk(i, carry):
            r0 = pl.multiple_of(i * T, T)
            lts = []
            for p in range(n_pairs):
                lt = lt_ref[p, :, pl.ds(r0, T)]
                lts += [lt[0:1, :], lt[1:2, :]]
            back = jnp.max(lt_ref[0, 2:3, pl.ds(r0, T)]).astype(jnp.int32)
            dqt[...] = jnp.zeros((n_pairs, 128, T), F32)
            rr[...] = jnp.zeros((H, 8, T), F32)
            gg[...] = jnp.zeros((H, 8, T), F32)

            def inner(j, c):
                kv([(j, False)], r0, lts)
                return c

            @pl.when(back == 0)
            def _():
                kv([(i, True)], r0, lts)

            @pl.when(back > 0)
            def _():
                lax.fori_loop(i - back, i - 1, inner, 0)
                kv([(i - 1, False), (i, True)], r0, lts)

            for p in range(n_pairs):
                dq_ref[pl.ds(r0, T), 128 * p:128 * (p + 1)] = (dqt[p] * Q_SCALE).T.astype(BF16)
            return carry

        lax.fori_loop(0, nq, qblk, 0)
        for p in range(n_pairs):
            dk_ref[:, 128 * p:128 * (p + 1)] = dka[p].astype(BF16)
            dv_ref[:, 128 * p:128 * (p + 1)] = dva[p].astype(BF16)

    wide = lambda off: pl.BlockSpec((S, SB_W), lambda g: (0, off), pipeline_mode=pl.Buffered(1))
    return pl.pallas_call(
        body, name="sb_bwd",
        grid=(1,),
        in_specs=[wide(0), wide(1), wide(2), wide(0),
                  pl.BlockSpec((n_pairs, 8, S), lambda g: (0, 0, 0), pipeline_mode=pl.Buffered(1)),
                  pl.BlockSpec((T, T), lambda g: (0, 0))],
        out_specs=[wide(0), wide(0), wide(0)],
        out_shape=[jax.ShapeDtypeStruct((S, SB_W), BF16)] * 3,
        scratch_shapes=[pltpu.VMEM((H, S, 128), BF16), pltpu.VMEM((n_pairs, 128, S), BF16),
                        pltpu.VMEM((n_pairs, S, 128), F32), pltpu.VMEM((n_pairs, S, 128), F32),
                        pltpu.VMEM((n_pairs, 128, T), F32), pltpu.VMEM((H, 8, T), F32), pltpu.VMEM((H, 8, T), F32)],
        compiler_params=_cp(("arbitrary",), _VMEM_BIG),
    )(psb, psb, psb, dso, ltot, tril)


def _head_norm_bwd(x, g, dy, bd):
    ss = _mm2(x * x, bd)
    r = lax.rsqrt(ss * (1.0 / HEAD_DIM) + EPS)
    xr = x * r
    gdy = g * dy
    m = _mm2(xr * gdy, bd) * (1.0 / HEAD_DIM)
    return r * (gdy - xr * m), dy * xr


def _qk_bwd(dqs, dkn, proj, pff, bfp, gq, gk, bd, dck, dcq, triu):
    S = proj.shape[0]
    T = triu.shape[0]
    n = S // T
    rev = lambda col: (lambda i: (n - 1 - i, col))

    def body(dq_ref, dk_ref, q_ref, k_ref, ff_ref, b_ref, gq_ref, gk_ref, bd_ref, dck_ref, dcq_ref, tri_ref,
             dfq_ref, dfk_ref, dff_ref, dgq_ref, dgk_ref, dbf_ref, carry):
        @pl.when(pl.program_id(0) == 0)
        def _():
            carry[...] = jnp.zeros_like(carry)
            dgq_ref[...] = jnp.zeros_like(dgq_ref)
            dgk_ref[...] = jnp.zeros_like(dgk_ref)
            dbf_ref[...] = jnp.zeros_like(dbf_ref)

        bdv = bd_ref[...]
        dxq, gq_rows = _head_norm_bwd(q_ref[...], gq_ref[...], dq_ref[...] * Q_SCALE, bdv)
        dfq_ref[...] = dxq.astype(BF16)
        dgq_ref[...] = dgq_ref[...] + jnp.sum(gq_rows, axis=0, keepdims=True)
        dxk, gk_rows = _head_norm_bwd(k_ref[...], gk_ref[...], dk_ref[...], bdv)
        dfk_ref[...] = dxk.astype(BF16)
        dgk_ref[...] = dgk_ref[...] + jnp.sum(gk_rows, axis=0, keepdims=True)
        first = (lax.broadcasted_iota(jnp.int32, (FOX_W, N_FFPAD), 0)
                 == HEAD_DIM * lax.broadcasted_iota(jnp.int32, (FOX_W, N_FFPAD), 1)).astype(BF16)
        dc = _mm3(dck_ref[...], first) + dcq_ref[...]
        dlf = _mm3(dc, tri_ref[...], left=True) + carry[0:1, :]
        carry[0:1, :] = dlf[0:1, :]
        u = ff_ref[...] + b_ref[...]
        lane = lax.broadcasted_iota(jnp.int32, u.shape, 1)
        dff = jnp.where(lane < N_FF, dlf * _sigmoid(-u), 0.0)
        dff_ref[...] = dff.astype(BF16)
        dbf_ref[...] = dbf_ref[...] + jnp.sum(dff, axis=0, keepdims=True)

    return pl.pallas_call(
        body, name="qk_bwd",
        grid=(n,),
        in_specs=[pl.BlockSpec((T, FOX_W), rev(0)), pl.BlockSpec((T, FOX_W), rev(0)),
                  pl.BlockSpec((T, FOX_W), rev(OFF_FQ // FOX_W)), pl.BlockSpec((T, FOX_W), rev(OFF_FK // FOX_W)),
                  pl.BlockSpec((T, N_FFPAD), rev(0)),
                  pl.BlockSpec((1, N_FFPAD), lambda i: (0, 0)),
                  pl.BlockSpec((1, FOX_W), lambda i: (0, 0)), pl.BlockSpec((1, FOX_W), lambda i: (0, 0)),
                  pl.BlockSpec((FOX_W, FOX_W), lambda i: (0, 0)),
                  pl.BlockSpec((T, FOX_W), rev(0)), pl.BlockSpec((T, N_FFPAD), rev(0)),
                  pl.BlockSpec((T, T), lambda i: (0, 0))],
        out_specs=[pl.BlockSpec((T, FOX_W), rev(0)), pl.BlockSpec((T, FOX_W), rev(0)),
                   pl.BlockSpec((T, N_FFPAD), rev(0)),
                   pl.BlockSpec((1, FOX_W), lambda i: (0, 0)), pl.BlockSpec((1, FOX_W), lambda i: (0, 0)),
                   pl.BlockSpec((1, N_FFPAD), lambda i: (0, 0))],
        out_shape=[jax.ShapeDtypeStruct((S, FOX_W), BF16), jax.ShapeDtypeStruct((S, FOX_W), BF16),
                   jax.ShapeDtypeStruct((S, N_FFPAD), BF16),
                   jax.ShapeDtypeStruct((1, FOX_W), F32), jax.ShapeDtypeStruct((1, FOX_W), F32),
                   jax.ShapeDtypeStruct((1, N_FFPAD), F32)],
        scratch_shapes=[pltpu.VMEM((8, N_FFPAD), F32)],
        compiler_params=_cp(("arbitrary",), _VMEM_MID),
    )(dqs, dkn, proj, proj, pff, bfp, gq, gk, bd, dck, dcq, triu)


def _dproj_layout(pieces):
    offs, o = [], 0
    for p in pieces:
        offs.append(o)
        o += p.shape[1]
    assert o == N_MAIN
    return offs


def _inproj_bwd_dx(pieces, dff, wm, wff, x, g, dy, ride=None):
    S, D = x.shape
    tm = min(_TM_DX, S)
    steps = S // tm
    offs = _dproj_layout(pieces)
    n = len(pieces)

    def body(*refs):
        p_refs = refs[:n]
        if ride is None:
            dff_ref, w_ref, wff_ref, x_ref, g_ref, dy_ref, dx_ref, dg_ref = refs[n:]
        else:
            dff_ref, w_ref, wff_ref, x_ref, g_ref, dy_ref, pa_ref, pb_ref = refs[n:n + 8]
            dx_ref, dg_ref, ra_ref, rb_ref = refs[n + 8:n + 12]
            xrefs = (pa_ref, pb_ref, ra_ref, rb_ref) + tuple(refs[n + 12:])

        @pl.when(pl.program_id(0) == 0)
        def _():
            dg_ref[...] = jnp.zeros_like(dg_ref)
            if ride is not None:
                _start_exchange("scatter", *xrefs)

        dh = _dot_nt(dff_ref[...], wff_ref[...])
        for p_ref, off in zip(p_refs, offs):
            dh = dh + _dot_nt(p_ref[...], w_ref[:, off:off + p_ref.shape[1]])
        xv = x_ref[...]
        r = _rms_rows(xv)
        xr = xv * r
        dg_ref[...] = dg_ref[...] + jnp.sum(dh * xr, axis=0, keepdims=True)
        gdh = g_ref[...] * dh
        m = jnp.mean(gdh * xr, axis=-1, keepdims=True)
        dx_ref[...] = dy_ref[...] + r * (gdh - xr * m)
        if ride is not None:
            @pl.when(pl.program_id(0) == steps - 1)
            def _():
                _wait_exchange("scatter", *xrefs)

    extra = () if ride is None else tuple(ride)
    return pl.pallas_call(
        body, name="inproj_bwd_dx" if ride is None else "inproj_bwd_dx_exchange",
        grid=(steps,),
        in_specs=[pl.BlockSpec((tm, p.shape[1]), lambda i: (i, 0)) for p in pieces]
        + [pl.BlockSpec((tm, N_FFPAD), lambda i: (i, 0)),
                  pl.BlockSpec((D, N_MAIN), lambda i: (0, 0)),
                  pl.BlockSpec((D, N_FFPAD), lambda i: (0, 0)),
                  pl.BlockSpec((tm, D), lambda i: (i, 0)),
                  pl.BlockSpec((1, D), lambda i: (0, 0)),
                  pl.BlockSpec((tm, D), lambda i: (i, 0))] + [_ANY] * len(extra),
        out_specs=[pl.BlockSpec((tm, D), lambda i: (i, 0)), pl.BlockSpec((1, D), lambda i: (0, 0))] + [_ANY] * len(extra),
        out_shape=[jax.ShapeDtypeStruct((S, D), F32), jax.ShapeDtypeStruct((1, D), F32)]
        + (_exchange_out_shapes("scatter", *extra) if extra else []),
        scratch_shapes=_EXCHANGE_SEMS if extra else [],
        compiler_params=_cp(("arbitrary",), _VMEM_WIDE),
    )(*pieces, dff, wm, wff, x, g, dy, *extra)


def _inproj_bwd_dw(ht, pieces, dff):
    D, S = ht.shape
    tk = min(_TK_DW, S)
    nk = S // tk
    offs = _dproj_layout(pieces)
    n = len(pieces)

    def body(*refs):
        ht_ref, p_refs, dff_ref = refs[0], refs[1:1 + n], refs[1 + n]
        dw_ref, dwff_ref, acc, accff = refs[2 + n:]
        k = pl.program_id(0)

        @pl.when(k == 0)
        def _():
            acc[...] = jnp.zeros_like(acc)
            accff[...] = jnp.zeros_like(accff)

        hb = ht_ref[...]
        for p_ref, off in zip(p_refs, offs):
            w = p_ref.shape[1]
            acc[:, off:off + w] = acc[:, off:off + w] + _dot(hb, p_ref[...])
        accff[...] = accff[...] + _dot(hb, dff_ref[...])

        @pl.when(k == nk - 1)
        def _():
            for c0 in range(0, N_MAIN, FOX_W):
                dw_ref[c0:c0 + FOX_W, :] = acc[:, c0:c0 + FOX_W].T.astype(BF16)
            dwff_ref[...] = accff[...].T.astype(BF16)

    return pl.pallas_call(
        body, name="inproj_bwd_dw",
        grid=(nk,),
        in_specs=[pl.BlockSpec((D, tk), lambda k: (0, k))]
        + [pl.BlockSpec((tk, p.shape[1]), lambda k: (k, 0)) for p in pieces]
        + [pl.BlockSpec((tk, N_FFPAD), lambda k: (k, 0))],
        out_specs=[pl.BlockSpec((N_MAIN, D), lambda k: (0, 0), pipeline_mode=pl.Buffered(1)),
                   pl.BlockSpec((N_FFPAD, D), lambda k: (0, 0), pipeline_mode=pl.Buffered(1))],
        out_shape=[jax.ShapeDtypeStruct((N_MAIN, D), BF16), jax.ShapeDtypeStruct((N_FFPAD, D), BF16)],
        scratch_shapes=[pltpu.VMEM((D, N_MAIN), F32), pltpu.VMEM((D, N_FFPAD), F32)],
        compiler_params=_cp(("arbitrary",), _VMEM_BIG),
    )(ht, *pieces, dff)


def _constants(T, rows):
    tril = jnp.tril(jnp.ones((T, T), F32)).astype(BF16)
    tril_rows = jnp.tril(jnp.ones((rows, rows), F32)).astype(BF16)
    hid = jnp.arange(FOX_W) // HEAD_DIM
    bd = (hid[:, None] == hid[None, :]).astype(BF16)
    ex = (jnp.arange(N_FFPAD)[:, None] == hid[None, :]).astype(BF16)
    return tril, tril.T, bd, ex, tril_rows, tril_rows.T


def _crow4(ccol, T):
    S = ccol.shape[0]
    c = ccol[:, :FOX_HEADS].T
    last = jnp.pad(c[:, T - 1::T], ((0, 0), (0, S - S // T)))
    rows = jnp.concatenate([c.reshape(FOX_HEADS // 2, 2, S), last.reshape(FOX_HEADS // 2, 2, S)], axis=1)
    return jnp.pad(rows, ((0, 0), (0, 4), (0, 0)))


def _layer_fwd(x, lw, consts, ride=None):
    tril, triu, bd, ex, tril_rows, _ = consts
    proj, pff, ht, psb = _inproj_fwd(x, lw["g"], lw["wm"], lw["wff"])
    qs, kn, ccol, cqb = _fox_prep(proj, pff, lw["bfp"], lw["gq"], lw["gk"], bd, ex, tril_rows)
    crow4 = _crow4(ccol, tril.shape[0])
    fo, lse, *gathered = _fox_fwd(qs, kn, proj, cqb, crow4, ride)
    so, ltot = _sb_fwd(psb, triu)
    pooled = _pool_fwd(proj)
    y, mixedt = _mix_out(fo, so, pooled, proj, lw["wbd"], lw["scale"], lw["wout"], x)
    return y, (x, proj, pff, ht, psb, qs, kn, cqb, crow4, fo, lse, so, ltot, pooled, mixedt), gathered


def _layer_bwd(dy, saved, lw, consts, ride=None, exchange_own=False):
    tril, _, bd, _, _, triu_rows = consts
    x, proj, pff, ht, psb, qs, kn, cqb, crow4, fo, lse, so, ltot, pooled, mixedt = saved
    S = x.shape[0]
    dfo, dfg, dso, dsg, dpg, dpooled, dscale, dwbd = _gate_bwd(dy, lw["wout"], fo, so, pooled, proj, lw["wbd"], lw["scale"])
    dwout = _matmul_acc(mixedt, dy, "dw_out")
    dpx = _pool_bwd(dpooled)
    dqs, dkn, dfv, dck, dcq4, *received = _fox_bwd(qs, kn, proj, dfo, fo, lse, cqb, crow4, ride)
    dsq, dsk, dsv = _sb_bwd(psb, dso, ltot, tril)
    dcq = jnp.pad(dcq4[:, :2, :].reshape(FOX_HEADS, S).T, ((0, 0), (0, N_FFPAD - FOX_HEADS)))
    dfq, dfk, dff, dgq, dgk, dbf = _qk_bwd(dqs, dkn, proj, pff, lw["bfp"], lw["gq"], lw["gk"], bd, dck, dcq, triu_rows)
    pieces = [dfq, dfk, dfv, dfg, dpx, dpg, dsq, dsk, dsv, dsg]
    dwm_t, dwff_t = _inproj_bwd_dw(ht, pieces, dff)
    dwin_t = jnp.concatenate([dwm_t[:OFF_PX], dwff_t[:N_FF], dwm_t[OFF_PX:]], axis=0)
    own = _grad_parts({"w_in_t": dwin_t, "w_out": dwout}) if exchange_own else None
    dx, dng, *received_own = _inproj_bwd_dx(pieces, dff, lw["wm"], lw["wff"], x, lw["g"], dy, own)
    grads = {
        "norm_g": dng[0],
        "w_in_t": dwin_t,
        "b_f": dbf[0, :N_FF],
        "q_norm_g": dgq[0].reshape(FOX_HEADS, HEAD_DIM).sum(0),
        "k_norm_g": dgk[0].reshape(FOX_HEADS, HEAD_DIM).sum(0),
        "w_pool": jnp.stack([dwbd[64 * i:64 * i + 64, 64 * i:64 * i + 64] for i in range(4)]),
        "pool_scale": dscale[0],
        "w_out": dwout,
    }
    return dx, grads, received, received_own


def _layer_weights(l, norm_g, gin, b_f, q_norm_g, k_norm_g, w_pool, pool_scale, gout):
    D = gin.shape[1]
    w = gin.transpose(1, 0, 2).reshape(D, D_IN)
    wm = jnp.concatenate([w[:, :2048], w[:, 2048 + N_FF:]], axis=1)
    wff = jnp.pad(w[:, 2048:2048 + N_FF], ((0, 0), (0, N_FFPAD - N_FF)))
    grp = jnp.arange(POOL_W) // 64
    wbd = jnp.where(grp[:, None] == grp[None, :], jnp.tile(w_pool[l].transpose(1, 0, 2).reshape(64, POOL_W), (4, 1)), 0.0)
    return {
        "g": norm_g[l].reshape(1, D),
        "wm": wm, "wff": wff,
        "bfp": jnp.pad(b_f[l], (0, N_FFPAD - N_FF)).reshape(1, N_FFPAD),
        "gq": jnp.tile(q_norm_g[l], FOX_HEADS).reshape(1, FOX_W),
        "gk": jnp.tile(k_norm_g[l], FOX_HEADS).reshape(1, FOX_W),
        "wbd": wbd.astype(BF16),
        "scale": pool_scale[l].reshape(1, POOL_W),
        "wout": gout.reshape(D_MIX, D),
    }


def _grad_parts(g):
    dwin_t, dwout = g["w_in_t"].astype(BF16), g["w_out"].astype(BF16)
    return (dwin_t.reshape(N_DEV, D_IN // N_DEV, dwin_t.shape[1]),
            dwout.reshape(N_DEV, D_MIX // N_DEV, dwout.shape[1]))


def _train_step(x, target, norm_g, win_sh, b_f, q_norm_g, k_norm_g, w_pool, pool_scale, wout_sh):
    L = norm_g.shape[0]
    consts = _constants(min(_T, x.shape[0]), min(_TM_ROWS, x.shape[0]))
    gathered = _gather_two_level(win_sh[0], wout_sh[0], "gather_weights")
    lws, saved = [], []
    h = x
    for l in range(L):
        lws.append(_layer_weights(l, norm_g, gathered[0], b_f, q_norm_g, k_norm_g, w_pool, pool_scale, gathered[1]))
        ride = (win_sh[l + 1], wout_sh[l + 1]) if l + 1 < L else None
        h, sv, gathered = _layer_fwd(h, lws[l], consts, ride)
        saved.append(sv)
    dy, loss = _loss_head(h, target)
    grads, received = [None] * L, [None] * L
    ride = None
    for l in reversed(range(L)):
        dy, grads[l], got, got_own = _layer_bwd(dy, saved[l], lws[l], consts, ride, exchange_own=(l == 0))
        if ride is not None:
            received[l + 1] = got
        if l == 0:
            received[0] = got_own
        else:
            ride = _grad_parts(grads[l])
    return loss, dy, grads, received


def _mesh_pos():
    return lax.axis_index("x"), lax.axis_index("y"), lax.axis_index("c")


_FLIPS = [(0, 0, 1), (1, 0, 0), (0, 1, 0), (1, 1, 0), (1, 0, 1), (0, 1, 1), (1, 1, 1)]


def _peers():
    x, y, c = _mesh_pos()
    out = []
    for fx, fy, fc in _FLIPS:
        px = 1 - x if fx else x
        py = 1 - y if fy else y
        pc = 1 - c if fc else c
        out.append(((px, py, pc), 4 * px + 2 * py + pc))
    return out, 4 * x + 2 * y + c


_EXCHANGE_SEMS = [pltpu.SemaphoreType.DMA((14,)), pltpu.SemaphoreType.DMA((14,)), pltpu.SemaphoreType.DMA((2,))]
_ANY = pl.BlockSpec(memory_space=pl.ANY)


def _exchange_copies(kind, a_ref, b_ref, oa_ref, ob_ref, send_sems, recv_sems, loc_sems):
    peers, me = _peers()
    pairs = ((a_ref, oa_ref), (b_ref, ob_ref))
    local = [pltpu.make_async_copy(src if kind == "gather" else src.at[me], dst.at[me], loc_sems.at[t])
             for t, (src, dst) in enumerate(pairs)]
    remote = []
    for k, (dev, idx) in enumerate(peers):
        for t, (src, dst) in enumerate(pairs):
            remote.append(pltpu.make_async_remote_copy(
                src_ref=src if kind == "gather" else src.at[idx], dst_ref=dst.at[me],
                send_sem=send_sems.at[2 * k + t], recv_sem=recv_sems.at[2 * k + t],
                device_id=dev, device_id_type=pl.DeviceIdType.MESH))
    return local, remote


def _start_exchange(kind, *refs):
    local, remote = _exchange_copies(kind, *refs)
    for cp in local + remote:
        cp.start()


def _wait_exchange(kind, *refs):
    local, remote = _exchange_copies(kind, *refs)
    for cp in remote:
        cp.wait_recv()
    for cp in remote:
        cp.wait_send()
    for cp in local:
        cp.wait()


def _exchange_out_shapes(kind, a, b):
    if kind == "gather":
        return [jax.ShapeDtypeStruct((N_DEV,) + a.shape, a.dtype), jax.ShapeDtypeStruct((N_DEV,) + b.shape, b.dtype)]
    return [jax.ShapeDtypeStruct(a.shape, a.dtype), jax.ShapeDtypeStruct(b.shape, b.dtype)]


def _gather_two_level(a, b, name):
    def body(a_ref, b_ref, ga_ref, gb_ref, send_sems, recv_sems, loc_sems):
        x, y, c = _mesh_pos()
        slot_of = lambda px, py, pc: 4 * px + 2 * py + pc
        me, sib = slot_of(x, y, c), slot_of(x, y, 1 - c)
        chips = [(1 - x, y), (x, 1 - y), (1 - x, 1 - y)]
        pairs = ((a_ref, ga_ref), (b_ref, gb_ref))

        def copy(k, t, slot, to, src=None):
            dst = pairs[t][1].at[slot]
            return pltpu.make_async_remote_copy(
                src_ref=dst if src is None else src, dst_ref=dst, send_sem=send_sems.at[2 * k + t],
                recv_sem=recv_sems.at[2 * k + t], device_id=to, device_id_type=pl.DeviceIdType.MESH)

        local = [pltpu.make_async_copy(src, dst.at[me], loc_sems.at[t]) for t, (src, dst) in enumerate(pairs)]
        first = []
        for t, (src, _) in enumerate(pairs):
            first.append(copy(0, t, me, (x, y, 1 - c), src))
            first += [copy(1 + j, t, me, (*chip, c), src) for j, chip in enumerate(chips)]
        for cp in local + first:
            cp.start()
        passed = []
        for j, chip in enumerate(chips):
            for t in range(2):
                landed = slot_of(*chip, c)
                copy(1 + j, t, landed, (x, y, c)).wait_recv()
                cp = copy(4 + j, t, landed, (x, y, 1 - c))
                cp.start()
                passed.append(cp)
        for t in range(2):
            copy(0, t, sib, (x, y, c)).wait_recv()
            for j, chip in enumerate(chips):
                copy(4 + j, t, slot_of(*chip, 1 - c), (x, y, c)).wait_recv()
        for cp in first + passed:
            cp.wait_send()
        for cp in local:
            cp.wait()

    return pl.pallas_call(
        body, name=name,
        in_specs=[_ANY, _ANY], out_specs=[_ANY, _ANY],
        out_shape=_exchange_out_shapes("gather", a, b),
        scratch_shapes=_EXCHANGE_SEMS,
    )(a, b)


def _adam_math(w, g, m, v):
    m_new = ADAM_B1 * m + (1.0 - ADAM_B1) * g
    v_new = ADAM_B2 * v + (1.0 - ADAM_B2) * (g * g)
    m_hat = m_new / (1.0 - ADAM_B1 ** ADAM_STEP)
    v_hat = v_new / (1.0 - ADAM_B2 ** ADAM_STEP)
    delta = -ADAM_LR * (m_hat / (jnp.sqrt(v_hat) + ADAM_EPS) + ADAM_WD * w)
    return delta, m_new, v_new


def _sum_adamw(gparts, w, m, v, name):
    L, R, C = w.shape
    tr = min(128, R)

    def body(*refs):
        gp_refs = refs[:L]
        w_ref, m_ref, v_ref, g_ref, d_ref, nm_ref, nv_ref = refs[L:]
        for l in range(L):
            g = gp_refs[l][0].astype(F32)
            for s in range(1, N_DEV):
                g = g + gp_refs[l][s].astype(F32)
            d, mn, vn = _adam_math(w_ref[l], g, m_ref[l], v_ref[l])
            g_ref[l] = g
            d_ref[l] = d
            nm_ref[l] = mn
            nv_ref[l] = vn

    blk = pl.BlockSpec((L, tr, C), lambda r: (0, r, 0))
    return pl.pallas_call(
        body, name=name,
        grid=(R // tr,),
        in_specs=[pl.BlockSpec((N_DEV, tr, C), lambda r: (0, r, 0))] * L + [blk, blk, blk],
        out_specs=[blk, blk, blk, blk],
        out_shape=[jax.ShapeDtypeStruct((L, R, C), F32)] * 4,
        compiler_params=_cp(("parallel",), _VMEM_WIDE),
    )(*gparts, w, m, v)


def _sum_adamw_cols(gparts, w_t, m_t, v_t, name):
    C, L, D = w_t.shape
    td = min(128, D)

    def body(*refs):
        gp_refs = refs[:L]
        w_ref, m_ref, v_ref, g_ref, d_ref, nm_ref, nv_ref = refs[L:]
        starts = list(range(0, C - _ADAM_ROWS + 1, _ADAM_ROWS))
        for c0 in starts:
            rows = slice(c0, C if c0 == starts[-1] else c0 + _ADAM_ROWS)
            for l in range(L):
                g = gp_refs[l][0, rows, :].astype(F32)
                for s in range(1, N_DEV):
                    g = g + gp_refs[l][s, rows, :].astype(F32)
                g_ref[rows, l, :] = g
            d, mn, vn = _adam_math(w_ref[rows], g_ref[rows], m_ref[rows], v_ref[rows])
            d_ref[rows] = d
            nm_ref[rows] = mn
            nv_ref[rows] = vn

    blk = pl.BlockSpec((C, L, td), lambda j: (0, 0, j))
    return pl.pallas_call(
        body, name=name,
        grid=(D // td,),
        in_specs=[pl.BlockSpec((N_DEV, C, td), lambda j: (0, 0, j))] * L + [blk, blk, blk],
        out_specs=[blk, blk, blk, blk],
        out_shape=[jax.ShapeDtypeStruct((C, L, D), F32)] * 4,
        compiler_params=_cp(("parallel",), _VMEM_WIDE),
    )(*gparts, w_t, m_t, v_t)


def _small_update(gpack, wpack, mpack, vpack):
    R = gpack.shape[0]
    VM = pl.BlockSpec(memory_space=pltpu.VMEM)

    def body(g_ref, w_ref, m_ref, v_ref, gs_ref, d_ref, nm_ref, nv_ref, buf, send_sems, recv_sems):
        peers, me = _peers()
        buf[me] = g_ref[...]
        copies = []
        for k, (dev, _) in enumerate(peers):
            cp = pltpu.make_async_remote_copy(
                src_ref=g_ref, dst_ref=buf.at[me], send_sem=send_sems.at[k], recv_sem=recv_sems.at[k],
                device_id=dev, device_id_type=pl.DeviceIdType.MESH)
            cp.start()
            copies.append(cp)
        for cp in copies:
            cp.wait_recv()
        for cp in copies:
            cp.wait_send()
        g = buf[0]
        for s in range(1, N_DEV):
            g = g + buf[s]
        d, mn, vn = _adam_math(w_ref[...], g, m_ref[...], v_ref[...])
        gs_ref[...] = g
        d_ref[...] = d
        nm_ref[...] = mn
        nv_ref[...] = vn

    return pl.pallas_call(
        body, name="small_update",
        in_specs=[VM] * 4, out_specs=[VM] * 4,
        out_shape=[jax.ShapeDtypeStruct((R, 128), F32)] * 4,
        scratch_shapes=[pltpu.VMEM((N_DEV, R, 128), F32), pltpu.SemaphoreType.DMA((7,)), pltpu.SemaphoreType.DMA((7,))],
        compiler_params=_cp(None, _VMEM_MID),
    )(gpack, wpack, mpack, vpack)


_SMALL = ("norm_g", "b_f", "q_norm_g", "k_norm_g", "w_pool", "pool_scale")


def _pack(parts):
    flat = jnp.concatenate([p.reshape(-1) for p in parts])
    n = flat.shape[0]
    rows = -(-n // (8 * 128)) * 8
    return jnp.pad(flat, (0, rows * 128 - n)).reshape(rows, 128)


def _unpack(packed, like):
    flat = packed.reshape(-1)
    out, o = [], 0
    for p in like:
        out.append(flat[o:o + p.size].reshape(p.shape))
        o += p.size
    return out


def kernel(x, norm_g, w_in, b_f, q_norm_g, k_norm_g, w_pool, pool_scale, w_out, loss_target, m_norm_g, m_w_in, m_b_f, m_q_norm_g, m_k_norm_g, m_w_pool, m_pool_scale, m_w_out, v_norm_g, v_w_in, v_b_f, v_q_norm_g, v_k_norm_g, v_w_pool, v_pool_scale, v_w_out):
    L = w_in.shape[0]

    loss_local, dx, grads, received = _train_step(x[0], loss_target[0], norm_g, w_in.astype(BF16), b_f, q_norm_g,
                                                  k_norm_g, w_pool, pool_scale, w_out.astype(BF16))
    loss = lax.psum(loss_local, MESH_AXES)
    g = {k: jnp.stack([grads[l][k] for l in range(L)]) for k in _SMALL}

    cols = lambda a: a.transpose(2, 0, 1)
    g_win, d_win, nm_win, nv_win = [a.transpose(1, 2, 0) for a in _sum_adamw_cols(
        [r[0] for r in received], cols(w_in), cols(m_w_in), cols(v_w_in), "adamw_w_in")]
    g_wout, d_wout, nm_wout, nv_wout = _sum_adamw([r[1] for r in received], w_out, m_w_out, v_w_out, "adamw_w_out")

    ws = dict(norm_g=norm_g, b_f=b_f, q_norm_g=q_norm_g, k_norm_g=k_norm_g, w_pool=w_pool, pool_scale=pool_scale)
    ms = dict(norm_g=m_norm_g, b_f=m_b_f, q_norm_g=m_q_norm_g, k_norm_g=m_k_norm_g, w_pool=m_w_pool, pool_scale=m_pool_scale)
    vs = dict(norm_g=v_norm_g, b_f=v_b_f, q_norm_g=v_q_norm_g, k_norm_g=v_k_norm_g, w_pool=v_w_pool, pool_scale=v_pool_scale)
    like = [ws[k] for k in _SMALL]
    gs_p, d_p, nm_p, nv_p = _small_update(_pack([g[k] for k in _SMALL]), _pack(like),
                                          _pack([ms[k] for k in _SMALL]), _pack([vs[k] for k in _SMALL]))
    gs = dict(zip(_SMALL, _unpack(gs_p, like)))
    ds = dict(zip(_SMALL, _unpack(d_p, like)))
    nms = dict(zip(_SMALL, _unpack(nm_p, like)))
    nvs = dict(zip(_SMALL, _unpack(nv_p, like)))
    gs["w_in"], ds["w_in"], nms["w_in"], nvs["w_in"] = g_win, d_win, nm_win, nv_win
    gs["w_out"], ds["w_out"], nms["w_out"], nvs["w_out"] = g_wout, d_wout, nm_wout, nv_wout

    order = ("norm_g", "w_in", "b_f", "q_norm_g", "k_norm_g", "w_pool", "pool_scale", "w_out")
    return (loss, dx[None], *[gs[k] for k in order], *[ds[k] for k in order],
            *[nms[k] for k in order], *[nvs[k] for k in order])
```

```python
import jax
import jax.numpy as jnp
from jax import lax
from jax.experimental import pallas as pl
from jax.experimental.pallas import tpu as pltpu

F32 = jnp.float32
BF16 = jnp.bfloat16

EPS = 1e-6
NEG = -1e30
HEAD_DIM = 64
FOX_HEADS = 8
FOX_W = 512
POOL_W = 256
SB_W = 256
D_MIX = 1024
N_FF = 8
N_MAIN = 3584
N_FFPAD = 128
OFF_FQ, OFF_FK, OFF_FV, OFF_FG = 0, 512, 1024, 1536
OFF_PX, OFF_PG = 2048, 2304
OFF_SQ, OFF_SK, OFF_SV, OFF_SG = 2560, 2816, 3072, 3328
D_IN = 3592
Q_SCALE = HEAD_DIM ** -0.5

ADAM_LR = 0.001
ADAM_B1 = 0.9
ADAM_B2 = 0.999
ADAM_EPS = 1e-08
ADAM_WD = 0.01
ADAM_STEP = 10

N_DEV = 8
MESH_AXES = ("x", "y", "c")

_T = 256
_TM = 512
_TM_ROWS = 512
_TM_FWD, _TN_FWD = 2048, 512
_TM_DX = 512
_TK_DW = 1024
_ADAM_ROWS = 16
_VMEM_V7X = 64 << 20
_VMEM_BIG = _VMEM_V7X - (8 << 20)
_VMEM_MID = 40 << 20
_VMEM_WIDE = 48 << 20


def _cp(sem=None, vmem=None):
    kw = {}
    if sem is not None:
        kw["dimension_semantics"] = sem
    if vmem is not None:
        kw["vmem_limit_bytes"] = vmem
    return pltpu.CompilerParams(**kw)


def _dot(a, b):
    return jnp.dot(a, b, preferred_element_type=F32)


def _dot_nt(a, b):
    return lax.dot_general(a, b, (((1,), (1,)), ((), ())), preferred_element_type=F32)


def _dot_tn(a, b):
    return lax.dot_general(a, b, (((0,), (0,)), ((), ())), preferred_element_type=F32)


def _mm2(v, m, left=False):
    hi = v.astype(BF16)
    lo = (v - hi.astype(F32)).astype(BF16)
    if left:
        return _dot(m, hi) + _dot(m, lo)
    return _dot(hi, m) + _dot(lo, m)


def _mm3(v, m, left=False):
    a1 = v.astype(BF16)
    r1 = v - a1.astype(F32)
    a2 = r1.astype(BF16)
    a3 = (r1 - a2.astype(F32)).astype(BF16)
    if left:
        return _dot(m, a1) + _dot(m, a2) + _dot(m, a3)
    return _dot(a1, m) + _dot(a2, m) + _dot(a3, m)


def _sigmoid(z):
    return 1.0 / (1.0 + jnp.exp(-z))


def _rms_rows(x):
    return lax.rsqrt(jnp.mean(x * x, axis=-1, keepdims=True) + EPS)


def _inproj_fwd(x, g, wm, wff):
    S, D = x.shape
    tm = min(_TM_FWD, S)
    tn = _TN_FWD
    assert OFF_SQ % tn == 0 and N_MAIN - OFF_SQ == 4 * SB_W
    j_sb = OFF_SQ // tn

    def body(x_ref, g_ref, w_ref, wff_ref, o_ref, off_ref, ht_ref, sb_ref, h_ref):
        j = pl.program_id(1)

        @pl.when(j == 0)
        def _():
            xv = x_ref[...]
            h = (xv * _rms_rows(xv)) * g_ref[...]
            h_ref[...] = h.astype(BF16)
            ht_ref[...] = h.T.astype(BF16)
            off_ref[...] = _dot(h_ref[...], wff_ref[...])

        res = _dot(h_ref[...], w_ref[...])
        o_ref[...] = res

        @pl.when(j >= j_sb)
        def _():
            sb_ref[...] = res.astype(BF16)

    return pl.pallas_call(
        body, name="inproj_fwd",
        grid=(S // tm, N_MAIN // tn),
        in_specs=[pl.BlockSpec((tm, D), lambda i, j: (i, 0)),
                  pl.BlockSpec((1, D), lambda i, j: (0, 0)),
                  pl.BlockSpec((D, tn), lambda i, j: (0, j)),
                  pl.BlockSpec((D, N_FFPAD), lambda i, j: (0, 0))],
        out_specs=[pl.BlockSpec((tm, tn), lambda i, j: (i, j)),
                   pl.BlockSpec((tm, N_FFPAD), lambda i, j: (i, 0)),
                   pl.BlockSpec((D, tm), lambda i, j: (0, i)),
                   pl.BlockSpec((tm, tn), lambda i, j: (i, jnp.maximum(j - j_sb, 0)))],
        out_shape=[jax.ShapeDtypeStruct((S, N_MAIN), F32), jax.ShapeDtypeStruct((S, N_FFPAD), F32),
                   jax.ShapeDtypeStruct((D, S), BF16), jax.ShapeDtypeStruct((S, 4 * SB_W), BF16)],
        scratch_shapes=[pltpu.VMEM((tm, D), BF16)],
        compiler_params=_cp(("parallel", "arbitrary"), _VMEM_BIG),
    )(x, g, wm, wff)


def _head_norm(x, g, bd):
    ss = _mm2(x * x, bd)
    r = lax.rsqrt(ss * (1.0 / HEAD_DIM) + EPS)
    return (x * r) * g


def _fox_prep(proj, pff, bfp, gq, gk, bd, ex, tril):
    S = proj.shape[0]
    T = tril.shape[0]

    def body(q_ref, k_ref, ff_ref, b_ref, gq_ref, gk_ref, bd_ref, ex_ref, tri_ref,
             qs_ref, kn_ref, cc_ref, cqb_ref, carry):
        @pl.when(pl.program_id(0) == 0)
        def _():
            carry[...] = jnp.zeros_like(carry)

        bdv = bd_ref[...]
        qs_ref[...] = (_head_norm(q_ref[...], gq_ref[...], bdv) * Q_SCALE).astype(BF16)
        kn_ref[...] = _head_norm(k_ref[...], gk_ref[...], bdv).astype(BF16)
        u = ff_ref[...] + b_ref[...]
        lf = jnp.minimum(u, 0.0) - jnp.log1p(jnp.exp(-jnp.abs(u)))
        c = _mm3(lf, tri_ref[...], left=True) + carry[0:1, :]
        carry[0:1, :] = c[T - 1:T, :]
        cc_ref[...] = c
        cqb_ref[...] = _mm3(c, ex_ref[...])

    return pl.pallas_call(
        body, name="fox_prep",
        grid=(S // T,),
        in_specs=[pl.BlockSpec((T, FOX_W), lambda i: (i, OFF_FQ // FOX_W)),
                  pl.BlockSpec((T, FOX_W), lambda i: (i, OFF_FK // FOX_W)),
                  pl.BlockSpec((T, N_FFPAD), lambda i: (i, 0)),
                  pl.BlockSpec((1, N_FFPAD), lambda i: (0, 0)),
                  pl.BlockSpec((1, FOX_W), lambda i: (0, 0)),
                  pl.BlockSpec((1, FOX_W), lambda i: (0, 0)),
                  pl.BlockSpec((FOX_W, FOX_W), lambda i: (0, 0)),
                  pl.BlockSpec((N_FFPAD, FOX_W), lambda i: (0, 0)),
                  pl.BlockSpec((T, T), lambda i: (0, 0))],
        out_specs=[pl.BlockSpec((T, FOX_W), lambda i: (i, 0)),
                   pl.BlockSpec((T, FOX_W), lambda i: (i, 0)),
                   pl.BlockSpec((T, N_FFPAD), lambda i: (i, 0)),
                   pl.BlockSpec((T, FOX_W), lambda i: (i, 0))],
        out_shape=[jax.ShapeDtypeStruct((S, FOX_W), BF16), jax.ShapeDtypeStruct((S, FOX_W), BF16),
                   jax.ShapeDtypeStruct((S, N_FFPAD), F32), jax.ShapeDtypeStruct((S, FOX_W), F32)],
        scratch_shapes=[pltpu.VMEM((8, N_FFPAD), F32)],
        compiler_params=_cp(("arbitrary",), _VMEM_MID),
    )(proj, proj, pff, bfp, gq, gk, bd, ex, tril)


def _pair_blk(S, off=0):
    return pl.BlockSpec((S, 128), lambda p: (0, off + p), pipeline_mode=pl.Buffered(1))


def _pair_rows(S):
    return pl.BlockSpec((None, 8, S), lambda p: (p, 0, 0), pipeline_mode=pl.Buffered(1))


def _head_masks(S):
    return lax.broadcasted_iota(jnp.int32, (S, 128), 1) < HEAD_DIM


_EXP_ZERO = 104.0


def _spread_heads(x):
    src = lax.broadcasted_iota(jnp.int32, (128, 128), 0)
    return (_mm3(x, (src == 0).astype(BF16)), _mm3(x, (src == HEAD_DIM).astype(BF16)))


def _score_bounds(q, k):
    same_head = ((lax.broadcasted_iota(jnp.int32, (128, 128), 0) < HEAD_DIM)
                 == (lax.broadcasted_iota(jnp.int32, (128, 128), 1) < HEAD_DIM)).astype(BF16)

    def max_norm2(x):
        xf = x.astype(F32)
        return jnp.max(_mm2(xf * xf, same_head), axis=0, keepdims=True)

    z = jnp.sqrt(max_norm2(q) * max_norm2(k))
    z = jnp.where(z == z, z, jnp.inf)
    return jnp.max(z[:, 0:1]) * 1.001 + 1e-3, jnp.max(z[:, 64:65]) * 1.001 + 1e-3


def _for_tiles_back(i, n, tiles_fn, fours=False):
    if fours:
        def four(t, c):
            tiles_fn([i - 1 - 4 * t, i - 2 - 4 * t, i - 3 - 4 * t, i - 4 - 4 * t])
            return c

        lax.fori_loop(0, lax.shift_right_logical(n, 2), four, 0)
        rest = i - (n & ~3)

        @pl.when((n & 2) != 0)
        def _():
            tiles_fn([rest - 1, rest - 2])
    else:
        def two(t, c):
            tiles_fn([i - 1 - 2 * t, i - 2 - 2 * t])
            return c

        lax.fori_loop(0, lax.shift_right_logical(n, 1), two, 0)

    @pl.when((n & 1) != 0)
    def _():
        tiles_fn([i - n])


def _fox_tiles_back(cr_ref, i, r0, zba, zbb):
    last = cr_ref[:, pl.ds(0, 128)]
    first = cr_ref[:, pl.ds(r0, 128)]
    alive_a = 2.0 * zba + first[0:1, 0:1] - last[2:3, :] > -_EXP_ZERO
    alive_b = 2.0 * zbb + first[1:2, 0:1] - last[3:4, :] > -_EXP_ZERO
    before = lax.broadcasted_iota(jnp.int32, (1, 128), 1) < i
    return jnp.sum((before & (alive_a | alive_b)).astype(jnp.int32))


def _fox_fwd(qs, kn, proj, cqb, crow4, ride=None):
    S = qs.shape[0]
    T = min(_T, S)
    nq = S // T
    n_pairs = FOX_W // 128

    def body(*refs):
        if ride is None:
            q_ref, k_ref, v_ref, cq_ref, cr_ref, o_ref, lse_ref = refs[:7]
            qa, qb, vta, vtb, cka, ckb, ma, mb, acca, accb = refs[7:]
        else:
            q_ref, k_ref, v_ref, cq_ref, cr_ref, wa_ref, wb_ref, o_ref, lse_ref, ga_ref, gb_ref = refs[:11]
            qa, qb, vta, vtb, cka, ckb, ma, mb, acca, accb = refs[11:21]
            xrefs = (wa_ref, wb_ref, ga_ref, gb_ref) + tuple(refs[21:])

            @pl.when(pl.program_id(0) == 0)
            def _():
                _start_exchange("gather", *xrefs)

        lane_s = _head_masks(S)
        q = q_ref[...]
        zq = jnp.zeros_like(q)
        qa[...] = jnp.where(lane_s, q, zq)
        qb[...] = jnp.where(lane_s, zq, q)
        cq = cq_ref[...]
        cka[...], ckb[...] = _spread_heads(cq)
        lse_ref[...] = jnp.zeros((8, S), F32)
        row_t = lax.broadcasted_iota(jnp.int32, (128, T), 0) < HEAD_DIM
        zba, zbb = _score_bounds(q, k_ref[...])

        def prep(c, carry):
            c0 = pl.multiple_of(c * T, T)
            vt = v_ref[pl.ds(c0, T), :].T
            vta[:, pl.ds(c0, T)] = jnp.where(row_t, vt, 1.0).astype(BF16)
            vtb[:, pl.ds(c0, T)] = jnp.where(row_t, 1.0, vt).astype(BF16)
            return carry

        lax.fori_loop(0, nq, prep, 0)
        causal = (lax.broadcasted_iota(jnp.int32, (T, T), 0) <= lax.broadcasted_iota(jnp.int32, (T, T), 1))

        heads = ((qa, vta, cka, ma, acca), (qb, vtb, ckb, mb, accb))

        def kv(js, r0, masked):
            cr = cr_ref[:, pl.ds(r0, T)]
            c0s = [pl.multiple_of(j * T, T) for j in js]
            ks = [k_ref[pl.ds(c0, T), :] for c0 in c0s]
            ss = []
            for h, (qr, _, ckr, _, _) in enumerate(heads):
                qh = qr[pl.ds(r0, T), :]
                row = []
                for k, c0 in zip(ks, c0s):
                    s = _dot_nt(k, qh) - jnp.tile(ckr[pl.ds(c0, T), :], (1, T // 128))
                    row.append(jnp.where(causal, s, NEG) if masked else s)
                ss.append(row)
            ms = []
            for h, (row, (_, _, _, mr, _)) in enumerate(zip(ss, heads)):
                top = row[0]
                for s in row[1:]:
                    top = jnp.maximum(top, s)
                m_old = mr[0:1, :]
                ms.append((m_old, jnp.maximum(m_old, jnp.max(top, axis=0, keepdims=True) + cr[h:h + 1, :])))
            ps = [[jnp.exp(s + (cr[h:h + 1, :] - m_new)).astype(BF16) for s in row]
                  for h, (row, (_, m_new)) in enumerate(zip(ss, ms))]
            pvs = []
            for row, (_, vr, _, _, _) in zip(ps, heads):
                pv = _dot(vr[:, pl.ds(c0s[0], T)], row[0])
                for p, c0 in zip(row[1:], c0s[1:]):
                    pv = pv + _dot(vr[:, pl.ds(c0, T)], p)
                pvs.append(pv)
            for pv, (m_old, m_new), (_, _, _, mr, ar) in zip(pvs, ms, heads):
                ar[...] = jnp.exp(m_old - m_new) * ar[...] + pv
                mr[0:1, :] = m_new

        def qblk(i, carry):
            r0 = pl.multiple_of(i * T, T)
            ma[...] = jnp.full((8, T), NEG, F32)
            mb[...] = jnp.full((8, T), NEG, F32)
            acca[...] = jnp.zeros((128, T), F32)
            accb[...] = jnp.zeros((128, T), F32)
            kv([i], r0, True)
            done = _fox_tiles_back(cr_ref, i, r0, zba, zbb)
            _for_tiles_back(i, done, lambda js: kv(js, r0, False), fours=True)
            aa = acca[...]
            ab = accb[...]
            la = aa[64:65, :]
            lb = ab[0:1, :]
            o_ref[pl.ds(r0, T), :] = jnp.where(row_t, aa / la, ab / lb).T
            lse_ref[0:1, pl.ds(r0, T)] = ma[0:1, :] + jnp.log(la)
            lse_ref[1:2, pl.ds(r0, T)] = mb[0:1, :] + jnp.log(lb)
            lse_ref[2:3, pl.ds(r0, T)] = jnp.broadcast_to(done.astype(F32), (1, T))
            return carry

        lax.fori_loop(0, nq, qblk, 0)
        if ride is not None:
            @pl.when(pl.program_id(0) == n_pairs - 1)
            def _():
                _wait_exchange("gather", *xrefs)

    extra = () if ride is None else tuple(ride)
    return pl.pallas_call(
        body, name="fox_fwd" if ride is None else "fox_fwd_gather",
        grid=(n_pairs,),
        in_specs=[_pair_blk(S), _pair_blk(S), _pair_blk(S, OFF_FV // 128), _pair_blk(S), _pair_rows(S)]
        + [_ANY] * len(extra),
        out_specs=[_pair_blk(S), _pair_rows(S)] + [_ANY] * len(extra),
        out_shape=[jax.ShapeDtypeStruct((S, FOX_W), F32), jax.ShapeDtypeStruct((n_pairs, 8, S), F32)]
        + (_exchange_out_shapes("gather", *extra) if extra else []),
        scratch_shapes=[pltpu.VMEM((S, 128), BF16)] * 2 + [pltpu.VMEM((128, S), BF16)] * 2
        + [pltpu.VMEM((S, 128), F32)] * 2 + [pltpu.VMEM((8, T), F32)] * 2 + [pltpu.VMEM((128, T), F32)] * 2
        + (_EXCHANGE_SEMS if extra else []),
        compiler_params=_cp(("arbitrary",), _VMEM_BIG),
    )(qs, kn, proj, cqb, crow4, *extra)


def _softplus_parts(z):
    e = jnp.exp(-jnp.abs(z))
    return e, jnp.maximum(z, 0.0) + jnp.log(1.0 + e)


def _sb_fwd(psb, triu):
    S = psb.shape[0]
    T = triu.shape[0]
    nq = S // T

    n_pairs = SB_W // 128
    H = 2 * n_pairs

    def body(q_ref, k_ref, v_ref, tri_ref, o_ref, lt_ref, qm, vt, rr, acc):
        lane_s = _head_masks(S)
        zbs = []
        for p in range(n_pairs):
            q = (q_ref[:, 128 * p:128 * (p + 1)].astype(F32) * Q_SCALE).astype(BF16)
            zq = jnp.zeros_like(q)
            qm[2 * p] = jnp.where(lane_s, q, zq)
            qm[2 * p + 1] = jnp.where(lane_s, zq, q)
            zbs += list(_score_bounds(q, k_ref[:, 128 * p:128 * (p + 1)]))
        lt_ref[...] = jnp.zeros((n_pairs, 8, S), F32)
        row_t = lax.broadcasted_iota(jnp.int32, (128, T), 0) < HEAD_DIM

        def prep(c, carry):
            c0 = pl.multiple_of(c * T, T)
            for p in range(n_pairs):
                vt[p, :, pl.ds(c0, T)] = v_ref[pl.ds(c0, T), 128 * p:128 * (p + 1)].astype(F32).T.astype(BF16)
            return carry

        lax.fori_loop(0, nq, prep, 0)
        strict = (lax.broadcasted_iota(jnp.int32, (T, T), 0) < lax.broadcasted_iota(jnp.int32, (T, T), 1))

        def kv(tiles, r0):
            tri = tri_ref[...]
            c0s = [pl.multiple_of(j * T, T) for j, _ in tiles]
            zs = [[_dot_nt(k_ref[pl.ds(c0, T), 128 * (h // 2):128 * (h // 2 + 1)], qm[h, pl.ds(r0, T), :])
                   for c0 in c0s] for h in range(H)]
            lbs = [[jnp.where(strict, -_softplus_parts(z)[1], 0.0) if masked else -_softplus_parts(z)[1]
                    for z, (_, masked) in zip(row, tiles)] for row in zs]
            incs = [[_mm2(lb, tri, left=True) for lb in row] for row in lbs]
            avs = []
            for h in range(H):
                r = rr[h, 0:1, :]
                av = None
                for z, inc, c0, (_, masked) in zip(zs[h], incs[h], c0s, tiles):
                    a = jnp.exp(z + inc + r)
                    if masked:
                        a = jnp.where(strict, a, 0.0)
                    term = _dot(vt[h // 2, :, pl.ds(c0, T)], a.astype(BF16))
                    av = term if av is None else av + term
                    r = r + inc[0:1, :]
                avs.append((av, r))
            for h, (av, r) in enumerate(avs):
                rr[h, 0:1, :] = r
                acc[h] = acc[h] + av

        def qblk(i, carry):
            r0 = pl.multiple_of(i * T, T)
            rr[...] = jnp.zeros((H, 8, T), F32)
            acc[...] = jnp.zeros((H, 128, T), F32)

            @pl.when(i == 0)
            def _():
                kv([(i, True)], r0)

            @pl.when(i > 0)
            def _():
                kv([(i, True), (i - 1, False)], r0)

            def alive():
                m = jnp.max(rr[0, 0:1, :]) + zbs[0]
                for h in range(1, H):
                    m = jnp.maximum(m, jnp.max(rr[h, 0:1, :]) + zbs[h])
                return m > -_EXP_ZERO

            def cond(st):
                return (st[0] < i) & st[1]

            def step(st):
                kv([(i - 1 - st[0], False)], r0)
                return st[0] + 1, alive()

            done, _ = lax.while_loop(cond, step, (jnp.minimum(i, 1), alive()))
            for p in range(n_pairs):
                o_ref[pl.ds(r0, T), 128 * p:128 * (p + 1)] = jnp.where(row_t, acc[2 * p], acc[2 * p + 1]).T
                lt_ref[p, 0:1, pl.ds(r0, T)] = rr[2 * p, 0:1, :]
                lt_ref[p, 1:2, pl.ds(r0, T)] = rr[2 * p + 1, 0:1, :]
                lt_ref[p, 2:3, pl.ds(r0, T)] = jnp.broadcast_to(done.astype(F32), (1, T))
            return carry

        lax.fori_loop(0, nq, qblk, 0)

    wide = lambda off: pl.BlockSpec((S, SB_W), lambda g: (0, off), pipeline_mode=pl.Buffered(1))
    return pl.pallas_call(
        body, name="sb_fwd",
        grid=(1,),
        in_specs=[wide(0), wide(1), wide(2), pl.BlockSpec((T, T), lambda g: (0, 0))],
        out_specs=[wide(0), pl.BlockSpec((n_pairs, 8, S), lambda g: (0, 0, 0), pipeline_mode=pl.Buffered(1))],
        out_shape=[jax.ShapeDtypeStruct((S, SB_W), F32), jax.ShapeDtypeStruct((n_pairs, 8, S), F32)],
        scratch_shapes=[pltpu.VMEM((H, S, 128), BF16), pltpu.VMEM((n_pairs, 128, S), BF16),
                        pltpu.VMEM((H, 8, T), F32), pltpu.VMEM((H, 128, T), F32)],
        compiler_params=_cp(("arbitrary",), _VMEM_BIG),
    )(psb, psb, psb, triu)


def _pool_window_lanes(shape):
    lane = lax.broadcasted_iota(jnp.int32, shape, 1)
    return jnp.where(lane < 64, 2, jnp.where(lane < 128, 4, jnp.where(lane < 192, 8, 16)))


def _pool_fwd(proj):
    S = proj.shape[0]

    def body(x_ref, o_ref):
        x = x_ref[...]
        t = lax.broadcasted_iota(jnp.int32, x.shape, 0)
        lane = lax.broadcasted_iota(jnp.int32, x.shape, 1)

        def back(a, k):
            return jnp.where(t >= k, pltpu.roll(a, k, 0), 0.0)

        s1 = x + back(x, 1)
        s2 = s1 + back(s1, 2)
        s4 = s2 + back(s2, 4)
        s8 = s4 + back(s4, 8)
        win = jnp.where(lane < 64, s1, jnp.where(lane < 128, s2, jnp.where(lane < 192, s4, s8)))
        cnt = jnp.minimum(t + 1, _pool_window_lanes(x.shape)).astype(F32)
        o_ref[...] = win / cnt - x

    return pl.pallas_call(
        body, name="pool_fwd",
        grid=(1,),
        in_specs=[pl.BlockSpec((S, POOL_W), lambda i: (0, OFF_PX // POOL_W))],
        out_specs=pl.BlockSpec((S, POOL_W), lambda i: (0, 0)),
        out_shape=jax.ShapeDtypeStruct((S, POOL_W), F32),
        compiler_params=_cp(("arbitrary",), _VMEM_BIG),
    )(proj)


def _silu(g):
    return g * _sigmoid(g)


def _mix_out(fo, so, pooled, proj, wbd, scale, wout, x):
    S, D = x.shape
    tm = min(_TM_ROWS, S)

    def body(fo_ref, fg_ref, so_ref, sg_ref, pl_ref, pg_ref, wbd_ref, sc_ref, w_ref, x_ref, y_ref, mxt_ref, mx_ref):
        parts = ((0, fo_ref[...] * _silu(fg_ref[...])),
                 (FOX_W, (_dot(pl_ref[...].astype(BF16), wbd_ref[...]) * sc_ref[...]) * _silu(pg_ref[...])),
                 (FOX_W + POOL_W, so_ref[...] * _silu(sg_ref[...])))
        for off, part in parts:
            w = part.shape[1]
            mx_ref[:, off:off + w] = part.astype(BF16)
            mxt_ref[off:off + w, :] = part.T.astype(BF16)
        y_ref[...] = x_ref[...] + _dot(mx_ref[...], w_ref[...])

    return pl.pallas_call(
        body, name="mix_out",
        grid=(S // tm,),
        in_specs=[pl.BlockSpec((tm, FOX_W), lambda i: (i, 0)),
                  pl.BlockSpec((tm, FOX_W), lambda i: (i, OFF_FG // FOX_W)),
                  pl.BlockSpec((tm, SB_W), lambda i: (i, 0)),
                  pl.BlockSpec((tm, SB_W), lambda i: (i, OFF_SG // SB_W)),
                  pl.BlockSpec((tm, POOL_W), lambda i: (i, 0)),
                  pl.BlockSpec((tm, POOL_W), lambda i: (i, OFF_PG // POOL_W)),
                  pl.BlockSpec((POOL_W, POOL_W), lambda i: (0, 0)),
                  pl.BlockSpec((1, POOL_W), lambda i: (0, 0)),
                  pl.BlockSpec((D_MIX, D), lambda i: (0, 0)),
                  pl.BlockSpec((tm, D), lambda i: (i, 0))],
        out_specs=[pl.BlockSpec((tm, D), lambda i: (i, 0)), pl.BlockSpec((D_MIX, tm), lambda i: (0, i))],
        out_shape=[jax.ShapeDtypeStruct((S, D), F32), jax.ShapeDtypeStruct((D_MIX, S), BF16)],
        scratch_shapes=[pltpu.VMEM((tm, D_MIX), BF16)],
        compiler_params=_cp(("parallel",), _VMEM_MID),
    )(fo, proj, so, proj, pooled, proj, wbd, scale, wout, x)


def _loss_head(y, target):
    S, D = y.shape
    tm = min(_TM, S)

    def body(y_ref, t_ref, dy_ref, ls_ref):
        @pl.when(pl.program_id(0) == 0)
        def _():
            ls_ref[...] = jnp.zeros_like(ls_ref)

        e = y_ref[...] - t_ref[...]
        dy_ref[...] = e * (1.0 / D)
        ls_ref[...] = ls_ref[...] + jnp.sum(e * e) * (0.5 / D)

    dy, ls = pl.pallas_call(
        body, name="loss_head",
        grid=(S // tm,),
        in_specs=[pl.BlockSpec((tm, D), lambda i: (i, 0)), pl.BlockSpec((tm, D), lambda i: (i, 0))],
        out_specs=[pl.BlockSpec((tm, D), lambda i: (i, 0)), pl.BlockSpec((8, 128), lambda i: (0, 0))],
        out_shape=[jax.ShapeDtypeStruct((S, D), F32), jax.ShapeDtypeStruct((8, 128), F32)],
        compiler_params=_cp(("arbitrary",), _VMEM_MID),
    )(y, target)
    return dy, ls[0, 0]


def _dsilu(g):
    s = _sigmoid(g)
    return s * (1.0 + g * (1.0 - s))


def _gate_bwd(dy, wout, fo, so, pooled, proj, wbd, scale):
    S, D = dy.shape
    tm = min(_TM_ROWS, S)

    def body(dy_ref, w_ref, fo_ref, fg_ref, so_ref, sg_ref, pl_ref, pg_ref, wbd_ref, sc_ref,
             dfo_ref, dfg_ref, dso_ref, dsg_ref, dpg_ref, dpl_ref, dsc_ref, dwbd_ref):
        @pl.when(pl.program_id(0) == 0)
        def _():
            dsc_ref[...] = jnp.zeros_like(dsc_ref)
            dwbd_ref[...] = jnp.zeros_like(dwbd_ref)

        dm = _dot_nt(dy_ref[...].astype(BF16), w_ref[...])
        dmf = dm[:, 0:FOX_W]
        dmp = dm[:, FOX_W:FOX_W + POOL_W]
        dms = dm[:, FOX_W + POOL_W:D_MIX]
        fg = fg_ref[...]
        dfo_ref[...] = dmf * _silu(fg)
        dfg_ref[...] = (dmf * fo_ref[...] * _dsilu(fg)).astype(BF16)
        sg = sg_ref[...]
        dso_ref[...] = (dms * _silu(sg)).astype(BF16)
        dsg_ref[...] = (dms * so_ref[...] * _dsilu(sg)).astype(BF16)
        pg = pg_ref[...]
        plb = pl_ref[...].astype(BF16)
        yw = _dot(plb, wbd_ref[...])
        sc = sc_ref[...]
        dpg_ref[...] = (dmp * (yw * sc) * _dsilu(pg)).astype(BF16)
        dys = dmp * _silu(pg)
        dsc_ref[...] = dsc_ref[...] + jnp.sum(dys * yw, axis=0, keepdims=True)
        dyw = (dys * sc).astype(BF16)
        dpl_ref[...] = _dot_nt(dyw, wbd_ref[...])
        dwbd_ref[...] = dwbd_ref[...] + _dot_tn(plb, dyw)

    return pl.pallas_call(
        body, name="gate_bwd",
        grid=(S // tm,),
        in_specs=[pl.BlockSpec((tm, D), lambda i: (i, 0)),
                  pl.BlockSpec((D_MIX, D), lambda i: (0, 0)),
                  pl.BlockSpec((tm, FOX_W), lambda i: (i, 0)),
                  pl.BlockSpec((tm, FOX_W), lambda i: (i, OFF_FG // FOX_W)),
                  pl.BlockSpec((tm, SB_W), lambda i: (i, 0)),
                  pl.BlockSpec((tm, SB_W), lambda i: (i, OFF_SG // SB_W)),
                  pl.BlockSpec((tm, POOL_W), lambda i: (i, 0)),
                  pl.BlockSpec((tm, POOL_W), lambda i: (i, OFF_PG // POOL_W)),
                  pl.BlockSpec((POOL_W, POOL_W), lambda i: (0, 0)),
                  pl.BlockSpec((1, POOL_W), lambda i: (0, 0))],
        out_specs=[pl.BlockSpec((tm, FOX_W), lambda i: (i, 0)),
                   pl.BlockSpec((tm, FOX_W), lambda i: (i, 0)),
                   pl.BlockSpec((tm, SB_W), lambda i: (i, 0)),
                   pl.BlockSpec((tm, SB_W), lambda i: (i, 0)),
                   pl.BlockSpec((tm, POOL_W), lambda i: (i, 0)),
                   pl.BlockSpec((tm, POOL_W), lambda i: (i, 0)),
                   pl.BlockSpec((1, POOL_W), lambda i: (0, 0)),
                   pl.BlockSpec((POOL_W, POOL_W), lambda i: (0, 0))],
        out_shape=[jax.ShapeDtypeStruct((S, FOX_W), F32), jax.ShapeDtypeStruct((S, FOX_W), BF16),
                   jax.ShapeDtypeStruct((S, SB_W), BF16), jax.ShapeDtypeStruct((S, SB_W), BF16),
                   jax.ShapeDtypeStruct((S, POOL_W), BF16), jax.ShapeDtypeStruct((S, POOL_W), F32),
                   jax.ShapeDtypeStruct((1, POOL_W), F32), jax.ShapeDtypeStruct((POOL_W, POOL_W), F32)],
        compiler_params=_cp(("arbitrary",), _VMEM_MID),
    )(dy, wout, fo, proj, so, proj, pooled, proj, wbd, scale)


def _matmul_acc(at, b, name):
    M, S = at.shape
    N = b.shape[1]
    tk = min(_TK_DW, S)
    tn = min(512, N)
    nk = S // tk

    def body(a_ref, b_ref, o_ref, acc):
        k = pl.program_id(1)

        @pl.when(k == 0)
        def _():
            acc[...] = jnp.zeros_like(acc)

        acc[...] = acc[...] + _dot(a_ref[...], b_ref[...].astype(BF16))

        @pl.when(k == nk - 1)
        def _():
            o_ref[...] = acc[...].astype(BF16)

    return pl.pallas_call(
        body, name=name,
        grid=(N // tn, nk),
        in_specs=[pl.BlockSpec((M, tk), lambda j, k: (0, k)), pl.BlockSpec((tk, tn), lambda j, k: (k, j))],
        out_specs=pl.BlockSpec((M, tn), lambda j, k: (0, j)),
        out_shape=jax.ShapeDtypeStruct((M, N), BF16),
        scratch_shapes=[pltpu.VMEM((M, tn), F32)],
        compiler_params=_cp(("parallel", "arbitrary"), _VMEM_MID),
    )(at, b)


def _pool_bwd(dpooled):
    S = dpooled.shape[0]

    def body(d_ref, o_ref):
        d = d_ref[...]
        t = lax.broadcasted_iota(jnp.int32, d.shape, 0)
        lane = lax.broadcasted_iota(jnp.int32, d.shape, 1)
        cnt = jnp.minimum(t + 1, _pool_window_lanes(d.shape)).astype(F32)
        u = d / cnt

        def fwd(a, k):
            return jnp.where(t < S - k, pltpu.roll(a, S - k, 0), 0.0)

        s1 = u + fwd(u, 1)
        s2 = s1 + fwd(s1, 2)
        s4 = s2 + fwd(s2, 4)
        s8 = s4 + fwd(s4, 8)
        win = jnp.where(lane < 64, s1, jnp.where(lane < 128, s2, jnp.where(lane < 192, s4, s8)))
        o_ref[...] = (win - d).astype(BF16)

    return pl.pallas_call(
        body, name="pool_bwd",
        grid=(1,),
        in_specs=[pl.BlockSpec((S, POOL_W), lambda i: (0, 0))],
        out_specs=pl.BlockSpec((S, POOL_W), lambda i: (0, 0)),
        out_shape=jax.ShapeDtypeStruct((S, POOL_W), BF16),
        compiler_params=_cp(("arbitrary",), _VMEM_BIG),
    )(dpooled)


def _fox_bwd(qs, kn, proj, dfo, fo, lse, cqb, crow4, ride=None):
    S = qs.shape[0]
    T = min(_T, S)
    nq = S // T
    n_pairs = FOX_W // 128

    def body(*refs):
        if ride is None:
            q_ref, k_ref, v_ref, do_ref, o_ref, lse_ref, cq_ref, cr_ref = refs[:8]
            dq_ref, dk_ref, dv_ref, dck_ref, dcq_ref = refs[8:13]
            scr = refs[13:]
        else:
            q_ref, k_ref, v_ref, do_ref, o_ref, lse_ref, cq_ref, cr_ref, pa_ref, pb_ref = refs[:10]
            dq_ref, dk_ref, dv_ref, dck_ref, dcq_ref, ra_ref, rb_ref = refs[10:17]
            scr = refs[17:32]
            xrefs = (pa_ref, pb_ref, ra_ref, rb_ref) + tuple(refs[32:])

            @pl.when(pl.program_id(0) == 0)
            def _():
                _start_exchange("scatter", *xrefs)

        qa, qb, kta, ktb, vb, doa, dob, cka, ckb, dcka, dckb, dva, dqt, dcqa, dcqb = scr
        lane_s = _head_masks(S)
        q = q_ref[...]
        zq = jnp.zeros_like(q)
        qa[...] = jnp.where(lane_s, q, zq)
        qb[...] = jnp.where(lane_s, zq, q)
        vb[...] = v_ref[...].astype(BF16)
        do = do_ref[...].astype(BF16)
        doa[...] = jnp.where(lane_s, do, zq)
        dob[...] = jnp.where(lane_s, zq, do)
        cq = cq_ref[...]
        cka[...], ckb[...] = _spread_heads(cq)
        zs = jnp.zeros((S, 128), F32)
        dk_ref[...] = zs
        dva[...] = zs
        dcka[...] = zs
        dckb[...] = zs
        dcq_ref[...] = jnp.zeros((8, S), F32)
        row_t = lax.broadcasted_iota(jnp.int32, (128, T), 0) < HEAD_DIM

        def prep(c, carry):
            c0 = pl.multiple_of(c * T, T)
            kt = k_ref[pl.ds(c0, T), :].astype(F32).T
            kta[:, pl.ds(c0, T)] = jnp.where(row_t, kt, 0.0).astype(BF16)
            ktb[:, pl.ds(c0, T)] = jnp.where(row_t, 0.0, kt).astype(BF16)
            return carry

        lax.fori_loop(0, nq, prep, 0)
        causal = (lax.broadcasted_iota(jnp.int32, (T, T), 0) <= lax.broadcasted_iota(jnp.int32, (T, T), 1))

        heads = ((qa, kta, doa, cka, dcka, dcqa), (qb, ktb, dob, ckb, dckb, dcqb))

        def kv(js, r0, lss, dls, masked):
            cr = cr_ref[:, pl.ds(r0, T)]
            c0s = [pl.multiple_of(j * T, T) for j in js]
            ks = [k_ref[pl.ds(c0, T), :] for c0 in c0s]
            vs = [vb[pl.ds(c0, T), :] for c0 in c0s]
            qhs = [hd[0][pl.ds(r0, T), :] for hd in heads]
            dohs = [hd[2][pl.ds(r0, T), :] for hd in heads]
            ss = []
            for h, hd in enumerate(heads):
                row = []
                for k, c0 in zip(ks, c0s):
                    s = _dot_nt(k, qhs[h]) - jnp.tile(hd[3][pl.ds(c0, T), :], (1, T // 128))
                    row.append(jnp.where(causal, s, NEG) if masked else s)
                ss.append(row)
            ps = [[jnp.exp(s + (cr[h:h + 1, :] - lss[h])) for s in row] for h, row in enumerate(ss)]
            dps = [[_dot_nt(v, dohs[h]) for v in vs] for h in range(2)]
            dss = [[p * (dp - dls[h]) for p, dp in zip(ps[h], dps[h])] for h in range(2)]
            pbs = [[p.astype(BF16) for p in row] for row in ps]
            dsbs = [[ds.astype(BF16) for ds in row] for row in dss]
            for t, c0 in enumerate(c0s):
                dva[pl.ds(c0, T), :] = dva[pl.ds(c0, T), :] + (_dot(pbs[0][t], dohs[0]) + _dot(pbs[1][t], dohs[1]))
                dk_ref[pl.ds(c0, T), :] = dk_ref[pl.ds(c0, T), :] + (_dot(dsbs[0][t], qhs[0]) + _dot(dsbs[1][t], qhs[1]))
            dq = None
            for h, hd in enumerate(heads):
                for t, c0 in enumerate(c0s):
                    term = _dot(hd[1][:, pl.ds(c0, T)], dsbs[h][t])
                    dq = term if dq is None else dq + term
            dqt[...] = dqt[...] + dq
            for h, hd in enumerate(heads):
                col = jnp.sum(dss[h][0], axis=0, keepdims=True)
                for ds in dss[h][1:]:
                    col = col + jnp.sum(ds, axis=0, keepdims=True)
                hd[5][0:1, :] = hd[5][0:1, :] + col
                for ds, c0 in zip(dss[h], c0s):
                    fold = ds[:, 0:128]
                    for u in range(1, T // 128):
                        fold = fold + ds[:, 128 * u:128 * (u + 1)]
                    hd[4][pl.ds(c0, T), :] = hd[4][pl.ds(c0, T), :] - fold

        def qblk(i, carry):
            r0 = pl.multiple_of(i * T, T)
            dt = (do_ref[pl.ds(r0, T), :] * o_ref[pl.ds(r0, T), :]).T
            dla = jnp.sum(jnp.where(row_t, dt, 0.0), axis=0, keepdims=True)
            dlb = jnp.sum(jnp.where(row_t, 0.0, dt), axis=0, keepdims=True)
            ls = lse_ref[:, pl.ds(r0, T)]
            lss = (ls[0:1, :], ls[1:2, :])
            back = jnp.max(ls[2:3, :]).astype(jnp.int32)
            dqt[...] = jnp.zeros((128, T), F32)
            dcqa[...] = jnp.zeros((8, T), F32)
            dcqb[...] = jnp.zeros((8, T), F32)
            kv([i], r0, lss, (dla, dlb), True)
            _for_tiles_back(i, back, lambda js: kv(js, r0, lss, (dla, dlb), False), fours=True)
            dq_ref[pl.ds(r0, T), :] = dqt[...].T
            dcq_ref[0:1, pl.ds(r0, T)] = dcqa[0:1, :]
            dcq_ref[1:2, pl.ds(r0, T)] = dcqb[0:1, :]
            return carry

        lax.fori_loop(0, nq, qblk, 0)
        dv_ref[...] = dva[...].astype(BF16)
        @pl.when(pl.program_id(0) == 0)
        def _():
            dck_ref[...] = jnp.zeros((S, N_FFPAD), F32)

        head_lane = lax.broadcasted_iota(jnp.int32, (S, N_FFPAD), 1) - 2 * pl.program_id(0)
        dck_ref[...] = jnp.where(head_lane == 0, jnp.sum(dcka[...], axis=1, keepdims=True),
                                 jnp.where(head_lane == 1, jnp.sum(dckb[...], axis=1, keepdims=True), dck_ref[...]))
        if ride is not None:
            @pl.when(pl.program_id(0) == n_pairs - 1)
            def _():
                _wait_exchange("scatter", *xrefs)

    extra = () if ride is None else tuple(ride)
    return pl.pallas_call(
        body, name="fox_bwd" if ride is None else "fox_bwd_exchange",
        grid=(n_pairs,),
        in_specs=[_pair_blk(S), _pair_blk(S), _pair_blk(S, OFF_FV // 128), _pair_blk(S), _pair_blk(S),
                  _pair_rows(S), _pair_blk(S), _pair_rows(S)] + [_ANY] * len(extra),
        out_specs=[_pair_blk(S), _pair_blk(S), _pair_blk(S), pl.BlockSpec((S, N_FFPAD), lambda p: (0, 0)),
                   _pair_rows(S)] + [_ANY] * len(extra),
        out_shape=[jax.ShapeDtypeStruct((S, FOX_W), F32), jax.ShapeDtypeStruct((S, FOX_W), F32),
                   jax.ShapeDtypeStruct((S, FOX_W), BF16), jax.ShapeDtypeStruct((S, N_FFPAD), F32),
                   jax.ShapeDtypeStruct((n_pairs, 8, S), F32)]
        + (_exchange_out_shapes("scatter", *extra) if extra else []),
        scratch_shapes=[pltpu.VMEM((S, 128), BF16)] * 2 + [pltpu.VMEM((128, S), BF16)] * 2
        + [pltpu.VMEM((S, 128), BF16)] * 3 + [pltpu.VMEM((S, 128), F32)] * 5
        + [pltpu.VMEM((128, T), F32)] + [pltpu.VMEM((8, T), F32)] * 2
        + (_EXCHANGE_SEMS if extra else []),
        compiler_params=_cp(("arbitrary",), _VMEM_BIG),
    )(qs, kn, proj, dfo, fo, lse, cqb, crow4, *extra)


def _sb_bwd(psb, dso, ltot, tril):
    S = psb.shape[0]
    T = tril.shape[0]
    nq = S // T
    n_pairs = SB_W // 128
    H = 2 * n_pairs

    def body(q_ref, k_ref, v_ref, do_ref, lt_ref, tri_ref, dq_ref, dk_ref, dv_ref,
             qm, kt, dka, dva, dqt, rr, gg):
        lane_s = _head_masks(S)
        for p in range(n_pairs):
            q = (q_ref[:, 128 * p:128 * (p + 1)].astype(F32) * Q_SCALE).astype(BF16)
            zq = jnp.zeros_like(q)
            qm[2 * p] = jnp.where(lane_s, q, zq)
            qm[2 * p + 1] = jnp.where(lane_s, zq, q)
        dka[...] = jnp.zeros((n_pairs, S, 128), F32)
        dva[...] = jnp.zeros((n_pairs, S, 128), F32)
        row_t = lax.broadcasted_iota(jnp.int32, (128, T), 0) < HEAD_DIM
        lane_t = lax.broadcasted_iota(jnp.int32, (T, 128), 1) < HEAD_DIM

        def prep(c, carry):
            c0 = pl.multiple_of(c * T, T)
            for p in range(n_pairs):
                kt[p, :, pl.ds(c0, T)] = k_ref[pl.ds(c0, T), 128 * p:128 * (p + 1)].astype(F32).T.astype(BF16)
            return carry

        lax.fori_loop(0, nq, prep, 0)
        strict = (lax.broadcasted_iota(jnp.int32, (T, T), 0) < lax.broadcasted_iota(jnp.int32, (T, T), 1))

        def own(x, h, mask):
            z = jnp.zeros_like(x)
            return jnp.where(mask, x, z) if h % 2 == 0 else jnp.where(mask, z, x)

        def pair(ref, p, c0):
            return ref[pl.ds(c0, T), 128 * p:128 * (p + 1)]

        def kv(tiles, r0, lts):
            tri = tri_ref[...]
            c0s = [pl.multiple_of(j * T, T) for j, _ in tiles]
            qhs = [qm[h, pl.ds(r0, T), :] for h in range(H)]
            dohs = [own(pair(do_ref, h // 2, r0), h, lane_t) for h in range(H)]
            zs = [[_dot_nt(pair(k_ref, h // 2, c0), qhs[h]) for c0 in c0s] for h in range(H)]
            das = [[_dot_nt(pair(v_ref, h // 2, c0), dohs[h]) for c0 in c0s] for h in range(H)]
            es, lbs = [], []
            for row in zs:
                erow, lrow = [], []
                for z, (_, masked) in zip(row, tiles):
                    e, sp = _softplus_parts(z)
                    erow.append(e)
                    lrow.append(jnp.where(strict, -sp, 0.0) if masked else -sp)
                es.append(erow)
                lbs.append(lrow)
            pres = [[_mm2(lb, tri, left=True) for lb in row] for row in lbs]
            aas, r_ends = [], []
            for h in range(H):
                r = rr[h, 0:1, :]
                arow = []
                for z, lb, pre, (_, masked) in zip(zs[h], lbs[h], pres[h], tiles):
                    a = jnp.exp(z + lb + ((lts[h] - r) - pre))
                    arow.append(jnp.where(strict, a, 0.0) if masked else a)
                    r = r + pre[T - 1:T, :]
                aas.append(arow)
                r_ends.append(r)
            gs = [[a * da for a, da in zip(arow, drow)] for arow, drow in zip(aas, das)]
            gpres = [[_mm2(g, tri, left=True) for g in row] for row in gs]
            dzbs, g_ends = [], []
            for h in range(H):
                gc = gg[h, 0:1, :]
                drow = []
                for z, e, g, gpre, (_, masked) in zip(zs[h], es[h], gs[h], gpres[h], tiles):
                    inv = 1.0 / (1.0 + e)
                    pos = z >= 0.0
                    sig = jnp.where(pos, 1.0, e) * inv
                    oms = jnp.where(pos, e, 1.0) * inv
                    dz = g * oms - sig * (gc + (gpre - g))
                    if masked:
                        dz = jnp.where(strict, dz, 0.0)
                    drow.append(dz.astype(BF16))
                    gc = gc + gpre[T - 1:T, :]
                dzbs.append(drow)
                g_ends.append(gc)
            for p in range(n_pairs):
                a, b = 2 * p, 2 * p + 1
                dq = None
                for h in (a, b):
                    for t, c0 in enumerate(c0s):
                        term = _dot(own(kt[p, :, pl.ds(c0, T)], h, row_t), dzbs[h][t])
                        dq = term if dq is None else dq + term
                dqt[p] = dqt[p] + dq
                for t, c0 in enumerate(c0s):
                    dka[p, pl.ds(c0, T), :] = dka[p, pl.ds(c0, T), :] + (_dot(dzbs[a][t], qhs[a]) + _dot(dzbs[b][t], qhs[b]))
                    dva[p, pl.ds(c0, T), :] = dva[p, pl.ds(c0, T), :] + (_dot(aas[a][t].astype(BF16), dohs[a])
                                                                      + _dot(aas[b][t].astype(BF16), dohs[b]))
            for h in range(H):
                rr[h, 0:1, :] = r_ends[h]
                gg[h, 0:1, :] = g_ends[h]

        def qblk(i, carry):
            r0 = pl.multiple_of(i * T, T)
            lts = []
            for p in range(n_pairs):
                lt = lt_ref[p, :, pl.ds(r0, T)]
                lts += [lt[0:1, :], lt[1:2, :]]
            back = jnp.max(lt_ref[0, 2:3, pl.ds(r0, T)]).astype(jnp.int32)
            dqt[...] = jnp.zeros((n_pairs, 128, T), F32)
            rr[...] = jnp.zeros((H, 8, T), F32)
            gg[...] = jnp.zeros((H, 8, T), F32)

            def inner(j, c):
                kv([(j, False)], r0, lts)
                return c

            @pl.when(back == 0)
            def _():
                kv([(i, True)], r0, lts)

            @pl.when(back > 0)
            def _():
                lax.fori_loop(i - back, i - 1, inner, 0)
                kv([(i - 1, False), (i, True)], r0, lts)

            for p in range(n_pairs):
                dq_ref[pl.ds(r0, T), 128 * p:128 * (p + 1)] = (dqt[p] * Q_SCALE).T.astype(BF16)
            return carry

        lax.fori_loop(0, nq, qblk, 0)
        for p in range(n_pairs):
            dk_ref[:, 128 * p:128 * (p + 1)] = dka[p].astype(BF16)
            dv_ref[:, 128 * p:128 * (p + 1)] = dva[p].astype(BF16)

    wide = lambda off: pl.BlockSpec((S, SB_W), lambda g: (0, off), pipeline_mode=pl.Buffered(1))
    return pl.pallas_call(
        body, name="sb_bwd",
        grid=(1,),
        in_specs=[wide(0), wide(1), wide(2), wide(0),
                  pl.BlockSpec((n_pairs, 8, S), lambda g: (0, 0, 0), pipeline_mode=pl.Buffered(1)),
                  pl.BlockSpec((T, T), lambda g: (0, 0))],
        out_specs=[wide(0), wide(0), wide(0)],
        out_shape=[jax.ShapeDtypeStruct((S, SB_W), BF16)] * 3,
        scratch_shapes=[pltpu.VMEM((H, S, 128), BF16), pltpu.VMEM((n_pairs, 128, S), BF16),
                        pltpu.VMEM((n_pairs, S, 128), F32), pltpu.VMEM((n_pairs, S, 128), F32),
                        pltpu.VMEM((n_pairs, 128, T), F32), pltpu.VMEM((H, 8, T), F32), pltpu.VMEM((H, 8, T), F32)],
        compiler_params=_cp(("arbitrary",), _VMEM_BIG),
    )(psb, psb, psb, dso, ltot, tril)


def _head_norm_bwd(x, g, dy, bd):
    ss = _mm2(x * x, bd)
    r = lax.rsqrt(ss * (1.0 / HEAD_DIM) + EPS)
    xr = x * r
    gdy = g * dy
    m = _mm2(xr * gdy, bd) * (1.0 / HEAD_DIM)
    return r * (gdy - xr * m), dy * xr


def _qk_bwd(dqs, dkn, proj, pff, bfp, gq, gk, bd, dck, dcq, triu):
    S = proj.shape[0]
    T = triu.shape[0]
    n = S // T
    rev = lambda col: (lambda i: (n - 1 - i, col))

    def body(dq_ref, dk_ref, q_ref, k_ref, ff_ref, b_ref, gq_ref, gk_ref, bd_ref, dck_ref, dcq_ref, tri_ref,
             dfq_ref, dfk_ref, dff_ref, dgq_ref, dgk_ref, dbf_ref, carry):
        @pl.when(pl.program_id(0) == 0)
        def _():
            carry[...] = jnp.zeros_like(carry)
            dgq_ref[...] = jnp.zeros_like(dgq_ref)
            dgk_ref[...] = jnp.zeros_like(dgk_ref)
            dbf_ref[...] = jnp.zeros_like(dbf_ref)

        bdv = bd_ref[...]
        dxq, gq_rows = _head_norm_bwd(q_ref[...], gq_ref[...], dq_ref[...] * Q_SCALE, bdv)
        dfq_ref[...] = dxq.astype(BF16)
        dgq_ref[...] = dgq_ref[...] + jnp.sum(gq_rows, axis=0, keepdims=True)
        dxk, gk_rows = _head_norm_bwd(k_ref[...], gk_ref[...], dk_ref[...], bdv)
        dfk_ref[...] = dxk.astype(BF16)
        dgk_ref[...] = dgk_ref[...] + jnp.sum(gk_rows, axis=0, keepdims=True)
        dlf = _mm3(dck_ref[...] + dcq_ref[...], tri_ref[...], left=True) + carry[0:1, :]
        carry[0:1, :] = dlf[0:1, :]
        u = ff_ref[...] + b_ref[...]
        lane = lax.broadcasted_iota(jnp.int32, u.shape, 1)
        dff = jnp.where(lane < N_FF, dlf * _sigmoid(-u), 0.0)
        dff_ref[...] = dff.astype(BF16)
        dbf_ref[...] = dbf_ref[...] + jnp.sum(dff, axis=0, keepdims=True)

    return pl.pallas_call(
        body, name="qk_bwd",
        grid=(n,),
        in_specs=[pl.BlockSpec((T, FOX_W), rev(0)), pl.BlockSpec((T, FOX_W), rev(0)),
                  pl.BlockSpec((T, FOX_W), rev(OFF_FQ // FOX_W)), pl.BlockSpec((T, FOX_W), rev(OFF_FK // FOX_W)),
                  pl.BlockSpec((T, N_FFPAD), rev(0)),
                  pl.BlockSpec((1, N_FFPAD), lambda i: (0, 0)),
                  pl.BlockSpec((1, FOX_W), lambda i: (0, 0)), pl.BlockSpec((1, FOX_W), lambda i: (0, 0)),
                  pl.BlockSpec((FOX_W, FOX_W), lambda i: (0, 0)),
                  pl.BlockSpec((T, N_FFPAD), rev(0)), pl.BlockSpec((T, N_FFPAD), rev(0)),
                  pl.BlockSpec((T, T), lambda i: (0, 0))],
        out_specs=[pl.BlockSpec((T, FOX_W), rev(0)), pl.BlockSpec((T, FOX_W), rev(0)),
                   pl.BlockSpec((T, N_FFPAD), rev(0)),
                   pl.BlockSpec((1, FOX_W), lambda i: (0, 0)), pl.BlockSpec((1, FOX_W), lambda i: (0, 0)),
                   pl.BlockSpec((1, N_FFPAD), lambda i: (0, 0))],
        out_shape=[jax.ShapeDtypeStruct((S, FOX_W), BF16), jax.ShapeDtypeStruct((S, FOX_W), BF16),
                   jax.ShapeDtypeStruct((S, N_FFPAD), BF16),
                   jax.ShapeDtypeStruct((1, FOX_W), F32), jax.ShapeDtypeStruct((1, FOX_W), F32),
                   jax.ShapeDtypeStruct((1, N_FFPAD), F32)],
        scratch_shapes=[pltpu.VMEM((8, N_FFPAD), F32)],
        compiler_params=_cp(("arbitrary",), _VMEM_MID),
    )(dqs, dkn, proj, proj, pff, bfp, gq, gk, bd, dck, dcq, triu)


def _dproj_layout(pieces):
    offs, o = [], 0
    for p in pieces:
        offs.append(o)
        o += p.shape[1]
    assert o == N_MAIN
    return offs


def _inproj_bwd_dx(pieces, dff, wm, wff, x, g, dy, ride=None):
    S, D = x.shape
    tm = min(_TM_DX, S)
    steps = S // tm
    offs = _dproj_layout(pieces)
    n = len(pieces)

    def body(*refs):
        p_refs = refs[:n]
        if ride is None:
            dff_ref, w_ref, wff_ref, x_ref, g_ref, dy_ref, dx_ref, dg_ref = refs[n:]
        else:
            dff_ref, w_ref, wff_ref, x_ref, g_ref, dy_ref, pa_ref, pb_ref = refs[n:n + 8]
            dx_ref, dg_ref, ra_ref, rb_ref = refs[n + 8:n + 12]
            xrefs = (pa_ref, pb_ref, ra_ref, rb_ref) + tuple(refs[n + 12:])

        @pl.when(pl.program_id(0) == 0)
        def _():
            dg_ref[...] = jnp.zeros_like(dg_ref)
            if ride is not None:
                _start_exchange("scatter", *xrefs)

        dh = _dot_nt(dff_ref[...], wff_ref[...])
        for p_ref, off in zip(p_refs, offs):
            dh = dh + _dot_nt(p_ref[...], w_ref[:, off:off + p_ref.shape[1]])
        xv = x_ref[...]
        r = _rms_rows(xv)
        xr = xv * r
        dg_ref[...] = dg_ref[...] + jnp.sum(dh * xr, axis=0, keepdims=True)
        gdh = g_ref[...] * dh
        m = jnp.mean(gdh * xr, axis=-1, keepdims=True)
        dx_ref[...] = dy_ref[...] + r * (gdh - xr * m)
        if ride is not None:
            @pl.when(pl.program_id(0) == steps - 1)
            def _():
                _wait_exchange("scatter", *xrefs)

    extra = () if ride is None else tuple(ride)
    return pl.pallas_call(
        body, name="inproj_bwd_dx" if ride is None else "inproj_bwd_dx_exchange",
        grid=(steps,),
        in_specs=[pl.BlockSpec((tm, p.shape[1]), lambda i: (i, 0)) for p in pieces]
        + [pl.BlockSpec((tm, N_FFPAD), lambda i: (i, 0)),
                  pl.BlockSpec((D, N_MAIN), lambda i: (0, 0)),
                  pl.BlockSpec((D, N_FFPAD), lambda i: (0, 0)),
                  pl.BlockSpec((tm, D), lambda i: (i, 0)),
                  pl.BlockSpec((1, D), lambda i: (0, 0)),
                  pl.BlockSpec((tm, D), lambda i: (i, 0))] + [_ANY] * len(extra),
        out_specs=[pl.BlockSpec((tm, D), lambda i: (i, 0)), pl.BlockSpec((1, D), lambda i: (0, 0))] + [_ANY] * len(extra),
        out_shape=[jax.ShapeDtypeStruct((S, D), F32), jax.ShapeDtypeStruct((1, D), F32)]
        + (_exchange_out_shapes("scatter", *extra) if extra else []),
        scratch_shapes=_EXCHANGE_SEMS if extra else [],
        compiler_params=_cp(("arbitrary",), _VMEM_WIDE),
    )(*pieces, dff, wm, wff, x, g, dy, *extra)


def _inproj_bwd_dw(ht, pieces, dff):
    D, S = ht.shape
    tk = min(_TK_DW, S)
    nk = S // tk
    offs = _dproj_layout(pieces)
    n = len(pieces)

    def body(*refs):
        ht_ref, p_refs, dff_ref = refs[0], refs[1:1 + n], refs[1 + n]
        dw_ref, dwff_ref, acc, accff = refs[2 + n:]
        k = pl.program_id(0)

        @pl.when(k == 0)
        def _():
            acc[...] = jnp.zeros_like(acc)
            accff[...] = jnp.zeros_like(accff)

        hb = ht_ref[...]
        for p_ref, off in zip(p_refs, offs):
            w = p_ref.shape[1]
            acc[:, off:off + w] = acc[:, off:off + w] + _dot(hb, p_ref[...])
        accff[...] = accff[...] + _dot(hb, dff_ref[...])

        @pl.when(k == nk - 1)
        def _():
            for c0 in range(0, N_MAIN, FOX_W):
                dw_ref[c0:c0 + FOX_W, :] = acc[:, c0:c0 + FOX_W].T.astype(BF16)
            dwff_ref[...] = accff[...].T.astype(BF16)

    return pl.pallas_call(
        body, name="inproj_bwd_dw",
        grid=(nk,),
        in_specs=[pl.BlockSpec((D, tk), lambda k: (0, k))]
        + [pl.BlockSpec((tk, p.shape[1]), lambda k: (k, 0)) for p in pieces]
        + [pl.BlockSpec((tk, N_FFPAD), lambda k: (k, 0))],
        out_specs=[pl.BlockSpec((N_MAIN, D), lambda k: (0, 0), pipeline_mode=pl.Buffered(1)),
                   pl.BlockSpec((N_FFPAD, D), lambda k: (0, 0), pipeline_mode=pl.Buffered(1))],
        out_shape=[jax.ShapeDtypeStruct((N_MAIN, D), BF16), jax.ShapeDtypeStruct((N_FFPAD, D), BF16)],
        scratch_shapes=[pltpu.VMEM((D, N_MAIN), F32), pltpu.VMEM((D, N_FFPAD), F32)],
        compiler_params=_cp(("arbitrary",), _VMEM_BIG),
    )(ht, *pieces, dff)


def _constants(T, rows):
    tril = jnp.tril(jnp.ones((T, T), F32)).astype(BF16)
    tril_rows = jnp.tril(jnp.ones((rows, rows), F32)).astype(BF16)
    hid = jnp.arange(FOX_W) // HEAD_DIM
    bd = (hid[:, None] == hid[None, :]).astype(BF16)
    ex = (jnp.arange(N_FFPAD)[:, None] == hid[None, :]).astype(BF16)
    return tril, tril.T, bd, ex, tril_rows, tril_rows.T


def _crow4(ccol, T):
    S = ccol.shape[0]
    c = ccol[:, :FOX_HEADS].T
    last = jnp.pad(c[:, T - 1::T], ((0, 0), (0, S - S // T)))
    rows = jnp.concatenate([c.reshape(FOX_HEADS // 2, 2, S), last.reshape(FOX_HEADS // 2, 2, S)], axis=1)
    return jnp.pad(rows, ((0, 0), (0, 4), (0, 0)))


def _layer_fwd(x, lw, consts, ride=None):
    tril, triu, bd, ex, tril_rows, _ = consts
    proj, pff, ht, psb = _inproj_fwd(x, lw["g"], lw["wm"], lw["wff"])
    qs, kn, ccol, cqb = _fox_prep(proj, pff, lw["bfp"], lw["gq"], lw["gk"], bd, ex, tril_rows)
    crow4 = _crow4(ccol, tril.shape[0])
    fo, lse, *gathered = _fox_fwd(qs, kn, proj, cqb, crow4, ride)
    so, ltot = _sb_fwd(psb, triu)
    pooled = _pool_fwd(proj)
    y, mixedt = _mix_out(fo, so, pooled, proj, lw["wbd"], lw["scale"], lw["wout"], x)
    return y, (x, proj, pff, ht, psb, qs, kn, cqb, crow4, fo, lse, so, ltot, pooled, mixedt), gathered


def _layer_bwd(dy, saved, lw, consts, ride=None, exchange_own=False):
    tril, _, bd, _, _, triu_rows = consts
    x, proj, pff, ht, psb, qs, kn, cqb, crow4, fo, lse, so, ltot, pooled, mixedt = saved
    S = x.shape[0]
    dfo, dfg, dso, dsg, dpg, dpooled, dscale, dwbd = _gate_bwd(dy, lw["wout"], fo, so, pooled, proj, lw["wbd"], lw["scale"])
    dwout = _matmul_acc(mixedt, dy, "dw_out")
    dpx = _pool_bwd(dpooled)
    dqs, dkn, dfv, dck, dcq4, *received = _fox_bwd(qs, kn, proj, dfo, fo, lse, cqb, crow4, ride)
    dsq, dsk, dsv = _sb_bwd(psb, dso, ltot, tril)
    dcq = jnp.pad(dcq4[:, :2, :].reshape(FOX_HEADS, S).T, ((0, 0), (0, N_FFPAD - FOX_HEADS)))
    dfq, dfk, dff, dgq, dgk, dbf = _qk_bwd(dqs, dkn, proj, pff, lw["bfp"], lw["gq"], lw["gk"], bd, dck, dcq, triu_rows)
    pieces = [dfq, dfk, dfv, dfg, dpx, dpg, dsq, dsk, dsv, dsg]
    dwm_t, dwff_t = _inproj_bwd_dw(ht, pieces, dff)
    dwin_t = jnp.concatenate([dwm_t[:OFF_PX], dwff_t[:N_FF], dwm_t[OFF_PX:]], axis=0)
    own = _grad_parts({"w_in_t": dwin_t, "w_out": dwout}) if exchange_own else None
    dx, dng, *received_own = _inproj_bwd_dx(pieces, dff, lw["wm"], lw["wff"], x, lw["g"], dy, own)
    grads = {
        "norm_g": dng[0],
        "w_in_t": dwin_t,
        "b_f": dbf[0, :N_FF],
        "q_norm_g": dgq[0].reshape(FOX_HEADS, HEAD_DIM).sum(0),
        "k_norm_g": dgk[0].reshape(FOX_HEADS, HEAD_DIM).sum(0),
        "w_pool": jnp.stack([dwbd[64 * i:64 * i + 64, 64 * i:64 * i + 64] for i in range(4)]),
        "pool_scale": dscale[0],
        "w_out": dwout,
    }
    return dx, grads, received, received_own


def _layer_weights(l, norm_g, gin, b_f, q_norm_g, k_norm_g, w_pool, pool_scale, gout):
    D = gin.shape[1]
    w = gin.transpose(1, 0, 2).reshape(D, D_IN)
    wm = jnp.concatenate([w[:, :2048], w[:, 2048 + N_FF:]], axis=1)
    wff = jnp.pad(w[:, 2048:2048 + N_FF], ((0, 0), (0, N_FFPAD - N_FF)))
    grp = jnp.arange(POOL_W) // 64
    wbd = jnp.where(grp[:, None] == grp[None, :], jnp.tile(w_pool[l].transpose(1, 0, 2).reshape(64, POOL_W), (4, 1)), 0.0)
    return {
        "g": norm_g[l].reshape(1, D),
        "wm": wm, "wff": wff,
        "bfp": jnp.pad(b_f[l], (0, N_FFPAD - N_FF)).reshape(1, N_FFPAD),
        "gq": jnp.tile(q_norm_g[l], FOX_HEADS).reshape(1, FOX_W),
        "gk": jnp.tile(k_norm_g[l], FOX_HEADS).reshape(1, FOX_W),
        "wbd": wbd.astype(BF16),
        "scale": pool_scale[l].reshape(1, POOL_W),
        "wout": gout.reshape(D_MIX, D),
    }


def _grad_parts(g):
    dwin_t, dwout = g["w_in_t"].astype(BF16), g["w_out"].astype(BF16)
    return (dwin_t.reshape(N_DEV, D_IN // N_DEV, dwin_t.shape[1]),
            dwout.reshape(N_DEV, D_MIX // N_DEV, dwout.shape[1]))


def _train_step(x, target, norm_g, win_sh, b_f, q_norm_g, k_norm_g, w_pool, pool_scale, wout_sh):
    L = norm_g.shape[0]
    consts = _constants(min(_T, x.shape[0]), min(_TM_ROWS, x.shape[0]))
    gathered = _gather_two_level(win_sh[0], wout_sh[0], "gather_weights")
    lws, saved = [], []
    h = x
    for l in range(L):
        lws.append(_layer_weights(l, norm_g, gathered[0], b_f, q_norm_g, k_norm_g, w_pool, pool_scale, gathered[1]))
        ride = (win_sh[l + 1], wout_sh[l + 1]) if l + 1 < L else None
        h, sv, gathered = _layer_fwd(h, lws[l], consts, ride)
        saved.append(sv)
    dy, loss = _loss_head(h, target)
    grads, received = [None] * L, [None] * L
    ride = None
    for l in reversed(range(L)):
        dy, grads[l], got, got_own = _layer_bwd(dy, saved[l], lws[l], consts, ride, exchange_own=(l == 0))
        if ride is not None:
            received[l + 1] = got
        if l == 0:
            received[0] = got_own
        else:
            ride = _grad_parts(grads[l])
    return loss, dy, grads, received


def _mesh_pos():
    return lax.axis_index("x"), lax.axis_index("y"), lax.axis_index("c")


_FLIPS = [(0, 0, 1), (1, 0, 0), (0, 1, 0), (1, 1, 0), (1, 0, 1), (0, 1, 1), (1, 1, 1)]


def _peers():
    x, y, c = _mesh_pos()
    out = []
    for fx, fy, fc in _FLIPS:
        px = 1 - x if fx else x
        py = 1 - y if fy else y
        pc = 1 - c if fc else c
        out.append(((px, py, pc), 4 * px + 2 * py + pc))
    return out, 4 * x + 2 * y + c


_EXCHANGE_SEMS = [pltpu.SemaphoreType.DMA((14,)), pltpu.SemaphoreType.DMA((14,)), pltpu.SemaphoreType.DMA((2,))]
_ANY = pl.BlockSpec(memory_space=pl.ANY)


def _exchange_copies(kind, a_ref, b_ref, oa_ref, ob_ref, send_sems, recv_sems, loc_sems):
    peers, me = _peers()
    pairs = ((a_ref, oa_ref), (b_ref, ob_ref))
    local = [pltpu.make_async_copy(src if kind == "gather" else src.at[me], dst.at[me], loc_sems.at[t])
             for t, (src, dst) in enumerate(pairs)]
    remote = []
    for k, (dev, idx) in enumerate(peers):
        for t, (src, dst) in enumerate(pairs):
            remote.append(pltpu.make_async_remote_copy(
                src_ref=src if kind == "gather" else src.at[idx], dst_ref=dst.at[me],
                send_sem=send_sems.at[2 * k + t], recv_sem=recv_sems.at[2 * k + t],
                device_id=dev, device_id_type=pl.DeviceIdType.MESH))
    return local, remote


def _start_exchange(kind, *refs):
    local, remote = _exchange_copies(kind, *refs)
    for cp in local + remote:
        cp.start()


def _wait_exchange(kind, *refs):
    local, remote = _exchange_copies(kind, *refs)
    for cp in remote:
        cp.wait_recv()
    for cp in remote:
        cp.wait_send()
    for cp in local:
        cp.wait()


def _exchange_out_shapes(kind, a, b):
    if kind == "gather":
        return [jax.ShapeDtypeStruct((N_DEV,) + a.shape, a.dtype), jax.ShapeDtypeStruct((N_DEV,) + b.shape, b.dtype)]
    return [jax.ShapeDtypeStruct(a.shape, a.dtype), jax.ShapeDtypeStruct(b.shape, b.dtype)]


def _gather_two_level(a, b, name):
    def body(a_ref, b_ref, ga_ref, gb_ref, send_sems, recv_sems, loc_sems):
        x, y, c = _mesh_pos()
        slot_of = lambda px, py, pc: 4 * px + 2 * py + pc
        me, sib = slot_of(x, y, c), slot_of(x, y, 1 - c)
        chips = [(1 - x, y), (x, 1 - y), (1 - x, 1 - y)]
        pairs = ((a_ref, ga_ref), (b_ref, gb_ref))

        def copy(k, t, slot, to, src=None):
            dst = pairs[t][1].at[slot]
            return pltpu.make_async_remote_copy(
                src_ref=dst if src is None else src, dst_ref=dst, send_sem=send_sems.at[2 * k + t],
                recv_sem=recv_sems.at[2 * k + t], device_id=to, device_id_type=pl.DeviceIdType.MESH)

        local = [pltpu.make_async_copy(src, dst.at[me], loc_sems.at[t]) for t, (src, dst) in enumerate(pairs)]
        first = []
        for t, (src, _) in enumerate(pairs):
            first.append(copy(0, t, me, (x, y, 1 - c), src))
            first += [copy(1 + j, t, me, (*chip, c), src) for j, chip in enumerate(chips)]
        for cp in local + first:
            cp.start()
        passed = []
        for j, chip in enumerate(chips):
            for t in range(2):
                landed = slot_of(*chip, c)
                copy(1 + j, t, landed, (x, y, c)).wait_recv()
                cp = copy(4 + j, t, landed, (x, y, 1 - c))
                cp.start()
                passed.append(cp)
        for t in range(2):
            copy(0, t, sib, (x, y, c)).wait_recv()
            for j, chip in enumerate(chips):
                copy(4 + j, t, slot_of(*chip, 1 - c), (x, y, c)).wait_recv()
        for cp in first + passed:
            cp.wait_send()
        for cp in local:
            cp.wait()

    return pl.pallas_call(
        body, name=name,
        in_specs=[_ANY, _ANY], out_specs=[_ANY, _ANY],
        out_shape=_exchange_out_shapes("gather", a, b),
        scratch_shapes=_EXCHANGE_SEMS,
    )(a, b)


def _adam_math(w, g, m, v):
    m_new = ADAM_B1 * m + (1.0 - ADAM_B1) * g
    v_new = ADAM_B2 * v + (1.0 - ADAM_B2) * (g * g)
    m_hat = m_new / (1.0 - ADAM_B1 ** ADAM_STEP)
    v_hat = v_new / (1.0 - ADAM_B2 ** ADAM_STEP)
    delta = -ADAM_LR * (m_hat / (jnp.sqrt(v_hat) + ADAM_EPS) + ADAM_WD * w)
    return delta, m_new, v_new


def _sum_adamw(gparts, w, m, v, name):
    L, R, C = w.shape
    tr = min(128, R)

    def body(*refs):
        gp_refs = refs[:L]
        w_ref, m_ref, v_ref, g_ref, d_ref, nm_ref, nv_ref = refs[L:]
        for l in range(L):
            g = gp_refs[l][0].astype(F32)
            for s in range(1, N_DEV):
                g = g + gp_refs[l][s].astype(F32)
            d, mn, vn = _adam_math(w_ref[l], g, m_ref[l], v_ref[l])
            g_ref[l] = g
            d_ref[l] = d
            nm_ref[l] = mn
            nv_ref[l] = vn

    blk = pl.BlockSpec((L, tr, C), lambda r: (0, r, 0))
    return pl.pallas_call(
        body, name=name,
        grid=(R // tr,),
        in_specs=[pl.BlockSpec((N_DEV, tr, C), lambda r: (0, r, 0))] * L + [blk, blk, blk],
        out_specs=[blk, blk, blk, blk],
        out_shape=[jax.ShapeDtypeStruct((L, R, C), F32)] * 4,
        compiler_params=_cp(("parallel",), _VMEM_WIDE),
    )(*gparts, w, m, v)


def _sum_adamw_cols(gparts, w_t, m_t, v_t, name):
    C, L, D = w_t.shape
    td = min(128, D)

    def body(*refs):
        gp_refs = refs[:L]
        w_ref, m_ref, v_ref, g_ref, d_ref, nm_ref, nv_ref = refs[L:]
        starts = list(range(0, C - _ADAM_ROWS + 1, _ADAM_ROWS))
        for c0 in starts:
            rows = slice(c0, C if c0 == starts[-1] else c0 + _ADAM_ROWS)
            for l in range(L):
                g = gp_refs[l][0, rows, :].astype(F32)
                for s in range(1, N_DEV):
                    g = g + gp_refs[l][s, rows, :].astype(F32)
                g_ref[rows, l, :] = g
            d, mn, vn = _adam_math(w_ref[rows], g_ref[rows], m_ref[rows], v_ref[rows])
            d_ref[rows] = d
            nm_ref[rows] = mn
            nv_ref[rows] = vn

    blk = pl.BlockSpec((C, L, td), lambda j: (0, 0, j))
    return pl.pallas_call(
        body, name=name,
        grid=(D // td,),
        in_specs=[pl.BlockSpec((N_DEV, C, td), lambda j: (0, 0, j))] * L + [blk, blk, blk],
        out_specs=[blk, blk, blk, blk],
        out_shape=[jax.ShapeDtypeStruct((C, L, D), F32)] * 4,
        compiler_params=_cp(("parallel",), _VMEM_WIDE),
    )(*gparts, w_t, m_t, v_t)


def _small_update(gpack, wpack, mpack, vpack):
    R = gpack.shape[0]
    VM = pl.BlockSpec(memory_space=pltpu.VMEM)

    def body(g_ref, w_ref, m_ref, v_ref, gs_ref, d_ref, nm_ref, nv_ref, buf, send_sems, recv_sems):
        peers, me = _peers()
        buf[me] = g_ref[...]
        copies = []
        for k, (dev, _) in enumerate(peers):
            cp = pltpu.make_async_remote_copy(
                src_ref=g_ref, dst_ref=buf.at[me], send_sem=send_sems.at[k], recv_sem=recv_sems.at[k],
                device_id=dev, device_id_type=pl.DeviceIdType.MESH)
            cp.start()
            copies.append(cp)
        for cp in copies:
            cp.wait_recv()
        for cp in copies:
            cp.wait_send()
        g = buf[0]
        for s in range(1, N_DEV):
            g = g + buf[s]
        d, mn, vn = _adam_math(w_ref[...], g, m_ref[...], v_ref[...])
        gs_ref[...] = g
        d_ref[...] = d
        nm_ref[...] = mn
        nv_ref[...] = vn

    return pl.pallas_call(
        body, name="small_update",
        in_specs=[VM] * 4, out_specs=[VM] * 4,
        out_shape=[jax.ShapeDtypeStruct((R, 128), F32)] * 4,
        scratch_shapes=[pltpu.VMEM((N_DEV, R, 128), F32), pltpu.SemaphoreType.DMA((7,)), pltpu.SemaphoreType.DMA((7,))],
        compiler_params=_cp(None, _VMEM_MID),
    )(gpack, wpack, mpack, vpack)


_SMALL = ("norm_g", "b_f", "q_norm_g", "k_norm_g", "w_pool", "pool_scale")


def _pack(parts):
    flat = jnp.concatenate([p.reshape(-1) for p in parts])
    n = flat.shape[0]
    rows = -(-n // (8 * 128)) * 8
    return jnp.pad(flat, (0, rows * 128 - n)).reshape(rows, 128)


def _unpack(packed, like):
    flat = packed.reshape(-1)
    out, o = [], 0
    for p in like:
        out.append(flat[o:o + p.size].reshape(p.shape))
        o += p.size
    return out


def kernel(x, norm_g, w_in, b_f, q_norm_g, k_norm_g, w_pool, pool_scale, w_out, loss_target, m_norm_g, m_w_in, m_b_f, m_q_norm_g, m_k_norm_g, m_w_pool, m_pool_scale, m_w_out, v_norm_g, v_w_in, v_b_f, v_q_norm_g, v_k_norm_g, v_w_pool, v_pool_scale, v_w_out):
    L = w_in.shape[0]

    loss_local, dx, grads, received = _train_step(x[0], loss_target[0], norm_g, w_in.astype(BF16), b_f, q_norm_g,
                                                  k_norm_g, w_pool, pool_scale, w_out.astype(BF16))
    loss = lax.psum(loss_local, MESH_AXES)
    g = {k: jnp.stack([grads[l][k] for l in range(L)]) for k in _SMALL}

    cols = lambda a: a.transpose(2, 0, 1)
    g_win, d_win, nm_win, nv_win = [a.transpose(1, 2, 0) for a in _sum_adamw_cols(
        [r[0] for r in received], cols(w_in), cols(m_w_in), cols(v_w_in), "adamw_w_in")]
    g_wout, d_wout, nm_wout, nv_wout = _sum_adamw([r[1] for r in received], w_out, m_w_out, v_w_out, "adamw_w_out")

    ws = dict(norm_g=norm_g, b_f=b_f, q_norm_g=q_norm_g, k_norm_g=k_norm_g, w_pool=w_pool, pool_scale=pool_scale)
    ms = dict(norm_g=m_norm_g, b_f=m_b_f, q_norm_g=m_q_norm_g, k_norm_g=m_k_norm_g, w_pool=m_w_pool, pool_scale=m_pool_scale)
    vs = dict(norm_g=v_norm_g, b_f=v_b_f, q_norm_g=v_q_norm_g, k_norm_g=v_k_norm_g, w_pool=v_w_pool, pool_scale=v_pool_scale)
    like = [ws[k] for k in _SMALL]
    gs_p, d_p, nm_p, nv_p = _small_update(_pack([g[k] for k in _SMALL]), _pack(like),
                                          _pack([ms[k] for k in _SMALL]), _pack([vs[k] for k in _SMALL]))
    gs = dict(zip(_SMALL, _unpack(gs_p, like)))
    ds = dict(zip(_SMALL, _unpack(d_p, like)))
    nms = dict(zip(_SMALL, _unpack(nm_p, like)))
    nvs = dict(zip(_SMALL, _unpack(nv_p, like)))
    gs["w_in"], ds["w_in"], nms["w_in"], nvs["w_in"] = g_win, d_win, nm_win, nv_win
    gs["w_out"], ds["w_out"], nms["w_out"], nvs["w_out"] = g_wout, d_wout, nm_wout, nv_wout

    order = ("norm_g", "w_in", "b_f", "q_norm_g", "k_norm_g", "w_pool", "pool_scale", "w_out")
    return (loss, dx[None], *[gs[k] for k in order], *[ds[k] for k in order],
            *[nms[k] for k in order], *[nvs[k] for k in order])
```

```python
import jax
import jax.numpy as jnp
from jax import lax
from jax.experimental import pallas as pl
from jax.experimental.pallas import tpu as pltpu

F32 = jnp.float32
BF16 = jnp.bfloat16

EPS = 1e-6
NEG = -1e30
HEAD_DIM = 64
FOX_HEADS = 8
FOX_W = 512
POOL_W = 256
SB_W = 256
D_MIX = 1024
N_FF = 8
N_MAIN = 3584
N_FFPAD = 128
OFF_FQ, OFF_FK, OFF_FV, OFF_FG = 0, 512, 1024, 1536
OFF_PX, OFF_PG = 2048, 2304
OFF_SQ, OFF_SK, OFF_SV, OFF_SG = 2560, 2816, 3072, 3328
D_IN = 3592
Q_SCALE = HEAD_DIM ** -0.5

ADAM_LR = 0.001
ADAM_B1 = 0.9
ADAM_B2 = 0.999
ADAM_EPS = 1e-08
ADAM_WD = 0.01
ADAM_STEP = 10

N_DEV = 8
MESH_AXES = ("x", "y", "c")

_T = 256
_TM = 512
_TM_ROWS = 512
_TM_FWD, _TN_FWD = 2048, 512
_TM_DX = 512
_TK_DW = 1024
_ADAM_ROWS = 16
_VMEM_V7X = 64 << 20
_VMEM_BIG = _VMEM_V7X - (8 << 20)
_VMEM_MID = 40 << 20
_VMEM_WIDE = 48 << 20


def _cp(sem=None, vmem=None):
    kw = {}
    if sem is not None:
        kw["dimension_semantics"] = sem
    if vmem is not None:
        kw["vmem_limit_bytes"] = vmem
    return pltpu.CompilerParams(**kw)


def _dot(a, b):
    return jnp.dot(a, b, preferred_element_type=F32)


def _dot_nt(a, b):
    return lax.dot_general(a, b, (((1,), (1,)), ((), ())), preferred_element_type=F32)


def _dot_tn(a, b):
    return lax.dot_general(a, b, (((0,), (0,)), ((), ())), preferred_element_type=F32)


def _mm2(v, m, left=False):
    hi = v.astype(BF16)
    lo = (v - hi.astype(F32)).astype(BF16)
    if left:
        return _dot(m, hi) + _dot(m, lo)
    return _dot(hi, m) + _dot(lo, m)


def _mm3(v, m, left=False):
    a1 = v.astype(BF16)
    r1 = v - a1.astype(F32)
    a2 = r1.astype(BF16)
    a3 = (r1 - a2.astype(F32)).astype(BF16)
    if left:
        return _dot(m, a1) + _dot(m, a2) + _dot(m, a3)
    return _dot(a1, m) + _dot(a2, m) + _dot(a3, m)


def _sigmoid(z):
    return 1.0 / (1.0 + jnp.exp(-z))


def _rms_rows(x):
    return lax.rsqrt(jnp.mean(x * x, axis=-1, keepdims=True) + EPS)


def _inproj_fwd(x, g, wm, wff):
    S, D = x.shape
    tm = min(_TM_FWD, S)
    tn = _TN_FWD
    assert OFF_SQ % tn == 0 and N_MAIN - OFF_SQ == 4 * SB_W
    j_sb = OFF_SQ // tn

    def body(x_ref, g_ref, w_ref, wff_ref, o_ref, off_ref, ht_ref, sb_ref, h_ref):
        j = pl.program_id(1)

        @pl.when(j == 0)
        def _():
            xv = x_ref[...]
            h = (xv * _rms_rows(xv)) * g_ref[...]
            h_ref[...] = h.astype(BF16)
            ht_ref[...] = h.T.astype(BF16)
            off_ref[...] = _dot(h_ref[...], wff_ref[...])

        res = _dot(h_ref[...], w_ref[...])
        o_ref[...] = res

        @pl.when(j >= j_sb)
        def _():
            sb_ref[...] = res.astype(BF16)

    return pl.pallas_call(
        body, name="inproj_fwd",
        grid=(S // tm, N_MAIN // tn),
        in_specs=[pl.BlockSpec((tm, D), lambda i, j: (i, 0)),
                  pl.BlockSpec((1, D), lambda i, j: (0, 0)),
                  pl.BlockSpec((D, tn), lambda i, j: (0, j)),
                  pl.BlockSpec((D, N_FFPAD), lambda i, j: (0, 0))],
        out_specs=[pl.BlockSpec((tm, tn), lambda i, j: (i, j)),
                   pl.BlockSpec((tm, N_FFPAD), lambda i, j: (i, 0)),
                   pl.BlockSpec((D, tm), lambda i, j: (0, i)),
                   pl.BlockSpec((tm, tn), lambda i, j: (i, jnp.maximum(j - j_sb, 0)))],
        out_shape=[jax.ShapeDtypeStruct((S, N_MAIN), F32), jax.ShapeDtypeStruct((S, N_FFPAD), F32),
                   jax.ShapeDtypeStruct((D, S), BF16), jax.ShapeDtypeStruct((S, 4 * SB_W), BF16)],
        scratch_shapes=[pltpu.VMEM((tm, D), BF16)],
        compiler_params=_cp(("parallel", "arbitrary"), _VMEM_BIG),
    )(x, g, wm, wff)


def _head_norm(x, g, bd):
    ss = _mm2(x * x, bd)
    r = lax.rsqrt(ss * (1.0 / HEAD_DIM) + EPS)
    return (x * r) * g


def _fox_prep(proj, pff, bfp, gq, gk, bd, ex, tril):
    S = proj.shape[0]
    T = tril.shape[0]

    def body(q_ref, k_ref, ff_ref, b_ref, gq_ref, gk_ref, bd_ref, ex_ref, tri_ref,
             qs_ref, kn_ref, cc_ref, cqb_ref, carry):
        @pl.when(pl.program_id(0) == 0)
        def _():
            carry[...] = jnp.zeros_like(carry)

        bdv = bd_ref[...]
        qs_ref[...] = (_head_norm(q_ref[...], gq_ref[...], bdv) * Q_SCALE).astype(BF16)
        kn_ref[...] = _head_norm(k_ref[...], gk_ref[...], bdv).astype(BF16)
        u = ff_ref[...] + b_ref[...]
        lf = jnp.minimum(u, 0.0) - jnp.log1p(jnp.exp(-jnp.abs(u)))
        c = _mm3(lf, tri_ref[...], left=True) + carry[0:1, :]
        carry[0:1, :] = c[T - 1:T, :]
        cc_ref[...] = c
        cqb_ref[...] = _mm3(c, ex_ref[...])

    return pl.pallas_call(
        body, name="fox_prep",
        grid=(S // T,),
        in_specs=[pl.BlockSpec((T, FOX_W), lambda i: (i, OFF_FQ // FOX_W)),
                  pl.BlockSpec((T, FOX_W), lambda i: (i, OFF_FK // FOX_W)),
                  pl.BlockSpec((T, N_FFPAD), lambda i: (i, 0)),
                  pl.BlockSpec((1, N_FFPAD), lambda i: (0, 0)),
                  pl.BlockSpec((1, FOX_W), lambda i: (0, 0)),
                  pl.BlockSpec((1, FOX_W), lambda i: (0, 0)),
                  pl.BlockSpec((FOX_W, FOX_W), lambda i: (0, 0)),
                  pl.BlockSpec((N_FFPAD, FOX_W), lambda i: (0, 0)),
                  pl.BlockSpec((T, T), lambda i: (0, 0))],
        out_specs=[pl.BlockSpec((T, FOX_W), lambda i: (i, 0)),
                   pl.BlockSpec((T, FOX_W), lambda i: (i, 0)),
                   pl.BlockSpec((T, N_FFPAD), lambda i: (i, 0)),
                   pl.BlockSpec((T, FOX_W), lambda i: (i, 0))],
        out_shape=[jax.ShapeDtypeStruct((S, FOX_W), BF16), jax.ShapeDtypeStruct((S, FOX_W), BF16),
                   jax.ShapeDtypeStruct((S, N_FFPAD), F32), jax.ShapeDtypeStruct((S, FOX_W), F32)],
        scratch_shapes=[pltpu.VMEM((8, N_FFPAD), F32)],
        compiler_params=_cp(("arbitrary",), _VMEM_MID),
    )(proj, proj, pff, bfp, gq, gk, bd, ex, tril)


def _pair_blk(S, off=0):
    return pl.BlockSpec((S, 128), lambda p: (0, off + p), pipeline_mode=pl.Buffered(1))


def _pair_rows(S):
    return pl.BlockSpec((None, 8, S), lambda p: (p, 0, 0), pipeline_mode=pl.Buffered(1))


def _head_masks(S):
    return lax.broadcasted_iota(jnp.int32, (S, 128), 1) < HEAD_DIM


_EXP_ZERO = 104.0


def _spread_heads(x):
    src = lax.broadcasted_iota(jnp.int32, (128, 128), 0)
    return (_mm3(x, (src == 0).astype(BF16)), _mm3(x, (src == HEAD_DIM).astype(BF16)))


def _max_norm2(x):
    same_head = ((lax.broadcasted_iota(jnp.int32, (128, 128), 0) < HEAD_DIM)
                 == (lax.broadcasted_iota(jnp.int32, (128, 128), 1) < HEAD_DIM)).astype(BF16)
    xf = x.astype(F32)
    return jnp.max(_mm2(xf * xf, same_head), axis=0, keepdims=True)


def _bounds_of(q_norm2, k_norm2):
    z = jnp.sqrt(q_norm2 * k_norm2)
    z = jnp.where(z == z, z, jnp.inf)
    return jnp.max(z[:, 0:1]) * 1.001 + 1e-3, jnp.max(z[:, 64:65]) * 1.001 + 1e-3


def _score_bounds(q, k):
    return _bounds_of(_max_norm2(q), _max_norm2(k))


def _for_tiles_back(i, n, tiles_fn, fours=False):
    if fours:
        def four(t, c):
            tiles_fn([i - 1 - 4 * t, i - 2 - 4 * t, i - 3 - 4 * t, i - 4 - 4 * t])
            return c

        lax.fori_loop(0, lax.shift_right_logical(n, 2), four, 0)
        rest = i - (n & ~3)

        @pl.when((n & 2) != 0)
        def _():
            tiles_fn([rest - 1, rest - 2])
    else:
        def two(t, c):
            tiles_fn([i - 1 - 2 * t, i - 2 - 2 * t])
            return c

        lax.fori_loop(0, lax.shift_right_logical(n, 1), two, 0)

    @pl.when((n & 1) != 0)
    def _():
        tiles_fn([i - n])


def _fox_tiles_back(cr_ref, i, r0, zba, zbb):
    last = cr_ref[:, pl.ds(0, 128)]
    first = cr_ref[:, pl.ds(r0, 128)]
    alive_a = 2.0 * zba + first[0:1, 0:1] - last[2:3, :] > -_EXP_ZERO
    alive_b = 2.0 * zbb + first[1:2, 0:1] - last[3:4, :] > -_EXP_ZERO
    before = lax.broadcasted_iota(jnp.int32, (1, 128), 1) < i
    return jnp.sum((before & (alive_a | alive_b)).astype(jnp.int32))


def _fox_fwd(qs, kn, proj, cqb, crow4, ride=None):
    S = qs.shape[0]
    T = min(_T, S)
    nq = S // T
    n_pairs = FOX_W // 128

    def body(*refs):
        if ride is None:
            q_ref, k_ref, v_ref, cq_ref, cr_ref, o_ref, lse_ref = refs[:7]
            qa, qb, vta, vtb, cka, ckb, ma, mb, acca, accb = refs[7:]
        else:
            q_ref, k_ref, v_ref, cq_ref, cr_ref, wa_ref, wb_ref, o_ref, lse_ref, ga_ref, gb_ref = refs[:11]
            qa, qb, vta, vtb, cka, ckb, ma, mb, acca, accb = refs[11:21]
            xrefs = (wa_ref, wb_ref, ga_ref, gb_ref) + tuple(refs[21:])

            @pl.when(pl.program_id(0) == 0)
            def _():
                _start_exchange("gather", *xrefs)

        lse_ref[...] = jnp.zeros((8, S), F32)
        row_t = lax.broadcasted_iota(jnp.int32, (128, T), 0) < HEAD_DIM
        lane_t = _head_masks(T)

        def prep(c, norms):
            rows = pl.ds(pl.multiple_of(c * T, T), T)
            q = q_ref[rows, :]
            zq = jnp.zeros_like(q)
            qa[rows, :] = jnp.where(lane_t, q, zq)
            qb[rows, :] = jnp.where(lane_t, zq, q)
            cka[rows, :], ckb[rows, :] = _spread_heads(cq_ref[rows, :])
            vt = v_ref[rows, :].T
            vta[:, rows] = jnp.where(row_t, vt, 1.0).astype(BF16)
            vtb[:, rows] = jnp.where(row_t, 1.0, vt).astype(BF16)
            return jnp.maximum(norms[0], _max_norm2(q)), jnp.maximum(norms[1], _max_norm2(k_ref[rows, :]))

        zn = jnp.zeros((1, 128), F32)
        zba, zbb = _bounds_of(*lax.fori_loop(0, nq, prep, (zn, zn)))
        causal =(lax.broadcasted_iota(jnp.int32, (T, T), 0) <= lax.broadcasted_iota(jnp.int32, (T, T), 1))

        heads = ((qa, vta, cka, ma, acca), (qb, vtb, ckb, mb, accb))

        def kv(js, r0, masked):
            cr = cr_ref[:, pl.ds(r0, T)]
            c0s = [pl.multiple_of(j * T, T) for j in js]
            ks = [k_ref[pl.ds(c0, T), :] for c0 in c0s]
            ss = []
            for h, (qr, _, ckr, _, _) in enumerate(heads):
                qh = qr[pl.ds(r0, T), :]
                row = []
                for k, c0 in zip(ks, c0s):
                    s = _dot_nt(k, qh) - jnp.tile(ckr[pl.ds(c0, T), :], (1, T // 128))
                    row.append(jnp.where(causal, s, NEG) if masked else s)
                ss.append(row)
            ms = []
            for h, (row, (_, _, _, mr, _)) in enumerate(zip(ss, heads)):
                top = row[0]
                for s in row[1:]:
                    top = jnp.maximum(top, s)
                m_old = mr[0:1, :]
                ms.append((m_old, jnp.maximum(m_old, jnp.max(top, axis=0, keepdims=True) + cr[h:h + 1, :])))
            ps = [[jnp.exp(s + (cr[h:h + 1, :] - m_new)).astype(BF16) for s in row]
                  for h, (row, (_, m_new)) in enumerate(zip(ss, ms))]
            pvs = []
            for row, (_, vr, _, _, _) in zip(ps, heads):
                pv = _dot(vr[:, pl.ds(c0s[0], T)], row[0])
                for p, c0 in zip(row[1:], c0s[1:]):
                    pv = pv + _dot(vr[:, pl.ds(c0, T)], p)
                pvs.append(pv)
            for pv, (m_old, m_new), (_, _, _, mr, ar) in zip(pvs, ms, heads):
                ar[...] = jnp.exp(m_old - m_new) * ar[...] + pv
                mr[0:1, :] = m_new

        def qblk(i, carry):
            r0 = pl.multiple_of(i * T, T)
            ma[...] = jnp.full((8, T), NEG, F32)
            mb[...] = jnp.full((8, T), NEG, F32)
            acca[...] = jnp.zeros((128, T), F32)
            accb[...] = jnp.zeros((128, T), F32)
            kv([i], r0, True)
            done = _fox_tiles_back(cr_ref, i, r0, zba, zbb)
            _for_tiles_back(i, done, lambda js: kv(js, r0, False), fours=True)
            aa = acca[...]
            ab = accb[...]
            la = aa[64:65, :]
            lb = ab[0:1, :]
            o_ref[pl.ds(r0, T), :] = jnp.where(row_t, aa / la, ab / lb).T
            lse_ref[0:1, pl.ds(r0, T)] = ma[0:1, :] + jnp.log(la)
            lse_ref[1:2, pl.ds(r0, T)] = mb[0:1, :] + jnp.log(lb)
            lse_ref[2:3, pl.ds(r0, T)] = jnp.broadcast_to(done.astype(F32), (1, T))
            return carry

        lax.fori_loop(0, nq, qblk, 0)
        if ride is not None:
            @pl.when(pl.program_id(0) == n_pairs - 1)
            def _():
                _wait_exchange("gather", *xrefs)

    extra = () if ride is None else tuple(ride)
    return pl.pallas_call(
        body, name="fox_fwd" if ride is None else "fox_fwd_gather",
        grid=(n_pairs,),
        in_specs=[_pair_blk(S), _pair_blk(S), _pair_blk(S, OFF_FV // 128), _pair_blk(S), _pair_rows(S)]
        + [_ANY] * len(extra),
        out_specs=[_pair_blk(S), _pair_rows(S)] + [_ANY] * len(extra),
        out_shape=[jax.ShapeDtypeStruct((S, FOX_W), F32), jax.ShapeDtypeStruct((n_pairs, 8, S), F32)]
        + (_exchange_out_shapes("gather", *extra) if extra else []),
        scratch_shapes=[pltpu.VMEM((S, 128), BF16)] * 2 + [pltpu.VMEM((128, S), BF16)] * 2
        + [pltpu.VMEM((S, 128), F32)] * 2 + [pltpu.VMEM((8, T), F32)] * 2 + [pltpu.VMEM((128, T), F32)] * 2
        + (_EXCHANGE_SEMS if extra else []),
        compiler_params=_cp(("arbitrary",), _VMEM_BIG),
    )(qs, kn, proj, cqb, crow4, *extra)


def _softplus_parts(z):
    e = jnp.exp(-jnp.abs(z))
    return e, jnp.maximum(z, 0.0) + jnp.log(1.0 + e)


def _sb_fwd(psb, triu):
    S = psb.shape[0]
    T = triu.shape[0]
    nq = S // T

    n_pairs = SB_W // 128
    H = 2 * n_pairs

    def body(q_ref, k_ref, v_ref, tri_ref, o_ref, lt_ref, qm, vt, rr, acc):
        lane_s = _head_masks(S)
        zbs = []
        for p in range(n_pairs):
            q = (q_ref[:, 128 * p:128 * (p + 1)].astype(F32) * Q_SCALE).astype(BF16)
            zq = jnp.zeros_like(q)
            qm[2 * p] = jnp.where(lane_s, q, zq)
            qm[2 * p + 1] = jnp.where(lane_s, zq, q)
            zbs += list(_score_bounds(q, k_ref[:, 128 * p:128 * (p + 1)]))
        lt_ref[...] = jnp.zeros((n_pairs, 8, S), F32)
        row_t = lax.broadcasted_iota(jnp.int32, (128, T), 0) < HEAD_DIM

        def prep(c, carry):
            c0 = pl.multiple_of(c * T, T)
            for p in range(n_pairs):
                vt[p, :, pl.ds(c0, T)] = v_ref[pl.ds(c0, T), 128 * p:128 * (p + 1)].astype(F32).T.astype(BF16)
            return carry

        lax.fori_loop(0, nq, prep, 0)
        strict = (lax.broadcasted_iota(jnp.int32, (T, T), 0) < lax.broadcasted_iota(jnp.int32, (T, T), 1))

        def kv(tiles, r0):
            tri = tri_ref[...]
            c0s = [pl.multiple_of(j * T, T) for j, _ in tiles]
            zs = [[_dot_nt(k_ref[pl.ds(c0, T), 128 * (h // 2):128 * (h // 2 + 1)], qm[h, pl.ds(r0, T), :])
                   for c0 in c0s] for h in range(H)]
            lbs = [[jnp.where(strict, -_softplus_parts(z)[1], 0.0) if masked else -_softplus_parts(z)[1]
                    for z, (_, masked) in zip(row, tiles)] for row in zs]
            incs = [[_mm2(lb, tri, left=True) for lb in row] for row in lbs]
            avs = []
            for h in range(H):
                r = rr[h, 0:1, :]
                av = None
                for z, inc, c0, (_, masked) in zip(zs[h], incs[h], c0s, tiles):
                    a = jnp.exp(z + inc + r)
                    if masked:
                        a = jnp.where(strict, a, 0.0)
                    term = _dot(vt[h // 2, :, pl.ds(c0, T)], a.astype(BF16))
                    av = term if av is None else av + term
                    r = r + inc[0:1, :]
                avs.append((av, r))
            for h, (av, r) in enumerate(avs):
                rr[h, 0:1, :] = r
                acc[h] = acc[h] + av

        def qblk(i, carry):
            r0 = pl.multiple_of(i * T, T)
            rr[...] = jnp.zeros((H, 8, T), F32)
            acc[...] = jnp.zeros((H, 128, T), F32)

            @pl.when(i == 0)
            def _():
                kv([(i, True)], r0)

            @pl.when(i > 0)
            def _():
                kv([(i, True), (i - 1, False)], r0)

            def alive():
                m = jnp.max(rr[0, 0:1, :]) + zbs[0]
                for h in range(1, H):
                    m = jnp.maximum(m, jnp.max(rr[h, 0:1, :]) + zbs[h])
                return m > -_EXP_ZERO

            def cond(st):
                return (st[0] < i) & st[1]

            def step(st):
                kv([(i - 1 - st[0], False)], r0)
                return st[0] + 1, alive()

            done, _ = lax.while_loop(cond, step, (jnp.minimum(i, 1), alive()))
            for p in range(n_pairs):
                o_ref[pl.ds(r0, T), 128 * p:128 * (p + 1)] = jnp.where(row_t, acc[2 * p], acc[2 * p + 1]).T
                lt_ref[p, 0:1, pl.ds(r0, T)] = rr[2 * p, 0:1, :]
                lt_ref[p, 1:2, pl.ds(r0, T)] = rr[2 * p + 1, 0:1, :]
                lt_ref[p, 2:3, pl.ds(r0, T)] = jnp.broadcast_to(done.astype(F32), (1, T))
            return carry

        lax.fori_loop(0, nq, qblk, 0)

    wide = lambda off: pl.BlockSpec((S, SB_W), lambda g: (0, off), pipeline_mode=pl.Buffered(1))
    return pl.pallas_call(
        body, name="sb_fwd",
        grid=(1,),
        in_specs=[wide(0), wide(1), wide(2), pl.BlockSpec((T, T), lambda g: (0, 0))],
        out_specs=[wide(0), pl.BlockSpec((n_pairs, 8, S), lambda g: (0, 0, 0), pipeline_mode=pl.Buffered(1))],
        out_shape=[jax.ShapeDtypeStruct((S, SB_W), F32), jax.ShapeDtypeStruct((n_pairs, 8, S), F32)],
        scratch_shapes=[pltpu.VMEM((H, S, 128), BF16), pltpu.VMEM((n_pairs, 128, S), BF16),
                        pltpu.VMEM((H, 8, T), F32), pltpu.VMEM((H, 128, T), F32)],
        compiler_params=_cp(("arbitrary",), _VMEM_BIG),
    )(psb, psb, psb, triu)


def _pool_window_lanes(shape):
    lane = lax.broadcasted_iota(jnp.int32, shape, 1)
    return jnp.where(lane < 64, 2, jnp.where(lane < 128, 4, jnp.where(lane < 192, 8, 16)))


def _pool_fwd(proj):
    S = proj.shape[0]

    def body(x_ref, o_ref):
        x = x_ref[...]
        t = lax.broadcasted_iota(jnp.int32, x.shape, 0)
        lane = lax.broadcasted_iota(jnp.int32, x.shape, 1)

        def back(a, k):
            return jnp.where(t >= k, pltpu.roll(a, k, 0), 0.0)

        s1 = x + back(x, 1)
        s2 = s1 + back(s1, 2)
        s4 = s2 + back(s2, 4)
        s8 = s4 + back(s4, 8)
        win = jnp.where(lane < 64, s1, jnp.where(lane < 128, s2, jnp.where(lane < 192, s4, s8)))
        cnt = jnp.minimum(t + 1, _pool_window_lanes(x.shape)).astype(F32)
        o_ref[...] = win / cnt - x

    return pl.pallas_call(
        body, name="pool_fwd",
        grid=(1,),
        in_specs=[pl.BlockSpec((S, POOL_W), lambda i: (0, OFF_PX // POOL_W))],
        out_specs=pl.BlockSpec((S, POOL_W), lambda i: (0, 0)),
        out_shape=jax.ShapeDtypeStruct((S, POOL_W), F32),
        compiler_params=_cp(("arbitrary",), _VMEM_BIG),
    )(proj)


def _silu(g):
    return g * _sigmoid(g)


def _mix_out(fo, so, pooled, proj, wbd, scale, wout, x):
    S, D = x.shape
    tm = min(_TM_ROWS, S)

    def body(fo_ref, fg_ref, so_ref, sg_ref, pl_ref, pg_ref, wbd_ref, sc_ref, w_ref, x_ref, y_ref, mxt_ref, mx_ref):
        parts = ((0, fo_ref[...] * _silu(fg_ref[...])),
                 (FOX_W, (_dot(pl_ref[...].astype(BF16), wbd_ref[...]) * sc_ref[...]) * _silu(pg_ref[...])),
                 (FOX_W + POOL_W, so_ref[...] * _silu(sg_ref[...])))
        for off, part in parts:
            w = part.shape[1]
            mx_ref[:, off:off + w] = part.astype(BF16)
            mxt_ref[off:off + w, :] = part.T.astype(BF16)
        y_ref[...] = x_ref[...] + _dot(mx_ref[...], w_ref[...])

    return pl.pallas_call(
        body, name="mix_out",
        grid=(S // tm,),
        in_specs=[pl.BlockSpec((tm, FOX_W), lambda i: (i, 0)),
                  pl.BlockSpec((tm, FOX_W), lambda i: (i, OFF_FG // FOX_W)),
                  pl.BlockSpec((tm, SB_W), lambda i: (i, 0)),
                  pl.BlockSpec((tm, SB_W), lambda i: (i, OFF_SG // SB_W)),
                  pl.BlockSpec((tm, POOL_W), lambda i: (i, 0)),
                  pl.BlockSpec((tm, POOL_W), lambda i: (i, OFF_PG // POOL_W)),
                  pl.BlockSpec((POOL_W, POOL_W), lambda i: (0, 0)),
                  pl.BlockSpec((1, POOL_W), lambda i: (0, 0)),
                  pl.BlockSpec((D_MIX, D), lambda i: (0, 0)),
                  pl.BlockSpec((tm, D), lambda i: (i, 0))],
        out_specs=[pl.BlockSpec((tm, D), lambda i: (i, 0)), pl.BlockSpec((D_MIX, tm), lambda i: (0, i))],
        out_shape=[jax.ShapeDtypeStruct((S, D), F32), jax.ShapeDtypeStruct((D_MIX, S), BF16)],
        scratch_shapes=[pltpu.VMEM((tm, D_MIX), BF16)],
        compiler_params=_cp(("parallel",), _VMEM_MID),
    )(fo, proj, so, proj, pooled, proj, wbd, scale, wout, x)


def _loss_head(y, target):
    S, D = y.shape
    tm = min(_TM, S)

    def body(y_ref, t_ref, dy_ref, ls_ref):
        @pl.when(pl.program_id(0) == 0)
        def _():
            ls_ref[...] = jnp.zeros_like(ls_ref)

        e = y_ref[...] - t_ref[...]
        dy_ref[...] = e * (1.0 / D)
        ls_ref[...] = ls_ref[...] + jnp.sum(e * e) * (0.5 / D)

    dy, ls = pl.pallas_call(
        body, name="loss_head",
        grid=(S // tm,),
        in_specs=[pl.BlockSpec((tm, D), lambda i: (i, 0)), pl.BlockSpec((tm, D), lambda i: (i, 0))],
        out_specs=[pl.BlockSpec((tm, D), lambda i: (i, 0)), pl.BlockSpec((8, 128), lambda i: (0, 0))],
        out_shape=[jax.ShapeDtypeStruct((S, D), F32), jax.ShapeDtypeStruct((8, 128), F32)],
        compiler_params=_cp(("arbitrary",), _VMEM_MID),
    )(y, target)
    return dy, ls[0, 0]


def _dsilu(g):
    s = _sigmoid(g)
    return s * (1.0 + g * (1.0 - s))


def _gate_bwd(dy, wout, fo, so, pooled, proj, wbd, scale):
    S, D = dy.shape
    tm = min(_TM_ROWS, S)

    def body(dy_ref, w_ref, fo_ref, fg_ref, so_ref, sg_ref, pl_ref, pg_ref, wbd_ref, sc_ref,
             dfo_ref, dfg_ref, dso_ref, dsg_ref, dpg_ref, dpl_ref, dsc_ref, dwbd_ref):
        @pl.when(pl.program_id(0) == 0)
        def _():
            dsc_ref[...] = jnp.zeros_like(dsc_ref)
            dwbd_ref[...] = jnp.zeros_like(dwbd_ref)

        dm = _dot_nt(dy_ref[...].astype(BF16), w_ref[...])
        dmf = dm[:, 0:FOX_W]
        dmp = dm[:, FOX_W:FOX_W + POOL_W]
        dms = dm[:, FOX_W + POOL_W:D_MIX]
        fg = fg_ref[...]
        dfo_ref[...] = dmf * _silu(fg)
        dfg_ref[...] = (dmf * fo_ref[...] * _dsilu(fg)).astype(BF16)
        sg = sg_ref[...]
        dso_ref[...] = (dms * _silu(sg)).astype(BF16)
        dsg_ref[...] = (dms * so_ref[...] * _dsilu(sg)).astype(BF16)
        pg = pg_ref[...]
        plb = pl_ref[...].astype(BF16)
        yw = _dot(plb, wbd_ref[...])
        sc = sc_ref[...]
        dpg_ref[...] = (dmp * (yw * sc) * _dsilu(pg)).astype(BF16)
        dys = dmp * _silu(pg)
        dsc_ref[...] = dsc_ref[...] + jnp.sum(dys * yw, axis=0, keepdims=True)
        dyw = (dys * sc).astype(BF16)
        dpl_ref[...] = _dot_nt(dyw, wbd_ref[...])
        dwbd_ref[...] = dwbd_ref[...] + _dot_tn(plb, dyw)

    return pl.pallas_call(
        body, name="gate_bwd",
        grid=(S // tm,),
        in_specs=[pl.BlockSpec((tm, D), lambda i: (i, 0)),
                  pl.BlockSpec((D_MIX, D), lambda i: (0, 0)),
                  pl.BlockSpec((tm, FOX_W), lambda i: (i, 0)),
                  pl.BlockSpec((tm, FOX_W), lambda i: (i, OFF_FG // FOX_W)),
                  pl.BlockSpec((tm, SB_W), lambda i: (i, 0)),
                  pl.BlockSpec((tm, SB_W), lambda i: (i, OFF_SG // SB_W)),
                  pl.BlockSpec((tm, POOL_W), lambda i: (i, 0)),
                  pl.BlockSpec((tm, POOL_W), lambda i: (i, OFF_PG // POOL_W)),
                  pl.BlockSpec((POOL_W, POOL_W), lambda i: (0, 0)),
                  pl.BlockSpec((1, POOL_W), lambda i: (0, 0))],
        out_specs=[pl.BlockSpec((tm, FOX_W), lambda i: (i, 0)),
                   pl.BlockSpec((tm, FOX_W), lambda i: (i, 0)),
                   pl.BlockSpec((tm, SB_W), lambda i: (i, 0)),
                   pl.BlockSpec((tm, SB_W), lambda i: (i, 0)),
                   pl.BlockSpec((tm, POOL_W), lambda i: (i, 0)),
                   pl.BlockSpec((tm, POOL_W), lambda i: (i, 0)),
                   pl.BlockSpec((1, POOL_W), lambda i: (0, 0)),
                   pl.BlockSpec((POOL_W, POOL_W), lambda i: (0, 0))],
        out_shape=[jax.ShapeDtypeStruct((S, FOX_W), F32), jax.ShapeDtypeStruct((S, FOX_W), BF16),
                   jax.ShapeDtypeStruct((S, SB_W), BF16), jax.ShapeDtypeStruct((S, SB_W), BF16),
                   jax.ShapeDtypeStruct((S, POOL_W), BF16), jax.ShapeDtypeStruct((S, POOL_W), F32),
                   jax.ShapeDtypeStruct((1, POOL_W), F32), jax.ShapeDtypeStruct((POOL_W, POOL_W), F32)],
        compiler_params=_cp(("arbitrary",), _VMEM_MID),
    )(dy, wout, fo, proj, so, proj, pooled, proj, wbd, scale)


def _matmul_acc(at, b, name):
    M, S = at.shape
    N = b.shape[1]
    tk = min(_TK_DW, S)
    tn = min(512, N)
    nk = S // tk

    def body(a_ref, b_ref, o_ref, acc):
        k = pl.program_id(1)

        @pl.when(k == 0)
        def _():
            acc[...] = jnp.zeros_like(acc)

        acc[...] = acc[...] + _dot(a_ref[...], b_ref[...].astype(BF16))

        @pl.when(k == nk - 1)
        def _():
            o_ref[...] = acc[...].astype(BF16)

    return pl.pallas_call(
        body, name=name,
        grid=(N // tn, nk),
        in_specs=[pl.BlockSpec((M, tk), lambda j, k: (0, k)), pl.BlockSpec((tk, tn), lambda j, k: (k, j))],
        out_specs=pl.BlockSpec((M, tn), lambda j, k: (0, j)),
        out_shape=jax.ShapeDtypeStruct((M, N), BF16),
        scratch_shapes=[pltpu.VMEM((M, tn), F32)],
        compiler_params=_cp(("parallel", "arbitrary"), _VMEM_MID),
    )(at, b)


def _pool_bwd(dpooled):
    S = dpooled.shape[0]

    def body(d_ref, o_ref):
        d = d_ref[...]
        t = lax.broadcasted_iota(jnp.int32, d.shape, 0)
        lane = lax.broadcasted_iota(jnp.int32, d.shape, 1)
        cnt = jnp.minimum(t + 1, _pool_window_lanes(d.shape)).astype(F32)
        u = d / cnt

        def fwd(a, k):
            return jnp.where(t < S - k, pltpu.roll(a, S - k, 0), 0.0)

        s1 = u + fwd(u, 1)
        s2 = s1 + fwd(s1, 2)
        s4 = s2 + fwd(s2, 4)
        s8 = s4 + fwd(s4, 8)
        win = jnp.where(lane < 64, s1, jnp.where(lane < 128, s2, jnp.where(lane < 192, s4, s8)))
        o_ref[...] = (win - d).astype(BF16)

    return pl.pallas_call(
        body, name="pool_bwd",
        grid=(1,),
        in_specs=[pl.BlockSpec((S, POOL_W), lambda i: (0, 0))],
        out_specs=pl.BlockSpec((S, POOL_W), lambda i: (0, 0)),
        out_shape=jax.ShapeDtypeStruct((S, POOL_W), BF16),
        compiler_params=_cp(("arbitrary",), _VMEM_BIG),
    )(dpooled)


def _fox_bwd(qs, kn, proj, dfo, fo, lse, cqb, crow4, ride=None):
    S = qs.shape[0]
    T = min(_T, S)
    nq = S // T
    n_pairs = FOX_W // 128

    def body(*refs):
        if ride is None:
            q_ref, k_ref, v_ref, do_ref, o_ref, lse_ref, cq_ref, cr_ref = refs[:8]
            dq_ref, dk_ref, dv_ref, dck_ref, dcq_ref = refs[8:13]
            scr = refs[13:]
        else:
            q_ref, k_ref, v_ref, do_ref, o_ref, lse_ref, cq_ref, cr_ref, pa_ref, pb_ref = refs[:10]
            dq_ref, dk_ref, dv_ref, dck_ref, dcq_ref, ra_ref, rb_ref = refs[10:17]
            scr = refs[17:32]
            xrefs = (pa_ref, pb_ref, ra_ref, rb_ref) + tuple(refs[32:])

            @pl.when(pl.program_id(0) == 0)
            def _():
                _start_exchange("scatter", *xrefs)

        qa, qb, kta, ktb, vb, doa, dob, cka, ckb, dcka, dckb, dva, dqt, dcqa, dcqb = scr
        lane_t = _head_masks(T)
        zs = jnp.zeros((S, 128), F32)
        dk_ref[...] = zs
        dva[...] = zs
        dcka[...] = zs
        dckb[...] = zs
        dcq_ref[...] = jnp.zeros((8, S), F32)
        row_t = lax.broadcasted_iota(jnp.int32, (128, T), 0) < HEAD_DIM

        def prep(c, carry):
            rows = pl.ds(pl.multiple_of(c * T, T), T)
            q = q_ref[rows, :]
            zq = jnp.zeros_like(q)
            qa[rows, :] = jnp.where(lane_t, q, zq)
            qb[rows, :] = jnp.where(lane_t, zq, q)
            vb[rows, :] = v_ref[rows, :].astype(BF16)
            do = do_ref[rows, :].astype(BF16)
            doa[rows, :] = jnp.where(lane_t, do, zq)
            dob[rows, :] = jnp.where(lane_t, zq, do)
            cka[rows, :], ckb[rows, :] = _spread_heads(cq_ref[rows, :])
            kt = k_ref[rows, :].astype(F32).T
            kta[:, rows] = jnp.where(row_t, kt, 0.0).astype(BF16)
            ktb[:, rows] = jnp.where(row_t, 0.0, kt).astype(BF16)
            return carry

        lax.fori_loop(0, nq, prep, 0)
        causal = (lax.broadcasted_iota(jnp.int32, (T, T), 0) <= lax.broadcasted_iota(jnp.int32, (T, T), 1))

        heads = ((qa, kta, doa, cka, dcka, dcqa), (qb, ktb, dob, ckb, dckb, dcqb))

        def kv(js, r0, lss, dls, masked):
            cr = cr_ref[:, pl.ds(r0, T)]
            c0s = [pl.multiple_of(j * T, T) for j in js]
            ks = [k_ref[pl.ds(c0, T), :] for c0 in c0s]
            vs = [vb[pl.ds(c0, T), :] for c0 in c0s]
            qhs = [hd[0][pl.ds(r0, T), :] for hd in heads]
            dohs = [hd[2][pl.ds(r0, T), :] for hd in heads]
            ss = []
            for h, hd in enumerate(heads):
                row = []
                for k, c0 in zip(ks, c0s):
                    s = _dot_nt(k, qhs[h]) - jnp.tile(hd[3][pl.ds(c0, T), :], (1, T // 128))
                    row.append(jnp.where(causal, s, NEG) if masked else s)
                ss.append(row)
            ps = [[jnp.exp(s + (cr[h:h + 1, :] - lss[h])) for s in row] for h, row in enumerate(ss)]
            dps = [[_dot_nt(v, dohs[h]) for v in vs] for h in range(2)]
            dss = [[p * (dp - dls[h]) for p, dp in zip(ps[h], dps[h])] for h in range(2)]
            pbs = [[p.astype(BF16) for p in row] for row in ps]
            dsbs = [[ds.astype(BF16) for ds in row] for row in dss]
            for t, c0 in enumerate(c0s):
                dva[pl.ds(c0, T), :] = dva[pl.ds(c0, T), :] + (_dot(pbs[0][t], dohs[0]) + _dot(pbs[1][t], dohs[1]))
                dk_ref[pl.ds(c0, T), :] = dk_ref[pl.ds(c0, T), :] + (_dot(dsbs[0][t], qhs[0]) + _dot(dsbs[1][t], qhs[1]))
            dq = None
            for h, hd in enumerate(heads):
                for t, c0 in enumerate(c0s):
                    term = _dot(hd[1][:, pl.ds(c0, T)], dsbs[h][t])
                    dq = term if dq is None else dq + term
            dqt[...] = dqt[...] + dq
            for h, hd in enumerate(heads):
                col = jnp.sum(dss[h][0], axis=0, keepdims=True)
                for ds in dss[h][1:]:
                    col = col + jnp.sum(ds, axis=0, keepdims=True)
                hd[5][0:1, :] = hd[5][0:1, :] + col
                for ds, c0 in zip(dss[h], c0s):
                    fold = ds[:, 0:128]
                    for u in range(1, T // 128):
                        fold = fold + ds[:, 128 * u:128 * (u + 1)]
                    hd[4][pl.ds(c0, T), :] = hd[4][pl.ds(c0, T), :] - fold

        def qblk(i, carry):
            r0 = pl.multiple_of(i * T, T)
            dt = (do_ref[pl.ds(r0, T), :] * o_ref[pl.ds(r0, T), :]).T
            dla = jnp.sum(jnp.where(row_t, dt, 0.0), axis=0, keepdims=True)
            dlb = jnp.sum(jnp.where(row_t, 0.0, dt), axis=0, keepdims=True)
            ls = lse_ref[:, pl.ds(r0, T)]
            lss = (ls[0:1, :], ls[1:2, :])
            back = jnp.max(ls[2:3, :]).astype(jnp.int32)
            dqt[...] = jnp.zeros((128, T), F32)
            dcqa[...] = jnp.zeros((8, T), F32)
            dcqb[...] = jnp.zeros((8, T), F32)
            kv([i], r0, lss, (dla, dlb), True)
            _for_tiles_back(i, back, lambda js: kv(js, r0, lss, (dla, dlb), False), fours=True)
            dq_ref[pl.ds(r0, T), :] = dqt[...].T
            dcq_ref[0:1, pl.ds(r0, T)] = dcqa[0:1, :]
            dcq_ref[1:2, pl.ds(r0, T)] = dcqb[0:1, :]
            return carry

        lax.fori_loop(0, nq, qblk, 0)
        dv_ref[...] = dva[...].astype(BF16)
        @pl.when(pl.program_id(0) == 0)
        def _():
            dck_ref[...] = jnp.zeros((S, N_FFPAD), F32)

        head_lane = lax.broadcasted_iota(jnp.int32, (S, N_FFPAD), 1) - 2 * pl.program_id(0)
        dck_ref[...] = jnp.where(head_lane == 0, jnp.sum(dcka[...], axis=1, keepdims=True),
                                 jnp.where(head_lane == 1, jnp.sum(dckb[...], axis=1, keepdims=True), dck_ref[...]))
        if ride is not None:
            @pl.when(pl.program_id(0) == n_pairs - 1)
            def _():
                _wait_exchange("scatter", *xrefs)

    extra = () if ride is None else tuple(ride)
    return pl.pallas_call(
        body, name="fox_bwd" if ride is None else "fox_bwd_exchange",
        grid=(n_pairs,),
        in_specs=[_pair_blk(S), _pair_blk(S), _pair_blk(S, OFF_FV // 128), _pair_blk(S), _pair_blk(S),
                  _pair_rows(S), _pair_blk(S), _pair_rows(S)] + [_ANY] * len(extra),
        out_specs=[_pair_blk(S), _pair_blk(S), _pair_blk(S), pl.BlockSpec((S, N_FFPAD), lambda p: (0, 0)),
                   _pair_rows(S)] + [_ANY] * len(extra),
        out_shape=[jax.ShapeDtypeStruct((S, FOX_W), F32), jax.ShapeDtypeStruct((S, FOX_W), F32),
                   jax.ShapeDtypeStruct((S, FOX_W), BF16), jax.ShapeDtypeStruct((S, N_FFPAD), F32),
                   jax.ShapeDtypeStruct((n_pairs, 8, S), F32)]
        + (_exchange_out_shapes("scatter", *extra) if extra else []),
        scratch_shapes=[pltpu.VMEM((S, 128), BF16)] * 2 + [pltpu.VMEM((128, S), BF16)] * 2
        + [pltpu.VMEM((S, 128), BF16)] * 3 + [pltpu.VMEM((S, 128), F32)] * 5
        + [pltpu.VMEM((128, T), F32)] + [pltpu.VMEM((8, T), F32)] * 2
        + (_EXCHANGE_SEMS if extra else []),
        compiler_params=_cp(("arbitrary",), _VMEM_BIG),
    )(qs, kn, proj, dfo, fo, lse, cqb, crow4, *extra)


def _sb_bwd(psb, dso, ltot, tril):
    S = psb.shape[0]
    T = tril.shape[0]
    nq = S // T
    n_pairs = SB_W // 128
    H = 2 * n_pairs

    def body(q_ref, k_ref, v_ref, do_ref, lt_ref, tri_ref, dq_ref, dk_ref, dv_ref,
             qm, kt, dka, dva, dqt, rr, gg):
        lane_s = _head_masks(S)
        for p in range(n_pairs):
            q = (q_ref[:, 128 * p:128 * (p + 1)].astype(F32) * Q_SCALE).astype(BF16)
            zq = jnp.zeros_like(q)
            qm[2 * p] = jnp.where(lane_s, q, zq)
            qm[2 * p + 1] = jnp.where(lane_s, zq, q)
        dka[...] = jnp.zeros((n_pairs, S, 128), F32)
        dva[...] = jnp.zeros((n_pairs, S, 128), F32)
        row_t = lax.broadcasted_iota(jnp.int32, (128, T), 0) < HEAD_DIM
        lane_t = lax.broadcasted_iota(jnp.int32, (T, 128), 1) < HEAD_DIM

        def prep(c, carry):
            c0 = pl.multiple_of(c * T, T)
            for p in range(n_pairs):
                kt[p, :, pl.ds(c0, T)] = k_ref[pl.ds(c0, T), 128 * p:128 * (p + 1)].astype(F32).T.astype(BF16)
            return carry

        lax.fori_loop(0, nq, prep, 0)
        strict = (lax.broadcasted_iota(jnp.int32, (T, T), 0) < lax.broadcasted_iota(jnp.int32, (T, T), 1))

        def own(x, h, mask):
            z = jnp.zeros_like(x)
            return jnp.where(mask, x, z) if h % 2 == 0 else jnp.where(mask, z, x)

        def pair(ref, p, c0):
            return ref[pl.ds(c0, T), 128 * p:128 * (p + 1)]

        def kv(tiles, r0, lts):
            tri = tri_ref[...]
            c0s = [pl.multiple_of(j * T, T) for j, _ in tiles]
            qhs = [qm[h, pl.ds(r0, T), :] for h in range(H)]
            dohs = [own(pair(do_ref, h // 2, r0), h, lane_t) for h in range(H)]
            zs = [[_dot_nt(pair(k_ref, h // 2, c0), qhs[h]) for c0 in c0s] for h in range(H)]
            das = [[_dot_nt(pair(v_ref, h // 2, c0), dohs[h]) for c0 in c0s] for h in range(H)]
            es, lbs = [], []
            for row in zs:
                erow, lrow = [], []
                for z, (_, masked) in zip(row, tiles):
                    e, sp = _softplus_parts(z)
                    erow.append(e)
                    lrow.append(jnp.where(strict, -sp, 0.0) if masked else -sp)
                es.append(erow)
                lbs.append(lrow)
            pres = [[_mm2(lb, tri, left=True) for lb in row] for row in lbs]
            aas, r_ends = [], []
            for h in range(H):
                r = rr[h, 0:1, :]
                arow = []
                for z, lb, pre, (_, masked) in zip(zs[h], lbs[h], pres[h], tiles):
                    a = jnp.exp(z + lb + ((lts[h] - r) - pre))
                    arow.append(jnp.where(strict, a, 0.0) if masked else a)
                    r = r + pre[T - 1:T, :]
                aas.append(arow)
                r_ends.append(r)
            gs = [[a * da for a, da in zip(arow, drow)] for arow, drow in zip(aas, das)]
            gpres = [[_mm2(g, tri, left=True) for g in row] for row in gs]
            dzbs, g_ends = [], []
            for h in range(H):
                gc = gg[h, 0:1, :]
                drow = []
                for z, e, g, gpre, (_, masked) in zip(zs[h], es[h], gs[h], gpres[h], tiles):
                    inv = 1.0 / (1.0 + e)
                    pos = z >= 0.0
                    sig = jnp.where(pos, 1.0, e) * inv
                    oms = jnp.where(pos, e, 1.0) * inv
                    dz = g * oms - sig * (gc + (gpre - g))
                    if masked:
                        dz = jnp.where(strict, dz, 0.0)
                    drow.append(dz.astype(BF16))
                    gc = gc + gpre[T - 1:T, :]
                dzbs.append(drow)
                g_ends.append(gc)
            for p in range(n_pairs):
                a, b = 2 * p, 2 * p + 1
                dq = None
                for h in (a, b):
                    for t, c0 in enumerate(c0s):
                        term = _dot(own(kt[p, :, pl.ds(c0, T)], h, row_t), dzbs[h][t])
                        dq = term if dq is None else dq + term
                dqt[p] = dqt[p] + dq
                for t, c0 in enumerate(c0s):
                    dka[p, pl.ds(c0, T), :] = dka[p, pl.ds(c0, T), :] + (_dot(dzbs[a][t], qhs[a]) + _dot(dzbs[b][t], qhs[b]))
                    dva[p, pl.ds(c0, T), :] = dva[p, pl.ds(c0, T), :] + (_dot(aas[a][t].astype(BF16), dohs[a])
                                                                      + _dot(aas[b][t].astype(BF16), dohs[b]))
            for h in range(H):
                rr[h, 0:1, :] = r_ends[h]
                gg[h, 0:1, :] = g_ends[h]

        def qblk(i, carry):
            r0 = pl.multiple_of(i * T, T)
            lts = []
            for p in range(n_pairs):
                lt = lt_ref[p, :, pl.ds(r0, T)]
                lts += [lt[0:1, :], lt[1:2, :]]
            back = jnp.max(lt_ref[0, 2:3, pl.ds(r0, T)]).astype(jnp.int32)
            dqt[...] = jnp.zeros((n_pairs, 128, T), F32)
            rr[...] = jnp.zeros((H, 8, T), F32)
            gg[...] = jnp.zeros((H, 8, T), F32)

            def inner(j, c):
                kv([(j, False)], r0, lts)
                return c

            @pl.when(back == 0)
            def _():
                kv([(i, True)], r0, lts)

            @pl.when(back > 0)
            def _():
                lax.fori_loop(i - back, i - 1, inner, 0)
                kv([(i - 1, False), (i, True)], r0, lts)

            for p in range(n_pairs):
                dq_ref[pl.ds(r0, T), 128 * p:128 * (p + 1)] = (dqt[p] * Q_SCALE).T.astype(BF16)
            return carry

        lax.fori_loop(0, nq, qblk, 0)
        for p in range(n_pairs):
            dk_ref[:, 128 * p:128 * (p + 1)] = dka[p].astype(BF16)
            dv_ref[:, 128 * p:128 * (p + 1)] = dva[p].astype(BF16)

    wide = lambda off: pl.BlockSpec((S, SB_W), lambda g: (0, off), pipeline_mode=pl.Buffered(1))
    return pl.pallas_call(
        body, name="sb_bwd",
        grid=(1,),
        in_specs=[wide(0), wide(1), wide(2), wide(0),
                  pl.BlockSpec((n_pairs, 8, S), lambda g: (0, 0, 0), pipeline_mode=pl.Buffered(1)),
                  pl.BlockSpec((T, T), lambda g: (0, 0))],
        out_specs=[wide(0), wide(0), wide(0)],
        out_shape=[jax.ShapeDtypeStruct((S, SB_W), BF16)] * 3,
        scratch_shapes=[pltpu.VMEM((H, S, 128), BF16), pltpu.VMEM((n_pairs, 128, S), BF16),
                        pltpu.VMEM((n_pairs, S, 128), F32), pltpu.VMEM((n_pairs, S, 128), F32),
                        pltpu.VMEM((n_pairs, 128, T), F32), pltpu.VMEM((H, 8, T), F32), pltpu.VMEM((H, 8, T), F32)],
        compiler_params=_cp(("arbitrary",), _VMEM_BIG),
    )(psb, psb, psb, dso, ltot, tril)


def _head_norm_bwd(x, g, dy, bd):
    ss = _mm2(x * x, bd)
    r = lax.rsqrt(ss * (1.0 / HEAD_DIM) + EPS)
    xr = x * r
    gdy = g * dy
    m = _mm2(xr * gdy, bd) * (1.0 / HEAD_DIM)
    return r * (gdy - xr * m), dy * xr


def _qk_bwd(dqs, dkn, proj, pff, bfp, gq, gk, bd, dck, dcq, triu):
    S = proj.shape[0]
    T = triu.shape[0]
    n = S // T
    rev = lambda col: (lambda i: (n - 1 - i, col))

    def body(dq_ref, dk_ref, q_ref, k_ref, ff_ref, b_ref, gq_ref, gk_ref, bd_ref, dck_ref, dcq_ref, tri_ref,
             dfq_ref, dfk_ref, dff_ref, dgq_ref, dgk_ref, dbf_ref, carry):
        @pl.when(pl.program_id(0) == 0)
        def _():
            carry[...] = jnp.zeros_like(carry)
            dgq_ref[...] = jnp.zeros_like(dgq_ref)
            dgk_ref[...] = jnp.zeros_like(dgk_ref)
            dbf_ref[...] = jnp.zeros_like(dbf_ref)

        bdv = bd_ref[...]
        dxq, gq_rows = _head_norm_bwd(q_ref[...], gq_ref[...], dq_ref[...] * Q_SCALE, bdv)
        dfq_ref[...] = dxq.astype(BF16)
        dgq_ref[...] = dgq_ref[...] + jnp.sum(gq_rows, axis=0, keepdims=True)
        dxk, gk_rows = _head_norm_bwd(k_ref[...], gk_ref[...], dk_ref[...], bdv)
        dfk_ref[...] = dxk.astype(BF16)
        dgk_ref[...] = dgk_ref[...] + jnp.sum(gk_rows, axis=0, keepdims=True)
        dlf = _mm3(dck_ref[...] + dcq_ref[...], tri_ref[...], left=True) + carry[0:1, :]
        carry[0:1, :] = dlf[0:1, :]
        u = ff_ref[...] + b_ref[...]
        lane = lax.broadcasted_iota(jnp.int32, u.shape, 1)
        dff = jnp.where(lane < N_FF, dlf * _sigmoid(-u), 0.0)
        dff_ref[...] = dff.astype(BF16)
        dbf_ref[...] = dbf_ref[...] + jnp.sum(dff, axis=0, keepdims=True)

    return pl.pallas_call(
        body, name="qk_bwd",
        grid=(n,),
        in_specs=[pl.BlockSpec((T, FOX_W), rev(0)), pl.BlockSpec((T, FOX_W), rev(0)),
                  pl.BlockSpec((T, FOX_W), rev(OFF_FQ // FOX_W)), pl.BlockSpec((T, FOX_W), rev(OFF_FK // FOX_W)),
                  pl.BlockSpec((T, N_FFPAD), rev(0)),
                  pl.BlockSpec((1, N_FFPAD), lambda i: (0, 0)),
                  pl.BlockSpec((1, FOX_W), lambda i: (0, 0)), pl.BlockSpec((1, FOX_W), lambda i: (0, 0)),
                  pl.BlockSpec((FOX_W, FOX_W), lambda i: (0, 0)),
                  pl.BlockSpec((T, N_FFPAD), rev(0)), pl.BlockSpec((T, N_FFPAD), rev(0)),
                  pl.BlockSpec((T, T), lambda i: (0, 0))],
        out_specs=[pl.BlockSpec((T, FOX_W), rev(0)), pl.BlockSpec((T, FOX_W), rev(0)),
                   pl.BlockSpec((T, N_FFPAD), rev(0)),
                   pl.BlockSpec((1, FOX_W), lambda i: (0, 0)), pl.BlockSpec((1, FOX_W), lambda i: (0, 0)),
                   pl.BlockSpec((1, N_FFPAD), lambda i: (0, 0))],
        out_shape=[jax.ShapeDtypeStruct((S, FOX_W), BF16), jax.ShapeDtypeStruct((S, FOX_W), BF16),
                   jax.ShapeDtypeStruct((S, N_FFPAD), BF16),
                   jax.ShapeDtypeStruct((1, FOX_W), F32), jax.ShapeDtypeStruct((1, FOX_W), F32),
                   jax.ShapeDtypeStruct((1, N_FFPAD), F32)],
        scratch_shapes=[pltpu.VMEM((8, N_FFPAD), F32)],
        compiler_params=_cp(("arbitrary",), _VMEM_MID),
    )(dqs, dkn, proj, proj, pff, bfp, gq, gk, bd, dck, dcq, triu)


def _dproj_layout(pieces):
    offs, o = [], 0
    for p in pieces:
        offs.append(o)
        o += p.shape[1]
    assert o == N_MAIN
    return offs


def _inproj_bwd_dx(pieces, dff, wm, wff, x, g, dy, ride=None):
    S, D = x.shape
    tm = min(_TM_DX, S)
    steps = S // tm
    offs = _dproj_layout(pieces)
    n = len(pieces)

    def body(*refs):
        p_refs = refs[:n]
        if ride is None:
            dff_ref, w_ref, wff_ref, x_ref, g_ref, dy_ref, dx_ref, dg_ref = refs[n:]
        else:
            dff_ref, w_ref, wff_ref, x_ref, g_ref, dy_ref, pa_ref, pb_ref = refs[n:n + 8]
            dx_ref, dg_ref, ra_ref, rb_ref = refs[n + 8:n + 12]
            xrefs = (pa_ref, pb_ref, ra_ref, rb_ref) + tuple(refs[n + 12:])

        @pl.when(pl.program_id(0) == 0)
        def _():
            dg_ref[...] = jnp.zeros_like(dg_ref)
            if ride is not None:
                _start_exchange("scatter", *xrefs)

        dh = _dot_nt(dff_ref[...], wff_ref[...])
        for p_ref, off in zip(p_refs, offs):
            dh = dh + _dot_nt(p_ref[...], w_ref[:, off:off + p_ref.shape[1]])
        xv = x_ref[...]
        r = _rms_rows(xv)
        xr = xv * r
        dg_ref[...] = dg_ref[...] + jnp.sum(dh * xr, axis=0, keepdims=True)
        gdh = g_ref[...] * dh
        m = jnp.mean(gdh * xr, axis=-1, keepdims=True)
        dx_ref[...] = dy_ref[...] + r * (gdh - xr * m)
        if ride is not None:
            @pl.when(pl.program_id(0) == steps - 1)
            def _():
                _wait_exchange("scatter", *xrefs)

    extra = () if ride is None else tuple(ride)
    return pl.pallas_call(
        body, name="inproj_bwd_dx" if ride is None else "inproj_bwd_dx_exchange",
        grid=(steps,),
        in_specs=[pl.BlockSpec((tm, p.shape[1]), lambda i: (i, 0)) for p in pieces]
        + [pl.BlockSpec((tm, N_FFPAD), lambda i: (i, 0)),
                  pl.BlockSpec((D, N_MAIN), lambda i: (0, 0)),
                  pl.BlockSpec((D, N_FFPAD), lambda i: (0, 0)),
                  pl.BlockSpec((tm, D), lambda i: (i, 0)),
                  pl.BlockSpec((1, D), lambda i: (0, 0)),
                  pl.BlockSpec((tm, D), lambda i: (i, 0))] + [_ANY] * len(extra),
        out_specs=[pl.BlockSpec((tm, D), lambda i: (i, 0)), pl.BlockSpec((1, D), lambda i: (0, 0))] + [_ANY] * len(extra),
        out_shape=[jax.ShapeDtypeStruct((S, D), F32), jax.ShapeDtypeStruct((1, D), F32)]
        + (_exchange_out_shapes("scatter", *extra) if extra else []),
        scratch_shapes=_EXCHANGE_SEMS if extra else [],
        compiler_params=_cp(("arbitrary",), _VMEM_WIDE),
    )(*pieces, dff, wm, wff, x, g, dy, *extra)


def _inproj_bwd_dw(ht, pieces, dff):
    D, S = ht.shape
    tk = min(_TK_DW, S)
    nk = S // tk
    offs = _dproj_layout(pieces)
    n = len(pieces)

    def body(*refs):
        ht_ref, p_refs, dff_ref = refs[0], refs[1:1 + n], refs[1 + n]
        dw_ref, dwff_ref, acc, accff = refs[2 + n:]
        k = pl.program_id(0)

        @pl.when(k == 0)
        def _():
            acc[...] = jnp.zeros_like(acc)
            accff[...] = jnp.zeros_like(accff)

        hb = ht_ref[...]
        for p_ref, off in zip(p_refs, offs):
            w = p_ref.shape[1]
            acc[:, off:off + w] = acc[:, off:off + w] + _dot(hb, p_ref[...])
        accff[...] = accff[...] + _dot(hb, dff_ref[...])

        @pl.when(k == nk - 1)
        def _():
            for c0 in range(0, N_MAIN, FOX_W):
                dw_ref[c0:c0 + FOX_W, :] = acc[:, c0:c0 + FOX_W].T.astype(BF16)
            dwff_ref[...] = accff[...].T.astype(BF16)

    return pl.pallas_call(
        body, name="inproj_bwd_dw",
        grid=(nk,),
        in_specs=[pl.BlockSpec((D, tk), lambda k: (0, k))]
        + [pl.BlockSpec((tk, p.shape[1]), lambda k: (k, 0)) for p in pieces]
        + [pl.BlockSpec((tk, N_FFPAD), lambda k: (k, 0))],
        out_specs=[pl.BlockSpec((N_MAIN, D), lambda k: (0, 0), pipeline_mode=pl.Buffered(1)),
                   pl.BlockSpec((N_FFPAD, D), lambda k: (0, 0), pipeline_mode=pl.Buffered(1))],
        out_shape=[jax.ShapeDtypeStruct((N_MAIN, D), BF16), jax.ShapeDtypeStruct((N_FFPAD, D), BF16)],
        scratch_shapes=[pltpu.VMEM((D, N_MAIN), F32), pltpu.VMEM((D, N_FFPAD), F32)],
        compiler_params=_cp(("arbitrary",), _VMEM_BIG),
    )(ht, *pieces, dff)


def _constants(T, rows):
    tril = jnp.tril(jnp.ones((T, T), F32)).astype(BF16)
    tril_rows = jnp.tril(jnp.ones((rows, rows), F32)).astype(BF16)
    hid = jnp.arange(FOX_W) // HEAD_DIM
    bd = (hid[:, None] == hid[None, :]).astype(BF16)
    ex = (jnp.arange(N_FFPAD)[:, None] == hid[None, :]).astype(BF16)
    return tril, tril.T, bd, ex, tril_rows, tril_rows.T


def _crow4(ccol, T):
    S = ccol.shape[0]
    c = ccol[:, :FOX_HEADS].T
    last = jnp.pad(c[:, T - 1::T], ((0, 0), (0, S - S // T)))
    rows = jnp.concatenate([c.reshape(FOX_HEADS // 2, 2, S), last.reshape(FOX_HEADS // 2, 2, S)], axis=1)
    return jnp.pad(rows, ((0, 0), (0, 4), (0, 0)))


def _layer_fwd(x, lw, consts, ride=None):
    tril, triu, bd, ex, tril_rows, _ = consts
    proj, pff, ht, psb = _inproj_fwd(x, lw["g"], lw["wm"], lw["wff"])
    qs, kn, ccol, cqb = _fox_prep(proj, pff, lw["bfp"], lw["gq"], lw["gk"], bd, ex, tril_rows)
    crow4 = _crow4(ccol, tril.shape[0])
    fo, lse, *gathered = _fox_fwd(qs, kn, proj, cqb, crow4, ride)
    so, ltot = _sb_fwd(psb, triu)
    pooled = _pool_fwd(proj)
    y, mixedt = _mix_out(fo, so, pooled, proj, lw["wbd"], lw["scale"], lw["wout"], x)
    return y, (x, proj, pff, ht, psb, qs, kn, cqb, crow4, fo, lse, so, ltot, pooled, mixedt), gathered


def _layer_bwd(dy, saved, lw, consts, ride=None, exchange_own=False):
    tril, _, bd, _, _, triu_rows = consts
    x, proj, pff, ht, psb, qs, kn, cqb, crow4, fo, lse, so, ltot, pooled, mixedt = saved
    S = x.shape[0]
    dfo, dfg, dso, dsg, dpg, dpooled, dscale, dwbd = _gate_bwd(dy, lw["wout"], fo, so, pooled, proj, lw["wbd"], lw["scale"])
    dwout = _matmul_acc(mixedt, dy, "dw_out")
    dpx = _pool_bwd(dpooled)
    dqs, dkn, dfv, dck, dcq4, *received = _fox_bwd(qs, kn, proj, dfo, fo, lse, cqb, crow4, ride)
    dsq, dsk, dsv = _sb_bwd(psb, dso, ltot, tril)
    dcq = jnp.pad(dcq4[:, :2, :].reshape(FOX_HEADS, S).T, ((0, 0), (0, N_FFPAD - FOX_HEADS)))
    dfq, dfk, dff, dgq, dgk, dbf = _qk_bwd(dqs, dkn, proj, pff, lw["bfp"], lw["gq"], lw["gk"], bd, dck, dcq, triu_rows)
    pieces = [dfq, dfk, dfv, dfg, dpx, dpg, dsq, dsk, dsv, dsg]
    dwm_t, dwff_t = _inproj_bwd_dw(ht, pieces, dff)
    dwin_t = jnp.concatenate([dwm_t[:OFF_PX], dwff_t[:N_FF], dwm_t[OFF_PX:]], axis=0)
    own = _grad_parts({"w_in_t": dwin_t, "w_out": dwout}) if exchange_own else None
    dx, dng, *received_own = _inproj_bwd_dx(pieces, dff, lw["wm"], lw["wff"], x, lw["g"], dy, own)
    grads = {
        "norm_g": dng[0],
        "w_in_t": dwin_t,
        "b_f": dbf[0, :N_FF],
        "q_norm_g": dgq[0].reshape(FOX_HEADS, HEAD_DIM).sum(0),
        "k_norm_g": dgk[0].reshape(FOX_HEADS, HEAD_DIM).sum(0),
        "w_pool": jnp.stack([dwbd[64 * i:64 * i + 64, 64 * i:64 * i + 64] for i in range(4)]),
        "pool_scale": dscale[0],
        "w_out": dwout,
    }
    return dx, grads, received, received_own


def _layer_weights(l, norm_g, gin, b_f, q_norm_g, k_norm_g, w_pool, pool_scale, gout):
    D = gin.shape[1]
    w = gin.transpose(1, 0, 2).reshape(D, D_IN)
    wm = jnp.concatenate([w[:, :2048], w[:, 2048 + N_FF:]], axis=1)
    wff = jnp.pad(w[:, 2048:2048 + N_FF], ((0, 0), (0, N_FFPAD - N_FF)))
    grp = jnp.arange(POOL_W) // 64
    wbd = jnp.where(grp[:, None] == grp[None, :], jnp.tile(w_pool[l].transpose(1, 0, 2).reshape(64, POOL_W), (4, 1)), 0.0)
    return {
        "g": norm_g[l].reshape(1, D),
        "wm": wm, "wff": wff,
        "bfp": jnp.pad(b_f[l], (0, N_FFPAD - N_FF)).reshape(1, N_FFPAD),
        "gq": jnp.tile(q_norm_g[l], FOX_HEADS).reshape(1, FOX_W),
        "gk": jnp.tile(k_norm_g[l], FOX_HEADS).reshape(1, FOX_W),
        "wbd": wbd.astype(BF16),
        "scale": pool_scale[l].reshape(1, POOL_W),
        "wout": gout.reshape(D_MIX, D),
    }


def _grad_parts(g):
    dwin_t, dwout = g["w_in_t"].astype(BF16), g["w_out"].astype(BF16)
    return (dwin_t.reshape(N_DEV, D_IN // N_DEV, dwin_t.shape[1]),
            dwout.reshape(N_DEV, D_MIX // N_DEV, dwout.shape[1]))


def _train_step(x, target, norm_g, win_sh, b_f, q_norm_g, k_norm_g, w_pool, pool_scale, wout_sh):
    L = norm_g.shape[0]
    consts = _constants(min(_T, x.shape[0]), min(_TM_ROWS, x.shape[0]))
    gathered = _gather_two_level(win_sh[0], wout_sh[0], "gather_weights")
    lws, saved = [], []
    h = x
    for l in range(L):
        lws.append(_layer_weights(l, norm_g, gathered[0], b_f, q_norm_g, k_norm_g, w_pool, pool_scale, gathered[1]))
        ride = (win_sh[l + 1], wout_sh[l + 1]) if l + 1 < L else None
        h, sv, gathered = _layer_fwd(h, lws[l], consts, ride)
        saved.append(sv)
    dy, loss = _loss_head(h, target)
    grads, received = [None] * L, [None] * L
    ride = None
    for l in reversed(range(L)):
        dy, grads[l], got, got_own = _layer_bwd(dy, saved[l], lws[l], consts, ride, exchange_own=(l == 0))
        if ride is not None:
            received[l + 1] = got
        if l == 0:
            received[0] = got_own
        else:
            ride = _grad_parts(grads[l])
    return loss, dy, grads, received


def _mesh_pos():
    return lax.axis_index("x"), lax.axis_index("y"), lax.axis_index("c")


_FLIPS = [(0, 0, 1), (1, 0, 0), (0, 1, 0), (1, 1, 0), (1, 0, 1), (0, 1, 1), (1, 1, 1)]


def _peers():
    x, y, c = _mesh_pos()
    out = []
    for fx, fy, fc in _FLIPS:
        px = 1 - x if fx else x
        py = 1 - y if fy else y
        pc = 1 - c if fc else c
        out.append(((px, py, pc), 4 * px + 2 * py + pc))
    return out, 4 * x + 2 * y + c


_EXCHANGE_SEMS = [pltpu.SemaphoreType.DMA((14,)), pltpu.SemaphoreType.DMA((14,)), pltpu.SemaphoreType.DMA((2,))]
_ANY = pl.BlockSpec(memory_space=pl.ANY)


def _exchange_copies(kind, a_ref, b_ref, oa_ref, ob_ref, send_sems, recv_sems, loc_sems):
    peers, me = _peers()
    pairs = ((a_ref, oa_ref), (b_ref, ob_ref))
    local = [pltpu.make_async_copy(src if kind == "gather" else src.at[me], dst.at[me], loc_sems.at[t])
             for t, (src, dst) in enumerate(pairs)]
    remote = []
    for k, (dev, idx) in enumerate(peers):
        for t, (src, dst) in enumerate(pairs):
            remote.append(pltpu.make_async_remote_copy(
                src_ref=src if kind == "gather" else src.at[idx], dst_ref=dst.at[me],
                send_sem=send_sems.at[2 * k + t], recv_sem=recv_sems.at[2 * k + t],
                device_id=dev, device_id_type=pl.DeviceIdType.MESH))
    return local, remote


def _start_exchange(kind, *refs):
    local, remote = _exchange_copies(kind, *refs)
    for cp in local + remote:
        cp.start()


def _wait_exchange(kind, *refs):
    local, remote = _exchange_copies(kind, *refs)
    for cp in remote:
        cp.wait_recv()
    for cp in remote:
        cp.wait_send()
    for cp in local:
        cp.wait()


def _exchange_out_shapes(kind, a, b):
    if kind == "gather":
        return [jax.ShapeDtypeStruct((N_DEV,) + a.shape, a.dtype), jax.ShapeDtypeStruct((N_DEV,) + b.shape, b.dtype)]
    return [jax.ShapeDtypeStruct(a.shape, a.dtype), jax.ShapeDtypeStruct(b.shape, b.dtype)]


def _gather_two_level(a, b, name):
    def body(a_ref, b_ref, ga_ref, gb_ref, send_sems, recv_sems, loc_sems):
        x, y, c = _mesh_pos()
        slot_of = lambda px, py, pc: 4 * px + 2 * py + pc
        me, sib = slot_of(x, y, c), slot_of(x, y, 1 - c)
        chips = [(1 - x, y), (x, 1 - y), (1 - x, 1 - y)]
        pairs = ((a_ref, ga_ref), (b_ref, gb_ref))

        def copy(k, t, slot, to, src=None):
            dst = pairs[t][1].at[slot]
            return pltpu.make_async_remote_copy(
                src_ref=dst if src is None else src, dst_ref=dst, send_sem=send_sems.at[2 * k + t],
                recv_sem=recv_sems.at[2 * k + t], device_id=to, device_id_type=pl.DeviceIdType.MESH)

        local = [pltpu.make_async_copy(src, dst.at[me], loc_sems.at[t]) for t, (src, dst) in enumerate(pairs)]
        first = []
        for t, (src, _) in enumerate(pairs):
            first.append(copy(0, t, me, (x, y, 1 - c), src))
            first += [copy(1 + j, t, me, (*chip, c), src) for j, chip in enumerate(chips)]
        for cp in local + first:
            cp.start()
        passed = []
        for j, chip in enumerate(chips):
            for t in range(2):
                landed = slot_of(*chip, c)
                copy(1 + j, t, landed, (x, y, c)).wait_recv()
                cp = copy(4 + j, t, landed, (x, y, 1 - c))
                cp.start()
                passed.append(cp)
        for t in range(2):
            copy(0, t, sib, (x, y, c)).wait_recv()
            for j, chip in enumerate(chips):
                copy(4 + j, t, slot_of(*chip, 1 - c), (x, y, c)).wait_recv()
        for cp in first + passed:
            cp.wait_send()
        for cp in local:
            cp.wait()

    return pl.pallas_call(
        body, name=name,
        in_specs=[_ANY, _ANY], out_specs=[_ANY, _ANY],
        out_shape=_exchange_out_shapes("gather", a, b),
        scratch_shapes=_EXCHANGE_SEMS,
    )(a, b)


def _adam_math(w, g, m, v):
    m_new = ADAM_B1 * m + (1.0 - ADAM_B1) * g
    v_new = ADAM_B2 * v + (1.0 - ADAM_B2) * (g * g)
    m_hat = m_new / (1.0 - ADAM_B1 ** ADAM_STEP)
    v_hat = v_new / (1.0 - ADAM_B2 ** ADAM_STEP)
    delta = -ADAM_LR * (m_hat / (jnp.sqrt(v_hat) + ADAM_EPS) + ADAM_WD * w)
    return delta, m_new, v_new


def _sum_adamw(gparts, w, m, v, name):
    L, R, C = w.shape
    tr = min(128, R)

    def body(*refs):
        gp_refs = refs[:L]
        w_ref, m_ref, v_ref, g_ref, d_ref, nm_ref, nv_ref = refs[L:]
        for l in range(L):
            g = gp_refs[l][0].astype(F32)
            for s in range(1, N_DEV):
                g = g + gp_refs[l][s].astype(F32)
            d, mn, vn = _adam_math(w_ref[l], g, m_ref[l], v_ref[l])
            g_ref[l] = g
            d_ref[l] = d
            nm_ref[l] = mn
            nv_ref[l] = vn

    blk = pl.BlockSpec((L, tr, C), lambda r: (0, r, 0))
    return pl.pallas_call(
        body, name=name,
        grid=(R // tr,),
        in_specs=[pl.BlockSpec((N_DEV, tr, C), lambda r: (0, r, 0))] * L + [blk, blk, blk],
        out_specs=[blk, blk, blk, blk],
        out_shape=[jax.ShapeDtypeStruct((L, R, C), F32)] * 4,
        compiler_params=_cp(("parallel",), _VMEM_WIDE),
    )(*gparts, w, m, v)


def _sum_adamw_cols(gparts, w_t, m_t, v_t, name):
    C, L, D = w_t.shape
    td = min(128, D)

    def body(*refs):
        gp_refs = refs[:L]
        w_ref, m_ref, v_ref, g_ref, d_ref, nm_ref, nv_ref = refs[L:]
        starts = list(range(0, C - _ADAM_ROWS + 1, _ADAM_ROWS))
        for c0 in starts:
            rows = slice(c0, C if c0 == starts[-1] else c0 + _ADAM_ROWS)
            for l in range(L):
                g = gp_refs[l][0, rows, :].astype(F32)
                for s in range(1, N_DEV):
                    g = g + gp_refs[l][s, rows, :].astype(F32)
                g_ref[rows, l, :] = g
            d, mn, vn = _adam_math(w_ref[rows], g_ref[rows], m_ref[rows], v_ref[rows])
            d_ref[rows] = d
            nm_ref[rows] = mn
            nv_ref[rows] = vn

    blk = pl.BlockSpec((C, L, td), lambda j: (0, 0, j))
    return pl.pallas_call(
        body, name=name,
        grid=(D // td,),
        in_specs=[pl.BlockSpec((N_DEV, C, td), lambda j: (0, 0, j))] * L + [blk, blk, blk],
        out_specs=[blk, blk, blk, blk],
        out_shape=[jax.ShapeDtypeStruct((C, L, D), F32)] * 4,
        compiler_params=_cp(("parallel",), _VMEM_WIDE),
    )(*gparts, w_t, m_t, v_t)


def _small_update(gpack, wpack, mpack, vpack):
    R = gpack.shape[0]
    VM = pl.BlockSpec(memory_space=pltpu.VMEM)

    def body(g_ref, w_ref, m_ref, v_ref, gs_ref, d_ref, nm_ref, nv_ref, buf, send_sems, recv_sems):
        peers, me = _peers()
        buf[me] = g_ref[...]
        copies = []
        for k, (dev, _) in enumerate(peers):
            cp = pltpu.make_async_remote_copy(
                src_ref=g_ref, dst_ref=buf.at[me], send_sem=send_sems.at[k], recv_sem=recv_sems.at[k],
                device_id=dev, device_id_type=pl.DeviceIdType.MESH)
            cp.start()
            copies.append(cp)
        for cp in copies:
            cp.wait_recv()
        for cp in copies:
            cp.wait_send()
        g = buf[0]
        for s in range(1, N_DEV):
            g = g + buf[s]
        d, mn, vn = _adam_math(w_ref[...], g, m_ref[...], v_ref[...])
        gs_ref[...] = g
        d_ref[...] = d
        nm_ref[...] = mn
        nv_ref[...] = vn

    return pl.pallas_call(
        body, name="small_update",
        in_specs=[VM] * 4, out_specs=[VM] * 4,
        out_shape=[jax.ShapeDtypeStruct((R, 128), F32)] * 4,
        scratch_shapes=[pltpu.VMEM((N_DEV, R, 128), F32), pltpu.SemaphoreType.DMA((7,)), pltpu.SemaphoreType.DMA((7,))],
        compiler_params=_cp(None, _VMEM_MID),
    )(gpack, wpack, mpack, vpack)


_SMALL = ("norm_g", "b_f", "q_norm_g", "k_norm_g", "w_pool", "pool_scale")


def _pack(parts):
    flat = jnp.concatenate([p.reshape(-1) for p in parts])
    n = flat.shape[0]
    rows = -(-n // (8 * 128)) * 8
    return jnp.pad(flat, (0, rows * 128 - n)).reshape(rows, 128)


def _unpack(packed, like):
    flat = packed.reshape(-1)
    out, o = [], 0
    for p in like:
        out.append(flat[o:o + p.size].reshape(p.shape))
        o += p.size
    return out


def kernel(x, norm_g, w_in, b_f, q_norm_g, k_norm_g, w_pool, pool_scale, w_out, loss_target, m_norm_g, m_w_in, m_b_f, m_q_norm_g, m_k_norm_g, m_w_pool, m_pool_scale, m_w_out, v_norm_g, v_w_in, v_b_f, v_q_norm_g, v_k_norm_g, v_w_pool, v_pool_scale, v_w_out):
    L = w_in.shape[0]

    loss_local, dx, grads, received = _train_step(x[0], loss_target[0], norm_g, w_in.astype(BF16), b_f, q_norm_g,
                                                  k_norm_g, w_pool, pool_scale, w_out.astype(BF16))
    loss = lax.psum(loss_local, MESH_AXES)
    g = {k: jnp.stack([grads[l][k] for l in range(L)]) for k in _SMALL}

    cols = lambda a: a.transpose(2, 0, 1)
    g_win, d_win, nm_win, nv_win = [a.transpose(1, 2, 0) for a in _sum_adamw_cols(
        [r[0] for r in received], cols(w_in), cols(m_w_in), cols(v_w_in), "adamw_w_in")]
    g_wout, d_wout, nm_wout, nv_wout = _sum_adamw([r[1] for r in received], w_out, m_w_out, v_w_out, "adamw_w_out")

    ws = dict(norm_g=norm_g, b_f=b_f, q_norm_g=q_norm_g, k_norm_g=k_norm_g, w_pool=w_pool, pool_scale=pool_scale)
    ms = dict(norm_g=m_norm_g, b_f=m_b_f, q_norm_g=m_q_norm_g, k_norm_g=m_k_norm_g, w_pool=m_w_pool, pool_scale=m_pool_scale)
    vs = dict(norm_g=v_norm_g, b_f=v_b_f, q_norm_g=v_q_norm_g, k_norm_g=v_k_norm_g, w_pool=v_w_pool, pool_scale=v_pool_scale)
    like = [ws[k] for k in _SMALL]
    gs_p, d_p, nm_p, nv_p = _small_update(_pack([g[k] for k in _SMALL]), _pack(like),
                                          _pack([ms[k] for k in _SMALL]), _pack([vs[k] for k in _SMALL]))
    gs = dict(zip(_SMALL, _unpack(gs_p, like)))
    ds = dict(zip(_SMALL, _unpack(d_p, like)))
    nms = dict(zip(_SMALL, _unpack(nm_p, like)))
    nvs = dict(zip(_SMALL, _unpack(nv_p, like)))
    gs["w_in"], ds["w_in"], nms["w_in"], nvs["w_in"] = g_win, d_win, nm_win, nv_win
    gs["w_out"], ds["w_out"], nms["w_out"], nvs["w_out"] = g_wout, d_wout, nm_wout, nv_wout

    order = ("norm_g", "w_in", "b_f", "q_norm_g", "k_norm_g", "w_pool", "pool_scale", "w_out")
    return (loss, dx[None], *[gs[k] for k in order], *[ds[k] for k in order],
            *[nms[k] for k in order], *[nvs[k] for k in order])
```
